```python
import jax, jax.numpy as jnp
from jax import lax
import numpy as np

D_MODEL = 1024
BATCH = 16
SEQ = 2048
DEPTH = 1

HEAD_DIM = 64
N_Q_HEADS = 16
N_KV_HEADS = 2
GQA_GROUP = N_Q_HEADS // N_KV_HEADS
WINDOW = 128
BLOCK = 128
Q_WIDTH = N_Q_HEADS * HEAD_DIM
KV_WIDTH = N_KV_HEADS * HEAD_DIM
CONV_CH = D_MODEL
CONV_WIDTH = 31
N_BRANCH = 2
IN_WIDTH = Q_WIDTH + 2 * KV_WIDTH + 2 * CONV_CH + N_BRANCH * D_MODEL
D_FF = 2816
FFN_RESIDUAL = 0.5
N_MOD = 9
EPS = 1e-6

kernel_name = "conditioned_hybrid_swa_conformer_macaron_layer"


def rmsnorm(x, g):
    xf = x.astype(jnp.float32)
    y = xf * lax.rsqrt(jnp.mean(xf * xf, axis=-1, keepdims=True) + EPS)
    return (y * g.astype(jnp.float32)).astype(x.dtype)


def layernorm(x, g, b):
    xf = x.astype(jnp.float32)
    mu = jnp.mean(xf, axis=-1, keepdims=True)
    var = jnp.mean(jnp.square(xf - mu), axis=-1, keepdims=True)
    y = (xf - mu) * lax.rsqrt(var + EPS)
    return (y * g.astype(jnp.float32) + b.astype(jnp.float32)).astype(x.dtype)


def modulate(h, shift, scale):
    return h * (1 + scale[:, None, :]) + shift[:, None, :]


def swiglu(h, w_gate, w_up, w_down):
    return (jax.nn.silu(h @ w_gate) * (h @ w_up)) @ w_down


def sliding_window_sink_attention(q, k, v, sinks):
    B, S = q.shape[0], q.shape[1]
    nb = S // BLOCK
    qb = q.reshape(B, nb, BLOCK, N_KV_HEADS, GQA_GROUP, HEAD_DIM)

    def band(t):
        tp = jnp.pad(t, ((0, 0), (BLOCK, 0), (0, 0), (0, 0)))
        tb = tp.reshape(B, nb + 1, BLOCK, N_KV_HEADS, HEAD_DIM)
        return jnp.concatenate([tb[:, :-1], tb[:, 1:]], axis=2)

    kb, vb = band(k), band(v)
    scores = jnp.einsum('bnqkgd,bnskd->bnkgqs', qb, kb).astype(jnp.float32) * (HEAD_DIM ** -0.5)
    qi = jnp.arange(BLOCK)[:, None]
    sj = jnp.arange(2 * BLOCK)[None, :]
    rel = qi + BLOCK - sj
    key_pos = jnp.arange(nb)[:, None, None] * BLOCK + sj[None] - BLOCK
    valid = ((rel >= 0) & (rel < WINDOW))[None] & (key_pos >= 0)
    valid = valid[None, :, None, None]
    sink = sinks.astype(jnp.float32).reshape(1, 1, N_KV_HEADS, GQA_GROUP, 1, 1)
    masked = jnp.where(valid, scores, -jnp.inf)
    m = jnp.maximum(jnp.max(masked, axis=-1, keepdims=True), sink)
    p = jnp.where(valid, jnp.exp(masked - m), 0.0)
    denom = jnp.sum(p, axis=-1, keepdims=True) + jnp.exp(sink - m)
    probs = (p / denom).astype(v.dtype)
    out = jnp.einsum('bnkgqs,bnskd->bnqkgd', probs, vb)
    return out.reshape(B, S, Q_WIDTH)


def conformer_conv(u2, w_dw, b_dw, ln_g, ln_b, w_pw):
    a, b = jnp.split(u2, 2, axis=-1)
    u = a * jax.nn.sigmoid(b)
    u = jnp.pad(u, ((0, 0), (CONV_WIDTH - 1, 0), (0, 0)))
    y = lax.conv_general_dilated(
        u, w_dw[:, None, :].astype(u.dtype), window_strides=(1,), padding='VALID',
        dimension_numbers=('NWC', 'WIO', 'NWC'), feature_group_count=CONV_CH)
    y = y + b_dw
    y = jax.nn.silu(layernorm(y, ln_g, ln_b))
    return y @ w_pw


def _fwd_setup_inputs(seed: int = 0) -> dict:
    key = jax.random.key(seed)
    ks = jax.random.split(key, 24)
    f32 = jnp.float32
    L = DEPTH

    def w(k, shape, fan_in):
        return jax.random.normal(k, shape, f32) * (fan_in ** -0.5)

    def gain(k, shape):
        return 1.0 + 0.02 * jax.random.normal(k, shape, f32)

    def small(k, shape):
        return 0.01 * jax.random.normal(k, shape, f32)

    return {
        "x": jax.random.normal(ks[0], (BATCH, SEQ, D_MODEL), f32),
        "c": jax.random.normal(ks[1], (BATCH, D_MODEL), f32),
        "w_ada": w(ks[2], (L, D_MODEL, N_MOD * D_MODEL), D_MODEL),
        "b_ada": small(ks[3], (L, N_MOD * D_MODEL)),
        "norm_ffn1_g": gain(ks[4], (L, D_MODEL)),
        "ffn1_w_gate": w(ks[5], (L, D_MODEL, D_FF), D_MODEL),
        "ffn1_w_up": w(ks[6], (L, D_MODEL, D_FF), D_MODEL),
        "ffn1_w_down": w(ks[7], (L, D_FF, D_MODEL), D_FF),
        "norm_mix_g": gain(ks[8], (L, D_MODEL)),
        "w_in": w(ks[9], (L, D_MODEL, IN_WIDTH), D_MODEL),
        "attn_sinks": 0.5 * jax.random.normal(ks[10], (L, N_Q_HEADS), f32),
        "w_attn_o": w(ks[11], (L, Q_WIDTH, D_MODEL), Q_WIDTH),
        "conv_w_dw": w(ks[12], (L, CONV_WIDTH, CONV_CH), CONV_WIDTH),
        "conv_b_dw": small(ks[13], (L, CONV_CH)),
        "conv_ln_g": gain(ks[14], (L, CONV_CH)),
        "conv_ln_b": small(ks[15], (L, CONV_CH)),
        "w_conv_o": w(ks[16], (L, CONV_CH, D_MODEL), CONV_CH),
        "w_out": w(ks[17], (L, D_MODEL, D_MODEL), D_MODEL),
        "norm_ffn2_g": gain(ks[18], (L, D_MODEL)),
        "ffn2_w_gate": w(ks[19], (L, D_MODEL, D_FF), D_MODEL),
        "ffn2_w_up": w(ks[20], (L, D_MODEL, D_FF), D_MODEL),
        "ffn2_w_down": w(ks[21], (L, D_FF, D_MODEL), D_FF),
        "final_norm_g": gain(ks[22], (D_MODEL,)),
    }


def _fwd_reference(x, c, w_ada, b_ada, norm_ffn1_g, ffn1_w_gate, ffn1_w_up, ffn1_w_down,
              norm_mix_g, w_in, attn_sinks, w_attn_o, conv_w_dw, conv_b_dw, conv_ln_g,
              conv_ln_b, w_conv_o, w_out, norm_ffn2_g, ffn2_w_gate, ffn2_w_up, ffn2_w_down,
              final_norm_g):
    B, S, _ = x.shape
    c_act = jax.nn.silu(c)
    split_idx = np.cumsum([Q_WIDTH, KV_WIDTH, KV_WIDTH, 2 * CONV_CH, D_MODEL]).tolist()
    for l in range(DEPTH):
        mod = (c_act @ w_ada[l] + b_ada[l]).reshape(B, N_MOD, D_MODEL)
        sh1, sc1, g1 = mod[:, 0], mod[:, 1], mod[:, 2]
        sh2, sc2, g2 = mod[:, 3], mod[:, 4], mod[:, 5]
        sh3, sc3, g3 = mod[:, 6], mod[:, 7], mod[:, 8]

        h = modulate(rmsnorm(x, norm_ffn1_g[l]), sh1, sc1)
        x = x + FFN_RESIDUAL * g1[:, None, :] * swiglu(h, ffn1_w_gate[l], ffn1_w_up[l], ffn1_w_down[l])

        h = modulate(rmsnorm(x, norm_mix_g[l]), sh2, sc2)
        proj = h @ w_in[l]
        q, k, v, conv_in, gate_a, gate_c = jnp.split(proj, split_idx, axis=-1)
        q = q.reshape(B, S, N_Q_HEADS, HEAD_DIM)
        k = k.reshape(B, S, N_KV_HEADS, HEAD_DIM)
        v = v.reshape(B, S, N_KV_HEADS, HEAD_DIM)
        y_attn = sliding_window_sink_attention(q, k, v, attn_sinks[l]) @ w_attn_o[l]
        y_conv = conformer_conv(conv_in, conv_w_dw[l], conv_b_dw[l], conv_ln_g[l],
                                conv_ln_b[l], w_conv_o[l])
        merged = jax.nn.sigmoid(gate_a) * y_attn + jax.nn.sigmoid(gate_c) * y_conv
        x = x + g2[:, None, :] * (merged @ w_out[l])

        h = modulate(rmsnorm(x, norm_ffn2_g[l]), sh3, sc3)
        x = x + FFN_RESIDUAL * g3[:, None, :] * swiglu(h, ffn2_w_gate[l], ffn2_w_up[l], ffn2_w_down[l])
    return rmsnorm(x, final_norm_g)


import jax as _jax
import jax.numpy as _jnp

TWIN_FORMAT = 'train_step'
FWD_PARAMS = ['x', 'c', 'w_ada', 'b_ada', 'norm_ffn1_g', 'ffn1_w_gate', 'ffn1_w_up', 'ffn1_w_down', 'norm_mix_g', 'w_in', 'attn_sinks', 'w_attn_o', 'conv_w_dw', 'conv_b_dw', 'conv_ln_g', 'conv_ln_b', 'w_conv_o', 'w_out', 'norm_ffn2_g', 'ffn2_w_gate', 'ffn2_w_up', 'ffn2_w_down', 'final_norm_g']
TWIN_WEIGHTS = ['w_ada', 'b_ada', 'norm_ffn1_g', 'ffn1_w_gate', 'ffn1_w_up', 'ffn1_w_down', 'norm_mix_g', 'w_in', 'attn_sinks', 'w_attn_o', 'conv_w_dw', 'conv_b_dw', 'conv_ln_g', 'conv_ln_b', 'w_conv_o', 'w_out', 'norm_ffn2_g', 'ffn2_w_gate', 'ffn2_w_up', 'ffn2_w_down', 'final_norm_g']
TWIN_DIFF_INPUT = 'x'
TWIN_INPUTS = ['x', 'c', 'w_ada', 'b_ada', 'norm_ffn1_g', 'ffn1_w_gate', 'ffn1_w_up', 'ffn1_w_down', 'norm_mix_g', 'w_in', 'attn_sinks', 'w_attn_o', 'conv_w_dw', 'conv_b_dw', 'conv_ln_g', 'conv_ln_b', 'w_conv_o', 'w_out', 'norm_ffn2_g', 'ffn2_w_gate', 'ffn2_w_up', 'ffn2_w_down', 'final_norm_g', 'loss_target', 'm_w_ada', 'm_b_ada', 'm_norm_ffn1_g', 'm_ffn1_w_gate', 'm_ffn1_w_up', 'm_ffn1_w_down', 'm_norm_mix_g', 'm_w_in', 'm_attn_sinks', 'm_w_attn_o', 'm_conv_w_dw', 'm_conv_b_dw', 'm_conv_ln_g', 'm_conv_ln_b', 'm_w_conv_o', 'm_w_out', 'm_norm_ffn2_g', 'm_ffn2_w_gate', 'm_ffn2_w_up', 'm_ffn2_w_down', 'm_final_norm_g', 'v_w_ada', 'v_b_ada', 'v_norm_ffn1_g', 'v_ffn1_w_gate', 'v_ffn1_w_up', 'v_ffn1_w_down', 'v_norm_mix_g', 'v_w_in', 'v_attn_sinks', 'v_w_attn_o', 'v_conv_w_dw', 'v_conv_b_dw', 'v_conv_ln_g', 'v_conv_ln_b', 'v_w_conv_o', 'v_w_out', 'v_norm_ffn2_g', 'v_ffn2_w_gate', 'v_ffn2_w_up', 'v_ffn2_w_down', 'v_final_norm_g']
TWIN_OUTPUTS = ['loss', 'grad_x', 'grad_w_ada', 'grad_b_ada', 'grad_norm_ffn1_g', 'grad_ffn1_w_gate', 'grad_ffn1_w_up', 'grad_ffn1_w_down', 'grad_norm_mix_g', 'grad_w_in', 'grad_attn_sinks', 'grad_w_attn_o', 'grad_conv_w_dw', 'grad_conv_b_dw', 'grad_conv_ln_g', 'grad_conv_ln_b', 'grad_w_conv_o', 'grad_w_out', 'grad_norm_ffn2_g', 'grad_ffn2_w_gate', 'grad_ffn2_w_up', 'grad_ffn2_w_down', 'grad_final_norm_g', 'delta_w_ada', 'delta_b_ada', 'delta_norm_ffn1_g', 'delta_ffn1_w_gate', 'delta_ffn1_w_up', 'delta_ffn1_w_down', 'delta_norm_mix_g', 'delta_w_in', 'delta_attn_sinks', 'delta_w_attn_o', 'delta_conv_w_dw', 'delta_conv_b_dw', 'delta_conv_ln_g', 'delta_conv_ln_b', 'delta_w_conv_o', 'delta_w_out', 'delta_norm_ffn2_g', 'delta_ffn2_w_gate', 'delta_ffn2_w_up', 'delta_ffn2_w_down', 'delta_final_norm_g', 'new_m_w_ada', 'new_m_b_ada', 'new_m_norm_ffn1_g', 'new_m_ffn1_w_gate', 'new_m_ffn1_w_up', 'new_m_ffn1_w_down', 'new_m_norm_mix_g', 'new_m_w_in', 'new_m_attn_sinks', 'new_m_w_attn_o', 'new_m_conv_w_dw', 'new_m_conv_b_dw', 'new_m_conv_ln_g', 'new_m_conv_ln_b', 'new_m_w_conv_o', 'new_m_w_out', 'new_m_norm_ffn2_g', 'new_m_ffn2_w_gate', 'new_m_ffn2_w_up', 'new_m_ffn2_w_down', 'new_m_final_norm_g', 'new_v_w_ada', 'new_v_b_ada', 'new_v_norm_ffn1_g', 'new_v_ffn1_w_gate', 'new_v_ffn1_w_up', 'new_v_ffn1_w_down', 'new_v_norm_mix_g', 'new_v_w_in', 'new_v_attn_sinks', 'new_v_w_attn_o', 'new_v_conv_w_dw', 'new_v_conv_b_dw', 'new_v_conv_ln_g', 'new_v_conv_ln_b', 'new_v_w_conv_o', 'new_v_w_out', 'new_v_norm_ffn2_g', 'new_v_ffn2_w_gate', 'new_v_ffn2_w_up', 'new_v_ffn2_w_down', 'new_v_final_norm_g']
TWIN_LEAF_KINDS = {'loss': 'loss', 'grad_x': 'grad_x', 'grad_w_ada': 'grad_w', 'grad_b_ada': 'grad_w', 'grad_norm_ffn1_g': 'grad_w', 'grad_ffn1_w_gate': 'grad_w', 'grad_ffn1_w_up': 'grad_w', 'grad_ffn1_w_down': 'grad_w', 'grad_norm_mix_g': 'grad_w', 'grad_w_in': 'grad_w', 'grad_attn_sinks': 'grad_w', 'grad_w_attn_o': 'grad_w', 'grad_conv_w_dw': 'grad_w', 'grad_conv_b_dw': 'grad_w', 'grad_conv_ln_g': 'grad_w', 'grad_conv_ln_b': 'grad_w', 'grad_w_conv_o': 'grad_w', 'grad_w_out': 'grad_w', 'grad_norm_ffn2_g': 'grad_w', 'grad_ffn2_w_gate': 'grad_w', 'grad_ffn2_w_up': 'grad_w', 'grad_ffn2_w_down': 'grad_w', 'grad_final_norm_g': 'grad_w', 'delta_w_ada': 'delta_w', 'delta_b_ada': 'delta_w', 'delta_norm_ffn1_g': 'delta_w', 'delta_ffn1_w_gate': 'delta_w', 'delta_ffn1_w_up': 'delta_w', 'delta_ffn1_w_down': 'delta_w', 'delta_norm_mix_g': 'delta_w', 'delta_w_in': 'delta_w', 'delta_attn_sinks': 'delta_w', 'delta_w_attn_o': 'delta_w', 'delta_conv_w_dw': 'delta_w', 'delta_conv_b_dw': 'delta_w', 'delta_conv_ln_g': 'delta_w', 'delta_conv_ln_b': 'delta_w', 'delta_w_conv_o': 'delta_w', 'delta_w_out': 'delta_w', 'delta_norm_ffn2_g': 'delta_w', 'delta_ffn2_w_gate': 'delta_w', 'delta_ffn2_w_up': 'delta_w', 'delta_ffn2_w_down': 'delta_w', 'delta_final_norm_g': 'delta_w', 'new_m_w_ada': 'new_m', 'new_m_b_ada': 'new_m', 'new_m_norm_ffn1_g': 'new_m', 'new_m_ffn1_w_gate': 'new_m', 'new_m_ffn1_w_up': 'new_m', 'new_m_ffn1_w_down': 'new_m', 'new_m_norm_mix_g': 'new_m', 'new_m_w_in': 'new_m', 'new_m_attn_sinks': 'new_m', 'new_m_w_attn_o': 'new_m', 'new_m_conv_w_dw': 'new_m', 'new_m_conv_b_dw': 'new_m', 'new_m_conv_ln_g': 'new_m', 'new_m_conv_ln_b': 'new_m', 'new_m_w_conv_o': 'new_m', 'new_m_w_out': 'new_m', 'new_m_norm_ffn2_g': 'new_m', 'new_m_ffn2_w_gate': 'new_m', 'new_m_ffn2_w_up': 'new_m', 'new_m_ffn2_w_down': 'new_m', 'new_m_final_norm_g': 'new_m', 'new_v_w_ada': 'new_v', 'new_v_b_ada': 'new_v', 'new_v_norm_ffn1_g': 'new_v', 'new_v_ffn1_w_gate': 'new_v', 'new_v_ffn1_w_up': 'new_v', 'new_v_ffn1_w_down': 'new_v', 'new_v_norm_mix_g': 'new_v', 'new_v_w_in': 'new_v', 'new_v_attn_sinks': 'new_v', 'new_v_w_attn_o': 'new_v', 'new_v_conv_w_dw': 'new_v', 'new_v_conv_b_dw': 'new_v', 'new_v_conv_ln_g': 'new_v', 'new_v_conv_ln_b': 'new_v', 'new_v_w_conv_o': 'new_v', 'new_v_w_out': 'new_v', 'new_v_norm_ffn2_g': 'new_v', 'new_v_ffn2_w_gate': 'new_v', 'new_v_ffn2_w_up': 'new_v', 'new_v_ffn2_w_down': 'new_v', 'new_v_final_norm_g': 'new_v'}


def _forward(args):
    return _fwd_reference(*[args[k] for k in FWD_PARAMS])


def _output_shape():
    out = _jax.eval_shape(lambda: _forward(_fwd_setup_inputs(0)))
    return out.shape, out.dtype

N_MICROBATCH = 1
ADAM_LR = 0.001
ADAM_B1 = 0.9
ADAM_B2 = 0.999
ADAM_EPS = 1e-08
ADAM_WD = 0.01
ADAM_STEP = 10
PER_EXAMPLE_BATCH_AXIS = {'x': 0, 'c': 0, 'loss_target': 0}
SHARED_INPUTS = []
_WEIGHT_DTYPES = {'w_ada': _jnp.float32, 'b_ada': _jnp.float32, 'norm_ffn1_g': _jnp.float32, 'ffn1_w_gate': _jnp.float32, 'ffn1_w_up': _jnp.float32, 'ffn1_w_down': _jnp.float32, 'norm_mix_g': _jnp.float32, 'w_in': _jnp.float32, 'attn_sinks': _jnp.float32, 'w_attn_o': _jnp.float32, 'conv_w_dw': _jnp.float32, 'conv_b_dw': _jnp.float32, 'conv_ln_g': _jnp.float32, 'conv_ln_b': _jnp.float32, 'w_conv_o': _jnp.float32, 'w_out': _jnp.float32, 'norm_ffn2_g': _jnp.float32, 'ffn2_w_gate': _jnp.float32, 'ffn2_w_up': _jnp.float32, 'ffn2_w_down': _jnp.float32, 'final_norm_g': _jnp.float32}
MOMENT_SCALE = {'w_ada': 4.720796e-02, 'b_ada': 7.658744e-02, 'norm_ffn1_g': 7.420195e-02, 'ffn1_w_gate': 3.584700e-02, 'ffn1_w_up': 3.468656e-02, 'ffn1_w_down': 5.763313e-02, 'norm_mix_g': 5.592120e-02, 'w_in': 3.059118e-02, 'attn_sinks': 1.624785e-02, 'w_attn_o': 4.383072e-02, 'conv_w_dw': 3.657552e-02, 'conv_b_dw': 5.173275e-02, 'conv_ln_g': 4.207211e-02, 'conv_ln_b': 4.014304e-02, 'w_conv_o': 3.611147e-02, 'w_out': 5.635231e-02, 'norm_ffn2_g': 7.621518e-02, 'ffn2_w_gate': 3.263481e-02, 'ffn2_w_up': 3.141508e-02, 'ffn2_w_down': 5.282874e-02, 'final_norm_g': 3.220568e+01}


def _to_microbatches(a, axis):
    t = _jnp.moveaxis(a, axis, 0)
    t = t.reshape((N_MICROBATCH, t.shape[0] // N_MICROBATCH) + t.shape[1:])
    return _jnp.moveaxis(t, 1, axis + 1)


def setup_inputs(seed: int = 0) -> dict:
    inp = _fwd_setup_inputs(seed)
    key = _jax.random.fold_in(_jax.random.key(seed), 7919)
    shape, _ = _output_shape()
    out = dict(inp)
    out["loss_target"] = _jax.random.normal(_jax.random.fold_in(key, 0), shape, _jnp.float32)
    for i, name in enumerate(TWIN_WEIGHTS):
        w = inp[name].astype(_jnp.float32)
        if MOMENT_SCALE is None:
            s = _jnp.sqrt(_jnp.mean(_jnp.square(w)) + 1e-30)
        else:
            s = MOMENT_SCALE[name]
        km, kv = _jax.random.split(_jax.random.fold_in(key, i + 1))
        out[name] = w
        out["m_" + name] = s * _jax.random.normal(km, w.shape, _jnp.float32)
        out["v_" + name] = (s * s) * _jax.random.uniform(kv, w.shape, _jnp.float32, 0.5, 1.5)
    if N_MICROBATCH > 1:
        for name, axis in PER_EXAMPLE_BATCH_AXIS.items():
            out[name] = _to_microbatches(out[name], axis)
    return {'x': out['x'], 'c': out['c'], 'w_ada': out['w_ada'], 'b_ada': out['b_ada'], 'norm_ffn1_g': out['norm_ffn1_g'], 'ffn1_w_gate': out['ffn1_w_gate'], 'ffn1_w_up': out['ffn1_w_up'], 'ffn1_w_down': out['ffn1_w_down'], 'norm_mix_g': out['norm_mix_g'], 'w_in': out['w_in'], 'attn_sinks': out['attn_sinks'], 'w_attn_o': out['w_attn_o'], 'conv_w_dw': out['conv_w_dw'], 'conv_b_dw': out['conv_b_dw'], 'conv_ln_g': out['conv_ln_g'], 'conv_ln_b': out['conv_ln_b'], 'w_conv_o': out['w_conv_o'], 'w_out': out['w_out'], 'norm_ffn2_g': out['norm_ffn2_g'], 'ffn2_w_gate': out['ffn2_w_gate'], 'ffn2_w_up': out['ffn2_w_up'], 'ffn2_w_down': out['ffn2_w_down'], 'final_norm_g': out['final_norm_g'], 'loss_target': out['loss_target'], 'm_w_ada': out['m_w_ada'], 'm_b_ada': out['m_b_ada'], 'm_norm_ffn1_g': out['m_norm_ffn1_g'], 'm_ffn1_w_gate': out['m_ffn1_w_gate'], 'm_ffn1_w_up': out['m_ffn1_w_up'], 'm_ffn1_w_down': out['m_ffn1_w_down'], 'm_norm_mix_g': out['m_norm_mix_g'], 'm_w_in': out['m_w_in'], 'm_attn_sinks': out['m_attn_sinks'], 'm_w_attn_o': out['m_w_attn_o'], 'm_conv_w_dw': out['m_conv_w_dw'], 'm_conv_b_dw': out['m_conv_b_dw'], 'm_conv_ln_g': out['m_conv_ln_g'], 'm_conv_ln_b': out['m_conv_ln_b'], 'm_w_conv_o': out['m_w_conv_o'], 'm_w_out': out['m_w_out'], 'm_norm_ffn2_g': out['m_norm_ffn2_g'], 'm_ffn2_w_gate': out['m_ffn2_w_gate'], 'm_ffn2_w_up': out['m_ffn2_w_up'], 'm_ffn2_w_down': out['m_ffn2_w_down'], 'm_final_norm_g': out['m_final_norm_g'], 'v_w_ada': out['v_w_ada'], 'v_b_ada': out['v_b_ada'], 'v_norm_ffn1_g': out['v_norm_ffn1_g'], 'v_ffn1_w_gate': out['v_ffn1_w_gate'], 'v_ffn1_w_up': out['v_ffn1_w_up'], 'v_ffn1_w_down': out['v_ffn1_w_down'], 'v_norm_mix_g': out['v_norm_mix_g'], 'v_w_in': out['v_w_in'], 'v_attn_sinks': out['v_attn_sinks'], 'v_w_attn_o': out['v_w_attn_o'], 'v_conv_w_dw': out['v_conv_w_dw'], 'v_conv_b_dw': out['v_conv_b_dw'], 'v_conv_ln_g': out['v_conv_ln_g'], 'v_conv_ln_b': out['v_conv_ln_b'], 'v_w_conv_o': out['v_w_conv_o'], 'v_w_out': out['v_w_out'], 'v_norm_ffn2_g': out['v_norm_ffn2_g'], 'v_ffn2_w_gate': out['v_ffn2_w_gate'], 'v_ffn2_w_up': out['v_ffn2_w_up'], 'v_ffn2_w_down': out['v_ffn2_w_down'], 'v_final_norm_g': out['v_final_norm_g']}


def _loss(weights, diff, rest, loss_target):
    with _jax.named_scope("forward"):
        args = {**rest, TWIN_DIFF_INPUT: diff, **{k: w.astype(_WEIGHT_DTYPES[k]) for k, w in weights.items()}}
        y = _forward(args)
    with _jax.named_scope("loss_head"):
        err = _jnp.square(y.astype(_jnp.float32) - loss_target)
        return 0.5 * _jnp.sum(_jnp.mean(err, axis=-1)) if err.ndim else 0.5 * err


def _adamw(w, g, m, v):
    m = ADAM_B1 * m + (1.0 - ADAM_B1) * g
    v = ADAM_B2 * v + (1.0 - ADAM_B2) * _jnp.square(g)
    m_hat = m / (1.0 - ADAM_B1 ** ADAM_STEP)
    v_hat = v / (1.0 - ADAM_B2 ** ADAM_STEP)
    delta = -ADAM_LR * (m_hat / (_jnp.sqrt(v_hat) + ADAM_EPS) + ADAM_WD * w)
    return delta, m, v


def reference(x, c, w_ada, b_ada, norm_ffn1_g, ffn1_w_gate, ffn1_w_up, ffn1_w_down, norm_mix_g, w_in, attn_sinks, w_attn_o, conv_w_dw, conv_b_dw, conv_ln_g, conv_ln_b, w_conv_o, w_out, norm_ffn2_g, ffn2_w_gate, ffn2_w_up, ffn2_w_down, final_norm_g, loss_target, m_w_ada, m_b_ada, m_norm_ffn1_g, m_ffn1_w_gate, m_ffn1_w_up, m_ffn1_w_down, m_norm_mix_g, m_w_in, m_attn_sinks, m_w_attn_o, m_conv_w_dw, m_conv_b_dw, m_conv_ln_g, m_conv_ln_b, m_w_conv_o, m_w_out, m_norm_ffn2_g, m_ffn2_w_gate, m_ffn2_w_up, m_ffn2_w_down, m_final_norm_g, v_w_ada, v_b_ada, v_norm_ffn1_g, v_ffn1_w_gate, v_ffn1_w_up, v_ffn1_w_down, v_norm_mix_g, v_w_in, v_attn_sinks, v_w_attn_o, v_conv_w_dw, v_conv_b_dw, v_conv_ln_g, v_conv_ln_b, v_w_conv_o, v_w_out, v_norm_ffn2_g, v_ffn2_w_gate, v_ffn2_w_up, v_ffn2_w_down, v_final_norm_g):
    given = dict(x=x, c=c, w_ada=w_ada, b_ada=b_ada, norm_ffn1_g=norm_ffn1_g, ffn1_w_gate=ffn1_w_gate, ffn1_w_up=ffn1_w_up, ffn1_w_down=ffn1_w_down, norm_mix_g=norm_mix_g, w_in=w_in, attn_sinks=attn_sinks, w_attn_o=w_attn_o, conv_w_dw=conv_w_dw, conv_b_dw=conv_b_dw, conv_ln_g=conv_ln_g, conv_ln_b=conv_ln_b, w_conv_o=w_conv_o, w_out=w_out, norm_ffn2_g=norm_ffn2_g, ffn2_w_gate=ffn2_w_gate, ffn2_w_up=ffn2_w_up, ffn2_w_down=ffn2_w_down, final_norm_g=final_norm_g, loss_target=loss_target, m_w_ada=m_w_ada, m_b_ada=m_b_ada, m_norm_ffn1_g=m_norm_ffn1_g, m_ffn1_w_gate=m_ffn1_w_gate, m_ffn1_w_up=m_ffn1_w_up, m_ffn1_w_down=m_ffn1_w_down, m_norm_mix_g=m_norm_mix_g, m_w_in=m_w_in, m_attn_sinks=m_attn_sinks, m_w_attn_o=m_w_attn_o, m_conv_w_dw=m_conv_w_dw, m_conv_b_dw=m_conv_b_dw, m_conv_ln_g=m_conv_ln_g, m_conv_ln_b=m_conv_ln_b, m_w_conv_o=m_w_conv_o, m_w_out=m_w_out, m_norm_ffn2_g=m_norm_ffn2_g, m_ffn2_w_gate=m_ffn2_w_gate, m_ffn2_w_up=m_ffn2_w_up, m_ffn2_w_down=m_ffn2_w_down, m_final_norm_g=m_final_norm_g, v_w_ada=v_w_ada, v_b_ada=v_b_ada, v_norm_ffn1_g=v_norm_ffn1_g, v_ffn1_w_gate=v_ffn1_w_gate, v_ffn1_w_up=v_ffn1_w_up, v_ffn1_w_down=v_ffn1_w_down, v_norm_mix_g=v_norm_mix_g, v_w_in=v_w_in, v_attn_sinks=v_attn_sinks, v_w_attn_o=v_w_attn_o, v_conv_w_dw=v_conv_w_dw, v_conv_b_dw=v_conv_b_dw, v_conv_ln_g=v_conv_ln_g, v_conv_ln_b=v_conv_ln_b, v_w_conv_o=v_w_conv_o, v_w_out=v_w_out, v_norm_ffn2_g=v_norm_ffn2_g, v_ffn2_w_gate=v_ffn2_w_gate, v_ffn2_w_up=v_ffn2_w_up, v_ffn2_w_down=v_ffn2_w_down, v_final_norm_g=v_final_norm_g)
    weights = {n: given[n] for n in TWIN_WEIGHTS}
    shared = {n: given[n] for n in SHARED_INPUTS}
    per_example = {n: given[n] for n in ['x', 'c']}
    grad_fn = _jax.value_and_grad(_loss, argnums=(0, 1))

    def one_microbatch(ex, loss_target):
        ex = dict(ex)
        diff = ex.pop(TWIN_DIFF_INPUT)
        return grad_fn(weights, diff, {**shared, **ex}, loss_target)

    if N_MICROBATCH == 1:
        loss, (grad_w, grad_x) = one_microbatch(per_example, given["loss_target"])
    else:
        def body(carry, xs):
            loss_sum, grad_sum = carry
            l_k, (gw_k, gx_k) = one_microbatch(xs[0], xs[1])
            with _jax.named_scope("update"):
                return (loss_sum + l_k, _jax.tree.map(_jnp.add, grad_sum, gw_k)), gx_k

        init = (_jnp.zeros((), _jnp.float32), _jax.tree.map(_jnp.zeros_like, weights))
        (loss, grad_w), grad_x = _jax.lax.scan(body, init, (per_example, given["loss_target"]))
    with _jax.named_scope("update"):
        delta_w, new_m, new_v = {}, {}, {}
        for n in TWIN_WEIGHTS:
            delta_w[n], new_m[n], new_v[n] = _adamw(weights[n], grad_w[n], given["m_" + n], given["v_" + n])
    return (loss, grad_x, *[grad_w[n] for n in TWIN_WEIGHTS], *[delta_w[n] for n in TWIN_WEIGHTS],
            *[new_m[n] for n in TWIN_WEIGHTS], *[new_v[n] for n in TWIN_WEIGHTS])
```

```python
import jax
import jax.numpy as jnp
from jax import lax
from jax.experimental import pallas as pl
from jax.experimental.pallas import tpu as pltpu

F32 = jnp.float32
BF16 = jnp.bfloat16
SDS = jax.ShapeDtypeStruct
MESH = pl.DeviceIdType.MESH

N_DEV = 8
EPS = 1e-6
HEAD_DIM = 64
N_Q_HEADS = 16
N_KV_HEADS = 2
GQA_GROUP = N_Q_HEADS // N_KV_HEADS
KV_WIDTH = N_KV_HEADS * HEAD_DIM
ATT_BLOCK = 128
CONV_WIDTH = 31
CONV_HALO = 32
N_MOD = 9
FFN_RESIDUAL = 0.5
ADAM_LR = 0.001
ADAM_B1 = 0.9
ADAM_B2 = 0.999
ADAM_EPS = 1e-08
ADAM_WD = 0.01
ADAM_STEP = 10
NEG_BIG = -1e30

V7X_VMEM_BYTES = 64 * 2**20
VMEM_CAP = V7X_VMEM_BYTES - 8 * 2**20


def _nbytes(shape, dtype):
    n = 1
    for s in shape:
        n *= s
    return n * jnp.dtype(dtype).itemsize


def _params(n_axes, blocks, temp_bytes=0):
    need = 2 * sum(_nbytes(s, d) for s, d in blocks) + temp_bytes + 4 * 2**20
    return pltpu.CompilerParams(dimension_semantics=("arbitrary",) * n_axes,
                                vmem_limit_bytes=int(min(max(need, 16 * 2**20), VMEM_CAP)))


def _dot_nt(a, b):
    return lax.dot_general(a, b, (((1,), (1,)), ((), ())), preferred_element_type=F32)


def _dot_tn(a, b):
    return lax.dot_general(a, b, (((0,), (0,)), ((), ())), preferred_element_type=F32)


def _dot(a, b):
    return jnp.dot(a, b, preferred_element_type=F32)


def _sigmoid(x):
    return jax.nn.sigmoid(x)


def _rowsum(v):
    return jnp.sum(v, axis=0, keepdims=True)


def _acc(ref, val, first):
    @pl.when(first)
    def _():
        ref[...] = val

    @pl.when(jnp.logical_not(first))
    def _():
        ref[...] = ref[...] + val


def _norm_mod(xf, gn, sh, sc):
    rstd = lax.rsqrt(jnp.mean(xf * xf, axis=-1, keepdims=True) + EPS)
    xhat = xf * rstd
    yn = xhat * gn
    return yn * (1.0 + sc) + sh, xhat, rstd, yn


def _pick(n, cands):
    for c in cands:
        if n % c == 0:
            return c
    return n


def _my_pos():
    return lax.axis_index("x"), lax.axis_index("y"), lax.axis_index("c")


def _peer(pos, k):
    x, y, c = pos
    return ((1 - x) if k & 4 else x, (1 - y) if k & 2 else y, (1 - c) if k & 1 else c)


def _lin(pos):
    return 4 * pos[0] + 2 * pos[1] + pos[2]


def _exchange(arrs, *, scatter, name):
    n = len(arrs)
    n_peer = N_DEV - 1

    def body(*refs):
        srcs, outs = refs[:n], refs[n:2 * n]
        send_sems, recv_sems, loc_sems = refs[2 * n:]
        me = _my_pos()
        me_lin = _lin(me)

        def remote(i, k):
            peer = _peer(me, k)
            src = srcs[i].at[_lin(peer)] if scatter else srcs[i]
            sem = i * n_peer + k - 1
            send = pltpu.make_async_remote_copy(
                src_ref=src, dst_ref=outs[i].at[me_lin], send_sem=send_sems.at[sem], recv_sem=recv_sems.at[sem],
                device_id=peer, device_id_type=MESH)
            recv = pltpu.make_async_remote_copy(
                src_ref=src, dst_ref=outs[i].at[_lin(peer)], send_sem=send_sems.at[sem], recv_sem=recv_sems.at[sem],
                device_id=peer, device_id_type=MESH)
            return send, recv

        local = []
        for i in range(n):
            own = srcs[i].at[me_lin] if scatter else srcs[i]
            cp = pltpu.make_async_copy(own, outs[i].at[me_lin], loc_sems.at[i])
            cp.start()
            local.append(cp)
            for k in range(1, N_DEV):
                remote(i, k)[0].start()
        for i in range(n):
            for k in range(1, N_DEV):
                send, recv = remote(i, k)
                send.wait_send()
                recv.wait_recv()
            local[i].wait()

    any_spec = pl.BlockSpec(memory_space=pl.ANY)
    out_shape = [SDS(a.shape if scatter else (N_DEV,) + a.shape, a.dtype) for a in arrs]
    outs = pl.pallas_call(
        body, name=name, out_shape=out_shape,
        in_specs=[any_spec] * n, out_specs=[any_spec] * n,
        scratch_shapes=[pltpu.SemaphoreType.DMA((n * n_peer,)), pltpu.SemaphoreType.DMA((n * n_peer,)),
                        pltpu.SemaphoreType.DMA((n,))],
    )(*arrs)
    return list(outs)


def _norm_mod_matmul(x, gn, sh, sc, wts, *, seq, tm, tn, name):
    T, D = x.shape
    N = wts[0].shape[0]
    nw = len(wts)
    tps = seq // tm

    def body(x_ref, gn_ref, sh_ref, sc_ref, *rest):
        w_refs, h_ref, o_refs = rest[:nw], rest[nw], rest[nw + 1:]

        @pl.when(pl.program_id(1) == 0)
        def _():
            h_ref[...] = _norm_mod(x_ref[...], gn_ref[...], sh_ref[0], sc_ref[0])[0].astype(BF16)

        h = h_ref[...]
        for w_ref, o_ref in zip(w_refs, o_refs):
            o_ref[...] = _dot_nt(h, w_ref[...]).astype(o_ref.dtype)

    row = pl.BlockSpec((tm, D), lambda i, j: (i, 0))
    vec = pl.BlockSpec((1, D), lambda i, j: (0, 0))
    per_b = pl.BlockSpec((1, 1, D), lambda i, j: (i // tps, 0, 0))
    wspec = pl.BlockSpec((tn, D), lambda i, j: (j, 0))
    ospec = pl.BlockSpec((tm, tn), lambda i, j: (i, j))
    blocks = [((tm, D), F32), ((tm, D), BF16)] + [((tn, D), BF16), ((tm, tn), BF16)] * nw
    outs = pl.pallas_call(
        body, name=name, grid=(T // tm, N // tn),
        in_specs=[row, vec, per_b, per_b] + [wspec] * nw,
        out_specs=[row] + [ospec] * nw,
        out_shape=[SDS((T, D), BF16)] + [SDS((T, N), BF16)] * nw,
        compiler_params=_params(2, blocks, temp_bytes=2 * _nbytes((tm, tn), F32) + 3 * _nbytes((tm, D), F32)),
    )(x, gn, sh, sc, *wts)
    return outs[0], list(outs[1:])


def _ffn_down(a, b, wd, x, g, *, seq, tm, name):
    T, F = a.shape
    D = wd.shape[1]
    tps = seq // tm

    def body(a_ref, b_ref, wd_ref, x_ref, g_ref, xo_ref, y_ref):
        af = a_ref[...].astype(F32)
        act = (af * _sigmoid(af) * b_ref[...].astype(F32)).astype(BF16)
        y = _dot(act, wd_ref[...])
        xo_ref[...] = x_ref[...] + (FFN_RESIDUAL * g_ref[0]) * y
        y_ref[...] = y.astype(BF16)

    wide = pl.BlockSpec((tm, F), lambda i: (i, 0))
    row = pl.BlockSpec((tm, D), lambda i: (i, 0))
    per_b = pl.BlockSpec((1, 1, D), lambda i: (i // tps, 0, 0))
    wspec = pl.BlockSpec((F, D), lambda i: (0, 0))
    blocks = [((tm, F), BF16)] * 2 + [((F, D), BF16), ((tm, D), F32), ((tm, D), F32), ((tm, D), BF16)]
    return pl.pallas_call(
        body, name=name, grid=(T // tm,),
        in_specs=[wide, wide, wspec, row, per_b], out_specs=[row, row],
        out_shape=[SDS((T, D), F32), SDS((T, D), BF16)],
        compiler_params=_params(1, blocks, temp_bytes=3 * _nbytes((tm, F), F32)),
    )(a, b, wd, x, g)


def _final_loss(x, gf, tgt, *, tm, name):
    T, D = x.shape
    nt = T // tm

    def body(x_ref, gf_ref, t_ref, dx_ref, loss_ref, dgf_ref, lacc):
        i = pl.program_id(0)
        xf = x_ref[...]
        gfv = gf_ref[...]
        rstd = lax.rsqrt(jnp.mean(xf * xf, axis=-1, keepdims=True) + EPS)
        xhat = xf * rstd
        err = xhat * gfv - t_ref[...]
        dy = err * (1.0 / D)
        dxhat = dy * gfv
        dx_ref[...] = rstd * (dxhat - xhat * jnp.mean(dxhat * xhat, axis=-1, keepdims=True))
        _acc(dgf_ref, _rowsum(dy * xhat), i == 0)
        _acc(lacc, _rowsum(err * err), i == 0)

        @pl.when(i == nt - 1)
        def _():
            loss_ref[...] = jnp.broadcast_to((0.5 / D) * jnp.sum(lacc[...]), loss_ref.shape)

    row = pl.BlockSpec((tm, D), lambda i: (i, 0))
    vec = pl.BlockSpec((1, D), lambda i: (0, 0))
    lspec = pl.BlockSpec((1, 128), lambda i: (0, 0))
    blocks = [((tm, D), F32)] * 3
    return pl.pallas_call(
        body, name=name, grid=(nt,),
        in_specs=[row, vec, row], out_specs=[row, lspec, vec],
        out_shape=[SDS((T, D), F32), SDS((1, 128), F32), SDS((1, D), F32)],
        scratch_shapes=[pltpu.VMEM((1, D), F32)],
        compiler_params=_params(1, blocks, temp_bytes=4 * _nbytes((tm, D), F32)),
    )(x, gf, tgt)


def _ffn_bwd_down(dxo, g, y, wd, a, b, *, seq, tm, tn, name):
    T, F = a.shape
    D = wd.shape[1]
    tps = seq // tm
    nb = T // seq

    def body(dxo_ref, g_ref, y_ref, wd_ref, a_ref, b_ref, dyb_ref, da_ref, db_ref, act_ref, dg_ref):
        i = pl.program_id(0)

        @pl.when(pl.program_id(1) == 0)
        def _():
            dx = dxo_ref[...]
            dyb_ref[...] = ((FFN_RESIDUAL * g_ref[0]) * dx).astype(BF16)
            part = _rowsum(FFN_RESIDUAL * dx * y_ref[...].astype(F32))
            _acc(dg_ref, part[None], i % tps == 0)

        dact = _dot_nt(dyb_ref[...], wd_ref[...])
        af = a_ref[...].astype(F32)
        bf = b_ref[...].astype(F32)
        sg = _sigmoid(af)
        silu = af * sg
        act_ref[...] = (silu * bf).astype(BF16)
        da_ref[...] = (dact * bf * (sg * (1.0 + af * (1.0 - sg)))).astype(BF16)
        db_ref[...] = (dact * silu).astype(BF16)

    row = pl.BlockSpec((tm, D), lambda i, j: (i, 0))
    per_b = pl.BlockSpec((1, 1, D), lambda i, j: (i // tps, 0, 0))
    wspec = pl.BlockSpec((tn, D), lambda i, j: (j, 0))
    chunk = pl.BlockSpec((tm, tn), lambda i, j: (i, j))
    blocks = [((tm, D), F32), ((tm, D), BF16), ((tn, D), BF16), ((tm, D), BF16)] + [((tm, tn), BF16)] * 5
    return pl.pallas_call(
        body, name=name, grid=(T // tm, F // tn),
        in_specs=[row, per_b, row, wspec, chunk, chunk],
        out_specs=[row, chunk, chunk, chunk, per_b],
        out_shape=[SDS((T, D), BF16)] + [SDS((T, F), BF16)] * 3 + [SDS((nb, 1, D), F32)],
        compiler_params=_params(2, blocks, temp_bytes=6 * _nbytes((tm, tn), F32)),
    )(dxo, g, y, wd, a, b)


def _matmul_norm_mod_bwd(ds, ws, x, gn, sc, dxo, *, seq, tm, name):
    T, D = x.shape
    nk = len(ds)
    tps = seq // tm
    nb = T // seq

    def body(*refs):
        d_refs, w_refs = refs[:nk], refs[nk:2 * nk]
        x_ref, gn_ref, sc_ref, dxo_ref, dxi_ref, dsh_ref, dsc_ref, dgn_ref = refs[2 * nk:]
        i = pl.program_id(0)
        dh = _dot(d_refs[0][...], w_refs[0][...])
        for d_ref, w_ref in zip(d_refs[1:], w_refs[1:]):
            dh = dh + _dot(d_ref[...], w_ref[...])
        gnv = gn_ref[...]
        scv = sc_ref[0]
        _, xhat, rstd, yn = _norm_mod(x_ref[...], gnv, 0.0, scv)
        dyn = dh * (1.0 + scv)
        dxhat = dyn * gnv
        dxi_ref[...] = dxo_ref[...] + rstd * (dxhat - xhat * jnp.mean(dxhat * xhat, axis=-1, keepdims=True))
        first_of_seq = i % tps == 0
        _acc(dsh_ref, _rowsum(dh)[None], first_of_seq)
        _acc(dsc_ref, _rowsum(dh * yn)[None], first_of_seq)
        _acc(dgn_ref, _rowsum(dyn * xhat), i == 0)

    row = pl.BlockSpec((tm, D), lambda i: (i, 0))
    vec = pl.BlockSpec((1, D), lambda i: (0, 0))
    per_b = pl.BlockSpec((1, 1, D), lambda i: (i // tps, 0, 0))
    d_specs = [pl.BlockSpec((tm, d.shape[1]), lambda i: (i, 0)) for d in ds]
    w_specs = [pl.BlockSpec(w.shape, lambda i: (0, 0)) for w in ws]
    blocks = ([((tm, d.shape[1]), BF16) for d in ds] + [(w.shape, BF16) for w in ws] + [((tm, D), F32)] * 3)
    return pl.pallas_call(
        body, name=name, grid=(T // tm,),
        in_specs=d_specs + w_specs + [row, vec, per_b, row],
        out_specs=[row, per_b, per_b, vec],
        out_shape=[SDS((T, D), F32), SDS((nb, 1, D), F32), SDS((nb, 1, D), F32), SDS((1, D), F32)],
        compiler_params=_params(1, blocks, temp_bytes=6 * _nbytes((tm, D), F32)),
    )(*ds, *ws, x, gn, sc, dxo)


def _layernorm_silu(yc, lg, lb):
    mu = jnp.mean(yc, axis=-1, keepdims=True)
    cen = yc - mu
    rstd = lax.rsqrt(jnp.mean(cen * cen, axis=-1, keepdims=True) + EPS)
    xh = cen * rstd
    l = xh * lg + lb
    s = _sigmoid(l)
    return l * s, xh, rstd, l, s


def _mix_out(ao, yc, projp, wao, wco, wout, x1, g2, lg, lb, *, seq, tm, ga_blk, gc_blk, name):
    T, D = x1.shape
    tps = seq // tm

    def body(ao_ref, yc_ref, ga_ref, gc_ref, wao_ref, wco_ref, wout_ref, x1_ref, g2_ref, lg_ref, lb_ref,
             x2_ref, z_ref, ya_ref, ycv_ref, cact_ref, mrg_ref):
        ya = _dot(ao_ref[...], wao_ref[...])
        cact = _layernorm_silu(yc_ref[...], lg_ref[...], lb_ref[...])[0].astype(BF16)
        ycv = _dot(cact, wco_ref[...])
        merged = (_sigmoid(ga_ref[...].astype(F32)) * ya + _sigmoid(gc_ref[...].astype(F32)) * ycv).astype(BF16)
        z = _dot(merged, wout_ref[...])
        x2_ref[...] = x1_ref[...] + g2_ref[0] * z
        z_ref[...] = z.astype(BF16)
        ya_ref[...] = ya.astype(BF16)
        ycv_ref[...] = ycv.astype(BF16)
        cact_ref[...] = cact
        mrg_ref[...] = merged

    row = pl.BlockSpec((tm, D), lambda i: (i, 0))
    vec = pl.BlockSpec((1, D), lambda i: (0, 0))
    per_b = pl.BlockSpec((1, 1, D), lambda i: (i // tps, 0, 0))
    wspec = pl.BlockSpec((D, D), lambda i: (0, 0))
    ga_spec = pl.BlockSpec((tm, D), lambda i: (i, ga_blk))
    gc_spec = pl.BlockSpec((tm, D), lambda i: (i, gc_blk))
    blocks = ([((tm, D), BF16), ((tm, D), F32), ((tm, D), BF16), ((tm, D), BF16)] + [((D, D), BF16)] * 3
              + [((tm, D), F32)] * 2 + [((tm, D), BF16)] * 5)
    return pl.pallas_call(
        body, name=name, grid=(T // tm,),
        in_specs=[row, row, ga_spec, gc_spec, wspec, wspec, wspec, row, per_b, vec, vec],
        out_specs=[row] * 6,
        out_shape=[SDS((T, D), F32)] + [SDS((T, D), BF16)] * 5,
        compiler_params=_params(1, blocks, temp_bytes=8 * _nbytes((tm, D), F32)),
    )(ao, yc, projp, projp, wao, wco, wout, x1, g2, lg, lb)


def _mix_out_bwd(dx2, g2, z, wout, projp, ya, ycv, wao, wco, yc, lg, lb, *, seq, tm, ga_blk, gc_blk, name):
    T, D = dx2.shape
    tps = seq // tm
    nb = T // seq

    def body(dx2_ref, g2_ref, z_ref, wout_ref, ga_ref, gc_ref, ya_ref, ycv_ref, wao_ref, wco_ref, yc_ref,
             lg_ref, lb_ref, dz_ref, dya_ref, dycv_ref, dga_ref, dgc_ref, dao_ref, dyc_ref, dg2_ref, dlg_ref,
             dlb_ref):
        i = pl.program_id(0)
        dx = dx2_ref[...]
        _acc(dg2_ref, _rowsum(dx * z_ref[...].astype(F32))[None], i % tps == 0)
        dzb = (g2_ref[0] * dx).astype(BF16)
        dz_ref[...] = dzb
        dmerged = _dot_nt(dzb, wout_ref[...])
        sa = _sigmoid(ga_ref[...].astype(F32))
        sc_ = _sigmoid(gc_ref[...].astype(F32))
        dya = (dmerged * sa).astype(BF16)
        dycv = (dmerged * sc_).astype(BF16)
        dya_ref[...] = dya
        dycv_ref[...] = dycv
        dga_ref[...] = (dmerged * ya_ref[...].astype(F32) * (sa * (1.0 - sa))).astype(BF16)
        dgc_ref[...] = (dmerged * ycv_ref[...].astype(F32) * (sc_ * (1.0 - sc_))).astype(BF16)
        dao_ref[...] = _dot_nt(dya, wao_ref[...]).astype(BF16)
        dcact = _dot_nt(dycv, wco_ref[...])
        lgv = lg_ref[...]
        _, xh, rstd, l, s = _layernorm_silu(yc_ref[...], lgv, lb_ref[...])
        dl = dcact * (s * (1.0 + l * (1.0 - s)))
        _acc(dlb_ref, _rowsum(dl), i == 0)
        _acc(dlg_ref, _rowsum(dl * xh), i == 0)
        dxh = dl * lgv
        dyc_ref[...] = rstd * (dxh - jnp.mean(dxh, axis=-1, keepdims=True)
                               - xh * jnp.mean(dxh * xh, axis=-1, keepdims=True))

    row = pl.BlockSpec((tm, D), lambda i: (i, 0))
    vec = pl.BlockSpec((1, D), lambda i: (0, 0))
    per_b = pl.BlockSpec((1, 1, D), lambda i: (i // tps, 0, 0))
    wspec = pl.BlockSpec((D, D), lambda i: (0, 0))
    ga_spec = pl.BlockSpec((tm, D), lambda i: (i, ga_blk))
    gc_spec = pl.BlockSpec((tm, D), lambda i: (i, gc_blk))
    blocks = ([((tm, D), F32)] * 3 + [((tm, D), BF16)] * 11 + [((D, D), BF16)] * 3)
    return pl.pallas_call(
        body, name=name, grid=(T // tm,),
        in_specs=[row, per_b, row, wspec, ga_spec, gc_spec, row, row, wspec, wspec, row, vec, vec],
        out_specs=[row] * 7 + [per_b, vec, vec],
        out_shape=[SDS((T, D), BF16)] * 6 + [SDS((T, D), F32), SDS((nb, 1, D), F32), SDS((1, D), F32),
                                             SDS((1, D), F32)],
        compiler_params=_params(1, blocks, temp_bytes=10 * _nbytes((tm, D), F32)),
    )(dx2, g2, z, wout, projp, projp, ya, ycv, wao, wco, yc, lg, lb)


def _band_mask(n):
    qi = lax.broadcasted_iota(jnp.int32, (ATT_BLOCK, 2 * ATT_BLOCK), 0)
    sj = lax.broadcasted_iota(jnp.int32, (ATT_BLOCK, 2 * ATT_BLOCK), 1)
    rel = qi + ATT_BLOCK - sj
    in_window = jnp.logical_and(rel >= 0, rel < ATT_BLOCK)
    return jnp.logical_and(in_window, jnp.logical_or(sj >= ATT_BLOCK, n > 0))


def _head_probs(qh, kg, valid, sink):
    s = _dot_nt(qh, kg) * (HEAD_DIM ** -0.5)
    s = jnp.where(valid, s, NEG_BIG)
    m = jnp.maximum(jnp.max(s, axis=-1, keepdims=True), sink)
    p = jnp.exp(s - m)
    psink = jnp.exp(sink - m)
    inv = 1.0 / (jnp.sum(p, axis=-1, keepdims=True) + psink)
    return p * inv, psink * inv


def _load_padded(dst, src, seq):
    dst[pl.ds(0, ATT_BLOCK), :] = jnp.zeros((ATT_BLOCK, dst.shape[1]), dst.dtype)
    dst[pl.ds(ATT_BLOCK, seq), :] = src[...]


def _attn_fwd(projp, sinks, *, seq, q_blk, k_blk, v_blk, name):
    T = projp.shape[0]
    QW = N_Q_HEADS * HEAD_DIM
    nblk = seq // ATT_BLOCK

    def body(q_ref, k_ref, v_ref, sink_ref, o_ref, kpad, vpad):
        _load_padded(kpad, k_ref, seq)
        _load_padded(vpad, v_ref, seq)

        def blk(n, carry):
            r0 = pl.multiple_of(n * ATT_BLOCK, ATT_BLOCK)
            qb = q_ref[pl.ds(r0, ATT_BLOCK), :]
            kb = kpad[pl.ds(r0, 2 * ATT_BLOCK), :]
            vb = vpad[pl.ds(r0, 2 * ATT_BLOCK), :]
            valid = _band_mask(n)
            for h in range(N_Q_HEADS):
                g = h // GQA_GROUP
                probs, _ = _head_probs(qb[:, h * HEAD_DIM:(h + 1) * HEAD_DIM],
                                       kb[:, g * HEAD_DIM:(g + 1) * HEAD_DIM], valid, sink_ref[0, h])
                o = _dot(probs.astype(BF16), vb[:, g * HEAD_DIM:(g + 1) * HEAD_DIM])
                o_ref[pl.ds(r0, ATT_BLOCK), h * HEAD_DIM:(h + 1) * HEAD_DIM] = o.astype(BF16)
            return carry

        lax.fori_loop(0, nblk, blk, 0)

    blocks = [((seq, QW), BF16)] * 2 + [((seq, KV_WIDTH), BF16)] * 2
    return pl.pallas_call(
        body, name=name, grid=(T // seq,),
        in_specs=[pl.BlockSpec((seq, QW), lambda b: (b, q_blk)),
                  pl.BlockSpec((seq, KV_WIDTH), lambda b: (b, k_blk)),
                  pl.BlockSpec((seq, KV_WIDTH), lambda b: (b, v_blk)),
                  pl.BlockSpec(memory_space=pltpu.SMEM)],
        out_specs=pl.BlockSpec((seq, QW), lambda b: (b, 0)),
        out_shape=SDS((T, QW), BF16),
        scratch_shapes=[pltpu.VMEM((seq + ATT_BLOCK, KV_WIDTH), BF16)] * 2,
        compiler_params=_params(1, blocks, temp_bytes=8 * 2**20),
    )(projp, projp, projp, sinks)


def _attn_bwd(projp, dao, sinks, *, seq, q_blk, k_blk, v_blk, name):
    T = projp.shape[0]
    QW = N_Q_HEADS * HEAD_DIM
    nblk = seq // ATT_BLOCK

    def body(q_ref, k_ref, v_ref, do_ref, sink_ref, dq_ref, dk_ref, dv_ref, dsink_ref, kpad, vpad, dkacc, dvacc):
        _load_padded(kpad, k_ref, seq)
        _load_padded(vpad, v_ref, seq)
        dkacc[...] = jnp.zeros(dkacc.shape, F32)
        dvacc[...] = jnp.zeros(dvacc.shape, F32)
        lane = lax.broadcasted_iota(jnp.int32, (1, 128), 1)

        def blk(n, dsink):
            r0 = pl.multiple_of(n * ATT_BLOCK, ATT_BLOCK)
            qb = q_ref[pl.ds(r0, ATT_BLOCK), :]
            dob = do_ref[pl.ds(r0, ATT_BLOCK), :]
            kb = kpad[pl.ds(r0, 2 * ATT_BLOCK), :]
            vb = vpad[pl.ds(r0, 2 * ATT_BLOCK), :]
            valid = _band_mask(n)
            for g in range(N_KV_HEADS):
                kg = kb[:, g * HEAD_DIM:(g + 1) * HEAD_DIM]
                vg = vb[:, g * HEAD_DIM:(g + 1) * HEAD_DIM]
                dkg = jnp.zeros((2 * ATT_BLOCK, HEAD_DIM), F32)
                dvg = jnp.zeros((2 * ATT_BLOCK, HEAD_DIM), F32)
                for h in range(g * GQA_GROUP, (g + 1) * GQA_GROUP):
                    qh = qb[:, h * HEAD_DIM:(h + 1) * HEAD_DIM]
                    doh = dob[:, h * HEAD_DIM:(h + 1) * HEAD_DIM]
                    probs, psink = _head_probs(qh, kg, valid, sink_ref[0, h])
                    dp = _dot_nt(doh, vg)
                    delta = jnp.sum(probs * dp, axis=-1, keepdims=True)
                    dsb = (probs * (dp - delta) * (HEAD_DIM ** -0.5)).astype(BF16)
                    dsink = dsink + jnp.where(lane == h, -jnp.sum(psink * delta), 0.0)
                    dq_ref[pl.ds(r0, ATT_BLOCK), h * HEAD_DIM:(h + 1) * HEAD_DIM] = _dot(dsb, kg).astype(BF16)
                    dkg = dkg + _dot_tn(dsb, qh)
                    dvg = dvg + _dot_tn(probs.astype(BF16), doh)
                cols = slice(g * HEAD_DIM, (g + 1) * HEAD_DIM)
                dkacc[pl.ds(r0, 2 * ATT_BLOCK), cols] = dkacc[pl.ds(r0, 2 * ATT_BLOCK), cols] + dkg
                dvacc[pl.ds(r0, 2 * ATT_BLOCK), cols] = dvacc[pl.ds(r0, 2 * ATT_BLOCK), cols] + dvg
            return dsink

        dsink = lax.fori_loop(0, nblk, blk, jnp.zeros((1, 128), F32))
        _acc(dsink_ref, dsink, pl.program_id(0) == 0)
        dk_ref[...] = dkacc[pl.ds(ATT_BLOCK, seq), :].astype(BF16)
        dv_ref[...] = dvacc[pl.ds(ATT_BLOCK, seq), :].astype(BF16)

    blocks = [((seq, QW), BF16)] * 3 + [((seq, KV_WIDTH), BF16)] * 4
    kv_spec_out = pl.BlockSpec((seq, KV_WIDTH), lambda b: (b, 0))
    return pl.pallas_call(
        body, name=name, grid=(T // seq,),
        in_specs=[pl.BlockSpec((seq, QW), lambda b: (b, q_blk)),
                  pl.BlockSpec((seq, KV_WIDTH), lambda b: (b, k_blk)),
                  pl.BlockSpec((seq, KV_WIDTH), lambda b: (b, v_blk)),
                  pl.BlockSpec((seq, QW), lambda b: (b, 0)),
                  pl.BlockSpec(memory_space=pltpu.SMEM)],
        out_specs=[pl.BlockSpec((seq, QW), lambda b: (b, 0)), kv_spec_out, kv_spec_out,
                   pl.BlockSpec((1, 128), lambda b: (0, 0))],
        out_shape=[SDS((T, QW), BF16), SDS((T, KV_WIDTH), BF16), SDS((T, KV_WIDTH), BF16), SDS((1, 128), F32)],
        scratch_shapes=[pltpu.VMEM((seq + ATT_BLOCK, KV_WIDTH), BF16)] * 2
        + [pltpu.VMEM((seq + ATT_BLOCK, KV_WIDTH), F32)] * 2,
        compiler_params=_params(1, blocks, temp_bytes=12 * 2**20),
    )(projp, projp, projp, dao, sinks)


CONV_ROWS = 64


def _conv_fwd(projp, w, bias, *, seq, cw, a_col, b_col, name):
    T = projp.shape[0]
    C = w.shape[1]
    nchunk = seq // CONV_ROWS

    def body(a_ref, b_ref, w_ref, bias_ref, y_ref, upad):
        upad[pl.ds(0, CONV_HALO), :] = jnp.zeros((CONV_HALO, cw), F32)
        upad[pl.ds(CONV_HALO, seq), :] = a_ref[...].astype(F32) * _sigmoid(b_ref[...].astype(F32))
        wv = w_ref[...]
        bv = bias_ref[...]

        def chunk(r, carry):
            r0 = pl.multiple_of(r * CONV_ROWS, CONV_ROWS)
            win = upad[pl.ds(r0, CONV_ROWS + CONV_HALO), :]
            acc = jnp.broadcast_to(bv, (CONV_ROWS, cw))
            for k in range(CONV_WIDTH):
                off = CONV_HALO - (CONV_WIDTH - 1) + k
                acc = acc + wv[k:k + 1, :] * win[off:off + CONV_ROWS, :]
            y_ref[pl.ds(r0, CONV_ROWS), :] = acc
            return carry

        lax.fori_loop(0, nchunk, chunk, 0)

    blocks = [((seq, cw), BF16)] * 2 + [((seq, cw), F32)]
    return pl.pallas_call(
        body, name=name, grid=(T // seq, C // cw),
        in_specs=[pl.BlockSpec((seq, cw), lambda b, c: (b, a_col // cw + c)),
                  pl.BlockSpec((seq, cw), lambda b, c: (b, b_col // cw + c)),
                  pl.BlockSpec((CONV_WIDTH, cw), lambda b, c: (0, c)),
                  pl.BlockSpec((1, cw), lambda b, c: (0, c))],
        out_specs=pl.BlockSpec((seq, cw), lambda b, c: (b, c)),
        out_shape=SDS((T, C), F32),
        scratch_shapes=[pltpu.VMEM((seq + CONV_HALO, cw), F32)],
        compiler_params=_params(2, blocks, temp_bytes=6 * _nbytes((seq, cw), F32)),
    )(projp, projp, w, bias)


def _conv_bwd(dy, projp, w, *, seq, cw, a_col, b_col, name):
    T = projp.shape[0]
    C = w.shape[1]
    nchunk = seq // CONV_ROWS
    SUB = 8

    def body(dy_ref, a_ref, b_ref, w_ref, da_ref, db_ref, dw_ref, dbias_ref, upad, dypad, dwp):
        first = pl.program_id(1) == 0
        af = a_ref[...].astype(F32)
        sb = _sigmoid(b_ref[...].astype(F32))
        upad[pl.ds(0, CONV_HALO), :] = jnp.zeros((CONV_HALO, cw), F32)
        upad[pl.ds(CONV_HALO, seq), :] = af * sb
        dyv = dy_ref[...]
        dypad[pl.ds(0, seq), :] = dyv
        dypad[pl.ds(seq, CONV_HALO), :] = jnp.zeros((CONV_HALO, cw), F32)
        dwp[...] = jnp.zeros(dwp.shape, F32)
        wv = w_ref[...]

        def chunk(r, carry):
            r0 = pl.multiple_of(r * CONV_ROWS, CONV_ROWS)
            wdy = dypad[pl.ds(r0, CONV_ROWS + CONV_HALO), :]
            wu = upad[pl.ds(r0, CONV_ROWS + CONV_HALO), :]
            dyc = wdy[0:CONV_ROWS, :]
            du = jnp.zeros((CONV_ROWS, cw), F32)
            for k in range(CONV_WIDTH):
                du = du + wv[k:k + 1, :] * wdy[CONV_WIDTH - 1 - k:CONV_WIDTH - 1 - k + CONV_ROWS, :]
                off = CONV_HALO - (CONV_WIDTH - 1) + k
                prod = dyc * wu[off:off + CONV_ROWS, :]
                part = prod[0:SUB, :]
                for s in range(1, CONV_ROWS // SUB):
                    part = part + prod[s * SUB:(s + 1) * SUB, :]
                dwp[pl.ds(k * SUB, SUB), :] = dwp[pl.ds(k * SUB, SUB), :] + part
            ac = a_ref[pl.ds(r0, CONV_ROWS), :].astype(F32)
            sbc = _sigmoid(b_ref[pl.ds(r0, CONV_ROWS), :].astype(F32))
            da_ref[pl.ds(r0, CONV_ROWS), :] = (du * sbc).astype(BF16)
            db_ref[pl.ds(r0, CONV_ROWS), :] = (du * ac * (sbc * (1.0 - sbc))).astype(BF16)
            return carry

        lax.fori_loop(0, nchunk, chunk, 0)

        @pl.when(first)
        def _():
            dw_ref[...] = jnp.zeros(dw_ref.shape, F32)
            dbias_ref[...] = jnp.zeros(dbias_ref.shape, F32)

        for k in range(CONV_WIDTH):
            dw_ref[k:k + 1, :] = dw_ref[k:k + 1, :] + _rowsum(dwp[pl.ds(k * SUB, SUB), :])
        dbias_ref[...] = dbias_ref[...] + _rowsum(dyv)

    blocks = [((seq, cw), F32)] + [((seq, cw), BF16)] * 4
    return pl.pallas_call(
        body, name=name, grid=(C // cw, T // seq),
        in_specs=[pl.BlockSpec((seq, cw), lambda c, b: (b, c)),
                  pl.BlockSpec((seq, cw), lambda c, b: (b, a_col // cw + c)),
                  pl.BlockSpec((seq, cw), lambda c, b: (b, b_col // cw + c)),
                  pl.BlockSpec((CONV_WIDTH, cw), lambda c, b: (0, c))],
        out_specs=[pl.BlockSpec((seq, cw), lambda c, b: (b, c)), pl.BlockSpec((seq, cw), lambda c, b: (b, c)),
                   pl.BlockSpec((CONV_WIDTH, cw), lambda c, b: (0, c)), pl.BlockSpec((1, cw), lambda c, b: (0, c))],
        out_shape=[SDS((T, C), BF16), SDS((T, C), BF16), SDS((CONV_WIDTH, C), F32), SDS((1, C), F32)],
        scratch_shapes=[pltpu.VMEM((seq + CONV_HALO, cw), F32), pltpu.VMEM((seq + CONV_HALO, cw), F32),
                        pltpu.VMEM((CONV_WIDTH * SUB, cw), F32)],
        compiler_params=_params(2, blocks, temp_bytes=8 * _nbytes((seq, cw), F32)),
    )(dy, projp, projp, w)


def _matmul_tn(a, b, *, name):
    T, M = a.shape
    N = b.shape[1]
    bm = _pick(M, (768, 512, 256))

    def body(a_ref, b_ref, o_ref):
        o_ref[...] = _dot_tn(a_ref[...], b_ref[...])

    blocks = [((T, bm), BF16), ((T, N), BF16), ((bm, N), F32)]
    return pl.pallas_call(
        body, name=name, grid=(M // bm,),
        in_specs=[pl.BlockSpec((T, bm), lambda i: (0, i)), pl.BlockSpec((T, N), lambda i: (0, 0))],
        out_specs=pl.BlockSpec((bm, N), lambda i: (i, 0)),
        out_shape=SDS((M, N), F32),
        compiler_params=_params(1, blocks, temp_bytes=2 * _nbytes((T, bm), BF16) + _nbytes((bm, N), F32)),
    )(a, b)


def _sum_parts(p_ref):
    g = p_ref[0]
    for s in range(1, N_DEV):
        g = g + p_ref[s]
    return g


def _adamw_update(w, g, m, v):
    m = ADAM_B1 * m + (1.0 - ADAM_B1) * g
    v = ADAM_B2 * v + (1.0 - ADAM_B2) * (g * g)
    m_hat = m / (1.0 - ADAM_B1 ** ADAM_STEP)
    v_hat = v / (1.0 - ADAM_B2 ** ADAM_STEP)
    delta = -ADAM_LR * (m_hat / (jnp.sqrt(v_hat) + ADAM_EPS) + ADAM_WD * w)
    return delta, m, v


def _row_tile(R):
    return _pick(R, (256, 128, 112, 88, 64, 32, 16, 8))


def _sum8(parts, *, name):
    _, R, W = parts.shape
    tr = _row_tile(R)

    def body(p_ref, o_ref):
        o_ref[...] = _sum_parts(p_ref)

    return pl.pallas_call(
        body, name=name, grid=(R // tr,),
        in_specs=[pl.BlockSpec((N_DEV, tr, W), lambda i: (0, i, 0))],
        out_specs=pl.BlockSpec((tr, W), lambda i: (i, 0)),
        out_shape=SDS((R, W), F32),
        compiler_params=_params(1, [((N_DEV, tr, W), F32), ((tr, W), F32)]),
    )(parts)


def _adamw(g, w, m, v, *, name):
    R, W = w.shape
    tr = _row_tile(R)

    def body(g_ref, w_ref, m_ref, v_ref, d_ref, mo_ref, vo_ref):
        d_ref[...], mo_ref[...], vo_ref[...] = _adamw_update(w_ref[...], g_ref[...], m_ref[...], v_ref[...])

    spec = pl.BlockSpec((tr, W), lambda i: (i, 0))
    return pl.pallas_call(
        body, name=name, grid=(R // tr,),
        in_specs=[spec] * 4, out_specs=[spec] * 3, out_shape=[SDS((R, W), F32)] * 3,
        compiler_params=_params(1, [((tr, W), F32)] * 7),
    )(g, w, m, v)


def _sum8_adamw(parts, w, m, v, *, name):
    R, W = w.shape
    tr = _row_tile(R)

    def body(p_ref, w_ref, m_ref, v_ref, g_ref, d_ref, mo_ref, vo_ref):
        g = _sum_parts(p_ref)
        g_ref[...] = g
        d_ref[...], mo_ref[...], vo_ref[...] = _adamw_update(w_ref[...], g, m_ref[...], v_ref[...])

    spec = pl.BlockSpec((tr, W), lambda i: (i, 0))
    return pl.pallas_call(
        body, name=name, grid=(R // tr,),
        in_specs=[pl.BlockSpec((N_DEV, tr, W), lambda i: (0, i, 0))] + [spec] * 3,
        out_specs=[spec] * 4, out_shape=[SDS((R, W), F32)] * 4,
        compiler_params=_params(1, [((N_DEV, tr, W), F32)] + [((tr, W), F32)] * 7),
    )(parts, w, m, v)


def _ada_fwd(c_all, w, bias, *, name):
    NB, D = c_all.shape
    N = w.shape[1]

    def body(c_ref, w_ref, b_ref, o_ref):
        cv = c_ref[...]
        ca = (cv * _sigmoid(cv)).astype(BF16)
        o_ref[...] = _dot(ca, w_ref[...].astype(BF16)) + b_ref[...]

    full = lambda s: pl.BlockSpec(s, lambda i: (0,) * len(s))
    return pl.pallas_call(
        body, name=name, grid=(1,),
        in_specs=[full((NB, D)), full((D, N)), full((1, N))], out_specs=full((NB, N)),
        out_shape=SDS((NB, N), F32),
        compiler_params=_params(1, [((D, N), F32)], temp_bytes=_nbytes((D, N), BF16)),
    )(c_all, w, bias)


def _ada_bwd(c_all, gcols, *, name):
    NB, D = c_all.shape
    N = gcols.shape[1]

    def body(c_ref, g_ref, gw_ref, gb_ref):
        cv = c_ref[...]
        ca = (cv * _sigmoid(cv)).astype(BF16)
        gv = g_ref[...]
        gw_ref[...] = _dot_tn(ca, gv.astype(BF16))
        gb_ref[...] = _rowsum(gv)

    full = lambda s: pl.BlockSpec(s, lambda i: (0,) * len(s))
    return pl.pallas_call(
        body, name=name, grid=(1,),
        in_specs=[full((NB, D)), full((NB, N))], out_specs=[full((D, N)), full((1, N))],
        out_shape=[SDS((D, N), F32), SDS((1, N), F32)],
        compiler_params=_params(1, [((D, N), F32)]),
    )(c_all, gcols)


def kernel(x, c, w_ada, b_ada, norm_ffn1_g, ffn1_w_gate, ffn1_w_up, ffn1_w_down, norm_mix_g, w_in, attn_sinks, w_attn_o, conv_w_dw, conv_b_dw, conv_ln_g, conv_ln_b, w_conv_o, w_out, norm_ffn2_g, ffn2_w_gate, ffn2_w_up, ffn2_w_down, final_norm_g, loss_target, m_w_ada, m_b_ada, m_norm_ffn1_g, m_ffn1_w_gate, m_ffn1_w_up, m_ffn1_w_down, m_norm_mix_g, m_w_in, m_attn_sinks, m_w_attn_o, m_conv_w_dw, m_conv_b_dw, m_conv_ln_g, m_conv_ln_b, m_w_conv_o, m_w_out, m_norm_ffn2_g, m_ffn2_w_gate, m_ffn2_w_up, m_ffn2_w_down, m_final_norm_g, v_w_ada, v_b_ada, v_norm_ffn1_g, v_ffn1_w_gate, v_ffn1_w_up, v_ffn1_w_down, v_norm_mix_g, v_w_in, v_attn_sinks, v_w_attn_o, v_conv_w_dw, v_conv_b_dw, v_conv_ln_g, v_conv_ln_b, v_w_conv_o, v_w_out, v_norm_ffn2_g, v_ffn2_w_gate, v_ffn2_w_up, v_ffn2_w_down, v_final_norm_g):
    B, S, D = x.shape
    T = B * S
    QW = N_Q_HEADS * HEAD_DIM
    CC = conv_w_dw.shape[2] * N_DEV
    me = _lin(_my_pos())
    xf = x.reshape(T, D)
    tgt = loss_target.reshape(T, D)
    tm = min(512, S)
    kw = dict(seq=S, tm=tm)

    o_k, o_v, o_ca, o_ga, o_gc, o_end = QW, QW + KV_WIDTH, QW + 2 * KV_WIDTH, QW + 2 * KV_WIDTH + 2 * CC, \
        QW + 2 * KV_WIDTH + 2 * CC + D, QW + 2 * KV_WIDTH + 2 * CC + 2 * D
    p_ca, p_cb, p_ga, p_gc, p_k, p_v = QW, QW + CC, QW + 2 * CC, QW + 2 * CC + D, QW + 2 * CC + 2 * D, \
        QW + 2 * CC + 2 * D + KV_WIDTH

    def to_local_order(w):
        return jnp.concatenate([w[:o_k], w[o_ca:o_end], w[o_k:o_ca]], axis=0)

    def to_ref_order(w):
        return jnp.concatenate([w[:QW], w[p_k:], w[p_ca:p_k]], axis=0)

    def col_t(w):
        return w[0].T.astype(BF16)

    def row_b(w):
        return w[0].astype(BF16)

    shards = [col_t(ffn1_w_gate), col_t(ffn1_w_up), row_b(ffn1_w_down), col_t(w_in), row_b(w_attn_o),
              row_b(w_conv_o), row_b(w_out), col_t(ffn2_w_gate), col_t(ffn2_w_up), row_b(ffn2_w_down),
              conv_w_dw[0], c]
    gathered = _exchange(shards, scatter=False, name="gather_weights")
    wg1, wu1, wd1, win_ref_order, wao, wco, wout, wg2, wu2, wd2 = [
        g.reshape(-1, g.shape[-1]) for g in gathered[:10]]
    winp = to_local_order(win_ref_order)
    conv_w = gathered[10].transpose(1, 0, 2).reshape(CONV_WIDTH, CC)
    c_all = gathered[11].reshape(N_DEV * B, D)

    n_col = N_MOD * D // N_DEV
    b_cols = lax.dynamic_slice(b_ada, (0, me * n_col), (1, n_col))
    mod_cols = _ada_fwd(c_all, w_ada[0], b_cols, name="ada_fwd")
    mod_mine = _exchange([mod_cols.reshape(N_DEV, B, n_col)], scatter=True, name="scatter_mod")[0]
    mod = mod_mine.transpose(1, 0, 2).reshape(B, N_MOD, 1, D)
    sh1, sc1, g1, sh2, sc2, g2, sh3, sc3, g3 = [mod[:, i] for i in range(N_MOD)]

    F = wg1.shape[0]
    tn_f = _pick(F, (1408, 1024, 512, 256))
    tn_in = _pick(winp.shape[0], (1792, 768, 512, 256))
    gate_blk = dict(ga_blk=p_ga // D, gc_blk=p_gc // D)
    att_blk = dict(q_blk=0, k_blk=p_k // KV_WIDTH, v_blk=p_v // KV_WIDTH)
    conv_kw = dict(seq=S, cw=256, a_col=p_ca, b_col=p_cb)

    h1, (a1, b1) = _norm_mod_matmul(xf, norm_ffn1_g, sh1, sc1, [wg1, wu1], tn=tn_f, name="ffn1_up", **kw)
    x1, y1 = _ffn_down(a1, b1, wd1, xf, g1, name="ffn1_down", **kw)
    h2, (projp,) = _norm_mod_matmul(x1, norm_mix_g, sh2, sc2, [winp], tn=tn_in, name="mix_in", **kw)
    ao = _attn_fwd(projp, attn_sinks, seq=S, name="attn_fwd", **att_blk)
    yc = _conv_fwd(projp, conv_w, conv_b_dw, name="conv_fwd", **conv_kw)
    x2, z, ya, ycv, cact, merged = _mix_out(ao, yc, projp, wao, wco, wout, x1, g2, conv_ln_g, conv_ln_b,
                                            name="mix_out", **gate_blk, **kw)
    h3, (a3, b3) = _norm_mod_matmul(x2, norm_ffn2_g, sh3, sc3, [wg2, wu2], tn=tn_f, name="ffn2_up", **kw)
    x3, y3 = _ffn_down(a3, b3, wd2, x2, g3, name="ffn2_down", **kw)
    dx3, loss_row, dgf = _final_loss(x3, final_norm_g[None], tgt, tm=tm, name="final_loss")
    loss = lax.psum(loss_row[0, 0], ("x", "y", "c"))

    dyb3, da3, db3, act3, dg3 = _ffn_bwd_down(dx3, g3, y3, wd2, a3, b3, tn=tn_f, name="ffn2_bwd_down", **kw)
    dx2, dsh3, dsc3, dgn3 = _matmul_norm_mod_bwd([da3, db3], [wg2, wu2], x2, norm_ffn2_g, sc3, dx3,
                                                 name="ffn2_bwd_up", **kw)
    gwd2 = _matmul_tn(act3, dyb3, name="gw_ffn2_down")
    gwg2 = _matmul_tn(da3, h3, name="gw_ffn2_gate")
    gwu2 = _matmul_tn(db3, h3, name="gw_ffn2_up")

    dzb, dyab, dycb, dga, dgc, dao, dyc, dg2, dlng, dlnb = _mix_out_bwd(
        dx2, g2, z, wout, projp, ya, ycv, wao, wco, yc, conv_ln_g, conv_ln_b, name="mix_out_bwd", **gate_blk, **kw)
    gwout = _matmul_tn(merged, dzb, name="gw_out")
    gwao = _matmul_tn(ao, dyab, name="gw_attn_o")
    gwco = _matmul_tn(cact, dycb, name="gw_conv_o")
    dq, dk, dv, dsinks = _attn_bwd(projp, dao, attn_sinks, seq=S, name="attn_bwd", **att_blk)
    dca, dcb, dconvw, dconvb = _conv_bwd(dyc, projp, conv_w, name="conv_bwd", **conv_kw)
    dprojp = jnp.concatenate([dq, dca, dcb, dga, dgc, dk, dv], axis=1)
    dx1, dsh2, dsc2, dgn2 = _matmul_norm_mod_bwd([dprojp], [winp], x1, norm_mix_g, sc2, dx2,
                                                 name="mix_in_bwd", **kw)
    gwin = to_ref_order(_matmul_tn(dprojp, h2, name="gw_in"))

    dyb1, da1, db1, act1, dg1 = _ffn_bwd_down(dx1, g1, y1, wd1, a1, b1, tn=tn_f, name="ffn1_bwd_down", **kw)
    dx0, dsh1, dsc1, dgn1 = _matmul_norm_mod_bwd([da1, db1], [wg1, wu1], xf, norm_ffn1_g, sc1, dx1,
                                                 name="ffn1_bwd_up", **kw)
    gwd1 = _matmul_tn(act1, dyb1, name="gw_ffn1_down")
    gwg1 = _matmul_tn(da1, h1, name="gw_ffn1_gate")
    gwu1 = _matmul_tn(db1, h1, name="gw_ffn1_up")

    gmod = jnp.concatenate([dsh1, dsc1, dg1, dsh2, dsc2, dg2, dsh3, dsc3, dg3], axis=1)
    gmod = gmod.reshape(B, N_DEV, n_col).transpose(1, 0, 2)
    gcols = _exchange([gmod], scatter=True, name="scatter_gmod")[0].reshape(N_DEV * B, n_col)
    g_w_ada, gb_cols = _ada_bwd(c_all, gcols, name="ada_bwd")

    n_small = 8
    sink_row = jnp.pad(dsinks[:, :N_Q_HEADS], ((0, 0), (0, D - N_Q_HEADS)))
    small = jnp.concatenate([dgn1, dgn2, dgn3, dgf, dconvb, dlng, dlnb, sink_row, dconvw,
                             jnp.zeros((1, D), F32)], axis=0)
    small_all, gb_all = _exchange([small, gb_cols], scatter=False, name="gather_small")
    gsmall = _sum8(small_all, name="sum_small")
    g_b_ada = gb_all.reshape(1, N_MOD * D)
    g_conv_w = lax.dynamic_slice(gsmall[n_small:n_small + CONV_WIDTH], (0, me * (CC // N_DEV)),
                                 (CONV_WIDTH, CC // N_DEV))

    big = [gwg1, gwu1, gwd1, gwin, gwao, gwco, gwout, gwg2, gwu2, gwd2]
    parts = _exchange([g.reshape(N_DEV, g.shape[0] // N_DEV, g.shape[1]) for g in big], scatter=True,
                      name="scatter_grads")

    def col_update(p, w, m, v, name):
        g = _sum8(p, name="sum_" + name).T
        return (g,) + tuple(_adamw(g, w[0], m[0], v[0], name="adamw_" + name))

    def row_update(p, w, m, v, name):
        return tuple(_sum8_adamw(p, w[0], m[0], v[0], name="adamw_" + name))

    upd = {
        "ffn1_w_gate": col_update(parts[0], ffn1_w_gate, m_ffn1_w_gate, v_ffn1_w_gate, "ffn1_w_gate"),
        "ffn1_w_up": col_update(parts[1], ffn1_w_up, m_ffn1_w_up, v_ffn1_w_up, "ffn1_w_up"),
        "ffn1_w_down": row_update(parts[2], ffn1_w_down, m_ffn1_w_down, v_ffn1_w_down, "ffn1_w_down"),
        "w_in": col_update(parts[3], w_in, m_w_in, v_w_in, "w_in"),
        "w_attn_o": row_update(parts[4], w_attn_o, m_w_attn_o, v_w_attn_o, "w_attn_o"),
        "w_conv_o": row_update(parts[5], w_conv_o, m_w_conv_o, v_w_conv_o, "w_conv_o"),
        "w_out": row_update(parts[6], w_out, m_w_out, v_w_out, "w_out"),
        "ffn2_w_gate": col_update(parts[7], ffn2_w_gate, m_ffn2_w_gate, v_ffn2_w_gate, "ffn2_w_gate"),
        "ffn2_w_up": col_update(parts[8], ffn2_w_up, m_ffn2_w_up, v_ffn2_w_up, "ffn2_w_up"),
        "ffn2_w_down": row_update(parts[9], ffn2_w_down, m_ffn2_w_down, v_ffn2_w_down, "ffn2_w_down"),
        "w_ada": (g_w_ada,) + tuple(_adamw(g_w_ada, w_ada[0], m_w_ada[0], v_w_ada[0], name="adamw_w_ada")),
        "conv_w_dw": (g_conv_w,) + tuple(_adamw(g_conv_w, conv_w_dw[0], m_conv_w_dw[0], v_conv_w_dw[0],
                                                name="adamw_conv_w_dw")),
    }
    for k in upd:
        upd[k] = tuple(t[None] for t in upd[k])

    def pad_sinks(t):
        return jnp.pad(t, ((0, 0), (0, D - N_Q_HEADS)))

    def pack(f1, mix, f2, fin, cb, lg, lb, sinks, bada):
        return jnp.concatenate([f1, mix, f2, fin[None], cb, lg, lb, pad_sinks(sinks), bada.reshape(N_MOD, D)], axis=0)

    w_s = pack(norm_ffn1_g, norm_mix_g, norm_ffn2_g, final_norm_g, conv_b_dw, conv_ln_g, conv_ln_b, attn_sinks, b_ada)
    m_s = pack(m_norm_ffn1_g, m_norm_mix_g, m_norm_ffn2_g, m_final_norm_g, m_conv_b_dw, m_conv_ln_g, m_conv_ln_b,
               m_attn_sinks, m_b_ada)
    v_s = pack(v_norm_ffn1_g, v_norm_mix_g, v_norm_ffn2_g, v_final_norm_g, v_conv_b_dw, v_conv_ln_g, v_conv_ln_b,
               v_attn_sinks, v_b_ada)
    g_s = jnp.concatenate([gsmall[:n_small], g_b_ada.reshape(N_MOD, D)], axis=0)
    small_out = (g_s,) + tuple(_adamw(g_s, w_s, m_s, v_s, name="adamw_vectors"))

    def unpack(t):
        return {
            "norm_ffn1_g": t[0:1], "norm_mix_g": t[1:2], "norm_ffn2_g": t[2:3], "final_norm_g": t[3],
            "conv_b_dw": t[4:5], "conv_ln_g": t[5:6], "conv_ln_b": t[6:7], "attn_sinks": t[7:8, :N_Q_HEADS],
            "b_ada": t[n_small:n_small + N_MOD].reshape(1, N_MOD * D),
        }

    small_un = [unpack(t) for t in small_out]
    for k in small_un[0]:
        upd[k] = tuple(s[k] for s in small_un)

    order = ["w_ada", "b_ada", "norm_ffn1_g", "ffn1_w_gate", "ffn1_w_up", "ffn1_w_down", "norm_mix_g", "w_in",
             "attn_sinks", "w_attn_o", "conv_w_dw", "conv_b_dw", "conv_ln_g", "conv_ln_b", "w_conv_o", "w_out",
             "norm_ffn2_g", "ffn2_w_gate", "ffn2_w_up", "ffn2_w_down", "final_norm_g"]
    grad_x = dx0.reshape(B, S, D)
    return (loss, grad_x, *[upd[k][0] for k in order], *[upd[k][1] for k in order],
            *[upd[k][2] for k in order], *[upd[k][3] for k in order])
```

```python
import jax
import jax.numpy as jnp
from jax import lax
from jax.experimental import pallas as pl
from jax.experimental.pallas import tpu as pltpu

F32 = jnp.float32
BF16 = jnp.bfloat16
SDS = jax.ShapeDtypeStruct
MESH = pl.DeviceIdType.MESH

N_DEV = 8
EPS = 1e-6
HEAD_DIM = 64
N_Q_HEADS = 16
N_KV_HEADS = 2
GQA_GROUP = N_Q_HEADS // N_KV_HEADS
KV_WIDTH = N_KV_HEADS * HEAD_DIM
ATT_BLOCK = 128
CONV_WIDTH = 31
CONV_HALO = 32
CONV_ROWS = 64
N_MOD = 9
FFN_RESIDUAL = 0.5
ADAM_LR = 0.001
ADAM_B1 = 0.9
ADAM_B2 = 0.999
ADAM_EPS = 1e-08
ADAM_WD = 0.01
ADAM_STEP = 10
NEG_BIG = -1e30

V7X_VMEM_BYTES = 64 * 2**20
VMEM_CAP = V7X_VMEM_BYTES - 8 * 2**20


def _nbytes(shape, dtype):
    n = 1
    for s in shape:
        n *= s
    return n * jnp.dtype(dtype).itemsize


def _params(n_axes, blocks, temp_bytes=0):
    need = 2 * sum(_nbytes(s, d) for s, d in blocks) + temp_bytes + 4 * 2**20
    return pltpu.CompilerParams(dimension_semantics=("arbitrary",) * n_axes,
                                vmem_limit_bytes=int(min(max(need, 16 * 2**20), VMEM_CAP)))


def _dot_nt(a, b):
    return lax.dot_general(a, b, (((1,), (1,)), ((), ())), preferred_element_type=F32)


def _dot_tn(a, b):
    return lax.dot_general(a, b, (((0,), (0,)), ((), ())), preferred_element_type=F32)


def _dot(a, b):
    return jnp.dot(a, b, preferred_element_type=F32)


def _sigmoid(x):
    return jax.nn.sigmoid(x)


def _rowsum(v):
    return jnp.sum(v, axis=0, keepdims=True)


def _acc(ref, val, first):
    @pl.when(first)
    def _():
        ref[...] = val

    @pl.when(jnp.logical_not(first))
    def _():
        ref[...] = ref[...] + val


def _norm_mod(xf, gn, sh, sc):
    rstd = lax.rsqrt(jnp.mean(xf * xf, axis=-1, keepdims=True) + EPS)
    xhat = xf * rstd
    yn = xhat * gn
    return yn * (1.0 + sc) + sh, xhat, rstd, yn


def _pick(n, cands):
    for c in cands:
        if n % c == 0:
            return c
    return n


def _my_pos():
    return lax.axis_index("x"), lax.axis_index("y"), lax.axis_index("c")


def _peer(pos, k):
    x, y, c = pos
    return ((1 - x) if k & 4 else x, (1 - y) if k & 2 else y, (1 - c) if k & 1 else c)


def _lin(pos):
    return 4 * pos[0] + 2 * pos[1] + pos[2]


class _Comm:
    N_COPY = N_DEV - 1

    def __init__(self, items):
        self.arrs = [a for a, _ in items]
        self.modes = [m for _, m in items]
        self.n = len(items)
        self.out = None

    def out_shape(self):
        return [SDS(a.shape if m == "scatter" else (N_DEV,) + a.shape, a.dtype)
                for a, m in zip(self.arrs, self.modes)]

    def scratch(self):
        return [pltpu.SemaphoreType.DMA((self.n * self.N_COPY,)), pltpu.SemaphoreType.DMA((self.n * self.N_COPY,)),
                pltpu.SemaphoreType.DMA((self.n,))]

    def _pair(self, refs, me, i, k):
        srcs, outs, (send_sems, recv_sems, _) = refs
        x, y, c = me
        if self.modes[i] == "scatter":
            peer = _peer(me, k + 1)
            src, send_slot, recv_slot = srcs[i].at[_lin(peer)], _lin(me), _lin(peer)
        else:
            sib = (x, y, 1 - c)
            chips = [(1 - x, y), (x, 1 - y), (1 - x, 1 - y)]
            if k == 0:
                peer, src, send_slot, recv_slot = sib, srcs[i], _lin(me), _lin(sib)
            elif k <= 3:
                peer = (*chips[k - 1], c)
                src, send_slot, recv_slot = srcs[i], _lin(me), _lin(peer)
            else:
                peer = sib
                send_slot = _lin((*chips[k - 4], c))
                src, recv_slot = outs[i].at[send_slot], _lin((*chips[k - 4], 1 - c))
        sem = i * self.N_COPY + k

        def copy(slot):
            return pltpu.make_async_remote_copy(
                src_ref=src, dst_ref=outs[i].at[slot], send_sem=send_sems.at[sem], recv_sem=recv_sems.at[sem],
                device_id=peer, device_id_type=MESH)

        return copy(send_slot), copy(recv_slot)

    def _local(self, refs, me, i):
        srcs, outs, (_, _, loc_sems) = refs
        own = srcs[i].at[_lin(me)] if self.modes[i] == "scatter" else srcs[i]
        return pltpu.make_async_copy(own, outs[i].at[_lin(me)], loc_sems.at[i])

    def start(self, refs):
        me = _my_pos()
        for i in range(self.n):
            self._local(refs, me, i).start()
            for k in range(self.N_COPY if self.modes[i] == "scatter" else 4):
                self._pair(refs, me, i, k)[0].start()

    def finish(self, refs):
        me = _my_pos()
        for i in range(self.n):
            if self.modes[i] == "gather":
                for j in range(3):
                    self._pair(refs, me, i, 1 + j)[1].wait_recv()
                    self._pair(refs, me, i, 4 + j)[0].start()
        for i in range(self.n):
            for k in range(self.N_COPY):
                send, recv = self._pair(refs, me, i, k)
                if not (self.modes[i] == "gather" and 1 <= k <= 3):
                    recv.wait_recv()
                send.wait_send()
            self._local(refs, me, i).wait()


_ANY = pl.BlockSpec(memory_space=pl.ANY)


def _call(body, args, *, name, grid, in_specs, out_specs, out_shape, params, scratch_shapes=(), comm=None):
    in_specs, out_specs, out_shape = list(in_specs), list(out_specs), list(out_shape)
    scratch_shapes = list(scratch_shapes)
    if comm is None:
        return list(pl.pallas_call(body, name=name, grid=grid, in_specs=in_specs, out_specs=out_specs,
                                   out_shape=out_shape, scratch_shapes=scratch_shapes, compiler_params=params)(*args))
    n_in, n_out, n_scr, nc = len(in_specs), len(out_specs), len(scratch_shapes), comm.n

    def hosted(*refs):
        ins, c_in = refs[:n_in], refs[n_in:n_in + nc]
        outs = refs[n_in + nc:n_in + nc + n_out]
        c_out = refs[n_in + nc + n_out:n_in + 2 * nc + n_out]
        scr = refs[n_in + 2 * nc + n_out:n_in + 2 * nc + n_out + n_scr]
        sems = refs[n_in + 2 * nc + n_out + n_scr:]
        first = pl.program_id(0) == 0
        last = pl.program_id(0) == grid[0] - 1
        for d in range(1, len(grid)):
            first = jnp.logical_and(first, pl.program_id(d) == 0)
            last = jnp.logical_and(last, pl.program_id(d) == grid[d] - 1)

        @pl.when(first)
        def _():
            comm.start((c_in, c_out, sems))

        body(*ins, *outs, *scr)

        @pl.when(last)
        def _():
            comm.finish((c_in, c_out, sems))

    res = pl.pallas_call(
        hosted, name=name, grid=grid, in_specs=in_specs + [_ANY] * nc, out_specs=out_specs + [_ANY] * nc,
        out_shape=out_shape + comm.out_shape(), scratch_shapes=scratch_shapes + comm.scratch(),
        compiler_params=params)(*args, *comm.arrs)
    comm.out = list(res[n_out:])
    return list(res[:n_out])


def _exchange(items, *, name):
    comm = _Comm(items)

    def body(*refs):
        r = (refs[:comm.n], refs[comm.n:2 * comm.n], refs[2 * comm.n:])
        comm.start(r)
        comm.finish(r)

    return list(pl.pallas_call(body, name=name, out_shape=comm.out_shape(), in_specs=[_ANY] * comm.n,
                               out_specs=[_ANY] * comm.n, scratch_shapes=comm.scratch())(*comm.arrs))


def _norm_mod_matmul(x, gn, sh, sc, wts, *, seq, tm, tn, name, comm=None):
    T, D = x.shape
    N = wts[0].shape[0]
    nw = len(wts)
    tps = seq // tm

    def body(x_ref, gn_ref, sh_ref, sc_ref, *rest):
        w_refs, h_ref, o_refs = rest[:nw], rest[nw], rest[nw + 1:]

        @pl.when(pl.program_id(1) == 0)
        def _():
            h_ref[...] = _norm_mod(x_ref[...], gn_ref[...], sh_ref[0], sc_ref[0])[0].astype(BF16)

        h = h_ref[...]
        for w_ref, o_ref in zip(w_refs, o_refs):
            o_ref[...] = _dot_nt(h, w_ref[...]).astype(o_ref.dtype)

    row = pl.BlockSpec((tm, D), lambda i, j: (i, 0))
    vec = pl.BlockSpec((1, D), lambda i, j: (0, 0))
    per_b = pl.BlockSpec((1, 1, D), lambda i, j: (i // tps, 0, 0))
    wspec = pl.BlockSpec((tn, D), lambda i, j: (j, 0))
    ospec = pl.BlockSpec((tm, tn), lambda i, j: (i, j))
    blocks = [((tm, D), F32), ((tm, D), BF16)] + [((tn, D), BF16), ((tm, tn), BF16)] * nw
    outs = _call(
        body, (x, gn, sh, sc, *wts), name=name, grid=(T // tm, N // tn),
        in_specs=[row, vec, per_b, per_b] + [wspec] * nw,
        out_specs=[row] + [ospec] * nw,
        out_shape=[SDS((T, D), BF16)] + [SDS((T, N), BF16)] * nw,
        params=_params(2, blocks, temp_bytes=2 * _nbytes((tm, tn), F32) + 3 * _nbytes((tm, D), F32)), comm=comm)
    return outs[0], outs[1:]


def _ffn_down(a, b, wd, x, g, *, seq, tm, name, comm=None):
    T, F = a.shape
    D = wd.shape[1]
    tps = seq // tm

    def body(a_ref, b_ref, wd_ref, x_ref, g_ref, xo_ref, y_ref):
        af = a_ref[...].astype(F32)
        act = (af * _sigmoid(af) * b_ref[...].astype(F32)).astype(BF16)
        y = _dot(act, wd_ref[...])
        xo_ref[...] = x_ref[...] + (FFN_RESIDUAL * g_ref[0]) * y
        y_ref[...] = y.astype(BF16)

    wide = pl.BlockSpec((tm, F), lambda i: (i, 0))
    row = pl.BlockSpec((tm, D), lambda i: (i, 0))
    per_b = pl.BlockSpec((1, 1, D), lambda i: (i // tps, 0, 0))
    wspec = pl.BlockSpec((F, D), lambda i: (0, 0))
    blocks = [((tm, F), BF16)] * 2 + [((F, D), BF16), ((tm, D), F32), ((tm, D), F32), ((tm, D), BF16)]
    return _call(
        body, (a, b, wd, x, g), name=name, grid=(T // tm,),
        in_specs=[wide, wide, wspec, row, per_b], out_specs=[row, row],
        out_shape=[SDS((T, D), F32), SDS((T, D), BF16)],
        params=_params(1, blocks, temp_bytes=3 * _nbytes((tm, F), F32)), comm=comm)


def _final_loss(x, gf, tgt, *, tm, name):
    T, D = x.shape
    nt = T // tm

    def body(x_ref, gf_ref, t_ref, dx_ref, loss_ref, dgf_ref, lacc):
        i = pl.program_id(0)
        xf = x_ref[...]
        gfv = gf_ref[...]
        rstd = lax.rsqrt(jnp.mean(xf * xf, axis=-1, keepdims=True) + EPS)
        xhat = xf * rstd
        err = xhat * gfv - t_ref[...]
        dy = err * (1.0 / D)
        dxhat = dy * gfv
        dx_ref[...] = rstd * (dxhat - xhat * jnp.mean(dxhat * xhat, axis=-1, keepdims=True))
        _acc(dgf_ref, _rowsum(dy * xhat), i == 0)
        _acc(lacc, _rowsum(err * err), i == 0)

        @pl.when(i == nt - 1)
        def _():
            loss_ref[...] = jnp.broadcast_to((0.5 / D) * jnp.sum(lacc[...]), loss_ref.shape)

    row = pl.BlockSpec((tm, D), lambda i: (i, 0))
    vec = pl.BlockSpec((1, D), lambda i: (0, 0))
    lspec = pl.BlockSpec((1, 128), lambda i: (0, 0))
    blocks = [((tm, D), F32)] * 3
    return _call(
        body, (x, gf, tgt), name=name, grid=(nt,),
        in_specs=[row, vec, row], out_specs=[row, lspec, vec],
        out_shape=[SDS((T, D), F32), SDS((1, 128), F32), SDS((1, D), F32)],
        scratch_shapes=[pltpu.VMEM((1, D), F32)],
        params=_params(1, blocks, temp_bytes=4 * _nbytes((tm, D), F32)))


def _ffn_bwd_down(dxo, g, y, wd, a, b, *, seq, tm, tn, name, comm=None):
    T, F = a.shape
    D = wd.shape[1]
    tps = seq // tm
    nb = T // seq

    def body(dxo_ref, g_ref, y_ref, wd_ref, a_ref, b_ref, dyb_ref, da_ref, db_ref, act_ref, dg_ref):
        i = pl.program_id(0)

        @pl.when(pl.program_id(1) == 0)
        def _():
            dx = dxo_ref[...]
            dyb_ref[...] = ((FFN_RESIDUAL * g_ref[0]) * dx).astype(BF16)
            part = _rowsum(FFN_RESIDUAL * dx * y_ref[...].astype(F32))
            _acc(dg_ref, part[None], i % tps == 0)

        dact = _dot_nt(dyb_ref[...], wd_ref[...])
        af = a_ref[...].astype(F32)
        bf = b_ref[...].astype(F32)
        sg = _sigmoid(af)
        silu = af * sg
        act_ref[...] = (silu * bf).astype(BF16)
        da_ref[...] = (dact * bf * (sg * (1.0 + af * (1.0 - sg)))).astype(BF16)
        db_ref[...] = (dact * silu).astype(BF16)

    row = pl.BlockSpec((tm, D), lambda i, j: (i, 0))
    per_b = pl.BlockSpec((1, 1, D), lambda i, j: (i // tps, 0, 0))
    wspec = pl.BlockSpec((tn, D), lambda i, j: (j, 0))
    chunk = pl.BlockSpec((tm, tn), lambda i, j: (i, j))
    blocks = [((tm, D), F32), ((tm, D), BF16), ((tn, D), BF16), ((tm, D), BF16)] + [((tm, tn), BF16)] * 5
    return _call(
        body, (dxo, g, y, wd, a, b), name=name, grid=(T // tm, F // tn),
        in_specs=[row, per_b, row, wspec, chunk, chunk],
        out_specs=[row, chunk, chunk, chunk, per_b],
        out_shape=[SDS((T, D), BF16)] + [SDS((T, F), BF16)] * 3 + [SDS((nb, 1, D), F32)],
        params=_params(2, blocks, temp_bytes=6 * _nbytes((tm, tn), F32)), comm=comm)


def _matmul_norm_mod_bwd(ds, ws, x, gn, sc, dxo, *, seq, tm, name, comm=None):
    T, D = x.shape
    nk = len(ds)
    tps = seq // tm
    nb = T // seq

    def body(*refs):
        d_refs, w_refs = refs[:nk], refs[nk:2 * nk]
        x_ref, gn_ref, sc_ref, dxo_ref, dxi_ref, dsh_ref, dsc_ref, dgn_ref = refs[2 * nk:]
        i = pl.program_id(0)
        dh = _dot(d_refs[0][...], w_refs[0][...])
        for d_ref, w_ref in zip(d_refs[1:], w_refs[1:]):
            dh = dh + _dot(d_ref[...], w_ref[...])
        gnv = gn_ref[...]
        scv = sc_ref[0]
        _, xhat, rstd, yn = _norm_mod(x_ref[...], gnv, 0.0, scv)
        dyn = dh * (1.0 + scv)
        dxhat = dyn * gnv
        dxi_ref[...] = dxo_ref[...] + rstd * (dxhat - xhat * jnp.mean(dxhat * xhat, axis=-1, keepdims=True))
        first_of_seq = i % tps == 0
        _acc(dsh_ref, _rowsum(dh)[None], first_of_seq)
        _acc(dsc_ref, _rowsum(dh * yn)[None], first_of_seq)
        _acc(dgn_ref, _rowsum(dyn * xhat), i == 0)

    row = pl.BlockSpec((tm, D), lambda i: (i, 0))
    vec = pl.BlockSpec((1, D), lambda i: (0, 0))
    per_b = pl.BlockSpec((1, 1, D), lambda i: (i // tps, 0, 0))
    d_specs = [pl.BlockSpec((tm, d.shape[1]), lambda i: (i, 0)) for d in ds]
    w_specs = [pl.BlockSpec(w.shape, lambda i: (0, 0)) for w in ws]
    blocks = ([((tm, d.shape[1]), BF16) for d in ds] + [(w.shape, BF16) for w in ws] + [((tm, D), F32)] * 3)
    return _call(
        body, (*ds, *ws, x, gn, sc, dxo), name=name, grid=(T // tm,),
        in_specs=d_specs + w_specs + [row, vec, per_b, row],
        out_specs=[row, per_b, per_b, vec],
        out_shape=[SDS((T, D), F32), SDS((nb, 1, D), F32), SDS((nb, 1, D), F32), SDS((1, D), F32)],
        params=_params(1, blocks, temp_bytes=6 * _nbytes((tm, D), F32)), comm=comm)


def _layernorm_silu(yc, lg, lb):
    mu = jnp.mean(yc, axis=-1, keepdims=True)
    cen = yc - mu
    rstd = lax.rsqrt(jnp.mean(cen * cen, axis=-1, keepdims=True) + EPS)
    xh = cen * rstd
    l = xh * lg + lb
    s = _sigmoid(l)
    return l * s, xh, rstd, l, s


def _mix_out(ao, yc, projp, wao, wco, wout, x1, g2, lg, lb, *, seq, tm, ga_blk, gc_blk, name, comm=None):
    T, D = x1.shape
    tps = seq // tm

    def body(ao_ref, yc_ref, ga_ref, gc_ref, wao_ref, wco_ref, wout_ref, x1_ref, g2_ref, lg_ref, lb_ref,
             x2_ref, z_ref, ya_ref, ycv_ref, cact_ref, mrg_ref):
        ya = _dot(ao_ref[...], wao_ref[...])
        cact = _layernorm_silu(yc_ref[...], lg_ref[...], lb_ref[...])[0].astype(BF16)
        ycv = _dot(cact, wco_ref[...])
        merged = (_sigmoid(ga_ref[...].astype(F32)) * ya + _sigmoid(gc_ref[...].astype(F32)) * ycv).astype(BF16)
        z = _dot(merged, wout_ref[...])
        x2_ref[...] = x1_ref[...] + g2_ref[0] * z
        z_ref[...] = z.astype(BF16)
        ya_ref[...] = ya.astype(BF16)
        ycv_ref[...] = ycv.astype(BF16)
        cact_ref[...] = cact
        mrg_ref[...] = merged

    row = pl.BlockSpec((tm, D), lambda i: (i, 0))
    vec = pl.BlockSpec((1, D), lambda i: (0, 0))
    per_b = pl.BlockSpec((1, 1, D), lambda i: (i // tps, 0, 0))
    wspec = pl.BlockSpec((D, D), lambda i: (0, 0))
    ga_spec = pl.BlockSpec((tm, D), lambda i: (i, ga_blk))
    gc_spec = pl.BlockSpec((tm, D), lambda i: (i, gc_blk))
    blocks = ([((tm, D), BF16), ((tm, D), F32), ((tm, D), BF16), ((tm, D), BF16)] + [((D, D), BF16)] * 3
              + [((tm, D), F32)] * 2 + [((tm, D), BF16)] * 5)
    return _call(
        body, (ao, yc, projp, projp, wao, wco, wout, x1, g2, lg, lb), name=name, grid=(T // tm,),
        in_specs=[row, row, ga_spec, gc_spec, wspec, wspec, wspec, row, per_b, vec, vec],
        out_specs=[row] * 6,
        out_shape=[SDS((T, D), F32)] + [SDS((T, D), BF16)] * 5,
        params=_params(1, blocks, temp_bytes=8 * _nbytes((tm, D), F32)), comm=comm)


def _mix_out_bwd(dx2, g2, z, wout, projp, ya, ycv, wao, wco, yc, lg, lb, *, seq, tm, ga_blk, gc_blk, name,
                 comm=None):
    T, D = dx2.shape
    tps = seq // tm
    nb = T // seq

    def body(dx2_ref, g2_ref, z_ref, wout_ref, ga_ref, gc_ref, ya_ref, ycv_ref, wao_ref, wco_ref, yc_ref,
             lg_ref, lb_ref, dz_ref, dya_ref, dycv_ref, dga_ref, dgc_ref, dao_ref, dyc_ref, dg2_ref, dlg_ref,
             dlb_ref):
        i = pl.program_id(0)
        dx = dx2_ref[...]
        _acc(dg2_ref, _rowsum(dx * z_ref[...].astype(F32))[None], i % tps == 0)
        dzb = (g2_ref[0] * dx).astype(BF16)
        dz_ref[...] = dzb
        dmerged = _dot_nt(dzb, wout_ref[...])
        sa = _sigmoid(ga_ref[...].astype(F32))
        sc_ = _sigmoid(gc_ref[...].astype(F32))
        dya = (dmerged * sa).astype(BF16)
        dycv = (dmerged * sc_).astype(BF16)
        dya_ref[...] = dya
        dycv_ref[...] = dycv
        dga_ref[...] = (dmerged * ya_ref[...].astype(F32) * (sa * (1.0 - sa))).astype(BF16)
        dgc_ref[...] = (dmerged * ycv_ref[...].astype(F32) * (sc_ * (1.0 - sc_))).astype(BF16)
        dao_ref[...] = _dot_nt(dya, wao_ref[...]).astype(BF16)
        dcact = _dot_nt(dycv, wco_ref[...])
        lgv = lg_ref[...]
        _, xh, rstd, l, s = _layernorm_silu(yc_ref[...], lgv, lb_ref[...])
        dl = dcact * (s * (1.0 + l * (1.0 - s)))
        _acc(dlb_ref, _rowsum(dl), i == 0)
        _acc(dlg_ref, _rowsum(dl * xh), i == 0)
        dxh = dl * lgv
        dyc_ref[...] = rstd * (dxh - jnp.mean(dxh, axis=-1, keepdims=True)
                               - xh * jnp.mean(dxh * xh, axis=-1, keepdims=True))

    row = pl.BlockSpec((tm, D), lambda i: (i, 0))
    vec = pl.BlockSpec((1, D), lambda i: (0, 0))
    per_b = pl.BlockSpec((1, 1, D), lambda i: (i // tps, 0, 0))
    wspec = pl.BlockSpec((D, D), lambda i: (0, 0))
    ga_spec = pl.BlockSpec((tm, D), lambda i: (i, ga_blk))
    gc_spec = pl.BlockSpec((tm, D), lambda i: (i, gc_blk))
    blocks = ([((tm, D), F32)] * 3 + [((tm, D), BF16)] * 11 + [((D, D), BF16)] * 3)
    return _call(
        body, (dx2, g2, z, wout, projp, projp, ya, ycv, wao, wco, yc, lg, lb), name=name, grid=(T // tm,),
        in_specs=[row, per_b, row, wspec, ga_spec, gc_spec, row, row, wspec, wspec, row, vec, vec],
        out_specs=[row] * 7 + [per_b, vec, vec],
        out_shape=[SDS((T, D), BF16)] * 6 + [SDS((T, D), F32), SDS((nb, 1, D), F32), SDS((1, D), F32),
                                             SDS((1, D), F32)],
        params=_params(1, blocks, temp_bytes=10 * _nbytes((tm, D), F32)), comm=comm)


def _band_mask(n):
    qi = lax.broadcasted_iota(jnp.int32, (ATT_BLOCK, 2 * ATT_BLOCK), 0)
    sj = lax.broadcasted_iota(jnp.int32, (ATT_BLOCK, 2 * ATT_BLOCK), 1)
    rel = qi + ATT_BLOCK - sj
    in_window = jnp.logical_and(rel >= 0, rel < ATT_BLOCK)
    return jnp.logical_and(in_window, jnp.logical_or(sj >= ATT_BLOCK, n > 0))


def _head_probs(qh, kg, valid, sink):
    s = _dot_nt(qh, kg) * (HEAD_DIM ** -0.5)
    s = jnp.where(valid, s, NEG_BIG)
    m = jnp.maximum(jnp.max(s, axis=-1, keepdims=True), sink)
    p = jnp.exp(s - m)
    psink = jnp.exp(sink - m)
    inv = 1.0 / (jnp.sum(p, axis=-1, keepdims=True) + psink)
    return p * inv, psink * inv


def _load_padded(dst, src, seq):
    dst[pl.ds(0, ATT_BLOCK), :] = jnp.zeros((ATT_BLOCK, dst.shape[1]), dst.dtype)
    dst[pl.ds(ATT_BLOCK, seq), :] = src[...]


def _attn_fwd(projp, sinks, *, seq, q_blk, k_blk, v_blk, name, comm=None):
    T = projp.shape[0]
    QW = N_Q_HEADS * HEAD_DIM
    nblk = seq // ATT_BLOCK

    def body(q_ref, k_ref, v_ref, sink_ref, o_ref, kpad, vpad):
        _load_padded(kpad, k_ref, seq)
        _load_padded(vpad, v_ref, seq)

        def blk(n, carry):
            r0 = pl.multiple_of(n * ATT_BLOCK, ATT_BLOCK)
            qb = q_ref[pl.ds(r0, ATT_BLOCK), :]
            kb = kpad[pl.ds(r0, 2 * ATT_BLOCK), :]
            vb = vpad[pl.ds(r0, 2 * ATT_BLOCK), :]
            valid = _band_mask(n)
            for h in range(N_Q_HEADS):
                g = h // GQA_GROUP
                probs, _ = _head_probs(qb[:, h * HEAD_DIM:(h + 1) * HEAD_DIM],
                                       kb[:, g * HEAD_DIM:(g + 1) * HEAD_DIM], valid, sink_ref[0, h])
                o = _dot(probs.astype(BF16), vb[:, g * HEAD_DIM:(g + 1) * HEAD_DIM])
                o_ref[pl.ds(r0, ATT_BLOCK), h * HEAD_DIM:(h + 1) * HEAD_DIM] = o.astype(BF16)
            return carry

        lax.fori_loop(0, nblk, blk, 0)

    blocks = [((seq, QW), BF16)] * 2 + [((seq, KV_WIDTH), BF16)] * 2
    return _call(
        body, (projp, projp, projp, sinks), name=name, grid=(T // seq,),
        in_specs=[pl.BlockSpec((seq, QW), lambda b: (b, q_blk)),
                  pl.BlockSpec((seq, KV_WIDTH), lambda b: (b, k_blk)),
                  pl.BlockSpec((seq, KV_WIDTH), lambda b: (b, v_blk)),
                  pl.BlockSpec(memory_space=pltpu.SMEM)],
        out_specs=[pl.BlockSpec((seq, QW), lambda b: (b, 0))],
        out_shape=[SDS((T, QW), BF16)],
        scratch_shapes=[pltpu.VMEM((seq + ATT_BLOCK, KV_WIDTH), BF16)] * 2,
        params=_params(1, blocks, temp_bytes=8 * 2**20), comm=comm)[0]


def _attn_bwd(projp, dao, sinks, *, seq, q_blk, k_blk, v_blk, name, comm=None):
    T = projp.shape[0]
    QW = N_Q_HEADS * HEAD_DIM
    nblk = seq // ATT_BLOCK

    def body(q_ref, k_ref, v_ref, do_ref, sink_ref, dq_ref, dk_ref, dv_ref, dsink_ref, kpad, vpad, dkacc, dvacc):
        _load_padded(kpad, k_ref, seq)
        _load_padded(vpad, v_ref, seq)
        dkacc[...] = jnp.zeros(dkacc.shape, F32)
        dvacc[...] = jnp.zeros(dvacc.shape, F32)
        lane = lax.broadcasted_iota(jnp.int32, (1, 128), 1)

        def blk(n, dsink):
            r0 = pl.multiple_of(n * ATT_BLOCK, ATT_BLOCK)
            qb = q_ref[pl.ds(r0, ATT_BLOCK), :]
            dob = do_ref[pl.ds(r0, ATT_BLOCK), :]
            kb = kpad[pl.ds(r0, 2 * ATT_BLOCK), :]
            vb = vpad[pl.ds(r0, 2 * ATT_BLOCK), :]
            valid = _band_mask(n)
            for g in range(N_KV_HEADS):
                kg = kb[:, g * HEAD_DIM:(g + 1) * HEAD_DIM]
                vg = vb[:, g * HEAD_DIM:(g + 1) * HEAD_DIM]
                dkg = jnp.zeros((2 * ATT_BLOCK, HEAD_DIM), F32)
                dvg = jnp.zeros((2 * ATT_BLOCK, HEAD_DIM), F32)
                for h in range(g * GQA_GROUP, (g + 1) * GQA_GROUP):
                    qh = qb[:, h * HEAD_DIM:(h + 1) * HEAD_DIM]
                    doh = dob[:, h * HEAD_DIM:(h + 1) * HEAD_DIM]
                    probs, psink = _head_probs(qh, kg, valid, sink_ref[0, h])
                    dp = _dot_nt(doh, vg)
                    delta = jnp.sum(probs * dp, axis=-1, keepdims=True)
                    dsb = (probs * (dp - delta) * (HEAD_DIM ** -0.5)).astype(BF16)
                    dsink = dsink + jnp.where(lane == h, -jnp.sum(psink * delta), 0.0)
                    dq_ref[pl.ds(r0, ATT_BLOCK), h * HEAD_DIM:(h + 1) * HEAD_DIM] = _dot(dsb, kg).astype(BF16)
                    dkg = dkg + _dot_tn(dsb, qh)
                    dvg = dvg + _dot_tn(probs.astype(BF16), doh)
                cols = slice(g * HEAD_DIM, (g + 1) * HEAD_DIM)
                dkacc[pl.ds(r0, 2 * ATT_BLOCK), cols] = dkacc[pl.ds(r0, 2 * ATT_BLOCK), cols] + dkg
                dvacc[pl.ds(r0, 2 * ATT_BLOCK), cols] = dvacc[pl.ds(r0, 2 * ATT_BLOCK), cols] + dvg
            return dsink

        dsink = lax.fori_loop(0, nblk, blk, jnp.zeros((1, 128), F32))
        _acc(dsink_ref, dsink, pl.program_id(0) == 0)
        dk_ref[...] = dkacc[pl.ds(ATT_BLOCK, seq), :].astype(BF16)
        dv_ref[...] = dvacc[pl.ds(ATT_BLOCK, seq), :].astype(BF16)

    blocks = [((seq, QW), BF16)] * 3 + [((seq, KV_WIDTH), BF16)] * 4
    kv_spec_out = pl.BlockSpec((seq, KV_WIDTH), lambda b: (b, 0))
    return _call(
        body, (projp, projp, projp, dao, sinks), name=name, grid=(T // seq,),
        in_specs=[pl.BlockSpec((seq, QW), lambda b: (b, q_blk)),
                  pl.BlockSpec((seq, KV_WIDTH), lambda b: (b, k_blk)),
                  pl.BlockSpec((seq, KV_WIDTH), lambda b: (b, v_blk)),
                  pl.BlockSpec((seq, QW), lambda b: (b, 0)),
                  pl.BlockSpec(memory_space=pltpu.SMEM)],
        out_specs=[pl.BlockSpec((seq, QW), lambda b: (b, 0)), kv_spec_out, kv_spec_out,
                   pl.BlockSpec((1, 128), lambda b: (0, 0))],
        out_shape=[SDS((T, QW), BF16), SDS((T, KV_WIDTH), BF16), SDS((T, KV_WIDTH), BF16), SDS((1, 128), F32)],
        scratch_shapes=[pltpu.VMEM((seq + ATT_BLOCK, KV_WIDTH), BF16)] * 2
        + [pltpu.VMEM((seq + ATT_BLOCK, KV_WIDTH), F32)] * 2,
        params=_params(1, blocks, temp_bytes=12 * 2**20), comm=comm)


def _conv_fwd(projp, w, bias, *, seq, cw, a_col, b_col, name, comm=None):
    T = projp.shape[0]
    C = w.shape[1]
    nchunk = seq // CONV_ROWS

    def body(a_ref, b_ref, w_ref, bias_ref, y_ref, upad):
        upad[pl.ds(0, CONV_HALO), :] = jnp.zeros((CONV_HALO, cw), F32)
        upad[pl.ds(CONV_HALO, seq), :] = a_ref[...].astype(F32) * _sigmoid(b_ref[...].astype(F32))
        wv = w_ref[...]
        bv = bias_ref[...]

        def chunk(r, carry):
            r0 = pl.multiple_of(r * CONV_ROWS, CONV_ROWS)
            win = upad[pl.ds(r0, CONV_ROWS + CONV_HALO), :]
            acc = jnp.broadcast_to(bv, (CONV_ROWS, cw))
            for k in range(CONV_WIDTH):
                off = CONV_HALO - (CONV_WIDTH - 1) + k
                acc = acc + wv[k:k + 1, :] * win[off:off + CONV_ROWS, :]
            y_ref[pl.ds(r0, CONV_ROWS), :] = acc
            return carry

        lax.fori_loop(0, nchunk, chunk, 0)

    blocks = [((seq, cw), BF16)] * 2 + [((seq, cw), F32)]
    return _call(
        body, (projp, projp, w, bias), name=name, grid=(T // seq, C // cw),
        in_specs=[pl.BlockSpec((seq, cw), lambda b, c: (b, a_col // cw + c)),
                  pl.BlockSpec((seq, cw), lambda b, c: (b, b_col // cw + c)),
                  pl.BlockSpec((CONV_WIDTH, cw), lambda b, c: (0, c)),
                  pl.BlockSpec((1, cw), lambda b, c: (0, c))],
        out_specs=[pl.BlockSpec((seq, cw), lambda b, c: (b, c))],
        out_shape=[SDS((T, C), F32)],
        scratch_shapes=[pltpu.VMEM((seq + CONV_HALO, cw), F32)],
        params=_params(2, blocks, temp_bytes=6 * _nbytes((seq, cw), F32)), comm=comm)[0]


def _conv_bwd(dy, projp, w, *, seq, cw, a_col, b_col, name, comm=None):
    T = projp.shape[0]
    C = w.shape[1]
    nchunk = seq // CONV_ROWS
    SUB = 8

    def body(dy_ref, a_ref, b_ref, w_ref, da_ref, db_ref, dw_ref, dbias_ref, upad, dypad, dwp):
        first = pl.program_id(1) == 0
        af = a_ref[...].astype(F32)
        sb = _sigmoid(b_ref[...].astype(F32))
        upad[pl.ds(0, CONV_HALO), :] = jnp.zeros((CONV_HALO, cw), F32)
        upad[pl.ds(CONV_HALO, seq), :] = af * sb
        dyv = dy_ref[...]
        dypad[pl.ds(0, seq), :] = dyv
        dypad[pl.ds(seq, CONV_HALO), :] = jnp.zeros((CONV_HALO, cw), F32)
        dwp[...] = jnp.zeros(dwp.shape, F32)
        wv = w_ref[...]

        def chunk(r, carry):
            r0 = pl.multiple_of(r * CONV_ROWS, CONV_ROWS)
            wdy = dypad[pl.ds(r0, CONV_ROWS + CONV_HALO), :]
            wu = upad[pl.ds(r0, CONV_ROWS + CONV_HALO), :]
            dyc = wdy[0:CONV_ROWS, :]
            du = jnp.zeros((CONV_ROWS, cw), F32)
            for k in range(CONV_WIDTH):
                du = du + wv[k:k + 1, :] * wdy[CONV_WIDTH - 1 - k:CONV_WIDTH - 1 - k + CONV_ROWS, :]
                off = CONV_HALO - (CONV_WIDTH - 1) + k
                prod = dyc * wu[off:off + CONV_ROWS, :]
                part = prod[0:SUB, :]
                for s in range(1, CONV_ROWS // SUB):
                    part = part + prod[s * SUB:(s + 1) * SUB, :]
                dwp[pl.ds(k * SUB, SUB), :] = dwp[pl.ds(k * SUB, SUB), :] + part
            ac = a_ref[pl.ds(r0, CONV_ROWS), :].astype(F32)
            sbc = _sigmoid(b_ref[pl.ds(r0, CONV_ROWS), :].astype(F32))
            da_ref[pl.ds(r0, CONV_ROWS), :] = (du * sbc).astype(BF16)
            db_ref[pl.ds(r0, CONV_ROWS), :] = (du * ac * (sbc * (1.0 - sbc))).astype(BF16)
            return carry

        lax.fori_loop(0, nchunk, chunk, 0)

        @pl.when(first)
        def _():
            dw_ref[...] = jnp.zeros(dw_ref.shape, F32)
            dbias_ref[...] = jnp.zeros(dbias_ref.shape, F32)

        for k in range(CONV_WIDTH):
            dw_ref[k:k + 1, :] = dw_ref[k:k + 1, :] + _rowsum(dwp[pl.ds(k * SUB, SUB), :])
        dbias_ref[...] = dbias_ref[...] + _rowsum(dyv)

    blocks = [((seq, cw), F32)] + [((seq, cw), BF16)] * 4
    return _call(
        body, (dy, projp, projp, w), name=name, grid=(C // cw, T // seq),
        in_specs=[pl.BlockSpec((seq, cw), lambda c, b: (b, c)),
                  pl.BlockSpec((seq, cw), lambda c, b: (b, a_col // cw + c)),
                  pl.BlockSpec((seq, cw), lambda c, b: (b, b_col // cw + c)),
                  pl.BlockSpec((CONV_WIDTH, cw), lambda c, b: (0, c))],
        out_specs=[pl.BlockSpec((seq, cw), lambda c, b: (b, c)), pl.BlockSpec((seq, cw), lambda c, b: (b, c)),
                   pl.BlockSpec((CONV_WIDTH, cw), lambda c, b: (0, c)), pl.BlockSpec((1, cw), lambda c, b: (0, c))],
        out_shape=[SDS((T, C), BF16), SDS((T, C), BF16), SDS((CONV_WIDTH, C), F32), SDS((1, C), F32)],
        scratch_shapes=[pltpu.VMEM((seq + CONV_HALO, cw), F32), pltpu.VMEM((seq + CONV_HALO, cw), F32),
                        pltpu.VMEM((CONV_WIDTH * SUB, cw), F32)],
        params=_params(2, blocks, temp_bytes=8 * _nbytes((seq, cw), F32)), comm=comm)


def _matmul_tn(a, b, *, name, comm=None):
    T, M = a.shape
    N = b.shape[1]
    bm = _pick(M, (768, 512, 256))

    def body(a_ref, b_ref, o_ref):
        o_ref[...] = _dot_tn(a_ref[...], b_ref[...]).astype(BF16)

    blocks = [((T, bm), BF16), ((T, N), BF16), ((bm, N), BF16)]
    return _call(
        body, (a, b), name=name, grid=(M // bm,),
        in_specs=[pl.BlockSpec((T, bm), lambda i: (0, i)), pl.BlockSpec((T, N), lambda i: (0, 0))],
        out_specs=[pl.BlockSpec((bm, N), lambda i: (i, 0))],
        out_shape=[SDS((M, N), BF16)],
        params=_params(1, blocks, temp_bytes=2 * _nbytes((T, bm), BF16) + 2 * _nbytes((bm, N), F32)),
        comm=comm)[0]


def _sum_parts(p_ref):
    g = p_ref[0].astype(F32)
    for s in range(1, N_DEV):
        g = g + p_ref[s].astype(F32)
    return g


def _adamw_update(w, g, m, v):
    m = ADAM_B1 * m + (1.0 - ADAM_B1) * g
    v = ADAM_B2 * v + (1.0 - ADAM_B2) * (g * g)
    m_hat = m / (1.0 - ADAM_B1 ** ADAM_STEP)
    v_hat = v / (1.0 - ADAM_B2 ** ADAM_STEP)
    delta = -ADAM_LR * (m_hat / (jnp.sqrt(v_hat) + ADAM_EPS) + ADAM_WD * w)
    return delta, m, v


def _row_tile(R):
    return _pick(R, (256, 128, 112, 88, 64, 32, 16, 8))


def _sum8(parts, *, name):
    _, R, W = parts.shape
    tr = _row_tile(R)

    def body(p_ref, o_ref):
        o_ref[...] = _sum_parts(p_ref)

    return _call(
        body, (parts,), name=name, grid=(R // tr,),
        in_specs=[pl.BlockSpec((N_DEV, tr, W), lambda i: (0, i, 0))],
        out_specs=[pl.BlockSpec((tr, W), lambda i: (i, 0))],
        out_shape=[SDS((R, W), F32)],
        params=_params(1, [((N_DEV, tr, W), parts.dtype), ((tr, W), F32)]))[0]


def _adamw(g, w, m, v, *, name):
    R, W = w.shape
    tr = _row_tile(R)

    def body(g_ref, w_ref, m_ref, v_ref, d_ref, mo_ref, vo_ref):
        d_ref[...], mo_ref[...], vo_ref[...] = _adamw_update(w_ref[...], g_ref[...], m_ref[...], v_ref[...])

    spec = pl.BlockSpec((tr, W), lambda i: (i, 0))
    return _call(
        body, (g, w, m, v), name=name, grid=(R // tr,),
        in_specs=[spec] * 4, out_specs=[spec] * 3, out_shape=[SDS((R, W), F32)] * 3,
        params=_params(1, [((tr, W), F32)] * 7))


def _sum8_adamw(parts, w, m, v, *, name):
    R, W = w.shape
    tr = _row_tile(R)

    def body(p_ref, w_ref, m_ref, v_ref, g_ref, d_ref, mo_ref, vo_ref):
        g = _sum_parts(p_ref)
        g_ref[...] = g
        d_ref[...], mo_ref[...], vo_ref[...] = _adamw_update(w_ref[...], g, m_ref[...], v_ref[...])

    spec = pl.BlockSpec((tr, W), lambda i: (i, 0))
    return _call(
        body, (parts, w, m, v), name=name, grid=(R // tr,),
        in_specs=[pl.BlockSpec((N_DEV, tr, W), lambda i: (0, i, 0))] + [spec] * 3,
        out_specs=[spec] * 4, out_shape=[SDS((R, W), F32)] * 4,
        params=_params(1, [((N_DEV, tr, W), parts.dtype)] + [((tr, W), F32)] * 7))


def _ada_fwd(c_all, w, bias, *, name):
    NB, D = c_all.shape
    N = w.shape[1]

    def body(c_ref, w_ref, b_ref, o_ref):
        cv = c_ref[...]
        ca = (cv * _sigmoid(cv)).astype(BF16)
        o_ref[...] = _dot(ca, w_ref[...].astype(BF16)) + b_ref[...]

    full = lambda s: pl.BlockSpec(s, lambda i: (0,) * len(s))
    return _call(
        body, (c_all, w, bias), name=name, grid=(1,),
        in_specs=[full((NB, D)), full((D, N)), full((1, N))], out_specs=[full((NB, N))],
        out_shape=[SDS((NB, N), F32)],
        params=_params(1, [((D, N), F32)], temp_bytes=_nbytes((D, N), BF16)))[0]


def _ada_bwd(c_all, gcols, *, name):
    NB, D = c_all.shape
    N = gcols.shape[1]

    def body(c_ref, g_ref, gw_ref, gb_ref):
        cv = c_ref[...]
        ca = (cv * _sigmoid(cv)).astype(BF16)
        gv = g_ref[...]
        gw_ref[...] = _dot_tn(ca, gv.astype(BF16))
        gb_ref[...] = _rowsum(gv)

    full = lambda s: pl.BlockSpec(s, lambda i: (0,) * len(s))
    return _call(
        body, (c_all, gcols), name=name, grid=(1,),
        in_specs=[full((NB, D)), full((NB, N))], out_specs=[full((D, N)), full((1, N))],
        out_shape=[SDS((D, N), F32), SDS((1, N), F32)],
        params=_params(1, [((D, N), F32)]))


def kernel(x, c, w_ada, b_ada, norm_ffn1_g, ffn1_w_gate, ffn1_w_up, ffn1_w_down, norm_mix_g, w_in, attn_sinks, w_attn_o, conv_w_dw, conv_b_dw, conv_ln_g, conv_ln_b, w_conv_o, w_out, norm_ffn2_g, ffn2_w_gate, ffn2_w_up, ffn2_w_down, final_norm_g, loss_target, m_w_ada, m_b_ada, m_norm_ffn1_g, m_ffn1_w_gate, m_ffn1_w_up, m_ffn1_w_down, m_norm_mix_g, m_w_in, m_attn_sinks, m_w_attn_o, m_conv_w_dw, m_conv_b_dw, m_conv_ln_g, m_conv_ln_b, m_w_conv_o, m_w_out, m_norm_ffn2_g, m_ffn2_w_gate, m_ffn2_w_up, m_ffn2_w_down, m_final_norm_g, v_w_ada, v_b_ada, v_norm_ffn1_g, v_ffn1_w_gate, v_ffn1_w_up, v_ffn1_w_down, v_norm_mix_g, v_w_in, v_attn_sinks, v_w_attn_o, v_conv_w_dw, v_conv_b_dw, v_conv_ln_g, v_conv_ln_b, v_w_conv_o, v_w_out, v_norm_ffn2_g, v_ffn2_w_gate, v_ffn2_w_up, v_ffn2_w_down, v_final_norm_g):
    B, S, D = x.shape
    T = B * S
    QW = N_Q_HEADS * HEAD_DIM
    CC = conv_w_dw.shape[2] * N_DEV
    me = _lin(_my_pos())
    xf = x.reshape(T, D)
    tgt = loss_target.reshape(T, D)
    tm = min(512, S)
    kw = dict(seq=S, tm=tm)

    o_k, o_ca, o_end = QW, QW + 2 * KV_WIDTH, QW + 2 * KV_WIDTH + 2 * CC + 2 * D
    p_ca, p_cb, p_ga, p_gc, p_k, p_v = QW, QW + CC, QW + 2 * CC, QW + 2 * CC + D, QW + 2 * CC + 2 * D, \
        QW + 2 * CC + 2 * D + KV_WIDTH

    def to_local_order(w):
        return jnp.concatenate([w[:o_k], w[o_ca:o_end], w[o_k:o_ca]], axis=0)

    def to_ref_order(w):
        return jnp.concatenate([w[:QW], w[p_k:], w[p_ca:p_k]], axis=0)

    def col_t(w):
        return w[0].T.astype(BF16)

    def row_b(w):
        return w[0].astype(BF16)

    def rows(g):
        return g.reshape(-1, g.shape[-1])

    def blocks8(g):
        return g.reshape(N_DEV, g.shape[0] // N_DEV, g.shape[1])

    def gather(*arrs):
        return _Comm([(a, "gather") for a in arrs])

    def scatter(*arrs):
        return _Comm([(blocks8(a), "scatter") for a in arrs])

    g_wg1, g_wu1, g_convw, g_c = _exchange(
        [(col_t(ffn1_w_gate), "gather"), (col_t(ffn1_w_up), "gather"), (conv_w_dw[0], "gather"), (c, "gather")],
        name="gather_first")
    wg1, wu1 = rows(g_wg1), rows(g_wu1)
    conv_w = g_convw.transpose(1, 0, 2).reshape(CONV_WIDTH, CC)
    c_all = g_c.reshape(N_DEV * B, D)

    n_col = N_MOD * D // N_DEV
    b_cols = lax.dynamic_slice(b_ada, (0, me * n_col), (1, n_col))
    mod_cols = _ada_fwd(c_all, w_ada[0], b_cols, name="ada_fwd")
    mod_mine = _exchange([(mod_cols.reshape(N_DEV, B, n_col), "scatter")], name="scatter_mod")[0]
    mod = mod_mine.transpose(1, 0, 2).reshape(B, N_MOD, 1, D)
    sh1, sc1, g1, sh2, sc2, g2, sh3, sc3, g3 = [mod[:, i] for i in range(N_MOD)]

    F = wg1.shape[0]
    tn_f = _pick(F, (1408, 1024, 512, 256))
    tn_in = _pick(w_in.shape[2] * N_DEV, (1792, 768, 512, 256))
    gate_blk = dict(ga_blk=p_ga // D, gc_blk=p_gc // D)
    att_blk = dict(q_blk=0, k_blk=p_k // KV_WIDTH, v_blk=p_v // KV_WIDTH)
    conv_kw = dict(seq=S, cw=256, a_col=p_ca, b_col=p_cb)

    cm = gather(row_b(ffn1_w_down), col_t(w_in))
    h1, (a1, b1) = _norm_mod_matmul(xf, norm_ffn1_g, sh1, sc1, [wg1, wu1], tn=tn_f, name="ffn1_up", comm=cm, **kw)
    wd1, winp = rows(cm.out[0]), to_local_order(rows(cm.out[1]))
    cm = gather(row_b(w_attn_o), row_b(w_conv_o), row_b(w_out))
    x1, y1 = _ffn_down(a1, b1, wd1, xf, g1, name="ffn1_down", comm=cm, **kw)
    wao, wco, wout = [rows(o) for o in cm.out]
    cm = gather(col_t(ffn2_w_gate), col_t(ffn2_w_up))
    h2, (projp,) = _norm_mod_matmul(x1, norm_mix_g, sh2, sc2, [winp], tn=tn_in, name="mix_in", comm=cm, **kw)
    wg2, wu2 = [rows(o) for o in cm.out]
    cm = gather(row_b(ffn2_w_down))
    ao = _attn_fwd(projp, attn_sinks, seq=S, name="attn_fwd", comm=cm, **att_blk)
    wd2 = rows(cm.out[0])
    yc = _conv_fwd(projp, conv_w, conv_b_dw, name="conv_fwd", **conv_kw)
    x2, z, ya, ycv, cact, merged = _mix_out(ao, yc, projp, wao, wco, wout, x1, g2, conv_ln_g, conv_ln_b,
                                            name="mix_out", **gate_blk, **kw)
    h3, (a3, b3) = _norm_mod_matmul(x2, norm_ffn2_g, sh3, sc3, [wg2, wu2], tn=tn_f, name="ffn2_up", **kw)
    x3, y3 = _ffn_down(a3, b3, wd2, x2, g3, name="ffn2_down", **kw)
    dx3, loss_row, dgf = _final_loss(x3, final_norm_g[None], tgt, tm=tm, name="final_loss")
    loss = lax.psum(loss_row[0, 0], ("x", "y", "c"))

    parts = {}
    dyb3, da3, db3, act3, dg3 = _ffn_bwd_down(dx3, g3, y3, wd2, a3, b3, tn=tn_f, name="ffn2_bwd_down", **kw)
    gwd2 = _matmul_tn(act3, dyb3, name="gw_ffn2_down")
    cm = scatter(gwd2)
    dx2, dsh3, dsc3, dgn3 = _matmul_norm_mod_bwd([da3, db3], [wg2, wu2], x2, norm_ffn2_g, sc3, dx3,
                                                 name="ffn2_bwd_up", comm=cm, **kw)
    parts["ffn2_w_down"], = cm.out
    gwg2 = _matmul_tn(da3, h3, name="gw_ffn2_gate")
    gwu2 = _matmul_tn(db3, h3, name="gw_ffn2_up")

    cm = scatter(gwg2)
    dzb, dyab, dycb, dga, dgc, dao, dyc, dg2, dlng, dlnb = _mix_out_bwd(
        dx2, g2, z, wout, projp, ya, ycv, wao, wco, yc, conv_ln_g, conv_ln_b, name="mix_out_bwd", comm=cm,
        **gate_blk, **kw)
    parts["ffn2_w_gate"], = cm.out
    gwout = _matmul_tn(merged, dzb, name="gw_out")
    gwao = _matmul_tn(ao, dyab, name="gw_attn_o")
    gwco = _matmul_tn(cact, dycb, name="gw_conv_o")
    cm = scatter(gwu2, gwout, gwao, gwco)
    dq, dk, dv, dsinks = _attn_bwd(projp, dao, attn_sinks, seq=S, name="attn_bwd", comm=cm, **att_blk)
    parts["ffn2_w_up"], parts["w_out"], parts["w_attn_o"], parts["w_conv_o"] = cm.out
    dca, dcb, dconvw, dconvb = _conv_bwd(dyc, projp, conv_w, name="conv_bwd", **conv_kw)
    dprojp = jnp.concatenate([dq, dca, dcb, dga, dgc, dk, dv], axis=1)
    dx1, dsh2, dsc2, dgn2 = _matmul_norm_mod_bwd([dprojp], [winp], x1, norm_mix_g, sc2, dx2,
                                                 name="mix_in_bwd", **kw)
    gwin = to_ref_order(_matmul_tn(dprojp, h2, name="gw_in"))

    cm = scatter(gwin)
    dyb1, da1, db1, act1, dg1 = _ffn_bwd_down(dx1, g1, y1, wd1, a1, b1, tn=tn_f, name="ffn1_bwd_down", comm=cm,
                                              **kw)
    parts["w_in"], = cm.out
    gwd1 = _matmul_tn(act1, dyb1, name="gw_ffn1_down")
    cm = scatter(gwd1)
    dx0, dsh1, dsc1, dgn1 = _matmul_norm_mod_bwd([da1, db1], [wg1, wu1], xf, norm_ffn1_g, sc1, dx1,
                                                 name="ffn1_bwd_up", comm=cm, **kw)
    parts["ffn1_w_down"], = cm.out

    gmod = jnp.concatenate([dsh1, dsc1, dg1, dsh2, dsc2, dg2, dsh3, dsc3, dg3], axis=1)
    gmod = gmod.reshape(B, N_DEV, n_col).transpose(1, 0, 2)
    cm = _Comm([(gmod, "scatter")])
    gwg1 = _matmul_tn(da1, h1, name="gw_ffn1_gate", comm=cm)
    gcols = cm.out[0].reshape(N_DEV * B, n_col)
    gwu1 = _matmul_tn(db1, h1, name="gw_ffn1_up")
    g_w_ada, gb_cols = _ada_bwd(c_all, gcols, name="ada_bwd")

    n_small = 8
    sink_row = jnp.pad(dsinks[:, :N_Q_HEADS], ((0, 0), (0, D - N_Q_HEADS)))
    small = jnp.concatenate([dgn1, dgn2, dgn3, dgf, dconvb, dlng, dlnb, sink_row, dconvw,
                             jnp.zeros((1, D), F32)], axis=0)
    parts["ffn1_w_gate"], parts["ffn1_w_up"], small_all, gb_all = _exchange(
        [(blocks8(gwg1), "scatter"), (blocks8(gwu1), "scatter"), (small, "gather"), (gb_cols, "gather")],
        name="exchange_last")
    gsmall = _sum8(small_all, name="sum_small")
    g_b_ada = gb_all.reshape(1, N_MOD * D)
    g_conv_w = lax.dynamic_slice(gsmall[n_small:n_small + CONV_WIDTH], (0, me * (CC // N_DEV)),
                                 (CONV_WIDTH, CC // N_DEV))

    def col_update(name, w, m, v):
        g = _sum8(parts[name], name="sum_" + name).T
        return (g,) + tuple(_adamw(g, w[0], m[0], v[0], name="adamw_" + name))

    def row_update(name, w, m, v):
        return tuple(_sum8_adamw(parts[name], w[0], m[0], v[0], name="adamw_" + name))

    upd = {
        "ffn1_w_gate": col_update("ffn1_w_gate", ffn1_w_gate, m_ffn1_w_gate, v_ffn1_w_gate),
        "ffn1_w_up": col_update("ffn1_w_up", ffn1_w_up, m_ffn1_w_up, v_ffn1_w_up),
        "ffn1_w_down": row_update("ffn1_w_down", ffn1_w_down, m_ffn1_w_down, v_ffn1_w_down),
        "w_in": col_update("w_in", w_in, m_w_in, v_w_in),
        "w_attn_o": row_update("w_attn_o", w_attn_o, m_w_attn_o, v_w_attn_o),
        "w_conv_o": row_update("w_conv_o", w_conv_o, m_w_conv_o, v_w_conv_o),
        "w_out": row_update("w_out", w_out, m_w_out, v_w_out),
        "ffn2_w_gate": col_update("ffn2_w_gate", ffn2_w_gate, m_ffn2_w_gate, v_ffn2_w_gate),
        "ffn2_w_up": col_update("ffn2_w_up", ffn2_w_up, m_ffn2_w_up, v_ffn2_w_up),
        "ffn2_w_down": row_update("ffn2_w_down", ffn2_w_down, m_ffn2_w_down, v_ffn2_w_down),
        "w_ada": (g_w_ada,) + tuple(_adamw(g_w_ada, w_ada[0], m_w_ada[0], v_w_ada[0], name="adamw_w_ada")),
        "conv_w_dw": (g_conv_w,) + tuple(_adamw(g_conv_w, conv_w_dw[0], m_conv_w_dw[0], v_conv_w_dw[0],
                                                name="adamw_conv_w_dw")),
    }
    for k in upd:
        upd[k] = tuple(t[None] for t in upd[k])

    def pad_sinks(t):
        return jnp.pad(t, ((0, 0), (0, D - N_Q_HEADS)))

    def pack(f1, mix, f2, fin, cb, lg, lb, sinks, bada):
        return jnp.concatenate([f1, mix, f2, fin[None], cb, lg, lb, pad_sinks(sinks), bada.reshape(N_MOD, D)], axis=0)

    w_s = pack(norm_ffn1_g, norm_mix_g, norm_ffn2_g, final_norm_g, conv_b_dw, conv_ln_g, conv_ln_b, attn_sinks, b_ada)
    m_s = pack(m_norm_ffn1_g, m_norm_mix_g, m_norm_ffn2_g, m_final_norm_g, m_conv_b_dw, m_conv_ln_g, m_conv_ln_b,
               m_attn_sinks, m_b_ada)
    v_s = pack(v_norm_ffn1_g, v_norm_mix_g, v_norm_ffn2_g, v_final_norm_g, v_conv_b_dw, v_conv_ln_g, v_conv_ln_b,
               v_attn_sinks, v_b_ada)
    g_s = jnp.concatenate([gsmall[:n_small], g_b_ada.reshape(N_MOD, D)], axis=0)
    small_out = (g_s,) + tuple(_adamw(g_s, w_s, m_s, v_s, name="adamw_vectors"))

    def unpack(t):
        return {
            "norm_ffn1_g": t[0:1], "norm_mix_g": t[1:2], "norm_ffn2_g": t[2:3], "final_norm_g": t[3],
            "conv_b_dw": t[4:5], "conv_ln_g": t[5:6], "conv_ln_b": t[6:7], "attn_sinks": t[7:8, :N_Q_HEADS],
            "b_ada": t[n_small:n_small + N_MOD].reshape(1, N_MOD * D),
        }

    small_un = [unpack(t) for t in small_out]
    for k in small_un[0]:
        upd[k] = tuple(s[k] for s in small_un)

    order = ["w_ada", "b_ada", "norm_ffn1_g", "ffn1_w_gate", "ffn1_w_up", "ffn1_w_down", "norm_mix_g", "w_in",
             "attn_sinks", "w_attn_o", "conv_w_dw", "conv_b_dw", "conv_ln_g", "conv_ln_b", "w_conv_o", "w_out",
             "norm_ffn2_g", "ffn2_w_gate", "ffn2_w_up", "ffn2_w_down", "final_norm_g"]
    grad_x = dx0.reshape(B, S, D)
    return (loss, grad_x, *[upd[k][0] for k in order], *[upd[k][1] for k in order],
            *[upd[k][2] for k in order], *[upd[k][3] for k in order])
```

```python
import jax
import jax.numpy as jnp
from jax import lax
from jax.experimental import pallas as pl
from jax.experimental.pallas import tpu as pltpu

F32 = jnp.float32
BF16 = jnp.bfloat16
SDS = jax.ShapeDtypeStruct
MESH = pl.DeviceIdType.MESH

N_DEV = 8
EPS = 1e-6
HEAD_DIM = 64
N_Q_HEADS = 16
N_KV_HEADS = 2
GQA_GROUP = N_Q_HEADS // N_KV_HEADS
KV_WIDTH = N_KV_HEADS * HEAD_DIM
ATT_BLOCK = 128
CONV_WIDTH = 31
CONV_HALO = 32
CONV_ROWS = 64
N_MOD = 9
FFN_RESIDUAL = 0.5
ADAM_LR = 0.001
ADAM_B1 = 0.9
ADAM_B2 = 0.999
ADAM_EPS = 1e-08
ADAM_WD = 0.01
ADAM_STEP = 10
NEG_BIG = -1e30

V7X_VMEM_BYTES = 64 * 2**20
VMEM_CAP = V7X_VMEM_BYTES - 8 * 2**20


def _nbytes(shape, dtype):
    n = 1
    for s in shape:
        n *= s
    return n * jnp.dtype(dtype).itemsize


def _params(n_axes, blocks, temp_bytes=0):
    need = 2 * sum(_nbytes(s, d) for s, d in blocks) + temp_bytes + 4 * 2**20
    return pltpu.CompilerParams(dimension_semantics=("arbitrary",) * n_axes,
                                vmem_limit_bytes=int(min(max(need, 16 * 2**20), VMEM_CAP)))


def _dot_nt(a, b):
    return lax.dot_general(a, b, (((1,), (1,)), ((), ())), preferred_element_type=F32)


def _dot_tn(a, b):
    return lax.dot_general(a, b, (((0,), (0,)), ((), ())), preferred_element_type=F32)


def _dot(a, b):
    return jnp.dot(a, b, preferred_element_type=F32)


def _sigmoid(x):
    return jax.nn.sigmoid(x)


def _rowsum(v):
    return jnp.sum(v, axis=0, keepdims=True)


def _acc(ref, val, first):
    @pl.when(first)
    def _():
        ref[...] = val

    @pl.when(jnp.logical_not(first))
    def _():
        ref[...] = ref[...] + val


def _norm_mod(xf, gn, sh, sc):
    rstd = lax.rsqrt(jnp.mean(xf * xf, axis=-1, keepdims=True) + EPS)
    xhat = xf * rstd
    yn = xhat * gn
    return yn * (1.0 + sc) + sh, xhat, rstd, yn


def _pick(n, cands):
    for c in cands:
        if n % c == 0:
            return c
    return n


def _my_pos():
    return lax.axis_index("x"), lax.axis_index("y"), lax.axis_index("c")


def _peer(pos, k):
    x, y, c = pos
    return ((1 - x) if k & 4 else x, (1 - y) if k & 2 else y, (1 - c) if k & 1 else c)


def _lin(pos):
    return 4 * pos[0] + 2 * pos[1] + pos[2]


class _Comm:
    N_COPY = N_DEV - 1

    def __init__(self, items):
        self.arrs = [a for a, _ in items]
        self.modes = [m for _, m in items]
        self.n = len(items)
        self.out = None

    def out_shape(self):
        return [SDS(a.shape if m == "scatter" else (N_DEV,) + a.shape, a.dtype)
                for a, m in zip(self.arrs, self.modes)]

    def scratch(self):
        return [pltpu.SemaphoreType.DMA((self.n * self.N_COPY,)), pltpu.SemaphoreType.DMA((self.n * self.N_COPY,)),
                pltpu.SemaphoreType.DMA((self.n,))]

    def _copy(self, refs, me, i, k, recv):
        srcs, outs, (send_sems, recv_sems, _) = refs
        x, y, c = me
        if self.modes[i] == "scatter":
            peer = _peer(me, k + 1)
            src, send_slot, recv_slot = srcs[i].at[_lin(peer)], _lin(me), _lin(peer)
        else:
            sib = (x, y, 1 - c)
            chips = [(1 - x, y), (x, 1 - y), (1 - x, 1 - y)]
            if k == 0:
                peer, src, send_slot, recv_slot = sib, srcs[i], _lin(me), _lin(sib)
            elif k <= 3:
                peer = (*chips[k - 1], c)
                src, send_slot, recv_slot = srcs[i], _lin(me), _lin(peer)
            else:
                peer = sib
                send_slot = _lin((*chips[k - 4], c))
                src, recv_slot = outs[i].at[send_slot], _lin((*chips[k - 4], 1 - c))
        sem = i * self.N_COPY + k
        return pltpu.make_async_remote_copy(
            src_ref=src, dst_ref=outs[i].at[recv_slot if recv else send_slot], send_sem=send_sems.at[sem],
            recv_sem=recv_sems.at[sem], device_id=peer, device_id_type=MESH)

    def _local(self, refs, me, i):
        srcs, outs, (_, _, loc_sems) = refs
        own = srcs[i].at[_lin(me)] if self.modes[i] == "scatter" else srcs[i]
        return pltpu.make_async_copy(own, outs[i].at[_lin(me)], loc_sems.at[i])

    def start(self, refs):
        me = _my_pos()
        for i in range(self.n):
            self._local(refs, me, i).start()
            for k in range(self.N_COPY if self.modes[i] == "scatter" else 4):
                self._copy(refs, me, i, k, False).start()

    def finish(self, refs):
        me = _my_pos()
        for i in range(self.n):
            if self.modes[i] == "gather":
                for j in range(3):
                    self._copy(refs, me, i, 1 + j, True).wait_recv()
                    self._copy(refs, me, i, 4 + j, False).start()
        for i in range(self.n):
            for k in range(self.N_COPY):
                if not (self.modes[i] == "gather" and 1 <= k <= 3):
                    self._copy(refs, me, i, k, True).wait_recv()
                self._copy(refs, me, i, k, False).wait_send()
            self._local(refs, me, i).wait()


_ANY = pl.BlockSpec(memory_space=pl.ANY)


def _call(body, args, *, name, grid, in_specs, out_specs, out_shape, params, scratch_shapes=(), comm=None):
    in_specs, out_specs, out_shape = list(in_specs), list(out_specs), list(out_shape)
    scratch_shapes = list(scratch_shapes)
    if comm is None:
        return list(pl.pallas_call(body, name=name, grid=grid, in_specs=in_specs, out_specs=out_specs,
                                   out_shape=out_shape, scratch_shapes=scratch_shapes, compiler_params=params)(*args))
    n_in, n_out, n_scr, nc = len(in_specs), len(out_specs), len(scratch_shapes), comm.n

    def hosted(*refs):
        ins, c_in = refs[:n_in], refs[n_in:n_in + nc]
        outs = refs[n_in + nc:n_in + nc + n_out]
        c_out = refs[n_in + nc + n_out:n_in + 2 * nc + n_out]
        scr = refs[n_in + 2 * nc + n_out:n_in + 2 * nc + n_out + n_scr]
        sems = refs[n_in + 2 * nc + n_out + n_scr:]
        first = pl.program_id(0) == 0
        last = pl.program_id(0) == grid[0] - 1
        for d in range(1, len(grid)):
            first = jnp.logical_and(first, pl.program_id(d) == 0)
            last = jnp.logical_and(last, pl.program_id(d) == grid[d] - 1)

        @pl.when(first)
        def _():
            comm.start((c_in, c_out, sems))

        body(*ins, *outs, *scr)

        @pl.when(last)
        def _():
            comm.finish((c_in, c_out, sems))

    res = pl.pallas_call(
        hosted, name=name, grid=grid, in_specs=in_specs + [_ANY] * nc, out_specs=out_specs + [_ANY] * nc,
        out_shape=out_shape + comm.out_shape(), scratch_shapes=scratch_shapes + comm.scratch(),
        compiler_params=params)(*args, *comm.arrs)
    comm.out = list(res[n_out:])
    return list(res[:n_out])


def _exchange(items, *, name):
    comm = _Comm(items)

    def body(*refs):
        r = (refs[:comm.n], refs[comm.n:2 * comm.n], refs[2 * comm.n:])
        comm.start(r)
        comm.finish(r)

    return list(pl.pallas_call(body, name=name, out_shape=comm.out_shape(), in_specs=[_ANY] * comm.n,
                               out_specs=[_ANY] * comm.n, scratch_shapes=comm.scratch())(*comm.arrs))


def _norm_mod_matmul(x, gn, sh, sc, wts, *, seq, tm, tn, name, comm=None):
    T, D = x.shape
    N = wts[0].shape[0]
    nw = len(wts)
    tps = seq // tm

    def body(x_ref, gn_ref, sh_ref, sc_ref, *rest):
        w_refs, h_ref, o_refs = rest[:nw], rest[nw], rest[nw + 1:]

        @pl.when(pl.program_id(1) == 0)
        def _():
            h_ref[...] = _norm_mod(x_ref[...], gn_ref[...], sh_ref[0], sc_ref[0])[0].astype(BF16)

        h = h_ref[...]
        for w_ref, o_ref in zip(w_refs, o_refs):
            o_ref[...] = _dot_nt(h, w_ref[...]).astype(o_ref.dtype)

    row = pl.BlockSpec((tm, D), lambda i, j: (i, 0))
    vec = pl.BlockSpec((1, D), lambda i, j: (0, 0))
    per_b = pl.BlockSpec((1, 1, D), lambda i, j: (i // tps, 0, 0))
    wspec = pl.BlockSpec((tn, D), lambda i, j: (j, 0))
    ospec = pl.BlockSpec((tm, tn), lambda i, j: (i, j))
    blocks = [((tm, D), F32), ((tm, D), BF16)] + [((tn, D), BF16), ((tm, tn), BF16)] * nw
    outs = _call(
        body, (x, gn, sh, sc, *wts), name=name, grid=(T // tm, N // tn),
        in_specs=[row, vec, per_b, per_b] + [wspec] * nw,
        out_specs=[row] + [ospec] * nw,
        out_shape=[SDS((T, D), BF16)] + [SDS((T, N), BF16)] * nw,
        params=_params(2, blocks, temp_bytes=2 * _nbytes((tm, tn), F32) + 3 * _nbytes((tm, D), F32)), comm=comm)
    return outs[0], outs[1:]


def _ffn_down(a, b, wd, x, g, *, seq, tm, name, comm=None):
    T, F = a.shape
    D = wd.shape[1]
    tps = seq // tm

    def body(a_ref, b_ref, wd_ref, x_ref, g_ref, xo_ref, y_ref):
        af = a_ref[...].astype(F32)
        act = (af * _sigmoid(af) * b_ref[...].astype(F32)).astype(BF16)
        y = _dot(act, wd_ref[...])
        xo_ref[...] = x_ref[...] + (FFN_RESIDUAL * g_ref[0]) * y
        y_ref[...] = y.astype(BF16)

    wide = pl.BlockSpec((tm, F), lambda i: (i, 0))
    row = pl.BlockSpec((tm, D), lambda i: (i, 0))
    per_b = pl.BlockSpec((1, 1, D), lambda i: (i // tps, 0, 0))
    wspec = pl.BlockSpec((F, D), lambda i: (0, 0))
    blocks = [((tm, F), BF16)] * 2 + [((F, D), BF16), ((tm, D), F32), ((tm, D), F32), ((tm, D), BF16)]
    return _call(
        body, (a, b, wd, x, g), name=name, grid=(T // tm,),
        in_specs=[wide, wide, wspec, row, per_b], out_specs=[row, row],
        out_shape=[SDS((T, D), F32), SDS((T, D), BF16)],
        params=_params(1, blocks, temp_bytes=3 * _nbytes((tm, F), F32)), comm=comm)


def _final_loss(x, gf, tgt, *, tm, name):
    T, D = x.shape
    nt = T // tm

    def body(x_ref, gf_ref, t_ref, dx_ref, loss_ref, dgf_ref, lacc):
        i = pl.program_id(0)
        xf = x_ref[...]
        gfv = gf_ref[...]
        rstd = lax.rsqrt(jnp.mean(xf * xf, axis=-1, keepdims=True) + EPS)
        xhat = xf * rstd
        err = xhat * gfv - t_ref[...]
        dy = err * (1.0 / D)
        dxhat = dy * gfv
        dx_ref[...] = rstd * (dxhat - xhat * jnp.mean(dxhat * xhat, axis=-1, keepdims=True))
        _acc(dgf_ref, _rowsum(dy * xhat), i == 0)
        _acc(lacc, _rowsum(err * err), i == 0)

        @pl.when(i == nt - 1)
        def _():
            loss_ref[...] = jnp.broadcast_to((0.5 / D) * jnp.sum(lacc[...]), loss_ref.shape)

    row = pl.BlockSpec((tm, D), lambda i: (i, 0))
    vec = pl.BlockSpec((1, D), lambda i: (0, 0))
    lspec = pl.BlockSpec((1, 128), lambda i: (0, 0))
    blocks = [((tm, D), F32)] * 3
    return _call(
        body, (x, gf, tgt), name=name, grid=(nt,),
        in_specs=[row, vec, row], out_specs=[row, lspec, vec],
        out_shape=[SDS((T, D), F32), SDS((1, 128), F32), SDS((1, D), F32)],
        scratch_shapes=[pltpu.VMEM((1, D), F32)],
        params=_params(1, blocks, temp_bytes=4 * _nbytes((tm, D), F32)))


def _ffn_bwd_down(dxo, g, y, wd, a, b, *, seq, tm, tn, name, comm=None):
    T, F = a.shape
    D = wd.shape[1]
    tps = seq // tm
    nb = T // seq

    def body(dxo_ref, g_ref, y_ref, wd_ref, a_ref, b_ref, dyb_ref, da_ref, db_ref, act_ref, dg_ref):
        i = pl.program_id(0)

        @pl.when(pl.program_id(1) == 0)
        def _():
            dx = dxo_ref[...]
            dyb_ref[...] = ((FFN_RESIDUAL * g_ref[0]) * dx).astype(BF16)
            part = _rowsum(FFN_RESIDUAL * dx * y_ref[...].astype(F32))
            _acc(dg_ref, part[None], i % tps == 0)

        dact = _dot_nt(dyb_ref[...], wd_ref[...])
        af = a_ref[...].astype(F32)
        bf = b_ref[...].astype(F32)
        sg = _sigmoid(af)
        silu = af * sg
        act_ref[...] = (silu * bf).astype(BF16)
        da_ref[...] = (dact * bf * (sg * (1.0 + af * (1.0 - sg)))).astype(BF16)
        db_ref[...] = (dact * silu).astype(BF16)

    row = pl.BlockSpec((tm, D), lambda i, j: (i, 0))
    per_b = pl.BlockSpec((1, 1, D), lambda i, j: (i // tps, 0, 0))
    wspec = pl.BlockSpec((tn, D), lambda i, j: (j, 0))
    chunk = pl.BlockSpec((tm, tn), lambda i, j: (i, j))
    blocks = [((tm, D), F32), ((tm, D), BF16), ((tn, D), BF16), ((tm, D), BF16)] + [((tm, tn), BF16)] * 5
    return _call(
        body, (dxo, g, y, wd, a, b), name=name, grid=(T // tm, F // tn),
        in_specs=[row, per_b, row, wspec, chunk, chunk],
        out_specs=[row, chunk, chunk, chunk, per_b],
        out_shape=[SDS((T, D), BF16)] + [SDS((T, F), BF16)] * 3 + [SDS((nb, 1, D), F32)],
        params=_params(2, blocks, temp_bytes=6 * _nbytes((tm, tn), F32)), comm=comm)


def _matmul_norm_mod_bwd(ds, ws, x, gn, sc, dxo, *, seq, tm, name, comm=None):
    T, D = x.shape
    nk = len(ds)
    tps = seq // tm
    nb = T // seq

    def body(*refs):
        d_refs, w_refs = refs[:nk], refs[nk:2 * nk]
        x_ref, gn_ref, sc_ref, dxo_ref, dxi_ref, dsh_ref, dsc_ref, dgn_ref = refs[2 * nk:]
        i = pl.program_id(0)
        dh = _dot(d_refs[0][...], w_refs[0][...])
        for d_ref, w_ref in zip(d_refs[1:], w_refs[1:]):
            dh = dh + _dot(d_ref[...], w_ref[...])
        gnv = gn_ref[...]
        scv = sc_ref[0]
        _, xhat, rstd, yn = _norm_mod(x_ref[...], gnv, 0.0, scv)
        dyn = dh * (1.0 + scv)
        dxhat = dyn * gnv
        dxi_ref[...] = dxo_ref[...] + rstd * (dxhat - xhat * jnp.mean(dxhat * xhat, axis=-1, keepdims=True))
        first_of_seq = i % tps == 0
        _acc(dsh_ref, _rowsum(dh)[None], first_of_seq)
        _acc(dsc_ref, _rowsum(dh * yn)[None], first_of_seq)
        _acc(dgn_ref, _rowsum(dyn * xhat), i == 0)

    row = pl.BlockSpec((tm, D), lambda i: (i, 0))
    vec = pl.BlockSpec((1, D), lambda i: (0, 0))
    per_b = pl.BlockSpec((1, 1, D), lambda i: (i // tps, 0, 0))
    d_specs = [pl.BlockSpec((tm, d.shape[1]), lambda i: (i, 0)) for d in ds]
    w_specs = [pl.BlockSpec(w.shape, lambda i: (0, 0)) for w in ws]
    blocks = ([((tm, d.shape[1]), BF16) for d in ds] + [(w.shape, BF16) for w in ws] + [((tm, D), F32)] * 3)
    return _call(
        body, (*ds, *ws, x, gn, sc, dxo), name=name, grid=(T // tm,),
        in_specs=d_specs + w_specs + [row, vec, per_b, row],
        out_specs=[row, per_b, per_b, vec],
        out_shape=[SDS((T, D), F32), SDS((nb, 1, D), F32), SDS((nb, 1, D), F32), SDS((1, D), F32)],
        params=_params(1, blocks, temp_bytes=6 * _nbytes((tm, D), F32)), comm=comm)


def _layernorm_silu(yc, lg, lb):
    mu = jnp.mean(yc, axis=-1, keepdims=True)
    cen = yc - mu
    rstd = lax.rsqrt(jnp.mean(cen * cen, axis=-1, keepdims=True) + EPS)
    xh = cen * rstd
    l = xh * lg + lb
    s = _sigmoid(l)
    return l * s, xh, rstd, l, s


def _mix_out(ao, yc, projp, wao, wco, wout, x1, g2, lg, lb, *, seq, tm, ga_blk, gc_blk, name, comm=None):
    T, D = x1.shape
    tps = seq // tm

    def body(ao_ref, yc_ref, ga_ref, gc_ref, wao_ref, wco_ref, wout_ref, x1_ref, g2_ref, lg_ref, lb_ref,
             x2_ref, z_ref, ya_ref, ycv_ref, cact_ref, mrg_ref):
        ya = _dot(ao_ref[...], wao_ref[...])
        cact = _layernorm_silu(yc_ref[...], lg_ref[...], lb_ref[...])[0].astype(BF16)
        ycv = _dot(cact, wco_ref[...])
        merged = (_sigmoid(ga_ref[...].astype(F32)) * ya + _sigmoid(gc_ref[...].astype(F32)) * ycv).astype(BF16)
        z = _dot(merged, wout_ref[...])
        x2_ref[...] = x1_ref[...] + g2_ref[0] * z
        z_ref[...] = z.astype(BF16)
        ya_ref[...] = ya.astype(BF16)
        ycv_ref[...] = ycv.astype(BF16)
        cact_ref[...] = cact
        mrg_ref[...] = merged

    row = pl.BlockSpec((tm, D), lambda i: (i, 0))
    vec = pl.BlockSpec((1, D), lambda i: (0, 0))
    per_b = pl.BlockSpec((1, 1, D), lambda i: (i // tps, 0, 0))
    wspec = pl.BlockSpec((D, D), lambda i: (0, 0))
    ga_spec = pl.BlockSpec((tm, D), lambda i: (i, ga_blk))
    gc_spec = pl.BlockSpec((tm, D), lambda i: (i, gc_blk))
    blocks = ([((tm, D), BF16), ((tm, D), F32), ((tm, D), BF16), ((tm, D), BF16)] + [((D, D), BF16)] * 3
              + [((tm, D), F32)] * 2 + [((tm, D), BF16)] * 5)
    return _call(
        body, (ao, yc, projp, projp, wao, wco, wout, x1, g2, lg, lb), name=name, grid=(T // tm,),
        in_specs=[row, row, ga_spec, gc_spec, wspec, wspec, wspec, row, per_b, vec, vec],
        out_specs=[row] * 6,
        out_shape=[SDS((T, D), F32)] + [SDS((T, D), BF16)] * 5,
        params=_params(1, blocks, temp_bytes=8 * _nbytes((tm, D), F32)), comm=comm)


def _mix_out_bwd(dx2, g2, z, wout, projp, ya, ycv, wao, wco, yc, lg, lb, *, seq, tm, ga_blk, gc_blk, name,
                 comm=None):
    T, D = dx2.shape
    tps = seq // tm
    nb = T // seq

    def body(dx2_ref, g2_ref, z_ref, wout_ref, ga_ref, gc_ref, ya_ref, ycv_ref, wao_ref, wco_ref, yc_ref,
             lg_ref, lb_ref, dz_ref, dya_ref, dycv_ref, dga_ref, dgc_ref, dao_ref, dyc_ref, dg2_ref, dlg_ref,
             dlb_ref):
        i = pl.program_id(0)
        dx = dx2_ref[...]
        _acc(dg2_ref, _rowsum(dx * z_ref[...].astype(F32))[None], i % tps == 0)
        dzb = (g2_ref[0] * dx).astype(BF16)
        dz_ref[...] = dzb
        dmerged = _dot_nt(dzb, wout_ref[...])
        sa = _sigmoid(ga_ref[...].astype(F32))
        sc_ = _sigmoid(gc_ref[...].astype(F32))
        dya = (dmerged * sa).astype(BF16)
        dycv = (dmerged * sc_).astype(BF16)
        dya_ref[...] = dya
        dycv_ref[...] = dycv
        dga_ref[...] = (dmerged * ya_ref[...].astype(F32) * (sa * (1.0 - sa))).astype(BF16)
        dgc_ref[...] = (dmerged * ycv_ref[...].astype(F32) * (sc_ * (1.0 - sc_))).astype(BF16)
        dao_ref[...] = _dot_nt(dya, wao_ref[...]).astype(BF16)
        dcact = _dot_nt(dycv, wco_ref[...])
        lgv = lg_ref[...]
        _, xh, rstd, l, s = _layernorm_silu(yc_ref[...], lgv, lb_ref[...])
        dl = dcact * (s * (1.0 + l * (1.0 - s)))
        _acc(dlb_ref, _rowsum(dl), i == 0)
        _acc(dlg_ref, _rowsum(dl * xh), i == 0)
        dxh = dl * lgv
        dyc_ref[...] = rstd * (dxh - jnp.mean(dxh, axis=-1, keepdims=True)
                               - xh * jnp.mean(dxh * xh, axis=-1, keepdims=True))

    row = pl.BlockSpec((tm, D), lambda i: (i, 0))
    vec = pl.BlockSpec((1, D), lambda i: (0, 0))
    per_b = pl.BlockSpec((1, 1, D), lambda i: (i // tps, 0, 0))
    wspec = pl.BlockSpec((D, D), lambda i: (0, 0))
    ga_spec = pl.BlockSpec((tm, D), lambda i: (i, ga_blk))
    gc_spec = pl.BlockSpec((tm, D), lambda i: (i, gc_blk))
    blocks = ([((tm, D), F32)] * 3 + [((tm, D), BF16)] * 11 + [((D, D), BF16)] * 3)
    return _call(
        body, (dx2, g2, z, wout, projp, projp, ya, ycv, wao, wco, yc, lg, lb), name=name, grid=(T // tm,),
        in_specs=[row, per_b, row, wspec, ga_spec, gc_spec, row, row, wspec, wspec, row, vec, vec],
        out_specs=[row] * 7 + [per_b, vec, vec],
        out_shape=[SDS((T, D), BF16)] * 6 + [SDS((T, D), F32), SDS((nb, 1, D), F32), SDS((1, D), F32),
                                             SDS((1, D), F32)],
        params=_params(1, blocks, temp_bytes=10 * _nbytes((tm, D), F32)), comm=comm)


GROUP_ROWS = GQA_GROUP * ATT_BLOCK
PAIR_W = 2 * HEAD_DIM
GROUP_W = GQA_GROUP * HEAD_DIM


def _lane_lo():
    return lax.broadcasted_iota(jnp.int32, (1, PAIR_W), 1) < HEAD_DIM


def _band_bias():
    sj = lax.broadcasted_iota(jnp.int32, (2 * ATT_BLOCK, GROUP_ROWS), 0)
    qi = lax.broadcasted_iota(jnp.int32, (2 * ATT_BLOCK, GROUP_ROWS), 1) & (ATT_BLOCK - 1)
    rel = qi + ATT_BLOCK - sj
    bias = jnp.where(jnp.logical_and(rel >= 0, rel < ATT_BLOCK), 0.0, NEG_BIG)
    sj1 = lax.broadcasted_iota(jnp.int32, (2 * ATT_BLOCK, 1), 0)
    return bias, jnp.where(sj1 < ATT_BLOCK, NEG_BIG, 0.0)


def _dup_heads(src_ref, dst, seq):
    x = src_ref[...]
    i = lax.broadcasted_iota(jnp.int32, (KV_WIDTH, PAIR_W), 0)
    j = lax.broadcasted_iota(jnp.int32, (KV_WIDTH, PAIR_W), 1) & (HEAD_DIM - 1)
    for g in range(N_KV_HEADS):
        sel = jnp.where(i == j + g * HEAD_DIM, 1.0, 0.0).astype(BF16)
        dst[g, pl.ds(0, ATT_BLOCK), :] = jnp.zeros((ATT_BLOCK, PAIR_W), BF16)
        dst[g, pl.ds(ATT_BLOCK, seq), :] = _dot(x, sel).astype(BF16)


def _stack_heads(blk, g, lo):
    parts = []
    for p in range(GQA_GROUP // 2):
        pair = blk[:, g * GROUP_W + p * PAIR_W:g * GROUP_W + (p + 1) * PAIR_W]
        parts += [jnp.where(lo, pair, jnp.zeros_like(pair)), jnp.where(lo, jnp.zeros_like(pair), pair)]
    return jnp.concatenate(parts, axis=0)


def _unstack_heads(full, ref, r0, g, lo):
    for p in range(GQA_GROUP // 2):
        even = full[(2 * p) * ATT_BLOCK:(2 * p + 1) * ATT_BLOCK, :]
        odd = full[(2 * p + 1) * ATT_BLOCK:(2 * p + 2) * ATT_BLOCK, :]
        ref[pl.ds(r0, ATT_BLOCK), g * GROUP_W + p * PAIR_W:g * GROUP_W + (p + 1) * PAIR_W] = (
            jnp.where(lo, even, odd).astype(ref.dtype))


def _sink_row(sink_ref, g):
    return jnp.concatenate([jnp.full((1, ATT_BLOCK), sink_ref[0, g * GQA_GROUP + h], F32)
                            for h in range(GQA_GROUP)], axis=1)


def _group_probs(qs, k2, bias, sink):
    s = _dot_nt(k2, qs) * (HEAD_DIM ** -0.5) + bias
    m = jnp.maximum(jnp.max(s, axis=0, keepdims=True), sink)
    p = jnp.exp(s - m)
    psink = jnp.exp(sink - m)
    inv = 1.0 / (jnp.sum(p, axis=0, keepdims=True) + psink)
    return p * inv, psink * inv


def _attn_fwd(projp, sinks, *, seq, q_blk, k_blk, v_blk, name, comm=None):
    T = projp.shape[0]
    QW = N_Q_HEADS * HEAD_DIM
    nblk = seq // ATT_BLOCK

    def body(q_ref, k_ref, v_ref, sink_ref, o_ref, k2s, v2s):
        _dup_heads(k_ref, k2s, seq)
        _dup_heads(v_ref, v2s, seq)
        lo = _lane_lo()
        bias0, first_pen = _band_bias()
        sink_rows = [_sink_row(sink_ref, g) for g in range(N_KV_HEADS)]

        def blk(n, carry):
            r0 = pl.multiple_of(n * ATT_BLOCK, ATT_BLOCK)
            qb = q_ref[pl.ds(r0, ATT_BLOCK), :]
            bias = bias0 + jnp.where(n == 0, 1.0, 0.0) * first_pen
            for g in range(N_KV_HEADS):
                probs_t, _ = _group_probs(_stack_heads(qb, g, lo), k2s[g, pl.ds(r0, 2 * ATT_BLOCK), :], bias,
                                          sink_rows[g])
                _unstack_heads(_dot_tn(probs_t.astype(BF16), v2s[g, pl.ds(r0, 2 * ATT_BLOCK), :]), o_ref, r0, g, lo)
            return carry

        lax.fori_loop(0, nblk, blk, 0)

    blocks = [((seq, QW), BF16)] * 2 + [((seq, KV_WIDTH), BF16)] * 2
    return _call(
        body, (projp, projp, projp, sinks), name=name, grid=(T // seq,),
        in_specs=[pl.BlockSpec((seq, QW), lambda b: (b, q_blk)),
                  pl.BlockSpec((seq, KV_WIDTH), lambda b: (b, k_blk)),
                  pl.BlockSpec((seq, KV_WIDTH), lambda b: (b, v_blk)),
                  pl.BlockSpec(memory_space=pltpu.SMEM)],
        out_specs=[pl.BlockSpec((seq, QW), lambda b: (b, 0))],
        out_shape=[SDS((T, QW), BF16)],
        scratch_shapes=[pltpu.VMEM((N_KV_HEADS, seq + ATT_BLOCK, PAIR_W), BF16)] * 2,
        params=_params(1, blocks, temp_bytes=16 * 2**20), comm=comm)[0]


def _attn_bwd(projp, dao, sinks, *, seq, q_blk, k_blk, v_blk, name, comm=None):
    T = projp.shape[0]
    QW = N_Q_HEADS * HEAD_DIM
    nblk = seq // ATT_BLOCK

    def body(q_ref, k_ref, v_ref, do_ref, sink_ref, dq_ref, dk_ref, dv_ref, dsink_ref, k2s, v2s, dkacc, dvacc):
        _dup_heads(k_ref, k2s, seq)
        _dup_heads(v_ref, v2s, seq)
        dkacc[...] = jnp.zeros(dkacc.shape, F32)
        dvacc[...] = jnp.zeros(dvacc.shape, F32)
        lane = lax.broadcasted_iota(jnp.int32, (1, PAIR_W), 1)
        lo = lane < HEAD_DIM
        bias0, first_pen = _band_bias()
        sink_rows = [_sink_row(sink_ref, g) for g in range(N_KV_HEADS)]

        def blk(n, dsink):
            r0 = pl.multiple_of(n * ATT_BLOCK, ATT_BLOCK)
            band = pl.ds(r0, 2 * ATT_BLOCK)
            qb = q_ref[pl.ds(r0, ATT_BLOCK), :]
            dob = do_ref[pl.ds(r0, ATT_BLOCK), :]
            bias = bias0 + jnp.where(n == 0, 1.0, 0.0) * first_pen
            for g in range(N_KV_HEADS):
                qs = _stack_heads(qb, g, lo)
                dos = _stack_heads(dob, g, lo)
                k2 = k2s[g, band, :]
                v2 = v2s[g, band, :]
                probs_t, psink = _group_probs(qs, k2, bias, sink_rows[g])
                dp_t = _dot_nt(v2, dos)
                delta = jnp.sum(probs_t * dp_t, axis=0, keepdims=True)
                ds_t = (probs_t * (dp_t - delta) * (HEAD_DIM ** -0.5)).astype(BF16)
                tsink = psink * delta
                for h in range(GQA_GROUP):
                    dsink = dsink + jnp.where(lane == g * GQA_GROUP + h,
                                              -jnp.sum(tsink[:, h * ATT_BLOCK:(h + 1) * ATT_BLOCK]), 0.0)
                _unstack_heads(_dot_tn(ds_t, k2), dq_ref, r0, g, lo)
                dkacc[g, band, :] = dkacc[g, band, :] + _dot(ds_t, qs)
                dvacc[g, band, :] = dvacc[g, band, :] + _dot(probs_t.astype(BF16), dos)
            return dsink

        dsink = lax.fori_loop(0, nblk, blk, jnp.zeros((1, PAIR_W), F32))
        _acc(dsink_ref, dsink, pl.program_id(0) == 0)

        def fold(acc, g):
            a = acc[g, pl.ds(ATT_BLOCK, seq), :]
            return a + pltpu.roll(a, HEAD_DIM, 1)

        dk_ref[...] = jnp.where(lo, fold(dkacc, 0), fold(dkacc, 1)).astype(BF16)
        dv_ref[...] = jnp.where(lo, fold(dvacc, 0), fold(dvacc, 1)).astype(BF16)

    blocks = [((seq, QW), BF16)] * 3 + [((seq, KV_WIDTH), BF16)] * 4
    kv_spec_out = pl.BlockSpec((seq, KV_WIDTH), lambda b: (b, 0))
    return _call(
        body, (projp, projp, projp, dao, sinks), name=name, grid=(T // seq,),
        in_specs=[pl.BlockSpec((seq, QW), lambda b: (b, q_blk)),
                  pl.BlockSpec((seq, KV_WIDTH), lambda b: (b, k_blk)),
                  pl.BlockSpec((seq, KV_WIDTH), lambda b: (b, v_blk)),
                  pl.BlockSpec((seq, QW), lambda b: (b, 0)),
                  pl.BlockSpec(memory_space=pltpu.SMEM)],
        out_specs=[pl.BlockSpec((seq, QW), lambda b: (b, 0)), kv_spec_out, kv_spec_out,
                   pl.BlockSpec((1, 128), lambda b: (0, 0))],
        out_shape=[SDS((T, QW), BF16), SDS((T, KV_WIDTH), BF16), SDS((T, KV_WIDTH), BF16), SDS((1, 128), F32)],
        scratch_shapes=[pltpu.VMEM((N_KV_HEADS, seq + ATT_BLOCK, PAIR_W), BF16)] * 2
        + [pltpu.VMEM((N_KV_HEADS, seq + ATT_BLOCK, PAIR_W), F32)] * 2,
        params=_params(1, blocks, temp_bytes=24 * 2**20), comm=comm)


def _conv_fwd(projp, w, bias, *, seq, cw, a_col, b_col, name, comm=None):
    T = projp.shape[0]
    C = w.shape[1]
    nchunk = seq // CONV_ROWS

    def body(a_ref, b_ref, w_ref, bias_ref, y_ref, upad):
        upad[pl.ds(0, CONV_HALO), :] = jnp.zeros((CONV_HALO, cw), F32)
        upad[pl.ds(CONV_HALO, seq), :] = a_ref[...].astype(F32) * _sigmoid(b_ref[...].astype(F32))
        wv = w_ref[...]
        bv = bias_ref[...]

        def chunk(r, carry):
            r0 = pl.multiple_of(r * CONV_ROWS, CONV_ROWS)
            win = upad[pl.ds(r0, CONV_ROWS + CONV_HALO), :]
            acc = jnp.broadcast_to(bv, (CONV_ROWS, cw))
            for k in range(CONV_WIDTH):
                off = CONV_HALO - (CONV_WIDTH - 1) + k
                acc = acc + wv[k:k + 1, :] * win[off:off + CONV_ROWS, :]
            y_ref[pl.ds(r0, CONV_ROWS), :] = acc
            return carry

        lax.fori_loop(0, nchunk, chunk, 0)

    blocks = [((seq, cw), BF16)] * 2 + [((seq, cw), F32)]
    return _call(
        body, (projp, projp, w, bias), name=name, grid=(T // seq, C // cw),
        in_specs=[pl.BlockSpec((seq, cw), lambda b, c: (b, a_col // cw + c)),
                  pl.BlockSpec((seq, cw), lambda b, c: (b, b_col // cw + c)),
                  pl.BlockSpec((CONV_WIDTH, cw), lambda b, c: (0, c)),
                  pl.BlockSpec((1, cw), lambda b, c: (0, c))],
        out_specs=[pl.BlockSpec((seq, cw), lambda b, c: (b, c))],
        out_shape=[SDS((T, C), F32)],
        scratch_shapes=[pltpu.VMEM((seq + CONV_HALO, cw), F32)],
        params=_params(2, blocks, temp_bytes=6 * _nbytes((seq, cw), F32)), comm=comm)[0]


def _conv_bwd(dy, projp, w, *, seq, cw, a_col, b_col, name, comm=None):
    T = projp.shape[0]
    C = w.shape[1]
    nchunk = seq // CONV_ROWS
    SUB = 8

    def body(dy_ref, a_ref, b_ref, w_ref, da_ref, db_ref, dw_ref, dbias_ref, upad, dypad, dwp):
        first = pl.program_id(1) == 0
        af = a_ref[...].astype(F32)
        sb = _sigmoid(b_ref[...].astype(F32))
        upad[pl.ds(0, CONV_HALO), :] = jnp.zeros((CONV_HALO, cw), F32)
        upad[pl.ds(CONV_HALO, seq), :] = af * sb
        dyv = dy_ref[...]
        dypad[pl.ds(0, seq), :] = dyv
        dypad[pl.ds(seq, CONV_HALO), :] = jnp.zeros((CONV_HALO, cw), F32)
        dwp[...] = jnp.zeros(dwp.shape, F32)
        wv = w_ref[...]

        def chunk(r, carry):
            r0 = pl.multiple_of(r * CONV_ROWS, CONV_ROWS)
            wdy = dypad[pl.ds(r0, CONV_ROWS + CONV_HALO), :]
            wu = upad[pl.ds(r0, CONV_ROWS + CONV_HALO), :]
            dyc = wdy[0:CONV_ROWS, :]
            du = jnp.zeros((CONV_ROWS, cw), F32)
            for k in range(CONV_WIDTH):
                du = du + wv[k:k + 1, :] * wdy[CONV_WIDTH - 1 - k:CONV_WIDTH - 1 - k + CONV_ROWS, :]
                off = CONV_HALO - (CONV_WIDTH - 1) + k
                prod = dyc * wu[off:off + CONV_ROWS, :]
                part = prod[0:SUB, :]
                for s in range(1, CONV_ROWS // SUB):
                    part = part + prod[s * SUB:(s + 1) * SUB, :]
                dwp[pl.ds(k * SUB, SUB), :] = dwp[pl.ds(k * SUB, SUB), :] + part
            ac = a_ref[pl.ds(r0, CONV_ROWS), :].astype(F32)
            sbc = _sigmoid(b_ref[pl.ds(r0, CONV_ROWS), :].astype(F32))
            da_ref[pl.ds(r0, CONV_ROWS), :] = (du * sbc).astype(BF16)
            db_ref[pl.ds(r0, CONV_ROWS), :] = (du * ac * (sbc * (1.0 - sbc))).astype(BF16)
            return carry

        lax.fori_loop(0, nchunk, chunk, 0)

        @pl.when(first)
        def _():
            dw_ref[...] = jnp.zeros(dw_ref.shape, F32)
            dbias_ref[...] = jnp.zeros(dbias_ref.shape, F32)

        for k in range(CONV_WIDTH):
            dw_ref[k:k + 1, :] = dw_ref[k:k + 1, :] + _rowsum(dwp[pl.ds(k * SUB, SUB), :])
        dbias_ref[...] = dbias_ref[...] + _rowsum(dyv)

    blocks = [((seq, cw), F32)] + [((seq, cw), BF16)] * 4
    return _call(
        body, (dy, projp, projp, w), name=name, grid=(C // cw, T // seq),
        in_specs=[pl.BlockSpec((seq, cw), lambda c, b: (b, c)),
                  pl.BlockSpec((seq, cw), lambda c, b: (b, a_col // cw + c)),
                  pl.BlockSpec((seq, cw), lambda c, b: (b, b_col // cw + c)),
                  pl.BlockSpec((CONV_WIDTH, cw), lambda c, b: (0, c))],
        out_specs=[pl.BlockSpec((seq, cw), lambda c, b: (b, c)), pl.BlockSpec((seq, cw), lambda c, b: (b, c)),
                   pl.BlockSpec((CONV_WIDTH, cw), lambda c, b: (0, c)), pl.BlockSpec((1, cw), lambda c, b: (0, c))],
        out_shape=[SDS((T, C), BF16), SDS((T, C), BF16), SDS((CONV_WIDTH, C), F32), SDS((1, C), F32)],
        scratch_shapes=[pltpu.VMEM((seq + CONV_HALO, cw), F32), pltpu.VMEM((seq + CONV_HALO, cw), F32),
                        pltpu.VMEM((CONV_WIDTH * SUB, cw), F32)],
        params=_params(2, blocks, temp_bytes=8 * _nbytes((seq, cw), F32)), comm=comm)


def _matmul_tn(a, b, *, name, comm=None):
    T, M = a.shape
    N = b.shape[1]
    bm = _pick(M, (768, 512, 256))

    def body(a_ref, b_ref, o_ref):
        o_ref[...] = _dot_tn(a_ref[...], b_ref[...]).astype(BF16)

    blocks = [((T, bm), BF16), ((T, N), BF16), ((bm, N), BF16)]
    return _call(
        body, (a, b), name=name, grid=(M // bm,),
        in_specs=[pl.BlockSpec((T, bm), lambda i: (0, i)), pl.BlockSpec((T, N), lambda i: (0, 0))],
        out_specs=[pl.BlockSpec((bm, N), lambda i: (i, 0))],
        out_shape=[SDS((M, N), BF16)],
        params=_params(1, blocks, temp_bytes=2 * _nbytes((T, bm), BF16) + 2 * _nbytes((bm, N), F32)),
        comm=comm)[0]


def _sum_parts(p_ref):
    g = p_ref[0].astype(F32)
    for s in range(1, N_DEV):
        g = g + p_ref[s].astype(F32)
    return g


def _adamw_update(w, g, m, v):
    m = ADAM_B1 * m + (1.0 - ADAM_B1) * g
    v = ADAM_B2 * v + (1.0 - ADAM_B2) * (g * g)
    m_hat = m / (1.0 - ADAM_B1 ** ADAM_STEP)
    v_hat = v / (1.0 - ADAM_B2 ** ADAM_STEP)
    delta = -ADAM_LR * (m_hat / (jnp.sqrt(v_hat) + ADAM_EPS) + ADAM_WD * w)
    return delta, m, v


def _row_tile(R):
    return _pick(R, (256, 128, 112, 88, 64, 32, 16, 8))


def _sum8(parts, *, name):
    _, R, W = parts.shape
    tr = _row_tile(R)

    def body(p_ref, o_ref):
        o_ref[...] = _sum_parts(p_ref)

    return _call(
        body, (parts,), name=name, grid=(R // tr,),
        in_specs=[pl.BlockSpec((N_DEV, tr, W), lambda i: (0, i, 0))],
        out_specs=[pl.BlockSpec((tr, W), lambda i: (i, 0))],
        out_shape=[SDS((R, W), F32)],
        params=_params(1, [((N_DEV, tr, W), parts.dtype), ((tr, W), F32)]))[0]


def _adamw(g, w, m, v, *, name):
    R, W = w.shape
    tr = _row_tile(R)

    def body(g_ref, w_ref, m_ref, v_ref, d_ref, mo_ref, vo_ref):
        d_ref[...], mo_ref[...], vo_ref[...] = _adamw_update(w_ref[...], g_ref[...], m_ref[...], v_ref[...])

    spec = pl.BlockSpec((tr, W), lambda i: (i, 0))
    return _call(
        body, (g, w, m, v), name=name, grid=(R // tr,),
        in_specs=[spec] * 4, out_specs=[spec] * 3, out_shape=[SDS((R, W), F32)] * 3,
        params=_params(1, [((tr, W), F32)] * 7))


def _sum8_adamw(parts, w, m, v, *, name):
    R, W = w.shape
    tr = _row_tile(R)

    def body(p_ref, w_ref, m_ref, v_ref, g_ref, d_ref, mo_ref, vo_ref):
        g = _sum_parts(p_ref)
        g_ref[...] = g
        d_ref[...], mo_ref[...], vo_ref[...] = _adamw_update(w_ref[...], g, m_ref[...], v_ref[...])

    spec = pl.BlockSpec((tr, W), lambda i: (i, 0))
    return _call(
        body, (parts, w, m, v), name=name, grid=(R // tr,),
        in_specs=[pl.BlockSpec((N_DEV, tr, W), lambda i: (0, i, 0))] + [spec] * 3,
        out_specs=[spec] * 4, out_shape=[SDS((R, W), F32)] * 4,
        params=_params(1, [((N_DEV, tr, W), parts.dtype)] + [((tr, W), F32)] * 7))


def _ada_fwd(c_all, w, bias, *, name):
    NB, D = c_all.shape
    N = w.shape[1]

    def body(c_ref, w_ref, b_ref, o_ref):
        cv = c_ref[...]
        ca = (cv * _sigmoid(cv)).astype(BF16)
        o_ref[...] = _dot(ca, w_ref[...].astype(BF16)) + b_ref[...]

    full = lambda s: pl.BlockSpec(s, lambda i: (0,) * len(s))
    return _call(
        body, (c_all, w, bias), name=name, grid=(1,),
        in_specs=[full((NB, D)), full((D, N)), full((1, N))], out_specs=[full((NB, N))],
        out_shape=[SDS((NB, N), F32)],
        params=_params(1, [((D, N), F32)], temp_bytes=_nbytes((D, N), BF16)))[0]


def _ada_bwd(c_all, gcols, *, name):
    NB, D = c_all.shape
    N = gcols.shape[1]

    def body(c_ref, g_ref, gw_ref, gb_ref):
        cv = c_ref[...]
        ca = (cv * _sigmoid(cv)).astype(BF16)
        gv = g_ref[...]
        gw_ref[...] = _dot_tn(ca, gv.astype(BF16))
        gb_ref[...] = _rowsum(gv)

    full = lambda s: pl.BlockSpec(s, lambda i: (0,) * len(s))
    return _call(
        body, (c_all, gcols), name=name, grid=(1,),
        in_specs=[full((NB, D)), full((NB, N))], out_specs=[full((D, N)), full((1, N))],
        out_shape=[SDS((D, N), F32), SDS((1, N), F32)],
        params=_params(1, [((D, N), F32)]))


def kernel(x, c, w_ada, b_ada, norm_ffn1_g, ffn1_w_gate, ffn1_w_up, ffn1_w_down, norm_mix_g, w_in, attn_sinks, w_attn_o, conv_w_dw, conv_b_dw, conv_ln_g, conv_ln_b, w_conv_o, w_out, norm_ffn2_g, ffn2_w_gate, ffn2_w_up, ffn2_w_down, final_norm_g, loss_target, m_w_ada, m_b_ada, m_norm_ffn1_g, m_ffn1_w_gate, m_ffn1_w_up, m_ffn1_w_down, m_norm_mix_g, m_w_in, m_attn_sinks, m_w_attn_o, m_conv_w_dw, m_conv_b_dw, m_conv_ln_g, m_conv_ln_b, m_w_conv_o, m_w_out, m_norm_ffn2_g, m_ffn2_w_gate, m_ffn2_w_up, m_ffn2_w_down, m_final_norm_g, v_w_ada, v_b_ada, v_norm_ffn1_g, v_ffn1_w_gate, v_ffn1_w_up, v_ffn1_w_down, v_norm_mix_g, v_w_in, v_attn_sinks, v_w_attn_o, v_conv_w_dw, v_conv_b_dw, v_conv_ln_g, v_conv_ln_b, v_w_conv_o, v_w_out, v_norm_ffn2_g, v_ffn2_w_gate, v_ffn2_w_up, v_ffn2_w_down, v_final_norm_g):
    B, S, D = x.shape
    T = B * S
    QW = N_Q_HEADS * HEAD_DIM
    CC = conv_w_dw.shape[2] * N_DEV
    me = _lin(_my_pos())
    xf = x.reshape(T, D)
    tgt = loss_target.reshape(T, D)
    tm = min(512, S)
    kw = dict(seq=S, tm=tm)

    o_k, o_ca, o_end = QW, QW + 2 * KV_WIDTH, QW + 2 * KV_WIDTH + 2 * CC + 2 * D
    p_ca, p_cb, p_ga, p_gc, p_k, p_v = QW, QW + CC, QW + 2 * CC, QW + 2 * CC + D, QW + 2 * CC + 2 * D, \
        QW + 2 * CC + 2 * D + KV_WIDTH

    def to_local_order(w):
        return jnp.concatenate([w[:o_k], w[o_ca:o_end], w[o_k:o_ca]], axis=0)

    def to_ref_order(w):
        return jnp.concatenate([w[:QW], w[p_k:], w[p_ca:p_k]], axis=0)

    def col_t(w):
        return w[0].T.astype(BF16)

    def row_b(w):
        return w[0].astype(BF16)

    def rows(g):
        return g.reshape(-1, g.shape[-1])

    def blocks8(g):
        return g.reshape(N_DEV, g.shape[0] // N_DEV, g.shape[1])

    def gather(*arrs):
        return _Comm([(a, "gather") for a in arrs])

    def scatter(*arrs):
        return _Comm([(blocks8(a), "scatter") for a in arrs])

    g_wg1, g_wu1, g_convw, g_c = _exchange(
        [(col_t(ffn1_w_gate), "gather"), (col_t(ffn1_w_up), "gather"), (conv_w_dw[0], "gather"), (c, "gather")],
        name="gather_first")
    wg1, wu1 = rows(g_wg1), rows(g_wu1)
    conv_w = g_convw.transpose(1, 0, 2).reshape(CONV_WIDTH, CC)
    c_all = g_c.reshape(N_DEV * B, D)

    n_col = N_MOD * D // N_DEV
    b_cols = lax.dynamic_slice(b_ada, (0, me * n_col), (1, n_col))
    mod_cols = _ada_fwd(c_all, w_ada[0], b_cols, name="ada_fwd")
    mod_mine = _exchange([(mod_cols.reshape(N_DEV, B, n_col), "scatter")], name="scatter_mod")[0]
    mod = mod_mine.transpose(1, 0, 2).reshape(B, N_MOD, 1, D)
    sh1, sc1, g1, sh2, sc2, g2, sh3, sc3, g3 = [mod[:, i] for i in range(N_MOD)]

    F = wg1.shape[0]
    tn_f = _pick(F, (1408, 1024, 512, 256))
    tn_in = _pick(w_in.shape[2] * N_DEV, (1792, 768, 512, 256))
    gate_blk = dict(ga_blk=p_ga // D, gc_blk=p_gc // D)
    att_blk = dict(q_blk=0, k_blk=p_k // KV_WIDTH, v_blk=p_v // KV_WIDTH)
    conv_kw = dict(seq=S, cw=256, a_col=p_ca, b_col=p_cb)

    cm = gather(row_b(ffn1_w_down), col_t(w_in))
    h1, (a1, b1) = _norm_mod_matmul(xf, norm_ffn1_g, sh1, sc1, [wg1, wu1], tn=tn_f, name="ffn1_up", comm=cm, **kw)
    wd1, winp = rows(cm.out[0]), to_local_order(rows(cm.out[1]))
    cm = gather(row_b(w_attn_o), row_b(w_conv_o), row_b(w_out))
    x1, y1 = _ffn_down(a1, b1, wd1, xf, g1, name="ffn1_down", comm=cm, **kw)
    wao, wco, wout = [rows(o) for o in cm.out]
    cm = gather(col_t(ffn2_w_gate), col_t(ffn2_w_up))
    h2, (projp,) = _norm_mod_matmul(x1, norm_mix_g, sh2, sc2, [winp], tn=tn_in, name="mix_in", comm=cm, **kw)
    wg2, wu2 = [rows(o) for o in cm.out]
    cm = gather(row_b(ffn2_w_down))
    ao = _attn_fwd(projp, attn_sinks, seq=S, name="attn_fwd", comm=cm, **att_blk)
    wd2 = rows(cm.out[0])
    yc = _conv_fwd(projp, conv_w, conv_b_dw, name="conv_fwd", **conv_kw)
    x2, z, ya, ycv, cact, merged = _mix_out(ao, yc, projp, wao, wco, wout, x1, g2, conv_ln_g, conv_ln_b,
                                            name="mix_out", **gate_blk, **kw)
    h3, (a3, b3) = _norm_mod_matmul(x2, norm_ffn2_g, sh3, sc3, [wg2, wu2], tn=tn_f, name="ffn2_up", **kw)
    x3, y3 = _ffn_down(a3, b3, wd2, x2, g3, name="ffn2_down", **kw)
    dx3, loss_row, dgf = _final_loss(x3, final_norm_g[None], tgt, tm=tm, name="final_loss")
    loss = lax.psum(loss_row[0, 0], ("x", "y", "c"))

    parts = {}
    dyb3, da3, db3, act3, dg3 = _ffn_bwd_down(dx3, g3, y3, wd2, a3, b3, tn=tn_f, name="ffn2_bwd_down", **kw)
    gwd2 = _matmul_tn(act3, dyb3, name="gw_ffn2_down")
    cm = scatter(gwd2)
    dx2, dsh3, dsc3, dgn3 = _matmul_norm_mod_bwd([da3, db3], [wg2, wu2], x2, norm_ffn2_g, sc3, dx3,
                                                 name="ffn2_bwd_up", comm=cm, **kw)
    parts["ffn2_w_down"], = cm.out
    gwg2 = _matmul_tn(da3, h3, name="gw_ffn2_gate")
    gwu2 = _matmul_tn(db3, h3, name="gw_ffn2_up")

    cm = scatter(gwg2)
    dzb, dyab, dycb, dga, dgc, dao, dyc, dg2, dlng, dlnb = _mix_out_bwd(
        dx2, g2, z, wout, projp, ya, ycv, wao, wco, yc, conv_ln_g, conv_ln_b, name="mix_out_bwd", comm=cm,
        **gate_blk, **kw)
    parts["ffn2_w_gate"], = cm.out
    gwout = _matmul_tn(merged, dzb, name="gw_out")
    gwao = _matmul_tn(ao, dyab, name="gw_attn_o")
    gwco = _matmul_tn(cact, dycb, name="gw_conv_o")
    cm = scatter(gwu2, gwout, gwao, gwco)
    dq, dk, dv, dsinks = _attn_bwd(projp, dao, attn_sinks, seq=S, name="attn_bwd", comm=cm, **att_blk)
    parts["ffn2_w_up"], parts["w_out"], parts["w_attn_o"], parts["w_conv_o"] = cm.out
    dca, dcb, dconvw, dconvb = _conv_bwd(dyc, projp, conv_w, name="conv_bwd", **conv_kw)
    dprojp = jnp.concatenate([dq, dca, dcb, dga, dgc, dk, dv], axis=1)
    dx1, dsh2, dsc2, dgn2 = _matmul_norm_mod_bwd([dprojp], [winp], x1, norm_mix_g, sc2, dx2,
                                                 name="mix_in_bwd", **kw)
    gwin = to_ref_order(_matmul_tn(dprojp, h2, name="gw_in"))

    cm = scatter(gwin)
    dyb1, da1, db1, act1, dg1 = _ffn_bwd_down(dx1, g1, y1, wd1, a1, b1, tn=tn_f, name="ffn1_bwd_down", comm=cm,
                                              **kw)
    parts["w_in"], = cm.out
    gwd1 = _matmul_tn(act1, dyb1, name="gw_ffn1_down")
    dx0, dsh1, dsc1, dgn1 = _matmul_norm_mod_bwd([da1, db1], [wg1, wu1], xf, norm_ffn1_g, sc1, dx1,
                                                 name="ffn1_bwd_up", **kw)

    gmod = jnp.concatenate([dsh1, dsc1, dg1, dsh2, dsc2, dg2, dsh3, dsc3, dg3], axis=1)
    gmod = gmod.reshape(B, N_DEV, n_col).transpose(1, 0, 2)
    cm = _Comm([(blocks8(gwd1), "scatter"), (gmod, "scatter")])
    gwg1 = _matmul_tn(da1, h1, name="gw_ffn1_gate", comm=cm)
    parts["ffn1_w_down"] = cm.out[0]
    gcols = cm.out[1].reshape(N_DEV * B, n_col)
    cm = scatter(gwg1)
    gwu1 = _matmul_tn(db1, h1, name="gw_ffn1_up", comm=cm)
    parts["ffn1_w_gate"], = cm.out
    g_w_ada, gb_cols = _ada_bwd(c_all, gcols, name="ada_bwd")

    n_small = 8
    sink_row = jnp.pad(dsinks[:, :N_Q_HEADS], ((0, 0), (0, D - N_Q_HEADS)))
    small = jnp.concatenate([dgn1, dgn2, dgn3, dgf, dconvb, dlng, dlnb, sink_row, dconvw,
                             jnp.zeros((1, D), F32)], axis=0)
    parts["ffn1_w_up"], small_all, gb_all = _exchange(
        [(blocks8(gwu1), "scatter"), (small, "gather"), (gb_cols, "gather")], name="exchange_last")
    gsmall = _sum8(small_all, name="sum_small")
    g_b_ada = gb_all.reshape(1, N_MOD * D)
    g_conv_w = lax.dynamic_slice(gsmall[n_small:n_small + CONV_WIDTH], (0, me * (CC // N_DEV)),
                                 (CONV_WIDTH, CC // N_DEV))

    def col_update(name, w, m, v):
        outs = _sum8_adamw(parts[name], w[0].T, m[0].T, v[0].T, name="adamw_" + name)
        return tuple(o.T for o in outs)

    def row_update(name, w, m, v):
        return tuple(_sum8_adamw(parts[name], w[0], m[0], v[0], name="adamw_" + name))

    upd = {
        "ffn1_w_gate": col_update("ffn1_w_gate", ffn1_w_gate, m_ffn1_w_gate, v_ffn1_w_gate),
        "ffn1_w_up": col_update("ffn1_w_up", ffn1_w_up, m_ffn1_w_up, v_ffn1_w_up),
        "ffn1_w_down": row_update("ffn1_w_down", ffn1_w_down, m_ffn1_w_down, v_ffn1_w_down),
        "w_in": col_update("w_in", w_in, m_w_in, v_w_in),
        "w_attn_o": row_update("w_attn_o", w_attn_o, m_w_attn_o, v_w_attn_o),
        "w_conv_o": row_update("w_conv_o", w_conv_o, m_w_conv_o, v_w_conv_o),
        "w_out": row_update("w_out", w_out, m_w_out, v_w_out),
        "ffn2_w_gate": col_update("ffn2_w_gate", ffn2_w_gate, m_ffn2_w_gate, v_ffn2_w_gate),
        "ffn2_w_up": col_update("ffn2_w_up", ffn2_w_up, m_ffn2_w_up, v_ffn2_w_up),
        "ffn2_w_down": row_update("ffn2_w_down", ffn2_w_down, m_ffn2_w_down, v_ffn2_w_down),
        "w_ada": (g_w_ada,) + tuple(_adamw(g_w_ada, w_ada[0], m_w_ada[0], v_w_ada[0], name="adamw_w_ada")),
        "conv_w_dw": (g_conv_w,) + tuple(_adamw(g_conv_w, conv_w_dw[0], m_conv_w_dw[0], v_conv_w_dw[0],
                                                name="adamw_conv_w_dw")),
    }
    for k in upd:
        upd[k] = tuple(t[None] for t in upd[k])

    def pad_sinks(t):
        return jnp.pad(t, ((0, 0), (0, D - N_Q_HEADS)))

    def pack(f1, mix, f2, fin, cb, lg, lb, sinks, bada):
        return jnp.concatenate([f1, mix, f2, fin[None], cb, lg, lb, pad_sinks(sinks), bada.reshape(N_MOD, D)], axis=0)

    w_s = pack(norm_ffn1_g, norm_mix_g, norm_ffn2_g, final_norm_g, conv_b_dw, conv_ln_g, conv_ln_b, attn_sinks, b_ada)
    m_s = pack(m_norm_ffn1_g, m_norm_mix_g, m_norm_ffn2_g, m_final_norm_g, m_conv_b_dw, m_conv_ln_g, m_conv_ln_b,
               m_attn_sinks, m_b_ada)
    v_s = pack(v_norm_ffn1_g, v_norm_mix_g, v_norm_ffn2_g, v_final_norm_g, v_conv_b_dw, v_conv_ln_g, v_conv_ln_b,
               v_attn_sinks, v_b_ada)
    g_s = jnp.concatenate([gsmall[:n_small], g_b_ada.reshape(N_MOD, D)], axis=0)
    small_out = (g_s,) + tuple(_adamw(g_s, w_s, m_s, v_s, name="adamw_vectors"))

    def unpack(t):
        return {
            "norm_ffn1_g": t[0:1], "norm_mix_g": t[1:2], "norm_ffn2_g": t[2:3], "final_norm_g": t[3],
            "conv_b_dw": t[4:5], "conv_ln_g": t[5:6], "conv_ln_b": t[6:7], "attn_sinks": t[7:8, :N_Q_HEADS],
            "b_ada": t[n_small:n_small + N_MOD].reshape(1, N_MOD * D),
        }

    small_un = [unpack(t) for t in small_out]
    for k in small_un[0]:
        upd[k] = tuple(s[k] for s in small_un)

    order = ["w_ada", "b_ada", "norm_ffn1_g", "ffn1_w_gate", "ffn1_w_up", "ffn1_w_down", "norm_mix_g", "w_in",
             "attn_sinks", "w_attn_o", "conv_w_dw", "conv_b_dw", "conv_ln_g", "conv_ln_b", "w_conv_o", "w_out",
             "norm_ffn2_g", "ffn2_w_gate", "ffn2_w_up", "ffn2_w_down", "final_norm_g"]
    grad_x = dx0.reshape(B, S, D)
    return (loss, grad_x, *[upd[k][0] for k in order], *[upd[k][1] for k in order],
            *[upd[k][2] for k in order], *[upd[k][3] for k in order])
```

```python
import jax
import jax.numpy as jnp
from jax import lax
from jax.experimental import pallas as pl
from jax.experimental.pallas import tpu as pltpu

F32 = jnp.float32
BF16 = jnp.bfloat16
SDS = jax.ShapeDtypeStruct
MESH = pl.DeviceIdType.MESH

N_DEV = 8
EPS = 1e-6
HEAD_DIM = 64
N_Q_HEADS = 16
N_KV_HEADS = 2
GQA_GROUP = N_Q_HEADS // N_KV_HEADS
KV_WIDTH = N_KV_HEADS * HEAD_DIM
ATT_BLOCK = 128
CONV_WIDTH = 31
CONV_HALO = 32
CONV_ROWS = 64
N_MOD = 9
FFN_RESIDUAL = 0.5
ADAM_LR = 0.001
ADAM_B1 = 0.9
ADAM_B2 = 0.999
ADAM_EPS = 1e-08
ADAM_WD = 0.01
ADAM_STEP = 10
NEG_BIG = -1e30

V7X_VMEM_BYTES = 64 * 2**20
VMEM_CAP = V7X_VMEM_BYTES - 8 * 2**20


def _nbytes(shape, dtype):
    n = 1
    for s in shape:
        n *= s
    return n * jnp.dtype(dtype).itemsize


def _params(n_axes, blocks, temp_bytes=0):
    need = 2 * sum(_nbytes(s, d) for s, d in blocks) + temp_bytes + 4 * 2**20
    return pltpu.CompilerParams(dimension_semantics=("arbitrary",) * n_axes,
                                vmem_limit_bytes=int(min(max(need, 16 * 2**20), VMEM_CAP)))


def _dot_nt(a, b):
    return lax.dot_general(a, b, (((1,), (1,)), ((), ())), preferred_element_type=F32)


def _dot_tn(a, b):
    return lax.dot_general(a, b, (((0,), (0,)), ((), ())), preferred_element_type=F32)


def _dot(a, b):
    return jnp.dot(a, b, preferred_element_type=F32)


def _sigmoid(x):
    return jax.nn.sigmoid(x)


def _rowsum(v):
    return jnp.sum(v, axis=0, keepdims=True)


def _acc(ref, val, first):
    @pl.when(first)
    def _():
        ref[...] = val

    @pl.when(jnp.logical_not(first))
    def _():
        ref[...] = ref[...] + val


def _norm_mod(xf, gn, sh, sc):
    rstd = lax.rsqrt(jnp.mean(xf * xf, axis=-1, keepdims=True) + EPS)
    xhat = xf * rstd
    yn = xhat * gn
    return yn * (1.0 + sc) + sh, xhat, rstd, yn


def _pick(n, cands):
    for c in cands:
        if n % c == 0:
            return c
    return n


def _my_pos():
    return lax.axis_index("x"), lax.axis_index("y"), lax.axis_index("c")


def _peer(pos, k):
    x, y, c = pos
    return ((1 - x) if k & 4 else x, (1 - y) if k & 2 else y, (1 - c) if k & 1 else c)


def _lin(pos):
    return 4 * pos[0] + 2 * pos[1] + pos[2]


class _Comm:
    N_COPY = N_DEV - 1
    N_CHIP = N_DEV // 2

    def __init__(self, items):
        self.arrs = [a for a, _ in items]
        self.modes = [m for _, m in items]
        self.n = len(items)
        self.out = None

    def out_shape(self):
        def shape(a, m):
            return {"gather": (N_DEV,) + a.shape, "scatter": a.shape, "pair": (self.N_CHIP,) + a.shape[1:],
                    "cross": a.shape}[m]
        return [SDS(shape(a, m), a.dtype) for a, m in zip(self.arrs, self.modes)]

    def scratch(self):
        return [pltpu.SemaphoreType.DMA((self.n * self.N_COPY,)), pltpu.SemaphoreType.DMA((self.n * self.N_COPY,)),
                pltpu.SemaphoreType.DMA((self.n,))]

    def _plan(self, mode, me):
        x, y, c = me
        sib = (x, y, 1 - c)
        chips = [(1 - x, y), (x, 1 - y), (1 - x, 1 - y)]

        def chip_lin(ch):
            return 2 * ch[0] + ch[1]

        if mode == "scatter":
            peers = [_peer(me, k + 1) for k in range(self.N_COPY)]
            return [(p, ("in", _lin(p)), _lin(me), _lin(p), None) for p in peers], (_lin(me), _lin(me))
        if mode == "gather":
            same = [(*ch, c) for ch in chips]
            other = [(*ch, 1 - c) for ch in chips]
            copies = [(sib, ("in", None), _lin(me), _lin(sib), None)]
            copies += [(p, ("in", None), _lin(me), _lin(p), None) for p in same]
            copies += [(sib, ("out", _lin(p)), _lin(p), _lin(o), 1 + j) for j, (p, o) in enumerate(zip(same, other))]
            return copies, (None, _lin(me))
        if mode == "pair":
            return [(sib, ("in", 2 * q + 1 - c), q, q, None) for q in range(self.N_CHIP)], None
        if mode == "cross":
            mine = chip_lin((x, y))
            return ([((*ch, c), ("in", chip_lin(ch)), mine, chip_lin(ch), None) for ch in chips], (mine, mine))
        raise ValueError(mode)

    def _copy(self, refs, me, i, k, recv):
        srcs, outs, (send_sems, recv_sems, _) = refs
        peer, (where, slot), send_slot, recv_slot, _ = self._plan(self.modes[i], me)[0][k]
        src = srcs[i] if where == "in" else outs[i]
        src = src if slot is None else src.at[slot]
        sem = i * self.N_COPY + k
        return pltpu.make_async_remote_copy(
            src_ref=src, dst_ref=outs[i].at[recv_slot if recv else send_slot], send_sem=send_sems.at[sem],
            recv_sem=recv_sems.at[sem], device_id=peer, device_id_type=MESH)

    def _local(self, refs, me, i):
        srcs, outs, (_, _, loc_sems) = refs
        local = self._plan(self.modes[i], me)[1]
        if local is None:
            return None
        own = srcs[i] if local[0] is None else srcs[i].at[local[0]]
        return pltpu.make_async_copy(own, outs[i].at[local[1]], loc_sems.at[i])

    def start(self, refs):
        me = _my_pos()
        for i in range(self.n):
            local = self._local(refs, me, i)
            if local is not None:
                local.start()
            for k, cp in enumerate(self._plan(self.modes[i], me)[0]):
                if cp[4] is None:
                    self._copy(refs, me, i, k, False).start()

    def finish(self, refs):
        me = _my_pos()
        plans = [self._plan(m, me)[0] for m in self.modes]
        for i in range(self.n):
            for k, cp in enumerate(plans[i]):
                if cp[4] is not None:
                    self._copy(refs, me, i, cp[4], True).wait_recv()
                    self._copy(refs, me, i, k, False).start()
        for i in range(self.n):
            passed_on = [cp[4] for cp in plans[i] if cp[4] is not None]
            for k in range(len(plans[i])):
                if k not in passed_on:
                    self._copy(refs, me, i, k, True).wait_recv()
                self._copy(refs, me, i, k, False).wait_send()
            local = self._local(refs, me, i)
            if local is not None:
                local.wait()


_ANY = pl.BlockSpec(memory_space=pl.ANY)


def _call(body, args, *, name, grid, in_specs, out_specs, out_shape, params, scratch_shapes=(), comm=None):
    in_specs, out_specs, out_shape = list(in_specs), list(out_specs), list(out_shape)
    scratch_shapes = list(scratch_shapes)
    if comm is None:
        return list(pl.pallas_call(body, name=name, grid=grid, in_specs=in_specs, out_specs=out_specs,
                                   out_shape=out_shape, scratch_shapes=scratch_shapes, compiler_params=params)(*args))
    n_in, n_out, n_scr, nc = len(in_specs), len(out_specs), len(scratch_shapes), comm.n

    def hosted(*refs):
        ins, c_in = refs[:n_in], refs[n_in:n_in + nc]
        outs = refs[n_in + nc:n_in + nc + n_out]
        c_out = refs[n_in + nc + n_out:n_in + 2 * nc + n_out]
        scr = refs[n_in + 2 * nc + n_out:n_in + 2 * nc + n_out + n_scr]
        sems = refs[n_in + 2 * nc + n_out + n_scr:]
        first = pl.program_id(0) == 0
        last = pl.program_id(0) == grid[0] - 1
        for d in range(1, len(grid)):
            first = jnp.logical_and(first, pl.program_id(d) == 0)
            last = jnp.logical_and(last, pl.program_id(d) == grid[d] - 1)

        @pl.when(first)
        def _():
            comm.start((c_in, c_out, sems))

        body(*ins, *outs, *scr)

        @pl.when(last)
        def _():
            comm.finish((c_in, c_out, sems))

    res = pl.pallas_call(
        hosted, name=name, grid=grid, in_specs=in_specs + [_ANY] * nc, out_specs=out_specs + [_ANY] * nc,
        out_shape=out_shape + comm.out_shape(), scratch_shapes=scratch_shapes + comm.scratch(),
        compiler_params=params)(*args, *comm.arrs)
    comm.out = list(res[n_out:])
    return list(res[:n_out])


def _exchange(items, *, name):
    comm = _Comm(items)

    def body(*refs):
        r = (refs[:comm.n], refs[comm.n:2 * comm.n], refs[2 * comm.n:])
        comm.start(r)
        comm.finish(r)

    return list(pl.pallas_call(body, name=name, out_shape=comm.out_shape(), in_specs=[_ANY] * comm.n,
                               out_specs=[_ANY] * comm.n, scratch_shapes=comm.scratch())(*comm.arrs))


def _norm_mod_matmul(x, gn, sh, sc, wts, *, seq, tm, tn, name, comm=None):
    T, D = x.shape
    N = wts[0].shape[0]
    nw = len(wts)
    tps = seq // tm

    def body(x_ref, gn_ref, sh_ref, sc_ref, *rest):
        w_refs, h_ref, o_refs = rest[:nw], rest[nw], rest[nw + 1:]

        @pl.when(pl.program_id(1) == 0)
        def _():
            h_ref[...] = _norm_mod(x_ref[...], gn_ref[...], sh_ref[0], sc_ref[0])[0].astype(BF16)

        h = h_ref[...]
        for w_ref, o_ref in zip(w_refs, o_refs):
            o_ref[...] = _dot_nt(h, w_ref[...]).astype(o_ref.dtype)

    row = pl.BlockSpec((tm, D), lambda i, j: (i, 0))
    vec = pl.BlockSpec((1, D), lambda i, j: (0, 0))
    per_b = pl.BlockSpec((1, 1, D), lambda i, j: (i // tps, 0, 0))
    wspec = pl.BlockSpec((tn, D), lambda i, j: (j, 0))
    ospec = pl.BlockSpec((tm, tn), lambda i, j: (i, j))
    blocks = [((tm, D), F32), ((tm, D), BF16)] + [((tn, D), BF16), ((tm, tn), BF16)] * nw
    outs = _call(
        body, (x, gn, sh, sc, *wts), name=name, grid=(T // tm, N // tn),
        in_specs=[row, vec, per_b, per_b] + [wspec] * nw,
        out_specs=[row] + [ospec] * nw,
        out_shape=[SDS((T, D), BF16)] + [SDS((T, N), BF16)] * nw,
        params=_params(2, blocks, temp_bytes=2 * _nbytes((tm, tn), F32) + 3 * _nbytes((tm, D), F32)), comm=comm)
    return outs[0], outs[1:]


def _ffn_down(a, b, wd, x, g, *, seq, tm, name, comm=None):
    T, F = a.shape
    D = wd.shape[1]
    tps = seq // tm

    def body(a_ref, b_ref, wd_ref, x_ref, g_ref, xo_ref, y_ref):
        af = a_ref[...].astype(F32)
        act = (af * _sigmoid(af) * b_ref[...].astype(F32)).astype(BF16)
        y = _dot(act, wd_ref[...])
        xo_ref[...] = x_ref[...] + (FFN_RESIDUAL * g_ref[0]) * y
        y_ref[...] = y.astype(BF16)

    wide = pl.BlockSpec((tm, F), lambda i: (i, 0))
    row = pl.BlockSpec((tm, D), lambda i: (i, 0))
    per_b = pl.BlockSpec((1, 1, D), lambda i: (i // tps, 0, 0))
    wspec = pl.BlockSpec((F, D), lambda i: (0, 0))
    blocks = [((tm, F), BF16)] * 2 + [((F, D), BF16), ((tm, D), F32), ((tm, D), F32), ((tm, D), BF16)]
    return _call(
        body, (a, b, wd, x, g), name=name, grid=(T // tm,),
        in_specs=[wide, wide, wspec, row, per_b], out_specs=[row, row],
        out_shape=[SDS((T, D), F32), SDS((T, D), BF16)],
        params=_params(1, blocks, temp_bytes=3 * _nbytes((tm, F), F32)), comm=comm)


def _final_loss(x, gf, tgt, *, tm, name):
    T, D = x.shape
    nt = T // tm

    def body(x_ref, gf_ref, t_ref, dx_ref, loss_ref, dgf_ref, lacc):
        i = pl.program_id(0)
        xf = x_ref[...]
        gfv = gf_ref[...]
        rstd = lax.rsqrt(jnp.mean(xf * xf, axis=-1, keepdims=True) + EPS)
        xhat = xf * rstd
        err = xhat * gfv - t_ref[...]
        dy = err * (1.0 / D)
        dxhat = dy * gfv
        dx_ref[...] = rstd * (dxhat - xhat * jnp.mean(dxhat * xhat, axis=-1, keepdims=True))
        _acc(dgf_ref, _rowsum(dy * xhat), i == 0)
        _acc(lacc, _rowsum(err * err), i == 0)

        @pl.when(i == nt - 1)
        def _():
            loss_ref[...] = jnp.broadcast_to((0.5 / D) * jnp.sum(lacc[...]), loss_ref.shape)

    row = pl.BlockSpec((tm, D), lambda i: (i, 0))
    vec = pl.BlockSpec((1, D), lambda i: (0, 0))
    lspec = pl.BlockSpec((1, 128), lambda i: (0, 0))
    blocks = [((tm, D), F32)] * 3
    return _call(
        body, (x, gf, tgt), name=name, grid=(nt,),
        in_specs=[row, vec, row], out_specs=[row, lspec, vec],
        out_shape=[SDS((T, D), F32), SDS((1, 128), F32), SDS((1, D), F32)],
        scratch_shapes=[pltpu.VMEM((1, D), F32)],
        params=_params(1, blocks, temp_bytes=4 * _nbytes((tm, D), F32)))


def _ffn_bwd_down(dxo, g, y, wd, a, b, *, seq, tm, tn, name, comm=None):
    T, F = a.shape
    D = wd.shape[1]
    tps = seq // tm
    nb = T // seq

    def body(dxo_ref, g_ref, y_ref, wd_ref, a_ref, b_ref, dyb_ref, da_ref, db_ref, act_ref, dg_ref):
        i = pl.program_id(0)

        @pl.when(pl.program_id(1) == 0)
        def _():
            dx = dxo_ref[...]
            dyb_ref[...] = ((FFN_RESIDUAL * g_ref[0]) * dx).astype(BF16)
            part = _rowsum(FFN_RESIDUAL * dx * y_ref[...].astype(F32))
            _acc(dg_ref, part[None], i % tps == 0)

        dact = _dot_nt(dyb_ref[...], wd_ref[...])
        af = a_ref[...].astype(F32)
        bf = b_ref[...].astype(F32)
        sg = _sigmoid(af)
        silu = af * sg
        act_ref[...] = (silu * bf).astype(BF16)
        da_ref[...] = (dact * bf * (sg * (1.0 + af * (1.0 - sg)))).astype(BF16)
        db_ref[...] = (dact * silu).astype(BF16)

    row = pl.BlockSpec((tm, D), lambda i, j: (i, 0))
    per_b = pl.BlockSpec((1, 1, D), lambda i, j: (i // tps, 0, 0))
    wspec = pl.BlockSpec((tn, D), lambda i, j: (j, 0))
    chunk = pl.BlockSpec((tm, tn), lambda i, j: (i, j))
    blocks = [((tm, D), F32), ((tm, D), BF16), ((tn, D), BF16), ((tm, D), BF16)] + [((tm, tn), BF16)] * 5
    return _call(
        body, (dxo, g, y, wd, a, b), name=name, grid=(T // tm, F // tn),
        in_specs=[row, per_b, row, wspec, chunk, chunk],
        out_specs=[row, chunk, chunk, chunk, per_b],
        out_shape=[SDS((T, D), BF16)] + [SDS((T, F), BF16)] * 3 + [SDS((nb, 1, D), F32)],
        params=_params(2, blocks, temp_bytes=6 * _nbytes((tm, tn), F32)), comm=comm)


def _matmul_norm_mod_bwd(ds, ws, x, gn, sc, dxo, *, seq, tm, name, comm=None):
    T, D = x.shape
    nk = len(ds)
    tps = seq // tm
    nb = T // seq

    def body(*refs):
        d_refs, w_refs = refs[:nk], refs[nk:2 * nk]
        x_ref, gn_ref, sc_ref, dxo_ref, dxi_ref, dsh_ref, dsc_ref, dgn_ref = refs[2 * nk:]
        i = pl.program_id(0)
        dh = _dot(d_refs[0][...], w_refs[0][...])
        for d_ref, w_ref in zip(d_refs[1:], w_refs[1:]):
            dh = dh + _dot(d_ref[...], w_ref[...])
        gnv = gn_ref[...]
        scv = sc_ref[0]
        _, xhat, rstd, yn = _norm_mod(x_ref[...], gnv, 0.0, scv)
        dyn = dh * (1.0 + scv)
        dxhat = dyn * gnv
        dxi_ref[...] = dxo_ref[...] + rstd * (dxhat - xhat * jnp.mean(dxhat * xhat, axis=-1, keepdims=True))
        first_of_seq = i % tps == 0
        _acc(dsh_ref, _rowsum(dh)[None], first_of_seq)
        _acc(dsc_ref, _rowsum(dh * yn)[None], first_of_seq)
        _acc(dgn_ref, _rowsum(dyn * xhat), i == 0)

    row = pl.BlockSpec((tm, D), lambda i: (i, 0))
    vec = pl.BlockSpec((1, D), lambda i: (0, 0))
    per_b = pl.BlockSpec((1, 1, D), lambda i: (i // tps, 0, 0))
    d_specs = [pl.BlockSpec((tm, d.shape[1]), lambda i: (i, 0)) for d in ds]
    w_specs = [pl.BlockSpec(w.shape, lambda i: (0, 0)) for w in ws]
    blocks = ([((tm, d.shape[1]), BF16) for d in ds] + [(w.shape, BF16) for w in ws] + [((tm, D), F32)] * 3)
    return _call(
        body, (*ds, *ws, x, gn, sc, dxo), name=name, grid=(T // tm,),
        in_specs=d_specs + w_specs + [row, vec, per_b, row],
        out_specs=[row, per_b, per_b, vec],
        out_shape=[SDS((T, D), F32), SDS((nb, 1, D), F32), SDS((nb, 1, D), F32), SDS((1, D), F32)],
        params=_params(1, blocks, temp_bytes=6 * _nbytes((tm, D), F32)), comm=comm)


def _layernorm_silu(yc, lg, lb):
    mu = jnp.mean(yc, axis=-1, keepdims=True)
    cen = yc - mu
    rstd = lax.rsqrt(jnp.mean(cen * cen, axis=-1, keepdims=True) + EPS)
    xh = cen * rstd
    l = xh * lg + lb
    s = _sigmoid(l)
    return l * s, xh, rstd, l, s


def _mix_out(ao, yc, projp, wao, wco, wout, x1, g2, lg, lb, *, seq, tm, ga_blk, gc_blk, name, comm=None):
    T, D = x1.shape
    tps = seq // tm

    def body(ao_ref, yc_ref, ga_ref, gc_ref, wao_ref, wco_ref, wout_ref, x1_ref, g2_ref, lg_ref, lb_ref,
             x2_ref, z_ref, ya_ref, ycv_ref, cact_ref, mrg_ref):
        ya = _dot(ao_ref[...], wao_ref[...])
        cact = _layernorm_silu(yc_ref[...], lg_ref[...], lb_ref[...])[0].astype(BF16)
        ycv = _dot(cact, wco_ref[...])
        merged = (_sigmoid(ga_ref[...].astype(F32)) * ya + _sigmoid(gc_ref[...].astype(F32)) * ycv).astype(BF16)
        z = _dot(merged, wout_ref[...])
        x2_ref[...] = x1_ref[...] + g2_ref[0] * z
        z_ref[...] = z.astype(BF16)
        ya_ref[...] = ya.astype(BF16)
        ycv_ref[...] = ycv.astype(BF16)
        cact_ref[...] = cact
        mrg_ref[...] = merged

    row = pl.BlockSpec((tm, D), lambda i: (i, 0))
    vec = pl.BlockSpec((1, D), lambda i: (0, 0))
    per_b = pl.BlockSpec((1, 1, D), lambda i: (i // tps, 0, 0))
    wspec = pl.BlockSpec((D, D), lambda i: (0, 0))
    ga_spec = pl.BlockSpec((tm, D), lambda i: (i, ga_blk))
    gc_spec = pl.BlockSpec((tm, D), lambda i: (i, gc_blk))
    blocks = ([((tm, D), BF16), ((tm, D), F32), ((tm, D), BF16), ((tm, D), BF16)] + [((D, D), BF16)] * 3
              + [((tm, D), F32)] * 2 + [((tm, D), BF16)] * 5)
    return _call(
        body, (ao, yc, projp, projp, wao, wco, wout, x1, g2, lg, lb), name=name, grid=(T // tm,),
        in_specs=[row, row, ga_spec, gc_spec, wspec, wspec, wspec, row, per_b, vec, vec],
        out_specs=[row] * 6,
        out_shape=[SDS((T, D), F32)] + [SDS((T, D), BF16)] * 5,
        params=_params(1, blocks, temp_bytes=8 * _nbytes((tm, D), F32)), comm=comm)


def _mix_out_bwd(dx2, g2, z, wout, projp, ya, ycv, wao, wco, yc, lg, lb, *, seq, tm, ga_blk, gc_blk, name,
                 comm=None):
    T, D = dx2.shape
    tps = seq // tm
    nb = T // seq

    def body(dx2_ref, g2_ref, z_ref, wout_ref, ga_ref, gc_ref, ya_ref, ycv_ref, wao_ref, wco_ref, yc_ref,
             lg_ref, lb_ref, dz_ref, dya_ref, dycv_ref, dga_ref, dgc_ref, dao_ref, dyc_ref, dg2_ref, dlg_ref,
             dlb_ref):
        i = pl.program_id(0)
        dx = dx2_ref[...]
        _acc(dg2_ref, _rowsum(dx * z_ref[...].astype(F32))[None], i % tps == 0)
        dzb = (g2_ref[0] * dx).astype(BF16)
        dz_ref[...] = dzb
        dmerged = _dot_nt(dzb, wout_ref[...])
        sa = _sigmoid(ga_ref[...].astype(F32))
        sc_ = _sigmoid(gc_ref[...].astype(F32))
        dya = (dmerged * sa).astype(BF16)
        dycv = (dmerged * sc_).astype(BF16)
        dya_ref[...] = dya
        dycv_ref[...] = dycv
        dga_ref[...] = (dmerged * ya_ref[...].astype(F32) * (sa * (1.0 - sa))).astype(BF16)
        dgc_ref[...] = (dmerged * ycv_ref[...].astype(F32) * (sc_ * (1.0 - sc_))).astype(BF16)
        dao_ref[...] = _dot_nt(dya, wao_ref[...]).astype(BF16)
        dcact = _dot_nt(dycv, wco_ref[...])
        lgv = lg_ref[...]
        _, xh, rstd, l, s = _layernorm_silu(yc_ref[...], lgv, lb_ref[...])
        dl = dcact * (s * (1.0 + l * (1.0 - s)))
        _acc(dlb_ref, _rowsum(dl), i == 0)
        _acc(dlg_ref, _rowsum(dl * xh), i == 0)
        dxh = dl * lgv
        dyc_ref[...] = rstd * (dxh - jnp.mean(dxh, axis=-1, keepdims=True)
                               - xh * jnp.mean(dxh * xh, axis=-1, keepdims=True))

    row = pl.BlockSpec((tm, D), lambda i: (i, 0))
    vec = pl.BlockSpec((1, D), lambda i: (0, 0))
    per_b = pl.BlockSpec((1, 1, D), lambda i: (i // tps, 0, 0))
    wspec = pl.BlockSpec((D, D), lambda i: (0, 0))
    ga_spec = pl.BlockSpec((tm, D), lambda i: (i, ga_blk))
    gc_spec = pl.BlockSpec((tm, D), lambda i: (i, gc_blk))
    blocks = ([((tm, D), F32)] * 3 + [((tm, D), BF16)] * 11 + [((D, D), BF16)] * 3)
    return _call(
        body, (dx2, g2, z, wout, projp, projp, ya, ycv, wao, wco, yc, lg, lb), name=name, grid=(T // tm,),
        in_specs=[row, per_b, row, wspec, ga_spec, gc_spec, row, row, wspec, wspec, row, vec, vec],
        out_specs=[row] * 7 + [per_b, vec, vec],
        out_shape=[SDS((T, D), BF16)] * 6 + [SDS((T, D), F32), SDS((nb, 1, D), F32), SDS((1, D), F32),
                                             SDS((1, D), F32)],
        params=_params(1, blocks, temp_bytes=10 * _nbytes((tm, D), F32)), comm=comm)


GROUP_ROWS = GQA_GROUP * ATT_BLOCK
PAIR_W = 2 * HEAD_DIM
GROUP_W = GQA_GROUP * HEAD_DIM


def _lane_lo():
    return lax.broadcasted_iota(jnp.int32, (1, PAIR_W), 1) < HEAD_DIM


def _band_bias():
    sj = lax.broadcasted_iota(jnp.int32, (2 * ATT_BLOCK, GROUP_ROWS), 0)
    qi = lax.broadcasted_iota(jnp.int32, (2 * ATT_BLOCK, GROUP_ROWS), 1) & (ATT_BLOCK - 1)
    rel = qi + ATT_BLOCK - sj
    bias = jnp.where(jnp.logical_and(rel >= 0, rel < ATT_BLOCK), 0.0, NEG_BIG)
    sj1 = lax.broadcasted_iota(jnp.int32, (2 * ATT_BLOCK, 1), 0)
    return bias, jnp.where(sj1 < ATT_BLOCK, NEG_BIG, 0.0)


def _dup_heads(src_ref, dst, seq):
    x = src_ref[...]
    i = lax.broadcasted_iota(jnp.int32, (KV_WIDTH, PAIR_W), 0)
    j = lax.broadcasted_iota(jnp.int32, (KV_WIDTH, PAIR_W), 1) & (HEAD_DIM - 1)
    for g in range(N_KV_HEADS):
        sel = jnp.where(i == j + g * HEAD_DIM, 1.0, 0.0).astype(BF16)
        dst[g, pl.ds(0, ATT_BLOCK), :] = jnp.zeros((ATT_BLOCK, PAIR_W), BF16)
        dst[g, pl.ds(ATT_BLOCK, seq), :] = _dot(x, sel).astype(BF16)


def _stack_heads(blk, g, lo):
    parts = []
    for p in range(GQA_GROUP // 2):
        pair = blk[:, g * GROUP_W + p * PAIR_W:g * GROUP_W + (p + 1) * PAIR_W]
        parts += [jnp.where(lo, pair, jnp.zeros_like(pair)), jnp.where(lo, jnp.zeros_like(pair), pair)]
    return jnp.concatenate(parts, axis=0)


def _unstack_heads(full, ref, r0, g, lo):
    for p in range(GQA_GROUP // 2):
        even = full[(2 * p) * ATT_BLOCK:(2 * p + 1) * ATT_BLOCK, :]
        odd = full[(2 * p + 1) * ATT_BLOCK:(2 * p + 2) * ATT_BLOCK, :]
        ref[pl.ds(r0, ATT_BLOCK), g * GROUP_W + p * PAIR_W:g * GROUP_W + (p + 1) * PAIR_W] = (
            jnp.where(lo, even, odd).astype(ref.dtype))


def _sink_row(sink_ref, g):
    return jnp.concatenate([jnp.full((1, ATT_BLOCK), sink_ref[0, g * GQA_GROUP + h], F32)
                            for h in range(GQA_GROUP)], axis=1)


def _group_probs(qs, k2, bias, sink):
    s = _dot_nt(k2, qs) * (HEAD_DIM ** -0.5) + bias
    m = jnp.maximum(jnp.max(s, axis=0, keepdims=True), sink)
    p = jnp.exp(s - m)
    psink = jnp.exp(sink - m)
    inv = 1.0 / (jnp.sum(p, axis=0, keepdims=True) + psink)
    return p * inv, psink * inv


def _attn_fwd(projp, sinks, *, seq, q_blk, k_blk, v_blk, name, comm=None):
    T = projp.shape[0]
    QW = N_Q_HEADS * HEAD_DIM
    nblk = seq // ATT_BLOCK

    def body(q_ref, k_ref, v_ref, sink_ref, o_ref, k2s, v2s):
        _dup_heads(k_ref, k2s, seq)
        _dup_heads(v_ref, v2s, seq)
        lo = _lane_lo()
        bias0, first_pen = _band_bias()
        sink_rows = [_sink_row(sink_ref, g) for g in range(N_KV_HEADS)]

        def blk(n, carry):
            r0 = pl.multiple_of(n * ATT_BLOCK, ATT_BLOCK)
            qb = q_ref[pl.ds(r0, ATT_BLOCK), :]
            bias = bias0 + jnp.where(n == 0, 1.0, 0.0) * first_pen
            for g in range(N_KV_HEADS):
                probs_t, _ = _group_probs(_stack_heads(qb, g, lo), k2s[g, pl.ds(r0, 2 * ATT_BLOCK), :], bias,
                                          sink_rows[g])
                _unstack_heads(_dot_tn(probs_t.astype(BF16), v2s[g, pl.ds(r0, 2 * ATT_BLOCK), :]), o_ref, r0, g, lo)
            return carry

        lax.fori_loop(0, nblk, blk, 0)

    blocks = [((seq, QW), BF16)] * 2 + [((seq, KV_WIDTH), BF16)] * 2
    return _call(
        body, (projp, projp, projp, sinks), name=name, grid=(T // seq,),
        in_specs=[pl.BlockSpec((seq, QW), lambda b: (b, q_blk)),
                  pl.BlockSpec((seq, KV_WIDTH), lambda b: (b, k_blk)),
                  pl.BlockSpec((seq, KV_WIDTH), lambda b: (b, v_blk)),
                  pl.BlockSpec(memory_space=pltpu.SMEM)],
        out_specs=[pl.BlockSpec((seq, QW), lambda b: (b, 0))],
        out_shape=[SDS((T, QW), BF16)],
        scratch_shapes=[pltpu.VMEM((N_KV_HEADS, seq + ATT_BLOCK, PAIR_W), BF16)] * 2,
        params=_params(1, blocks, temp_bytes=16 * 2**20), comm=comm)[0]


def _attn_bwd(projp, dao, sinks, *, seq, q_blk, k_blk, v_blk, name, comm=None):
    T = projp.shape[0]
    QW = N_Q_HEADS * HEAD_DIM
    nblk = seq // ATT_BLOCK

    def body(q_ref, k_ref, v_ref, do_ref, sink_ref, dq_ref, dk_ref, dv_ref, dsink_ref, k2s, v2s, dkacc, dvacc):
        _dup_heads(k_ref, k2s, seq)
        _dup_heads(v_ref, v2s, seq)
        dkacc[...] = jnp.zeros(dkacc.shape, F32)
        dvacc[...] = jnp.zeros(dvacc.shape, F32)
        lane = lax.broadcasted_iota(jnp.int32, (1, PAIR_W), 1)
        lo = lane < HEAD_DIM
        bias0, first_pen = _band_bias()
        sink_rows = [_sink_row(sink_ref, g) for g in range(N_KV_HEADS)]

        def blk(n, dsink):
            r0 = pl.multiple_of(n * ATT_BLOCK, ATT_BLOCK)
            band = pl.ds(r0, 2 * ATT_BLOCK)
            qb = q_ref[pl.ds(r0, ATT_BLOCK), :]
            dob = do_ref[pl.ds(r0, ATT_BLOCK), :]
            bias = bias0 + jnp.where(n == 0, 1.0, 0.0) * first_pen
            for g in range(N_KV_HEADS):
                qs = _stack_heads(qb, g, lo)
                dos = _stack_heads(dob, g, lo)
                k2 = k2s[g, band, :]
                v2 = v2s[g, band, :]
                probs_t, psink = _group_probs(qs, k2, bias, sink_rows[g])
                dp_t = _dot_nt(v2, dos)
                delta = jnp.sum(probs_t * dp_t, axis=0, keepdims=True)
                ds_t = (probs_t * (dp_t - delta) * (HEAD_DIM ** -0.5)).astype(BF16)
                tsink = psink * delta
                for h in range(GQA_GROUP):
                    dsink = dsink + jnp.where(lane == g * GQA_GROUP + h,
                                              -jnp.sum(tsink[:, h * ATT_BLOCK:(h + 1) * ATT_BLOCK]), 0.0)
                _unstack_heads(_dot_tn(ds_t, k2), dq_ref, r0, g, lo)
                dkacc[g, band, :] = dkacc[g, band, :] + _dot(ds_t, qs)
                dvacc[g, band, :] = dvacc[g, band, :] + _dot(probs_t.astype(BF16), dos)
            return dsink

        dsink = lax.fori_loop(0, nblk, blk, jnp.zeros((1, PAIR_W), F32))
        _acc(dsink_ref, dsink, pl.program_id(0) == 0)

        def fold(acc, g):
            a = acc[g, pl.ds(ATT_BLOCK, seq), :]
            return a + pltpu.roll(a, HEAD_DIM, 1)

        dk_ref[...] = jnp.where(lo, fold(dkacc, 0), fold(dkacc, 1)).astype(BF16)
        dv_ref[...] = jnp.where(lo, fold(dvacc, 0), fold(dvacc, 1)).astype(BF16)

    blocks = [((seq, QW), BF16)] * 3 + [((seq, KV_WIDTH), BF16)] * 4
    kv_spec_out = pl.BlockSpec((seq, KV_WIDTH), lambda b: (b, 0))
    return _call(
        body, (projp, projp, projp, dao, sinks), name=name, grid=(T // seq,),
        in_specs=[pl.BlockSpec((seq, QW), lambda b: (b, q_blk)),
                  pl.BlockSpec((seq, KV_WIDTH), lambda b: (b, k_blk)),
                  pl.BlockSpec((seq, KV_WIDTH), lambda b: (b, v_blk)),
                  pl.BlockSpec((seq, QW), lambda b: (b, 0)),
                  pl.BlockSpec(memory_space=pltpu.SMEM)],
        out_specs=[pl.BlockSpec((seq, QW), lambda b: (b, 0)), kv_spec_out, kv_spec_out,
                   pl.BlockSpec((1, 128), lambda b: (0, 0))],
        out_shape=[SDS((T, QW), BF16), SDS((T, KV_WIDTH), BF16), SDS((T, KV_WIDTH), BF16), SDS((1, 128), F32)],
        scratch_shapes=[pltpu.VMEM((N_KV_HEADS, seq + ATT_BLOCK, PAIR_W), BF16)] * 2
        + [pltpu.VMEM((N_KV_HEADS, seq + ATT_BLOCK, PAIR_W), F32)] * 2,
        params=_params(1, blocks, temp_bytes=24 * 2**20), comm=comm)


SUBLANES = 8


def _sublane_shifts(win):
    n = CONV_ROWS + CONV_HALO
    return [win] + [pltpu.roll(win, n - b, 0) for b in range(1, SUBLANES)]


def _window(shifted, off):
    a = off // SUBLANES * SUBLANES
    return shifted[off % SUBLANES][a:a + CONV_ROWS, :]


def _conv_fwd(projp, w, bias, *, seq, cw, a_col, b_col, name, comm=None):
    T = projp.shape[0]
    C = w.shape[1]
    nchunk = seq // CONV_ROWS

    def body(a_ref, b_ref, w_ref, bias_ref, y_ref, upad):
        upad[pl.ds(0, CONV_HALO), :] = jnp.zeros((CONV_HALO, cw), F32)
        upad[pl.ds(CONV_HALO, seq), :] = a_ref[...].astype(F32) * _sigmoid(b_ref[...].astype(F32))
        wv = w_ref[...]
        bv = bias_ref[...]

        def chunk(r, carry):
            r0 = pl.multiple_of(r * CONV_ROWS, CONV_ROWS)
            shifted = _sublane_shifts(upad[pl.ds(r0, CONV_ROWS + CONV_HALO), :])
            acc = jnp.broadcast_to(bv, (CONV_ROWS, cw))
            for k in range(CONV_WIDTH):
                acc = acc + wv[k:k + 1, :] * _window(shifted, CONV_HALO - (CONV_WIDTH - 1) + k)
            y_ref[pl.ds(r0, CONV_ROWS), :] = acc
            return carry

        lax.fori_loop(0, nchunk, chunk, 0)

    blocks = [((seq, cw), BF16)] * 2 + [((seq, cw), F32)]
    return _call(
        body, (projp, projp, w, bias), name=name, grid=(T // seq, C // cw),
        in_specs=[pl.BlockSpec((seq, cw), lambda b, c: (b, a_col // cw + c)),
                  pl.BlockSpec((seq, cw), lambda b, c: (b, b_col // cw + c)),
                  pl.BlockSpec((CONV_WIDTH, cw), lambda b, c: (0, c)),
                  pl.BlockSpec((1, cw), lambda b, c: (0, c))],
        out_specs=[pl.BlockSpec((seq, cw), lambda b, c: (b, c))],
        out_shape=[SDS((T, C), F32)],
        scratch_shapes=[pltpu.VMEM((seq + CONV_HALO, cw), F32)],
        params=_params(2, blocks, temp_bytes=6 * _nbytes((seq, cw), F32)), comm=comm)[0]


def _conv_bwd(dy, projp, w, *, seq, cw, a_col, b_col, name, comm=None):
    T = projp.shape[0]
    C = w.shape[1]
    nchunk = seq // CONV_ROWS
    SUB = 8

    def body(dy_ref, a_ref, b_ref, w_ref, da_ref, db_ref, dw_ref, dbias_ref, upad, dypad, dwp):
        first = pl.program_id(1) == 0
        af = a_ref[...].astype(F32)
        sb = _sigmoid(b_ref[...].astype(F32))
        upad[pl.ds(0, CONV_HALO), :] = jnp.zeros((CONV_HALO, cw), F32)
        upad[pl.ds(CONV_HALO, seq), :] = af * sb
        dyv = dy_ref[...]
        dypad[pl.ds(0, seq), :] = dyv
        dypad[pl.ds(seq, CONV_HALO), :] = jnp.zeros((CONV_HALO, cw), F32)
        dwp[...] = jnp.zeros(dwp.shape, F32)
        wv = w_ref[...]

        def chunk(r, carry):
            r0 = pl.multiple_of(r * CONV_ROWS, CONV_ROWS)
            wdy = dypad[pl.ds(r0, CONV_ROWS + CONV_HALO), :]
            dy_shifts = _sublane_shifts(wdy)
            u_shifts = _sublane_shifts(upad[pl.ds(r0, CONV_ROWS + CONV_HALO), :])
            dyc = wdy[0:CONV_ROWS, :]
            du = jnp.zeros((CONV_ROWS, cw), F32)
            for k in range(CONV_WIDTH):
                du = du + wv[k:k + 1, :] * _window(dy_shifts, CONV_WIDTH - 1 - k)
                prod = dyc * _window(u_shifts, CONV_HALO - (CONV_WIDTH - 1) + k)
                part = prod[0:SUB, :]
                for s in range(1, CONV_ROWS // SUB):
                    part = part + prod[s * SUB:(s + 1) * SUB, :]
                dwp[pl.ds(k * SUB, SUB), :] = dwp[pl.ds(k * SUB, SUB), :] + part
            ac = a_ref[pl.ds(r0, CONV_ROWS), :].astype(F32)
            sbc = _sigmoid(b_ref[pl.ds(r0, CONV_ROWS), :].astype(F32))
            da_ref[pl.ds(r0, CONV_ROWS), :] = (du * sbc).astype(BF16)
            db_ref[pl.ds(r0, CONV_ROWS), :] = (du * ac * (sbc * (1.0 - sbc))).astype(BF16)
            return carry

        lax.fori_loop(0, nchunk, chunk, 0)

        @pl.when(first)
        def _():
            dw_ref[...] = jnp.zeros(dw_ref.shape, F32)
            dbias_ref[...] = jnp.zeros(dbias_ref.shape, F32)

        for k in range(CONV_WIDTH):
            dw_ref[k:k + 1, :] = dw_ref[k:k + 1, :] + _rowsum(dwp[pl.ds(k * SUB, SUB), :])
        dbias_ref[...] = dbias_ref[...] + _rowsum(dyv)

    blocks = [((seq, cw), F32)] + [((seq, cw), BF16)] * 4
    return _call(
        body, (dy, projp, projp, w), name=name, grid=(C // cw, T // seq),
        in_specs=[pl.BlockSpec((seq, cw), lambda c, b: (b, c)),
                  pl.BlockSpec((seq, cw), lambda c, b: (b, a_col // cw + c)),
                  pl.BlockSpec((seq, cw), lambda c, b: (b, b_col // cw + c)),
                  pl.BlockSpec((CONV_WIDTH, cw), lambda c, b: (0, c))],
        out_specs=[pl.BlockSpec((seq, cw), lambda c, b: (b, c)), pl.BlockSpec((seq, cw), lambda c, b: (b, c)),
                   pl.BlockSpec((CONV_WIDTH, cw), lambda c, b: (0, c)), pl.BlockSpec((1, cw), lambda c, b: (0, c))],
        out_shape=[SDS((T, C), BF16), SDS((T, C), BF16), SDS((CONV_WIDTH, C), F32), SDS((1, C), F32)],
        scratch_shapes=[pltpu.VMEM((seq + CONV_HALO, cw), F32), pltpu.VMEM((seq + CONV_HALO, cw), F32),
                        pltpu.VMEM((CONV_WIDTH * SUB, cw), F32)],
        params=_params(2, blocks, temp_bytes=8 * _nbytes((seq, cw), F32)), comm=comm)


def _matmul_tn(a, b, *, name, comm=None):
    T, M = a.shape
    N = b.shape[1]
    bm = _pick(M, (768, 512, 256))

    def body(a_ref, b_ref, o_ref):
        o_ref[...] = _dot_tn(a_ref[...], b_ref[...]).astype(BF16)

    blocks = [((T, bm), BF16), ((T, N), BF16), ((bm, N), BF16)]
    return _call(
        body, (a, b), name=name, grid=(M // bm,),
        in_specs=[pl.BlockSpec((T, bm), lambda i: (0, i)), pl.BlockSpec((T, N), lambda i: (0, 0))],
        out_specs=[pl.BlockSpec((bm, N), lambda i: (i, 0))],
        out_shape=[SDS((M, N), BF16)],
        params=_params(1, blocks, temp_bytes=2 * _nbytes((T, bm), BF16) + 2 * _nbytes((bm, N), F32)),
        comm=comm)[0]


def _sum_parts(p_ref):
    g = p_ref[0].astype(F32)
    for s in range(1, p_ref.shape[0]):
        g = g + p_ref[s].astype(F32)
    return g


def _pair_add(g, staged, *, name):
    _, R, W = g.shape
    nq = staged.shape[0]
    tr = _row_tile(R)

    def body(g_ref, s_ref, o_ref):
        mine = jnp.where(lax.axis_index("c") == 0, g_ref[0, 0].astype(F32), g_ref[0, 1].astype(F32))
        o_ref[0] = (mine + s_ref[0].astype(F32)).astype(o_ref.dtype)

    return _call(
        body, (g.reshape(nq, 2, R, W), staged), name=name, grid=(nq, R // tr),
        in_specs=[pl.BlockSpec((1, 2, tr, W), lambda q, i: (q, 0, i, 0)),
                  pl.BlockSpec((1, tr, W), lambda q, i: (q, i, 0))],
        out_specs=[pl.BlockSpec((1, tr, W), lambda q, i: (q, i, 0))],
        out_shape=[SDS((nq, R, W), g.dtype)],
        params=_params(2, [((4, tr, W), g.dtype)], temp_bytes=3 * _nbytes((tr, W), F32)))[0]


def _adamw_update(w, g, m, v):
    m = ADAM_B1 * m + (1.0 - ADAM_B1) * g
    v = ADAM_B2 * v + (1.0 - ADAM_B2) * (g * g)
    m_hat = m / (1.0 - ADAM_B1 ** ADAM_STEP)
    v_hat = v / (1.0 - ADAM_B2 ** ADAM_STEP)
    delta = -ADAM_LR * (m_hat / (jnp.sqrt(v_hat) + ADAM_EPS) + ADAM_WD * w)
    return delta, m, v


def _row_tile(R):
    return _pick(R, (256, 128, 112, 88, 64, 32, 16, 8))


def _sum8(parts, *, name):
    n, R, W = parts.shape
    tr = _row_tile(R)

    def body(p_ref, o_ref):
        o_ref[...] = _sum_parts(p_ref)

    return _call(
        body, (parts,), name=name, grid=(R // tr,),
        in_specs=[pl.BlockSpec((n, tr, W), lambda i: (0, i, 0))],
        out_specs=[pl.BlockSpec((tr, W), lambda i: (i, 0))],
        out_shape=[SDS((R, W), F32)],
        params=_params(1, [((n, tr, W), parts.dtype), ((tr, W), F32)]))[0]


def _adamw(g, w, m, v, *, name):
    R, W = w.shape
    tr = _row_tile(R)

    def body(g_ref, w_ref, m_ref, v_ref, d_ref, mo_ref, vo_ref):
        d_ref[...], mo_ref[...], vo_ref[...] = _adamw_update(w_ref[...], g_ref[...], m_ref[...], v_ref[...])

    spec = pl.BlockSpec((tr, W), lambda i: (i, 0))
    return _call(
        body, (g, w, m, v), name=name, grid=(R // tr,),
        in_specs=[spec] * 4, out_specs=[spec] * 3, out_shape=[SDS((R, W), F32)] * 3,
        params=_params(1, [((tr, W), F32)] * 7))


def _sum8_adamw(parts, w, m, v, *, name):
    R, W = w.shape
    n = parts.shape[0]
    tr = _row_tile(R)

    def body(p_ref, w_ref, m_ref, v_ref, g_ref, d_ref, mo_ref, vo_ref):
        g = _sum_parts(p_ref)
        g_ref[...] = g
        d_ref[...], mo_ref[...], vo_ref[...] = _adamw_update(w_ref[...], g, m_ref[...], v_ref[...])

    spec = pl.BlockSpec((tr, W), lambda i: (i, 0))
    return _call(
        body, (parts, w, m, v), name=name, grid=(R // tr,),
        in_specs=[pl.BlockSpec((n, tr, W), lambda i: (0, i, 0))] + [spec] * 3,
        out_specs=[spec] * 4, out_shape=[SDS((R, W), F32)] * 4,
        params=_params(1, [((n, tr, W), parts.dtype)] + [((tr, W), F32)] * 7))


def _ada_fwd(c_all, w, bias, *, name):
    NB, D = c_all.shape
    N = w.shape[1]

    def body(c_ref, w_ref, b_ref, o_ref):
        cv = c_ref[...]
        ca = (cv * _sigmoid(cv)).astype(BF16)
        o_ref[...] = _dot(ca, w_ref[...].astype(BF16)) + b_ref[...]

    full = lambda s: pl.BlockSpec(s, lambda i: (0,) * len(s))
    return _call(
        body, (c_all, w, bias), name=name, grid=(1,),
        in_specs=[full((NB, D)), full((D, N)), full((1, N))], out_specs=[full((NB, N))],
        out_shape=[SDS((NB, N), F32)],
        params=_params(1, [((D, N), F32)], temp_bytes=_nbytes((D, N), BF16)))[0]


def _ada_bwd(c_all, gmod_all, *, n_col, name):
    NB, D = c_all.shape
    N = gmod_all.shape[1]

    def body(c_ref, g_ref, gw_ref, gb_ref):
        cv = c_ref[...]
        ca = (cv * _sigmoid(cv)).astype(BF16)
        first = pl.multiple_of(_lin(_my_pos()) * n_col, 128)
        gw_ref[...] = _dot_tn(ca, g_ref[:, pl.ds(first, n_col)].astype(BF16))
        gb_ref[...] = _rowsum(g_ref[...])

    full = lambda s: pl.BlockSpec(s, lambda i: (0,) * len(s))
    return _call(
        body, (c_all, gmod_all), name=name, grid=(1,),
        in_specs=[full((NB, D)), full((NB, N))], out_specs=[full((D, n_col)), full((1, N))],
        out_shape=[SDS((D, n_col), F32), SDS((1, N), F32)],
        params=_params(1, [((D, n_col), F32), ((NB, N), F32)]))


def kernel(x, c, w_ada, b_ada, norm_ffn1_g, ffn1_w_gate, ffn1_w_up, ffn1_w_down, norm_mix_g, w_in, attn_sinks, w_attn_o, conv_w_dw, conv_b_dw, conv_ln_g, conv_ln_b, w_conv_o, w_out, norm_ffn2_g, ffn2_w_gate, ffn2_w_up, ffn2_w_down, final_norm_g, loss_target, m_w_ada, m_b_ada, m_norm_ffn1_g, m_ffn1_w_gate, m_ffn1_w_up, m_ffn1_w_down, m_norm_mix_g, m_w_in, m_attn_sinks, m_w_attn_o, m_conv_w_dw, m_conv_b_dw, m_conv_ln_g, m_conv_ln_b, m_w_conv_o, m_w_out, m_norm_ffn2_g, m_ffn2_w_gate, m_ffn2_w_up, m_ffn2_w_down, m_final_norm_g, v_w_ada, v_b_ada, v_norm_ffn1_g, v_ffn1_w_gate, v_ffn1_w_up, v_ffn1_w_down, v_norm_mix_g, v_w_in, v_attn_sinks, v_w_attn_o, v_conv_w_dw, v_conv_b_dw, v_conv_ln_g, v_conv_ln_b, v_w_conv_o, v_w_out, v_norm_ffn2_g, v_ffn2_w_gate, v_ffn2_w_up, v_ffn2_w_down, v_final_norm_g):
    B, S, D = x.shape
    T = B * S
    QW = N_Q_HEADS * HEAD_DIM
    CC = conv_w_dw.shape[2] * N_DEV
    me = _lin(_my_pos())
    xf = x.reshape(T, D)
    tgt = loss_target.reshape(T, D)
    tm = min(512, S)
    kw = dict(seq=S, tm=tm)

    o_k, o_ca, o_end = QW, QW + 2 * KV_WIDTH, QW + 2 * KV_WIDTH + 2 * CC + 2 * D
    p_ca, p_cb, p_ga, p_gc, p_k, p_v = QW, QW + CC, QW + 2 * CC, QW + 2 * CC + D, QW + 2 * CC + 2 * D, \
        QW + 2 * CC + 2 * D + KV_WIDTH

    def to_local_order(w):
        return jnp.concatenate([w[:o_k], w[o_ca:o_end], w[o_k:o_ca]], axis=0)

    def to_ref_order(w):
        return jnp.concatenate([w[:QW], w[p_k:], w[p_ca:p_k]], axis=0)

    def col_t(w):
        return w[0].T.astype(BF16)

    def row_b(w):
        return w[0].astype(BF16)

    def rows(g):
        return g.reshape(-1, g.shape[-1])

    def blocks8(g):
        return g.reshape(N_DEV, g.shape[0] // N_DEV, g.shape[1])

    def gather(*arrs):
        return _Comm([(a, "gather") for a in arrs])

    g_wg1, g_wu1, g_convw, g_c = _exchange(
        [(col_t(ffn1_w_gate), "gather"), (col_t(ffn1_w_up), "gather"), (conv_w_dw[0], "gather"), (c, "gather")],
        name="gather_first")
    wg1, wu1 = rows(g_wg1), rows(g_wu1)
    conv_w = g_convw.transpose(1, 0, 2).reshape(CONV_WIDTH, CC)
    c_all = g_c.reshape(N_DEV * B, D)

    n_col = N_MOD * D // N_DEV
    b_cols = lax.dynamic_slice(b_ada, (0, me * n_col), (1, n_col))
    mod_cols = _ada_fwd(c_all, w_ada[0], b_cols, name="ada_fwd")
    mod_mine = _exchange([(mod_cols.reshape(N_DEV, B, n_col), "scatter")], name="scatter_mod")[0]
    mod = mod_mine.transpose(1, 0, 2).reshape(B, N_MOD, 1, D)
    sh1, sc1, g1, sh2, sc2, g2, sh3, sc3, g3 = [mod[:, i] for i in range(N_MOD)]

    F = wg1.shape[0]
    tn_f = _pick(F, (1408, 1024, 512, 256))
    tn_in = _pick(w_in.shape[2] * N_DEV, (1792, 768, 512, 256))
    gate_blk = dict(ga_blk=p_ga // D, gc_blk=p_gc // D)
    att_blk = dict(q_blk=0, k_blk=p_k // KV_WIDTH, v_blk=p_v // KV_WIDTH)
    conv_kw = dict(seq=S, cw=256, a_col=p_ca, b_col=p_cb)

    cm = gather(row_b(ffn1_w_down), col_t(w_in))
    h1, (a1, b1) = _norm_mod_matmul(xf, norm_ffn1_g, sh1, sc1, [wg1, wu1], tn=tn_f, name="ffn1_up", comm=cm, **kw)
    wd1, winp = rows(cm.out[0]), to_local_order(rows(cm.out[1]))
    cm = gather(row_b(w_attn_o), row_b(w_conv_o), row_b(w_out))
    x1, y1 = _ffn_down(a1, b1, wd1, xf, g1, name="ffn1_down", comm=cm, **kw)
    wao, wco, wout = [rows(o) for o in cm.out]
    cm = gather(col_t(ffn2_w_gate), col_t(ffn2_w_up))
    h2, (projp,) = _norm_mod_matmul(x1, norm_mix_g, sh2, sc2, [winp], tn=tn_in, name="mix_in", comm=cm, **kw)
    wg2, wu2 = [rows(o) for o in cm.out]
    cm = gather(row_b(ffn2_w_down))
    ao = _attn_fwd(projp, attn_sinks, seq=S, name="attn_fwd", comm=cm, **att_blk)
    wd2 = rows(cm.out[0])
    yc = _conv_fwd(projp, conv_w, conv_b_dw, name="conv_fwd", **conv_kw)
    x2, z, ya, ycv, cact, merged = _mix_out(ao, yc, projp, wao, wco, wout, x1, g2, conv_ln_g, conv_ln_b,
                                            name="mix_out", **gate_blk, **kw)
    h3, (a3, b3) = _norm_mod_matmul(x2, norm_ffn2_g, sh3, sc3, [wg2, wu2], tn=tn_f, name="ffn2_up", **kw)
    x3, y3 = _ffn_down(a3, b3, wd2, x2, g3, name="ffn2_down", **kw)
    dx3, loss_row, dgf = _final_loss(x3, final_norm_g[None], tgt, tm=tm, name="final_loss")
    loss = lax.psum(loss_row[0, 0], ("x", "y", "c"))

    parts = {}

    def pair(*gs):
        return [(blocks8(g), "pair") for g in gs]

    def cross(*rs):
        return [(r, "cross") for r in rs]

    def reduce_pairs(gs, staged, names):
        return [_pair_add(blocks8(g), s, name="pair_add_" + n) for g, s, n in zip(gs, staged, names)]

    dyb3, da3, db3, act3, dg3 = _ffn_bwd_down(dx3, g3, y3, wd2, a3, b3, tn=tn_f, name="ffn2_bwd_down", **kw)
    gwd2 = _matmul_tn(act3, dyb3, name="gw_ffn2_down")
    cm = _Comm(pair(gwd2))
    dx2, dsh3, dsc3, dgn3 = _matmul_norm_mod_bwd([da3, db3], [wg2, wu2], x2, norm_ffn2_g, sc3, dx3,
                                                 name="ffn2_bwd_up", comm=cm, **kw)
    r_wd2, = reduce_pairs([gwd2], cm.out, ["ffn2_w_down"])
    cm = _Comm(cross(r_wd2))
    gwg2 = _matmul_tn(da3, h3, name="gw_ffn2_gate", comm=cm)
    parts["ffn2_w_down"], = cm.out
    cm = _Comm(pair(gwg2))
    gwu2 = _matmul_tn(db3, h3, name="gw_ffn2_up", comm=cm)
    r_wg2, = reduce_pairs([gwg2], cm.out, ["ffn2_w_gate"])

    cm = _Comm(cross(r_wg2) + pair(gwu2))
    dzb, dyab, dycb, dga, dgc, dao, dyc, dg2, dlng, dlnb = _mix_out_bwd(
        dx2, g2, z, wout, projp, ya, ycv, wao, wco, yc, conv_ln_g, conv_ln_b, name="mix_out_bwd", comm=cm,
        **gate_blk, **kw)
    parts["ffn2_w_gate"] = cm.out[0]
    r_wu2, = reduce_pairs([gwu2], cm.out[1:], ["ffn2_w_up"])
    gwout = _matmul_tn(merged, dzb, name="gw_out")
    gwao = _matmul_tn(ao, dyab, name="gw_attn_o")
    gwco = _matmul_tn(cact, dycb, name="gw_conv_o")
    cm = _Comm(cross(r_wu2) + pair(gwout, gwao, gwco))
    dq, dk, dv, dsinks = _attn_bwd(projp, dao, attn_sinks, seq=S, name="attn_bwd", comm=cm, **att_blk)
    parts["ffn2_w_up"] = cm.out[0]
    r_mix = reduce_pairs([gwout, gwao, gwco], cm.out[1:], ["w_out", "w_attn_o", "w_conv_o"])
    cm = _Comm(cross(*r_mix))
    dca, dcb, dconvw, dconvb = _conv_bwd(dyc, projp, conv_w, name="conv_bwd", comm=cm, **conv_kw)
    parts["w_out"], parts["w_attn_o"], parts["w_conv_o"] = cm.out
    dprojp = jnp.concatenate([dq, dca, dcb, dga, dgc, dk, dv], axis=1)
    gwin = to_ref_order(_matmul_tn(dprojp, h2, name="gw_in"))
    cm = _Comm(pair(gwin))
    dx1, dsh2, dsc2, dgn2 = _matmul_norm_mod_bwd([dprojp], [winp], x1, norm_mix_g, sc2, dx2,
                                                 name="mix_in_bwd", comm=cm, **kw)
    r_win, = reduce_pairs([gwin], cm.out, ["w_in"])

    cm = _Comm(cross(r_win))
    dyb1, da1, db1, act1, dg1 = _ffn_bwd_down(dx1, g1, y1, wd1, a1, b1, tn=tn_f, name="ffn1_bwd_down", comm=cm,
                                              **kw)
    parts["w_in"], = cm.out
    gwd1 = _matmul_tn(act1, dyb1, name="gw_ffn1_down")
    cm = _Comm(pair(gwd1))
    gwg1 = _matmul_tn(da1, h1, name="gw_ffn1_gate", comm=cm)
    r_wd1, = reduce_pairs([gwd1], cm.out, ["ffn1_w_down"])
    cm = _Comm(cross(r_wd1) + pair(gwg1))
    gwu1 = _matmul_tn(db1, h1, name="gw_ffn1_up", comm=cm)
    parts["ffn1_w_down"] = cm.out[0]
    r_wg1, = reduce_pairs([gwg1], cm.out[1:], ["ffn1_w_gate"])
    cm = _Comm(cross(r_wg1) + pair(gwu1))
    dx0, dsh1, dsc1, dgn1 = _matmul_norm_mod_bwd([da1, db1], [wg1, wu1], xf, norm_ffn1_g, sc1, dx1,
                                                 name="ffn1_bwd_up", comm=cm, **kw)
    parts["ffn1_w_gate"] = cm.out[0]
    r_wu1, = reduce_pairs([gwu1], cm.out[1:], ["ffn1_w_up"])

    n_small = 8
    gmod = jnp.concatenate([dsh1, dsc1, dg1, dsh2, dsc2, dg2, dsh3, dsc3, dg3], axis=1).reshape(B, N_MOD * D)
    sink_row = jnp.pad(dsinks[:, :N_Q_HEADS], ((0, 0), (0, D - N_Q_HEADS)))
    small = jnp.concatenate([dgn1, dgn2, dgn3, dgf, dconvb, dlng, dlnb, sink_row, dconvw,
                             jnp.zeros((1, D), F32)], axis=0)
    parts["ffn1_w_up"], small_all, gmod_all = _exchange(
        cross(r_wu1) + [(small, "gather"), (gmod, "gather")], name="exchange_last")
    gsmall = _sum8(small_all, name="sum_small")
    g_w_ada, g_b_ada = _ada_bwd(c_all, gmod_all.reshape(N_DEV * B, N_MOD * D), n_col=n_col, name="ada_bwd")
    g_conv_w = lax.dynamic_slice(gsmall[n_small:n_small + CONV_WIDTH], (0, me * (CC // N_DEV)),
                                 (CONV_WIDTH, CC // N_DEV))

    def col_update(name, w, m, v):
        outs = _sum8_adamw(parts[name], w[0].T, m[0].T, v[0].T, name="adamw_" + name)
        return tuple(o.T for o in outs)

    def row_update(name, w, m, v):
        return tuple(_sum8_adamw(parts[name], w[0], m[0], v[0], name="adamw_" + name))

    upd = {
        "ffn1_w_gate": col_update("ffn1_w_gate", ffn1_w_gate, m_ffn1_w_gate, v_ffn1_w_gate),
        "ffn1_w_up": col_update("ffn1_w_up", ffn1_w_up, m_ffn1_w_up, v_ffn1_w_up),
        "ffn1_w_down": row_update("ffn1_w_down", ffn1_w_down, m_ffn1_w_down, v_ffn1_w_down),
        "w_in": col_update("w_in", w_in, m_w_in, v_w_in),
        "w_attn_o": row_update("w_attn_o", w_attn_o, m_w_attn_o, v_w_attn_o),
        "w_conv_o": row_update("w_conv_o", w_conv_o, m_w_conv_o, v_w_conv_o),
        "w_out": row_update("w_out", w_out, m_w_out, v_w_out),
        "ffn2_w_gate": col_update("ffn2_w_gate", ffn2_w_gate, m_ffn2_w_gate, v_ffn2_w_gate),
        "ffn2_w_up": col_update("ffn2_w_up", ffn2_w_up, m_ffn2_w_up, v_ffn2_w_up),
        "ffn2_w_down": row_update("ffn2_w_down", ffn2_w_down, m_ffn2_w_down, v_ffn2_w_down),
        "w_ada": (g_w_ada,) + tuple(_adamw(g_w_ada, w_ada[0], m_w_ada[0], v_w_ada[0], name="adamw_w_ada")),
        "conv_w_dw": (g_conv_w,) + tuple(_adamw(g_conv_w, conv_w_dw[0], m_conv_w_dw[0], v_conv_w_dw[0],
                                                name="adamw_conv_w_dw")),
    }
    for k in upd:
        upd[k] = tuple(t[None] for t in upd[k])

    def pad_sinks(t):
        return jnp.pad(t, ((0, 0), (0, D - N_Q_HEADS)))

    def pack(f1, mix, f2, fin, cb, lg, lb, sinks, bada):
        return jnp.concatenate([f1, mix, f2, fin[None], cb, lg, lb, pad_sinks(sinks), bada.reshape(N_MOD, D)], axis=0)

    w_s = pack(norm_ffn1_g, norm_mix_g, norm_ffn2_g, final_norm_g, conv_b_dw, conv_ln_g, conv_ln_b, attn_sinks, b_ada)
    m_s = pack(m_norm_ffn1_g, m_norm_mix_g, m_norm_ffn2_g, m_final_norm_g, m_conv_b_dw, m_conv_ln_g, m_conv_ln_b,
               m_attn_sinks, m_b_ada)
    v_s = pack(v_norm_ffn1_g, v_norm_mix_g, v_norm_ffn2_g, v_final_norm_g, v_conv_b_dw, v_conv_ln_g, v_conv_ln_b,
               v_attn_sinks, v_b_ada)
    g_s = jnp.concatenate([gsmall[:n_small], g_b_ada.reshape(N_MOD, D)], axis=0)
    small_out = (g_s,) + tuple(_adamw(g_s, w_s, m_s, v_s, name="adamw_vectors"))

    def unpack(t):
        return {
            "norm_ffn1_g": t[0:1], "norm_mix_g": t[1:2], "norm_ffn2_g": t[2:3], "final_norm_g": t[3],
            "conv_b_dw": t[4:5], "conv_ln_g": t[5:6], "conv_ln_b": t[6:7], "attn_sinks": t[7:8, :N_Q_HEADS],
            "b_ada": t[n_small:n_small + N_MOD].reshape(1, N_MOD * D),
        }

    small_un = [unpack(t) for t in small_out]
    for k in small_un[0]:
        upd[k] = tuple(s[k] for s in small_un)

    order = ["w_ada", "b_ada", "norm_ffn1_g", "ffn1_w_gate", "ffn1_w_up", "ffn1_w_down", "norm_mix_g", "w_in",
             "attn_sinks", "w_attn_o", "conv_w_dw", "conv_b_dw", "conv_ln_g", "conv_ln_b", "w_conv_o", "w_out",
             "norm_ffn2_g", "ffn2_w_gate", "ffn2_w_up", "ffn2_w_down", "final_norm_g"]
    grad_x = dx0.reshape(B, S, D)
    return (loss, grad_x, *[upd[k][0] for k in order], *[upd[k][1] for k in order],
            *[upd[k][2] for k in order], *[upd[k][3] for k in order])
```

```python
import jax
import jax.numpy as jnp
from jax import lax
from jax.experimental import pallas as pl
from jax.experimental.pallas import tpu as pltpu

F32 = jnp.float32
BF16 = jnp.bfloat16
SDS = jax.ShapeDtypeStruct
MESH = pl.DeviceIdType.MESH

N_DEV = 8
EPS = 1e-6
HEAD_DIM = 64
N_Q_HEADS = 16
N_KV_HEADS = 2
GQA_GROUP = N_Q_HEADS // N_KV_HEADS
KV_WIDTH = N_KV_HEADS * HEAD_DIM
ATT_BLOCK = 128
CONV_WIDTH = 31
CONV_HALO = 32
CONV_ROWS = 64
N_MOD = 9
FFN_RESIDUAL = 0.5
ADAM_LR = 0.001
ADAM_B1 = 0.9
ADAM_B2 = 0.999
ADAM_EPS = 1e-08
ADAM_WD = 0.01
ADAM_STEP = 10
NEG_BIG = -1e30

V7X_VMEM_BYTES = 64 * 2**20
VMEM_CAP = V7X_VMEM_BYTES - 8 * 2**20


def _nbytes(shape, dtype):
    n = 1
    for s in shape:
        n *= s
    return n * jnp.dtype(dtype).itemsize


def _params(n_axes, blocks, temp_bytes=0):
    need = 2 * sum(_nbytes(s, d) for s, d in blocks) + temp_bytes + 4 * 2**20
    return pltpu.CompilerParams(dimension_semantics=("arbitrary",) * n_axes,
                                vmem_limit_bytes=int(min(max(need, 16 * 2**20), VMEM_CAP)))


def _dot_nt(a, b):
    return lax.dot_general(a, b, (((1,), (1,)), ((), ())), preferred_element_type=F32)


def _dot_tn(a, b):
    return lax.dot_general(a, b, (((0,), (0,)), ((), ())), preferred_element_type=F32)


def _dot(a, b):
    return jnp.dot(a, b, preferred_element_type=F32)


def _sigmoid(x):
    return jax.nn.sigmoid(x)


def _rowsum(v):
    return jnp.sum(v, axis=0, keepdims=True)


def _acc(ref, val, first):
    @pl.when(first)
    def _():
        ref[...] = val

    @pl.when(jnp.logical_not(first))
    def _():
        ref[...] = ref[...] + val


def _norm_mod(xf, gn, sh, sc):
    rstd = lax.rsqrt(jnp.mean(xf * xf, axis=-1, keepdims=True) + EPS)
    xhat = xf * rstd
    yn = xhat * gn
    return yn * (1.0 + sc) + sh, xhat, rstd, yn


def _pick(n, cands):
    for c in cands:
        if n % c == 0:
            return c
    return n


def _my_pos():
    return lax.axis_index("x"), lax.axis_index("y"), lax.axis_index("c")


def _peer(pos, k):
    x, y, c = pos
    return ((1 - x) if k & 4 else x, (1 - y) if k & 2 else y, (1 - c) if k & 1 else c)


def _lin(pos):
    return 4 * pos[0] + 2 * pos[1] + pos[2]


class _Comm:
    N_COPY = N_DEV - 1
    N_CHIP = N_DEV // 2

    def __init__(self, items):
        self.arrs = [a for a, _ in items]
        self.modes = [m for _, m in items]
        self.n = len(items)
        self.out = None

    def out_shape(self):
        def shape(a, m):
            return {"gather": (N_DEV,) + a.shape, "scatter": a.shape, "pair": (self.N_CHIP,) + a.shape[1:],
                    "cross": a.shape}[m]
        return [SDS(shape(a, m), a.dtype) for a, m in zip(self.arrs, self.modes)]

    def scratch(self):
        return [pltpu.SemaphoreType.DMA((self.n * self.N_COPY,)), pltpu.SemaphoreType.DMA((self.n * self.N_COPY,)),
                pltpu.SemaphoreType.DMA((self.n,))]

    def _plan(self, mode, me):
        x, y, c = me
        sib = (x, y, 1 - c)
        chips = [(1 - x, y), (x, 1 - y), (1 - x, 1 - y)]

        def chip_lin(ch):
            return 2 * ch[0] + ch[1]

        if mode == "scatter":
            peers = [_peer(me, k + 1) for k in range(self.N_COPY)]
            return [(p, ("in", _lin(p)), _lin(me), _lin(p), None) for p in peers], (_lin(me), _lin(me))
        if mode == "gather":
            same = [(*ch, c) for ch in chips]
            other = [(*ch, 1 - c) for ch in chips]
            copies = [(sib, ("in", None), _lin(me), _lin(sib), None)]
            copies += [(p, ("in", None), _lin(me), _lin(p), None) for p in same]
            copies += [(sib, ("out", _lin(p)), _lin(p), _lin(o), 1 + j) for j, (p, o) in enumerate(zip(same, other))]
            return copies, (None, _lin(me))
        if mode == "pair":
            return [(sib, ("in", 2 * q + 1 - c), q, q, None) for q in range(self.N_CHIP)], None
        if mode == "cross":
            mine = chip_lin((x, y))
            return ([((*ch, c), ("in", chip_lin(ch)), mine, chip_lin(ch), None) for ch in chips], (mine, mine))
        raise ValueError(mode)

    def _copy(self, refs, me, i, k, recv):
        srcs, outs, (send_sems, recv_sems, _) = refs
        peer, (where, slot), send_slot, recv_slot, _ = self._plan(self.modes[i], me)[0][k]
        src = srcs[i] if where == "in" else outs[i]
        src = src if slot is None else src.at[slot]
        sem = i * self.N_COPY + k
        return pltpu.make_async_remote_copy(
            src_ref=src, dst_ref=outs[i].at[recv_slot if recv else send_slot], send_sem=send_sems.at[sem],
            recv_sem=recv_sems.at[sem], device_id=peer, device_id_type=MESH)

    def _local(self, refs, me, i):
        srcs, outs, (_, _, loc_sems) = refs
        local = self._plan(self.modes[i], me)[1]
        if local is None:
            return None
        own = srcs[i] if local[0] is None else srcs[i].at[local[0]]
        return pltpu.make_async_copy(own, outs[i].at[local[1]], loc_sems.at[i])

    def start(self, refs):
        me = _my_pos()
        for i in range(self.n):
            local = self._local(refs, me, i)
            if local is not None:
                local.start()
            for k, cp in enumerate(self._plan(self.modes[i], me)[0]):
                if cp[4] is None:
                    self._copy(refs, me, i, k, False).start()

    def finish(self, refs):
        me = _my_pos()
        plans = [self._plan(m, me)[0] for m in self.modes]
        for i in range(self.n):
            for k, cp in enumerate(plans[i]):
                if cp[4] is not None:
                    self._copy(refs, me, i, cp[4], True).wait_recv()
                    self._copy(refs, me, i, k, False).start()
        for i in range(self.n):
            passed_on = [cp[4] for cp in plans[i] if cp[4] is not None]
            for k in range(len(plans[i])):
                if k not in passed_on:
                    self._copy(refs, me, i, k, True).wait_recv()
                self._copy(refs, me, i, k, False).wait_send()
            local = self._local(refs, me, i)
            if local is not None:
                local.wait()


_ANY = pl.BlockSpec(memory_space=pl.ANY)


def _call(body, args, *, name, grid, in_specs, out_specs, out_shape, params, scratch_shapes=(), comm=None):
    in_specs, out_specs, out_shape = list(in_specs), list(out_specs), list(out_shape)
    scratch_shapes = list(scratch_shapes)
    if comm is None:
        return list(pl.pallas_call(body, name=name, grid=grid, in_specs=in_specs, out_specs=out_specs,
                                   out_shape=out_shape, scratch_shapes=scratch_shapes, compiler_params=params)(*args))
    n_in, n_out, n_scr, nc = len(in_specs), len(out_specs), len(scratch_shapes), comm.n

    def hosted(*refs):
        ins, c_in = refs[:n_in], refs[n_in:n_in + nc]
        outs = refs[n_in + nc:n_in + nc + n_out]
        c_out = refs[n_in + nc + n_out:n_in + 2 * nc + n_out]
        scr = refs[n_in + 2 * nc + n_out:n_in + 2 * nc + n_out + n_scr]
        sems = refs[n_in + 2 * nc + n_out + n_scr:]
        first = pl.program_id(0) == 0
        last = pl.program_id(0) == grid[0] - 1
        for d in range(1, len(grid)):
            first = jnp.logical_and(first, pl.program_id(d) == 0)
            last = jnp.logical_and(last, pl.program_id(d) == grid[d] - 1)

        @pl.when(first)
        def _():
            comm.start((c_in, c_out, sems))

        body(*ins, *outs, *scr)

        @pl.when(last)
        def _():
            comm.finish((c_in, c_out, sems))

    res = pl.pallas_call(
        hosted, name=name, grid=grid, in_specs=in_specs + [_ANY] * nc, out_specs=out_specs + [_ANY] * nc,
        out_shape=out_shape + comm.out_shape(), scratch_shapes=scratch_shapes + comm.scratch(),
        compiler_params=params)(*args, *comm.arrs)
    comm.out = list(res[n_out:])
    return list(res[:n_out])


def _exchange(items, *, name):
    comm = _Comm(items)

    def body(*refs):
        r = (refs[:comm.n], refs[comm.n:2 * comm.n], refs[2 * comm.n:])
        comm.start(r)
        comm.finish(r)

    return list(pl.pallas_call(body, name=name, out_shape=comm.out_shape(), in_specs=[_ANY] * comm.n,
                               out_specs=[_ANY] * comm.n, scratch_shapes=comm.scratch())(*comm.arrs))


def _norm_mod_matmul(x, gn, sh, sc, wts, *, seq, tm, tn, name, comm=None):
    T, D = x.shape
    N = wts[0].shape[0]
    nw = len(wts)
    tps = seq // tm

    def body(x_ref, gn_ref, sh_ref, sc_ref, *rest):
        w_refs, h_ref, o_refs = rest[:nw], rest[nw], rest[nw + 1:]

        @pl.when(pl.program_id(1) == 0)
        def _():
            h_ref[...] = _norm_mod(x_ref[...], gn_ref[...], sh_ref[0], sc_ref[0])[0].astype(BF16)

        h = h_ref[...]
        for w_ref, o_ref in zip(w_refs, o_refs):
            o_ref[...] = _dot_nt(h, w_ref[...]).astype(o_ref.dtype)

    row = pl.BlockSpec((tm, D), lambda i, j: (i, 0))
    vec = pl.BlockSpec((1, D), lambda i, j: (0, 0))
    per_b = pl.BlockSpec((1, 1, D), lambda i, j: (i // tps, 0, 0))
    wspec = pl.BlockSpec((tn, D), lambda i, j: (j, 0))
    ospec = pl.BlockSpec((tm, tn), lambda i, j: (i, j))
    blocks = [((tm, D), F32), ((tm, D), BF16)] + [((tn, D), BF16), ((tm, tn), BF16)] * nw
    outs = _call(
        body, (x, gn, sh, sc, *wts), name=name, grid=(T // tm, N // tn),
        in_specs=[row, vec, per_b, per_b] + [wspec] * nw,
        out_specs=[row] + [ospec] * nw,
        out_shape=[SDS((T, D), BF16)] + [SDS((T, N), BF16)] * nw,
        params=_params(2, blocks, temp_bytes=2 * _nbytes((tm, tn), F32) + 3 * _nbytes((tm, D), F32)), comm=comm)
    return outs[0], outs[1:]


def _matmul_nt(h, w, *, tm, tn, name, comm=None):
    T, D = h.shape
    N = w.shape[0]

    def body(h_ref, w_ref, o_ref):
        o_ref[...] = _dot_nt(h_ref[...], w_ref[...]).astype(o_ref.dtype)

    blocks = [((tm, D), BF16), ((tn, D), BF16), ((tm, tn), BF16)]
    return _call(
        body, (h, w), name=name, grid=(T // tm, N // tn),
        in_specs=[pl.BlockSpec((tm, D), lambda i, j: (i, 0)), pl.BlockSpec((tn, D), lambda i, j: (j, 0))],
        out_specs=[pl.BlockSpec((tm, tn), lambda i, j: (i, j))],
        out_shape=[SDS((T, N), BF16)],
        params=_params(2, blocks, temp_bytes=2 * _nbytes((tm, tn), F32)), comm=comm)[0]


def _ffn_down(a, b, wd, x, g, *, seq, tm, name, comm=None):
    T, F = a.shape
    D = wd.shape[1]
    tps = seq // tm

    def body(a_ref, b_ref, wd_ref, x_ref, g_ref, xo_ref, y_ref):
        af = a_ref[...].astype(F32)
        act = (af * _sigmoid(af) * b_ref[...].astype(F32)).astype(BF16)
        y = _dot(act, wd_ref[...])
        xo_ref[...] = x_ref[...] + (FFN_RESIDUAL * g_ref[0]) * y
        y_ref[...] = y.astype(BF16)

    wide = pl.BlockSpec((tm, F), lambda i: (i, 0))
    row = pl.BlockSpec((tm, D), lambda i: (i, 0))
    per_b = pl.BlockSpec((1, 1, D), lambda i: (i // tps, 0, 0))
    wspec = pl.BlockSpec((F, D), lambda i: (0, 0))
    blocks = [((tm, F), BF16)] * 2 + [((F, D), BF16), ((tm, D), F32), ((tm, D), F32), ((tm, D), BF16)]
    return _call(
        body, (a, b, wd, x, g), name=name, grid=(T // tm,),
        in_specs=[wide, wide, wspec, row, per_b], out_specs=[row, row],
        out_shape=[SDS((T, D), F32), SDS((T, D), BF16)],
        params=_params(1, blocks, temp_bytes=3 * _nbytes((tm, F), F32)), comm=comm)


def _final_loss(x, gf, tgt, *, tm, name):
    T, D = x.shape
    nt = T // tm

    def body(x_ref, gf_ref, t_ref, dx_ref, loss_ref, dgf_ref, lacc):
        i = pl.program_id(0)
        xf = x_ref[...]
        gfv = gf_ref[...]
        rstd = lax.rsqrt(jnp.mean(xf * xf, axis=-1, keepdims=True) + EPS)
        xhat = xf * rstd
        err = xhat * gfv - t_ref[...]
        dy = err * (1.0 / D)
        dxhat = dy * gfv
        dx_ref[...] = rstd * (dxhat - xhat * jnp.mean(dxhat * xhat, axis=-1, keepdims=True))
        _acc(dgf_ref, _rowsum(dy * xhat), i == 0)
        _acc(lacc, _rowsum(err * err), i == 0)

        @pl.when(i == nt - 1)
        def _():
            loss_ref[...] = jnp.broadcast_to((0.5 / D) * jnp.sum(lacc[...]), loss_ref.shape)

    row = pl.BlockSpec((tm, D), lambda i: (i, 0))
    vec = pl.BlockSpec((1, D), lambda i: (0, 0))
    lspec = pl.BlockSpec((1, 128), lambda i: (0, 0))
    blocks = [((tm, D), F32)] * 3
    return _call(
        body, (x, gf, tgt), name=name, grid=(nt,),
        in_specs=[row, vec, row], out_specs=[row, lspec, vec],
        out_shape=[SDS((T, D), F32), SDS((1, 128), F32), SDS((1, D), F32)],
        scratch_shapes=[pltpu.VMEM((1, D), F32)],
        params=_params(1, blocks, temp_bytes=4 * _nbytes((tm, D), F32)))


def _ffn_bwd_down(dxo, g, y, wd, a, b, *, seq, tm, tn, name, comm=None):
    T, F = a.shape
    D = wd.shape[1]
    tps = seq // tm
    nb = T // seq

    def body(dxo_ref, g_ref, y_ref, wd_ref, a_ref, b_ref, dyb_ref, da_ref, db_ref, act_ref, dg_ref):
        i = pl.program_id(0)

        @pl.when(pl.program_id(1) == 0)
        def _():
            dx = dxo_ref[...]
            dyb_ref[...] = ((FFN_RESIDUAL * g_ref[0]) * dx).astype(BF16)
            part = _rowsum(FFN_RESIDUAL * dx * y_ref[...].astype(F32))
            _acc(dg_ref, part[None], i % tps == 0)

        dact = _dot_nt(dyb_ref[...], wd_ref[...])
        af = a_ref[...].astype(F32)
        bf = b_ref[...].astype(F32)
        sg = _sigmoid(af)
        silu = af * sg
        act_ref[...] = (silu * bf).astype(BF16)
        da_ref[...] = (dact * bf * (sg * (1.0 + af * (1.0 - sg)))).astype(BF16)
        db_ref[...] = (dact * silu).astype(BF16)

    row = pl.BlockSpec((tm, D), lambda i, j: (i, 0))
    per_b = pl.BlockSpec((1, 1, D), lambda i, j: (i // tps, 0, 0))
    wspec = pl.BlockSpec((tn, D), lambda i, j: (j, 0))
    chunk = pl.BlockSpec((tm, tn), lambda i, j: (i, j))
    blocks = [((tm, D), F32), ((tm, D), BF16), ((tn, D), BF16), ((tm, D), BF16)] + [((tm, tn), BF16)] * 5
    return _call(
        body, (dxo, g, y, wd, a, b), name=name, grid=(T // tm, F // tn),
        in_specs=[row, per_b, row, wspec, chunk, chunk],
        out_specs=[row, chunk, chunk, chunk, per_b],
        out_shape=[SDS((T, D), BF16)] + [SDS((T, F), BF16)] * 3 + [SDS((nb, 1, D), F32)],
        params=_params(2, blocks, temp_bytes=6 * _nbytes((tm, tn), F32)), comm=comm)


def _matmul_norm_mod_bwd(ds, ws, x, gn, sc, dxo, *, seq, tm, name, comm=None):
    T, D = x.shape
    nk = len(ds)
    tps = seq // tm
    nb = T // seq

    def body(*refs):
        d_refs, w_refs = refs[:nk], refs[nk:2 * nk]
        x_ref, gn_ref, sc_ref, dxo_ref, dxi_ref, dsh_ref, dsc_ref, dgn_ref = refs[2 * nk:]
        i = pl.program_id(0)
        dh = _dot(d_refs[0][...], w_refs[0][...])
        for d_ref, w_ref in zip(d_refs[1:], w_refs[1:]):
            dh = dh + _dot(d_ref[...], w_ref[...])
        gnv = gn_ref[...]
        scv = sc_ref[0]
        _, xhat, rstd, yn = _norm_mod(x_ref[...], gnv, 0.0, scv)
        dyn = dh * (1.0 + scv)
        dxhat = dyn * gnv
        dxi_ref[...] = dxo_ref[...] + rstd * (dxhat - xhat * jnp.mean(dxhat * xhat, axis=-1, keepdims=True))
        first_of_seq = i % tps == 0
        _acc(dsh_ref, _rowsum(dh)[None], first_of_seq)
        _acc(dsc_ref, _rowsum(dh * yn)[None], first_of_seq)
        _acc(dgn_ref, _rowsum(dyn * xhat), i == 0)

    row = pl.BlockSpec((tm, D), lambda i: (i, 0))
    vec = pl.BlockSpec((1, D), lambda i: (0, 0))
    per_b = pl.BlockSpec((1, 1, D), lambda i: (i // tps, 0, 0))
    d_specs = [pl.BlockSpec((tm, d.shape[1]), lambda i: (i, 0)) for d in ds]
    w_specs = [pl.BlockSpec(w.shape, lambda i: (0, 0)) for w in ws]
    blocks = ([((tm, d.shape[1]), BF16) for d in ds] + [(w.shape, BF16) for w in ws] + [((tm, D), F32)] * 3)
    return _call(
        body, (*ds, *ws, x, gn, sc, dxo), name=name, grid=(T // tm,),
        in_specs=d_specs + w_specs + [row, vec, per_b, row],
        out_specs=[row, per_b, per_b, vec],
        out_shape=[SDS((T, D), F32), SDS((nb, 1, D), F32), SDS((nb, 1, D), F32), SDS((1, D), F32)],
        params=_params(1, blocks, temp_bytes=6 * _nbytes((tm, D), F32)), comm=comm)


def _layernorm_silu(yc, lg, lb):
    mu = jnp.mean(yc, axis=-1, keepdims=True)
    cen = yc - mu
    rstd = lax.rsqrt(jnp.mean(cen * cen, axis=-1, keepdims=True) + EPS)
    xh = cen * rstd
    l = xh * lg + lb
    s = _sigmoid(l)
    return l * s, xh, rstd, l, s


GATE_W = 256


def _gate_specs(tm, D, col):
    return [pl.BlockSpec((tm, GATE_W), lambda i, blk=col // GATE_W + t: (i, blk)) for t in range(D // GATE_W)]


def _gate(refs):
    return jnp.concatenate([r[...] for r in refs], axis=1).astype(F32)


def _mix_out(ao, yc, proj, wao, wco, wout, x1, g2, lg, lb, *, seq, tm, ga_col, gc_col, name, comm=None):
    T, D = x1.shape
    tps = seq // tm
    ng = D // GATE_W

    def body(ao_ref, yc_ref, *rest):
        ga_refs, gc_refs = rest[:ng], rest[ng:2 * ng]
        (wao_ref, wco_ref, wout_ref, x1_ref, g2_ref, lg_ref, lb_ref,
         x2_ref, z_ref, ya_ref, ycv_ref, cact_ref, mrg_ref) = rest[2 * ng:]
        ya = _dot(ao_ref[...], wao_ref[...])
        cact = _layernorm_silu(yc_ref[...], lg_ref[...], lb_ref[...])[0].astype(BF16)
        ycv = _dot(cact, wco_ref[...])
        merged = (_sigmoid(_gate(ga_refs)) * ya + _sigmoid(_gate(gc_refs)) * ycv).astype(BF16)
        z = _dot(merged, wout_ref[...])
        x2_ref[...] = x1_ref[...] + g2_ref[0] * z
        z_ref[...] = z.astype(BF16)
        ya_ref[...] = ya.astype(BF16)
        ycv_ref[...] = ycv.astype(BF16)
        cact_ref[...] = cact
        mrg_ref[...] = merged

    row = pl.BlockSpec((tm, D), lambda i: (i, 0))
    vec = pl.BlockSpec((1, D), lambda i: (0, 0))
    per_b = pl.BlockSpec((1, 1, D), lambda i: (i // tps, 0, 0))
    wspec = pl.BlockSpec((D, D), lambda i: (0, 0))
    gates = _gate_specs(tm, D, ga_col) + _gate_specs(tm, D, gc_col)
    blocks = ([((tm, D), BF16), ((tm, D), F32), ((tm, D), BF16), ((tm, D), BF16)] + [((D, D), BF16)] * 3
              + [((tm, D), F32)] * 2 + [((tm, D), BF16)] * 5)
    return _call(
        body, (ao, yc, *[proj] * (2 * ng), wao, wco, wout, x1, g2, lg, lb), name=name, grid=(T // tm,),
        in_specs=[row, row, *gates, wspec, wspec, wspec, row, per_b, vec, vec],
        out_specs=[row] * 6,
        out_shape=[SDS((T, D), F32)] + [SDS((T, D), BF16)] * 5,
        params=_params(1, blocks, temp_bytes=8 * _nbytes((tm, D), F32)), comm=comm)


def _mix_out_bwd(dx2, g2, z, wout, proj, ya, ycv, wao, wco, yc, lg, lb, *, seq, tm, ga_col, gc_col, name,
                 comm=None):
    T, D = dx2.shape
    tps = seq // tm
    nb = T // seq
    ng = D // GATE_W

    def body(dx2_ref, g2_ref, z_ref, wout_ref, *rest):
        ga_refs, gc_refs = rest[:ng], rest[ng:2 * ng]
        (ya_ref, ycv_ref, wao_ref, wco_ref, yc_ref, lg_ref, lb_ref, dz_ref, dya_ref, dycv_ref, dga_ref, dgc_ref,
         dao_ref, dyc_ref, dg2_ref, dlg_ref, dlb_ref) = rest[2 * ng:]
        i = pl.program_id(0)
        dx = dx2_ref[...]
        _acc(dg2_ref, _rowsum(dx * z_ref[...].astype(F32))[None], i % tps == 0)
        dzb = (g2_ref[0] * dx).astype(BF16)
        dz_ref[...] = dzb
        dmerged = _dot_nt(dzb, wout_ref[...])
        sa = _sigmoid(_gate(ga_refs))
        sc_ = _sigmoid(_gate(gc_refs))
        dya = (dmerged * sa).astype(BF16)
        dycv = (dmerged * sc_).astype(BF16)
        dya_ref[...] = dya
        dycv_ref[...] = dycv
        dga_ref[...] = (dmerged * ya_ref[...].astype(F32) * (sa * (1.0 - sa))).astype(BF16)
        dgc_ref[...] = (dmerged * ycv_ref[...].astype(F32) * (sc_ * (1.0 - sc_))).astype(BF16)
        dao_ref[...] = _dot_nt(dya, wao_ref[...]).astype(BF16)
        dcact = _dot_nt(dycv, wco_ref[...])
        lgv = lg_ref[...]
        _, xh, rstd, l, s = _layernorm_silu(yc_ref[...], lgv, lb_ref[...])
        dl = dcact * (s * (1.0 + l * (1.0 - s)))
        _acc(dlb_ref, _rowsum(dl), i == 0)
        _acc(dlg_ref, _rowsum(dl * xh), i == 0)
        dxh = dl * lgv
        dyc_ref[...] = rstd * (dxh - jnp.mean(dxh, axis=-1, keepdims=True)
                               - xh * jnp.mean(dxh * xh, axis=-1, keepdims=True))

    row = pl.BlockSpec((tm, D), lambda i: (i, 0))
    vec = pl.BlockSpec((1, D), lambda i: (0, 0))
    per_b = pl.BlockSpec((1, 1, D), lambda i: (i // tps, 0, 0))
    wspec = pl.BlockSpec((D, D), lambda i: (0, 0))
    gates = _gate_specs(tm, D, ga_col) + _gate_specs(tm, D, gc_col)
    blocks = ([((tm, D), F32)] * 3 + [((tm, D), BF16)] * 11 + [((D, D), BF16)] * 3)
    return _call(
        body, (dx2, g2, z, wout, *[proj] * (2 * ng), ya, ycv, wao, wco, yc, lg, lb), name=name, grid=(T // tm,),
        in_specs=[row, per_b, row, wspec, *gates, row, row, wspec, wspec, row, vec, vec],
        out_specs=[row] * 7 + [per_b, vec, vec],
        out_shape=[SDS((T, D), BF16)] * 6 + [SDS((T, D), F32), SDS((nb, 1, D), F32), SDS((1, D), F32),
                                             SDS((1, D), F32)],
        params=_params(1, blocks, temp_bytes=10 * _nbytes((tm, D), F32)), comm=comm)


GROUP_ROWS = GQA_GROUP * ATT_BLOCK
PAIR_W = 2 * HEAD_DIM
GROUP_W = GQA_GROUP * HEAD_DIM


def _lane_lo():
    return lax.broadcasted_iota(jnp.int32, (1, PAIR_W), 1) < HEAD_DIM


def _band_bias():
    sj = lax.broadcasted_iota(jnp.int32, (2 * ATT_BLOCK, GROUP_ROWS), 0)
    qi = lax.broadcasted_iota(jnp.int32, (2 * ATT_BLOCK, GROUP_ROWS), 1) & (ATT_BLOCK - 1)
    rel = qi + ATT_BLOCK - sj
    bias = jnp.where(jnp.logical_and(rel >= 0, rel < ATT_BLOCK), 0.0, NEG_BIG)
    sj1 = lax.broadcasted_iota(jnp.int32, (2 * ATT_BLOCK, 1), 0)
    return bias, jnp.where(sj1 < ATT_BLOCK, NEG_BIG, 0.0)


def _dup_heads(src_ref, dst, seq):
    x = src_ref[...]
    i = lax.broadcasted_iota(jnp.int32, (KV_WIDTH, PAIR_W), 0)
    j = lax.broadcasted_iota(jnp.int32, (KV_WIDTH, PAIR_W), 1) & (HEAD_DIM - 1)
    for g in range(N_KV_HEADS):
        sel = jnp.where(i == j + g * HEAD_DIM, 1.0, 0.0).astype(BF16)
        dst[g, pl.ds(0, ATT_BLOCK), :] = jnp.zeros((ATT_BLOCK, PAIR_W), BF16)
        dst[g, pl.ds(ATT_BLOCK, seq), :] = _dot(x, sel).astype(BF16)


def _stack_heads(blk, g, lo):
    parts = []
    for p in range(GQA_GROUP // 2):
        pair = blk[:, g * GROUP_W + p * PAIR_W:g * GROUP_W + (p + 1) * PAIR_W]
        parts += [jnp.where(lo, pair, jnp.zeros_like(pair)), jnp.where(lo, jnp.zeros_like(pair), pair)]
    return jnp.concatenate(parts, axis=0)


def _unstack_heads(full, ref, r0, g, lo):
    for p in range(GQA_GROUP // 2):
        even = full[(2 * p) * ATT_BLOCK:(2 * p + 1) * ATT_BLOCK, :]
        odd = full[(2 * p + 1) * ATT_BLOCK:(2 * p + 2) * ATT_BLOCK, :]
        ref[pl.ds(r0, ATT_BLOCK), g * GROUP_W + p * PAIR_W:g * GROUP_W + (p + 1) * PAIR_W] = (
            jnp.where(lo, even, odd).astype(ref.dtype))


def _sink_row(sink_ref, g):
    return jnp.concatenate([jnp.full((1, ATT_BLOCK), sink_ref[0, g * GQA_GROUP + h], F32)
                            for h in range(GQA_GROUP)], axis=1)


def _group_probs(qs, k2, bias, sink):
    s = _dot_nt(k2, qs) * (HEAD_DIM ** -0.5) + bias
    m = jnp.maximum(jnp.max(s, axis=0, keepdims=True), sink)
    p = jnp.exp(s - m)
    psink = jnp.exp(sink - m)
    inv = 1.0 / (jnp.sum(p, axis=0, keepdims=True) + psink)
    return p * inv, psink * inv


def _attn_fwd(projp, sinks, *, seq, q_blk, k_blk, v_blk, name, comm=None):
    T = projp.shape[0]
    QW = N_Q_HEADS * HEAD_DIM
    nblk = seq // ATT_BLOCK

    def body(q_ref, k_ref, v_ref, sink_ref, o_ref, k2s, v2s):
        _dup_heads(k_ref, k2s, seq)
        _dup_heads(v_ref, v2s, seq)
        lo = _lane_lo()
        bias0, first_pen = _band_bias()
        sink_rows = [_sink_row(sink_ref, g) for g in range(N_KV_HEADS)]

        def blk(n, carry):
            r0 = pl.multiple_of(n * ATT_BLOCK, ATT_BLOCK)
            qb = q_ref[pl.ds(r0, ATT_BLOCK), :]
            bias = bias0 + jnp.where(n == 0, 1.0, 0.0) * first_pen
            for g in range(N_KV_HEADS):
                probs_t, _ = _group_probs(_stack_heads(qb, g, lo), k2s[g, pl.ds(r0, 2 * ATT_BLOCK), :], bias,
                                          sink_rows[g])
                _unstack_heads(_dot_tn(probs_t.astype(BF16), v2s[g, pl.ds(r0, 2 * ATT_BLOCK), :]), o_ref, r0, g, lo)
            return carry

        lax.fori_loop(0, nblk, blk, 0)

    blocks = [((seq, QW), BF16)] * 2 + [((seq, KV_WIDTH), BF16)] * 2
    return _call(
        body, (projp, projp, projp, sinks), name=name, grid=(T // seq,),
        in_specs=[pl.BlockSpec((seq, QW), lambda b: (b, q_blk)),
                  pl.BlockSpec((seq, KV_WIDTH), lambda b: (b, k_blk)),
                  pl.BlockSpec((seq, KV_WIDTH), lambda b: (b, v_blk)),
                  pl.BlockSpec(memory_space=pltpu.SMEM)],
        out_specs=[pl.BlockSpec((seq, QW), lambda b: (b, 0))],
        out_shape=[SDS((T, QW), BF16)],
        scratch_shapes=[pltpu.VMEM((N_KV_HEADS, seq + ATT_BLOCK, PAIR_W), BF16)] * 2,
        params=_params(1, blocks, temp_bytes=16 * 2**20), comm=comm)[0]


def _attn_bwd(projp, dao, sinks, *, seq, q_blk, k_blk, v_blk, name, comm=None):
    T = projp.shape[0]
    QW = N_Q_HEADS * HEAD_DIM
    nblk = seq // ATT_BLOCK

    def body(q_ref, k_ref, v_ref, do_ref, sink_ref, dq_ref, dk_ref, dv_ref, dsink_ref, k2s, v2s, dkacc, dvacc):
        _dup_heads(k_ref, k2s, seq)
        _dup_heads(v_ref, v2s, seq)
        dkacc[...] = jnp.zeros(dkacc.shape, F32)
        dvacc[...] = jnp.zeros(dvacc.shape, F32)
        lane = lax.broadcasted_iota(jnp.int32, (1, PAIR_W), 1)
        lo = lane < HEAD_DIM
        bias0, first_pen = _band_bias()
        sink_rows = [_sink_row(sink_ref, g) for g in range(N_KV_HEADS)]

        def blk(n, dsink):
            r0 = pl.multiple_of(n * ATT_BLOCK, ATT_BLOCK)
            band = pl.ds(r0, 2 * ATT_BLOCK)
            qb = q_ref[pl.ds(r0, ATT_BLOCK), :]
            dob = do_ref[pl.ds(r0, ATT_BLOCK), :]
            bias = bias0 + jnp.where(n == 0, 1.0, 0.0) * first_pen
            for g in range(N_KV_HEADS):
                qs = _stack_heads(qb, g, lo)
                dos = _stack_heads(dob, g, lo)
                k2 = k2s[g, band, :]
                v2 = v2s[g, band, :]
                probs_t, psink = _group_probs(qs, k2, bias, sink_rows[g])
                dp_t = _dot_nt(v2, dos)
                delta = jnp.sum(probs_t * dp_t, axis=0, keepdims=True)
                ds_t = (probs_t * (dp_t - delta) * (HEAD_DIM ** -0.5)).astype(BF16)
                tsink = psink * delta
                for h in range(GQA_GROUP):
                    dsink = dsink + jnp.where(lane == g * GQA_GROUP + h,
                                              -jnp.sum(tsink[:, h * ATT_BLOCK:(h + 1) * ATT_BLOCK]), 0.0)
                _unstack_heads(_dot_tn(ds_t, k2), dq_ref, r0, g, lo)
                dkacc[g, band, :] = dkacc[g, band, :] + _dot(ds_t, qs)
                dvacc[g, band, :] = dvacc[g, band, :] + _dot(probs_t.astype(BF16), dos)
            return dsink

        dsink = lax.fori_loop(0, nblk, blk, jnp.zeros((1, PAIR_W), F32))
        _acc(dsink_ref, dsink, pl.program_id(0) == 0)

        def fold(acc, g):
            a = acc[g, pl.ds(ATT_BLOCK, seq), :]
            return a + pltpu.roll(a, HEAD_DIM, 1)

        dk_ref[...] = jnp.where(lo, fold(dkacc, 0), fold(dkacc, 1)).astype(BF16)
        dv_ref[...] = jnp.where(lo, fold(dvacc, 0), fold(dvacc, 1)).astype(BF16)

    blocks = [((seq, QW), BF16)] * 3 + [((seq, KV_WIDTH), BF16)] * 4
    kv_spec_out = pl.BlockSpec((seq, KV_WIDTH), lambda b: (b, 0))
    return _call(
        body, (projp, projp, projp, dao, sinks), name=name, grid=(T // seq,),
        in_specs=[pl.BlockSpec((seq, QW), lambda b: (b, q_blk)),
                  pl.BlockSpec((seq, KV_WIDTH), lambda b: (b, k_blk)),
                  pl.BlockSpec((seq, KV_WIDTH), lambda b: (b, v_blk)),
                  pl.BlockSpec((seq, QW), lambda b: (b, 0)),
                  pl.BlockSpec(memory_space=pltpu.SMEM)],
        out_specs=[pl.BlockSpec((seq, QW), lambda b: (b, 0)), kv_spec_out, kv_spec_out,
                   pl.BlockSpec((1, 128), lambda b: (0, 0))],
        out_shape=[SDS((T, QW), BF16), SDS((T, KV_WIDTH), BF16), SDS((T, KV_WIDTH), BF16), SDS((1, 128), F32)],
        scratch_shapes=[pltpu.VMEM((N_KV_HEADS, seq + ATT_BLOCK, PAIR_W), BF16)] * 2
        + [pltpu.VMEM((N_KV_HEADS, seq + ATT_BLOCK, PAIR_W), F32)] * 2,
        params=_params(1, blocks, temp_bytes=24 * 2**20), comm=comm)


SUBLANES = 8


def _sublane_shifts(win):
    n = CONV_ROWS + CONV_HALO
    return [win] + [pltpu.roll(win, n - b, 0) for b in range(1, SUBLANES)]


def _window(shifted, off):
    a = off // SUBLANES * SUBLANES
    return shifted[off % SUBLANES][a:a + CONV_ROWS, :]


def _conv_fwd(projp, w, bias, *, seq, cw, a_col, b_col, name, comm=None):
    T = projp.shape[0]
    C = w.shape[1]
    nchunk = seq // CONV_ROWS

    def body(a_ref, b_ref, w_ref, bias_ref, y_ref, upad):
        upad[pl.ds(0, CONV_HALO), :] = jnp.zeros((CONV_HALO, cw), F32)
        upad[pl.ds(CONV_HALO, seq), :] = a_ref[...].astype(F32) * _sigmoid(b_ref[...].astype(F32))
        wv = w_ref[...]
        bv = bias_ref[...]

        def chunk(r, carry):
            r0 = pl.multiple_of(r * CONV_ROWS, CONV_ROWS)
            shifted = _sublane_shifts(upad[pl.ds(r0, CONV_ROWS + CONV_HALO), :])
            acc = jnp.broadcast_to(bv, (CONV_ROWS, cw))
            for k in range(CONV_WIDTH):
                acc = acc + wv[k:k + 1, :] * _window(shifted, CONV_HALO - (CONV_WIDTH - 1) + k)
            y_ref[pl.ds(r0, CONV_ROWS), :] = acc
            return carry

        lax.fori_loop(0, nchunk, chunk, 0)

    blocks = [((seq, cw), BF16)] * 2 + [((seq, cw), F32)]
    return _call(
        body, (projp, projp, w, bias), name=name, grid=(T // seq, C // cw),
        in_specs=[pl.BlockSpec((seq, cw), lambda b, c: (b, a_col // cw + c)),
                  pl.BlockSpec((seq, cw), lambda b, c: (b, b_col // cw + c)),
                  pl.BlockSpec((CONV_WIDTH, cw), lambda b, c: (0, c)),
                  pl.BlockSpec((1, cw), lambda b, c: (0, c))],
        out_specs=[pl.BlockSpec((seq, cw), lambda b, c: (b, c))],
        out_shape=[SDS((T, C), F32)],
        scratch_shapes=[pltpu.VMEM((seq + CONV_HALO, cw), F32)],
        params=_params(2, blocks, temp_bytes=6 * _nbytes((seq, cw), F32)), comm=comm)[0]


def _conv_bwd(dy, projp, w, *, seq, cw, a_col, b_col, name, comm=None):
    T = projp.shape[0]
    C = w.shape[1]
    nchunk = seq // CONV_ROWS
    SUB = 8

    def body(dy_ref, a_ref, b_ref, w_ref, da_ref, db_ref, dw_ref, dbias_ref, dypad, dwp):
        first = pl.program_id(1) == 0
        dyv = dy_ref[...]
        dypad[pl.ds(0, seq), :] = dyv
        dypad[pl.ds(seq, CONV_HALO), :] = jnp.zeros((CONV_HALO, cw), F32)
        dwp[...] = jnp.zeros(dwp.shape, F32)
        wv = w_ref[...]

        def chunk(r, carry):
            r0 = pl.multiple_of(r * CONV_ROWS, CONV_ROWS)
            dy_shifts = _sublane_shifts(dypad[pl.ds(r0, CONV_ROWS + CONV_HALO), :])
            ac = a_ref[pl.ds(r0, CONV_ROWS), :].astype(F32)
            sbc = _sigmoid(b_ref[pl.ds(r0, CONV_ROWS), :].astype(F32))
            uc = ac * sbc
            du = jnp.zeros((CONV_ROWS, cw), F32)
            for k in range(CONV_WIDTH):
                dyk = _window(dy_shifts, CONV_WIDTH - 1 - k)
                du = du + wv[k:k + 1, :] * dyk
                prod = uc * dyk
                part = prod[0:SUB, :]
                for s in range(1, CONV_ROWS // SUB):
                    part = part + prod[s * SUB:(s + 1) * SUB, :]
                dwp[pl.ds(k * SUB, SUB), :] = dwp[pl.ds(k * SUB, SUB), :] + part
            da_ref[pl.ds(r0, CONV_ROWS), :] = (du * sbc).astype(BF16)
            db_ref[pl.ds(r0, CONV_ROWS), :] = (du * ac * (sbc * (1.0 - sbc))).astype(BF16)
            return carry

        lax.fori_loop(0, nchunk, chunk, 0)

        @pl.when(first)
        def _():
            dw_ref[...] = jnp.zeros(dw_ref.shape, F32)
            dbias_ref[...] = jnp.zeros(dbias_ref.shape, F32)

        for k in range(CONV_WIDTH):
            dw_ref[k:k + 1, :] = dw_ref[k:k + 1, :] + _rowsum(dwp[pl.ds(k * SUB, SUB), :])
        dbias_ref[...] = dbias_ref[...] + _rowsum(dyv)

    blocks = [((seq, cw), F32)] + [((seq, cw), BF16)] * 4
    return _call(
        body, (dy, projp, projp, w), name=name, grid=(C // cw, T // seq),
        in_specs=[pl.BlockSpec((seq, cw), lambda c, b: (b, c)),
                  pl.BlockSpec((seq, cw), lambda c, b: (b, a_col // cw + c)),
                  pl.BlockSpec((seq, cw), lambda c, b: (b, b_col // cw + c)),
                  pl.BlockSpec((CONV_WIDTH, cw), lambda c, b: (0, c))],
        out_specs=[pl.BlockSpec((seq, cw), lambda c, b: (b, c)), pl.BlockSpec((seq, cw), lambda c, b: (b, c)),
                   pl.BlockSpec((CONV_WIDTH, cw), lambda c, b: (0, c)), pl.BlockSpec((1, cw), lambda c, b: (0, c))],
        out_shape=[SDS((T, C), BF16), SDS((T, C), BF16), SDS((CONV_WIDTH, C), F32), SDS((1, C), F32)],
        scratch_shapes=[pltpu.VMEM((seq + CONV_HALO, cw), F32), pltpu.VMEM((CONV_WIDTH * SUB, cw), F32)],
        params=_params(2, blocks, temp_bytes=8 * _nbytes((seq, cw), F32)), comm=comm)


def _matmul_tn(a, b, *, name, comm=None):
    T, M = a.shape
    N = b.shape[1]
    bm = _pick(M, (768, 512, 256))

    def body(a_ref, b_ref, o_ref):
        o_ref[...] = _dot_tn(a_ref[...], b_ref[...]).astype(BF16)

    blocks = [((T, bm), BF16), ((T, N), BF16), ((bm, N), BF16)]
    return _call(
        body, (a, b), name=name, grid=(M // bm,),
        in_specs=[pl.BlockSpec((T, bm), lambda i: (0, i)), pl.BlockSpec((T, N), lambda i: (0, 0))],
        out_specs=[pl.BlockSpec((bm, N), lambda i: (i, 0))],
        out_shape=[SDS((M, N), BF16)],
        params=_params(1, blocks, temp_bytes=2 * _nbytes((T, bm), BF16) + 2 * _nbytes((bm, N), F32)),
        comm=comm)[0]


def _sum_parts(p_ref):
    g = p_ref[0].astype(F32)
    for s in range(1, p_ref.shape[0]):
        g = g + p_ref[s].astype(F32)
    return g


def _pair_add(g, staged, *, name):
    _, R, W = g.shape
    nq = staged.shape[0]
    tr = _row_tile(R)

    def body(g_ref, s_ref, o_ref):
        mine = jnp.where(lax.axis_index("c") == 0, g_ref[0, 0].astype(F32), g_ref[0, 1].astype(F32))
        o_ref[0] = (mine + s_ref[0].astype(F32)).astype(o_ref.dtype)

    return _call(
        body, (g.reshape(nq, 2, R, W), staged), name=name, grid=(nq, R // tr),
        in_specs=[pl.BlockSpec((1, 2, tr, W), lambda q, i: (q, 0, i, 0)),
                  pl.BlockSpec((1, tr, W), lambda q, i: (q, i, 0))],
        out_specs=[pl.BlockSpec((1, tr, W), lambda q, i: (q, i, 0))],
        out_shape=[SDS((nq, R, W), g.dtype)],
        params=_params(2, [((4, tr, W), g.dtype)], temp_bytes=3 * _nbytes((tr, W), F32)))[0]


def _adamw_update(w, g, m, v):
    m = ADAM_B1 * m + (1.0 - ADAM_B1) * g
    v = ADAM_B2 * v + (1.0 - ADAM_B2) * (g * g)
    m_hat = m / (1.0 - ADAM_B1 ** ADAM_STEP)
    v_hat = v / (1.0 - ADAM_B2 ** ADAM_STEP)
    delta = -ADAM_LR * (m_hat / (jnp.sqrt(v_hat) + ADAM_EPS) + ADAM_WD * w)
    return delta, m, v


def _row_tile(R):
    return _pick(R, (256, 128, 112, 88, 64, 32, 16, 8))


def _sum8(parts, *, name):
    n, R, W = parts.shape
    tr = _row_tile(R)

    def body(p_ref, o_ref):
        o_ref[...] = _sum_parts(p_ref)

    return _call(
        body, (parts,), name=name, grid=(R // tr,),
        in_specs=[pl.BlockSpec((n, tr, W), lambda i: (0, i, 0))],
        out_specs=[pl.BlockSpec((tr, W), lambda i: (i, 0))],
        out_shape=[SDS((R, W), F32)],
        params=_params(1, [((n, tr, W), parts.dtype), ((tr, W), F32)]))[0]


def _adamw(g, w, m, v, *, name):
    R, W = w.shape
    tr = _row_tile(R)

    def body(g_ref, w_ref, m_ref, v_ref, d_ref, mo_ref, vo_ref):
        d_ref[...], mo_ref[...], vo_ref[...] = _adamw_update(w_ref[...], g_ref[...], m_ref[...], v_ref[...])

    spec = pl.BlockSpec((tr, W), lambda i: (i, 0))
    return _call(
        body, (g, w, m, v), name=name, grid=(R // tr,),
        in_specs=[spec] * 4, out_specs=[spec] * 3, out_shape=[SDS((R, W), F32)] * 3,
        params=_params(1, [((tr, W), F32)] * 7))


def _sum8_adamw(parts, w, m, v, *, name):
    R, W = w.shape
    n = parts.shape[0]
    tr = _row_tile(R)

    def body(p_ref, w_ref, m_ref, v_ref, g_ref, d_ref, mo_ref, vo_ref):
        g = _sum_parts(p_ref)
        g_ref[...] = g
        d_ref[...], mo_ref[...], vo_ref[...] = _adamw_update(w_ref[...], g, m_ref[...], v_ref[...])

    spec = pl.BlockSpec((tr, W), lambda i: (i, 0))
    return _call(
        body, (parts, w, m, v), name=name, grid=(R // tr,),
        in_specs=[pl.BlockSpec((n, tr, W), lambda i: (0, i, 0))] + [spec] * 3,
        out_specs=[spec] * 4, out_shape=[SDS((R, W), F32)] * 4,
        params=_params(1, [((n, tr, W), parts.dtype)] + [((tr, W), F32)] * 7))


def _ada_fwd(c_all, w, bias, *, name):
    NB, D = c_all.shape
    N = w.shape[1]

    def body(c_ref, w_ref, b_ref, o_ref):
        cv = c_ref[...]
        ca = (cv * _sigmoid(cv)).astype(BF16)
        o_ref[...] = _dot(ca, w_ref[...].astype(BF16)) + b_ref[...]

    full = lambda s: pl.BlockSpec(s, lambda i: (0,) * len(s))
    return _call(
        body, (c_all, w, bias), name=name, grid=(1,),
        in_specs=[full((NB, D)), full((D, N)), full((1, N))], out_specs=[full((NB, N))],
        out_shape=[SDS((NB, N), F32)],
        params=_params(1, [((D, N), F32)], temp_bytes=_nbytes((D, N), BF16)))[0]


def _ada_bwd(c_all, gmod_all, *, n_col, name):
    NB, D = c_all.shape
    N = gmod_all.shape[1]

    def body(c_ref, g_ref, gw_ref, gb_ref):
        cv = c_ref[...]
        ca = (cv * _sigmoid(cv)).astype(BF16)
        first = pl.multiple_of(_lin(_my_pos()) * n_col, 128)
        gw_ref[...] = _dot_tn(ca, g_ref[:, pl.ds(first, n_col)].astype(BF16))
        gb_ref[...] = _rowsum(g_ref[...])

    full = lambda s: pl.BlockSpec(s, lambda i: (0,) * len(s))
    return _call(
        body, (c_all, gmod_all), name=name, grid=(1,),
        in_specs=[full((NB, D)), full((NB, N))], out_specs=[full((D, n_col)), full((1, N))],
        out_shape=[SDS((D, n_col), F32), SDS((1, N), F32)],
        params=_params(1, [((D, n_col), F32), ((NB, N), F32)]))


def kernel(x, c, w_ada, b_ada, norm_ffn1_g, ffn1_w_gate, ffn1_w_up, ffn1_w_down, norm_mix_g, w_in, attn_sinks, w_attn_o, conv_w_dw, conv_b_dw, conv_ln_g, conv_ln_b, w_conv_o, w_out, norm_ffn2_g, ffn2_w_gate, ffn2_w_up, ffn2_w_down, final_norm_g, loss_target, m_w_ada, m_b_ada, m_norm_ffn1_g, m_ffn1_w_gate, m_ffn1_w_up, m_ffn1_w_down, m_norm_mix_g, m_w_in, m_attn_sinks, m_w_attn_o, m_conv_w_dw, m_conv_b_dw, m_conv_ln_g, m_conv_ln_b, m_w_conv_o, m_w_out, m_norm_ffn2_g, m_ffn2_w_gate, m_ffn2_w_up, m_ffn2_w_down, m_final_norm_g, v_w_ada, v_b_ada, v_norm_ffn1_g, v_ffn1_w_gate, v_ffn1_w_up, v_ffn1_w_down, v_norm_mix_g, v_w_in, v_attn_sinks, v_w_attn_o, v_conv_w_dw, v_conv_b_dw, v_conv_ln_g, v_conv_ln_b, v_w_conv_o, v_w_out, v_norm_ffn2_g, v_ffn2_w_gate, v_ffn2_w_up, v_ffn2_w_down, v_final_norm_g):
    B, S, D = x.shape
    T = B * S
    QW = N_Q_HEADS * HEAD_DIM
    CC = conv_w_dw.shape[2] * N_DEV
    me = _lin(_my_pos())
    xf = x.reshape(T, D)
    tgt = loss_target.reshape(T, D)
    tm = min(512, S)
    kw = dict(seq=S, tm=tm)

    p_k, p_v, p_ca = QW, QW + KV_WIDTH, QW + 2 * KV_WIDTH
    p_cb, p_ga, p_gc = p_ca + CC, p_ca + 2 * CC, p_ca + 2 * CC + D

    def col_t(w):
        return w[0].T.astype(BF16)

    def row_b(w):
        return w[0].astype(BF16)

    def rows(g):
        return g.reshape(-1, g.shape[-1])

    def blocks8(g):
        return g.reshape(N_DEV, g.shape[0] // N_DEV, g.shape[1])

    def gather(*arrs):
        return _Comm([(a, "gather") for a in arrs])

    g_wg1, g_convw, g_c = _exchange(
        [(col_t(ffn1_w_gate), "gather"), (conv_w_dw[0], "gather"), (c, "gather")], name="gather_first")
    wg1 = rows(g_wg1)
    conv_w = g_convw.transpose(1, 0, 2).reshape(CONV_WIDTH, CC)
    c_all = g_c.reshape(N_DEV * B, D)

    n_col = N_MOD * D // N_DEV
    b_cols = lax.dynamic_slice(b_ada, (0, me * n_col), (1, n_col))
    mod_cols = _ada_fwd(c_all, w_ada[0], b_cols, name="ada_fwd")
    mod_mine = _exchange([(mod_cols.reshape(N_DEV, B, n_col), "scatter")], name="scatter_mod")[0]
    mod = mod_mine.transpose(1, 0, 2).reshape(B, N_MOD, 1, D)
    sh1, sc1, g1, sh2, sc2, g2, sh3, sc3, g3 = [mod[:, i] for i in range(N_MOD)]

    F = wg1.shape[0]
    tn_f = _pick(F, (1408, 1024, 512, 256))
    tn_in = _pick(w_in.shape[2] * N_DEV, (1792, 768, 512, 256))
    gate_blk = dict(ga_col=p_ga, gc_col=p_gc)
    att_blk = dict(q_blk=0, k_blk=p_k // KV_WIDTH, v_blk=p_v // KV_WIDTH)
    conv_kw = dict(seq=S, cw=256, a_col=p_ca, b_col=p_cb)

    cm = gather(col_t(ffn1_w_up))
    h1, (a1,) = _norm_mod_matmul(xf, norm_ffn1_g, sh1, sc1, [wg1], tn=tn_f, name="ffn1_gate", comm=cm, **kw)
    wu1 = rows(cm.out[0])
    cm = gather(row_b(ffn1_w_down))
    b1 = _matmul_nt(h1, wu1, tm=tm, tn=tn_f, name="ffn1_up", comm=cm)
    wd1 = rows(cm.out[0])
    cm = gather(col_t(w_in))
    x1, y1 = _ffn_down(a1, b1, wd1, xf, g1, name="ffn1_down", comm=cm, **kw)
    winp = rows(cm.out[0])
    cm = gather(row_b(w_attn_o), row_b(w_conv_o), row_b(w_out), col_t(ffn2_w_gate))
    h2, (projp,) = _norm_mod_matmul(x1, norm_mix_g, sh2, sc2, [winp], tn=tn_in, name="mix_in", comm=cm, **kw)
    wao, wco, wout, wg2 = [rows(o) for o in cm.out]
    cm = gather(col_t(ffn2_w_up))
    ao = _attn_fwd(projp, attn_sinks, seq=S, name="attn_fwd", comm=cm, **att_blk)
    wu2 = rows(cm.out[0])
    cm = gather(row_b(ffn2_w_down))
    yc = _conv_fwd(projp, conv_w, conv_b_dw, name="conv_fwd", comm=cm, **conv_kw)
    wd2 = rows(cm.out[0])
    x2, z, ya, ycv, cact, merged = _mix_out(ao, yc, projp, wao, wco, wout, x1, g2, conv_ln_g, conv_ln_b,
                                            name="mix_out", **gate_blk, **kw)
    h3, (a3, b3) = _norm_mod_matmul(x2, norm_ffn2_g, sh3, sc3, [wg2, wu2], tn=tn_f, name="ffn2_up", **kw)
    x3, y3 = _ffn_down(a3, b3, wd2, x2, g3, name="ffn2_down", **kw)
    dx3, loss_row, dgf = _final_loss(x3, final_norm_g[None], tgt, tm=tm, name="final_loss")
    loss = lax.psum(loss_row[0, 0], ("x", "y", "c"))

    parts = {}

    def pair(*gs):
        return [(blocks8(g), "pair") for g in gs]

    def cross(*rs):
        return [(r, "cross") for r in rs]

    def reduce_pairs(gs, staged, names):
        return [_pair_add(blocks8(g), s, name="pair_add_" + n) for g, s, n in zip(gs, staged, names)]

    dyb3, da3, db3, act3, dg3 = _ffn_bwd_down(dx3, g3, y3, wd2, a3, b3, tn=tn_f, name="ffn2_bwd_down", **kw)
    gwd2 = _matmul_tn(act3, dyb3, name="gw_ffn2_down")
    cm = _Comm(pair(gwd2))
    dx2, dsh3, dsc3, dgn3 = _matmul_norm_mod_bwd([da3, db3], [wg2, wu2], x2, norm_ffn2_g, sc3, dx3,
                                                 name="ffn2_bwd_up", comm=cm, **kw)
    r_wd2, = reduce_pairs([gwd2], cm.out, ["ffn2_w_down"])
    cm = _Comm(cross(r_wd2))
    gwg2 = _matmul_tn(da3, h3, name="gw_ffn2_gate", comm=cm)
    parts["ffn2_w_down"], = cm.out
    cm = _Comm(pair(gwg2))
    gwu2 = _matmul_tn(db3, h3, name="gw_ffn2_up", comm=cm)
    r_wg2, = reduce_pairs([gwg2], cm.out, ["ffn2_w_gate"])

    cm = _Comm(cross(r_wg2) + pair(gwu2))
    dzb, dyab, dycb, dga, dgc, dao, dyc, dg2, dlng, dlnb = _mix_out_bwd(
        dx2, g2, z, wout, projp, ya, ycv, wao, wco, yc, conv_ln_g, conv_ln_b, name="mix_out_bwd", comm=cm,
        **gate_blk, **kw)
    parts["ffn2_w_gate"] = cm.out[0]
    r_wu2, = reduce_pairs([gwu2], cm.out[1:], ["ffn2_w_up"])
    gwout = _matmul_tn(merged, dzb, name="gw_out")
    gwao = _matmul_tn(ao, dyab, name="gw_attn_o")
    gwco = _matmul_tn(cact, dycb, name="gw_conv_o")
    cm = _Comm(cross(r_wu2) + pair(gwout, gwao, gwco))
    dq, dk, dv, dsinks = _attn_bwd(projp, dao, attn_sinks, seq=S, name="attn_bwd", comm=cm, **att_blk)
    parts["ffn2_w_up"] = cm.out[0]
    r_mix = reduce_pairs([gwout, gwao, gwco], cm.out[1:], ["w_out", "w_attn_o", "w_conv_o"])
    cm = _Comm(cross(*r_mix))
    dca, dcb, dconvw, dconvb = _conv_bwd(dyc, projp, conv_w, name="conv_bwd", comm=cm, **conv_kw)
    parts["w_out"], parts["w_attn_o"], parts["w_conv_o"] = cm.out
    dprojp = jnp.concatenate([dq, dk, dv, dca, dcb, dga, dgc], axis=1)
    gwin = _matmul_tn(dprojp, h2, name="gw_in")
    cm = _Comm(pair(gwin))
    dx1, dsh2, dsc2, dgn2 = _matmul_norm_mod_bwd([dprojp], [winp], x1, norm_mix_g, sc2, dx2,
                                                 name="mix_in_bwd", comm=cm, **kw)
    r_win, = reduce_pairs([gwin], cm.out, ["w_in"])

    cm = _Comm(cross(r_win))
    dyb1, da1, db1, act1, dg1 = _ffn_bwd_down(dx1, g1, y1, wd1, a1, b1, tn=tn_f, name="ffn1_bwd_down", comm=cm,
                                              **kw)
    parts["w_in"], = cm.out
    gwd1 = _matmul_tn(act1, dyb1, name="gw_ffn1_down")
    cm = _Comm(pair(gwd1))
    gwg1 = _matmul_tn(da1, h1, name="gw_ffn1_gate", comm=cm)
    r_wd1, = reduce_pairs([gwd1], cm.out, ["ffn1_w_down"])
    cm = _Comm(cross(r_wd1) + pair(gwg1))
    gwu1 = _matmul_tn(db1, h1, name="gw_ffn1_up", comm=cm)
    parts["ffn1_w_down"] = cm.out[0]
    r_wg1, = reduce_pairs([gwg1], cm.out[1:], ["ffn1_w_gate"])
    cm = _Comm(cross(r_wg1) + pair(gwu1))
    dx0, dsh1, dsc1, dgn1 = _matmul_norm_mod_bwd([da1, db1], [wg1, wu1], xf, norm_ffn1_g, sc1, dx1,
                                                 name="ffn1_bwd_up", comm=cm, **kw)
    parts["ffn1_w_gate"] = cm.out[0]
    r_wu1, = reduce_pairs([gwu1], cm.out[1:], ["ffn1_w_up"])

    n_small = 8
    gmod = jnp.concatenate([dsh1, dsc1, dg1, dsh2, dsc2, dg2, dsh3, dsc3, dg3], axis=1).reshape(B, N_MOD * D)
    sink_row = jnp.pad(dsinks[:, :N_Q_HEADS], ((0, 0), (0, D - N_Q_HEADS)))
    small = jnp.concatenate([dgn1, dgn2, dgn3, dgf, dconvb, dlng, dlnb, sink_row, dconvw,
                             jnp.zeros((1, D), F32)], axis=0)
    parts["ffn1_w_up"], small_all, gmod_all = _exchange(
        cross(r_wu1) + [(small, "gather"), (gmod, "gather")], name="exchange_last")
    gsmall = _sum8(small_all, name="sum_small")
    g_w_ada, g_b_ada = _ada_bwd(c_all, gmod_all.reshape(N_DEV * B, N_MOD * D), n_col=n_col, name="ada_bwd")
    g_conv_w = lax.dynamic_slice(gsmall[n_small:n_small + CONV_WIDTH], (0, me * (CC // N_DEV)),
                                 (CONV_WIDTH, CC // N_DEV))

    def col_update(name, w, m, v):
        outs = _sum8_adamw(parts[name], w[0].T, m[0].T, v[0].T, name="adamw_" + name)
        return tuple(o.T for o in outs)

    def row_update(name, w, m, v):
        return tuple(_sum8_adamw(parts[name], w[0], m[0], v[0], name="adamw_" + name))

    upd = {
        "ffn1_w_gate": col_update("ffn1_w_gate", ffn1_w_gate, m_ffn1_w_gate, v_ffn1_w_gate),
        "ffn1_w_up": col_update("ffn1_w_up", ffn1_w_up, m_ffn1_w_up, v_ffn1_w_up),
        "ffn1_w_down": row_update("ffn1_w_down", ffn1_w_down, m_ffn1_w_down, v_ffn1_w_down),
        "w_in": col_update("w_in", w_in, m_w_in, v_w_in),
        "w_attn_o": row_update("w_attn_o", w_attn_o, m_w_attn_o, v_w_attn_o),
        "w_conv_o": row_update("w_conv_o", w_conv_o, m_w_conv_o, v_w_conv_o),
        "w_out": row_update("w_out", w_out, m_w_out, v_w_out),
        "ffn2_w_gate": col_update("ffn2_w_gate", ffn2_w_gate, m_ffn2_w_gate, v_ffn2_w_gate),
        "ffn2_w_up": col_update("ffn2_w_up", ffn2_w_up, m_ffn2_w_up, v_ffn2_w_up),
        "ffn2_w_down": row_update("ffn2_w_down", ffn2_w_down, m_ffn2_w_down, v_ffn2_w_down),
        "w_ada": (g_w_ada,) + tuple(_adamw(g_w_ada, w_ada[0], m_w_ada[0], v_w_ada[0], name="adamw_w_ada")),
        "conv_w_dw": (g_conv_w,) + tuple(_adamw(g_conv_w, conv_w_dw[0], m_conv_w_dw[0], v_conv_w_dw[0],
                                                name="adamw_conv_w_dw")),
    }
    for k in upd:
        upd[k] = tuple(t[None] for t in upd[k])

    def pad_sinks(t):
        return jnp.pad(t, ((0, 0), (0, D - N_Q_HEADS)))

    def pack(f1, mix, f2, fin, cb, lg, lb, sinks, bada):
        return jnp.concatenate([f1, mix, f2, fin[None], cb, lg, lb, pad_sinks(sinks), bada.reshape(N_MOD, D)], axis=0)

    w_s = pack(norm_ffn1_g, norm_mix_g, norm_ffn2_g, final_norm_g, conv_b_dw, conv_ln_g, conv_ln_b, attn_sinks, b_ada)
    m_s = pack(m_norm_ffn1_g, m_norm_mix_g, m_norm_ffn2_g, m_final_norm_g, m_conv_b_dw, m_conv_ln_g, m_conv_ln_b,
               m_attn_sinks, m_b_ada)
    v_s = pack(v_norm_ffn1_g, v_norm_mix_g, v_norm_ffn2_g, v_final_norm_g, v_conv_b_dw, v_conv_ln_g, v_conv_ln_b,
               v_attn_sinks, v_b_ada)
    g_s = jnp.concatenate([gsmall[:n_small], g_b_ada.reshape(N_MOD, D)], axis=0)
    small_out = (g_s,) + tuple(_adamw(g_s, w_s, m_s, v_s, name="adamw_vectors"))

    def unpack(t):
        return {
            "norm_ffn1_g": t[0:1], "norm_mix_g": t[1:2], "norm_ffn2_g": t[2:3], "final_norm_g": t[3],
            "conv_b_dw": t[4:5], "conv_ln_g": t[5:6], "conv_ln_b": t[6:7], "attn_sinks": t[7:8, :N_Q_HEADS],
            "b_ada": t[n_small:n_small + N_MOD].reshape(1, N_MOD * D),
        }

    small_un = [unpack(t) for t in small_out]
    for k in small_un[0]:
        upd[k] = tuple(s[k] for s in small_un)

    order = ["w_ada", "b_ada", "norm_ffn1_g", "ffn1_w_gate", "ffn1_w_up", "ffn1_w_down", "norm_mix_g", "w_in",
             "attn_sinks", "w_attn_o", "conv_w_dw", "conv_b_dw", "conv_ln_g", "conv_ln_b", "w_conv_o", "w_out",
             "norm_ffn2_g", "ffn2_w_gate", "ffn2_w_up", "ffn2_w_down", "final_norm_g"]
    grad_x = dx0.reshape(B, S, D)
    return (loss, grad_x, *[upd[k][0] for k in order], *[upd[k][1] for k in order],
            *[upd[k][2] for k in order], *[upd[k][3] for k in order])
```

```python
import jax
import jax.numpy as jnp
from jax import lax
from jax.experimental import pallas as pl
from jax.experimental.pallas import tpu as pltpu

F32 = jnp.float32
BF16 = jnp.bfloat16
SDS = jax.ShapeDtypeStruct
MESH = pl.DeviceIdType.MESH

N_DEV = 8
EPS = 1e-6
HEAD_DIM = 64
N_Q_HEADS = 16
N_KV_HEADS = 2
GQA_GROUP = N_Q_HEADS // N_KV_HEADS
KV_WIDTH = N_KV_HEADS * HEAD_DIM
ATT_BLOCK = 128
CONV_WIDTH = 31
CONV_HALO = 32
CONV_ROWS = 64
N_MOD = 9
FFN_RESIDUAL = 0.5
ADAM_LR = 0.001
ADAM_B1 = 0.9
ADAM_B2 = 0.999
ADAM_EPS = 1e-08
ADAM_WD = 0.01
ADAM_STEP = 10
NEG_BIG = -1e30

V7X_VMEM_BYTES = 64 * 2**20
VMEM_CAP = V7X_VMEM_BYTES - 8 * 2**20


def _nbytes(shape, dtype):
    n = 1
    for s in shape:
        n *= s
    return n * jnp.dtype(dtype).itemsize


def _params(n_axes, blocks, temp_bytes=0):
    need = 2 * sum(_nbytes(s, d) for s, d in blocks) + temp_bytes + 4 * 2**20
    return pltpu.CompilerParams(dimension_semantics=("arbitrary",) * n_axes,
                                vmem_limit_bytes=int(min(max(need, 16 * 2**20), VMEM_CAP)))


def _dot_nt(a, b):
    return lax.dot_general(a, b, (((1,), (1,)), ((), ())), preferred_element_type=F32)


def _dot_tn(a, b):
    return lax.dot_general(a, b, (((0,), (0,)), ((), ())), preferred_element_type=F32)


def _dot(a, b):
    return jnp.dot(a, b, preferred_element_type=F32)


def _sigmoid(x):
    return jax.nn.sigmoid(x)


def _rowsum(v):
    return jnp.sum(v, axis=0, keepdims=True)


def _acc(ref, val, first):
    @pl.when(first)
    def _():
        ref[...] = val

    @pl.when(jnp.logical_not(first))
    def _():
        ref[...] = ref[...] + val


def _norm_mod(xf, gn, sh, sc):
    rstd = lax.rsqrt(jnp.mean(xf * xf, axis=-1, keepdims=True) + EPS)
    xhat = xf * rstd
    yn = xhat * gn
    return yn * (1.0 + sc) + sh, xhat, rstd, yn


def _pick(n, cands):
    for c in cands:
        if n % c == 0:
            return c
    return n


def _my_pos():
    return lax.axis_index("x"), lax.axis_index("y"), lax.axis_index("c")


def _peer(pos, k):
    x, y, c = pos
    return ((1 - x) if k & 4 else x, (1 - y) if k & 2 else y, (1 - c) if k & 1 else c)


def _lin(pos):
    return 4 * pos[0] + 2 * pos[1] + pos[2]


class _Comm:
    N_COPY = N_DEV - 1
    N_CHIP = N_DEV // 2

    def __init__(self, items):
        self.arrs = [a for a, _ in items]
        self.modes = [m for _, m in items]
        self.n = len(items)
        self.out = None

    def out_shape(self):
        def shape(a, m):
            return {"gather": (N_DEV,) + a.shape, "scatter": a.shape, "pair": (self.N_CHIP,) + a.shape[1:],
                    "cross": a.shape}[m]
        return [SDS(shape(a, m), a.dtype) for a, m in zip(self.arrs, self.modes)]

    def scratch(self):
        return [pltpu.SemaphoreType.DMA((self.n * self.N_COPY,)), pltpu.SemaphoreType.DMA((self.n * self.N_COPY,)),
                pltpu.SemaphoreType.DMA((self.n,))]

    def _plan(self, mode, me):
        x, y, c = me
        sib = (x, y, 1 - c)
        chips = [(1 - x, y), (x, 1 - y), (1 - x, 1 - y)]

        def chip_lin(ch):
            return 2 * ch[0] + ch[1]

        if mode == "scatter":
            peers = [_peer(me, k + 1) for k in range(self.N_COPY)]
            return [(p, ("in", _lin(p)), _lin(me), _lin(p), None) for p in peers], (_lin(me), _lin(me))
        if mode == "gather":
            same = [(*ch, c) for ch in chips]
            other = [(*ch, 1 - c) for ch in chips]
            copies = [(sib, ("in", None), _lin(me), _lin(sib), None)]
            copies += [(p, ("in", None), _lin(me), _lin(p), None) for p in same]
            copies += [(sib, ("out", _lin(p)), _lin(p), _lin(o), 1 + j) for j, (p, o) in enumerate(zip(same, other))]
            return copies, (None, _lin(me))
        if mode == "pair":
            return [(sib, ("in", 2 * q + 1 - c), q, q, None) for q in range(self.N_CHIP)], None
        if mode == "cross":
            mine = chip_lin((x, y))
            return ([((*ch, c), ("in", chip_lin(ch)), mine, chip_lin(ch), None) for ch in chips], (mine, mine))
        raise ValueError(mode)

    def _copy(self, refs, me, i, k, recv):
        srcs, outs, (send_sems, recv_sems, _) = refs
        peer, (where, slot), send_slot, recv_slot, _ = self._plan(self.modes[i], me)[0][k]
        src = srcs[i] if where == "in" else outs[i]
        src = src if slot is None else src.at[slot]
        sem = i * self.N_COPY + k
        return pltpu.make_async_remote_copy(
            src_ref=src, dst_ref=outs[i].at[recv_slot if recv else send_slot], send_sem=send_sems.at[sem],
            recv_sem=recv_sems.at[sem], device_id=peer, device_id_type=MESH)

    def _local(self, refs, me, i):
        srcs, outs, (_, _, loc_sems) = refs
        local = self._plan(self.modes[i], me)[1]
        if local is None:
            return None
        own = srcs[i] if local[0] is None else srcs[i].at[local[0]]
        return pltpu.make_async_copy(own, outs[i].at[local[1]], loc_sems.at[i])

    def start(self, refs):
        me = _my_pos()
        for i in range(self.n):
            local = self._local(refs, me, i)
            if local is not None:
                local.start()
            for k, cp in enumerate(self._plan(self.modes[i], me)[0]):
                if cp[4] is None:
                    self._copy(refs, me, i, k, False).start()

    def forward(self, refs):
        me = _my_pos()
        for i in range(self.n):
            for k, cp in enumerate(self._plan(self.modes[i], me)[0]):
                if cp[4] is not None:
                    self._copy(refs, me, i, cp[4], True).wait_recv()
                    self._copy(refs, me, i, k, False).start()

    def finish(self, refs):
        me = _my_pos()
        plans = [self._plan(m, me)[0] for m in self.modes]
        for i in range(self.n):
            passed_on = [cp[4] for cp in plans[i] if cp[4] is not None]
            for k in range(len(plans[i])):
                if k not in passed_on:
                    self._copy(refs, me, i, k, True).wait_recv()
                self._copy(refs, me, i, k, False).wait_send()
            local = self._local(refs, me, i)
            if local is not None:
                local.wait()


_ANY = pl.BlockSpec(memory_space=pl.ANY)


def _call(body, args, *, name, grid, in_specs, out_specs, out_shape, params, scratch_shapes=(), comm=None):
    in_specs, out_specs, out_shape = list(in_specs), list(out_specs), list(out_shape)
    scratch_shapes = list(scratch_shapes)
    if comm is None:
        return list(pl.pallas_call(body, name=name, grid=grid, in_specs=in_specs, out_specs=out_specs,
                                   out_shape=out_shape, scratch_shapes=scratch_shapes, compiler_params=params)(*args))
    n_in, n_out, n_scr, nc = len(in_specs), len(out_specs), len(scratch_shapes), comm.n
    n_steps = 1
    for g in grid:
        n_steps *= g

    def hosted(*refs):
        ins, c_in = refs[:n_in], refs[n_in:n_in + nc]
        outs = refs[n_in + nc:n_in + nc + n_out]
        c_out = refs[n_in + nc + n_out:n_in + 2 * nc + n_out]
        scr = refs[n_in + 2 * nc + n_out:n_in + 2 * nc + n_out + n_scr]
        sems = refs[n_in + 2 * nc + n_out + n_scr:]
        step = pl.program_id(0)
        for d in range(1, len(grid)):
            step = step * grid[d] + pl.program_id(d)
        c_refs = (c_in, c_out, sems)

        @pl.when(step == 0)
        def _():
            comm.start(c_refs)

        if n_steps >= 3:
            @pl.when(step == n_steps - 2)
            def _():
                comm.forward(c_refs)

        body(*ins, *outs, *scr)

        @pl.when(step == n_steps - 1)
        def _():
            if n_steps < 3:
                comm.forward(c_refs)
            comm.finish(c_refs)

    res = pl.pallas_call(
        hosted, name=name, grid=grid, in_specs=in_specs + [_ANY] * nc, out_specs=out_specs + [_ANY] * nc,
        out_shape=out_shape + comm.out_shape(), scratch_shapes=scratch_shapes + comm.scratch(),
        compiler_params=params)(*args, *comm.arrs)
    comm.out = list(res[n_out:])
    return list(res[:n_out])


def _exchange(items, *, name):
    comm = _Comm(items)

    def body(*refs):
        r = (refs[:comm.n], refs[comm.n:2 * comm.n], refs[2 * comm.n:])
        comm.start(r)
        comm.forward(r)
        comm.finish(r)

    return list(pl.pallas_call(body, name=name, out_shape=comm.out_shape(), in_specs=[_ANY] * comm.n,
                               out_specs=[_ANY] * comm.n, scratch_shapes=comm.scratch())(*comm.arrs))


class _ModVec:
    def __init__(self, arr, idx):
        self.arr, self.idx = arr, idx

    def spec(self, tps, n_axes):
        idx, blk = self.idx, (1, 1, self.arr.shape[2])
        if n_axes == 1:
            return pl.BlockSpec(blk, lambda i: (i // tps * N_MOD + idx, 0, 0))
        return pl.BlockSpec(blk, lambda i, j: (i // tps * N_MOD + idx, 0, 0))


def _norm_mod_matmul(x, gn, sh, sc, wts, *, seq, tm, tn, name, comm=None):
    T, D = x.shape
    N = wts[0].shape[0]
    nw = len(wts)
    tps = seq // tm

    def body(x_ref, gn_ref, sh_ref, sc_ref, *rest):
        w_refs, h_ref, o_refs = rest[:nw], rest[nw], rest[nw + 1:]

        @pl.when(pl.program_id(1) == 0)
        def _():
            h_ref[...] = _norm_mod(x_ref[...], gn_ref[...], sh_ref[0], sc_ref[0])[0].astype(BF16)

        h = h_ref[...]
        for w_ref, o_ref in zip(w_refs, o_refs):
            o_ref[...] = _dot_nt(h, w_ref[...]).astype(o_ref.dtype)

    row = pl.BlockSpec((tm, D), lambda i, j: (i, 0))
    vec = pl.BlockSpec((1, D), lambda i, j: (0, 0))
    per_b = pl.BlockSpec((1, 1, D), lambda i, j: (i // tps, 0, 0))
    wspec = pl.BlockSpec((tn, D), lambda i, j: (j, 0))
    ospec = pl.BlockSpec((tm, tn), lambda i, j: (i, j))
    blocks = [((tm, D), F32), ((tm, D), BF16)] + [((tn, D), BF16), ((tm, tn), BF16)] * nw
    outs = _call(
        body, (x, gn, sh.arr, sc.arr, *wts), name=name, grid=(T // tm, N // tn),
        in_specs=[row, vec, sh.spec(tps, 2), sc.spec(tps, 2)] + [wspec] * nw,
        out_specs=[row] + [ospec] * nw,
        out_shape=[SDS((T, D), BF16)] + [SDS((T, N), BF16)] * nw,
        params=_params(2, blocks, temp_bytes=2 * _nbytes((tm, tn), F32) + 3 * _nbytes((tm, D), F32)), comm=comm)
    return outs[0], outs[1:]


def _matmul_nt(h, w, *, tm, tn, name, comm=None):
    T, D = h.shape
    N = w.shape[0]

    def body(h_ref, w_ref, o_ref):
        o_ref[...] = _dot_nt(h_ref[...], w_ref[...]).astype(o_ref.dtype)

    blocks = [((tm, D), BF16), ((tn, D), BF16), ((tm, tn), BF16)]
    return _call(
        body, (h, w), name=name, grid=(T // tm, N // tn),
        in_specs=[pl.BlockSpec((tm, D), lambda i, j: (i, 0)), pl.BlockSpec((tn, D), lambda i, j: (j, 0))],
        out_specs=[pl.BlockSpec((tm, tn), lambda i, j: (i, j))],
        out_shape=[SDS((T, N), BF16)],
        params=_params(2, blocks, temp_bytes=2 * _nbytes((tm, tn), F32)), comm=comm)[0]


def _ffn_down(a, b, wd, x, g, *, seq, tm, name, comm=None):
    T, F = a.shape
    D = wd.shape[1]
    tps = seq // tm

    def body(a_ref, b_ref, wd_ref, x_ref, g_ref, xo_ref, y_ref):
        af = a_ref[...].astype(F32)
        act = (af * _sigmoid(af) * b_ref[...].astype(F32)).astype(BF16)
        y = _dot(act, wd_ref[...])
        xo_ref[...] = x_ref[...] + (FFN_RESIDUAL * g_ref[0]) * y
        y_ref[...] = y.astype(BF16)

    wide = pl.BlockSpec((tm, F), lambda i: (i, 0))
    row = pl.BlockSpec((tm, D), lambda i: (i, 0))
    per_b = pl.BlockSpec((1, 1, D), lambda i: (i // tps, 0, 0))
    wspec = pl.BlockSpec((F, D), lambda i: (0, 0))
    blocks = [((tm, F), BF16)] * 2 + [((F, D), BF16), ((tm, D), F32), ((tm, D), F32), ((tm, D), BF16)]
    return _call(
        body, (a, b, wd, x, g.arr), name=name, grid=(T // tm,),
        in_specs=[wide, wide, wspec, row, g.spec(tps, 1)], out_specs=[row, row],
        out_shape=[SDS((T, D), F32), SDS((T, D), BF16)],
        params=_params(1, blocks, temp_bytes=3 * _nbytes((tm, F), F32)), comm=comm)


def _final_loss(x, gf, tgt, *, tm, name):
    T, D = x.shape
    nt = T // tm

    def body(x_ref, gf_ref, t_ref, dx_ref, loss_ref, dgf_ref, lacc):
        i = pl.program_id(0)
        xf = x_ref[...]
        gfv = gf_ref[...]
        rstd = lax.rsqrt(jnp.mean(xf * xf, axis=-1, keepdims=True) + EPS)
        xhat = xf * rstd
        err = xhat * gfv - t_ref[...]
        dy = err * (1.0 / D)
        dxhat = dy * gfv
        dx_ref[...] = rstd * (dxhat - xhat * jnp.mean(dxhat * xhat, axis=-1, keepdims=True))
        _acc(dgf_ref, _rowsum(dy * xhat), i == 0)
        _acc(lacc, _rowsum(err * err), i == 0)

        @pl.when(i == nt - 1)
        def _():
            loss_ref[...] = jnp.broadcast_to((0.5 / D) * jnp.sum(lacc[...]), loss_ref.shape)

    row = pl.BlockSpec((tm, D), lambda i: (i, 0))
    vec = pl.BlockSpec((1, D), lambda i: (0, 0))
    lspec = pl.BlockSpec((1, 128), lambda i: (0, 0))
    blocks = [((tm, D), F32)] * 3
    return _call(
        body, (x, gf, tgt), name=name, grid=(nt,),
        in_specs=[row, vec, row], out_specs=[row, lspec, vec],
        out_shape=[SDS((T, D), F32), SDS((1, 128), F32), SDS((1, D), F32)],
        scratch_shapes=[pltpu.VMEM((1, D), F32)],
        params=_params(1, blocks, temp_bytes=4 * _nbytes((tm, D), F32)))


def _ffn_bwd_down(dxo, g, y, wd, a, b, *, seq, tm, tn, name, comm=None):
    T, F = a.shape
    D = wd.shape[1]
    tps = seq // tm
    nb = T // seq

    def body(dxo_ref, g_ref, y_ref, wd_ref, a_ref, b_ref, dyb_ref, da_ref, db_ref, act_ref, dg_ref):
        i = pl.program_id(0)

        @pl.when(pl.program_id(1) == 0)
        def _():
            dx = dxo_ref[...]
            dyb_ref[...] = ((FFN_RESIDUAL * g_ref[0]) * dx).astype(BF16)
            part = _rowsum(FFN_RESIDUAL * dx * y_ref[...].astype(F32))
            _acc(dg_ref, part[None], i % tps == 0)

        dact = _dot_nt(dyb_ref[...], wd_ref[...])
        af = a_ref[...].astype(F32)
        bf = b_ref[...].astype(F32)
        sg = _sigmoid(af)
        silu = af * sg
        act_ref[...] = (silu * bf).astype(BF16)
        da_ref[...] = (dact * bf * (sg * (1.0 + af * (1.0 - sg)))).astype(BF16)
        db_ref[...] = (dact * silu).astype(BF16)

    row = pl.BlockSpec((tm, D), lambda i, j: (i, 0))
    per_b = pl.BlockSpec((1, 1, D), lambda i, j: (i // tps, 0, 0))
    wspec = pl.BlockSpec((tn, D), lambda i, j: (j, 0))
    chunk = pl.BlockSpec((tm, tn), lambda i, j: (i, j))
    blocks = [((tm, D), F32), ((tm, D), BF16), ((tn, D), BF16), ((tm, D), BF16)] + [((tm, tn), BF16)] * 5
    return _call(
        body, (dxo, g.arr, y, wd, a, b), name=name, grid=(T // tm, F // tn),
        in_specs=[row, g.spec(tps, 2), row, wspec, chunk, chunk],
        out_specs=[row, chunk, chunk, chunk, per_b],
        out_shape=[SDS((T, D), BF16)] + [SDS((T, F), BF16)] * 3 + [SDS((nb, 1, D), F32)],
        params=_params(2, blocks, temp_bytes=6 * _nbytes((tm, tn), F32)), comm=comm)


def _matmul_norm_mod_bwd(ds, ws, x, gn, sc, dxo, *, seq, tm, name, comm=None):
    T, D = x.shape
    nk = len(ds)
    tps = seq // tm
    nb = T // seq

    def body(*refs):
        d_refs, w_refs = refs[:nk], refs[nk:2 * nk]
        x_ref, gn_ref, sc_ref, dxo_ref, dxi_ref, dsh_ref, dsc_ref, dgn_ref = refs[2 * nk:]
        i = pl.program_id(0)
        dh = _dot(d_refs[0][...], w_refs[0][...])
        for d_ref, w_ref in zip(d_refs[1:], w_refs[1:]):
            dh = dh + _dot(d_ref[...], w_ref[...])
        gnv = gn_ref[...]
        scv = sc_ref[0]
        _, xhat, rstd, yn = _norm_mod(x_ref[...], gnv, 0.0, scv)
        dyn = dh * (1.0 + scv)
        dxhat = dyn * gnv
        dxi_ref[...] = dxo_ref[...] + rstd * (dxhat - xhat * jnp.mean(dxhat * xhat, axis=-1, keepdims=True))
        first_of_seq = i % tps == 0
        _acc(dsh_ref, _rowsum(dh)[None], first_of_seq)
        _acc(dsc_ref, _rowsum(dh * yn)[None], first_of_seq)
        _acc(dgn_ref, _rowsum(dyn * xhat), i == 0)

    row = pl.BlockSpec((tm, D), lambda i: (i, 0))
    vec = pl.BlockSpec((1, D), lambda i: (0, 0))
    per_b = pl.BlockSpec((1, 1, D), lambda i: (i // tps, 0, 0))
    d_specs = [pl.BlockSpec((tm, d.shape[1]), lambda i: (i, 0)) for d in ds]
    w_specs = [pl.BlockSpec(w.shape, lambda i: (0, 0)) for w in ws]
    blocks = ([((tm, d.shape[1]), BF16) for d in ds] + [(w.shape, BF16) for w in ws] + [((tm, D), F32)] * 3)
    return _call(
        body, (*ds, *ws, x, gn, sc.arr, dxo), name=name, grid=(T // tm,),
        in_specs=d_specs + w_specs + [row, vec, sc.spec(tps, 1), row],
        out_specs=[row, per_b, per_b, vec],
        out_shape=[SDS((T, D), F32), SDS((nb, 1, D), F32), SDS((nb, 1, D), F32), SDS((1, D), F32)],
        params=_params(1, blocks, temp_bytes=6 * _nbytes((tm, D), F32)), comm=comm)


def _layernorm_silu(yc, lg, lb):
    mu = jnp.mean(yc, axis=-1, keepdims=True)
    cen = yc - mu
    rstd = lax.rsqrt(jnp.mean(cen * cen, axis=-1, keepdims=True) + EPS)
    xh = cen * rstd
    l = xh * lg + lb
    s = _sigmoid(l)
    return l * s, xh, rstd, l, s


GATE_W = 256


def _gate_specs(tm, D, col):
    return [pl.BlockSpec((tm, GATE_W), lambda i, blk=col // GATE_W + t: (i, blk)) for t in range(D // GATE_W)]


def _gate(refs):
    return jnp.concatenate([r[...] for r in refs], axis=1).astype(F32)


def _mix_out(ao, yc, proj, wao, wco, wout, x1, g2, lg, lb, *, seq, tm, ga_col, gc_col, name, comm=None):
    T, D = x1.shape
    tps = seq // tm
    ng = D // GATE_W

    def body(ao_ref, yc_ref, *rest):
        ga_refs, gc_refs = rest[:ng], rest[ng:2 * ng]
        (wao_ref, wco_ref, wout_ref, x1_ref, g2_ref, lg_ref, lb_ref,
         x2_ref, z_ref, ya_ref, ycv_ref, cact_ref, mrg_ref) = rest[2 * ng:]
        ya = _dot(ao_ref[...], wao_ref[...])
        cact = _layernorm_silu(yc_ref[...], lg_ref[...], lb_ref[...])[0].astype(BF16)
        ycv = _dot(cact, wco_ref[...])
        merged = (_sigmoid(_gate(ga_refs)) * ya + _sigmoid(_gate(gc_refs)) * ycv).astype(BF16)
        z = _dot(merged, wout_ref[...])
        x2_ref[...] = x1_ref[...] + g2_ref[0] * z
        z_ref[...] = z.astype(BF16)
        ya_ref[...] = ya.astype(BF16)
        ycv_ref[...] = ycv.astype(BF16)
        cact_ref[...] = cact
        mrg_ref[...] = merged

    row = pl.BlockSpec((tm, D), lambda i: (i, 0))
    vec = pl.BlockSpec((1, D), lambda i: (0, 0))
    per_b = pl.BlockSpec((1, 1, D), lambda i: (i // tps, 0, 0))
    wspec = pl.BlockSpec((D, D), lambda i: (0, 0))
    gates = _gate_specs(tm, D, ga_col) + _gate_specs(tm, D, gc_col)
    blocks = ([((tm, D), BF16), ((tm, D), F32), ((tm, D), BF16), ((tm, D), BF16)] + [((D, D), BF16)] * 3
              + [((tm, D), F32)] * 2 + [((tm, D), BF16)] * 5)
    return _call(
        body, (ao, yc, *[proj] * (2 * ng), wao, wco, wout, x1, g2.arr, lg, lb), name=name, grid=(T // tm,),
        in_specs=[row, row, *gates, wspec, wspec, wspec, row, g2.spec(tps, 1), vec, vec],
        out_specs=[row] * 6,
        out_shape=[SDS((T, D), F32)] + [SDS((T, D), BF16)] * 5,
        params=_params(1, blocks, temp_bytes=8 * _nbytes((tm, D), F32)), comm=comm)


def _mix_out_bwd(dx2, g2, z, wout, proj, ya, ycv, wao, wco, yc, lg, lb, *, seq, tm, ga_col, gc_col, name,
                 comm=None):
    T, D = dx2.shape
    tps = seq // tm
    nb = T // seq
    ng = D // GATE_W

    def body(dx2_ref, g2_ref, z_ref, wout_ref, *rest):
        ga_refs, gc_refs = rest[:ng], rest[ng:2 * ng]
        (ya_ref, ycv_ref, wao_ref, wco_ref, yc_ref, lg_ref, lb_ref, dz_ref, dya_ref, dycv_ref, dga_ref, dgc_ref,
         dao_ref, dyc_ref, dg2_ref, dlg_ref, dlb_ref) = rest[2 * ng:]
        i = pl.program_id(0)
        dx = dx2_ref[...]
        _acc(dg2_ref, _rowsum(dx * z_ref[...].astype(F32))[None], i % tps == 0)
        dzb = (g2_ref[0] * dx).astype(BF16)
        dz_ref[...] = dzb
        dmerged = _dot_nt(dzb, wout_ref[...])
        sa = _sigmoid(_gate(ga_refs))
        sc_ = _sigmoid(_gate(gc_refs))
        dya = (dmerged * sa).astype(BF16)
        dycv = (dmerged * sc_).astype(BF16)
        dya_ref[...] = dya
        dycv_ref[...] = dycv
        dga_ref[...] = (dmerged * ya_ref[...].astype(F32) * (sa * (1.0 - sa))).astype(BF16)
        dgc_ref[...] = (dmerged * ycv_ref[...].astype(F32) * (sc_ * (1.0 - sc_))).astype(BF16)
        dao_ref[...] = _dot_nt(dya, wao_ref[...]).astype(BF16)
        dcact = _dot_nt(dycv, wco_ref[...])
        lgv = lg_ref[...]
        _, xh, rstd, l, s = _layernorm_silu(yc_ref[...], lgv, lb_ref[...])
        dl = dcact * (s * (1.0 + l * (1.0 - s)))
        _acc(dlb_ref, _rowsum(dl), i == 0)
        _acc(dlg_ref, _rowsum(dl * xh), i == 0)
        dxh = dl * lgv
        dyc_ref[...] = rstd * (dxh - jnp.mean(dxh, axis=-1, keepdims=True)
                               - xh * jnp.mean(dxh * xh, axis=-1, keepdims=True))

    row = pl.BlockSpec((tm, D), lambda i: (i, 0))
    vec = pl.BlockSpec((1, D), lambda i: (0, 0))
    per_b = pl.BlockSpec((1, 1, D), lambda i: (i // tps, 0, 0))
    wspec = pl.BlockSpec((D, D), lambda i: (0, 0))
    gates = _gate_specs(tm, D, ga_col) + _gate_specs(tm, D, gc_col)
    blocks = ([((tm, D), F32)] * 3 + [((tm, D), BF16)] * 11 + [((D, D), BF16)] * 3)
    return _call(
        body, (dx2, g2.arr, z, wout, *[proj] * (2 * ng), ya, ycv, wao, wco, yc, lg, lb), name=name,
        grid=(T // tm,),
        in_specs=[row, g2.spec(tps, 1), row, wspec, *gates, row, row, wspec, wspec, row, vec, vec],
        out_specs=[row] * 7 + [per_b, vec, vec],
        out_shape=[SDS((T, D), BF16)] * 6 + [SDS((T, D), F32), SDS((nb, 1, D), F32), SDS((1, D), F32),
                                             SDS((1, D), F32)],
        params=_params(1, blocks, temp_bytes=10 * _nbytes((tm, D), F32)), comm=comm)


GROUP_ROWS = GQA_GROUP * ATT_BLOCK
PAIR_W = 2 * HEAD_DIM
GROUP_W = GQA_GROUP * HEAD_DIM


def _lane_lo():
    return lax.broadcasted_iota(jnp.int32, (1, PAIR_W), 1) < HEAD_DIM


def _band_bias():
    sj = lax.broadcasted_iota(jnp.int32, (2 * ATT_BLOCK, GROUP_ROWS), 0)
    qi = lax.broadcasted_iota(jnp.int32, (2 * ATT_BLOCK, GROUP_ROWS), 1) & (ATT_BLOCK - 1)
    rel = qi + ATT_BLOCK - sj
    bias = jnp.where(jnp.logical_and(rel >= 0, rel < ATT_BLOCK), 0.0, NEG_BIG)
    sj1 = lax.broadcasted_iota(jnp.int32, (2 * ATT_BLOCK, 1), 0)
    return bias, jnp.where(sj1 < ATT_BLOCK, NEG_BIG, 0.0)


def _dup_heads(src_ref, dst, seq):
    x = src_ref[...]
    i = lax.broadcasted_iota(jnp.int32, (KV_WIDTH, PAIR_W), 0)
    j = lax.broadcasted_iota(jnp.int32, (KV_WIDTH, PAIR_W), 1) & (HEAD_DIM - 1)
    for g in range(N_KV_HEADS):
        sel = jnp.where(i == j + g * HEAD_DIM, 1.0, 0.0).astype(BF16)
        dst[g, pl.ds(0, ATT_BLOCK), :] = jnp.zeros((ATT_BLOCK, PAIR_W), BF16)
        dst[g, pl.ds(ATT_BLOCK, seq), :] = _dot(x, sel).astype(BF16)


def _stack_heads(blk, g, lo):
    parts = []
    for p in range(GQA_GROUP // 2):
        pair = blk[:, g * GROUP_W + p * PAIR_W:g * GROUP_W + (p + 1) * PAIR_W]
        parts += [jnp.where(lo, pair, jnp.zeros_like(pair)), jnp.where(lo, jnp.zeros_like(pair), pair)]
    return jnp.concatenate(parts, axis=0)


def _unstack_heads(full, ref, r0, g, lo):
    for p in range(GQA_GROUP // 2):
        even = full[(2 * p) * ATT_BLOCK:(2 * p + 1) * ATT_BLOCK, :]
        odd = full[(2 * p + 1) * ATT_BLOCK:(2 * p + 2) * ATT_BLOCK, :]
        ref[pl.ds(r0, ATT_BLOCK), g * GROUP_W + p * PAIR_W:g * GROUP_W + (p + 1) * PAIR_W] = (
            jnp.where(lo, even, odd).astype(ref.dtype))


def _sink_row(sink_ref, g):
    return jnp.concatenate([jnp.full((1, ATT_BLOCK), sink_ref[0, g * GQA_GROUP + h], F32)
                            for h in range(GQA_GROUP)], axis=1)


def _group_probs(qs, k2, bias, sink):
    s = _dot_nt(k2, qs) * (HEAD_DIM ** -0.5) + bias
    m = jnp.maximum(jnp.max(s, axis=0, keepdims=True), sink)
    p = jnp.exp(s - m)
    psink = jnp.exp(sink - m)
    inv = 1.0 / (jnp.sum(p, axis=0, keepdims=True) + psink)
    return p * inv, psink * inv


def _attn_fwd(projp, sinks, *, seq, q_blk, k_blk, v_blk, name, comm=None):
    T = projp.shape[0]
    QW = N_Q_HEADS * HEAD_DIM
    nblk = seq // ATT_BLOCK

    def body(q_ref, k_ref, v_ref, sink_ref, o_ref, k2s, v2s):
        _dup_heads(k_ref, k2s, seq)
        _dup_heads(v_ref, v2s, seq)
        lo = _lane_lo()
        bias0, first_pen = _band_bias()
        sink_rows = [_sink_row(sink_ref, g) for g in range(N_KV_HEADS)]

        def blk(n, carry):
            r0 = pl.multiple_of(n * ATT_BLOCK, ATT_BLOCK)
            qb = q_ref[pl.ds(r0, ATT_BLOCK), :]
            bias = bias0 + jnp.where(n == 0, 1.0, 0.0) * first_pen
            for g in range(N_KV_HEADS):
                probs_t, _ = _group_probs(_stack_heads(qb, g, lo), k2s[g, pl.ds(r0, 2 * ATT_BLOCK), :], bias,
                                          sink_rows[g])
                _unstack_heads(_dot_tn(probs_t.astype(BF16), v2s[g, pl.ds(r0, 2 * ATT_BLOCK), :]), o_ref, r0, g, lo)
            return carry

        lax.fori_loop(0, nblk, blk, 0)

    blocks = [((seq, QW), BF16)] * 2 + [((seq, KV_WIDTH), BF16)] * 2
    return _call(
        body, (projp, projp, projp, sinks), name=name, grid=(T // seq,),
        in_specs=[pl.BlockSpec((seq, QW), lambda b: (b, q_blk)),
                  pl.BlockSpec((seq, KV_WIDTH), lambda b: (b, k_blk)),
                  pl.BlockSpec((seq, KV_WIDTH), lambda b: (b, v_blk)),
                  pl.BlockSpec(memory_space=pltpu.SMEM)],
        out_specs=[pl.BlockSpec((seq, QW), lambda b: (b, 0))],
        out_shape=[SDS((T, QW), BF16)],
        scratch_shapes=[pltpu.VMEM((N_KV_HEADS, seq + ATT_BLOCK, PAIR_W), BF16)] * 2,
        params=_params(1, blocks, temp_bytes=16 * 2**20), comm=comm)[0]


def _attn_bwd(projp, dao, sinks, *, seq, q_blk, k_blk, v_blk, name, comm=None):
    T = projp.shape[0]
    QW = N_Q_HEADS * HEAD_DIM
    nblk = seq // ATT_BLOCK

    def body(q_ref, k_ref, v_ref, do_ref, sink_ref, dq_ref, dk_ref, dv_ref, dsink_ref, k2s, v2s, dkacc, dvacc):
        _dup_heads(k_ref, k2s, seq)
        _dup_heads(v_ref, v2s, seq)
        dkacc[...] = jnp.zeros(dkacc.shape, F32)
        dvacc[...] = jnp.zeros(dvacc.shape, F32)
        lane = lax.broadcasted_iota(jnp.int32, (1, PAIR_W), 1)
        lo = lane < HEAD_DIM
        bias0, first_pen = _band_bias()
        sink_rows = [_sink_row(sink_ref, g) for g in range(N_KV_HEADS)]

        def blk(n, dsink):
            r0 = pl.multiple_of(n * ATT_BLOCK, ATT_BLOCK)
            band = pl.ds(r0, 2 * ATT_BLOCK)
            qb = q_ref[pl.ds(r0, ATT_BLOCK), :]
            dob = do_ref[pl.ds(r0, ATT_BLOCK), :]
            bias = bias0 + jnp.where(n == 0, 1.0, 0.0) * first_pen
            for g in range(N_KV_HEADS):
                qs = _stack_heads(qb, g, lo)
                dos = _stack_heads(dob, g, lo)
                k2 = k2s[g, band, :]
                v2 = v2s[g, band, :]
                probs_t, psink = _group_probs(qs, k2, bias, sink_rows[g])
                dp_t = _dot_nt(v2, dos)
                delta = jnp.sum(probs_t * dp_t, axis=0, keepdims=True)
                ds_t = (probs_t * (dp_t - delta) * (HEAD_DIM ** -0.5)).astype(BF16)
                tsink = psink * delta
                for h in range(GQA_GROUP):
                    dsink = dsink + jnp.where(lane == g * GQA_GROUP + h,
                                              -jnp.sum(tsink[:, h * ATT_BLOCK:(h + 1) * ATT_BLOCK]), 0.0)
                _unstack_heads(_dot_tn(ds_t, k2), dq_ref, r0, g, lo)
                dkacc[g, band, :] = dkacc[g, band, :] + _dot(ds_t, qs)
                dvacc[g, band, :] = dvacc[g, band, :] + _dot(probs_t.astype(BF16), dos)
            return dsink

        dsink = lax.fori_loop(0, nblk, blk, jnp.zeros((1, PAIR_W), F32))
        _acc(dsink_ref, dsink, pl.program_id(0) == 0)

        def fold(acc, g):
            a = acc[g, pl.ds(ATT_BLOCK, seq), :]
            return a + pltpu.roll(a, HEAD_DIM, 1)

        dk_ref[...] = jnp.where(lo, fold(dkacc, 0), fold(dkacc, 1)).astype(BF16)
        dv_ref[...] = jnp.where(lo, fold(dvacc, 0), fold(dvacc, 1)).astype(BF16)

    blocks = [((seq, QW), BF16)] * 3 + [((seq, KV_WIDTH), BF16)] * 4
    kv_spec_out = pl.BlockSpec((seq, KV_WIDTH), lambda b: (b, 0))
    return _call(
        body, (projp, projp, projp, dao, sinks), name=name, grid=(T // seq,),
        in_specs=[pl.BlockSpec((seq, QW), lambda b: (b, q_blk)),
                  pl.BlockSpec((seq, KV_WIDTH), lambda b: (b, k_blk)),
                  pl.BlockSpec((seq, KV_WIDTH), lambda b: (b, v_blk)),
                  pl.BlockSpec((seq, QW), lambda b: (b, 0)),
                  pl.BlockSpec(memory_space=pltpu.SMEM)],
        out_specs=[pl.BlockSpec((seq, QW), lambda b: (b, 0)), kv_spec_out, kv_spec_out,
                   pl.BlockSpec((1, 128), lambda b: (0, 0))],
        out_shape=[SDS((T, QW), BF16), SDS((T, KV_WIDTH), BF16), SDS((T, KV_WIDTH), BF16), SDS((1, 128), F32)],
        scratch_shapes=[pltpu.VMEM((N_KV_HEADS, seq + ATT_BLOCK, PAIR_W), BF16)] * 2
        + [pltpu.VMEM((N_KV_HEADS, seq + ATT_BLOCK, PAIR_W), F32)] * 2,
        params=_params(1, blocks, temp_bytes=24 * 2**20), comm=comm)


SUBLANES = 8


def _sublane_shifts(win):
    n = CONV_ROWS + CONV_HALO
    return [win] + [pltpu.roll(win, n - b, 0) for b in range(1, SUBLANES)]


def _window(shifted, off):
    a = off // SUBLANES * SUBLANES
    return shifted[off % SUBLANES][a:a + CONV_ROWS, :]


def _conv_fwd(projp, w, bias, *, seq, cw, a_col, b_col, name, comm=None):
    T = projp.shape[0]
    C = w.shape[1]
    nchunk = seq // CONV_ROWS

    def body(a_ref, b_ref, w_ref, bias_ref, y_ref, upad):
        upad[pl.ds(0, CONV_HALO), :] = jnp.zeros((CONV_HALO, cw), F32)
        upad[pl.ds(CONV_HALO, seq), :] = a_ref[...].astype(F32) * _sigmoid(b_ref[...].astype(F32))
        wv = w_ref[...]
        bv = bias_ref[...]

        def chunk(r, carry):
            r0 = pl.multiple_of(r * CONV_ROWS, CONV_ROWS)
            shifted = _sublane_shifts(upad[pl.ds(r0, CONV_ROWS + CONV_HALO), :])
            acc = jnp.broadcast_to(bv, (CONV_ROWS, cw))
            for k in range(CONV_WIDTH):
                acc = acc + wv[k:k + 1, :] * _window(shifted, CONV_HALO - (CONV_WIDTH - 1) + k)
            y_ref[pl.ds(r0, CONV_ROWS), :] = acc
            return carry

        lax.fori_loop(0, nchunk, chunk, 0)

    blocks = [((seq, cw), BF16)] * 2 + [((seq, cw), F32)]
    return _call(
        body, (projp, projp, w, bias), name=name, grid=(T // seq, C // cw),
        in_specs=[pl.BlockSpec((seq, cw), lambda b, c: (b, a_col // cw + c)),
                  pl.BlockSpec((seq, cw), lambda b, c: (b, b_col // cw + c)),
                  pl.BlockSpec((CONV_WIDTH, cw), lambda b, c: (0, c)),
                  pl.BlockSpec((1, cw), lambda b, c: (0, c))],
        out_specs=[pl.BlockSpec((seq, cw), lambda b, c: (b, c))],
        out_shape=[SDS((T, C), F32)],
        scratch_shapes=[pltpu.VMEM((seq + CONV_HALO, cw), F32)],
        params=_params(2, blocks, temp_bytes=6 * _nbytes((seq, cw), F32)), comm=comm)[0]


def _conv_bwd(dy, projp, w, *, seq, cw, a_col, b_col, name, comm=None):
    T = projp.shape[0]
    C = w.shape[1]
    nchunk = seq // CONV_ROWS
    SUB = 8

    def body(dy_ref, a_ref, b_ref, w_ref, da_ref, db_ref, dw_ref, dbias_ref, dypad, dwp):
        first = pl.program_id(1) == 0
        dyv = dy_ref[...]
        dypad[pl.ds(0, seq), :] = dyv
        dypad[pl.ds(seq, CONV_HALO), :] = jnp.zeros((CONV_HALO, cw), F32)
        dwp[...] = jnp.zeros(dwp.shape, F32)
        wv = w_ref[...]

        def chunk(r, carry):
            r0 = pl.multiple_of(r * CONV_ROWS, CONV_ROWS)
            dy_shifts = _sublane_shifts(dypad[pl.ds(r0, CONV_ROWS + CONV_HALO), :])
            ac = a_ref[pl.ds(r0, CONV_ROWS), :].astype(F32)
            sbc = _sigmoid(b_ref[pl.ds(r0, CONV_ROWS), :].astype(F32))
            uc = ac * sbc
            du = jnp.zeros((CONV_ROWS, cw), F32)
            for k in range(CONV_WIDTH):
                dyk = _window(dy_shifts, CONV_WIDTH - 1 - k)
                du = du + wv[k:k + 1, :] * dyk
                prod = uc * dyk
                part = prod[0:SUB, :]
                for s in range(1, CONV_ROWS // SUB):
                    part = part + prod[s * SUB:(s + 1) * SUB, :]
                dwp[pl.ds(k * SUB, SUB), :] = dwp[pl.ds(k * SUB, SUB), :] + part
            da_ref[pl.ds(r0, CONV_ROWS), :] = (du * sbc).astype(BF16)
            db_ref[pl.ds(r0, CONV_ROWS), :] = (du * ac * (sbc * (1.0 - sbc))).astype(BF16)
            return carry

        lax.fori_loop(0, nchunk, chunk, 0)

        @pl.when(first)
        def _():
            dw_ref[...] = jnp.zeros(dw_ref.shape, F32)
            dbias_ref[...] = jnp.zeros(dbias_ref.shape, F32)

        for k in range(CONV_WIDTH):
            dw_ref[k:k + 1, :] = dw_ref[k:k + 1, :] + _rowsum(dwp[pl.ds(k * SUB, SUB), :])
        dbias_ref[...] = dbias_ref[...] + _rowsum(dyv)

    blocks = [((seq, cw), F32)] + [((seq, cw), BF16)] * 4
    return _call(
        body, (dy, projp, projp, w), name=name, grid=(C // cw, T // seq),
        in_specs=[pl.BlockSpec((seq, cw), lambda c, b: (b, c)),
                  pl.BlockSpec((seq, cw), lambda c, b: (b, a_col // cw + c)),
                  pl.BlockSpec((seq, cw), lambda c, b: (b, b_col // cw + c)),
                  pl.BlockSpec((CONV_WIDTH, cw), lambda c, b: (0, c))],
        out_specs=[pl.BlockSpec((seq, cw), lambda c, b: (b, c)), pl.BlockSpec((seq, cw), lambda c, b: (b, c)),
                   pl.BlockSpec((CONV_WIDTH, cw), lambda c, b: (0, c)), pl.BlockSpec((1, cw), lambda c, b: (0, c))],
        out_shape=[SDS((T, C), BF16), SDS((T, C), BF16), SDS((CONV_WIDTH, C), F32), SDS((1, C), F32)],
        scratch_shapes=[pltpu.VMEM((seq + CONV_HALO, cw), F32), pltpu.VMEM((CONV_WIDTH * SUB, cw), F32)],
        params=_params(2, blocks, temp_bytes=8 * _nbytes((seq, cw), F32)), comm=comm)


def _matmul_tn(a, b, *, name, comm=None):
    T, M = a.shape
    N = b.shape[1]
    bm = _pick(M, (768, 512, 256))

    def body(a_ref, b_ref, o_ref):
        o_ref[...] = _dot_tn(a_ref[...], b_ref[...]).astype(BF16)

    blocks = [((T, bm), BF16), ((T, N), BF16), ((bm, N), BF16)]
    return _call(
        body, (a, b), name=name, grid=(M // bm,),
        in_specs=[pl.BlockSpec((T, bm), lambda i: (0, i)), pl.BlockSpec((T, N), lambda i: (0, 0))],
        out_specs=[pl.BlockSpec((bm, N), lambda i: (i, 0))],
        out_shape=[SDS((M, N), BF16)],
        params=_params(1, blocks, temp_bytes=2 * _nbytes((T, bm), BF16) + 2 * _nbytes((bm, N), F32)),
        comm=comm)[0]


def _sum_parts(p_ref):
    g = p_ref[0].astype(F32)
    for s in range(1, p_ref.shape[0]):
        g = g + p_ref[s].astype(F32)
    return g


def _pair_add(g, staged, *, name):
    _, R, W = g.shape
    nq = staged.shape[0]
    tr = _row_tile(R)

    def body(g_ref, s_ref, o_ref):
        mine = jnp.where(lax.axis_index("c") == 0, g_ref[0, 0].astype(F32), g_ref[0, 1].astype(F32))
        o_ref[0] = (mine + s_ref[0].astype(F32)).astype(o_ref.dtype)

    return _call(
        body, (g.reshape(nq, 2, R, W), staged), name=name, grid=(nq, R // tr),
        in_specs=[pl.BlockSpec((1, 2, tr, W), lambda q, i: (q, 0, i, 0)),
                  pl.BlockSpec((1, tr, W), lambda q, i: (q, i, 0))],
        out_specs=[pl.BlockSpec((1, tr, W), lambda q, i: (q, i, 0))],
        out_shape=[SDS((nq, R, W), g.dtype)],
        params=_params(2, [((4, tr, W), g.dtype)], temp_bytes=3 * _nbytes((tr, W), F32)))[0]


def _adamw_update(w, g, m, v):
    m = ADAM_B1 * m + (1.0 - ADAM_B1) * g
    v = ADAM_B2 * v + (1.0 - ADAM_B2) * (g * g)
    m_hat = m / (1.0 - ADAM_B1 ** ADAM_STEP)
    v_hat = v / (1.0 - ADAM_B2 ** ADAM_STEP)
    delta = -ADAM_LR * (m_hat / (jnp.sqrt(v_hat) + ADAM_EPS) + ADAM_WD * w)
    return delta, m, v


def _row_tile(R):
    return _pick(R, (256, 128, 112, 88, 64, 32, 16, 8))


def _sum8(parts, *, name):
    n, R, W = parts.shape
    tr = _row_tile(R)

    def body(p_ref, o_ref):
        o_ref[...] = _sum_parts(p_ref)

    return _call(
        body, (parts,), name=name, grid=(R // tr,),
        in_specs=[pl.BlockSpec((n, tr, W), lambda i: (0, i, 0))],
        out_specs=[pl.BlockSpec((tr, W), lambda i: (i, 0))],
        out_shape=[SDS((R, W), F32)],
        params=_params(1, [((n, tr, W), parts.dtype), ((tr, W), F32)]))[0]


def _adamw(g, w, m, v, *, name):
    R, W = w.shape
    tr = _row_tile(R)

    def body(g_ref, w_ref, m_ref, v_ref, d_ref, mo_ref, vo_ref):
        d_ref[...], mo_ref[...], vo_ref[...] = _adamw_update(w_ref[...], g_ref[...], m_ref[...], v_ref[...])

    spec = pl.BlockSpec((tr, W), lambda i: (i, 0))
    return _call(
        body, (g, w, m, v), name=name, grid=(R // tr,),
        in_specs=[spec] * 4, out_specs=[spec] * 3, out_shape=[SDS((R, W), F32)] * 3,
        params=_params(1, [((tr, W), F32)] * 7))


def _sum8_adamw(parts, w, m, v, *, name):
    R, W = w.shape
    n = parts.shape[0]
    tr = _row_tile(R)

    def body(p_ref, w_ref, m_ref, v_ref, g_ref, d_ref, mo_ref, vo_ref):
        g = _sum_parts(p_ref)
        g_ref[...] = g
        d_ref[...], mo_ref[...], vo_ref[...] = _adamw_update(w_ref[...], g, m_ref[...], v_ref[...])

    spec = pl.BlockSpec((tr, W), lambda i: (i, 0))
    return _call(
        body, (parts, w, m, v), name=name, grid=(R // tr,),
        in_specs=[pl.BlockSpec((n, tr, W), lambda i: (0, i, 0))] + [spec] * 3,
        out_specs=[spec] * 4, out_shape=[SDS((R, W), F32)] * 4,
        params=_params(1, [((n, tr, W), parts.dtype)] + [((tr, W), F32)] * 7))


def _ada_fwd(c_all, w, bias, *, name):
    NB, D = c_all.shape
    N = w.shape[1]

    def body(c_ref, w_ref, b_ref, o_ref):
        cv = c_ref[...]
        ca = (cv * _sigmoid(cv)).astype(BF16)
        o_ref[...] = _dot(ca, w_ref[...].astype(BF16)) + b_ref[...]

    full = lambda s: pl.BlockSpec(s, lambda i: (0,) * len(s))
    return _call(
        body, (c_all, w, bias), name=name, grid=(1,),
        in_specs=[full((NB, D)), full((D, N)), full((1, N))], out_specs=[full((NB, N))],
        out_shape=[SDS((NB, N), F32)],
        params=_params(1, [((D, N), F32)], temp_bytes=_nbytes((D, N), BF16)))[0]


def _ada_bwd(c_all, gmod_all, *, n_col, name):
    NB, D = c_all.shape
    N = gmod_all.shape[1]

    def body(c_ref, g_ref, gw_ref, gb_ref):
        cv = c_ref[...]
        ca = (cv * _sigmoid(cv)).astype(BF16)
        first = pl.multiple_of(_lin(_my_pos()) * n_col, 128)
        gw_ref[...] = _dot_tn(ca, g_ref[:, pl.ds(first, n_col)].astype(BF16))
        gb_ref[...] = _rowsum(g_ref[...])

    full = lambda s: pl.BlockSpec(s, lambda i: (0,) * len(s))
    return _call(
        body, (c_all, gmod_all), name=name, grid=(1,),
        in_specs=[full((NB, D)), full((NB, N))], out_specs=[full((D, n_col)), full((1, N))],
        out_shape=[SDS((D, n_col), F32), SDS((1, N), F32)],
        params=_params(1, [((D, n_col), F32), ((NB, N), F32)]))


def kernel(x, c, w_ada, b_ada, norm_ffn1_g, ffn1_w_gate, ffn1_w_up, ffn1_w_down, norm_mix_g, w_in, attn_sinks, w_attn_o, conv_w_dw, conv_b_dw, conv_ln_g, conv_ln_b, w_conv_o, w_out, norm_ffn2_g, ffn2_w_gate, ffn2_w_up, ffn2_w_down, final_norm_g, loss_target, m_w_ada, m_b_ada, m_norm_ffn1_g, m_ffn1_w_gate, m_ffn1_w_up, m_ffn1_w_down, m_norm_mix_g, m_w_in, m_attn_sinks, m_w_attn_o, m_conv_w_dw, m_conv_b_dw, m_conv_ln_g, m_conv_ln_b, m_w_conv_o, m_w_out, m_norm_ffn2_g, m_ffn2_w_gate, m_ffn2_w_up, m_ffn2_w_down, m_final_norm_g, v_w_ada, v_b_ada, v_norm_ffn1_g, v_ffn1_w_gate, v_ffn1_w_up, v_ffn1_w_down, v_norm_mix_g, v_w_in, v_attn_sinks, v_w_attn_o, v_conv_w_dw, v_conv_b_dw, v_conv_ln_g, v_conv_ln_b, v_w_conv_o, v_w_out, v_norm_ffn2_g, v_ffn2_w_gate, v_ffn2_w_up, v_ffn2_w_down, v_final_norm_g):
    B, S, D = x.shape
    T = B * S
    QW = N_Q_HEADS * HEAD_DIM
    CC = conv_w_dw.shape[2] * N_DEV
    me = _lin(_my_pos())
    xf = x.reshape(T, D)
    tgt = loss_target.reshape(T, D)
    tm = min(512, S)
    kw = dict(seq=S, tm=tm)

    p_k, p_v, p_ca = QW, QW + KV_WIDTH, QW + 2 * KV_WIDTH
    p_cb, p_ga, p_gc = p_ca + CC, p_ca + 2 * CC, p_ca + 2 * CC + D

    def col_t(w):
        return w[0].T.astype(BF16)

    def row_b(w):
        return w[0].astype(BF16)

    def rows(g):
        return g.reshape(-1, g.shape[-1])

    def blocks8(g):
        return g.reshape(N_DEV, g.shape[0] // N_DEV, g.shape[1])

    def gather(*arrs):
        return _Comm([(a, "gather") for a in arrs])

    g_wg1, g_convw, g_c = _exchange(
        [(col_t(ffn1_w_gate), "gather"), (conv_w_dw[0], "gather"), (c, "gather")], name="gather_first")
    wg1 = rows(g_wg1)
    conv_w = g_convw.transpose(1, 0, 2).reshape(CONV_WIDTH, CC)
    c_all = g_c.reshape(N_DEV * B, D)

    n_col = N_MOD * D // N_DEV
    b_cols = lax.dynamic_slice(b_ada, (0, me * n_col), (1, n_col))
    mod_cols = _ada_fwd(c_all, w_ada[0], b_cols, name="ada_fwd")
    mod_mine = _exchange([(mod_cols.reshape(N_DEV, B, n_col), "scatter")], name="scatter_mod")[0]
    mod = mod_mine.transpose(1, 0, 2).reshape(B * N_MOD, 1, D)
    sh1, sc1, g1, sh2, sc2, g2, sh3, sc3, g3 = [_ModVec(mod, i) for i in range(N_MOD)]

    F = wg1.shape[0]
    tn_f = _pick(F, (1408, 1024, 512, 256))
    tn_in = _pick(w_in.shape[2] * N_DEV, (1792, 768, 512, 256))
    gate_blk = dict(ga_col=p_ga, gc_col=p_gc)
    att_blk = dict(q_blk=0, k_blk=p_k // KV_WIDTH, v_blk=p_v // KV_WIDTH)
    conv_kw = dict(seq=S, cw=256, a_col=p_ca, b_col=p_cb)

    cm = gather(col_t(ffn1_w_up))
    h1, (a1,) = _norm_mod_matmul(xf, norm_ffn1_g, sh1, sc1, [wg1], tn=tn_f, name="ffn1_gate", comm=cm, **kw)
    wu1 = rows(cm.out[0])
    cm = gather(row_b(ffn1_w_down))
    b1 = _matmul_nt(h1, wu1, tm=tm, tn=tn_f, name="ffn1_up", comm=cm)
    wd1 = rows(cm.out[0])
    cm = gather(col_t(w_in))
    x1, y1 = _ffn_down(a1, b1, wd1, xf, g1, name="ffn1_down", comm=cm, **kw)
    winp = rows(cm.out[0])
    cm = gather(row_b(w_attn_o), row_b(w_conv_o), row_b(w_out), col_t(ffn2_w_gate))
    h2, (projp,) = _norm_mod_matmul(x1, norm_mix_g, sh2, sc2, [winp], tn=tn_in, name="mix_in", comm=cm, **kw)
    wao, wco, wout, wg2 = [rows(o) for o in cm.out]
    cm = gather(col_t(ffn2_w_up))
    ao = _attn_fwd(projp, attn_sinks, seq=S, name="attn_fwd", comm=cm, **att_blk)
    wu2 = rows(cm.out[0])
    cm = gather(row_b(ffn2_w_down))
    yc = _conv_fwd(projp, conv_w, conv_b_dw, name="conv_fwd", comm=cm, **conv_kw)
    wd2 = rows(cm.out[0])
    x2, z, ya, ycv, cact, merged = _mix_out(ao, yc, projp, wao, wco, wout, x1, g2, conv_ln_g, conv_ln_b,
                                            name="mix_out", **gate_blk, **kw)
    h3, (a3, b3) = _norm_mod_matmul(x2, norm_ffn2_g, sh3, sc3, [wg2, wu2], tn=tn_f, name="ffn2_up", **kw)
    x3, y3 = _ffn_down(a3, b3, wd2, x2, g3, name="ffn2_down", **kw)
    dx3, loss_row, dgf = _final_loss(x3, final_norm_g[None], tgt, tm=tm, name="final_loss")

    parts = {}

    def pair(*gs):
        return [(blocks8(g), "pair") for g in gs]

    def cross(*rs):
        return [(r, "cross") for r in rs]

    def reduce_pairs(gs, staged, names):
        return [_pair_add(blocks8(g), s, name="pair_add_" + n) for g, s, n in zip(gs, staged, names)]

    dyb3, da3, db3, act3, dg3 = _ffn_bwd_down(dx3, g3, y3, wd2, a3, b3, tn=tn_f, name="ffn2_bwd_down", **kw)
    gwd2 = _matmul_tn(act3, dyb3, name="gw_ffn2_down")
    cm = _Comm(pair(gwd2))
    dx2, dsh3, dsc3, dgn3 = _matmul_norm_mod_bwd([da3, db3], [wg2, wu2], x2, norm_ffn2_g, sc3, dx3,
                                                 name="ffn2_bwd_up", comm=cm, **kw)
    r_wd2, = reduce_pairs([gwd2], cm.out, ["ffn2_w_down"])
    cm = _Comm(cross(r_wd2))
    gwg2 = _matmul_tn(da3, h3, name="gw_ffn2_gate", comm=cm)
    parts["ffn2_w_down"], = cm.out
    cm = _Comm(pair(gwg2))
    gwu2 = _matmul_tn(db3, h3, name="gw_ffn2_up", comm=cm)
    r_wg2, = reduce_pairs([gwg2], cm.out, ["ffn2_w_gate"])

    cm = _Comm(cross(r_wg2) + pair(gwu2))
    dzb, dyab, dycb, dga, dgc, dao, dyc, dg2, dlng, dlnb = _mix_out_bwd(
        dx2, g2, z, wout, projp, ya, ycv, wao, wco, yc, conv_ln_g, conv_ln_b, name="mix_out_bwd", comm=cm,
        **gate_blk, **kw)
    parts["ffn2_w_gate"] = cm.out[0]
    r_wu2, = reduce_pairs([gwu2], cm.out[1:], ["ffn2_w_up"])
    gwout = _matmul_tn(merged, dzb, name="gw_out")
    gwao = _matmul_tn(ao, dyab, name="gw_attn_o")
    gwco = _matmul_tn(cact, dycb, name="gw_conv_o")
    cm = _Comm(cross(r_wu2) + pair(gwout, gwao, gwco))
    dq, dk, dv, dsinks = _attn_bwd(projp, dao, attn_sinks, seq=S, name="attn_bwd", comm=cm, **att_blk)
    parts["ffn2_w_up"] = cm.out[0]
    r_mix = reduce_pairs([gwout, gwao, gwco], cm.out[1:], ["w_out", "w_attn_o", "w_conv_o"])
    cm = _Comm(cross(*r_mix))
    dca, dcb, dconvw, dconvb = _conv_bwd(dyc, projp, conv_w, name="conv_bwd", comm=cm, **conv_kw)
    parts["w_out"], parts["w_attn_o"], parts["w_conv_o"] = cm.out
    dprojp = jnp.concatenate([dq, dk, dv, dca, dcb, dga, dgc], axis=1)
    gwin = _matmul_tn(dprojp, h2, name="gw_in")
    cm = _Comm(pair(gwin))
    dx1, dsh2, dsc2, dgn2 = _matmul_norm_mod_bwd([dprojp], [winp], x1, norm_mix_g, sc2, dx2,
                                                 name="mix_in_bwd", comm=cm, **kw)
    r_win, = reduce_pairs([gwin], cm.out, ["w_in"])

    cm = _Comm(cross(r_win))
    dyb1, da1, db1, act1, dg1 = _ffn_bwd_down(dx1, g1, y1, wd1, a1, b1, tn=tn_f, name="ffn1_bwd_down", comm=cm,
                                              **kw)
    parts["w_in"], = cm.out
    gwd1 = _matmul_tn(act1, dyb1, name="gw_ffn1_down")
    cm = _Comm(pair(gwd1))
    gwg1 = _matmul_tn(da1, h1, name="gw_ffn1_gate", comm=cm)
    r_wd1, = reduce_pairs([gwd1], cm.out, ["ffn1_w_down"])
    cm = _Comm(cross(r_wd1) + pair(gwg1))
    gwu1 = _matmul_tn(db1, h1, name="gw_ffn1_up", comm=cm)
    parts["ffn1_w_down"] = cm.out[0]
    r_wg1, = reduce_pairs([gwg1], cm.out[1:], ["ffn1_w_gate"])
    r_wu1, = reduce_pairs([gwu1], _exchange(pair(gwu1), name="pair_last"), ["ffn1_w_up"])
    cm = _Comm(cross(r_wg1, r_wu1))
    dx0, dsh1, dsc1, dgn1 = _matmul_norm_mod_bwd([da1, db1], [wg1, wu1], xf, norm_ffn1_g, sc1, dx1,
                                                 name="ffn1_bwd_up", comm=cm, **kw)
    parts["ffn1_w_gate"], parts["ffn1_w_up"] = cm.out

    n_small = 8
    gmod = jnp.concatenate([dsh1, dsc1, dg1, dsh2, dsc2, dg2, dsh3, dsc3, dg3], axis=1).reshape(B, N_MOD * D)
    sink_row = jnp.pad(dsinks[:, :N_Q_HEADS], ((0, 0), (0, D - N_Q_HEADS)))
    loss_pad = jnp.pad(loss_row, ((0, 0), (0, D - loss_row.shape[1])))
    small = jnp.concatenate([dgn1, dgn2, dgn3, dgf, dconvb, dlng, dlnb, sink_row, dconvw, loss_pad], axis=0)
    small_all, gmod_all = _exchange([(small, "gather"), (gmod, "gather")], name="exchange_last")
    gsmall = _sum8(small_all, name="sum_small")
    loss = gsmall[n_small + CONV_WIDTH, 0]
    g_w_ada, g_b_ada = _ada_bwd(c_all, gmod_all.reshape(N_DEV * B, N_MOD * D), n_col=n_col, name="ada_bwd")
    g_conv_w = lax.dynamic_slice(gsmall[n_small:n_small + CONV_WIDTH], (0, me * (CC // N_DEV)),
                                 (CONV_WIDTH, CC // N_DEV))

    def col_update(name, w, m, v):
        outs = _sum8_adamw(parts[name], w[0].T, m[0].T, v[0].T, name="adamw_" + name)
        return tuple(o.T for o in outs)

    def row_update(name, w, m, v):
        return tuple(_sum8_adamw(parts[name], w[0], m[0], v[0], name="adamw_" + name))

    upd = {
        "ffn1_w_gate": col_update("ffn1_w_gate", ffn1_w_gate, m_ffn1_w_gate, v_ffn1_w_gate),
        "ffn1_w_up": col_update("ffn1_w_up", ffn1_w_up, m_ffn1_w_up, v_ffn1_w_up),
        "ffn1_w_down": row_update("ffn1_w_down", ffn1_w_down, m_ffn1_w_down, v_ffn1_w_down),
        "w_in": col_update("w_in", w_in, m_w_in, v_w_in),
        "w_attn_o": row_update("w_attn_o", w_attn_o, m_w_attn_o, v_w_attn_o),
        "w_conv_o": row_update("w_conv_o", w_conv_o, m_w_conv_o, v_w_conv_o),
        "w_out": row_update("w_out", w_out, m_w_out, v_w_out),
        "ffn2_w_gate": col_update("ffn2_w_gate", ffn2_w_gate, m_ffn2_w_gate, v_ffn2_w_gate),
        "ffn2_w_up": col_update("ffn2_w_up", ffn2_w_up, m_ffn2_w_up, v_ffn2_w_up),
        "ffn2_w_down": row_update("ffn2_w_down", ffn2_w_down, m_ffn2_w_down, v_ffn2_w_down),
        "w_ada": (g_w_ada,) + tuple(_adamw(g_w_ada, w_ada[0], m_w_ada[0], v_w_ada[0], name="adamw_w_ada")),
        "conv_w_dw": (g_conv_w,) + tuple(_adamw(g_conv_w, conv_w_dw[0], m_conv_w_dw[0], v_conv_w_dw[0],
                                                name="adamw_conv_w_dw")),
    }
    for k in upd:
        upd[k] = tuple(t[None] for t in upd[k])

    def pad_sinks(t):
        return jnp.pad(t, ((0, 0), (0, D - N_Q_HEADS)))

    def pack(f1, mix, f2, fin, cb, lg, lb, sinks, bada):
        return jnp.concatenate([f1, mix, f2, fin[None], cb, lg, lb, pad_sinks(sinks), bada.reshape(N_MOD, D)], axis=0)

    w_s = pack(norm_ffn1_g, norm_mix_g, norm_ffn2_g, final_norm_g, conv_b_dw, conv_ln_g, conv_ln_b, attn_sinks, b_ada)
    m_s = pack(m_norm_ffn1_g, m_norm_mix_g, m_norm_ffn2_g, m_final_norm_g, m_conv_b_dw, m_conv_ln_g, m_conv_ln_b,
               m_attn_sinks, m_b_ada)
    v_s = pack(v_norm_ffn1_g, v_norm_mix_g, v_norm_ffn2_g, v_final_norm_g, v_conv_b_dw, v_conv_ln_g, v_conv_ln_b,
               v_attn_sinks, v_b_ada)
    g_s = jnp.concatenate([gsmall[:n_small], g_b_ada.reshape(N_MOD, D)], axis=0)
    small_out = (g_s,) + tuple(_adamw(g_s, w_s, m_s, v_s, name="adamw_vectors"))

    def unpack(t):
        return {
            "norm_ffn1_g": t[0:1], "norm_mix_g": t[1:2], "norm_ffn2_g": t[2:3], "final_norm_g": t[3],
            "conv_b_dw": t[4:5], "conv_ln_g": t[5:6], "conv_ln_b": t[6:7], "attn_sinks": t[7:8, :N_Q_HEADS],
            "b_ada": t[n_small:n_small + N_MOD].reshape(1, N_MOD * D),
        }

    small_un = [unpack(t) for t in small_out]
    for k in small_un[0]:
        upd[k] = tuple(s[k] for s in small_un)

    order = ["w_ada", "b_ada", "norm_ffn1_g", "ffn1_w_gate", "ffn1_w_up", "ffn1_w_down", "norm_mix_g", "w_in",
             "attn_sinks", "w_attn_o", "conv_w_dw", "conv_b_dw", "conv_ln_g", "conv_ln_b", "w_conv_o", "w_out",
             "norm_ffn2_g", "ffn2_w_gate", "ffn2_w_up", "ffn2_w_down", "final_norm_g"]
    grad_x = dx0.reshape(B, S, D)
    return (loss, grad_x, *[upd[k][0] for k in order], *[upd[k][1] for k in order],
            *[upd[k][2] for k in order], *[upd[k][3] for k in order])
```

```python
import jax
import jax.numpy as jnp
from jax import lax
from jax.experimental import pallas as pl
from jax.experimental.pallas import tpu as pltpu

F32 = jnp.float32
BF16 = jnp.bfloat16
SDS = jax.ShapeDtypeStruct
MESH = pl.DeviceIdType.MESH

N_DEV = 8
EPS = 1e-6
HEAD_DIM = 64
N_Q_HEADS = 16
N_KV_HEADS = 2
GQA_GROUP = N_Q_HEADS // N_KV_HEADS
KV_WIDTH = N_KV_HEADS * HEAD_DIM
ATT_BLOCK = 128
CONV_WIDTH = 31
CONV_HALO = 32
CONV_ROWS = 64
N_MOD = 9
FFN_RESIDUAL = 0.5
ADAM_LR = 0.001
ADAM_B1 = 0.9
ADAM_B2 = 0.999
ADAM_EPS = 1e-08
ADAM_WD = 0.01
ADAM_STEP = 10
NEG_BIG = -1e30

V7X_VMEM_BYTES = 64 * 2**20
VMEM_CAP = V7X_VMEM_BYTES - 8 * 2**20


def _nbytes(shape, dtype):
    n = 1
    for s in shape:
        n *= s
    return n * jnp.dtype(dtype).itemsize


def _params(n_axes, blocks, temp_bytes=0):
    need = 2 * sum(_nbytes(s, d) for s, d in blocks) + temp_bytes + 4 * 2**20
    return pltpu.CompilerParams(dimension_semantics=("arbitrary",) * n_axes,
                                vmem_limit_bytes=int(min(max(need, 16 * 2**20), VMEM_CAP)))


def _dot_nt(a, b):
    return lax.dot_general(a, b, (((1,), (1,)), ((), ())), preferred_element_type=F32)


def _dot_tn(a, b):
    return lax.dot_general(a, b, (((0,), (0,)), ((), ())), preferred_element_type=F32)


def _dot(a, b):
    return jnp.dot(a, b, preferred_element_type=F32)


def _sigmoid(x):
    return jax.nn.sigmoid(x)


def _rowsum(v):
    return jnp.sum(v, axis=0, keepdims=True)


def _acc(ref, val, first):
    @pl.when(first)
    def _():
        ref[...] = val

    @pl.when(jnp.logical_not(first))
    def _():
        ref[...] = ref[...] + val


def _norm_mod(xf, gn, sh, sc):
    rstd = lax.rsqrt(jnp.mean(xf * xf, axis=-1, keepdims=True) + EPS)
    xhat = xf * rstd
    yn = xhat * gn
    return yn * (1.0 + sc) + sh, xhat, rstd, yn


def _pick(n, cands):
    for c in cands:
        if n % c == 0:
            return c
    return n


def _my_pos():
    return lax.axis_index("x"), lax.axis_index("y"), lax.axis_index("c")


def _peer(pos, k):
    x, y, c = pos
    return ((1 - x) if k & 4 else x, (1 - y) if k & 2 else y, (1 - c) if k & 1 else c)


def _lin(pos):
    return 4 * pos[0] + 2 * pos[1] + pos[2]


class _Comm:
    N_COPY = N_DEV - 1
    N_CHIP = N_DEV // 2

    def __init__(self, items):
        self.arrs = [a for a, _ in items]
        self.modes = [m for _, m in items]
        self.n = len(items)
        self.out = None

    def out_shape(self):
        def shape(a, m):
            return {"gather": (N_DEV,) + a.shape, "scatter": a.shape, "pair": (self.N_CHIP,) + a.shape[1:],
                    "cross": a.shape}[m]
        return [pltpu.HBM(shape(a, m), a.dtype) for a, m in zip(self.arrs, self.modes)]

    def operands(self):
        return [pltpu.with_memory_space_constraint(a, pltpu.HBM) for a in self.arrs]

    def scratch(self):
        return [pltpu.SemaphoreType.DMA((self.n * self.N_COPY,)), pltpu.SemaphoreType.DMA((self.n * self.N_COPY,)),
                pltpu.SemaphoreType.DMA((self.n,))]

    def _plan(self, mode, me):
        x, y, c = me
        sib = (x, y, 1 - c)
        chips = [(1 - x, y), (x, 1 - y), (1 - x, 1 - y)]

        def chip_lin(ch):
            return 2 * ch[0] + ch[1]

        if mode == "scatter":
            peers = [_peer(me, k + 1) for k in range(self.N_COPY)]
            return [(p, ("in", _lin(p)), _lin(me), _lin(p), None) for p in peers], (_lin(me), _lin(me))
        if mode == "gather":
            same = [(*ch, c) for ch in chips]
            other = [(*ch, 1 - c) for ch in chips]
            copies = [(sib, ("in", None), _lin(me), _lin(sib), None)]
            copies += [(p, ("in", None), _lin(me), _lin(p), None) for p in same]
            copies += [(sib, ("out", _lin(p)), _lin(p), _lin(o), 1 + j) for j, (p, o) in enumerate(zip(same, other))]
            return copies, (None, _lin(me))
        if mode == "pair":
            return [(sib, ("in", 2 * q + 1 - c), q, q, None) for q in range(self.N_CHIP)], None
        if mode == "cross":
            mine = chip_lin((x, y))
            return ([((*ch, c), ("in", chip_lin(ch)), mine, chip_lin(ch), None) for ch in chips], (mine, mine))
        raise ValueError(mode)

    def _copy(self, refs, me, i, k, recv):
        srcs, outs, (send_sems, recv_sems, _) = refs
        peer, (where, slot), send_slot, recv_slot, _ = self._plan(self.modes[i], me)[0][k]
        src = srcs[i] if where == "in" else outs[i]
        src = src if slot is None else src.at[slot]
        sem = i * self.N_COPY + k
        return pltpu.make_async_remote_copy(
            src_ref=src, dst_ref=outs[i].at[recv_slot if recv else send_slot], send_sem=send_sems.at[sem],
            recv_sem=recv_sems.at[sem], device_id=peer, device_id_type=MESH)

    def _local(self, refs, me, i):
        srcs, outs, (_, _, loc_sems) = refs
        local = self._plan(self.modes[i], me)[1]
        if local is None:
            return None
        own = srcs[i] if local[0] is None else srcs[i].at[local[0]]
        return pltpu.make_async_copy(own, outs[i].at[local[1]], loc_sems.at[i])

    def start(self, refs):
        me = _my_pos()
        for i in range(self.n):
            local = self._local(refs, me, i)
            if local is not None:
                local.start()
            for k, cp in enumerate(self._plan(self.modes[i], me)[0]):
                if cp[4] is None:
                    self._copy(refs, me, i, k, False).start()

    def forward(self, refs):
        me = _my_pos()
        for i in range(self.n):
            for k, cp in enumerate(self._plan(self.modes[i], me)[0]):
                if cp[4] is not None:
                    self._copy(refs, me, i, cp[4], True).wait_recv()
                    self._copy(refs, me, i, k, False).start()

    def finish(self, refs):
        me = _my_pos()
        plans = [self._plan(m, me)[0] for m in self.modes]
        for i in range(self.n):
            passed_on = [cp[4] for cp in plans[i] if cp[4] is not None]
            for k in range(len(plans[i])):
                if k not in passed_on:
                    self._copy(refs, me, i, k, True).wait_recv()
                self._copy(refs, me, i, k, False).wait_send()
            local = self._local(refs, me, i)
            if local is not None:
                local.wait()


_ANY = pl.BlockSpec(memory_space=pl.ANY)


def _call(body, args, *, name, grid, in_specs, out_specs, out_shape, params, scratch_shapes=(), comm=None):
    in_specs, out_specs, out_shape = list(in_specs), list(out_specs), list(out_shape)
    scratch_shapes = list(scratch_shapes)
    args = [a if s.memory_space == pltpu.SMEM else pltpu.with_memory_space_constraint(a, pltpu.HBM)
            for a, s in zip(args, in_specs)]
    out_shape = [pltpu.HBM(o.shape, o.dtype) for o in out_shape]
    if comm is None:
        return list(pl.pallas_call(body, name=name, grid=grid, in_specs=in_specs, out_specs=out_specs,
                                   out_shape=out_shape, scratch_shapes=scratch_shapes, compiler_params=params)(*args))
    n_in, n_out, n_scr, nc = len(in_specs), len(out_specs), len(scratch_shapes), comm.n
    n_steps = 1
    for g in grid:
        n_steps *= g

    def hosted(*refs):
        ins, c_in = refs[:n_in], refs[n_in:n_in + nc]
        outs = refs[n_in + nc:n_in + nc + n_out]
        c_out = refs[n_in + nc + n_out:n_in + 2 * nc + n_out]
        scr = refs[n_in + 2 * nc + n_out:n_in + 2 * nc + n_out + n_scr]
        sems = refs[n_in + 2 * nc + n_out + n_scr:]
        step = pl.program_id(0)
        for d in range(1, len(grid)):
            step = step * grid[d] + pl.program_id(d)
        c_refs = (c_in, c_out, sems)

        @pl.when(step == 0)
        def _():
            comm.start(c_refs)

        if n_steps >= 3:
            @pl.when(step == n_steps - 2)
            def _():
                comm.forward(c_refs)

        body(*ins, *outs, *scr)

        @pl.when(step == n_steps - 1)
        def _():
            if n_steps < 3:
                comm.forward(c_refs)
            comm.finish(c_refs)

    res = pl.pallas_call(
        hosted, name=name, grid=grid, in_specs=in_specs + [_ANY] * nc, out_specs=out_specs + [_ANY] * nc,
        out_shape=out_shape + comm.out_shape(), scratch_shapes=scratch_shapes + comm.scratch(),
        compiler_params=params)(*args, *comm.operands())
    comm.out = list(res[n_out:])
    return list(res[:n_out])


def _exchange(items, *, name):
    comm = _Comm(items)

    def body(*refs):
        r = (refs[:comm.n], refs[comm.n:2 * comm.n], refs[2 * comm.n:])
        comm.start(r)
        comm.forward(r)
        comm.finish(r)

    return list(pl.pallas_call(body, name=name, out_shape=comm.out_shape(), in_specs=[_ANY] * comm.n,
                               out_specs=[_ANY] * comm.n, scratch_shapes=comm.scratch())(*comm.operands()))


class _ModVec:
    def __init__(self, arr, idx):
        self.arr, self.idx = arr, idx

    def spec(self, tps, n_axes):
        idx, blk = self.idx, (1, 1, self.arr.shape[2])
        if n_axes == 1:
            return pl.BlockSpec(blk, lambda i: (i // tps * N_MOD + idx, 0, 0))
        return pl.BlockSpec(blk, lambda i, j: (i // tps * N_MOD + idx, 0, 0))


def _norm_mod_matmul(x, gn, sh, sc, wts, *, seq, tm, tn, name, comm=None):
    T, D = x.shape
    N = wts[0].shape[0]
    nw = len(wts)
    tps = seq // tm

    def body(x_ref, gn_ref, sh_ref, sc_ref, *rest):
        w_refs, h_ref, o_refs = rest[:nw], rest[nw], rest[nw + 1:]

        @pl.when(pl.program_id(1) == 0)
        def _():
            h_ref[...] = _norm_mod(x_ref[...], gn_ref[...], sh_ref[0], sc_ref[0])[0].astype(BF16)

        h = h_ref[...]
        for w_ref, o_ref in zip(w_refs, o_refs):
            o_ref[...] = _dot_nt(h, w_ref[...]).astype(o_ref.dtype)

    row = pl.BlockSpec((tm, D), lambda i, j: (i, 0))
    vec = pl.BlockSpec((1, D), lambda i, j: (0, 0))
    per_b = pl.BlockSpec((1, 1, D), lambda i, j: (i // tps, 0, 0))
    wspec = pl.BlockSpec((tn, D), lambda i, j: (j, 0))
    ospec = pl.BlockSpec((tm, tn), lambda i, j: (i, j))
    blocks = [((tm, D), F32), ((tm, D), BF16)] + [((tn, D), BF16), ((tm, tn), BF16)] * nw
    outs = _call(
        body, (x, gn, sh.arr, sc.arr, *wts), name=name, grid=(T // tm, N // tn),
        in_specs=[row, vec, sh.spec(tps, 2), sc.spec(tps, 2)] + [wspec] * nw,
        out_specs=[row] + [ospec] * nw,
        out_shape=[SDS((T, D), BF16)] + [SDS((T, N), BF16)] * nw,
        params=_params(2, blocks, temp_bytes=2 * _nbytes((tm, tn), F32) + 3 * _nbytes((tm, D), F32)), comm=comm)
    return outs[0], outs[1:]


def _matmul_nt(h, w, *, tm, tn, name, comm=None):
    T, D = h.shape
    N = w.shape[0]

    def body(h_ref, w_ref, o_ref):
        o_ref[...] = _dot_nt(h_ref[...], w_ref[...]).astype(o_ref.dtype)

    blocks = [((tm, D), BF16), ((tn, D), BF16), ((tm, tn), BF16)]
    return _call(
        body, (h, w), name=name, grid=(T // tm, N // tn),
        in_specs=[pl.BlockSpec((tm, D), lambda i, j: (i, 0)), pl.BlockSpec((tn, D), lambda i, j: (j, 0))],
        out_specs=[pl.BlockSpec((tm, tn), lambda i, j: (i, j))],
        out_shape=[SDS((T, N), BF16)],
        params=_params(2, blocks, temp_bytes=2 * _nbytes((tm, tn), F32)), comm=comm)[0]


def _ffn_down(a, b, wd, x, g, *, seq, tm, name, comm=None):
    T, F = a.shape
    D = wd.shape[1]
    tps = seq // tm

    def body(a_ref, b_ref, wd_ref, x_ref, g_ref, xo_ref, y_ref):
        af = a_ref[...].astype(F32)
        act = (af * _sigmoid(af) * b_ref[...].astype(F32)).astype(BF16)
        y = _dot(act, wd_ref[...])
        xo_ref[...] = x_ref[...] + (FFN_RESIDUAL * g_ref[0]) * y
        y_ref[...] = y.astype(BF16)

    wide = pl.BlockSpec((tm, F), lambda i: (i, 0))
    row = pl.BlockSpec((tm, D), lambda i: (i, 0))
    per_b = pl.BlockSpec((1, 1, D), lambda i: (i // tps, 0, 0))
    wspec = pl.BlockSpec((F, D), lambda i: (0, 0))
    blocks = [((tm, F), BF16)] * 2 + [((F, D), BF16), ((tm, D), F32), ((tm, D), F32), ((tm, D), BF16)]
    return _call(
        body, (a, b, wd, x, g.arr), name=name, grid=(T // tm,),
        in_specs=[wide, wide, wspec, row, g.spec(tps, 1)], out_specs=[row, row],
        out_shape=[SDS((T, D), F32), SDS((T, D), BF16)],
        params=_params(1, blocks, temp_bytes=3 * _nbytes((tm, F), F32)), comm=comm)


def _final_loss(x, gf, tgt, *, tm, name):
    T, D = x.shape
    nt = T // tm

    def body(x_ref, gf_ref, t_ref, dx_ref, loss_ref, dgf_ref, lacc):
        i = pl.program_id(0)
        xf = x_ref[...]
        gfv = gf_ref[...]
        rstd = lax.rsqrt(jnp.mean(xf * xf, axis=-1, keepdims=True) + EPS)
        xhat = xf * rstd
        err = xhat * gfv - t_ref[...]
        dy = err * (1.0 / D)
        dxhat = dy * gfv
        dx_ref[...] = rstd * (dxhat - xhat * jnp.mean(dxhat * xhat, axis=-1, keepdims=True))
        _acc(dgf_ref, _rowsum(dy * xhat), i == 0)
        _acc(lacc, _rowsum(err * err), i == 0)

        @pl.when(i == nt - 1)
        def _():
            loss_ref[...] = jnp.broadcast_to((0.5 / D) * jnp.sum(lacc[...]), loss_ref.shape)

    row = pl.BlockSpec((tm, D), lambda i: (i, 0))
    vec = pl.BlockSpec((1, D), lambda i: (0, 0))
    lspec = pl.BlockSpec((1, 128), lambda i: (0, 0))
    blocks = [((tm, D), F32)] * 3
    return _call(
        body, (x, gf, tgt), name=name, grid=(nt,),
        in_specs=[row, vec, row], out_specs=[row, lspec, vec],
        out_shape=[SDS((T, D), F32), SDS((1, 128), F32), SDS((1, D), F32)],
        scratch_shapes=[pltpu.VMEM((1, D), F32)],
        params=_params(1, blocks, temp_bytes=4 * _nbytes((tm, D), F32)))


def _ffn_bwd_down(dxo, g, y, wd, a, b, *, seq, tm, tn, name, comm=None):
    T, F = a.shape
    D = wd.shape[1]
    tps = seq // tm
    nb = T // seq

    def body(dxo_ref, g_ref, y_ref, wd_ref, a_ref, b_ref, dyb_ref, da_ref, db_ref, act_ref, dg_ref):
        i = pl.program_id(0)

        @pl.when(pl.program_id(1) == 0)
        def _():
            dx = dxo_ref[...]
            dyb_ref[...] = ((FFN_RESIDUAL * g_ref[0]) * dx).astype(BF16)
            part = _rowsum(FFN_RESIDUAL * dx * y_ref[...].astype(F32))
            _acc(dg_ref, part[None], i % tps == 0)

        dact = _dot_nt(dyb_ref[...], wd_ref[...])
        af = a_ref[...].astype(F32)
        bf = b_ref[...].astype(F32)
        sg = _sigmoid(af)
        silu = af * sg
        act_ref[...] = (silu * bf).astype(BF16)
        da_ref[...] = (dact * bf * (sg * (1.0 + af * (1.0 - sg)))).astype(BF16)
        db_ref[...] = (dact * silu).astype(BF16)

    row = pl.BlockSpec((tm, D), lambda i, j: (i, 0))
    per_b = pl.BlockSpec((1, 1, D), lambda i, j: (i // tps, 0, 0))
    wspec = pl.BlockSpec((tn, D), lambda i, j: (j, 0))
    chunk = pl.BlockSpec((tm, tn), lambda i, j: (i, j))
    blocks = [((tm, D), F32), ((tm, D), BF16), ((tn, D), BF16), ((tm, D), BF16)] + [((tm, tn), BF16)] * 5
    return _call(
        body, (dxo, g.arr, y, wd, a, b), name=name, grid=(T // tm, F // tn),
        in_specs=[row, g.spec(tps, 2), row, wspec, chunk, chunk],
        out_specs=[row, chunk, chunk, chunk, per_b],
        out_shape=[SDS((T, D), BF16)] + [SDS((T, F), BF16)] * 3 + [SDS((nb, 1, D), F32)],
        params=_params(2, blocks, temp_bytes=6 * _nbytes((tm, tn), F32)), comm=comm)


def _matmul_norm_mod_bwd(ds, ws, x, gn, sc, dxo, *, seq, tm, name, comm=None):
    T, D = x.shape
    nk = len(ds)
    tps = seq // tm
    nb = T // seq

    def body(*refs):
        d_refs, w_refs = refs[:nk], refs[nk:2 * nk]
        x_ref, gn_ref, sc_ref, dxo_ref, dxi_ref, dsh_ref, dsc_ref, dgn_ref = refs[2 * nk:]
        i = pl.program_id(0)
        dh = _dot(d_refs[0][...], w_refs[0][...])
        for d_ref, w_ref in zip(d_refs[1:], w_refs[1:]):
            dh = dh + _dot(d_ref[...], w_ref[...])
        gnv = gn_ref[...]
        scv = sc_ref[0]
        _, xhat, rstd, yn = _norm_mod(x_ref[...], gnv, 0.0, scv)
        dyn = dh * (1.0 + scv)
        dxhat = dyn * gnv
        dxi_ref[...] = dxo_ref[...] + rstd * (dxhat - xhat * jnp.mean(dxhat * xhat, axis=-1, keepdims=True))
        first_of_seq = i % tps == 0
        _acc(dsh_ref, _rowsum(dh)[None], first_of_seq)
        _acc(dsc_ref, _rowsum(dh * yn)[None], first_of_seq)
        _acc(dgn_ref, _rowsum(dyn * xhat), i == 0)

    row = pl.BlockSpec((tm, D), lambda i: (i, 0))
    vec = pl.BlockSpec((1, D), lambda i: (0, 0))
    per_b = pl.BlockSpec((1, 1, D), lambda i: (i // tps, 0, 0))
    d_specs = [pl.BlockSpec((tm, d.shape[1]), lambda i: (i, 0)) for d in ds]
    w_specs = [pl.BlockSpec(w.shape, lambda i: (0, 0)) for w in ws]
    blocks = ([((tm, d.shape[1]), BF16) for d in ds] + [(w.shape, BF16) for w in ws] + [((tm, D), F32)] * 3)
    return _call(
        body, (*ds, *ws, x, gn, sc.arr, dxo), name=name, grid=(T // tm,),
        in_specs=d_specs + w_specs + [row, vec, sc.spec(tps, 1), row],
        out_specs=[row, per_b, per_b, vec],
        out_shape=[SDS((T, D), F32), SDS((nb, 1, D), F32), SDS((nb, 1, D), F32), SDS((1, D), F32)],
        params=_params(1, blocks, temp_bytes=6 * _nbytes((tm, D), F32)), comm=comm)


def _layernorm_silu(yc, lg, lb):
    mu = jnp.mean(yc, axis=-1, keepdims=True)
    cen = yc - mu
    rstd = lax.rsqrt(jnp.mean(cen * cen, axis=-1, keepdims=True) + EPS)
    xh = cen * rstd
    l = xh * lg + lb
    s = _sigmoid(l)
    return l * s, xh, rstd, l, s


GATE_W = 256


def _gate_specs(tm, D, col):
    return [pl.BlockSpec((tm, GATE_W), lambda i, blk=col // GATE_W + t: (i, blk)) for t in range(D // GATE_W)]


def _gate(refs):
    return jnp.concatenate([r[...] for r in refs], axis=1).astype(F32)


def _mix_out(ao, yc, proj, wao, wco, wout, x1, g2, lg, lb, *, seq, tm, ga_col, gc_col, name, comm=None):
    T, D = x1.shape
    tps = seq // tm
    ng = D // GATE_W

    def body(ao_ref, yc_ref, *rest):
        ga_refs, gc_refs = rest[:ng], rest[ng:2 * ng]
        (wao_ref, wco_ref, wout_ref, x1_ref, g2_ref, lg_ref, lb_ref,
         x2_ref, z_ref, ya_ref, ycv_ref, cact_ref, mrg_ref) = rest[2 * ng:]
        ya = _dot(ao_ref[...], wao_ref[...])
        cact = _layernorm_silu(yc_ref[...], lg_ref[...], lb_ref[...])[0].astype(BF16)
        ycv = _dot(cact, wco_ref[...])
        merged = (_sigmoid(_gate(ga_refs)) * ya + _sigmoid(_gate(gc_refs)) * ycv).astype(BF16)
        z = _dot(merged, wout_ref[...])
        x2_ref[...] = x1_ref[...] + g2_ref[0] * z
        z_ref[...] = z.astype(BF16)
        ya_ref[...] = ya.astype(BF16)
        ycv_ref[...] = ycv.astype(BF16)
        cact_ref[...] = cact
        mrg_ref[...] = merged

    row = pl.BlockSpec((tm, D), lambda i: (i, 0))
    vec = pl.BlockSpec((1, D), lambda i: (0, 0))
    per_b = pl.BlockSpec((1, 1, D), lambda i: (i // tps, 0, 0))
    wspec = pl.BlockSpec((D, D), lambda i: (0, 0))
    gates = _gate_specs(tm, D, ga_col) + _gate_specs(tm, D, gc_col)
    blocks = ([((tm, D), BF16), ((tm, D), F32), ((tm, D), BF16), ((tm, D), BF16)] + [((D, D), BF16)] * 3
              + [((tm, D), F32)] * 2 + [((tm, D), BF16)] * 5)
    return _call(
        body, (ao, yc, *[proj] * (2 * ng), wao, wco, wout, x1, g2.arr, lg, lb), name=name, grid=(T // tm,),
        in_specs=[row, row, *gates, wspec, wspec, wspec, row, g2.spec(tps, 1), vec, vec],
        out_specs=[row] * 6,
        out_shape=[SDS((T, D), F32)] + [SDS((T, D), BF16)] * 5,
        params=_params(1, blocks, temp_bytes=8 * _nbytes((tm, D), F32)), comm=comm)


def _mix_out_bwd(dx2, g2, z, wout, proj, ya, ycv, wao, wco, yc, lg, lb, *, seq, tm, ga_col, gc_col, name,
                 comm=None):
    T, D = dx2.shape
    tps = seq // tm
    nb = T // seq
    ng = D // GATE_W

    def body(dx2_ref, g2_ref, z_ref, wout_ref, *rest):
        ga_refs, gc_refs = rest[:ng], rest[ng:2 * ng]
        (ya_ref, ycv_ref, wao_ref, wco_ref, yc_ref, lg_ref, lb_ref, dz_ref, dya_ref, dycv_ref, dga_ref, dgc_ref,
         dao_ref, dyc_ref, dg2_ref, dlg_ref, dlb_ref) = rest[2 * ng:]
        i = pl.program_id(0)
        dx = dx2_ref[...]
        _acc(dg2_ref, _rowsum(dx * z_ref[...].astype(F32))[None], i % tps == 0)
        dzb = (g2_ref[0] * dx).astype(BF16)
        dz_ref[...] = dzb
        dmerged = _dot_nt(dzb, wout_ref[...])
        sa = _sigmoid(_gate(ga_refs))
        sc_ = _sigmoid(_gate(gc_refs))
        dya = (dmerged * sa).astype(BF16)
        dycv = (dmerged * sc_).astype(BF16)
        dya_ref[...] = dya
        dycv_ref[...] = dycv
        dga_ref[...] = (dmerged * ya_ref[...].astype(F32) * (sa * (1.0 - sa))).astype(BF16)
        dgc_ref[...] = (dmerged * ycv_ref[...].astype(F32) * (sc_ * (1.0 - sc_))).astype(BF16)
        dao_ref[...] = _dot_nt(dya, wao_ref[...]).astype(BF16)
        dcact = _dot_nt(dycv, wco_ref[...])
        lgv = lg_ref[...]
        _, xh, rstd, l, s = _layernorm_silu(yc_ref[...], lgv, lb_ref[...])
        dl = dcact * (s * (1.0 + l * (1.0 - s)))
        _acc(dlb_ref, _rowsum(dl), i == 0)
        _acc(dlg_ref, _rowsum(dl * xh), i == 0)
        dxh = dl * lgv
        dyc_ref[...] = rstd * (dxh - jnp.mean(dxh, axis=-1, keepdims=True)
                               - xh * jnp.mean(dxh * xh, axis=-1, keepdims=True))

    row = pl.BlockSpec((tm, D), lambda i: (i, 0))
    vec = pl.BlockSpec((1, D), lambda i: (0, 0))
    per_b = pl.BlockSpec((1, 1, D), lambda i: (i // tps, 0, 0))
    wspec = pl.BlockSpec((D, D), lambda i: (0, 0))
    gates = _gate_specs(tm, D, ga_col) + _gate_specs(tm, D, gc_col)
    blocks = ([((tm, D), F32)] * 3 + [((tm, D), BF16)] * 11 + [((D, D), BF16)] * 3)
    return _call(
        body, (dx2, g2.arr, z, wout, *[proj] * (2 * ng), ya, ycv, wao, wco, yc, lg, lb), name=name,
        grid=(T // tm,),
        in_specs=[row, g2.spec(tps, 1), row, wspec, *gates, row, row, wspec, wspec, row, vec, vec],
        out_specs=[row] * 7 + [per_b, vec, vec],
        out_shape=[SDS((T, D), BF16)] * 6 + [SDS((T, D), F32), SDS((nb, 1, D), F32), SDS((1, D), F32),
                                             SDS((1, D), F32)],
        params=_params(1, blocks, temp_bytes=10 * _nbytes((tm, D), F32)), comm=comm)


GROUP_ROWS = GQA_GROUP * ATT_BLOCK
PAIR_W = 2 * HEAD_DIM
GROUP_W = GQA_GROUP * HEAD_DIM


def _lane_lo():
    return lax.broadcasted_iota(jnp.int32, (1, PAIR_W), 1) < HEAD_DIM


def _band_bias():
    sj = lax.broadcasted_iota(jnp.int32, (2 * ATT_BLOCK, GROUP_ROWS), 0)
    qi = lax.broadcasted_iota(jnp.int32, (2 * ATT_BLOCK, GROUP_ROWS), 1) & (ATT_BLOCK - 1)
    rel = qi + ATT_BLOCK - sj
    bias = jnp.where(jnp.logical_and(rel >= 0, rel < ATT_BLOCK), 0.0, NEG_BIG)
    sj1 = lax.broadcasted_iota(jnp.int32, (2 * ATT_BLOCK, 1), 0)
    return bias, jnp.where(sj1 < ATT_BLOCK, NEG_BIG, 0.0)


def _dup_heads(src_ref, dst, seq):
    x = src_ref[...]
    i = lax.broadcasted_iota(jnp.int32, (KV_WIDTH, PAIR_W), 0)
    j = lax.broadcasted_iota(jnp.int32, (KV_WIDTH, PAIR_W), 1) & (HEAD_DIM - 1)
    for g in range(N_KV_HEADS):
        sel = jnp.where(i == j + g * HEAD_DIM, 1.0, 0.0).astype(BF16)
        dst[g, pl.ds(0, ATT_BLOCK), :] = jnp.zeros((ATT_BLOCK, PAIR_W), BF16)
        dst[g, pl.ds(ATT_BLOCK, seq), :] = _dot(x, sel).astype(BF16)


def _stack_heads(blk, g, lo):
    parts = []
    for p in range(GQA_GROUP // 2):
        pair = blk[:, g * GROUP_W + p * PAIR_W:g * GROUP_W + (p + 1) * PAIR_W]
        parts += [jnp.where(lo, pair, jnp.zeros_like(pair)), jnp.where(lo, jnp.zeros_like(pair), pair)]
    return jnp.concatenate(parts, axis=0)


def _unstack_heads(full, ref, r0, g, lo):
    for p in range(GQA_GROUP // 2):
        even = full[(2 * p) * ATT_BLOCK:(2 * p + 1) * ATT_BLOCK, :]
        odd = full[(2 * p + 1) * ATT_BLOCK:(2 * p + 2) * ATT_BLOCK, :]
        ref[pl.ds(r0, ATT_BLOCK), g * GROUP_W + p * PAIR_W:g * GROUP_W + (p + 1) * PAIR_W] = (
            jnp.where(lo, even, odd).astype(ref.dtype))


def _sink_row(sink_ref, g):
    return jnp.concatenate([jnp.full((1, ATT_BLOCK), sink_ref[0, g * GQA_GROUP + h], F32)
                            for h in range(GQA_GROUP)], axis=1)


def _group_probs(qs, k2, bias, sink):
    s = _dot_nt(k2, qs) * (HEAD_DIM ** -0.5) + bias
    m = jnp.maximum(jnp.max(s, axis=0, keepdims=True), sink)
    p = jnp.exp(s - m)
    psink = jnp.exp(sink - m)
    inv = 1.0 / (jnp.sum(p, axis=0, keepdims=True) + psink)
    return p * inv, psink * inv


def _attn_fwd(projp, sinks, *, seq, q_blk, k_blk, v_blk, name, comm=None):
    T = projp.shape[0]
    QW = N_Q_HEADS * HEAD_DIM
    nblk = seq // ATT_BLOCK

    def body(q_ref, k_ref, v_ref, sink_ref, o_ref, k2s, v2s):
        _dup_heads(k_ref, k2s, seq)
        _dup_heads(v_ref, v2s, seq)
        lo = _lane_lo()
        bias0, first_pen = _band_bias()
        sink_rows = [_sink_row(sink_ref, g) for g in range(N_KV_HEADS)]

        def blk(n, carry):
            r0 = pl.multiple_of(n * ATT_BLOCK, ATT_BLOCK)
            qb = q_ref[pl.ds(r0, ATT_BLOCK), :]
            bias = bias0 + jnp.where(n == 0, 1.0, 0.0) * first_pen
            for g in range(N_KV_HEADS):
                probs_t, _ = _group_probs(_stack_heads(qb, g, lo), k2s[g, pl.ds(r0, 2 * ATT_BLOCK), :], bias,
                                          sink_rows[g])
                _unstack_heads(_dot_tn(probs_t.astype(BF16), v2s[g, pl.ds(r0, 2 * ATT_BLOCK), :]), o_ref, r0, g, lo)
            return carry

        lax.fori_loop(0, nblk, blk, 0)

    blocks = [((seq, QW), BF16)] * 2 + [((seq, KV_WIDTH), BF16)] * 2
    return _call(
        body, (projp, projp, projp, sinks), name=name, grid=(T // seq,),
        in_specs=[pl.BlockSpec((seq, QW), lambda b: (b, q_blk)),
                  pl.BlockSpec((seq, KV_WIDTH), lambda b: (b, k_blk)),
                  pl.BlockSpec((seq, KV_WIDTH), lambda b: (b, v_blk)),
                  pl.BlockSpec(memory_space=pltpu.SMEM)],
        out_specs=[pl.BlockSpec((seq, QW), lambda b: (b, 0))],
        out_shape=[SDS((T, QW), BF16)],
        scratch_shapes=[pltpu.VMEM((N_KV_HEADS, seq + ATT_BLOCK, PAIR_W), BF16)] * 2,
        params=_params(1, blocks, temp_bytes=16 * 2**20), comm=comm)[0]


def _attn_bwd(projp, dao, sinks, *, seq, q_blk, k_blk, v_blk, name, comm=None):
    T = projp.shape[0]
    QW = N_Q_HEADS * HEAD_DIM
    nblk = seq // ATT_BLOCK

    def body(q_ref, k_ref, v_ref, do_ref, sink_ref, dq_ref, dk_ref, dv_ref, dsink_ref, k2s, v2s, dkacc, dvacc):
        _dup_heads(k_ref, k2s, seq)
        _dup_heads(v_ref, v2s, seq)
        dkacc[...] = jnp.zeros(dkacc.shape, F32)
        dvacc[...] = jnp.zeros(dvacc.shape, F32)
        lane = lax.broadcasted_iota(jnp.int32, (1, PAIR_W), 1)
        lo = lane < HEAD_DIM
        bias0, first_pen = _band_bias()
        sink_rows = [_sink_row(sink_ref, g) for g in range(N_KV_HEADS)]

        def blk(n, dsink):
            r0 = pl.multiple_of(n * ATT_BLOCK, ATT_BLOCK)
            band = pl.ds(r0, 2 * ATT_BLOCK)
            qb = q_ref[pl.ds(r0, ATT_BLOCK), :]
            dob = do_ref[pl.ds(r0, ATT_BLOCK), :]
            bias = bias0 + jnp.where(n == 0, 1.0, 0.0) * first_pen
            for g in range(N_KV_HEADS):
                qs = _stack_heads(qb, g, lo)
                dos = _stack_heads(dob, g, lo)
                k2 = k2s[g, band, :]
                v2 = v2s[g, band, :]
                probs_t, psink = _group_probs(qs, k2, bias, sink_rows[g])
                dp_t = _dot_nt(v2, dos)
                delta = jnp.sum(probs_t * dp_t, axis=0, keepdims=True)
                ds_t = (probs_t * (dp_t - delta) * (HEAD_DIM ** -0.5)).astype(BF16)
                tsink = psink * delta
                for h in range(GQA_GROUP):
                    dsink = dsink + jnp.where(lane == g * GQA_GROUP + h,
                                              -jnp.sum(tsink[:, h * ATT_BLOCK:(h + 1) * ATT_BLOCK]), 0.0)
                _unstack_heads(_dot_tn(ds_t, k2), dq_ref, r0, g, lo)
                dkacc[g, band, :] = dkacc[g, band, :] + _dot(ds_t, qs)
                dvacc[g, band, :] = dvacc[g, band, :] + _dot(probs_t.astype(BF16), dos)
            return dsink

        dsink = lax.fori_loop(0, nblk, blk, jnp.zeros((1, PAIR_W), F32))
        _acc(dsink_ref, dsink, pl.program_id(0) == 0)

        def fold(acc, g):
            a = acc[g, pl.ds(ATT_BLOCK, seq), :]
            return a + pltpu.roll(a, HEAD_DIM, 1)

        dk_ref[...] = jnp.where(lo, fold(dkacc, 0), fold(dkacc, 1)).astype(BF16)
        dv_ref[...] = jnp.where(lo, fold(dvacc, 0), fold(dvacc, 1)).astype(BF16)

    blocks = [((seq, QW), BF16)] * 3 + [((seq, KV_WIDTH), BF16)] * 4
    kv_spec_out = pl.BlockSpec((seq, KV_WIDTH), lambda b: (b, 0))
    return _call(
        body, (projp, projp, projp, dao, sinks), name=name, grid=(T // seq,),
        in_specs=[pl.BlockSpec((seq, QW), lambda b: (b, q_blk)),
                  pl.BlockSpec((seq, KV_WIDTH), lambda b: (b, k_blk)),
                  pl.BlockSpec((seq, KV_WIDTH), lambda b: (b, v_blk)),
                  pl.BlockSpec((seq, QW), lambda b: (b, 0)),
                  pl.BlockSpec(memory_space=pltpu.SMEM)],
        out_specs=[pl.BlockSpec((seq, QW), lambda b: (b, 0)), kv_spec_out, kv_spec_out,
                   pl.BlockSpec((1, 128), lambda b: (0, 0))],
        out_shape=[SDS((T, QW), BF16), SDS((T, KV_WIDTH), BF16), SDS((T, KV_WIDTH), BF16), SDS((1, 128), F32)],
        scratch_shapes=[pltpu.VMEM((N_KV_HEADS, seq + ATT_BLOCK, PAIR_W), BF16)] * 2
        + [pltpu.VMEM((N_KV_HEADS, seq + ATT_BLOCK, PAIR_W), F32)] * 2,
        params=_params(1, blocks, temp_bytes=24 * 2**20), comm=comm)


SUBLANES = 8


def _sublane_shifts(win):
    n = CONV_ROWS + CONV_HALO
    return [win] + [pltpu.roll(win, n - b, 0) for b in range(1, SUBLANES)]


def _window(shifted, off):
    a = off // SUBLANES * SUBLANES
    return shifted[off % SUBLANES][a:a + CONV_ROWS, :]


def _conv_fwd(projp, w, bias, *, seq, cw, a_col, b_col, name, comm=None):
    T = projp.shape[0]
    C = w.shape[1]
    nchunk = seq // CONV_ROWS

    def body(a_ref, b_ref, w_ref, bias_ref, y_ref, upad):
        upad[pl.ds(0, CONV_HALO), :] = jnp.zeros((CONV_HALO, cw), F32)
        upad[pl.ds(CONV_HALO, seq), :] = a_ref[...].astype(F32) * _sigmoid(b_ref[...].astype(F32))
        wv = w_ref[...]
        bv = bias_ref[...]

        def chunk(r, carry):
            r0 = pl.multiple_of(r * CONV_ROWS, CONV_ROWS)
            shifted = _sublane_shifts(upad[pl.ds(r0, CONV_ROWS + CONV_HALO), :])
            acc = jnp.broadcast_to(bv, (CONV_ROWS, cw))
            for k in range(CONV_WIDTH):
                acc = acc + wv[k:k + 1, :] * _window(shifted, CONV_HALO - (CONV_WIDTH - 1) + k)
            y_ref[pl.ds(r0, CONV_ROWS), :] = acc
            return carry

        lax.fori_loop(0, nchunk, chunk, 0)

    blocks = [((seq, cw), BF16)] * 2 + [((seq, cw), F32)]
    return _call(
        body, (projp, projp, w, bias), name=name, grid=(T // seq, C // cw),
        in_specs=[pl.BlockSpec((seq, cw), lambda b, c: (b, a_col // cw + c)),
                  pl.BlockSpec((seq, cw), lambda b, c: (b, b_col // cw + c)),
                  pl.BlockSpec((CONV_WIDTH, cw), lambda b, c: (0, c)),
                  pl.BlockSpec((1, cw), lambda b, c: (0, c))],
        out_specs=[pl.BlockSpec((seq, cw), lambda b, c: (b, c))],
        out_shape=[SDS((T, C), F32)],
        scratch_shapes=[pltpu.VMEM((seq + CONV_HALO, cw), F32)],
        params=_params(2, blocks, temp_bytes=6 * _nbytes((seq, cw), F32)), comm=comm)[0]


def _conv_bwd(dy, projp, w, *, seq, cw, a_col, b_col, name, comm=None):
    T = projp.shape[0]
    C = w.shape[1]
    nchunk = seq // CONV_ROWS
    SUB = 8

    def body(dy_ref, a_ref, b_ref, w_ref, da_ref, db_ref, dw_ref, dbias_ref, dypad, dwp):
        first = pl.program_id(1) == 0
        dyv = dy_ref[...]
        dypad[pl.ds(0, seq), :] = dyv
        dypad[pl.ds(seq, CONV_HALO), :] = jnp.zeros((CONV_HALO, cw), F32)
        dwp[...] = jnp.zeros(dwp.shape, F32)
        wv = w_ref[...]

        def chunk(r, carry):
            r0 = pl.multiple_of(r * CONV_ROWS, CONV_ROWS)
            dy_shifts = _sublane_shifts(dypad[pl.ds(r0, CONV_ROWS + CONV_HALO), :])
            ac = a_ref[pl.ds(r0, CONV_ROWS), :].astype(F32)
            sbc = _sigmoid(b_ref[pl.ds(r0, CONV_ROWS), :].astype(F32))
            uc = ac * sbc
            du = jnp.zeros((CONV_ROWS, cw), F32)
            for k in range(CONV_WIDTH):
                dyk = _window(dy_shifts, CONV_WIDTH - 1 - k)
                du = du + wv[k:k + 1, :] * dyk
                prod = uc * dyk
                part = prod[0:SUB, :]
                for s in range(1, CONV_ROWS // SUB):
                    part = part + prod[s * SUB:(s + 1) * SUB, :]
                dwp[pl.ds(k * SUB, SUB), :] = dwp[pl.ds(k * SUB, SUB), :] + part
            da_ref[pl.ds(r0, CONV_ROWS), :] = (du * sbc).astype(BF16)
            db_ref[pl.ds(r0, CONV_ROWS), :] = (du * ac * (sbc * (1.0 - sbc))).astype(BF16)
            return carry

        lax.fori_loop(0, nchunk, chunk, 0)

        @pl.when(first)
        def _():
            dw_ref[...] = jnp.zeros(dw_ref.shape, F32)
            dbias_ref[...] = jnp.zeros(dbias_ref.shape, F32)

        for k in range(CONV_WIDTH):
            dw_ref[k:k + 1, :] = dw_ref[k:k + 1, :] + _rowsum(dwp[pl.ds(k * SUB, SUB), :])
        dbias_ref[...] = dbias_ref[...] + _rowsum(dyv)

    blocks = [((seq, cw), F32)] + [((seq, cw), BF16)] * 4
    return _call(
        body, (dy, projp, projp, w), name=name, grid=(C // cw, T // seq),
        in_specs=[pl.BlockSpec((seq, cw), lambda c, b: (b, c)),
                  pl.BlockSpec((seq, cw), lambda c, b: (b, a_col // cw + c)),
                  pl.BlockSpec((seq, cw), lambda c, b: (b, b_col // cw + c)),
                  pl.BlockSpec((CONV_WIDTH, cw), lambda c, b: (0, c))],
        out_specs=[pl.BlockSpec((seq, cw), lambda c, b: (b, c)), pl.BlockSpec((seq, cw), lambda c, b: (b, c)),
                   pl.BlockSpec((CONV_WIDTH, cw), lambda c, b: (0, c)), pl.BlockSpec((1, cw), lambda c, b: (0, c))],
        out_shape=[SDS((T, C), BF16), SDS((T, C), BF16), SDS((CONV_WIDTH, C), F32), SDS((1, C), F32)],
        scratch_shapes=[pltpu.VMEM((seq + CONV_HALO, cw), F32), pltpu.VMEM((CONV_WIDTH * SUB, cw), F32)],
        params=_params(2, blocks, temp_bytes=8 * _nbytes((seq, cw), F32)), comm=comm)


def _matmul_tn(a, b, *, name, comm=None):
    T, M = a.shape
    N = b.shape[1]
    bm = _pick(M, (768, 512, 256))

    def body(a_ref, b_ref, o_ref):
        o_ref[...] = _dot_tn(a_ref[...], b_ref[...]).astype(BF16)

    blocks = [((T, bm), BF16), ((T, N), BF16), ((bm, N), BF16)]
    return _call(
        body, (a, b), name=name, grid=(M // bm,),
        in_specs=[pl.BlockSpec((T, bm), lambda i: (0, i)), pl.BlockSpec((T, N), lambda i: (0, 0))],
        out_specs=[pl.BlockSpec((bm, N), lambda i: (i, 0))],
        out_shape=[SDS((M, N), BF16)],
        params=_params(1, blocks, temp_bytes=2 * _nbytes((T, bm), BF16) + 2 * _nbytes((bm, N), F32)),
        comm=comm)[0]


def _sum_parts(p_ref):
    g = p_ref[0].astype(F32)
    for s in range(1, p_ref.shape[0]):
        g = g + p_ref[s].astype(F32)
    return g


def _pair_add(g, staged, *, name):
    _, R, W = g.shape
    nq = staged.shape[0]
    tr = _row_tile(R)

    def body(g_ref, s_ref, o_ref):
        mine = jnp.where(lax.axis_index("c") == 0, g_ref[0, 0].astype(F32), g_ref[0, 1].astype(F32))
        o_ref[0] = (mine + s_ref[0].astype(F32)).astype(o_ref.dtype)

    return _call(
        body, (g.reshape(nq, 2, R, W), staged), name=name, grid=(nq, R // tr),
        in_specs=[pl.BlockSpec((1, 2, tr, W), lambda q, i: (q, 0, i, 0)),
                  pl.BlockSpec((1, tr, W), lambda q, i: (q, i, 0))],
        out_specs=[pl.BlockSpec((1, tr, W), lambda q, i: (q, i, 0))],
        out_shape=[SDS((nq, R, W), g.dtype)],
        params=_params(2, [((4, tr, W), g.dtype)], temp_bytes=3 * _nbytes((tr, W), F32)))[0]


def _adamw_update(w, g, m, v):
    m = ADAM_B1 * m + (1.0 - ADAM_B1) * g
    v = ADAM_B2 * v + (1.0 - ADAM_B2) * (g * g)
    m_hat = m / (1.0 - ADAM_B1 ** ADAM_STEP)
    v_hat = v / (1.0 - ADAM_B2 ** ADAM_STEP)
    delta = -ADAM_LR * (m_hat / (jnp.sqrt(v_hat) + ADAM_EPS) + ADAM_WD * w)
    return delta, m, v


def _row_tile(R):
    return _pick(R, (256, 128, 112, 88, 64, 32, 16, 8))


def _sum8(parts, *, name):
    n, R, W = parts.shape
    tr = _row_tile(R)

    def body(p_ref, o_ref):
        o_ref[...] = _sum_parts(p_ref)

    return _call(
        body, (parts,), name=name, grid=(R // tr,),
        in_specs=[pl.BlockSpec((n, tr, W), lambda i: (0, i, 0))],
        out_specs=[pl.BlockSpec((tr, W), lambda i: (i, 0))],
        out_shape=[SDS((R, W), F32)],
        params=_params(1, [((n, tr, W), parts.dtype), ((tr, W), F32)]))[0]


def _adamw(g, w, m, v, *, name):
    R, W = w.shape
    tr = _row_tile(R)

    def body(g_ref, w_ref, m_ref, v_ref, d_ref, mo_ref, vo_ref):
        d_ref[...], mo_ref[...], vo_ref[...] = _adamw_update(w_ref[...], g_ref[...], m_ref[...], v_ref[...])

    spec = pl.BlockSpec((tr, W), lambda i: (i, 0))
    return _call(
        body, (g, w, m, v), name=name, grid=(R // tr,),
        in_specs=[spec] * 4, out_specs=[spec] * 3, out_shape=[SDS((R, W), F32)] * 3,
        params=_params(1, [((tr, W), F32)] * 7))


def _sum8_adamw(parts, w, m, v, *, name):
    R, W = w.shape
    n = parts.shape[0]
    tr = _row_tile(R)

    def body(p_ref, w_ref, m_ref, v_ref, g_ref, d_ref, mo_ref, vo_ref):
        g = _sum_parts(p_ref)
        g_ref[...] = g
        d_ref[...], mo_ref[...], vo_ref[...] = _adamw_update(w_ref[...], g, m_ref[...], v_ref[...])

    spec = pl.BlockSpec((tr, W), lambda i: (i, 0))
    return _call(
        body, (parts, w, m, v), name=name, grid=(R // tr,),
        in_specs=[pl.BlockSpec((n, tr, W), lambda i: (0, i, 0))] + [spec] * 3,
        out_specs=[spec] * 4, out_shape=[SDS((R, W), F32)] * 4,
        params=_params(1, [((n, tr, W), parts.dtype)] + [((tr, W), F32)] * 7))


def _ada_fwd(c_all, w, bias, *, name):
    NB, D = c_all.shape
    N = w.shape[1]

    def body(c_ref, w_ref, b_ref, o_ref):
        cv = c_ref[...]
        ca = (cv * _sigmoid(cv)).astype(BF16)
        o_ref[...] = _dot(ca, w_ref[...].astype(BF16)) + b_ref[...]

    full = lambda s: pl.BlockSpec(s, lambda i: (0,) * len(s))
    return _call(
        body, (c_all, w, bias), name=name, grid=(1,),
        in_specs=[full((NB, D)), full((D, N)), full((1, N))], out_specs=[full((NB, N))],
        out_shape=[SDS((NB, N), F32)],
        params=_params(1, [((D, N), F32)], temp_bytes=_nbytes((D, N), BF16)))[0]


def _ada_bwd(c_all, gmod_all, *, n_col, name):
    NB, D = c_all.shape
    N = gmod_all.shape[1]

    def body(c_ref, g_ref, gw_ref, gb_ref):
        cv = c_ref[...]
        ca = (cv * _sigmoid(cv)).astype(BF16)
        first = pl.multiple_of(_lin(_my_pos()) * n_col, 128)
        gw_ref[...] = _dot_tn(ca, g_ref[:, pl.ds(first, n_col)].astype(BF16))
        gb_ref[...] = _rowsum(g_ref[...])

    full = lambda s: pl.BlockSpec(s, lambda i: (0,) * len(s))
    return _call(
        body, (c_all, gmod_all), name=name, grid=(1,),
        in_specs=[full((NB, D)), full((NB, N))], out_specs=[full((D, n_col)), full((1, N))],
        out_shape=[SDS((D, n_col), F32), SDS((1, N), F32)],
        params=_params(1, [((D, n_col), F32), ((NB, N), F32)]))


def kernel(x, c, w_ada, b_ada, norm_ffn1_g, ffn1_w_gate, ffn1_w_up, ffn1_w_down, norm_mix_g, w_in, attn_sinks, w_attn_o, conv_w_dw, conv_b_dw, conv_ln_g, conv_ln_b, w_conv_o, w_out, norm_ffn2_g, ffn2_w_gate, ffn2_w_up, ffn2_w_down, final_norm_g, loss_target, m_w_ada, m_b_ada, m_norm_ffn1_g, m_ffn1_w_gate, m_ffn1_w_up, m_ffn1_w_down, m_norm_mix_g, m_w_in, m_attn_sinks, m_w_attn_o, m_conv_w_dw, m_conv_b_dw, m_conv_ln_g, m_conv_ln_b, m_w_conv_o, m_w_out, m_norm_ffn2_g, m_ffn2_w_gate, m_ffn2_w_up, m_ffn2_w_down, m_final_norm_g, v_w_ada, v_b_ada, v_norm_ffn1_g, v_ffn1_w_gate, v_ffn1_w_up, v_ffn1_w_down, v_norm_mix_g, v_w_in, v_attn_sinks, v_w_attn_o, v_conv_w_dw, v_conv_b_dw, v_conv_ln_g, v_conv_ln_b, v_w_conv_o, v_w_out, v_norm_ffn2_g, v_ffn2_w_gate, v_ffn2_w_up, v_ffn2_w_down, v_final_norm_g):
    B, S, D = x.shape
    T = B * S
    QW = N_Q_HEADS * HEAD_DIM
    CC = conv_w_dw.shape[2] * N_DEV
    me = _lin(_my_pos())
    xf = x.reshape(T, D)
    tgt = loss_target.reshape(T, D)
    tm = min(512, S)
    kw = dict(seq=S, tm=tm)

    p_k, p_v, p_ca = QW, QW + KV_WIDTH, QW + 2 * KV_WIDTH
    p_cb, p_ga, p_gc = p_ca + CC, p_ca + 2 * CC, p_ca + 2 * CC + D

    def col_t(w):
        return w[0].T.astype(BF16)

    def row_b(w):
        return w[0].astype(BF16)

    def rows(g):
        return g.reshape(-1, g.shape[-1])

    def blocks8(g):
        return g.reshape(N_DEV, g.shape[0] // N_DEV, g.shape[1])

    def gather(*arrs):
        return _Comm([(a, "gather") for a in arrs])

    g_wg1, g_convw, g_c = _exchange(
        [(col_t(ffn1_w_gate), "gather"), (conv_w_dw[0], "gather"), (c, "gather")], name="gather_first")
    wg1 = rows(g_wg1)
    conv_w = g_convw.transpose(1, 0, 2).reshape(CONV_WIDTH, CC)
    c_all = g_c.reshape(N_DEV * B, D)

    n_col = N_MOD * D // N_DEV
    b_cols = lax.dynamic_slice(b_ada, (0, me * n_col), (1, n_col))
    mod_cols = _ada_fwd(c_all, w_ada[0], b_cols, name="ada_fwd")
    mod_mine = _exchange([(mod_cols.reshape(N_DEV, B, n_col), "scatter")], name="scatter_mod")[0]
    mod = mod_mine.transpose(1, 0, 2).reshape(B * N_MOD, 1, D)
    sh1, sc1, g1, sh2, sc2, g2, sh3, sc3, g3 = [_ModVec(mod, i) for i in range(N_MOD)]

    F = wg1.shape[0]
    tn_f = _pick(F, (1408, 1024, 512, 256))
    tn_in = _pick(w_in.shape[2] * N_DEV, (1792, 768, 512, 256))
    gate_blk = dict(ga_col=p_ga, gc_col=p_gc)
    att_blk = dict(q_blk=0, k_blk=p_k // KV_WIDTH, v_blk=p_v // KV_WIDTH)
    conv_kw = dict(seq=S, cw=256, a_col=p_ca, b_col=p_cb)

    cm = gather(col_t(ffn1_w_up))
    h1, (a1,) = _norm_mod_matmul(xf, norm_ffn1_g, sh1, sc1, [wg1], tn=tn_f, name="ffn1_gate", comm=cm, **kw)
    wu1 = rows(cm.out[0])
    cm = gather(row_b(ffn1_w_down))
    b1 = _matmul_nt(h1, wu1, tm=tm, tn=tn_f, name="ffn1_up", comm=cm)
    wd1 = rows(cm.out[0])
    cm = gather(col_t(w_in))
    x1, y1 = _ffn_down(a1, b1, wd1, xf, g1, name="ffn1_down", comm=cm, **kw)
    winp = rows(cm.out[0])
    cm = gather(row_b(w_attn_o), row_b(w_conv_o), row_b(w_out), col_t(ffn2_w_gate))
    h2, (projp,) = _norm_mod_matmul(x1, norm_mix_g, sh2, sc2, [winp], tn=tn_in, name="mix_in", comm=cm, **kw)
    wao, wco, wout, wg2 = [rows(o) for o in cm.out]
    cm = gather(col_t(ffn2_w_up))
    ao = _attn_fwd(projp, attn_sinks, seq=S, name="attn_fwd", comm=cm, **att_blk)
    wu2 = rows(cm.out[0])
    cm = gather(row_b(ffn2_w_down))
    yc = _conv_fwd(projp, conv_w, conv_b_dw, name="conv_fwd", comm=cm, **conv_kw)
    wd2 = rows(cm.out[0])
    x2, z, ya, ycv, cact, merged = _mix_out(ao, yc, projp, wao, wco, wout, x1, g2, conv_ln_g, conv_ln_b,
                                            name="mix_out", **gate_blk, **kw)
    h3, (a3, b3) = _norm_mod_matmul(x2, norm_ffn2_g, sh3, sc3, [wg2, wu2], tn=tn_f, name="ffn2_up", **kw)
    x3, y3 = _ffn_down(a3, b3, wd2, x2, g3, name="ffn2_down", **kw)
    dx3, loss_row, dgf = _final_loss(x3, final_norm_g[None], tgt, tm=tm, name="final_loss")

    parts = {}

    def pair(*gs):
        return [(blocks8(g), "pair") for g in gs]

    def cross(*rs):
        return [(r, "cross") for r in rs]

    def reduce_pairs(gs, staged, names):
        return [_pair_add(blocks8(g), s, name="pair_add_" + n) for g, s, n in zip(gs, staged, names)]

    dyb3, da3, db3, act3, dg3 = _ffn_bwd_down(dx3, g3, y3, wd2, a3, b3, tn=tn_f, name="ffn2_bwd_down", **kw)
    gwd2 = _matmul_tn(act3, dyb3, name="gw_ffn2_down")
    cm = _Comm(pair(gwd2))
    dx2, dsh3, dsc3, dgn3 = _matmul_norm_mod_bwd([da3, db3], [wg2, wu2], x2, norm_ffn2_g, sc3, dx3,
                                                 name="ffn2_bwd_up", comm=cm, **kw)
    r_wd2, = reduce_pairs([gwd2], cm.out, ["ffn2_w_down"])
    cm = _Comm(cross(r_wd2))
    gwg2 = _matmul_tn(da3, h3, name="gw_ffn2_gate", comm=cm)
    parts["ffn2_w_down"], = cm.out
    cm = _Comm(pair(gwg2))
    gwu2 = _matmul_tn(db3, h3, name="gw_ffn2_up", comm=cm)
    r_wg2, = reduce_pairs([gwg2], cm.out, ["ffn2_w_gate"])

    cm = _Comm(cross(r_wg2) + pair(gwu2))
    dzb, dyab, dycb, dga, dgc, dao, dyc, dg2, dlng, dlnb = _mix_out_bwd(
        dx2, g2, z, wout, projp, ya, ycv, wao, wco, yc, conv_ln_g, conv_ln_b, name="mix_out_bwd", comm=cm,
        **gate_blk, **kw)
    parts["ffn2_w_gate"] = cm.out[0]
    r_wu2, = reduce_pairs([gwu2], cm.out[1:], ["ffn2_w_up"])
    gwout = _matmul_tn(merged, dzb, name="gw_out")
    gwao = _matmul_tn(ao, dyab, name="gw_attn_o")
    gwco = _matmul_tn(cact, dycb, name="gw_conv_o")
    cm = _Comm(cross(r_wu2) + pair(gwout, gwao, gwco))
    dq, dk, dv, dsinks = _attn_bwd(projp, dao, attn_sinks, seq=S, name="attn_bwd", comm=cm, **att_blk)
    parts["ffn2_w_up"] = cm.out[0]
    r_mix = reduce_pairs([gwout, gwao, gwco], cm.out[1:], ["w_out", "w_attn_o", "w_conv_o"])
    cm = _Comm(cross(*r_mix))
    dca, dcb, dconvw, dconvb = _conv_bwd(dyc, projp, conv_w, name="conv_bwd", comm=cm, **conv_kw)
    parts["w_out"], parts["w_attn_o"], parts["w_conv_o"] = cm.out
    dprojp = jnp.concatenate([dq, dk, dv, dca, dcb, dga, dgc], axis=1)
    gwin = _matmul_tn(dprojp, h2, name="gw_in")
    cm = _Comm(pair(gwin))
    dx1, dsh2, dsc2, dgn2 = _matmul_norm_mod_bwd([dprojp], [winp], x1, norm_mix_g, sc2, dx2,
                                                 name="mix_in_bwd", comm=cm, **kw)
    r_win, = reduce_pairs([gwin], cm.out, ["w_in"])

    cm = _Comm(cross(r_win))
    dyb1, da1, db1, act1, dg1 = _ffn_bwd_down(dx1, g1, y1, wd1, a1, b1, tn=tn_f, name="ffn1_bwd_down", comm=cm,
                                              **kw)
    parts["w_in"], = cm.out
    gwd1 = _matmul_tn(act1, dyb1, name="gw_ffn1_down")
    cm = _Comm(pair(gwd1))
    gwg1 = _matmul_tn(da1, h1, name="gw_ffn1_gate", comm=cm)
    r_wd1, = reduce_pairs([gwd1], cm.out, ["ffn1_w_down"])
    cm = _Comm(cross(r_wd1) + pair(gwg1))
    gwu1 = _matmul_tn(db1, h1, name="gw_ffn1_up", comm=cm)
    parts["ffn1_w_down"] = cm.out[0]
    r_wg1, = reduce_pairs([gwg1], cm.out[1:], ["ffn1_w_gate"])
    r_wu1, = reduce_pairs([gwu1], _exchange(pair(gwu1), name="pair_last"), ["ffn1_w_up"])
    cm = _Comm(cross(r_wg1, r_wu1))
    dx0, dsh1, dsc1, dgn1 = _matmul_norm_mod_bwd([da1, db1], [wg1, wu1], xf, norm_ffn1_g, sc1, dx1,
                                                 name="ffn1_bwd_up", comm=cm, **kw)
    parts["ffn1_w_gate"], parts["ffn1_w_up"] = cm.out

    n_small = 8
    gmod = jnp.concatenate([dsh1, dsc1, dg1, dsh2, dsc2, dg2, dsh3, dsc3, dg3], axis=1).reshape(B, N_MOD * D)
    sink_row = jnp.pad(dsinks[:, :N_Q_HEADS], ((0, 0), (0, D - N_Q_HEADS)))
    loss_pad = jnp.pad(loss_row, ((0, 0), (0, D - loss_row.shape[1])))
    small = jnp.concatenate([dgn1, dgn2, dgn3, dgf, dconvb, dlng, dlnb, sink_row, dconvw, loss_pad], axis=0)
    small_all, gmod_all = _exchange([(small, "gather"), (gmod, "gather")], name="exchange_last")
    gsmall = _sum8(small_all, name="sum_small")
    loss = gsmall[n_small + CONV_WIDTH, 0]
    g_w_ada, g_b_ada = _ada_bwd(c_all, gmod_all.reshape(N_DEV * B, N_MOD * D), n_col=n_col, name="ada_bwd")
    g_conv_w = lax.dynamic_slice(gsmall[n_small:n_small + CONV_WIDTH], (0, me * (CC // N_DEV)),
                                 (CONV_WIDTH, CC // N_DEV))

    def col_update(name, w, m, v):
        outs = _sum8_adamw(parts[name], w[0].T, m[0].T, v[0].T, name="adamw_" + name)
        return tuple(o.T for o in outs)

    def row_update(name, w, m, v):
        return tuple(_sum8_adamw(parts[name], w[0], m[0], v[0], name="adamw_" + name))

    upd = {
        "ffn1_w_gate": col_update("ffn1_w_gate", ffn1_w_gate, m_ffn1_w_gate, v_ffn1_w_gate),
        "ffn1_w_up": col_update("ffn1_w_up", ffn1_w_up, m_ffn1_w_up, v_ffn1_w_up),
        "ffn1_w_down": row_update("ffn1_w_down", ffn1_w_down, m_ffn1_w_down, v_ffn1_w_down),
        "w_in": col_update("w_in", w_in, m_w_in, v_w_in),
        "w_attn_o": row_update("w_attn_o", w_attn_o, m_w_attn_o, v_w_attn_o),
        "w_conv_o": row_update("w_conv_o", w_conv_o, m_w_conv_o, v_w_conv_o),
        "w_out": row_update("w_out", w_out, m_w_out, v_w_out),
        "ffn2_w_gate": col_update("ffn2_w_gate", ffn2_w_gate, m_ffn2_w_gate, v_ffn2_w_gate),
        "ffn2_w_up": col_update("ffn2_w_up", ffn2_w_up, m_ffn2_w_up, v_ffn2_w_up),
        "ffn2_w_down": row_update("ffn2_w_down", ffn2_w_down, m_ffn2_w_down, v_ffn2_w_down),
        "w_ada": (g_w_ada,) + tuple(_adamw(g_w_ada, w_ada[0], m_w_ada[0], v_w_ada[0], name="adamw_w_ada")),
        "conv_w_dw": (g_conv_w,) + tuple(_adamw(g_conv_w, conv_w_dw[0], m_conv_w_dw[0], v_conv_w_dw[0],
                                                name="adamw_conv_w_dw")),
    }
    for k in upd:
        upd[k] = tuple(t[None] for t in upd[k])

    def pad_sinks(t):
        return jnp.pad(t, ((0, 0), (0, D - N_Q_HEADS)))

    def pack(f1, mix, f2, fin, cb, lg, lb, sinks, bada):
        return jnp.concatenate([f1, mix, f2, fin[None], cb, lg, lb, pad_sinks(sinks), bada.reshape(N_MOD, D)], axis=0)

    w_s = pack(norm_ffn1_g, norm_mix_g, norm_ffn2_g, final_norm_g, conv_b_dw, conv_ln_g, conv_ln_b, attn_sinks, b_ada)
    m_s = pack(m_norm_ffn1_g, m_norm_mix_g, m_norm_ffn2_g, m_final_norm_g, m_conv_b_dw, m_conv_ln_g, m_conv_ln_b,
               m_attn_sinks, m_b_ada)
    v_s = pack(v_norm_ffn1_g, v_norm_mix_g, v_norm_ffn2_g, v_final_norm_g, v_conv_b_dw, v_conv_ln_g, v_conv_ln_b,
               v_attn_sinks, v_b_ada)
    g_s = jnp.concatenate([gsmall[:n_small], g_b_ada.reshape(N_MOD, D)], axis=0)
    small_out = (g_s,) + tuple(_adamw(g_s, w_s, m_s, v_s, name="adamw_vectors"))

    def unpack(t):
        return {
            "norm_ffn1_g": t[0:1], "norm_mix_g": t[1:2], "norm_ffn2_g": t[2:3], "final_norm_g": t[3],
            "conv_b_dw": t[4:5], "conv_ln_g": t[5:6], "conv_ln_b": t[6:7], "attn_sinks": t[7:8, :N_Q_HEADS],
            "b_ada": t[n_small:n_small + N_MOD].reshape(1, N_MOD * D),
        }

    small_un = [unpack(t) for t in small_out]
    for k in small_un[0]:
        upd[k] = tuple(s[k] for s in small_un)

    order = ["w_ada", "b_ada", "norm_ffn1_g", "ffn1_w_gate", "ffn1_w_up", "ffn1_w_down", "norm_mix_g", "w_in",
             "attn_sinks", "w_attn_o", "conv_w_dw", "conv_b_dw", "conv_ln_g", "conv_ln_b", "w_conv_o", "w_out",
             "norm_ffn2_g", "ffn2_w_gate", "ffn2_w_up", "ffn2_w_down", "final_norm_g"]
    grad_x = dx0.reshape(B, S, D)
    return (loss, grad_x, *[upd[k][0] for k in order], *[upd[k][1] for k in order],
            *[upd[k][2] for k in order], *[upd[k][3] for k in order])
```

```python
import jax
import jax.numpy as jnp
from jax import lax
from jax.experimental import pallas as pl
from jax.experimental.pallas import tpu as pltpu

F32 = jnp.float32
BF16 = jnp.bfloat16
SDS = jax.ShapeDtypeStruct
MESH = pl.DeviceIdType.MESH

N_DEV = 8
EPS = 1e-6
HEAD_DIM = 64
N_Q_HEADS = 16
N_KV_HEADS = 2
GQA_GROUP = N_Q_HEADS // N_KV_HEADS
KV_WIDTH = N_KV_HEADS * HEAD_DIM
ATT_BLOCK = 128
CONV_WIDTH = 31
CONV_HALO = 32
CONV_ROWS = 64
N_MOD = 9
FFN_RESIDUAL = 0.5
ADAM_LR = 0.001
ADAM_B1 = 0.9
ADAM_B2 = 0.999
ADAM_EPS = 1e-08
ADAM_WD = 0.01
ADAM_STEP = 10
NEG_BIG = -1e30

V7X_VMEM_BYTES = 64 * 2**20
VMEM_CAP = V7X_VMEM_BYTES - 8 * 2**20


def _nbytes(shape, dtype):
    n = 1
    for s in shape:
        n *= s
    return n * jnp.dtype(dtype).itemsize


def _params(n_axes, blocks, temp_bytes=0):
    need = 2 * sum(_nbytes(s, d) for s, d in blocks) + temp_bytes + 4 * 2**20
    return pltpu.CompilerParams(dimension_semantics=("arbitrary",) * n_axes,
                                vmem_limit_bytes=int(min(max(need, 16 * 2**20), VMEM_CAP)))


def _dot_nt(a, b):
    return lax.dot_general(a, b, (((1,), (1,)), ((), ())), preferred_element_type=F32)


def _dot_tn(a, b):
    return lax.dot_general(a, b, (((0,), (0,)), ((), ())), preferred_element_type=F32)


def _dot(a, b):
    return jnp.dot(a, b, preferred_element_type=F32)


def _sigmoid(x):
    return jax.nn.sigmoid(x)


def _rowsum(v):
    return jnp.sum(v, axis=0, keepdims=True)


def _acc(ref, val, first):
    @pl.when(first)
    def _():
        ref[...] = val

    @pl.when(jnp.logical_not(first))
    def _():
        ref[...] = ref[...] + val


def _norm_mod(xf, gn, sh, sc):
    rstd = lax.rsqrt(jnp.mean(xf * xf, axis=-1, keepdims=True) + EPS)
    xhat = xf * rstd
    yn = xhat * gn
    return yn * (1.0 + sc) + sh, xhat, rstd, yn


def _pick(n, cands):
    for c in cands:
        if n % c == 0:
            return c
    return n


def _my_pos():
    return lax.axis_index("x"), lax.axis_index("y"), lax.axis_index("c")


def _peer(pos, k):
    x, y, c = pos
    return ((1 - x) if k & 4 else x, (1 - y) if k & 2 else y, (1 - c) if k & 1 else c)


def _lin(pos):
    return 4 * pos[0] + 2 * pos[1] + pos[2]


class _Comm:
    N_COPY = N_DEV - 1
    N_CHIP = N_DEV // 2

    def __init__(self, items):
        self.arrs = [a for a, _ in items]
        self.modes = [m for _, m in items]
        self.n = len(items)
        self.out = None

    def out_shape(self):
        def shape(a, m):
            return {"gather": (N_DEV,) + a.shape, "scatter": a.shape, "pair": (self.N_CHIP,) + a.shape[1:],
                    "cross": a.shape}[m]
        return [SDS(shape(a, m), a.dtype) for a, m in zip(self.arrs, self.modes)]

    def scratch(self):
        return [pltpu.SemaphoreType.DMA((self.n * self.N_COPY,)), pltpu.SemaphoreType.DMA((self.n * self.N_COPY,)),
                pltpu.SemaphoreType.DMA((self.n,))]

    def _plan(self, mode, me):
        x, y, c = me
        sib = (x, y, 1 - c)
        chips = [(1 - x, y), (x, 1 - y), (1 - x, 1 - y)]

        def chip_lin(ch):
            return 2 * ch[0] + ch[1]

        if mode == "scatter":
            peers = [_peer(me, k + 1) for k in range(self.N_COPY)]
            return [(p, ("in", _lin(p)), _lin(me), _lin(p), None) for p in peers], (_lin(me), _lin(me))
        if mode == "gather":
            same = [(*ch, c) for ch in chips]
            other = [(*ch, 1 - c) for ch in chips]
            copies = [(sib, ("in", None), _lin(me), _lin(sib), None)]
            copies += [(p, ("in", None), _lin(me), _lin(p), None) for p in same]
            copies += [(sib, ("out", _lin(p)), _lin(p), _lin(o), 1 + j) for j, (p, o) in enumerate(zip(same, other))]
            return copies, (None, _lin(me))
        if mode == "pair":
            return [(sib, ("in", 2 * q + 1 - c), q, q, None) for q in range(self.N_CHIP)], None
        if mode == "cross":
            mine = chip_lin((x, y))
            return ([((*ch, c), ("in", chip_lin(ch)), mine, chip_lin(ch), None) for ch in chips], (mine, mine))
        raise ValueError(mode)

    def _copy(self, refs, me, i, k, recv):
        srcs, outs, (send_sems, recv_sems, _) = refs
        peer, (where, slot), send_slot, recv_slot, _ = self._plan(self.modes[i], me)[0][k]
        src = srcs[i] if where == "in" else outs[i]
        src = src if slot is None else src.at[slot]
        sem = i * self.N_COPY + k
        return pltpu.make_async_remote_copy(
            src_ref=src, dst_ref=outs[i].at[recv_slot if recv else send_slot], send_sem=send_sems.at[sem],
            recv_sem=recv_sems.at[sem], device_id=peer, device_id_type=MESH)

    def _local(self, refs, me, i):
        srcs, outs, (_, _, loc_sems) = refs
        local = self._plan(self.modes[i], me)[1]
        if local is None:
            return None
        own = srcs[i] if local[0] is None else srcs[i].at[local[0]]
        return pltpu.make_async_copy(own, outs[i].at[local[1]], loc_sems.at[i])

    def start(self, refs):
        me = _my_pos()
        for i in range(self.n):
            local = self._local(refs, me, i)
            if local is not None:
                local.start()
            for k, cp in enumerate(self._plan(self.modes[i], me)[0]):
                if cp[4] is None:
                    self._copy(refs, me, i, k, False).start()

    def forward(self, refs):
        me = _my_pos()
        for i in range(self.n):
            for k, cp in enumerate(self._plan(self.modes[i], me)[0]):
                if cp[4] is not None:
                    self._copy(refs, me, i, cp[4], True).wait_recv()
                    self._copy(refs, me, i, k, False).start()

    def finish(self, refs):
        me = _my_pos()
        plans = [self._plan(m, me)[0] for m in self.modes]
        for i in range(self.n):
            passed_on = [cp[4] for cp in plans[i] if cp[4] is not None]
            for k in range(len(plans[i])):
                if k not in passed_on:
                    self._copy(refs, me, i, k, True).wait_recv()
                self._copy(refs, me, i, k, False).wait_send()
            local = self._local(refs, me, i)
            if local is not None:
                local.wait()


_ANY = pl.BlockSpec(memory_space=pl.ANY)


def _call(body, args, *, name, grid, in_specs, out_specs, out_shape, params, scratch_shapes=(), comm=None):
    in_specs, out_specs, out_shape = list(in_specs), list(out_specs), list(out_shape)
    scratch_shapes = list(scratch_shapes)
    if comm is None:
        return list(pl.pallas_call(body, name=name, grid=grid, in_specs=in_specs, out_specs=out_specs,
                                   out_shape=out_shape, scratch_shapes=scratch_shapes, compiler_params=params)(*args))
    n_in, n_out, n_scr, nc = len(in_specs), len(out_specs), len(scratch_shapes), comm.n
    n_steps = 1
    for g in grid:
        n_steps *= g

    def hosted(*refs):
        ins, c_in = refs[:n_in], refs[n_in:n_in + nc]
        outs = refs[n_in + nc:n_in + nc + n_out]
        c_out = refs[n_in + nc + n_out:n_in + 2 * nc + n_out]
        scr = refs[n_in + 2 * nc + n_out:n_in + 2 * nc + n_out + n_scr]
        sems = refs[n_in + 2 * nc + n_out + n_scr:]
        step = pl.program_id(0)
        for d in range(1, len(grid)):
            step = step * grid[d] + pl.program_id(d)
        c_refs = (c_in, c_out, sems)

        @pl.when(step == 0)
        def _():
            comm.start(c_refs)

        if n_steps >= 3:
            @pl.when(step == n_steps - 2)
            def _():
                comm.forward(c_refs)

        body(*ins, *outs, *scr)

        @pl.when(step == n_steps - 1)
        def _():
            if n_steps < 3:
                comm.forward(c_refs)
            comm.finish(c_refs)

    res = pl.pallas_call(
        hosted, name=name, grid=grid, in_specs=in_specs + [_ANY] * nc, out_specs=out_specs + [_ANY] * nc,
        out_shape=out_shape + comm.out_shape(), scratch_shapes=scratch_shapes + comm.scratch(),
        compiler_params=params)(*args, *comm.arrs)
    comm.out = list(res[n_out:])
    return list(res[:n_out])


def _exchange(items, *, name):
    comm = _Comm(items)

    def body(*refs):
        r = (refs[:comm.n], refs[comm.n:2 * comm.n], refs[2 * comm.n:])
        comm.start(r)
        comm.forward(r)
        comm.finish(r)

    return list(pl.pallas_call(body, name=name, out_shape=comm.out_shape(), in_specs=[_ANY] * comm.n,
                               out_specs=[_ANY] * comm.n, scratch_shapes=comm.scratch())(*comm.arrs))


class _ModVec:
    def __init__(self, arr, idx):
        self.arr, self.idx = arr, idx

    def spec(self, tps, n_axes):
        idx, blk = self.idx, (1, 1, self.arr.shape[2])
        if n_axes == 1:
            return pl.BlockSpec(blk, lambda i: (i // tps * N_MOD + idx, 0, 0))
        return pl.BlockSpec(blk, lambda i, j: (i // tps * N_MOD + idx, 0, 0))


def _norm_mod_matmul(x, gn, sh, sc, wts, *, seq, tm, tn, name, comm=None):
    T, D = x.shape
    N = wts[0].shape[0]
    nw = len(wts)
    tps = seq // tm

    def body(x_ref, gn_ref, sh_ref, sc_ref, *rest):
        w_refs, h_ref, o_refs = rest[:nw], rest[nw], rest[nw + 1:]

        @pl.when(pl.program_id(1) == 0)
        def _():
            h_ref[...] = _norm_mod(x_ref[...], gn_ref[...], sh_ref[0], sc_ref[0])[0].astype(BF16)

        h = h_ref[...]
        for w_ref, o_ref in zip(w_refs, o_refs):
            o_ref[...] = _dot_nt(h, w_ref[...]).astype(o_ref.dtype)

    row = pl.BlockSpec((tm, D), lambda i, j: (i, 0))
    vec = pl.BlockSpec((1, D), lambda i, j: (0, 0))
    per_b = pl.BlockSpec((1, 1, D), lambda i, j: (i // tps, 0, 0))
    wspec = pl.BlockSpec((tn, D), lambda i, j: (j, 0))
    ospec = pl.BlockSpec((tm, tn), lambda i, j: (i, j))
    blocks = [((tm, D), F32), ((tm, D), BF16)] + [((tn, D), BF16), ((tm, tn), BF16)] * nw
    outs = _call(
        body, (x, gn, sh.arr, sc.arr, *wts), name=name, grid=(T // tm, N // tn),
        in_specs=[row, vec, sh.spec(tps, 2), sc.spec(tps, 2)] + [wspec] * nw,
        out_specs=[row] + [ospec] * nw,
        out_shape=[SDS((T, D), BF16)] + [SDS((T, N), BF16)] * nw,
        params=_params(2, blocks, temp_bytes=2 * _nbytes((tm, tn), F32) + 3 * _nbytes((tm, D), F32)), comm=comm)
    return outs[0], outs[1:]


def _matmul_nt(h, w, *, tm, tn, name, comm=None):
    T, D = h.shape
    N = w.shape[0]

    def body(h_ref, w_ref, o_ref):
        o_ref[...] = _dot_nt(h_ref[...], w_ref[...]).astype(o_ref.dtype)

    blocks = [((tm, D), BF16), ((tn, D), BF16), ((tm, tn), BF16)]
    return _call(
        body, (h, w), name=name, grid=(T // tm, N // tn),
        in_specs=[pl.BlockSpec((tm, D), lambda i, j: (i, 0)), pl.BlockSpec((tn, D), lambda i, j: (j, 0))],
        out_specs=[pl.BlockSpec((tm, tn), lambda i, j: (i, j))],
        out_shape=[SDS((T, N), BF16)],
        params=_params(2, blocks, temp_bytes=2 * _nbytes((tm, tn), F32)), comm=comm)[0]


def _ffn_down(a, b, wd, x, g, *, seq, tm, name, comm=None):
    T, F = a.shape
    D = wd.shape[1]
    tps = seq // tm

    def body(a_ref, b_ref, wd_ref, x_ref, g_ref, xo_ref, y_ref):
        af = a_ref[...].astype(F32)
        act = (af * _sigmoid(af) * b_ref[...].astype(F32)).astype(BF16)
        y = _dot(act, wd_ref[...])
        xo_ref[...] = x_ref[...] + (FFN_RESIDUAL * g_ref[0]) * y
        y_ref[...] = y.astype(BF16)

    wide = pl.BlockSpec((tm, F), lambda i: (i, 0))
    row = pl.BlockSpec((tm, D), lambda i: (i, 0))
    per_b = pl.BlockSpec((1, 1, D), lambda i: (i // tps, 0, 0))
    wspec = pl.BlockSpec((F, D), lambda i: (0, 0))
    blocks = [((tm, F), BF16)] * 2 + [((F, D), BF16), ((tm, D), F32), ((tm, D), F32), ((tm, D), BF16)]
    return _call(
        body, (a, b, wd, x, g.arr), name=name, grid=(T // tm,),
        in_specs=[wide, wide, wspec, row, g.spec(tps, 1)], out_specs=[row, row],
        out_shape=[SDS((T, D), F32), SDS((T, D), BF16)],
        params=_params(1, blocks, temp_bytes=3 * _nbytes((tm, F), F32)), comm=comm)


def _final_loss(x, gf, tgt, *, tm, name):
    T, D = x.shape
    nt = T // tm

    def body(x_ref, gf_ref, t_ref, dx_ref, loss_ref, dgf_ref, lacc):
        i = pl.program_id(0)
        xf = x_ref[...]
        gfv = gf_ref[...]
        rstd = lax.rsqrt(jnp.mean(xf * xf, axis=-1, keepdims=True) + EPS)
        xhat = xf * rstd
        err = xhat * gfv - t_ref[...]
        dy = err * (1.0 / D)
        dxhat = dy * gfv
        dx_ref[...] = rstd * (dxhat - xhat * jnp.mean(dxhat * xhat, axis=-1, keepdims=True))
        _acc(dgf_ref, _rowsum(dy * xhat), i == 0)
        _acc(lacc, _rowsum(err * err), i == 0)

        @pl.when(i == nt - 1)
        def _():
            loss_ref[...] = jnp.broadcast_to((0.5 / D) * jnp.sum(lacc[...]), loss_ref.shape)

    row = pl.BlockSpec((tm, D), lambda i: (i, 0))
    vec = pl.BlockSpec((1, D), lambda i: (0, 0))
    lspec = pl.BlockSpec((1, 128), lambda i: (0, 0))
    blocks = [((tm, D), F32)] * 3
    return _call(
        body, (x, gf, tgt), name=name, grid=(nt,),
        in_specs=[row, vec, row], out_specs=[row, lspec, vec],
        out_shape=[SDS((T, D), F32), SDS((1, 128), F32), SDS((1, D), F32)],
        scratch_shapes=[pltpu.VMEM((1, D), F32)],
        params=_params(1, blocks, temp_bytes=4 * _nbytes((tm, D), F32)))


def _ffn_bwd_down(dxo, g, y, wd, a, b, *, seq, tm, tn, name, comm=None):
    T, F = a.shape
    D = wd.shape[1]
    tps = seq // tm
    nb = T // seq

    def body(dxo_ref, g_ref, y_ref, wd_ref, a_ref, b_ref, dyb_ref, da_ref, db_ref, act_ref, dg_ref):
        i = pl.program_id(0)

        @pl.when(pl.program_id(1) == 0)
        def _():
            dx = dxo_ref[...]
            dyb_ref[...] = ((FFN_RESIDUAL * g_ref[0]) * dx).astype(BF16)
            part = _rowsum(FFN_RESIDUAL * dx * y_ref[...].astype(F32))
            _acc(dg_ref, part[None], i % tps == 0)

        dact = _dot_nt(dyb_ref[...], wd_ref[...])
        af = a_ref[...].astype(F32)
        bf = b_ref[...].astype(F32)
        sg = _sigmoid(af)
        silu = af * sg
        act_ref[...] = (silu * bf).astype(BF16)
        da_ref[...] = (dact * bf * (sg * (1.0 + af * (1.0 - sg)))).astype(BF16)
        db_ref[...] = (dact * silu).astype(BF16)

    row = pl.BlockSpec((tm, D), lambda i, j: (i, 0))
    per_b = pl.BlockSpec((1, 1, D), lambda i, j: (i // tps, 0, 0))
    wspec = pl.BlockSpec((tn, D), lambda i, j: (j, 0))
    chunk = pl.BlockSpec((tm, tn), lambda i, j: (i, j))
    blocks = [((tm, D), F32), ((tm, D), BF16), ((tn, D), BF16), ((tm, D), BF16)] + [((tm, tn), BF16)] * 5
    return _call(
        body, (dxo, g.arr, y, wd, a, b), name=name, grid=(T // tm, F // tn),
        in_specs=[row, g.spec(tps, 2), row, wspec, chunk, chunk],
        out_specs=[row, chunk, chunk, chunk, per_b],
        out_shape=[SDS((T, D), BF16)] + [SDS((T, F), BF16)] * 3 + [SDS((nb, 1, D), F32)],
        params=_params(2, blocks, temp_bytes=6 * _nbytes((tm, tn), F32)), comm=comm)


def _matmul_norm_mod_bwd(ds, ws, x, gn, sc, dxo, *, seq, tm, name, comm=None):
    T, D = x.shape
    nk = len(ws)
    sizes = [len(g) for g in ds]
    ds = [d for g in ds for d in g]
    tps = seq // tm
    nb = T // seq

    def body(*refs):
        w_refs = refs[len(ds):len(ds) + nk]
        x_ref, gn_ref, sc_ref, dxo_ref, dxi_ref, dsh_ref, dsc_ref, dgn_ref = refs[len(ds) + nk:]
        i = pl.program_id(0)
        dh, at = None, 0
        for n, w_ref in zip(sizes, w_refs):
            pieces = [r[...] for r in refs[at:at + n]]
            at += n
            part = _dot(pieces[0] if n == 1 else jnp.concatenate(pieces, axis=1), w_ref[...])
            dh = part if dh is None else dh + part
        gnv = gn_ref[...]
        scv = sc_ref[0]
        _, xhat, rstd, yn = _norm_mod(x_ref[...], gnv, 0.0, scv)
        dyn = dh * (1.0 + scv)
        dxhat = dyn * gnv
        dxi_ref[...] = dxo_ref[...] + rstd * (dxhat - xhat * jnp.mean(dxhat * xhat, axis=-1, keepdims=True))
        first_of_seq = i % tps == 0
        _acc(dsh_ref, _rowsum(dh)[None], first_of_seq)
        _acc(dsc_ref, _rowsum(dh * yn)[None], first_of_seq)
        _acc(dgn_ref, _rowsum(dyn * xhat), i == 0)

    row = pl.BlockSpec((tm, D), lambda i: (i, 0))
    vec = pl.BlockSpec((1, D), lambda i: (0, 0))
    per_b = pl.BlockSpec((1, 1, D), lambda i: (i // tps, 0, 0))
    d_specs = [pl.BlockSpec((tm, d.shape[1]), lambda i: (i, 0)) for d in ds]
    w_specs = [pl.BlockSpec(w.shape, lambda i: (0, 0)) for w in ws]
    blocks = ([((tm, d.shape[1]), BF16) for d in ds] + [(w.shape, BF16) for w in ws] + [((tm, D), F32)] * 3)
    return _call(
        body, (*ds, *ws, x, gn, sc.arr, dxo), name=name, grid=(T // tm,),
        in_specs=d_specs + w_specs + [row, vec, sc.spec(tps, 1), row],
        out_specs=[row, per_b, per_b, vec],
        out_shape=[SDS((T, D), F32), SDS((nb, 1, D), F32), SDS((nb, 1, D), F32), SDS((1, D), F32)],
        params=_params(1, blocks, temp_bytes=6 * _nbytes((tm, D), F32)), comm=comm)


def _layernorm_silu(yc, lg, lb):
    mu = jnp.mean(yc, axis=-1, keepdims=True)
    cen = yc - mu
    rstd = lax.rsqrt(jnp.mean(cen * cen, axis=-1, keepdims=True) + EPS)
    xh = cen * rstd
    l = xh * lg + lb
    s = _sigmoid(l)
    return l * s, xh, rstd, l, s


GATE_W = 256


def _gate_specs(tm, D, col):
    return [pl.BlockSpec((tm, GATE_W), lambda i, blk=col // GATE_W + t: (i, blk)) for t in range(D // GATE_W)]


def _gate(refs):
    return jnp.concatenate([r[...] for r in refs], axis=1).astype(F32)


def _mix_out(ao, yc, proj, wao, wco, wout, x1, g2, lg, lb, *, seq, tm, ga_col, gc_col, name, comm=None):
    T, D = x1.shape
    tps = seq // tm
    ng = D // GATE_W

    def body(ao_ref, yc_ref, *rest):
        ga_refs, gc_refs = rest[:ng], rest[ng:2 * ng]
        (wao_ref, wco_ref, wout_ref, x1_ref, g2_ref, lg_ref, lb_ref,
         x2_ref, z_ref, ya_ref, ycv_ref, cact_ref, mrg_ref) = rest[2 * ng:]
        ya = _dot(ao_ref[...], wao_ref[...])
        cact = _layernorm_silu(yc_ref[...], lg_ref[...], lb_ref[...])[0].astype(BF16)
        ycv = _dot(cact, wco_ref[...])
        merged = (_sigmoid(_gate(ga_refs)) * ya + _sigmoid(_gate(gc_refs)) * ycv).astype(BF16)
        z = _dot(merged, wout_ref[...])
        x2_ref[...] = x1_ref[...] + g2_ref[0] * z
        z_ref[...] = z.astype(BF16)
        ya_ref[...] = ya.astype(BF16)
        ycv_ref[...] = ycv.astype(BF16)
        cact_ref[...] = cact
        mrg_ref[...] = merged

    row = pl.BlockSpec((tm, D), lambda i: (i, 0))
    vec = pl.BlockSpec((1, D), lambda i: (0, 0))
    per_b = pl.BlockSpec((1, 1, D), lambda i: (i // tps, 0, 0))
    wspec = pl.BlockSpec((D, D), lambda i: (0, 0))
    gates = _gate_specs(tm, D, ga_col) + _gate_specs(tm, D, gc_col)
    blocks = ([((tm, D), BF16), ((tm, D), F32), ((tm, D), BF16), ((tm, D), BF16)] + [((D, D), BF16)] * 3
              + [((tm, D), F32)] * 2 + [((tm, D), BF16)] * 5)
    return _call(
        body, (ao, yc, *[proj] * (2 * ng), wao, wco, wout, x1, g2.arr, lg, lb), name=name, grid=(T // tm,),
        in_specs=[row, row, *gates, wspec, wspec, wspec, row, g2.spec(tps, 1), vec, vec],
        out_specs=[row] * 6,
        out_shape=[SDS((T, D), F32)] + [SDS((T, D), BF16)] * 5,
        params=_params(1, blocks, temp_bytes=8 * _nbytes((tm, D), F32)), comm=comm)


def _mix_out_bwd(dx2, g2, z, wout, proj, ya, ycv, wao, wco, yc, lg, lb, *, seq, tm, ga_col, gc_col, name,
                 comm=None):
    T, D = dx2.shape
    tps = seq // tm
    nb = T // seq
    ng = D // GATE_W

    def body(dx2_ref, g2_ref, z_ref, wout_ref, *rest):
        ga_refs, gc_refs = rest[:ng], rest[ng:2 * ng]
        (ya_ref, ycv_ref, wao_ref, wco_ref, yc_ref, lg_ref, lb_ref, dz_ref, dya_ref, dycv_ref, dga_ref, dgc_ref,
         dao_ref, dyc_ref, dg2_ref, dlg_ref, dlb_ref) = rest[2 * ng:]
        i = pl.program_id(0)
        dx = dx2_ref[...]
        _acc(dg2_ref, _rowsum(dx * z_ref[...].astype(F32))[None], i % tps == 0)
        dzb = (g2_ref[0] * dx).astype(BF16)
        dz_ref[...] = dzb
        dmerged = _dot_nt(dzb, wout_ref[...])
        sa = _sigmoid(_gate(ga_refs))
        sc_ = _sigmoid(_gate(gc_refs))
        dya = (dmerged * sa).astype(BF16)
        dycv = (dmerged * sc_).astype(BF16)
        dya_ref[...] = dya
        dycv_ref[...] = dycv
        dga_ref[...] = (dmerged * ya_ref[...].astype(F32) * (sa * (1.0 - sa))).astype(BF16)
        dgc_ref[...] = (dmerged * ycv_ref[...].astype(F32) * (sc_ * (1.0 - sc_))).astype(BF16)
        dao_ref[...] = _dot_nt(dya, wao_ref[...]).astype(BF16)
        dcact = _dot_nt(dycv, wco_ref[...])
        lgv = lg_ref[...]
        _, xh, rstd, l, s = _layernorm_silu(yc_ref[...], lgv, lb_ref[...])
        dl = dcact * (s * (1.0 + l * (1.0 - s)))
        _acc(dlb_ref, _rowsum(dl), i == 0)
        _acc(dlg_ref, _rowsum(dl * xh), i == 0)
        dxh = dl * lgv
        dyc_ref[...] = rstd * (dxh - jnp.mean(dxh, axis=-1, keepdims=True)
                               - xh * jnp.mean(dxh * xh, axis=-1, keepdims=True))

    row = pl.BlockSpec((tm, D), lambda i: (i, 0))
    vec = pl.BlockSpec((1, D), lambda i: (0, 0))
    per_b = pl.BlockSpec((1, 1, D), lambda i: (i // tps, 0, 0))
    wspec = pl.BlockSpec((D, D), lambda i: (0, 0))
    gates = _gate_specs(tm, D, ga_col) + _gate_specs(tm, D, gc_col)
    blocks = ([((tm, D), F32)] * 3 + [((tm, D), BF16)] * 11 + [((D, D), BF16)] * 3)
    return _call(
        body, (dx2, g2.arr, z, wout, *[proj] * (2 * ng), ya, ycv, wao, wco, yc, lg, lb), name=name,
        grid=(T // tm,),
        in_specs=[row, g2.spec(tps, 1), row, wspec, *gates, row, row, wspec, wspec, row, vec, vec],
        out_specs=[row] * 7 + [per_b, vec, vec],
        out_shape=[SDS((T, D), BF16)] * 6 + [SDS((T, D), F32), SDS((nb, 1, D), F32), SDS((1, D), F32),
                                             SDS((1, D), F32)],
        params=_params(1, blocks, temp_bytes=10 * _nbytes((tm, D), F32)), comm=comm)


GROUP_ROWS = GQA_GROUP * ATT_BLOCK
PAIR_W = 2 * HEAD_DIM
GROUP_W = GQA_GROUP * HEAD_DIM


def _lane_lo():
    return lax.broadcasted_iota(jnp.int32, (1, PAIR_W), 1) < HEAD_DIM


def _band_bias():
    sj = lax.broadcasted_iota(jnp.int32, (2 * ATT_BLOCK, GROUP_ROWS), 0)
    qi = lax.broadcasted_iota(jnp.int32, (2 * ATT_BLOCK, GROUP_ROWS), 1) & (ATT_BLOCK - 1)
    rel = qi + ATT_BLOCK - sj
    bias = jnp.where(jnp.logical_and(rel >= 0, rel < ATT_BLOCK), 0.0, NEG_BIG)
    sj1 = lax.broadcasted_iota(jnp.int32, (2 * ATT_BLOCK, 1), 0)
    return bias, jnp.where(sj1 < ATT_BLOCK, NEG_BIG, 0.0)


def _dup_heads(src_ref, dst, seq):
    x = src_ref[...]
    i = lax.broadcasted_iota(jnp.int32, (KV_WIDTH, PAIR_W), 0)
    j = lax.broadcasted_iota(jnp.int32, (KV_WIDTH, PAIR_W), 1) & (HEAD_DIM - 1)
    for g in range(N_KV_HEADS):
        sel = jnp.where(i == j + g * HEAD_DIM, 1.0, 0.0).astype(BF16)
        dst[g, pl.ds(0, ATT_BLOCK), :] = jnp.zeros((ATT_BLOCK, PAIR_W), BF16)
        dst[g, pl.ds(ATT_BLOCK, seq), :] = _dot(x, sel).astype(BF16)


def _stack_heads(blk, g, lo):
    parts = []
    for p in range(GQA_GROUP // 2):
        pair = blk[:, g * GROUP_W + p * PAIR_W:g * GROUP_W + (p + 1) * PAIR_W]
        parts += [jnp.where(lo, pair, jnp.zeros_like(pair)), jnp.where(lo, jnp.zeros_like(pair), pair)]
    return jnp.concatenate(parts, axis=0)


def _unstack_heads(full, ref, r0, g, lo):
    for p in range(GQA_GROUP // 2):
        even = full[(2 * p) * ATT_BLOCK:(2 * p + 1) * ATT_BLOCK, :]
        odd = full[(2 * p + 1) * ATT_BLOCK:(2 * p + 2) * ATT_BLOCK, :]
        ref[pl.ds(r0, ATT_BLOCK), g * GROUP_W + p * PAIR_W:g * GROUP_W + (p + 1) * PAIR_W] = (
            jnp.where(lo, even, odd).astype(ref.dtype))


def _sink_row(sink_ref, g):
    return jnp.concatenate([jnp.full((1, ATT_BLOCK), sink_ref[0, g * GQA_GROUP + h], F32)
                            for h in range(GQA_GROUP)], axis=1)


def _group_probs(qs, k2, bias, sink):
    s = _dot_nt(k2, qs) * (HEAD_DIM ** -0.5) + bias
    m = jnp.maximum(jnp.max(s, axis=0, keepdims=True), sink)
    p = jnp.exp(s - m)
    psink = jnp.exp(sink - m)
    inv = 1.0 / (jnp.sum(p, axis=0, keepdims=True) + psink)
    return p * inv, psink * inv


def _attn_fwd(projp, sinks, *, seq, q_blk, k_blk, v_blk, name, comm=None):
    T = projp.shape[0]
    QW = N_Q_HEADS * HEAD_DIM
    nblk = seq // ATT_BLOCK

    def body(q_ref, k_ref, v_ref, sink_ref, o_ref, k2s, v2s):
        _dup_heads(k_ref, k2s, seq)
        _dup_heads(v_ref, v2s, seq)
        lo = _lane_lo()
        bias0, first_pen = _band_bias()
        sink_rows = [_sink_row(sink_ref, g) for g in range(N_KV_HEADS)]

        def blk(n, carry):
            r0 = pl.multiple_of(n * ATT_BLOCK, ATT_BLOCK)
            qb = q_ref[pl.ds(r0, ATT_BLOCK), :]
            bias = bias0 + jnp.where(n == 0, 1.0, 0.0) * first_pen
            for g in range(N_KV_HEADS):
                probs_t, _ = _group_probs(_stack_heads(qb, g, lo), k2s[g, pl.ds(r0, 2 * ATT_BLOCK), :], bias,
                                          sink_rows[g])
                _unstack_heads(_dot_tn(probs_t.astype(BF16), v2s[g, pl.ds(r0, 2 * ATT_BLOCK), :]), o_ref, r0, g, lo)
            return carry

        lax.fori_loop(0, nblk, blk, 0)

    blocks = [((seq, QW), BF16)] * 2 + [((seq, KV_WIDTH), BF16)] * 2
    return _call(
        body, (projp, projp, projp, sinks), name=name, grid=(T // seq,),
        in_specs=[pl.BlockSpec((seq, QW), lambda b: (b, q_blk)),
                  pl.BlockSpec((seq, KV_WIDTH), lambda b: (b, k_blk)),
                  pl.BlockSpec((seq, KV_WIDTH), lambda b: (b, v_blk)),
                  pl.BlockSpec(memory_space=pltpu.SMEM)],
        out_specs=[pl.BlockSpec((seq, QW), lambda b: (b, 0))],
        out_shape=[SDS((T, QW), BF16)],
        scratch_shapes=[pltpu.VMEM((N_KV_HEADS, seq + ATT_BLOCK, PAIR_W), BF16)] * 2,
        params=_params(1, blocks, temp_bytes=16 * 2**20), comm=comm)[0]


def _attn_bwd(projp, dao, sinks, *, seq, q_blk, k_blk, v_blk, name, comm=None):
    T = projp.shape[0]
    QW = N_Q_HEADS * HEAD_DIM
    nblk = seq // ATT_BLOCK

    def body(q_ref, k_ref, v_ref, do_ref, sink_ref, dq_ref, dk_ref, dv_ref, dsink_ref, k2s, v2s, dkacc, dvacc):
        _dup_heads(k_ref, k2s, seq)
        _dup_heads(v_ref, v2s, seq)
        dkacc[...] = jnp.zeros(dkacc.shape, F32)
        dvacc[...] = jnp.zeros(dvacc.shape, F32)
        lane = lax.broadcasted_iota(jnp.int32, (1, PAIR_W), 1)
        lo = lane < HEAD_DIM
        bias0, first_pen = _band_bias()
        sink_rows = [_sink_row(sink_ref, g) for g in range(N_KV_HEADS)]

        def blk(n, dsink):
            r0 = pl.multiple_of(n * ATT_BLOCK, ATT_BLOCK)
            band = pl.ds(r0, 2 * ATT_BLOCK)
            qb = q_ref[pl.ds(r0, ATT_BLOCK), :]
            dob = do_ref[pl.ds(r0, ATT_BLOCK), :]
            bias = bias0 + jnp.where(n == 0, 1.0, 0.0) * first_pen
            for g in range(N_KV_HEADS):
                qs = _stack_heads(qb, g, lo)
                dos = _stack_heads(dob, g, lo)
                k2 = k2s[g, band, :]
                v2 = v2s[g, band, :]
                probs_t, psink = _group_probs(qs, k2, bias, sink_rows[g])
                dp_t = _dot_nt(v2, dos)
                delta = jnp.sum(probs_t * dp_t, axis=0, keepdims=True)
                ds_t = (probs_t * (dp_t - delta) * (HEAD_DIM ** -0.5)).astype(BF16)
                tsink = psink * delta
                for h in range(GQA_GROUP):
                    dsink = dsink + jnp.where(lane == g * GQA_GROUP + h,
                                              -jnp.sum(tsink[:, h * ATT_BLOCK:(h + 1) * ATT_BLOCK]), 0.0)
                _unstack_heads(_dot_tn(ds_t, k2), dq_ref, r0, g, lo)
                dkacc[g, band, :] = dkacc[g, band, :] + _dot(ds_t, qs)
                dvacc[g, band, :] = dvacc[g, band, :] + _dot(probs_t.astype(BF16), dos)
            return dsink

        dsink = lax.fori_loop(0, nblk, blk, jnp.zeros((1, PAIR_W), F32))
        _acc(dsink_ref, dsink, pl.program_id(0) == 0)

        def fold(acc, g):
            a = acc[g, pl.ds(ATT_BLOCK, seq), :]
            return a + pltpu.roll(a, HEAD_DIM, 1)

        dk_ref[...] = jnp.where(lo, fold(dkacc, 0), fold(dkacc, 1)).astype(BF16)
        dv_ref[...] = jnp.where(lo, fold(dvacc, 0), fold(dvacc, 1)).astype(BF16)

    blocks = [((seq, QW), BF16)] * 3 + [((seq, KV_WIDTH), BF16)] * 4
    kv_spec_out = pl.BlockSpec((seq, KV_WIDTH), lambda b: (b, 0))
    return _call(
        body, (projp, projp, projp, dao, sinks), name=name, grid=(T // seq,),
        in_specs=[pl.BlockSpec((seq, QW), lambda b: (b, q_blk)),
                  pl.BlockSpec((seq, KV_WIDTH), lambda b: (b, k_blk)),
                  pl.BlockSpec((seq, KV_WIDTH), lambda b: (b, v_blk)),
                  pl.BlockSpec((seq, QW), lambda b: (b, 0)),
                  pl.BlockSpec(memory_space=pltpu.SMEM)],
        out_specs=[pl.BlockSpec((seq, QW), lambda b: (b, 0)), kv_spec_out, kv_spec_out,
                   pl.BlockSpec((1, 128), lambda b: (0, 0))],
        out_shape=[SDS((T, QW), BF16), SDS((T, KV_WIDTH), BF16), SDS((T, KV_WIDTH), BF16), SDS((1, 128), F32)],
        scratch_shapes=[pltpu.VMEM((N_KV_HEADS, seq + ATT_BLOCK, PAIR_W), BF16)] * 2
        + [pltpu.VMEM((N_KV_HEADS, seq + ATT_BLOCK, PAIR_W), F32)] * 2,
        params=_params(1, blocks, temp_bytes=24 * 2**20), comm=comm)


SUBLANES = 8


def _sublane_shifts(win):
    n = CONV_ROWS + CONV_HALO
    return [win] + [pltpu.roll(win, n - b, 0) for b in range(1, SUBLANES)]


def _window(shifted, off):
    a = off // SUBLANES * SUBLANES
    return shifted[off % SUBLANES][a:a + CONV_ROWS, :]


def _conv_fwd(projp, w, bias, *, seq, cw, a_col, b_col, name, comm=None):
    T = projp.shape[0]
    C = w.shape[1]
    nchunk = seq // CONV_ROWS

    def body(a_ref, b_ref, w_ref, bias_ref, y_ref, upad):
        upad[pl.ds(0, CONV_HALO), :] = jnp.zeros((CONV_HALO, cw), F32)
        upad[pl.ds(CONV_HALO, seq), :] = a_ref[...].astype(F32) * _sigmoid(b_ref[...].astype(F32))
        wv = w_ref[...]
        bv = bias_ref[...]

        def chunk(r, carry):
            r0 = pl.multiple_of(r * CONV_ROWS, CONV_ROWS)
            shifted = _sublane_shifts(upad[pl.ds(r0, CONV_ROWS + CONV_HALO), :])
            acc = jnp.broadcast_to(bv, (CONV_ROWS, cw))
            for k in range(CONV_WIDTH):
                acc = acc + wv[k:k + 1, :] * _window(shifted, CONV_HALO - (CONV_WIDTH - 1) + k)
            y_ref[pl.ds(r0, CONV_ROWS), :] = acc
            return carry

        lax.fori_loop(0, nchunk, chunk, 0)

    blocks = [((seq, cw), BF16)] * 2 + [((seq, cw), F32)]
    return _call(
        body, (projp, projp, w, bias), name=name, grid=(T // seq, C // cw),
        in_specs=[pl.BlockSpec((seq, cw), lambda b, c: (b, a_col // cw + c)),
                  pl.BlockSpec((seq, cw), lambda b, c: (b, b_col // cw + c)),
                  pl.BlockSpec((CONV_WIDTH, cw), lambda b, c: (0, c)),
                  pl.BlockSpec((1, cw), lambda b, c: (0, c))],
        out_specs=[pl.BlockSpec((seq, cw), lambda b, c: (b, c))],
        out_shape=[SDS((T, C), F32)],
        scratch_shapes=[pltpu.VMEM((seq + CONV_HALO, cw), F32)],
        params=_params(2, blocks, temp_bytes=6 * _nbytes((seq, cw), F32)), comm=comm)[0]


def _conv_bwd(dy, projp, w, *, seq, cw, a_col, b_col, name, comm=None):
    T = projp.shape[0]
    C = w.shape[1]
    nchunk = seq // CONV_ROWS
    SUB = 8

    def body(dy_ref, a_ref, b_ref, w_ref, da_ref, db_ref, dw_ref, dbias_ref, dypad, dwp):
        first = pl.program_id(1) == 0
        dyv = dy_ref[...]
        dypad[pl.ds(0, seq), :] = dyv
        dypad[pl.ds(seq, CONV_HALO), :] = jnp.zeros((CONV_HALO, cw), F32)
        dwp[...] = jnp.zeros(dwp.shape, F32)
        wv = w_ref[...]

        def chunk(r, carry):
            r0 = pl.multiple_of(r * CONV_ROWS, CONV_ROWS)
            dy_shifts = _sublane_shifts(dypad[pl.ds(r0, CONV_ROWS + CONV_HALO), :])
            ac = a_ref[pl.ds(r0, CONV_ROWS), :].astype(F32)
            sbc = _sigmoid(b_ref[pl.ds(r0, CONV_ROWS), :].astype(F32))
            uc = ac * sbc
            du = jnp.zeros((CONV_ROWS, cw), F32)
            for k in range(CONV_WIDTH):
                dyk = _window(dy_shifts, CONV_WIDTH - 1 - k)
                du = du + wv[k:k + 1, :] * dyk
                prod = uc * dyk
                part = prod[0:SUB, :]
                for s in range(1, CONV_ROWS // SUB):
                    part = part + prod[s * SUB:(s + 1) * SUB, :]
                dwp[pl.ds(k * SUB, SUB), :] = dwp[pl.ds(k * SUB, SUB), :] + part
            da_ref[pl.ds(r0, CONV_ROWS), :] = (du * sbc).astype(BF16)
            db_ref[pl.ds(r0, CONV_ROWS), :] = (du * ac * (sbc * (1.0 - sbc))).astype(BF16)
            return carry

        lax.fori_loop(0, nchunk, chunk, 0)

        @pl.when(first)
        def _():
            dw_ref[...] = jnp.zeros(dw_ref.shape, F32)
            dbias_ref[...] = jnp.zeros(dbias_ref.shape, F32)

        for k in range(CONV_WIDTH):
            dw_ref[k:k + 1, :] = dw_ref[k:k + 1, :] + _rowsum(dwp[pl.ds(k * SUB, SUB), :])
        dbias_ref[...] = dbias_ref[...] + _rowsum(dyv)

    blocks = [((seq, cw), F32)] + [((seq, cw), BF16)] * 4
    return _call(
        body, (dy, projp, projp, w), name=name, grid=(C // cw, T // seq),
        in_specs=[pl.BlockSpec((seq, cw), lambda c, b: (b, c)),
                  pl.BlockSpec((seq, cw), lambda c, b: (b, a_col // cw + c)),
                  pl.BlockSpec((seq, cw), lambda c, b: (b, b_col // cw + c)),
                  pl.BlockSpec((CONV_WIDTH, cw), lambda c, b: (0, c))],
        out_specs=[pl.BlockSpec((seq, cw), lambda c, b: (b, c)), pl.BlockSpec((seq, cw), lambda c, b: (b, c)),
                   pl.BlockSpec((CONV_WIDTH, cw), lambda c, b: (0, c)), pl.BlockSpec((1, cw), lambda c, b: (0, c))],
        out_shape=[SDS((T, C), BF16), SDS((T, C), BF16), SDS((CONV_WIDTH, C), F32), SDS((1, C), F32)],
        scratch_shapes=[pltpu.VMEM((seq + CONV_HALO, cw), F32), pltpu.VMEM((CONV_WIDTH * SUB, cw), F32)],
        params=_params(2, blocks, temp_bytes=8 * _nbytes((seq, cw), F32)), comm=comm)


def _matmul_tn(a, b, *, name, comm=None):
    T, M = a.shape
    N = b.shape[1]
    bm = _pick(M, (768, 512, 256))

    def body(a_ref, b_ref, o_ref):
        o_ref[...] = _dot_tn(a_ref[...], b_ref[...]).astype(BF16)

    blocks = [((T, bm), BF16), ((T, N), BF16), ((bm, N), BF16)]
    return _call(
        body, (a, b), name=name, grid=(M // bm,),
        in_specs=[pl.BlockSpec((T, bm), lambda i: (0, i)), pl.BlockSpec((T, N), lambda i: (0, 0))],
        out_specs=[pl.BlockSpec((bm, N), lambda i: (i, 0))],
        out_shape=[SDS((M, N), BF16)],
        params=_params(1, blocks, temp_bytes=2 * _nbytes((T, bm), BF16) + 2 * _nbytes((bm, N), F32)),
        comm=comm)[0]


TN_BLOCK = 256


def _matmul_tn_pieces(groups, b, *, name, comm=None):
    T, N = b.shape
    flat = [a for g in groups for a in g]
    starts, n_steps = [], 0
    for g in groups:
        width = sum(a.shape[1] for a in g)
        assert width % TN_BLOCK == 0 and (len(g) == 1 or width == TN_BLOCK), [a.shape for a in g]
        starts.append(n_steps)
        n_steps += width // TN_BLOCK

    def body(*refs):
        a_refs, b_ref, o_ref = refs[:len(flat)], refs[len(flat)], refs[len(flat) + 1]
        i = pl.program_id(0)
        at = 0
        for g, start in zip(groups, starts):
            mine = a_refs[at:at + len(g)]
            at += len(g)
            steps = sum(a.shape[1] for a in g) // TN_BLOCK

            @pl.when(jnp.logical_and(i >= start, i < start + steps))
            def _(mine=mine):
                a = mine[0][...] if len(mine) == 1 else jnp.concatenate([r[...] for r in mine], axis=1)
                o_ref[...] = _dot_tn(a, b_ref[...]).astype(BF16)

    a_specs = []
    for g, start in zip(groups, starts):
        for a in g:
            if len(g) == 1:
                last = a.shape[1] // TN_BLOCK - 1
                a_specs.append(pl.BlockSpec(
                    (T, TN_BLOCK), lambda i, start=start, last=last: (0, jnp.clip(i - start, 0, last))))
            else:
                a_specs.append(pl.BlockSpec((T, a.shape[1]), lambda i: (0, 0)))
    blocks = [((T, TN_BLOCK), BF16)] * len(flat) + [((T, N), BF16), ((TN_BLOCK, N), BF16)]
    return _call(
        body, (*flat, b), name=name, grid=(n_steps,),
        in_specs=a_specs + [pl.BlockSpec((T, N), lambda i: (0, 0))],
        out_specs=[pl.BlockSpec((TN_BLOCK, N), lambda i: (i, 0))],
        out_shape=[SDS((n_steps * TN_BLOCK, N), BF16)],
        params=_params(1, blocks, temp_bytes=2 * _nbytes((T, TN_BLOCK), BF16) + 2 * _nbytes((TN_BLOCK, N), F32)),
        comm=comm)[0]


def _sum_parts(p_ref):
    g = p_ref[0].astype(F32)
    for s in range(1, p_ref.shape[0]):
        g = g + p_ref[s].astype(F32)
    return g


def _pair_add(g, staged, *, name):
    _, R, W = g.shape
    nq = staged.shape[0]
    tr = _row_tile(R)

    def body(g_ref, s_ref, o_ref):
        mine = jnp.where(lax.axis_index("c") == 0, g_ref[0, 0].astype(F32), g_ref[0, 1].astype(F32))
        o_ref[0] = (mine + s_ref[0].astype(F32)).astype(o_ref.dtype)

    return _call(
        body, (g.reshape(nq, 2, R, W), staged), name=name, grid=(nq, R // tr),
        in_specs=[pl.BlockSpec((1, 2, tr, W), lambda q, i: (q, 0, i, 0)),
                  pl.BlockSpec((1, tr, W), lambda q, i: (q, i, 0))],
        out_specs=[pl.BlockSpec((1, tr, W), lambda q, i: (q, i, 0))],
        out_shape=[SDS((nq, R, W), g.dtype)],
        params=_params(2, [((4, tr, W), g.dtype)], temp_bytes=3 * _nbytes((tr, W), F32)))[0]


def _adamw_update(w, g, m, v):
    m = ADAM_B1 * m + (1.0 - ADAM_B1) * g
    v = ADAM_B2 * v + (1.0 - ADAM_B2) * (g * g)
    m_hat = m / (1.0 - ADAM_B1 ** ADAM_STEP)
    v_hat = v / (1.0 - ADAM_B2 ** ADAM_STEP)
    delta = -ADAM_LR * (m_hat / (jnp.sqrt(v_hat) + ADAM_EPS) + ADAM_WD * w)
    return delta, m, v


def _row_tile(R):
    return _pick(R, (256, 128, 112, 88, 64, 32, 16, 8))


def _sum8(parts, *, name):
    n, R, W = parts.shape
    tr = _row_tile(R)

    def body(p_ref, o_ref):
        o_ref[...] = _sum_parts(p_ref)

    return _call(
        body, (parts,), name=name, grid=(R // tr,),
        in_specs=[pl.BlockSpec((n, tr, W), lambda i: (0, i, 0))],
        out_specs=[pl.BlockSpec((tr, W), lambda i: (i, 0))],
        out_shape=[SDS((R, W), F32)],
        params=_params(1, [((n, tr, W), parts.dtype), ((tr, W), F32)]))[0]


def _adamw(g, w, m, v, *, name):
    R, W = w.shape
    tr = _row_tile(R)

    def body(g_ref, w_ref, m_ref, v_ref, d_ref, mo_ref, vo_ref):
        d_ref[...], mo_ref[...], vo_ref[...] = _adamw_update(w_ref[...], g_ref[...], m_ref[...], v_ref[...])

    spec = pl.BlockSpec((tr, W), lambda i: (i, 0))
    return _call(
        body, (g, w, m, v), name=name, grid=(R // tr,),
        in_specs=[spec] * 4, out_specs=[spec] * 3, out_shape=[SDS((R, W), F32)] * 3,
        params=_params(1, [((tr, W), F32)] * 7))


def _sum8_adamw(parts, w, m, v, *, name):
    R, W = w.shape
    n = parts.shape[0]
    tr = _row_tile(R)

    def body(p_ref, w_ref, m_ref, v_ref, g_ref, d_ref, mo_ref, vo_ref):
        g = _sum_parts(p_ref)
        g_ref[...] = g
        d_ref[...], mo_ref[...], vo_ref[...] = _adamw_update(w_ref[...], g, m_ref[...], v_ref[...])

    spec = pl.BlockSpec((tr, W), lambda i: (i, 0))
    return _call(
        body, (parts, w, m, v), name=name, grid=(R // tr,),
        in_specs=[pl.BlockSpec((n, tr, W), lambda i: (0, i, 0))] + [spec] * 3,
        out_specs=[spec] * 4, out_shape=[SDS((R, W), F32)] * 4,
        params=_params(1, [((n, tr, W), parts.dtype)] + [((tr, W), F32)] * 7))


def _ada_fwd(c_all, w, bias, *, name):
    NB, D = c_all.shape
    N = w.shape[1]

    def body(c_ref, w_ref, b_ref, o_ref):
        cv = c_ref[...]
        ca = (cv * _sigmoid(cv)).astype(BF16)
        o_ref[...] = _dot(ca, w_ref[...].astype(BF16)) + b_ref[...]

    full = lambda s: pl.BlockSpec(s, lambda i: (0,) * len(s))
    return _call(
        body, (c_all, w, bias), name=name, grid=(1,),
        in_specs=[full((NB, D)), full((D, N)), full((1, N))], out_specs=[full((NB, N))],
        out_shape=[SDS((NB, N), F32)],
        params=_params(1, [((D, N), F32)], temp_bytes=_nbytes((D, N), BF16)))[0]


def _ada_bwd(c_all, gmod_all, *, n_col, name):
    NB, D = c_all.shape
    N = gmod_all.shape[1]

    def body(c_ref, g_ref, gw_ref, gb_ref):
        cv = c_ref[...]
        ca = (cv * _sigmoid(cv)).astype(BF16)
        first = pl.multiple_of(_lin(_my_pos()) * n_col, 128)
        gw_ref[...] = _dot_tn(ca, g_ref[:, pl.ds(first, n_col)].astype(BF16))
        gb_ref[...] = _rowsum(g_ref[...])

    full = lambda s: pl.BlockSpec(s, lambda i: (0,) * len(s))
    return _call(
        body, (c_all, gmod_all), name=name, grid=(1,),
        in_specs=[full((NB, D)), full((NB, N))], out_specs=[full((D, n_col)), full((1, N))],
        out_shape=[SDS((D, n_col), F32), SDS((1, N), F32)],
        params=_params(1, [((D, n_col), F32), ((NB, N), F32)]))


def kernel(x, c, w_ada, b_ada, norm_ffn1_g, ffn1_w_gate, ffn1_w_up, ffn1_w_down, norm_mix_g, w_in, attn_sinks, w_attn_o, conv_w_dw, conv_b_dw, conv_ln_g, conv_ln_b, w_conv_o, w_out, norm_ffn2_g, ffn2_w_gate, ffn2_w_up, ffn2_w_down, final_norm_g, loss_target, m_w_ada, m_b_ada, m_norm_ffn1_g, m_ffn1_w_gate, m_ffn1_w_up, m_ffn1_w_down, m_norm_mix_g, m_w_in, m_attn_sinks, m_w_attn_o, m_conv_w_dw, m_conv_b_dw, m_conv_ln_g, m_conv_ln_b, m_w_conv_o, m_w_out, m_norm_ffn2_g, m_ffn2_w_gate, m_ffn2_w_up, m_ffn2_w_down, m_final_norm_g, v_w_ada, v_b_ada, v_norm_ffn1_g, v_ffn1_w_gate, v_ffn1_w_up, v_ffn1_w_down, v_norm_mix_g, v_w_in, v_attn_sinks, v_w_attn_o, v_conv_w_dw, v_conv_b_dw, v_conv_ln_g, v_conv_ln_b, v_w_conv_o, v_w_out, v_norm_ffn2_g, v_ffn2_w_gate, v_ffn2_w_up, v_ffn2_w_down, v_final_norm_g):
    B, S, D = x.shape
    T = B * S
    QW = N_Q_HEADS * HEAD_DIM
    CC = conv_w_dw.shape[2] * N_DEV
    me = _lin(_my_pos())
    xf = x.reshape(T, D)
    tgt = loss_target.reshape(T, D)
    tm = min(512, S)
    kw = dict(seq=S, tm=tm)

    p_k, p_v, p_ca = QW, QW + KV_WIDTH, QW + 2 * KV_WIDTH
    p_cb, p_ga, p_gc = p_ca + CC, p_ca + 2 * CC, p_ca + 2 * CC + D

    def col_t(w):
        return w[0].T.astype(BF16)

    def row_b(w):
        return w[0].astype(BF16)

    def rows(g):
        return g.reshape(-1, g.shape[-1])

    def blocks8(g):
        return g.reshape(N_DEV, g.shape[0] // N_DEV, g.shape[1])

    def gather(*arrs):
        return _Comm([(a, "gather") for a in arrs])

    g_wg1, g_convw, g_c = _exchange(
        [(col_t(ffn1_w_gate), "gather"), (conv_w_dw[0], "gather"), (c, "gather")], name="gather_first")
    wg1 = rows(g_wg1)
    conv_w = g_convw.transpose(1, 0, 2).reshape(CONV_WIDTH, CC)
    c_all = g_c.reshape(N_DEV * B, D)

    n_col = N_MOD * D // N_DEV
    b_cols = lax.dynamic_slice(b_ada, (0, me * n_col), (1, n_col))
    mod_cols = _ada_fwd(c_all, w_ada[0], b_cols, name="ada_fwd")
    mod_mine = _exchange([(mod_cols.reshape(N_DEV, B, n_col), "scatter")], name="scatter_mod")[0]
    mod = mod_mine.transpose(1, 0, 2).reshape(B * N_MOD, 1, D)
    sh1, sc1, g1, sh2, sc2, g2, sh3, sc3, g3 = [_ModVec(mod, i) for i in range(N_MOD)]

    F = wg1.shape[0]
    tn_f = _pick(F, (1408, 1024, 512, 256))
    tn_in = _pick(w_in.shape[2] * N_DEV, (1792, 768, 512, 256))
    gate_blk = dict(ga_col=p_ga, gc_col=p_gc)
    att_blk = dict(q_blk=0, k_blk=p_k // KV_WIDTH, v_blk=p_v // KV_WIDTH)
    conv_kw = dict(seq=S, cw=256, a_col=p_ca, b_col=p_cb)

    cm = gather(col_t(ffn1_w_up))
    h1, (a1,) = _norm_mod_matmul(xf, norm_ffn1_g, sh1, sc1, [wg1], tn=tn_f, name="ffn1_gate", comm=cm, **kw)
    wu1 = rows(cm.out[0])
    cm = gather(row_b(ffn1_w_down))
    b1 = _matmul_nt(h1, wu1, tm=tm, tn=tn_f, name="ffn1_up", comm=cm)
    wd1 = rows(cm.out[0])
    cm = gather(col_t(w_in))
    x1, y1 = _ffn_down(a1, b1, wd1, xf, g1, name="ffn1_down", comm=cm, **kw)
    winp = rows(cm.out[0])
    cm = gather(row_b(w_attn_o), row_b(w_conv_o), row_b(w_out), col_t(ffn2_w_gate))
    h2, (projp,) = _norm_mod_matmul(x1, norm_mix_g, sh2, sc2, [winp], tn=tn_in, name="mix_in", comm=cm, **kw)
    wao, wco, wout, wg2 = [rows(o) for o in cm.out]
    cm = gather(col_t(ffn2_w_up))
    ao = _attn_fwd(projp, attn_sinks, seq=S, name="attn_fwd", comm=cm, **att_blk)
    wu2 = rows(cm.out[0])
    cm = gather(row_b(ffn2_w_down))
    yc = _conv_fwd(projp, conv_w, conv_b_dw, name="conv_fwd", comm=cm, **conv_kw)
    wd2 = rows(cm.out[0])
    x2, z, ya, ycv, cact, merged = _mix_out(ao, yc, projp, wao, wco, wout, x1, g2, conv_ln_g, conv_ln_b,
                                            name="mix_out", **gate_blk, **kw)
    h3, (a3, b3) = _norm_mod_matmul(x2, norm_ffn2_g, sh3, sc3, [wg2, wu2], tn=tn_f, name="ffn2_up", **kw)
    x3, y3 = _ffn_down(a3, b3, wd2, x2, g3, name="ffn2_down", **kw)
    dx3, loss_row, dgf = _final_loss(x3, final_norm_g[None], tgt, tm=tm, name="final_loss")

    parts = {}

    def pair(*gs):
        return [(blocks8(g), "pair") for g in gs]

    def cross(*rs):
        return [(r, "cross") for r in rs]

    def reduce_pairs(gs, staged, names):
        return [_pair_add(blocks8(g), s, name="pair_add_" + n) for g, s, n in zip(gs, staged, names)]

    dyb3, da3, db3, act3, dg3 = _ffn_bwd_down(dx3, g3, y3, wd2, a3, b3, tn=tn_f, name="ffn2_bwd_down", **kw)
    gwd2 = _matmul_tn(act3, dyb3, name="gw_ffn2_down")
    cm = _Comm(pair(gwd2))
    dx2, dsh3, dsc3, dgn3 = _matmul_norm_mod_bwd([[da3], [db3]], [wg2, wu2], x2, norm_ffn2_g, sc3, dx3,
                                                 name="ffn2_bwd_up", comm=cm, **kw)
    r_wd2, = reduce_pairs([gwd2], cm.out, ["ffn2_w_down"])
    cm = _Comm(cross(r_wd2))
    gwg2 = _matmul_tn(da3, h3, name="gw_ffn2_gate", comm=cm)
    parts["ffn2_w_down"], = cm.out
    cm = _Comm(pair(gwg2))
    gwu2 = _matmul_tn(db3, h3, name="gw_ffn2_up", comm=cm)
    r_wg2, = reduce_pairs([gwg2], cm.out, ["ffn2_w_gate"])

    cm = _Comm(cross(r_wg2) + pair(gwu2))
    dzb, dyab, dycb, dga, dgc, dao, dyc, dg2, dlng, dlnb = _mix_out_bwd(
        dx2, g2, z, wout, projp, ya, ycv, wao, wco, yc, conv_ln_g, conv_ln_b, name="mix_out_bwd", comm=cm,
        **gate_blk, **kw)
    parts["ffn2_w_gate"] = cm.out[0]
    r_wu2, = reduce_pairs([gwu2], cm.out[1:], ["ffn2_w_up"])
    gwout = _matmul_tn(merged, dzb, name="gw_out")
    gwao = _matmul_tn(ao, dyab, name="gw_attn_o")
    gwco = _matmul_tn(cact, dycb, name="gw_conv_o")
    cm = _Comm(cross(r_wu2) + pair(gwout, gwao, gwco))
    dq, dk, dv, dsinks = _attn_bwd(projp, dao, attn_sinks, seq=S, name="attn_bwd", comm=cm, **att_blk)
    parts["ffn2_w_up"] = cm.out[0]
    r_mix = reduce_pairs([gwout, gwao, gwco], cm.out[1:], ["w_out", "w_attn_o", "w_conv_o"])
    cm = _Comm(cross(*r_mix))
    dca, dcb, dconvw, dconvb = _conv_bwd(dyc, projp, conv_w, name="conv_bwd", comm=cm, **conv_kw)
    parts["w_out"], parts["w_attn_o"], parts["w_conv_o"] = cm.out
    gwin = _matmul_tn_pieces([[dq], [dk, dv], [dca], [dcb], [dga], [dgc]], h2, name="gw_in")
    cm = _Comm(pair(gwin))
    dx1, dsh2, dsc2, dgn2 = _matmul_norm_mod_bwd([[dq, dk, dv, dca, dcb, dga, dgc]], [winp], x1, norm_mix_g, sc2, dx2,
                                                 name="mix_in_bwd", comm=cm, **kw)
    r_win, = reduce_pairs([gwin], cm.out, ["w_in"])

    cm = _Comm(cross(r_win))
    dyb1, da1, db1, act1, dg1 = _ffn_bwd_down(dx1, g1, y1, wd1, a1, b1, tn=tn_f, name="ffn1_bwd_down", comm=cm,
                                              **kw)
    parts["w_in"], = cm.out
    gwd1 = _matmul_tn(act1, dyb1, name="gw_ffn1_down")
    cm = _Comm(pair(gwd1))
    gwg1 = _matmul_tn(da1, h1, name="gw_ffn1_gate", comm=cm)
    r_wd1, = reduce_pairs([gwd1], cm.out, ["ffn1_w_down"])
    cm = _Comm(cross(r_wd1) + pair(gwg1))
    gwu1 = _matmul_tn(db1, h1, name="gw_ffn1_up", comm=cm)
    parts["ffn1_w_down"] = cm.out[0]
    r_wg1, = reduce_pairs([gwg1], cm.out[1:], ["ffn1_w_gate"])
    r_wu1, = reduce_pairs([gwu1], _exchange(pair(gwu1), name="pair_last"), ["ffn1_w_up"])
    cm = _Comm(cross(r_wg1, r_wu1))
    dx0, dsh1, dsc1, dgn1 = _matmul_norm_mod_bwd([[da1], [db1]], [wg1, wu1], xf, norm_ffn1_g, sc1, dx1,
                                                 name="ffn1_bwd_up", comm=cm, **kw)
    parts["ffn1_w_gate"], parts["ffn1_w_up"] = cm.out

    n_small = 8
    gmod = jnp.concatenate([dsh1, dsc1, dg1, dsh2, dsc2, dg2, dsh3, dsc3, dg3], axis=1).reshape(B, N_MOD * D)
    sink_row = jnp.pad(dsinks[:, :N_Q_HEADS], ((0, 0), (0, D - N_Q_HEADS)))
    loss_pad = jnp.pad(loss_row, ((0, 0), (0, D - loss_row.shape[1])))
    small = jnp.concatenate([dgn1, dgn2, dgn3, dgf, dconvb, dlng, dlnb, sink_row, dconvw, loss_pad], axis=0)
    small_all, gmod_all = _exchange([(small, "gather"), (gmod, "gather")], name="exchange_last")
    gsmall = _sum8(small_all, name="sum_small")
    loss = gsmall[n_small + CONV_WIDTH, 0]
    g_w_ada, g_b_ada = _ada_bwd(c_all, gmod_all.reshape(N_DEV * B, N_MOD * D), n_col=n_col, name="ada_bwd")
    g_conv_w = lax.dynamic_slice(gsmall[n_small:n_small + CONV_WIDTH], (0, me * (CC // N_DEV)),
                                 (CONV_WIDTH, CC // N_DEV))

    def col_update(name, w, m, v):
        outs = _sum8_adamw(parts[name], w[0].T, m[0].T, v[0].T, name="adamw_" + name)
        return tuple(o.T for o in outs)

    def row_update(name, w, m, v):
        return tuple(_sum8_adamw(parts[name], w[0], m[0], v[0], name="adamw_" + name))

    upd = {
        "ffn1_w_gate": col_update("ffn1_w_gate", ffn1_w_gate, m_ffn1_w_gate, v_ffn1_w_gate),
        "ffn1_w_up": col_update("ffn1_w_up", ffn1_w_up, m_ffn1_w_up, v_ffn1_w_up),
        "ffn1_w_down": row_update("ffn1_w_down", ffn1_w_down, m_ffn1_w_down, v_ffn1_w_down),
        "w_in": col_update("w_in", w_in, m_w_in, v_w_in),
        "w_attn_o": row_update("w_attn_o", w_attn_o, m_w_attn_o, v_w_attn_o),
        "w_conv_o": row_update("w_conv_o", w_conv_o, m_w_conv_o, v_w_conv_o),
        "w_out": row_update("w_out", w_out, m_w_out, v_w_out),
        "ffn2_w_gate": col_update("ffn2_w_gate", ffn2_w_gate, m_ffn2_w_gate, v_ffn2_w_gate),
        "ffn2_w_up": col_update("ffn2_w_up", ffn2_w_up, m_ffn2_w_up, v_ffn2_w_up),
        "ffn2_w_down": row_update("ffn2_w_down", ffn2_w_down, m_ffn2_w_down, v_ffn2_w_down),
        "w_ada": (g_w_ada,) + tuple(_adamw(g_w_ada, w_ada[0], m_w_ada[0], v_w_ada[0], name="adamw_w_ada")),
        "conv_w_dw": (g_conv_w,) + tuple(_adamw(g_conv_w, conv_w_dw[0], m_conv_w_dw[0], v_conv_w_dw[0],
                                                name="adamw_conv_w_dw")),
    }
    for k in upd:
        upd[k] = tuple(t[None] for t in upd[k])

    def pad_sinks(t):
        return jnp.pad(t, ((0, 0), (0, D - N_Q_HEADS)))

    def pack(f1, mix, f2, fin, cb, lg, lb, sinks, bada):
        return jnp.concatenate([f1, mix, f2, fin[None], cb, lg, lb, pad_sinks(sinks), bada.reshape(N_MOD, D)], axis=0)

    w_s = pack(norm_ffn1_g, norm_mix_g, norm_ffn2_g, final_norm_g, conv_b_dw, conv_ln_g, conv_ln_b, attn_sinks, b_ada)
    m_s = pack(m_norm_ffn1_g, m_norm_mix_g, m_norm_ffn2_g, m_final_norm_g, m_conv_b_dw, m_conv_ln_g, m_conv_ln_b,
               m_attn_sinks, m_b_ada)
    v_s = pack(v_norm_ffn1_g, v_norm_mix_g, v_norm_ffn2_g, v_final_norm_g, v_conv_b_dw, v_conv_ln_g, v_conv_ln_b,
               v_attn_sinks, v_b_ada)
    g_s = jnp.concatenate([gsmall[:n_small], g_b_ada.reshape(N_MOD, D)], axis=0)
    small_out = (g_s,) + tuple(_adamw(g_s, w_s, m_s, v_s, name="adamw_vectors"))

    def unpack(t):
        return {
            "norm_ffn1_g": t[0:1], "norm_mix_g": t[1:2], "norm_ffn2_g": t[2:3], "final_norm_g": t[3],
            "conv_b_dw": t[4:5], "conv_ln_g": t[5:6], "conv_ln_b": t[6:7], "attn_sinks": t[7:8, :N_Q_HEADS],
            "b_ada": t[n_small:n_small + N_MOD].reshape(1, N_MOD * D),
        }

    small_un = [unpack(t) for t in small_out]
    for k in small_un[0]:
        upd[k] = tuple(s[k] for s in small_un)

    order = ["w_ada", "b_ada", "norm_ffn1_g", "ffn1_w_gate", "ffn1_w_up", "ffn1_w_down", "norm_mix_g", "w_in",
             "attn_sinks", "w_attn_o", "conv_w_dw", "conv_b_dw", "conv_ln_g", "conv_ln_b", "w_conv_o", "w_out",
             "norm_ffn2_g", "ffn2_w_gate", "ffn2_w_up", "ffn2_w_down", "final_norm_g"]
    grad_x = dx0.reshape(B, S, D)
    return (loss, grad_x, *[upd[k][0] for k in order], *[upd[k][1] for k in order],
            *[upd[k][2] for k in order], *[upd[k][3] for k in order])
```

```python
import jax
import jax.numpy as jnp
from jax import lax
from jax.experimental import pallas as pl
from jax.experimental.pallas import tpu as pltpu

F32 = jnp.float32
BF16 = jnp.bfloat16
SDS = jax.ShapeDtypeStruct
MESH = pl.DeviceIdType.MESH

N_DEV = 8
EPS = 1e-6
HEAD_DIM = 64
N_Q_HEADS = 16
N_KV_HEADS = 2
GQA_GROUP = N_Q_HEADS // N_KV_HEADS
KV_WIDTH = N_KV_HEADS * HEAD_DIM
ATT_BLOCK = 128
CONV_WIDTH = 31
CONV_HALO = 32
CONV_ROWS = 64
N_MOD = 9
FFN_RESIDUAL = 0.5
ADAM_LR = 0.001
ADAM_B1 = 0.9
ADAM_B2 = 0.999
ADAM_EPS = 1e-08
ADAM_WD = 0.01
ADAM_STEP = 10
NEG_BIG = -1e30

V7X_VMEM_BYTES = 64 * 2**20
VMEM_CAP = V7X_VMEM_BYTES - 8 * 2**20


def _nbytes(shape, dtype):
    n = 1
    for s in shape:
        n *= s
    return n * jnp.dtype(dtype).itemsize


def _params(n_axes, blocks, temp_bytes=0):
    need = 2 * sum(_nbytes(s, d) for s, d in blocks) + temp_bytes + 4 * 2**20
    return pltpu.CompilerParams(dimension_semantics=("arbitrary",) * n_axes,
                                vmem_limit_bytes=int(min(max(need, 16 * 2**20), VMEM_CAP)))


def _dot_nt(a, b):
    return lax.dot_general(a, b, (((1,), (1,)), ((), ())), preferred_element_type=F32)


def _dot_tn(a, b):
    return lax.dot_general(a, b, (((0,), (0,)), ((), ())), preferred_element_type=F32)


def _dot(a, b):
    return jnp.dot(a, b, preferred_element_type=F32)


def _sigmoid(x):
    return jax.nn.sigmoid(x)


def _rowsum(v):
    return jnp.sum(v, axis=0, keepdims=True)


def _acc(ref, val, first):
    @pl.when(first)
    def _():
        ref[...] = val

    @pl.when(jnp.logical_not(first))
    def _():
        ref[...] = ref[...] + val


def _norm_mod(xf, gn, sh, sc):
    rstd = lax.rsqrt(jnp.mean(xf * xf, axis=-1, keepdims=True) + EPS)
    xhat = xf * rstd
    yn = xhat * gn
    return yn * (1.0 + sc) + sh, xhat, rstd, yn


def _pick(n, cands):
    for c in cands:
        if n % c == 0:
            return c
    return n


def _my_pos():
    return lax.axis_index("x"), lax.axis_index("y"), lax.axis_index("c")


def _peer(pos, k):
    x, y, c = pos
    return ((1 - x) if k & 4 else x, (1 - y) if k & 2 else y, (1 - c) if k & 1 else c)


def _lin(pos):
    return 4 * pos[0] + 2 * pos[1] + pos[2]


class _Comm:
    N_COPY = N_DEV - 1
    N_CHIP = N_DEV // 2

    def __init__(self, items):
        self.arrs = [a for a, _ in items]
        self.modes = [m for _, m in items]
        self.n = len(items)
        self.out = None

    def out_shape(self):
        def shape(a, m):
            return {"gather": (N_DEV,) + a.shape, "scatter": a.shape, "pair": (self.N_CHIP,) + a.shape[1:],
                    "cross": a.shape}[m]
        return [(pltpu.HBM if m == "gather" else SDS)(shape(a, m), a.dtype) for a, m in zip(self.arrs, self.modes)]

    def scratch(self):
        return [pltpu.SemaphoreType.DMA((self.n * self.N_COPY,)), pltpu.SemaphoreType.DMA((self.n * self.N_COPY,)),
                pltpu.SemaphoreType.DMA((self.n,))]

    def _plan(self, mode, me):
        x, y, c = me
        sib = (x, y, 1 - c)
        chips = [(1 - x, y), (x, 1 - y), (1 - x, 1 - y)]

        def chip_lin(ch):
            return 2 * ch[0] + ch[1]

        if mode == "scatter":
            peers = [_peer(me, k + 1) for k in range(self.N_COPY)]
            return [(p, ("in", _lin(p)), _lin(me), _lin(p), None) for p in peers], (_lin(me), _lin(me))
        if mode == "gather":
            same = [(*ch, c) for ch in chips]
            other = [(*ch, 1 - c) for ch in chips]
            copies = [(sib, ("in", None), _lin(me), _lin(sib), None)]
            copies += [(p, ("in", None), _lin(me), _lin(p), None) for p in same]
            copies += [(sib, ("out", _lin(p)), _lin(p), _lin(o), 1 + j) for j, (p, o) in enumerate(zip(same, other))]
            return copies, (None, _lin(me))
        if mode == "pair":
            return [(sib, ("in", 2 * q + 1 - c), q, q, None) for q in range(self.N_CHIP)], None
        if mode == "cross":
            mine = chip_lin((x, y))
            return ([((*ch, c), ("in", chip_lin(ch)), mine, chip_lin(ch), None) for ch in chips], (mine, mine))
        raise ValueError(mode)

    def _copy(self, refs, me, i, k, recv):
        srcs, outs, (send_sems, recv_sems, _) = refs
        peer, (where, slot), send_slot, recv_slot, _ = self._plan(self.modes[i], me)[0][k]
        src = srcs[i] if where == "in" else outs[i]
        src = src if slot is None else src.at[slot]
        sem = i * self.N_COPY + k
        return pltpu.make_async_remote_copy(
            src_ref=src, dst_ref=outs[i].at[recv_slot if recv else send_slot], send_sem=send_sems.at[sem],
            recv_sem=recv_sems.at[sem], device_id=peer, device_id_type=MESH)

    def _local(self, refs, me, i):
        srcs, outs, (_, _, loc_sems) = refs
        local = self._plan(self.modes[i], me)[1]
        if local is None:
            return None
        own = srcs[i] if local[0] is None else srcs[i].at[local[0]]
        return pltpu.make_async_copy(own, outs[i].at[local[1]], loc_sems.at[i])

    def start(self, refs):
        me = _my_pos()
        for i in range(self.n):
            local = self._local(refs, me, i)
            if local is not None:
                local.start()
            for k, cp in enumerate(self._plan(self.modes[i], me)[0]):
                if cp[4] is None:
                    self._copy(refs, me, i, k, False).start()

    def forward(self, refs):
        me = _my_pos()
        for i in range(self.n):
            for k, cp in enumerate(self._plan(self.modes[i], me)[0]):
                if cp[4] is not None:
                    self._copy(refs, me, i, cp[4], True).wait_recv()
                    self._copy(refs, me, i, k, False).start()

    def finish(self, refs):
        me = _my_pos()
        plans = [self._plan(m, me)[0] for m in self.modes]
        for i in range(self.n):
            passed_on = [cp[4] for cp in plans[i] if cp[4] is not None]
            for k in range(len(plans[i])):
                if k not in passed_on:
                    self._copy(refs, me, i, k, True).wait_recv()
                self._copy(refs, me, i, k, False).wait_send()
            local = self._local(refs, me, i)
            if local is not None:
                local.wait()


_ANY = pl.BlockSpec(memory_space=pl.ANY)


def _call(body, args, *, name, grid, in_specs, out_specs, out_shape, params, scratch_shapes=(), comm=None,
          hbm_out=()):
    in_specs, out_specs, out_shape = list(in_specs), list(out_specs), list(out_shape)
    scratch_shapes = list(scratch_shapes)
    for k in hbm_out:
        out_shape[k] = pltpu.HBM(out_shape[k].shape, out_shape[k].dtype)
    if comm is None:
        return list(pl.pallas_call(body, name=name, grid=grid, in_specs=in_specs, out_specs=out_specs,
                                   out_shape=out_shape, scratch_shapes=scratch_shapes, compiler_params=params)(*args))
    n_in, n_out, n_scr, nc = len(in_specs), len(out_specs), len(scratch_shapes), comm.n
    n_steps = 1
    for g in grid:
        n_steps *= g

    def hosted(*refs):
        ins, c_in = refs[:n_in], refs[n_in:n_in + nc]
        outs = refs[n_in + nc:n_in + nc + n_out]
        c_out = refs[n_in + nc + n_out:n_in + 2 * nc + n_out]
        scr = refs[n_in + 2 * nc + n_out:n_in + 2 * nc + n_out + n_scr]
        sems = refs[n_in + 2 * nc + n_out + n_scr:]
        step = pl.program_id(0)
        for d in range(1, len(grid)):
            step = step * grid[d] + pl.program_id(d)
        c_refs = (c_in, c_out, sems)

        @pl.when(step == 0)
        def _():
            comm.start(c_refs)

        if n_steps >= 3:
            @pl.when(step == n_steps - 2)
            def _():
                comm.forward(c_refs)

        body(*ins, *outs, *scr)

        @pl.when(step == n_steps - 1)
        def _():
            if n_steps < 3:
                comm.forward(c_refs)
            comm.finish(c_refs)

    res = pl.pallas_call(
        hosted, name=name, grid=grid, in_specs=in_specs + [_ANY] * nc, out_specs=out_specs + [_ANY] * nc,
        out_shape=out_shape + comm.out_shape(), scratch_shapes=scratch_shapes + comm.scratch(),
        compiler_params=params)(*args, *comm.arrs)
    comm.out = list(res[n_out:])
    return list(res[:n_out])


def _exchange(items, *, name):
    comm = _Comm(items)

    def body(*refs):
        r = (refs[:comm.n], refs[comm.n:2 * comm.n], refs[2 * comm.n:])
        comm.start(r)
        comm.forward(r)
        comm.finish(r)

    return list(pl.pallas_call(body, name=name, out_shape=comm.out_shape(), in_specs=[_ANY] * comm.n,
                               out_specs=[_ANY] * comm.n, scratch_shapes=comm.scratch())(*comm.arrs))


class _ModVec:
    def __init__(self, arr, idx):
        self.arr, self.idx = arr, idx

    def spec(self, tps, n_axes):
        idx, blk = self.idx, (1, 1, self.arr.shape[2])
        if n_axes == 1:
            return pl.BlockSpec(blk, lambda i: (i // tps * N_MOD + idx, 0, 0))
        return pl.BlockSpec(blk, lambda i, j: (i // tps * N_MOD + idx, 0, 0))


def _norm_mod_matmul(x, gn, sh, sc, wts, *, seq, tm, tn, name, comm=None):
    T, D = x.shape
    N = wts[0].shape[0]
    nw = len(wts)
    tps = seq // tm

    def body(x_ref, gn_ref, sh_ref, sc_ref, *rest):
        w_refs, h_ref, o_refs = rest[:nw], rest[nw], rest[nw + 1:]

        @pl.when(pl.program_id(1) == 0)
        def _():
            h_ref[...] = _norm_mod(x_ref[...], gn_ref[...], sh_ref[0], sc_ref[0])[0].astype(BF16)

        h = h_ref[...]
        for w_ref, o_ref in zip(w_refs, o_refs):
            o_ref[...] = _dot_nt(h, w_ref[...]).astype(o_ref.dtype)

    row = pl.BlockSpec((tm, D), lambda i, j: (i, 0))
    vec = pl.BlockSpec((1, D), lambda i, j: (0, 0))
    per_b = pl.BlockSpec((1, 1, D), lambda i, j: (i // tps, 0, 0))
    wspec = pl.BlockSpec((tn, D), lambda i, j: (j, 0))
    ospec = pl.BlockSpec((tm, tn), lambda i, j: (i, j))
    blocks = [((tm, D), F32), ((tm, D), BF16)] + [((tn, D), BF16), ((tm, tn), BF16)] * nw
    outs = _call(
        body, (x, gn, sh.arr, sc.arr, *wts), name=name, grid=(T // tm, N // tn),
        in_specs=[row, vec, sh.spec(tps, 2), sc.spec(tps, 2)] + [wspec] * nw,
        out_specs=[row] + [ospec] * nw,
        out_shape=[SDS((T, D), BF16)] + [SDS((T, N), BF16)] * nw,
        params=_params(2, blocks, temp_bytes=2 * _nbytes((tm, tn), F32) + 3 * _nbytes((tm, D), F32)), comm=comm,
        hbm_out=(0,))
    return outs[0], outs[1:]


def _matmul_nt(h, w, *, tm, tn, name, comm=None):
    T, D = h.shape
    N = w.shape[0]

    def body(h_ref, w_ref, o_ref):
        o_ref[...] = _dot_nt(h_ref[...], w_ref[...]).astype(o_ref.dtype)

    blocks = [((tm, D), BF16), ((tn, D), BF16), ((tm, tn), BF16)]
    return _call(
        body, (h, w), name=name, grid=(T // tm, N // tn),
        in_specs=[pl.BlockSpec((tm, D), lambda i, j: (i, 0)), pl.BlockSpec((tn, D), lambda i, j: (j, 0))],
        out_specs=[pl.BlockSpec((tm, tn), lambda i, j: (i, j))],
        out_shape=[SDS((T, N), BF16)],
        params=_params(2, blocks, temp_bytes=2 * _nbytes((tm, tn), F32)), comm=comm)[0]


def _ffn_down(a, b, wd, x, g, *, seq, tm, name, comm=None):
    T, F = a.shape
    D = wd.shape[1]
    tps = seq // tm

    def body(a_ref, b_ref, wd_ref, x_ref, g_ref, xo_ref, y_ref):
        af = a_ref[...].astype(F32)
        act = (af * _sigmoid(af) * b_ref[...].astype(F32)).astype(BF16)
        y = _dot(act, wd_ref[...])
        xo_ref[...] = x_ref[...] + (FFN_RESIDUAL * g_ref[0]) * y
        y_ref[...] = y.astype(BF16)

    wide = pl.BlockSpec((tm, F), lambda i: (i, 0))
    row = pl.BlockSpec((tm, D), lambda i: (i, 0))
    per_b = pl.BlockSpec((1, 1, D), lambda i: (i // tps, 0, 0))
    wspec = pl.BlockSpec((F, D), lambda i: (0, 0))
    blocks = [((tm, F), BF16)] * 2 + [((F, D), BF16), ((tm, D), F32), ((tm, D), F32), ((tm, D), BF16)]
    return _call(
        body, (a, b, wd, x, g.arr), name=name, grid=(T // tm,),
        in_specs=[wide, wide, wspec, row, g.spec(tps, 1)], out_specs=[row, row],
        out_shape=[SDS((T, D), F32), SDS((T, D), BF16)],
        params=_params(1, blocks, temp_bytes=3 * _nbytes((tm, F), F32)), comm=comm)


def _final_loss(x, gf, tgt, *, tm, name):
    T, D = x.shape
    nt = T // tm

    def body(x_ref, gf_ref, t_ref, dx_ref, loss_ref, dgf_ref, lacc):
        i = pl.program_id(0)
        xf = x_ref[...]
        gfv = gf_ref[...]
        rstd = lax.rsqrt(jnp.mean(xf * xf, axis=-1, keepdims=True) + EPS)
        xhat = xf * rstd
        err = xhat * gfv - t_ref[...]
        dy = err * (1.0 / D)
        dxhat = dy * gfv
        dx_ref[...] = rstd * (dxhat - xhat * jnp.mean(dxhat * xhat, axis=-1, keepdims=True))
        _acc(dgf_ref, _rowsum(dy * xhat), i == 0)
        _acc(lacc, _rowsum(err * err), i == 0)

        @pl.when(i == nt - 1)
        def _():
            loss_ref[...] = jnp.broadcast_to((0.5 / D) * jnp.sum(lacc[...]), loss_ref.shape)

    row = pl.BlockSpec((tm, D), lambda i: (i, 0))
    vec = pl.BlockSpec((1, D), lambda i: (0, 0))
    lspec = pl.BlockSpec((1, 128), lambda i: (0, 0))
    blocks = [((tm, D), F32)] * 3
    return _call(
        body, (x, gf, tgt), name=name, grid=(nt,),
        in_specs=[row, vec, row], out_specs=[row, lspec, vec],
        out_shape=[SDS((T, D), F32), SDS((1, 128), F32), SDS((1, D), F32)],
        scratch_shapes=[pltpu.VMEM((1, D), F32)],
        params=_params(1, blocks, temp_bytes=4 * _nbytes((tm, D), F32)), hbm_out=(0,))


def _ffn_bwd_down(dxo, g, y, wd, a, b, *, seq, tm, tn, name, comm=None):
    T, F = a.shape
    D = wd.shape[1]
    tps = seq // tm
    nb = T // seq

    def body(dxo_ref, g_ref, y_ref, wd_ref, a_ref, b_ref, dyb_ref, da_ref, db_ref, act_ref, dg_ref):
        i = pl.program_id(0)

        @pl.when(pl.program_id(1) == 0)
        def _():
            dx = dxo_ref[...]
            dyb_ref[...] = ((FFN_RESIDUAL * g_ref[0]) * dx).astype(BF16)
            part = _rowsum(FFN_RESIDUAL * dx * y_ref[...].astype(F32))
            _acc(dg_ref, part[None], i % tps == 0)

        dact = _dot_nt(dyb_ref[...], wd_ref[...])
        af = a_ref[...].astype(F32)
        bf = b_ref[...].astype(F32)
        sg = _sigmoid(af)
        silu = af * sg
        act_ref[...] = (silu * bf).astype(BF16)
        da_ref[...] = (dact * bf * (sg * (1.0 + af * (1.0 - sg)))).astype(BF16)
        db_ref[...] = (dact * silu).astype(BF16)

    row = pl.BlockSpec((tm, D), lambda i, j: (i, 0))
    per_b = pl.BlockSpec((1, 1, D), lambda i, j: (i // tps, 0, 0))
    wspec = pl.BlockSpec((tn, D), lambda i, j: (j, 0))
    chunk = pl.BlockSpec((tm, tn), lambda i, j: (i, j))
    blocks = [((tm, D), F32), ((tm, D), BF16), ((tn, D), BF16), ((tm, D), BF16)] + [((tm, tn), BF16)] * 5
    return _call(
        body, (dxo, g.arr, y, wd, a, b), name=name, grid=(T // tm, F // tn),
        in_specs=[row, g.spec(tps, 2), row, wspec, chunk, chunk],
        out_specs=[row, chunk, chunk, chunk, per_b],
        out_shape=[SDS((T, D), BF16)] + [SDS((T, F), BF16)] * 3 + [SDS((nb, 1, D), F32)],
        params=_params(2, blocks, temp_bytes=6 * _nbytes((tm, tn), F32)), comm=comm)


def _matmul_norm_mod_bwd(ds, ws, x, gn, sc, dxo, *, seq, tm, name, comm=None):
    T, D = x.shape
    nk = len(ws)
    sizes = [len(g) for g in ds]
    ds = [d for g in ds for d in g]
    tps = seq // tm
    nb = T // seq

    def body(*refs):
        w_refs = refs[len(ds):len(ds) + nk]
        x_ref, gn_ref, sc_ref, dxo_ref, dxi_ref, dsh_ref, dsc_ref, dgn_ref = refs[len(ds) + nk:]
        i = pl.program_id(0)
        dh, at = None, 0
        for n, w_ref in zip(sizes, w_refs):
            pieces = [r[...] for r in refs[at:at + n]]
            at += n
            part = _dot(pieces[0] if n == 1 else jnp.concatenate(pieces, axis=1), w_ref[...])
            dh = part if dh is None else dh + part
        gnv = gn_ref[...]
        scv = sc_ref[0]
        _, xhat, rstd, yn = _norm_mod(x_ref[...], gnv, 0.0, scv)
        dyn = dh * (1.0 + scv)
        dxhat = dyn * gnv
        dxi_ref[...] = dxo_ref[...] + rstd * (dxhat - xhat * jnp.mean(dxhat * xhat, axis=-1, keepdims=True))
        first_of_seq = i % tps == 0
        _acc(dsh_ref, _rowsum(dh)[None], first_of_seq)
        _acc(dsc_ref, _rowsum(dh * yn)[None], first_of_seq)
        _acc(dgn_ref, _rowsum(dyn * xhat), i == 0)

    row = pl.BlockSpec((tm, D), lambda i: (i, 0))
    vec = pl.BlockSpec((1, D), lambda i: (0, 0))
    per_b = pl.BlockSpec((1, 1, D), lambda i: (i // tps, 0, 0))
    d_specs = [pl.BlockSpec((tm, d.shape[1]), lambda i: (i, 0)) for d in ds]
    w_specs = [pl.BlockSpec(w.shape, lambda i: (0, 0)) for w in ws]
    blocks = ([((tm, d.shape[1]), BF16) for d in ds] + [(w.shape, BF16) for w in ws] + [((tm, D), F32)] * 3)
    return _call(
        body, (*ds, *ws, x, gn, sc.arr, dxo), name=name, grid=(T // tm,),
        in_specs=d_specs + w_specs + [row, vec, sc.spec(tps, 1), row],
        out_specs=[row, per_b, per_b, vec],
        out_shape=[SDS((T, D), F32), SDS((nb, 1, D), F32), SDS((nb, 1, D), F32), SDS((1, D), F32)],
        params=_params(1, blocks, temp_bytes=6 * _nbytes((tm, D), F32)), comm=comm)


def _layernorm_silu(yc, lg, lb):
    mu = jnp.mean(yc, axis=-1, keepdims=True)
    cen = yc - mu
    rstd = lax.rsqrt(jnp.mean(cen * cen, axis=-1, keepdims=True) + EPS)
    xh = cen * rstd
    l = xh * lg + lb
    s = _sigmoid(l)
    return l * s, xh, rstd, l, s


GATE_W = 256


def _gate_specs(tm, D, col):
    return [pl.BlockSpec((tm, GATE_W), lambda i, blk=col // GATE_W + t: (i, blk)) for t in range(D // GATE_W)]


def _gate(refs):
    return jnp.concatenate([r[...] for r in refs], axis=1).astype(F32)


def _mix_out(ao, yc, proj, wao, wco, wout, x1, g2, lg, lb, *, seq, tm, ga_col, gc_col, name, comm=None):
    T, D = x1.shape
    tps = seq // tm
    ng = D // GATE_W

    def body(ao_ref, yc_ref, *rest):
        ga_refs, gc_refs = rest[:ng], rest[ng:2 * ng]
        (wao_ref, wco_ref, wout_ref, x1_ref, g2_ref, lg_ref, lb_ref,
         x2_ref, z_ref, ya_ref, ycv_ref, cact_ref, mrg_ref) = rest[2 * ng:]
        ya = _dot(ao_ref[...], wao_ref[...])
        cact = _layernorm_silu(yc_ref[...], lg_ref[...], lb_ref[...])[0].astype(BF16)
        ycv = _dot(cact, wco_ref[...])
        merged = (_sigmoid(_gate(ga_refs)) * ya + _sigmoid(_gate(gc_refs)) * ycv).astype(BF16)
        z = _dot(merged, wout_ref[...])
        x2_ref[...] = x1_ref[...] + g2_ref[0] * z
        z_ref[...] = z.astype(BF16)
        ya_ref[...] = ya.astype(BF16)
        ycv_ref[...] = ycv.astype(BF16)
        cact_ref[...] = cact
        mrg_ref[...] = merged

    row = pl.BlockSpec((tm, D), lambda i: (i, 0))
    vec = pl.BlockSpec((1, D), lambda i: (0, 0))
    per_b = pl.BlockSpec((1, 1, D), lambda i: (i // tps, 0, 0))
    wspec = pl.BlockSpec((D, D), lambda i: (0, 0))
    gates = _gate_specs(tm, D, ga_col) + _gate_specs(tm, D, gc_col)
    blocks = ([((tm, D), BF16), ((tm, D), F32), ((tm, D), BF16), ((tm, D), BF16)] + [((D, D), BF16)] * 3
              + [((tm, D), F32)] * 2 + [((tm, D), BF16)] * 5)
    return _call(
        body, (ao, yc, *[proj] * (2 * ng), wao, wco, wout, x1, g2.arr, lg, lb), name=name, grid=(T // tm,),
        in_specs=[row, row, *gates, wspec, wspec, wspec, row, g2.spec(tps, 1), vec, vec],
        out_specs=[row] * 6,
        out_shape=[SDS((T, D), F32)] + [SDS((T, D), BF16)] * 5,
        params=_params(1, blocks, temp_bytes=8 * _nbytes((tm, D), F32)), comm=comm)


def _mix_out_bwd(dx2, g2, z, wout, proj, ya, ycv, wao, wco, yc, lg, lb, *, seq, tm, ga_col, gc_col, name,
                 comm=None):
    T, D = dx2.shape
    tps = seq // tm
    nb = T // seq
    ng = D // GATE_W

    def body(dx2_ref, g2_ref, z_ref, wout_ref, *rest):
        ga_refs, gc_refs = rest[:ng], rest[ng:2 * ng]
        (ya_ref, ycv_ref, wao_ref, wco_ref, yc_ref, lg_ref, lb_ref, dz_ref, dya_ref, dycv_ref, dga_ref, dgc_ref,
         dao_ref, dyc_ref, dg2_ref, dlg_ref, dlb_ref) = rest[2 * ng:]
        i = pl.program_id(0)
        dx = dx2_ref[...]
        _acc(dg2_ref, _rowsum(dx * z_ref[...].astype(F32))[None], i % tps == 0)
        dzb = (g2_ref[0] * dx).astype(BF16)
        dz_ref[...] = dzb
        dmerged = _dot_nt(dzb, wout_ref[...])
        sa = _sigmoid(_gate(ga_refs))
        sc_ = _sigmoid(_gate(gc_refs))
        dya = (dmerged * sa).astype(BF16)
        dycv = (dmerged * sc_).astype(BF16)
        dya_ref[...] = dya
        dycv_ref[...] = dycv
        dga_ref[...] = (dmerged * ya_ref[...].astype(F32) * (sa * (1.0 - sa))).astype(BF16)
        dgc_ref[...] = (dmerged * ycv_ref[...].astype(F32) * (sc_ * (1.0 - sc_))).astype(BF16)
        dao_ref[...] = _dot_nt(dya, wao_ref[...]).astype(BF16)
        dcact = _dot_nt(dycv, wco_ref[...])
        lgv = lg_ref[...]
        _, xh, rstd, l, s = _layernorm_silu(yc_ref[...], lgv, lb_ref[...])
        dl = dcact * (s * (1.0 + l * (1.0 - s)))
        _acc(dlb_ref, _rowsum(dl), i == 0)
        _acc(dlg_ref, _rowsum(dl * xh), i == 0)
        dxh = dl * lgv
        dyc_ref[...] = rstd * (dxh - jnp.mean(dxh, axis=-1, keepdims=True)
                               - xh * jnp.mean(dxh * xh, axis=-1, keepdims=True))

    row = pl.BlockSpec((tm, D), lambda i: (i, 0))
    vec = pl.BlockSpec((1, D), lambda i: (0, 0))
    per_b = pl.BlockSpec((1, 1, D), lambda i: (i // tps, 0, 0))
    wspec = pl.BlockSpec((D, D), lambda i: (0, 0))
    gates = _gate_specs(tm, D, ga_col) + _gate_specs(tm, D, gc_col)
    blocks = ([((tm, D), F32)] * 3 + [((tm, D), BF16)] * 11 + [((D, D), BF16)] * 3)
    return _call(
        body, (dx2, g2.arr, z, wout, *[proj] * (2 * ng), ya, ycv, wao, wco, yc, lg, lb), name=name,
        grid=(T // tm,),
        in_specs=[row, g2.spec(tps, 1), row, wspec, *gates, row, row, wspec, wspec, row, vec, vec],
        out_specs=[row] * 7 + [per_b, vec, vec],
        out_shape=[SDS((T, D), BF16)] * 6 + [SDS((T, D), F32), SDS((nb, 1, D), F32), SDS((1, D), F32),
                                             SDS((1, D), F32)],
        params=_params(1, blocks, temp_bytes=10 * _nbytes((tm, D), F32)), comm=comm)


GROUP_ROWS = GQA_GROUP * ATT_BLOCK
PAIR_W = 2 * HEAD_DIM
GROUP_W = GQA_GROUP * HEAD_DIM


def _lane_lo():
    return lax.broadcasted_iota(jnp.int32, (1, PAIR_W), 1) < HEAD_DIM


def _band_bias():
    sj = lax.broadcasted_iota(jnp.int32, (2 * ATT_BLOCK, GROUP_ROWS), 0)
    qi = lax.broadcasted_iota(jnp.int32, (2 * ATT_BLOCK, GROUP_ROWS), 1) & (ATT_BLOCK - 1)
    rel = qi + ATT_BLOCK - sj
    bias = jnp.where(jnp.logical_and(rel >= 0, rel < ATT_BLOCK), 0.0, NEG_BIG)
    sj1 = lax.broadcasted_iota(jnp.int32, (2 * ATT_BLOCK, 1), 0)
    return bias, jnp.where(sj1 < ATT_BLOCK, NEG_BIG, 0.0)


def _dup_heads(src_ref, dst, seq):
    x = src_ref[...]
    i = lax.broadcasted_iota(jnp.int32, (KV_WIDTH, PAIR_W), 0)
    j = lax.broadcasted_iota(jnp.int32, (KV_WIDTH, PAIR_W), 1) & (HEAD_DIM - 1)
    for g in range(N_KV_HEADS):
        sel = jnp.where(i == j + g * HEAD_DIM, 1.0, 0.0).astype(BF16)
        dst[g, pl.ds(0, ATT_BLOCK), :] = jnp.zeros((ATT_BLOCK, PAIR_W), BF16)
        dst[g, pl.ds(ATT_BLOCK, seq), :] = _dot(x, sel).astype(BF16)


def _stack_heads(blk, g, lo):
    parts = []
    for p in range(GQA_GROUP // 2):
        pair = blk[:, g * GROUP_W + p * PAIR_W:g * GROUP_W + (p + 1) * PAIR_W]
        parts += [jnp.where(lo, pair, jnp.zeros_like(pair)), jnp.where(lo, jnp.zeros_like(pair), pair)]
    return jnp.concatenate(parts, axis=0)


def _unstack_heads(full, ref, r0, g, lo):
    for p in range(GQA_GROUP // 2):
        even = full[(2 * p) * ATT_BLOCK:(2 * p + 1) * ATT_BLOCK, :]
        odd = full[(2 * p + 1) * ATT_BLOCK:(2 * p + 2) * ATT_BLOCK, :]
        ref[pl.ds(r0, ATT_BLOCK), g * GROUP_W + p * PAIR_W:g * GROUP_W + (p + 1) * PAIR_W] = (
            jnp.where(lo, even, odd).astype(ref.dtype))


def _sink_row(sink_ref, g):
    return jnp.concatenate([jnp.full((1, ATT_BLOCK), sink_ref[0, g * GQA_GROUP + h], F32)
                            for h in range(GQA_GROUP)], axis=1)


def _group_probs(qs, k2, bias, sink):
    s = _dot_nt(k2, qs) * (HEAD_DIM ** -0.5) + bias
    m = jnp.maximum(jnp.max(s, axis=0, keepdims=True), sink)
    p = jnp.exp(s - m)
    psink = jnp.exp(sink - m)
    inv = 1.0 / (jnp.sum(p, axis=0, keepdims=True) + psink)
    return p * inv, psink * inv


def _attn_fwd(projp, sinks, *, seq, q_blk, k_blk, v_blk, name, comm=None):
    T = projp.shape[0]
    QW = N_Q_HEADS * HEAD_DIM
    nblk = seq // ATT_BLOCK

    def body(q_ref, k_ref, v_ref, sink_ref, o_ref, k2s, v2s):
        _dup_heads(k_ref, k2s, seq)
        _dup_heads(v_ref, v2s, seq)
        lo = _lane_lo()
        bias0, first_pen = _band_bias()
        sink_rows = [_sink_row(sink_ref, g) for g in range(N_KV_HEADS)]

        def blk(n, carry):
            r0 = pl.multiple_of(n * ATT_BLOCK, ATT_BLOCK)
            qb = q_ref[pl.ds(r0, ATT_BLOCK), :]
            bias = bias0 + jnp.where(n == 0, 1.0, 0.0) * first_pen
            for g in range(N_KV_HEADS):
                probs_t, _ = _group_probs(_stack_heads(qb, g, lo), k2s[g, pl.ds(r0, 2 * ATT_BLOCK), :], bias,
                                          sink_rows[g])
                _unstack_heads(_dot_tn(probs_t.astype(BF16), v2s[g, pl.ds(r0, 2 * ATT_BLOCK), :]), o_ref, r0, g, lo)
            return carry

        lax.fori_loop(0, nblk, blk, 0)

    blocks = [((seq, QW), BF16)] * 2 + [((seq, KV_WIDTH), BF16)] * 2
    return _call(
        body, (projp, projp, projp, sinks), name=name, grid=(T // seq,),
        in_specs=[pl.BlockSpec((seq, QW), lambda b: (b, q_blk)),
                  pl.BlockSpec((seq, KV_WIDTH), lambda b: (b, k_blk)),
                  pl.BlockSpec((seq, KV_WIDTH), lambda b: (b, v_blk)),
                  pl.BlockSpec(memory_space=pltpu.SMEM)],
        out_specs=[pl.BlockSpec((seq, QW), lambda b: (b, 0))],
        out_shape=[SDS((T, QW), BF16)],
        scratch_shapes=[pltpu.VMEM((N_KV_HEADS, seq + ATT_BLOCK, PAIR_W), BF16)] * 2,
        params=_params(1, blocks, temp_bytes=16 * 2**20), comm=comm, hbm_out=(0,))[0]


def _attn_bwd(projp, dao, sinks, *, seq, q_blk, k_blk, v_blk, name, comm=None):
    T = projp.shape[0]
    QW = N_Q_HEADS * HEAD_DIM
    nblk = seq // ATT_BLOCK

    def body(q_ref, k_ref, v_ref, do_ref, sink_ref, dq_ref, dk_ref, dv_ref, dsink_ref, k2s, v2s, dkacc, dvacc):
        _dup_heads(k_ref, k2s, seq)
        _dup_heads(v_ref, v2s, seq)
        dkacc[...] = jnp.zeros(dkacc.shape, F32)
        dvacc[...] = jnp.zeros(dvacc.shape, F32)
        lane = lax.broadcasted_iota(jnp.int32, (1, PAIR_W), 1)
        lo = lane < HEAD_DIM
        bias0, first_pen = _band_bias()
        sink_rows = [_sink_row(sink_ref, g) for g in range(N_KV_HEADS)]

        def blk(n, dsink):
            r0 = pl.multiple_of(n * ATT_BLOCK, ATT_BLOCK)
            band = pl.ds(r0, 2 * ATT_BLOCK)
            qb = q_ref[pl.ds(r0, ATT_BLOCK), :]
            dob = do_ref[pl.ds(r0, ATT_BLOCK), :]
            bias = bias0 + jnp.where(n == 0, 1.0, 0.0) * first_pen
            for g in range(N_KV_HEADS):
                qs = _stack_heads(qb, g, lo)
                dos = _stack_heads(dob, g, lo)
                k2 = k2s[g, band, :]
                v2 = v2s[g, band, :]
                probs_t, psink = _group_probs(qs, k2, bias, sink_rows[g])
                dp_t = _dot_nt(v2, dos)
                delta = jnp.sum(probs_t * dp_t, axis=0, keepdims=True)
                ds_t = (probs_t * (dp_t - delta) * (HEAD_DIM ** -0.5)).astype(BF16)
                tsink = psink * delta
                for h in range(GQA_GROUP):
                    dsink = dsink + jnp.where(lane == g * GQA_GROUP + h,
                                              -jnp.sum(tsink[:, h * ATT_BLOCK:(h + 1) * ATT_BLOCK]), 0.0)
                _unstack_heads(_dot_tn(ds_t, k2), dq_ref, r0, g, lo)
                dkacc[g, band, :] = dkacc[g, band, :] + _dot(ds_t, qs)
                dvacc[g, band, :] = dvacc[g, band, :] + _dot(probs_t.astype(BF16), dos)
            return dsink

        dsink = lax.fori_loop(0, nblk, blk, jnp.zeros((1, PAIR_W), F32))
        _acc(dsink_ref, dsink, pl.program_id(0) == 0)

        def fold(acc, g):
            a = acc[g, pl.ds(ATT_BLOCK, seq), :]
            return a + pltpu.roll(a, HEAD_DIM, 1)

        dk_ref[...] = jnp.where(lo, fold(dkacc, 0), fold(dkacc, 1)).astype(BF16)
        dv_ref[...] = jnp.where(lo, fold(dvacc, 0), fold(dvacc, 1)).astype(BF16)

    blocks = [((seq, QW), BF16)] * 3 + [((seq, KV_WIDTH), BF16)] * 4
    kv_spec_out = pl.BlockSpec((seq, KV_WIDTH), lambda b: (b, 0))
    return _call(
        body, (projp, projp, projp, dao, sinks), name=name, grid=(T // seq,),
        in_specs=[pl.BlockSpec((seq, QW), lambda b: (b, q_blk)),
                  pl.BlockSpec((seq, KV_WIDTH), lambda b: (b, k_blk)),
                  pl.BlockSpec((seq, KV_WIDTH), lambda b: (b, v_blk)),
                  pl.BlockSpec((seq, QW), lambda b: (b, 0)),
                  pl.BlockSpec(memory_space=pltpu.SMEM)],
        out_specs=[pl.BlockSpec((seq, QW), lambda b: (b, 0)), kv_spec_out, kv_spec_out,
                   pl.BlockSpec((1, 128), lambda b: (0, 0))],
        out_shape=[SDS((T, QW), BF16), SDS((T, KV_WIDTH), BF16), SDS((T, KV_WIDTH), BF16), SDS((1, 128), F32)],
        scratch_shapes=[pltpu.VMEM((N_KV_HEADS, seq + ATT_BLOCK, PAIR_W), BF16)] * 2
        + [pltpu.VMEM((N_KV_HEADS, seq + ATT_BLOCK, PAIR_W), F32)] * 2,
        params=_params(1, blocks, temp_bytes=24 * 2**20), comm=comm)


SUBLANES = 8


def _sublane_shifts(win):
    n = CONV_ROWS + CONV_HALO
    return [win] + [pltpu.roll(win, n - b, 0) for b in range(1, SUBLANES)]


def _window(shifted, off):
    a = off // SUBLANES * SUBLANES
    return shifted[off % SUBLANES][a:a + CONV_ROWS, :]


def _conv_fwd(projp, w, bias, *, seq, cw, a_col, b_col, name, comm=None):
    T = projp.shape[0]
    C = w.shape[1]
    nchunk = seq // CONV_ROWS

    def body(a_ref, b_ref, w_ref, bias_ref, y_ref, upad):
        upad[pl.ds(0, CONV_HALO), :] = jnp.zeros((CONV_HALO, cw), F32)
        upad[pl.ds(CONV_HALO, seq), :] = a_ref[...].astype(F32) * _sigmoid(b_ref[...].astype(F32))
        wv = w_ref[...]
        bv = bias_ref[...]

        def chunk(r, carry):
            r0 = pl.multiple_of(r * CONV_ROWS, CONV_ROWS)
            shifted = _sublane_shifts(upad[pl.ds(r0, CONV_ROWS + CONV_HALO), :])
            acc = jnp.broadcast_to(bv, (CONV_ROWS, cw))
            for k in range(CONV_WIDTH):
                acc = acc + wv[k:k + 1, :] * _window(shifted, CONV_HALO - (CONV_WIDTH - 1) + k)
            y_ref[pl.ds(r0, CONV_ROWS), :] = acc
            return carry

        lax.fori_loop(0, nchunk, chunk, 0)

    blocks = [((seq, cw), BF16)] * 2 + [((seq, cw), F32)]
    return _call(
        body, (projp, projp, w, bias), name=name, grid=(T // seq, C // cw),
        in_specs=[pl.BlockSpec((seq, cw), lambda b, c: (b, a_col // cw + c)),
                  pl.BlockSpec((seq, cw), lambda b, c: (b, b_col // cw + c)),
                  pl.BlockSpec((CONV_WIDTH, cw), lambda b, c: (0, c)),
                  pl.BlockSpec((1, cw), lambda b, c: (0, c))],
        out_specs=[pl.BlockSpec((seq, cw), lambda b, c: (b, c))],
        out_shape=[SDS((T, C), F32)],
        scratch_shapes=[pltpu.VMEM((seq + CONV_HALO, cw), F32)],
        params=_params(2, blocks, temp_bytes=6 * _nbytes((seq, cw), F32)), comm=comm, hbm_out=(0,))[0]


def _conv_bwd(dy, projp, w, *, seq, cw, a_col, b_col, name, comm=None):
    T = projp.shape[0]
    C = w.shape[1]
    nchunk = seq // CONV_ROWS
    SUB = 8

    def body(dy_ref, a_ref, b_ref, w_ref, da_ref, db_ref, dw_ref, dbias_ref, dypad, dwp):
        first = pl.program_id(1) == 0
        dyv = dy_ref[...]
        dypad[pl.ds(0, seq), :] = dyv
        dypad[pl.ds(seq, CONV_HALO), :] = jnp.zeros((CONV_HALO, cw), F32)
        dwp[...] = jnp.zeros(dwp.shape, F32)
        wv = w_ref[...]

        def chunk(r, carry):
            r0 = pl.multiple_of(r * CONV_ROWS, CONV_ROWS)
            dy_shifts = _sublane_shifts(dypad[pl.ds(r0, CONV_ROWS + CONV_HALO), :])
            ac = a_ref[pl.ds(r0, CONV_ROWS), :].astype(F32)
            sbc = _sigmoid(b_ref[pl.ds(r0, CONV_ROWS), :].astype(F32))
            uc = ac * sbc
            du = jnp.zeros((CONV_ROWS, cw), F32)
            for k in range(CONV_WIDTH):
                dyk = _window(dy_shifts, CONV_WIDTH - 1 - k)
                du = du + wv[k:k + 1, :] * dyk
                prod = uc * dyk
                part = prod[0:SUB, :]
                for s in range(1, CONV_ROWS // SUB):
                    part = part + prod[s * SUB:(s + 1) * SUB, :]
                dwp[pl.ds(k * SUB, SUB), :] = dwp[pl.ds(k * SUB, SUB), :] + part
            da_ref[pl.ds(r0, CONV_ROWS), :] = (du * sbc).astype(BF16)
            db_ref[pl.ds(r0, CONV_ROWS), :] = (du * ac * (sbc * (1.0 - sbc))).astype(BF16)
            return carry

        lax.fori_loop(0, nchunk, chunk, 0)

        @pl.when(first)
        def _():
            dw_ref[...] = jnp.zeros(dw_ref.shape, F32)
            dbias_ref[...] = jnp.zeros(dbias_ref.shape, F32)

        for k in range(CONV_WIDTH):
            dw_ref[k:k + 1, :] = dw_ref[k:k + 1, :] + _rowsum(dwp[pl.ds(k * SUB, SUB), :])
        dbias_ref[...] = dbias_ref[...] + _rowsum(dyv)

    blocks = [((seq, cw), F32)] + [((seq, cw), BF16)] * 4
    return _call(
        body, (dy, projp, projp, w), name=name, grid=(C // cw, T // seq),
        in_specs=[pl.BlockSpec((seq, cw), lambda c, b: (b, c)),
                  pl.BlockSpec((seq, cw), lambda c, b: (b, a_col // cw + c)),
                  pl.BlockSpec((seq, cw), lambda c, b: (b, b_col // cw + c)),
                  pl.BlockSpec((CONV_WIDTH, cw), lambda c, b: (0, c))],
        out_specs=[pl.BlockSpec((seq, cw), lambda c, b: (b, c)), pl.BlockSpec((seq, cw), lambda c, b: (b, c)),
                   pl.BlockSpec((CONV_WIDTH, cw), lambda c, b: (0, c)), pl.BlockSpec((1, cw), lambda c, b: (0, c))],
        out_shape=[SDS((T, C), BF16), SDS((T, C), BF16), SDS((CONV_WIDTH, C), F32), SDS((1, C), F32)],
        scratch_shapes=[pltpu.VMEM((seq + CONV_HALO, cw), F32), pltpu.VMEM((CONV_WIDTH * SUB, cw), F32)],
        params=_params(2, blocks, temp_bytes=8 * _nbytes((seq, cw), F32)), comm=comm)


def _matmul_tn(a, b, *, name, comm=None):
    T, M = a.shape
    N = b.shape[1]
    bm = _pick(M, (768, 512, 256))

    def body(a_ref, b_ref, o_ref):
        o_ref[...] = _dot_tn(a_ref[...], b_ref[...]).astype(BF16)

    blocks = [((T, bm), BF16), ((T, N), BF16), ((bm, N), BF16)]
    return _call(
        body, (a, b), name=name, grid=(M // bm,),
        in_specs=[pl.BlockSpec((T, bm), lambda i: (0, i)), pl.BlockSpec((T, N), lambda i: (0, 0))],
        out_specs=[pl.BlockSpec((bm, N), lambda i: (i, 0))],
        out_shape=[SDS((M, N), BF16)],
        params=_params(1, blocks, temp_bytes=2 * _nbytes((T, bm), BF16) + 2 * _nbytes((bm, N), F32)),
        comm=comm, hbm_out=(0,))[0]


TN_BLOCK = 256


def _matmul_tn_pieces(groups, b, *, name, comm=None):
    T, N = b.shape
    flat = [a for g in groups for a in g]
    starts, n_steps = [], 0
    for g in groups:
        width = sum(a.shape[1] for a in g)
        assert width % TN_BLOCK == 0 and (len(g) == 1 or width == TN_BLOCK), [a.shape for a in g]
        starts.append(n_steps)
        n_steps += width // TN_BLOCK

    def body(*refs):
        a_refs, b_ref, o_ref = refs[:len(flat)], refs[len(flat)], refs[len(flat) + 1]
        i = pl.program_id(0)
        at = 0
        for g, start in zip(groups, starts):
            mine = a_refs[at:at + len(g)]
            at += len(g)
            steps = sum(a.shape[1] for a in g) // TN_BLOCK

            @pl.when(jnp.logical_and(i >= start, i < start + steps))
            def _(mine=mine):
                a = mine[0][...] if len(mine) == 1 else jnp.concatenate([r[...] for r in mine], axis=1)
                o_ref[...] = _dot_tn(a, b_ref[...]).astype(BF16)

    a_specs = []
    for g, start in zip(groups, starts):
        for a in g:
            if len(g) == 1:
                last = a.shape[1] // TN_BLOCK - 1
                a_specs.append(pl.BlockSpec(
                    (T, TN_BLOCK), lambda i, start=start, last=last: (0, jnp.clip(i - start, 0, last))))
            else:
                a_specs.append(pl.BlockSpec((T, a.shape[1]), lambda i: (0, 0)))
    blocks = [((T, TN_BLOCK), BF16)] * len(flat) + [((T, N), BF16), ((TN_BLOCK, N), BF16)]
    return _call(
        body, (*flat, b), name=name, grid=(n_steps,),
        in_specs=a_specs + [pl.BlockSpec((T, N), lambda i: (0, 0))],
        out_specs=[pl.BlockSpec((TN_BLOCK, N), lambda i: (i, 0))],
        out_shape=[SDS((n_steps * TN_BLOCK, N), BF16)],
        params=_params(1, blocks, temp_bytes=2 * _nbytes((T, TN_BLOCK), BF16) + 2 * _nbytes((TN_BLOCK, N), F32)),
        comm=comm, hbm_out=(0,))[0]


def _sum_parts(p_ref):
    g = p_ref[0].astype(F32)
    for s in range(1, p_ref.shape[0]):
        g = g + p_ref[s].astype(F32)
    return g


def _pair_add(g, staged, *, name):
    _, R, W = g.shape
    nq = staged.shape[0]
    tr = _row_tile(R)

    def body(g_ref, s_ref, o_ref):
        mine = jnp.where(lax.axis_index("c") == 0, g_ref[0, 0].astype(F32), g_ref[0, 1].astype(F32))
        o_ref[0] = (mine + s_ref[0].astype(F32)).astype(o_ref.dtype)

    return _call(
        body, (g.reshape(nq, 2, R, W), staged), name=name, grid=(nq, R // tr),
        in_specs=[pl.BlockSpec((1, 2, tr, W), lambda q, i: (q, 0, i, 0)),
                  pl.BlockSpec((1, tr, W), lambda q, i: (q, i, 0))],
        out_specs=[pl.BlockSpec((1, tr, W), lambda q, i: (q, i, 0))],
        out_shape=[SDS((nq, R, W), g.dtype)],
        params=_params(2, [((4, tr, W), g.dtype)], temp_bytes=3 * _nbytes((tr, W), F32)), hbm_out=(0,))[0]


def _adamw_update(w, g, m, v):
    m = ADAM_B1 * m + (1.0 - ADAM_B1) * g
    v = ADAM_B2 * v + (1.0 - ADAM_B2) * (g * g)
    m_hat = m / (1.0 - ADAM_B1 ** ADAM_STEP)
    v_hat = v / (1.0 - ADAM_B2 ** ADAM_STEP)
    delta = -ADAM_LR * (m_hat / (jnp.sqrt(v_hat) + ADAM_EPS) + ADAM_WD * w)
    return delta, m, v


def _row_tile(R):
    return _pick(R, (256, 128, 112, 88, 64, 32, 16, 8))


def _sum8(parts, *, name):
    n, R, W = parts.shape
    tr = _row_tile(R)

    def body(p_ref, o_ref):
        o_ref[...] = _sum_parts(p_ref)

    return _call(
        body, (parts,), name=name, grid=(R // tr,),
        in_specs=[pl.BlockSpec((n, tr, W), lambda i: (0, i, 0))],
        out_specs=[pl.BlockSpec((tr, W), lambda i: (i, 0))],
        out_shape=[SDS((R, W), F32)],
        params=_params(1, [((n, tr, W), parts.dtype), ((tr, W), F32)]))[0]


def _adamw(g, w, m, v, *, name):
    R, W = w.shape
    tr = _row_tile(R)

    def body(g_ref, w_ref, m_ref, v_ref, d_ref, mo_ref, vo_ref):
        d_ref[...], mo_ref[...], vo_ref[...] = _adamw_update(w_ref[...], g_ref[...], m_ref[...], v_ref[...])

    spec = pl.BlockSpec((tr, W), lambda i: (i, 0))
    return _call(
        body, (g, w, m, v), name=name, grid=(R // tr,),
        in_specs=[spec] * 4, out_specs=[spec] * 3, out_shape=[SDS((R, W), F32)] * 3,
        params=_params(1, [((tr, W), F32)] * 7))


def _sum8_adamw(parts, w, m, v, *, name):
    R, W = w.shape
    n = parts.shape[0]
    tr = _row_tile(R)

    def body(p_ref, w_ref, m_ref, v_ref, g_ref, d_ref, mo_ref, vo_ref):
        g = _sum_parts(p_ref)
        g_ref[...] = g
        d_ref[...], mo_ref[...], vo_ref[...] = _adamw_update(w_ref[...], g, m_ref[...], v_ref[...])

    spec = pl.BlockSpec((tr, W), lambda i: (i, 0))
    return _call(
        body, (parts, w, m, v), name=name, grid=(R // tr,),
        in_specs=[pl.BlockSpec((n, tr, W), lambda i: (0, i, 0))] + [spec] * 3,
        out_specs=[spec] * 4, out_shape=[SDS((R, W), F32)] * 4,
        params=_params(1, [((n, tr, W), parts.dtype)] + [((tr, W), F32)] * 7))


def _ada_fwd(c_all, w, bias, *, name):
    NB, D = c_all.shape
    N = w.shape[1]

    def body(c_ref, w_ref, b_ref, o_ref):
        cv = c_ref[...]
        ca = (cv * _sigmoid(cv)).astype(BF16)
        o_ref[...] = _dot(ca, w_ref[...].astype(BF16)) + b_ref[...]

    full = lambda s: pl.BlockSpec(s, lambda i: (0,) * len(s))
    return _call(
        body, (c_all, w, bias), name=name, grid=(1,),
        in_specs=[full((NB, D)), full((D, N)), full((1, N))], out_specs=[full((NB, N))],
        out_shape=[SDS((NB, N), F32)],
        params=_params(1, [((D, N), F32)], temp_bytes=_nbytes((D, N), BF16)))[0]


def _ada_bwd(c_all, gmod_all, *, n_col, name):
    NB, D = c_all.shape
    N = gmod_all.shape[1]

    def body(c_ref, g_ref, gw_ref, gb_ref):
        cv = c_ref[...]
        ca = (cv * _sigmoid(cv)).astype(BF16)
        first = pl.multiple_of(_lin(_my_pos()) * n_col, 128)
        gw_ref[...] = _dot_tn(ca, g_ref[:, pl.ds(first, n_col)].astype(BF16))
        gb_ref[...] = _rowsum(g_ref[...])

    full = lambda s: pl.BlockSpec(s, lambda i: (0,) * len(s))
    return _call(
        body, (c_all, gmod_all), name=name, grid=(1,),
        in_specs=[full((NB, D)), full((NB, N))], out_specs=[full((D, n_col)), full((1, N))],
        out_shape=[SDS((D, n_col), F32), SDS((1, N), F32)],
        params=_params(1, [((D, n_col), F32), ((NB, N), F32)]))


def kernel(x, c, w_ada, b_ada, norm_ffn1_g, ffn1_w_gate, ffn1_w_up, ffn1_w_down, norm_mix_g, w_in, attn_sinks, w_attn_o, conv_w_dw, conv_b_dw, conv_ln_g, conv_ln_b, w_conv_o, w_out, norm_ffn2_g, ffn2_w_gate, ffn2_w_up, ffn2_w_down, final_norm_g, loss_target, m_w_ada, m_b_ada, m_norm_ffn1_g, m_ffn1_w_gate, m_ffn1_w_up, m_ffn1_w_down, m_norm_mix_g, m_w_in, m_attn_sinks, m_w_attn_o, m_conv_w_dw, m_conv_b_dw, m_conv_ln_g, m_conv_ln_b, m_w_conv_o, m_w_out, m_norm_ffn2_g, m_ffn2_w_gate, m_ffn2_w_up, m_ffn2_w_down, m_final_norm_g, v_w_ada, v_b_ada, v_norm_ffn1_g, v_ffn1_w_gate, v_ffn1_w_up, v_ffn1_w_down, v_norm_mix_g, v_w_in, v_attn_sinks, v_w_attn_o, v_conv_w_dw, v_conv_b_dw, v_conv_ln_g, v_conv_ln_b, v_w_conv_o, v_w_out, v_norm_ffn2_g, v_ffn2_w_gate, v_ffn2_w_up, v_ffn2_w_down, v_final_norm_g):
    B, S, D = x.shape
    T = B * S
    QW = N_Q_HEADS * HEAD_DIM
    CC = conv_w_dw.shape[2] * N_DEV
    me = _lin(_my_pos())
    xf = x.reshape(T, D)
    tgt = loss_target.reshape(T, D)
    tm = min(512, S)
    kw = dict(seq=S, tm=tm)

    p_k, p_v, p_ca = QW, QW + KV_WIDTH, QW + 2 * KV_WIDTH
    p_cb, p_ga, p_gc = p_ca + CC, p_ca + 2 * CC, p_ca + 2 * CC + D

    def col_t(w):
        return w[0].T.astype(BF16)

    def row_b(w):
        return w[0].astype(BF16)

    def rows(g):
        return g.reshape(-1, g.shape[-1])

    def blocks8(g):
        return g.reshape(N_DEV, g.shape[0] // N_DEV, g.shape[1])

    def gather(*arrs):
        return _Comm([(a, "gather") for a in arrs])

    g_wg1, g_convw, g_c = _exchange(
        [(col_t(ffn1_w_gate), "gather"), (conv_w_dw[0], "gather"), (c, "gather")], name="gather_first")
    wg1 = rows(g_wg1)
    conv_w = g_convw.transpose(1, 0, 2).reshape(CONV_WIDTH, CC)
    c_all = g_c.reshape(N_DEV * B, D)

    n_col = N_MOD * D // N_DEV
    b_cols = lax.dynamic_slice(b_ada, (0, me * n_col), (1, n_col))
    mod_cols = _ada_fwd(c_all, w_ada[0], b_cols, name="ada_fwd")
    mod_mine = _exchange([(mod_cols.reshape(N_DEV, B, n_col), "scatter")], name="scatter_mod")[0]
    mod = mod_mine.transpose(1, 0, 2).reshape(B * N_MOD, 1, D)
    sh1, sc1, g1, sh2, sc2, g2, sh3, sc3, g3 = [_ModVec(mod, i) for i in range(N_MOD)]

    F = wg1.shape[0]
    tn_f = _pick(F, (1408, 1024, 512, 256))
    tn_in = _pick(w_in.shape[2] * N_DEV, (1792, 768, 512, 256))
    gate_blk = dict(ga_col=p_ga, gc_col=p_gc)
    att_blk = dict(q_blk=0, k_blk=p_k // KV_WIDTH, v_blk=p_v // KV_WIDTH)
    conv_kw = dict(seq=S, cw=256, a_col=p_ca, b_col=p_cb)

    cm = gather(col_t(ffn1_w_up))
    h1, (a1,) = _norm_mod_matmul(xf, norm_ffn1_g, sh1, sc1, [wg1], tn=tn_f, name="ffn1_gate", comm=cm, **kw)
    wu1 = rows(cm.out[0])
    cm = gather(row_b(ffn1_w_down))
    b1 = _matmul_nt(h1, wu1, tm=tm, tn=tn_f, name="ffn1_up", comm=cm)
    wd1 = rows(cm.out[0])
    cm = gather(col_t(w_in))
    x1, y1 = _ffn_down(a1, b1, wd1, xf, g1, name="ffn1_down", comm=cm, **kw)
    winp = rows(cm.out[0])
    cm = gather(row_b(w_attn_o), row_b(w_conv_o), row_b(w_out), col_t(ffn2_w_gate))
    h2, (projp,) = _norm_mod_matmul(x1, norm_mix_g, sh2, sc2, [winp], tn=tn_in, name="mix_in", comm=cm, **kw)
    wao, wco, wout, wg2 = [rows(o) for o in cm.out]
    cm = gather(col_t(ffn2_w_up))
    ao = _attn_fwd(projp, attn_sinks, seq=S, name="attn_fwd", comm=cm, **att_blk)
    wu2 = rows(cm.out[0])
    cm = gather(row_b(ffn2_w_down))
    yc = _conv_fwd(projp, conv_w, conv_b_dw, name="conv_fwd", comm=cm, **conv_kw)
    wd2 = rows(cm.out[0])
    x2, z, ya, ycv, cact, merged = _mix_out(ao, yc, projp, wao, wco, wout, x1, g2, conv_ln_g, conv_ln_b,
                                            name="mix_out", **gate_blk, **kw)
    h3, (a3, b3) = _norm_mod_matmul(x2, norm_ffn2_g, sh3, sc3, [wg2, wu2], tn=tn_f, name="ffn2_up", **kw)
    x3, y3 = _ffn_down(a3, b3, wd2, x2, g3, name="ffn2_down", **kw)
    dx3, loss_row, dgf = _final_loss(x3, final_norm_g[None], tgt, tm=tm, name="final_loss")

    parts = {}

    def pair(*gs):
        return [(blocks8(g), "pair") for g in gs]

    def cross(*rs):
        return [(r, "cross") for r in rs]

    def reduce_pairs(gs, staged, names):
        return [_pair_add(blocks8(g), s, name="pair_add_" + n) for g, s, n in zip(gs, staged, names)]

    dyb3, da3, db3, act3, dg3 = _ffn_bwd_down(dx3, g3, y3, wd2, a3, b3, tn=tn_f, name="ffn2_bwd_down", **kw)
    gwd2 = _matmul_tn(act3, dyb3, name="gw_ffn2_down")
    cm = _Comm(pair(gwd2))
    dx2, dsh3, dsc3, dgn3 = _matmul_norm_mod_bwd([[da3], [db3]], [wg2, wu2], x2, norm_ffn2_g, sc3, dx3,
                                                 name="ffn2_bwd_up", comm=cm, **kw)
    r_wd2, = reduce_pairs([gwd2], cm.out, ["ffn2_w_down"])
    cm = _Comm(cross(r_wd2))
    gwg2 = _matmul_tn(da3, h3, name="gw_ffn2_gate", comm=cm)
    parts["ffn2_w_down"], = cm.out
    cm = _Comm(pair(gwg2))
    gwu2 = _matmul_tn(db3, h3, name="gw_ffn2_up", comm=cm)
    r_wg2, = reduce_pairs([gwg2], cm.out, ["ffn2_w_gate"])

    cm = _Comm(cross(r_wg2) + pair(gwu2))
    dzb, dyab, dycb, dga, dgc, dao, dyc, dg2, dlng, dlnb = _mix_out_bwd(
        dx2, g2, z, wout, projp, ya, ycv, wao, wco, yc, conv_ln_g, conv_ln_b, name="mix_out_bwd", comm=cm,
        **gate_blk, **kw)
    parts["ffn2_w_gate"] = cm.out[0]
    r_wu2, = reduce_pairs([gwu2], cm.out[1:], ["ffn2_w_up"])
    gwout = _matmul_tn(merged, dzb, name="gw_out")
    gwao = _matmul_tn(ao, dyab, name="gw_attn_o")
    gwco = _matmul_tn(cact, dycb, name="gw_conv_o")
    cm = _Comm(cross(r_wu2) + pair(gwout, gwao, gwco))
    dq, dk, dv, dsinks = _attn_bwd(projp, dao, attn_sinks, seq=S, name="attn_bwd", comm=cm, **att_blk)
    parts["ffn2_w_up"] = cm.out[0]
    r_mix = reduce_pairs([gwout, gwao, gwco], cm.out[1:], ["w_out", "w_attn_o", "w_conv_o"])
    cm = _Comm(cross(*r_mix))
    dca, dcb, dconvw, dconvb = _conv_bwd(dyc, projp, conv_w, name="conv_bwd", comm=cm, **conv_kw)
    parts["w_out"], parts["w_attn_o"], parts["w_conv_o"] = cm.out
    gwin = _matmul_tn_pieces([[dq], [dk, dv], [dca], [dcb], [dga], [dgc]], h2, name="gw_in")
    cm = _Comm(pair(gwin))
    dx1, dsh2, dsc2, dgn2 = _matmul_norm_mod_bwd([[dq, dk, dv, dca, dcb, dga, dgc]], [winp], x1, norm_mix_g, sc2, dx2,
                                                 name="mix_in_bwd", comm=cm, **kw)
    r_win, = reduce_pairs([gwin], cm.out, ["w_in"])

    cm = _Comm(cross(r_win))
    dyb1, da1, db1, act1, dg1 = _ffn_bwd_down(dx1, g1, y1, wd1, a1, b1, tn=tn_f, name="ffn1_bwd_down", comm=cm,
                                              **kw)
    parts["w_in"], = cm.out
    gwd1 = _matmul_tn(act1, dyb1, name="gw_ffn1_down")
    cm = _Comm(pair(gwd1))
    gwg1 = _matmul_tn(da1, h1, name="gw_ffn1_gate", comm=cm)
    r_wd1, = reduce_pairs([gwd1], cm.out, ["ffn1_w_down"])
    cm = _Comm(cross(r_wd1) + pair(gwg1))
    gwu1 = _matmul_tn(db1, h1, name="gw_ffn1_up", comm=cm)
    parts["ffn1_w_down"] = cm.out[0]
    r_wg1, = reduce_pairs([gwg1], cm.out[1:], ["ffn1_w_gate"])
    r_wu1, = reduce_pairs([gwu1], _exchange(pair(gwu1), name="pair_last"), ["ffn1_w_up"])
    cm = _Comm(cross(r_wg1, r_wu1))
    dx0, dsh1, dsc1, dgn1 = _matmul_norm_mod_bwd([[da1], [db1]], [wg1, wu1], xf, norm_ffn1_g, sc1, dx1,
                                                 name="ffn1_bwd_up", comm=cm, **kw)
    parts["ffn1_w_gate"], parts["ffn1_w_up"] = cm.out

    n_small = 8
    gmod = jnp.concatenate([dsh1, dsc1, dg1, dsh2, dsc2, dg2, dsh3, dsc3, dg3], axis=1).reshape(B, N_MOD * D)
    sink_row = jnp.pad(dsinks[:, :N_Q_HEADS], ((0, 0), (0, D - N_Q_HEADS)))
    loss_pad = jnp.pad(loss_row, ((0, 0), (0, D - loss_row.shape[1])))
    small = jnp.concatenate([dgn1, dgn2, dgn3, dgf, dconvb, dlng, dlnb, sink_row, dconvw, loss_pad], axis=0)
    small_all, gmod_all = _exchange([(small, "gather"), (gmod, "gather")], name="exchange_last")
    gsmall = _sum8(small_all, name="sum_small")
    loss = gsmall[n_small + CONV_WIDTH, 0]
    g_w_ada, g_b_ada = _ada_bwd(c_all, gmod_all.reshape(N_DEV * B, N_MOD * D), n_col=n_col, name="ada_bwd")
    g_conv_w = lax.dynamic_slice(gsmall[n_small:n_small + CONV_WIDTH], (0, me * (CC // N_DEV)),
                                 (CONV_WIDTH, CC // N_DEV))

    def col_update(name, w, m, v):
        outs = _sum8_adamw(parts[name], w[0].T, m[0].T, v[0].T, name="adamw_" + name)
        return tuple(o.T for o in outs)

    def row_update(name, w, m, v):
        return tuple(_sum8_adamw(parts[name], w[0], m[0], v[0], name="adamw_" + name))

    upd = {
        "ffn1_w_gate": col_update("ffn1_w_gate", ffn1_w_gate, m_ffn1_w_gate, v_ffn1_w_gate),
        "ffn1_w_up": col_update("ffn1_w_up", ffn1_w_up, m_ffn1_w_up, v_ffn1_w_up),
        "ffn1_w_down": row_update("ffn1_w_down", ffn1_w_down, m_ffn1_w_down, v_ffn1_w_down),
        "w_in": col_update("w_in", w_in, m_w_in, v_w_in),
        "w_attn_o": row_update("w_attn_o", w_attn_o, m_w_attn_o, v_w_attn_o),
        "w_conv_o": row_update("w_conv_o", w_conv_o, m_w_conv_o, v_w_conv_o),
        "w_out": row_update("w_out", w_out, m_w_out, v_w_out),
        "ffn2_w_gate": col_update("ffn2_w_gate", ffn2_w_gate, m_ffn2_w_gate, v_ffn2_w_gate),
        "ffn2_w_up": col_update("ffn2_w_up", ffn2_w_up, m_ffn2_w_up, v_ffn2_w_up),
        "ffn2_w_down": row_update("ffn2_w_down", ffn2_w_down, m_ffn2_w_down, v_ffn2_w_down),
        "w_ada": (g_w_ada,) + tuple(_adamw(g_w_ada, w_ada[0], m_w_ada[0], v_w_ada[0], name="adamw_w_ada")),
        "conv_w_dw": (g_conv_w,) + tuple(_adamw(g_conv_w, conv_w_dw[0], m_conv_w_dw[0], v_conv_w_dw[0],
                                                name="adamw_conv_w_dw")),
    }
    for k in upd:
        upd[k] = tuple(t[None] for t in upd[k])

    def pad_sinks(t):
        return jnp.pad(t, ((0, 0), (0, D - N_Q_HEADS)))

    def pack(f1, mix, f2, fin, cb, lg, lb, sinks, bada):
        return jnp.concatenate([f1, mix, f2, fin[None], cb, lg, lb, pad_sinks(sinks), bada.reshape(N_MOD, D)], axis=0)

    w_s = pack(norm_ffn1_g, norm_mix_g, norm_ffn2_g, final_norm_g, conv_b_dw, conv_ln_g, conv_ln_b, attn_sinks, b_ada)
    m_s = pack(m_norm_ffn1_g, m_norm_mix_g, m_norm_ffn2_g, m_final_norm_g, m_conv_b_dw, m_conv_ln_g, m_conv_ln_b,
               m_attn_sinks, m_b_ada)
    v_s = pack(v_norm_ffn1_g, v_norm_mix_g, v_norm_ffn2_g, v_final_norm_g, v_conv_b_dw, v_conv_ln_g, v_conv_ln_b,
               v_attn_sinks, v_b_ada)
    g_s = jnp.concatenate([gsmall[:n_small], g_b_ada.reshape(N_MOD, D)], axis=0)
    small_out = (g_s,) + tuple(_adamw(g_s, w_s, m_s, v_s, name="adamw_vectors"))

    def unpack(t):
        return {
            "norm_ffn1_g": t[0:1], "norm_mix_g": t[1:2], "norm_ffn2_g": t[2:3], "final_norm_g": t[3],
            "conv_b_dw": t[4:5], "conv_ln_g": t[5:6], "conv_ln_b": t[6:7], "attn_sinks": t[7:8, :N_Q_HEADS],
            "b_ada": t[n_small:n_small + N_MOD].reshape(1, N_MOD * D),
        }

    small_un = [unpack(t) for t in small_out]
    for k in small_un[0]:
        upd[k] = tuple(s[k] for s in small_un)

    order = ["w_ada", "b_ada", "norm_ffn1_g", "ffn1_w_gate", "ffn1_w_up", "ffn1_w_down", "norm_mix_g", "w_in",
             "attn_sinks", "w_attn_o", "conv_w_dw", "conv_b_dw", "conv_ln_g", "conv_ln_b", "w_conv_o", "w_out",
             "norm_ffn2_g", "ffn2_w_gate", "ffn2_w_up", "ffn2_w_down", "final_norm_g"]
    grad_x = dx0.reshape(B, S, D)
    return (loss, grad_x, *[upd[k][0] for k in order], *[upd[k][1] for k in order],
            *[upd[k][2] for k in order], *[upd[k][3] for k in order])
```

```python
import dataclasses

import jax
import jax.numpy as jnp
from jax import lax
from jax.experimental import pallas as pl
from jax.experimental.pallas import tpu as pltpu

F32 = jnp.float32
BF16 = jnp.bfloat16
SDS = jax.ShapeDtypeStruct
MESH = pl.DeviceIdType.MESH

N_DEV = 8
EPS = 1e-6
HEAD_DIM = 64
N_Q_HEADS = 16
N_KV_HEADS = 2
GQA_GROUP = N_Q_HEADS // N_KV_HEADS
KV_WIDTH = N_KV_HEADS * HEAD_DIM
ATT_BLOCK = 128
CONV_WIDTH = 31
CONV_HALO = 32
CONV_ROWS = 64
N_MOD = 9
FFN_RESIDUAL = 0.5
ADAM_LR = 0.001
ADAM_B1 = 0.9
ADAM_B2 = 0.999
ADAM_EPS = 1e-08
ADAM_WD = 0.01
ADAM_STEP = 10
NEG_BIG = -1e30

V7X_VMEM_BYTES = 64 * 2**20
VMEM_CAP = V7X_VMEM_BYTES - 8 * 2**20


def _nbytes(shape, dtype):
    n = 1
    for s in shape:
        n *= s
    return n * jnp.dtype(dtype).itemsize


def _params(n_axes, blocks, temp_bytes=0):
    need = 2 * sum(_nbytes(s, d) for s, d in blocks) + temp_bytes + 4 * 2**20
    return pltpu.CompilerParams(dimension_semantics=("arbitrary",) * n_axes,
                                vmem_limit_bytes=int(min(max(need, 16 * 2**20), VMEM_CAP)))


def _dot_nt(a, b):
    return lax.dot_general(a, b, (((1,), (1,)), ((), ())), preferred_element_type=F32)


def _dot_tn(a, b):
    return lax.dot_general(a, b, (((0,), (0,)), ((), ())), preferred_element_type=F32)


def _dot(a, b):
    return jnp.dot(a, b, preferred_element_type=F32)


def _sigmoid(x):
    return jax.nn.sigmoid(x)


def _rowsum(v):
    return jnp.sum(v, axis=0, keepdims=True)


def _acc(ref, val, first):
    @pl.when(first)
    def _():
        ref[...] = val

    @pl.when(jnp.logical_not(first))
    def _():
        ref[...] = ref[...] + val


def _norm_mod(xf, gn, sh, sc):
    rstd = lax.rsqrt(jnp.mean(xf * xf, axis=-1, keepdims=True) + EPS)
    xhat = xf * rstd
    yn = xhat * gn
    return yn * (1.0 + sc) + sh, xhat, rstd, yn


def _pick(n, cands):
    for c in cands:
        if n % c == 0:
            return c
    return n


def _my_pos():
    return lax.axis_index("x"), lax.axis_index("y"), lax.axis_index("c")


def _peer(pos, k):
    x, y, c = pos
    return ((1 - x) if k & 4 else x, (1 - y) if k & 2 else y, (1 - c) if k & 1 else c)


def _lin(pos):
    return 4 * pos[0] + 2 * pos[1] + pos[2]


class _Comm:
    N_COPY = N_DEV - 1
    N_CHIP = N_DEV // 2

    def __init__(self, items):
        self.arrs = [a for a, _ in items]
        self.modes = [m for _, m in items]
        self.n = len(items)
        self.out = None

    def out_shape(self):
        def shape(a, m):
            return {"gather": (N_DEV,) + a.shape, "scatter": a.shape, "pair": (self.N_CHIP,) + a.shape[1:],
                    "cross": a.shape}[m]
        return [SDS(shape(a, m), a.dtype) for a, m in zip(self.arrs, self.modes)]

    def scratch(self):
        return [pltpu.SemaphoreType.DMA((self.n * self.N_COPY,)), pltpu.SemaphoreType.DMA((self.n * self.N_COPY,)),
                pltpu.SemaphoreType.DMA((self.n,))]

    def collective_id(self):
        modes = set(self.modes)
        if "scatter" in modes:
            return 3
        d2d, ici = bool(modes & {"gather", "pair"}), bool(modes & {"gather", "cross"})
        return {(True, False): 0, (False, True): 1, (True, True): 2}[(d2d, ici)]

    def barrier(self):
        x, y, c = _my_pos()
        peers = {0: [(x, y, 1 - c)],
                 1: [(1 - x, y, c), (x, 1 - y, c), (1 - x, 1 - y, c)],
                 2: [(x, y, 1 - c), (1 - x, y, c), (x, 1 - y, c), (1 - x, 1 - y, c)],
                 3: [_peer((x, y, c), k) for k in range(1, N_DEV)]}[self.collective_id()]
        sem = pltpu.get_barrier_semaphore()
        for p in peers:
            pl.semaphore_signal(sem, inc=1, device_id=p, device_id_type=MESH)
        pl.semaphore_wait(sem, len(peers))

    def _plan(self, mode, me):
        x, y, c = me
        sib = (x, y, 1 - c)
        chips = [(1 - x, y), (x, 1 - y), (1 - x, 1 - y)]

        def chip_lin(ch):
            return 2 * ch[0] + ch[1]

        if mode == "scatter":
            peers = [_peer(me, k + 1) for k in range(self.N_COPY)]
            return [(p, ("in", _lin(p)), _lin(me), _lin(p), None) for p in peers], (_lin(me), _lin(me))
        if mode == "gather":
            same = [(*ch, c) for ch in chips]
            other = [(*ch, 1 - c) for ch in chips]
            copies = [(sib, ("in", None), _lin(me), _lin(sib), None)]
            copies += [(p, ("in", None), _lin(me), _lin(p), None) for p in same]
            copies += [(sib, ("out", _lin(p)), _lin(p), _lin(o), 1 + j) for j, (p, o) in enumerate(zip(same, other))]
            return copies, (None, _lin(me))
        if mode == "pair":
            return [(sib, ("in", 2 * q + 1 - c), q, q, None) for q in range(self.N_CHIP)], None
        if mode == "cross":
            mine = chip_lin((x, y))
            return ([((*ch, c), ("in", chip_lin(ch)), mine, chip_lin(ch), None) for ch in chips], (mine, mine))
        raise ValueError(mode)

    def _copy(self, refs, me, i, k, recv):
        srcs, outs, (send_sems, recv_sems, _) = refs
        peer, (where, slot), send_slot, recv_slot, _ = self._plan(self.modes[i], me)[0][k]
        src = srcs[i] if where == "in" else outs[i]
        src = src if slot is None else src.at[slot]
        sem = i * self.N_COPY + k
        return pltpu.make_async_remote_copy(
            src_ref=src, dst_ref=outs[i].at[recv_slot if recv else send_slot], send_sem=send_sems.at[sem],
            recv_sem=recv_sems.at[sem], device_id=peer, device_id_type=MESH)

    def _local(self, refs, me, i):
        srcs, outs, (_, _, loc_sems) = refs
        local = self._plan(self.modes[i], me)[1]
        if local is None:
            return None
        own = srcs[i] if local[0] is None else srcs[i].at[local[0]]
        return pltpu.make_async_copy(own, outs[i].at[local[1]], loc_sems.at[i])

    def start(self, refs):
        me = _my_pos()
        for i in range(self.n):
            local = self._local(refs, me, i)
            if local is not None:
                local.start()
            for k, cp in enumerate(self._plan(self.modes[i], me)[0]):
                if cp[4] is None:
                    self._copy(refs, me, i, k, False).start()

    def forward(self, refs):
        me = _my_pos()
        for i in range(self.n):
            for k, cp in enumerate(self._plan(self.modes[i], me)[0]):
                if cp[4] is not None:
                    self._copy(refs, me, i, cp[4], True).wait_recv()
                    self._copy(refs, me, i, k, False).start()

    def finish(self, refs):
        me = _my_pos()
        plans = [self._plan(m, me)[0] for m in self.modes]
        for i in range(self.n):
            passed_on = [cp[4] for cp in plans[i] if cp[4] is not None]
            for k in range(len(plans[i])):
                if k not in passed_on:
                    self._copy(refs, me, i, k, True).wait_recv()
                self._copy(refs, me, i, k, False).wait_send()
            local = self._local(refs, me, i)
            if local is not None:
                local.wait()


_ANY = pl.BlockSpec(memory_space=pl.ANY)


def _call(body, args, *, name, grid, in_specs, out_specs, out_shape, params, scratch_shapes=(), comm=None):
    in_specs, out_specs, out_shape = list(in_specs), list(out_specs), list(out_shape)
    scratch_shapes = list(scratch_shapes)
    if comm is None:
        return list(pl.pallas_call(body, name=name, grid=grid, in_specs=in_specs, out_specs=out_specs,
                                   out_shape=out_shape, scratch_shapes=scratch_shapes, compiler_params=params)(*args))
    n_in, n_out, n_scr, nc = len(in_specs), len(out_specs), len(scratch_shapes), comm.n
    n_steps = 1
    for g in grid:
        n_steps *= g

    def hosted(*refs):
        ins, c_in = refs[:n_in], refs[n_in:n_in + nc]
        outs = refs[n_in + nc:n_in + nc + n_out]
        c_out = refs[n_in + nc + n_out:n_in + 2 * nc + n_out]
        scr = refs[n_in + 2 * nc + n_out:n_in + 2 * nc + n_out + n_scr]
        sems = refs[n_in + 2 * nc + n_out + n_scr:]
        step = pl.program_id(0)
        for d in range(1, len(grid)):
            step = step * grid[d] + pl.program_id(d)
        c_refs = (c_in, c_out, sems)

        @pl.when(step == 0)
        def _():
            comm.barrier()
            comm.start(c_refs)

        if n_steps >= 3:
            @pl.when(step == n_steps - 2)
            def _():
                comm.forward(c_refs)

        body(*ins, *outs, *scr)

        @pl.when(step == n_steps - 1)
        def _():
            if n_steps < 3:
                comm.forward(c_refs)
            comm.finish(c_refs)

    res = pl.pallas_call(
        hosted, name=name, grid=grid, in_specs=in_specs + [_ANY] * nc, out_specs=out_specs + [_ANY] * nc,
        out_shape=out_shape + comm.out_shape(), scratch_shapes=scratch_shapes + comm.scratch(),
        compiler_params=dataclasses.replace(params, collective_id=comm.collective_id()))(*args, *comm.arrs)
    comm.out = list(res[n_out:])
    return list(res[:n_out])


def _exchange(items, *, name):
    comm = _Comm(items)

    def body(*refs):
        r = (refs[:comm.n], refs[comm.n:2 * comm.n], refs[2 * comm.n:])
        comm.barrier()
        comm.start(r)
        comm.forward(r)
        comm.finish(r)

    return list(pl.pallas_call(body, name=name, out_shape=comm.out_shape(), in_specs=[_ANY] * comm.n,
                               out_specs=[_ANY] * comm.n, scratch_shapes=comm.scratch(),
                               compiler_params=pltpu.CompilerParams(collective_id=comm.collective_id()))(*comm.arrs))


class _ModVec:
    def __init__(self, arr, idx):
        self.arr, self.idx = arr, idx

    def spec(self, tps, n_axes):
        idx, blk = self.idx, (1, 1, self.arr.shape[2])
        if n_axes == 1:
            return pl.BlockSpec(blk, lambda i: (i // tps * N_MOD + idx, 0, 0))
        return pl.BlockSpec(blk, lambda i, j: (i // tps * N_MOD + idx, 0, 0))


def _norm_mod_matmul(x, gn, sh, sc, wts, *, seq, tm, tn, name, comm=None):
    T, D = x.shape
    N = wts[0].shape[0]
    nw = len(wts)
    tps = seq // tm

    def body(x_ref, gn_ref, sh_ref, sc_ref, *rest):
        w_refs, h_ref, o_refs = rest[:nw], rest[nw], rest[nw + 1:]

        @pl.when(pl.program_id(1) == 0)
        def _():
            h_ref[...] = _norm_mod(x_ref[...], gn_ref[...], sh_ref[0], sc_ref[0])[0].astype(BF16)

        h = h_ref[...]
        for w_ref, o_ref in zip(w_refs, o_refs):
            o_ref[...] = _dot_nt(h, w_ref[...]).astype(o_ref.dtype)

    row = pl.BlockSpec((tm, D), lambda i, j: (i, 0))
    vec = pl.BlockSpec((1, D), lambda i, j: (0, 0))
    per_b = pl.BlockSpec((1, 1, D), lambda i, j: (i // tps, 0, 0))
    wspec = pl.BlockSpec((tn, D), lambda i, j: (j, 0))
    ospec = pl.BlockSpec((tm, tn), lambda i, j: (i, j))
    blocks = [((tm, D), F32), ((tm, D), BF16)] + [((tn, D), BF16), ((tm, tn), BF16)] * nw
    outs = _call(
        body, (x, gn, sh.arr, sc.arr, *wts), name=name, grid=(T // tm, N // tn),
        in_specs=[row, vec, sh.spec(tps, 2), sc.spec(tps, 2)] + [wspec] * nw,
        out_specs=[row] + [ospec] * nw,
        out_shape=[SDS((T, D), BF16)] + [SDS((T, N), BF16)] * nw,
        params=_params(2, blocks, temp_bytes=2 * _nbytes((tm, tn), F32) + 3 * _nbytes((tm, D), F32)), comm=comm)
    return outs[0], outs[1:]


def _matmul_nt(h, w, *, tm, tn, name, comm=None):
    T, D = h.shape
    N = w.shape[0]

    def body(h_ref, w_ref, o_ref):
        o_ref[...] = _dot_nt(h_ref[...], w_ref[...]).astype(o_ref.dtype)

    blocks = [((tm, D), BF16), ((tn, D), BF16), ((tm, tn), BF16)]
    return _call(
        body, (h, w), name=name, grid=(T // tm, N // tn),
        in_specs=[pl.BlockSpec((tm, D), lambda i, j: (i, 0)), pl.BlockSpec((tn, D), lambda i, j: (j, 0))],
        out_specs=[pl.BlockSpec((tm, tn), lambda i, j: (i, j))],
        out_shape=[SDS((T, N), BF16)],
        params=_params(2, blocks, temp_bytes=2 * _nbytes((tm, tn), F32)), comm=comm)[0]


def _ffn_down(a, b, wd, x, g, *, seq, tm, name, comm=None):
    T, F = a.shape
    D = wd.shape[1]
    tps = seq // tm

    def body(a_ref, b_ref, wd_ref, x_ref, g_ref, xo_ref, y_ref):
        af = a_ref[...].astype(F32)
        act = (af * _sigmoid(af) * b_ref[...].astype(F32)).astype(BF16)
        y = _dot(act, wd_ref[...])
        xo_ref[...] = x_ref[...] + (FFN_RESIDUAL * g_ref[0]) * y
        y_ref[...] = y.astype(BF16)

    wide = pl.BlockSpec((tm, F), lambda i: (i, 0))
    row = pl.BlockSpec((tm, D), lambda i: (i, 0))
    per_b = pl.BlockSpec((1, 1, D), lambda i: (i // tps, 0, 0))
    wspec = pl.BlockSpec((F, D), lambda i: (0, 0))
    blocks = [((tm, F), BF16)] * 2 + [((F, D), BF16), ((tm, D), F32), ((tm, D), F32), ((tm, D), BF16)]
    return _call(
        body, (a, b, wd, x, g.arr), name=name, grid=(T // tm,),
        in_specs=[wide, wide, wspec, row, g.spec(tps, 1)], out_specs=[row, row],
        out_shape=[SDS((T, D), F32), SDS((T, D), BF16)],
        params=_params(1, blocks, temp_bytes=3 * _nbytes((tm, F), F32)), comm=comm)


def _final_loss(x, gf, tgt, *, tm, name):
    T, D = x.shape
    nt = T // tm

    def body(x_ref, gf_ref, t_ref, dx_ref, loss_ref, dgf_ref, lacc):
        i = pl.program_id(0)
        xf = x_ref[...]
        gfv = gf_ref[...]
        rstd = lax.rsqrt(jnp.mean(xf * xf, axis=-1, keepdims=True) + EPS)
        xhat = xf * rstd
        err = xhat * gfv - t_ref[...]
        dy = err * (1.0 / D)
        dxhat = dy * gfv
        dx_ref[...] = rstd * (dxhat - xhat * jnp.mean(dxhat * xhat, axis=-1, keepdims=True))
        _acc(dgf_ref, _rowsum(dy * xhat), i == 0)
        _acc(lacc, _rowsum(err * err), i == 0)

        @pl.when(i == nt - 1)
        def _():
            loss_ref[...] = jnp.broadcast_to((0.5 / D) * jnp.sum(lacc[...]), loss_ref.shape)

    row = pl.BlockSpec((tm, D), lambda i: (i, 0))
    vec = pl.BlockSpec((1, D), lambda i: (0, 0))
    lspec = pl.BlockSpec((1, 128), lambda i: (0, 0))
    blocks = [((tm, D), F32)] * 3
    return _call(
        body, (x, gf, tgt), name=name, grid=(nt,),
        in_specs=[row, vec, row], out_specs=[row, lspec, vec],
        out_shape=[SDS((T, D), F32), SDS((1, 128), F32), SDS((1, D), F32)],
        scratch_shapes=[pltpu.VMEM((1, D), F32)],
        params=_params(1, blocks, temp_bytes=4 * _nbytes((tm, D), F32)))


def _ffn_bwd_down(dxo, g, y, wd, a, b, *, seq, tm, tn, name, comm=None):
    T, F = a.shape
    D = wd.shape[1]
    tps = seq // tm
    nb = T // seq

    def body(dxo_ref, g_ref, y_ref, wd_ref, a_ref, b_ref, dyb_ref, da_ref, db_ref, act_ref, dg_ref):
        i = pl.program_id(0)

        @pl.when(pl.program_id(1) == 0)
        def _():
            dx = dxo_ref[...]
            dyb_ref[...] = ((FFN_RESIDUAL * g_ref[0]) * dx).astype(BF16)
            part = _rowsum(FFN_RESIDUAL * dx * y_ref[...].astype(F32))
            _acc(dg_ref, part[None], i % tps == 0)

        dact = _dot_nt(dyb_ref[...], wd_ref[...])
        af = a_ref[...].astype(F32)
        bf = b_ref[...].astype(F32)
        sg = _sigmoid(af)
        silu = af * sg
        act_ref[...] = (silu * bf).astype(BF16)
        da_ref[...] = (dact * bf * (sg + silu * (1.0 - sg))).astype(BF16)
        db_ref[...] = (dact * silu).astype(BF16)

    row = pl.BlockSpec((tm, D), lambda i, j: (i, 0))
    per_b = pl.BlockSpec((1, 1, D), lambda i, j: (i // tps, 0, 0))
    wspec = pl.BlockSpec((tn, D), lambda i, j: (j, 0))
    chunk = pl.BlockSpec((tm, tn), lambda i, j: (i, j))
    blocks = [((tm, D), F32), ((tm, D), BF16), ((tn, D), BF16), ((tm, D), BF16)] + [((tm, tn), BF16)] * 5
    return _call(
        body, (dxo, g.arr, y, wd, a, b), name=name, grid=(T // tm, F // tn),
        in_specs=[row, g.spec(tps, 2), row, wspec, chunk, chunk],
        out_specs=[row, chunk, chunk, chunk, per_b],
        out_shape=[SDS((T, D), BF16)] + [SDS((T, F), BF16)] * 3 + [SDS((nb, 1, D), F32)],
        params=_params(2, blocks, temp_bytes=6 * _nbytes((tm, tn), F32)), comm=comm)


def _matmul_norm_mod_bwd(ds, ws, x, gn, sc, dxo, *, seq, tm, name, comm=None):
    T, D = x.shape
    nk = len(ws)
    sizes = [len(g) for g in ds]
    ds = [d for g in ds for d in g]
    tps = seq // tm
    nb = T // seq

    def body(*refs):
        w_refs = refs[len(ds):len(ds) + nk]
        x_ref, gn_ref, sc_ref, dxo_ref, dxi_ref, dsh_ref, dsc_ref, dgn_ref = refs[len(ds) + nk:]
        i = pl.program_id(0)
        dh, at = None, 0
        for n, w_ref in zip(sizes, w_refs):
            pieces = [r[...] for r in refs[at:at + n]]
            at += n
            part = _dot(pieces[0] if n == 1 else jnp.concatenate(pieces, axis=1), w_ref[...])
            dh = part if dh is None else dh + part
        gnv = gn_ref[...]
        scv = sc_ref[0]
        _, xhat, rstd, yn = _norm_mod(x_ref[...], gnv, 0.0, scv)
        dyn = dh * (1.0 + scv)
        dxhat = dyn * gnv
        dxi_ref[...] = dxo_ref[...] + rstd * (dxhat - xhat * jnp.mean(dxhat * xhat, axis=-1, keepdims=True))
        first_of_seq = i % tps == 0
        _acc(dsh_ref, _rowsum(dh)[None], first_of_seq)
        _acc(dsc_ref, _rowsum(dh * yn)[None], first_of_seq)
        _acc(dgn_ref, _rowsum(dyn * xhat), i == 0)

    row = pl.BlockSpec((tm, D), lambda i: (i, 0))
    vec = pl.BlockSpec((1, D), lambda i: (0, 0))
    per_b = pl.BlockSpec((1, 1, D), lambda i: (i // tps, 0, 0))
    d_specs = [pl.BlockSpec((tm, d.shape[1]), lambda i: (i, 0)) for d in ds]
    w_specs = [pl.BlockSpec(w.shape, lambda i: (0, 0)) for w in ws]
    blocks = ([((tm, d.shape[1]), BF16) for d in ds] + [(w.shape, BF16) for w in ws] + [((tm, D), F32)] * 3)
    return _call(
        body, (*ds, *ws, x, gn, sc.arr, dxo), name=name, grid=(T // tm,),
        in_specs=d_specs + w_specs + [row, vec, sc.spec(tps, 1), row],
        out_specs=[row, per_b, per_b, vec],
        out_shape=[SDS((T, D), F32), SDS((nb, 1, D), F32), SDS((nb, 1, D), F32), SDS((1, D), F32)],
        params=_params(1, blocks, temp_bytes=6 * _nbytes((tm, D), F32)), comm=comm)


def _layernorm_silu(yc, lg, lb):
    mu = jnp.mean(yc, axis=-1, keepdims=True)
    cen = yc - mu
    rstd = lax.rsqrt(jnp.mean(cen * cen, axis=-1, keepdims=True) + EPS)
    xh = cen * rstd
    l = xh * lg + lb
    s = _sigmoid(l)
    return l * s, xh, rstd, l, s


GATE_W = 256


def _gate_specs(tm, D, col):
    return [pl.BlockSpec((tm, GATE_W), lambda i, blk=col // GATE_W + t: (i, blk)) for t in range(D // GATE_W)]


def _gate(refs):
    return jnp.concatenate([r[...] for r in refs], axis=1).astype(F32)


def _mix_out(ao, yc, proj, wao, wco, wout, x1, g2, lg, lb, *, seq, tm, ga_col, gc_col, name, comm=None):
    T, D = x1.shape
    tps = seq // tm
    ng = D // GATE_W

    def body(ao_ref, yc_ref, *rest):
        ga_refs, gc_refs = rest[:ng], rest[ng:2 * ng]
        (wao_ref, wco_ref, wout_ref, x1_ref, g2_ref, lg_ref, lb_ref,
         x2_ref, z_ref, ya_ref, ycv_ref, cact_ref, mrg_ref) = rest[2 * ng:]
        ya = _dot(ao_ref[...], wao_ref[...])
        cact = _layernorm_silu(yc_ref[...], lg_ref[...], lb_ref[...])[0].astype(BF16)
        ycv = _dot(cact, wco_ref[...])
        merged = (_sigmoid(_gate(ga_refs)) * ya + _sigmoid(_gate(gc_refs)) * ycv).astype(BF16)
        z = _dot(merged, wout_ref[...])
        x2_ref[...] = x1_ref[...] + g2_ref[0] * z
        z_ref[...] = z.astype(BF16)
        ya_ref[...] = ya.astype(BF16)
        ycv_ref[...] = ycv.astype(BF16)
        cact_ref[...] = cact
        mrg_ref[...] = merged

    row = pl.BlockSpec((tm, D), lambda i: (i, 0))
    vec = pl.BlockSpec((1, D), lambda i: (0, 0))
    per_b = pl.BlockSpec((1, 1, D), lambda i: (i // tps, 0, 0))
    wspec = pl.BlockSpec((D, D), lambda i: (0, 0))
    gates = _gate_specs(tm, D, ga_col) + _gate_specs(tm, D, gc_col)
    blocks = ([((tm, D), BF16), ((tm, D), F32), ((tm, D), BF16), ((tm, D), BF16)] + [((D, D), BF16)] * 3
              + [((tm, D), F32)] * 2 + [((tm, D), BF16)] * 5)
    return _call(
        body, (ao, yc, *[proj] * (2 * ng), wao, wco, wout, x1, g2.arr, lg, lb), name=name, grid=(T // tm,),
        in_specs=[row, row, *gates, wspec, wspec, wspec, row, g2.spec(tps, 1), vec, vec],
        out_specs=[row] * 6,
        out_shape=[SDS((T, D), F32)] + [SDS((T, D), BF16)] * 5,
        params=_params(1, blocks, temp_bytes=8 * _nbytes((tm, D), F32)), comm=comm)


def _mix_out_bwd(dx2, g2, z, wout, proj, ya, ycv, wao, wco, yc, lg, lb, *, seq, tm, ga_col, gc_col, name,
                 comm=None):
    T, D = dx2.shape
    tps = seq // tm
    nb = T // seq
    ng = D // GATE_W

    def body(dx2_ref, g2_ref, z_ref, wout_ref, *rest):
        ga_refs, gc_refs = rest[:ng], rest[ng:2 * ng]
        (ya_ref, ycv_ref, wao_ref, wco_ref, yc_ref, lg_ref, lb_ref, dz_ref, dya_ref, dycv_ref, dga_ref, dgc_ref,
         dao_ref, dyc_ref, dg2_ref, dlg_ref, dlb_ref) = rest[2 * ng:]
        i = pl.program_id(0)
        dx = dx2_ref[...]
        _acc(dg2_ref, _rowsum(dx * z_ref[...].astype(F32))[None], i % tps == 0)
        dzb = (g2_ref[0] * dx).astype(BF16)
        dz_ref[...] = dzb
        dmerged = _dot_nt(dzb, wout_ref[...])
        sa = _sigmoid(_gate(ga_refs))
        sc_ = _sigmoid(_gate(gc_refs))
        dya = (dmerged * sa).astype(BF16)
        dycv = (dmerged * sc_).astype(BF16)
        dya_ref[...] = dya
        dycv_ref[...] = dycv
        dga_ref[...] = (dmerged * ya_ref[...].astype(F32) * (sa * (1.0 - sa))).astype(BF16)
        dgc_ref[...] = (dmerged * ycv_ref[...].astype(F32) * (sc_ * (1.0 - sc_))).astype(BF16)
        dao_ref[...] = _dot_nt(dya, wao_ref[...]).astype(BF16)
        dcact = _dot_nt(dycv, wco_ref[...])
        lgv = lg_ref[...]
        _, xh, rstd, l, s = _layernorm_silu(yc_ref[...], lgv, lb_ref[...])
        dl = dcact * (s * (1.0 + l * (1.0 - s)))
        _acc(dlb_ref, _rowsum(dl), i == 0)
        _acc(dlg_ref, _rowsum(dl * xh), i == 0)
        dxh = dl * lgv
        dyc_ref[...] = rstd * (dxh - jnp.mean(dxh, axis=-1, keepdims=True)
                               - xh * jnp.mean(dxh * xh, axis=-1, keepdims=True))

    row = pl.BlockSpec((tm, D), lambda i: (i, 0))
    vec = pl.BlockSpec((1, D), lambda i: (0, 0))
    per_b = pl.BlockSpec((1, 1, D), lambda i: (i // tps, 0, 0))
    wspec = pl.BlockSpec((D, D), lambda i: (0, 0))
    gates = _gate_specs(tm, D, ga_col) + _gate_specs(tm, D, gc_col)
    blocks = ([((tm, D), F32)] * 3 + [((tm, D), BF16)] * 11 + [((D, D), BF16)] * 3)
    return _call(
        body, (dx2, g2.arr, z, wout, *[proj] * (2 * ng), ya, ycv, wao, wco, yc, lg, lb), name=name,
        grid=(T // tm,),
        in_specs=[row, g2.spec(tps, 1), row, wspec, *gates, row, row, wspec, wspec, row, vec, vec],
        out_specs=[row] * 7 + [per_b, vec, vec],
        out_shape=[SDS((T, D), BF16)] * 6 + [SDS((T, D), F32), SDS((nb, 1, D), F32), SDS((1, D), F32),
                                             SDS((1, D), F32)],
        params=_params(1, blocks, temp_bytes=10 * _nbytes((tm, D), F32)), comm=comm)


GROUP_ROWS = GQA_GROUP * ATT_BLOCK
PAIR_W = 2 * HEAD_DIM
GROUP_W = GQA_GROUP * HEAD_DIM


def _lane_lo():
    return lax.broadcasted_iota(jnp.int32, (1, PAIR_W), 1) < HEAD_DIM


def _band_bias():
    sj = lax.broadcasted_iota(jnp.int32, (2 * ATT_BLOCK, GROUP_ROWS), 0)
    qi = lax.broadcasted_iota(jnp.int32, (2 * ATT_BLOCK, GROUP_ROWS), 1) & (ATT_BLOCK - 1)
    rel = qi + ATT_BLOCK - sj
    bias = jnp.where(jnp.logical_and(rel >= 0, rel < ATT_BLOCK), 0.0, NEG_BIG)
    sj1 = lax.broadcasted_iota(jnp.int32, (2 * ATT_BLOCK, 1), 0)
    return bias, jnp.where(sj1 < ATT_BLOCK, NEG_BIG, 0.0)


def _dup_heads(src_ref, dst, seq):
    x = src_ref[...]
    i = lax.broadcasted_iota(jnp.int32, (KV_WIDTH, PAIR_W), 0)
    j = lax.broadcasted_iota(jnp.int32, (KV_WIDTH, PAIR_W), 1) & (HEAD_DIM - 1)
    for g in range(N_KV_HEADS):
        sel = jnp.where(i == j + g * HEAD_DIM, 1.0, 0.0).astype(BF16)
        dst[g, pl.ds(0, ATT_BLOCK), :] = jnp.zeros((ATT_BLOCK, PAIR_W), BF16)
        dst[g, pl.ds(ATT_BLOCK, seq), :] = _dot(x, sel).astype(BF16)


def _stack_heads(blk, g, lo):
    parts = []
    for p in range(GQA_GROUP // 2):
        pair = blk[:, g * GROUP_W + p * PAIR_W:g * GROUP_W + (p + 1) * PAIR_W]
        parts += [jnp.where(lo, pair, jnp.zeros_like(pair)), jnp.where(lo, jnp.zeros_like(pair), pair)]
    return jnp.concatenate(parts, axis=0)


def _unstack_heads(full, ref, r0, g, lo):
    for p in range(GQA_GROUP // 2):
        even = full[(2 * p) * ATT_BLOCK:(2 * p + 1) * ATT_BLOCK, :]
        odd = full[(2 * p + 1) * ATT_BLOCK:(2 * p + 2) * ATT_BLOCK, :]
        ref[pl.ds(r0, ATT_BLOCK), g * GROUP_W + p * PAIR_W:g * GROUP_W + (p + 1) * PAIR_W] = (
            jnp.where(lo, even, odd).astype(ref.dtype))


def _sink_row(sink_ref, g):
    return jnp.concatenate([jnp.full((1, ATT_BLOCK), sink_ref[0, g * GQA_GROUP + h], F32)
                            for h in range(GQA_GROUP)], axis=1)


def _group_probs(qs, k2, bias, sink):
    s = _dot_nt(k2, qs) * (HEAD_DIM ** -0.5) + bias
    m = jnp.maximum(jnp.max(s, axis=0, keepdims=True), sink)
    p = jnp.exp(s - m)
    psink = jnp.exp(sink - m)
    inv = 1.0 / (jnp.sum(p, axis=0, keepdims=True) + psink)
    return p * inv, psink * inv


def _attn_fwd(projp, sinks, *, seq, q_blk, k_blk, v_blk, name, comm=None):
    T = projp.shape[0]
    QW = N_Q_HEADS * HEAD_DIM
    nblk = seq // ATT_BLOCK

    def body(q_ref, k_ref, v_ref, sink_ref, o_ref, k2s, v2s):
        _dup_heads(k_ref, k2s, seq)
        _dup_heads(v_ref, v2s, seq)
        lo = _lane_lo()
        bias0, first_pen = _band_bias()
        sink_rows = [_sink_row(sink_ref, g) for g in range(N_KV_HEADS)]

        def blk(n, carry):
            r0 = pl.multiple_of(n * ATT_BLOCK, ATT_BLOCK)
            qb = q_ref[pl.ds(r0, ATT_BLOCK), :]
            bias = bias0 + jnp.where(n == 0, 1.0, 0.0) * first_pen
            for g in range(N_KV_HEADS):
                probs_t, _ = _group_probs(_stack_heads(qb, g, lo), k2s[g, pl.ds(r0, 2 * ATT_BLOCK), :], bias,
                                          sink_rows[g])
                _unstack_heads(_dot_tn(probs_t.astype(BF16), v2s[g, pl.ds(r0, 2 * ATT_BLOCK), :]), o_ref, r0, g, lo)
            return carry

        lax.fori_loop(0, nblk, blk, 0)

    blocks = [((seq, QW), BF16)] * 2 + [((seq, KV_WIDTH), BF16)] * 2
    return _call(
        body, (projp, projp, projp, sinks), name=name, grid=(T // seq,),
        in_specs=[pl.BlockSpec((seq, QW), lambda b: (b, q_blk)),
                  pl.BlockSpec((seq, KV_WIDTH), lambda b: (b, k_blk)),
                  pl.BlockSpec((seq, KV_WIDTH), lambda b: (b, v_blk)),
                  pl.BlockSpec(memory_space=pltpu.SMEM)],
        out_specs=[pl.BlockSpec((seq, QW), lambda b: (b, 0))],
        out_shape=[SDS((T, QW), BF16)],
        scratch_shapes=[pltpu.VMEM((N_KV_HEADS, seq + ATT_BLOCK, PAIR_W), BF16)] * 2,
        params=_params(1, blocks, temp_bytes=16 * 2**20), comm=comm)[0]


def _attn_bwd(projp, dao, sinks, *, seq, q_blk, k_blk, v_blk, name, comm=None):
    T = projp.shape[0]
    QW = N_Q_HEADS * HEAD_DIM
    nblk = seq // ATT_BLOCK

    def body(q_ref, k_ref, v_ref, do_ref, sink_ref, dq_ref, dk_ref, dv_ref, dsink_ref, k2s, v2s, dkacc, dvacc):
        _dup_heads(k_ref, k2s, seq)
        _dup_heads(v_ref, v2s, seq)
        dkacc[...] = jnp.zeros(dkacc.shape, F32)
        dvacc[...] = jnp.zeros(dvacc.shape, F32)
        lane = lax.broadcasted_iota(jnp.int32, (1, PAIR_W), 1)
        lo = lane < HEAD_DIM
        bias0, first_pen = _band_bias()
        sink_rows = [_sink_row(sink_ref, g) for g in range(N_KV_HEADS)]

        def blk(n, dsink):
            r0 = pl.multiple_of(n * ATT_BLOCK, ATT_BLOCK)
            band = pl.ds(r0, 2 * ATT_BLOCK)
            qb = q_ref[pl.ds(r0, ATT_BLOCK), :]
            dob = do_ref[pl.ds(r0, ATT_BLOCK), :]
            bias = bias0 + jnp.where(n == 0, 1.0, 0.0) * first_pen
            for g in range(N_KV_HEADS):
                qs = _stack_heads(qb, g, lo)
                dos = _stack_heads(dob, g, lo)
                k2 = k2s[g, band, :]
                v2 = v2s[g, band, :]
                probs_t, psink = _group_probs(qs, k2, bias, sink_rows[g])
                dp_t = _dot_nt(v2, dos)
                delta = jnp.sum(probs_t * dp_t, axis=0, keepdims=True)
                ds_t = (probs_t * (dp_t - delta) * (HEAD_DIM ** -0.5)).astype(BF16)
                tsink = psink * delta
                for h in range(GQA_GROUP):
                    dsink = dsink + jnp.where(lane == g * GQA_GROUP + h,
                                              -jnp.sum(tsink[:, h * ATT_BLOCK:(h + 1) * ATT_BLOCK]), 0.0)
                _unstack_heads(_dot_tn(ds_t, k2), dq_ref, r0, g, lo)
                dkacc[g, band, :] = dkacc[g, band, :] + _dot(ds_t, qs)
                dvacc[g, band, :] = dvacc[g, band, :] + _dot(probs_t.astype(BF16), dos)
            return dsink

        dsink = lax.fori_loop(0, nblk, blk, jnp.zeros((1, PAIR_W), F32))
        _acc(dsink_ref, dsink, pl.program_id(0) == 0)

        def fold(acc, g):
            a = acc[g, pl.ds(ATT_BLOCK, seq), :]
            return a + pltpu.roll(a, HEAD_DIM, 1)

        dk_ref[...] = jnp.where(lo, fold(dkacc, 0), fold(dkacc, 1)).astype(BF16)
        dv_ref[...] = jnp.where(lo, fold(dvacc, 0), fold(dvacc, 1)).astype(BF16)

    blocks = [((seq, QW), BF16)] * 3 + [((seq, KV_WIDTH), BF16)] * 4
    kv_spec_out = pl.BlockSpec((seq, KV_WIDTH), lambda b: (b, 0))
    return _call(
        body, (projp, projp, projp, dao, sinks), name=name, grid=(T // seq,),
        in_specs=[pl.BlockSpec((seq, QW), lambda b: (b, q_blk)),
                  pl.BlockSpec((seq, KV_WIDTH), lambda b: (b, k_blk)),
                  pl.BlockSpec((seq, KV_WIDTH), lambda b: (b, v_blk)),
                  pl.BlockSpec((seq, QW), lambda b: (b, 0)),
                  pl.BlockSpec(memory_space=pltpu.SMEM)],
        out_specs=[pl.BlockSpec((seq, QW), lambda b: (b, 0)), kv_spec_out, kv_spec_out,
                   pl.BlockSpec((1, 128), lambda b: (0, 0))],
        out_shape=[SDS((T, QW), BF16), SDS((T, KV_WIDTH), BF16), SDS((T, KV_WIDTH), BF16), SDS((1, 128), F32)],
        scratch_shapes=[pltpu.VMEM((N_KV_HEADS, seq + ATT_BLOCK, PAIR_W), BF16)] * 2
        + [pltpu.VMEM((N_KV_HEADS, seq + ATT_BLOCK, PAIR_W), F32)] * 2,
        params=_params(1, blocks, temp_bytes=24 * 2**20), comm=comm)


SUBLANES = 8


def _sublane_shifts(win):
    n = CONV_ROWS + CONV_HALO
    return [win] + [pltpu.roll(win, n - b, 0) for b in range(1, SUBLANES)]


def _window(shifted, off):
    a = off // SUBLANES * SUBLANES
    return shifted[off % SUBLANES][a:a + CONV_ROWS, :]


def _conv_fwd(projp, w, bias, *, seq, cw, a_col, b_col, name, comm=None):
    T = projp.shape[0]
    C = w.shape[1]
    nchunk = seq // CONV_ROWS

    def body(a_ref, b_ref, w_ref, bias_ref, y_ref, upad):
        upad[pl.ds(0, CONV_HALO), :] = jnp.zeros((CONV_HALO, cw), F32)
        upad[pl.ds(CONV_HALO, seq), :] = a_ref[...].astype(F32) * _sigmoid(b_ref[...].astype(F32))
        wv = w_ref[...]
        bv = bias_ref[...]

        def chunk(r, carry):
            r0 = pl.multiple_of(r * CONV_ROWS, CONV_ROWS)
            shifted = _sublane_shifts(upad[pl.ds(r0, CONV_ROWS + CONV_HALO), :])
            acc = jnp.broadcast_to(bv, (CONV_ROWS, cw))
            for k in range(CONV_WIDTH):
                acc = acc + wv[k:k + 1, :] * _window(shifted, CONV_HALO - (CONV_WIDTH - 1) + k)
            y_ref[pl.ds(r0, CONV_ROWS), :] = acc
            return carry

        lax.fori_loop(0, nchunk, chunk, 0)

    blocks = [((seq, cw), BF16)] * 2 + [((seq, cw), F32)]
    return _call(
        body, (projp, projp, w, bias), name=name, grid=(T // seq, C // cw),
        in_specs=[pl.BlockSpec((seq, cw), lambda b, c: (b, a_col // cw + c)),
                  pl.BlockSpec((seq, cw), lambda b, c: (b, b_col // cw + c)),
                  pl.BlockSpec((CONV_WIDTH, cw), lambda b, c: (0, c)),
                  pl.BlockSpec((1, cw), lambda b, c: (0, c))],
        out_specs=[pl.BlockSpec((seq, cw), lambda b, c: (b, c))],
        out_shape=[SDS((T, C), F32)],
        scratch_shapes=[pltpu.VMEM((seq + CONV_HALO, cw), F32)],
        params=_params(2, blocks, temp_bytes=6 * _nbytes((seq, cw), F32)), comm=comm)[0]


def _conv_bwd(dy, projp, w, *, seq, cw, a_col, b_col, name, comm=None):
    T = projp.shape[0]
    C = w.shape[1]
    nchunk = seq // CONV_ROWS
    SUB = 8

    def body(dy_ref, a_ref, b_ref, w_ref, da_ref, db_ref, dw_ref, dbias_ref, dypad, dwp):
        first = pl.program_id(1) == 0
        dyv = dy_ref[...]
        dypad[pl.ds(0, seq), :] = dyv
        dypad[pl.ds(seq, CONV_HALO), :] = jnp.zeros((CONV_HALO, cw), F32)
        dwp[...] = jnp.zeros(dwp.shape, F32)
        wv = w_ref[...]

        def chunk(r, carry):
            r0 = pl.multiple_of(r * CONV_ROWS, CONV_ROWS)
            dy_shifts = _sublane_shifts(dypad[pl.ds(r0, CONV_ROWS + CONV_HALO), :])
            ac = a_ref[pl.ds(r0, CONV_ROWS), :].astype(F32)
            sbc = _sigmoid(b_ref[pl.ds(r0, CONV_ROWS), :].astype(F32))
            uc = ac * sbc
            du = jnp.zeros((CONV_ROWS, cw), F32)
            for k in range(CONV_WIDTH):
                dyk = _window(dy_shifts, CONV_WIDTH - 1 - k)
                du = du + wv[k:k + 1, :] * dyk
                prod = uc * dyk
                part = prod[0:SUB, :]
                for s in range(1, CONV_ROWS // SUB):
                    part = part + prod[s * SUB:(s + 1) * SUB, :]
                dwp[pl.ds(k * SUB, SUB), :] = dwp[pl.ds(k * SUB, SUB), :] + part
            da_ref[pl.ds(r0, CONV_ROWS), :] = (du * sbc).astype(BF16)
            db_ref[pl.ds(r0, CONV_ROWS), :] = (du * ac * (sbc * (1.0 - sbc))).astype(BF16)
            return carry

        lax.fori_loop(0, nchunk, chunk, 0)

        @pl.when(first)
        def _():
            dw_ref[...] = jnp.zeros(dw_ref.shape, F32)
            dbias_ref[...] = jnp.zeros(dbias_ref.shape, F32)

        for k in range(CONV_WIDTH):
            dw_ref[k:k + 1, :] = dw_ref[k:k + 1, :] + _rowsum(dwp[pl.ds(k * SUB, SUB), :])
        dbias_ref[...] = dbias_ref[...] + _rowsum(dyv)

    blocks = [((seq, cw), F32)] + [((seq, cw), BF16)] * 4
    return _call(
        body, (dy, projp, projp, w), name=name, grid=(C // cw, T // seq),
        in_specs=[pl.BlockSpec((seq, cw), lambda c, b: (b, c)),
                  pl.BlockSpec((seq, cw), lambda c, b: (b, a_col // cw + c)),
                  pl.BlockSpec((seq, cw), lambda c, b: (b, b_col // cw + c)),
                  pl.BlockSpec((CONV_WIDTH, cw), lambda c, b: (0, c))],
        out_specs=[pl.BlockSpec((seq, cw), lambda c, b: (b, c)), pl.BlockSpec((seq, cw), lambda c, b: (b, c)),
                   pl.BlockSpec((CONV_WIDTH, cw), lambda c, b: (0, c)), pl.BlockSpec((1, cw), lambda c, b: (0, c))],
        out_shape=[SDS((T, C), BF16), SDS((T, C), BF16), SDS((CONV_WIDTH, C), F32), SDS((1, C), F32)],
        scratch_shapes=[pltpu.VMEM((seq + CONV_HALO, cw), F32), pltpu.VMEM((CONV_WIDTH * SUB, cw), F32)],
        params=_params(2, blocks, temp_bytes=8 * _nbytes((seq, cw), F32)), comm=comm)


def _matmul_tn(a, b, *, name, comm=None):
    T, M = a.shape
    N = b.shape[1]
    bm = _pick(M, (768, 512, 256))

    def body(a_ref, b_ref, o_ref):
        o_ref[...] = _dot_tn(a_ref[...], b_ref[...]).astype(BF16)

    blocks = [((T, bm), BF16), ((T, N), BF16), ((bm, N), BF16)]
    return _call(
        body, (a, b), name=name, grid=(M // bm,),
        in_specs=[pl.BlockSpec((T, bm), lambda i: (0, i)), pl.BlockSpec((T, N), lambda i: (0, 0))],
        out_specs=[pl.BlockSpec((bm, N), lambda i: (i, 0))],
        out_shape=[SDS((M, N), BF16)],
        params=_params(1, blocks, temp_bytes=2 * _nbytes((T, bm), BF16) + 2 * _nbytes((bm, N), F32)),
        comm=comm)[0]


TN_BLOCK = 256


def _matmul_tn_pieces(groups, b, *, name, comm=None):
    T, N = b.shape
    flat = [a for g in groups for a in g]
    starts, n_steps = [], 0
    for g in groups:
        width = sum(a.shape[1] for a in g)
        assert width % TN_BLOCK == 0 and (len(g) == 1 or width == TN_BLOCK), [a.shape for a in g]
        starts.append(n_steps)
        n_steps += width // TN_BLOCK

    def body(*refs):
        a_refs, b_ref, o_ref = refs[:len(flat)], refs[len(flat)], refs[len(flat) + 1]
        i = pl.program_id(0)
        at = 0
        for g, start in zip(groups, starts):
            mine = a_refs[at:at + len(g)]
            at += len(g)
            steps = sum(a.shape[1] for a in g) // TN_BLOCK

            @pl.when(jnp.logical_and(i >= start, i < start + steps))
            def _(mine=mine):
                a = mine[0][...] if len(mine) == 1 else jnp.concatenate([r[...] for r in mine], axis=1)
                o_ref[...] = _dot_tn(a, b_ref[...]).astype(BF16)

    a_specs = []
    for g, start in zip(groups, starts):
        for a in g:
            if len(g) == 1:
                last = a.shape[1] // TN_BLOCK - 1
                a_specs.append(pl.BlockSpec(
                    (T, TN_BLOCK), lambda i, start=start, last=last: (0, jnp.clip(i - start, 0, last))))
            else:
                a_specs.append(pl.BlockSpec((T, a.shape[1]), lambda i: (0, 0)))
    blocks = [((T, TN_BLOCK), BF16)] * len(flat) + [((T, N), BF16), ((TN_BLOCK, N), BF16)]
    return _call(
        body, (*flat, b), name=name, grid=(n_steps,),
        in_specs=a_specs + [pl.BlockSpec((T, N), lambda i: (0, 0))],
        out_specs=[pl.BlockSpec((TN_BLOCK, N), lambda i: (i, 0))],
        out_shape=[SDS((n_steps * TN_BLOCK, N), BF16)],
        params=_params(1, blocks, temp_bytes=2 * _nbytes((T, TN_BLOCK), BF16) + 2 * _nbytes((TN_BLOCK, N), F32)),
        comm=comm)[0]


def _sum_parts(p_ref):
    g = p_ref[0].astype(F32)
    for s in range(1, p_ref.shape[0]):
        g = g + p_ref[s].astype(F32)
    return g


def _pair_add(g, staged, *, name):
    _, R, W = g.shape
    nq = staged.shape[0]
    tr = _row_tile(R)

    def body(g_ref, s_ref, o_ref):
        mine = jnp.where(lax.axis_index("c") == 0, g_ref[0, 0].astype(F32), g_ref[0, 1].astype(F32))
        o_ref[0] = (mine + s_ref[0].astype(F32)).astype(o_ref.dtype)

    return _call(
        body, (g.reshape(nq, 2, R, W), staged), name=name, grid=(nq, R // tr),
        in_specs=[pl.BlockSpec((1, 2, tr, W), lambda q, i: (q, 0, i, 0)),
                  pl.BlockSpec((1, tr, W), lambda q, i: (q, i, 0))],
        out_specs=[pl.BlockSpec((1, tr, W), lambda q, i: (q, i, 0))],
        out_shape=[SDS((nq, R, W), g.dtype)],
        params=_params(2, [((4, tr, W), g.dtype)], temp_bytes=3 * _nbytes((tr, W), F32)))[0]


def _adamw_update(w, g, m, v):
    m = ADAM_B1 * m + (1.0 - ADAM_B1) * g
    v = ADAM_B2 * v + (1.0 - ADAM_B2) * (g * g)
    m_hat = m / (1.0 - ADAM_B1 ** ADAM_STEP)
    v_hat = v / (1.0 - ADAM_B2 ** ADAM_STEP)
    delta = -ADAM_LR * (m_hat / (jnp.sqrt(v_hat) + ADAM_EPS) + ADAM_WD * w)
    return delta, m, v


def _row_tile(R):
    return _pick(R, (256, 128, 112, 88, 64, 32, 16, 8))


def _sum8(parts, *, name):
    n, R, W = parts.shape
    tr = _row_tile(R)

    def body(p_ref, o_ref):
        o_ref[...] = _sum_parts(p_ref)

    return _call(
        body, (parts,), name=name, grid=(R // tr,),
        in_specs=[pl.BlockSpec((n, tr, W), lambda i: (0, i, 0))],
        out_specs=[pl.BlockSpec((tr, W), lambda i: (i, 0))],
        out_shape=[SDS((R, W), F32)],
        params=_params(1, [((n, tr, W), parts.dtype), ((tr, W), F32)]))[0]


def _adamw(g, w, m, v, *, name):
    R, W = w.shape
    tr = _row_tile(R)

    def body(g_ref, w_ref, m_ref, v_ref, d_ref, mo_ref, vo_ref):
        d_ref[...], mo_ref[...], vo_ref[...] = _adamw_update(w_ref[...], g_ref[...], m_ref[...], v_ref[...])

    spec = pl.BlockSpec((tr, W), lambda i: (i, 0))
    return _call(
        body, (g, w, m, v), name=name, grid=(R // tr,),
        in_specs=[spec] * 4, out_specs=[spec] * 3, out_shape=[SDS((R, W), F32)] * 3,
        params=_params(1, [((tr, W), F32)] * 7))


def _sum8_adamw(parts, w, m, v, *, name):
    R, W = w.shape
    n = parts.shape[0]
    tr = _row_tile(R)

    def body(p_ref, w_ref, m_ref, v_ref, g_ref, d_ref, mo_ref, vo_ref):
        g = _sum_parts(p_ref)
        g_ref[...] = g
        d_ref[...], mo_ref[...], vo_ref[...] = _adamw_update(w_ref[...], g, m_ref[...], v_ref[...])

    spec = pl.BlockSpec((tr, W), lambda i: (i, 0))
    return _call(
        body, (parts, w, m, v), name=name, grid=(R // tr,),
        in_specs=[pl.BlockSpec((n, tr, W), lambda i: (0, i, 0))] + [spec] * 3,
        out_specs=[spec] * 4, out_shape=[SDS((R, W), F32)] * 4,
        params=_params(1, [((n, tr, W), parts.dtype)] + [((tr, W), F32)] * 7))


def _ada_fwd(c_all, w, bias, *, name):
    NB, D = c_all.shape
    N = w.shape[1]

    def body(c_ref, w_ref, b_ref, o_ref):
        cv = c_ref[...]
        ca = (cv * _sigmoid(cv)).astype(BF16)
        o_ref[...] = _dot(ca, w_ref[...].astype(BF16)) + b_ref[...]

    full = lambda s: pl.BlockSpec(s, lambda i: (0,) * len(s))
    return _call(
        body, (c_all, w, bias), name=name, grid=(1,),
        in_specs=[full((NB, D)), full((D, N)), full((1, N))], out_specs=[full((NB, N))],
        out_shape=[SDS((NB, N), F32)],
        params=_params(1, [((D, N), F32)], temp_bytes=_nbytes((D, N), BF16)))[0]


def _ada_bwd(c_all, gmod_all, *, n_col, name):
    NB, D = c_all.shape
    N = gmod_all.shape[1]

    def body(c_ref, g_ref, gw_ref, gb_ref):
        cv = c_ref[...]
        ca = (cv * _sigmoid(cv)).astype(BF16)
        first = pl.multiple_of(_lin(_my_pos()) * n_col, 128)
        gw_ref[...] = _dot_tn(ca, g_ref[:, pl.ds(first, n_col)].astype(BF16))
        gb_ref[...] = _rowsum(g_ref[...])

    full = lambda s: pl.BlockSpec(s, lambda i: (0,) * len(s))
    return _call(
        body, (c_all, gmod_all), name=name, grid=(1,),
        in_specs=[full((NB, D)), full((NB, N))], out_specs=[full((D, n_col)), full((1, N))],
        out_shape=[SDS((D, n_col), F32), SDS((1, N), F32)],
        params=_params(1, [((D, n_col), F32), ((NB, N), F32)]))


def kernel(x, c, w_ada, b_ada, norm_ffn1_g, ffn1_w_gate, ffn1_w_up, ffn1_w_down, norm_mix_g, w_in, attn_sinks, w_attn_o, conv_w_dw, conv_b_dw, conv_ln_g, conv_ln_b, w_conv_o, w_out, norm_ffn2_g, ffn2_w_gate, ffn2_w_up, ffn2_w_down, final_norm_g, loss_target, m_w_ada, m_b_ada, m_norm_ffn1_g, m_ffn1_w_gate, m_ffn1_w_up, m_ffn1_w_down, m_norm_mix_g, m_w_in, m_attn_sinks, m_w_attn_o, m_conv_w_dw, m_conv_b_dw, m_conv_ln_g, m_conv_ln_b, m_w_conv_o, m_w_out, m_norm_ffn2_g, m_ffn2_w_gate, m_ffn2_w_up, m_ffn2_w_down, m_final_norm_g, v_w_ada, v_b_ada, v_norm_ffn1_g, v_ffn1_w_gate, v_ffn1_w_up, v_ffn1_w_down, v_norm_mix_g, v_w_in, v_attn_sinks, v_w_attn_o, v_conv_w_dw, v_conv_b_dw, v_conv_ln_g, v_conv_ln_b, v_w_conv_o, v_w_out, v_norm_ffn2_g, v_ffn2_w_gate, v_ffn2_w_up, v_ffn2_w_down, v_final_norm_g):
    B, S, D = x.shape
    T = B * S
    QW = N_Q_HEADS * HEAD_DIM
    CC = conv_w_dw.shape[2] * N_DEV
    me = _lin(_my_pos())
    xf = x.reshape(T, D)
    tgt = loss_target.reshape(T, D)
    tm = min(512, S)
    kw = dict(seq=S, tm=tm)

    p_k, p_v, p_ca = QW, QW + KV_WIDTH, QW + 2 * KV_WIDTH
    p_cb, p_ga, p_gc = p_ca + CC, p_ca + 2 * CC, p_ca + 2 * CC + D

    def col_t(w):
        return w[0].T.astype(BF16)

    def row_b(w):
        return w[0].astype(BF16)

    def rows(g):
        return g.reshape(-1, g.shape[-1])

    def blocks8(g):
        return g.reshape(N_DEV, g.shape[0] // N_DEV, g.shape[1])

    def gather(*arrs):
        return _Comm([(a, "gather") for a in arrs])

    g_wg1, g_convw, g_c = _exchange(
        [(col_t(ffn1_w_gate), "gather"), (conv_w_dw[0], "gather"), (c, "gather")], name="gather_first")
    wg1 = rows(g_wg1)
    conv_w = g_convw.transpose(1, 0, 2).reshape(CONV_WIDTH, CC)
    c_all = g_c.reshape(N_DEV * B, D)

    n_col = N_MOD * D // N_DEV
    b_cols = lax.dynamic_slice(b_ada, (0, me * n_col), (1, n_col))
    mod_cols = _ada_fwd(c_all, w_ada[0], b_cols, name="ada_fwd")
    mod_mine = _exchange([(mod_cols.reshape(N_DEV, B, n_col), "scatter")], name="scatter_mod")[0]
    mod = mod_mine.transpose(1, 0, 2).reshape(B * N_MOD, 1, D)
    sh1, sc1, g1, sh2, sc2, g2, sh3, sc3, g3 = [_ModVec(mod, i) for i in range(N_MOD)]

    F = wg1.shape[0]
    tn_f = _pick(F, (1408, 1024, 512, 256))
    tn_in = _pick(w_in.shape[2] * N_DEV, (1792, 768, 512, 256))
    gate_blk = dict(ga_col=p_ga, gc_col=p_gc)
    att_blk = dict(q_blk=0, k_blk=p_k // KV_WIDTH, v_blk=p_v // KV_WIDTH)
    conv_kw = dict(seq=S, cw=256, a_col=p_ca, b_col=p_cb)

    cm = gather(col_t(ffn1_w_up))
    h1, (a1,) = _norm_mod_matmul(xf, norm_ffn1_g, sh1, sc1, [wg1], tn=tn_f, name="ffn1_gate", comm=cm, **kw)
    wu1 = rows(cm.out[0])
    cm = gather(row_b(ffn1_w_down))
    b1 = _matmul_nt(h1, wu1, tm=tm, tn=tn_f, name="ffn1_up", comm=cm)
    wd1 = rows(cm.out[0])
    cm = gather(col_t(w_in))
    x1, y1 = _ffn_down(a1, b1, wd1, xf, g1, name="ffn1_down", comm=cm, **kw)
    winp = rows(cm.out[0])
    cm = gather(row_b(w_attn_o), row_b(w_conv_o), row_b(w_out), col_t(ffn2_w_gate))
    h2, (projp,) = _norm_mod_matmul(x1, norm_mix_g, sh2, sc2, [winp], tn=tn_in, name="mix_in", comm=cm, **kw)
    wao, wco, wout, wg2 = [rows(o) for o in cm.out]
    cm = gather(col_t(ffn2_w_up))
    ao = _attn_fwd(projp, attn_sinks, seq=S, name="attn_fwd", comm=cm, **att_blk)
    wu2 = rows(cm.out[0])
    cm = gather(row_b(ffn2_w_down))
    yc = _conv_fwd(projp, conv_w, conv_b_dw, name="conv_fwd", comm=cm, **conv_kw)
    wd2 = rows(cm.out[0])
    x2, z, ya, ycv, cact, merged = _mix_out(ao, yc, projp, wao, wco, wout, x1, g2, conv_ln_g, conv_ln_b,
                                            name="mix_out", **gate_blk, **kw)
    h3, (a3, b3) = _norm_mod_matmul(x2, norm_ffn2_g, sh3, sc3, [wg2, wu2], tn=tn_f, name="ffn2_up", **kw)
    x3, y3 = _ffn_down(a3, b3, wd2, x2, g3, name="ffn2_down", **kw)
    dx3, loss_row, dgf = _final_loss(x3, final_norm_g[None], tgt, tm=tm, name="final_loss")

    parts = {}

    def pair(*gs):
        return [(blocks8(g), "pair") for g in gs]

    def cross(*rs):
        return [(r, "cross") for r in rs]

    def reduce_pairs(gs, staged, names):
        return [_pair_add(blocks8(g), s, name="pair_add_" + n) for g, s, n in zip(gs, staged, names)]

    dyb3, da3, db3, act3, dg3 = _ffn_bwd_down(dx3, g3, y3, wd2, a3, b3, tn=tn_f, name="ffn2_bwd_down", **kw)
    gwd2 = _matmul_tn(act3, dyb3, name="gw_ffn2_down")
    cm = _Comm(pair(gwd2))
    dx2, dsh3, dsc3, dgn3 = _matmul_norm_mod_bwd([[da3], [db3]], [wg2, wu2], x2, norm_ffn2_g, sc3, dx3,
                                                 name="ffn2_bwd_up", comm=cm, **kw)
    r_wd2, = reduce_pairs([gwd2], cm.out, ["ffn2_w_down"])
    cm = _Comm(cross(r_wd2))
    gwg2 = _matmul_tn(da3, h3, name="gw_ffn2_gate", comm=cm)
    parts["ffn2_w_down"], = cm.out
    cm = _Comm(pair(gwg2))
    gwu2 = _matmul_tn(db3, h3, name="gw_ffn2_up", comm=cm)
    r_wg2, = reduce_pairs([gwg2], cm.out, ["ffn2_w_gate"])

    cm = _Comm(cross(r_wg2) + pair(gwu2))
    dzb, dyab, dycb, dga, dgc, dao, dyc, dg2, dlng, dlnb = _mix_out_bwd(
        dx2, g2, z, wout, projp, ya, ycv, wao, wco, yc, conv_ln_g, conv_ln_b, name="mix_out_bwd", comm=cm,
        **gate_blk, **kw)
    parts["ffn2_w_gate"] = cm.out[0]
    r_wu2, = reduce_pairs([gwu2], cm.out[1:], ["ffn2_w_up"])
    gwout = _matmul_tn(merged, dzb, name="gw_out")
    gwao = _matmul_tn(ao, dyab, name="gw_attn_o")
    gwco = _matmul_tn(cact, dycb, name="gw_conv_o")
    cm = _Comm(cross(r_wu2) + pair(gwout, gwao, gwco))
    dq, dk, dv, dsinks = _attn_bwd(projp, dao, attn_sinks, seq=S, name="attn_bwd", comm=cm, **att_blk)
    parts["ffn2_w_up"] = cm.out[0]
    r_mix = reduce_pairs([gwout, gwao, gwco], cm.out[1:], ["w_out", "w_attn_o", "w_conv_o"])
    cm = _Comm(cross(*r_mix))
    dca, dcb, dconvw, dconvb = _conv_bwd(dyc, projp, conv_w, name="conv_bwd", comm=cm, **conv_kw)
    parts["w_out"], parts["w_attn_o"], parts["w_conv_o"] = cm.out
    gwin = _matmul_tn_pieces([[dq], [dk, dv], [dca], [dcb], [dga], [dgc]], h2, name="gw_in")
    cm = _Comm(pair(gwin))
    dx1, dsh2, dsc2, dgn2 = _matmul_norm_mod_bwd([[dq, dk, dv, dca, dcb, dga, dgc]], [winp], x1, norm_mix_g, sc2, dx2,
                                                 name="mix_in_bwd", comm=cm, **kw)
    r_win, = reduce_pairs([gwin], cm.out, ["w_in"])

    cm = _Comm(cross(r_win))
    dyb1, da1, db1, act1, dg1 = _ffn_bwd_down(dx1, g1, y1, wd1, a1, b1, tn=tn_f, name="ffn1_bwd_down", comm=cm,
                                              **kw)
    parts["w_in"], = cm.out
    gwd1 = _matmul_tn(act1, dyb1, name="gw_ffn1_down")
    cm = _Comm(pair(gwd1))
    gwg1 = _matmul_tn(da1, h1, name="gw_ffn1_gate", comm=cm)
    r_wd1, = reduce_pairs([gwd1], cm.out, ["ffn1_w_down"])
    cm = _Comm(cross(r_wd1) + pair(gwg1))
    gwu1 = _matmul_tn(db1, h1, name="gw_ffn1_up", comm=cm)
    parts["ffn1_w_down"] = cm.out[0]
    r_wg1, = reduce_pairs([gwg1], cm.out[1:], ["ffn1_w_gate"])
    r_wu1, = reduce_pairs([gwu1], _exchange(pair(gwu1), name="pair_last"), ["ffn1_w_up"])
    cm = _Comm(cross(r_wg1, r_wu1))
    dx0, dsh1, dsc1, dgn1 = _matmul_norm_mod_bwd([[da1], [db1]], [wg1, wu1], xf, norm_ffn1_g, sc1, dx1,
                                                 name="ffn1_bwd_up", comm=cm, **kw)
    parts["ffn1_w_gate"], parts["ffn1_w_up"] = cm.out

    n_small = 8
    gmod = jnp.concatenate([dsh1, dsc1, dg1, dsh2, dsc2, dg2, dsh3, dsc3, dg3], axis=1).reshape(B, N_MOD * D)
    sink_row = jnp.pad(dsinks[:, :N_Q_HEADS], ((0, 0), (0, D - N_Q_HEADS)))
    loss_pad = jnp.pad(loss_row, ((0, 0), (0, D - loss_row.shape[1])))
    small = jnp.concatenate([dgn1, dgn2, dgn3, dgf, dconvb, dlng, dlnb, sink_row, dconvw, loss_pad], axis=0)
    small_all, gmod_all = _exchange([(small, "gather"), (gmod, "gather")], name="exchange_last")
    gsmall = _sum8(small_all, name="sum_small")
    loss = gsmall[n_small + CONV_WIDTH, 0]
    g_w_ada, g_b_ada = _ada_bwd(c_all, gmod_all.reshape(N_DEV * B, N_MOD * D), n_col=n_col, name="ada_bwd")
    g_conv_w = lax.dynamic_slice(gsmall[n_small:n_small + CONV_WIDTH], (0, me * (CC // N_DEV)),
                                 (CONV_WIDTH, CC // N_DEV))

    def col_update(name, w, m, v):
        outs = _sum8_adamw(parts[name], w[0].T, m[0].T, v[0].T, name="adamw_" + name)
        return tuple(o.T for o in outs)

    def row_update(name, w, m, v):
        return tuple(_sum8_adamw(parts[name], w[0], m[0], v[0], name="adamw_" + name))

    upd = {
        "ffn1_w_gate": col_update("ffn1_w_gate", ffn1_w_gate, m_ffn1_w_gate, v_ffn1_w_gate),
        "ffn1_w_up": col_update("ffn1_w_up", ffn1_w_up, m_ffn1_w_up, v_ffn1_w_up),
        "ffn1_w_down": row_update("ffn1_w_down", ffn1_w_down, m_ffn1_w_down, v_ffn1_w_down),
        "w_in": col_update("w_in", w_in, m_w_in, v_w_in),
        "w_attn_o": row_update("w_attn_o", w_attn_o, m_w_attn_o, v_w_attn_o),
        "w_conv_o": row_update("w_conv_o", w_conv_o, m_w_conv_o, v_w_conv_o),
        "w_out": row_update("w_out", w_out, m_w_out, v_w_out),
        "ffn2_w_gate": col_update("ffn2_w_gate", ffn2_w_gate, m_ffn2_w_gate, v_ffn2_w_gate),
        "ffn2_w_up": col_update("ffn2_w_up", ffn2_w_up, m_ffn2_w_up, v_ffn2_w_up),
        "ffn2_w_down": row_update("ffn2_w_down", ffn2_w_down, m_ffn2_w_down, v_ffn2_w_down),
        "w_ada": (g_w_ada,) + tuple(_adamw(g_w_ada, w_ada[0], m_w_ada[0], v_w_ada[0], name="adamw_w_ada")),
        "conv_w_dw": (g_conv_w,) + tuple(_adamw(g_conv_w, conv_w_dw[0], m_conv_w_dw[0], v_conv_w_dw[0],
                                                name="adamw_conv_w_dw")),
    }
    for k in upd:
        upd[k] = tuple(t[None] for t in upd[k])

    def pad_sinks(t):
        return jnp.pad(t, ((0, 0), (0, D - N_Q_HEADS)))

    def pack(f1, mix, f2, fin, cb, lg, lb, sinks, bada):
        return jnp.concatenate([f1, mix, f2, fin[None], cb, lg, lb, pad_sinks(sinks), bada.reshape(N_MOD, D)], axis=0)

    w_s = pack(norm_ffn1_g, norm_mix_g, norm_ffn2_g, final_norm_g, conv_b_dw, conv_ln_g, conv_ln_b, attn_sinks, b_ada)
    m_s = pack(m_norm_ffn1_g, m_norm_mix_g, m_norm_ffn2_g, m_final_norm_g, m_conv_b_dw, m_conv_ln_g, m_conv_ln_b,
               m_attn_sinks, m_b_ada)
    v_s = pack(v_norm_ffn1_g, v_norm_mix_g, v_norm_ffn2_g, v_final_norm_g, v_conv_b_dw, v_conv_ln_g, v_conv_ln_b,
               v_attn_sinks, v_b_ada)
    g_s = jnp.concatenate([gsmall[:n_small], g_b_ada.reshape(N_MOD, D)], axis=0)
    small_out = (g_s,) + tuple(_adamw(g_s, w_s, m_s, v_s, name="adamw_vectors"))

    def unpack(t):
        return {
            "norm_ffn1_g": t[0:1], "norm_mix_g": t[1:2], "norm_ffn2_g": t[2:3], "final_norm_g": t[3],
            "conv_b_dw": t[4:5], "conv_ln_g": t[5:6], "conv_ln_b": t[6:7], "attn_sinks": t[7:8, :N_Q_HEADS],
            "b_ada": t[n_small:n_small + N_MOD].reshape(1, N_MOD * D),
        }

    small_un = [unpack(t) for t in small_out]
    for k in small_un[0]:
        upd[k] = tuple(s[k] for s in small_un)

    order = ["w_ada", "b_ada", "norm_ffn1_g", "ffn1_w_gate", "ffn1_w_up", "ffn1_w_down", "norm_mix_g", "w_in",
             "attn_sinks", "w_attn_o", "conv_w_dw", "conv_b_dw", "conv_ln_g", "conv_ln_b", "w_conv_o", "w_out",
             "norm_ffn2_g", "ffn2_w_gate", "ffn2_w_up", "ffn2_w_down", "final_norm_g"]
    grad_x = dx0.reshape(B, S, D)
    return (loss, grad_x, *[upd[k][0] for k in order], *[upd[k][1] for k in order],
            *[upd[k][2] for k in order], *[upd[k][3] for k in order])
```

```python
import dataclasses

import jax
import jax.numpy as jnp
from jax import lax
from jax.experimental import pallas as pl
from jax.experimental.pallas import tpu as pltpu

F32 = jnp.float32
BF16 = jnp.bfloat16
SDS = jax.ShapeDtypeStruct
MESH = pl.DeviceIdType.MESH

N_DEV = 8
EPS = 1e-6
HEAD_DIM = 64
N_Q_HEADS = 16
N_KV_HEADS = 2
GQA_GROUP = N_Q_HEADS // N_KV_HEADS
KV_WIDTH = N_KV_HEADS * HEAD_DIM
ATT_BLOCK = 128
CONV_WIDTH = 31
CONV_HALO = 32
CONV_ROWS = 64
N_MOD = 9
FFN_RESIDUAL = 0.5
ADAM_LR = 0.001
ADAM_B1 = 0.9
ADAM_B2 = 0.999
ADAM_EPS = 1e-08
ADAM_WD = 0.01
ADAM_STEP = 10
NEG_BIG = -1e30

V7X_VMEM_BYTES = 64 * 2**20
VMEM_CAP = V7X_VMEM_BYTES - 8 * 2**20


def _nbytes(shape, dtype):
    n = 1
    for s in shape:
        n *= s
    return n * jnp.dtype(dtype).itemsize


def _params(n_axes, blocks, temp_bytes=0):
    need = 2 * sum(_nbytes(s, d) for s, d in blocks) + temp_bytes + 4 * 2**20
    return pltpu.CompilerParams(dimension_semantics=("arbitrary",) * n_axes,
                                vmem_limit_bytes=int(min(max(need, 16 * 2**20), VMEM_CAP)))


def _dot_nt(a, b):
    return lax.dot_general(a, b, (((1,), (1,)), ((), ())), preferred_element_type=F32)


def _dot_tn(a, b):
    return lax.dot_general(a, b, (((0,), (0,)), ((), ())), preferred_element_type=F32)


def _dot(a, b):
    return jnp.dot(a, b, preferred_element_type=F32)


def _sigmoid(x):
    return jax.nn.sigmoid(x)


def _rowsum(v):
    return jnp.sum(v, axis=0, keepdims=True)


def _acc(ref, val, first):
    @pl.when(first)
    def _():
        ref[...] = val

    @pl.when(jnp.logical_not(first))
    def _():
        ref[...] = ref[...] + val


def _norm_mod(xf, gn, sh, sc):
    rstd = lax.rsqrt(jnp.mean(xf * xf, axis=-1, keepdims=True) + EPS)
    xhat = xf * rstd
    yn = xhat * gn
    return yn * (1.0 + sc) + sh, xhat, rstd, yn


def _pick(n, cands):
    for c in cands:
        if n % c == 0:
            return c
    return n


def _my_pos():
    return lax.axis_index("x"), lax.axis_index("y"), lax.axis_index("c")


def _peer(pos, k):
    x, y, c = pos
    return ((1 - x) if k & 4 else x, (1 - y) if k & 2 else y, (1 - c) if k & 1 else c)


def _lin(pos):
    return 4 * pos[0] + 2 * pos[1] + pos[2]


class _Comm:
    N_COPY = N_DEV - 1
    N_CHIP = N_DEV // 2

    def __init__(self, items):
        self.arrs = [a for a, _ in items]
        self.modes = [m for _, m in items]
        self.n = len(items)
        self.out = None

    def out_shape(self):
        def shape(a, m):
            return {"gather": (N_DEV,) + a.shape, "scatter": a.shape, "pair": (self.N_CHIP,) + a.shape[1:],
                    "cross": a.shape}[m]
        return [SDS(shape(a, m), a.dtype) for a, m in zip(self.arrs, self.modes)]

    def scratch(self):
        return [pltpu.SemaphoreType.DMA((self.n * self.N_COPY,)), pltpu.SemaphoreType.DMA((self.n * self.N_COPY,)),
                pltpu.SemaphoreType.DMA((self.n,))]

    def collective_id(self):
        modes = set(self.modes)
        if "scatter" in modes:
            return 3
        d2d, ici = bool(modes & {"gather", "pair"}), bool(modes & {"gather", "cross"})
        return {(True, False): 0, (False, True): 1, (True, True): 2}[(d2d, ici)]

    def barrier(self):
        x, y, c = _my_pos()
        peers = {0: [(x, y, 1 - c)],
                 1: [(1 - x, y, c), (x, 1 - y, c), (1 - x, 1 - y, c)],
                 2: [(x, y, 1 - c), (1 - x, y, c), (x, 1 - y, c), (1 - x, 1 - y, c)],
                 3: [_peer((x, y, c), k) for k in range(1, N_DEV)]}[self.collective_id()]
        sem = pltpu.get_barrier_semaphore()
        for p in peers:
            pl.semaphore_signal(sem, inc=1, device_id=p, device_id_type=MESH)
        pl.semaphore_wait(sem, len(peers))

    def _plan(self, mode, me):
        x, y, c = me
        sib = (x, y, 1 - c)
        chips = [(1 - x, y), (x, 1 - y), (1 - x, 1 - y)]

        def chip_lin(ch):
            return 2 * ch[0] + ch[1]

        if mode == "scatter":
            peers = [_peer(me, k + 1) for k in range(self.N_COPY)]
            return [(p, ("in", _lin(p)), _lin(me), _lin(p), None) for p in peers], (_lin(me), _lin(me))
        if mode == "gather":
            same = [(*ch, c) for ch in chips]
            other = [(*ch, 1 - c) for ch in chips]
            copies = [(sib, ("in", None), _lin(me), _lin(sib), None)]
            copies += [(p, ("in", None), _lin(me), _lin(p), None) for p in same]
            copies += [(sib, ("out", _lin(p)), _lin(p), _lin(o), 1 + j) for j, (p, o) in enumerate(zip(same, other))]
            return copies, (None, _lin(me))
        if mode == "pair":
            return [(sib, ("in", 2 * q + 1 - c), q, q, None) for q in range(self.N_CHIP)], None
        if mode == "cross":
            mine = chip_lin((x, y))
            return ([((*ch, c), ("in", chip_lin(ch)), mine, chip_lin(ch), None) for ch in chips], (mine, mine))
        raise ValueError(mode)

    def _copy(self, refs, me, i, k, recv):
        srcs, outs, (send_sems, recv_sems, _) = refs
        peer, (where, slot), send_slot, recv_slot, _ = self._plan(self.modes[i], me)[0][k]
        src = srcs[i] if where == "in" else outs[i]
        src = src if slot is None else src.at[slot]
        sem = i * self.N_COPY + k
        return pltpu.make_async_remote_copy(
            src_ref=src, dst_ref=outs[i].at[recv_slot if recv else send_slot], send_sem=send_sems.at[sem],
            recv_sem=recv_sems.at[sem], device_id=peer, device_id_type=MESH)

    def _local(self, refs, me, i):
        srcs, outs, (_, _, loc_sems) = refs
        local = self._plan(self.modes[i], me)[1]
        if local is None:
            return None
        own = srcs[i] if local[0] is None else srcs[i].at[local[0]]
        return pltpu.make_async_copy(own, outs[i].at[local[1]], loc_sems.at[i])

    def start(self, refs):
        me = _my_pos()
        for i in range(self.n):
            local = self._local(refs, me, i)
            if local is not None:
                local.start()
            for k, cp in enumerate(self._plan(self.modes[i], me)[0]):
                if cp[4] is None:
                    self._copy(refs, me, i, k, False).start()

    def forward(self, refs):
        me = _my_pos()
        for i in range(self.n):
            for k, cp in enumerate(self._plan(self.modes[i], me)[0]):
                if cp[4] is not None:
                    self._copy(refs, me, i, cp[4], True).wait_recv()
                    self._copy(refs, me, i, k, False).start()

    def finish(self, refs):
        me = _my_pos()
        plans = [self._plan(m, me)[0] for m in self.modes]
        for i in range(self.n):
            passed_on = [cp[4] for cp in plans[i] if cp[4] is not None]
            for k in range(len(plans[i])):
                if k not in passed_on:
                    self._copy(refs, me, i, k, True).wait_recv()
                self._copy(refs, me, i, k, False).wait_send()
            local = self._local(refs, me, i)
            if local is not None:
                local.wait()


_ANY = pl.BlockSpec(memory_space=pl.ANY)


def _call(body, args, *, name, grid, in_specs, out_specs, out_shape, params, scratch_shapes=(), comm=None):
    in_specs, out_specs, out_shape = list(in_specs), list(out_specs), list(out_shape)
    scratch_shapes = list(scratch_shapes)
    if comm is None:
        return list(pl.pallas_call(body, name=name, grid=grid, in_specs=in_specs, out_specs=out_specs,
                                   out_shape=out_shape, scratch_shapes=scratch_shapes, compiler_params=params)(*args))
    n_in, n_out, n_scr, nc = len(in_specs), len(out_specs), len(scratch_shapes), comm.n
    n_steps = 1
    for g in grid:
        n_steps *= g

    def hosted(*refs):
        ins, c_in = refs[:n_in], refs[n_in:n_in + nc]
        outs = refs[n_in + nc:n_in + nc + n_out]
        c_out = refs[n_in + nc + n_out:n_in + 2 * nc + n_out]
        scr = refs[n_in + 2 * nc + n_out:n_in + 2 * nc + n_out + n_scr]
        sems = refs[n_in + 2 * nc + n_out + n_scr:]
        step = pl.program_id(0)
        for d in range(1, len(grid)):
            step = step * grid[d] + pl.program_id(d)
        c_refs = (c_in, c_out, sems)

        @pl.when(step == 0)
        def _():
            comm.barrier()
            comm.start(c_refs)

        if n_steps >= 3:
            @pl.when(step == n_steps - 2)
            def _():
                comm.forward(c_refs)

        body(*ins, *outs, *scr)

        @pl.when(step == n_steps - 1)
        def _():
            if n_steps < 3:
                comm.forward(c_refs)
            comm.finish(c_refs)

    res = pl.pallas_call(
        hosted, name=name, grid=grid, in_specs=in_specs + [_ANY] * nc, out_specs=out_specs + [_ANY] * nc,
        out_shape=out_shape + comm.out_shape(), scratch_shapes=scratch_shapes + comm.scratch(),
        compiler_params=dataclasses.replace(params, collective_id=comm.collective_id()))(*args, *comm.arrs)
    comm.out = list(res[n_out:])
    return list(res[:n_out])


def _exchange(items, *, name):
    comm = _Comm(items)

    def body(*refs):
        r = (refs[:comm.n], refs[comm.n:2 * comm.n], refs[2 * comm.n:])
        comm.barrier()
        comm.start(r)
        comm.forward(r)
        comm.finish(r)

    return list(pl.pallas_call(body, name=name, out_shape=comm.out_shape(), in_specs=[_ANY] * comm.n,
                               out_specs=[_ANY] * comm.n, scratch_shapes=comm.scratch(),
                               compiler_params=pltpu.CompilerParams(collective_id=comm.collective_id()))(*comm.arrs))


class _ModVec:
    def __init__(self, arr, idx):
        self.arr, self.idx = arr, idx

    def spec(self, tps, n_axes):
        idx, blk = self.idx, (1, 1, self.arr.shape[2])
        if n_axes == 1:
            return pl.BlockSpec(blk, lambda i: (i // tps * N_MOD + idx, 0, 0))
        return pl.BlockSpec(blk, lambda i, j: (i // tps * N_MOD + idx, 0, 0))


def _norm_mod_matmul(x, gn, sh, sc, wts, *, seq, tm, tn, name, comm=None):
    T, D = x.shape
    N = wts[0].shape[0]
    nw = len(wts)
    tps = seq // tm

    def body(x_ref, gn_ref, sh_ref, sc_ref, *rest):
        w_refs, h_ref, o_refs = rest[:nw], rest[nw], rest[nw + 1:]

        @pl.when(pl.program_id(1) == 0)
        def _():
            h_ref[...] = _norm_mod(x_ref[...], gn_ref[...], sh_ref[0], sc_ref[0])[0].astype(BF16)

        h = h_ref[...]
        for w_ref, o_ref in zip(w_refs, o_refs):
            o_ref[...] = _dot_nt(h, w_ref[...]).astype(o_ref.dtype)

    row = pl.BlockSpec((tm, D), lambda i, j: (i, 0))
    vec = pl.BlockSpec((1, D), lambda i, j: (0, 0))
    per_b = pl.BlockSpec((1, 1, D), lambda i, j: (i // tps, 0, 0))
    wspec = pl.BlockSpec((tn, D), lambda i, j: (j, 0))
    ospec = pl.BlockSpec((tm, tn), lambda i, j: (i, j))
    blocks = [((tm, D), F32), ((tm, D), BF16)] + [((tn, D), BF16), ((tm, tn), BF16)] * nw
    outs = _call(
        body, (x, gn, sh.arr, sc.arr, *wts), name=name, grid=(T // tm, N // tn),
        in_specs=[row, vec, sh.spec(tps, 2), sc.spec(tps, 2)] + [wspec] * nw,
        out_specs=[row] + [ospec] * nw,
        out_shape=[SDS((T, D), BF16)] + [SDS((T, N), BF16)] * nw,
        params=_params(2, blocks, temp_bytes=2 * _nbytes((tm, tn), F32) + 3 * _nbytes((tm, D), F32)), comm=comm)
    return outs[0], outs[1:]


def _matmul_nt(h, w, *, tm, tn, name, comm=None):
    T, D = h.shape
    N = w.shape[0]

    def body(h_ref, w_ref, o_ref):
        o_ref[...] = _dot_nt(h_ref[...], w_ref[...]).astype(o_ref.dtype)

    blocks = [((tm, D), BF16), ((tn, D), BF16), ((tm, tn), BF16)]
    return _call(
        body, (h, w), name=name, grid=(T // tm, N // tn),
        in_specs=[pl.BlockSpec((tm, D), lambda i, j: (i, 0)), pl.BlockSpec((tn, D), lambda i, j: (j, 0))],
        out_specs=[pl.BlockSpec((tm, tn), lambda i, j: (i, j))],
        out_shape=[SDS((T, N), BF16)],
        params=_params(2, blocks, temp_bytes=2 * _nbytes((tm, tn), F32)), comm=comm)[0]


def _ffn_down(a, b, wd, x, g, *, seq, tm, name, comm=None):
    T, F = a.shape
    D = wd.shape[1]
    tps = seq // tm

    def body(a_ref, b_ref, wd_ref, x_ref, g_ref, xo_ref, y_ref, sg_ref):
        af = a_ref[...].astype(F32)
        sg = _sigmoid(af)
        sg_ref[...] = sg.astype(BF16)
        act = (af * sg * b_ref[...].astype(F32)).astype(BF16)
        y = _dot(act, wd_ref[...])
        xo_ref[...] = x_ref[...] + (FFN_RESIDUAL * g_ref[0]) * y
        y_ref[...] = y.astype(BF16)

    wide = pl.BlockSpec((tm, F), lambda i: (i, 0))
    row = pl.BlockSpec((tm, D), lambda i: (i, 0))
    wspec = pl.BlockSpec((F, D), lambda i: (0, 0), pipeline_mode=pl.Buffered(1))
    blocks = [((tm, F), BF16)] * 3 + [((F // 2, D), BF16), ((tm, D), F32), ((tm, D), F32), ((tm, D), BF16)]
    return _call(
        body, (a, b, wd, x, g.arr), name=name, grid=(T // tm,),
        in_specs=[wide, wide, wspec, row, g.spec(tps, 1)], out_specs=[row, row, wide],
        out_shape=[SDS((T, D), F32), SDS((T, D), BF16), SDS((T, F), BF16)],
        params=_params(1, blocks, temp_bytes=3 * _nbytes((tm, F), F32)), comm=comm)


def _final_loss(x, gf, tgt, *, tm, name):
    T, D = x.shape
    nt = T // tm

    def body(x_ref, gf_ref, t_ref, dx_ref, loss_ref, dgf_ref, lacc):
        i = pl.program_id(0)
        xf = x_ref[...]
        gfv = gf_ref[...]
        rstd = lax.rsqrt(jnp.mean(xf * xf, axis=-1, keepdims=True) + EPS)
        xhat = xf * rstd
        err = xhat * gfv - t_ref[...]
        dy = err * (1.0 / D)
        dxhat = dy * gfv
        dx_ref[...] = rstd * (dxhat - xhat * jnp.mean(dxhat * xhat, axis=-1, keepdims=True))
        _acc(dgf_ref, _rowsum(dy * xhat), i == 0)
        _acc(lacc, _rowsum(err * err), i == 0)

        @pl.when(i == nt - 1)
        def _():
            loss_ref[...] = jnp.broadcast_to((0.5 / D) * jnp.sum(lacc[...]), loss_ref.shape)

    row = pl.BlockSpec((tm, D), lambda i: (i, 0))
    vec = pl.BlockSpec((1, D), lambda i: (0, 0))
    lspec = pl.BlockSpec((1, 128), lambda i: (0, 0))
    blocks = [((tm, D), F32)] * 3
    return _call(
        body, (x, gf, tgt), name=name, grid=(nt,),
        in_specs=[row, vec, row], out_specs=[row, lspec, vec],
        out_shape=[SDS((T, D), F32), SDS((1, 128), F32), SDS((1, D), F32)],
        scratch_shapes=[pltpu.VMEM((1, D), F32)],
        params=_params(1, blocks, temp_bytes=4 * _nbytes((tm, D), F32)))


def _ffn_bwd_down(dxo, g, y, wd, a, b, sg, *, seq, tm, tn, name, comm=None):
    T, F = a.shape
    D = wd.shape[1]
    tps = seq // tm
    nb = T // seq

    def body(dxo_ref, g_ref, y_ref, wd_ref, a_ref, b_ref, sg_ref, dyb_ref, da_ref, db_ref, act_ref, dg_ref):
        i = pl.program_id(0)

        @pl.when(pl.program_id(1) == 0)
        def _():
            dx = dxo_ref[...]
            dyb_ref[...] = ((FFN_RESIDUAL * g_ref[0]) * dx).astype(BF16)
            part = _rowsum(FFN_RESIDUAL * dx * y_ref[...].astype(F32))
            _acc(dg_ref, part[None], i % tps == 0)

        dact = _dot_nt(dyb_ref[...], wd_ref[...])
        af = a_ref[...].astype(F32)
        bf = b_ref[...].astype(F32)
        sg = sg_ref[...].astype(F32)
        silu = af * sg
        act_ref[...] = (silu * bf).astype(BF16)
        da_ref[...] = (dact * bf * (sg + silu * (1.0 - sg))).astype(BF16)
        db_ref[...] = (dact * silu).astype(BF16)

    row = pl.BlockSpec((tm, D), lambda i, j: (i, 0))
    per_b = pl.BlockSpec((1, 1, D), lambda i, j: (i // tps, 0, 0))
    wspec = pl.BlockSpec((tn, D), lambda i, j: (j, 0))
    chunk = pl.BlockSpec((tm, tn), lambda i, j: (i, j))
    blocks = [((tm, D), F32), ((tm, D), BF16), ((tn, D), BF16), ((tm, D), BF16)] + [((tm, tn), BF16)] * 6
    return _call(
        body, (dxo, g.arr, y, wd, a, b, sg), name=name, grid=(T // tm, F // tn),
        in_specs=[row, g.spec(tps, 2), row, wspec, chunk, chunk, chunk],
        out_specs=[row, chunk, chunk, chunk, per_b],
        out_shape=[SDS((T, D), BF16)] + [SDS((T, F), BF16)] * 3 + [SDS((nb, 1, D), F32)],
        params=_params(2, blocks, temp_bytes=6 * _nbytes((tm, tn), F32)), comm=comm)


def _matmul_norm_mod_bwd(ds, ws, x, gn, sc, dxo, *, seq, tm, name, comm=None):
    T, D = x.shape
    nk = len(ws)
    sizes = [len(g) for g in ds]
    ds = [d for g in ds for d in g]
    tps = seq // tm
    nb = T // seq

    def body(*refs):
        w_refs = refs[len(ds):len(ds) + nk]
        x_ref, gn_ref, sc_ref, dxo_ref, dxi_ref, dsh_ref, dsc_ref, dgn_ref = refs[len(ds) + nk:]
        i = pl.program_id(0)
        dh, at = None, 0
        for n, w_ref in zip(sizes, w_refs):
            pieces = [r[...] for r in refs[at:at + n]]
            at += n
            part = _dot(pieces[0] if n == 1 else jnp.concatenate(pieces, axis=1), w_ref[...])
            dh = part if dh is None else dh + part
        gnv = gn_ref[...]
        scv = sc_ref[0]
        _, xhat, rstd, yn = _norm_mod(x_ref[...], gnv, 0.0, scv)
        dyn = dh * (1.0 + scv)
        dxhat = dyn * gnv
        dxi_ref[...] = dxo_ref[...] + rstd * (dxhat - xhat * jnp.mean(dxhat * xhat, axis=-1, keepdims=True))
        first_of_seq = i % tps == 0
        _acc(dsh_ref, _rowsum(dh)[None], first_of_seq)
        _acc(dsc_ref, _rowsum(dh * yn)[None], first_of_seq)
        _acc(dgn_ref, _rowsum(dyn * xhat), i == 0)

    row = pl.BlockSpec((tm, D), lambda i: (i, 0))
    vec = pl.BlockSpec((1, D), lambda i: (0, 0))
    per_b = pl.BlockSpec((1, 1, D), lambda i: (i // tps, 0, 0))
    d_specs = [pl.BlockSpec((tm, d.shape[1]), lambda i: (i, 0)) for d in ds]
    w_specs = [pl.BlockSpec(w.shape, lambda i: (0, 0)) for w in ws]
    blocks = ([((tm, d.shape[1]), BF16) for d in ds] + [(w.shape, BF16) for w in ws] + [((tm, D), F32)] * 3)
    return _call(
        body, (*ds, *ws, x, gn, sc.arr, dxo), name=name, grid=(T // tm,),
        in_specs=d_specs + w_specs + [row, vec, sc.spec(tps, 1), row],
        out_specs=[row, per_b, per_b, vec],
        out_shape=[SDS((T, D), F32), SDS((nb, 1, D), F32), SDS((nb, 1, D), F32), SDS((1, D), F32)],
        params=_params(1, blocks, temp_bytes=6 * _nbytes((tm, D), F32)), comm=comm)


def _layernorm_silu(yc, lg, lb):
    mu = jnp.mean(yc, axis=-1, keepdims=True)
    cen = yc - mu
    rstd = lax.rsqrt(jnp.mean(cen * cen, axis=-1, keepdims=True) + EPS)
    xh = cen * rstd
    l = xh * lg + lb
    s = _sigmoid(l)
    return l * s, xh, rstd, l, s


GATE_W = 256


def _gate_specs(tm, D, col):
    return [pl.BlockSpec((tm, GATE_W), lambda i, blk=col // GATE_W + t: (i, blk)) for t in range(D // GATE_W)]


def _gate(refs):
    return jnp.concatenate([r[...] for r in refs], axis=1).astype(F32)


def _mix_out(ao, yc, proj, wao, wco, wout, x1, g2, lg, lb, *, seq, tm, ga_col, gc_col, name, comm=None):
    T, D = x1.shape
    tps = seq // tm
    ng = D // GATE_W

    def body(ao_ref, yc_ref, *rest):
        ga_refs, gc_refs = rest[:ng], rest[ng:2 * ng]
        (wao_ref, wco_ref, wout_ref, x1_ref, g2_ref, lg_ref, lb_ref,
         x2_ref, z_ref, ya_ref, ycv_ref, cact_ref, mrg_ref, sa_ref, sc_ref) = rest[2 * ng:]
        ya = _dot(ao_ref[...], wao_ref[...])
        cact = _layernorm_silu(yc_ref[...], lg_ref[...], lb_ref[...])[0].astype(BF16)
        ycv = _dot(cact, wco_ref[...])
        sa = _sigmoid(_gate(ga_refs))
        sc_ = _sigmoid(_gate(gc_refs))
        merged = (sa * ya + sc_ * ycv).astype(BF16)
        z = _dot(merged, wout_ref[...])
        x2_ref[...] = x1_ref[...] + g2_ref[0] * z
        z_ref[...] = z.astype(BF16)
        ya_ref[...] = ya.astype(BF16)
        ycv_ref[...] = ycv.astype(BF16)
        cact_ref[...] = cact
        mrg_ref[...] = merged
        sa_ref[...] = sa.astype(BF16)
        sc_ref[...] = sc_.astype(BF16)

    row = pl.BlockSpec((tm, D), lambda i: (i, 0))
    vec = pl.BlockSpec((1, D), lambda i: (0, 0))
    per_b = pl.BlockSpec((1, 1, D), lambda i: (i // tps, 0, 0))
    wspec = pl.BlockSpec((D, D), lambda i: (0, 0))
    gates = _gate_specs(tm, D, ga_col) + _gate_specs(tm, D, gc_col)
    blocks = ([((tm, D), BF16), ((tm, D), F32), ((tm, D), BF16), ((tm, D), BF16)] + [((D, D), BF16)] * 3
              + [((tm, D), F32)] * 2 + [((tm, D), BF16)] * 7)
    return _call(
        body, (ao, yc, *[proj] * (2 * ng), wao, wco, wout, x1, g2.arr, lg, lb), name=name, grid=(T // tm,),
        in_specs=[row, row, *gates, wspec, wspec, wspec, row, g2.spec(tps, 1), vec, vec],
        out_specs=[row] * 8,
        out_shape=[SDS((T, D), F32)] + [SDS((T, D), BF16)] * 7,
        params=_params(1, blocks, temp_bytes=8 * _nbytes((tm, D), F32)), comm=comm)


def _mix_out_bwd(dx2, g2, z, wout, sa, sc, ya, ycv, wao, wco, yc, lg, lb, *, seq, tm, name, comm=None):
    T, D = dx2.shape
    tps = seq // tm
    nb = T // seq

    def body(dx2_ref, g2_ref, z_ref, wout_ref, sa_ref, sc_ref, ya_ref, ycv_ref, wao_ref, wco_ref, yc_ref, lg_ref,
             lb_ref, dz_ref, dya_ref, dycv_ref, dga_ref, dgc_ref, dao_ref, dyc_ref, dg2_ref, dlg_ref, dlb_ref):
        i = pl.program_id(0)
        dx = dx2_ref[...]
        _acc(dg2_ref, _rowsum(dx * z_ref[...].astype(F32))[None], i % tps == 0)
        dzb = (g2_ref[0] * dx).astype(BF16)
        dz_ref[...] = dzb
        dmerged = _dot_nt(dzb, wout_ref[...])
        sa = sa_ref[...].astype(F32)
        sc_ = sc_ref[...].astype(F32)
        dya = (dmerged * sa).astype(BF16)
        dycv = (dmerged * sc_).astype(BF16)
        dya_ref[...] = dya
        dycv_ref[...] = dycv
        dga_ref[...] = (dmerged * ya_ref[...].astype(F32) * (sa * (1.0 - sa))).astype(BF16)
        dgc_ref[...] = (dmerged * ycv_ref[...].astype(F32) * (sc_ * (1.0 - sc_))).astype(BF16)
        dao_ref[...] = _dot_nt(dya, wao_ref[...]).astype(BF16)
        dcact = _dot_nt(dycv, wco_ref[...])
        lgv = lg_ref[...]
        _, xh, rstd, l, s = _layernorm_silu(yc_ref[...], lgv, lb_ref[...])
        dl = dcact * (s * (1.0 + l * (1.0 - s)))
        _acc(dlb_ref, _rowsum(dl), i == 0)
        _acc(dlg_ref, _rowsum(dl * xh), i == 0)
        dxh = dl * lgv
        dyc_ref[...] = rstd * (dxh - jnp.mean(dxh, axis=-1, keepdims=True)
                               - xh * jnp.mean(dxh * xh, axis=-1, keepdims=True))

    row = pl.BlockSpec((tm, D), lambda i: (i, 0))
    vec = pl.BlockSpec((1, D), lambda i: (0, 0))
    per_b = pl.BlockSpec((1, 1, D), lambda i: (i // tps, 0, 0))
    wspec = pl.BlockSpec((D, D), lambda i: (0, 0))
    blocks = ([((tm, D), F32)] * 3 + [((tm, D), BF16)] * 11 + [((D, D), BF16)] * 3)
    return _call(
        body, (dx2, g2.arr, z, wout, sa, sc, ya, ycv, wao, wco, yc, lg, lb), name=name, grid=(T // tm,),
        in_specs=[row, g2.spec(tps, 1), row, wspec, row, row, row, row, wspec, wspec, row, vec, vec],
        out_specs=[row] * 7 + [per_b, vec, vec],
        out_shape=[SDS((T, D), BF16)] * 6 + [SDS((T, D), F32), SDS((nb, 1, D), F32), SDS((1, D), F32),
                                             SDS((1, D), F32)],
        params=_params(1, blocks, temp_bytes=10 * _nbytes((tm, D), F32)), comm=comm)


GROUP_ROWS = GQA_GROUP * ATT_BLOCK
PAIR_W = 2 * HEAD_DIM
GROUP_W = GQA_GROUP * HEAD_DIM


def _lane_lo():
    return lax.broadcasted_iota(jnp.int32, (1, PAIR_W), 1) < HEAD_DIM


def _band_bias():
    sj = lax.broadcasted_iota(jnp.int32, (2 * ATT_BLOCK, GROUP_ROWS), 0)
    qi = lax.broadcasted_iota(jnp.int32, (2 * ATT_BLOCK, GROUP_ROWS), 1) & (ATT_BLOCK - 1)
    rel = qi + ATT_BLOCK - sj
    bias = jnp.where(jnp.logical_and(rel >= 0, rel < ATT_BLOCK), 0.0, NEG_BIG)
    sj1 = lax.broadcasted_iota(jnp.int32, (2 * ATT_BLOCK, 1), 0)
    return bias, jnp.where(sj1 < ATT_BLOCK, NEG_BIG, 0.0)


def _dup_heads(src_ref, dst, seq):
    x = src_ref[...]
    i = lax.broadcasted_iota(jnp.int32, (KV_WIDTH, PAIR_W), 0)
    j = lax.broadcasted_iota(jnp.int32, (KV_WIDTH, PAIR_W), 1) & (HEAD_DIM - 1)
    for g in range(N_KV_HEADS):
        sel = jnp.where(i == j + g * HEAD_DIM, 1.0, 0.0).astype(BF16)
        dst[g, pl.ds(0, ATT_BLOCK), :] = jnp.zeros((ATT_BLOCK, PAIR_W), BF16)
        dst[g, pl.ds(ATT_BLOCK, seq), :] = _dot(x, sel).astype(BF16)


def _stack_heads(blk, g, lo):
    parts = []
    for p in range(GQA_GROUP // 2):
        pair = blk[:, g * GROUP_W + p * PAIR_W:g * GROUP_W + (p + 1) * PAIR_W]
        parts += [jnp.where(lo, pair, jnp.zeros_like(pair)), jnp.where(lo, jnp.zeros_like(pair), pair)]
    return jnp.concatenate(parts, axis=0)


def _unstack_heads(full, ref, r0, g, lo):
    for p in range(GQA_GROUP // 2):
        even = full[(2 * p) * ATT_BLOCK:(2 * p + 1) * ATT_BLOCK, :]
        odd = full[(2 * p + 1) * ATT_BLOCK:(2 * p + 2) * ATT_BLOCK, :]
        ref[pl.ds(r0, ATT_BLOCK), g * GROUP_W + p * PAIR_W:g * GROUP_W + (p + 1) * PAIR_W] = (
            jnp.where(lo, even, odd).astype(ref.dtype))


def _sink_row(sink_ref, g):
    return jnp.concatenate([jnp.full((1, ATT_BLOCK), sink_ref[0, g * GQA_GROUP + h], F32)
                            for h in range(GQA_GROUP)], axis=1)


def _group_probs(qs, k2, bias, sink):
    s = _dot_nt(k2, qs) * (HEAD_DIM ** -0.5) + bias
    m = jnp.maximum(jnp.max(s, axis=0, keepdims=True), sink)
    p = jnp.exp(s - m)
    psink = jnp.exp(sink - m)
    inv = 1.0 / (jnp.sum(p, axis=0, keepdims=True) + psink)
    return p * inv, psink * inv


def _attn_fwd(projp, sinks, *, seq, q_blk, k_blk, v_blk, name, comm=None):
    T = projp.shape[0]
    QW = N_Q_HEADS * HEAD_DIM
    nblk = seq // ATT_BLOCK

    def body(q_ref, k_ref, v_ref, sink_ref, o_ref, k2s, v2s):
        _dup_heads(k_ref, k2s, seq)
        _dup_heads(v_ref, v2s, seq)
        lo = _lane_lo()
        bias0, first_pen = _band_bias()
        sink_rows = [_sink_row(sink_ref, g) for g in range(N_KV_HEADS)]

        def blk(n, carry):
            r0 = pl.multiple_of(n * ATT_BLOCK, ATT_BLOCK)
            qb = q_ref[pl.ds(r0, ATT_BLOCK), :]
            bias = bias0 + jnp.where(n == 0, 1.0, 0.0) * first_pen
            for g in range(N_KV_HEADS):
                probs_t, _ = _group_probs(_stack_heads(qb, g, lo), k2s[g, pl.ds(r0, 2 * ATT_BLOCK), :], bias,
                                          sink_rows[g])
                _unstack_heads(_dot_tn(probs_t.astype(BF16), v2s[g, pl.ds(r0, 2 * ATT_BLOCK), :]), o_ref, r0, g, lo)
            return carry

        lax.fori_loop(0, nblk, blk, 0)

    blocks = [((seq, QW), BF16)] * 2 + [((seq, KV_WIDTH), BF16)] * 2
    return _call(
        body, (projp, projp, projp, sinks), name=name, grid=(T // seq,),
        in_specs=[pl.BlockSpec((seq, QW), lambda b: (b, q_blk)),
                  pl.BlockSpec((seq, KV_WIDTH), lambda b: (b, k_blk)),
                  pl.BlockSpec((seq, KV_WIDTH), lambda b: (b, v_blk)),
                  pl.BlockSpec(memory_space=pltpu.SMEM)],
        out_specs=[pl.BlockSpec((seq, QW), lambda b: (b, 0))],
        out_shape=[SDS((T, QW), BF16)],
        scratch_shapes=[pltpu.VMEM((N_KV_HEADS, seq + ATT_BLOCK, PAIR_W), BF16)] * 2,
        params=_params(1, blocks, temp_bytes=16 * 2**20), comm=comm)[0]


def _attn_bwd(projp, dao, sinks, *, seq, q_blk, k_blk, v_blk, name, comm=None):
    T = projp.shape[0]
    QW = N_Q_HEADS * HEAD_DIM
    nblk = seq // ATT_BLOCK

    def body(q_ref, k_ref, v_ref, do_ref, sink_ref, dq_ref, dk_ref, dv_ref, dsink_ref, k2s, v2s, dkacc, dvacc):
        _dup_heads(k_ref, k2s, seq)
        _dup_heads(v_ref, v2s, seq)
        dkacc[...] = jnp.zeros(dkacc.shape, F32)
        dvacc[...] = jnp.zeros(dvacc.shape, F32)
        lane = lax.broadcasted_iota(jnp.int32, (1, PAIR_W), 1)
        lo = lane < HEAD_DIM
        bias0, first_pen = _band_bias()
        sink_rows = [_sink_row(sink_ref, g) for g in range(N_KV_HEADS)]

        def blk(n, dsink):
            r0 = pl.multiple_of(n * ATT_BLOCK, ATT_BLOCK)
            band = pl.ds(r0, 2 * ATT_BLOCK)
            qb = q_ref[pl.ds(r0, ATT_BLOCK), :]
            dob = do_ref[pl.ds(r0, ATT_BLOCK), :]
            bias = bias0 + jnp.where(n == 0, 1.0, 0.0) * first_pen
            for g in range(N_KV_HEADS):
                qs = _stack_heads(qb, g, lo)
                dos = _stack_heads(dob, g, lo)
                k2 = k2s[g, band, :]
                v2 = v2s[g, band, :]
                probs_t, psink = _group_probs(qs, k2, bias, sink_rows[g])
                dp_t = _dot_nt(v2, dos)
                delta = jnp.sum(probs_t * dp_t, axis=0, keepdims=True)
                ds_t = (probs_t * (dp_t - delta) * (HEAD_DIM ** -0.5)).astype(BF16)
                tsink = psink * delta
                for h in range(GQA_GROUP):
                    dsink = dsink + jnp.where(lane == g * GQA_GROUP + h,
                                              -jnp.sum(tsink[:, h * ATT_BLOCK:(h + 1) * ATT_BLOCK]), 0.0)
                _unstack_heads(_dot_tn(ds_t, k2), dq_ref, r0, g, lo)
                dkacc[g, band, :] = dkacc[g, band, :] + _dot(ds_t, qs)
                dvacc[g, band, :] = dvacc[g, band, :] + _dot(probs_t.astype(BF16), dos)
            return dsink

        dsink = lax.fori_loop(0, nblk, blk, jnp.zeros((1, PAIR_W), F32))
        _acc(dsink_ref, dsink, pl.program_id(0) == 0)

        def fold(acc, g):
            a = acc[g, pl.ds(ATT_BLOCK, seq), :]
            return a + pltpu.roll(a, HEAD_DIM, 1)

        dk_ref[...] = jnp.where(lo, fold(dkacc, 0), fold(dkacc, 1)).astype(BF16)
        dv_ref[...] = jnp.where(lo, fold(dvacc, 0), fold(dvacc, 1)).astype(BF16)

    blocks = [((seq, QW), BF16)] * 3 + [((seq, KV_WIDTH), BF16)] * 4
    kv_spec_out = pl.BlockSpec((seq, KV_WIDTH), lambda b: (b, 0))
    return _call(
        body, (projp, projp, projp, dao, sinks), name=name, grid=(T // seq,),
        in_specs=[pl.BlockSpec((seq, QW), lambda b: (b, q_blk)),
                  pl.BlockSpec((seq, KV_WIDTH), lambda b: (b, k_blk)),
                  pl.BlockSpec((seq, KV_WIDTH), lambda b: (b, v_blk)),
                  pl.BlockSpec((seq, QW), lambda b: (b, 0)),
                  pl.BlockSpec(memory_space=pltpu.SMEM)],
        out_specs=[pl.BlockSpec((seq, QW), lambda b: (b, 0)), kv_spec_out, kv_spec_out,
                   pl.BlockSpec((1, 128), lambda b: (0, 0))],
        out_shape=[SDS((T, QW), BF16), SDS((T, KV_WIDTH), BF16), SDS((T, KV_WIDTH), BF16), SDS((1, 128), F32)],
        scratch_shapes=[pltpu.VMEM((N_KV_HEADS, seq + ATT_BLOCK, PAIR_W), BF16)] * 2
        + [pltpu.VMEM((N_KV_HEADS, seq + ATT_BLOCK, PAIR_W), F32)] * 2,
        params=_params(1, blocks, temp_bytes=24 * 2**20), comm=comm)


SUBLANES = 8


def _sublane_shifts(win):
    n = CONV_ROWS + CONV_HALO
    return [win] + [pltpu.roll(win, n - b, 0) for b in range(1, SUBLANES)]


def _window(shifted, off):
    a = off // SUBLANES * SUBLANES
    return shifted[off % SUBLANES][a:a + CONV_ROWS, :]


def _conv_fwd(projp, w, bias, *, seq, cw, a_col, b_col, name, comm=None):
    T = projp.shape[0]
    C = w.shape[1]
    nchunk = seq // CONV_ROWS

    def body(a_ref, b_ref, w_ref, bias_ref, y_ref, upad):
        upad[pl.ds(0, CONV_HALO), :] = jnp.zeros((CONV_HALO, cw), F32)
        upad[pl.ds(CONV_HALO, seq), :] = a_ref[...].astype(F32) * _sigmoid(b_ref[...].astype(F32))
        wv = w_ref[...]
        bv = bias_ref[...]

        def chunk(r, carry):
            r0 = pl.multiple_of(r * CONV_ROWS, CONV_ROWS)
            shifted = _sublane_shifts(upad[pl.ds(r0, CONV_ROWS + CONV_HALO), :])
            acc = jnp.broadcast_to(bv, (CONV_ROWS, cw))
            for k in range(CONV_WIDTH):
                acc = acc + wv[k:k + 1, :] * _window(shifted, CONV_HALO - (CONV_WIDTH - 1) + k)
            y_ref[pl.ds(r0, CONV_ROWS), :] = acc
            return carry

        lax.fori_loop(0, nchunk, chunk, 0)

    blocks = [((seq, cw), BF16)] * 2 + [((seq, cw), F32)]
    return _call(
        body, (projp, projp, w, bias), name=name, grid=(T // seq, C // cw),
        in_specs=[pl.BlockSpec((seq, cw), lambda b, c: (b, a_col // cw + c)),
                  pl.BlockSpec((seq, cw), lambda b, c: (b, b_col // cw + c)),
                  pl.BlockSpec((CONV_WIDTH, cw), lambda b, c: (0, c)),
                  pl.BlockSpec((1, cw), lambda b, c: (0, c))],
        out_specs=[pl.BlockSpec((seq, cw), lambda b, c: (b, c))],
        out_shape=[SDS((T, C), F32)],
        scratch_shapes=[pltpu.VMEM((seq + CONV_HALO, cw), F32)],
        params=_params(2, blocks, temp_bytes=6 * _nbytes((seq, cw), F32)), comm=comm)[0]


def _conv_bwd(dy, projp, w, *, seq, cw, a_col, b_col, name, comm=None):
    T = projp.shape[0]
    C = w.shape[1]
    nchunk = seq // CONV_ROWS
    SUB = 8

    def body(dy_ref, a_ref, b_ref, w_ref, da_ref, db_ref, dw_ref, dbias_ref, dypad, dwp):
        first = pl.program_id(1) == 0
        dyv = dy_ref[...]
        dypad[pl.ds(0, seq), :] = dyv
        dypad[pl.ds(seq, CONV_HALO), :] = jnp.zeros((CONV_HALO, cw), F32)
        dwp[...] = jnp.zeros(dwp.shape, F32)
        wv = w_ref[...]

        def chunk(r, carry):
            r0 = pl.multiple_of(r * CONV_ROWS, CONV_ROWS)
            dy_shifts = _sublane_shifts(dypad[pl.ds(r0, CONV_ROWS + CONV_HALO), :])
            ac = a_ref[pl.ds(r0, CONV_ROWS), :].astype(F32)
            sbc = _sigmoid(b_ref[pl.ds(r0, CONV_ROWS), :].astype(F32))
            uc = ac * sbc
            du = jnp.zeros((CONV_ROWS, cw), F32)
            for k in range(CONV_WIDTH):
                dyk = _window(dy_shifts, CONV_WIDTH - 1 - k)
                du = du + wv[k:k + 1, :] * dyk
                prod = uc * dyk
                part = prod[0:SUB, :]
                for s in range(1, CONV_ROWS // SUB):
                    part = part + prod[s * SUB:(s + 1) * SUB, :]
                dwp[pl.ds(k * SUB, SUB), :] = dwp[pl.ds(k * SUB, SUB), :] + part
            da_ref[pl.ds(r0, CONV_ROWS), :] = (du * sbc).astype(BF16)
            db_ref[pl.ds(r0, CONV_ROWS), :] = (du * ac * (sbc * (1.0 - sbc))).astype(BF16)
            return carry

        lax.fori_loop(0, nchunk, chunk, 0)

        @pl.when(first)
        def _():
            dw_ref[...] = jnp.zeros(dw_ref.shape, F32)
            dbias_ref[...] = jnp.zeros(dbias_ref.shape, F32)

        for k in range(CONV_WIDTH):
            dw_ref[k:k + 1, :] = dw_ref[k:k + 1, :] + _rowsum(dwp[pl.ds(k * SUB, SUB), :])
        dbias_ref[...] = dbias_ref[...] + _rowsum(dyv)

    blocks = [((seq, cw), F32)] + [((seq, cw), BF16)] * 4
    return _call(
        body, (dy, projp, projp, w), name=name, grid=(C // cw, T // seq),
        in_specs=[pl.BlockSpec((seq, cw), lambda c, b: (b, c)),
                  pl.BlockSpec((seq, cw), lambda c, b: (b, a_col // cw + c)),
                  pl.BlockSpec((seq, cw), lambda c, b: (b, b_col // cw + c)),
                  pl.BlockSpec((CONV_WIDTH, cw), lambda c, b: (0, c))],
        out_specs=[pl.BlockSpec((seq, cw), lambda c, b: (b, c)), pl.BlockSpec((seq, cw), lambda c, b: (b, c)),
                   pl.BlockSpec((CONV_WIDTH, cw), lambda c, b: (0, c)), pl.BlockSpec((1, cw), lambda c, b: (0, c))],
        out_shape=[SDS((T, C), BF16), SDS((T, C), BF16), SDS((CONV_WIDTH, C), F32), SDS((1, C), F32)],
        scratch_shapes=[pltpu.VMEM((seq + CONV_HALO, cw), F32), pltpu.VMEM((CONV_WIDTH * SUB, cw), F32)],
        params=_params(2, blocks, temp_bytes=8 * _nbytes((seq, cw), F32)), comm=comm)


def _matmul_tn(a, b, *, name, comm=None):
    T, M = a.shape
    N = b.shape[1]
    bm = _pick(M, (768, 512, 256))

    def body(a_ref, b_ref, o_ref):
        o_ref[...] = _dot_tn(a_ref[...], b_ref[...]).astype(BF16)

    blocks = [((T, bm), BF16), ((T, N), BF16), ((bm, N), BF16)]
    return _call(
        body, (a, b), name=name, grid=(M // bm,),
        in_specs=[pl.BlockSpec((T, bm), lambda i: (0, i)), pl.BlockSpec((T, N), lambda i: (0, 0))],
        out_specs=[pl.BlockSpec((bm, N), lambda i: (i, 0))],
        out_shape=[SDS((M, N), BF16)],
        params=_params(1, blocks, temp_bytes=2 * _nbytes((T, bm), BF16) + 2 * _nbytes((bm, N), F32)),
        comm=comm)[0]


TN_BLOCK = 256


def _matmul_tn_pieces(groups, b, *, name, comm=None):
    T, N = b.shape
    flat = [a for g in groups for a in g]
    starts, n_steps = [], 0
    for g in groups:
        width = sum(a.shape[1] for a in g)
        assert width % TN_BLOCK == 0 and (len(g) == 1 or width == TN_BLOCK), [a.shape for a in g]
        starts.append(n_steps)
        n_steps += width // TN_BLOCK

    def body(*refs):
        a_refs, b_ref, o_ref = refs[:len(flat)], refs[len(flat)], refs[len(flat) + 1]
        i = pl.program_id(0)
        at = 0
        for g, start in zip(groups, starts):
            mine = a_refs[at:at + len(g)]
            at += len(g)
            steps = sum(a.shape[1] for a in g) // TN_BLOCK

            @pl.when(jnp.logical_and(i >= start, i < start + steps))
            def _(mine=mine):
                a = mine[0][...] if len(mine) == 1 else jnp.concatenate([r[...] for r in mine], axis=1)
                o_ref[...] = _dot_tn(a, b_ref[...]).astype(BF16)

    a_specs = []
    for g, start in zip(groups, starts):
        for a in g:
            if len(g) == 1:
                last = a.shape[1] // TN_BLOCK - 1
                a_specs.append(pl.BlockSpec(
                    (T, TN_BLOCK), lambda i, start=start, last=last: (0, jnp.clip(i - start, 0, last))))
            else:
                a_specs.append(pl.BlockSpec((T, a.shape[1]), lambda i: (0, 0)))
    blocks = [((T, TN_BLOCK), BF16)] * len(flat) + [((T, N), BF16), ((TN_BLOCK, N), BF16)]
    return _call(
        body, (*flat, b), name=name, grid=(n_steps,),
        in_specs=a_specs + [pl.BlockSpec((T, N), lambda i: (0, 0))],
        out_specs=[pl.BlockSpec((TN_BLOCK, N), lambda i: (i, 0))],
        out_shape=[SDS((n_steps * TN_BLOCK, N), BF16)],
        params=_params(1, blocks, temp_bytes=2 * _nbytes((T, TN_BLOCK), BF16) + 2 * _nbytes((TN_BLOCK, N), F32)),
        comm=comm)[0]


def _sum_parts(p_ref):
    g = p_ref[0].astype(F32)
    for s in range(1, p_ref.shape[0]):
        g = g + p_ref[s].astype(F32)
    return g


def _pair_add(g, staged, *, name):
    _, R, W = g.shape
    nq = staged.shape[0]
    tr = _row_tile(R)

    def body(g_ref, s_ref, o_ref):
        mine = jnp.where(lax.axis_index("c") == 0, g_ref[0, 0].astype(F32), g_ref[0, 1].astype(F32))
        o_ref[0] = (mine + s_ref[0].astype(F32)).astype(o_ref.dtype)

    return _call(
        body, (g.reshape(nq, 2, R, W), staged), name=name, grid=(nq, R // tr),
        in_specs=[pl.BlockSpec((1, 2, tr, W), lambda q, i: (q, 0, i, 0)),
                  pl.BlockSpec((1, tr, W), lambda q, i: (q, i, 0))],
        out_specs=[pl.BlockSpec((1, tr, W), lambda q, i: (q, i, 0))],
        out_shape=[SDS((nq, R, W), g.dtype)],
        params=_params(2, [((4, tr, W), g.dtype)], temp_bytes=3 * _nbytes((tr, W), F32)))[0]


def _adamw_update(w, g, m, v):
    m = ADAM_B1 * m + (1.0 - ADAM_B1) * g
    v = ADAM_B2 * v + (1.0 - ADAM_B2) * (g * g)
    m_hat = m / (1.0 - ADAM_B1 ** ADAM_STEP)
    v_hat = v / (1.0 - ADAM_B2 ** ADAM_STEP)
    delta = -ADAM_LR * (m_hat / (jnp.sqrt(v_hat) + ADAM_EPS) + ADAM_WD * w)
    return delta, m, v


def _row_tile(R):
    return _pick(R, (256, 128, 112, 88, 64, 32, 16, 8))


def _sum8(parts, *, name):
    n, R, W = parts.shape
    tr = _row_tile(R)

    def body(p_ref, o_ref):
        o_ref[...] = _sum_parts(p_ref)

    return _call(
        body, (parts,), name=name, grid=(R // tr,),
        in_specs=[pl.BlockSpec((n, tr, W), lambda i: (0, i, 0))],
        out_specs=[pl.BlockSpec((tr, W), lambda i: (i, 0))],
        out_shape=[SDS((R, W), F32)],
        params=_params(1, [((n, tr, W), parts.dtype), ((tr, W), F32)]))[0]


def _adamw(g, w, m, v, *, name):
    R, W = w.shape
    tr = _row_tile(R)

    def body(g_ref, w_ref, m_ref, v_ref, d_ref, mo_ref, vo_ref):
        d_ref[...], mo_ref[...], vo_ref[...] = _adamw_update(w_ref[...], g_ref[...], m_ref[...], v_ref[...])

    spec = pl.BlockSpec((tr, W), lambda i: (i, 0))
    return _call(
        body, (g, w, m, v), name=name, grid=(R // tr,),
        in_specs=[spec] * 4, out_specs=[spec] * 3, out_shape=[SDS((R, W), F32)] * 3,
        params=_params(1, [((tr, W), F32)] * 7))


def _sum8_adamw(parts, w, m, v, *, name):
    R, W = w.shape
    n = parts.shape[0]
    tr = _row_tile(R)

    def body(p_ref, w_ref, m_ref, v_ref, g_ref, d_ref, mo_ref, vo_ref):
        g = _sum_parts(p_ref)
        g_ref[...] = g
        d_ref[...], mo_ref[...], vo_ref[...] = _adamw_update(w_ref[...], g, m_ref[...], v_ref[...])

    spec = pl.BlockSpec((tr, W), lambda i: (i, 0))
    return _call(
        body, (parts, w, m, v), name=name, grid=(R // tr,),
        in_specs=[pl.BlockSpec((n, tr, W), lambda i: (0, i, 0))] + [spec] * 3,
        out_specs=[spec] * 4, out_shape=[SDS((R, W), F32)] * 4,
        params=_params(1, [((n, tr, W), parts.dtype)] + [((tr, W), F32)] * 7))


def _ada_fwd(c_all, w, bias, *, name):
    NB, D = c_all.shape
    N = w.shape[1]

    def body(c_ref, w_ref, b_ref, o_ref):
        cv = c_ref[...]
        ca = (cv * _sigmoid(cv)).astype(BF16)
        o_ref[...] = _dot(ca, w_ref[...].astype(BF16)) + b_ref[...]

    full = lambda s: pl.BlockSpec(s, lambda i: (0,) * len(s))
    return _call(
        body, (c_all, w, bias), name=name, grid=(1,),
        in_specs=[full((NB, D)), full((D, N)), full((1, N))], out_specs=[full((NB, N))],
        out_shape=[SDS((NB, N), F32)],
        params=_params(1, [((D, N), F32)], temp_bytes=_nbytes((D, N), BF16)))[0]


def _ada_bwd(c_all, gmod_all, *, n_col, name):
    NB, D = c_all.shape
    N = gmod_all.shape[1]

    def body(c_ref, g_ref, gw_ref, gb_ref):
        cv = c_ref[...]
        ca = (cv * _sigmoid(cv)).astype(BF16)
        first = pl.multiple_of(_lin(_my_pos()) * n_col, 128)
        gw_ref[...] = _dot_tn(ca, g_ref[:, pl.ds(first, n_col)].astype(BF16))
        gb_ref[...] = _rowsum(g_ref[...])

    full = lambda s: pl.BlockSpec(s, lambda i: (0,) * len(s))
    return _call(
        body, (c_all, gmod_all), name=name, grid=(1,),
        in_specs=[full((NB, D)), full((NB, N))], out_specs=[full((D, n_col)), full((1, N))],
        out_shape=[SDS((D, n_col), F32), SDS((1, N), F32)],
        params=_params(1, [((D, n_col), F32), ((NB, N), F32)]))


def kernel(x, c, w_ada, b_ada, norm_ffn1_g, ffn1_w_gate, ffn1_w_up, ffn1_w_down, norm_mix_g, w_in, attn_sinks, w_attn_o, conv_w_dw, conv_b_dw, conv_ln_g, conv_ln_b, w_conv_o, w_out, norm_ffn2_g, ffn2_w_gate, ffn2_w_up, ffn2_w_down, final_norm_g, loss_target, m_w_ada, m_b_ada, m_norm_ffn1_g, m_ffn1_w_gate, m_ffn1_w_up, m_ffn1_w_down, m_norm_mix_g, m_w_in, m_attn_sinks, m_w_attn_o, m_conv_w_dw, m_conv_b_dw, m_conv_ln_g, m_conv_ln_b, m_w_conv_o, m_w_out, m_norm_ffn2_g, m_ffn2_w_gate, m_ffn2_w_up, m_ffn2_w_down, m_final_norm_g, v_w_ada, v_b_ada, v_norm_ffn1_g, v_ffn1_w_gate, v_ffn1_w_up, v_ffn1_w_down, v_norm_mix_g, v_w_in, v_attn_sinks, v_w_attn_o, v_conv_w_dw, v_conv_b_dw, v_conv_ln_g, v_conv_ln_b, v_w_conv_o, v_w_out, v_norm_ffn2_g, v_ffn2_w_gate, v_ffn2_w_up, v_ffn2_w_down, v_final_norm_g):
    B, S, D = x.shape
    T = B * S
    QW = N_Q_HEADS * HEAD_DIM
    CC = conv_w_dw.shape[2] * N_DEV
    me = _lin(_my_pos())
    xf = x.reshape(T, D)
    tgt = loss_target.reshape(T, D)
    tm = min(512, S)
    kw = dict(seq=S, tm=tm)

    p_k, p_v, p_ca = QW, QW + KV_WIDTH, QW + 2 * KV_WIDTH
    p_cb, p_ga, p_gc = p_ca + CC, p_ca + 2 * CC, p_ca + 2 * CC + D

    def col_t(w):
        return w[0].T.astype(BF16)

    def row_b(w):
        return w[0].astype(BF16)

    def rows(g):
        return g.reshape(-1, g.shape[-1])

    def blocks8(g):
        return g.reshape(N_DEV, g.shape[0] // N_DEV, g.shape[1])

    def gather(*arrs):
        return _Comm([(a, "gather") for a in arrs])

    g_wg1, g_convw, g_c = _exchange(
        [(col_t(ffn1_w_gate), "gather"), (conv_w_dw[0], "gather"), (c, "gather")], name="gather_first")
    wg1 = rows(g_wg1)
    conv_w = g_convw.transpose(1, 0, 2).reshape(CONV_WIDTH, CC)
    c_all = g_c.reshape(N_DEV * B, D)

    n_col = N_MOD * D // N_DEV
    b_cols = lax.dynamic_slice(b_ada, (0, me * n_col), (1, n_col))
    mod_cols = _ada_fwd(c_all, w_ada[0], b_cols, name="ada_fwd")
    mod_mine = _exchange([(mod_cols.reshape(N_DEV, B, n_col), "scatter")], name="scatter_mod")[0]
    mod = mod_mine.transpose(1, 0, 2).reshape(B * N_MOD, 1, D)
    sh1, sc1, g1, sh2, sc2, g2, sh3, sc3, g3 = [_ModVec(mod, i) for i in range(N_MOD)]

    F = wg1.shape[0]
    tn_f = _pick(F, (1408, 1024, 512, 256))
    tn_in = _pick(w_in.shape[2] * N_DEV, (1792, 768, 512, 256))
    gate_blk = dict(ga_col=p_ga, gc_col=p_gc)
    att_blk = dict(q_blk=0, k_blk=p_k // KV_WIDTH, v_blk=p_v // KV_WIDTH)
    conv_kw = dict(seq=S, cw=256, a_col=p_ca, b_col=p_cb)

    cm = gather(col_t(ffn1_w_up))
    h1, (a1,) = _norm_mod_matmul(xf, norm_ffn1_g, sh1, sc1, [wg1], tn=tn_f, name="ffn1_gate", comm=cm, **kw)
    wu1 = rows(cm.out[0])
    cm = gather(row_b(ffn1_w_down))
    b1 = _matmul_nt(h1, wu1, tm=tm, tn=tn_f, name="ffn1_up", comm=cm)
    wd1 = rows(cm.out[0])
    cm = gather(col_t(w_in))
    x1, y1, sg1 = _ffn_down(a1, b1, wd1, xf, g1, name="ffn1_down", comm=cm, **kw)
    winp = rows(cm.out[0])
    cm = gather(row_b(w_attn_o), row_b(w_conv_o), row_b(w_out))
    h2, (projp,) = _norm_mod_matmul(x1, norm_mix_g, sh2, sc2, [winp], tn=tn_in, name="mix_in", comm=cm, **kw)
    wao, wco, wout = [rows(o) for o in cm.out]
    cm = gather(col_t(ffn2_w_gate))
    ao = _attn_fwd(projp, attn_sinks, seq=S, name="attn_fwd", comm=cm, **att_blk)
    wg2 = rows(cm.out[0])
    cm = gather(col_t(ffn2_w_up))
    yc = _conv_fwd(projp, conv_w, conv_b_dw, name="conv_fwd", comm=cm, **conv_kw)
    wu2 = rows(cm.out[0])
    cm = gather(row_b(ffn2_w_down))
    x2, z, ya, ycv, cact, merged, sga, sgc = _mix_out(ao, yc, projp, wao, wco, wout, x1, g2, conv_ln_g, conv_ln_b,
                                                      name="mix_out", comm=cm, **gate_blk, **kw)
    wd2 = rows(cm.out[0])
    h3, (a3, b3) = _norm_mod_matmul(x2, norm_ffn2_g, sh3, sc3, [wg2, wu2], tn=tn_f, name="ffn2_up", **kw)
    x3, y3, sg3 = _ffn_down(a3, b3, wd2, x2, g3, name="ffn2_down", **kw)
    dx3, loss_row, dgf = _final_loss(x3, final_norm_g[None], tgt, tm=tm, name="final_loss")

    parts = {}

    def pair(*gs):
        return [(blocks8(g), "pair") for g in gs]

    def cross(*rs):
        return [(r, "cross") for r in rs]

    def reduce_pairs(gs, staged, names):
        return [_pair_add(blocks8(g), s, name="pair_add_" + n) for g, s, n in zip(gs, staged, names)]

    dyb3, da3, db3, act3, dg3 = _ffn_bwd_down(dx3, g3, y3, wd2, a3, b3, sg3, tn=tn_f, name="ffn2_bwd_down", **kw)
    gwd2 = _matmul_tn(act3, dyb3, name="gw_ffn2_down")
    cm = _Comm(pair(gwd2))
    dx2, dsh3, dsc3, dgn3 = _matmul_norm_mod_bwd([[da3], [db3]], [wg2, wu2], x2, norm_ffn2_g, sc3, dx3,
                                                 name="ffn2_bwd_up", comm=cm, **kw)
    r_wd2, = reduce_pairs([gwd2], cm.out, ["ffn2_w_down"])
    cm = _Comm(cross(r_wd2))
    gwg2 = _matmul_tn(da3, h3, name="gw_ffn2_gate", comm=cm)
    parts["ffn2_w_down"], = cm.out
    cm = _Comm(pair(gwg2))
    gwu2 = _matmul_tn(db3, h3, name="gw_ffn2_up", comm=cm)
    r_wg2, = reduce_pairs([gwg2], cm.out, ["ffn2_w_gate"])

    cm = _Comm(cross(r_wg2) + pair(gwu2))
    dzb, dyab, dycb, dga, dgc, dao, dyc, dg2, dlng, dlnb = _mix_out_bwd(
        dx2, g2, z, wout, sga, sgc, ya, ycv, wao, wco, yc, conv_ln_g, conv_ln_b, name="mix_out_bwd", comm=cm, **kw)
    parts["ffn2_w_gate"] = cm.out[0]
    r_wu2, = reduce_pairs([gwu2], cm.out[1:], ["ffn2_w_up"])
    gwout = _matmul_tn(merged, dzb, name="gw_out")
    gwao = _matmul_tn(ao, dyab, name="gw_attn_o")
    gwco = _matmul_tn(cact, dycb, name="gw_conv_o")
    cm = _Comm(cross(r_wu2) + pair(gwout, gwao, gwco))
    dq, dk, dv, dsinks = _attn_bwd(projp, dao, attn_sinks, seq=S, name="attn_bwd", comm=cm, **att_blk)
    parts["ffn2_w_up"] = cm.out[0]
    r_mix = reduce_pairs([gwout, gwao, gwco], cm.out[1:], ["w_out", "w_attn_o", "w_conv_o"])
    cm = _Comm(cross(*r_mix))
    dca, dcb, dconvw, dconvb = _conv_bwd(dyc, projp, conv_w, name="conv_bwd", comm=cm, **conv_kw)
    parts["w_out"], parts["w_attn_o"], parts["w_conv_o"] = cm.out
    gwin = _matmul_tn_pieces([[dq], [dk, dv], [dca], [dcb], [dga], [dgc]], h2, name="gw_in")
    cm = _Comm(pair(gwin))
    dx1, dsh2, dsc2, dgn2 = _matmul_norm_mod_bwd([[dq, dk, dv, dca, dcb, dga, dgc]], [winp], x1, norm_mix_g, sc2, dx2,
                                                 name="mix_in_bwd", comm=cm, **kw)
    r_win, = reduce_pairs([gwin], cm.out, ["w_in"])

    cm = _Comm(cross(r_win))
    dyb1, da1, db1, act1, dg1 = _ffn_bwd_down(dx1, g1, y1, wd1, a1, b1, sg1, tn=tn_f, name="ffn1_bwd_down", comm=cm,
                                              **kw)
    parts["w_in"], = cm.out
    gwd1 = _matmul_tn(act1, dyb1, name="gw_ffn1_down")
    cm = _Comm(pair(gwd1))
    gwg1 = _matmul_tn(da1, h1, name="gw_ffn1_gate", comm=cm)
    r_wd1, = reduce_pairs([gwd1], cm.out, ["ffn1_w_down"])
    cm = _Comm(cross(r_wd1) + pair(gwg1))
    gwu1 = _matmul_tn(db1, h1, name="gw_ffn1_up", comm=cm)
    parts["ffn1_w_down"] = cm.out[0]
    r_wg1, = reduce_pairs([gwg1], cm.out[1:], ["ffn1_w_gate"])
    r_wu1, = reduce_pairs([gwu1], _exchange(pair(gwu1), name="pair_last"), ["ffn1_w_up"])
    cm = _Comm(cross(r_wg1, r_wu1))
    dx0, dsh1, dsc1, dgn1 = _matmul_norm_mod_bwd([[da1], [db1]], [wg1, wu1], xf, norm_ffn1_g, sc1, dx1,
                                                 name="ffn1_bwd_up", comm=cm, **kw)
    parts["ffn1_w_gate"], parts["ffn1_w_up"] = cm.out

    n_small = 8
    gmod = jnp.concatenate([dsh1, dsc1, dg1, dsh2, dsc2, dg2, dsh3, dsc3, dg3], axis=1).reshape(B, N_MOD * D)
    sink_row = jnp.pad(dsinks[:, :N_Q_HEADS], ((0, 0), (0, D - N_Q_HEADS)))
    loss_pad = jnp.pad(loss_row, ((0, 0), (0, D - loss_row.shape[1])))
    small = jnp.concatenate([dgn1, dgn2, dgn3, dgf, dconvb, dlng, dlnb, sink_row, dconvw, loss_pad], axis=0)
    small_all, gmod_all = _exchange([(small, "gather"), (gmod, "gather")], name="exchange_last")
    gsmall = _sum8(small_all, name="sum_small")
    loss = gsmall[n_small + CONV_WIDTH, 0]
    g_w_ada, g_b_ada = _ada_bwd(c_all, gmod_all.reshape(N_DEV * B, N_MOD * D), n_col=n_col, name="ada_bwd")
    g_conv_w = lax.dynamic_slice(gsmall[n_small:n_small + CONV_WIDTH], (0, me * (CC // N_DEV)),
                                 (CONV_WIDTH, CC // N_DEV))

    def col_update(name, w, m, v):
        outs = _sum8_adamw(parts[name], w[0].T, m[0].T, v[0].T, name="adamw_" + name)
        return tuple(o.T for o in outs)

    def row_update(name, w, m, v):
        return tuple(_sum8_adamw(parts[name], w[0], m[0], v[0], name="adamw_" + name))

    upd = {
        "ffn1_w_gate": col_update("ffn1_w_gate", ffn1_w_gate, m_ffn1_w_gate, v_ffn1_w_gate),
        "ffn1_w_up": col_update("ffn1_w_up", ffn1_w_up, m_ffn1_w_up, v_ffn1_w_up),
        "ffn1_w_down": row_update("ffn1_w_down", ffn1_w_down, m_ffn1_w_down, v_ffn1_w_down),
        "w_in": col_update("w_in", w_in, m_w_in, v_w_in),
        "w_attn_o": row_update("w_attn_o", w_attn_o, m_w_attn_o, v_w_attn_o),
        "w_conv_o": row_update("w_conv_o", w_conv_o, m_w_conv_o, v_w_conv_o),
        "w_out": row_update("w_out", w_out, m_w_out, v_w_out),
        "ffn2_w_gate": col_update("ffn2_w_gate", ffn2_w_gate, m_ffn2_w_gate, v_ffn2_w_gate),
        "ffn2_w_up": col_update("ffn2_w_up", ffn2_w_up, m_ffn2_w_up, v_ffn2_w_up),
        "ffn2_w_down": row_update("ffn2_w_down", ffn2_w_down, m_ffn2_w_down, v_ffn2_w_down),
        "w_ada": (g_w_ada,) + tuple(_adamw(g_w_ada, w_ada[0], m_w_ada[0], v_w_ada[0], name="adamw_w_ada")),
        "conv_w_dw": (g_conv_w,) + tuple(_adamw(g_conv_w, conv_w_dw[0], m_conv_w_dw[0], v_conv_w_dw[0],
                                                name="adamw_conv_w_dw")),
    }
    for k in upd:
        upd[k] = tuple(t[None] for t in upd[k])

    def pad_sinks(t):
        return jnp.pad(t, ((0, 0), (0, D - N_Q_HEADS)))

    def pack(f1, mix, f2, fin, cb, lg, lb, sinks, bada):
        return jnp.concatenate([f1, mix, f2, fin[None], cb, lg, lb, pad_sinks(sinks), bada.reshape(N_MOD, D)], axis=0)

    w_s = pack(norm_ffn1_g, norm_mix_g, norm_ffn2_g, final_norm_g, conv_b_dw, conv_ln_g, conv_ln_b, attn_sinks, b_ada)
    m_s = pack(m_norm_ffn1_g, m_norm_mix_g, m_norm_ffn2_g, m_final_norm_g, m_conv_b_dw, m_conv_ln_g, m_conv_ln_b,
               m_attn_sinks, m_b_ada)
    v_s = pack(v_norm_ffn1_g, v_norm_mix_g, v_norm_ffn2_g, v_final_norm_g, v_conv_b_dw, v_conv_ln_g, v_conv_ln_b,
               v_attn_sinks, v_b_ada)
    g_s = jnp.concatenate([gsmall[:n_small], g_b_ada.reshape(N_MOD, D)], axis=0)
    small_out = (g_s,) + tuple(_adamw(g_s, w_s, m_s, v_s, name="adamw_vectors"))

    def unpack(t):
        return {
            "norm_ffn1_g": t[0:1], "norm_mix_g": t[1:2], "norm_ffn2_g": t[2:3], "final_norm_g": t[3],
            "conv_b_dw": t[4:5], "conv_ln_g": t[5:6], "conv_ln_b": t[6:7], "attn_sinks": t[7:8, :N_Q_HEADS],
            "b_ada": t[n_small:n_small + N_MOD].reshape(1, N_MOD * D),
        }

    small_un = [unpack(t) for t in small_out]
    for k in small_un[0]:
        upd[k] = tuple(s[k] for s in small_un)

    order = ["w_ada", "b_ada", "norm_ffn1_g", "ffn1_w_gate", "ffn1_w_up", "ffn1_w_down", "norm_mix_g", "w_in",
             "attn_sinks", "w_attn_o", "conv_w_dw", "conv_b_dw", "conv_ln_g", "conv_ln_b", "w_conv_o", "w_out",
             "norm_ffn2_g", "ffn2_w_gate", "ffn2_w_up", "ffn2_w_down", "final_norm_g"]
    grad_x = dx0.reshape(B, S, D)
    return (loss, grad_x, *[upd[k][0] for k in order], *[upd[k][1] for k in order],
            *[upd[k][2] for k in order], *[upd[k][3] for k in order])
```

```python
import dataclasses

import jax
import jax.numpy as jnp
from jax import lax
from jax.experimental import pallas as pl
from jax.experimental.pallas import tpu as pltpu

F32 = jnp.float32
BF16 = jnp.bfloat16
SDS = jax.ShapeDtypeStruct
MESH = pl.DeviceIdType.MESH

N_DEV = 8
EPS = 1e-6
HEAD_DIM = 64
N_Q_HEADS = 16
N_KV_HEADS = 2
GQA_GROUP = N_Q_HEADS // N_KV_HEADS
KV_WIDTH = N_KV_HEADS * HEAD_DIM
ATT_BLOCK = 128
CONV_WIDTH = 31
CONV_HALO = 32
CONV_ROWS = 64
N_MOD = 9
FFN_RESIDUAL = 0.5
ADAM_LR = 0.001
ADAM_B1 = 0.9
ADAM_B2 = 0.999
ADAM_EPS = 1e-08
ADAM_WD = 0.01
ADAM_STEP = 10
NEG_BIG = -1e30
GRAD_STREAM = BF16

V7X_VMEM_BYTES = 64 * 2**20
VMEM_CAP = V7X_VMEM_BYTES - 8 * 2**20


def _nbytes(shape, dtype):
    n = 1
    for s in shape:
        n *= s
    return n * jnp.dtype(dtype).itemsize


def _params(n_axes, blocks, temp_bytes=0):
    need = 2 * sum(_nbytes(s, d) for s, d in blocks) + temp_bytes + 4 * 2**20
    return pltpu.CompilerParams(dimension_semantics=("arbitrary",) * n_axes,
                                vmem_limit_bytes=int(min(max(need, 16 * 2**20), VMEM_CAP)))


def _dot_nt(a, b):
    return lax.dot_general(a, b, (((1,), (1,)), ((), ())), preferred_element_type=F32)


def _dot_tn(a, b):
    return lax.dot_general(a, b, (((0,), (0,)), ((), ())), preferred_element_type=F32)


def _dot(a, b):
    return jnp.dot(a, b, preferred_element_type=F32)


def _sigmoid(x):
    return jax.nn.sigmoid(x)


def _rowsum(v):
    return jnp.sum(v, axis=0, keepdims=True)


def _acc(ref, val, first):
    @pl.when(first)
    def _():
        ref[...] = val

    @pl.when(jnp.logical_not(first))
    def _():
        ref[...] = ref[...] + val


def _norm_mod(xf, gn, sh, sc):
    rstd = lax.rsqrt(jnp.mean(xf * xf, axis=-1, keepdims=True) + EPS)
    xhat = xf * rstd
    yn = xhat * gn
    return yn * (1.0 + sc) + sh, xhat, rstd, yn


def _pick(n, cands):
    for c in cands:
        if n % c == 0:
            return c
    return n


def _my_pos():
    return lax.axis_index("x"), lax.axis_index("y"), lax.axis_index("c")


def _peer(pos, k):
    x, y, c = pos
    return ((1 - x) if k & 4 else x, (1 - y) if k & 2 else y, (1 - c) if k & 1 else c)


def _lin(pos):
    return 4 * pos[0] + 2 * pos[1] + pos[2]


class _Comm:
    N_COPY = N_DEV - 1
    N_CHIP = N_DEV // 2

    def __init__(self, items):
        self.arrs = [a for a, _ in items]
        self.modes = [m for _, m in items]
        self.n = len(items)
        self.out = None

    def out_shape(self):
        def shape(a, m):
            return {"gather": (N_DEV,) + a.shape, "scatter": a.shape, "pair": (self.N_CHIP,) + a.shape[1:],
                    "cross": a.shape}[m]
        return [SDS(shape(a, m), a.dtype) for a, m in zip(self.arrs, self.modes)]

    def scratch(self):
        return [pltpu.SemaphoreType.DMA((self.n * self.N_COPY,)), pltpu.SemaphoreType.DMA((self.n * self.N_COPY,)),
                pltpu.SemaphoreType.DMA((self.n,))]

    def collective_id(self):
        modes = set(self.modes)
        if "scatter" in modes:
            return 3
        d2d, ici = bool(modes & {"gather", "pair"}), bool(modes & {"gather", "cross"})
        return {(True, False): 0, (False, True): 1, (True, True): 2}[(d2d, ici)]

    def barrier(self):
        x, y, c = _my_pos()
        peers = {0: [(x, y, 1 - c)],
                 1: [(1 - x, y, c), (x, 1 - y, c), (1 - x, 1 - y, c)],
                 2: [(x, y, 1 - c), (1 - x, y, c), (x, 1 - y, c), (1 - x, 1 - y, c)],
                 3: [_peer((x, y, c), k) for k in range(1, N_DEV)]}[self.collective_id()]
        sem = pltpu.get_barrier_semaphore()
        for p in peers:
            pl.semaphore_signal(sem, inc=1, device_id=p, device_id_type=MESH)
        pl.semaphore_wait(sem, len(peers))

    def _plan(self, mode, me):
        x, y, c = me
        sib = (x, y, 1 - c)
        chips = [(1 - x, y), (x, 1 - y), (1 - x, 1 - y)]

        def chip_lin(ch):
            return 2 * ch[0] + ch[1]

        if mode == "scatter":
            peers = [_peer(me, k + 1) for k in range(self.N_COPY)]
            return [(p, ("in", _lin(p)), _lin(me), _lin(p), None) for p in peers], (_lin(me), _lin(me))
        if mode == "gather":
            same = [(*ch, c) for ch in chips]
            other = [(*ch, 1 - c) for ch in chips]
            copies = [(sib, ("in", None), _lin(me), _lin(sib), None)]
            copies += [(p, ("in", None), _lin(me), _lin(p), None) for p in same]
            copies += [(sib, ("out", _lin(p)), _lin(p), _lin(o), 1 + j) for j, (p, o) in enumerate(zip(same, other))]
            return copies, (None, _lin(me))
        if mode == "pair":
            return [(sib, ("in", 2 * q + 1 - c), q, q, None) for q in range(self.N_CHIP)], None
        if mode == "cross":
            mine = chip_lin((x, y))
            return ([((*ch, c), ("in", chip_lin(ch)), mine, chip_lin(ch), None) for ch in chips], (mine, mine))
        raise ValueError(mode)

    def _copy(self, refs, me, i, k, recv):
        srcs, outs, (send_sems, recv_sems, _) = refs
        peer, (where, slot), send_slot, recv_slot, _ = self._plan(self.modes[i], me)[0][k]
        src = srcs[i] if where == "in" else outs[i]
        src = src if slot is None else src.at[slot]
        sem = i * self.N_COPY + k
        return pltpu.make_async_remote_copy(
            src_ref=src, dst_ref=outs[i].at[recv_slot if recv else send_slot], send_sem=send_sems.at[sem],
            recv_sem=recv_sems.at[sem], device_id=peer, device_id_type=MESH)

    def _local(self, refs, me, i):
        srcs, outs, (_, _, loc_sems) = refs
        local = self._plan(self.modes[i], me)[1]
        if local is None:
            return None
        own = srcs[i] if local[0] is None else srcs[i].at[local[0]]
        return pltpu.make_async_copy(own, outs[i].at[local[1]], loc_sems.at[i])

    def start(self, refs):
        me = _my_pos()
        for i in range(self.n):
            local = self._local(refs, me, i)
            if local is not None:
                local.start()
            for k, cp in enumerate(self._plan(self.modes[i], me)[0]):
                if cp[4] is None:
                    self._copy(refs, me, i, k, False).start()

    def forward(self, refs):
        me = _my_pos()
        for i in range(self.n):
            for k, cp in enumerate(self._plan(self.modes[i], me)[0]):
                if cp[4] is not None:
                    self._copy(refs, me, i, cp[4], True).wait_recv()
                    self._copy(refs, me, i, k, False).start()

    def finish(self, refs):
        me = _my_pos()
        plans = [self._plan(m, me)[0] for m in self.modes]
        for i in range(self.n):
            passed_on = [cp[4] for cp in plans[i] if cp[4] is not None]
            for k in range(len(plans[i])):
                if k not in passed_on:
                    self._copy(refs, me, i, k, True).wait_recv()
                self._copy(refs, me, i, k, False).wait_send()
            local = self._local(refs, me, i)
            if local is not None:
                local.wait()


_ANY = pl.BlockSpec(memory_space=pl.ANY)


def _call(body, args, *, name, grid, in_specs, out_specs, out_shape, params, scratch_shapes=(), comm=None):
    in_specs, out_specs, out_shape = list(in_specs), list(out_specs), list(out_shape)
    scratch_shapes = list(scratch_shapes)
    if comm is None:
        return list(pl.pallas_call(body, name=name, grid=grid, in_specs=in_specs, out_specs=out_specs,
                                   out_shape=out_shape, scratch_shapes=scratch_shapes, compiler_params=params)(*args))
    n_in, n_out, n_scr, nc = len(in_specs), len(out_specs), len(scratch_shapes), comm.n
    n_steps = 1
    for g in grid:
        n_steps *= g

    def hosted(*refs):
        ins, c_in = refs[:n_in], refs[n_in:n_in + nc]
        outs = refs[n_in + nc:n_in + nc + n_out]
        c_out = refs[n_in + nc + n_out:n_in + 2 * nc + n_out]
        scr = refs[n_in + 2 * nc + n_out:n_in + 2 * nc + n_out + n_scr]
        sems = refs[n_in + 2 * nc + n_out + n_scr:]
        step = pl.program_id(0)
        for d in range(1, len(grid)):
            step = step * grid[d] + pl.program_id(d)
        c_refs = (c_in, c_out, sems)

        @pl.when(step == 0)
        def _():
            comm.barrier()
            comm.start(c_refs)

        if n_steps >= 3:
            @pl.when(step == n_steps - 2)
            def _():
                comm.forward(c_refs)

        body(*ins, *outs, *scr)

        @pl.when(step == n_steps - 1)
        def _():
            if n_steps < 3:
                comm.forward(c_refs)
            comm.finish(c_refs)

    res = pl.pallas_call(
        hosted, name=name, grid=grid, in_specs=in_specs + [_ANY] * nc, out_specs=out_specs + [_ANY] * nc,
        out_shape=out_shape + comm.out_shape(), scratch_shapes=scratch_shapes + comm.scratch(),
        compiler_params=dataclasses.replace(params, collective_id=comm.collective_id()))(*args, *comm.arrs)
    comm.out = list(res[n_out:])
    return list(res[:n_out])


def _exchange(items, *, name):
    comm = _Comm(items)

    def body(*refs):
        r = (refs[:comm.n], refs[comm.n:2 * comm.n], refs[2 * comm.n:])
        comm.barrier()
        comm.start(r)
        comm.forward(r)
        comm.finish(r)

    return list(pl.pallas_call(body, name=name, out_shape=comm.out_shape(), in_specs=[_ANY] * comm.n,
                               out_specs=[_ANY] * comm.n, scratch_shapes=comm.scratch(),
                               compiler_params=pltpu.CompilerParams(collective_id=comm.collective_id()))(*comm.arrs))


class _ModVec:
    def __init__(self, arr, idx):
        self.arr, self.idx = arr, idx

    def spec(self, tps, n_axes):
        idx, blk = self.idx, (1, 1, self.arr.shape[2])
        if n_axes == 1:
            return pl.BlockSpec(blk, lambda i: (i // tps * N_MOD + idx, 0, 0))
        return pl.BlockSpec(blk, lambda i, j: (i // tps * N_MOD + idx, 0, 0))


def _norm_mod_matmul(x, gn, sh, sc, wts, *, seq, tm, tn, name, comm=None):
    T, D = x.shape
    N = wts[0].shape[0]
    nw = len(wts)
    tps = seq // tm

    def body(x_ref, gn_ref, sh_ref, sc_ref, *rest):
        w_refs, h_ref, o_refs = rest[:nw], rest[nw], rest[nw + 1:]

        @pl.when(pl.program_id(1) == 0)
        def _():
            h_ref[...] = _norm_mod(x_ref[...], gn_ref[...], sh_ref[0], sc_ref[0])[0].astype(BF16)

        h = h_ref[...]
        for w_ref, o_ref in zip(w_refs, o_refs):
            o_ref[...] = _dot_nt(h, w_ref[...]).astype(o_ref.dtype)

    row = pl.BlockSpec((tm, D), lambda i, j: (i, 0))
    vec = pl.BlockSpec((1, D), lambda i, j: (0, 0))
    per_b = pl.BlockSpec((1, 1, D), lambda i, j: (i // tps, 0, 0))
    wspec = pl.BlockSpec((tn, D), lambda i, j: (j, 0))
    ospec = pl.BlockSpec((tm, tn), lambda i, j: (i, j))
    blocks = [((tm, D), F32), ((tm, D), BF16)] + [((tn, D), BF16), ((tm, tn), BF16)] * nw
    outs = _call(
        body, (x, gn, sh.arr, sc.arr, *wts), name=name, grid=(T // tm, N // tn),
        in_specs=[row, vec, sh.spec(tps, 2), sc.spec(tps, 2)] + [wspec] * nw,
        out_specs=[row] + [ospec] * nw,
        out_shape=[SDS((T, D), BF16)] + [SDS((T, N), BF16)] * nw,
        params=_params(2, blocks, temp_bytes=2 * _nbytes((tm, tn), F32) + 3 * _nbytes((tm, D), F32)), comm=comm)
    return outs[0], outs[1:]


def _matmul_nt(h, w, *, tm, tn, name, comm=None):
    T, D = h.shape
    N = w.shape[0]

    def body(h_ref, w_ref, o_ref):
        o_ref[...] = _dot_nt(h_ref[...], w_ref[...]).astype(o_ref.dtype)

    blocks = [((tm, D), BF16), ((tn, D), BF16), ((tm, tn), BF16)]
    return _call(
        body, (h, w), name=name, grid=(T // tm, N // tn),
        in_specs=[pl.BlockSpec((tm, D), lambda i, j: (i, 0)), pl.BlockSpec((tn, D), lambda i, j: (j, 0))],
        out_specs=[pl.BlockSpec((tm, tn), lambda i, j: (i, j))],
        out_shape=[SDS((T, N), BF16)],
        params=_params(2, blocks, temp_bytes=2 * _nbytes((tm, tn), F32)), comm=comm)[0]


def _ffn_down(a, b, wd, x, g, *, seq, tm, name, comm=None):
    T, F = a.shape
    D = wd.shape[1]
    tps = seq // tm

    def body(a_ref, b_ref, wd_ref, x_ref, g_ref, xo_ref, y_ref):
        af = a_ref[...].astype(F32)
        act = (af * _sigmoid(af) * b_ref[...].astype(F32)).astype(BF16)
        y = _dot(act, wd_ref[...])
        xo_ref[...] = x_ref[...] + (FFN_RESIDUAL * g_ref[0]) * y
        y_ref[...] = y.astype(BF16)

    wide = pl.BlockSpec((tm, F), lambda i: (i, 0))
    row = pl.BlockSpec((tm, D), lambda i: (i, 0))
    per_b = pl.BlockSpec((1, 1, D), lambda i: (i // tps, 0, 0))
    wspec = pl.BlockSpec((F, D), lambda i: (0, 0))
    blocks = [((tm, F), BF16)] * 2 + [((F, D), BF16), ((tm, D), F32), ((tm, D), F32), ((tm, D), BF16)]
    return _call(
        body, (a, b, wd, x, g.arr), name=name, grid=(T // tm,),
        in_specs=[wide, wide, wspec, row, g.spec(tps, 1)], out_specs=[row, row],
        out_shape=[SDS((T, D), F32), SDS((T, D), BF16)],
        params=_params(1, blocks, temp_bytes=3 * _nbytes((tm, F), F32)), comm=comm)


def _final_loss(x, gf, tgt, *, tm, name):
    T, D = x.shape
    nt = T // tm

    def body(x_ref, gf_ref, t_ref, dx_ref, loss_ref, dgf_ref, lacc):
        i = pl.program_id(0)
        xf = x_ref[...]
        gfv = gf_ref[...]
        rstd = lax.rsqrt(jnp.mean(xf * xf, axis=-1, keepdims=True) + EPS)
        xhat = xf * rstd
        err = xhat * gfv - t_ref[...]
        dy = err * (1.0 / D)
        dxhat = dy * gfv
        dx_ref[...] = (rstd * (dxhat - xhat * jnp.mean(dxhat * xhat, axis=-1, keepdims=True))).astype(dx_ref.dtype)
        _acc(dgf_ref, _rowsum(dy * xhat), i == 0)
        _acc(lacc, _rowsum(err * err), i == 0)

        @pl.when(i == nt - 1)
        def _():
            loss_ref[...] = jnp.broadcast_to((0.5 / D) * jnp.sum(lacc[...]), loss_ref.shape)

    row = pl.BlockSpec((tm, D), lambda i: (i, 0))
    vec = pl.BlockSpec((1, D), lambda i: (0, 0))
    lspec = pl.BlockSpec((1, 128), lambda i: (0, 0))
    blocks = [((tm, D), F32)] * 3
    return _call(
        body, (x, gf, tgt), name=name, grid=(nt,),
        in_specs=[row, vec, row], out_specs=[row, lspec, vec],
        out_shape=[SDS((T, D), GRAD_STREAM), SDS((1, 128), F32), SDS((1, D), F32)],
        scratch_shapes=[pltpu.VMEM((1, D), F32)],
        params=_params(1, blocks, temp_bytes=4 * _nbytes((tm, D), F32)))


def _ffn_bwd_down(dxo, g, y, wd, a, b, *, seq, tm, tn, name, comm=None):
    T, F = a.shape
    D = wd.shape[1]
    tps = seq // tm
    nb = T // seq

    def body(dxo_ref, g_ref, y_ref, wd_ref, a_ref, b_ref, dyb_ref, da_ref, db_ref, act_ref, dg_ref):
        i = pl.program_id(0)

        @pl.when(pl.program_id(1) == 0)
        def _():
            dx = dxo_ref[...].astype(F32)
            dyb_ref[...] = ((FFN_RESIDUAL * g_ref[0]) * dx).astype(BF16)
            part = _rowsum(FFN_RESIDUAL * dx * y_ref[...].astype(F32))
            _acc(dg_ref, part[None], i % tps == 0)

        dact = _dot_nt(dyb_ref[...], wd_ref[...])
        af = a_ref[...].astype(F32)
        bf = b_ref[...].astype(F32)
        sg = _sigmoid(af)
        silu = af * sg
        act_ref[...] = (silu * bf).astype(BF16)
        da_ref[...] = (dact * bf * (sg + silu * (1.0 - sg))).astype(BF16)
        db_ref[...] = (dact * silu).astype(BF16)

    row = pl.BlockSpec((tm, D), lambda i, j: (i, 0))
    per_b = pl.BlockSpec((1, 1, D), lambda i, j: (i // tps, 0, 0))
    wspec = pl.BlockSpec((tn, D), lambda i, j: (j, 0))
    chunk = pl.BlockSpec((tm, tn), lambda i, j: (i, j))
    blocks = [((tm, D), F32), ((tm, D), BF16), ((tn, D), BF16), ((tm, D), BF16)] + [((tm, tn), BF16)] * 5
    return _call(
        body, (dxo, g.arr, y, wd, a, b), name=name, grid=(T // tm, F // tn),
        in_specs=[row, g.spec(tps, 2), row, wspec, chunk, chunk],
        out_specs=[row, chunk, chunk, chunk, per_b],
        out_shape=[SDS((T, D), BF16)] + [SDS((T, F), BF16)] * 3 + [SDS((nb, 1, D), F32)],
        params=_params(2, blocks, temp_bytes=6 * _nbytes((tm, tn), F32)), comm=comm)


def _matmul_norm_mod_bwd(ds, ws, x, gn, sc, dxo, *, seq, tm, name, out_dtype, comm=None):
    T, D = x.shape
    nk = len(ws)
    sizes = [len(g) for g in ds]
    ds = [d for g in ds for d in g]
    tps = seq // tm
    nb = T // seq

    def body(*refs):
        w_refs = refs[len(ds):len(ds) + nk]
        x_ref, gn_ref, sc_ref, dxo_ref, dxi_ref, dsh_ref, dsc_ref, dgn_ref = refs[len(ds) + nk:]
        i = pl.program_id(0)
        dh, at = None, 0
        for n, w_ref in zip(sizes, w_refs):
            pieces = [r[...] for r in refs[at:at + n]]
            at += n
            part = _dot(pieces[0] if n == 1 else jnp.concatenate(pieces, axis=1), w_ref[...])
            dh = part if dh is None else dh + part
        gnv = gn_ref[...]
        scv = sc_ref[0]
        _, xhat, rstd, yn = _norm_mod(x_ref[...], gnv, 0.0, scv)
        dyn = dh * (1.0 + scv)
        dxhat = dyn * gnv
        dxi_ref[...] = (dxo_ref[...].astype(F32)
                        + rstd * (dxhat - xhat * jnp.mean(dxhat * xhat, axis=-1, keepdims=True))).astype(out_dtype)
        first_of_seq = i % tps == 0
        _acc(dsh_ref, _rowsum(dh)[None], first_of_seq)
        _acc(dsc_ref, _rowsum(dh * yn)[None], first_of_seq)
        _acc(dgn_ref, _rowsum(dyn * xhat), i == 0)

    row = pl.BlockSpec((tm, D), lambda i: (i, 0))
    vec = pl.BlockSpec((1, D), lambda i: (0, 0))
    per_b = pl.BlockSpec((1, 1, D), lambda i: (i // tps, 0, 0))
    d_specs = [pl.BlockSpec((tm, d.shape[1]), lambda i: (i, 0)) for d in ds]
    w_specs = [pl.BlockSpec(w.shape, lambda i: (0, 0)) for w in ws]
    blocks = ([((tm, d.shape[1]), BF16) for d in ds] + [(w.shape, BF16) for w in ws] + [((tm, D), F32)] * 3)
    return _call(
        body, (*ds, *ws, x, gn, sc.arr, dxo), name=name, grid=(T // tm,),
        in_specs=d_specs + w_specs + [row, vec, sc.spec(tps, 1), row],
        out_specs=[row, per_b, per_b, vec],
        out_shape=[SDS((T, D), out_dtype), SDS((nb, 1, D), F32), SDS((nb, 1, D), F32), SDS((1, D), F32)],
        params=_params(1, blocks, temp_bytes=6 * _nbytes((tm, D), F32)), comm=comm)


def _layernorm_silu(yc, lg, lb):
    mu = jnp.mean(yc, axis=-1, keepdims=True)
    cen = yc - mu
    rstd = lax.rsqrt(jnp.mean(cen * cen, axis=-1, keepdims=True) + EPS)
    xh = cen * rstd
    l = xh * lg + lb
    s = _sigmoid(l)
    return l * s, xh, rstd, l, s


GATE_W = 256


def _gate_specs(tm, D, col):
    return [pl.BlockSpec((tm, GATE_W), lambda i, blk=col // GATE_W + t: (i, blk)) for t in range(D // GATE_W)]


def _gate(refs):
    return jnp.concatenate([r[...] for r in refs], axis=1).astype(F32)


def _mix_out(ao, yc, proj, wao, wco, wout, x1, g2, lg, lb, *, seq, tm, ga_col, gc_col, name, comm=None):
    T, D = x1.shape
    tps = seq // tm
    ng = D // GATE_W

    def body(ao_ref, yc_ref, *rest):
        ga_refs, gc_refs = rest[:ng], rest[ng:2 * ng]
        (wao_ref, wco_ref, wout_ref, x1_ref, g2_ref, lg_ref, lb_ref,
         x2_ref, z_ref, ya_ref, ycv_ref, cact_ref, mrg_ref) = rest[2 * ng:]
        ya = _dot(ao_ref[...], wao_ref[...])
        cact = _layernorm_silu(yc_ref[...], lg_ref[...], lb_ref[...])[0].astype(BF16)
        ycv = _dot(cact, wco_ref[...])
        merged = (_sigmoid(_gate(ga_refs)) * ya + _sigmoid(_gate(gc_refs)) * ycv).astype(BF16)
        z = _dot(merged, wout_ref[...])
        x2_ref[...] = x1_ref[...] + g2_ref[0] * z
        z_ref[...] = z.astype(BF16)
        ya_ref[...] = ya.astype(BF16)
        ycv_ref[...] = ycv.astype(BF16)
        cact_ref[...] = cact
        mrg_ref[...] = merged

    row = pl.BlockSpec((tm, D), lambda i: (i, 0))
    vec = pl.BlockSpec((1, D), lambda i: (0, 0))
    per_b = pl.BlockSpec((1, 1, D), lambda i: (i // tps, 0, 0))
    wspec = pl.BlockSpec((D, D), lambda i: (0, 0))
    gates = _gate_specs(tm, D, ga_col) + _gate_specs(tm, D, gc_col)
    blocks = ([((tm, D), BF16), ((tm, D), F32), ((tm, D), BF16), ((tm, D), BF16)] + [((D, D), BF16)] * 3
              + [((tm, D), F32)] * 2 + [((tm, D), BF16)] * 5)
    return _call(
        body, (ao, yc, *[proj] * (2 * ng), wao, wco, wout, x1, g2.arr, lg, lb), name=name, grid=(T // tm,),
        in_specs=[row, row, *gates, wspec, wspec, wspec, row, g2.spec(tps, 1), vec, vec],
        out_specs=[row] * 6,
        out_shape=[SDS((T, D), F32)] + [SDS((T, D), BF16)] * 5,
        params=_params(1, blocks, temp_bytes=8 * _nbytes((tm, D), F32)), comm=comm)


def _mix_out_bwd(dx2, g2, z, wout, proj, ya, ycv, wao, wco, yc, lg, lb, *, seq, tm, ga_col, gc_col, name,
                 comm=None):
    T, D = dx2.shape
    tps = seq // tm
    nb = T // seq
    ng = D // GATE_W

    def body(dx2_ref, g2_ref, z_ref, wout_ref, *rest):
        ga_refs, gc_refs = rest[:ng], rest[ng:2 * ng]
        (ya_ref, ycv_ref, wao_ref, wco_ref, yc_ref, lg_ref, lb_ref, dz_ref, dya_ref, dycv_ref, dga_ref, dgc_ref,
         dao_ref, dyc_ref, dg2_ref, dlg_ref, dlb_ref) = rest[2 * ng:]
        i = pl.program_id(0)
        dx = dx2_ref[...].astype(F32)
        _acc(dg2_ref, _rowsum(dx * z_ref[...].astype(F32))[None], i % tps == 0)
        dzb = (g2_ref[0] * dx).astype(BF16)
        dz_ref[...] = dzb
        dmerged = _dot_nt(dzb, wout_ref[...])
        sa = _sigmoid(_gate(ga_refs))
        sc_ = _sigmoid(_gate(gc_refs))
        dya = (dmerged * sa).astype(BF16)
        dycv = (dmerged * sc_).astype(BF16)
        dya_ref[...] = dya
        dycv_ref[...] = dycv
        dga_ref[...] = (dmerged * ya_ref[...].astype(F32) * (sa * (1.0 - sa))).astype(BF16)
        dgc_ref[...] = (dmerged * ycv_ref[...].astype(F32) * (sc_ * (1.0 - sc_))).astype(BF16)
        dao_ref[...] = _dot_nt(dya, wao_ref[...]).astype(BF16)
        dcact = _dot_nt(dycv, wco_ref[...])
        lgv = lg_ref[...]
        _, xh, rstd, l, s = _layernorm_silu(yc_ref[...], lgv, lb_ref[...])
        dl = dcact * (s * (1.0 + l * (1.0 - s)))
        _acc(dlb_ref, _rowsum(dl), i == 0)
        _acc(dlg_ref, _rowsum(dl * xh), i == 0)
        dxh = dl * lgv
        dyc_ref[...] = rstd * (dxh - jnp.mean(dxh, axis=-1, keepdims=True)
                               - xh * jnp.mean(dxh * xh, axis=-1, keepdims=True))

    row = pl.BlockSpec((tm, D), lambda i: (i, 0))
    vec = pl.BlockSpec((1, D), lambda i: (0, 0))
    per_b = pl.BlockSpec((1, 1, D), lambda i: (i // tps, 0, 0))
    wspec = pl.BlockSpec((D, D), lambda i: (0, 0))
    gates = _gate_specs(tm, D, ga_col) + _gate_specs(tm, D, gc_col)
    blocks = ([((tm, D), F32)] * 3 + [((tm, D), BF16)] * 11 + [((D, D), BF16)] * 3)
    return _call(
        body, (dx2, g2.arr, z, wout, *[proj] * (2 * ng), ya, ycv, wao, wco, yc, lg, lb), name=name,
        grid=(T // tm,),
        in_specs=[row, g2.spec(tps, 1), row, wspec, *gates, row, row, wspec, wspec, row, vec, vec],
        out_specs=[row] * 7 + [per_b, vec, vec],
        out_shape=[SDS((T, D), BF16)] * 6 + [SDS((T, D), F32), SDS((nb, 1, D), F32), SDS((1, D), F32),
                                             SDS((1, D), F32)],
        params=_params(1, blocks, temp_bytes=10 * _nbytes((tm, D), F32)), comm=comm)


GROUP_ROWS = GQA_GROUP * ATT_BLOCK
PAIR_W = 2 * HEAD_DIM
GROUP_W = GQA_GROUP * HEAD_DIM


def _lane_lo():
    return lax.broadcasted_iota(jnp.int32, (1, PAIR_W), 1) < HEAD_DIM


def _band_bias():
    sj = lax.broadcasted_iota(jnp.int32, (2 * ATT_BLOCK, GROUP_ROWS), 0)
    qi = lax.broadcasted_iota(jnp.int32, (2 * ATT_BLOCK, GROUP_ROWS), 1) & (ATT_BLOCK - 1)
    rel = qi + ATT_BLOCK - sj
    bias = jnp.where(jnp.logical_and(rel >= 0, rel < ATT_BLOCK), 0.0, NEG_BIG)
    sj1 = lax.broadcasted_iota(jnp.int32, (2 * ATT_BLOCK, 1), 0)
    return bias, jnp.where(sj1 < ATT_BLOCK, NEG_BIG, 0.0)


def _dup_heads(src_ref, dst, seq):
    x = src_ref[...]
    i = lax.broadcasted_iota(jnp.int32, (KV_WIDTH, PAIR_W), 0)
    j = lax.broadcasted_iota(jnp.int32, (KV_WIDTH, PAIR_W), 1) & (HEAD_DIM - 1)
    for g in range(N_KV_HEADS):
        sel = jnp.where(i == j + g * HEAD_DIM, 1.0, 0.0).astype(BF16)
        dst[g, pl.ds(0, ATT_BLOCK), :] = jnp.zeros((ATT_BLOCK, PAIR_W), BF16)
        dst[g, pl.ds(ATT_BLOCK, seq), :] = _dot(x, sel).astype(BF16)


def _stack_heads(blk, g, lo):
    parts = []
    for p in range(GQA_GROUP // 2):
        pair = blk[:, g * GROUP_W + p * PAIR_W:g * GROUP_W + (p + 1) * PAIR_W]
        parts += [jnp.where(lo, pair, jnp.zeros_like(pair)), jnp.where(lo, jnp.zeros_like(pair), pair)]
    return jnp.concatenate(parts, axis=0)


def _unstack_heads(full, ref, r0, g, lo):
    for p in range(GQA_GROUP // 2):
        even = full[(2 * p) * ATT_BLOCK:(2 * p + 1) * ATT_BLOCK, :]
        odd = full[(2 * p + 1) * ATT_BLOCK:(2 * p + 2) * ATT_BLOCK, :]
        ref[pl.ds(r0, ATT_BLOCK), g * GROUP_W + p * PAIR_W:g * GROUP_W + (p + 1) * PAIR_W] = (
            jnp.where(lo, even, odd).astype(ref.dtype))


def _sink_row(sink_ref, g):
    return jnp.concatenate([jnp.full((1, ATT_BLOCK), sink_ref[0, g * GQA_GROUP + h], F32)
                            for h in range(GQA_GROUP)], axis=1)


def _group_probs(qs, k2, bias, sink):
    s = _dot_nt(k2, qs) * (HEAD_DIM ** -0.5) + bias
    m = jnp.maximum(jnp.max(s, axis=0, keepdims=True), sink)
    p = jnp.exp(s - m)
    psink = jnp.exp(sink - m)
    inv = 1.0 / (jnp.sum(p, axis=0, keepdims=True) + psink)
    return p * inv, psink * inv


def _attn_fwd(projp, sinks, *, seq, q_blk, k_blk, v_blk, name, comm=None):
    T = projp.shape[0]
    QW = N_Q_HEADS * HEAD_DIM
    nblk = seq // ATT_BLOCK

    def body(q_ref, k_ref, v_ref, sink_ref, o_ref, k2s, v2s):
        _dup_heads(k_ref, k2s, seq)
        _dup_heads(v_ref, v2s, seq)
        lo = _lane_lo()
        bias0, first_pen = _band_bias()
        sink_rows = [_sink_row(sink_ref, g) for g in range(N_KV_HEADS)]

        def blk(n, carry):
            r0 = pl.multiple_of(n * ATT_BLOCK, ATT_BLOCK)
            qb = q_ref[pl.ds(r0, ATT_BLOCK), :]
            bias = bias0 + jnp.where(n == 0, 1.0, 0.0) * first_pen
            for g in range(N_KV_HEADS):
                probs_t, _ = _group_probs(_stack_heads(qb, g, lo), k2s[g, pl.ds(r0, 2 * ATT_BLOCK), :], bias,
                                          sink_rows[g])
                _unstack_heads(_dot_tn(probs_t.astype(BF16), v2s[g, pl.ds(r0, 2 * ATT_BLOCK), :]), o_ref, r0, g, lo)
            return carry

        lax.fori_loop(0, nblk, blk, 0)

    blocks = [((seq, QW), BF16)] * 2 + [((seq, KV_WIDTH), BF16)] * 2
    return _call(
        body, (projp, projp, projp, sinks), name=name, grid=(T // seq,),
        in_specs=[pl.BlockSpec((seq, QW), lambda b: (b, q_blk)),
                  pl.BlockSpec((seq, KV_WIDTH), lambda b: (b, k_blk)),
                  pl.BlockSpec((seq, KV_WIDTH), lambda b: (b, v_blk)),
                  pl.BlockSpec(memory_space=pltpu.SMEM)],
        out_specs=[pl.BlockSpec((seq, QW), lambda b: (b, 0))],
        out_shape=[SDS((T, QW), BF16)],
        scratch_shapes=[pltpu.VMEM((N_KV_HEADS, seq + ATT_BLOCK, PAIR_W), BF16)] * 2,
        params=_params(1, blocks, temp_bytes=16 * 2**20), comm=comm)[0]


def _attn_bwd(projp, dao, sinks, *, seq, q_blk, k_blk, v_blk, name, comm=None):
    T = projp.shape[0]
    QW = N_Q_HEADS * HEAD_DIM
    nblk = seq // ATT_BLOCK

    def body(q_ref, k_ref, v_ref, do_ref, sink_ref, dq_ref, dk_ref, dv_ref, dsink_ref, k2s, v2s, dkacc, dvacc):
        _dup_heads(k_ref, k2s, seq)
        _dup_heads(v_ref, v2s, seq)
        dkacc[...] = jnp.zeros(dkacc.shape, F32)
        dvacc[...] = jnp.zeros(dvacc.shape, F32)
        lane = lax.broadcasted_iota(jnp.int32, (1, PAIR_W), 1)
        lo = lane < HEAD_DIM
        bias0, first_pen = _band_bias()
        sink_rows = [_sink_row(sink_ref, g) for g in range(N_KV_HEADS)]

        def blk(n, dsink):
            r0 = pl.multiple_of(n * ATT_BLOCK, ATT_BLOCK)
            band = pl.ds(r0, 2 * ATT_BLOCK)
            qb = q_ref[pl.ds(r0, ATT_BLOCK), :]
            dob = do_ref[pl.ds(r0, ATT_BLOCK), :]
            bias = bias0 + jnp.where(n == 0, 1.0, 0.0) * first_pen
            for g in range(N_KV_HEADS):
                qs = _stack_heads(qb, g, lo)
                dos = _stack_heads(dob, g, lo)
                k2 = k2s[g, band, :]
                v2 = v2s[g, band, :]
                probs_t, psink = _group_probs(qs, k2, bias, sink_rows[g])
                dp_t = _dot_nt(v2, dos)
                delta = jnp.sum(probs_t * dp_t, axis=0, keepdims=True)
                ds_t = (probs_t * (dp_t - delta) * (HEAD_DIM ** -0.5)).astype(BF16)
                tsink = psink * delta
                for h in range(GQA_GROUP):
                    dsink = dsink + jnp.where(lane == g * GQA_GROUP + h,
                                              -jnp.sum(tsink[:, h * ATT_BLOCK:(h + 1) * ATT_BLOCK]), 0.0)
                _unstack_heads(_dot_tn(ds_t, k2), dq_ref, r0, g, lo)
                dkacc[g, band, :] = dkacc[g, band, :] + _dot(ds_t, qs)
                dvacc[g, band, :] = dvacc[g, band, :] + _dot(probs_t.astype(BF16), dos)
            return dsink

        dsink = lax.fori_loop(0, nblk, blk, jnp.zeros((1, PAIR_W), F32))
        _acc(dsink_ref, dsink, pl.program_id(0) == 0)

        def fold(acc, g):
            a = acc[g, pl.ds(ATT_BLOCK, seq), :]
            return a + pltpu.roll(a, HEAD_DIM, 1)

        dk_ref[...] = jnp.where(lo, fold(dkacc, 0), fold(dkacc, 1)).astype(BF16)
        dv_ref[...] = jnp.where(lo, fold(dvacc, 0), fold(dvacc, 1)).astype(BF16)

    blocks = [((seq, QW), BF16)] * 3 + [((seq, KV_WIDTH), BF16)] * 4
    kv_spec_out = pl.BlockSpec((seq, KV_WIDTH), lambda b: (b, 0))
    return _call(
        body, (projp, projp, projp, dao, sinks), name=name, grid=(T // seq,),
        in_specs=[pl.BlockSpec((seq, QW), lambda b: (b, q_blk)),
                  pl.BlockSpec((seq, KV_WIDTH), lambda b: (b, k_blk)),
                  pl.BlockSpec((seq, KV_WIDTH), lambda b: (b, v_blk)),
                  pl.BlockSpec((seq, QW), lambda b: (b, 0)),
                  pl.BlockSpec(memory_space=pltpu.SMEM)],
        out_specs=[pl.BlockSpec((seq, QW), lambda b: (b, 0)), kv_spec_out, kv_spec_out,
                   pl.BlockSpec((1, 128), lambda b: (0, 0))],
        out_shape=[SDS((T, QW), BF16), SDS((T, KV_WIDTH), BF16), SDS((T, KV_WIDTH), BF16), SDS((1, 128), F32)],
        scratch_shapes=[pltpu.VMEM((N_KV_HEADS, seq + ATT_BLOCK, PAIR_W), BF16)] * 2
        + [pltpu.VMEM((N_KV_HEADS, seq + ATT_BLOCK, PAIR_W), F32)] * 2,
        params=_params(1, blocks, temp_bytes=24 * 2**20), comm=comm)


SUBLANES = 8


def _sublane_shifts(win):
    n = CONV_ROWS + CONV_HALO
    return [win] + [pltpu.roll(win, n - b, 0) for b in range(1, SUBLANES)]


def _window(shifted, off):
    a = off // SUBLANES * SUBLANES
    return shifted[off % SUBLANES][a:a + CONV_ROWS, :]


def _conv_fwd(projp, w, bias, *, seq, cw, a_col, b_col, name, comm=None):
    T = projp.shape[0]
    C = w.shape[1]
    nchunk = seq // CONV_ROWS

    def body(a_ref, b_ref, w_ref, bias_ref, y_ref, upad):
        upad[pl.ds(0, CONV_HALO), :] = jnp.zeros((CONV_HALO, cw), F32)
        upad[pl.ds(CONV_HALO, seq), :] = a_ref[...].astype(F32) * _sigmoid(b_ref[...].astype(F32))
        wv = w_ref[...]
        bv = bias_ref[...]

        def chunk(r, carry):
            r0 = pl.multiple_of(r * CONV_ROWS, CONV_ROWS)
            shifted = _sublane_shifts(upad[pl.ds(r0, CONV_ROWS + CONV_HALO), :])
            acc = jnp.broadcast_to(bv, (CONV_ROWS, cw))
            for k in range(CONV_WIDTH):
                acc = acc + wv[k:k + 1, :] * _window(shifted, CONV_HALO - (CONV_WIDTH - 1) + k)
            y_ref[pl.ds(r0, CONV_ROWS), :] = acc
            return carry

        lax.fori_loop(0, nchunk, chunk, 0)

    blocks = [((seq, cw), BF16)] * 2 + [((seq, cw), F32)]
    return _call(
        body, (projp, projp, w, bias), name=name, grid=(T // seq, C // cw),
        in_specs=[pl.BlockSpec((seq, cw), lambda b, c: (b, a_col // cw + c)),
                  pl.BlockSpec((seq, cw), lambda b, c: (b, b_col // cw + c)),
                  pl.BlockSpec((CONV_WIDTH, cw), lambda b, c: (0, c)),
                  pl.BlockSpec((1, cw), lambda b, c: (0, c))],
        out_specs=[pl.BlockSpec((seq, cw), lambda b, c: (b, c))],
        out_shape=[SDS((T, C), F32)],
        scratch_shapes=[pltpu.VMEM((seq + CONV_HALO, cw), F32)],
        params=_params(2, blocks, temp_bytes=6 * _nbytes((seq, cw), F32)), comm=comm)[0]


def _conv_bwd(dy, projp, w, *, seq, cw, a_col, b_col, name, comm=None):
    T = projp.shape[0]
    C = w.shape[1]
    nchunk = seq // CONV_ROWS
    SUB = 8

    def body(dy_ref, a_ref, b_ref, w_ref, da_ref, db_ref, dw_ref, dbias_ref, dypad, dwp):
        first = pl.program_id(1) == 0
        dyv = dy_ref[...]
        dypad[pl.ds(0, seq), :] = dyv
        dypad[pl.ds(seq, CONV_HALO), :] = jnp.zeros((CONV_HALO, cw), F32)
        dwp[...] = jnp.zeros(dwp.shape, F32)
        wv = w_ref[...]

        def chunk(r, carry):
            r0 = pl.multiple_of(r * CONV_ROWS, CONV_ROWS)
            dy_shifts = _sublane_shifts(dypad[pl.ds(r0, CONV_ROWS + CONV_HALO), :])
            ac = a_ref[pl.ds(r0, CONV_ROWS), :].astype(F32)
            sbc = _sigmoid(b_ref[pl.ds(r0, CONV_ROWS), :].astype(F32))
            uc = ac * sbc
            du = jnp.zeros((CONV_ROWS, cw), F32)
            for k in range(CONV_WIDTH):
                dyk = _window(dy_shifts, CONV_WIDTH - 1 - k)
                du = du + wv[k:k + 1, :] * dyk
                prod = uc * dyk
                part = prod[0:SUB, :]
                for s in range(1, CONV_ROWS // SUB):
                    part = part + prod[s * SUB:(s + 1) * SUB, :]
                dwp[pl.ds(k * SUB, SUB), :] = dwp[pl.ds(k * SUB, SUB), :] + part
            da_ref[pl.ds(r0, CONV_ROWS), :] = (du * sbc).astype(BF16)
            db_ref[pl.ds(r0, CONV_ROWS), :] = (du * ac * (sbc * (1.0 - sbc))).astype(BF16)
            return carry

        lax.fori_loop(0, nchunk, chunk, 0)

        @pl.when(first)
        def _():
            dw_ref[...] = jnp.zeros(dw_ref.shape, F32)
            dbias_ref[...] = jnp.zeros(dbias_ref.shape, F32)

        for k in range(CONV_WIDTH):
            dw_ref[k:k + 1, :] = dw_ref[k:k + 1, :] + _rowsum(dwp[pl.ds(k * SUB, SUB), :])
        dbias_ref[...] = dbias_ref[...] + _rowsum(dyv)

    blocks = [((seq, cw), F32)] + [((seq, cw), BF16)] * 4
    return _call(
        body, (dy, projp, projp, w), name=name, grid=(C // cw, T // seq),
        in_specs=[pl.BlockSpec((seq, cw), lambda c, b: (b, c)),
                  pl.BlockSpec((seq, cw), lambda c, b: (b, a_col // cw + c)),
                  pl.BlockSpec((seq, cw), lambda c, b: (b, b_col // cw + c)),
                  pl.BlockSpec((CONV_WIDTH, cw), lambda c, b: (0, c))],
        out_specs=[pl.BlockSpec((seq, cw), lambda c, b: (b, c)), pl.BlockSpec((seq, cw), lambda c, b: (b, c)),
                   pl.BlockSpec((CONV_WIDTH, cw), lambda c, b: (0, c)), pl.BlockSpec((1, cw), lambda c, b: (0, c))],
        out_shape=[SDS((T, C), BF16), SDS((T, C), BF16), SDS((CONV_WIDTH, C), F32), SDS((1, C), F32)],
        scratch_shapes=[pltpu.VMEM((seq + CONV_HALO, cw), F32), pltpu.VMEM((CONV_WIDTH * SUB, cw), F32)],
        params=_params(2, blocks, temp_bytes=8 * _nbytes((seq, cw), F32)), comm=comm)


def _matmul_tn(a, b, *, name, comm=None):
    T, M = a.shape
    N = b.shape[1]
    bm = _pick(M, (768, 512, 256))

    def body(a_ref, b_ref, o_ref):
        o_ref[...] = _dot_tn(a_ref[...], b_ref[...]).astype(BF16)

    blocks = [((T, bm), BF16), ((T, N), BF16), ((bm, N), BF16)]
    return _call(
        body, (a, b), name=name, grid=(M // bm,),
        in_specs=[pl.BlockSpec((T, bm), lambda i: (0, i)), pl.BlockSpec((T, N), lambda i: (0, 0))],
        out_specs=[pl.BlockSpec((bm, N), lambda i: (i, 0))],
        out_shape=[SDS((M, N), BF16)],
        params=_params(1, blocks, temp_bytes=2 * _nbytes((T, bm), BF16) + 2 * _nbytes((bm, N), F32)),
        comm=comm)[0]


TN_BLOCK = 256


def _matmul_tn_pieces(groups, b, *, name, comm=None):
    T, N = b.shape
    flat = [a for g in groups for a in g]
    starts, n_steps = [], 0
    for g in groups:
        width = sum(a.shape[1] for a in g)
        assert width % TN_BLOCK == 0 and (len(g) == 1 or width == TN_BLOCK), [a.shape for a in g]
        starts.append(n_steps)
        n_steps += width // TN_BLOCK

    def body(*refs):
        a_refs, b_ref, o_ref = refs[:len(flat)], refs[len(flat)], refs[len(flat) + 1]
        i = pl.program_id(0)
        at = 0
        for g, start in zip(groups, starts):
            mine = a_refs[at:at + len(g)]
            at += len(g)
            steps = sum(a.shape[1] for a in g) // TN_BLOCK

            @pl.when(jnp.logical_and(i >= start, i < start + steps))
            def _(mine=mine):
                a = mine[0][...] if len(mine) == 1 else jnp.concatenate([r[...] for r in mine], axis=1)
                o_ref[...] = _dot_tn(a, b_ref[...]).astype(BF16)

    a_specs = []
    for g, start in zip(groups, starts):
        for a in g:
            if len(g) == 1:
                last = a.shape[1] // TN_BLOCK - 1
                a_specs.append(pl.BlockSpec(
                    (T, TN_BLOCK), lambda i, start=start, last=last: (0, jnp.clip(i - start, 0, last))))
            else:
                a_specs.append(pl.BlockSpec((T, a.shape[1]), lambda i: (0, 0)))
    blocks = [((T, TN_BLOCK), BF16)] * len(flat) + [((T, N), BF16), ((TN_BLOCK, N), BF16)]
    return _call(
        body, (*flat, b), name=name, grid=(n_steps,),
        in_specs=a_specs + [pl.BlockSpec((T, N), lambda i: (0, 0))],
        out_specs=[pl.BlockSpec((TN_BLOCK, N), lambda i: (i, 0))],
        out_shape=[SDS((n_steps * TN_BLOCK, N), BF16)],
        params=_params(1, blocks, temp_bytes=2 * _nbytes((T, TN_BLOCK), BF16) + 2 * _nbytes((TN_BLOCK, N), F32)),
        comm=comm)[0]


def _sum_parts(p_ref):
    g = p_ref[0].astype(F32)
    for s in range(1, p_ref.shape[0]):
        g = g + p_ref[s].astype(F32)
    return g


def _pair_add(g, staged, *, name):
    _, R, W = g.shape
    nq = staged.shape[0]
    tr = _row_tile(R)

    def body(g_ref, s_ref, o_ref):
        mine = jnp.where(lax.axis_index("c") == 0, g_ref[0, 0].astype(F32), g_ref[0, 1].astype(F32))
        o_ref[0] = (mine + s_ref[0].astype(F32)).astype(o_ref.dtype)

    return _call(
        body, (g.reshape(nq, 2, R, W), staged), name=name, grid=(nq, R // tr),
        in_specs=[pl.BlockSpec((1, 2, tr, W), lambda q, i: (q, 0, i, 0)),
                  pl.BlockSpec((1, tr, W), lambda q, i: (q, i, 0))],
        out_specs=[pl.BlockSpec((1, tr, W), lambda q, i: (q, i, 0))],
        out_shape=[SDS((nq, R, W), g.dtype)],
        params=_params(2, [((4, tr, W), g.dtype)], temp_bytes=3 * _nbytes((tr, W), F32)))[0]


def _adamw_update(w, g, m, v):
    m = ADAM_B1 * m + (1.0 - ADAM_B1) * g
    v = ADAM_B2 * v + (1.0 - ADAM_B2) * (g * g)
    m_hat = m / (1.0 - ADAM_B1 ** ADAM_STEP)
    v_hat = v / (1.0 - ADAM_B2 ** ADAM_STEP)
    delta = -ADAM_LR * (m_hat / (jnp.sqrt(v_hat) + ADAM_EPS) + ADAM_WD * w)
    return delta, m, v


def _row_tile(R):
    return _pick(R, (256, 128, 112, 88, 64, 32, 16, 8))


def _sum8(parts, *, name):
    n, R, W = parts.shape
    tr = _row_tile(R)

    def body(p_ref, o_ref):
        o_ref[...] = _sum_parts(p_ref)

    return _call(
        body, (parts,), name=name, grid=(R // tr,),
        in_specs=[pl.BlockSpec((n, tr, W), lambda i: (0, i, 0))],
        out_specs=[pl.BlockSpec((tr, W), lambda i: (i, 0))],
        out_shape=[SDS((R, W), F32)],
        params=_params(1, [((n, tr, W), parts.dtype), ((tr, W), F32)]))[0]


def _adamw(g, w, m, v, *, name):
    R, W = w.shape
    tr = _row_tile(R)

    def body(g_ref, w_ref, m_ref, v_ref, d_ref, mo_ref, vo_ref):
        d_ref[...], mo_ref[...], vo_ref[...] = _adamw_update(w_ref[...], g_ref[...], m_ref[...], v_ref[...])

    spec = pl.BlockSpec((tr, W), lambda i: (i, 0))
    return _call(
        body, (g, w, m, v), name=name, grid=(R // tr,),
        in_specs=[spec] * 4, out_specs=[spec] * 3, out_shape=[SDS((R, W), F32)] * 3,
        params=_params(1, [((tr, W), F32)] * 7))


def _sum8_adamw(parts, w, m, v, *, name):
    R, W = w.shape
    n = parts.shape[0]
    tr = _row_tile(R)

    def body(p_ref, w_ref, m_ref, v_ref, g_ref, d_ref, mo_ref, vo_ref):
        g = _sum_parts(p_ref)
        g_ref[...] = g
        d_ref[...], mo_ref[...], vo_ref[...] = _adamw_update(w_ref[...], g, m_ref[...], v_ref[...])

    spec = pl.BlockSpec((tr, W), lambda i: (i, 0))
    return _call(
        body, (parts, w, m, v), name=name, grid=(R // tr,),
        in_specs=[pl.BlockSpec((n, tr, W), lambda i: (0, i, 0))] + [spec] * 3,
        out_specs=[spec] * 4, out_shape=[SDS((R, W), F32)] * 4,
        params=_params(1, [((n, tr, W), parts.dtype)] + [((tr, W), F32)] * 7))


def _ada_fwd(c_all, w, bias, *, name):
    NB, D = c_all.shape
    N = w.shape[1]

    def body(c_ref, w_ref, b_ref, o_ref):
        cv = c_ref[...]
        ca = (cv * _sigmoid(cv)).astype(BF16)
        o_ref[...] = _dot(ca, w_ref[...].astype(BF16)) + b_ref[...]

    full = lambda s: pl.BlockSpec(s, lambda i: (0,) * len(s))
    return _call(
        body, (c_all, w, bias), name=name, grid=(1,),
        in_specs=[full((NB, D)), full((D, N)), full((1, N))], out_specs=[full((NB, N))],
        out_shape=[SDS((NB, N), F32)],
        params=_params(1, [((D, N), F32)], temp_bytes=_nbytes((D, N), BF16)))[0]


def _ada_bwd(c_all, gmod_all, *, n_col, name):
    NB, D = c_all.shape
    N = gmod_all.shape[1]

    def body(c_ref, g_ref, gw_ref, gb_ref):
        cv = c_ref[...]
        ca = (cv * _sigmoid(cv)).astype(BF16)
        first = pl.multiple_of(_lin(_my_pos()) * n_col, 128)
        gw_ref[...] = _dot_tn(ca, g_ref[:, pl.ds(first, n_col)].astype(BF16))
        gb_ref[...] = _rowsum(g_ref[...])

    full = lambda s: pl.BlockSpec(s, lambda i: (0,) * len(s))
    return _call(
        body, (c_all, gmod_all), name=name, grid=(1,),
        in_specs=[full((NB, D)), full((NB, N))], out_specs=[full((D, n_col)), full((1, N))],
        out_shape=[SDS((D, n_col), F32), SDS((1, N), F32)],
        params=_params(1, [((D, n_col), F32), ((NB, N), F32)]))


def kernel(x, c, w_ada, b_ada, norm_ffn1_g, ffn1_w_gate, ffn1_w_up, ffn1_w_down, norm_mix_g, w_in, attn_sinks, w_attn_o, conv_w_dw, conv_b_dw, conv_ln_g, conv_ln_b, w_conv_o, w_out, norm_ffn2_g, ffn2_w_gate, ffn2_w_up, ffn2_w_down, final_norm_g, loss_target, m_w_ada, m_b_ada, m_norm_ffn1_g, m_ffn1_w_gate, m_ffn1_w_up, m_ffn1_w_down, m_norm_mix_g, m_w_in, m_attn_sinks, m_w_attn_o, m_conv_w_dw, m_conv_b_dw, m_conv_ln_g, m_conv_ln_b, m_w_conv_o, m_w_out, m_norm_ffn2_g, m_ffn2_w_gate, m_ffn2_w_up, m_ffn2_w_down, m_final_norm_g, v_w_ada, v_b_ada, v_norm_ffn1_g, v_ffn1_w_gate, v_ffn1_w_up, v_ffn1_w_down, v_norm_mix_g, v_w_in, v_attn_sinks, v_w_attn_o, v_conv_w_dw, v_conv_b_dw, v_conv_ln_g, v_conv_ln_b, v_w_conv_o, v_w_out, v_norm_ffn2_g, v_ffn2_w_gate, v_ffn2_w_up, v_ffn2_w_down, v_final_norm_g):
    B, S, D = x.shape
    T = B * S
    QW = N_Q_HEADS * HEAD_DIM
    CC = conv_w_dw.shape[2] * N_DEV
    me = _lin(_my_pos())
    xf = x.reshape(T, D)
    tgt = loss_target.reshape(T, D)
    tm = min(512, S)
    kw = dict(seq=S, tm=tm)

    p_k, p_v, p_ca = QW, QW + KV_WIDTH, QW + 2 * KV_WIDTH
    p_cb, p_ga, p_gc = p_ca + CC, p_ca + 2 * CC, p_ca + 2 * CC + D

    def col_t(w):
        return w[0].T.astype(BF16)

    def row_b(w):
        return w[0].astype(BF16)

    def rows(g):
        return g.reshape(-1, g.shape[-1])

    def blocks8(g):
        return g.reshape(N_DEV, g.shape[0] // N_DEV, g.shape[1])

    def gather(*arrs):
        return _Comm([(a, "gather") for a in arrs])

    g_wg1, g_convw, g_c = _exchange(
        [(col_t(ffn1_w_gate), "gather"), (conv_w_dw[0], "gather"), (c, "gather")], name="gather_first")
    wg1 = rows(g_wg1)
    conv_w = g_convw.transpose(1, 0, 2).reshape(CONV_WIDTH, CC)
    c_all = g_c.reshape(N_DEV * B, D)

    n_col = N_MOD * D // N_DEV
    b_cols = lax.dynamic_slice(b_ada, (0, me * n_col), (1, n_col))
    mod_cols = _ada_fwd(c_all, w_ada[0], b_cols, name="ada_fwd")
    mod_mine = _exchange([(mod_cols.reshape(N_DEV, B, n_col), "scatter")], name="scatter_mod")[0]
    mod = mod_mine.transpose(1, 0, 2).reshape(B * N_MOD, 1, D)
    sh1, sc1, g1, sh2, sc2, g2, sh3, sc3, g3 = [_ModVec(mod, i) for i in range(N_MOD)]

    F = wg1.shape[0]
    tn_f = _pick(F, (1408, 1024, 512, 256))
    tn_in = _pick(w_in.shape[2] * N_DEV, (1792, 768, 512, 256))
    gate_blk = dict(ga_col=p_ga, gc_col=p_gc)
    att_blk = dict(q_blk=0, k_blk=p_k // KV_WIDTH, v_blk=p_v // KV_WIDTH)
    conv_kw = dict(seq=S, cw=256, a_col=p_ca, b_col=p_cb)

    cm = gather(col_t(ffn1_w_up))
    h1, (a1,) = _norm_mod_matmul(xf, norm_ffn1_g, sh1, sc1, [wg1], tn=tn_f, name="ffn1_gate", comm=cm, **kw)
    wu1 = rows(cm.out[0])
    cm = gather(row_b(ffn1_w_down))
    b1 = _matmul_nt(h1, wu1, tm=tm, tn=tn_f, name="ffn1_up", comm=cm)
    wd1 = rows(cm.out[0])
    cm = gather(col_t(w_in))
    x1, y1 = _ffn_down(a1, b1, wd1, xf, g1, name="ffn1_down", comm=cm, **kw)
    winp = rows(cm.out[0])
    cm = gather(row_b(w_attn_o), row_b(w_conv_o), row_b(w_out), col_t(ffn2_w_gate))
    h2, (projp,) = _norm_mod_matmul(x1, norm_mix_g, sh2, sc2, [winp], tn=tn_in, name="mix_in", comm=cm, **kw)
    wao, wco, wout, wg2 = [rows(o) for o in cm.out]
    cm = gather(col_t(ffn2_w_up))
    ao = _attn_fwd(projp, attn_sinks, seq=S, name="attn_fwd", comm=cm, **att_blk)
    wu2 = rows(cm.out[0])
    cm = gather(row_b(ffn2_w_down))
    yc = _conv_fwd(projp, conv_w, conv_b_dw, name="conv_fwd", comm=cm, **conv_kw)
    wd2 = rows(cm.out[0])
    x2, z, ya, ycv, cact, merged = _mix_out(ao, yc, projp, wao, wco, wout, x1, g2, conv_ln_g, conv_ln_b,
                                            name="mix_out", **gate_blk, **kw)
    h3, (a3, b3) = _norm_mod_matmul(x2, norm_ffn2_g, sh3, sc3, [wg2, wu2], tn=tn_f, name="ffn2_up", **kw)
    x3, y3 = _ffn_down(a3, b3, wd2, x2, g3, name="ffn2_down", **kw)
    dx3, loss_row, dgf = _final_loss(x3, final_norm_g[None], tgt, tm=tm, name="final_loss")

    parts = {}

    def pair(*gs):
        return [(blocks8(g), "pair") for g in gs]

    def cross(*rs):
        return [(r, "cross") for r in rs]

    def reduce_pairs(gs, staged, names):
        return [_pair_add(blocks8(g), s, name="pair_add_" + n) for g, s, n in zip(gs, staged, names)]

    dyb3, da3, db3, act3, dg3 = _ffn_bwd_down(dx3, g3, y3, wd2, a3, b3, tn=tn_f, name="ffn2_bwd_down", **kw)
    gwd2 = _matmul_tn(act3, dyb3, name="gw_ffn2_down")
    cm = _Comm(pair(gwd2))
    dx2, dsh3, dsc3, dgn3 = _matmul_norm_mod_bwd([[da3], [db3]], [wg2, wu2], x2, norm_ffn2_g, sc3, dx3,
                                                 name="ffn2_bwd_up", out_dtype=GRAD_STREAM, comm=cm, **kw)
    r_wd2, = reduce_pairs([gwd2], cm.out, ["ffn2_w_down"])
    cm = _Comm(cross(r_wd2))
    gwg2 = _matmul_tn(da3, h3, name="gw_ffn2_gate", comm=cm)
    parts["ffn2_w_down"], = cm.out
    cm = _Comm(pair(gwg2))
    gwu2 = _matmul_tn(db3, h3, name="gw_ffn2_up", comm=cm)
    r_wg2, = reduce_pairs([gwg2], cm.out, ["ffn2_w_gate"])

    cm = _Comm(cross(r_wg2) + pair(gwu2))
    dzb, dyab, dycb, dga, dgc, dao, dyc, dg2, dlng, dlnb = _mix_out_bwd(
        dx2, g2, z, wout, projp, ya, ycv, wao, wco, yc, conv_ln_g, conv_ln_b, name="mix_out_bwd", comm=cm,
        **gate_blk, **kw)
    parts["ffn2_w_gate"] = cm.out[0]
    r_wu2, = reduce_pairs([gwu2], cm.out[1:], ["ffn2_w_up"])
    gwout = _matmul_tn(merged, dzb, name="gw_out")
    gwao = _matmul_tn(ao, dyab, name="gw_attn_o")
    gwco = _matmul_tn(cact, dycb, name="gw_conv_o")
    cm = _Comm(cross(r_wu2) + pair(gwout, gwao, gwco))
    dq, dk, dv, dsinks = _attn_bwd(projp, dao, attn_sinks, seq=S, name="attn_bwd", comm=cm, **att_blk)
    parts["ffn2_w_up"] = cm.out[0]
    r_mix = reduce_pairs([gwout, gwao, gwco], cm.out[1:], ["w_out", "w_attn_o", "w_conv_o"])
    cm = _Comm(cross(*r_mix))
    dca, dcb, dconvw, dconvb = _conv_bwd(dyc, projp, conv_w, name="conv_bwd", comm=cm, **conv_kw)
    parts["w_out"], parts["w_attn_o"], parts["w_conv_o"] = cm.out
    gwin = _matmul_tn_pieces([[dq], [dk, dv], [dca], [dcb], [dga], [dgc]], h2, name="gw_in")
    cm = _Comm(pair(gwin))
    dx1, dsh2, dsc2, dgn2 = _matmul_norm_mod_bwd([[dq, dk, dv, dca, dcb, dga, dgc]], [winp], x1, norm_mix_g, sc2, dx2,
                                                 name="mix_in_bwd", out_dtype=GRAD_STREAM, comm=cm, **kw)
    r_win, = reduce_pairs([gwin], cm.out, ["w_in"])

    cm = _Comm(cross(r_win))
    dyb1, da1, db1, act1, dg1 = _ffn_bwd_down(dx1, g1, y1, wd1, a1, b1, tn=tn_f, name="ffn1_bwd_down", comm=cm,
                                              **kw)
    parts["w_in"], = cm.out
    gwd1 = _matmul_tn(act1, dyb1, name="gw_ffn1_down")
    cm = _Comm(pair(gwd1))
    gwg1 = _matmul_tn(da1, h1, name="gw_ffn1_gate", comm=cm)
    r_wd1, = reduce_pairs([gwd1], cm.out, ["ffn1_w_down"])
    cm = _Comm(cross(r_wd1) + pair(gwg1))
    gwu1 = _matmul_tn(db1, h1, name="gw_ffn1_up", comm=cm)
    parts["ffn1_w_down"] = cm.out[0]
    r_wg1, = reduce_pairs([gwg1], cm.out[1:], ["ffn1_w_gate"])
    r_wu1, = reduce_pairs([gwu1], _exchange(pair(gwu1), name="pair_last"), ["ffn1_w_up"])
    cm = _Comm(cross(r_wg1, r_wu1))
    dx0, dsh1, dsc1, dgn1 = _matmul_norm_mod_bwd([[da1], [db1]], [wg1, wu1], xf, norm_ffn1_g, sc1, dx1,
                                                 name="ffn1_bwd_up", out_dtype=F32, comm=cm, **kw)
    parts["ffn1_w_gate"], parts["ffn1_w_up"] = cm.out

    n_small = 8
    gmod = jnp.concatenate([dsh1, dsc1, dg1, dsh2, dsc2, dg2, dsh3, dsc3, dg3], axis=1).reshape(B, N_MOD * D)
    sink_row = jnp.pad(dsinks[:, :N_Q_HEADS], ((0, 0), (0, D - N_Q_HEADS)))
    loss_pad = jnp.pad(loss_row, ((0, 0), (0, D - loss_row.shape[1])))
    small = jnp.concatenate([dgn1, dgn2, dgn3, dgf, dconvb, dlng, dlnb, sink_row, dconvw, loss_pad], axis=0)
    small_all, gmod_all = _exchange([(small, "gather"), (gmod, "gather")], name="exchange_last")
    gsmall = _sum8(small_all, name="sum_small")
    loss = gsmall[n_small + CONV_WIDTH, 0]
    g_w_ada, g_b_ada = _ada_bwd(c_all, gmod_all.reshape(N_DEV * B, N_MOD * D), n_col=n_col, name="ada_bwd")
    g_conv_w = lax.dynamic_slice(gsmall[n_small:n_small + CONV_WIDTH], (0, me * (CC // N_DEV)),
                                 (CONV_WIDTH, CC // N_DEV))

    def col_update(name, w, m, v):
        outs = _sum8_adamw(parts[name], w[0].T, m[0].T, v[0].T, name="adamw_" + name)
        return tuple(o.T for o in outs)

    def row_update(name, w, m, v):
        return tuple(_sum8_adamw(parts[name], w[0], m[0], v[0], name="adamw_" + name))

    upd = {
        "ffn1_w_gate": col_update("ffn1_w_gate", ffn1_w_gate, m_ffn1_w_gate, v_ffn1_w_gate),
        "ffn1_w_up": col_update("ffn1_w_up", ffn1_w_up, m_ffn1_w_up, v_ffn1_w_up),
        "ffn1_w_down": row_update("ffn1_w_down", ffn1_w_down, m_ffn1_w_down, v_ffn1_w_down),
        "w_in": col_update("w_in", w_in, m_w_in, v_w_in),
        "w_attn_o": row_update("w_attn_o", w_attn_o, m_w_attn_o, v_w_attn_o),
        "w_conv_o": row_update("w_conv_o", w_conv_o, m_w_conv_o, v_w_conv_o),
        "w_out": row_update("w_out", w_out, m_w_out, v_w_out),
        "ffn2_w_gate": col_update("ffn2_w_gate", ffn2_w_gate, m_ffn2_w_gate, v_ffn2_w_gate),
        "ffn2_w_up": col_update("ffn2_w_up", ffn2_w_up, m_ffn2_w_up, v_ffn2_w_up),
        "ffn2_w_down": row_update("ffn2_w_down", ffn2_w_down, m_ffn2_w_down, v_ffn2_w_down),
        "w_ada": (g_w_ada,) + tuple(_adamw(g_w_ada, w_ada[0], m_w_ada[0], v_w_ada[0], name="adamw_w_ada")),
        "conv_w_dw": (g_conv_w,) + tuple(_adamw(g_conv_w, conv_w_dw[0], m_conv_w_dw[0], v_conv_w_dw[0],
                                                name="adamw_conv_w_dw")),
    }
    for k in upd:
        upd[k] = tuple(t[None] for t in upd[k])

    def pad_sinks(t):
        return jnp.pad(t, ((0, 0), (0, D - N_Q_HEADS)))

    def pack(f1, mix, f2, fin, cb, lg, lb, sinks, bada):
        return jnp.concatenate([f1, mix, f2, fin[None], cb, lg, lb, pad_sinks(sinks), bada.reshape(N_MOD, D)], axis=0)

    w_s = pack(norm_ffn1_g, norm_mix_g, norm_ffn2_g, final_norm_g, conv_b_dw, conv_ln_g, conv_ln_b, attn_sinks, b_ada)
    m_s = pack(m_norm_ffn1_g, m_norm_mix_g, m_norm_ffn2_g, m_final_norm_g, m_conv_b_dw, m_conv_ln_g, m_conv_ln_b,
               m_attn_sinks, m_b_ada)
    v_s = pack(v_norm_ffn1_g, v_norm_mix_g, v_norm_ffn2_g, v_final_norm_g, v_conv_b_dw, v_conv_ln_g, v_conv_ln_b,
               v_attn_sinks, v_b_ada)
    g_s = jnp.concatenate([gsmall[:n_small], g_b_ada.reshape(N_MOD, D)], axis=0)
    small_out = (g_s,) + tuple(_adamw(g_s, w_s, m_s, v_s, name="adamw_vectors"))

    def unpack(t):
        return {
            "norm_ffn1_g": t[0:1], "norm_mix_g": t[1:2], "norm_ffn2_g": t[2:3], "final_norm_g": t[3],
            "conv_b_dw": t[4:5], "conv_ln_g": t[5:6], "conv_ln_b": t[6:7], "attn_sinks": t[7:8, :N_Q_HEADS],
            "b_ada": t[n_small:n_small + N_MOD].reshape(1, N_MOD * D),
        }

    small_un = [unpack(t) for t in small_out]
    for k in small_un[0]:
        upd[k] = tuple(s[k] for s in small_un)

    order = ["w_ada", "b_ada", "norm_ffn1_g", "ffn1_w_gate", "ffn1_w_up", "ffn1_w_down", "norm_mix_g", "w_in",
             "attn_sinks", "w_attn_o", "conv_w_dw", "conv_b_dw", "conv_ln_g", "conv_ln_b", "w_conv_o", "w_out",
             "norm_ffn2_g", "ffn2_w_gate", "ffn2_w_up", "ffn2_w_down", "final_norm_g"]
    grad_x = dx0.reshape(B, S, D)
    return (loss, grad_x, *[upd[k][0] for k in order], *[upd[k][1] for k in order],
            *[upd[k][2] for k in order], *[upd[k][3] for k in order])
```

```python
import dataclasses

import jax
import jax.numpy as jnp
from jax import lax
from jax.experimental import pallas as pl
from jax.experimental.pallas import tpu as pltpu

F32 = jnp.float32
BF16 = jnp.bfloat16
SDS = jax.ShapeDtypeStruct
MESH = pl.DeviceIdType.MESH

N_DEV = 8
EPS = 1e-6
HEAD_DIM = 64
N_Q_HEADS = 16
N_KV_HEADS = 2
GQA_GROUP = N_Q_HEADS // N_KV_HEADS
KV_WIDTH = N_KV_HEADS * HEAD_DIM
ATT_BLOCK = 128
CONV_WIDTH = 31
CONV_HALO = 32
CONV_ROWS = 64
N_MOD = 9
FFN_RESIDUAL = 0.5
ADAM_LR = 0.001
ADAM_B1 = 0.9
ADAM_B2 = 0.999
ADAM_EPS = 1e-08
ADAM_WD = 0.01
ADAM_STEP = 10
NEG_BIG = -1e30
GRAD_STREAM = BF16

V7X_VMEM_BYTES = 64 * 2**20
VMEM_CAP = V7X_VMEM_BYTES - 8 * 2**20


def _nbytes(shape, dtype):
    n = 1
    for s in shape:
        n *= s
    return n * jnp.dtype(dtype).itemsize


def _params(n_axes, blocks, temp_bytes=0):
    need = 2 * sum(_nbytes(s, d) for s, d in blocks) + temp_bytes + 4 * 2**20
    return pltpu.CompilerParams(dimension_semantics=("arbitrary",) * n_axes,
                                vmem_limit_bytes=int(min(max(need, 16 * 2**20), VMEM_CAP)))


def _dot_nt(a, b):
    return lax.dot_general(a, b, (((1,), (1,)), ((), ())), preferred_element_type=F32)


def _dot_tn(a, b):
    return lax.dot_general(a, b, (((0,), (0,)), ((), ())), preferred_element_type=F32)


def _dot(a, b):
    return jnp.dot(a, b, preferred_element_type=F32)


def _sigmoid(x):
    return jax.nn.sigmoid(x)


def _rowsum(v):
    return jnp.sum(v, axis=0, keepdims=True)


def _acc(ref, val, first):
    @pl.when(first)
    def _():
        ref[...] = val

    @pl.when(jnp.logical_not(first))
    def _():
        ref[...] = ref[...] + val


def _norm_mod(xf, gn, sh, sc):
    rstd = lax.rsqrt(jnp.mean(xf * xf, axis=-1, keepdims=True) + EPS)
    xhat = xf * rstd
    yn = xhat * gn
    return yn * (1.0 + sc) + sh, xhat, rstd, yn


def _pick(n, cands):
    for c in cands:
        if n % c == 0:
            return c
    return n


def _my_pos():
    return lax.axis_index("x"), lax.axis_index("y"), lax.axis_index("c")


def _peer(pos, k):
    x, y, c = pos
    return ((1 - x) if k & 4 else x, (1 - y) if k & 2 else y, (1 - c) if k & 1 else c)


def _lin(pos):
    return 4 * pos[0] + 2 * pos[1] + pos[2]


class _Comm:
    N_COPY = N_DEV - 1
    N_CHIP = N_DEV // 2

    def __init__(self, items):
        self.arrs = [a for a, _ in items]
        self.modes = [m for _, m in items]
        self.n = len(items)
        self.out = None

    def out_shape(self):
        def shape(a, m):
            return {"gather": (N_DEV,) + a.shape, "scatter": a.shape, "pair": (self.N_CHIP,) + a.shape[1:],
                    "cross": a.shape}[m]
        return [SDS(shape(a, m), a.dtype) for a, m in zip(self.arrs, self.modes)]

    def scratch(self):
        return [pltpu.SemaphoreType.DMA((self.n * self.N_COPY,)), pltpu.SemaphoreType.DMA((self.n * self.N_COPY,)),
                pltpu.SemaphoreType.DMA((self.n,))]

    def collective_id(self):
        modes = set(self.modes)
        if "scatter" in modes:
            return 3
        d2d, ici = bool(modes & {"gather", "pair"}), bool(modes & {"gather", "cross"})
        return {(True, False): 0, (False, True): 1, (True, True): 2}[(d2d, ici)]

    def barrier(self):
        x, y, c = _my_pos()
        peers = {0: [(x, y, 1 - c)],
                 1: [(1 - x, y, c), (x, 1 - y, c), (1 - x, 1 - y, c)],
                 2: [(x, y, 1 - c), (1 - x, y, c), (x, 1 - y, c), (1 - x, 1 - y, c)],
                 3: [_peer((x, y, c), k) for k in range(1, N_DEV)]}[self.collective_id()]
        sem = pltpu.get_barrier_semaphore()
        for p in peers:
            pl.semaphore_signal(sem, inc=1, device_id=p, device_id_type=MESH)
        pl.semaphore_wait(sem, len(peers))

    def _plan(self, mode, me):
        x, y, c = me
        sib = (x, y, 1 - c)
        chips = [(1 - x, y), (x, 1 - y), (1 - x, 1 - y)]

        def chip_lin(ch):
            return 2 * ch[0] + ch[1]

        if mode == "scatter":
            peers = [_peer(me, k + 1) for k in range(self.N_COPY)]
            return [(p, ("in", _lin(p)), _lin(me), _lin(p), None) for p in peers], (_lin(me), _lin(me))
        if mode == "gather":
            same = [(*ch, c) for ch in chips]
            other = [(*ch, 1 - c) for ch in chips]
            copies = [(sib, ("in", None), _lin(me), _lin(sib), None)]
            copies += [(p, ("in", None), _lin(me), _lin(p), None) for p in same]
            copies += [(sib, ("out", _lin(p)), _lin(p), _lin(o), 1 + j) for j, (p, o) in enumerate(zip(same, other))]
            return copies, (None, _lin(me))
        if mode == "pair":
            return [(sib, ("in", 2 * q + 1 - c), q, q, None) for q in range(self.N_CHIP)], None
        if mode == "cross":
            mine = chip_lin((x, y))
            return ([((*ch, c), ("in", chip_lin(ch)), mine, chip_lin(ch), None) for ch in chips], (mine, mine))
        raise ValueError(mode)

    def _copy(self, refs, me, i, k, recv):
        srcs, outs, (send_sems, recv_sems, _) = refs
        peer, (where, slot), send_slot, recv_slot, _ = self._plan(self.modes[i], me)[0][k]
        src = srcs[i] if where == "in" else outs[i]
        src = src if slot is None else src.at[slot]
        sem = i * self.N_COPY + k
        return pltpu.make_async_remote_copy(
            src_ref=src, dst_ref=outs[i].at[recv_slot if recv else send_slot], send_sem=send_sems.at[sem],
            recv_sem=recv_sems.at[sem], device_id=peer, device_id_type=MESH)

    def _local(self, refs, me, i):
        srcs, outs, (_, _, loc_sems) = refs
        local = self._plan(self.modes[i], me)[1]
        if local is None:
            return None
        own = srcs[i] if local[0] is None else srcs[i].at[local[0]]
        return pltpu.make_async_copy(own, outs[i].at[local[1]], loc_sems.at[i])

    def start(self, refs):
        me = _my_pos()
        for i in range(self.n):
            local = self._local(refs, me, i)
            if local is not None:
                local.start()
            for k, cp in enumerate(self._plan(self.modes[i], me)[0]):
                if cp[4] is None:
                    self._copy(refs, me, i, k, False).start()

    def forward(self, refs):
        me = _my_pos()
        for i in range(self.n):
            for k, cp in enumerate(self._plan(self.modes[i], me)[0]):
                if cp[4] is not None:
                    self._copy(refs, me, i, cp[4], True).wait_recv()
                    self._copy(refs, me, i, k, False).start()

    def finish(self, refs):
        me = _my_pos()
        plans = [self._plan(m, me)[0] for m in self.modes]
        for i in range(self.n):
            passed_on = [cp[4] for cp in plans[i] if cp[4] is not None]
            for k in range(len(plans[i])):
                if k not in passed_on:
                    self._copy(refs, me, i, k, True).wait_recv()
                self._copy(refs, me, i, k, False).wait_send()
            local = self._local(refs, me, i)
            if local is not None:
                local.wait()


_ANY = pl.BlockSpec(memory_space=pl.ANY)


def _call(body, args, *, name, grid, in_specs, out_specs, out_shape, params, scratch_shapes=(), comm=None):
    in_specs, out_specs, out_shape = list(in_specs), list(out_specs), list(out_shape)
    scratch_shapes = list(scratch_shapes)
    if comm is None:
        return list(pl.pallas_call(body, name=name, grid=grid, in_specs=in_specs, out_specs=out_specs,
                                   out_shape=out_shape, scratch_shapes=scratch_shapes, compiler_params=params)(*args))
    n_in, n_out, n_scr, nc = len(in_specs), len(out_specs), len(scratch_shapes), comm.n
    n_steps = 1
    for g in grid:
        n_steps *= g

    def hosted(*refs):
        ins, c_in = refs[:n_in], refs[n_in:n_in + nc]
        outs = refs[n_in + nc:n_in + nc + n_out]
        c_out = refs[n_in + nc + n_out:n_in + 2 * nc + n_out]
        scr = refs[n_in + 2 * nc + n_out:n_in + 2 * nc + n_out + n_scr]
        sems = refs[n_in + 2 * nc + n_out + n_scr:]
        step = pl.program_id(0)
        for d in range(1, len(grid)):
            step = step * grid[d] + pl.program_id(d)
        c_refs = (c_in, c_out, sems)

        @pl.when(step == 0)
        def _():
            comm.barrier()
            comm.start(c_refs)

        if n_steps >= 3:
            @pl.when(step == n_steps - 2)
            def _():
                comm.forward(c_refs)

        body(*ins, *outs, *scr)

        @pl.when(step == n_steps - 1)
        def _():
            if n_steps < 3:
                comm.forward(c_refs)
            comm.finish(c_refs)

    res = pl.pallas_call(
        hosted, name=name, grid=grid, in_specs=in_specs + [_ANY] * nc, out_specs=out_specs + [_ANY] * nc,
        out_shape=out_shape + comm.out_shape(), scratch_shapes=scratch_shapes + comm.scratch(),
        compiler_params=dataclasses.replace(params, collective_id=comm.collective_id()))(*args, *comm.arrs)
    comm.out = list(res[n_out:])
    return list(res[:n_out])


def _exchange(items, *, name):
    comm = _Comm(items)

    def body(*refs):
        r = (refs[:comm.n], refs[comm.n:2 * comm.n], refs[2 * comm.n:])
        comm.barrier()
        comm.start(r)
        comm.forward(r)
        comm.finish(r)

    return list(pl.pallas_call(body, name=name, out_shape=comm.out_shape(), in_specs=[_ANY] * comm.n,
                               out_specs=[_ANY] * comm.n, scratch_shapes=comm.scratch(),
                               compiler_params=pltpu.CompilerParams(collective_id=comm.collective_id()))(*comm.arrs))


class _ModVec:
    def __init__(self, arr, idx):
        self.arr, self.idx = arr, idx

    def spec(self, tps, n_axes):
        idx, blk = self.idx, (1, 1, self.arr.shape[2])
        if n_axes == 1:
            return pl.BlockSpec(blk, lambda i: (i // tps * N_MOD + idx, 0, 0))
        return pl.BlockSpec(blk, lambda i, j: (i // tps * N_MOD + idx, 0, 0))


def _norm_mod_matmul(x, gn, sh, sc, wts, *, seq, tm, tn, name, comm=None):
    T, D = x.shape
    N = wts[0].shape[0]
    nw = len(wts)
    tps = seq // tm

    def body(x_ref, gn_ref, sh_ref, sc_ref, *rest):
        w_refs, h_ref, o_refs = rest[:nw], rest[nw], rest[nw + 1:]

        @pl.when(pl.program_id(1) == 0)
        def _():
            h_ref[...] = _norm_mod(x_ref[...], gn_ref[...], sh_ref[0], sc_ref[0])[0].astype(BF16)

        h = h_ref[...]
        for w_ref, o_ref in zip(w_refs, o_refs):
            o_ref[...] = _dot_nt(h, w_ref[...]).astype(o_ref.dtype)

    row = pl.BlockSpec((tm, D), lambda i, j: (i, 0))
    vec = pl.BlockSpec((1, D), lambda i, j: (0, 0))
    wspec = pl.BlockSpec((tn, D), lambda i, j: (j, 0))
    ospec = pl.BlockSpec((tm, tn), lambda i, j: (i, j))
    blocks = [((tm, D), F32), ((tm, D), BF16)] + [((tn, D), BF16), ((tm, tn), BF16)] * nw
    outs = _call(
        body, (x, gn, sh.arr, sc.arr, *wts), name=name, grid=(T // tm, N // tn),
        in_specs=[row, vec, sh.spec(tps, 2), sc.spec(tps, 2)] + [wspec] * nw,
        out_specs=[row] + [ospec] * nw,
        out_shape=[SDS((T, D), BF16)] + [SDS((T, N), BF16)] * nw,
        params=_params(2, blocks, temp_bytes=2 * _nbytes((tm, tn), F32) + 3 * _nbytes((tm, D), F32)), comm=comm)
    return outs[0], outs[1:]


def _matmul_nt(h, w, *, tm, tn, name, comm=None):
    T, D = h.shape
    N = w.shape[0]

    def body(h_ref, w_ref, o_ref):
        o_ref[...] = _dot_nt(h_ref[...], w_ref[...]).astype(o_ref.dtype)

    blocks = [((tm, D), BF16), ((tn, D), BF16), ((tm, tn), BF16)]
    return _call(
        body, (h, w), name=name, grid=(T // tm, N // tn),
        in_specs=[pl.BlockSpec((tm, D), lambda i, j: (i, 0)), pl.BlockSpec((tn, D), lambda i, j: (j, 0))],
        out_specs=[pl.BlockSpec((tm, tn), lambda i, j: (i, j))],
        out_shape=[SDS((T, N), BF16)],
        params=_params(2, blocks, temp_bytes=2 * _nbytes((tm, tn), F32)), comm=comm)[0]


def _ffn_down(a, b, wd, x, g, *, seq, tm, name, comm=None):
    T, F = a.shape
    D = wd.shape[1]
    tps = seq // tm

    def body(a_ref, b_ref, wd_ref, x_ref, g_ref, xo_ref, y_ref):
        af = a_ref[...].astype(F32)
        act = (af * _sigmoid(af) * b_ref[...].astype(F32)).astype(BF16)
        y = _dot(act, wd_ref[...])
        xo_ref[...] = x_ref[...] + (FFN_RESIDUAL * g_ref[0]) * y
        y_ref[...] = y.astype(BF16)

    wide = pl.BlockSpec((tm, F), lambda i: (i, 0))
    row = pl.BlockSpec((tm, D), lambda i: (i, 0))
    wspec = pl.BlockSpec((F, D), lambda i: (0, 0))
    blocks = [((tm, F), BF16)] * 2 + [((F, D), BF16), ((tm, D), F32), ((tm, D), F32), ((tm, D), BF16)]
    return _call(
        body, (a, b, wd, x, g.arr), name=name, grid=(T // tm,),
        in_specs=[wide, wide, wspec, row, g.spec(tps, 1)], out_specs=[row, row],
        out_shape=[SDS((T, D), F32), SDS((T, D), BF16)],
        params=_params(1, blocks, temp_bytes=3 * _nbytes((tm, F), F32)), comm=comm)


def _final_loss(x, gf, tgt, *, tm, name):
    T, D = x.shape
    nt = T // tm

    def body(x_ref, gf_ref, t_ref, dx_ref, loss_ref, dgf_ref, lacc):
        i = pl.program_id(0)
        xf = x_ref[...]
        gfv = gf_ref[...]
        rstd = lax.rsqrt(jnp.mean(xf * xf, axis=-1, keepdims=True) + EPS)
        xhat = xf * rstd
        err = xhat * gfv - t_ref[...]
        dy = err * (1.0 / D)
        dxhat = dy * gfv
        dx_ref[...] = (rstd * (dxhat - xhat * jnp.mean(dxhat * xhat, axis=-1, keepdims=True))).astype(dx_ref.dtype)
        _acc(dgf_ref, _rowsum(dy * xhat), i == 0)
        _acc(lacc, _rowsum(err * err), i == 0)

        @pl.when(i == nt - 1)
        def _():
            loss_ref[...] = jnp.broadcast_to((0.5 / D) * jnp.sum(lacc[...]), loss_ref.shape)

    row = pl.BlockSpec((tm, D), lambda i: (i, 0))
    vec = pl.BlockSpec((1, D), lambda i: (0, 0))
    lspec = pl.BlockSpec((1, 128), lambda i: (0, 0))
    blocks = [((tm, D), F32)] * 3
    return _call(
        body, (x, gf, tgt), name=name, grid=(nt,),
        in_specs=[row, vec, row], out_specs=[row, lspec, vec],
        out_shape=[SDS((T, D), GRAD_STREAM), SDS((1, 128), F32), SDS((1, D), F32)],
        scratch_shapes=[pltpu.VMEM((1, D), F32)],
        params=_params(1, blocks, temp_bytes=4 * _nbytes((tm, D), F32)))


def _ffn_bwd_down(dxo, g, y, wd, a, b, *, seq, tm, tn, name, comm=None):
    T, F = a.shape
    D = wd.shape[1]
    tps = seq // tm
    nb = T // seq

    def body(dxo_ref, g_ref, y_ref, wd_ref, a_ref, b_ref, dyb_ref, da_ref, db_ref, dg_ref):
        i = pl.program_id(0)

        @pl.when(pl.program_id(1) == 0)
        def _():
            dx = dxo_ref[...].astype(F32)
            dyb_ref[...] = ((FFN_RESIDUAL * g_ref[0]) * dx).astype(BF16)
            part = _rowsum(FFN_RESIDUAL * dx * y_ref[...].astype(F32))
            _acc(dg_ref, part[None], i % tps == 0)

        dact = _dot_nt(dyb_ref[...], wd_ref[...])
        af = a_ref[...].astype(F32)
        bf = b_ref[...].astype(F32)
        sg = _sigmoid(af)
        silu = af * sg
        da_ref[...] = (dact * bf * (sg + silu * (1.0 - sg))).astype(BF16)
        db_ref[...] = (dact * silu).astype(BF16)

    row = pl.BlockSpec((tm, D), lambda i, j: (i, 0))
    per_b = pl.BlockSpec((1, 1, D), lambda i, j: (i // tps, 0, 0))
    wspec = pl.BlockSpec((tn, D), lambda i, j: (j, 0))
    chunk = pl.BlockSpec((tm, tn), lambda i, j: (i, j))
    blocks = [((tm, D), F32), ((tm, D), BF16), ((tn, D), BF16), ((tm, D), BF16)] + [((tm, tn), BF16)] * 4
    return _call(
        body, (dxo, g.arr, y, wd, a, b), name=name, grid=(T // tm, F // tn),
        in_specs=[row, g.spec(tps, 2), row, wspec, chunk, chunk],
        out_specs=[row, chunk, chunk, per_b],
        out_shape=[SDS((T, D), BF16)] + [SDS((T, F), BF16)] * 2 + [SDS((nb, 1, D), F32)],
        params=_params(2, blocks, temp_bytes=6 * _nbytes((tm, tn), F32)), comm=comm)


def _matmul_norm_mod_bwd(ds, ws, x, gn, sc, dxo, *, seq, tm, name, out_dtype, comm=None):
    T, D = x.shape
    nk = len(ws)
    sizes = [len(g) for g in ds]
    ds = [d for g in ds for d in g]
    tps = seq // tm
    nb = T // seq

    def body(*refs):
        w_refs = refs[len(ds):len(ds) + nk]
        x_ref, gn_ref, sc_ref, dxo_ref, dxi_ref, dsh_ref, dsc_ref, dgn_ref = refs[len(ds) + nk:]
        i = pl.program_id(0)
        dh, at = None, 0
        for n, w_ref in zip(sizes, w_refs):
            pieces = [r[...] for r in refs[at:at + n]]
            at += n
            part = _dot(pieces[0] if n == 1 else jnp.concatenate(pieces, axis=1), w_ref[...])
            dh = part if dh is None else dh + part
        gnv = gn_ref[...]
        scv = sc_ref[0]
        _, xhat, rstd, yn = _norm_mod(x_ref[...], gnv, 0.0, scv)
        dyn = dh * (1.0 + scv)
        dxhat = dyn * gnv
        dxi_ref[...] = (dxo_ref[...].astype(F32)
                        + rstd * (dxhat - xhat * jnp.mean(dxhat * xhat, axis=-1, keepdims=True))).astype(out_dtype)
        first_of_seq = i % tps == 0
        _acc(dsh_ref, _rowsum(dh)[None], first_of_seq)
        _acc(dsc_ref, _rowsum(dh * yn)[None], first_of_seq)
        _acc(dgn_ref, _rowsum(dyn * xhat), i == 0)

    row = pl.BlockSpec((tm, D), lambda i: (i, 0))
    vec = pl.BlockSpec((1, D), lambda i: (0, 0))
    per_b = pl.BlockSpec((1, 1, D), lambda i: (i // tps, 0, 0))
    d_specs = [pl.BlockSpec((tm, d.shape[1]), lambda i: (i, 0)) for d in ds]
    w_specs = [pl.BlockSpec(w.shape, lambda i: (0, 0)) for w in ws]
    blocks = ([((tm, d.shape[1]), BF16) for d in ds] + [(w.shape, BF16) for w in ws] + [((tm, D), F32)] * 3)
    return _call(
        body, (*ds, *ws, x, gn, sc.arr, dxo), name=name, grid=(T // tm,),
        in_specs=d_specs + w_specs + [row, vec, sc.spec(tps, 1), row],
        out_specs=[row, per_b, per_b, vec],
        out_shape=[SDS((T, D), out_dtype), SDS((nb, 1, D), F32), SDS((nb, 1, D), F32), SDS((1, D), F32)],
        params=_params(1, blocks, temp_bytes=6 * _nbytes((tm, D), F32)), comm=comm)


def _layernorm_silu(yc, lg, lb):
    mu = jnp.mean(yc, axis=-1, keepdims=True)
    cen = yc - mu
    rstd = lax.rsqrt(jnp.mean(cen * cen, axis=-1, keepdims=True) + EPS)
    xh = cen * rstd
    l = xh * lg + lb
    s = _sigmoid(l)
    return l * s, xh, rstd, l, s


GATE_W = 256


def _gate_specs(tm, D, col):
    return [pl.BlockSpec((tm, GATE_W), lambda i, blk=col // GATE_W + t: (i, blk)) for t in range(D // GATE_W)]


def _gate(refs):
    return jnp.concatenate([r[...] for r in refs], axis=1).astype(F32)


def _mix_out(ao, yc, proj, wao, wco, wout, x1, g2, lg, lb, *, seq, tm, ga_col, gc_col, name, comm=None):
    T, D = x1.shape
    tps = seq // tm
    ng = D // GATE_W

    def body(ao_ref, yc_ref, *rest):
        ga_refs, gc_refs = rest[:ng], rest[ng:2 * ng]
        (wao_ref, wco_ref, wout_ref, x1_ref, g2_ref, lg_ref, lb_ref,
         x2_ref, z_ref, ya_ref, ycv_ref, cact_ref, mrg_ref) = rest[2 * ng:]
        ya = _dot(ao_ref[...], wao_ref[...])
        cact = _layernorm_silu(yc_ref[...], lg_ref[...], lb_ref[...])[0].astype(BF16)
        ycv = _dot(cact, wco_ref[...])
        merged = (_sigmoid(_gate(ga_refs)) * ya + _sigmoid(_gate(gc_refs)) * ycv).astype(BF16)
        z = _dot(merged, wout_ref[...])
        x2_ref[...] = x1_ref[...] + g2_ref[0] * z
        z_ref[...] = z.astype(BF16)
        ya_ref[...] = ya.astype(BF16)
        ycv_ref[...] = ycv.astype(BF16)
        cact_ref[...] = cact
        mrg_ref[...] = merged

    row = pl.BlockSpec((tm, D), lambda i: (i, 0))
    vec = pl.BlockSpec((1, D), lambda i: (0, 0))
    wspec = pl.BlockSpec((D, D), lambda i: (0, 0))
    gates = _gate_specs(tm, D, ga_col) + _gate_specs(tm, D, gc_col)
    blocks = ([((tm, D), BF16), ((tm, D), F32), ((tm, D), BF16), ((tm, D), BF16)] + [((D, D), BF16)] * 3
              + [((tm, D), F32)] * 2 + [((tm, D), BF16)] * 5)
    return _call(
        body, (ao, yc, *[proj] * (2 * ng), wao, wco, wout, x1, g2.arr, lg, lb), name=name, grid=(T // tm,),
        in_specs=[row, row, *gates, wspec, wspec, wspec, row, g2.spec(tps, 1), vec, vec],
        out_specs=[row] * 6,
        out_shape=[SDS((T, D), F32)] + [SDS((T, D), BF16)] * 5,
        params=_params(1, blocks, temp_bytes=8 * _nbytes((tm, D), F32)), comm=comm)


def _mix_out_bwd(dx2, g2, z, wout, proj, ya, ycv, wao, wco, yc, lg, lb, *, seq, tm, ga_col, gc_col, name,
                 comm=None):
    T, D = dx2.shape
    tps = seq // tm
    nb = T // seq
    ng = D // GATE_W

    def body(dx2_ref, g2_ref, z_ref, wout_ref, *rest):
        ga_refs, gc_refs = rest[:ng], rest[ng:2 * ng]
        (ya_ref, ycv_ref, wao_ref, wco_ref, yc_ref, lg_ref, lb_ref, dz_ref, dya_ref, dycv_ref, dga_ref, dgc_ref,
         dao_ref, dyc_ref, dg2_ref, dlg_ref, dlb_ref) = rest[2 * ng:]
        i = pl.program_id(0)
        dx = dx2_ref[...].astype(F32)
        _acc(dg2_ref, _rowsum(dx * z_ref[...].astype(F32))[None], i % tps == 0)
        dzb = (g2_ref[0] * dx).astype(BF16)
        dz_ref[...] = dzb
        dmerged = _dot_nt(dzb, wout_ref[...])
        sa = _sigmoid(_gate(ga_refs))
        sc_ = _sigmoid(_gate(gc_refs))
        dya = (dmerged * sa).astype(BF16)
        dycv = (dmerged * sc_).astype(BF16)
        dya_ref[...] = dya
        dycv_ref[...] = dycv
        dga_ref[...] = (dmerged * ya_ref[...].astype(F32) * (sa * (1.0 - sa))).astype(BF16)
        dgc_ref[...] = (dmerged * ycv_ref[...].astype(F32) * (sc_ * (1.0 - sc_))).astype(BF16)
        dao_ref[...] = _dot_nt(dya, wao_ref[...]).astype(BF16)
        dcact = _dot_nt(dycv, wco_ref[...])
        lgv = lg_ref[...]
        _, xh, rstd, l, s = _layernorm_silu(yc_ref[...], lgv, lb_ref[...])
        dl = dcact * (s * (1.0 + l * (1.0 - s)))
        _acc(dlb_ref, _rowsum(dl), i == 0)
        _acc(dlg_ref, _rowsum(dl * xh), i == 0)
        dxh = dl * lgv
        dyc_ref[...] = rstd * (dxh - jnp.mean(dxh, axis=-1, keepdims=True)
                               - xh * jnp.mean(dxh * xh, axis=-1, keepdims=True))

    row = pl.BlockSpec((tm, D), lambda i: (i, 0))
    vec = pl.BlockSpec((1, D), lambda i: (0, 0))
    per_b = pl.BlockSpec((1, 1, D), lambda i: (i // tps, 0, 0))
    wspec = pl.BlockSpec((D, D), lambda i: (0, 0))
    gates = _gate_specs(tm, D, ga_col) + _gate_specs(tm, D, gc_col)
    blocks = ([((tm, D), F32)] * 3 + [((tm, D), BF16)] * 11 + [((D, D), BF16)] * 3)
    return _call(
        body, (dx2, g2.arr, z, wout, *[proj] * (2 * ng), ya, ycv, wao, wco, yc, lg, lb), name=name,
        grid=(T // tm,),
        in_specs=[row, g2.spec(tps, 1), row, wspec, *gates, row, row, wspec, wspec, row, vec, vec],
        out_specs=[row] * 7 + [per_b, vec, vec],
        out_shape=[SDS((T, D), BF16)] * 6 + [SDS((T, D), F32), SDS((nb, 1, D), F32), SDS((1, D), F32),
                                             SDS((1, D), F32)],
        params=_params(1, blocks, temp_bytes=10 * _nbytes((tm, D), F32)), comm=comm)


GROUP_ROWS = GQA_GROUP * ATT_BLOCK
PAIR_W = 2 * HEAD_DIM
GROUP_W = GQA_GROUP * HEAD_DIM


def _lane_lo():
    return lax.broadcasted_iota(jnp.int32, (1, PAIR_W), 1) < HEAD_DIM


def _band_bias():
    sj = lax.broadcasted_iota(jnp.int32, (2 * ATT_BLOCK, GROUP_ROWS), 0)
    qi = lax.broadcasted_iota(jnp.int32, (2 * ATT_BLOCK, GROUP_ROWS), 1) & (ATT_BLOCK - 1)
    rel = qi + ATT_BLOCK - sj
    bias = jnp.where(jnp.logical_and(rel >= 0, rel < ATT_BLOCK), 0.0, NEG_BIG)
    sj1 = lax.broadcasted_iota(jnp.int32, (2 * ATT_BLOCK, 1), 0)
    return bias, jnp.where(sj1 < ATT_BLOCK, NEG_BIG, 0.0)


def _dup_heads(src_ref, dst, seq):
    x = src_ref[...]
    i = lax.broadcasted_iota(jnp.int32, (KV_WIDTH, PAIR_W), 0)
    j = lax.broadcasted_iota(jnp.int32, (KV_WIDTH, PAIR_W), 1) & (HEAD_DIM - 1)
    for g in range(N_KV_HEADS):
        sel = jnp.where(i == j + g * HEAD_DIM, 1.0, 0.0).astype(BF16)
        dst[g, pl.ds(0, ATT_BLOCK), :] = jnp.zeros((ATT_BLOCK, PAIR_W), BF16)
        dst[g, pl.ds(ATT_BLOCK, seq), :] = _dot(x, sel).astype(BF16)


def _stack_heads(blk, g, lo):
    parts = []
    for p in range(GQA_GROUP // 2):
        pair = blk[:, g * GROUP_W + p * PAIR_W:g * GROUP_W + (p + 1) * PAIR_W]
        parts += [jnp.where(lo, pair, jnp.zeros_like(pair)), jnp.where(lo, jnp.zeros_like(pair), pair)]
    return jnp.concatenate(parts, axis=0)


def _unstack_heads(full, ref, r0, g, lo):
    for p in range(GQA_GROUP // 2):
        even = full[(2 * p) * ATT_BLOCK:(2 * p + 1) * ATT_BLOCK, :]
        odd = full[(2 * p + 1) * ATT_BLOCK:(2 * p + 2) * ATT_BLOCK, :]
        ref[pl.ds(r0, ATT_BLOCK), g * GROUP_W + p * PAIR_W:g * GROUP_W + (p + 1) * PAIR_W] = (
            jnp.where(lo, even, odd).astype(ref.dtype))


def _sink_row(sink_ref, g):
    return jnp.concatenate([jnp.full((1, ATT_BLOCK), sink_ref[0, g * GQA_GROUP + h], F32)
                            for h in range(GQA_GROUP)], axis=1)


def _group_probs(qs, k2, bias, sink):
    s = _dot_nt(k2, qs) * (HEAD_DIM ** -0.5) + bias
    m = jnp.maximum(jnp.max(s, axis=0, keepdims=True), sink)
    p = jnp.exp(s - m)
    psink = jnp.exp(sink - m)
    inv = 1.0 / (jnp.sum(p, axis=0, keepdims=True) + psink)
    return p * inv, psink * inv


def _attn_fwd(projp, sinks, *, seq, q_blk, k_blk, v_blk, name, comm=None):
    T = projp.shape[0]
    QW = N_Q_HEADS * HEAD_DIM
    nblk = seq // ATT_BLOCK

    def body(q_ref, k_ref, v_ref, sink_ref, o_ref, k2s, v2s):
        _dup_heads(k_ref, k2s, seq)
        _dup_heads(v_ref, v2s, seq)
        lo = _lane_lo()
        bias0, first_pen = _band_bias()
        sink_rows = [_sink_row(sink_ref, g) for g in range(N_KV_HEADS)]

        def blk(n, carry):
            r0 = pl.multiple_of(n * ATT_BLOCK, ATT_BLOCK)
            qb = q_ref[pl.ds(r0, ATT_BLOCK), :]
            bias = bias0 + jnp.where(n == 0, 1.0, 0.0) * first_pen
            for g in range(N_KV_HEADS):
                probs_t, _ = _group_probs(_stack_heads(qb, g, lo), k2s[g, pl.ds(r0, 2 * ATT_BLOCK), :], bias,
                                          sink_rows[g])
                _unstack_heads(_dot_tn(probs_t.astype(BF16), v2s[g, pl.ds(r0, 2 * ATT_BLOCK), :]), o_ref, r0, g, lo)
            return carry

        lax.fori_loop(0, nblk, blk, 0)

    blocks = [((seq, QW), BF16)] * 2 + [((seq, KV_WIDTH), BF16)] * 2
    return _call(
        body, (projp, projp, projp, sinks), name=name, grid=(T // seq,),
        in_specs=[pl.BlockSpec((seq, QW), lambda b: (b, q_blk)),
                  pl.BlockSpec((seq, KV_WIDTH), lambda b: (b, k_blk)),
                  pl.BlockSpec((seq, KV_WIDTH), lambda b: (b, v_blk)),
                  pl.BlockSpec(memory_space=pltpu.SMEM)],
        out_specs=[pl.BlockSpec((seq, QW), lambda b: (b, 0))],
        out_shape=[SDS((T, QW), BF16)],
        scratch_shapes=[pltpu.VMEM((N_KV_HEADS, seq + ATT_BLOCK, PAIR_W), BF16)] * 2,
        params=_params(1, blocks, temp_bytes=16 * 2**20), comm=comm)[0]


def _attn_bwd(projp, dao, sinks, *, seq, q_blk, k_blk, v_blk, name, comm=None):
    T = projp.shape[0]
    QW = N_Q_HEADS * HEAD_DIM
    nblk = seq // ATT_BLOCK

    def body(q_ref, k_ref, v_ref, do_ref, sink_ref, dq_ref, dk_ref, dv_ref, dsink_ref, k2s, v2s, dkacc, dvacc):
        _dup_heads(k_ref, k2s, seq)
        _dup_heads(v_ref, v2s, seq)
        dkacc[...] = jnp.zeros(dkacc.shape, F32)
        dvacc[...] = jnp.zeros(dvacc.shape, F32)
        lane = lax.broadcasted_iota(jnp.int32, (1, PAIR_W), 1)
        lo = lane < HEAD_DIM
        bias0, first_pen = _band_bias()
        sink_rows = [_sink_row(sink_ref, g) for g in range(N_KV_HEADS)]

        def blk(n, dsink):
            r0 = pl.multiple_of(n * ATT_BLOCK, ATT_BLOCK)
            band = pl.ds(r0, 2 * ATT_BLOCK)
            qb = q_ref[pl.ds(r0, ATT_BLOCK), :]
            dob = do_ref[pl.ds(r0, ATT_BLOCK), :]
            bias = bias0 + jnp.where(n == 0, 1.0, 0.0) * first_pen
            for g in range(N_KV_HEADS):
                qs = _stack_heads(qb, g, lo)
                dos = _stack_heads(dob, g, lo)
                k2 = k2s[g, band, :]
                v2 = v2s[g, band, :]
                probs_t, psink = _group_probs(qs, k2, bias, sink_rows[g])
                dp_t = _dot_nt(v2, dos)
                delta = jnp.sum(probs_t * dp_t, axis=0, keepdims=True)
                ds_t = (probs_t * (dp_t - delta) * (HEAD_DIM ** -0.5)).astype(BF16)
                tsink = psink * delta
                for h in range(GQA_GROUP):
                    dsink = dsink + jnp.where(lane == g * GQA_GROUP + h,
                                              -jnp.sum(tsink[:, h * ATT_BLOCK:(h + 1) * ATT_BLOCK]), 0.0)
                _unstack_heads(_dot_tn(ds_t, k2), dq_ref, r0, g, lo)
                dkacc[g, band, :] = dkacc[g, band, :] + _dot(ds_t, qs)
                dvacc[g, band, :] = dvacc[g, band, :] + _dot(probs_t.astype(BF16), dos)
            return dsink

        dsink = lax.fori_loop(0, nblk, blk, jnp.zeros((1, PAIR_W), F32))
        _acc(dsink_ref, dsink, pl.program_id(0) == 0)

        def fold(acc, g):
            a = acc[g, pl.ds(ATT_BLOCK, seq), :]
            return a + pltpu.roll(a, HEAD_DIM, 1)

        dk_ref[...] = jnp.where(lo, fold(dkacc, 0), fold(dkacc, 1)).astype(BF16)
        dv_ref[...] = jnp.where(lo, fold(dvacc, 0), fold(dvacc, 1)).astype(BF16)

    blocks = [((seq, QW), BF16)] * 3 + [((seq, KV_WIDTH), BF16)] * 4
    kv_spec_out = pl.BlockSpec((seq, KV_WIDTH), lambda b: (b, 0))
    return _call(
        body, (projp, projp, projp, dao, sinks), name=name, grid=(T // seq,),
        in_specs=[pl.BlockSpec((seq, QW), lambda b: (b, q_blk)),
                  pl.BlockSpec((seq, KV_WIDTH), lambda b: (b, k_blk)),
                  pl.BlockSpec((seq, KV_WIDTH), lambda b: (b, v_blk)),
                  pl.BlockSpec((seq, QW), lambda b: (b, 0)),
                  pl.BlockSpec(memory_space=pltpu.SMEM)],
        out_specs=[pl.BlockSpec((seq, QW), lambda b: (b, 0)), kv_spec_out, kv_spec_out,
                   pl.BlockSpec((1, 128), lambda b: (0, 0))],
        out_shape=[SDS((T, QW), BF16), SDS((T, KV_WIDTH), BF16), SDS((T, KV_WIDTH), BF16), SDS((1, 128), F32)],
        scratch_shapes=[pltpu.VMEM((N_KV_HEADS, seq + ATT_BLOCK, PAIR_W), BF16)] * 2
        + [pltpu.VMEM((N_KV_HEADS, seq + ATT_BLOCK, PAIR_W), F32)] * 2,
        params=_params(1, blocks, temp_bytes=24 * 2**20), comm=comm)


SUBLANES = 8


def _sublane_shifts(win):
    n = CONV_ROWS + CONV_HALO
    return [win] + [pltpu.roll(win, n - b, 0) for b in range(1, SUBLANES)]


def _window(shifted, off):
    a = off // SUBLANES * SUBLANES
    return shifted[off % SUBLANES][a:a + CONV_ROWS, :]


def _conv_fwd(projp, w, bias, *, seq, cw, a_col, b_col, name, comm=None):
    T = projp.shape[0]
    C = w.shape[1]
    nchunk = seq // CONV_ROWS

    def body(a_ref, b_ref, w_ref, bias_ref, y_ref, upad):
        upad[pl.ds(0, CONV_HALO), :] = jnp.zeros((CONV_HALO, cw), F32)
        upad[pl.ds(CONV_HALO, seq), :] = a_ref[...].astype(F32) * _sigmoid(b_ref[...].astype(F32))
        wv = w_ref[...]
        bv = bias_ref[...]

        def chunk(r, carry):
            r0 = pl.multiple_of(r * CONV_ROWS, CONV_ROWS)
            shifted = _sublane_shifts(upad[pl.ds(r0, CONV_ROWS + CONV_HALO), :])
            acc = jnp.broadcast_to(bv, (CONV_ROWS, cw))
            for k in range(CONV_WIDTH):
                acc = acc + wv[k:k + 1, :] * _window(shifted, CONV_HALO - (CONV_WIDTH - 1) + k)
            y_ref[pl.ds(r0, CONV_ROWS), :] = acc
            return carry

        lax.fori_loop(0, nchunk, chunk, 0)

    blocks = [((seq, cw), BF16)] * 2 + [((seq, cw), F32)]
    return _call(
        body, (projp, projp, w, bias), name=name, grid=(T // seq, C // cw),
        in_specs=[pl.BlockSpec((seq, cw), lambda b, c: (b, a_col // cw + c)),
                  pl.BlockSpec((seq, cw), lambda b, c: (b, b_col // cw + c)),
                  pl.BlockSpec((CONV_WIDTH, cw), lambda b, c: (0, c)),
                  pl.BlockSpec((1, cw), lambda b, c: (0, c))],
        out_specs=[pl.BlockSpec((seq, cw), lambda b, c: (b, c))],
        out_shape=[SDS((T, C), F32)],
        scratch_shapes=[pltpu.VMEM((seq + CONV_HALO, cw), F32)],
        params=_params(2, blocks, temp_bytes=6 * _nbytes((seq, cw), F32)), comm=comm)[0]


def _conv_bwd(dy, projp, w, *, seq, cw, a_col, b_col, name, comm=None):
    T = projp.shape[0]
    C = w.shape[1]
    nchunk = seq // CONV_ROWS
    SUB = 8

    def body(dy_ref, a_ref, b_ref, w_ref, da_ref, db_ref, dw_ref, dbias_ref, dypad, dwp):
        first = pl.program_id(1) == 0
        dyv = dy_ref[...]
        dypad[pl.ds(0, seq), :] = dyv
        dypad[pl.ds(seq, CONV_HALO), :] = jnp.zeros((CONV_HALO, cw), F32)
        dwp[...] = jnp.zeros(dwp.shape, F32)
        wv = w_ref[...]

        def chunk(r, carry):
            r0 = pl.multiple_of(r * CONV_ROWS, CONV_ROWS)
            dy_shifts = _sublane_shifts(dypad[pl.ds(r0, CONV_ROWS + CONV_HALO), :])
            ac = a_ref[pl.ds(r0, CONV_ROWS), :].astype(F32)
            sbc = _sigmoid(b_ref[pl.ds(r0, CONV_ROWS), :].astype(F32))
            uc = ac * sbc
            du = jnp.zeros((CONV_ROWS, cw), F32)
            for k in range(CONV_WIDTH):
                dyk = _window(dy_shifts, CONV_WIDTH - 1 - k)
                du = du + wv[k:k + 1, :] * dyk
                prod = uc * dyk
                part = prod[0:SUB, :]
                for s in range(1, CONV_ROWS // SUB):
                    part = part + prod[s * SUB:(s + 1) * SUB, :]
                dwp[pl.ds(k * SUB, SUB), :] = dwp[pl.ds(k * SUB, SUB), :] + part
            da_ref[pl.ds(r0, CONV_ROWS), :] = (du * sbc).astype(BF16)
            db_ref[pl.ds(r0, CONV_ROWS), :] = (du * ac * (sbc * (1.0 - sbc))).astype(BF16)
            return carry

        lax.fori_loop(0, nchunk, chunk, 0)

        @pl.when(first)
        def _():
            dw_ref[...] = jnp.zeros(dw_ref.shape, F32)
            dbias_ref[...] = jnp.zeros(dbias_ref.shape, F32)

        for k in range(CONV_WIDTH):
            dw_ref[k:k + 1, :] = dw_ref[k:k + 1, :] + _rowsum(dwp[pl.ds(k * SUB, SUB), :])
        dbias_ref[...] = dbias_ref[...] + _rowsum(dyv)

    blocks = [((seq, cw), F32)] + [((seq, cw), BF16)] * 4
    return _call(
        body, (dy, projp, projp, w), name=name, grid=(C // cw, T // seq),
        in_specs=[pl.BlockSpec((seq, cw), lambda c, b: (b, c)),
                  pl.BlockSpec((seq, cw), lambda c, b: (b, a_col // cw + c)),
                  pl.BlockSpec((seq, cw), lambda c, b: (b, b_col // cw + c)),
                  pl.BlockSpec((CONV_WIDTH, cw), lambda c, b: (0, c))],
        out_specs=[pl.BlockSpec((seq, cw), lambda c, b: (b, c)), pl.BlockSpec((seq, cw), lambda c, b: (b, c)),
                   pl.BlockSpec((CONV_WIDTH, cw), lambda c, b: (0, c)), pl.BlockSpec((1, cw), lambda c, b: (0, c))],
        out_shape=[SDS((T, C), BF16), SDS((T, C), BF16), SDS((CONV_WIDTH, C), F32), SDS((1, C), F32)],
        scratch_shapes=[pltpu.VMEM((seq + CONV_HALO, cw), F32), pltpu.VMEM((CONV_WIDTH * SUB, cw), F32)],
        params=_params(2, blocks, temp_bytes=8 * _nbytes((seq, cw), F32)), comm=comm)


def _matmul_tn(a, b, *, name, gate=None, comm=None):
    T, M = a.shape
    N = b.shape[1]
    bm = _pick(M, (768, 512, 256))
    lhs = [a] if gate is None else [a, gate]

    def body(*refs):
        b_ref, o_ref = refs[len(lhs)], refs[len(lhs) + 1]
        av = refs[0][...]
        if gate is not None:
            af = av.astype(F32)
            av = (af * _sigmoid(af) * refs[1][...].astype(F32)).astype(BF16)
        o_ref[...] = _dot_tn(av, b_ref[...]).astype(BF16)

    blocks = [((T, bm), BF16)] * len(lhs) + [((T, N), BF16), ((bm, N), BF16)]
    return _call(
        body, (*lhs, b), name=name, grid=(M // bm,),
        in_specs=[pl.BlockSpec((T, bm), lambda i: (0, i))] * len(lhs) + [pl.BlockSpec((T, N), lambda i: (0, 0))],
        out_specs=[pl.BlockSpec((bm, N), lambda i: (i, 0))],
        out_shape=[SDS((M, N), BF16)],
        params=_params(1, blocks, temp_bytes=(2 + 4 * len(lhs)) * _nbytes((T, bm), BF16) + 2 * _nbytes((bm, N), F32)),
        comm=comm)[0]


TN_BLOCK = 256


def _matmul_tn_pieces(groups, b, *, name, comm=None):
    T, N = b.shape
    flat = [a for g in groups for a in g]
    starts, n_steps = [], 0
    for g in groups:
        width = sum(a.shape[1] for a in g)
        assert width % TN_BLOCK == 0 and (len(g) == 1 or width == TN_BLOCK), [a.shape for a in g]
        starts.append(n_steps)
        n_steps += width // TN_BLOCK

    def body(*refs):
        a_refs, b_ref, o_ref = refs[:len(flat)], refs[len(flat)], refs[len(flat) + 1]
        i = pl.program_id(0)
        at = 0
        for g, start in zip(groups, starts):
            mine = a_refs[at:at + len(g)]
            at += len(g)
            steps = sum(a.shape[1] for a in g) // TN_BLOCK

            @pl.when(jnp.logical_and(i >= start, i < start + steps))
            def _(mine=mine):
                a = mine[0][...] if len(mine) == 1 else jnp.concatenate([r[...] for r in mine], axis=1)
                o_ref[...] = _dot_tn(a, b_ref[...]).astype(BF16)

    a_specs = []
    for g, start in zip(groups, starts):
        for a in g:
            if len(g) == 1:
                last = a.shape[1] // TN_BLOCK - 1
                a_specs.append(pl.BlockSpec(
                    (T, TN_BLOCK), lambda i, start=start, last=last: (0, jnp.clip(i - start, 0, last))))
            else:
                a_specs.append(pl.BlockSpec((T, a.shape[1]), lambda i: (0, 0)))
    blocks = [((T, TN_BLOCK), BF16)] * len(flat) + [((T, N), BF16), ((TN_BLOCK, N), BF16)]
    return _call(
        body, (*flat, b), name=name, grid=(n_steps,),
        in_specs=a_specs + [pl.BlockSpec((T, N), lambda i: (0, 0))],
        out_specs=[pl.BlockSpec((TN_BLOCK, N), lambda i: (i, 0))],
        out_shape=[SDS((n_steps * TN_BLOCK, N), BF16)],
        params=_params(1, blocks, temp_bytes=2 * _nbytes((T, TN_BLOCK), BF16) + 2 * _nbytes((TN_BLOCK, N), F32)),
        comm=comm)[0]


def _sum_parts(p_ref):
    g = p_ref[0].astype(F32)
    for s in range(1, p_ref.shape[0]):
        g = g + p_ref[s].astype(F32)
    return g


def _pair_add(g, staged, *, name):
    _, R, W = g.shape
    nq = staged.shape[0]
    tr = _row_tile(R)

    def body(g_ref, s_ref, o_ref):
        mine = jnp.where(lax.axis_index("c") == 0, g_ref[0, 0].astype(F32), g_ref[0, 1].astype(F32))
        o_ref[0] = (mine + s_ref[0].astype(F32)).astype(o_ref.dtype)

    return _call(
        body, (g.reshape(nq, 2, R, W), staged), name=name, grid=(nq, R // tr),
        in_specs=[pl.BlockSpec((1, 2, tr, W), lambda q, i: (q, 0, i, 0)),
                  pl.BlockSpec((1, tr, W), lambda q, i: (q, i, 0))],
        out_specs=[pl.BlockSpec((1, tr, W), lambda q, i: (q, i, 0))],
        out_shape=[SDS((nq, R, W), g.dtype)],
        params=_params(2, [((4, tr, W), g.dtype)], temp_bytes=3 * _nbytes((tr, W), F32)))[0]


def _adamw_update(w, g, m, v):
    m = ADAM_B1 * m + (1.0 - ADAM_B1) * g
    v = ADAM_B2 * v + (1.0 - ADAM_B2) * (g * g)
    m_hat = m / (1.0 - ADAM_B1 ** ADAM_STEP)
    v_hat = v / (1.0 - ADAM_B2 ** ADAM_STEP)
    delta = -ADAM_LR * (m_hat / (jnp.sqrt(v_hat) + ADAM_EPS) + ADAM_WD * w)
    return delta, m, v


def _row_tile(R):
    return _pick(R, (256, 128, 112, 88, 64, 32, 16, 8))


def _sum8(parts, *, name):
    n, R, W = parts.shape
    tr = _row_tile(R)

    def body(p_ref, o_ref):
        o_ref[...] = _sum_parts(p_ref)

    return _call(
        body, (parts,), name=name, grid=(R // tr,),
        in_specs=[pl.BlockSpec((n, tr, W), lambda i: (0, i, 0))],
        out_specs=[pl.BlockSpec((tr, W), lambda i: (i, 0))],
        out_shape=[SDS((R, W), F32)],
        params=_params(1, [((n, tr, W), parts.dtype), ((tr, W), F32)]))[0]


def _adamw(g, w, m, v, *, name):
    R, W = w.shape
    tr = _row_tile(R)

    def body(g_ref, w_ref, m_ref, v_ref, d_ref, mo_ref, vo_ref):
        d_ref[...], mo_ref[...], vo_ref[...] = _adamw_update(w_ref[...], g_ref[...], m_ref[...], v_ref[...])

    spec = pl.BlockSpec((tr, W), lambda i: (i, 0))
    return _call(
        body, (g, w, m, v), name=name, grid=(R // tr,),
        in_specs=[spec] * 4, out_specs=[spec] * 3, out_shape=[SDS((R, W), F32)] * 3,
        params=_params(1, [((tr, W), F32)] * 7))


def _sum8_adamw(parts, w, m, v, *, name):
    R, W = w.shape
    n = parts.shape[0]
    tr = _row_tile(R)

    def body(p_ref, w_ref, m_ref, v_ref, g_ref, d_ref, mo_ref, vo_ref):
        g = _sum_parts(p_ref)
        g_ref[...] = g
        d_ref[...], mo_ref[...], vo_ref[...] = _adamw_update(w_ref[...], g, m_ref[...], v_ref[...])

    spec = pl.BlockSpec((tr, W), lambda i: (i, 0))
    return _call(
        body, (parts, w, m, v), name=name, grid=(R // tr,),
        in_specs=[pl.BlockSpec((n, tr, W), lambda i: (0, i, 0))] + [spec] * 3,
        out_specs=[spec] * 4, out_shape=[SDS((R, W), F32)] * 4,
        params=_params(1, [((n, tr, W), parts.dtype)] + [((tr, W), F32)] * 7))


def _ada_fwd(c_all, w, bias, *, name):
    NB, D = c_all.shape
    N = w.shape[1]

    def body(c_ref, w_ref, b_ref, o_ref):
        cv = c_ref[...]
        ca = (cv * _sigmoid(cv)).astype(BF16)
        o_ref[...] = _dot(ca, w_ref[...].astype(BF16)) + b_ref[...]

    full = lambda s: pl.BlockSpec(s, lambda i: (0,) * len(s))
    return _call(
        body, (c_all, w, bias), name=name, grid=(1,),
        in_specs=[full((NB, D)), full((D, N)), full((1, N))], out_specs=[full((NB, N))],
        out_shape=[SDS((NB, N), F32)],
        params=_params(1, [((D, N), F32)], temp_bytes=_nbytes((D, N), BF16)))[0]


def _ada_bwd(c_all, gmod_all, *, n_col, name):
    NB, D = c_all.shape
    N = gmod_all.shape[1]

    def body(c_ref, g_ref, gw_ref, gb_ref):
        cv = c_ref[...]
        ca = (cv * _sigmoid(cv)).astype(BF16)
        first = pl.multiple_of(_lin(_my_pos()) * n_col, 128)
        gw_ref[...] = _dot_tn(ca, g_ref[:, pl.ds(first, n_col)].astype(BF16))
        gb_ref[...] = _rowsum(g_ref[...])

    full = lambda s: pl.BlockSpec(s, lambda i: (0,) * len(s))
    return _call(
        body, (c_all, gmod_all), name=name, grid=(1,),
        in_specs=[full((NB, D)), full((NB, N))], out_specs=[full((D, n_col)), full((1, N))],
        out_shape=[SDS((D, n_col), F32), SDS((1, N), F32)],
        params=_params(1, [((D, n_col), F32), ((NB, N), F32)]))


def kernel(x, c, w_ada, b_ada, norm_ffn1_g, ffn1_w_gate, ffn1_w_up, ffn1_w_down, norm_mix_g, w_in, attn_sinks, w_attn_o, conv_w_dw, conv_b_dw, conv_ln_g, conv_ln_b, w_conv_o, w_out, norm_ffn2_g, ffn2_w_gate, ffn2_w_up, ffn2_w_down, final_norm_g, loss_target, m_w_ada, m_b_ada, m_norm_ffn1_g, m_ffn1_w_gate, m_ffn1_w_up, m_ffn1_w_down, m_norm_mix_g, m_w_in, m_attn_sinks, m_w_attn_o, m_conv_w_dw, m_conv_b_dw, m_conv_ln_g, m_conv_ln_b, m_w_conv_o, m_w_out, m_norm_ffn2_g, m_ffn2_w_gate, m_ffn2_w_up, m_ffn2_w_down, m_final_norm_g, v_w_ada, v_b_ada, v_norm_ffn1_g, v_ffn1_w_gate, v_ffn1_w_up, v_ffn1_w_down, v_norm_mix_g, v_w_in, v_attn_sinks, v_w_attn_o, v_conv_w_dw, v_conv_b_dw, v_conv_ln_g, v_conv_ln_b, v_w_conv_o, v_w_out, v_norm_ffn2_g, v_ffn2_w_gate, v_ffn2_w_up, v_ffn2_w_down, v_final_norm_g):
    B, S, D = x.shape
    T = B * S
    QW = N_Q_HEADS * HEAD_DIM
    CC = conv_w_dw.shape[2] * N_DEV
    me = _lin(_my_pos())
    xf = x.reshape(T, D)
    tgt = loss_target.reshape(T, D)
    tm = min(512, S)
    kw = dict(seq=S, tm=tm)

    p_k, p_v, p_ca = QW, QW + KV_WIDTH, QW + 2 * KV_WIDTH
    p_cb, p_ga, p_gc = p_ca + CC, p_ca + 2 * CC, p_ca + 2 * CC + D

    def col_t(w):
        return w[0].T.astype(BF16)

    def row_b(w):
        return w[0].astype(BF16)

    def rows(g):
        return g.reshape(-1, g.shape[-1])

    def blocks8(g):
        return g.reshape(N_DEV, g.shape[0] // N_DEV, g.shape[1])

    def gather(*arrs):
        return _Comm([(a, "gather") for a in arrs])

    g_wg1, g_convw, g_c = _exchange(
        [(col_t(ffn1_w_gate), "gather"), (conv_w_dw[0], "gather"), (c, "gather")], name="gather_first")
    wg1 = rows(g_wg1)
    conv_w = g_convw.transpose(1, 0, 2).reshape(CONV_WIDTH, CC)
    c_all = g_c.reshape(N_DEV * B, D)

    n_col = N_MOD * D // N_DEV
    b_cols = lax.dynamic_slice(b_ada, (0, me * n_col), (1, n_col))
    mod_cols = _ada_fwd(c_all, w_ada[0], b_cols, name="ada_fwd")
    mod_mine = _exchange([(mod_cols.reshape(N_DEV, B, n_col), "scatter")], name="scatter_mod")[0]
    mod = mod_mine.transpose(1, 0, 2).reshape(B * N_MOD, 1, D)
    sh1, sc1, g1, sh2, sc2, g2, sh3, sc3, g3 = [_ModVec(mod, i) for i in range(N_MOD)]

    F = wg1.shape[0]
    tn_f = _pick(F, (1408, 1024, 512, 256))
    tn_in = _pick(w_in.shape[2] * N_DEV, (1792, 768, 512, 256))
    gate_blk = dict(ga_col=p_ga, gc_col=p_gc)
    att_blk = dict(q_blk=0, k_blk=p_k // KV_WIDTH, v_blk=p_v // KV_WIDTH)
    conv_kw = dict(seq=S, cw=256, a_col=p_ca, b_col=p_cb)

    cm = gather(col_t(ffn1_w_up))
    h1, (a1,) = _norm_mod_matmul(xf, norm_ffn1_g, sh1, sc1, [wg1], tn=tn_f, name="ffn1_gate", comm=cm, **kw)
    wu1 = rows(cm.out[0])
    cm = gather(row_b(ffn1_w_down))
    b1 = _matmul_nt(h1, wu1, tm=tm, tn=tn_f, name="ffn1_up", comm=cm)
    wd1 = rows(cm.out[0])
    cm = gather(col_t(w_in))
    x1, y1 = _ffn_down(a1, b1, wd1, xf, g1, name="ffn1_down", comm=cm, **kw)
    winp = rows(cm.out[0])
    cm = gather(row_b(w_attn_o), row_b(w_conv_o), row_b(w_out), col_t(ffn2_w_gate))
    h2, (projp,) = _norm_mod_matmul(x1, norm_mix_g, sh2, sc2, [winp], tn=tn_in, name="mix_in", comm=cm, **kw)
    wao, wco, wout, wg2 = [rows(o) for o in cm.out]
    cm = gather(col_t(ffn2_w_up))
    ao = _attn_fwd(projp, attn_sinks, seq=S, name="attn_fwd", comm=cm, **att_blk)
    wu2 = rows(cm.out[0])
    cm = gather(row_b(ffn2_w_down))
    yc = _conv_fwd(projp, conv_w, conv_b_dw, name="conv_fwd", comm=cm, **conv_kw)
    wd2 = rows(cm.out[0])
    x2, z, ya, ycv, cact, merged = _mix_out(ao, yc, projp, wao, wco, wout, x1, g2, conv_ln_g, conv_ln_b,
                                            name="mix_out", **gate_blk, **kw)
    h3, (a3, b3) = _norm_mod_matmul(x2, norm_ffn2_g, sh3, sc3, [wg2, wu2], tn=tn_f, name="ffn2_up", **kw)
    x3, y3 = _ffn_down(a3, b3, wd2, x2, g3, name="ffn2_down", **kw)
    dx3, loss_row, dgf = _final_loss(x3, final_norm_g[None], tgt, tm=tm, name="final_loss")

    parts = {}

    def pair(*gs):
        return [(blocks8(g), "pair") for g in gs]

    def cross(*rs):
        return [(r, "cross") for r in rs]

    def reduce_pairs(gs, staged, names):
        return [_pair_add(blocks8(g), s, name="pair_add_" + n) for g, s, n in zip(gs, staged, names)]

    dyb3, da3, db3, dg3 = _ffn_bwd_down(dx3, g3, y3, wd2, a3, b3, tn=tn_f, name="ffn2_bwd_down", **kw)
    gwd2 = _matmul_tn(a3, dyb3, gate=b3, name="gw_ffn2_down")
    cm = _Comm(pair(gwd2))
    dx2, dsh3, dsc3, dgn3 = _matmul_norm_mod_bwd([[da3], [db3]], [wg2, wu2], x2, norm_ffn2_g, sc3, dx3,
                                                 name="ffn2_bwd_up", out_dtype=GRAD_STREAM, comm=cm, **kw)
    r_wd2, = reduce_pairs([gwd2], cm.out, ["ffn2_w_down"])
    cm = _Comm(cross(r_wd2))
    gwg2 = _matmul_tn(da3, h3, name="gw_ffn2_gate", comm=cm)
    parts["ffn2_w_down"], = cm.out
    cm = _Comm(pair(gwg2))
    gwu2 = _matmul_tn(db3, h3, name="gw_ffn2_up", comm=cm)
    r_wg2, = reduce_pairs([gwg2], cm.out, ["ffn2_w_gate"])

    cm = _Comm(cross(r_wg2) + pair(gwu2))
    dzb, dyab, dycb, dga, dgc, dao, dyc, dg2, dlng, dlnb = _mix_out_bwd(
        dx2, g2, z, wout, projp, ya, ycv, wao, wco, yc, conv_ln_g, conv_ln_b, name="mix_out_bwd", comm=cm,
        **gate_blk, **kw)
    parts["ffn2_w_gate"] = cm.out[0]
    r_wu2, = reduce_pairs([gwu2], cm.out[1:], ["ffn2_w_up"])
    gwout = _matmul_tn(merged, dzb, name="gw_out")
    gwao = _matmul_tn(ao, dyab, name="gw_attn_o")
    gwco = _matmul_tn(cact, dycb, name="gw_conv_o")
    cm = _Comm(cross(r_wu2) + pair(gwout, gwao, gwco))
    dq, dk, dv, dsinks = _attn_bwd(projp, dao, attn_sinks, seq=S, name="attn_bwd", comm=cm, **att_blk)
    parts["ffn2_w_up"] = cm.out[0]
    r_mix = reduce_pairs([gwout, gwao, gwco], cm.out[1:], ["w_out", "w_attn_o", "w_conv_o"])
    cm = _Comm(cross(*r_mix))
    dca, dcb, dconvw, dconvb = _conv_bwd(dyc, projp, conv_w, name="conv_bwd", comm=cm, **conv_kw)
    parts["w_out"], parts["w_attn_o"], parts["w_conv_o"] = cm.out
    gwin = _matmul_tn_pieces([[dq], [dk, dv], [dca], [dcb], [dga], [dgc]], h2, name="gw_in")
    cm = _Comm(pair(gwin))
    dx1, dsh2, dsc2, dgn2 = _matmul_norm_mod_bwd([[dq, dk, dv, dca, dcb, dga, dgc]], [winp], x1, norm_mix_g, sc2, dx2,
                                                 name="mix_in_bwd", out_dtype=GRAD_STREAM, comm=cm, **kw)
    r_win, = reduce_pairs([gwin], cm.out, ["w_in"])

    cm = _Comm(cross(r_win))
    dyb1, da1, db1, dg1 = _ffn_bwd_down(dx1, g1, y1, wd1, a1, b1, tn=tn_f, name="ffn1_bwd_down", comm=cm,
                                              **kw)
    parts["w_in"], = cm.out
    gwd1 = _matmul_tn(a1, dyb1, gate=b1, name="gw_ffn1_down")
    cm = _Comm(pair(gwd1))
    gwg1 = _matmul_tn(da1, h1, name="gw_ffn1_gate", comm=cm)
    r_wd1, = reduce_pairs([gwd1], cm.out, ["ffn1_w_down"])
    cm = _Comm(cross(r_wd1) + pair(gwg1))
    gwu1 = _matmul_tn(db1, h1, name="gw_ffn1_up", comm=cm)
    parts["ffn1_w_down"] = cm.out[0]
    r_wg1, = reduce_pairs([gwg1], cm.out[1:], ["ffn1_w_gate"])
    r_wu1, = reduce_pairs([gwu1], _exchange(pair(gwu1), name="pair_last"), ["ffn1_w_up"])
    cm = _Comm(cross(r_wg1, r_wu1))
    dx0, dsh1, dsc1, dgn1 = _matmul_norm_mod_bwd([[da1], [db1]], [wg1, wu1], xf, norm_ffn1_g, sc1, dx1,
                                                 name="ffn1_bwd_up", out_dtype=F32, comm=cm, **kw)
    parts["ffn1_w_gate"], parts["ffn1_w_up"] = cm.out

    n_small = 8
    gmod = jnp.concatenate([dsh1, dsc1, dg1, dsh2, dsc2, dg2, dsh3, dsc3, dg3], axis=1).reshape(B, N_MOD * D)
    sink_row = jnp.pad(dsinks[:, :N_Q_HEADS], ((0, 0), (0, D - N_Q_HEADS)))
    loss_pad = jnp.pad(loss_row, ((0, 0), (0, D - loss_row.shape[1])))
    small = jnp.concatenate([dgn1, dgn2, dgn3, dgf, dconvb, dlng, dlnb, sink_row, dconvw, loss_pad], axis=0)
    small_all, gmod_all = _exchange([(small, "gather"), (gmod, "gather")], name="exchange_last")
    gsmall = _sum8(small_all, name="sum_small")
    loss = gsmall[n_small + CONV_WIDTH, 0]
    g_w_ada, g_b_ada = _ada_bwd(c_all, gmod_all.reshape(N_DEV * B, N_MOD * D), n_col=n_col, name="ada_bwd")
    g_conv_w = lax.dynamic_slice(gsmall[n_small:n_small + CONV_WIDTH], (0, me * (CC // N_DEV)),
                                 (CONV_WIDTH, CC // N_DEV))

    def col_update(name, w, m, v):
        outs = _sum8_adamw(parts[name], w[0].T, m[0].T, v[0].T, name="adamw_" + name)
        return tuple(o.T for o in outs)

    def row_update(name, w, m, v):
        return tuple(_sum8_adamw(parts[name], w[0], m[0], v[0], name="adamw_" + name))

    upd = {
        "ffn1_w_gate": col_update("ffn1_w_gate", ffn1_w_gate, m_ffn1_w_gate, v_ffn1_w_gate),
        "ffn1_w_up": col_update("ffn1_w_up", ffn1_w_up, m_ffn1_w_up, v_ffn1_w_up),
        "ffn1_w_down": row_update("ffn1_w_down", ffn1_w_down, m_ffn1_w_down, v_ffn1_w_down),
        "w_in": col_update("w_in", w_in, m_w_in, v_w_in),
        "w_attn_o": row_update("w_attn_o", w_attn_o, m_w_attn_o, v_w_attn_o),
        "w_conv_o": row_update("w_conv_o", w_conv_o, m_w_conv_o, v_w_conv_o),
        "w_out": row_update("w_out", w_out, m_w_out, v_w_out),
        "ffn2_w_gate": col_update("ffn2_w_gate", ffn2_w_gate, m_ffn2_w_gate, v_ffn2_w_gate),
        "ffn2_w_up": col_update("ffn2_w_up", ffn2_w_up, m_ffn2_w_up, v_ffn2_w_up),
        "ffn2_w_down": row_update("ffn2_w_down", ffn2_w_down, m_ffn2_w_down, v_ffn2_w_down),
        "w_ada": (g_w_ada,) + tuple(_adamw(g_w_ada, w_ada[0], m_w_ada[0], v_w_ada[0], name="adamw_w_ada")),
        "conv_w_dw": (g_conv_w,) + tuple(_adamw(g_conv_w, conv_w_dw[0], m_conv_w_dw[0], v_conv_w_dw[0],
                                                name="adamw_conv_w_dw")),
    }
    for k in upd:
        upd[k] = tuple(t[None] for t in upd[k])

    def pad_sinks(t):
        return jnp.pad(t, ((0, 0), (0, D - N_Q_HEADS)))

    def pack(f1, mix, f2, fin, cb, lg, lb, sinks, bada):
        return jnp.concatenate([f1, mix, f2, fin[None], cb, lg, lb, pad_sinks(sinks), bada.reshape(N_MOD, D)], axis=0)

    w_s = pack(norm_ffn1_g, norm_mix_g, norm_ffn2_g, final_norm_g, conv_b_dw, conv_ln_g, conv_ln_b, attn_sinks, b_ada)
    m_s = pack(m_norm_ffn1_g, m_norm_mix_g, m_norm_ffn2_g, m_final_norm_g, m_conv_b_dw, m_conv_ln_g, m_conv_ln_b,
               m_attn_sinks, m_b_ada)
    v_s = pack(v_norm_ffn1_g, v_norm_mix_g, v_norm_ffn2_g, v_final_norm_g, v_conv_b_dw, v_conv_ln_g, v_conv_ln_b,
               v_attn_sinks, v_b_ada)
    g_s = jnp.concatenate([gsmall[:n_small], g_b_ada.reshape(N_MOD, D)], axis=0)
    small_out = (g_s,) + tuple(_adamw(g_s, w_s, m_s, v_s, name="adamw_vectors"))

    def unpack(t):
        return {
            "norm_ffn1_g": t[0:1], "norm_mix_g": t[1:2], "norm_ffn2_g": t[2:3], "final_norm_g": t[3],
            "conv_b_dw": t[4:5], "conv_ln_g": t[5:6], "conv_ln_b": t[6:7], "attn_sinks": t[7:8, :N_Q_HEADS],
            "b_ada": t[n_small:n_small + N_MOD].reshape(1, N_MOD * D),
        }

    small_un = [unpack(t) for t in small_out]
    for k in small_un[0]:
        upd[k] = tuple(s[k] for s in small_un)

    order = ["w_ada", "b_ada", "norm_ffn1_g", "ffn1_w_gate", "ffn1_w_up", "ffn1_w_down", "norm_mix_g", "w_in",
             "attn_sinks", "w_attn_o", "conv_w_dw", "conv_b_dw", "conv_ln_g", "conv_ln_b", "w_conv_o", "w_out",
             "norm_ffn2_g", "ffn2_w_gate", "ffn2_w_up", "ffn2_w_down", "final_norm_g"]
    grad_x = dx0.reshape(B, S, D)
    return (loss, grad_x, *[upd[k][0] for k in order], *[upd[k][1] for k in order],
            *[upd[k][2] for k in order], *[upd[k][3] for k in order])
```

```python
import dataclasses

import jax
import jax.numpy as jnp
from jax import lax
from jax.experimental import pallas as pl
from jax.experimental.pallas import tpu as pltpu

F32 = jnp.float32
BF16 = jnp.bfloat16
SDS = jax.ShapeDtypeStruct
MESH = pl.DeviceIdType.MESH

N_DEV = 8
EPS = 1e-6
HEAD_DIM = 64
N_Q_HEADS = 16
N_KV_HEADS = 2
GQA_GROUP = N_Q_HEADS // N_KV_HEADS
KV_WIDTH = N_KV_HEADS * HEAD_DIM
ATT_BLOCK = 128
CONV_WIDTH = 31
CONV_HALO = 32
CONV_ROWS = 128
N_MOD = 9
FFN_RESIDUAL = 0.5
ADAM_LR = 0.001
ADAM_B1 = 0.9
ADAM_B2 = 0.999
ADAM_EPS = 1e-08
ADAM_WD = 0.01
ADAM_STEP = 10
NEG_BIG = -1e30
GRAD_STREAM = BF16

V7X_VMEM_BYTES = 64 * 2**20
VMEM_CAP = V7X_VMEM_BYTES - 8 * 2**20


def _nbytes(shape, dtype):
    n = 1
    for s in shape:
        n *= s
    return n * jnp.dtype(dtype).itemsize


def _params(n_axes, blocks, temp_bytes=0):
    need = 2 * sum(_nbytes(s, d) for s, d in blocks) + temp_bytes + 4 * 2**20
    return pltpu.CompilerParams(dimension_semantics=("arbitrary",) * n_axes,
                                vmem_limit_bytes=int(min(max(need, 16 * 2**20), VMEM_CAP)))


def _dot_nt(a, b):
    return lax.dot_general(a, b, (((1,), (1,)), ((), ())), preferred_element_type=F32)


def _dot_tn(a, b):
    return lax.dot_general(a, b, (((0,), (0,)), ((), ())), preferred_element_type=F32)


def _dot(a, b):
    return jnp.dot(a, b, preferred_element_type=F32)


def _sigmoid(x):
    return jax.nn.sigmoid(x)


def _rowsum(v):
    return jnp.sum(v, axis=0, keepdims=True)


def _acc(ref, val, first):
    @pl.when(first)
    def _():
        ref[...] = val

    @pl.when(jnp.logical_not(first))
    def _():
        ref[...] = ref[...] + val


def _norm_mod(xf, gn, sh, sc):
    rstd = lax.rsqrt(jnp.mean(xf * xf, axis=-1, keepdims=True) + EPS)
    xhat = xf * rstd
    yn = xhat * gn
    return yn * (1.0 + sc) + sh, xhat, rstd, yn


def _pick(n, cands):
    for c in cands:
        if n % c == 0:
            return c
    return n


def _my_pos():
    return lax.axis_index("x"), lax.axis_index("y"), lax.axis_index("c")


def _peer(pos, k):
    x, y, c = pos
    return ((1 - x) if k & 4 else x, (1 - y) if k & 2 else y, (1 - c) if k & 1 else c)


def _lin(pos):
    return 4 * pos[0] + 2 * pos[1] + pos[2]


class _Comm:
    N_COPY = N_DEV - 1
    N_CHIP = N_DEV // 2

    def __init__(self, items):
        self.arrs = [a for a, _ in items]
        self.modes = [m for _, m in items]
        self.n = len(items)
        self.out = None

    def out_shape(self):
        def shape(a, m):
            return {"gather": (N_DEV,) + a.shape, "scatter": a.shape, "pair": (self.N_CHIP,) + a.shape[1:],
                    "cross": a.shape}[m]
        return [SDS(shape(a, m), a.dtype) for a, m in zip(self.arrs, self.modes)]

    def scratch(self):
        return [pltpu.SemaphoreType.DMA((self.n * self.N_COPY,)), pltpu.SemaphoreType.DMA((self.n * self.N_COPY,)),
                pltpu.SemaphoreType.DMA((self.n,))]

    def collective_id(self):
        modes = set(self.modes)
        if "scatter" in modes:
            return 3
        d2d, ici = bool(modes & {"gather", "pair"}), bool(modes & {"gather", "cross"})
        return {(True, False): 0, (False, True): 1, (True, True): 2}[(d2d, ici)]

    def barrier(self):
        x, y, c = _my_pos()
        peers = {0: [(x, y, 1 - c)],
                 1: [(1 - x, y, c), (x, 1 - y, c), (1 - x, 1 - y, c)],
                 2: [(x, y, 1 - c), (1 - x, y, c), (x, 1 - y, c), (1 - x, 1 - y, c)],
                 3: [_peer((x, y, c), k) for k in range(1, N_DEV)]}[self.collective_id()]
        sem = pltpu.get_barrier_semaphore()
        for p in peers:
            pl.semaphore_signal(sem, inc=1, device_id=p, device_id_type=MESH)
        pl.semaphore_wait(sem, len(peers))

    def _plan(self, mode, me):
        x, y, c = me
        sib = (x, y, 1 - c)
        chips = [(1 - x, y), (x, 1 - y), (1 - x, 1 - y)]

        def chip_lin(ch):
            return 2 * ch[0] + ch[1]

        if mode == "scatter":
            peers = [_peer(me, k + 1) for k in range(self.N_COPY)]
            return [(p, ("in", _lin(p)), _lin(me), _lin(p), None) for p in peers], (_lin(me), _lin(me))
        if mode == "gather":
            same = [(*ch, c) for ch in chips]
            other = [(*ch, 1 - c) for ch in chips]
            copies = [(sib, ("in", None), _lin(me), _lin(sib), None)]
            copies += [(p, ("in", None), _lin(me), _lin(p), None) for p in same]
            copies += [(sib, ("out", _lin(p)), _lin(p), _lin(o), 1 + j) for j, (p, o) in enumerate(zip(same, other))]
            return copies, (None, _lin(me))
        if mode == "pair":
            return [(sib, ("in", 2 * q + 1 - c), q, q, None) for q in range(self.N_CHIP)], None
        if mode == "cross":
            mine = chip_lin((x, y))
            return ([((*ch, c), ("in", chip_lin(ch)), mine, chip_lin(ch), None) for ch in chips], (mine, mine))
        raise ValueError(mode)

    def _copy(self, refs, me, i, k, recv):
        srcs, outs, (send_sems, recv_sems, _) = refs
        peer, (where, slot), send_slot, recv_slot, _ = self._plan(self.modes[i], me)[0][k]
        src = srcs[i] if where == "in" else outs[i]
        src = src if slot is None else src.at[slot]
        sem = i * self.N_COPY + k
        return pltpu.make_async_remote_copy(
            src_ref=src, dst_ref=outs[i].at[recv_slot if recv else send_slot], send_sem=send_sems.at[sem],
            recv_sem=recv_sems.at[sem], device_id=peer, device_id_type=MESH)

    def _local(self, refs, me, i):
        srcs, outs, (_, _, loc_sems) = refs
        local = self._plan(self.modes[i], me)[1]
        if local is None:
            return None
        own = srcs[i] if local[0] is None else srcs[i].at[local[0]]
        return pltpu.make_async_copy(own, outs[i].at[local[1]], loc_sems.at[i])

    def start(self, refs):
        me = _my_pos()
        for i in range(self.n):
            local = self._local(refs, me, i)
            if local is not None:
                local.start()
            for k, cp in enumerate(self._plan(self.modes[i], me)[0]):
                if cp[4] is None:
                    self._copy(refs, me, i, k, False).start()

    def forward(self, refs):
        me = _my_pos()
        for i in range(self.n):
            for k, cp in enumerate(self._plan(self.modes[i], me)[0]):
                if cp[4] is not None:
                    self._copy(refs, me, i, cp[4], True).wait_recv()
                    self._copy(refs, me, i, k, False).start()

    def finish(self, refs):
        me = _my_pos()
        plans = [self._plan(m, me)[0] for m in self.modes]
        for i in range(self.n):
            passed_on = [cp[4] for cp in plans[i] if cp[4] is not None]
            for k in range(len(plans[i])):
                if k not in passed_on:
                    self._copy(refs, me, i, k, True).wait_recv()
                self._copy(refs, me, i, k, False).wait_send()
            local = self._local(refs, me, i)
            if local is not None:
                local.wait()


_ANY = pl.BlockSpec(memory_space=pl.ANY)


def _call(body, args, *, name, grid, in_specs, out_specs, out_shape, params, scratch_shapes=(), comm=None):
    in_specs, out_specs, out_shape = list(in_specs), list(out_specs), list(out_shape)
    scratch_shapes = list(scratch_shapes)
    if comm is None:
        return list(pl.pallas_call(body, name=name, grid=grid, in_specs=in_specs, out_specs=out_specs,
                                   out_shape=out_shape, scratch_shapes=scratch_shapes, compiler_params=params)(*args))
    n_in, n_out, n_scr, nc = len(in_specs), len(out_specs), len(scratch_shapes), comm.n
    n_steps = 1
    for g in grid:
        n_steps *= g

    def hosted(*refs):
        ins, c_in = refs[:n_in], refs[n_in:n_in + nc]
        outs = refs[n_in + nc:n_in + nc + n_out]
        c_out = refs[n_in + nc + n_out:n_in + 2 * nc + n_out]
        scr = refs[n_in + 2 * nc + n_out:n_in + 2 * nc + n_out + n_scr]
        sems = refs[n_in + 2 * nc + n_out + n_scr:]
        step = pl.program_id(0)
        for d in range(1, len(grid)):
            step = step * grid[d] + pl.program_id(d)
        c_refs = (c_in, c_out, sems)

        @pl.when(step == 0)
        def _():
            comm.barrier()
            comm.start(c_refs)

        if n_steps >= 3:
            @pl.when(step == n_steps - 2)
            def _():
                comm.forward(c_refs)

        body(*ins, *outs, *scr)

        @pl.when(step == n_steps - 1)
        def _():
            if n_steps < 3:
                comm.forward(c_refs)
            comm.finish(c_refs)

    res = pl.pallas_call(
        hosted, name=name, grid=grid, in_specs=in_specs + [_ANY] * nc, out_specs=out_specs + [_ANY] * nc,
        out_shape=out_shape + comm.out_shape(), scratch_shapes=scratch_shapes + comm.scratch(),
        compiler_params=dataclasses.replace(params, collective_id=comm.collective_id()))(*args, *comm.arrs)
    comm.out = list(res[n_out:])
    return list(res[:n_out])


def _exchange(items, *, name):
    comm = _Comm(items)

    def body(*refs):
        r = (refs[:comm.n], refs[comm.n:2 * comm.n], refs[2 * comm.n:])
        comm.barrier()
        comm.start(r)
        comm.forward(r)
        comm.finish(r)

    return list(pl.pallas_call(body, name=name, out_shape=comm.out_shape(), in_specs=[_ANY] * comm.n,
                               out_specs=[_ANY] * comm.n, scratch_shapes=comm.scratch(),
                               compiler_params=pltpu.CompilerParams(collective_id=comm.collective_id()))(*comm.arrs))


class _ModVec:
    def __init__(self, arr, idx):
        self.arr, self.idx = arr, idx

    def spec(self, tps, n_axes):
        idx, blk = self.idx, (1, 1, self.arr.shape[2])
        if n_axes == 1:
            return pl.BlockSpec(blk, lambda i: (i // tps * N_MOD + idx, 0, 0))
        return pl.BlockSpec(blk, lambda i, j: (i // tps * N_MOD + idx, 0, 0))


def _norm_mod_matmul(x, gn, sh, sc, wts, *, seq, tm, tn, name, comm=None):
    T, D = x.shape
    N = wts[0].shape[0]
    nw = len(wts)
    tps = seq // tm

    def body(x_ref, gn_ref, sh_ref, sc_ref, *rest):
        w_refs, h_ref, o_refs = rest[:nw], rest[nw], rest[nw + 1:]

        @pl.when(pl.program_id(1) == 0)
        def _():
            h_ref[...] = _norm_mod(x_ref[...], gn_ref[...], sh_ref[0], sc_ref[0])[0].astype(BF16)

        h = h_ref[...]
        for w_ref, o_ref in zip(w_refs, o_refs):
            o_ref[...] = _dot_nt(h, w_ref[...]).astype(o_ref.dtype)

    row = pl.BlockSpec((tm, D), lambda i, j: (i, 0))
    vec = pl.BlockSpec((1, D), lambda i, j: (0, 0))
    wspec = pl.BlockSpec((tn, D), lambda i, j: (j, 0))
    ospec = pl.BlockSpec((tm, tn), lambda i, j: (i, j))
    blocks = [((tm, D), F32), ((tm, D), BF16)] + [((tn, D), BF16), ((tm, tn), BF16)] * nw
    outs = _call(
        body, (x, gn, sh.arr, sc.arr, *wts), name=name, grid=(T // tm, N // tn),
        in_specs=[row, vec, sh.spec(tps, 2), sc.spec(tps, 2)] + [wspec] * nw,
        out_specs=[row] + [ospec] * nw,
        out_shape=[SDS((T, D), BF16)] + [SDS((T, N), BF16)] * nw,
        params=_params(2, blocks, temp_bytes=2 * _nbytes((tm, tn), F32) + 3 * _nbytes((tm, D), F32)), comm=comm)
    return outs[0], outs[1:]


def _matmul_nt(h, w, *, tm, tn, name, comm=None):
    T, D = h.shape
    N = w.shape[0]

    def body(h_ref, w_ref, o_ref):
        o_ref[...] = _dot_nt(h_ref[...], w_ref[...]).astype(o_ref.dtype)

    blocks = [((tm, D), BF16), ((tn, D), BF16), ((tm, tn), BF16)]
    return _call(
        body, (h, w), name=name, grid=(T // tm, N // tn),
        in_specs=[pl.BlockSpec((tm, D), lambda i, j: (i, 0)), pl.BlockSpec((tn, D), lambda i, j: (j, 0))],
        out_specs=[pl.BlockSpec((tm, tn), lambda i, j: (i, j))],
        out_shape=[SDS((T, N), BF16)],
        params=_params(2, blocks, temp_bytes=2 * _nbytes((tm, tn), F32)), comm=comm)[0]


def _ffn_down(a, b, wd, x, g, *, seq, tm, name, comm=None):
    T, F = a.shape
    D = wd.shape[1]
    tps = seq // tm

    def body(a_ref, b_ref, wd_ref, x_ref, g_ref, xo_ref, y_ref):
        af = a_ref[...].astype(F32)
        act = (af * _sigmoid(af) * b_ref[...].astype(F32)).astype(BF16)
        y = _dot(act, wd_ref[...])
        xo_ref[...] = x_ref[...] + (FFN_RESIDUAL * g_ref[0]) * y
        y_ref[...] = y.astype(BF16)

    wide = pl.BlockSpec((tm, F), lambda i: (i, 0))
    row = pl.BlockSpec((tm, D), lambda i: (i, 0))
    wspec = pl.BlockSpec((F, D), lambda i: (0, 0))
    blocks = [((tm, F), BF16)] * 2 + [((F, D), BF16), ((tm, D), F32), ((tm, D), F32), ((tm, D), BF16)]
    return _call(
        body, (a, b, wd, x, g.arr), name=name, grid=(T // tm,),
        in_specs=[wide, wide, wspec, row, g.spec(tps, 1)], out_specs=[row, row],
        out_shape=[SDS((T, D), F32), SDS((T, D), BF16)],
        params=_params(1, blocks, temp_bytes=3 * _nbytes((tm, F), F32)), comm=comm)


def _final_loss(x, gf, tgt, *, tm, name):
    T, D = x.shape
    nt = T // tm

    def body(x_ref, gf_ref, t_ref, dx_ref, loss_ref, dgf_ref, lacc):
        i = pl.program_id(0)
        xf = x_ref[...]
        gfv = gf_ref[...]
        rstd = lax.rsqrt(jnp.mean(xf * xf, axis=-1, keepdims=True) + EPS)
        xhat = xf * rstd
        err = xhat * gfv - t_ref[...]
        dy = err * (1.0 / D)
        dxhat = dy * gfv
        dx_ref[...] = (rstd * (dxhat - xhat * jnp.mean(dxhat * xhat, axis=-1, keepdims=True))).astype(dx_ref.dtype)
        _acc(dgf_ref, _rowsum(dy * xhat), i == 0)
        _acc(lacc, _rowsum(err * err), i == 0)

        @pl.when(i == nt - 1)
        def _():
            loss_ref[...] = jnp.broadcast_to((0.5 / D) * jnp.sum(lacc[...]), loss_ref.shape)

    row = pl.BlockSpec((tm, D), lambda i: (i, 0))
    vec = pl.BlockSpec((1, D), lambda i: (0, 0))
    lspec = pl.BlockSpec((1, 128), lambda i: (0, 0))
    blocks = [((tm, D), F32)] * 3
    return _call(
        body, (x, gf, tgt), name=name, grid=(nt,),
        in_specs=[row, vec, row], out_specs=[row, lspec, vec],
        out_shape=[SDS((T, D), GRAD_STREAM), SDS((1, 128), F32), SDS((1, D), F32)],
        scratch_shapes=[pltpu.VMEM((1, D), F32)],
        params=_params(1, blocks, temp_bytes=4 * _nbytes((tm, D), F32)))


def _ffn_bwd_down(dxo, g, y, wd, a, b, *, seq, tm, tn, name, comm=None):
    T, F = a.shape
    D = wd.shape[1]
    tps = seq // tm
    nb = T // seq

    def body(dxo_ref, g_ref, y_ref, wd_ref, a_ref, b_ref, dyb_ref, da_ref, db_ref, dg_ref):
        i = pl.program_id(0)

        @pl.when(pl.program_id(1) == 0)
        def _():
            dx = dxo_ref[...].astype(F32)
            dyb_ref[...] = ((FFN_RESIDUAL * g_ref[0]) * dx).astype(BF16)
            part = _rowsum(FFN_RESIDUAL * dx * y_ref[...].astype(F32))
            _acc(dg_ref, part[None], i % tps == 0)

        dact = _dot_nt(dyb_ref[...], wd_ref[...])
        af = a_ref[...].astype(F32)
        bf = b_ref[...].astype(F32)
        sg = _sigmoid(af)
        silu = af * sg
        da_ref[...] = (dact * bf * (sg + silu * (1.0 - sg))).astype(BF16)
        db_ref[...] = (dact * silu).astype(BF16)

    row = pl.BlockSpec((tm, D), lambda i, j: (i, 0))
    per_b = pl.BlockSpec((1, 1, D), lambda i, j: (i // tps, 0, 0))
    wspec = pl.BlockSpec((tn, D), lambda i, j: (j, 0))
    chunk = pl.BlockSpec((tm, tn), lambda i, j: (i, j))
    blocks = [((tm, D), F32), ((tm, D), BF16), ((tn, D), BF16), ((tm, D), BF16)] + [((tm, tn), BF16)] * 4
    return _call(
        body, (dxo, g.arr, y, wd, a, b), name=name, grid=(T // tm, F // tn),
        in_specs=[row, g.spec(tps, 2), row, wspec, chunk, chunk],
        out_specs=[row, chunk, chunk, per_b],
        out_shape=[SDS((T, D), BF16)] + [SDS((T, F), BF16)] * 2 + [SDS((nb, 1, D), F32)],
        params=_params(2, blocks, temp_bytes=6 * _nbytes((tm, tn), F32)), comm=comm)


def _matmul_norm_mod_bwd(ds, ws, x, gn, sc, dxo, *, seq, tm, name, out_dtype, comm=None):
    T, D = x.shape
    nk = len(ws)
    sizes = [len(g) for g in ds]
    ds = [d for g in ds for d in g]
    tps = seq // tm
    nb = T // seq

    def body(*refs):
        w_refs = refs[len(ds):len(ds) + nk]
        x_ref, gn_ref, sc_ref, dxo_ref, dxi_ref, dsh_ref, dsc_ref, dgn_ref = refs[len(ds) + nk:]
        i = pl.program_id(0)
        dh, at = None, 0
        for n, w_ref in zip(sizes, w_refs):
            pieces = [r[...] for r in refs[at:at + n]]
            at += n
            part = _dot(pieces[0] if n == 1 else jnp.concatenate(pieces, axis=1), w_ref[...])
            dh = part if dh is None else dh + part
        gnv = gn_ref[...]
        scv = sc_ref[0]
        _, xhat, rstd, yn = _norm_mod(x_ref[...], gnv, 0.0, scv)
        dyn = dh * (1.0 + scv)
        dxhat = dyn * gnv
        dxi_ref[...] = (dxo_ref[...].astype(F32)
                        + rstd * (dxhat - xhat * jnp.mean(dxhat * xhat, axis=-1, keepdims=True))).astype(out_dtype)
        first_of_seq = i % tps == 0
        _acc(dsh_ref, _rowsum(dh)[None], first_of_seq)
        _acc(dsc_ref, _rowsum(dh * yn)[None], first_of_seq)
        _acc(dgn_ref, _rowsum(dyn * xhat), i == 0)

    row = pl.BlockSpec((tm, D), lambda i: (i, 0))
    vec = pl.BlockSpec((1, D), lambda i: (0, 0))
    per_b = pl.BlockSpec((1, 1, D), lambda i: (i // tps, 0, 0))
    d_specs = [pl.BlockSpec((tm, d.shape[1]), lambda i: (i, 0)) for d in ds]
    w_specs = [pl.BlockSpec(w.shape, lambda i: (0, 0)) for w in ws]
    blocks = ([((tm, d.shape[1]), BF16) for d in ds] + [(w.shape, BF16) for w in ws] + [((tm, D), F32)] * 3)
    return _call(
        body, (*ds, *ws, x, gn, sc.arr, dxo), name=name, grid=(T // tm,),
        in_specs=d_specs + w_specs + [row, vec, sc.spec(tps, 1), row],
        out_specs=[row, per_b, per_b, vec],
        out_shape=[SDS((T, D), out_dtype), SDS((nb, 1, D), F32), SDS((nb, 1, D), F32), SDS((1, D), F32)],
        params=_params(1, blocks, temp_bytes=6 * _nbytes((tm, D), F32)), comm=comm)


def _layernorm_silu(yc, lg, lb):
    mu = jnp.mean(yc, axis=-1, keepdims=True)
    cen = yc - mu
    rstd = lax.rsqrt(jnp.mean(cen * cen, axis=-1, keepdims=True) + EPS)
    xh = cen * rstd
    l = xh * lg + lb
    s = _sigmoid(l)
    return l * s, xh, rstd, l, s


GATE_W = 256


def _gate_specs(tm, D, col):
    return [pl.BlockSpec((tm, GATE_W), lambda i, blk=col // GATE_W + t: (i, blk)) for t in range(D // GATE_W)]


def _gate(refs):
    return jnp.concatenate([r[...] for r in refs], axis=1).astype(F32)


def _mix_out(ao, yc, proj, wao, wco, wout, x1, g2, lg, lb, *, seq, tm, ga_col, gc_col, name, comm=None):
    T, D = x1.shape
    tps = seq // tm
    ng = D // GATE_W

    def body(ao_ref, yc_ref, *rest):
        ga_refs, gc_refs = rest[:ng], rest[ng:2 * ng]
        (wao_ref, wco_ref, wout_ref, x1_ref, g2_ref, lg_ref, lb_ref,
         x2_ref, z_ref, ya_ref, ycv_ref, cact_ref, mrg_ref) = rest[2 * ng:]
        ya = _dot(ao_ref[...], wao_ref[...])
        cact = _layernorm_silu(yc_ref[...], lg_ref[...], lb_ref[...])[0].astype(BF16)
        ycv = _dot(cact, wco_ref[...])
        merged = (_sigmoid(_gate(ga_refs)) * ya + _sigmoid(_gate(gc_refs)) * ycv).astype(BF16)
        z = _dot(merged, wout_ref[...])
        x2_ref[...] = x1_ref[...] + g2_ref[0] * z
        z_ref[...] = z.astype(BF16)
        ya_ref[...] = ya.astype(BF16)
        ycv_ref[...] = ycv.astype(BF16)
        cact_ref[...] = cact
        mrg_ref[...] = merged

    row = pl.BlockSpec((tm, D), lambda i: (i, 0))
    vec = pl.BlockSpec((1, D), lambda i: (0, 0))
    wspec = pl.BlockSpec((D, D), lambda i: (0, 0))
    gates = _gate_specs(tm, D, ga_col) + _gate_specs(tm, D, gc_col)
    blocks = ([((tm, D), BF16), ((tm, D), F32), ((tm, D), BF16), ((tm, D), BF16)] + [((D, D), BF16)] * 3
              + [((tm, D), F32)] * 2 + [((tm, D), BF16)] * 5)
    return _call(
        body, (ao, yc, *[proj] * (2 * ng), wao, wco, wout, x1, g2.arr, lg, lb), name=name, grid=(T // tm,),
        in_specs=[row, row, *gates, wspec, wspec, wspec, row, g2.spec(tps, 1), vec, vec],
        out_specs=[row] * 6,
        out_shape=[SDS((T, D), F32)] + [SDS((T, D), BF16)] * 5,
        params=_params(1, blocks, temp_bytes=8 * _nbytes((tm, D), F32)), comm=comm)


def _mix_out_bwd(dx2, g2, z, wout, proj, ya, ycv, wao, wco, yc, lg, lb, *, seq, tm, ga_col, gc_col, name,
                 comm=None):
    T, D = dx2.shape
    tps = seq // tm
    nb = T // seq
    ng = D // GATE_W

    def body(dx2_ref, g2_ref, z_ref, wout_ref, *rest):
        ga_refs, gc_refs = rest[:ng], rest[ng:2 * ng]
        (ya_ref, ycv_ref, wao_ref, wco_ref, yc_ref, lg_ref, lb_ref, dz_ref, dya_ref, dycv_ref, dga_ref, dgc_ref,
         dao_ref, dyc_ref, dg2_ref, dlg_ref, dlb_ref) = rest[2 * ng:]
        i = pl.program_id(0)
        dx = dx2_ref[...].astype(F32)
        _acc(dg2_ref, _rowsum(dx * z_ref[...].astype(F32))[None], i % tps == 0)
        dzb = (g2_ref[0] * dx).astype(BF16)
        dz_ref[...] = dzb
        dmerged = _dot_nt(dzb, wout_ref[...])
        sa = _sigmoid(_gate(ga_refs))
        sc_ = _sigmoid(_gate(gc_refs))
        dya = (dmerged * sa).astype(BF16)
        dycv = (dmerged * sc_).astype(BF16)
        dya_ref[...] = dya
        dycv_ref[...] = dycv
        dga_ref[...] = (dmerged * ya_ref[...].astype(F32) * (sa * (1.0 - sa))).astype(BF16)
        dgc_ref[...] = (dmerged * ycv_ref[...].astype(F32) * (sc_ * (1.0 - sc_))).astype(BF16)
        dao_ref[...] = _dot_nt(dya, wao_ref[...]).astype(BF16)
        dcact = _dot_nt(dycv, wco_ref[...])
        lgv = lg_ref[...]
        _, xh, rstd, l, s = _layernorm_silu(yc_ref[...], lgv, lb_ref[...])
        dl = dcact * (s * (1.0 + l * (1.0 - s)))
        _acc(dlb_ref, _rowsum(dl), i == 0)
        _acc(dlg_ref, _rowsum(dl * xh), i == 0)
        dxh = dl * lgv
        dyc_ref[...] = rstd * (dxh - jnp.mean(dxh, axis=-1, keepdims=True)
                               - xh * jnp.mean(dxh * xh, axis=-1, keepdims=True))

    row = pl.BlockSpec((tm, D), lambda i: (i, 0))
    vec = pl.BlockSpec((1, D), lambda i: (0, 0))
    per_b = pl.BlockSpec((1, 1, D), lambda i: (i // tps, 0, 0))
    wspec = pl.BlockSpec((D, D), lambda i: (0, 0))
    gates = _gate_specs(tm, D, ga_col) + _gate_specs(tm, D, gc_col)
    blocks = ([((tm, D), F32)] * 3 + [((tm, D), BF16)] * 11 + [((D, D), BF16)] * 3)
    return _call(
        body, (dx2, g2.arr, z, wout, *[proj] * (2 * ng), ya, ycv, wao, wco, yc, lg, lb), name=name,
        grid=(T // tm,),
        in_specs=[row, g2.spec(tps, 1), row, wspec, *gates, row, row, wspec, wspec, row, vec, vec],
        out_specs=[row] * 7 + [per_b, vec, vec],
        out_shape=[SDS((T, D), BF16)] * 6 + [SDS((T, D), F32), SDS((nb, 1, D), F32), SDS((1, D), F32),
                                             SDS((1, D), F32)],
        params=_params(1, blocks, temp_bytes=10 * _nbytes((tm, D), F32)), comm=comm)


GROUP_ROWS = GQA_GROUP * ATT_BLOCK
PAIR_W = 2 * HEAD_DIM
GROUP_W = GQA_GROUP * HEAD_DIM


def _lane_lo():
    return lax.broadcasted_iota(jnp.int32, (1, PAIR_W), 1) < HEAD_DIM


def _band_bias():
    sj = lax.broadcasted_iota(jnp.int32, (2 * ATT_BLOCK, GROUP_ROWS), 0)
    qi = lax.broadcasted_iota(jnp.int32, (2 * ATT_BLOCK, GROUP_ROWS), 1) & (ATT_BLOCK - 1)
    rel = qi + ATT_BLOCK - sj
    bias = jnp.where(jnp.logical_and(rel >= 0, rel < ATT_BLOCK), 0.0, NEG_BIG)
    sj1 = lax.broadcasted_iota(jnp.int32, (2 * ATT_BLOCK, 1), 0)
    return bias, jnp.where(sj1 < ATT_BLOCK, NEG_BIG, 0.0)


def _dup_heads(src_ref, dst, seq):
    x = src_ref[...]
    i = lax.broadcasted_iota(jnp.int32, (KV_WIDTH, PAIR_W), 0)
    j = lax.broadcasted_iota(jnp.int32, (KV_WIDTH, PAIR_W), 1) & (HEAD_DIM - 1)
    for g in range(N_KV_HEADS):
        sel = jnp.where(i == j + g * HEAD_DIM, 1.0, 0.0).astype(BF16)
        dst[g, pl.ds(0, ATT_BLOCK), :] = jnp.zeros((ATT_BLOCK, PAIR_W), BF16)
        dst[g, pl.ds(ATT_BLOCK, seq), :] = _dot(x, sel).astype(BF16)


def _stack_heads(blk, g, lo):
    parts = []
    for p in range(GQA_GROUP // 2):
        pair = blk[:, g * GROUP_W + p * PAIR_W:g * GROUP_W + (p + 1) * PAIR_W]
        parts += [jnp.where(lo, pair, jnp.zeros_like(pair)), jnp.where(lo, jnp.zeros_like(pair), pair)]
    return jnp.concatenate(parts, axis=0)


def _unstack_heads(full, ref, r0, g, lo):
    for p in range(GQA_GROUP // 2):
        even = full[(2 * p) * ATT_BLOCK:(2 * p + 1) * ATT_BLOCK, :]
        odd = full[(2 * p + 1) * ATT_BLOCK:(2 * p + 2) * ATT_BLOCK, :]
        ref[pl.ds(r0, ATT_BLOCK), g * GROUP_W + p * PAIR_W:g * GROUP_W + (p + 1) * PAIR_W] = (
            jnp.where(lo, even, odd).astype(ref.dtype))


def _sink_row(sink_ref, g):
    return jnp.concatenate([jnp.full((1, ATT_BLOCK), sink_ref[0, g * GQA_GROUP + h], F32)
                            for h in range(GQA_GROUP)], axis=1)


def _group_probs(qs, k2, bias, sink):
    s = _dot_nt(k2, qs) * (HEAD_DIM ** -0.5) + bias
    m = jnp.maximum(jnp.max(s, axis=0, keepdims=True), sink)
    p = jnp.exp(s - m)
    psink = jnp.exp(sink - m)
    inv = 1.0 / (jnp.sum(p, axis=0, keepdims=True) + psink)
    return p * inv, psink * inv


def _attn_fwd(projp, sinks, *, seq, q_blk, k_blk, v_blk, name, comm=None):
    T = projp.shape[0]
    QW = N_Q_HEADS * HEAD_DIM
    nblk = seq // ATT_BLOCK

    def body(q_ref, k_ref, v_ref, sink_ref, o_ref, k2s, v2s):
        _dup_heads(k_ref, k2s, seq)
        _dup_heads(v_ref, v2s, seq)
        lo = _lane_lo()
        bias0, first_pen = _band_bias()
        sink_rows = [_sink_row(sink_ref, g) for g in range(N_KV_HEADS)]

        def blk(n, carry):
            r0 = pl.multiple_of(n * ATT_BLOCK, ATT_BLOCK)
            qb = q_ref[pl.ds(r0, ATT_BLOCK), :]
            bias = bias0 + jnp.where(n == 0, 1.0, 0.0) * first_pen
            for g in range(N_KV_HEADS):
                probs_t, _ = _group_probs(_stack_heads(qb, g, lo), k2s[g, pl.ds(r0, 2 * ATT_BLOCK), :], bias,
                                          sink_rows[g])
                _unstack_heads(_dot_tn(probs_t.astype(BF16), v2s[g, pl.ds(r0, 2 * ATT_BLOCK), :]), o_ref, r0, g, lo)
            return carry

        lax.fori_loop(0, nblk, blk, 0)

    blocks = [((seq, QW), BF16)] * 2 + [((seq, KV_WIDTH), BF16)] * 2
    return _call(
        body, (projp, projp, projp, sinks), name=name, grid=(T // seq,),
        in_specs=[pl.BlockSpec((seq, QW), lambda b: (b, q_blk)),
                  pl.BlockSpec((seq, KV_WIDTH), lambda b: (b, k_blk)),
                  pl.BlockSpec((seq, KV_WIDTH), lambda b: (b, v_blk)),
                  pl.BlockSpec(memory_space=pltpu.SMEM)],
        out_specs=[pl.BlockSpec((seq, QW), lambda b: (b, 0))],
        out_shape=[SDS((T, QW), BF16)],
        scratch_shapes=[pltpu.VMEM((N_KV_HEADS, seq + ATT_BLOCK, PAIR_W), BF16)] * 2,
        params=_params(1, blocks, temp_bytes=16 * 2**20), comm=comm)[0]


def _attn_bwd(projp, dao, sinks, *, seq, q_blk, k_blk, v_blk, name, comm=None):
    T = projp.shape[0]
    QW = N_Q_HEADS * HEAD_DIM
    nblk = seq // ATT_BLOCK

    def body(q_ref, k_ref, v_ref, do_ref, sink_ref, dq_ref, dk_ref, dv_ref, dsink_ref, k2s, v2s, dkacc, dvacc):
        _dup_heads(k_ref, k2s, seq)
        _dup_heads(v_ref, v2s, seq)
        dkacc[...] = jnp.zeros(dkacc.shape, F32)
        dvacc[...] = jnp.zeros(dvacc.shape, F32)
        lane = lax.broadcasted_iota(jnp.int32, (1, PAIR_W), 1)
        lo = lane < HEAD_DIM
        bias0, first_pen = _band_bias()
        sink_rows = [_sink_row(sink_ref, g) for g in range(N_KV_HEADS)]

        def blk(n, dsink):
            r0 = pl.multiple_of(n * ATT_BLOCK, ATT_BLOCK)
            band = pl.ds(r0, 2 * ATT_BLOCK)
            qb = q_ref[pl.ds(r0, ATT_BLOCK), :]
            dob = do_ref[pl.ds(r0, ATT_BLOCK), :]
            bias = bias0 + jnp.where(n == 0, 1.0, 0.0) * first_pen
            for g in range(N_KV_HEADS):
                qs = _stack_heads(qb, g, lo)
                dos = _stack_heads(dob, g, lo)
                k2 = k2s[g, band, :]
                v2 = v2s[g, band, :]
                probs_t, psink = _group_probs(qs, k2, bias, sink_rows[g])
                dp_t = _dot_nt(v2, dos)
                delta = jnp.sum(probs_t * dp_t, axis=0, keepdims=True)
                ds_t = (probs_t * (dp_t - delta) * (HEAD_DIM ** -0.5)).astype(BF16)
                tsink = psink * delta
                for h in range(GQA_GROUP):
                    dsink = dsink + jnp.where(lane == g * GQA_GROUP + h,
                                              -jnp.sum(tsink[:, h * ATT_BLOCK:(h + 1) * ATT_BLOCK]), 0.0)
                _unstack_heads(_dot_tn(ds_t, k2), dq_ref, r0, g, lo)
                dkacc[g, band, :] = dkacc[g, band, :] + _dot(ds_t, qs)
                dvacc[g, band, :] = dvacc[g, band, :] + _dot(probs_t.astype(BF16), dos)
            return dsink

        dsink = lax.fori_loop(0, nblk, blk, jnp.zeros((1, PAIR_W), F32))
        _acc(dsink_ref, dsink, pl.program_id(0) == 0)

        def fold(acc, g):
            a = acc[g, pl.ds(ATT_BLOCK, seq), :]
            return a + pltpu.roll(a, HEAD_DIM, 1)

        dk_ref[...] = jnp.where(lo, fold(dkacc, 0), fold(dkacc, 1)).astype(BF16)
        dv_ref[...] = jnp.where(lo, fold(dvacc, 0), fold(dvacc, 1)).astype(BF16)

    blocks = [((seq, QW), BF16)] * 3 + [((seq, KV_WIDTH), BF16)] * 4
    kv_spec_out = pl.BlockSpec((seq, KV_WIDTH), lambda b: (b, 0))
    return _call(
        body, (projp, projp, projp, dao, sinks), name=name, grid=(T // seq,),
        in_specs=[pl.BlockSpec((seq, QW), lambda b: (b, q_blk)),
                  pl.BlockSpec((seq, KV_WIDTH), lambda b: (b, k_blk)),
                  pl.BlockSpec((seq, KV_WIDTH), lambda b: (b, v_blk)),
                  pl.BlockSpec((seq, QW), lambda b: (b, 0)),
                  pl.BlockSpec(memory_space=pltpu.SMEM)],
        out_specs=[pl.BlockSpec((seq, QW), lambda b: (b, 0)), kv_spec_out, kv_spec_out,
                   pl.BlockSpec((1, 128), lambda b: (0, 0))],
        out_shape=[SDS((T, QW), BF16), SDS((T, KV_WIDTH), BF16), SDS((T, KV_WIDTH), BF16), SDS((1, 128), F32)],
        scratch_shapes=[pltpu.VMEM((N_KV_HEADS, seq + ATT_BLOCK, PAIR_W), BF16)] * 2
        + [pltpu.VMEM((N_KV_HEADS, seq + ATT_BLOCK, PAIR_W), F32)] * 2,
        params=_params(1, blocks, temp_bytes=24 * 2**20), comm=comm)


SUBLANES = 8


def _sublane_shifts(win):
    n = CONV_ROWS + CONV_HALO
    return [win] + [pltpu.roll(win, n - b, 0) for b in range(1, SUBLANES)]


def _window(shifted, off):
    a = off // SUBLANES * SUBLANES
    return shifted[off % SUBLANES][a:a + CONV_ROWS, :]


def _conv_fwd(projp, w, bias, *, seq, cw, a_col, b_col, name, comm=None):
    T = projp.shape[0]
    C = w.shape[1]
    nchunk = seq // CONV_ROWS

    def body(a_ref, b_ref, w_ref, bias_ref, y_ref, upad):
        upad[pl.ds(0, CONV_HALO), :] = jnp.zeros((CONV_HALO, cw), F32)
        upad[pl.ds(CONV_HALO, seq), :] = a_ref[...].astype(F32) * _sigmoid(b_ref[...].astype(F32))
        wv = w_ref[...]
        bv = bias_ref[...]

        def chunk(r, carry):
            r0 = pl.multiple_of(r * CONV_ROWS, CONV_ROWS)
            shifted = _sublane_shifts(upad[pl.ds(r0, CONV_ROWS + CONV_HALO), :])
            acc = jnp.broadcast_to(bv, (CONV_ROWS, cw))
            for k in range(CONV_WIDTH):
                acc = acc + wv[k:k + 1, :] * _window(shifted, CONV_HALO - (CONV_WIDTH - 1) + k)
            y_ref[pl.ds(r0, CONV_ROWS), :] = acc
            return carry

        lax.fori_loop(0, nchunk, chunk, 0)

    blocks = [((seq, cw), BF16)] * 2 + [((seq, cw), F32)]
    return _call(
        body, (projp, projp, w, bias), name=name, grid=(T // seq, C // cw),
        in_specs=[pl.BlockSpec((seq, cw), lambda b, c: (b, a_col // cw + c)),
                  pl.BlockSpec((seq, cw), lambda b, c: (b, b_col // cw + c)),
                  pl.BlockSpec((CONV_WIDTH, cw), lambda b, c: (0, c)),
                  pl.BlockSpec((1, cw), lambda b, c: (0, c))],
        out_specs=[pl.BlockSpec((seq, cw), lambda b, c: (b, c))],
        out_shape=[SDS((T, C), F32)],
        scratch_shapes=[pltpu.VMEM((seq + CONV_HALO, cw), F32)],
        params=_params(2, blocks, temp_bytes=6 * _nbytes((seq, cw), F32)), comm=comm)[0]


def _conv_bwd(dy, projp, w, *, seq, cw, a_col, b_col, name, comm=None):
    T = projp.shape[0]
    C = w.shape[1]
    nchunk = seq // CONV_ROWS
    SUB = 8

    def body(dy_ref, a_ref, b_ref, w_ref, da_ref, db_ref, dw_ref, dbias_ref, dypad, dwp):
        first = pl.program_id(1) == 0
        dyv = dy_ref[...]
        dypad[pl.ds(0, seq), :] = dyv
        dypad[pl.ds(seq, CONV_HALO), :] = jnp.zeros((CONV_HALO, cw), F32)
        dwp[...] = jnp.zeros(dwp.shape, F32)
        wv = w_ref[...]

        def chunk(r, carry):
            r0 = pl.multiple_of(r * CONV_ROWS, CONV_ROWS)
            dy_shifts = _sublane_shifts(dypad[pl.ds(r0, CONV_ROWS + CONV_HALO), :])
            ac = a_ref[pl.ds(r0, CONV_ROWS), :].astype(F32)
            sbc = _sigmoid(b_ref[pl.ds(r0, CONV_ROWS), :].astype(F32))
            uc = ac * sbc
            du = jnp.zeros((CONV_ROWS, cw), F32)
            for k in range(CONV_WIDTH):
                dyk = _window(dy_shifts, CONV_WIDTH - 1 - k)
                du = du + wv[k:k + 1, :] * dyk
                prod = uc * dyk
                part = prod[0:SUB, :]
                for s in range(1, CONV_ROWS // SUB):
                    part = part + prod[s * SUB:(s + 1) * SUB, :]
                dwp[pl.ds(k * SUB, SUB), :] = dwp[pl.ds(k * SUB, SUB), :] + part
            da_ref[pl.ds(r0, CONV_ROWS), :] = (du * sbc).astype(BF16)
            db_ref[pl.ds(r0, CONV_ROWS), :] = (du * ac * (sbc * (1.0 - sbc))).astype(BF16)
            return carry

        lax.fori_loop(0, nchunk, chunk, 0)

        @pl.when(first)
        def _():
            dw_ref[...] = jnp.zeros(dw_ref.shape, F32)
            dbias_ref[...] = jnp.zeros(dbias_ref.shape, F32)

        for k in range(CONV_WIDTH):
            dw_ref[k:k + 1, :] = dw_ref[k:k + 1, :] + _rowsum(dwp[pl.ds(k * SUB, SUB), :])
        dbias_ref[...] = dbias_ref[...] + _rowsum(dyv)

    blocks = [((seq, cw), F32)] + [((seq, cw), BF16)] * 4
    return _call(
        body, (dy, projp, projp, w), name=name, grid=(C // cw, T // seq),
        in_specs=[pl.BlockSpec((seq, cw), lambda c, b: (b, c)),
                  pl.BlockSpec((seq, cw), lambda c, b: (b, a_col // cw + c)),
                  pl.BlockSpec((seq, cw), lambda c, b: (b, b_col // cw + c)),
                  pl.BlockSpec((CONV_WIDTH, cw), lambda c, b: (0, c))],
        out_specs=[pl.BlockSpec((seq, cw), lambda c, b: (b, c)), pl.BlockSpec((seq, cw), lambda c, b: (b, c)),
                   pl.BlockSpec((CONV_WIDTH, cw), lambda c, b: (0, c)), pl.BlockSpec((1, cw), lambda c, b: (0, c))],
        out_shape=[SDS((T, C), BF16), SDS((T, C), BF16), SDS((CONV_WIDTH, C), F32), SDS((1, C), F32)],
        scratch_shapes=[pltpu.VMEM((seq + CONV_HALO, cw), F32), pltpu.VMEM((CONV_WIDTH * SUB, cw), F32)],
        params=_params(2, blocks, temp_bytes=8 * _nbytes((seq, cw), F32)), comm=comm)


def _matmul_tn(a, b, *, name, gate=None, comm=None):
    T, M = a.shape
    N = b.shape[1]
    bm = _pick(M, (768, 512, 256))
    lhs = [a] if gate is None else [a, gate]

    def body(*refs):
        b_ref, o_ref = refs[len(lhs)], refs[len(lhs) + 1]
        av = refs[0][...]
        if gate is not None:
            af = av.astype(F32)
            av = (af * _sigmoid(af) * refs[1][...].astype(F32)).astype(BF16)
        o_ref[...] = _dot_tn(av, b_ref[...]).astype(BF16)

    blocks = [((T, bm), BF16)] * len(lhs) + [((T, N), BF16), ((bm, N), BF16)]
    return _call(
        body, (*lhs, b), name=name, grid=(M // bm,),
        in_specs=[pl.BlockSpec((T, bm), lambda i: (0, i))] * len(lhs) + [pl.BlockSpec((T, N), lambda i: (0, 0))],
        out_specs=[pl.BlockSpec((bm, N), lambda i: (i, 0))],
        out_shape=[SDS((M, N), BF16)],
        params=_params(1, blocks, temp_bytes=(2 + 4 * len(lhs)) * _nbytes((T, bm), BF16) + 2 * _nbytes((bm, N), F32)),
        comm=comm)[0]


TN_BLOCK = 256


def _matmul_tn_pieces(groups, b, *, name, comm=None):
    T, N = b.shape
    flat = [a for g in groups for a in g]
    starts, n_steps = [], 0
    for g in groups:
        width = sum(a.shape[1] for a in g)
        assert width % TN_BLOCK == 0 and (len(g) == 1 or width == TN_BLOCK), [a.shape for a in g]
        starts.append(n_steps)
        n_steps += width // TN_BLOCK

    def body(*refs):
        a_refs, b_ref, o_ref = refs[:len(flat)], refs[len(flat)], refs[len(flat) + 1]
        i = pl.program_id(0)
        at = 0
        for g, start in zip(groups, starts):
            mine = a_refs[at:at + len(g)]
            at += len(g)
            steps = sum(a.shape[1] for a in g) // TN_BLOCK

            @pl.when(jnp.logical_and(i >= start, i < start + steps))
            def _(mine=mine):
                a = mine[0][...] if len(mine) == 1 else jnp.concatenate([r[...] for r in mine], axis=1)
                o_ref[...] = _dot_tn(a, b_ref[...]).astype(BF16)

    a_specs = []
    for g, start in zip(groups, starts):
        for a in g:
            if len(g) == 1:
                last = a.shape[1] // TN_BLOCK - 1
                a_specs.append(pl.BlockSpec(
                    (T, TN_BLOCK), lambda i, start=start, last=last: (0, jnp.clip(i - start, 0, last))))
            else:
                a_specs.append(pl.BlockSpec((T, a.shape[1]), lambda i: (0, 0)))
    blocks = [((T, TN_BLOCK), BF16)] * len(flat) + [((T, N), BF16), ((TN_BLOCK, N), BF16)]
    return _call(
        body, (*flat, b), name=name, grid=(n_steps,),
        in_specs=a_specs + [pl.BlockSpec((T, N), lambda i: (0, 0))],
        out_specs=[pl.BlockSpec((TN_BLOCK, N), lambda i: (i, 0))],
        out_shape=[SDS((n_steps * TN_BLOCK, N), BF16)],
        params=_params(1, blocks, temp_bytes=2 * _nbytes((T, TN_BLOCK), BF16) + 2 * _nbytes((TN_BLOCK, N), F32)),
        comm=comm)[0]


def _sum_parts(p_ref):
    g = p_ref[0].astype(F32)
    for s in range(1, p_ref.shape[0]):
        g = g + p_ref[s].astype(F32)
    return g


def _pair_add(g, staged, *, name):
    _, R, W = g.shape
    nq = staged.shape[0]
    tr = _row_tile(R)

    def body(g_ref, s_ref, o_ref):
        mine = jnp.where(lax.axis_index("c") == 0, g_ref[0, 0].astype(F32), g_ref[0, 1].astype(F32))
        o_ref[0] = (mine + s_ref[0].astype(F32)).astype(o_ref.dtype)

    return _call(
        body, (g.reshape(nq, 2, R, W), staged), name=name, grid=(nq, R // tr),
        in_specs=[pl.BlockSpec((1, 2, tr, W), lambda q, i: (q, 0, i, 0)),
                  pl.BlockSpec((1, tr, W), lambda q, i: (q, i, 0))],
        out_specs=[pl.BlockSpec((1, tr, W), lambda q, i: (q, i, 0))],
        out_shape=[SDS((nq, R, W), g.dtype)],
        params=_params(2, [((4, tr, W), g.dtype)], temp_bytes=3 * _nbytes((tr, W), F32)))[0]


def _adamw_update(w, g, m, v):
    m = ADAM_B1 * m + (1.0 - ADAM_B1) * g
    v = ADAM_B2 * v + (1.0 - ADAM_B2) * (g * g)
    m_hat = m / (1.0 - ADAM_B1 ** ADAM_STEP)
    v_hat = v / (1.0 - ADAM_B2 ** ADAM_STEP)
    delta = -ADAM_LR * (m_hat / (jnp.sqrt(v_hat) + ADAM_EPS) + ADAM_WD * w)
    return delta, m, v


def _row_tile(R):
    return _pick(R, (256, 128, 112, 88, 64, 32, 16, 8))


def _sum8(parts, *, name):
    n, R, W = parts.shape
    tr = _row_tile(R)

    def body(p_ref, o_ref):
        o_ref[...] = _sum_parts(p_ref)

    return _call(
        body, (parts,), name=name, grid=(R // tr,),
        in_specs=[pl.BlockSpec((n, tr, W), lambda i: (0, i, 0))],
        out_specs=[pl.BlockSpec((tr, W), lambda i: (i, 0))],
        out_shape=[SDS((R, W), F32)],
        params=_params(1, [((n, tr, W), parts.dtype), ((tr, W), F32)]))[0]


def _adamw(g, w, m, v, *, name):
    R, W = w.shape
    tr = _row_tile(R)

    def body(g_ref, w_ref, m_ref, v_ref, d_ref, mo_ref, vo_ref):
        d_ref[...], mo_ref[...], vo_ref[...] = _adamw_update(w_ref[...], g_ref[...], m_ref[...], v_ref[...])

    spec = pl.BlockSpec((tr, W), lambda i: (i, 0))
    return _call(
        body, (g, w, m, v), name=name, grid=(R // tr,),
        in_specs=[spec] * 4, out_specs=[spec] * 3, out_shape=[SDS((R, W), F32)] * 3,
        params=_params(1, [((tr, W), F32)] * 7))


def _sum8_adamw(parts, w, m, v, *, name):
    R, W = w.shape
    n = parts.shape[0]
    tr = _row_tile(R)

    def body(p_ref, w_ref, m_ref, v_ref, g_ref, d_ref, mo_ref, vo_ref):
        g = _sum_parts(p_ref)
        g_ref[...] = g
        d_ref[...], mo_ref[...], vo_ref[...] = _adamw_update(w_ref[...], g, m_ref[...], v_ref[...])

    spec = pl.BlockSpec((tr, W), lambda i: (i, 0))
    return _call(
        body, (parts, w, m, v), name=name, grid=(R // tr,),
        in_specs=[pl.BlockSpec((n, tr, W), lambda i: (0, i, 0))] + [spec] * 3,
        out_specs=[spec] * 4, out_shape=[SDS((R, W), F32)] * 4,
        params=_params(1, [((n, tr, W), parts.dtype)] + [((tr, W), F32)] * 7))


def _ada_fwd(c_all, w, bias, *, name):
    NB, D = c_all.shape
    N = w.shape[1]

    def body(c_ref, w_ref, b_ref, o_ref):
        cv = c_ref[...]
        ca = (cv * _sigmoid(cv)).astype(BF16)
        o_ref[...] = _dot(ca, w_ref[...].astype(BF16)) + b_ref[...]

    full = lambda s: pl.BlockSpec(s, lambda i: (0,) * len(s))
    return _call(
        body, (c_all, w, bias), name=name, grid=(1,),
        in_specs=[full((NB, D)), full((D, N)), full((1, N))], out_specs=[full((NB, N))],
        out_shape=[SDS((NB, N), F32)],
        params=_params(1, [((D, N), F32)], temp_bytes=_nbytes((D, N), BF16)))[0]


def _ada_bwd(c_all, gmod_all, *, n_col, name):
    NB, D = c_all.shape
    N = gmod_all.shape[1]

    def body(c_ref, g_ref, gw_ref, gb_ref):
        cv = c_ref[...]
        ca = (cv * _sigmoid(cv)).astype(BF16)
        first = pl.multiple_of(_lin(_my_pos()) * n_col, 128)
        gw_ref[...] = _dot_tn(ca, g_ref[:, pl.ds(first, n_col)].astype(BF16))
        gb_ref[...] = _rowsum(g_ref[...])

    full = lambda s: pl.BlockSpec(s, lambda i: (0,) * len(s))
    return _call(
        body, (c_all, gmod_all), name=name, grid=(1,),
        in_specs=[full((NB, D)), full((NB, N))], out_specs=[full((D, n_col)), full((1, N))],
        out_shape=[SDS((D, n_col), F32), SDS((1, N), F32)],
        params=_params(1, [((D, n_col), F32), ((NB, N), F32)]))


def kernel(x, c, w_ada, b_ada, norm_ffn1_g, ffn1_w_gate, ffn1_w_up, ffn1_w_down, norm_mix_g, w_in, attn_sinks, w_attn_o, conv_w_dw, conv_b_dw, conv_ln_g, conv_ln_b, w_conv_o, w_out, norm_ffn2_g, ffn2_w_gate, ffn2_w_up, ffn2_w_down, final_norm_g, loss_target, m_w_ada, m_b_ada, m_norm_ffn1_g, m_ffn1_w_gate, m_ffn1_w_up, m_ffn1_w_down, m_norm_mix_g, m_w_in, m_attn_sinks, m_w_attn_o, m_conv_w_dw, m_conv_b_dw, m_conv_ln_g, m_conv_ln_b, m_w_conv_o, m_w_out, m_norm_ffn2_g, m_ffn2_w_gate, m_ffn2_w_up, m_ffn2_w_down, m_final_norm_g, v_w_ada, v_b_ada, v_norm_ffn1_g, v_ffn1_w_gate, v_ffn1_w_up, v_ffn1_w_down, v_norm_mix_g, v_w_in, v_attn_sinks, v_w_attn_o, v_conv_w_dw, v_conv_b_dw, v_conv_ln_g, v_conv_ln_b, v_w_conv_o, v_w_out, v_norm_ffn2_g, v_ffn2_w_gate, v_ffn2_w_up, v_ffn2_w_down, v_final_norm_g):
    B, S, D = x.shape
    T = B * S
    QW = N_Q_HEADS * HEAD_DIM
    CC = conv_w_dw.shape[2] * N_DEV
    me = _lin(_my_pos())
    xf = x.reshape(T, D)
    tgt = loss_target.reshape(T, D)
    tm = min(512, S)
    kw = dict(seq=S, tm=tm)

    p_k, p_v, p_ca = QW, QW + KV_WIDTH, QW + 2 * KV_WIDTH
    p_cb, p_ga, p_gc = p_ca + CC, p_ca + 2 * CC, p_ca + 2 * CC + D

    def col_t(w):
        return w[0].T.astype(BF16)

    def row_b(w):
        return w[0].astype(BF16)

    def rows(g):
        return g.reshape(-1, g.shape[-1])

    def blocks8(g):
        return g.reshape(N_DEV, g.shape[0] // N_DEV, g.shape[1])

    def gather(*arrs):
        return _Comm([(a, "gather") for a in arrs])

    g_wg1, g_convw, g_c = _exchange(
        [(col_t(ffn1_w_gate), "gather"), (conv_w_dw[0], "gather"), (c, "gather")], name="gather_first")
    wg1 = rows(g_wg1)
    conv_w = g_convw.transpose(1, 0, 2).reshape(CONV_WIDTH, CC)
    c_all = g_c.reshape(N_DEV * B, D)

    n_col = N_MOD * D // N_DEV
    b_cols = lax.dynamic_slice(b_ada, (0, me * n_col), (1, n_col))
    mod_cols = _ada_fwd(c_all, w_ada[0], b_cols, name="ada_fwd")
    mod_mine = _exchange([(mod_cols.reshape(N_DEV, B, n_col), "scatter")], name="scatter_mod")[0]
    mod = mod_mine.transpose(1, 0, 2).reshape(B * N_MOD, 1, D)
    sh1, sc1, g1, sh2, sc2, g2, sh3, sc3, g3 = [_ModVec(mod, i) for i in range(N_MOD)]

    F = wg1.shape[0]
    tn_f = _pick(F, (1408, 1024, 512, 256))
    tn_in = _pick(w_in.shape[2] * N_DEV, (1792, 768, 512, 256))
    gate_blk = dict(ga_col=p_ga, gc_col=p_gc)
    att_blk = dict(q_blk=0, k_blk=p_k // KV_WIDTH, v_blk=p_v // KV_WIDTH)
    conv_kw = dict(seq=S, cw=256, a_col=p_ca, b_col=p_cb)

    cm = gather(col_t(ffn1_w_up))
    h1, (a1,) = _norm_mod_matmul(xf, norm_ffn1_g, sh1, sc1, [wg1], tn=tn_f, name="ffn1_gate", comm=cm, **kw)
    wu1 = rows(cm.out[0])
    cm = gather(row_b(ffn1_w_down))
    b1 = _matmul_nt(h1, wu1, tm=tm, tn=tn_f, name="ffn1_up", comm=cm)
    wd1 = rows(cm.out[0])
    cm = gather(col_t(w_in))
    x1, y1 = _ffn_down(a1, b1, wd1, xf, g1, name="ffn1_down", comm=cm, **kw)
    winp = rows(cm.out[0])
    cm = gather(row_b(w_attn_o), row_b(w_conv_o), row_b(w_out), col_t(ffn2_w_gate))
    h2, (projp,) = _norm_mod_matmul(x1, norm_mix_g, sh2, sc2, [winp], tn=tn_in, name="mix_in", comm=cm, **kw)
    wao, wco, wout, wg2 = [rows(o) for o in cm.out]
    cm = gather(col_t(ffn2_w_up))
    ao = _attn_fwd(projp, attn_sinks, seq=S, name="attn_fwd", comm=cm, **att_blk)
    wu2 = rows(cm.out[0])
    cm = gather(row_b(ffn2_w_down))
    yc = _conv_fwd(projp, conv_w, conv_b_dw, name="conv_fwd", comm=cm, **conv_kw)
    wd2 = rows(cm.out[0])
    x2, z, ya, ycv, cact, merged = _mix_out(ao, yc, projp, wao, wco, wout, x1, g2, conv_ln_g, conv_ln_b,
                                            name="mix_out", **gate_blk, **kw)
    h3, (a3, b3) = _norm_mod_matmul(x2, norm_ffn2_g, sh3, sc3, [wg2, wu2], tn=tn_f, name="ffn2_up", **kw)
    x3, y3 = _ffn_down(a3, b3, wd2, x2, g3, name="ffn2_down", **kw)
    dx3, loss_row, dgf = _final_loss(x3, final_norm_g[None], tgt, tm=tm, name="final_loss")

    parts = {}

    def pair(*gs):
        return [(blocks8(g), "pair") for g in gs]

    def cross(*rs):
        return [(r, "cross") for r in rs]

    def reduce_pairs(gs, staged, names):
        return [_pair_add(blocks8(g), s, name="pair_add_" + n) for g, s, n in zip(gs, staged, names)]

    dyb3, da3, db3, dg3 = _ffn_bwd_down(dx3, g3, y3, wd2, a3, b3, tn=tn_f, name="ffn2_bwd_down", **kw)
    gwd2 = _matmul_tn(a3, dyb3, gate=b3, name="gw_ffn2_down")
    cm = _Comm(pair(gwd2))
    dx2, dsh3, dsc3, dgn3 = _matmul_norm_mod_bwd([[da3], [db3]], [wg2, wu2], x2, norm_ffn2_g, sc3, dx3,
                                                 name="ffn2_bwd_up", out_dtype=GRAD_STREAM, comm=cm, **kw)
    r_wd2, = reduce_pairs([gwd2], cm.out, ["ffn2_w_down"])
    cm = _Comm(cross(r_wd2))
    gwg2 = _matmul_tn(da3, h3, name="gw_ffn2_gate", comm=cm)
    parts["ffn2_w_down"], = cm.out
    cm = _Comm(pair(gwg2))
    gwu2 = _matmul_tn(db3, h3, name="gw_ffn2_up", comm=cm)
    r_wg2, = reduce_pairs([gwg2], cm.out, ["ffn2_w_gate"])

    cm = _Comm(cross(r_wg2) + pair(gwu2))
    dzb, dyab, dycb, dga, dgc, dao, dyc, dg2, dlng, dlnb = _mix_out_bwd(
        dx2, g2, z, wout, projp, ya, ycv, wao, wco, yc, conv_ln_g, conv_ln_b, name="mix_out_bwd", comm=cm,
        **gate_blk, **kw)
    parts["ffn2_w_gate"] = cm.out[0]
    r_wu2, = reduce_pairs([gwu2], cm.out[1:], ["ffn2_w_up"])
    gwout = _matmul_tn(merged, dzb, name="gw_out")
    gwao = _matmul_tn(ao, dyab, name="gw_attn_o")
    gwco = _matmul_tn(cact, dycb, name="gw_conv_o")
    cm = _Comm(cross(r_wu2) + pair(gwout, gwao, gwco))
    dq, dk, dv, dsinks = _attn_bwd(projp, dao, attn_sinks, seq=S, name="attn_bwd", comm=cm, **att_blk)
    parts["ffn2_w_up"] = cm.out[0]
    r_mix = reduce_pairs([gwout, gwao, gwco], cm.out[1:], ["w_out", "w_attn_o", "w_conv_o"])
    cm = _Comm(cross(*r_mix))
    dca, dcb, dconvw, dconvb = _conv_bwd(dyc, projp, conv_w, name="conv_bwd", comm=cm, **conv_kw)
    parts["w_out"], parts["w_attn_o"], parts["w_conv_o"] = cm.out
    gwin = _matmul_tn_pieces([[dq], [dk, dv], [dca], [dcb], [dga], [dgc]], h2, name="gw_in")
    cm = _Comm(pair(gwin))
    dx1, dsh2, dsc2, dgn2 = _matmul_norm_mod_bwd([[dq, dk, dv, dca, dcb, dga, dgc]], [winp], x1, norm_mix_g, sc2, dx2,
                                                 name="mix_in_bwd", out_dtype=GRAD_STREAM, comm=cm, **kw)
    r_win, = reduce_pairs([gwin], cm.out, ["w_in"])

    cm = _Comm(cross(r_win))
    dyb1, da1, db1, dg1 = _ffn_bwd_down(dx1, g1, y1, wd1, a1, b1, tn=tn_f, name="ffn1_bwd_down", comm=cm,
                                              **kw)
    parts["w_in"], = cm.out
    gwd1 = _matmul_tn(a1, dyb1, gate=b1, name="gw_ffn1_down")
    cm = _Comm(pair(gwd1))
    gwg1 = _matmul_tn(da1, h1, name="gw_ffn1_gate", comm=cm)
    r_wd1, = reduce_pairs([gwd1], cm.out, ["ffn1_w_down"])
    cm = _Comm(cross(r_wd1) + pair(gwg1))
    gwu1 = _matmul_tn(db1, h1, name="gw_ffn1_up", comm=cm)
    parts["ffn1_w_down"] = cm.out[0]
    r_wg1, = reduce_pairs([gwg1], cm.out[1:], ["ffn1_w_gate"])
    r_wu1, = reduce_pairs([gwu1], _exchange(pair(gwu1), name="pair_last"), ["ffn1_w_up"])
    cm = _Comm(cross(r_wg1, r_wu1))
    dx0, dsh1, dsc1, dgn1 = _matmul_norm_mod_bwd([[da1], [db1]], [wg1, wu1], xf, norm_ffn1_g, sc1, dx1,
                                                 name="ffn1_bwd_up", out_dtype=F32, comm=cm, **kw)
    parts["ffn1_w_gate"], parts["ffn1_w_up"] = cm.out

    n_small = 8
    gmod = jnp.concatenate([dsh1, dsc1, dg1, dsh2, dsc2, dg2, dsh3, dsc3, dg3], axis=1).reshape(B, N_MOD * D)
    sink_row = jnp.pad(dsinks[:, :N_Q_HEADS], ((0, 0), (0, D - N_Q_HEADS)))
    loss_pad = jnp.pad(loss_row, ((0, 0), (0, D - loss_row.shape[1])))
    small = jnp.concatenate([dgn1, dgn2, dgn3, dgf, dconvb, dlng, dlnb, sink_row, dconvw, loss_pad], axis=0)
    small_all, gmod_all = _exchange([(small, "gather"), (gmod, "gather")], name="exchange_last")
    gsmall = _sum8(small_all, name="sum_small")
    loss = gsmall[n_small + CONV_WIDTH, 0]
    g_w_ada, g_b_ada = _ada_bwd(c_all, gmod_all.reshape(N_DEV * B, N_MOD * D), n_col=n_col, name="ada_bwd")
    g_conv_w = lax.dynamic_slice(gsmall[n_small:n_small + CONV_WIDTH], (0, me * (CC // N_DEV)),
                                 (CONV_WIDTH, CC // N_DEV))

    def col_update(name, w, m, v):
        outs = _sum8_adamw(parts[name], w[0].T, m[0].T, v[0].T, name="adamw_" + name)
        return tuple(o.T for o in outs)

    def row_update(name, w, m, v):
        return tuple(_sum8_adamw(parts[name], w[0], m[0], v[0], name="adamw_" + name))

    upd = {
        "ffn1_w_gate": col_update("ffn1_w_gate", ffn1_w_gate, m_ffn1_w_gate, v_ffn1_w_gate),
        "ffn1_w_up": col_update("ffn1_w_up", ffn1_w_up, m_ffn1_w_up, v_ffn1_w_up),
        "ffn1_w_down": row_update("ffn1_w_down", ffn1_w_down, m_ffn1_w_down, v_ffn1_w_down),
        "w_in": col_update("w_in", w_in, m_w_in, v_w_in),
        "w_attn_o": row_update("w_attn_o", w_attn_o, m_w_attn_o, v_w_attn_o),
        "w_conv_o": row_update("w_conv_o", w_conv_o, m_w_conv_o, v_w_conv_o),
        "w_out": row_update("w_out", w_out, m_w_out, v_w_out),
        "ffn2_w_gate": col_update("ffn2_w_gate", ffn2_w_gate, m_ffn2_w_gate, v_ffn2_w_gate),
        "ffn2_w_up": col_update("ffn2_w_up", ffn2_w_up, m_ffn2_w_up, v_ffn2_w_up),
        "ffn2_w_down": row_update("ffn2_w_down", ffn2_w_down, m_ffn2_w_down, v_ffn2_w_down),
        "w_ada": (g_w_ada,) + tuple(_adamw(g_w_ada, w_ada[0], m_w_ada[0], v_w_ada[0], name="adamw_w_ada")),
        "conv_w_dw": (g_conv_w,) + tuple(_adamw(g_conv_w, conv_w_dw[0], m_conv_w_dw[0], v_conv_w_dw[0],
                                                name="adamw_conv_w_dw")),
    }
    for k in upd:
        upd[k] = tuple(t[None] for t in upd[k])

    def pad_sinks(t):
        return jnp.pad(t, ((0, 0), (0, D - N_Q_HEADS)))

    def pack(f1, mix, f2, fin, cb, lg, lb, sinks, bada):
        return jnp.concatenate([f1, mix, f2, fin[None], cb, lg, lb, pad_sinks(sinks), bada.reshape(N_MOD, D)], axis=0)

    w_s = pack(norm_ffn1_g, norm_mix_g, norm_ffn2_g, final_norm_g, conv_b_dw, conv_ln_g, conv_ln_b, attn_sinks, b_ada)
    m_s = pack(m_norm_ffn1_g, m_norm_mix_g, m_norm_ffn2_g, m_final_norm_g, m_conv_b_dw, m_conv_ln_g, m_conv_ln_b,
               m_attn_sinks, m_b_ada)
    v_s = pack(v_norm_ffn1_g, v_norm_mix_g, v_norm_ffn2_g, v_final_norm_g, v_conv_b_dw, v_conv_ln_g, v_conv_ln_b,
               v_attn_sinks, v_b_ada)
    g_s = jnp.concatenate([gsmall[:n_small], g_b_ada.reshape(N_MOD, D)], axis=0)
    small_out = (g_s,) + tuple(_adamw(g_s, w_s, m_s, v_s, name="adamw_vectors"))

    def unpack(t):
        return {
            "norm_ffn1_g": t[0:1], "norm_mix_g": t[1:2], "norm_ffn2_g": t[2:3], "final_norm_g": t[3],
            "conv_b_dw": t[4:5], "conv_ln_g": t[5:6], "conv_ln_b": t[6:7], "attn_sinks": t[7:8, :N_Q_HEADS],
            "b_ada": t[n_small:n_small + N_MOD].reshape(1, N_MOD * D),
        }

    small_un = [unpack(t) for t in small_out]
    for k in small_un[0]:
        upd[k] = tuple(s[k] for s in small_un)

    order = ["w_ada", "b_ada", "norm_ffn1_g", "ffn1_w_gate", "ffn1_w_up", "ffn1_w_down", "norm_mix_g", "w_in",
             "attn_sinks", "w_attn_o", "conv_w_dw", "conv_b_dw", "conv_ln_g", "conv_ln_b", "w_conv_o", "w_out",
             "norm_ffn2_g", "ffn2_w_gate", "ffn2_w_up", "ffn2_w_down", "final_norm_g"]
    grad_x = dx0.reshape(B, S, D)
    return (loss, grad_x, *[upd[k][0] for k in order], *[upd[k][1] for k in order],
            *[upd[k][2] for k in order], *[upd[k][3] for k in order])
```

```python
import dataclasses

import jax
import jax.numpy as jnp
from jax import lax
from jax.experimental import pallas as pl
from jax.experimental.pallas import tpu as pltpu

F32 = jnp.float32
BF16 = jnp.bfloat16
SDS = jax.ShapeDtypeStruct
MESH = pl.DeviceIdType.MESH

N_DEV = 8
EPS = 1e-6
HEAD_DIM = 64
N_Q_HEADS = 16
N_KV_HEADS = 2
GQA_GROUP = N_Q_HEADS // N_KV_HEADS
KV_WIDTH = N_KV_HEADS * HEAD_DIM
ATT_BLOCK = 128
CONV_WIDTH = 31
CONV_HALO = 32
CONV_ROWS = 128
N_MOD = 9
FFN_RESIDUAL = 0.5
ADAM_LR = 0.001
ADAM_B1 = 0.9
ADAM_B2 = 0.999
ADAM_EPS = 1e-08
ADAM_WD = 0.01
ADAM_STEP = 10
NEG_BIG = -1e30
GRAD_STREAM = BF16

V7X_VMEM_BYTES = 64 * 2**20
VMEM_CAP = V7X_VMEM_BYTES - 8 * 2**20


def _nbytes(shape, dtype):
    n = 1
    for s in shape:
        n *= s
    return n * jnp.dtype(dtype).itemsize


def _params(n_axes, blocks, temp_bytes=0):
    need = 2 * sum(_nbytes(s, d) for s, d in blocks) + temp_bytes + 4 * 2**20
    return pltpu.CompilerParams(dimension_semantics=("arbitrary",) * n_axes,
                                vmem_limit_bytes=int(min(max(need, 16 * 2**20), VMEM_CAP)))


def _dot_nt(a, b):
    return lax.dot_general(a, b, (((1,), (1,)), ((), ())), preferred_element_type=F32)


def _dot_tn(a, b):
    return lax.dot_general(a, b, (((0,), (0,)), ((), ())), preferred_element_type=F32)


def _dot(a, b):
    return jnp.dot(a, b, preferred_element_type=F32)


def _sigmoid(x):
    return jax.nn.sigmoid(x)


def _rowsum(v):
    return jnp.sum(v, axis=0, keepdims=True)


def _acc(ref, val, first):
    @pl.when(first)
    def _():
        ref[...] = val

    @pl.when(jnp.logical_not(first))
    def _():
        ref[...] = ref[...] + val


def _norm_mod(xf, gn, sh, sc):
    rstd = lax.rsqrt(jnp.mean(xf * xf, axis=-1, keepdims=True) + EPS)
    xhat = xf * rstd
    yn = xhat * gn
    return yn * (1.0 + sc) + sh, xhat, rstd, yn


def _pick(n, cands):
    for c in cands:
        if n % c == 0:
            return c
    return n


def _my_pos():
    return lax.axis_index("x"), lax.axis_index("y"), lax.axis_index("c")


def _peer(pos, k):
    x, y, c = pos
    return ((1 - x) if k & 4 else x, (1 - y) if k & 2 else y, (1 - c) if k & 1 else c)


def _lin(pos):
    return 4 * pos[0] + 2 * pos[1] + pos[2]


class _Comm:
    N_COPY = N_DEV - 1
    N_CHIP = N_DEV // 2

    def __init__(self, items):
        self.arrs = [a for a, _ in items]
        self.modes = [m for _, m in items]
        self.n = len(items)
        self.out = None

    def out_shape(self):
        def shape(a, m):
            return {"gather": (N_DEV,) + a.shape, "scatter": a.shape, "pair": (self.N_CHIP,) + a.shape[1:],
                    "cross": a.shape}[m]
        return [SDS(shape(a, m), a.dtype) for a, m in zip(self.arrs, self.modes)]

    def scratch(self):
        return [pltpu.SemaphoreType.DMA((self.n * self.N_COPY,)), pltpu.SemaphoreType.DMA((self.n * self.N_COPY,)),
                pltpu.SemaphoreType.DMA((self.n,))]

    def collective_id(self):
        modes = set(self.modes)
        if "scatter" in modes:
            return 3
        d2d, ici = bool(modes & {"gather", "pair"}), bool(modes & {"gather", "cross"})
        return {(True, False): 0, (False, True): 1, (True, True): 2}[(d2d, ici)]

    def barrier(self):
        x, y, c = _my_pos()
        peers = {0: [(x, y, 1 - c)],
                 1: [(1 - x, y, c), (x, 1 - y, c), (1 - x, 1 - y, c)],
                 2: [(x, y, 1 - c), (1 - x, y, c), (x, 1 - y, c), (1 - x, 1 - y, c)],
                 3: [_peer((x, y, c), k) for k in range(1, N_DEV)]}[self.collective_id()]
        sem = pltpu.get_barrier_semaphore()
        for p in peers:
            pl.semaphore_signal(sem, inc=1, device_id=p, device_id_type=MESH)
        pl.semaphore_wait(sem, len(peers))

    def _plan(self, mode, me):
        x, y, c = me
        sib = (x, y, 1 - c)
        chips = [(1 - x, y), (x, 1 - y), (1 - x, 1 - y)]

        def chip_lin(ch):
            return 2 * ch[0] + ch[1]

        if mode == "scatter":
            peers = [_peer(me, k + 1) for k in range(self.N_COPY)]
            return [(p, ("in", _lin(p)), _lin(me), _lin(p), None) for p in peers], (_lin(me), _lin(me))
        if mode == "gather":
            same = [(*ch, c) for ch in chips]
            other = [(*ch, 1 - c) for ch in chips]
            copies = [(sib, ("in", None), _lin(me), _lin(sib), None)]
            copies += [(p, ("in", None), _lin(me), _lin(p), None) for p in same]
            copies += [(sib, ("out", _lin(p)), _lin(p), _lin(o), 1 + j) for j, (p, o) in enumerate(zip(same, other))]
            return copies, (None, _lin(me))
        if mode == "pair":
            return [(sib, ("in", 2 * q + 1 - c), q, q, None) for q in range(self.N_CHIP)], None
        if mode == "cross":
            mine = chip_lin((x, y))
            return ([((*ch, c), ("in", chip_lin(ch)), mine, chip_lin(ch), None) for ch in chips], (mine, mine))
        raise ValueError(mode)

    def _copy(self, refs, me, i, k, recv):
        srcs, outs, (send_sems, recv_sems, _) = refs
        peer, (where, slot), send_slot, recv_slot, _ = self._plan(self.modes[i], me)[0][k]
        src = srcs[i] if where == "in" else outs[i]
        src = src if slot is None else src.at[slot]
        sem = i * self.N_COPY + k
        return pltpu.make_async_remote_copy(
            src_ref=src, dst_ref=outs[i].at[recv_slot if recv else send_slot], send_sem=send_sems.at[sem],
            recv_sem=recv_sems.at[sem], device_id=peer, device_id_type=MESH)

    def _local(self, refs, me, i):
        srcs, outs, (_, _, loc_sems) = refs
        local = self._plan(self.modes[i], me)[1]
        if local is None:
            return None
        own = srcs[i] if local[0] is None else srcs[i].at[local[0]]
        return pltpu.make_async_copy(own, outs[i].at[local[1]], loc_sems.at[i])

    def start(self, refs):
        me = _my_pos()
        for i in range(self.n):
            local = self._local(refs, me, i)
            if local is not None:
                local.start()
            for k, cp in enumerate(self._plan(self.modes[i], me)[0]):
                if cp[4] is None:
                    self._copy(refs, me, i, k, False).start()

    def forward(self, refs):
        me = _my_pos()
        for i in range(self.n):
            for k, cp in enumerate(self._plan(self.modes[i], me)[0]):
                if cp[4] is not None:
                    self._copy(refs, me, i, cp[4], True).wait_recv()
                    self._copy(refs, me, i, k, False).start()

    def finish(self, refs):
        me = _my_pos()
        plans = [self._plan(m, me)[0] for m in self.modes]
        for i in range(self.n):
            passed_on = [cp[4] for cp in plans[i] if cp[4] is not None]
            for k in range(len(plans[i])):
                if k not in passed_on:
                    self._copy(refs, me, i, k, True).wait_recv()
                self._copy(refs, me, i, k, False).wait_send()
            local = self._local(refs, me, i)
            if local is not None:
                local.wait()


_ANY = pl.BlockSpec(memory_space=pl.ANY)


def _call(body, args, *, name, grid, in_specs, out_specs, out_shape, params, scratch_shapes=(), comm=None):
    in_specs, out_specs, out_shape = list(in_specs), list(out_specs), list(out_shape)
    scratch_shapes = list(scratch_shapes)
    if comm is None:
        return list(pl.pallas_call(body, name=name, grid=grid, in_specs=in_specs, out_specs=out_specs,
                                   out_shape=out_shape, scratch_shapes=scratch_shapes, compiler_params=params)(*args))
    n_in, n_out, n_scr, nc = len(in_specs), len(out_specs), len(scratch_shapes), comm.n
    n_steps = 1
    for g in grid:
        n_steps *= g

    def hosted(*refs):
        ins, c_in = refs[:n_in], refs[n_in:n_in + nc]
        outs = refs[n_in + nc:n_in + nc + n_out]
        c_out = refs[n_in + nc + n_out:n_in + 2 * nc + n_out]
        scr = refs[n_in + 2 * nc + n_out:n_in + 2 * nc + n_out + n_scr]
        sems = refs[n_in + 2 * nc + n_out + n_scr:]
        step = pl.program_id(0)
        for d in range(1, len(grid)):
            step = step * grid[d] + pl.program_id(d)
        c_refs = (c_in, c_out, sems)

        @pl.when(step == 0)
        def _():
            comm.barrier()
            comm.start(c_refs)

        if n_steps >= 3:
            @pl.when(step == n_steps - 2)
            def _():
                comm.forward(c_refs)

        body(*ins, *outs, *scr)

        @pl.when(step == n_steps - 1)
        def _():
            if n_steps < 3:
                comm.forward(c_refs)
            comm.finish(c_refs)

    res = pl.pallas_call(
        hosted, name=name, grid=grid, in_specs=in_specs + [_ANY] * nc, out_specs=out_specs + [_ANY] * nc,
        out_shape=out_shape + comm.out_shape(), scratch_shapes=scratch_shapes + comm.scratch(),
        compiler_params=dataclasses.replace(params, collective_id=comm.collective_id()))(*args, *comm.arrs)
    comm.out = list(res[n_out:])
    return list(res[:n_out])


def _exchange(items, *, name):
    comm = _Comm(items)

    def body(*refs):
        r = (refs[:comm.n], refs[comm.n:2 * comm.n], refs[2 * comm.n:])
        comm.barrier()
        comm.start(r)
        comm.forward(r)
        comm.finish(r)

    return list(pl.pallas_call(body, name=name, out_shape=comm.out_shape(), in_specs=[_ANY] * comm.n,
                               out_specs=[_ANY] * comm.n, scratch_shapes=comm.scratch(),
                               compiler_params=pltpu.CompilerParams(collective_id=comm.collective_id()))(*comm.arrs))


class _ModVec:
    def __init__(self, arr, idx):
        self.arr, self.idx = arr, idx

    def spec(self, tps, n_axes):
        idx, blk = self.idx, (1, 1, self.arr.shape[2])
        if n_axes == 1:
            return pl.BlockSpec(blk, lambda i: (i // tps * N_MOD + idx, 0, 0))
        return pl.BlockSpec(blk, lambda i, j: (i // tps * N_MOD + idx, 0, 0))


def _norm_mod_matmul(x, gn, sh, sc, wts, *, seq, tm, tn, name, comm=None):
    T, D = x.shape
    N = wts[0].shape[0]
    nw = len(wts)
    tps = seq // tm

    def body(x_ref, gn_ref, sh_ref, sc_ref, *rest):
        w_refs, h_ref, o_refs = rest[:nw], rest[nw], rest[nw + 1:]

        @pl.when(pl.program_id(1) == 0)
        def _():
            h_ref[...] = _norm_mod(x_ref[...], gn_ref[...], sh_ref[0], sc_ref[0])[0].astype(BF16)

        h = h_ref[...]
        for w_ref, o_ref in zip(w_refs, o_refs):
            o_ref[...] = _dot_nt(h, w_ref[...]).astype(o_ref.dtype)

    row = pl.BlockSpec((tm, D), lambda i, j: (i, 0))
    vec = pl.BlockSpec((1, D), lambda i, j: (0, 0))
    wspec = pl.BlockSpec((tn, D), lambda i, j: (j, 0))
    ospec = pl.BlockSpec((tm, tn), lambda i, j: (i, j))
    blocks = [((tm, D), F32), ((tm, D), BF16)] + [((tn, D), BF16), ((tm, tn), BF16)] * nw
    outs = _call(
        body, (x, gn, sh.arr, sc.arr, *wts), name=name, grid=(T // tm, N // tn),
        in_specs=[row, vec, sh.spec(tps, 2), sc.spec(tps, 2)] + [wspec] * nw,
        out_specs=[row] + [ospec] * nw,
        out_shape=[SDS((T, D), BF16)] + [SDS((T, N), BF16)] * nw,
        params=_params(2, blocks, temp_bytes=2 * _nbytes((tm, tn), F32) + 3 * _nbytes((tm, D), F32)), comm=comm)
    return outs[0], outs[1:]


def _matmul_nt(h, w, *, tm, tn, name, comm=None):
    T, D = h.shape
    N = w.shape[0]

    def body(h_ref, w_ref, o_ref):
        o_ref[...] = _dot_nt(h_ref[...], w_ref[...]).astype(o_ref.dtype)

    blocks = [((tm, D), BF16), ((tn, D), BF16), ((tm, tn), BF16)]
    return _call(
        body, (h, w), name=name, grid=(T // tm, N // tn),
        in_specs=[pl.BlockSpec((tm, D), lambda i, j: (i, 0)), pl.BlockSpec((tn, D), lambda i, j: (j, 0))],
        out_specs=[pl.BlockSpec((tm, tn), lambda i, j: (i, j))],
        out_shape=[SDS((T, N), BF16)],
        params=_params(2, blocks, temp_bytes=2 * _nbytes((tm, tn), F32)), comm=comm)[0]


def _ffn_down(a, b, wd, x, g, *, seq, tm, name, comm=None):
    T, F = a.shape
    D = wd.shape[1]
    tps = seq // tm

    def body(a_ref, b_ref, wd_ref, x_ref, g_ref, xo_ref, y_ref):
        af = a_ref[...].astype(F32)
        act = (af * _sigmoid(af) * b_ref[...].astype(F32)).astype(BF16)
        y = _dot(act, wd_ref[...])
        xo_ref[...] = x_ref[...] + (FFN_RESIDUAL * g_ref[0]) * y
        y_ref[...] = y.astype(BF16)

    wide = pl.BlockSpec((tm, F), lambda i: (i, 0))
    row = pl.BlockSpec((tm, D), lambda i: (i, 0))
    wspec = pl.BlockSpec((F, D), lambda i: (0, 0))
    blocks = [((tm, F), BF16)] * 2 + [((F, D), BF16), ((tm, D), F32), ((tm, D), F32), ((tm, D), BF16)]
    return _call(
        body, (a, b, wd, x, g.arr), name=name, grid=(T // tm,),
        in_specs=[wide, wide, wspec, row, g.spec(tps, 1)], out_specs=[row, row],
        out_shape=[SDS((T, D), F32), SDS((T, D), BF16)],
        params=_params(1, blocks, temp_bytes=3 * _nbytes((tm, F), F32)), comm=comm)


def _final_loss(x, gf, tgt, *, tm, name):
    T, D = x.shape
    nt = T // tm

    def body(x_ref, gf_ref, t_ref, dx_ref, loss_ref, dgf_ref, lacc):
        i = pl.program_id(0)
        xf = x_ref[...]
        gfv = gf_ref[...]
        rstd = lax.rsqrt(jnp.mean(xf * xf, axis=-1, keepdims=True) + EPS)
        xhat = xf * rstd
        err = xhat * gfv - t_ref[...]
        dy = err * (1.0 / D)
        dxhat = dy * gfv
        dx_ref[...] = (rstd * (dxhat - xhat * jnp.mean(dxhat * xhat, axis=-1, keepdims=True))).astype(dx_ref.dtype)
        _acc(dgf_ref, _rowsum(dy * xhat), i == 0)
        _acc(lacc, _rowsum(err * err), i == 0)

        @pl.when(i == nt - 1)
        def _():
            loss_ref[...] = jnp.broadcast_to((0.5 / D) * jnp.sum(lacc[...]), loss_ref.shape)

    row = pl.BlockSpec((tm, D), lambda i: (i, 0))
    vec = pl.BlockSpec((1, D), lambda i: (0, 0))
    lspec = pl.BlockSpec((1, 128), lambda i: (0, 0))
    blocks = [((tm, D), F32)] * 3
    return _call(
        body, (x, gf, tgt), name=name, grid=(nt,),
        in_specs=[row, vec, row], out_specs=[row, lspec, vec],
        out_shape=[SDS((T, D), GRAD_STREAM), SDS((1, 128), F32), SDS((1, D), F32)],
        scratch_shapes=[pltpu.VMEM((1, D), F32)],
        params=_params(1, blocks, temp_bytes=4 * _nbytes((tm, D), F32)))


def _ffn_bwd_down(dxo, g, y, wd, a, b, *, seq, tm, tn, name, comm=None):
    T, F = a.shape
    D = wd.shape[1]
    tps = seq // tm
    nb = T // seq

    def body(dxo_ref, g_ref, y_ref, wd_ref, a_ref, b_ref, dyb_ref, da_ref, db_ref, dg_ref):
        i = pl.program_id(0)

        @pl.when(pl.program_id(1) == 0)
        def _():
            dx = dxo_ref[...].astype(F32)
            dyb_ref[...] = ((FFN_RESIDUAL * g_ref[0]) * dx).astype(BF16)
            part = _rowsum(FFN_RESIDUAL * dx * y_ref[...].astype(F32))
            _acc(dg_ref, part[None], i % tps == 0)

        dact = _dot_nt(dyb_ref[...], wd_ref[...])
        af = a_ref[...].astype(F32)
        bf = b_ref[...].astype(F32)
        sg = _sigmoid(af)
        silu = af * sg
        da_ref[...] = (dact * bf * (sg + silu * (1.0 - sg))).astype(BF16)
        db_ref[...] = (dact * silu).astype(BF16)

    row = pl.BlockSpec((tm, D), lambda i, j: (i, 0))
    per_b = pl.BlockSpec((1, 1, D), lambda i, j: (i // tps, 0, 0))
    wspec = pl.BlockSpec((tn, D), lambda i, j: (j, 0))
    chunk = pl.BlockSpec((tm, tn), lambda i, j: (i, j))
    blocks = [((tm, D), F32), ((tm, D), BF16), ((tn, D), BF16), ((tm, D), BF16)] + [((tm, tn), BF16)] * 4
    return _call(
        body, (dxo, g.arr, y, wd, a, b), name=name, grid=(T // tm, F // tn),
        in_specs=[row, g.spec(tps, 2), row, wspec, chunk, chunk],
        out_specs=[row, chunk, chunk, per_b],
        out_shape=[SDS((T, D), BF16)] + [SDS((T, F), BF16)] * 2 + [SDS((nb, 1, D), F32)],
        params=_params(2, blocks, temp_bytes=6 * _nbytes((tm, tn), F32)), comm=comm)


def _matmul_norm_mod_bwd(ds, ws, x, gn, sc, dxo, *, seq, tm, name, out_dtype, comm=None):
    T, D = x.shape
    nk = len(ws)
    sizes = [len(g) for g in ds]
    ds = [d for g in ds for d in g]
    tps = seq // tm
    nb = T // seq

    def body(*refs):
        w_refs = refs[len(ds):len(ds) + nk]
        x_ref, gn_ref, sc_ref, dxo_ref, dxi_ref, dsh_ref, dsc_ref, dgn_ref = refs[len(ds) + nk:]
        i = pl.program_id(0)
        dh, at = None, 0
        for n, w_ref in zip(sizes, w_refs):
            pieces = [r[...] for r in refs[at:at + n]]
            at += n
            part = _dot(pieces[0] if n == 1 else jnp.concatenate(pieces, axis=1), w_ref[...])
            dh = part if dh is None else dh + part
        gnv = gn_ref[...]
        scv = sc_ref[0]
        _, xhat, rstd, yn = _norm_mod(x_ref[...], gnv, 0.0, scv)
        dyn = dh * (1.0 + scv)
        dxhat = dyn * gnv
        dxi_ref[...] = (dxo_ref[...].astype(F32)
                        + rstd * (dxhat - xhat * jnp.mean(dxhat * xhat, axis=-1, keepdims=True))).astype(out_dtype)
        first_of_seq = i % tps == 0
        _acc(dsh_ref, _rowsum(dh)[None], first_of_seq)
        _acc(dsc_ref, _rowsum(dh * yn)[None], first_of_seq)
        _acc(dgn_ref, _rowsum(dyn * xhat), i == 0)

    row = pl.BlockSpec((tm, D), lambda i: (i, 0))
    vec = pl.BlockSpec((1, D), lambda i: (0, 0))
    per_b = pl.BlockSpec((1, 1, D), lambda i: (i // tps, 0, 0))
    d_specs = [pl.BlockSpec((tm, d.shape[1]), lambda i: (i, 0)) for d in ds]
    w_specs = [pl.BlockSpec(w.shape, lambda i: (0, 0)) for w in ws]
    blocks = ([((tm, d.shape[1]), BF16) for d in ds] + [(w.shape, BF16) for w in ws] + [((tm, D), F32)] * 3)
    return _call(
        body, (*ds, *ws, x, gn, sc.arr, dxo), name=name, grid=(T // tm,),
        in_specs=d_specs + w_specs + [row, vec, sc.spec(tps, 1), row],
        out_specs=[row, per_b, per_b, vec],
        out_shape=[SDS((T, D), out_dtype), SDS((nb, 1, D), F32), SDS((nb, 1, D), F32), SDS((1, D), F32)],
        params=_params(1, blocks, temp_bytes=6 * _nbytes((tm, D), F32)), comm=comm)


def _layernorm_silu(yc, lg, lb):
    mu = jnp.mean(yc, axis=-1, keepdims=True)
    cen = yc - mu
    rstd = lax.rsqrt(jnp.mean(cen * cen, axis=-1, keepdims=True) + EPS)
    xh = cen * rstd
    l = xh * lg + lb
    s = _sigmoid(l)
    return l * s, xh, rstd, l, s


GATE_W = 256


def _gate_specs(tm, D, col):
    return [pl.BlockSpec((tm, GATE_W), lambda i, blk=col // GATE_W + t: (i, blk)) for t in range(D // GATE_W)]


def _gate(refs):
    return jnp.concatenate([r[...] for r in refs], axis=1).astype(F32)


def _mix_out(ao, yc, proj, wao, wco, wout, x1, g2, lg, lb, *, seq, tm, ga_col, gc_col, name, comm=None):
    T, D = x1.shape
    tps = seq // tm
    ng = D // GATE_W

    def body(ao_ref, yc_ref, *rest):
        ga_refs, gc_refs = rest[:ng], rest[ng:2 * ng]
        (wao_ref, wco_ref, wout_ref, x1_ref, g2_ref, lg_ref, lb_ref,
         x2_ref, z_ref, ya_ref, ycv_ref, cact_ref, mrg_ref) = rest[2 * ng:]
        ya = _dot(ao_ref[...], wao_ref[...])
        cact = _layernorm_silu(yc_ref[...], lg_ref[...], lb_ref[...])[0].astype(BF16)
        ycv = _dot(cact, wco_ref[...])
        merged = (_sigmoid(_gate(ga_refs)) * ya + _sigmoid(_gate(gc_refs)) * ycv).astype(BF16)
        z = _dot(merged, wout_ref[...])
        x2_ref[...] = x1_ref[...] + g2_ref[0] * z
        z_ref[...] = z.astype(BF16)
        ya_ref[...] = ya.astype(BF16)
        ycv_ref[...] = ycv.astype(BF16)
        cact_ref[...] = cact
        mrg_ref[...] = merged

    row = pl.BlockSpec((tm, D), lambda i: (i, 0))
    vec = pl.BlockSpec((1, D), lambda i: (0, 0))
    wspec = pl.BlockSpec((D, D), lambda i: (0, 0))
    gates = _gate_specs(tm, D, ga_col) + _gate_specs(tm, D, gc_col)
    blocks = ([((tm, D), BF16), ((tm, D), F32), ((tm, D), BF16), ((tm, D), BF16)] + [((D, D), BF16)] * 3
              + [((tm, D), F32)] * 2 + [((tm, D), BF16)] * 5)
    return _call(
        body, (ao, yc, *[proj] * (2 * ng), wao, wco, wout, x1, g2.arr, lg, lb), name=name, grid=(T // tm,),
        in_specs=[row, row, *gates, wspec, wspec, wspec, row, g2.spec(tps, 1), vec, vec],
        out_specs=[row] * 6,
        out_shape=[SDS((T, D), F32)] + [SDS((T, D), BF16)] * 5,
        params=_params(1, blocks, temp_bytes=8 * _nbytes((tm, D), F32)), comm=comm)


def _mix_out_bwd(dx2, g2, z, wout, proj, ya, ycv, wao, wco, yc, lg, lb, *, seq, tm, ga_col, gc_col, name,
                 comm=None):
    T, D = dx2.shape
    tps = seq // tm
    nb = T // seq
    ng = D // GATE_W

    def body(dx2_ref, g2_ref, z_ref, wout_ref, *rest):
        ga_refs, gc_refs = rest[:ng], rest[ng:2 * ng]
        (ya_ref, ycv_ref, wao_ref, wco_ref, yc_ref, lg_ref, lb_ref, dz_ref, dya_ref, dycv_ref, dga_ref, dgc_ref,
         dao_ref, dyc_ref, dg2_ref, dlg_ref, dlb_ref) = rest[2 * ng:]
        i = pl.program_id(0)
        dx = dx2_ref[...].astype(F32)
        _acc(dg2_ref, _rowsum(dx * z_ref[...].astype(F32))[None], i % tps == 0)
        dzb = (g2_ref[0] * dx).astype(BF16)
        dz_ref[...] = dzb
        dmerged = _dot_nt(dzb, wout_ref[...])
        sa = _sigmoid(_gate(ga_refs))
        sc_ = _sigmoid(_gate(gc_refs))
        dya = (dmerged * sa).astype(BF16)
        dycv = (dmerged * sc_).astype(BF16)
        dya_ref[...] = dya
        dycv_ref[...] = dycv
        dga_ref[...] = (dmerged * ya_ref[...].astype(F32) * (sa * (1.0 - sa))).astype(BF16)
        dgc_ref[...] = (dmerged * ycv_ref[...].astype(F32) * (sc_ * (1.0 - sc_))).astype(BF16)
        dao_ref[...] = _dot_nt(dya, wao_ref[...]).astype(BF16)
        dcact = _dot_nt(dycv, wco_ref[...])
        lgv = lg_ref[...]
        _, xh, rstd, l, s = _layernorm_silu(yc_ref[...], lgv, lb_ref[...])
        dl = dcact * (s * (1.0 + l * (1.0 - s)))
        _acc(dlb_ref, _rowsum(dl), i == 0)
        _acc(dlg_ref, _rowsum(dl * xh), i == 0)
        dxh = dl * lgv
        dyc_ref[...] = rstd * (dxh - jnp.mean(dxh, axis=-1, keepdims=True)
                               - xh * jnp.mean(dxh * xh, axis=-1, keepdims=True))

    row = pl.BlockSpec((tm, D), lambda i: (i, 0))
    vec = pl.BlockSpec((1, D), lambda i: (0, 0))
    per_b = pl.BlockSpec((1, 1, D), lambda i: (i // tps, 0, 0))
    wspec = pl.BlockSpec((D, D), lambda i: (0, 0))
    gates = _gate_specs(tm, D, ga_col) + _gate_specs(tm, D, gc_col)
    blocks = ([((tm, D), F32)] * 3 + [((tm, D), BF16)] * 11 + [((D, D), BF16)] * 3)
    return _call(
        body, (dx2, g2.arr, z, wout, *[proj] * (2 * ng), ya, ycv, wao, wco, yc, lg, lb), name=name,
        grid=(T // tm,),
        in_specs=[row, g2.spec(tps, 1), row, wspec, *gates, row, row, wspec, wspec, row, vec, vec],
        out_specs=[row] * 7 + [per_b, vec, vec],
        out_shape=[SDS((T, D), BF16)] * 6 + [SDS((T, D), F32), SDS((nb, 1, D), F32), SDS((1, D), F32),
                                             SDS((1, D), F32)],
        params=_params(1, blocks, temp_bytes=10 * _nbytes((tm, D), F32)), comm=comm)


GROUP_ROWS = GQA_GROUP * ATT_BLOCK
PAIR_W = 2 * HEAD_DIM
GROUP_W = GQA_GROUP * HEAD_DIM


def _lane_lo():
    return lax.broadcasted_iota(jnp.int32, (1, PAIR_W), 1) < HEAD_DIM


def _band_bias():
    sj = lax.broadcasted_iota(jnp.int32, (2 * ATT_BLOCK, GROUP_ROWS), 0)
    qi = lax.broadcasted_iota(jnp.int32, (2 * ATT_BLOCK, GROUP_ROWS), 1) & (ATT_BLOCK - 1)
    rel = qi + ATT_BLOCK - sj
    bias = jnp.where(jnp.logical_and(rel >= 0, rel < ATT_BLOCK), 0.0, NEG_BIG)
    sj1 = lax.broadcasted_iota(jnp.int32, (2 * ATT_BLOCK, 1), 0)
    return bias, jnp.where(sj1 < ATT_BLOCK, NEG_BIG, 0.0)


def _dup_heads(src_ref, dst, seq):
    x = src_ref[...]
    i = lax.broadcasted_iota(jnp.int32, (KV_WIDTH, PAIR_W), 0)
    j = lax.broadcasted_iota(jnp.int32, (KV_WIDTH, PAIR_W), 1) & (HEAD_DIM - 1)
    for g in range(N_KV_HEADS):
        sel = jnp.where(i == j + g * HEAD_DIM, 1.0, 0.0).astype(BF16)
        dst[g, pl.ds(0, ATT_BLOCK), :] = jnp.zeros((ATT_BLOCK, PAIR_W), BF16)
        dst[g, pl.ds(ATT_BLOCK, seq), :] = _dot(x, sel).astype(BF16)


def _stack_heads(blk, g, lo):
    parts = []
    for p in range(GQA_GROUP // 2):
        pair = blk[:, g * GROUP_W + p * PAIR_W:g * GROUP_W + (p + 1) * PAIR_W]
        parts += [jnp.where(lo, pair, jnp.zeros_like(pair)), jnp.where(lo, jnp.zeros_like(pair), pair)]
    return jnp.concatenate(parts, axis=0)


def _unstack_heads(full, ref, r0, g, lo):
    for p in range(GQA_GROUP // 2):
        even = full[(2 * p) * ATT_BLOCK:(2 * p + 1) * ATT_BLOCK, :]
        odd = full[(2 * p + 1) * ATT_BLOCK:(2 * p + 2) * ATT_BLOCK, :]
        ref[pl.ds(r0, ATT_BLOCK), g * GROUP_W + p * PAIR_W:g * GROUP_W + (p + 1) * PAIR_W] = (
            jnp.where(lo, even, odd).astype(ref.dtype))


def _sink_row(sink_ref, g):
    return jnp.concatenate([jnp.full((1, ATT_BLOCK), sink_ref[0, g * GQA_GROUP + h], F32)
                            for h in range(GQA_GROUP)], axis=1)


def _group_probs(qs, k2, bias, sink):
    s = _dot_nt(k2, qs) * (HEAD_DIM ** -0.5) + bias
    m = jnp.maximum(jnp.max(s, axis=0, keepdims=True), sink)
    p = jnp.exp(s - m)
    psink = jnp.exp(sink - m)
    inv = 1.0 / (jnp.sum(p, axis=0, keepdims=True) + psink)
    return p * inv, psink * inv


def _attn_fwd(projp, sinks, *, seq, q_blk, k_blk, v_blk, name, comm=None):
    T = projp.shape[0]
    QW = N_Q_HEADS * HEAD_DIM
    nblk = seq // ATT_BLOCK

    def body(q_ref, k_ref, v_ref, sink_ref, o_ref, k2s, v2s):
        _dup_heads(k_ref, k2s, seq)
        _dup_heads(v_ref, v2s, seq)
        lo = _lane_lo()
        bias0, first_pen = _band_bias()
        sink_rows = [_sink_row(sink_ref, g) for g in range(N_KV_HEADS)]

        def blk(n, carry):
            r0 = pl.multiple_of(n * ATT_BLOCK, ATT_BLOCK)
            qb = q_ref[pl.ds(r0, ATT_BLOCK), :]
            bias = bias0 + jnp.where(n == 0, 1.0, 0.0) * first_pen
            for g in range(N_KV_HEADS):
                probs_t, _ = _group_probs(_stack_heads(qb, g, lo), k2s[g, pl.ds(r0, 2 * ATT_BLOCK), :], bias,
                                          sink_rows[g])
                _unstack_heads(_dot_tn(probs_t.astype(BF16), v2s[g, pl.ds(r0, 2 * ATT_BLOCK), :]), o_ref, r0, g, lo)
            return carry

        lax.fori_loop(0, nblk, blk, 0)

    blocks = [((seq, QW), BF16)] * 2 + [((seq, KV_WIDTH), BF16)] * 2
    return _call(
        body, (projp, projp, projp, sinks), name=name, grid=(T // seq,),
        in_specs=[pl.BlockSpec((seq, QW), lambda b: (b, q_blk)),
                  pl.BlockSpec((seq, KV_WIDTH), lambda b: (b, k_blk)),
                  pl.BlockSpec((seq, KV_WIDTH), lambda b: (b, v_blk)),
                  pl.BlockSpec(memory_space=pltpu.SMEM)],
        out_specs=[pl.BlockSpec((seq, QW), lambda b: (b, 0))],
        out_shape=[SDS((T, QW), BF16)],
        scratch_shapes=[pltpu.VMEM((N_KV_HEADS, seq + ATT_BLOCK, PAIR_W), BF16)] * 2,
        params=_params(1, blocks, temp_bytes=16 * 2**20), comm=comm)[0]


def _attn_bwd(projp, dao, sinks, *, seq, q_blk, k_blk, v_blk, name, comm=None):
    T = projp.shape[0]
    QW = N_Q_HEADS * HEAD_DIM
    nblk = seq // ATT_BLOCK

    def body(q_ref, k_ref, v_ref, do_ref, sink_ref, dq_ref, dk_ref, dv_ref, dsink_ref, k2s, v2s, dkacc, dvacc):
        _dup_heads(k_ref, k2s, seq)
        _dup_heads(v_ref, v2s, seq)
        dkacc[...] = jnp.zeros(dkacc.shape, F32)
        dvacc[...] = jnp.zeros(dvacc.shape, F32)
        lane = lax.broadcasted_iota(jnp.int32, (1, PAIR_W), 1)
        lo = lane < HEAD_DIM
        bias0, first_pen = _band_bias()
        sink_rows = [_sink_row(sink_ref, g) for g in range(N_KV_HEADS)]

        def blk(n, dsink):
            r0 = pl.multiple_of(n * ATT_BLOCK, ATT_BLOCK)
            band = pl.ds(r0, 2 * ATT_BLOCK)
            qb = q_ref[pl.ds(r0, ATT_BLOCK), :]
            dob = do_ref[pl.ds(r0, ATT_BLOCK), :]
            bias = bias0 + jnp.where(n == 0, 1.0, 0.0) * first_pen
            for g in range(N_KV_HEADS):
                qs = _stack_heads(qb, g, lo)
                dos = _stack_heads(dob, g, lo)
                k2 = k2s[g, band, :]
                v2 = v2s[g, band, :]
                probs_t, psink = _group_probs(qs, k2, bias, sink_rows[g])
                dp_t = _dot_nt(v2, dos)
                delta = jnp.sum(probs_t * dp_t, axis=0, keepdims=True)
                ds_t = (probs_t * (dp_t - delta) * (HEAD_DIM ** -0.5)).astype(BF16)
                tsink = psink * delta
                for h in range(GQA_GROUP):
                    dsink = dsink + jnp.where(lane == g * GQA_GROUP + h,
                                              -jnp.sum(tsink[:, h * ATT_BLOCK:(h + 1) * ATT_BLOCK]), 0.0)
                _unstack_heads(_dot_tn(ds_t, k2), dq_ref, r0, g, lo)
                dkacc[g, band, :] = dkacc[g, band, :] + _dot(ds_t, qs)
                dvacc[g, band, :] = dvacc[g, band, :] + _dot(probs_t.astype(BF16), dos)
            return dsink

        dsink = lax.fori_loop(0, nblk, blk, jnp.zeros((1, PAIR_W), F32))
        _acc(dsink_ref, dsink, pl.program_id(0) == 0)

        def fold(acc, g):
            a = acc[g, pl.ds(ATT_BLOCK, seq), :]
            return a + pltpu.roll(a, HEAD_DIM, 1)

        dk_ref[...] = jnp.where(lo, fold(dkacc, 0), fold(dkacc, 1)).astype(BF16)
        dv_ref[...] = jnp.where(lo, fold(dvacc, 0), fold(dvacc, 1)).astype(BF16)

    blocks = [((seq, QW), BF16)] * 3 + [((seq, KV_WIDTH), BF16)] * 4
    kv_spec_out = pl.BlockSpec((seq, KV_WIDTH), lambda b: (b, 0))
    return _call(
        body, (projp, projp, projp, dao, sinks), name=name, grid=(T // seq,),
        in_specs=[pl.BlockSpec((seq, QW), lambda b: (b, q_blk)),
                  pl.BlockSpec((seq, KV_WIDTH), lambda b: (b, k_blk)),
                  pl.BlockSpec((seq, KV_WIDTH), lambda b: (b, v_blk)),
                  pl.BlockSpec((seq, QW), lambda b: (b, 0)),
                  pl.BlockSpec(memory_space=pltpu.SMEM)],
        out_specs=[pl.BlockSpec((seq, QW), lambda b: (b, 0)), kv_spec_out, kv_spec_out,
                   pl.BlockSpec((1, 128), lambda b: (0, 0))],
        out_shape=[SDS((T, QW), BF16), SDS((T, KV_WIDTH), BF16), SDS((T, KV_WIDTH), BF16), SDS((1, 128), F32)],
        scratch_shapes=[pltpu.VMEM((N_KV_HEADS, seq + ATT_BLOCK, PAIR_W), BF16)] * 2
        + [pltpu.VMEM((N_KV_HEADS, seq + ATT_BLOCK, PAIR_W), F32)] * 2,
        params=_params(1, blocks, temp_bytes=24 * 2**20), comm=comm)


SUBLANES = 8


def _sublane_shifts(win):
    n = CONV_ROWS + CONV_HALO
    return [win] + [pltpu.roll(win, n - b, 0) for b in range(1, SUBLANES)]


def _window(shifted, off):
    a = off // SUBLANES * SUBLANES
    return shifted[off % SUBLANES][a:a + CONV_ROWS, :]


def _conv_fwd(projp, w, bias, *, seq, cw, a_col, b_col, name, comm=None):
    T = projp.shape[0]
    C = w.shape[1]
    nchunk = seq // CONV_ROWS

    def body(a_ref, b_ref, w_ref, bias_ref, y_ref, upad):
        upad[pl.ds(0, CONV_HALO), :] = jnp.zeros((CONV_HALO, cw), F32)
        upad[pl.ds(CONV_HALO, seq), :] = a_ref[...].astype(F32) * _sigmoid(b_ref[...].astype(F32))
        wv = w_ref[...]
        bv = bias_ref[...]

        def chunk(r, carry):
            r0 = pl.multiple_of(r * CONV_ROWS, CONV_ROWS)
            shifted = _sublane_shifts(upad[pl.ds(r0, CONV_ROWS + CONV_HALO), :])
            acc = jnp.broadcast_to(bv, (CONV_ROWS, cw))
            for k in range(CONV_WIDTH):
                acc = acc + wv[k:k + 1, :] * _window(shifted, CONV_HALO - (CONV_WIDTH - 1) + k)
            y_ref[pl.ds(r0, CONV_ROWS), :] = acc
            return carry

        lax.fori_loop(0, nchunk, chunk, 0)

    blocks = [((seq, cw), BF16)] * 2 + [((seq, cw), F32)]
    return _call(
        body, (projp, projp, w, bias), name=name, grid=(T // seq, C // cw),
        in_specs=[pl.BlockSpec((seq, cw), lambda b, c: (b, a_col // cw + c)),
                  pl.BlockSpec((seq, cw), lambda b, c: (b, b_col // cw + c)),
                  pl.BlockSpec((CONV_WIDTH, cw), lambda b, c: (0, c)),
                  pl.BlockSpec((1, cw), lambda b, c: (0, c))],
        out_specs=[pl.BlockSpec((seq, cw), lambda b, c: (b, c))],
        out_shape=[SDS((T, C), F32)],
        scratch_shapes=[pltpu.VMEM((seq + CONV_HALO, cw), F32)],
        params=_params(2, blocks, temp_bytes=6 * _nbytes((seq, cw), F32)), comm=comm)[0]


def _conv_bwd(dy, projp, w, *, seq, cw, a_col, b_col, name, comm=None):
    T = projp.shape[0]
    C = w.shape[1]
    nchunk = seq // CONV_ROWS
    SUB = 8

    def body(dy_ref, a_ref, b_ref, w_ref, da_ref, db_ref, dw_ref, dbias_ref, dypad, dwp):
        first = pl.program_id(1) == 0
        dyv = dy_ref[...]
        dypad[pl.ds(0, seq), :] = dyv
        dypad[pl.ds(seq, CONV_HALO), :] = jnp.zeros((CONV_HALO, cw), F32)
        dwp[...] = jnp.zeros(dwp.shape, F32)
        wv = w_ref[...]

        def chunk(r, carry):
            r0 = pl.multiple_of(r * CONV_ROWS, CONV_ROWS)
            dy_shifts = _sublane_shifts(dypad[pl.ds(r0, CONV_ROWS + CONV_HALO), :])
            ac = a_ref[pl.ds(r0, CONV_ROWS), :].astype(F32)
            sbc = _sigmoid(b_ref[pl.ds(r0, CONV_ROWS), :].astype(F32))
            uc = ac * sbc
            du = jnp.zeros((CONV_ROWS, cw), F32)
            for k in range(CONV_WIDTH):
                dyk = _window(dy_shifts, CONV_WIDTH - 1 - k)
                du = du + wv[k:k + 1, :] * dyk
                prod = uc * dyk
                part = prod[0:SUB, :]
                for s in range(1, CONV_ROWS // SUB):
                    part = part + prod[s * SUB:(s + 1) * SUB, :]
                dwp[pl.ds(k * SUB, SUB), :] = dwp[pl.ds(k * SUB, SUB), :] + part
            da_ref[pl.ds(r0, CONV_ROWS), :] = (du * sbc).astype(BF16)
            db_ref[pl.ds(r0, CONV_ROWS), :] = (du * ac * (sbc * (1.0 - sbc))).astype(BF16)
            return carry

        lax.fori_loop(0, nchunk, chunk, 0)

        @pl.when(first)
        def _():
            dw_ref[...] = jnp.zeros(dw_ref.shape, F32)
            dbias_ref[...] = jnp.zeros(dbias_ref.shape, F32)

        for k in range(CONV_WIDTH):
            dw_ref[k:k + 1, :] = dw_ref[k:k + 1, :] + _rowsum(dwp[pl.ds(k * SUB, SUB), :])
        dbias_ref[...] = dbias_ref[...] + _rowsum(dyv)

    blocks = [((seq, cw), F32)] + [((seq, cw), BF16)] * 4
    return _call(
        body, (dy, projp, projp, w), name=name, grid=(C // cw, T // seq),
        in_specs=[pl.BlockSpec((seq, cw), lambda c, b: (b, c)),
                  pl.BlockSpec((seq, cw), lambda c, b: (b, a_col // cw + c)),
                  pl.BlockSpec((seq, cw), lambda c, b: (b, b_col // cw + c)),
                  pl.BlockSpec((CONV_WIDTH, cw), lambda c, b: (0, c))],
        out_specs=[pl.BlockSpec((seq, cw), lambda c, b: (b, c)), pl.BlockSpec((seq, cw), lambda c, b: (b, c)),
                   pl.BlockSpec((CONV_WIDTH, cw), lambda c, b: (0, c)), pl.BlockSpec((1, cw), lambda c, b: (0, c))],
        out_shape=[SDS((T, C), BF16), SDS((T, C), BF16), SDS((CONV_WIDTH, C), F32), SDS((1, C), F32)],
        scratch_shapes=[pltpu.VMEM((seq + CONV_HALO, cw), F32), pltpu.VMEM((CONV_WIDTH * SUB, cw), F32)],
        params=_params(2, blocks, temp_bytes=8 * _nbytes((seq, cw), F32)), comm=comm)


def _matmul_tn(a, b, *, name, gate=None, comm=None):
    T, M = a.shape
    N = b.shape[1]
    bm = _pick(M, (768, 512, 256))
    lhs = [a] if gate is None else [a, gate]

    def body(*refs):
        b_ref, o_ref = refs[len(lhs)], refs[len(lhs) + 1]
        av = refs[0][...]
        if gate is not None:
            af = av.astype(F32)
            av = (af * _sigmoid(af) * refs[1][...].astype(F32)).astype(BF16)
        o_ref[...] = _dot_tn(av, b_ref[...]).astype(BF16)

    blocks = [((T, bm), BF16)] * len(lhs) + [((T, N), BF16), ((bm, N), BF16)]
    return _call(
        body, (*lhs, b), name=name, grid=(M // bm,),
        in_specs=[pl.BlockSpec((T, bm), lambda i: (0, i))] * len(lhs) + [pl.BlockSpec((T, N), lambda i: (0, 0))],
        out_specs=[pl.BlockSpec((bm, N), lambda i: (i, 0))],
        out_shape=[SDS((M, N), BF16)],
        params=_params(1, blocks, temp_bytes=(2 + 4 * len(lhs)) * _nbytes((T, bm), BF16) + 2 * _nbytes((bm, N), F32)),
        comm=comm)[0]


TN_BLOCK = 256


def _matmul_tn_pieces(groups, b, *, name, comm=None):
    T, N = b.shape
    flat = [a for g in groups for a in g]
    starts, n_steps = [], 0
    for g in groups:
        width = sum(a.shape[1] for a in g)
        assert width % TN_BLOCK == 0 and (len(g) == 1 or width == TN_BLOCK), [a.shape for a in g]
        starts.append(n_steps)
        n_steps += width // TN_BLOCK

    def body(*refs):
        a_refs, b_ref, o_ref = refs[:len(flat)], refs[len(flat)], refs[len(flat) + 1]
        i = pl.program_id(0)
        at = 0
        for g, start in zip(groups, starts):
            mine = a_refs[at:at + len(g)]
            at += len(g)
            steps = sum(a.shape[1] for a in g) // TN_BLOCK

            @pl.when(jnp.logical_and(i >= start, i < start + steps))
            def _(mine=mine):
                a = mine[0][...] if len(mine) == 1 else jnp.concatenate([r[...] for r in mine], axis=1)
                o_ref[...] = _dot_tn(a, b_ref[...]).astype(BF16)

    a_specs = []
    for g, start in zip(groups, starts):
        for a in g:
            if len(g) == 1:
                last = a.shape[1] // TN_BLOCK - 1
                a_specs.append(pl.BlockSpec(
                    (T, TN_BLOCK), lambda i, start=start, last=last: (0, jnp.clip(i - start, 0, last))))
            else:
                a_specs.append(pl.BlockSpec((T, a.shape[1]), lambda i: (0, 0)))
    blocks = [((T, TN_BLOCK), BF16)] * len(flat) + [((T, N), BF16), ((TN_BLOCK, N), BF16)]
    return _call(
        body, (*flat, b), name=name, grid=(n_steps,),
        in_specs=a_specs + [pl.BlockSpec((T, N), lambda i: (0, 0))],
        out_specs=[pl.BlockSpec((TN_BLOCK, N), lambda i: (i, 0))],
        out_shape=[SDS((n_steps * TN_BLOCK, N), BF16)],
        params=_params(1, blocks, temp_bytes=2 * _nbytes((T, TN_BLOCK), BF16) + 2 * _nbytes((TN_BLOCK, N), F32)),
        comm=comm)[0]


def _sum_parts(p_ref):
    g = p_ref[0].astype(F32)
    for s in range(1, p_ref.shape[0]):
        g = g + p_ref[s].astype(F32)
    return g


def _pair_add(g, staged, *, name):
    _, R, W = g.shape
    nq = staged.shape[0]
    tr = _row_tile(R)

    def body(g_ref, s_ref, o_ref):
        mine = jnp.where(lax.axis_index("c") == 0, g_ref[0, 0].astype(F32), g_ref[0, 1].astype(F32))
        o_ref[0] = (mine + s_ref[0].astype(F32)).astype(o_ref.dtype)

    return _call(
        body, (g.reshape(nq, 2, R, W), staged), name=name, grid=(nq, R // tr),
        in_specs=[pl.BlockSpec((1, 2, tr, W), lambda q, i: (q, 0, i, 0)),
                  pl.BlockSpec((1, tr, W), lambda q, i: (q, i, 0))],
        out_specs=[pl.BlockSpec((1, tr, W), lambda q, i: (q, i, 0))],
        out_shape=[SDS((nq, R, W), g.dtype)],
        params=_params(2, [((4, tr, W), g.dtype)], temp_bytes=3 * _nbytes((tr, W), F32)))[0]


def _adamw_update(w, g, m, v):
    m = ADAM_B1 * m + (1.0 - ADAM_B1) * g
    v = ADAM_B2 * v + (1.0 - ADAM_B2) * (g * g)
    m_hat = m / (1.0 - ADAM_B1 ** ADAM_STEP)
    v_hat = v / (1.0 - ADAM_B2 ** ADAM_STEP)
    delta = -ADAM_LR * (m_hat / (jnp.sqrt(v_hat) + ADAM_EPS) + ADAM_WD * w)
    return delta, m, v


def _row_tile(R):
    return _pick(R, (256, 128, 112, 88, 64, 32, 16, 8))


def _sum8(parts, *, name):
    n, R, W = parts.shape
    tr = _row_tile(R)

    def body(p_ref, o_ref):
        o_ref[...] = _sum_parts(p_ref)

    return _call(
        body, (parts,), name=name, grid=(R // tr,),
        in_specs=[pl.BlockSpec((n, tr, W), lambda i: (0, i, 0))],
        out_specs=[pl.BlockSpec((tr, W), lambda i: (i, 0))],
        out_shape=[SDS((R, W), F32)],
        params=_params(1, [((n, tr, W), parts.dtype), ((tr, W), F32)]))[0]


def _adamw(g, w, m, v, *, name):
    R, W = w.shape
    tr = _row_tile(R)

    def body(g_ref, w_ref, m_ref, v_ref, d_ref, mo_ref, vo_ref):
        d_ref[...], mo_ref[...], vo_ref[...] = _adamw_update(w_ref[...], g_ref[...], m_ref[...], v_ref[...])

    spec = pl.BlockSpec((tr, W), lambda i: (i, 0))
    return _call(
        body, (g, w, m, v), name=name, grid=(R // tr,),
        in_specs=[spec] * 4, out_specs=[spec] * 3, out_shape=[SDS((R, W), F32)] * 3,
        params=_params(1, [((tr, W), F32)] * 7))


def _sum8_adamw(items, *, name, comm=None):
    R, W = items[0][1].shape
    n = items[0][0].shape[0]
    tr = _row_tile(R)

    def body(*refs):
        ins, outs = refs[:4 * len(items)], refs[4 * len(items):]
        for k in range(len(items)):
            p_ref, w_ref, m_ref, v_ref = ins[4 * k:4 * k + 4]
            g_ref, d_ref, mo_ref, vo_ref = outs[4 * k:4 * k + 4]
            g = _sum_parts(p_ref)
            g_ref[...] = g
            d_ref[...], mo_ref[...], vo_ref[...] = _adamw_update(w_ref[...], g, m_ref[...], v_ref[...])

    spec = pl.BlockSpec((tr, W), lambda i: (i, 0))
    blocks = ([((n, tr, W), items[0][0].dtype)] + [((tr, W), F32)] * 7) * len(items)
    outs = _call(
        body, tuple(a for item in items for a in item), name=name, grid=(R // tr,),
        in_specs=([pl.BlockSpec((n, tr, W), lambda i: (0, i, 0))] + [spec] * 3) * len(items),
        out_specs=[spec] * (4 * len(items)), out_shape=[SDS((R, W), F32)] * (4 * len(items)),
        params=_params(1, blocks), comm=comm)
    return [tuple(outs[4 * k:4 * k + 4]) for k in range(len(items))]


def _ada_fwd(c_all, w, bias, *, name):
    NB, D = c_all.shape
    N = w.shape[1]

    def body(c_ref, w_ref, b_ref, o_ref):
        cv = c_ref[...]
        ca = (cv * _sigmoid(cv)).astype(BF16)
        o_ref[...] = _dot(ca, w_ref[...].astype(BF16)) + b_ref[...]

    full = lambda s: pl.BlockSpec(s, lambda i: (0,) * len(s))
    return _call(
        body, (c_all, w, bias), name=name, grid=(1,),
        in_specs=[full((NB, D)), full((D, N)), full((1, N))], out_specs=[full((NB, N))],
        out_shape=[SDS((NB, N), F32)],
        params=_params(1, [((D, N), F32)], temp_bytes=_nbytes((D, N), BF16)))[0]


def _ada_bwd(c_all, gmod_all, *, n_col, name):
    NB, D = c_all.shape
    N = gmod_all.shape[1]

    def body(c_ref, g_ref, gw_ref, gb_ref):
        cv = c_ref[...]
        ca = (cv * _sigmoid(cv)).astype(BF16)
        first = pl.multiple_of(_lin(_my_pos()) * n_col, 128)
        gw_ref[...] = _dot_tn(ca, g_ref[:, pl.ds(first, n_col)].astype(BF16))
        gb_ref[...] = _rowsum(g_ref[...])

    full = lambda s: pl.BlockSpec(s, lambda i: (0,) * len(s))
    return _call(
        body, (c_all, gmod_all), name=name, grid=(1,),
        in_specs=[full((NB, D)), full((NB, N))], out_specs=[full((D, n_col)), full((1, N))],
        out_shape=[SDS((D, n_col), F32), SDS((1, N), F32)],
        params=_params(1, [((D, n_col), F32), ((NB, N), F32)]))


def kernel(x, c, w_ada, b_ada, norm_ffn1_g, ffn1_w_gate, ffn1_w_up, ffn1_w_down, norm_mix_g, w_in, attn_sinks, w_attn_o, conv_w_dw, conv_b_dw, conv_ln_g, conv_ln_b, w_conv_o, w_out, norm_ffn2_g, ffn2_w_gate, ffn2_w_up, ffn2_w_down, final_norm_g, loss_target, m_w_ada, m_b_ada, m_norm_ffn1_g, m_ffn1_w_gate, m_ffn1_w_up, m_ffn1_w_down, m_norm_mix_g, m_w_in, m_attn_sinks, m_w_attn_o, m_conv_w_dw, m_conv_b_dw, m_conv_ln_g, m_conv_ln_b, m_w_conv_o, m_w_out, m_norm_ffn2_g, m_ffn2_w_gate, m_ffn2_w_up, m_ffn2_w_down, m_final_norm_g, v_w_ada, v_b_ada, v_norm_ffn1_g, v_ffn1_w_gate, v_ffn1_w_up, v_ffn1_w_down, v_norm_mix_g, v_w_in, v_attn_sinks, v_w_attn_o, v_conv_w_dw, v_conv_b_dw, v_conv_ln_g, v_conv_ln_b, v_w_conv_o, v_w_out, v_norm_ffn2_g, v_ffn2_w_gate, v_ffn2_w_up, v_ffn2_w_down, v_final_norm_g):
    B, S, D = x.shape
    T = B * S
    QW = N_Q_HEADS * HEAD_DIM
    CC = conv_w_dw.shape[2] * N_DEV
    me = _lin(_my_pos())
    xf = x.reshape(T, D)
    tgt = loss_target.reshape(T, D)
    tm = min(512, S)
    kw = dict(seq=S, tm=tm)

    p_k, p_v, p_ca = QW, QW + KV_WIDTH, QW + 2 * KV_WIDTH
    p_cb, p_ga, p_gc = p_ca + CC, p_ca + 2 * CC, p_ca + 2 * CC + D

    def col_t(w):
        return w[0].T.astype(BF16)

    def row_b(w):
        return w[0].astype(BF16)

    def rows(g):
        return g.reshape(-1, g.shape[-1])

    def blocks8(g):
        return g.reshape(N_DEV, g.shape[0] // N_DEV, g.shape[1])

    def gather(*arrs):
        return _Comm([(a, "gather") for a in arrs])

    g_wg1, g_convw, g_c = _exchange(
        [(col_t(ffn1_w_gate), "gather"), (conv_w_dw[0], "gather"), (c, "gather")], name="gather_first")
    wg1 = rows(g_wg1)
    conv_w = g_convw.transpose(1, 0, 2).reshape(CONV_WIDTH, CC)
    c_all = g_c.reshape(N_DEV * B, D)

    n_col = N_MOD * D // N_DEV
    b_cols = lax.dynamic_slice(b_ada, (0, me * n_col), (1, n_col))
    mod_cols = _ada_fwd(c_all, w_ada[0], b_cols, name="ada_fwd")
    mod_mine = _exchange([(mod_cols.reshape(N_DEV, B, n_col), "scatter")], name="scatter_mod")[0]
    mod = mod_mine.transpose(1, 0, 2).reshape(B * N_MOD, 1, D)
    sh1, sc1, g1, sh2, sc2, g2, sh3, sc3, g3 = [_ModVec(mod, i) for i in range(N_MOD)]

    F = wg1.shape[0]
    tn_f = _pick(F, (1408, 1024, 512, 256))
    tn_in = _pick(w_in.shape[2] * N_DEV, (1792, 768, 512, 256))
    gate_blk = dict(ga_col=p_ga, gc_col=p_gc)
    att_blk = dict(q_blk=0, k_blk=p_k // KV_WIDTH, v_blk=p_v // KV_WIDTH)
    conv_kw = dict(seq=S, cw=256, a_col=p_ca, b_col=p_cb)

    cm = gather(col_t(ffn1_w_up))
    h1, (a1,) = _norm_mod_matmul(xf, norm_ffn1_g, sh1, sc1, [wg1], tn=tn_f, name="ffn1_gate", comm=cm, **kw)
    wu1 = rows(cm.out[0])
    cm = gather(row_b(ffn1_w_down))
    b1 = _matmul_nt(h1, wu1, tm=tm, tn=tn_f, name="ffn1_up", comm=cm)
    wd1 = rows(cm.out[0])
    cm = gather(col_t(w_in))
    x1, y1 = _ffn_down(a1, b1, wd1, xf, g1, name="ffn1_down", comm=cm, **kw)
    winp = rows(cm.out[0])
    cm = gather(row_b(w_attn_o), row_b(w_conv_o), row_b(w_out), col_t(ffn2_w_gate))
    h2, (projp,) = _norm_mod_matmul(x1, norm_mix_g, sh2, sc2, [winp], tn=tn_in, name="mix_in", comm=cm, **kw)
    wao, wco, wout, wg2 = [rows(o) for o in cm.out]
    cm = gather(col_t(ffn2_w_up))
    ao = _attn_fwd(projp, attn_sinks, seq=S, name="attn_fwd", comm=cm, **att_blk)
    wu2 = rows(cm.out[0])
    cm = gather(row_b(ffn2_w_down))
    yc = _conv_fwd(projp, conv_w, conv_b_dw, name="conv_fwd", comm=cm, **conv_kw)
    wd2 = rows(cm.out[0])
    x2, z, ya, ycv, cact, merged = _mix_out(ao, yc, projp, wao, wco, wout, x1, g2, conv_ln_g, conv_ln_b,
                                            name="mix_out", **gate_blk, **kw)
    h3, (a3, b3) = _norm_mod_matmul(x2, norm_ffn2_g, sh3, sc3, [wg2, wu2], tn=tn_f, name="ffn2_up", **kw)
    x3, y3 = _ffn_down(a3, b3, wd2, x2, g3, name="ffn2_down", **kw)
    dx3, loss_row, dgf = _final_loss(x3, final_norm_g[None], tgt, tm=tm, name="final_loss")

    parts = {}

    def pair(*gs):
        return [(blocks8(g), "pair") for g in gs]

    def cross(*rs):
        return [(r, "cross") for r in rs]

    def reduce_pairs(gs, staged, names):
        return [_pair_add(blocks8(g), s, name="pair_add_" + n) for g, s, n in zip(gs, staged, names)]

    dyb3, da3, db3, dg3 = _ffn_bwd_down(dx3, g3, y3, wd2, a3, b3, tn=tn_f, name="ffn2_bwd_down", **kw)
    gwd2 = _matmul_tn(a3, dyb3, gate=b3, name="gw_ffn2_down")
    cm = _Comm(pair(gwd2))
    dx2, dsh3, dsc3, dgn3 = _matmul_norm_mod_bwd([[da3], [db3]], [wg2, wu2], x2, norm_ffn2_g, sc3, dx3,
                                                 name="ffn2_bwd_up", out_dtype=GRAD_STREAM, comm=cm, **kw)
    r_wd2, = reduce_pairs([gwd2], cm.out, ["ffn2_w_down"])
    cm = _Comm(cross(r_wd2))
    gwg2 = _matmul_tn(da3, h3, name="gw_ffn2_gate", comm=cm)
    parts["ffn2_w_down"], = cm.out
    cm = _Comm(pair(gwg2))
    gwu2 = _matmul_tn(db3, h3, name="gw_ffn2_up", comm=cm)
    r_wg2, = reduce_pairs([gwg2], cm.out, ["ffn2_w_gate"])

    cm = _Comm(cross(r_wg2) + pair(gwu2))
    dzb, dyab, dycb, dga, dgc, dao, dyc, dg2, dlng, dlnb = _mix_out_bwd(
        dx2, g2, z, wout, projp, ya, ycv, wao, wco, yc, conv_ln_g, conv_ln_b, name="mix_out_bwd", comm=cm,
        **gate_blk, **kw)
    parts["ffn2_w_gate"] = cm.out[0]
    r_wu2, = reduce_pairs([gwu2], cm.out[1:], ["ffn2_w_up"])
    gwout = _matmul_tn(merged, dzb, name="gw_out")
    gwao = _matmul_tn(ao, dyab, name="gw_attn_o")
    gwco = _matmul_tn(cact, dycb, name="gw_conv_o")
    cm = _Comm(cross(r_wu2) + pair(gwout, gwao, gwco))
    dq, dk, dv, dsinks = _attn_bwd(projp, dao, attn_sinks, seq=S, name="attn_bwd", comm=cm, **att_blk)
    parts["ffn2_w_up"] = cm.out[0]
    r_mix = reduce_pairs([gwout, gwao, gwco], cm.out[1:], ["w_out", "w_attn_o", "w_conv_o"])
    cm = _Comm(cross(*r_mix))
    dca, dcb, dconvw, dconvb = _conv_bwd(dyc, projp, conv_w, name="conv_bwd", comm=cm, **conv_kw)
    parts["w_out"], parts["w_attn_o"], parts["w_conv_o"] = cm.out
    gwin = _matmul_tn_pieces([[dq], [dk, dv], [dca], [dcb], [dga], [dgc]], h2, name="gw_in")
    cm = _Comm(pair(gwin))
    dx1, dsh2, dsc2, dgn2 = _matmul_norm_mod_bwd([[dq, dk, dv, dca, dcb, dga, dgc]], [winp], x1, norm_mix_g, sc2, dx2,
                                                 name="mix_in_bwd", out_dtype=GRAD_STREAM, comm=cm, **kw)
    r_win, = reduce_pairs([gwin], cm.out, ["w_in"])

    cm = _Comm(cross(r_win))
    dyb1, da1, db1, dg1 = _ffn_bwd_down(dx1, g1, y1, wd1, a1, b1, tn=tn_f, name="ffn1_bwd_down", comm=cm,
                                              **kw)
    parts["w_in"], = cm.out
    gwd1 = _matmul_tn(a1, dyb1, gate=b1, name="gw_ffn1_down")
    cm = _Comm(pair(gwd1))
    gwg1 = _matmul_tn(da1, h1, name="gw_ffn1_gate", comm=cm)
    r_wd1, = reduce_pairs([gwd1], cm.out, ["ffn1_w_down"])
    cm = _Comm(cross(r_wd1) + pair(gwg1))
    gwu1 = _matmul_tn(db1, h1, name="gw_ffn1_up", comm=cm)
    parts["ffn1_w_down"] = cm.out[0]
    r_wg1, = reduce_pairs([gwg1], cm.out[1:], ["ffn1_w_gate"])
    r_wu1, = reduce_pairs([gwu1], _exchange(pair(gwu1), name="pair_last"), ["ffn1_w_up"])
    cm = _Comm(cross(r_wg1, r_wu1))
    dx0, dsh1, dsc1, dgn1 = _matmul_norm_mod_bwd([[da1], [db1]], [wg1, wu1], xf, norm_ffn1_g, sc1, dx1,
                                                 name="ffn1_bwd_up", out_dtype=F32, comm=cm, **kw)
    parts["ffn1_w_gate"], parts["ffn1_w_up"] = cm.out

    n_small = 8
    gmod = jnp.concatenate([dsh1, dsc1, dg1, dsh2, dsc2, dg2, dsh3, dsc3, dg3], axis=1).reshape(B, N_MOD * D)
    sink_row = jnp.pad(dsinks[:, :N_Q_HEADS], ((0, 0), (0, D - N_Q_HEADS)))
    loss_pad = jnp.pad(loss_row, ((0, 0), (0, D - loss_row.shape[1])))
    small = jnp.concatenate([dgn1, dgn2, dgn3, dgf, dconvb, dlng, dlnb, sink_row, dconvw, loss_pad], axis=0)
    def updates(names, columns, name, comm=None):
        items = [(parts[n], *[(t[0].T if c else t[0]) for t in wmv[n]]) for n, c in zip(names, columns)]
        outs = _sum8_adamw(items, name=name, comm=comm)
        return {n: tuple(o.T if c else o for o in out) for n, c, out in zip(names, columns, outs)}

    wmv = {
        "ffn1_w_gate": (ffn1_w_gate, m_ffn1_w_gate, v_ffn1_w_gate), "ffn1_w_up": (ffn1_w_up, m_ffn1_w_up, v_ffn1_w_up),
        "ffn1_w_down": (ffn1_w_down, m_ffn1_w_down, v_ffn1_w_down), "w_in": (w_in, m_w_in, v_w_in),
        "w_attn_o": (w_attn_o, m_w_attn_o, v_w_attn_o), "w_conv_o": (w_conv_o, m_w_conv_o, v_w_conv_o),
        "w_out": (w_out, m_w_out, v_w_out), "ffn2_w_gate": (ffn2_w_gate, m_ffn2_w_gate, v_ffn2_w_gate),
        "ffn2_w_up": (ffn2_w_up, m_ffn2_w_up, v_ffn2_w_up), "ffn2_w_down": (ffn2_w_down, m_ffn2_w_down, v_ffn2_w_down),
    }
    cm = _Comm([(small, "gather"), (gmod, "gather")])
    upd = updates(["ffn2_w_gate", "ffn2_w_up", "ffn2_w_down"], [True, True, False], "adamw_ffn2", comm=cm)
    small_all, gmod_all = cm.out
    upd.update(updates(["ffn1_w_gate", "ffn1_w_up", "ffn1_w_down"], [True, True, False], "adamw_ffn1"))
    upd.update(updates(["w_attn_o", "w_conv_o", "w_out"], [False] * 3, "adamw_mix"))
    upd.update(updates(["w_in"], [True], "adamw_w_in"))

    gsmall = _sum8(small_all, name="sum_small")
    loss = gsmall[n_small + CONV_WIDTH, 0]
    g_w_ada, g_b_ada = _ada_bwd(c_all, gmod_all.reshape(N_DEV * B, N_MOD * D), n_col=n_col, name="ada_bwd")
    g_conv_w = lax.dynamic_slice(gsmall[n_small:n_small + CONV_WIDTH], (0, me * (CC // N_DEV)),
                                 (CONV_WIDTH, CC // N_DEV))
    upd["w_ada"] = (g_w_ada,) + tuple(_adamw(g_w_ada, w_ada[0], m_w_ada[0], v_w_ada[0], name="adamw_w_ada"))
    upd["conv_w_dw"] = (g_conv_w,) + tuple(_adamw(g_conv_w, conv_w_dw[0], m_conv_w_dw[0], v_conv_w_dw[0],
                                                  name="adamw_conv_w_dw"))
    for k in upd:
        upd[k] = tuple(t[None] for t in upd[k])

    def pad_sinks(t):
        return jnp.pad(t, ((0, 0), (0, D - N_Q_HEADS)))

    def pack(f1, mix, f2, fin, cb, lg, lb, sinks, bada):
        return jnp.concatenate([f1, mix, f2, fin[None], cb, lg, lb, pad_sinks(sinks), bada.reshape(N_MOD, D)], axis=0)

    w_s = pack(norm_ffn1_g, norm_mix_g, norm_ffn2_g, final_norm_g, conv_b_dw, conv_ln_g, conv_ln_b, attn_sinks, b_ada)
    m_s = pack(m_norm_ffn1_g, m_norm_mix_g, m_norm_ffn2_g, m_final_norm_g, m_conv_b_dw, m_conv_ln_g, m_conv_ln_b,
               m_attn_sinks, m_b_ada)
    v_s = pack(v_norm_ffn1_g, v_norm_mix_g, v_norm_ffn2_g, v_final_norm_g, v_conv_b_dw, v_conv_ln_g, v_conv_ln_b,
               v_attn_sinks, v_b_ada)
    g_s = jnp.concatenate([gsmall[:n_small], g_b_ada.reshape(N_MOD, D)], axis=0)
    small_out = (g_s,) + tuple(_adamw(g_s, w_s, m_s, v_s, name="adamw_vectors"))

    def unpack(t):
        return {
            "norm_ffn1_g": t[0:1], "norm_mix_g": t[1:2], "norm_ffn2_g": t[2:3], "final_norm_g": t[3],
            "conv_b_dw": t[4:5], "conv_ln_g": t[5:6], "conv_ln_b": t[6:7], "attn_sinks": t[7:8, :N_Q_HEADS],
            "b_ada": t[n_small:n_small + N_MOD].reshape(1, N_MOD * D),
        }

    small_un = [unpack(t) for t in small_out]
    for k in small_un[0]:
        upd[k] = tuple(s[k] for s in small_un)

    order = ["w_ada", "b_ada", "norm_ffn1_g", "ffn1_w_gate", "ffn1_w_up", "ffn1_w_down", "norm_mix_g", "w_in",
             "attn_sinks", "w_attn_o", "conv_w_dw", "conv_b_dw", "conv_ln_g", "conv_ln_b", "w_conv_o", "w_out",
             "norm_ffn2_g", "ffn2_w_gate", "ffn2_w_up", "ffn2_w_down", "final_norm_g"]
    grad_x = dx0.reshape(B, S, D)
    return (loss, grad_x, *[upd[k][0] for k in order], *[upd[k][1] for k in order],
            *[upd[k][2] for k in order], *[upd[k][3] for k in order])
```

```python
import dataclasses

import jax
import jax.numpy as jnp
from jax import lax
from jax.experimental import pallas as pl
from jax.experimental.pallas import tpu as pltpu

F32 = jnp.float32
BF16 = jnp.bfloat16
SDS = jax.ShapeDtypeStruct
MESH = pl.DeviceIdType.MESH

N_DEV = 8
EPS = 1e-6
HEAD_DIM = 64
N_Q_HEADS = 16
N_KV_HEADS = 2
GQA_GROUP = N_Q_HEADS // N_KV_HEADS
KV_WIDTH = N_KV_HEADS * HEAD_DIM
ATT_BLOCK = 128
CONV_WIDTH = 31
CONV_HALO = 32
CONV_ROWS = 128
N_MOD = 9
FFN_RESIDUAL = 0.5
ADAM_LR = 0.001
ADAM_B1 = 0.9
ADAM_B2 = 0.999
ADAM_EPS = 1e-08
ADAM_WD = 0.01
ADAM_STEP = 10
NEG_BIG = -1e30
GRAD_STREAM = BF16

V7X_VMEM_BYTES = 64 * 2**20
VMEM_CAP = V7X_VMEM_BYTES - 8 * 2**20


def _nbytes(shape, dtype):
    n = 1
    for s in shape:
        n *= s
    return n * jnp.dtype(dtype).itemsize


def _params(n_axes, blocks, temp_bytes=0):
    need = 2 * sum(_nbytes(s, d) for s, d in blocks) + temp_bytes + 4 * 2**20
    return pltpu.CompilerParams(dimension_semantics=("arbitrary",) * n_axes,
                                vmem_limit_bytes=int(min(max(need, 16 * 2**20), VMEM_CAP)))


def _dot_nt(a, b):
    return lax.dot_general(a, b, (((1,), (1,)), ((), ())), preferred_element_type=F32)


def _dot_tn(a, b):
    return lax.dot_general(a, b, (((0,), (0,)), ((), ())), preferred_element_type=F32)


def _dot(a, b):
    return jnp.dot(a, b, preferred_element_type=F32)


def _sigmoid(x):
    return jax.nn.sigmoid(x)


def _rowsum(v):
    return jnp.sum(v, axis=0, keepdims=True)


def _acc(ref, val, first):
    @pl.when(first)
    def _():
        ref[...] = val

    @pl.when(jnp.logical_not(first))
    def _():
        ref[...] = ref[...] + val


def _norm_mod(xf, gn, sh, sc):
    rstd = lax.rsqrt(jnp.mean(xf * xf, axis=-1, keepdims=True) + EPS)
    xhat = xf * rstd
    yn = xhat * gn
    return yn * (1.0 + sc) + sh, xhat, rstd, yn


def _pick(n, cands):
    for c in cands:
        if n % c == 0:
            return c
    return n


def _my_pos():
    return lax.axis_index("x"), lax.axis_index("y"), lax.axis_index("c")


def _peer(pos, k):
    x, y, c = pos
    return ((1 - x) if k & 4 else x, (1 - y) if k & 2 else y, (1 - c) if k & 1 else c)


def _lin(pos):
    return 4 * pos[0] + 2 * pos[1] + pos[2]


class _Comm:
    N_COPY = N_DEV - 1
    N_CHIP = N_DEV // 2

    def __init__(self, items):
        self.arrs = [a for a, _ in items]
        self.modes = [m for _, m in items]
        self.n = len(items)
        self.out = None

    def out_shape(self):
        def shape(a, m):
            return {"gather": (N_DEV,) + a.shape, "scatter": a.shape, "pair": (self.N_CHIP,) + a.shape[1:],
                    "cross": a.shape}[m]
        return [SDS(shape(a, m), a.dtype) for a, m in zip(self.arrs, self.modes)]

    def scratch(self):
        return [pltpu.SemaphoreType.DMA((self.n * self.N_COPY,)), pltpu.SemaphoreType.DMA((self.n * self.N_COPY,)),
                pltpu.SemaphoreType.DMA((self.n,))]

    def collective_id(self):
        modes = set(self.modes)
        if "scatter" in modes:
            return 3
        d2d, ici = bool(modes & {"gather", "pair"}), bool(modes & {"gather", "cross"})
        return {(True, False): 0, (False, True): 1, (True, True): 2}[(d2d, ici)]

    def barrier(self):
        x, y, c = _my_pos()
        peers = {0: [(x, y, 1 - c)],
                 1: [(1 - x, y, c), (x, 1 - y, c), (1 - x, 1 - y, c)],
                 2: [(x, y, 1 - c), (1 - x, y, c), (x, 1 - y, c), (1 - x, 1 - y, c)],
                 3: [_peer((x, y, c), k) for k in range(1, N_DEV)]}[self.collective_id()]
        sem = pltpu.get_barrier_semaphore()
        for p in peers:
            pl.semaphore_signal(sem, inc=1, device_id=p, device_id_type=MESH)
        pl.semaphore_wait(sem, len(peers))

    def _plan(self, mode, me):
        x, y, c = me
        sib = (x, y, 1 - c)
        chips = [(1 - x, y), (x, 1 - y), (1 - x, 1 - y)]

        def chip_lin(ch):
            return 2 * ch[0] + ch[1]

        if mode == "scatter":
            peers = [_peer(me, k + 1) for k in range(self.N_COPY)]
            return [(p, ("in", _lin(p)), _lin(me), _lin(p), None) for p in peers], (_lin(me), _lin(me))
        if mode == "gather":
            same = [(*ch, c) for ch in chips]
            other = [(*ch, 1 - c) for ch in chips]
            copies = [(sib, ("in", None), _lin(me), _lin(sib), None)]
            copies += [(p, ("in", None), _lin(me), _lin(p), None) for p in same]
            copies += [(sib, ("out", _lin(p)), _lin(p), _lin(o), 1 + j) for j, (p, o) in enumerate(zip(same, other))]
            return copies, (None, _lin(me))
        if mode == "pair":
            return [(sib, ("in", 2 * q + 1 - c), q, q, None) for q in range(self.N_CHIP)], None
        if mode == "cross":
            mine = chip_lin((x, y))
            return ([((*ch, c), ("in", chip_lin(ch)), mine, chip_lin(ch), None) for ch in chips], (mine, mine))
        raise ValueError(mode)

    def _copy(self, refs, me, i, k, recv):
        srcs, outs, (send_sems, recv_sems, _) = refs
        peer, (where, slot), send_slot, recv_slot, _ = self._plan(self.modes[i], me)[0][k]
        src = srcs[i] if where == "in" else outs[i]
        src = src if slot is None else src.at[slot]
        sem = i * self.N_COPY + k
        return pltpu.make_async_remote_copy(
            src_ref=src, dst_ref=outs[i].at[recv_slot if recv else send_slot], send_sem=send_sems.at[sem],
            recv_sem=recv_sems.at[sem], device_id=peer, device_id_type=MESH)

    def _local(self, refs, me, i):
        srcs, outs, (_, _, loc_sems) = refs
        local = self._plan(self.modes[i], me)[1]
        if local is None:
            return None
        own = srcs[i] if local[0] is None else srcs[i].at[local[0]]
        return pltpu.make_async_copy(own, outs[i].at[local[1]], loc_sems.at[i])

    def start(self, refs):
        me = _my_pos()
        for i in range(self.n):
            local = self._local(refs, me, i)
            if local is not None:
                local.start()
            for k, cp in enumerate(self._plan(self.modes[i], me)[0]):
                if cp[4] is None:
                    self._copy(refs, me, i, k, False).start()

    def forward(self, refs):
        me = _my_pos()
        for i in range(self.n):
            for k, cp in enumerate(self._plan(self.modes[i], me)[0]):
                if cp[4] is not None:
                    self._copy(refs, me, i, cp[4], True).wait_recv()
                    self._copy(refs, me, i, k, False).start()

    def finish(self, refs):
        me = _my_pos()
        plans = [self._plan(m, me)[0] for m in self.modes]
        for i in range(self.n):
            passed_on = [cp[4] for cp in plans[i] if cp[4] is not None]
            for k in range(len(plans[i])):
                if k not in passed_on:
                    self._copy(refs, me, i, k, True).wait_recv()
                self._copy(refs, me, i, k, False).wait_send()
            local = self._local(refs, me, i)
            if local is not None:
                local.wait()


_ANY = pl.BlockSpec(memory_space=pl.ANY)


def _call(body, args, *, name, grid, in_specs, out_specs, out_shape, params, scratch_shapes=(), comm=None):
    in_specs, out_specs, out_shape = list(in_specs), list(out_specs), list(out_shape)
    scratch_shapes = list(scratch_shapes)
    if comm is None:
        return list(pl.pallas_call(body, name=name, grid=grid, in_specs=in_specs, out_specs=out_specs,
                                   out_shape=out_shape, scratch_shapes=scratch_shapes, compiler_params=params)(*args))
    n_in, n_out, n_scr, nc = len(in_specs), len(out_specs), len(scratch_shapes), comm.n
    n_steps = 1
    for g in grid:
        n_steps *= g

    def hosted(*refs):
        ins, c_in = refs[:n_in], refs[n_in:n_in + nc]
        outs = refs[n_in + nc:n_in + nc + n_out]
        c_out = refs[n_in + nc + n_out:n_in + 2 * nc + n_out]
        scr = refs[n_in + 2 * nc + n_out:n_in + 2 * nc + n_out + n_scr]
        sems = refs[n_in + 2 * nc + n_out + n_scr:]
        step = pl.program_id(0)
        for d in range(1, len(grid)):
            step = step * grid[d] + pl.program_id(d)
        c_refs = (c_in, c_out, sems)

        @pl.when(step == 0)
        def _():
            comm.barrier()
            comm.start(c_refs)

        if n_steps >= 3:
            @pl.when(step == n_steps - 2)
            def _():
                comm.forward(c_refs)

        body(*ins, *outs, *scr)

        @pl.when(step == n_steps - 1)
        def _():
            if n_steps < 3:
                comm.forward(c_refs)
            comm.finish(c_refs)

    res = pl.pallas_call(
        hosted, name=name, grid=grid, in_specs=in_specs + [_ANY] * nc, out_specs=out_specs + [_ANY] * nc,
        out_shape=out_shape + comm.out_shape(), scratch_shapes=scratch_shapes + comm.scratch(),
        compiler_params=dataclasses.replace(params, collective_id=comm.collective_id()))(*args, *comm.arrs)
    comm.out = list(res[n_out:])
    return list(res[:n_out])


def _exchange(items, *, name):
    comm = _Comm(items)

    def body(*refs):
        r = (refs[:comm.n], refs[comm.n:2 * comm.n], refs[2 * comm.n:])
        comm.barrier()
        comm.start(r)
        comm.forward(r)
        comm.finish(r)

    return list(pl.pallas_call(body, name=name, out_shape=comm.out_shape(), in_specs=[_ANY] * comm.n,
                               out_specs=[_ANY] * comm.n, scratch_shapes=comm.scratch(),
                               compiler_params=pltpu.CompilerParams(collective_id=comm.collective_id()))(*comm.arrs))


class _ModVec:
    def __init__(self, arr, idx):
        self.arr, self.idx = arr, idx

    def spec(self, tps, n_axes):
        idx, blk = self.idx, (1, 1, self.arr.shape[2])
        if n_axes == 1:
            return pl.BlockSpec(blk, lambda i: (i // tps * N_MOD + idx, 0, 0))
        return pl.BlockSpec(blk, lambda i, j: (i // tps * N_MOD + idx, 0, 0))


def _norm_mod_matmul(x, gn, sh, sc, wts, *, seq, tm, tn, name, comm=None):
    T, D = x.shape
    N = wts[0].shape[0]
    nw = len(wts)
    tps = seq // tm

    def body(x_ref, gn_ref, sh_ref, sc_ref, *rest):
        w_refs, h_ref, o_refs = rest[:nw], rest[nw], rest[nw + 1:]

        @pl.when(pl.program_id(1) == 0)
        def _():
            h_ref[...] = _norm_mod(x_ref[...], gn_ref[...], sh_ref[0], sc_ref[0])[0].astype(BF16)

        h = h_ref[...]
        for w_ref, o_ref in zip(w_refs, o_refs):
            o_ref[...] = _dot_nt(h, w_ref[...]).astype(o_ref.dtype)

    row = pl.BlockSpec((tm, D), lambda i, j: (i, 0))
    vec = pl.BlockSpec((1, D), lambda i, j: (0, 0))
    wspec = pl.BlockSpec((tn, D), lambda i, j: (j, 0))
    ospec = pl.BlockSpec((tm, tn), lambda i, j: (i, j))
    blocks = [((tm, D), F32), ((tm, D), BF16)] + [((tn, D), BF16), ((tm, tn), BF16)] * nw
    outs = _call(
        body, (x, gn, sh.arr, sc.arr, *wts), name=name, grid=(T // tm, N // tn),
        in_specs=[row, vec, sh.spec(tps, 2), sc.spec(tps, 2)] + [wspec] * nw,
        out_specs=[row] + [ospec] * nw,
        out_shape=[SDS((T, D), BF16)] + [SDS((T, N), BF16)] * nw,
        params=_params(2, blocks, temp_bytes=2 * _nbytes((tm, tn), F32) + 3 * _nbytes((tm, D), F32)), comm=comm)
    return outs[0], outs[1:]


def _matmul_nt(h, w, *, tm, tn, name, comm=None):
    T, D = h.shape
    N = w.shape[0]

    def body(h_ref, w_ref, o_ref):
        o_ref[...] = _dot_nt(h_ref[...], w_ref[...]).astype(o_ref.dtype)

    blocks = [((tm, D), BF16), ((tn, D), BF16), ((tm, tn), BF16)]
    return _call(
        body, (h, w), name=name, grid=(T // tm, N // tn),
        in_specs=[pl.BlockSpec((tm, D), lambda i, j: (i, 0)), pl.BlockSpec((tn, D), lambda i, j: (j, 0))],
        out_specs=[pl.BlockSpec((tm, tn), lambda i, j: (i, j))],
        out_shape=[SDS((T, N), BF16)],
        params=_params(2, blocks, temp_bytes=2 * _nbytes((tm, tn), F32)), comm=comm)[0]


def _ffn_down(a, b, wd, x, g, *, seq, tm, name, comm=None):
    T, F = a.shape
    D = wd.shape[1]
    tps = seq // tm

    def body(a_ref, b_ref, wd_ref, x_ref, g_ref, xo_ref, y_ref):
        af = a_ref[...].astype(F32)
        act = (af * _sigmoid(af) * b_ref[...].astype(F32)).astype(BF16)
        y = _dot(act, wd_ref[...])
        xo_ref[...] = x_ref[...] + (FFN_RESIDUAL * g_ref[0]) * y
        y_ref[...] = y.astype(BF16)

    wide = pl.BlockSpec((tm, F), lambda i: (i, 0))
    row = pl.BlockSpec((tm, D), lambda i: (i, 0))
    wspec = pl.BlockSpec((F, D), lambda i: (0, 0))
    blocks = [((tm, F), BF16)] * 2 + [((F, D), BF16), ((tm, D), F32), ((tm, D), F32), ((tm, D), BF16)]
    return _call(
        body, (a, b, wd, x, g.arr), name=name, grid=(T // tm,),
        in_specs=[wide, wide, wspec, row, g.spec(tps, 1)], out_specs=[row, row],
        out_shape=[SDS((T, D), F32), SDS((T, D), BF16)],
        params=_params(1, blocks, temp_bytes=3 * _nbytes((tm, F), F32)), comm=comm)


def _final_loss(x, gf, tgt, *, tm, name):
    T, D = x.shape
    nt = T // tm

    def body(x_ref, gf_ref, t_ref, dx_ref, loss_ref, dgf_ref, lacc):
        i = pl.program_id(0)
        xf = x_ref[...]
        gfv = gf_ref[...]
        rstd = lax.rsqrt(jnp.mean(xf * xf, axis=-1, keepdims=True) + EPS)
        xhat = xf * rstd
        err = xhat * gfv - t_ref[...]
        dy = err * (1.0 / D)
        dxhat = dy * gfv
        dx_ref[...] = (rstd * (dxhat - xhat * jnp.mean(dxhat * xhat, axis=-1, keepdims=True))).astype(dx_ref.dtype)
        _acc(dgf_ref, _rowsum(dy * xhat), i == 0)
        _acc(lacc, _rowsum(err * err), i == 0)

        @pl.when(i == nt - 1)
        def _():
            loss_ref[...] = jnp.broadcast_to((0.5 / D) * jnp.sum(lacc[...]), loss_ref.shape)

    row = pl.BlockSpec((tm, D), lambda i: (i, 0))
    vec = pl.BlockSpec((1, D), lambda i: (0, 0))
    lspec = pl.BlockSpec((1, 128), lambda i: (0, 0))
    blocks = [((tm, D), F32)] * 3
    return _call(
        body, (x, gf, tgt), name=name, grid=(nt,),
        in_specs=[row, vec, row], out_specs=[row, lspec, vec],
        out_shape=[SDS((T, D), GRAD_STREAM), SDS((1, 128), F32), SDS((1, D), F32)],
        scratch_shapes=[pltpu.VMEM((1, D), F32)],
        params=_params(1, blocks, temp_bytes=4 * _nbytes((tm, D), F32)))


def _ffn_bwd_down(dxo, g, y, wd, a, b, *, seq, tm, tn, name, comm=None):
    T, F = a.shape
    D = wd.shape[1]
    tps = seq // tm
    nb = T // seq

    def body(dxo_ref, g_ref, y_ref, wd_ref, a_ref, b_ref, dyb_ref, da_ref, db_ref, dg_ref):
        i = pl.program_id(0)

        @pl.when(pl.program_id(1) == 0)
        def _():
            dx = dxo_ref[...].astype(F32)
            dyb_ref[...] = ((FFN_RESIDUAL * g_ref[0]) * dx).astype(BF16)
            part = _rowsum(FFN_RESIDUAL * dx * y_ref[...].astype(F32))
            _acc(dg_ref, part[None], i % tps == 0)

        dact = _dot_nt(dyb_ref[...], wd_ref[...])
        af = a_ref[...].astype(F32)
        bf = b_ref[...].astype(F32)
        sg = _sigmoid(af)
        silu = af * sg
        da_ref[...] = (dact * bf * (sg + silu * (1.0 - sg))).astype(BF16)
        db_ref[...] = (dact * silu).astype(BF16)

    row = pl.BlockSpec((tm, D), lambda i, j: (i, 0))
    per_b = pl.BlockSpec((1, 1, D), lambda i, j: (i // tps, 0, 0))
    wspec = pl.BlockSpec((tn, D), lambda i, j: (j, 0))
    chunk = pl.BlockSpec((tm, tn), lambda i, j: (i, j))
    blocks = [((tm, D), F32), ((tm, D), BF16), ((tn, D), BF16), ((tm, D), BF16)] + [((tm, tn), BF16)] * 4
    return _call(
        body, (dxo, g.arr, y, wd, a, b), name=name, grid=(T // tm, F // tn),
        in_specs=[row, g.spec(tps, 2), row, wspec, chunk, chunk],
        out_specs=[row, chunk, chunk, per_b],
        out_shape=[SDS((T, D), BF16)] + [SDS((T, F), BF16)] * 2 + [SDS((nb, 1, D), F32)],
        params=_params(2, blocks, temp_bytes=6 * _nbytes((tm, tn), F32)), comm=comm)


def _matmul_norm_mod_bwd(ds, ws, x, gn, sc, dxo, *, seq, tm, name, out_dtype, comm=None):
    T, D = x.shape
    nk = len(ws)
    sizes = [len(g) for g in ds]
    ds = [d for g in ds for d in g]
    tps = seq // tm
    nb = T // seq

    def body(*refs):
        w_refs = refs[len(ds):len(ds) + nk]
        x_ref, gn_ref, sc_ref, dxo_ref, dxi_ref, dsh_ref, dsc_ref, dgn_ref = refs[len(ds) + nk:]
        i = pl.program_id(0)
        dh, at = None, 0
        for n, w_ref in zip(sizes, w_refs):
            pieces = [r[...] for r in refs[at:at + n]]
            at += n
            part = _dot(pieces[0] if n == 1 else jnp.concatenate(pieces, axis=1), w_ref[...])
            dh = part if dh is None else dh + part
        gnv = gn_ref[...]
        scv = sc_ref[0]
        _, xhat, rstd, yn = _norm_mod(x_ref[...], gnv, 0.0, scv)
        dyn = dh * (1.0 + scv)
        dxhat = dyn * gnv
        dxi_ref[...] = (dxo_ref[...].astype(F32)
                        + rstd * (dxhat - xhat * jnp.mean(dxhat * xhat, axis=-1, keepdims=True))).astype(out_dtype)
        first_of_seq = i % tps == 0
        _acc(dsh_ref, _rowsum(dh)[None], first_of_seq)
        _acc(dsc_ref, _rowsum(dh * yn)[None], first_of_seq)
        _acc(dgn_ref, _rowsum(dyn * xhat), i == 0)

    row = pl.BlockSpec((tm, D), lambda i: (i, 0))
    vec = pl.BlockSpec((1, D), lambda i: (0, 0))
    per_b = pl.BlockSpec((1, 1, D), lambda i: (i // tps, 0, 0))
    d_specs = [pl.BlockSpec((tm, d.shape[1]), lambda i: (i, 0)) for d in ds]
    w_specs = [pl.BlockSpec(w.shape, lambda i: (0, 0)) for w in ws]
    blocks = ([((tm, d.shape[1]), BF16) for d in ds] + [(w.shape, BF16) for w in ws] + [((tm, D), F32)] * 3)
    return _call(
        body, (*ds, *ws, x, gn, sc.arr, dxo), name=name, grid=(T // tm,),
        in_specs=d_specs + w_specs + [row, vec, sc.spec(tps, 1), row],
        out_specs=[row, per_b, per_b, vec],
        out_shape=[SDS((T, D), out_dtype), SDS((nb, 1, D), F32), SDS((nb, 1, D), F32), SDS((1, D), F32)],
        params=_params(1, blocks, temp_bytes=6 * _nbytes((tm, D), F32)), comm=comm)


def _layernorm_silu(yc, lg, lb):
    mu = jnp.mean(yc, axis=-1, keepdims=True)
    cen = yc - mu
    rstd = lax.rsqrt(jnp.mean(cen * cen, axis=-1, keepdims=True) + EPS)
    xh = cen * rstd
    l = xh * lg + lb
    s = _sigmoid(l)
    return l * s, xh, rstd, l, s


GATE_W = 256


def _gate_specs(tm, D, col):
    return [pl.BlockSpec((tm, GATE_W), lambda i, blk=col // GATE_W + t: (i, blk)) for t in range(D // GATE_W)]


def _gate(refs):
    return jnp.concatenate([r[...] for r in refs], axis=1).astype(F32)


def _mix_out(ao, yc, proj, wao, wco, wout, x1, g2, lg, lb, *, seq, tm, ga_col, gc_col, name, comm=None):
    T, D = x1.shape
    tps = seq // tm
    ng = D // GATE_W

    def body(ao_ref, yc_ref, *rest):
        ga_refs, gc_refs = rest[:ng], rest[ng:2 * ng]
        (wao_ref, wco_ref, wout_ref, x1_ref, g2_ref, lg_ref, lb_ref,
         x2_ref, z_ref, ya_ref, ycv_ref, cact_ref, mrg_ref) = rest[2 * ng:]
        ya = _dot(ao_ref[...], wao_ref[...])
        cact = _layernorm_silu(yc_ref[...], lg_ref[...], lb_ref[...])[0].astype(BF16)
        ycv = _dot(cact, wco_ref[...])
        merged = (_sigmoid(_gate(ga_refs)) * ya + _sigmoid(_gate(gc_refs)) * ycv).astype(BF16)
        z = _dot(merged, wout_ref[...])
        x2_ref[...] = x1_ref[...] + g2_ref[0] * z
        z_ref[...] = z.astype(BF16)
        ya_ref[...] = ya.astype(BF16)
        ycv_ref[...] = ycv.astype(BF16)
        cact_ref[...] = cact
        mrg_ref[...] = merged

    row = pl.BlockSpec((tm, D), lambda i: (i, 0))
    vec = pl.BlockSpec((1, D), lambda i: (0, 0))
    wspec = pl.BlockSpec((D, D), lambda i: (0, 0))
    gates = _gate_specs(tm, D, ga_col) + _gate_specs(tm, D, gc_col)
    blocks = ([((tm, D), BF16), ((tm, D), F32), ((tm, D), BF16), ((tm, D), BF16)] + [((D, D), BF16)] * 3
              + [((tm, D), F32)] * 2 + [((tm, D), BF16)] * 5)
    return _call(
        body, (ao, yc, *[proj] * (2 * ng), wao, wco, wout, x1, g2.arr, lg, lb), name=name, grid=(T // tm,),
        in_specs=[row, row, *gates, wspec, wspec, wspec, row, g2.spec(tps, 1), vec, vec],
        out_specs=[row] * 6,
        out_shape=[SDS((T, D), F32)] + [SDS((T, D), BF16)] * 5,
        params=_params(1, blocks, temp_bytes=8 * _nbytes((tm, D), F32)), comm=comm)


def _mix_out_bwd(dx2, g2, z, wout, proj, ya, ycv, wao, wco, yc, lg, lb, *, seq, tm, ga_col, gc_col, name,
                 comm=None):
    T, D = dx2.shape
    tps = seq // tm
    nb = T // seq
    ng = D // GATE_W

    def body(dx2_ref, g2_ref, z_ref, wout_ref, *rest):
        ga_refs, gc_refs = rest[:ng], rest[ng:2 * ng]
        (ya_ref, ycv_ref, wao_ref, wco_ref, yc_ref, lg_ref, lb_ref, dz_ref, dya_ref, dycv_ref, dga_ref, dgc_ref,
         dao_ref, dyc_ref, dg2_ref, dlg_ref, dlb_ref) = rest[2 * ng:]
        i = pl.program_id(0)
        dx = dx2_ref[...].astype(F32)
        _acc(dg2_ref, _rowsum(dx * z_ref[...].astype(F32))[None], i % tps == 0)
        dzb = (g2_ref[0] * dx).astype(BF16)
        dz_ref[...] = dzb
        dmerged = _dot_nt(dzb, wout_ref[...])
        sa = _sigmoid(_gate(ga_refs))
        sc_ = _sigmoid(_gate(gc_refs))
        dya = (dmerged * sa).astype(BF16)
        dycv = (dmerged * sc_).astype(BF16)
        dya_ref[...] = dya
        dycv_ref[...] = dycv
        dga_ref[...] = (dmerged * ya_ref[...].astype(F32) * (sa * (1.0 - sa))).astype(BF16)
        dgc_ref[...] = (dmerged * ycv_ref[...].astype(F32) * (sc_ * (1.0 - sc_))).astype(BF16)
        dao_ref[...] = _dot_nt(dya, wao_ref[...]).astype(BF16)
        dcact = _dot_nt(dycv, wco_ref[...])
        lgv = lg_ref[...]
        _, xh, rstd, l, s = _layernorm_silu(yc_ref[...], lgv, lb_ref[...])
        dl = dcact * (s * (1.0 + l * (1.0 - s)))
        _acc(dlb_ref, _rowsum(dl), i == 0)
        _acc(dlg_ref, _rowsum(dl * xh), i == 0)
        dxh = dl * lgv
        dyc_ref[...] = rstd * (dxh - jnp.mean(dxh, axis=-1, keepdims=True)
                               - xh * jnp.mean(dxh * xh, axis=-1, keepdims=True))

    row = pl.BlockSpec((tm, D), lambda i: (i, 0))
    vec = pl.BlockSpec((1, D), lambda i: (0, 0))
    per_b = pl.BlockSpec((1, 1, D), lambda i: (i // tps, 0, 0))
    wspec = pl.BlockSpec((D, D), lambda i: (0, 0))
    gates = _gate_specs(tm, D, ga_col) + _gate_specs(tm, D, gc_col)
    blocks = ([((tm, D), F32)] * 3 + [((tm, D), BF16)] * 11 + [((D, D), BF16)] * 3)
    return _call(
        body, (dx2, g2.arr, z, wout, *[proj] * (2 * ng), ya, ycv, wao, wco, yc, lg, lb), name=name,
        grid=(T // tm,),
        in_specs=[row, g2.spec(tps, 1), row, wspec, *gates, row, row, wspec, wspec, row, vec, vec],
        out_specs=[row] * 7 + [per_b, vec, vec],
        out_shape=[SDS((T, D), BF16)] * 6 + [SDS((T, D), F32), SDS((nb, 1, D), F32), SDS((1, D), F32),
                                             SDS((1, D), F32)],
        params=_params(1, blocks, temp_bytes=10 * _nbytes((tm, D), F32)), comm=comm)


Q_BLOCK = 64
BAND = Q_BLOCK + ATT_BLOCK
GROUP_ROWS = GQA_GROUP * Q_BLOCK
PAIR_W = 2 * HEAD_DIM
GROUP_W = GQA_GROUP * HEAD_DIM


def _lane_lo():
    return lax.broadcasted_iota(jnp.int32, (1, PAIR_W), 1) < HEAD_DIM


def _band_bias():
    sj = lax.broadcasted_iota(jnp.int32, (BAND, GROUP_ROWS), 0)
    qi = lax.broadcasted_iota(jnp.int32, (BAND, GROUP_ROWS), 1) & (Q_BLOCK - 1)
    rel = qi + ATT_BLOCK - sj
    bias = jnp.where(jnp.logical_and(rel >= 0, rel < ATT_BLOCK), 0.0, NEG_BIG)
    return bias, lax.broadcasted_iota(jnp.int32, (BAND, 1), 0)


def _block_bias(bias0, key_index, r0):
    return bias0 + jnp.where(key_index + r0 < ATT_BLOCK, NEG_BIG, 0.0)


def _dup_heads(src_ref, dst, seq):
    x = src_ref[...]
    i = lax.broadcasted_iota(jnp.int32, (KV_WIDTH, PAIR_W), 0)
    j = lax.broadcasted_iota(jnp.int32, (KV_WIDTH, PAIR_W), 1) & (HEAD_DIM - 1)
    for g in range(N_KV_HEADS):
        sel = jnp.where(i == j + g * HEAD_DIM, 1.0, 0.0).astype(BF16)
        dst[g, pl.ds(0, ATT_BLOCK), :] = jnp.zeros((ATT_BLOCK, PAIR_W), BF16)
        dst[g, pl.ds(ATT_BLOCK, seq), :] = _dot(x, sel).astype(BF16)


def _stack_heads(blk, g, lo):
    parts = []
    for p in range(GQA_GROUP // 2):
        pair = blk[:, g * GROUP_W + p * PAIR_W:g * GROUP_W + (p + 1) * PAIR_W]
        parts += [jnp.where(lo, pair, jnp.zeros_like(pair)), jnp.where(lo, jnp.zeros_like(pair), pair)]
    return jnp.concatenate(parts, axis=0)


def _unstack_heads(full, ref, r0, g, lo):
    for p in range(GQA_GROUP // 2):
        even = full[(2 * p) * Q_BLOCK:(2 * p + 1) * Q_BLOCK, :]
        odd = full[(2 * p + 1) * Q_BLOCK:(2 * p + 2) * Q_BLOCK, :]
        ref[pl.ds(r0, Q_BLOCK), g * GROUP_W + p * PAIR_W:g * GROUP_W + (p + 1) * PAIR_W] = (
            jnp.where(lo, even, odd).astype(ref.dtype))


def _sink_row(sink_ref, g):
    return jnp.concatenate([jnp.full((1, Q_BLOCK), sink_ref[0, g * GQA_GROUP + h], F32)
                            for h in range(GQA_GROUP)], axis=1)


def _group_probs(qs, k2, bias, sink):
    s = _dot_nt(k2, qs) * (HEAD_DIM ** -0.5) + bias
    m = jnp.maximum(jnp.max(s, axis=0, keepdims=True), sink)
    p = jnp.exp(s - m)
    psink = jnp.exp(sink - m)
    inv = 1.0 / (jnp.sum(p, axis=0, keepdims=True) + psink)
    return p * inv, psink * inv


def _attn_fwd(projp, sinks, *, seq, q_blk, k_blk, v_blk, name, comm=None):
    T = projp.shape[0]
    QW = N_Q_HEADS * HEAD_DIM
    nblk = seq // Q_BLOCK

    def body(q_ref, k_ref, v_ref, sink_ref, o_ref, k2s, v2s):
        _dup_heads(k_ref, k2s, seq)
        _dup_heads(v_ref, v2s, seq)
        lo = _lane_lo()
        bias0, key_index = _band_bias()
        sink_rows = [_sink_row(sink_ref, g) for g in range(N_KV_HEADS)]

        def blk(n, carry):
            r0 = pl.multiple_of(n * Q_BLOCK, Q_BLOCK)
            band = pl.ds(r0, BAND)
            qb = q_ref[pl.ds(r0, Q_BLOCK), :]
            bias = _block_bias(bias0, key_index, r0)
            for g in range(N_KV_HEADS):
                probs_t, _ = _group_probs(_stack_heads(qb, g, lo), k2s[g, band, :], bias, sink_rows[g])
                _unstack_heads(_dot_tn(probs_t.astype(BF16), v2s[g, band, :]), o_ref, r0, g, lo)
            return carry

        lax.fori_loop(0, nblk, blk, 0, unroll=2)

    blocks = [((seq, QW), BF16)] * 2 + [((seq, KV_WIDTH), BF16)] * 2
    return _call(
        body, (projp, projp, projp, sinks), name=name, grid=(T // seq,),
        in_specs=[pl.BlockSpec((seq, QW), lambda b: (b, q_blk)),
                  pl.BlockSpec((seq, KV_WIDTH), lambda b: (b, k_blk)),
                  pl.BlockSpec((seq, KV_WIDTH), lambda b: (b, v_blk)),
                  pl.BlockSpec(memory_space=pltpu.SMEM)],
        out_specs=[pl.BlockSpec((seq, QW), lambda b: (b, 0))],
        out_shape=[SDS((T, QW), BF16)],
        scratch_shapes=[pltpu.VMEM((N_KV_HEADS, seq + ATT_BLOCK, PAIR_W), BF16)] * 2,
        params=_params(1, blocks, temp_bytes=16 * 2**20), comm=comm)[0]


def _attn_bwd(projp, dao, sinks, *, seq, q_blk, k_blk, v_blk, name, comm=None):
    T = projp.shape[0]
    QW = N_Q_HEADS * HEAD_DIM
    assert seq % (2 * Q_BLOCK) == 0
    nblk = seq // Q_BLOCK

    def body(q_ref, k_ref, v_ref, do_ref, sink_ref, dq_ref, dk_ref, dv_ref, dsink_ref, k2s, v2s, dkacc, dvacc):
        _dup_heads(k_ref, k2s, seq)
        _dup_heads(v_ref, v2s, seq)
        dkacc[...] = jnp.zeros(dkacc.shape, F32)
        dvacc[...] = jnp.zeros(dvacc.shape, F32)
        lane = lax.broadcasted_iota(jnp.int32, (1, PAIR_W), 1)
        lo = lane < HEAD_DIM
        bias0, key_index = _band_bias()
        sink_rows = [_sink_row(sink_ref, g) for g in range(N_KV_HEADS)]

        def blk(n, tsinks):
            tsinks = list(tsinks)
            r0 = pl.multiple_of(n * Q_BLOCK, Q_BLOCK)
            band = pl.ds(r0, BAND)
            qb = q_ref[pl.ds(r0, Q_BLOCK), :]
            dob = do_ref[pl.ds(r0, Q_BLOCK), :]
            bias = _block_bias(bias0, key_index, r0)
            for g in range(N_KV_HEADS):
                qs = _stack_heads(qb, g, lo)
                dos = _stack_heads(dob, g, lo)
                k2 = k2s[g, band, :]
                v2 = v2s[g, band, :]
                probs_t, psink = _group_probs(qs, k2, bias, sink_rows[g])
                dp_t = _dot_nt(v2, dos)
                delta = jnp.sum(probs_t * dp_t, axis=0, keepdims=True)
                ds_t = (probs_t * (dp_t - delta) * (HEAD_DIM ** -0.5)).astype(BF16)
                tsinks[g] = tsinks[g] + psink * delta
                _unstack_heads(_dot_tn(ds_t, k2), dq_ref, r0, g, lo)
                dkacc[g, band, :] = dkacc[g, band, :] + _dot(ds_t, qs)
                dvacc[g, band, :] = dvacc[g, band, :] + _dot(probs_t.astype(BF16), dos)
            return tuple(tsinks)

        def two_blocks(m, tsinks):
            return blk(2 * m + 1, blk(2 * m, tsinks))

        tsinks = lax.fori_loop(0, nblk // 2, two_blocks, (jnp.zeros((1, GROUP_ROWS), F32),) * N_KV_HEADS)
        dsink = jnp.zeros((1, PAIR_W), F32)
        for g in range(N_KV_HEADS):
            for h in range(GQA_GROUP):
                dsink = dsink + jnp.where(lane == g * GQA_GROUP + h,
                                          -jnp.sum(tsinks[g][:, h * Q_BLOCK:(h + 1) * Q_BLOCK]), 0.0)
        _acc(dsink_ref, dsink, pl.program_id(0) == 0)

        def fold(acc, g):
            a = acc[g, pl.ds(ATT_BLOCK, seq), :]
            return a + pltpu.roll(a, HEAD_DIM, 1)

        dk_ref[...] = jnp.where(lo, fold(dkacc, 0), fold(dkacc, 1)).astype(BF16)
        dv_ref[...] = jnp.where(lo, fold(dvacc, 0), fold(dvacc, 1)).astype(BF16)

    blocks = [((seq, QW), BF16)] * 3 + [((seq, KV_WIDTH), BF16)] * 4
    kv_spec_out = pl.BlockSpec((seq, KV_WIDTH), lambda b: (b, 0))
    return _call(
        body, (projp, projp, projp, dao, sinks), name=name, grid=(T // seq,),
        in_specs=[pl.BlockSpec((seq, QW), lambda b: (b, q_blk)),
                  pl.BlockSpec((seq, KV_WIDTH), lambda b: (b, k_blk)),
                  pl.BlockSpec((seq, KV_WIDTH), lambda b: (b, v_blk)),
                  pl.BlockSpec((seq, QW), lambda b: (b, 0)),
                  pl.BlockSpec(memory_space=pltpu.SMEM)],
        out_specs=[pl.BlockSpec((seq, QW), lambda b: (b, 0)), kv_spec_out, kv_spec_out,
                   pl.BlockSpec((1, 128), lambda b: (0, 0))],
        out_shape=[SDS((T, QW), BF16), SDS((T, KV_WIDTH), BF16), SDS((T, KV_WIDTH), BF16), SDS((1, 128), F32)],
        scratch_shapes=[pltpu.VMEM((N_KV_HEADS, seq + ATT_BLOCK, PAIR_W), BF16)] * 2
        + [pltpu.VMEM((N_KV_HEADS, seq + ATT_BLOCK, PAIR_W), F32)] * 2,
        params=_params(1, blocks, temp_bytes=24 * 2**20), comm=comm)


SUBLANES = 8


def _sublane_shifts(win):
    n = CONV_ROWS + CONV_HALO
    return [win] + [pltpu.roll(win, n - b, 0) for b in range(1, SUBLANES)]


def _window(shifted, off):
    a = off // SUBLANES * SUBLANES
    return shifted[off % SUBLANES][a:a + CONV_ROWS, :]


def _conv_fwd(projp, w, bias, *, seq, cw, a_col, b_col, name, comm=None):
    T = projp.shape[0]
    C = w.shape[1]
    nchunk = seq // CONV_ROWS

    def body(a_ref, b_ref, w_ref, bias_ref, y_ref, upad):
        upad[pl.ds(0, CONV_HALO), :] = jnp.zeros((CONV_HALO, cw), F32)
        upad[pl.ds(CONV_HALO, seq), :] = a_ref[...].astype(F32) * _sigmoid(b_ref[...].astype(F32))
        wv = w_ref[...]
        bv = bias_ref[...]

        def chunk(r, carry):
            r0 = pl.multiple_of(r * CONV_ROWS, CONV_ROWS)
            shifted = _sublane_shifts(upad[pl.ds(r0, CONV_ROWS + CONV_HALO), :])
            acc = jnp.broadcast_to(bv, (CONV_ROWS, cw))
            for k in range(CONV_WIDTH):
                acc = acc + wv[k:k + 1, :] * _window(shifted, CONV_HALO - (CONV_WIDTH - 1) + k)
            y_ref[pl.ds(r0, CONV_ROWS), :] = acc
            return carry

        lax.fori_loop(0, nchunk, chunk, 0)

    blocks = [((seq, cw), BF16)] * 2 + [((seq, cw), F32)]
    return _call(
        body, (projp, projp, w, bias), name=name, grid=(T // seq, C // cw),
        in_specs=[pl.BlockSpec((seq, cw), lambda b, c: (b, a_col // cw + c)),
                  pl.BlockSpec((seq, cw), lambda b, c: (b, b_col // cw + c)),
                  pl.BlockSpec((CONV_WIDTH, cw), lambda b, c: (0, c)),
                  pl.BlockSpec((1, cw), lambda b, c: (0, c))],
        out_specs=[pl.BlockSpec((seq, cw), lambda b, c: (b, c))],
        out_shape=[SDS((T, C), F32)],
        scratch_shapes=[pltpu.VMEM((seq + CONV_HALO, cw), F32)],
        params=_params(2, blocks, temp_bytes=6 * _nbytes((seq, cw), F32)), comm=comm)[0]


def _conv_bwd(dy, projp, w, *, seq, cw, a_col, b_col, name, comm=None):
    T = projp.shape[0]
    C = w.shape[1]
    nchunk = seq // CONV_ROWS
    SUB = 8

    def body(dy_ref, a_ref, b_ref, w_ref, da_ref, db_ref, dw_ref, dbias_ref, dypad, dwp):
        first = pl.program_id(1) == 0
        dyv = dy_ref[...]
        dypad[pl.ds(0, seq), :] = dyv
        dypad[pl.ds(seq, CONV_HALO), :] = jnp.zeros((CONV_HALO, cw), F32)
        dwp[...] = jnp.zeros(dwp.shape, F32)
        wv = w_ref[...]

        def chunk(r, carry):
            r0 = pl.multiple_of(r * CONV_ROWS, CONV_ROWS)
            dy_shifts = _sublane_shifts(dypad[pl.ds(r0, CONV_ROWS + CONV_HALO), :])
            ac = a_ref[pl.ds(r0, CONV_ROWS), :].astype(F32)
            sbc = _sigmoid(b_ref[pl.ds(r0, CONV_ROWS), :].astype(F32))
            uc = ac * sbc
            du = jnp.zeros((CONV_ROWS, cw), F32)
            for k in range(CONV_WIDTH):
                dyk = _window(dy_shifts, CONV_WIDTH - 1 - k)
                du = du + wv[k:k + 1, :] * dyk
                prod = uc * dyk
                part = prod[0:SUB, :]
                for s in range(1, CONV_ROWS // SUB):
                    part = part + prod[s * SUB:(s + 1) * SUB, :]
                dwp[pl.ds(k * SUB, SUB), :] = dwp[pl.ds(k * SUB, SUB), :] + part
            da_ref[pl.ds(r0, CONV_ROWS), :] = (du * sbc).astype(BF16)
            db_ref[pl.ds(r0, CONV_ROWS), :] = (du * ac * (sbc * (1.0 - sbc))).astype(BF16)
            return carry

        lax.fori_loop(0, nchunk, chunk, 0)

        @pl.when(first)
        def _():
            dw_ref[...] = jnp.zeros(dw_ref.shape, F32)
            dbias_ref[...] = jnp.zeros(dbias_ref.shape, F32)

        for k in range(CONV_WIDTH):
            dw_ref[k:k + 1, :] = dw_ref[k:k + 1, :] + _rowsum(dwp[pl.ds(k * SUB, SUB), :])
        dbias_ref[...] = dbias_ref[...] + _rowsum(dyv)

    blocks = [((seq, cw), F32)] + [((seq, cw), BF16)] * 4
    return _call(
        body, (dy, projp, projp, w), name=name, grid=(C // cw, T // seq),
        in_specs=[pl.BlockSpec((seq, cw), lambda c, b: (b, c)),
                  pl.BlockSpec((seq, cw), lambda c, b: (b, a_col // cw + c)),
                  pl.BlockSpec((seq, cw), lambda c, b: (b, b_col // cw + c)),
                  pl.BlockSpec((CONV_WIDTH, cw), lambda c, b: (0, c))],
        out_specs=[pl.BlockSpec((seq, cw), lambda c, b: (b, c)), pl.BlockSpec((seq, cw), lambda c, b: (b, c)),
                   pl.BlockSpec((CONV_WIDTH, cw), lambda c, b: (0, c)), pl.BlockSpec((1, cw), lambda c, b: (0, c))],
        out_shape=[SDS((T, C), BF16), SDS((T, C), BF16), SDS((CONV_WIDTH, C), F32), SDS((1, C), F32)],
        scratch_shapes=[pltpu.VMEM((seq + CONV_HALO, cw), F32), pltpu.VMEM((CONV_WIDTH * SUB, cw), F32)],
        params=_params(2, blocks, temp_bytes=8 * _nbytes((seq, cw), F32)), comm=comm)


def _matmul_tn(a, b, *, name, gate=None, comm=None):
    T, M = a.shape
    N = b.shape[1]
    bm = _pick(M, (768, 512, 256))
    lhs = [a] if gate is None else [a, gate]

    def body(*refs):
        b_ref, o_ref = refs[len(lhs)], refs[len(lhs) + 1]
        av = refs[0][...]
        if gate is not None:
            af = av.astype(F32)
            av = (af * _sigmoid(af) * refs[1][...].astype(F32)).astype(BF16)
        o_ref[...] = _dot_tn(av, b_ref[...]).astype(BF16)

    blocks = [((T, bm), BF16)] * len(lhs) + [((T, N), BF16), ((bm, N), BF16)]
    return _call(
        body, (*lhs, b), name=name, grid=(M // bm,),
        in_specs=[pl.BlockSpec((T, bm), lambda i: (0, i))] * len(lhs) + [pl.BlockSpec((T, N), lambda i: (0, 0))],
        out_specs=[pl.BlockSpec((bm, N), lambda i: (i, 0))],
        out_shape=[SDS((M, N), BF16)],
        params=_params(1, blocks, temp_bytes=(2 + 4 * len(lhs)) * _nbytes((T, bm), BF16) + 2 * _nbytes((bm, N), F32)),
        comm=comm)[0]


TN_BLOCK = 256


def _matmul_tn_pieces(groups, b, *, name, comm=None):
    T, N = b.shape
    flat = [a for g in groups for a in g]
    starts, n_steps = [], 0
    for g in groups:
        width = sum(a.shape[1] for a in g)
        assert width % TN_BLOCK == 0 and (len(g) == 1 or width == TN_BLOCK), [a.shape for a in g]
        starts.append(n_steps)
        n_steps += width // TN_BLOCK

    def body(*refs):
        a_refs, b_ref, o_ref = refs[:len(flat)], refs[len(flat)], refs[len(flat) + 1]
        i = pl.program_id(0)
        at = 0
        for g, start in zip(groups, starts):
            mine = a_refs[at:at + len(g)]
            at += len(g)
            steps = sum(a.shape[1] for a in g) // TN_BLOCK

            @pl.when(jnp.logical_and(i >= start, i < start + steps))
            def _(mine=mine):
                a = mine[0][...] if len(mine) == 1 else jnp.concatenate([r[...] for r in mine], axis=1)
                o_ref[...] = _dot_tn(a, b_ref[...]).astype(BF16)

    a_specs = []
    for g, start in zip(groups, starts):
        for a in g:
            if len(g) == 1:
                last = a.shape[1] // TN_BLOCK - 1
                a_specs.append(pl.BlockSpec(
                    (T, TN_BLOCK), lambda i, start=start, last=last: (0, jnp.clip(i - start, 0, last))))
            else:
                a_specs.append(pl.BlockSpec((T, a.shape[1]), lambda i: (0, 0)))
    blocks = [((T, TN_BLOCK), BF16)] * len(flat) + [((T, N), BF16), ((TN_BLOCK, N), BF16)]
    return _call(
        body, (*flat, b), name=name, grid=(n_steps,),
        in_specs=a_specs + [pl.BlockSpec((T, N), lambda i: (0, 0))],
        out_specs=[pl.BlockSpec((TN_BLOCK, N), lambda i: (i, 0))],
        out_shape=[SDS((n_steps * TN_BLOCK, N), BF16)],
        params=_params(1, blocks, temp_bytes=2 * _nbytes((T, TN_BLOCK), BF16) + 2 * _nbytes((TN_BLOCK, N), F32)),
        comm=comm)[0]


def _sum_parts(p_ref):
    g = p_ref[0].astype(F32)
    for s in range(1, p_ref.shape[0]):
        g = g + p_ref[s].astype(F32)
    return g


def _pair_add(g, staged, *, name):
    _, R, W = g.shape
    nq = staged.shape[0]
    tr = _row_tile(R)

    def body(g_ref, s_ref, o_ref):
        mine = jnp.where(lax.axis_index("c") == 0, g_ref[0, 0].astype(F32), g_ref[0, 1].astype(F32))
        o_ref[0] = (mine + s_ref[0].astype(F32)).astype(o_ref.dtype)

    return _call(
        body, (g.reshape(nq, 2, R, W), staged), name=name, grid=(nq, R // tr),
        in_specs=[pl.BlockSpec((1, 2, tr, W), lambda q, i: (q, 0, i, 0)),
                  pl.BlockSpec((1, tr, W), lambda q, i: (q, i, 0))],
        out_specs=[pl.BlockSpec((1, tr, W), lambda q, i: (q, i, 0))],
        out_shape=[SDS((nq, R, W), g.dtype)],
        params=_params(2, [((4, tr, W), g.dtype)], temp_bytes=3 * _nbytes((tr, W), F32)))[0]


def _adamw_update(w, g, m, v):
    m = ADAM_B1 * m + (1.0 - ADAM_B1) * g
    v = ADAM_B2 * v + (1.0 - ADAM_B2) * (g * g)
    m_hat = m / (1.0 - ADAM_B1 ** ADAM_STEP)
    v_hat = v / (1.0 - ADAM_B2 ** ADAM_STEP)
    delta = -ADAM_LR * (m_hat / (jnp.sqrt(v_hat) + ADAM_EPS) + ADAM_WD * w)
    return delta, m, v


def _row_tile(R):
    return _pick(R, (256, 128, 112, 88, 64, 32, 16, 8))


def _sum8(parts, *, name):
    n, R, W = parts.shape
    tr = _row_tile(R)

    def body(p_ref, o_ref):
        o_ref[...] = _sum_parts(p_ref)

    return _call(
        body, (parts,), name=name, grid=(R // tr,),
        in_specs=[pl.BlockSpec((n, tr, W), lambda i: (0, i, 0))],
        out_specs=[pl.BlockSpec((tr, W), lambda i: (i, 0))],
        out_shape=[SDS((R, W), F32)],
        params=_params(1, [((n, tr, W), parts.dtype), ((tr, W), F32)]))[0]


def _adamw(g, w, m, v, *, name):
    R, W = w.shape
    tr = _row_tile(R)

    def body(g_ref, w_ref, m_ref, v_ref, d_ref, mo_ref, vo_ref):
        d_ref[...], mo_ref[...], vo_ref[...] = _adamw_update(w_ref[...], g_ref[...], m_ref[...], v_ref[...])

    spec = pl.BlockSpec((tr, W), lambda i: (i, 0))
    return _call(
        body, (g, w, m, v), name=name, grid=(R // tr,),
        in_specs=[spec] * 4, out_specs=[spec] * 3, out_shape=[SDS((R, W), F32)] * 3,
        params=_params(1, [((tr, W), F32)] * 7))


def _sum8_adamw(parts, w, m, v, *, name):
    R, W = w.shape
    n = parts.shape[0]
    tr = _row_tile(R)

    def body(p_ref, w_ref, m_ref, v_ref, g_ref, d_ref, mo_ref, vo_ref):
        g = _sum_parts(p_ref)
        g_ref[...] = g
        d_ref[...], mo_ref[...], vo_ref[...] = _adamw_update(w_ref[...], g, m_ref[...], v_ref[...])

    spec = pl.BlockSpec((tr, W), lambda i: (i, 0))
    return _call(
        body, (parts, w, m, v), name=name, grid=(R // tr,),
        in_specs=[pl.BlockSpec((n, tr, W), lambda i: (0, i, 0))] + [spec] * 3,
        out_specs=[spec] * 4, out_shape=[SDS((R, W), F32)] * 4,
        params=_params(1, [((n, tr, W), parts.dtype)] + [((tr, W), F32)] * 7))


def _ada_fwd(c_all, w, bias, *, name):
    NB, D = c_all.shape
    N = w.shape[1]

    def body(c_ref, w_ref, b_ref, o_ref):
        cv = c_ref[...]
        ca = (cv * _sigmoid(cv)).astype(BF16)
        o_ref[...] = _dot(ca, w_ref[...].astype(BF16)) + b_ref[...]

    full = lambda s: pl.BlockSpec(s, lambda i: (0,) * len(s))
    return _call(
        body, (c_all, w, bias), name=name, grid=(1,),
        in_specs=[full((NB, D)), full((D, N)), full((1, N))], out_specs=[full((NB, N))],
        out_shape=[SDS((NB, N), F32)],
        params=_params(1, [((D, N), F32)], temp_bytes=_nbytes((D, N), BF16)))[0]


def _ada_bwd(c_all, gmod_all, *, n_col, name):
    NB, D = c_all.shape
    N = gmod_all.shape[1]

    def body(c_ref, g_ref, gw_ref, gb_ref):
        cv = c_ref[...]
        ca = (cv * _sigmoid(cv)).astype(BF16)
        first = pl.multiple_of(_lin(_my_pos()) * n_col, 128)
        gw_ref[...] = _dot_tn(ca, g_ref[:, pl.ds(first, n_col)].astype(BF16))
        gb_ref[...] = _rowsum(g_ref[...])

    full = lambda s: pl.BlockSpec(s, lambda i: (0,) * len(s))
    return _call(
        body, (c_all, gmod_all), name=name, grid=(1,),
        in_specs=[full((NB, D)), full((NB, N))], out_specs=[full((D, n_col)), full((1, N))],
        out_shape=[SDS((D, n_col), F32), SDS((1, N), F32)],
        params=_params(1, [((D, n_col), F32), ((NB, N), F32)]))


def kernel(x, c, w_ada, b_ada, norm_ffn1_g, ffn1_w_gate, ffn1_w_up, ffn1_w_down, norm_mix_g, w_in, attn_sinks, w_attn_o, conv_w_dw, conv_b_dw, conv_ln_g, conv_ln_b, w_conv_o, w_out, norm_ffn2_g, ffn2_w_gate, ffn2_w_up, ffn2_w_down, final_norm_g, loss_target, m_w_ada, m_b_ada, m_norm_ffn1_g, m_ffn1_w_gate, m_ffn1_w_up, m_ffn1_w_down, m_norm_mix_g, m_w_in, m_attn_sinks, m_w_attn_o, m_conv_w_dw, m_conv_b_dw, m_conv_ln_g, m_conv_ln_b, m_w_conv_o, m_w_out, m_norm_ffn2_g, m_ffn2_w_gate, m_ffn2_w_up, m_ffn2_w_down, m_final_norm_g, v_w_ada, v_b_ada, v_norm_ffn1_g, v_ffn1_w_gate, v_ffn1_w_up, v_ffn1_w_down, v_norm_mix_g, v_w_in, v_attn_sinks, v_w_attn_o, v_conv_w_dw, v_conv_b_dw, v_conv_ln_g, v_conv_ln_b, v_w_conv_o, v_w_out, v_norm_ffn2_g, v_ffn2_w_gate, v_ffn2_w_up, v_ffn2_w_down, v_final_norm_g):
    B, S, D = x.shape
    T = B * S
    QW = N_Q_HEADS * HEAD_DIM
    CC = conv_w_dw.shape[2] * N_DEV
    me = _lin(_my_pos())
    xf = x.reshape(T, D)
    tgt = loss_target.reshape(T, D)
    tm = min(512, S)
    kw = dict(seq=S, tm=tm)

    p_k, p_v, p_ca = QW, QW + KV_WIDTH, QW + 2 * KV_WIDTH
    p_cb, p_ga, p_gc = p_ca + CC, p_ca + 2 * CC, p_ca + 2 * CC + D

    def col_t(w):
        return w[0].T.astype(BF16)

    def row_b(w):
        return w[0].astype(BF16)

    def rows(g):
        return g.reshape(-1, g.shape[-1])

    def blocks8(g):
        return g.reshape(N_DEV, g.shape[0] // N_DEV, g.shape[1])

    def gather(*arrs):
        return _Comm([(a, "gather") for a in arrs])

    g_wg1, g_convw, g_c = _exchange(
        [(col_t(ffn1_w_gate), "gather"), (conv_w_dw[0], "gather"), (c, "gather")], name="gather_first")
    wg1 = rows(g_wg1)
    conv_w = g_convw.transpose(1, 0, 2).reshape(CONV_WIDTH, CC)
    c_all = g_c.reshape(N_DEV * B, D)

    n_col = N_MOD * D // N_DEV
    b_cols = lax.dynamic_slice(b_ada, (0, me * n_col), (1, n_col))
    mod_cols = _ada_fwd(c_all, w_ada[0], b_cols, name="ada_fwd")
    mod_mine = _exchange([(mod_cols.reshape(N_DEV, B, n_col), "scatter")], name="scatter_mod")[0]
    mod = mod_mine.transpose(1, 0, 2).reshape(B * N_MOD, 1, D)
    sh1, sc1, g1, sh2, sc2, g2, sh3, sc3, g3 = [_ModVec(mod, i) for i in range(N_MOD)]

    F = wg1.shape[0]
    tn_f = _pick(F, (1408, 1024, 512, 256))
    tn_in = _pick(w_in.shape[2] * N_DEV, (1792, 768, 512, 256))
    gate_blk = dict(ga_col=p_ga, gc_col=p_gc)
    att_blk = dict(q_blk=0, k_blk=p_k // KV_WIDTH, v_blk=p_v // KV_WIDTH)
    conv_kw = dict(seq=S, cw=256, a_col=p_ca, b_col=p_cb)

    cm = gather(col_t(ffn1_w_up))
    h1, (a1,) = _norm_mod_matmul(xf, norm_ffn1_g, sh1, sc1, [wg1], tn=tn_f, name="ffn1_gate", comm=cm, **kw)
    wu1 = rows(cm.out[0])
    cm = gather(row_b(ffn1_w_down))
    b1 = _matmul_nt(h1, wu1, tm=tm, tn=tn_f, name="ffn1_up", comm=cm)
    wd1 = rows(cm.out[0])
    cm = gather(col_t(w_in))
    x1, y1 = _ffn_down(a1, b1, wd1, xf, g1, name="ffn1_down", comm=cm, **kw)
    winp = rows(cm.out[0])
    cm = gather(row_b(w_attn_o), row_b(w_conv_o), row_b(w_out), col_t(ffn2_w_gate))
    h2, (projp,) = _norm_mod_matmul(x1, norm_mix_g, sh2, sc2, [winp], tn=tn_in, name="mix_in", comm=cm, **kw)
    wao, wco, wout, wg2 = [rows(o) for o in cm.out]
    cm = gather(col_t(ffn2_w_up))
    ao = _attn_fwd(projp, attn_sinks, seq=S, name="attn_fwd", comm=cm, **att_blk)
    wu2 = rows(cm.out[0])
    cm = gather(row_b(ffn2_w_down))
    yc = _conv_fwd(projp, conv_w, conv_b_dw, name="conv_fwd", comm=cm, **conv_kw)
    wd2 = rows(cm.out[0])
    x2, z, ya, ycv, cact, merged = _mix_out(ao, yc, projp, wao, wco, wout, x1, g2, conv_ln_g, conv_ln_b,
                                            name="mix_out", **gate_blk, **kw)
    h3, (a3, b3) = _norm_mod_matmul(x2, norm_ffn2_g, sh3, sc3, [wg2, wu2], tn=tn_f, name="ffn2_up", **kw)
    x3, y3 = _ffn_down(a3, b3, wd2, x2, g3, name="ffn2_down", **kw)
    dx3, loss_row, dgf = _final_loss(x3, final_norm_g[None], tgt, tm=tm, name="final_loss")

    parts = {}

    def pair(*gs):
        return [(blocks8(g), "pair") for g in gs]

    def cross(*rs):
        return [(r, "cross") for r in rs]

    def reduce_pairs(gs, staged, names):
        return [_pair_add(blocks8(g), s, name="pair_add_" + n) for g, s, n in zip(gs, staged, names)]

    dyb3, da3, db3, dg3 = _ffn_bwd_down(dx3, g3, y3, wd2, a3, b3, tn=tn_f, name="ffn2_bwd_down", **kw)
    gwd2 = _matmul_tn(a3, dyb3, gate=b3, name="gw_ffn2_down")
    cm = _Comm(pair(gwd2))
    dx2, dsh3, dsc3, dgn3 = _matmul_norm_mod_bwd([[da3], [db3]], [wg2, wu2], x2, norm_ffn2_g, sc3, dx3,
                                                 name="ffn2_bwd_up", out_dtype=GRAD_STREAM, comm=cm, **kw)
    r_wd2, = reduce_pairs([gwd2], cm.out, ["ffn2_w_down"])
    cm = _Comm(cross(r_wd2))
    gwg2 = _matmul_tn(da3, h3, name="gw_ffn2_gate", comm=cm)
    parts["ffn2_w_down"], = cm.out
    cm = _Comm(pair(gwg2))
    gwu2 = _matmul_tn(db3, h3, name="gw_ffn2_up", comm=cm)
    r_wg2, = reduce_pairs([gwg2], cm.out, ["ffn2_w_gate"])

    cm = _Comm(cross(r_wg2) + pair(gwu2))
    dzb, dyab, dycb, dga, dgc, dao, dyc, dg2, dlng, dlnb = _mix_out_bwd(
        dx2, g2, z, wout, projp, ya, ycv, wao, wco, yc, conv_ln_g, conv_ln_b, name="mix_out_bwd", comm=cm,
        **gate_blk, **kw)
    parts["ffn2_w_gate"] = cm.out[0]
    r_wu2, = reduce_pairs([gwu2], cm.out[1:], ["ffn2_w_up"])
    gwout = _matmul_tn(merged, dzb, name="gw_out")
    gwao = _matmul_tn(ao, dyab, name="gw_attn_o")
    gwco = _matmul_tn(cact, dycb, name="gw_conv_o")
    cm = _Comm(cross(r_wu2) + pair(gwout, gwao, gwco))
    dq, dk, dv, dsinks = _attn_bwd(projp, dao, attn_sinks, seq=S, name="attn_bwd", comm=cm, **att_blk)
    parts["ffn2_w_up"] = cm.out[0]
    r_mix = reduce_pairs([gwout, gwao, gwco], cm.out[1:], ["w_out", "w_attn_o", "w_conv_o"])
    cm = _Comm(cross(*r_mix))
    dca, dcb, dconvw, dconvb = _conv_bwd(dyc, projp, conv_w, name="conv_bwd", comm=cm, **conv_kw)
    parts["w_out"], parts["w_attn_o"], parts["w_conv_o"] = cm.out
    gwin = _matmul_tn_pieces([[dq], [dk, dv], [dca], [dcb], [dga], [dgc]], h2, name="gw_in")
    cm = _Comm(pair(gwin))
    dx1, dsh2, dsc2, dgn2 = _matmul_norm_mod_bwd([[dq, dk, dv, dca, dcb, dga, dgc]], [winp], x1, norm_mix_g, sc2, dx2,
                                                 name="mix_in_bwd", out_dtype=GRAD_STREAM, comm=cm, **kw)
    r_win, = reduce_pairs([gwin], cm.out, ["w_in"])

    cm = _Comm(cross(r_win))
    dyb1, da1, db1, dg1 = _ffn_bwd_down(dx1, g1, y1, wd1, a1, b1, tn=tn_f, name="ffn1_bwd_down", comm=cm,
                                              **kw)
    parts["w_in"], = cm.out
    gwd1 = _matmul_tn(a1, dyb1, gate=b1, name="gw_ffn1_down")
    cm = _Comm(pair(gwd1))
    gwg1 = _matmul_tn(da1, h1, name="gw_ffn1_gate", comm=cm)
    r_wd1, = reduce_pairs([gwd1], cm.out, ["ffn1_w_down"])
    cm = _Comm(cross(r_wd1) + pair(gwg1))
    gwu1 = _matmul_tn(db1, h1, name="gw_ffn1_up", comm=cm)
    parts["ffn1_w_down"] = cm.out[0]
    r_wg1, = reduce_pairs([gwg1], cm.out[1:], ["ffn1_w_gate"])
    r_wu1, = reduce_pairs([gwu1], _exchange(pair(gwu1), name="pair_last"), ["ffn1_w_up"])
    cm = _Comm(cross(r_wg1, r_wu1))
    dx0, dsh1, dsc1, dgn1 = _matmul_norm_mod_bwd([[da1], [db1]], [wg1, wu1], xf, norm_ffn1_g, sc1, dx1,
                                                 name="ffn1_bwd_up", out_dtype=F32, comm=cm, **kw)
    parts["ffn1_w_gate"], parts["ffn1_w_up"] = cm.out

    n_small = 8
    gmod = jnp.concatenate([dsh1, dsc1, dg1, dsh2, dsc2, dg2, dsh3, dsc3, dg3], axis=1).reshape(B, N_MOD * D)
    sink_row = jnp.pad(dsinks[:, :N_Q_HEADS], ((0, 0), (0, D - N_Q_HEADS)))
    loss_pad = jnp.pad(loss_row, ((0, 0), (0, D - loss_row.shape[1])))
    small = jnp.concatenate([dgn1, dgn2, dgn3, dgf, dconvb, dlng, dlnb, sink_row, dconvw, loss_pad], axis=0)
    small_all, gmod_all = _exchange([(small, "gather"), (gmod, "gather")], name="exchange_last")
    gsmall = _sum8(small_all, name="sum_small")
    loss = gsmall[n_small + CONV_WIDTH, 0]
    g_w_ada, g_b_ada = _ada_bwd(c_all, gmod_all.reshape(N_DEV * B, N_MOD * D), n_col=n_col, name="ada_bwd")
    g_conv_w = lax.dynamic_slice(gsmall[n_small:n_small + CONV_WIDTH], (0, me * (CC // N_DEV)),
                                 (CONV_WIDTH, CC // N_DEV))

    def col_update(name, w, m, v):
        outs = _sum8_adamw(parts[name], w[0].T, m[0].T, v[0].T, name="adamw_" + name)
        return tuple(o.T for o in outs)

    def row_update(name, w, m, v):
        return tuple(_sum8_adamw(parts[name], w[0], m[0], v[0], name="adamw_" + name))

    upd = {
        "ffn1_w_gate": col_update("ffn1_w_gate", ffn1_w_gate, m_ffn1_w_gate, v_ffn1_w_gate),
        "ffn1_w_up": col_update("ffn1_w_up", ffn1_w_up, m_ffn1_w_up, v_ffn1_w_up),
        "ffn1_w_down": row_update("ffn1_w_down", ffn1_w_down, m_ffn1_w_down, v_ffn1_w_down),
        "w_in": col_update("w_in", w_in, m_w_in, v_w_in),
        "w_attn_o": row_update("w_attn_o", w_attn_o, m_w_attn_o, v_w_attn_o),
        "w_conv_o": row_update("w_conv_o", w_conv_o, m_w_conv_o, v_w_conv_o),
        "w_out": row_update("w_out", w_out, m_w_out, v_w_out),
        "ffn2_w_gate": col_update("ffn2_w_gate", ffn2_w_gate, m_ffn2_w_gate, v_ffn2_w_gate),
        "ffn2_w_up": col_update("ffn2_w_up", ffn2_w_up, m_ffn2_w_up, v_ffn2_w_up),
        "ffn2_w_down": row_update("ffn2_w_down", ffn2_w_down, m_ffn2_w_down, v_ffn2_w_down),
        "w_ada": (g_w_ada,) + tuple(_adamw(g_w_ada, w_ada[0], m_w_ada[0], v_w_ada[0], name="adamw_w_ada")),
        "conv_w_dw": (g_conv_w,) + tuple(_adamw(g_conv_w, conv_w_dw[0], m_conv_w_dw[0], v_conv_w_dw[0],
                                                name="adamw_conv_w_dw")),
    }
    for k in upd:
        upd[k] = tuple(t[None] for t in upd[k])

    def pad_sinks(t):
        return jnp.pad(t, ((0, 0), (0, D - N_Q_HEADS)))

    def pack(f1, mix, f2, fin, cb, lg, lb, sinks, bada):
        return jnp.concatenate([f1, mix, f2, fin[None], cb, lg, lb, pad_sinks(sinks), bada.reshape(N_MOD, D)], axis=0)

    w_s = pack(norm_ffn1_g, norm_mix_g, norm_ffn2_g, final_norm_g, conv_b_dw, conv_ln_g, conv_ln_b, attn_sinks, b_ada)
    m_s = pack(m_norm_ffn1_g, m_norm_mix_g, m_norm_ffn2_g, m_final_norm_g, m_conv_b_dw, m_conv_ln_g, m_conv_ln_b,
               m_attn_sinks, m_b_ada)
    v_s = pack(v_norm_ffn1_g, v_norm_mix_g, v_norm_ffn2_g, v_final_norm_g, v_conv_b_dw, v_conv_ln_g, v_conv_ln_b,
               v_attn_sinks, v_b_ada)
    g_s = jnp.concatenate([gsmall[:n_small], g_b_ada.reshape(N_MOD, D)], axis=0)
    small_out = (g_s,) + tuple(_adamw(g_s, w_s, m_s, v_s, name="adamw_vectors"))

    def unpack(t):
        return {
            "norm_ffn1_g": t[0:1], "norm_mix_g": t[1:2], "norm_ffn2_g": t[2:3], "final_norm_g": t[3],
            "conv_b_dw": t[4:5], "conv_ln_g": t[5:6], "conv_ln_b": t[6:7], "attn_sinks": t[7:8, :N_Q_HEADS],
            "b_ada": t[n_small:n_small + N_MOD].reshape(1, N_MOD * D),
        }

    small_un = [unpack(t) for t in small_out]
    for k in small_un[0]:
        upd[k] = tuple(s[k] for s in small_un)

    order = ["w_ada", "b_ada", "norm_ffn1_g", "ffn1_w_gate", "ffn1_w_up", "ffn1_w_down", "norm_mix_g", "w_in",
             "attn_sinks", "w_attn_o", "conv_w_dw", "conv_b_dw", "conv_ln_g", "conv_ln_b", "w_conv_o", "w_out",
             "norm_ffn2_g", "ffn2_w_gate", "ffn2_w_up", "ffn2_w_down", "final_norm_g"]
    grad_x = dx0.reshape(B, S, D)
    return (loss, grad_x, *[upd[k][0] for k in order], *[upd[k][1] for k in order],
            *[upd[k][2] for k in order], *[upd[k][3] for k in order])
```

```python
import dataclasses

import jax
import jax.numpy as jnp
from jax import lax
from jax.experimental import pallas as pl
from jax.experimental.pallas import tpu as pltpu

F32 = jnp.float32
BF16 = jnp.bfloat16
SDS = jax.ShapeDtypeStruct
MESH = pl.DeviceIdType.MESH

N_DEV = 8
EPS = 1e-6
HEAD_DIM = 64
N_Q_HEADS = 16
N_KV_HEADS = 2
GQA_GROUP = N_Q_HEADS // N_KV_HEADS
KV_WIDTH = N_KV_HEADS * HEAD_DIM
ATT_BLOCK = 128
CONV_WIDTH = 31
CONV_HALO = 32
CONV_ROWS = 128
N_MOD = 9
FFN_RESIDUAL = 0.5
ADAM_LR = 0.001
ADAM_B1 = 0.9
ADAM_B2 = 0.999
ADAM_EPS = 1e-08
ADAM_WD = 0.01
ADAM_STEP = 10
NEG_BIG = -1e30
GRAD_STREAM = BF16

V7X_VMEM_BYTES = 64 * 2**20
VMEM_CAP = V7X_VMEM_BYTES - 8 * 2**20


def _nbytes(shape, dtype):
    n = 1
    for s in shape:
        n *= s
    return n * jnp.dtype(dtype).itemsize


def _params(n_axes, blocks, temp_bytes=0):
    need = 2 * sum(_nbytes(s, d) for s, d in blocks) + temp_bytes + 4 * 2**20
    return pltpu.CompilerParams(dimension_semantics=("arbitrary",) * n_axes,
                                vmem_limit_bytes=int(min(max(need, 16 * 2**20), VMEM_CAP)))


def _dot_nt(a, b):
    return lax.dot_general(a, b, (((1,), (1,)), ((), ())), preferred_element_type=F32)


def _dot_tn(a, b):
    return lax.dot_general(a, b, (((0,), (0,)), ((), ())), preferred_element_type=F32)


def _dot(a, b):
    return jnp.dot(a, b, preferred_element_type=F32)


def _sigmoid(x):
    return jax.nn.sigmoid(x)


def _rowsum(v):
    return jnp.sum(v, axis=0, keepdims=True)


def _acc(ref, val, first):
    @pl.when(first)
    def _():
        ref[...] = val

    @pl.when(jnp.logical_not(first))
    def _():
        ref[...] = ref[...] + val


def _norm_mod(xf, gn, sh, sc):
    rstd = lax.rsqrt(jnp.mean(xf * xf, axis=-1, keepdims=True) + EPS)
    xhat = xf * rstd
    yn = xhat * gn
    return yn * (1.0 + sc) + sh, xhat, rstd, yn


def _pick(n, cands):
    for c in cands:
        if n % c == 0:
            return c
    return n


def _my_pos():
    return lax.axis_index("x"), lax.axis_index("y"), lax.axis_index("c")


def _peer(pos, k):
    x, y, c = pos
    return ((1 - x) if k & 4 else x, (1 - y) if k & 2 else y, (1 - c) if k & 1 else c)


def _lin(pos):
    return 4 * pos[0] + 2 * pos[1] + pos[2]


class _Comm:
    N_COPY = N_DEV - 1
    N_CHIP = N_DEV // 2

    def __init__(self, items):
        self.arrs = [a for a, _ in items]
        self.modes = [m for _, m in items]
        self.n = len(items)
        self.out = None

    def out_shape(self):
        def shape(a, m):
            return {"gather": (N_DEV,) + a.shape, "scatter": a.shape, "pair": (self.N_CHIP,) + a.shape[1:],
                    "cross": a.shape}[m]
        return [SDS(shape(a, m), a.dtype) for a, m in zip(self.arrs, self.modes)]

    def scratch(self):
        return [pltpu.SemaphoreType.DMA((self.n * self.N_COPY,)), pltpu.SemaphoreType.DMA((self.n * self.N_COPY,)),
                pltpu.SemaphoreType.DMA((self.n,))]

    def collective_id(self):
        modes = set(self.modes)
        if "scatter" in modes:
            return 3
        d2d, ici = bool(modes & {"gather", "pair"}), bool(modes & {"gather", "cross"})
        return {(True, False): 0, (False, True): 1, (True, True): 2}[(d2d, ici)]

    def barrier(self):
        x, y, c = _my_pos()
        peers = {0: [(x, y, 1 - c)],
                 1: [(1 - x, y, c), (x, 1 - y, c), (1 - x, 1 - y, c)],
                 2: [(x, y, 1 - c), (1 - x, y, c), (x, 1 - y, c), (1 - x, 1 - y, c)],
                 3: [_peer((x, y, c), k) for k in range(1, N_DEV)]}[self.collective_id()]
        sem = pltpu.get_barrier_semaphore()
        for p in peers:
            pl.semaphore_signal(sem, inc=1, device_id=p, device_id_type=MESH)
        pl.semaphore_wait(sem, len(peers))

    def _plan(self, mode, me):
        x, y, c = me
        sib = (x, y, 1 - c)
        chips = [(1 - x, y), (x, 1 - y), (1 - x, 1 - y)]

        def chip_lin(ch):
            return 2 * ch[0] + ch[1]

        if mode == "scatter":
            peers = [_peer(me, k + 1) for k in range(self.N_COPY)]
            return [(p, ("in", _lin(p)), _lin(me), _lin(p), None) for p in peers], (_lin(me), _lin(me))
        if mode == "gather":
            same = [(*ch, c) for ch in chips]
            other = [(*ch, 1 - c) for ch in chips]
            copies = [(sib, ("in", None), _lin(me), _lin(sib), None)]
            copies += [(p, ("in", None), _lin(me), _lin(p), None) for p in same]
            copies += [(sib, ("out", _lin(p)), _lin(p), _lin(o), 1 + j) for j, (p, o) in enumerate(zip(same, other))]
            return copies, (None, _lin(me))
        if mode == "pair":
            return [(sib, ("in", 2 * q + 1 - c), q, q, None) for q in range(self.N_CHIP)], None
        if mode == "cross":
            mine = chip_lin((x, y))
            return ([((*ch, c), ("in", chip_lin(ch)), mine, chip_lin(ch), None) for ch in chips], (mine, mine))
        raise ValueError(mode)

    def _copy(self, refs, me, i, k, recv):
        srcs, outs, (send_sems, recv_sems, _) = refs
        peer, (where, slot), send_slot, recv_slot, _ = self._plan(self.modes[i], me)[0][k]
        src = srcs[i] if where == "in" else outs[i]
        src = src if slot is None else src.at[slot]
        sem = i * self.N_COPY + k
        return pltpu.make_async_remote_copy(
            src_ref=src, dst_ref=outs[i].at[recv_slot if recv else send_slot], send_sem=send_sems.at[sem],
            recv_sem=recv_sems.at[sem], device_id=peer, device_id_type=MESH)

    def _local(self, refs, me, i):
        srcs, outs, (_, _, loc_sems) = refs
        local = self._plan(self.modes[i], me)[1]
        if local is None:
            return None
        own = srcs[i] if local[0] is None else srcs[i].at[local[0]]
        return pltpu.make_async_copy(own, outs[i].at[local[1]], loc_sems.at[i])

    def start(self, refs):
        me = _my_pos()
        for i in range(self.n):
            local = self._local(refs, me, i)
            if local is not None:
                local.start()
            for k, cp in enumerate(self._plan(self.modes[i], me)[0]):
                if cp[4] is None:
                    self._copy(refs, me, i, k, False).start()

    def forward(self, refs):
        me = _my_pos()
        for i in range(self.n):
            for k, cp in enumerate(self._plan(self.modes[i], me)[0]):
                if cp[4] is not None:
                    self._copy(refs, me, i, cp[4], True).wait_recv()
                    self._copy(refs, me, i, k, False).start()

    def finish(self, refs):
        me = _my_pos()
        plans = [self._plan(m, me)[0] for m in self.modes]
        for i in range(self.n):
            passed_on = [cp[4] for cp in plans[i] if cp[4] is not None]
            for k in range(len(plans[i])):
                if k not in passed_on:
                    self._copy(refs, me, i, k, True).wait_recv()
                self._copy(refs, me, i, k, False).wait_send()
            local = self._local(refs, me, i)
            if local is not None:
                local.wait()


_ANY = pl.BlockSpec(memory_space=pl.ANY)


def _in_hbm(a):
    return pltpu.with_memory_space_constraint(a, pltpu.HBM)


def _call(body, args, *, name, grid, in_specs, out_specs, out_shape, params, scratch_shapes=(), comm=None,
          hbm_out=()):
    in_specs, out_specs, out_shape = list(in_specs), list(out_specs), list(out_shape)
    scratch_shapes = list(scratch_shapes)
    for k in hbm_out:
        out_shape[k] = pltpu.HBM(out_shape[k].shape, out_shape[k].dtype)
    if comm is None:
        return list(pl.pallas_call(body, name=name, grid=grid, in_specs=in_specs, out_specs=out_specs,
                                   out_shape=out_shape, scratch_shapes=scratch_shapes, compiler_params=params)(*args))
    n_in, n_out, n_scr, nc = len(in_specs), len(out_specs), len(scratch_shapes), comm.n
    n_steps = 1
    for g in grid:
        n_steps *= g

    def hosted(*refs):
        ins, c_in = refs[:n_in], refs[n_in:n_in + nc]
        outs = refs[n_in + nc:n_in + nc + n_out]
        c_out = refs[n_in + nc + n_out:n_in + 2 * nc + n_out]
        scr = refs[n_in + 2 * nc + n_out:n_in + 2 * nc + n_out + n_scr]
        sems = refs[n_in + 2 * nc + n_out + n_scr:]
        step = pl.program_id(0)
        for d in range(1, len(grid)):
            step = step * grid[d] + pl.program_id(d)
        c_refs = (c_in, c_out, sems)

        @pl.when(step == 0)
        def _():
            comm.barrier()
            comm.start(c_refs)

        if n_steps >= 3:
            @pl.when(step == n_steps - 2)
            def _():
                comm.forward(c_refs)

        body(*ins, *outs, *scr)

        @pl.when(step == n_steps - 1)
        def _():
            if n_steps < 3:
                comm.forward(c_refs)
            comm.finish(c_refs)

    res = pl.pallas_call(
        hosted, name=name, grid=grid, in_specs=in_specs + [_ANY] * nc, out_specs=out_specs + [_ANY] * nc,
        out_shape=out_shape + comm.out_shape(), scratch_shapes=scratch_shapes + comm.scratch(),
        compiler_params=dataclasses.replace(params, collective_id=comm.collective_id()))(*args, *comm.arrs)
    comm.out = list(res[n_out:])
    return list(res[:n_out])


def _exchange(items, *, name):
    comm = _Comm(items)

    def body(*refs):
        r = (refs[:comm.n], refs[comm.n:2 * comm.n], refs[2 * comm.n:])
        comm.barrier()
        comm.start(r)
        comm.forward(r)
        comm.finish(r)

    return list(pl.pallas_call(body, name=name, out_shape=comm.out_shape(), in_specs=[_ANY] * comm.n,
                               out_specs=[_ANY] * comm.n, scratch_shapes=comm.scratch(),
                               compiler_params=pltpu.CompilerParams(collective_id=comm.collective_id()))(*comm.arrs))


class _ModVec:
    def __init__(self, arr, idx):
        self.arr, self.idx = arr, idx

    def spec(self, tps, n_axes):
        idx, blk = self.idx, (1, 1, self.arr.shape[2])
        if n_axes == 1:
            return pl.BlockSpec(blk, lambda i: (i // tps * N_MOD + idx, 0, 0))
        return pl.BlockSpec(blk, lambda i, j: (i // tps * N_MOD + idx, 0, 0))


def _norm_mod_matmul(x, gn, sh, sc, wts, *, seq, tm, tn, name, comm=None):
    T, D = x.shape
    N = wts[0].shape[0]
    nw = len(wts)
    tps = seq // tm

    def body(x_ref, gn_ref, sh_ref, sc_ref, *rest):
        w_refs, h_ref, o_refs = rest[:nw], rest[nw], rest[nw + 1:]

        @pl.when(pl.program_id(1) == 0)
        def _():
            h_ref[...] = _norm_mod(x_ref[...], gn_ref[...], sh_ref[0], sc_ref[0])[0].astype(BF16)

        h = h_ref[...]
        for w_ref, o_ref in zip(w_refs, o_refs):
            o_ref[...] = _dot_nt(h, w_ref[...]).astype(o_ref.dtype)

    row = pl.BlockSpec((tm, D), lambda i, j: (i, 0))
    vec = pl.BlockSpec((1, D), lambda i, j: (0, 0))
    wspec = pl.BlockSpec((tn, D), lambda i, j: (j, 0))
    ospec = pl.BlockSpec((tm, tn), lambda i, j: (i, j))
    blocks = [((tm, D), F32), ((tm, D), BF16)] + [((tn, D), BF16), ((tm, tn), BF16)] * nw
    outs = _call(
        body, (x, gn, sh.arr, sc.arr, *wts), name=name, grid=(T // tm, N // tn),
        in_specs=[row, vec, sh.spec(tps, 2), sc.spec(tps, 2)] + [wspec] * nw,
        out_specs=[row] + [ospec] * nw,
        out_shape=[SDS((T, D), BF16)] + [SDS((T, N), BF16)] * nw,
        params=_params(2, blocks, temp_bytes=2 * _nbytes((tm, tn), F32) + 3 * _nbytes((tm, D), F32)), comm=comm)
    return outs[0], outs[1:]


def _matmul_nt(h, w, *, tm, tn, name, comm=None):
    T, D = h.shape
    N = w.shape[0]

    def body(h_ref, w_ref, o_ref):
        o_ref[...] = _dot_nt(h_ref[...], w_ref[...]).astype(o_ref.dtype)

    blocks = [((tm, D), BF16), ((tn, D), BF16), ((tm, tn), BF16)]
    return _call(
        body, (h, w), name=name, grid=(T // tm, N // tn),
        in_specs=[pl.BlockSpec((tm, D), lambda i, j: (i, 0)), pl.BlockSpec((tn, D), lambda i, j: (j, 0))],
        out_specs=[pl.BlockSpec((tm, tn), lambda i, j: (i, j))],
        out_shape=[SDS((T, N), BF16)],
        params=_params(2, blocks, temp_bytes=2 * _nbytes((tm, tn), F32)), comm=comm)[0]


def _ffn_down(a, b, wd, x, g, *, seq, tm, name, comm=None):
    T, F = a.shape
    D = wd.shape[1]
    tps = seq // tm

    def body(a_ref, b_ref, wd_ref, x_ref, g_ref, xo_ref, y_ref):
        af = a_ref[...].astype(F32)
        act = (af * _sigmoid(af) * b_ref[...].astype(F32)).astype(BF16)
        y = _dot(act, wd_ref[...])
        xo_ref[...] = x_ref[...] + (FFN_RESIDUAL * g_ref[0]) * y
        y_ref[...] = y.astype(BF16)

    wide = pl.BlockSpec((tm, F), lambda i: (i, 0))
    row = pl.BlockSpec((tm, D), lambda i: (i, 0))
    wspec = pl.BlockSpec((F, D), lambda i: (0, 0))
    blocks = [((tm, F), BF16)] * 2 + [((F, D), BF16), ((tm, D), F32), ((tm, D), F32), ((tm, D), BF16)]
    return _call(
        body, (a, b, wd, x, g.arr), name=name, grid=(T // tm,),
        in_specs=[wide, wide, wspec, row, g.spec(tps, 1)], out_specs=[row, row],
        out_shape=[SDS((T, D), F32), SDS((T, D), BF16)],
        params=_params(1, blocks, temp_bytes=3 * _nbytes((tm, F), F32)), comm=comm)


def _final_loss(x, gf, tgt, *, tm, name):
    T, D = x.shape
    nt = T // tm

    def body(x_ref, gf_ref, t_ref, dx_ref, loss_ref, dgf_ref, lacc):
        i = pl.program_id(0)
        xf = x_ref[...]
        gfv = gf_ref[...]
        rstd = lax.rsqrt(jnp.mean(xf * xf, axis=-1, keepdims=True) + EPS)
        xhat = xf * rstd
        err = xhat * gfv - t_ref[...]
        dy = err * (1.0 / D)
        dxhat = dy * gfv
        dx_ref[...] = (rstd * (dxhat - xhat * jnp.mean(dxhat * xhat, axis=-1, keepdims=True))).astype(dx_ref.dtype)
        _acc(dgf_ref, _rowsum(dy * xhat), i == 0)
        _acc(lacc, _rowsum(err * err), i == 0)

        @pl.when(i == nt - 1)
        def _():
            loss_ref[...] = jnp.broadcast_to((0.5 / D) * jnp.sum(lacc[...]), loss_ref.shape)

    row = pl.BlockSpec((tm, D), lambda i: (i, 0))
    vec = pl.BlockSpec((1, D), lambda i: (0, 0))
    lspec = pl.BlockSpec((1, 128), lambda i: (0, 0))
    blocks = [((tm, D), F32)] * 3
    return _call(
        body, (x, gf, tgt), name=name, grid=(nt,),
        in_specs=[row, vec, row], out_specs=[row, lspec, vec],
        out_shape=[SDS((T, D), GRAD_STREAM), SDS((1, 128), F32), SDS((1, D), F32)],
        scratch_shapes=[pltpu.VMEM((1, D), F32)],
        params=_params(1, blocks, temp_bytes=4 * _nbytes((tm, D), F32)), hbm_out=(0,))


def _ffn_bwd_down(dxo, g, y, wd, a, b, *, seq, tm, tn, name, comm=None):
    T, F = a.shape
    D = wd.shape[1]
    tps = seq // tm
    nb = T // seq

    def body(dxo_ref, g_ref, y_ref, wd_ref, a_ref, b_ref, dyb_ref, da_ref, db_ref, dg_ref):
        i = pl.program_id(0)

        @pl.when(pl.program_id(1) == 0)
        def _():
            dx = dxo_ref[...].astype(F32)
            dyb_ref[...] = ((FFN_RESIDUAL * g_ref[0]) * dx).astype(BF16)
            part = _rowsum(FFN_RESIDUAL * dx * y_ref[...].astype(F32))
            _acc(dg_ref, part[None], i % tps == 0)

        dact = _dot_nt(dyb_ref[...], wd_ref[...])
        af = a_ref[...].astype(F32)
        bf = b_ref[...].astype(F32)
        sg = _sigmoid(af)
        silu = af * sg
        da_ref[...] = (dact * bf * (sg + silu * (1.0 - sg))).astype(BF16)
        db_ref[...] = (dact * silu).astype(BF16)

    row = pl.BlockSpec((tm, D), lambda i, j: (i, 0))
    per_b = pl.BlockSpec((1, 1, D), lambda i, j: (i // tps, 0, 0))
    wspec = pl.BlockSpec((tn, D), lambda i, j: (j, 0))
    chunk = pl.BlockSpec((tm, tn), lambda i, j: (i, j))
    blocks = [((tm, D), F32), ((tm, D), BF16), ((tn, D), BF16), ((tm, D), BF16)] + [((tm, tn), BF16)] * 4
    return _call(
        body, (dxo, g.arr, y, wd, a, b), name=name, grid=(T // tm, F // tn),
        in_specs=[row, g.spec(tps, 2), row, wspec, chunk, chunk],
        out_specs=[row, chunk, chunk, per_b],
        out_shape=[SDS((T, D), BF16)] + [SDS((T, F), BF16)] * 2 + [SDS((nb, 1, D), F32)],
        params=_params(2, blocks, temp_bytes=6 * _nbytes((tm, tn), F32)), comm=comm)


def _matmul_norm_mod_bwd(ds, ws, x, gn, sc, dxo, *, seq, tm, name, out_dtype, comm=None):
    T, D = x.shape
    nk = len(ws)
    sizes = [len(g) for g in ds]
    ds = [d for g in ds for d in g]
    tps = seq // tm
    nb = T // seq

    def body(*refs):
        w_refs = refs[len(ds):len(ds) + nk]
        x_ref, gn_ref, sc_ref, dxo_ref, dxi_ref, dsh_ref, dsc_ref, dgn_ref = refs[len(ds) + nk:]
        i = pl.program_id(0)
        dh, at = None, 0
        for n, w_ref in zip(sizes, w_refs):
            pieces = [r[...] for r in refs[at:at + n]]
            at += n
            part = _dot(pieces[0] if n == 1 else jnp.concatenate(pieces, axis=1), w_ref[...])
            dh = part if dh is None else dh + part
        gnv = gn_ref[...]
        scv = sc_ref[0]
        _, xhat, rstd, yn = _norm_mod(x_ref[...], gnv, 0.0, scv)
        dyn = dh * (1.0 + scv)
        dxhat = dyn * gnv
        dxi_ref[...] = (dxo_ref[...].astype(F32)
                        + rstd * (dxhat - xhat * jnp.mean(dxhat * xhat, axis=-1, keepdims=True))).astype(out_dtype)
        first_of_seq = i % tps == 0
        _acc(dsh_ref, _rowsum(dh)[None], first_of_seq)
        _acc(dsc_ref, _rowsum(dh * yn)[None], first_of_seq)
        _acc(dgn_ref, _rowsum(dyn * xhat), i == 0)

    row = pl.BlockSpec((tm, D), lambda i: (i, 0))
    vec = pl.BlockSpec((1, D), lambda i: (0, 0))
    per_b = pl.BlockSpec((1, 1, D), lambda i: (i // tps, 0, 0))
    d_specs = [pl.BlockSpec((tm, d.shape[1]), lambda i: (i, 0)) for d in ds]
    w_specs = [pl.BlockSpec(w.shape, lambda i: (0, 0)) for w in ws]
    blocks = ([((tm, d.shape[1]), BF16) for d in ds] + [(w.shape, BF16) for w in ws] + [((tm, D), F32)] * 3)
    return _call(
        body, (*ds, *ws, x, gn, sc.arr, dxo), name=name, grid=(T // tm,),
        in_specs=d_specs + w_specs + [row, vec, sc.spec(tps, 1), row],
        out_specs=[row, per_b, per_b, vec],
        out_shape=[SDS((T, D), out_dtype), SDS((nb, 1, D), F32), SDS((nb, 1, D), F32), SDS((1, D), F32)],
        params=_params(1, blocks, temp_bytes=6 * _nbytes((tm, D), F32)), comm=comm)


def _layernorm_silu(yc, lg, lb):
    mu = jnp.mean(yc, axis=-1, keepdims=True)
    cen = yc - mu
    rstd = lax.rsqrt(jnp.mean(cen * cen, axis=-1, keepdims=True) + EPS)
    xh = cen * rstd
    l = xh * lg + lb
    s = _sigmoid(l)
    return l * s, xh, rstd, l, s


GATE_W = 256


def _gate_specs(tm, D, col):
    return [pl.BlockSpec((tm, GATE_W), lambda i, blk=col // GATE_W + t: (i, blk)) for t in range(D // GATE_W)]


def _gate(refs):
    return jnp.concatenate([r[...] for r in refs], axis=1).astype(F32)


def _mix_out(ao, yc, proj, wao, wco, wout, x1, g2, lg, lb, *, seq, tm, ga_col, gc_col, name, comm=None):
    T, D = x1.shape
    tps = seq // tm
    ng = D // GATE_W

    def body(ao_ref, yc_ref, *rest):
        ga_refs, gc_refs = rest[:ng], rest[ng:2 * ng]
        (wao_ref, wco_ref, wout_ref, x1_ref, g2_ref, lg_ref, lb_ref,
         x2_ref, z_ref, ya_ref, ycv_ref, cact_ref, mrg_ref) = rest[2 * ng:]
        ya = _dot(ao_ref[...], wao_ref[...])
        cact = _layernorm_silu(yc_ref[...], lg_ref[...], lb_ref[...])[0].astype(BF16)
        ycv = _dot(cact, wco_ref[...])
        merged = (_sigmoid(_gate(ga_refs)) * ya + _sigmoid(_gate(gc_refs)) * ycv).astype(BF16)
        z = _dot(merged, wout_ref[...])
        x2_ref[...] = x1_ref[...] + g2_ref[0] * z
        z_ref[...] = z.astype(BF16)
        ya_ref[...] = ya.astype(BF16)
        ycv_ref[...] = ycv.astype(BF16)
        cact_ref[...] = cact
        mrg_ref[...] = merged

    row = pl.BlockSpec((tm, D), lambda i: (i, 0))
    vec = pl.BlockSpec((1, D), lambda i: (0, 0))
    wspec = pl.BlockSpec((D, D), lambda i: (0, 0))
    gates = _gate_specs(tm, D, ga_col) + _gate_specs(tm, D, gc_col)
    blocks = ([((tm, D), BF16), ((tm, D), F32), ((tm, D), BF16), ((tm, D), BF16)] + [((D, D), BF16)] * 3
              + [((tm, D), F32)] * 2 + [((tm, D), BF16)] * 5)
    return _call(
        body, (ao, yc, *[proj] * (2 * ng), wao, wco, wout, x1, g2.arr, lg, lb), name=name, grid=(T // tm,),
        in_specs=[row, row, *gates, wspec, wspec, wspec, row, g2.spec(tps, 1), vec, vec],
        out_specs=[row] * 6,
        out_shape=[SDS((T, D), F32)] + [SDS((T, D), BF16)] * 5,
        params=_params(1, blocks, temp_bytes=8 * _nbytes((tm, D), F32)), comm=comm)


def _mix_out_bwd(dx2, g2, z, wout, proj, ya, ycv, wao, wco, yc, lg, lb, *, seq, tm, ga_col, gc_col, name,
                 comm=None):
    T, D = dx2.shape
    tps = seq // tm
    nb = T // seq
    ng = D // GATE_W

    def body(dx2_ref, g2_ref, z_ref, wout_ref, *rest):
        ga_refs, gc_refs = rest[:ng], rest[ng:2 * ng]
        (ya_ref, ycv_ref, wao_ref, wco_ref, yc_ref, lg_ref, lb_ref, dz_ref, dya_ref, dycv_ref, dga_ref, dgc_ref,
         dao_ref, dyc_ref, dg2_ref, dlg_ref, dlb_ref) = rest[2 * ng:]
        i = pl.program_id(0)
        dx = dx2_ref[...].astype(F32)
        _acc(dg2_ref, _rowsum(dx * z_ref[...].astype(F32))[None], i % tps == 0)
        dzb = (g2_ref[0] * dx).astype(BF16)
        dz_ref[...] = dzb
        dmerged = _dot_nt(dzb, wout_ref[...])
        sa = _sigmoid(_gate(ga_refs))
        sc_ = _sigmoid(_gate(gc_refs))
        dya = (dmerged * sa).astype(BF16)
        dycv = (dmerged * sc_).astype(BF16)
        dya_ref[...] = dya
        dycv_ref[...] = dycv
        dga_ref[...] = (dmerged * ya_ref[...].astype(F32) * (sa * (1.0 - sa))).astype(BF16)
        dgc_ref[...] = (dmerged * ycv_ref[...].astype(F32) * (sc_ * (1.0 - sc_))).astype(BF16)
        dao_ref[...] = _dot_nt(dya, wao_ref[...]).astype(BF16)
        dcact = _dot_nt(dycv, wco_ref[...])
        lgv = lg_ref[...]
        _, xh, rstd, l, s = _layernorm_silu(yc_ref[...], lgv, lb_ref[...])
        dl = dcact * (s * (1.0 + l * (1.0 - s)))
        _acc(dlb_ref, _rowsum(dl), i == 0)
        _acc(dlg_ref, _rowsum(dl * xh), i == 0)
        dxh = dl * lgv
        dyc_ref[...] = rstd * (dxh - jnp.mean(dxh, axis=-1, keepdims=True)
                               - xh * jnp.mean(dxh * xh, axis=-1, keepdims=True))

    row = pl.BlockSpec((tm, D), lambda i: (i, 0))
    vec = pl.BlockSpec((1, D), lambda i: (0, 0))
    per_b = pl.BlockSpec((1, 1, D), lambda i: (i // tps, 0, 0))
    wspec = pl.BlockSpec((D, D), lambda i: (0, 0))
    gates = _gate_specs(tm, D, ga_col) + _gate_specs(tm, D, gc_col)
    blocks = ([((tm, D), F32)] * 3 + [((tm, D), BF16)] * 11 + [((D, D), BF16)] * 3)
    return _call(
        body, (dx2, g2.arr, z, wout, *[proj] * (2 * ng), ya, ycv, wao, wco, yc, lg, lb), name=name,
        grid=(T // tm,),
        in_specs=[row, g2.spec(tps, 1), row, wspec, *gates, row, row, wspec, wspec, row, vec, vec],
        out_specs=[row] * 7 + [per_b, vec, vec],
        out_shape=[SDS((T, D), BF16)] * 6 + [SDS((T, D), F32), SDS((nb, 1, D), F32), SDS((1, D), F32),
                                             SDS((1, D), F32)],
        params=_params(1, blocks, temp_bytes=10 * _nbytes((tm, D), F32)), comm=comm)


Q_BLOCK = 64
BAND = Q_BLOCK + ATT_BLOCK
GROUP_ROWS = GQA_GROUP * Q_BLOCK
PAIR_W = 2 * HEAD_DIM
GROUP_W = GQA_GROUP * HEAD_DIM


def _lane_lo():
    return lax.broadcasted_iota(jnp.int32, (1, PAIR_W), 1) < HEAD_DIM


def _band_bias():
    sj = lax.broadcasted_iota(jnp.int32, (BAND, GROUP_ROWS), 0)
    qi = lax.broadcasted_iota(jnp.int32, (BAND, GROUP_ROWS), 1) & (Q_BLOCK - 1)
    rel = qi + ATT_BLOCK - sj
    bias = jnp.where(jnp.logical_and(rel >= 0, rel < ATT_BLOCK), 0.0, NEG_BIG)
    return bias, lax.broadcasted_iota(jnp.int32, (BAND, 1), 0)


def _block_bias(bias0, key_index, r0):
    return bias0 + jnp.where(key_index + r0 < ATT_BLOCK, NEG_BIG, 0.0)


def _dup_heads(src_ref, dst, seq):
    x = src_ref[...]
    i = lax.broadcasted_iota(jnp.int32, (KV_WIDTH, PAIR_W), 0)
    j = lax.broadcasted_iota(jnp.int32, (KV_WIDTH, PAIR_W), 1) & (HEAD_DIM - 1)
    for g in range(N_KV_HEADS):
        sel = jnp.where(i == j + g * HEAD_DIM, 1.0, 0.0).astype(BF16)
        dst[g, pl.ds(0, ATT_BLOCK), :] = jnp.zeros((ATT_BLOCK, PAIR_W), BF16)
        dst[g, pl.ds(ATT_BLOCK, seq), :] = _dot(x, sel).astype(BF16)


def _stack_heads(blk, g, lo):
    parts = []
    for p in range(GQA_GROUP // 2):
        pair = blk[:, g * GROUP_W + p * PAIR_W:g * GROUP_W + (p + 1) * PAIR_W]
        parts += [jnp.where(lo, pair, jnp.zeros_like(pair)), jnp.where(lo, jnp.zeros_like(pair), pair)]
    return jnp.concatenate(parts, axis=0)


def _unstack_heads(full, ref, r0, g, lo):
    for p in range(GQA_GROUP // 2):
        even = full[(2 * p) * Q_BLOCK:(2 * p + 1) * Q_BLOCK, :]
        odd = full[(2 * p + 1) * Q_BLOCK:(2 * p + 2) * Q_BLOCK, :]
        ref[pl.ds(r0, Q_BLOCK), g * GROUP_W + p * PAIR_W:g * GROUP_W + (p + 1) * PAIR_W] = (
            jnp.where(lo, even, odd).astype(ref.dtype))


def _sink_row(sink_ref, g):
    return jnp.concatenate([jnp.full((1, Q_BLOCK), sink_ref[0, g * GQA_GROUP + h], F32)
                            for h in range(GQA_GROUP)], axis=1)


def _group_probs(qs, k2, bias, sink):
    s = _dot_nt(k2, qs) * (HEAD_DIM ** -0.5) + bias
    m = jnp.maximum(jnp.max(s, axis=0, keepdims=True), sink)
    p = jnp.exp(s - m)
    psink = jnp.exp(sink - m)
    inv = 1.0 / (jnp.sum(p, axis=0, keepdims=True) + psink)
    return p * inv, psink * inv


def _attn_fwd(projp, sinks, *, seq, q_blk, k_blk, v_blk, name, comm=None):
    T = projp.shape[0]
    QW = N_Q_HEADS * HEAD_DIM
    nblk = seq // Q_BLOCK

    def body(q_ref, k_ref, v_ref, sink_ref, o_ref, k2s, v2s):
        _dup_heads(k_ref, k2s, seq)
        _dup_heads(v_ref, v2s, seq)
        lo = _lane_lo()
        bias0, key_index = _band_bias()
        sink_rows = [_sink_row(sink_ref, g) for g in range(N_KV_HEADS)]

        def blk(n, carry):
            r0 = pl.multiple_of(n * Q_BLOCK, Q_BLOCK)
            band = pl.ds(r0, BAND)
            qb = q_ref[pl.ds(r0, Q_BLOCK), :]
            bias = _block_bias(bias0, key_index, r0)
            for g in range(N_KV_HEADS):
                probs_t, _ = _group_probs(_stack_heads(qb, g, lo), k2s[g, band, :], bias, sink_rows[g])
                _unstack_heads(_dot_tn(probs_t.astype(BF16), v2s[g, band, :]), o_ref, r0, g, lo)
            return carry

        lax.fori_loop(0, nblk, blk, 0, unroll=2)

    blocks = [((seq, QW), BF16)] * 2 + [((seq, KV_WIDTH), BF16)] * 2
    return _call(
        body, (projp, projp, projp, sinks), name=name, grid=(T // seq,),
        in_specs=[pl.BlockSpec((seq, QW), lambda b: (b, q_blk)),
                  pl.BlockSpec((seq, KV_WIDTH), lambda b: (b, k_blk)),
                  pl.BlockSpec((seq, KV_WIDTH), lambda b: (b, v_blk)),
                  pl.BlockSpec(memory_space=pltpu.SMEM)],
        out_specs=[pl.BlockSpec((seq, QW), lambda b: (b, 0))],
        out_shape=[SDS((T, QW), BF16)],
        scratch_shapes=[pltpu.VMEM((N_KV_HEADS, seq + ATT_BLOCK, PAIR_W), BF16)] * 2,
        params=_params(1, blocks, temp_bytes=2 * _nbytes((N_KV_HEADS, seq + ATT_BLOCK, PAIR_W), BF16)
                       + 8 * _nbytes((BAND, GROUP_ROWS), F32)), comm=comm, hbm_out=(0,))[0]


def _attn_bwd(projp, dao, sinks, *, seq, q_blk, k_blk, v_blk, name, comm=None):
    T = projp.shape[0]
    QW = N_Q_HEADS * HEAD_DIM
    assert seq % (2 * Q_BLOCK) == 0
    nblk = seq // Q_BLOCK

    def body(q_ref, k_ref, v_ref, do_ref, sink_ref, dq_ref, dk_ref, dv_ref, dsink_ref, k2s, v2s, dkacc, dvacc):
        _dup_heads(k_ref, k2s, seq)
        _dup_heads(v_ref, v2s, seq)
        dkacc[...] = jnp.zeros(dkacc.shape, F32)
        dvacc[...] = jnp.zeros(dvacc.shape, F32)
        lane = lax.broadcasted_iota(jnp.int32, (1, PAIR_W), 1)
        lo = lane < HEAD_DIM
        bias0, key_index = _band_bias()
        sink_rows = [_sink_row(sink_ref, g) for g in range(N_KV_HEADS)]

        def blk(n, tsinks):
            tsinks = list(tsinks)
            r0 = pl.multiple_of(n * Q_BLOCK, Q_BLOCK)
            band = pl.ds(r0, BAND)
            qb = q_ref[pl.ds(r0, Q_BLOCK), :]
            dob = do_ref[pl.ds(r0, Q_BLOCK), :]
            bias = _block_bias(bias0, key_index, r0)
            for g in range(N_KV_HEADS):
                qs = _stack_heads(qb, g, lo)
                dos = _stack_heads(dob, g, lo)
                k2 = k2s[g, band, :]
                v2 = v2s[g, band, :]
                probs_t, psink = _group_probs(qs, k2, bias, sink_rows[g])
                dp_t = _dot_nt(v2, dos)
                delta = jnp.sum(probs_t * dp_t, axis=0, keepdims=True)
                ds_t = (probs_t * (dp_t - delta) * (HEAD_DIM ** -0.5)).astype(BF16)
                tsinks[g] = tsinks[g] + psink * delta
                _unstack_heads(_dot_tn(ds_t, k2), dq_ref, r0, g, lo)
                dkacc[g, band, :] = dkacc[g, band, :] + _dot(ds_t, qs)
                dvacc[g, band, :] = dvacc[g, band, :] + _dot(probs_t.astype(BF16), dos)
            return tuple(tsinks)

        def two_blocks(m, tsinks):
            return blk(2 * m + 1, blk(2 * m, tsinks))

        tsinks = lax.fori_loop(0, nblk // 2, two_blocks, (jnp.zeros((1, GROUP_ROWS), F32),) * N_KV_HEADS)
        dsink = jnp.zeros((1, PAIR_W), F32)
        for g in range(N_KV_HEADS):
            for h in range(GQA_GROUP):
                dsink = dsink + jnp.where(lane == g * GQA_GROUP + h,
                                          -jnp.sum(tsinks[g][:, h * Q_BLOCK:(h + 1) * Q_BLOCK]), 0.0)
        _acc(dsink_ref, dsink, pl.program_id(0) == 0)

        def fold(acc, g):
            a = acc[g, pl.ds(ATT_BLOCK, seq), :]
            return a + pltpu.roll(a, HEAD_DIM, 1)

        dk_ref[...] = jnp.where(lo, fold(dkacc, 0), fold(dkacc, 1)).astype(BF16)
        dv_ref[...] = jnp.where(lo, fold(dvacc, 0), fold(dvacc, 1)).astype(BF16)

    blocks = [((seq, QW), BF16)] * 3 + [((seq, KV_WIDTH), BF16)] * 4
    kv_spec_out = pl.BlockSpec((seq, KV_WIDTH), lambda b: (b, 0))
    return _call(
        body, (projp, projp, projp, dao, sinks), name=name, grid=(T // seq,),
        in_specs=[pl.BlockSpec((seq, QW), lambda b: (b, q_blk)),
                  pl.BlockSpec((seq, KV_WIDTH), lambda b: (b, k_blk)),
                  pl.BlockSpec((seq, KV_WIDTH), lambda b: (b, v_blk)),
                  pl.BlockSpec((seq, QW), lambda b: (b, 0)),
                  pl.BlockSpec(memory_space=pltpu.SMEM)],
        out_specs=[pl.BlockSpec((seq, QW), lambda b: (b, 0)), kv_spec_out, kv_spec_out,
                   pl.BlockSpec((1, 128), lambda b: (0, 0))],
        out_shape=[SDS((T, QW), BF16), SDS((T, KV_WIDTH), BF16), SDS((T, KV_WIDTH), BF16), SDS((1, 128), F32)],
        scratch_shapes=[pltpu.VMEM((N_KV_HEADS, seq + ATT_BLOCK, PAIR_W), BF16)] * 2
        + [pltpu.VMEM((N_KV_HEADS, seq + ATT_BLOCK, PAIR_W), F32)] * 2,
        params=_params(1, blocks, temp_bytes=6 * _nbytes((N_KV_HEADS, seq + ATT_BLOCK, PAIR_W), BF16)
                       + 16 * _nbytes((BAND, GROUP_ROWS), F32)), comm=comm)


SUBLANES = 8


def _sublane_shifts(win):
    n = CONV_ROWS + CONV_HALO
    return [win] + [pltpu.roll(win, n - b, 0) for b in range(1, SUBLANES)]


def _window(shifted, off):
    a = off // SUBLANES * SUBLANES
    return shifted[off % SUBLANES][a:a + CONV_ROWS, :]


def _conv_fwd(projp, w, bias, *, seq, cw, a_col, b_col, name, comm=None):
    T = projp.shape[0]
    C = w.shape[1]
    nchunk = seq // CONV_ROWS

    def body(a_ref, b_ref, w_ref, bias_ref, y_ref, upad):
        upad[pl.ds(0, CONV_HALO), :] = jnp.zeros((CONV_HALO, cw), F32)
        upad[pl.ds(CONV_HALO, seq), :] = a_ref[...].astype(F32) * _sigmoid(b_ref[...].astype(F32))
        wv = w_ref[...]
        bv = bias_ref[...]

        def chunk(r, carry):
            r0 = pl.multiple_of(r * CONV_ROWS, CONV_ROWS)
            shifted = _sublane_shifts(upad[pl.ds(r0, CONV_ROWS + CONV_HALO), :])
            acc = jnp.broadcast_to(bv, (CONV_ROWS, cw))
            for k in range(CONV_WIDTH):
                acc = acc + wv[k:k + 1, :] * _window(shifted, CONV_HALO - (CONV_WIDTH - 1) + k)
            y_ref[pl.ds(r0, CONV_ROWS), :] = acc
            return carry

        lax.fori_loop(0, nchunk, chunk, 0)

    blocks = [((seq, cw), BF16)] * 2 + [((seq, cw), F32)]
    return _call(
        body, (projp, projp, w, bias), name=name, grid=(T // seq, C // cw),
        in_specs=[pl.BlockSpec((seq, cw), lambda b, c: (b, a_col // cw + c)),
                  pl.BlockSpec((seq, cw), lambda b, c: (b, b_col // cw + c)),
                  pl.BlockSpec((CONV_WIDTH, cw), lambda b, c: (0, c)),
                  pl.BlockSpec((1, cw), lambda b, c: (0, c))],
        out_specs=[pl.BlockSpec((seq, cw), lambda b, c: (b, c))],
        out_shape=[SDS((T, C), F32)],
        scratch_shapes=[pltpu.VMEM((seq + CONV_HALO, cw), F32)],
        params=_params(2, blocks, temp_bytes=6 * _nbytes((seq, cw), F32)), comm=comm, hbm_out=(0,))[0]


def _conv_bwd(dy, projp, w, *, seq, cw, a_col, b_col, name, comm=None):
    T = projp.shape[0]
    C = w.shape[1]
    nchunk = seq // CONV_ROWS
    SUB = 8

    def body(dy_ref, a_ref, b_ref, w_ref, da_ref, db_ref, dw_ref, dbias_ref, dypad, dwp):
        first = pl.program_id(1) == 0
        dyv = dy_ref[...]
        dypad[pl.ds(0, seq), :] = dyv
        dypad[pl.ds(seq, CONV_HALO), :] = jnp.zeros((CONV_HALO, cw), F32)
        dwp[...] = jnp.zeros(dwp.shape, F32)
        wv = w_ref[...]

        def chunk(r, carry):
            r0 = pl.multiple_of(r * CONV_ROWS, CONV_ROWS)
            dy_shifts = _sublane_shifts(dypad[pl.ds(r0, CONV_ROWS + CONV_HALO), :])
            ac = a_ref[pl.ds(r0, CONV_ROWS), :].astype(F32)
            sbc = _sigmoid(b_ref[pl.ds(r0, CONV_ROWS), :].astype(F32))
            uc = ac * sbc
            du = jnp.zeros((CONV_ROWS, cw), F32)
            for k in range(CONV_WIDTH):
                dyk = _window(dy_shifts, CONV_WIDTH - 1 - k)
                du = du + wv[k:k + 1, :] * dyk
                prod = uc * dyk
                part = prod[0:SUB, :]
                for s in range(1, CONV_ROWS // SUB):
                    part = part + prod[s * SUB:(s + 1) * SUB, :]
                dwp[pl.ds(k * SUB, SUB), :] = dwp[pl.ds(k * SUB, SUB), :] + part
            da_ref[pl.ds(r0, CONV_ROWS), :] = (du * sbc).astype(BF16)
            db_ref[pl.ds(r0, CONV_ROWS), :] = (du * ac * (sbc * (1.0 - sbc))).astype(BF16)
            return carry

        lax.fori_loop(0, nchunk, chunk, 0)

        @pl.when(first)
        def _():
            dw_ref[...] = jnp.zeros(dw_ref.shape, F32)
            dbias_ref[...] = jnp.zeros(dbias_ref.shape, F32)

        for k in range(CONV_WIDTH):
            dw_ref[k:k + 1, :] = dw_ref[k:k + 1, :] + _rowsum(dwp[pl.ds(k * SUB, SUB), :])
        dbias_ref[...] = dbias_ref[...] + _rowsum(dyv)

    blocks = [((seq, cw), F32)] + [((seq, cw), BF16)] * 4
    return _call(
        body, (dy, projp, projp, w), name=name, grid=(C // cw, T // seq),
        in_specs=[pl.BlockSpec((seq, cw), lambda c, b: (b, c)),
                  pl.BlockSpec((seq, cw), lambda c, b: (b, a_col // cw + c)),
                  pl.BlockSpec((seq, cw), lambda c, b: (b, b_col // cw + c)),
                  pl.BlockSpec((CONV_WIDTH, cw), lambda c, b: (0, c))],
        out_specs=[pl.BlockSpec((seq, cw), lambda c, b: (b, c)), pl.BlockSpec((seq, cw), lambda c, b: (b, c)),
                   pl.BlockSpec((CONV_WIDTH, cw), lambda c, b: (0, c)), pl.BlockSpec((1, cw), lambda c, b: (0, c))],
        out_shape=[SDS((T, C), BF16), SDS((T, C), BF16), SDS((CONV_WIDTH, C), F32), SDS((1, C), F32)],
        scratch_shapes=[pltpu.VMEM((seq + CONV_HALO, cw), F32), pltpu.VMEM((CONV_WIDTH * SUB, cw), F32)],
        params=_params(2, blocks, temp_bytes=8 * _nbytes((seq, cw), F32)), comm=comm)


def _matmul_tn(a, b, *, name, gate=None, comm=None):
    T, M = a.shape
    N = b.shape[1]
    bm = _pick(M, (768, 512, 256))
    lhs = [a] if gate is None else [a, gate]

    def body(*refs):
        b_ref, o_ref = refs[len(lhs)], refs[len(lhs) + 1]
        av = refs[0][...]
        if gate is not None:
            af = av.astype(F32)
            av = (af * _sigmoid(af) * refs[1][...].astype(F32)).astype(BF16)
        o_ref[...] = _dot_tn(av, b_ref[...]).astype(BF16)

    blocks = [((T, bm), BF16)] * len(lhs) + [((T, N), BF16), ((bm, N), BF16)]
    return _call(
        body, (*lhs, b), name=name, grid=(M // bm,),
        in_specs=[pl.BlockSpec((T, bm), lambda i: (0, i))] * len(lhs) + [pl.BlockSpec((T, N), lambda i: (0, 0))],
        out_specs=[pl.BlockSpec((bm, N), lambda i: (i, 0))],
        out_shape=[SDS((M, N), BF16)],
        params=_params(1, blocks, temp_bytes=(2 + 4 * len(lhs)) * _nbytes((T, bm), BF16) + 2 * _nbytes((bm, N), F32)),
        comm=comm)[0]


TN_BLOCK = 256


def _matmul_tn_pieces(groups, b, *, name, comm=None):
    T, N = b.shape
    flat = [a for g in groups for a in g]
    starts, n_steps = [], 0
    for g in groups:
        width = sum(a.shape[1] for a in g)
        assert width % TN_BLOCK == 0 and (len(g) == 1 or width == TN_BLOCK), [a.shape for a in g]
        starts.append(n_steps)
        n_steps += width // TN_BLOCK

    def body(*refs):
        a_refs, b_ref, o_ref = refs[:len(flat)], refs[len(flat)], refs[len(flat) + 1]
        i = pl.program_id(0)
        at = 0
        for g, start in zip(groups, starts):
            mine = a_refs[at:at + len(g)]
            at += len(g)
            steps = sum(a.shape[1] for a in g) // TN_BLOCK

            @pl.when(jnp.logical_and(i >= start, i < start + steps))
            def _(mine=mine):
                a = mine[0][...] if len(mine) == 1 else jnp.concatenate([r[...] for r in mine], axis=1)
                o_ref[...] = _dot_tn(a, b_ref[...]).astype(BF16)

    a_specs = []
    for g, start in zip(groups, starts):
        for a in g:
            if len(g) == 1:
                last = a.shape[1] // TN_BLOCK - 1
                a_specs.append(pl.BlockSpec(
                    (T, TN_BLOCK), lambda i, start=start, last=last: (0, jnp.clip(i - start, 0, last))))
            else:
                a_specs.append(pl.BlockSpec((T, a.shape[1]), lambda i: (0, 0)))
    blocks = [((T, TN_BLOCK), BF16)] * len(flat) + [((T, N), BF16), ((TN_BLOCK, N), BF16)]
    return _call(
        body, (*flat, b), name=name, grid=(n_steps,),
        in_specs=a_specs + [pl.BlockSpec((T, N), lambda i: (0, 0))],
        out_specs=[pl.BlockSpec((TN_BLOCK, N), lambda i: (i, 0))],
        out_shape=[SDS((n_steps * TN_BLOCK, N), BF16)],
        params=_params(1, blocks, temp_bytes=2 * _nbytes((T, TN_BLOCK), BF16) + 2 * _nbytes((TN_BLOCK, N), F32)),
        comm=comm)[0]


def _sum_parts(p_ref):
    g = p_ref[0].astype(F32)
    for s in range(1, p_ref.shape[0]):
        g = g + p_ref[s].astype(F32)
    return g


def _pair_add(g, staged, *, name):
    _, R, W = g.shape
    nq = staged.shape[0]
    tr = _row_tile(R)

    def body(g_ref, s_ref, o_ref):
        mine = jnp.where(lax.axis_index("c") == 0, g_ref[0, 0].astype(F32), g_ref[0, 1].astype(F32))
        o_ref[0] = (mine + s_ref[0].astype(F32)).astype(o_ref.dtype)

    return _call(
        body, (g.reshape(nq, 2, R, W), staged), name=name, grid=(nq, R // tr),
        in_specs=[pl.BlockSpec((1, 2, tr, W), lambda q, i: (q, 0, i, 0)),
                  pl.BlockSpec((1, tr, W), lambda q, i: (q, i, 0))],
        out_specs=[pl.BlockSpec((1, tr, W), lambda q, i: (q, i, 0))],
        out_shape=[SDS((nq, R, W), g.dtype)],
        params=_params(2, [((4, tr, W), g.dtype)], temp_bytes=3 * _nbytes((tr, W), F32)))[0]


def _adamw_update(w, g, m, v):
    m = ADAM_B1 * m + (1.0 - ADAM_B1) * g
    v = ADAM_B2 * v + (1.0 - ADAM_B2) * (g * g)
    m_hat = m / (1.0 - ADAM_B1 ** ADAM_STEP)
    v_hat = v / (1.0 - ADAM_B2 ** ADAM_STEP)
    delta = -ADAM_LR * (m_hat / (jnp.sqrt(v_hat) + ADAM_EPS) + ADAM_WD * w)
    return delta, m, v


def _row_tile(R):
    return _pick(R, (256, 128, 112, 88, 64, 32, 16, 8))


def _sum8(parts, *, name):
    n, R, W = parts.shape
    tr = _row_tile(R)

    def body(p_ref, o_ref):
        o_ref[...] = _sum_parts(p_ref)

    return _call(
        body, (parts,), name=name, grid=(R // tr,),
        in_specs=[pl.BlockSpec((n, tr, W), lambda i: (0, i, 0))],
        out_specs=[pl.BlockSpec((tr, W), lambda i: (i, 0))],
        out_shape=[SDS((R, W), F32)],
        params=_params(1, [((n, tr, W), parts.dtype), ((tr, W), F32)]))[0]


def _adamw(g, w, m, v, *, name):
    R, W = w.shape
    tr = _row_tile(R)

    def body(g_ref, w_ref, m_ref, v_ref, d_ref, mo_ref, vo_ref):
        d_ref[...], mo_ref[...], vo_ref[...] = _adamw_update(w_ref[...], g_ref[...], m_ref[...], v_ref[...])

    spec = pl.BlockSpec((tr, W), lambda i: (i, 0))
    return _call(
        body, (g, w, m, v), name=name, grid=(R // tr,),
        in_specs=[spec] * 4, out_specs=[spec] * 3, out_shape=[SDS((R, W), F32)] * 3,
        params=_params(1, [((tr, W), F32)] * 7))


def _sum8_adamw(parts, w, m, v, *, name):
    R, W = w.shape
    n = parts.shape[0]
    tr = _row_tile(R)

    def body(p_ref, w_ref, m_ref, v_ref, g_ref, d_ref, mo_ref, vo_ref):
        g = _sum_parts(p_ref)
        g_ref[...] = g
        d_ref[...], mo_ref[...], vo_ref[...] = _adamw_update(w_ref[...], g, m_ref[...], v_ref[...])

    spec = pl.BlockSpec((tr, W), lambda i: (i, 0))
    return _call(
        body, (parts, w, m, v), name=name, grid=(R // tr,),
        in_specs=[pl.BlockSpec((n, tr, W), lambda i: (0, i, 0))] + [spec] * 3,
        out_specs=[spec] * 4, out_shape=[SDS((R, W), F32)] * 4,
        params=_params(1, [((n, tr, W), parts.dtype)] + [((tr, W), F32)] * 7))


def _ada_fwd(c_all, w, bias, *, name):
    NB, D = c_all.shape
    N = w.shape[1]

    def body(c_ref, w_ref, b_ref, o_ref):
        cv = c_ref[...]
        ca = (cv * _sigmoid(cv)).astype(BF16)
        o_ref[...] = _dot(ca, w_ref[...].astype(BF16)) + b_ref[...]

    full = lambda s: pl.BlockSpec(s, lambda i: (0,) * len(s))
    return _call(
        body, (c_all, w, bias), name=name, grid=(1,),
        in_specs=[full((NB, D)), full((D, N)), full((1, N))], out_specs=[full((NB, N))],
        out_shape=[SDS((NB, N), F32)],
        params=_params(1, [((D, N), F32)], temp_bytes=_nbytes((D, N), BF16)))[0]


def _ada_bwd(c_all, gmod_all, *, n_col, name):
    NB, D = c_all.shape
    N = gmod_all.shape[1]

    def body(c_ref, g_ref, gw_ref, gb_ref):
        cv = c_ref[...]
        ca = (cv * _sigmoid(cv)).astype(BF16)
        first = pl.multiple_of(_lin(_my_pos()) * n_col, 128)
        gw_ref[...] = _dot_tn(ca, g_ref[:, pl.ds(first, n_col)].astype(BF16))
        gb_ref[...] = _rowsum(g_ref[...])

    full = lambda s: pl.BlockSpec(s, lambda i: (0,) * len(s))
    return _call(
        body, (c_all, gmod_all), name=name, grid=(1,),
        in_specs=[full((NB, D)), full((NB, N))], out_specs=[full((D, n_col)), full((1, N))],
        out_shape=[SDS((D, n_col), F32), SDS((1, N), F32)],
        params=_params(1, [((D, n_col), F32), ((NB, N), F32)]))


def kernel(x, c, w_ada, b_ada, norm_ffn1_g, ffn1_w_gate, ffn1_w_up, ffn1_w_down, norm_mix_g, w_in, attn_sinks, w_attn_o, conv_w_dw, conv_b_dw, conv_ln_g, conv_ln_b, w_conv_o, w_out, norm_ffn2_g, ffn2_w_gate, ffn2_w_up, ffn2_w_down, final_norm_g, loss_target, m_w_ada, m_b_ada, m_norm_ffn1_g, m_ffn1_w_gate, m_ffn1_w_up, m_ffn1_w_down, m_norm_mix_g, m_w_in, m_attn_sinks, m_w_attn_o, m_conv_w_dw, m_conv_b_dw, m_conv_ln_g, m_conv_ln_b, m_w_conv_o, m_w_out, m_norm_ffn2_g, m_ffn2_w_gate, m_ffn2_w_up, m_ffn2_w_down, m_final_norm_g, v_w_ada, v_b_ada, v_norm_ffn1_g, v_ffn1_w_gate, v_ffn1_w_up, v_ffn1_w_down, v_norm_mix_g, v_w_in, v_attn_sinks, v_w_attn_o, v_conv_w_dw, v_conv_b_dw, v_conv_ln_g, v_conv_ln_b, v_w_conv_o, v_w_out, v_norm_ffn2_g, v_ffn2_w_gate, v_ffn2_w_up, v_ffn2_w_down, v_final_norm_g):
    B, S, D = x.shape
    T = B * S
    QW = N_Q_HEADS * HEAD_DIM
    CC = conv_w_dw.shape[2] * N_DEV
    me = _lin(_my_pos())
    xf = x.reshape(T, D)
    tgt = loss_target.reshape(T, D)
    tm = min(512, S)
    kw = dict(seq=S, tm=tm)

    p_k, p_v, p_ca = QW, QW + KV_WIDTH, QW + 2 * KV_WIDTH
    p_cb, p_ga, p_gc = p_ca + CC, p_ca + 2 * CC, p_ca + 2 * CC + D

    def col_t(w):
        return w[0].T.astype(BF16)

    def row_b(w):
        return w[0].astype(BF16)

    def rows(g):
        return g.reshape(-1, g.shape[-1])

    def blocks8(g):
        return g.reshape(N_DEV, g.shape[0] // N_DEV, g.shape[1])

    def gather(*arrs):
        return _Comm([(a, "gather") for a in arrs])

    g_wg1, g_convw, g_c = _exchange(
        [(col_t(ffn1_w_gate), "gather"), (conv_w_dw[0], "gather"), (c, "gather")], name="gather_first")
    wg1 = rows(g_wg1)
    conv_w = g_convw.transpose(1, 0, 2).reshape(CONV_WIDTH, CC)
    c_all = g_c.reshape(N_DEV * B, D)

    n_col = N_MOD * D // N_DEV
    b_cols = lax.dynamic_slice(b_ada, (0, me * n_col), (1, n_col))
    mod_cols = _ada_fwd(c_all, w_ada[0], b_cols, name="ada_fwd")
    mod_mine = _exchange([(mod_cols.reshape(N_DEV, B, n_col), "scatter")], name="scatter_mod")[0]
    mod = mod_mine.transpose(1, 0, 2).reshape(B * N_MOD, 1, D)
    sh1, sc1, g1, sh2, sc2, g2, sh3, sc3, g3 = [_ModVec(mod, i) for i in range(N_MOD)]

    F = wg1.shape[0]
    tn_f = _pick(F, (1408, 1024, 512, 256))
    tn_in = _pick(w_in.shape[2] * N_DEV, (1792, 768, 512, 256))
    gate_blk = dict(ga_col=p_ga, gc_col=p_gc)
    att_blk = dict(q_blk=0, k_blk=p_k // KV_WIDTH, v_blk=p_v // KV_WIDTH)
    conv_kw = dict(seq=S, cw=256, a_col=p_ca, b_col=p_cb)

    cm = gather(col_t(ffn1_w_up))
    h1, (a1,) = _norm_mod_matmul(xf, norm_ffn1_g, sh1, sc1, [wg1], tn=tn_f, name="ffn1_gate", comm=cm, **kw)
    wu1 = rows(cm.out[0])
    cm = gather(row_b(ffn1_w_down))
    b1 = _matmul_nt(h1, wu1, tm=tm, tn=tn_f, name="ffn1_up", comm=cm)
    wd1 = rows(cm.out[0])
    cm = gather(col_t(w_in))
    x1, y1 = _ffn_down(a1, b1, wd1, xf, g1, name="ffn1_down", comm=cm, **kw)
    winp = rows(cm.out[0])
    cm = gather(row_b(w_attn_o), row_b(w_conv_o), row_b(w_out), col_t(ffn2_w_gate))
    h2, (projp,) = _norm_mod_matmul(x1, norm_mix_g, sh2, sc2, [winp], tn=tn_in, name="mix_in", comm=cm, **kw)
    wao, wco, wout, wg2 = [rows(o) for o in cm.out]
    cm = gather(col_t(ffn2_w_up))
    ao = _in_hbm(_attn_fwd(projp, attn_sinks, seq=S, name="attn_fwd", comm=cm, **att_blk))
    wu2 = rows(cm.out[0])
    cm = gather(row_b(ffn2_w_down))
    yc = _in_hbm(_conv_fwd(projp, conv_w, conv_b_dw, name="conv_fwd", comm=cm, **conv_kw))
    wd2 = rows(cm.out[0])
    x2, z, ya, ycv, cact, merged = _mix_out(ao, yc, projp, wao, wco, wout, x1, g2, conv_ln_g, conv_ln_b,
                                            name="mix_out", **gate_blk, **kw)
    h3, (a3, b3) = _norm_mod_matmul(x2, norm_ffn2_g, sh3, sc3, [wg2, wu2], tn=tn_f, name="ffn2_up", **kw)
    x3, y3 = _ffn_down(a3, b3, wd2, x2, g3, name="ffn2_down", **kw)
    dx3, loss_row, dgf = _final_loss(_in_hbm(x3), final_norm_g[None], _in_hbm(tgt), tm=tm, name="final_loss")
    dx3 = _in_hbm(dx3)

    parts = {}

    def pair(*gs):
        return [(blocks8(g), "pair") for g in gs]

    def cross(*rs):
        return [(r, "cross") for r in rs]

    def reduce_pairs(gs, staged, names):
        return [_pair_add(blocks8(g), s, name="pair_add_" + n) for g, s, n in zip(gs, staged, names)]

    dyb3, da3, db3, dg3 = _ffn_bwd_down(dx3, g3, y3, wd2, a3, b3, tn=tn_f, name="ffn2_bwd_down", **kw)
    gwd2 = _matmul_tn(a3, dyb3, gate=b3, name="gw_ffn2_down")
    cm = _Comm(pair(gwd2))
    dx2, dsh3, dsc3, dgn3 = _matmul_norm_mod_bwd([[da3], [db3]], [wg2, wu2], x2, norm_ffn2_g, sc3, dx3,
                                                 name="ffn2_bwd_up", out_dtype=GRAD_STREAM, comm=cm, **kw)
    r_wd2, = reduce_pairs([gwd2], cm.out, ["ffn2_w_down"])
    cm = _Comm(cross(r_wd2))
    gwg2 = _matmul_tn(da3, h3, name="gw_ffn2_gate", comm=cm)
    parts["ffn2_w_down"], = cm.out
    cm = _Comm(pair(gwg2))
    gwu2 = _matmul_tn(db3, h3, name="gw_ffn2_up", comm=cm)
    r_wg2, = reduce_pairs([gwg2], cm.out, ["ffn2_w_gate"])

    cm = _Comm(cross(r_wg2) + pair(gwu2))
    dzb, dyab, dycb, dga, dgc, dao, dyc, dg2, dlng, dlnb = _mix_out_bwd(
        dx2, g2, z, wout, projp, ya, ycv, wao, wco, yc, conv_ln_g, conv_ln_b, name="mix_out_bwd", comm=cm,
        **gate_blk, **kw)
    parts["ffn2_w_gate"] = cm.out[0]
    r_wu2, = reduce_pairs([gwu2], cm.out[1:], ["ffn2_w_up"])
    gwout = _matmul_tn(merged, dzb, name="gw_out")
    gwao = _matmul_tn(ao, dyab, name="gw_attn_o")
    gwco = _matmul_tn(cact, dycb, name="gw_conv_o")
    cm = _Comm(cross(r_wu2) + pair(gwout, gwao, gwco))
    dq, dk, dv, dsinks = _attn_bwd(projp, dao, attn_sinks, seq=S, name="attn_bwd", comm=cm, **att_blk)
    parts["ffn2_w_up"] = cm.out[0]
    r_mix = reduce_pairs([gwout, gwao, gwco], cm.out[1:], ["w_out", "w_attn_o", "w_conv_o"])
    cm = _Comm(cross(*r_mix))
    dca, dcb, dconvw, dconvb = _conv_bwd(dyc, projp, conv_w, name="conv_bwd", comm=cm, **conv_kw)
    parts["w_out"], parts["w_attn_o"], parts["w_conv_o"] = cm.out
    gwin = _matmul_tn_pieces([[dq], [dk, dv], [dca], [dcb], [dga], [dgc]], h2, name="gw_in")
    cm = _Comm(pair(gwin))
    dx1, dsh2, dsc2, dgn2 = _matmul_norm_mod_bwd([[dq, dk, dv, dca, dcb, dga, dgc]], [winp], x1, norm_mix_g, sc2, dx2,
                                                 name="mix_in_bwd", out_dtype=GRAD_STREAM, comm=cm, **kw)
    r_win, = reduce_pairs([gwin], cm.out, ["w_in"])

    cm = _Comm(cross(r_win))
    dyb1, da1, db1, dg1 = _ffn_bwd_down(dx1, g1, y1, wd1, a1, b1, tn=tn_f, name="ffn1_bwd_down", comm=cm,
                                              **kw)
    parts["w_in"], = cm.out
    gwd1 = _matmul_tn(a1, dyb1, gate=b1, name="gw_ffn1_down")
    cm = _Comm(pair(gwd1))
    gwg1 = _matmul_tn(da1, h1, name="gw_ffn1_gate", comm=cm)
    r_wd1, = reduce_pairs([gwd1], cm.out, ["ffn1_w_down"])
    cm = _Comm(cross(r_wd1) + pair(gwg1))
    gwu1 = _matmul_tn(db1, h1, name="gw_ffn1_up", comm=cm)
    parts["ffn1_w_down"] = cm.out[0]
    r_wg1, = reduce_pairs([gwg1], cm.out[1:], ["ffn1_w_gate"])
    r_wu1, = reduce_pairs([gwu1], _exchange(pair(gwu1), name="pair_last"), ["ffn1_w_up"])
    cm = _Comm(cross(r_wg1, r_wu1))
    dx0, dsh1, dsc1, dgn1 = _matmul_norm_mod_bwd([[da1], [db1]], [wg1, wu1], xf, norm_ffn1_g, sc1, dx1,
                                                 name="ffn1_bwd_up", out_dtype=F32, comm=cm, **kw)
    parts["ffn1_w_gate"], parts["ffn1_w_up"] = cm.out

    n_small = 8
    gmod = jnp.concatenate([dsh1, dsc1, dg1, dsh2, dsc2, dg2, dsh3, dsc3, dg3], axis=1).reshape(B, N_MOD * D)
    sink_row = jnp.pad(dsinks[:, :N_Q_HEADS], ((0, 0), (0, D - N_Q_HEADS)))
    loss_pad = jnp.pad(loss_row, ((0, 0), (0, D - loss_row.shape[1])))
    small = jnp.concatenate([dgn1, dgn2, dgn3, dgf, dconvb, dlng, dlnb, sink_row, dconvw, loss_pad], axis=0)
    small_all, gmod_all = _exchange([(small, "gather"), (gmod, "gather")], name="exchange_last")
    gsmall = _sum8(small_all, name="sum_small")
    loss = gsmall[n_small + CONV_WIDTH, 0]
    g_w_ada, g_b_ada = _ada_bwd(c_all, gmod_all.reshape(N_DEV * B, N_MOD * D), n_col=n_col, name="ada_bwd")
    g_conv_w = lax.dynamic_slice(gsmall[n_small:n_small + CONV_WIDTH], (0, me * (CC // N_DEV)),
                                 (CONV_WIDTH, CC // N_DEV))

    def col_update(name, w, m, v):
        outs = _sum8_adamw(parts[name], w[0].T, m[0].T, v[0].T, name="adamw_" + name)
        return tuple(o.T for o in outs)

    def row_update(name, w, m, v):
        return tuple(_sum8_adamw(parts[name], w[0], m[0], v[0], name="adamw_" + name))

    upd = {
        "ffn1_w_gate": col_update("ffn1_w_gate", ffn1_w_gate, m_ffn1_w_gate, v_ffn1_w_gate),
        "ffn1_w_up": col_update("ffn1_w_up", ffn1_w_up, m_ffn1_w_up, v_ffn1_w_up),
        "ffn1_w_down": row_update("ffn1_w_down", ffn1_w_down, m_ffn1_w_down, v_ffn1_w_down),
        "w_in": col_update("w_in", w_in, m_w_in, v_w_in),
        "w_attn_o": row_update("w_attn_o", w_attn_o, m_w_attn_o, v_w_attn_o),
        "w_conv_o": row_update("w_conv_o", w_conv_o, m_w_conv_o, v_w_conv_o),
        "w_out": row_update("w_out", w_out, m_w_out, v_w_out),
        "ffn2_w_gate": col_update("ffn2_w_gate", ffn2_w_gate, m_ffn2_w_gate, v_ffn2_w_gate),
        "ffn2_w_up": col_update("ffn2_w_up", ffn2_w_up, m_ffn2_w_up, v_ffn2_w_up),
        "ffn2_w_down": row_update("ffn2_w_down", ffn2_w_down, m_ffn2_w_down, v_ffn2_w_down),
        "w_ada": (g_w_ada,) + tuple(_adamw(g_w_ada, w_ada[0], m_w_ada[0], v_w_ada[0], name="adamw_w_ada")),
        "conv_w_dw": (g_conv_w,) + tuple(_adamw(g_conv_w, conv_w_dw[0], m_conv_w_dw[0], v_conv_w_dw[0],
                                                name="adamw_conv_w_dw")),
    }
    for k in upd:
        upd[k] = tuple(t[None] for t in upd[k])

    def pad_sinks(t):
        return jnp.pad(t, ((0, 0), (0, D - N_Q_HEADS)))

    def pack(f1, mix, f2, fin, cb, lg, lb, sinks, bada):
        return jnp.concatenate([f1, mix, f2, fin[None], cb, lg, lb, pad_sinks(sinks), bada.reshape(N_MOD, D)], axis=0)

    w_s = pack(norm_ffn1_g, norm_mix_g, norm_ffn2_g, final_norm_g, conv_b_dw, conv_ln_g, conv_ln_b, attn_sinks, b_ada)
    m_s = pack(m_norm_ffn1_g, m_norm_mix_g, m_norm_ffn2_g, m_final_norm_g, m_conv_b_dw, m_conv_ln_g, m_conv_ln_b,
               m_attn_sinks, m_b_ada)
    v_s = pack(v_norm_ffn1_g, v_norm_mix_g, v_norm_ffn2_g, v_final_norm_g, v_conv_b_dw, v_conv_ln_g, v_conv_ln_b,
               v_attn_sinks, v_b_ada)
    g_s = jnp.concatenate([gsmall[:n_small], g_b_ada.reshape(N_MOD, D)], axis=0)
    small_out = (g_s,) + tuple(_adamw(g_s, w_s, m_s, v_s, name="adamw_vectors"))

    def unpack(t):
        return {
            "norm_ffn1_g": t[0:1], "norm_mix_g": t[1:2], "norm_ffn2_g": t[2:3], "final_norm_g": t[3],
            "conv_b_dw": t[4:5], "conv_ln_g": t[5:6], "conv_ln_b": t[6:7], "attn_sinks": t[7:8, :N_Q_HEADS],
            "b_ada": t[n_small:n_small + N_MOD].reshape(1, N_MOD * D),
        }

    small_un = [unpack(t) for t in small_out]
    for k in small_un[0]:
        upd[k] = tuple(s[k] for s in small_un)

    order = ["w_ada", "b_ada", "norm_ffn1_g", "ffn1_w_gate", "ffn1_w_up", "ffn1_w_down", "norm_mix_g", "w_in",
             "attn_sinks", "w_attn_o", "conv_w_dw", "conv_b_dw", "conv_ln_g", "conv_ln_b", "w_conv_o", "w_out",
             "norm_ffn2_g", "ffn2_w_gate", "ffn2_w_up", "ffn2_w_down", "final_norm_g"]
    grad_x = dx0.reshape(B, S, D)
    return (loss, grad_x, *[upd[k][0] for k in order], *[upd[k][1] for k in order],
            *[upd[k][2] for k in order], *[upd[k][3] for k in order])
```

```python
import dataclasses

import jax
import jax.numpy as jnp
from jax import lax
from jax.experimental import pallas as pl
from jax.experimental.pallas import tpu as pltpu

F32 = jnp.float32
BF16 = jnp.bfloat16
SDS = jax.ShapeDtypeStruct
MESH = pl.DeviceIdType.MESH

N_DEV = 8
EPS = 1e-6
HEAD_DIM = 64
N_Q_HEADS = 16
N_KV_HEADS = 2
GQA_GROUP = N_Q_HEADS // N_KV_HEADS
KV_WIDTH = N_KV_HEADS * HEAD_DIM
ATT_BLOCK = 128
CONV_WIDTH = 31
CONV_HALO = 32
CONV_ROWS = 128
N_MOD = 9
FFN_RESIDUAL = 0.5
ADAM_LR = 0.001
ADAM_B1 = 0.9
ADAM_B2 = 0.999
ADAM_EPS = 1e-08
ADAM_WD = 0.01
ADAM_STEP = 10
NEG_BIG = -1e30
GRAD_STREAM = BF16

V7X_VMEM_BYTES = 64 * 2**20
VMEM_CAP = V7X_VMEM_BYTES - 8 * 2**20


def _nbytes(shape, dtype):
    n = 1
    for s in shape:
        n *= s
    return n * jnp.dtype(dtype).itemsize


def _params(n_axes, blocks, temp_bytes=0):
    need = 2 * sum(_nbytes(s, d) for s, d in blocks) + temp_bytes + 4 * 2**20
    return pltpu.CompilerParams(dimension_semantics=("arbitrary",) * n_axes,
                                vmem_limit_bytes=int(min(max(need, 16 * 2**20), VMEM_CAP)))


def _dot_nt(a, b):
    return lax.dot_general(a, b, (((1,), (1,)), ((), ())), preferred_element_type=F32)


def _dot_tn(a, b):
    return lax.dot_general(a, b, (((0,), (0,)), ((), ())), preferred_element_type=F32)


def _dot(a, b):
    return jnp.dot(a, b, preferred_element_type=F32)


def _sigmoid(x):
    return jax.nn.sigmoid(x)


def _rowsum(v):
    return jnp.sum(v, axis=0, keepdims=True)


def _acc(ref, val, first):
    @pl.when(first)
    def _():
        ref[...] = val

    @pl.when(jnp.logical_not(first))
    def _():
        ref[...] = ref[...] + val


def _norm_mod(xf, gn, sh, sc):
    rstd = lax.rsqrt(jnp.mean(xf * xf, axis=-1, keepdims=True) + EPS)
    xhat = xf * rstd
    yn = xhat * gn
    return yn * (1.0 + sc) + sh, xhat, rstd, yn


def _pick(n, cands):
    for c in cands:
        if n % c == 0:
            return c
    return n


def _my_pos():
    return lax.axis_index("x"), lax.axis_index("y"), lax.axis_index("c")


def _peer(pos, k):
    x, y, c = pos
    return ((1 - x) if k & 4 else x, (1 - y) if k & 2 else y, (1 - c) if k & 1 else c)


def _lin(pos):
    return 4 * pos[0] + 2 * pos[1] + pos[2]


def _in_hbm(a):
    return pltpu.with_memory_space_constraint(a, pltpu.HBM)


class _Comm:
    N_COPY = N_DEV - 1
    N_CHIP = N_DEV // 2

    def __init__(self, items, hbm_out=False):
        self.hbm_out = hbm_out
        self.arrs = [a for a, _ in items]
        self.modes = [m for _, m in items]
        self.n = len(items)
        self.out = None

    def out_shape(self):
        def shape(a, m):
            return {"gather": (N_DEV,) + a.shape, "scatter": a.shape, "pair": (self.N_CHIP,) + a.shape[1:],
                    "cross": a.shape}[m]
        kind = pltpu.HBM if self.hbm_out else SDS
        return [kind(shape(a, m), a.dtype) for a, m in zip(self.arrs, self.modes)]

    def scratch(self):
        return [pltpu.SemaphoreType.DMA((self.n * self.N_COPY,)), pltpu.SemaphoreType.DMA((self.n * self.N_COPY,)),
                pltpu.SemaphoreType.DMA((self.n,))]

    def collective_id(self):
        modes = set(self.modes)
        if "scatter" in modes:
            return 3
        d2d, ici = bool(modes & {"gather", "pair"}), bool(modes & {"gather", "cross"})
        return {(True, False): 0, (False, True): 1, (True, True): 2}[(d2d, ici)]

    def barrier(self):
        x, y, c = _my_pos()
        peers = {0: [(x, y, 1 - c)],
                 1: [(1 - x, y, c), (x, 1 - y, c), (1 - x, 1 - y, c)],
                 2: [(x, y, 1 - c), (1 - x, y, c), (x, 1 - y, c), (1 - x, 1 - y, c)],
                 3: [_peer((x, y, c), k) for k in range(1, N_DEV)]}[self.collective_id()]
        sem = pltpu.get_barrier_semaphore()
        for p in peers:
            pl.semaphore_signal(sem, inc=1, device_id=p, device_id_type=MESH)
        pl.semaphore_wait(sem, len(peers))

    def _plan(self, mode, me):
        x, y, c = me
        sib = (x, y, 1 - c)
        chips = [(1 - x, y), (x, 1 - y), (1 - x, 1 - y)]

        def chip_lin(ch):
            return 2 * ch[0] + ch[1]

        if mode == "scatter":
            peers = [_peer(me, k + 1) for k in range(self.N_COPY)]
            return [(p, ("in", _lin(p)), _lin(me), _lin(p), None) for p in peers], (_lin(me), _lin(me))
        if mode == "gather":
            same = [(*ch, c) for ch in chips]
            other = [(*ch, 1 - c) for ch in chips]
            copies = [(sib, ("in", None), _lin(me), _lin(sib), None)]
            copies += [(p, ("in", None), _lin(me), _lin(p), None) for p in same]
            copies += [(sib, ("out", _lin(p)), _lin(p), _lin(o), 1 + j) for j, (p, o) in enumerate(zip(same, other))]
            return copies, (None, _lin(me))
        if mode == "pair":
            return [(sib, ("in", 2 * q + 1 - c), q, q, None) for q in range(self.N_CHIP)], None
        if mode == "cross":
            mine = chip_lin((x, y))
            return ([((*ch, c), ("in", chip_lin(ch)), mine, chip_lin(ch), None) for ch in chips], (mine, mine))
        raise ValueError(mode)

    def _copy(self, refs, me, i, k, recv):
        srcs, outs, (send_sems, recv_sems, _) = refs
        peer, (where, slot), send_slot, recv_slot, _ = self._plan(self.modes[i], me)[0][k]
        src = srcs[i] if where == "in" else outs[i]
        src = src if slot is None else src.at[slot]
        sem = i * self.N_COPY + k
        return pltpu.make_async_remote_copy(
            src_ref=src, dst_ref=outs[i].at[recv_slot if recv else send_slot], send_sem=send_sems.at[sem],
            recv_sem=recv_sems.at[sem], device_id=peer, device_id_type=MESH)

    def _local(self, refs, me, i):
        srcs, outs, (_, _, loc_sems) = refs
        local = self._plan(self.modes[i], me)[1]
        if local is None:
            return None
        own = srcs[i] if local[0] is None else srcs[i].at[local[0]]
        return pltpu.make_async_copy(own, outs[i].at[local[1]], loc_sems.at[i])

    def start(self, refs):
        me = _my_pos()
        for i in range(self.n):
            local = self._local(refs, me, i)
            if local is not None:
                local.start()
            for k, cp in enumerate(self._plan(self.modes[i], me)[0]):
                if cp[4] is None:
                    self._copy(refs, me, i, k, False).start()

    def forward(self, refs):
        me = _my_pos()
        for i in range(self.n):
            for k, cp in enumerate(self._plan(self.modes[i], me)[0]):
                if cp[4] is not None:
                    self._copy(refs, me, i, cp[4], True).wait_recv()
                    self._copy(refs, me, i, k, False).start()

    def finish(self, refs):
        me = _my_pos()
        plans = [self._plan(m, me)[0] for m in self.modes]
        for i in range(self.n):
            passed_on = [cp[4] for cp in plans[i] if cp[4] is not None]
            for k in range(len(plans[i])):
                if k not in passed_on:
                    self._copy(refs, me, i, k, True).wait_recv()
                self._copy(refs, me, i, k, False).wait_send()
            local = self._local(refs, me, i)
            if local is not None:
                local.wait()


_ANY = pl.BlockSpec(memory_space=pl.ANY)


def _call(body, args, *, name, grid, in_specs, out_specs, out_shape, params, scratch_shapes=(), comm=None,
          hbm_out=()):
    in_specs, out_specs, out_shape = list(in_specs), list(out_specs), list(out_shape)
    scratch_shapes = list(scratch_shapes)
    for k in hbm_out:
        out_shape[k] = pltpu.HBM(out_shape[k].shape, out_shape[k].dtype)
    if comm is None:
        return list(pl.pallas_call(body, name=name, grid=grid, in_specs=in_specs, out_specs=out_specs,
                                   out_shape=out_shape, scratch_shapes=scratch_shapes, compiler_params=params)(*args))
    n_in, n_out, n_scr, nc = len(in_specs), len(out_specs), len(scratch_shapes), comm.n
    n_steps = 1
    for g in grid:
        n_steps *= g

    def hosted(*refs):
        ins, c_in = refs[:n_in], refs[n_in:n_in + nc]
        outs = refs[n_in + nc:n_in + nc + n_out]
        c_out = refs[n_in + nc + n_out:n_in + 2 * nc + n_out]
        scr = refs[n_in + 2 * nc + n_out:n_in + 2 * nc + n_out + n_scr]
        sems = refs[n_in + 2 * nc + n_out + n_scr:]
        step = pl.program_id(0)
        for d in range(1, len(grid)):
            step = step * grid[d] + pl.program_id(d)
        c_refs = (c_in, c_out, sems)

        @pl.when(step == 0)
        def _():
            comm.barrier()
            comm.start(c_refs)

        if n_steps >= 3:
            @pl.when(step == n_steps - 2)
            def _():
                comm.forward(c_refs)

        body(*ins, *outs, *scr)

        @pl.when(step == n_steps - 1)
        def _():
            if n_steps < 3:
                comm.forward(c_refs)
            comm.finish(c_refs)

    res = pl.pallas_call(
        hosted, name=name, grid=grid, in_specs=in_specs + [_ANY] * nc, out_specs=out_specs + [_ANY] * nc,
        out_shape=out_shape + comm.out_shape(), scratch_shapes=scratch_shapes + comm.scratch(),
        compiler_params=dataclasses.replace(params, collective_id=comm.collective_id()))(*args, *comm.arrs)
    comm.out = list(res[n_out:])
    return list(res[:n_out])


def _exchange(items, *, name):
    comm = _Comm(items)

    def body(*refs):
        r = (refs[:comm.n], refs[comm.n:2 * comm.n], refs[2 * comm.n:])
        comm.barrier()
        comm.start(r)
        comm.forward(r)
        comm.finish(r)

    return list(pl.pallas_call(body, name=name, out_shape=comm.out_shape(), in_specs=[_ANY] * comm.n,
                               out_specs=[_ANY] * comm.n, scratch_shapes=comm.scratch(),
                               compiler_params=pltpu.CompilerParams(collective_id=comm.collective_id()))(*comm.arrs))


class _ModVec:
    def __init__(self, arr, idx):
        self.arr, self.idx = arr, idx

    def spec(self, tps, n_axes):
        idx, blk = self.idx, (1, 1, self.arr.shape[2])
        if n_axes == 1:
            return pl.BlockSpec(blk, lambda i: (i // tps * N_MOD + idx, 0, 0))
        return pl.BlockSpec(blk, lambda i, j: (i // tps * N_MOD + idx, 0, 0))


def _norm_mod_matmul(x, gn, sh, sc, wts, *, seq, tm, tn, name, comm=None):
    T, D = x.shape
    N = wts[0].shape[0]
    nw = len(wts)
    tps = seq // tm

    def body(x_ref, gn_ref, sh_ref, sc_ref, *rest):
        w_refs, h_ref, o_refs = rest[:nw], rest[nw], rest[nw + 1:]

        @pl.when(pl.program_id(1) == 0)
        def _():
            h_ref[...] = _norm_mod(x_ref[...], gn_ref[...], sh_ref[0], sc_ref[0])[0].astype(BF16)

        h = h_ref[...]
        for w_ref, o_ref in zip(w_refs, o_refs):
            o_ref[...] = _dot_nt(h, w_ref[...]).astype(o_ref.dtype)

    row = pl.BlockSpec((tm, D), lambda i, j: (i, 0))
    vec = pl.BlockSpec((1, D), lambda i, j: (0, 0))
    wspec = pl.BlockSpec((tn, D), lambda i, j: (j, 0))
    ospec = pl.BlockSpec((tm, tn), lambda i, j: (i, j))
    blocks = [((tm, D), F32), ((tm, D), BF16)] + [((tn, D), BF16), ((tm, tn), BF16)] * nw
    outs = _call(
        body, (x, gn, sh.arr, sc.arr, *wts), name=name, grid=(T // tm, N // tn),
        in_specs=[row, vec, sh.spec(tps, 2), sc.spec(tps, 2)] + [wspec] * nw,
        out_specs=[row] + [ospec] * nw,
        out_shape=[SDS((T, D), BF16)] + [SDS((T, N), BF16)] * nw,
        params=_params(2, blocks, temp_bytes=2 * _nbytes((tm, tn), F32) + 3 * _nbytes((tm, D), F32)), comm=comm)
    return outs[0], outs[1:]


def _matmul_nt(h, w, *, tm, tn, name, comm=None):
    T, D = h.shape
    N = w.shape[0]

    def body(h_ref, w_ref, o_ref):
        o_ref[...] = _dot_nt(h_ref[...], w_ref[...]).astype(o_ref.dtype)

    blocks = [((tm, D), BF16), ((tn, D), BF16), ((tm, tn), BF16)]
    return _call(
        body, (h, w), name=name, grid=(T // tm, N // tn),
        in_specs=[pl.BlockSpec((tm, D), lambda i, j: (i, 0)), pl.BlockSpec((tn, D), lambda i, j: (j, 0))],
        out_specs=[pl.BlockSpec((tm, tn), lambda i, j: (i, j))],
        out_shape=[SDS((T, N), BF16)],
        params=_params(2, blocks, temp_bytes=2 * _nbytes((tm, tn), F32)), comm=comm)[0]


def _ffn_down(a, b, wd, x, g, *, seq, tm, name, comm=None):
    T, F = a.shape
    D = wd.shape[1]
    tps = seq // tm

    def body(a_ref, b_ref, wd_ref, x_ref, g_ref, xo_ref, y_ref):
        af = a_ref[...].astype(F32)
        act = (af * _sigmoid(af) * b_ref[...].astype(F32)).astype(BF16)
        y = _dot(act, wd_ref[...])
        xo_ref[...] = x_ref[...] + (FFN_RESIDUAL * g_ref[0]) * y
        y_ref[...] = y.astype(BF16)

    wide = pl.BlockSpec((tm, F), lambda i: (i, 0))
    row = pl.BlockSpec((tm, D), lambda i: (i, 0))
    wspec = pl.BlockSpec((F, D), lambda i: (0, 0))
    blocks = [((tm, F), BF16)] * 2 + [((F, D), BF16), ((tm, D), F32), ((tm, D), F32), ((tm, D), BF16)]
    return _call(
        body, (a, b, wd, x, g.arr), name=name, grid=(T // tm,),
        in_specs=[wide, wide, wspec, row, g.spec(tps, 1)], out_specs=[row, row],
        out_shape=[SDS((T, D), F32), SDS((T, D), BF16)],
        params=_params(1, blocks, temp_bytes=3 * _nbytes((tm, F), F32)), comm=comm)


def _final_loss(x, gf, tgt, *, tm, name):
    T, D = x.shape
    nt = T // tm

    def body(x_ref, gf_ref, t_ref, dx_ref, loss_ref, dgf_ref, lacc):
        i = pl.program_id(0)
        xf = x_ref[...]
        gfv = gf_ref[...]
        rstd = lax.rsqrt(jnp.mean(xf * xf, axis=-1, keepdims=True) + EPS)
        xhat = xf * rstd
        err = xhat * gfv - t_ref[...]
        dy = err * (1.0 / D)
        dxhat = dy * gfv
        dx_ref[...] = (rstd * (dxhat - xhat * jnp.mean(dxhat * xhat, axis=-1, keepdims=True))).astype(dx_ref.dtype)
        _acc(dgf_ref, _rowsum(dy * xhat), i == 0)
        _acc(lacc, _rowsum(err * err), i == 0)

        @pl.when(i == nt - 1)
        def _():
            loss_ref[...] = jnp.broadcast_to((0.5 / D) * jnp.sum(lacc[...]), loss_ref.shape)

    row = pl.BlockSpec((tm, D), lambda i: (i, 0))
    vec = pl.BlockSpec((1, D), lambda i: (0, 0))
    lspec = pl.BlockSpec((1, 128), lambda i: (0, 0))
    blocks = [((tm, D), F32)] * 3
    return _call(
        body, (x, gf, tgt), name=name, grid=(nt,),
        in_specs=[row, vec, row], out_specs=[row, lspec, vec],
        out_shape=[SDS((T, D), GRAD_STREAM), SDS((1, 128), F32), SDS((1, D), F32)],
        scratch_shapes=[pltpu.VMEM((1, D), F32)],
        params=_params(1, blocks, temp_bytes=4 * _nbytes((tm, D), F32)), hbm_out=(0,))


def _ffn_bwd_down(dxo, g, y, wd, a, b, *, seq, tm, tn, name, comm=None):
    T, F = a.shape
    D = wd.shape[1]
    tps = seq // tm
    nb = T // seq

    def body(dxo_ref, g_ref, y_ref, wd_ref, a_ref, b_ref, dyb_ref, da_ref, db_ref, dg_ref):
        i = pl.program_id(0)

        @pl.when(pl.program_id(1) == 0)
        def _():
            dx = dxo_ref[...].astype(F32)
            dyb_ref[...] = ((FFN_RESIDUAL * g_ref[0]) * dx).astype(BF16)
            part = _rowsum(FFN_RESIDUAL * dx * y_ref[...].astype(F32))
            _acc(dg_ref, part[None], i % tps == 0)

        dact = _dot_nt(dyb_ref[...], wd_ref[...])
        af = a_ref[...].astype(F32)
        bf = b_ref[...].astype(F32)
        sg = _sigmoid(af)
        silu = af * sg
        da_ref[...] = (dact * bf * (sg + silu * (1.0 - sg))).astype(BF16)
        db_ref[...] = (dact * silu).astype(BF16)

    row = pl.BlockSpec((tm, D), lambda i, j: (i, 0))
    per_b = pl.BlockSpec((1, 1, D), lambda i, j: (i // tps, 0, 0))
    wspec = pl.BlockSpec((tn, D), lambda i, j: (j, 0))
    chunk = pl.BlockSpec((tm, tn), lambda i, j: (i, j))
    blocks = [((tm, D), F32), ((tm, D), BF16), ((tn, D), BF16), ((tm, D), BF16)] + [((tm, tn), BF16)] * 4
    return _call(
        body, (dxo, g.arr, y, wd, a, b), name=name, grid=(T // tm, F // tn),
        in_specs=[row, g.spec(tps, 2), row, wspec, chunk, chunk],
        out_specs=[row, chunk, chunk, per_b],
        out_shape=[SDS((T, D), BF16)] + [SDS((T, F), BF16)] * 2 + [SDS((nb, 1, D), F32)],
        params=_params(2, blocks, temp_bytes=6 * _nbytes((tm, tn), F32)), comm=comm)


def _matmul_norm_mod_bwd(ds, ws, x, gn, sc, dxo, *, seq, tm, name, out_dtype, comm=None):
    T, D = x.shape
    nk = len(ws)
    sizes = [len(g) for g in ds]
    ds = [d for g in ds for d in g]
    tps = seq // tm
    nb = T // seq

    def body(*refs):
        w_refs = refs[len(ds):len(ds) + nk]
        x_ref, gn_ref, sc_ref, dxo_ref, dxi_ref, dsh_ref, dsc_ref, dgn_ref = refs[len(ds) + nk:]
        i = pl.program_id(0)
        dh, at = None, 0
        for n, w_ref in zip(sizes, w_refs):
            pieces = [r[...] for r in refs[at:at + n]]
            at += n
            part = _dot(pieces[0] if n == 1 else jnp.concatenate(pieces, axis=1), w_ref[...])
            dh = part if dh is None else dh + part
        gnv = gn_ref[...]
        scv = sc_ref[0]
        _, xhat, rstd, yn = _norm_mod(x_ref[...], gnv, 0.0, scv)
        dyn = dh * (1.0 + scv)
        dxhat = dyn * gnv
        dxi_ref[...] = (dxo_ref[...].astype(F32)
                        + rstd * (dxhat - xhat * jnp.mean(dxhat * xhat, axis=-1, keepdims=True))).astype(out_dtype)
        first_of_seq = i % tps == 0
        _acc(dsh_ref, _rowsum(dh)[None], first_of_seq)
        _acc(dsc_ref, _rowsum(dh * yn)[None], first_of_seq)
        _acc(dgn_ref, _rowsum(dyn * xhat), i == 0)

    row = pl.BlockSpec((tm, D), lambda i: (i, 0))
    vec = pl.BlockSpec((1, D), lambda i: (0, 0))
    per_b = pl.BlockSpec((1, 1, D), lambda i: (i // tps, 0, 0))
    d_specs = [pl.BlockSpec((tm, d.shape[1]), lambda i: (i, 0)) for d in ds]
    w_specs = [pl.BlockSpec(w.shape, lambda i: (0, 0)) for w in ws]
    blocks = ([((tm, d.shape[1]), BF16) for d in ds] + [(w.shape, BF16) for w in ws] + [((tm, D), F32)] * 3)
    return _call(
        body, (*ds, *ws, x, gn, sc.arr, dxo), name=name, grid=(T // tm,),
        in_specs=d_specs + w_specs + [row, vec, sc.spec(tps, 1), row],
        out_specs=[row, per_b, per_b, vec],
        out_shape=[SDS((T, D), out_dtype), SDS((nb, 1, D), F32), SDS((nb, 1, D), F32), SDS((1, D), F32)],
        params=_params(1, blocks, temp_bytes=6 * _nbytes((tm, D), F32)), comm=comm)


def _layernorm_silu(yc, lg, lb):
    mu = jnp.mean(yc, axis=-1, keepdims=True)
    cen = yc - mu
    rstd = lax.rsqrt(jnp.mean(cen * cen, axis=-1, keepdims=True) + EPS)
    xh = cen * rstd
    l = xh * lg + lb
    s = _sigmoid(l)
    return l * s, xh, rstd, l, s


GATE_W = 256


def _gate_specs(tm, D, col):
    return [pl.BlockSpec((tm, GATE_W), lambda i, blk=col // GATE_W + t: (i, blk)) for t in range(D // GATE_W)]


def _gate(refs):
    return jnp.concatenate([r[...] for r in refs], axis=1).astype(F32)


def _mix_out(ao, yc, proj, wao, wco, wout, x1, g2, lg, lb, *, seq, tm, ga_col, gc_col, name, comm=None):
    T, D = x1.shape
    tps = seq // tm
    ng = D // GATE_W

    def body(ao_ref, yc_ref, *rest):
        ga_refs, gc_refs = rest[:ng], rest[ng:2 * ng]
        (wao_ref, wco_ref, wout_ref, x1_ref, g2_ref, lg_ref, lb_ref,
         x2_ref, z_ref, ya_ref, ycv_ref, cact_ref, mrg_ref) = rest[2 * ng:]
        ya = _dot(ao_ref[...], wao_ref[...])
        cact = _layernorm_silu(yc_ref[...], lg_ref[...], lb_ref[...])[0].astype(BF16)
        ycv = _dot(cact, wco_ref[...])
        merged = (_sigmoid(_gate(ga_refs)) * ya + _sigmoid(_gate(gc_refs)) * ycv).astype(BF16)
        z = _dot(merged, wout_ref[...])
        x2_ref[...] = x1_ref[...] + g2_ref[0] * z
        z_ref[...] = z.astype(BF16)
        ya_ref[...] = ya.astype(BF16)
        ycv_ref[...] = ycv.astype(BF16)
        cact_ref[...] = cact
        mrg_ref[...] = merged

    row = pl.BlockSpec((tm, D), lambda i: (i, 0))
    vec = pl.BlockSpec((1, D), lambda i: (0, 0))
    wspec = pl.BlockSpec((D, D), lambda i: (0, 0))
    gates = _gate_specs(tm, D, ga_col) + _gate_specs(tm, D, gc_col)
    blocks = ([((tm, D), BF16), ((tm, D), F32), ((tm, D), BF16), ((tm, D), BF16)] + [((D, D), BF16)] * 3
              + [((tm, D), F32)] * 2 + [((tm, D), BF16)] * 5)
    return _call(
        body, (ao, yc, *[proj] * (2 * ng), wao, wco, wout, x1, g2.arr, lg, lb), name=name, grid=(T // tm,),
        in_specs=[row, row, *gates, wspec, wspec, wspec, row, g2.spec(tps, 1), vec, vec],
        out_specs=[row] * 6,
        out_shape=[SDS((T, D), F32)] + [SDS((T, D), BF16)] * 5,
        params=_params(1, blocks, temp_bytes=8 * _nbytes((tm, D), F32)), comm=comm)


def _mix_out_bwd(dx2, g2, z, wout, proj, ya, ycv, wao, wco, yc, lg, lb, *, seq, tm, ga_col, gc_col, name,
                 comm=None):
    T, D = dx2.shape
    tps = seq // tm
    nb = T // seq
    ng = D // GATE_W

    def body(dx2_ref, g2_ref, z_ref, wout_ref, *rest):
        ga_refs, gc_refs = rest[:ng], rest[ng:2 * ng]
        (ya_ref, ycv_ref, wao_ref, wco_ref, yc_ref, lg_ref, lb_ref, dz_ref, dya_ref, dycv_ref, dga_ref, dgc_ref,
         dao_ref, dyc_ref, dg2_ref, dlg_ref, dlb_ref) = rest[2 * ng:]
        i = pl.program_id(0)
        dx = dx2_ref[...].astype(F32)
        _acc(dg2_ref, _rowsum(dx * z_ref[...].astype(F32))[None], i % tps == 0)
        dzb = (g2_ref[0] * dx).astype(BF16)
        dz_ref[...] = dzb
        dmerged = _dot_nt(dzb, wout_ref[...])
        sa = _sigmoid(_gate(ga_refs))
        sc_ = _sigmoid(_gate(gc_refs))
        dya = (dmerged * sa).astype(BF16)
        dycv = (dmerged * sc_).astype(BF16)
        dya_ref[...] = dya
        dycv_ref[...] = dycv
        dga_ref[...] = (dmerged * ya_ref[...].astype(F32) * (sa * (1.0 - sa))).astype(BF16)
        dgc_ref[...] = (dmerged * ycv_ref[...].astype(F32) * (sc_ * (1.0 - sc_))).astype(BF16)
        dao_ref[...] = _dot_nt(dya, wao_ref[...]).astype(BF16)
        dcact = _dot_nt(dycv, wco_ref[...])
        lgv = lg_ref[...]
        _, xh, rstd, l, s = _layernorm_silu(yc_ref[...], lgv, lb_ref[...])
        dl = dcact * (s * (1.0 + l * (1.0 - s)))
        _acc(dlb_ref, _rowsum(dl), i == 0)
        _acc(dlg_ref, _rowsum(dl * xh), i == 0)
        dxh = dl * lgv
        dyc_ref[...] = rstd * (dxh - jnp.mean(dxh, axis=-1, keepdims=True)
                               - xh * jnp.mean(dxh * xh, axis=-1, keepdims=True))

    row = pl.BlockSpec((tm, D), lambda i: (i, 0))
    vec = pl.BlockSpec((1, D), lambda i: (0, 0))
    per_b = pl.BlockSpec((1, 1, D), lambda i: (i // tps, 0, 0))
    wspec = pl.BlockSpec((D, D), lambda i: (0, 0))
    gates = _gate_specs(tm, D, ga_col) + _gate_specs(tm, D, gc_col)
    blocks = ([((tm, D), F32)] * 3 + [((tm, D), BF16)] * 11 + [((D, D), BF16)] * 3)
    return _call(
        body, (dx2, g2.arr, z, wout, *[proj] * (2 * ng), ya, ycv, wao, wco, yc, lg, lb), name=name,
        grid=(T // tm,),
        in_specs=[row, g2.spec(tps, 1), row, wspec, *gates, row, row, wspec, wspec, row, vec, vec],
        out_specs=[row] * 7 + [per_b, vec, vec],
        out_shape=[SDS((T, D), BF16)] * 6 + [SDS((T, D), F32), SDS((nb, 1, D), F32), SDS((1, D), F32),
                                             SDS((1, D), F32)],
        params=_params(1, blocks, temp_bytes=10 * _nbytes((tm, D), F32)), comm=comm)


Q_BLOCK = 64
BAND = Q_BLOCK + ATT_BLOCK
GROUP_ROWS = GQA_GROUP * Q_BLOCK
PAIR_W = 2 * HEAD_DIM
GROUP_W = GQA_GROUP * HEAD_DIM


def _lane_lo():
    return lax.broadcasted_iota(jnp.int32, (1, PAIR_W), 1) < HEAD_DIM


def _band_bias():
    sj = lax.broadcasted_iota(jnp.int32, (BAND, GROUP_ROWS), 0)
    qi = lax.broadcasted_iota(jnp.int32, (BAND, GROUP_ROWS), 1) & (Q_BLOCK - 1)
    rel = qi + ATT_BLOCK - sj
    bias = jnp.where(jnp.logical_and(rel >= 0, rel < ATT_BLOCK), 0.0, NEG_BIG)
    return bias, lax.broadcasted_iota(jnp.int32, (BAND, 1), 0)


def _block_bias(bias0, key_index, r0):
    return bias0 + jnp.where(key_index + r0 < ATT_BLOCK, NEG_BIG, 0.0)


def _dup_heads(src_ref, dst, seq):
    x = src_ref[...]
    i = lax.broadcasted_iota(jnp.int32, (KV_WIDTH, PAIR_W), 0)
    j = lax.broadcasted_iota(jnp.int32, (KV_WIDTH, PAIR_W), 1) & (HEAD_DIM - 1)
    for g in range(N_KV_HEADS):
        sel = jnp.where(i == j + g * HEAD_DIM, 1.0, 0.0).astype(BF16)
        dst[g, pl.ds(0, ATT_BLOCK), :] = jnp.zeros((ATT_BLOCK, PAIR_W), BF16)
        dst[g, pl.ds(ATT_BLOCK, seq), :] = _dot(x, sel).astype(BF16)


def _stack_heads(blk, g, lo):
    parts = []
    for p in range(GQA_GROUP // 2):
        pair = blk[:, g * GROUP_W + p * PAIR_W:g * GROUP_W + (p + 1) * PAIR_W]
        parts += [jnp.where(lo, pair, jnp.zeros_like(pair)), jnp.where(lo, jnp.zeros_like(pair), pair)]
    return jnp.concatenate(parts, axis=0)


def _unstack_heads(full, ref, r0, g, lo):
    for p in range(GQA_GROUP // 2):
        even = full[(2 * p) * Q_BLOCK:(2 * p + 1) * Q_BLOCK, :]
        odd = full[(2 * p + 1) * Q_BLOCK:(2 * p + 2) * Q_BLOCK, :]
        ref[pl.ds(r0, Q_BLOCK), g * GROUP_W + p * PAIR_W:g * GROUP_W + (p + 1) * PAIR_W] = (
            jnp.where(lo, even, odd).astype(ref.dtype))


def _sink_row(sink_ref, g):
    return jnp.concatenate([jnp.full((1, Q_BLOCK), sink_ref[0, g * GQA_GROUP + h], F32)
                            for h in range(GQA_GROUP)], axis=1)


def _group_probs(qs, k2, bias, sink):
    s = _dot_nt(k2, qs) * (HEAD_DIM ** -0.5) + bias
    m = jnp.maximum(jnp.max(s, axis=0, keepdims=True), sink)
    p = jnp.exp(s - m)
    psink = jnp.exp(sink - m)
    inv = 1.0 / (jnp.sum(p, axis=0, keepdims=True) + psink)
    return p * inv, psink * inv


def _attn_fwd(projp, sinks, *, seq, q_blk, k_blk, v_blk, name, comm=None):
    T = projp.shape[0]
    QW = N_Q_HEADS * HEAD_DIM
    nblk = seq // Q_BLOCK

    def body(q_ref, k_ref, v_ref, sink_ref, o_ref, k2s, v2s):
        _dup_heads(k_ref, k2s, seq)
        _dup_heads(v_ref, v2s, seq)
        lo = _lane_lo()
        bias0, key_index = _band_bias()
        sink_rows = [_sink_row(sink_ref, g) for g in range(N_KV_HEADS)]

        def blk(n, carry):
            r0 = pl.multiple_of(n * Q_BLOCK, Q_BLOCK)
            band = pl.ds(r0, BAND)
            qb = q_ref[pl.ds(r0, Q_BLOCK), :]
            bias = _block_bias(bias0, key_index, r0)
            for g in range(N_KV_HEADS):
                probs_t, _ = _group_probs(_stack_heads(qb, g, lo), k2s[g, band, :], bias, sink_rows[g])
                _unstack_heads(_dot_tn(probs_t.astype(BF16), v2s[g, band, :]), o_ref, r0, g, lo)
            return carry

        lax.fori_loop(0, nblk, blk, 0, unroll=2)

    blocks = [((seq, QW), BF16)] * 2 + [((seq, KV_WIDTH), BF16)] * 2
    return _call(
        body, (projp, projp, projp, sinks), name=name, grid=(T // seq,),
        in_specs=[pl.BlockSpec((seq, QW), lambda b: (b, q_blk)),
                  pl.BlockSpec((seq, KV_WIDTH), lambda b: (b, k_blk)),
                  pl.BlockSpec((seq, KV_WIDTH), lambda b: (b, v_blk)),
                  pl.BlockSpec(memory_space=pltpu.SMEM)],
        out_specs=[pl.BlockSpec((seq, QW), lambda b: (b, 0))],
        out_shape=[SDS((T, QW), BF16)],
        scratch_shapes=[pltpu.VMEM((N_KV_HEADS, seq + ATT_BLOCK, PAIR_W), BF16)] * 2,
        params=_params(1, blocks, temp_bytes=2 * _nbytes((N_KV_HEADS, seq + ATT_BLOCK, PAIR_W), BF16)
                       + 8 * _nbytes((BAND, GROUP_ROWS), F32)), comm=comm, hbm_out=(0,))[0]


def _attn_bwd(projp, dao, sinks, *, seq, q_blk, k_blk, v_blk, name, comm=None):
    T = projp.shape[0]
    QW = N_Q_HEADS * HEAD_DIM
    assert seq % (2 * Q_BLOCK) == 0
    nblk = seq // Q_BLOCK

    def body(q_ref, k_ref, v_ref, do_ref, sink_ref, dq_ref, dk_ref, dv_ref, dsink_ref, k2s, v2s, dkacc, dvacc):
        _dup_heads(k_ref, k2s, seq)
        _dup_heads(v_ref, v2s, seq)
        dkacc[...] = jnp.zeros(dkacc.shape, F32)
        dvacc[...] = jnp.zeros(dvacc.shape, F32)
        lane = lax.broadcasted_iota(jnp.int32, (1, PAIR_W), 1)
        lo = lane < HEAD_DIM
        bias0, key_index = _band_bias()
        sink_rows = [_sink_row(sink_ref, g) for g in range(N_KV_HEADS)]

        def blk(n, tsinks):
            tsinks = list(tsinks)
            r0 = pl.multiple_of(n * Q_BLOCK, Q_BLOCK)
            band = pl.ds(r0, BAND)
            qb = q_ref[pl.ds(r0, Q_BLOCK), :]
            dob = do_ref[pl.ds(r0, Q_BLOCK), :]
            bias = _block_bias(bias0, key_index, r0)
            for g in range(N_KV_HEADS):
                qs = _stack_heads(qb, g, lo)
                dos = _stack_heads(dob, g, lo)
                k2 = k2s[g, band, :]
                v2 = v2s[g, band, :]
                probs_t, psink = _group_probs(qs, k2, bias, sink_rows[g])
                dp_t = _dot_nt(v2, dos)
                delta = jnp.sum(probs_t * dp_t, axis=0, keepdims=True)
                ds_t = (probs_t * (dp_t - delta) * (HEAD_DIM ** -0.5)).astype(BF16)
                tsinks[g] = tsinks[g] + psink * delta
                _unstack_heads(_dot_tn(ds_t, k2), dq_ref, r0, g, lo)
                dkacc[g, band, :] = dkacc[g, band, :] + _dot(ds_t, qs)
                dvacc[g, band, :] = dvacc[g, band, :] + _dot(probs_t.astype(BF16), dos)
            return tuple(tsinks)

        def two_blocks(m, tsinks):
            return blk(2 * m + 1, blk(2 * m, tsinks))

        tsinks = lax.fori_loop(0, nblk // 2, two_blocks, (jnp.zeros((1, GROUP_ROWS), F32),) * N_KV_HEADS)
        dsink = jnp.zeros((1, PAIR_W), F32)
        for g in range(N_KV_HEADS):
            for h in range(GQA_GROUP):
                dsink = dsink + jnp.where(lane == g * GQA_GROUP + h,
                                          -jnp.sum(tsinks[g][:, h * Q_BLOCK:(h + 1) * Q_BLOCK]), 0.0)
        _acc(dsink_ref, dsink, pl.program_id(0) == 0)

        def fold(acc, g):
            a = acc[g, pl.ds(ATT_BLOCK, seq), :]
            return a + pltpu.roll(a, HEAD_DIM, 1)

        dk_ref[...] = jnp.where(lo, fold(dkacc, 0), fold(dkacc, 1)).astype(BF16)
        dv_ref[...] = jnp.where(lo, fold(dvacc, 0), fold(dvacc, 1)).astype(BF16)

    blocks = [((seq, QW), BF16)] * 3 + [((seq, KV_WIDTH), BF16)] * 4
    kv_spec_out = pl.BlockSpec((seq, KV_WIDTH), lambda b: (b, 0))
    return _call(
        body, (projp, projp, projp, dao, sinks), name=name, grid=(T // seq,),
        in_specs=[pl.BlockSpec((seq, QW), lambda b: (b, q_blk)),
                  pl.BlockSpec((seq, KV_WIDTH), lambda b: (b, k_blk)),
                  pl.BlockSpec((seq, KV_WIDTH), lambda b: (b, v_blk)),
                  pl.BlockSpec((seq, QW), lambda b: (b, 0)),
                  pl.BlockSpec(memory_space=pltpu.SMEM)],
        out_specs=[pl.BlockSpec((seq, QW), lambda b: (b, 0)), kv_spec_out, kv_spec_out,
                   pl.BlockSpec((1, 128), lambda b: (0, 0))],
        out_shape=[SDS((T, QW), BF16), SDS((T, KV_WIDTH), BF16), SDS((T, KV_WIDTH), BF16), SDS((1, 128), F32)],
        scratch_shapes=[pltpu.VMEM((N_KV_HEADS, seq + ATT_BLOCK, PAIR_W), BF16)] * 2
        + [pltpu.VMEM((N_KV_HEADS, seq + ATT_BLOCK, PAIR_W), F32)] * 2,
        params=_params(1, blocks, temp_bytes=6 * _nbytes((N_KV_HEADS, seq + ATT_BLOCK, PAIR_W), BF16)
                       + 16 * _nbytes((BAND, GROUP_ROWS), F32)), comm=comm)


SUBLANES = 8


def _sublane_shifts(win):
    n = CONV_ROWS + CONV_HALO
    return [win] + [pltpu.roll(win, n - b, 0) for b in range(1, SUBLANES)]


def _window(shifted, off):
    a = off // SUBLANES * SUBLANES
    return shifted[off % SUBLANES][a:a + CONV_ROWS, :]


def _conv_fwd(projp, w, bias, *, seq, cw, a_col, b_col, name, comm=None):
    T = projp.shape[0]
    C = w.shape[1]
    nchunk = seq // CONV_ROWS

    def body(a_ref, b_ref, w_ref, bias_ref, y_ref, upad):
        upad[pl.ds(0, CONV_HALO), :] = jnp.zeros((CONV_HALO, cw), F32)
        upad[pl.ds(CONV_HALO, seq), :] = a_ref[...].astype(F32) * _sigmoid(b_ref[...].astype(F32))
        wv = w_ref[...]
        bv = bias_ref[...]

        def chunk(r, carry):
            r0 = pl.multiple_of(r * CONV_ROWS, CONV_ROWS)
            shifted = _sublane_shifts(upad[pl.ds(r0, CONV_ROWS + CONV_HALO), :])
            acc = jnp.broadcast_to(bv, (CONV_ROWS, cw))
            for k in range(CONV_WIDTH):
                acc = acc + wv[k:k + 1, :] * _window(shifted, CONV_HALO - (CONV_WIDTH - 1) + k)
            y_ref[pl.ds(r0, CONV_ROWS), :] = acc
            return carry

        lax.fori_loop(0, nchunk, chunk, 0)

    blocks = [((seq, cw), BF16)] * 2 + [((seq, cw), F32)]
    return _call(
        body, (projp, projp, w, bias), name=name, grid=(T // seq, C // cw),
        in_specs=[pl.BlockSpec((seq, cw), lambda b, c: (b, a_col // cw + c)),
                  pl.BlockSpec((seq, cw), lambda b, c: (b, b_col // cw + c)),
                  pl.BlockSpec((CONV_WIDTH, cw), lambda b, c: (0, c)),
                  pl.BlockSpec((1, cw), lambda b, c: (0, c))],
        out_specs=[pl.BlockSpec((seq, cw), lambda b, c: (b, c))],
        out_shape=[SDS((T, C), F32)],
        scratch_shapes=[pltpu.VMEM((seq + CONV_HALO, cw), F32)],
        params=_params(2, blocks, temp_bytes=6 * _nbytes((seq, cw), F32)), comm=comm, hbm_out=(0,))[0]


def _conv_bwd(dy, projp, w, *, seq, cw, a_col, b_col, name, comm=None):
    T = projp.shape[0]
    C = w.shape[1]
    nchunk = seq // CONV_ROWS
    SUB = 8

    def body(dy_ref, a_ref, b_ref, w_ref, da_ref, db_ref, dw_ref, dbias_ref, dypad, dwp):
        first = pl.program_id(1) == 0
        dyv = dy_ref[...]
        dypad[pl.ds(0, seq), :] = dyv
        dypad[pl.ds(seq, CONV_HALO), :] = jnp.zeros((CONV_HALO, cw), F32)
        dwp[...] = jnp.zeros(dwp.shape, F32)
        wv = w_ref[...]

        def chunk(r, carry):
            r0 = pl.multiple_of(r * CONV_ROWS, CONV_ROWS)
            dy_shifts = _sublane_shifts(dypad[pl.ds(r0, CONV_ROWS + CONV_HALO), :])
            ac = a_ref[pl.ds(r0, CONV_ROWS), :].astype(F32)
            sbc = _sigmoid(b_ref[pl.ds(r0, CONV_ROWS), :].astype(F32))
            uc = ac * sbc
            du = jnp.zeros((CONV_ROWS, cw), F32)
            for k in range(CONV_WIDTH):
                dyk = _window(dy_shifts, CONV_WIDTH - 1 - k)
                du = du + wv[k:k + 1, :] * dyk
                prod = uc * dyk
                part = prod[0:SUB, :]
                for s in range(1, CONV_ROWS // SUB):
                    part = part + prod[s * SUB:(s + 1) * SUB, :]
                dwp[pl.ds(k * SUB, SUB), :] = dwp[pl.ds(k * SUB, SUB), :] + part
            da_ref[pl.ds(r0, CONV_ROWS), :] = (du * sbc).astype(BF16)
            db_ref[pl.ds(r0, CONV_ROWS), :] = (du * ac * (sbc * (1.0 - sbc))).astype(BF16)
            return carry

        lax.fori_loop(0, nchunk, chunk, 0)

        @pl.when(first)
        def _():
            dw_ref[...] = jnp.zeros(dw_ref.shape, F32)
            dbias_ref[...] = jnp.zeros(dbias_ref.shape, F32)

        for k in range(CONV_WIDTH):
            dw_ref[k:k + 1, :] = dw_ref[k:k + 1, :] + _rowsum(dwp[pl.ds(k * SUB, SUB), :])
        dbias_ref[...] = dbias_ref[...] + _rowsum(dyv)

    blocks = [((seq, cw), F32)] + [((seq, cw), BF16)] * 4
    return _call(
        body, (dy, projp, projp, w), name=name, grid=(C // cw, T // seq),
        in_specs=[pl.BlockSpec((seq, cw), lambda c, b: (b, c)),
                  pl.BlockSpec((seq, cw), lambda c, b: (b, a_col // cw + c)),
                  pl.BlockSpec((seq, cw), lambda c, b: (b, b_col // cw + c)),
                  pl.BlockSpec((CONV_WIDTH, cw), lambda c, b: (0, c))],
        out_specs=[pl.BlockSpec((seq, cw), lambda c, b: (b, c)), pl.BlockSpec((seq, cw), lambda c, b: (b, c)),
                   pl.BlockSpec((CONV_WIDTH, cw), lambda c, b: (0, c)), pl.BlockSpec((1, cw), lambda c, b: (0, c))],
        out_shape=[SDS((T, C), BF16), SDS((T, C), BF16), SDS((CONV_WIDTH, C), F32), SDS((1, C), F32)],
        scratch_shapes=[pltpu.VMEM((seq + CONV_HALO, cw), F32), pltpu.VMEM((CONV_WIDTH * SUB, cw), F32)],
        params=_params(2, blocks, temp_bytes=8 * _nbytes((seq, cw), F32)), comm=comm, hbm_out=(0, 1))


def _matmul_tn(a, b, *, name, gate=None, comm=None):
    T, M = a.shape
    N = b.shape[1]
    bm = _pick(M, (768, 512, 256))
    lhs = [a] if gate is None else [a, gate]

    def body(*refs):
        b_ref, o_ref = refs[len(lhs)], refs[len(lhs) + 1]
        av = refs[0][...]
        if gate is not None:
            af = av.astype(F32)
            av = (af * _sigmoid(af) * refs[1][...].astype(F32)).astype(BF16)
        o_ref[...] = _dot_tn(av, b_ref[...]).astype(BF16)

    blocks = [((T, bm), BF16)] * len(lhs) + [((T, N), BF16), ((bm, N), BF16)]
    return _call(
        body, (*lhs, b), name=name, grid=(M // bm,),
        in_specs=[pl.BlockSpec((T, bm), lambda i: (0, i))] * len(lhs) + [pl.BlockSpec((T, N), lambda i: (0, 0))],
        out_specs=[pl.BlockSpec((bm, N), lambda i: (i, 0))],
        out_shape=[SDS((M, N), BF16)],
        params=_params(1, blocks, temp_bytes=(2 + 4 * len(lhs)) * _nbytes((T, bm), BF16) + 2 * _nbytes((bm, N), F32)),
        comm=comm)[0]


TN_BLOCK = 256


def _matmul_tn_pieces(groups, b, *, name, comm=None):
    T, N = b.shape
    flat = [a for g in groups for a in g]
    starts, n_steps = [], 0
    for g in groups:
        width = sum(a.shape[1] for a in g)
        assert width % TN_BLOCK == 0 and (len(g) == 1 or width == TN_BLOCK), [a.shape for a in g]
        starts.append(n_steps)
        n_steps += width // TN_BLOCK

    def body(*refs):
        a_refs, b_ref, o_ref = refs[:len(flat)], refs[len(flat)], refs[len(flat) + 1]
        i = pl.program_id(0)
        at = 0
        for g, start in zip(groups, starts):
            mine = a_refs[at:at + len(g)]
            at += len(g)
            steps = sum(a.shape[1] for a in g) // TN_BLOCK

            @pl.when(jnp.logical_and(i >= start, i < start + steps))
            def _(mine=mine):
                a = mine[0][...] if len(mine) == 1 else jnp.concatenate([r[...] for r in mine], axis=1)
                o_ref[...] = _dot_tn(a, b_ref[...]).astype(BF16)

    a_specs = []
    for g, start in zip(groups, starts):
        for a in g:
            if len(g) == 1:
                last = a.shape[1] // TN_BLOCK - 1
                a_specs.append(pl.BlockSpec(
                    (T, TN_BLOCK), lambda i, start=start, last=last: (0, jnp.clip(i - start, 0, last))))
            else:
                a_specs.append(pl.BlockSpec((T, a.shape[1]), lambda i: (0, 0)))
    blocks = [((T, TN_BLOCK), BF16)] * len(flat) + [((T, N), BF16), ((TN_BLOCK, N), BF16)]
    return _call(
        body, (*flat, b), name=name, grid=(n_steps,),
        in_specs=a_specs + [pl.BlockSpec((T, N), lambda i: (0, 0))],
        out_specs=[pl.BlockSpec((TN_BLOCK, N), lambda i: (i, 0))],
        out_shape=[SDS((n_steps * TN_BLOCK, N), BF16)],
        params=_params(1, blocks, temp_bytes=2 * _nbytes((T, TN_BLOCK), BF16) + 2 * _nbytes((TN_BLOCK, N), F32)),
        comm=comm)[0]


def _sum_parts(p_ref):
    g = p_ref[0].astype(F32)
    for s in range(1, p_ref.shape[0]):
        g = g + p_ref[s].astype(F32)
    return g


def _pair_add(g, staged, *, name):
    _, R, W = g.shape
    nq = staged.shape[0]
    tr = _row_tile(R)

    def body(g_ref, s_ref, o_ref):
        mine = jnp.where(lax.axis_index("c") == 0, g_ref[0, 0].astype(F32), g_ref[0, 1].astype(F32))
        o_ref[0] = (mine + s_ref[0].astype(F32)).astype(o_ref.dtype)

    return _call(
        body, (g.reshape(nq, 2, R, W), staged), name=name, grid=(nq, R // tr),
        in_specs=[pl.BlockSpec((1, 2, tr, W), lambda q, i: (q, 0, i, 0)),
                  pl.BlockSpec((1, tr, W), lambda q, i: (q, i, 0))],
        out_specs=[pl.BlockSpec((1, tr, W), lambda q, i: (q, i, 0))],
        out_shape=[SDS((nq, R, W), g.dtype)],
        params=_params(2, [((4, tr, W), g.dtype)], temp_bytes=3 * _nbytes((tr, W), F32)))[0]


def _adamw_update(w, g, m, v):
    m = ADAM_B1 * m + (1.0 - ADAM_B1) * g
    v = ADAM_B2 * v + (1.0 - ADAM_B2) * (g * g)
    m_hat = m / (1.0 - ADAM_B1 ** ADAM_STEP)
    v_hat = v / (1.0 - ADAM_B2 ** ADAM_STEP)
    delta = -ADAM_LR * (m_hat / (jnp.sqrt(v_hat) + ADAM_EPS) + ADAM_WD * w)
    return delta, m, v


def _row_tile(R):
    return _pick(R, (256, 128, 112, 88, 64, 32, 16, 8))


def _sum8(parts, *, name):
    n, R, W = parts.shape
    tr = _row_tile(R)

    def body(p_ref, o_ref):
        o_ref[...] = _sum_parts(p_ref)

    return _call(
        body, (parts,), name=name, grid=(R // tr,),
        in_specs=[pl.BlockSpec((n, tr, W), lambda i: (0, i, 0))],
        out_specs=[pl.BlockSpec((tr, W), lambda i: (i, 0))],
        out_shape=[SDS((R, W), F32)],
        params=_params(1, [((n, tr, W), parts.dtype), ((tr, W), F32)]))[0]


def _adamw(g, w, m, v, *, name):
    R, W = w.shape
    tr = _row_tile(R)

    def body(g_ref, w_ref, m_ref, v_ref, d_ref, mo_ref, vo_ref):
        d_ref[...], mo_ref[...], vo_ref[...] = _adamw_update(w_ref[...], g_ref[...], m_ref[...], v_ref[...])

    spec = pl.BlockSpec((tr, W), lambda i: (i, 0))
    return _call(
        body, (g, w, m, v), name=name, grid=(R // tr,),
        in_specs=[spec] * 4, out_specs=[spec] * 3, out_shape=[SDS((R, W), F32)] * 3,
        params=_params(1, [((tr, W), F32)] * 7))


def _sum8_adamw(parts, w, m, v, *, name):
    R, W = w.shape
    n = parts.shape[0]
    tr = _row_tile(R)

    def body(p_ref, w_ref, m_ref, v_ref, g_ref, d_ref, mo_ref, vo_ref):
        g = _sum_parts(p_ref)
        g_ref[...] = g
        d_ref[...], mo_ref[...], vo_ref[...] = _adamw_update(w_ref[...], g, m_ref[...], v_ref[...])

    spec = pl.BlockSpec((tr, W), lambda i: (i, 0))
    return _call(
        body, (parts, w, m, v), name=name, grid=(R // tr,),
        in_specs=[pl.BlockSpec((n, tr, W), lambda i: (0, i, 0))] + [spec] * 3,
        out_specs=[spec] * 4, out_shape=[SDS((R, W), F32)] * 4,
        params=_params(1, [((n, tr, W), parts.dtype)] + [((tr, W), F32)] * 7))


def _ada_fwd(c_all, w, bias, *, name):
    NB, D = c_all.shape
    N = w.shape[1]

    def body(c_ref, w_ref, b_ref, o_ref):
        cv = c_ref[...]
        ca = (cv * _sigmoid(cv)).astype(BF16)
        o_ref[...] = _dot(ca, w_ref[...].astype(BF16)) + b_ref[...]

    full = lambda s: pl.BlockSpec(s, lambda i: (0,) * len(s))
    return _call(
        body, (c_all, w, bias), name=name, grid=(1,),
        in_specs=[full((NB, D)), full((D, N)), full((1, N))], out_specs=[full((NB, N))],
        out_shape=[SDS((NB, N), F32)],
        params=_params(1, [((D, N), F32)], temp_bytes=_nbytes((D, N), BF16)))[0]


def _ada_bwd(c_all, gmod_all, *, n_col, name):
    NB, D = c_all.shape
    N = gmod_all.shape[1]

    def body(c_ref, g_ref, gw_ref, gb_ref):
        cv = c_ref[...]
        ca = (cv * _sigmoid(cv)).astype(BF16)
        first = pl.multiple_of(_lin(_my_pos()) * n_col, 128)
        gw_ref[...] = _dot_tn(ca, g_ref[:, pl.ds(first, n_col)].astype(BF16))
        gb_ref[...] = _rowsum(g_ref[...])

    full = lambda s: pl.BlockSpec(s, lambda i: (0,) * len(s))
    return _call(
        body, (c_all, gmod_all), name=name, grid=(1,),
        in_specs=[full((NB, D)), full((NB, N))], out_specs=[full((D, n_col)), full((1, N))],
        out_shape=[SDS((D, n_col), F32), SDS((1, N), F32)],
        params=_params(1, [((D, n_col), F32), ((NB, N), F32)]))


def kernel(x, c, w_ada, b_ada, norm_ffn1_g, ffn1_w_gate, ffn1_w_up, ffn1_w_down, norm_mix_g, w_in, attn_sinks, w_attn_o, conv_w_dw, conv_b_dw, conv_ln_g, conv_ln_b, w_conv_o, w_out, norm_ffn2_g, ffn2_w_gate, ffn2_w_up, ffn2_w_down, final_norm_g, loss_target, m_w_ada, m_b_ada, m_norm_ffn1_g, m_ffn1_w_gate, m_ffn1_w_up, m_ffn1_w_down, m_norm_mix_g, m_w_in, m_attn_sinks, m_w_attn_o, m_conv_w_dw, m_conv_b_dw, m_conv_ln_g, m_conv_ln_b, m_w_conv_o, m_w_out, m_norm_ffn2_g, m_ffn2_w_gate, m_ffn2_w_up, m_ffn2_w_down, m_final_norm_g, v_w_ada, v_b_ada, v_norm_ffn1_g, v_ffn1_w_gate, v_ffn1_w_up, v_ffn1_w_down, v_norm_mix_g, v_w_in, v_attn_sinks, v_w_attn_o, v_conv_w_dw, v_conv_b_dw, v_conv_ln_g, v_conv_ln_b, v_w_conv_o, v_w_out, v_norm_ffn2_g, v_ffn2_w_gate, v_ffn2_w_up, v_ffn2_w_down, v_final_norm_g):
    B, S, D = x.shape
    T = B * S
    QW = N_Q_HEADS * HEAD_DIM
    CC = conv_w_dw.shape[2] * N_DEV
    me = _lin(_my_pos())
    xf = x.reshape(T, D)
    tgt = loss_target.reshape(T, D)
    tm = min(512, S)
    kw = dict(seq=S, tm=tm)

    p_k, p_v, p_ca = QW, QW + KV_WIDTH, QW + 2 * KV_WIDTH
    p_cb, p_ga, p_gc = p_ca + CC, p_ca + 2 * CC, p_ca + 2 * CC + D

    def col_t(w):
        return w[0].T.astype(BF16)

    def row_b(w):
        return w[0].astype(BF16)

    def rows(g):
        return g.reshape(-1, g.shape[-1])

    def blocks8(g):
        return g.reshape(N_DEV, g.shape[0] // N_DEV, g.shape[1])

    def gather(*arrs, hbm_out=False):
        return _Comm([(a, "gather") for a in arrs], hbm_out=hbm_out)

    g_wg1, g_convw, g_c = _exchange(
        [(col_t(ffn1_w_gate), "gather"), (conv_w_dw[0], "gather"), (c, "gather")], name="gather_first")
    wg1 = rows(g_wg1)
    conv_w = g_convw.transpose(1, 0, 2).reshape(CONV_WIDTH, CC)
    c_all = g_c.reshape(N_DEV * B, D)

    n_col = N_MOD * D // N_DEV
    b_cols = lax.dynamic_slice(b_ada, (0, me * n_col), (1, n_col))
    mod_cols = _ada_fwd(c_all, w_ada[0], b_cols, name="ada_fwd")
    mod_mine = _exchange([(mod_cols.reshape(N_DEV, B, n_col), "scatter")], name="scatter_mod")[0]
    mod = mod_mine.transpose(1, 0, 2).reshape(B * N_MOD, 1, D)
    sh1, sc1, g1, sh2, sc2, g2, sh3, sc3, g3 = [_ModVec(mod, i) for i in range(N_MOD)]

    F = wg1.shape[0]
    tn_f = _pick(F, (1408, 1024, 512, 256))
    tn_in = _pick(w_in.shape[2] * N_DEV, (1792, 768, 512, 256))
    gate_blk = dict(ga_col=p_ga, gc_col=p_gc)
    att_blk = dict(q_blk=0, k_blk=p_k // KV_WIDTH, v_blk=p_v // KV_WIDTH)
    conv_kw = dict(seq=S, cw=256, a_col=p_ca, b_col=p_cb)

    cm = gather(col_t(ffn1_w_up))
    h1, (a1,) = _norm_mod_matmul(xf, norm_ffn1_g, sh1, sc1, [wg1], tn=tn_f, name="ffn1_gate", comm=cm, **kw)
    wu1 = rows(cm.out[0])
    cm = gather(row_b(ffn1_w_down), hbm_out=True)
    b1 = _matmul_nt(h1, wu1, tm=tm, tn=tn_f, name="ffn1_up", comm=cm)
    wd1 = rows(cm.out[0])
    cm = gather(col_t(w_in))
    x1, y1 = _ffn_down(a1, b1, wd1, xf, g1, name="ffn1_down", comm=cm, **kw)
    winp = rows(cm.out[0])
    cm = gather(row_b(w_attn_o), row_b(w_conv_o), row_b(w_out), col_t(ffn2_w_gate))
    h2, (projp,) = _norm_mod_matmul(x1, norm_mix_g, sh2, sc2, [winp], tn=tn_in, name="mix_in", comm=cm, **kw)
    wao, wco, wout, wg2 = [rows(o) for o in cm.out]
    cm = gather(col_t(ffn2_w_up), hbm_out=True)
    ao = _in_hbm(_attn_fwd(projp, attn_sinks, seq=S, name="attn_fwd", comm=cm, **att_blk))
    wu2 = rows(cm.out[0])
    cm = gather(row_b(ffn2_w_down), hbm_out=True)
    yc = _in_hbm(_conv_fwd(projp, conv_w, conv_b_dw, name="conv_fwd", comm=cm, **conv_kw))
    wd2 = rows(cm.out[0])
    x2, z, ya, ycv, cact, merged = _mix_out(ao, yc, projp, wao, wco, wout, x1, g2, conv_ln_g, conv_ln_b,
                                            name="mix_out", **gate_blk, **kw)
    h3, (a3, b3) = _norm_mod_matmul(x2, norm_ffn2_g, sh3, sc3, [wg2, wu2], tn=tn_f, name="ffn2_up", **kw)
    x3, y3 = _ffn_down(a3, b3, wd2, x2, g3, name="ffn2_down", **kw)
    dx3, loss_row, dgf = _final_loss(_in_hbm(x3), final_norm_g[None], _in_hbm(tgt), tm=tm, name="final_loss")
    dx3 = _in_hbm(dx3)

    parts = {}

    def pair(*gs):
        return [(blocks8(g), "pair") for g in gs]

    def cross(*rs):
        return [(r, "cross") for r in rs]

    def reduce_pairs(gs, staged, names):
        return [_pair_add(blocks8(g), s, name="pair_add_" + n) for g, s, n in zip(gs, staged, names)]

    dyb3, da3, db3, dg3 = _ffn_bwd_down(dx3, g3, y3, wd2, a3, b3, tn=tn_f, name="ffn2_bwd_down", **kw)
    gwd2 = _matmul_tn(a3, dyb3, gate=b3, name="gw_ffn2_down")
    cm = _Comm(pair(gwd2))
    dx2, dsh3, dsc3, dgn3 = _matmul_norm_mod_bwd([[da3], [db3]], [wg2, wu2], x2, norm_ffn2_g, sc3, dx3,
                                                 name="ffn2_bwd_up", out_dtype=GRAD_STREAM, comm=cm, **kw)
    r_wd2, = reduce_pairs([gwd2], cm.out, ["ffn2_w_down"])
    cm = _Comm(cross(r_wd2))
    gwg2 = _matmul_tn(da3, h3, name="gw_ffn2_gate", comm=cm)
    parts["ffn2_w_down"], = cm.out
    cm = _Comm(pair(gwg2))
    gwu2 = _matmul_tn(db3, h3, name="gw_ffn2_up", comm=cm)
    r_wg2, = reduce_pairs([gwg2], cm.out, ["ffn2_w_gate"])

    cm = _Comm(cross(r_wg2) + pair(gwu2))
    dzb, dyab, dycb, dga, dgc, dao, dyc, dg2, dlng, dlnb = _mix_out_bwd(
        dx2, g2, z, wout, projp, ya, ycv, wao, wco, yc, conv_ln_g, conv_ln_b, name="mix_out_bwd", comm=cm,
        **gate_blk, **kw)
    parts["ffn2_w_gate"] = cm.out[0]
    r_wu2, = reduce_pairs([gwu2], cm.out[1:], ["ffn2_w_up"])
    gwout = _matmul_tn(merged, dzb, name="gw_out")
    gwao = _matmul_tn(ao, dyab, name="gw_attn_o")
    gwco = _matmul_tn(cact, dycb, name="gw_conv_o")
    cm = _Comm(cross(r_wu2) + pair(gwout, gwao, gwco))
    dq, dk, dv, dsinks = _attn_bwd(projp, dao, attn_sinks, seq=S, name="attn_bwd", comm=cm, **att_blk)
    parts["ffn2_w_up"] = cm.out[0]
    r_mix = reduce_pairs([gwout, gwao, gwco], cm.out[1:], ["w_out", "w_attn_o", "w_conv_o"])
    cm = _Comm(cross(*r_mix))
    dca, dcb, dconvw, dconvb = _conv_bwd(dyc, projp, conv_w, name="conv_bwd", comm=cm, **conv_kw)
    dca, dcb = _in_hbm(dca), _in_hbm(dcb)
    parts["w_out"], parts["w_attn_o"], parts["w_conv_o"] = cm.out
    gwin = _matmul_tn_pieces([[dq], [dk, dv], [dca], [dcb], [dga], [dgc]], h2, name="gw_in")
    cm = _Comm(pair(gwin))
    dx1, dsh2, dsc2, dgn2 = _matmul_norm_mod_bwd([[dq, dk, dv, dca, dcb, dga, dgc]], [winp], x1, norm_mix_g, sc2, dx2,
                                                 name="mix_in_bwd", out_dtype=GRAD_STREAM, comm=cm, **kw)
    r_win, = reduce_pairs([gwin], cm.out, ["w_in"])

    cm = _Comm(cross(r_win))
    dyb1, da1, db1, dg1 = _ffn_bwd_down(dx1, g1, y1, wd1, a1, b1, tn=tn_f, name="ffn1_bwd_down", comm=cm,
                                              **kw)
    parts["w_in"], = cm.out
    gwd1 = _matmul_tn(a1, dyb1, gate=b1, name="gw_ffn1_down")
    cm = _Comm(pair(gwd1))
    gwg1 = _matmul_tn(da1, h1, name="gw_ffn1_gate", comm=cm)
    r_wd1, = reduce_pairs([gwd1], cm.out, ["ffn1_w_down"])
    cm = _Comm(cross(r_wd1) + pair(gwg1))
    gwu1 = _matmul_tn(db1, h1, name="gw_ffn1_up", comm=cm)
    parts["ffn1_w_down"] = cm.out[0]
    r_wg1, = reduce_pairs([gwg1], cm.out[1:], ["ffn1_w_gate"])
    r_wu1, = reduce_pairs([gwu1], _exchange(pair(gwu1), name="pair_last"), ["ffn1_w_up"])
    cm = _Comm(cross(r_wg1, r_wu1))
    dx0, dsh1, dsc1, dgn1 = _matmul_norm_mod_bwd([[da1], [db1]], [wg1, wu1], xf, norm_ffn1_g, sc1, dx1,
                                                 name="ffn1_bwd_up", out_dtype=F32, comm=cm, **kw)
    parts["ffn1_w_gate"], parts["ffn1_w_up"] = cm.out

    n_small = 8
    gmod = jnp.concatenate([dsh1, dsc1, dg1, dsh2, dsc2, dg2, dsh3, dsc3, dg3], axis=1).reshape(B, N_MOD * D)
    sink_row = jnp.pad(dsinks[:, :N_Q_HEADS], ((0, 0), (0, D - N_Q_HEADS)))
    loss_pad = jnp.pad(loss_row, ((0, 0), (0, D - loss_row.shape[1])))
    small = jnp.concatenate([dgn1, dgn2, dgn3, dgf, dconvb, dlng, dlnb, sink_row, dconvw, loss_pad], axis=0)
    small_all, gmod_all = _exchange([(small, "gather"), (gmod, "gather")], name="exchange_last")
    gsmall = _sum8(small_all, name="sum_small")
    loss = gsmall[n_small + CONV_WIDTH, 0]
    g_w_ada, g_b_ada = _ada_bwd(c_all, gmod_all.reshape(N_DEV * B, N_MOD * D), n_col=n_col, name="ada_bwd")
    g_conv_w = lax.dynamic_slice(gsmall[n_small:n_small + CONV_WIDTH], (0, me * (CC // N_DEV)),
                                 (CONV_WIDTH, CC // N_DEV))

    def col_update(name, w, m, v):
        outs = _sum8_adamw(parts[name], w[0].T, m[0].T, v[0].T, name="adamw_" + name)
        return tuple(o.T for o in outs)

    def row_update(name, w, m, v):
        return tuple(_sum8_adamw(parts[name], w[0], m[0], v[0], name="adamw_" + name))

    upd = {
        "ffn1_w_gate": col_update("ffn1_w_gate", ffn1_w_gate, m_ffn1_w_gate, v_ffn1_w_gate),
        "ffn1_w_up": col_update("ffn1_w_up", ffn1_w_up, m_ffn1_w_up, v_ffn1_w_up),
        "ffn1_w_down": row_update("ffn1_w_down", ffn1_w_down, m_ffn1_w_down, v_ffn1_w_down),
        "w_in": col_update("w_in", w_in, m_w_in, v_w_in),
        "w_attn_o": row_update("w_attn_o", w_attn_o, m_w_attn_o, v_w_attn_o),
        "w_conv_o": row_update("w_conv_o", w_conv_o, m_w_conv_o, v_w_conv_o),
        "w_out": row_update("w_out", w_out, m_w_out, v_w_out),
        "ffn2_w_gate": col_update("ffn2_w_gate", ffn2_w_gate, m_ffn2_w_gate, v_ffn2_w_gate),
        "ffn2_w_up": col_update("ffn2_w_up", ffn2_w_up, m_ffn2_w_up, v_ffn2_w_up),
        "ffn2_w_down": row_update("ffn2_w_down", ffn2_w_down, m_ffn2_w_down, v_ffn2_w_down),
        "w_ada": (g_w_ada,) + tuple(_adamw(g_w_ada, w_ada[0], m_w_ada[0], v_w_ada[0], name="adamw_w_ada")),
        "conv_w_dw": (g_conv_w,) + tuple(_adamw(g_conv_w, conv_w_dw[0], m_conv_w_dw[0], v_conv_w_dw[0],
                                                name="adamw_conv_w_dw")),
    }
    for k in upd:
        upd[k] = tuple(t[None] for t in upd[k])

    def pad_sinks(t):
        return jnp.pad(t, ((0, 0), (0, D - N_Q_HEADS)))

    def pack(f1, mix, f2, fin, cb, lg, lb, sinks, bada):
        return jnp.concatenate([f1, mix, f2, fin[None], cb, lg, lb, pad_sinks(sinks), bada.reshape(N_MOD, D)], axis=0)

    w_s = pack(norm_ffn1_g, norm_mix_g, norm_ffn2_g, final_norm_g, conv_b_dw, conv_ln_g, conv_ln_b, attn_sinks, b_ada)
    m_s = pack(m_norm_ffn1_g, m_norm_mix_g, m_norm_ffn2_g, m_final_norm_g, m_conv_b_dw, m_conv_ln_g, m_conv_ln_b,
               m_attn_sinks, m_b_ada)
    v_s = pack(v_norm_ffn1_g, v_norm_mix_g, v_norm_ffn2_g, v_final_norm_g, v_conv_b_dw, v_conv_ln_g, v_conv_ln_b,
               v_attn_sinks, v_b_ada)
    g_s = jnp.concatenate([gsmall[:n_small], g_b_ada.reshape(N_MOD, D)], axis=0)
    small_out = (g_s,) + tuple(_adamw(g_s, w_s, m_s, v_s, name="adamw_vectors"))

    def unpack(t):
        return {
            "norm_ffn1_g": t[0:1], "norm_mix_g": t[1:2], "norm_ffn2_g": t[2:3], "final_norm_g": t[3],
            "conv_b_dw": t[4:5], "conv_ln_g": t[5:6], "conv_ln_b": t[6:7], "attn_sinks": t[7:8, :N_Q_HEADS],
            "b_ada": t[n_small:n_small + N_MOD].reshape(1, N_MOD * D),
        }

    small_un = [unpack(t) for t in small_out]
    for k in small_un[0]:
        upd[k] = tuple(s[k] for s in small_un)

    order = ["w_ada", "b_ada", "norm_ffn1_g", "ffn1_w_gate", "ffn1_w_up", "ffn1_w_down", "norm_mix_g", "w_in",
             "attn_sinks", "w_attn_o", "conv_w_dw", "conv_b_dw", "conv_ln_g", "conv_ln_b", "w_conv_o", "w_out",
             "norm_ffn2_g", "ffn2_w_gate", "ffn2_w_up", "ffn2_w_down", "final_norm_g"]
    grad_x = dx0.reshape(B, S, D)
    return (loss, grad_x, *[upd[k][0] for k in order], *[upd[k][1] for k in order],
            *[upd[k][2] for k in order], *[upd[k][3] for k in order])
```

```python
import dataclasses

import jax
import jax.numpy as jnp
from jax import lax
from jax.experimental import pallas as pl
from jax.experimental.pallas import tpu as pltpu

F32 = jnp.float32
BF16 = jnp.bfloat16
SDS = jax.ShapeDtypeStruct
MESH = pl.DeviceIdType.MESH

N_DEV = 8
EPS = 1e-6
HEAD_DIM = 64
N_Q_HEADS = 16
N_KV_HEADS = 2
GQA_GROUP = N_Q_HEADS // N_KV_HEADS
KV_WIDTH = N_KV_HEADS * HEAD_DIM
ATT_BLOCK = 128
CONV_WIDTH = 31
CONV_HALO = 32
CONV_ROWS = 128
N_MOD = 9
FFN_RESIDUAL = 0.5
ADAM_LR = 0.001
ADAM_B1 = 0.9
ADAM_B2 = 0.999
ADAM_EPS = 1e-08
ADAM_WD = 0.01
ADAM_STEP = 10
NEG_BIG = -1e30
GRAD_STREAM = BF16

V7X_VMEM_BYTES = 64 * 2**20
VMEM_CAP = V7X_VMEM_BYTES - 8 * 2**20


def _nbytes(shape, dtype):
    n = 1
    for s in shape:
        n *= s
    return n * jnp.dtype(dtype).itemsize


def _params(n_axes, blocks, temp_bytes=0):
    need = 2 * sum(_nbytes(s, d) for s, d in blocks) + temp_bytes + 4 * 2**20
    return pltpu.CompilerParams(dimension_semantics=("arbitrary",) * n_axes,
                                vmem_limit_bytes=int(min(max(need, 16 * 2**20), VMEM_CAP)))


def _dot_nt(a, b):
    return lax.dot_general(a, b, (((1,), (1,)), ((), ())), preferred_element_type=F32)


def _dot_tn(a, b):
    return lax.dot_general(a, b, (((0,), (0,)), ((), ())), preferred_element_type=F32)


def _dot(a, b):
    return jnp.dot(a, b, preferred_element_type=F32)


def _sigmoid(x):
    return jax.nn.sigmoid(x)


def _rowsum(v):
    return jnp.sum(v, axis=0, keepdims=True)


def _acc(ref, val, first):
    @pl.when(first)
    def _():
        ref[...] = val

    @pl.when(jnp.logical_not(first))
    def _():
        ref[...] = ref[...] + val


def _norm_mod(xf, gn, sh, sc):
    rstd = lax.rsqrt(jnp.mean(xf * xf, axis=-1, keepdims=True) + EPS)
    xhat = xf * rstd
    yn = xhat * gn
    return yn * (1.0 + sc) + sh, xhat, rstd, yn


def _pick(n, cands):
    for c in cands:
        if n % c == 0:
            return c
    return n


def _my_pos():
    return lax.axis_index("x"), lax.axis_index("y"), lax.axis_index("c")


def _peer(pos, k):
    x, y, c = pos
    return ((1 - x) if k & 4 else x, (1 - y) if k & 2 else y, (1 - c) if k & 1 else c)


def _lin(pos):
    return 4 * pos[0] + 2 * pos[1] + pos[2]


def _in_hbm(a):
    return pltpu.with_memory_space_constraint(a, pltpu.HBM)


class _Comm:
    N_COPY = N_DEV - 1
    N_CHIP = N_DEV // 2

    def __init__(self, items, hbm_out=False):
        self.hbm_out = hbm_out
        self.arrs = [_in_hbm(a) if m == "cross" else a for a, m in items]
        self.modes = [m for _, m in items]
        self.n = len(items)
        self.out = None

    def out_shape(self):
        def shape(a, m):
            return {"gather": (N_DEV,) + a.shape, "scatter": a.shape, "pair": (self.N_CHIP,) + a.shape[1:],
                    "cross": a.shape}[m]
        kind = pltpu.HBM if self.hbm_out else SDS
        return [kind(shape(a, m), a.dtype) for a, m in zip(self.arrs, self.modes)]

    def scratch(self):
        return [pltpu.SemaphoreType.DMA((self.n * self.N_COPY,)), pltpu.SemaphoreType.DMA((self.n * self.N_COPY,)),
                pltpu.SemaphoreType.DMA((self.n,))]

    def collective_id(self):
        modes = set(self.modes)
        if "scatter" in modes:
            return 3
        d2d, ici = bool(modes & {"gather", "pair"}), bool(modes & {"gather", "cross"})
        return {(True, False): 0, (False, True): 1, (True, True): 2}[(d2d, ici)]

    def barrier(self):
        x, y, c = _my_pos()
        peers = {0: [(x, y, 1 - c)],
                 1: [(1 - x, y, c), (x, 1 - y, c), (1 - x, 1 - y, c)],
                 2: [(x, y, 1 - c), (1 - x, y, c), (x, 1 - y, c), (1 - x, 1 - y, c)],
                 3: [_peer((x, y, c), k) for k in range(1, N_DEV)]}[self.collective_id()]
        sem = pltpu.get_barrier_semaphore()
        for p in peers:
            pl.semaphore_signal(sem, inc=1, device_id=p, device_id_type=MESH)
        pl.semaphore_wait(sem, len(peers))

    def _plan(self, mode, me):
        x, y, c = me
        sib = (x, y, 1 - c)
        chips = [(1 - x, y), (x, 1 - y), (1 - x, 1 - y)]

        def chip_lin(ch):
            return 2 * ch[0] + ch[1]

        if mode == "scatter":
            peers = [_peer(me, k + 1) for k in range(self.N_COPY)]
            return [(p, ("in", _lin(p)), _lin(me), _lin(p), None) for p in peers], (_lin(me), _lin(me))
        if mode == "gather":
            same = [(*ch, c) for ch in chips]
            other = [(*ch, 1 - c) for ch in chips]
            copies = [(sib, ("in", None), _lin(me), _lin(sib), None)]
            copies += [(p, ("in", None), _lin(me), _lin(p), None) for p in same]
            copies += [(sib, ("out", _lin(p)), _lin(p), _lin(o), 1 + j) for j, (p, o) in enumerate(zip(same, other))]
            return copies, (None, _lin(me))
        if mode == "pair":
            return [(sib, ("in", 2 * q + 1 - c), q, q, None) for q in range(self.N_CHIP)], None
        if mode == "cross":
            mine = chip_lin((x, y))
            return ([((*ch, c), ("in", chip_lin(ch)), mine, chip_lin(ch), None) for ch in chips], (mine, mine))
        raise ValueError(mode)

    def _copy(self, refs, me, i, k, recv):
        srcs, outs, (send_sems, recv_sems, _) = refs
        peer, (where, slot), send_slot, recv_slot, _ = self._plan(self.modes[i], me)[0][k]
        src = srcs[i] if where == "in" else outs[i]
        src = src if slot is None else src.at[slot]
        sem = i * self.N_COPY + k
        return pltpu.make_async_remote_copy(
            src_ref=src, dst_ref=outs[i].at[recv_slot if recv else send_slot], send_sem=send_sems.at[sem],
            recv_sem=recv_sems.at[sem], device_id=peer, device_id_type=MESH)

    def _local(self, refs, me, i):
        srcs, outs, (_, _, loc_sems) = refs
        local = self._plan(self.modes[i], me)[1]
        if local is None:
            return None
        own = srcs[i] if local[0] is None else srcs[i].at[local[0]]
        return pltpu.make_async_copy(own, outs[i].at[local[1]], loc_sems.at[i])

    def start(self, refs):
        me = _my_pos()
        for i in range(self.n):
            local = self._local(refs, me, i)
            if local is not None:
                local.start()
            for k, cp in enumerate(self._plan(self.modes[i], me)[0]):
                if cp[4] is None:
                    self._copy(refs, me, i, k, False).start()

    def forward(self, refs):
        me = _my_pos()
        for i in range(self.n):
            for k, cp in enumerate(self._plan(self.modes[i], me)[0]):
                if cp[4] is not None:
                    self._copy(refs, me, i, cp[4], True).wait_recv()
                    self._copy(refs, me, i, k, False).start()

    def finish(self, refs):
        me = _my_pos()
        plans = [self._plan(m, me)[0] for m in self.modes]
        for i in range(self.n):
            passed_on = [cp[4] for cp in plans[i] if cp[4] is not None]
            for k in range(len(plans[i])):
                if k not in passed_on:
                    self._copy(refs, me, i, k, True).wait_recv()
                self._copy(refs, me, i, k, False).wait_send()
            local = self._local(refs, me, i)
            if local is not None:
                local.wait()


_ANY = pl.BlockSpec(memory_space=pl.ANY)


def _call(body, args, *, name, grid, in_specs, out_specs, out_shape, params, scratch_shapes=(), comm=None,
          hbm_out=()):
    in_specs, out_specs, out_shape = list(in_specs), list(out_specs), list(out_shape)
    scratch_shapes = list(scratch_shapes)
    for k in hbm_out:
        out_shape[k] = pltpu.HBM(out_shape[k].shape, out_shape[k].dtype)
    if comm is None:
        return list(pl.pallas_call(body, name=name, grid=grid, in_specs=in_specs, out_specs=out_specs,
                                   out_shape=out_shape, scratch_shapes=scratch_shapes, compiler_params=params)(*args))
    n_in, n_out, n_scr, nc = len(in_specs), len(out_specs), len(scratch_shapes), comm.n
    n_steps = 1
    for g in grid:
        n_steps *= g

    def hosted(*refs):
        ins, c_in = refs[:n_in], refs[n_in:n_in + nc]
        outs = refs[n_in + nc:n_in + nc + n_out]
        c_out = refs[n_in + nc + n_out:n_in + 2 * nc + n_out]
        scr = refs[n_in + 2 * nc + n_out:n_in + 2 * nc + n_out + n_scr]
        sems = refs[n_in + 2 * nc + n_out + n_scr:]
        step = pl.program_id(0)
        for d in range(1, len(grid)):
            step = step * grid[d] + pl.program_id(d)
        c_refs = (c_in, c_out, sems)

        @pl.when(step == 0)
        def _():
            comm.barrier()
            comm.start(c_refs)

        if n_steps >= 3:
            @pl.when(step == n_steps - 2)
            def _():
                comm.forward(c_refs)

        body(*ins, *outs, *scr)

        @pl.when(step == n_steps - 1)
        def _():
            if n_steps < 3:
                comm.forward(c_refs)
            comm.finish(c_refs)

    res = pl.pallas_call(
        hosted, name=name, grid=grid, in_specs=in_specs + [_ANY] * nc, out_specs=out_specs + [_ANY] * nc,
        out_shape=out_shape + comm.out_shape(), scratch_shapes=scratch_shapes + comm.scratch(),
        compiler_params=dataclasses.replace(params, collective_id=comm.collective_id()))(*args, *comm.arrs)
    comm.out = list(res[n_out:])
    return list(res[:n_out])


def _exchange(items, *, name):
    comm = _Comm(items)

    def body(*refs):
        r = (refs[:comm.n], refs[comm.n:2 * comm.n], refs[2 * comm.n:])
        comm.barrier()
        comm.start(r)
        comm.forward(r)
        comm.finish(r)

    return list(pl.pallas_call(body, name=name, out_shape=comm.out_shape(), in_specs=[_ANY] * comm.n,
                               out_specs=[_ANY] * comm.n, scratch_shapes=comm.scratch(),
                               compiler_params=pltpu.CompilerParams(collective_id=comm.collective_id()))(*comm.arrs))


class _ModVec:
    def __init__(self, arr, idx):
        self.arr, self.idx = arr, idx

    def spec(self, tps, n_axes):
        idx, blk = self.idx, (1, 1, self.arr.shape[2])
        if n_axes == 1:
            return pl.BlockSpec(blk, lambda i: (i // tps * N_MOD + idx, 0, 0))
        return pl.BlockSpec(blk, lambda i, j: (i // tps * N_MOD + idx, 0, 0))


def _norm_mod_matmul(x, gn, sh, sc, wts, *, seq, tm, tn, name, comm=None):
    T, D = x.shape
    N = wts[0].shape[0]
    nw = len(wts)
    tps = seq // tm

    def body(x_ref, gn_ref, sh_ref, sc_ref, *rest):
        w_refs, h_ref, o_refs = rest[:nw], rest[nw], rest[nw + 1:]

        @pl.when(pl.program_id(1) == 0)
        def _():
            h_ref[...] = _norm_mod(x_ref[...], gn_ref[...], sh_ref[0], sc_ref[0])[0].astype(BF16)

        h = h_ref[...]
        for w_ref, o_ref in zip(w_refs, o_refs):
            o_ref[...] = _dot_nt(h, w_ref[...]).astype(o_ref.dtype)

    row = pl.BlockSpec((tm, D), lambda i, j: (i, 0))
    vec = pl.BlockSpec((1, D), lambda i, j: (0, 0))
    wspec = pl.BlockSpec((tn, D), lambda i, j: (j, 0))
    ospec = pl.BlockSpec((tm, tn), lambda i, j: (i, j))
    blocks = [((tm, D), F32), ((tm, D), BF16)] + [((tn, D), BF16), ((tm, tn), BF16)] * nw
    outs = _call(
        body, (x, gn, sh.arr, sc.arr, *wts), name=name, grid=(T // tm, N // tn),
        in_specs=[row, vec, sh.spec(tps, 2), sc.spec(tps, 2)] + [wspec] * nw,
        out_specs=[row] + [ospec] * nw,
        out_shape=[SDS((T, D), BF16)] + [SDS((T, N), BF16)] * nw,
        params=_params(2, blocks, temp_bytes=2 * _nbytes((tm, tn), F32) + 3 * _nbytes((tm, D), F32)), comm=comm)
    return outs[0], outs[1:]


def _matmul_nt(h, w, *, tm, tn, name, comm=None):
    T, D = h.shape
    N = w.shape[0]

    def body(h_ref, w_ref, o_ref):
        o_ref[...] = _dot_nt(h_ref[...], w_ref[...]).astype(o_ref.dtype)

    blocks = [((tm, D), BF16), ((tn, D), BF16), ((tm, tn), BF16)]
    return _call(
        body, (h, w), name=name, grid=(T // tm, N // tn),
        in_specs=[pl.BlockSpec((tm, D), lambda i, j: (i, 0)), pl.BlockSpec((tn, D), lambda i, j: (j, 0))],
        out_specs=[pl.BlockSpec((tm, tn), lambda i, j: (i, j))],
        out_shape=[SDS((T, N), BF16)],
        params=_params(2, blocks, temp_bytes=2 * _nbytes((tm, tn), F32)), comm=comm)[0]


def _ffn_down(a, b, wd, x, g, *, seq, tm, name, comm=None):
    T, F = a.shape
    D = wd.shape[1]
    tps = seq // tm

    def body(a_ref, b_ref, wd_ref, x_ref, g_ref, xo_ref, y_ref):
        af = a_ref[...].astype(F32)
        act = (af * _sigmoid(af) * b_ref[...].astype(F32)).astype(BF16)
        y = _dot(act, wd_ref[...])
        xo_ref[...] = x_ref[...] + (FFN_RESIDUAL * g_ref[0]) * y
        y_ref[...] = y.astype(BF16)

    wide = pl.BlockSpec((tm, F), lambda i: (i, 0))
    row = pl.BlockSpec((tm, D), lambda i: (i, 0))
    wspec = pl.BlockSpec((F, D), lambda i: (0, 0))
    blocks = [((tm, F), BF16)] * 2 + [((F, D), BF16), ((tm, D), F32), ((tm, D), F32), ((tm, D), BF16)]
    return _call(
        body, (a, b, wd, x, g.arr), name=name, grid=(T // tm,),
        in_specs=[wide, wide, wspec, row, g.spec(tps, 1)], out_specs=[row, row],
        out_shape=[SDS((T, D), F32), SDS((T, D), BF16)],
        params=_params(1, blocks, temp_bytes=3 * _nbytes((tm, F), F32)), comm=comm)


def _final_loss(x, gf, tgt, *, tm, name):
    T, D = x.shape
    nt = T // tm

    def body(x_ref, gf_ref, t_ref, dx_ref, loss_ref, dgf_ref, lacc):
        i = pl.program_id(0)
        xf = x_ref[...]
        gfv = gf_ref[...]
        rstd = lax.rsqrt(jnp.mean(xf * xf, axis=-1, keepdims=True) + EPS)
        xhat = xf * rstd
        err = xhat * gfv - t_ref[...]
        dy = err * (1.0 / D)
        dxhat = dy * gfv
        dx_ref[...] = (rstd * (dxhat - xhat * jnp.mean(dxhat * xhat, axis=-1, keepdims=True))).astype(dx_ref.dtype)
        _acc(dgf_ref, _rowsum(dy * xhat), i == 0)
        _acc(lacc, _rowsum(err * err), i == 0)

        @pl.when(i == nt - 1)
        def _():
            loss_ref[...] = jnp.broadcast_to((0.5 / D) * jnp.sum(lacc[...]), loss_ref.shape)

    row = pl.BlockSpec((tm, D), lambda i: (i, 0))
    vec = pl.BlockSpec((1, D), lambda i: (0, 0))
    lspec = pl.BlockSpec((1, 128), lambda i: (0, 0))
    blocks = [((tm, D), F32)] * 3
    return _call(
        body, (x, gf, tgt), name=name, grid=(nt,),
        in_specs=[row, vec, row], out_specs=[row, lspec, vec],
        out_shape=[SDS((T, D), GRAD_STREAM), SDS((1, 128), F32), SDS((1, D), F32)],
        scratch_shapes=[pltpu.VMEM((1, D), F32)],
        params=_params(1, blocks, temp_bytes=4 * _nbytes((tm, D), F32)), hbm_out=(0,))


def _ffn_bwd_down(dxo, g, y, wd, a, b, *, seq, tm, tn, name, comm=None):
    T, F = a.shape
    D = wd.shape[1]
    tps = seq // tm
    nb = T // seq

    def body(dxo_ref, g_ref, y_ref, wd_ref, a_ref, b_ref, dyb_ref, da_ref, db_ref, dg_ref):
        i = pl.program_id(0)

        @pl.when(pl.program_id(1) == 0)
        def _():
            dx = dxo_ref[...].astype(F32)
            dyb_ref[...] = ((FFN_RESIDUAL * g_ref[0]) * dx).astype(BF16)
            part = _rowsum(FFN_RESIDUAL * dx * y_ref[...].astype(F32))
            _acc(dg_ref, part[None], i % tps == 0)

        dact = _dot_nt(dyb_ref[...], wd_ref[...])
        af = a_ref[...].astype(F32)
        bf = b_ref[...].astype(F32)
        sg = _sigmoid(af)
        silu = af * sg
        da_ref[...] = (dact * bf * (sg + silu * (1.0 - sg))).astype(BF16)
        db_ref[...] = (dact * silu).astype(BF16)

    row = pl.BlockSpec((tm, D), lambda i, j: (i, 0))
    per_b = pl.BlockSpec((1, 1, D), lambda i, j: (i // tps, 0, 0))
    wspec = pl.BlockSpec((tn, D), lambda i, j: (j, 0))
    chunk = pl.BlockSpec((tm, tn), lambda i, j: (i, j))
    blocks = [((tm, D), F32), ((tm, D), BF16), ((tn, D), BF16), ((tm, D), BF16)] + [((tm, tn), BF16)] * 4
    return _call(
        body, (dxo, g.arr, y, wd, a, b), name=name, grid=(T // tm, F // tn),
        in_specs=[row, g.spec(tps, 2), row, wspec, chunk, chunk],
        out_specs=[row, chunk, chunk, per_b],
        out_shape=[SDS((T, D), BF16)] + [SDS((T, F), BF16)] * 2 + [SDS((nb, 1, D), F32)],
        params=_params(2, blocks, temp_bytes=6 * _nbytes((tm, tn), F32)), comm=comm)


def _matmul_norm_mod_bwd(ds, ws, x, gn, sc, dxo, *, seq, tm, name, out_dtype, comm=None):
    T, D = x.shape
    nk = len(ws)
    sizes = [len(g) for g in ds]
    ds = [d for g in ds for d in g]
    tps = seq // tm
    nb = T // seq

    def body(*refs):
        w_refs = refs[len(ds):len(ds) + nk]
        x_ref, gn_ref, sc_ref, dxo_ref, dxi_ref, dsh_ref, dsc_ref, dgn_ref = refs[len(ds) + nk:]
        i = pl.program_id(0)
        dh, at = None, 0
        for n, w_ref in zip(sizes, w_refs):
            pieces = [r[...] for r in refs[at:at + n]]
            at += n
            part = _dot(pieces[0] if n == 1 else jnp.concatenate(pieces, axis=1), w_ref[...])
            dh = part if dh is None else dh + part
        gnv = gn_ref[...]
        scv = sc_ref[0]
        _, xhat, rstd, yn = _norm_mod(x_ref[...], gnv, 0.0, scv)
        dyn = dh * (1.0 + scv)
        dxhat = dyn * gnv
        dxi_ref[...] = (dxo_ref[...].astype(F32)
                        + rstd * (dxhat - xhat * jnp.mean(dxhat * xhat, axis=-1, keepdims=True))).astype(out_dtype)
        first_of_seq = i % tps == 0
        _acc(dsh_ref, _rowsum(dh)[None], first_of_seq)
        _acc(dsc_ref, _rowsum(dh * yn)[None], first_of_seq)
        _acc(dgn_ref, _rowsum(dyn * xhat), i == 0)

    row = pl.BlockSpec((tm, D), lambda i: (i, 0))
    vec = pl.BlockSpec((1, D), lambda i: (0, 0))
    per_b = pl.BlockSpec((1, 1, D), lambda i: (i // tps, 0, 0))
    d_specs = [pl.BlockSpec((tm, d.shape[1]), lambda i: (i, 0)) for d in ds]
    w_specs = [pl.BlockSpec(w.shape, lambda i: (0, 0)) for w in ws]
    blocks = ([((tm, d.shape[1]), BF16) for d in ds] + [(w.shape, BF16) for w in ws] + [((tm, D), F32)] * 3)
    return _call(
        body, (*ds, *ws, x, gn, sc.arr, dxo), name=name, grid=(T // tm,),
        in_specs=d_specs + w_specs + [row, vec, sc.spec(tps, 1), row],
        out_specs=[row, per_b, per_b, vec],
        out_shape=[SDS((T, D), out_dtype), SDS((nb, 1, D), F32), SDS((nb, 1, D), F32), SDS((1, D), F32)],
        params=_params(1, blocks, temp_bytes=6 * _nbytes((tm, D), F32)), comm=comm)


def _layernorm_silu(yc, lg, lb):
    mu = jnp.mean(yc, axis=-1, keepdims=True)
    cen = yc - mu
    rstd = lax.rsqrt(jnp.mean(cen * cen, axis=-1, keepdims=True) + EPS)
    xh = cen * rstd
    l = xh * lg + lb
    s = _sigmoid(l)
    return l * s, xh, rstd, l, s


GATE_W = 256


def _gate_specs(tm, D, col):
    return [pl.BlockSpec((tm, GATE_W), lambda i, blk=col // GATE_W + t: (i, blk)) for t in range(D // GATE_W)]


def _gate(refs):
    return jnp.concatenate([r[...] for r in refs], axis=1).astype(F32)


def _mix_out(ao, yc, proj, wao, wco, wout, x1, g2, lg, lb, *, seq, tm, ga_col, gc_col, name, comm=None):
    T, D = x1.shape
    tps = seq // tm
    ng = D // GATE_W

    def body(ao_ref, yc_ref, *rest):
        ga_refs, gc_refs = rest[:ng], rest[ng:2 * ng]
        (wao_ref, wco_ref, wout_ref, x1_ref, g2_ref, lg_ref, lb_ref,
         x2_ref, z_ref, ya_ref, ycv_ref, cact_ref, mrg_ref) = rest[2 * ng:]
        ya = _dot(ao_ref[...], wao_ref[...])
        cact = _layernorm_silu(yc_ref[...], lg_ref[...], lb_ref[...])[0].astype(BF16)
        ycv = _dot(cact, wco_ref[...])
        merged = (_sigmoid(_gate(ga_refs)) * ya + _sigmoid(_gate(gc_refs)) * ycv).astype(BF16)
        z = _dot(merged, wout_ref[...])
        x2_ref[...] = x1_ref[...] + g2_ref[0] * z
        z_ref[...] = z.astype(BF16)
        ya_ref[...] = ya.astype(BF16)
        ycv_ref[...] = ycv.astype(BF16)
        cact_ref[...] = cact
        mrg_ref[...] = merged

    row = pl.BlockSpec((tm, D), lambda i: (i, 0))
    vec = pl.BlockSpec((1, D), lambda i: (0, 0))
    wspec = pl.BlockSpec((D, D), lambda i: (0, 0))
    gates = _gate_specs(tm, D, ga_col) + _gate_specs(tm, D, gc_col)
    blocks = ([((tm, D), BF16), ((tm, D), F32), ((tm, D), BF16), ((tm, D), BF16)] + [((D, D), BF16)] * 3
              + [((tm, D), F32)] * 2 + [((tm, D), BF16)] * 5)
    return _call(
        body, (ao, yc, *[proj] * (2 * ng), wao, wco, wout, x1, g2.arr, lg, lb), name=name, grid=(T // tm,),
        in_specs=[row, row, *gates, wspec, wspec, wspec, row, g2.spec(tps, 1), vec, vec],
        out_specs=[row] * 6,
        out_shape=[SDS((T, D), F32)] + [SDS((T, D), BF16)] * 5,
        params=_params(1, blocks, temp_bytes=8 * _nbytes((tm, D), F32)), comm=comm)


def _mix_out_bwd(dx2, g2, z, wout, proj, ya, ycv, wao, wco, yc, lg, lb, *, seq, tm, ga_col, gc_col, name,
                 comm=None):
    T, D = dx2.shape
    tps = seq // tm
    nb = T // seq
    ng = D // GATE_W

    def body(dx2_ref, g2_ref, z_ref, wout_ref, *rest):
        ga_refs, gc_refs = rest[:ng], rest[ng:2 * ng]
        (ya_ref, ycv_ref, wao_ref, wco_ref, yc_ref, lg_ref, lb_ref, dz_ref, dya_ref, dycv_ref, dga_ref, dgc_ref,
         dao_ref, dyc_ref, dg2_ref, dlg_ref, dlb_ref) = rest[2 * ng:]
        i = pl.program_id(0)
        dx = dx2_ref[...].astype(F32)
        _acc(dg2_ref, _rowsum(dx * z_ref[...].astype(F32))[None], i % tps == 0)
        dzb = (g2_ref[0] * dx).astype(BF16)
        dz_ref[...] = dzb
        dmerged = _dot_nt(dzb, wout_ref[...])
        sa = _sigmoid(_gate(ga_refs))
        sc_ = _sigmoid(_gate(gc_refs))
        dya = (dmerged * sa).astype(BF16)
        dycv = (dmerged * sc_).astype(BF16)
        dya_ref[...] = dya
        dycv_ref[...] = dycv
        dga_ref[...] = (dmerged * ya_ref[...].astype(F32) * (sa * (1.0 - sa))).astype(BF16)
        dgc_ref[...] = (dmerged * ycv_ref[...].astype(F32) * (sc_ * (1.0 - sc_))).astype(BF16)
        dao_ref[...] = _dot_nt(dya, wao_ref[...]).astype(BF16)
        dcact = _dot_nt(dycv, wco_ref[...])
        lgv = lg_ref[...]
        _, xh, rstd, l, s = _layernorm_silu(yc_ref[...], lgv, lb_ref[...])
        dl = dcact * (s * (1.0 + l * (1.0 - s)))
        _acc(dlb_ref, _rowsum(dl), i == 0)
        _acc(dlg_ref, _rowsum(dl * xh), i == 0)
        dxh = dl * lgv
        dyc_ref[...] = rstd * (dxh - jnp.mean(dxh, axis=-1, keepdims=True)
                               - xh * jnp.mean(dxh * xh, axis=-1, keepdims=True))

    row = pl.BlockSpec((tm, D), lambda i: (i, 0))
    vec = pl.BlockSpec((1, D), lambda i: (0, 0))
    per_b = pl.BlockSpec((1, 1, D), lambda i: (i // tps, 0, 0))
    wspec = pl.BlockSpec((D, D), lambda i: (0, 0))
    gates = _gate_specs(tm, D, ga_col) + _gate_specs(tm, D, gc_col)
    blocks = ([((tm, D), F32)] * 3 + [((tm, D), BF16)] * 11 + [((D, D), BF16)] * 3)
    return _call(
        body, (dx2, g2.arr, z, wout, *[proj] * (2 * ng), ya, ycv, wao, wco, yc, lg, lb), name=name,
        grid=(T // tm,),
        in_specs=[row, g2.spec(tps, 1), row, wspec, *gates, row, row, wspec, wspec, row, vec, vec],
        out_specs=[row] * 7 + [per_b, vec, vec],
        out_shape=[SDS((T, D), BF16)] * 6 + [SDS((T, D), F32), SDS((nb, 1, D), F32), SDS((1, D), F32),
                                             SDS((1, D), F32)],
        params=_params(1, blocks, temp_bytes=10 * _nbytes((tm, D), F32)), comm=comm)


Q_BLOCK = 64
BAND = Q_BLOCK + ATT_BLOCK
GROUP_ROWS = GQA_GROUP * Q_BLOCK
PAIR_W = 2 * HEAD_DIM
GROUP_W = GQA_GROUP * HEAD_DIM


def _lane_lo():
    return lax.broadcasted_iota(jnp.int32, (1, PAIR_W), 1) < HEAD_DIM


def _band_bias():
    sj = lax.broadcasted_iota(jnp.int32, (BAND, GROUP_ROWS), 0)
    qi = lax.broadcasted_iota(jnp.int32, (BAND, GROUP_ROWS), 1) & (Q_BLOCK - 1)
    rel = qi + ATT_BLOCK - sj
    bias = jnp.where(jnp.logical_and(rel >= 0, rel < ATT_BLOCK), 0.0, NEG_BIG)
    return bias, lax.broadcasted_iota(jnp.int32, (BAND, 1), 0)


def _block_bias(bias0, key_index, r0):
    return bias0 + jnp.where(key_index + r0 < ATT_BLOCK, NEG_BIG, 0.0)


def _dup_heads(src_ref, dst, seq):
    x = src_ref[...]
    i = lax.broadcasted_iota(jnp.int32, (KV_WIDTH, PAIR_W), 0)
    j = lax.broadcasted_iota(jnp.int32, (KV_WIDTH, PAIR_W), 1) & (HEAD_DIM - 1)
    for g in range(N_KV_HEADS):
        sel = jnp.where(i == j + g * HEAD_DIM, 1.0, 0.0).astype(BF16)
        dst[g, pl.ds(0, ATT_BLOCK), :] = jnp.zeros((ATT_BLOCK, PAIR_W), BF16)
        dst[g, pl.ds(ATT_BLOCK, seq), :] = _dot(x, sel).astype(BF16)


def _stack_heads(blk, g, lo):
    parts = []
    for p in range(GQA_GROUP // 2):
        pair = blk[:, g * GROUP_W + p * PAIR_W:g * GROUP_W + (p + 1) * PAIR_W]
        parts += [jnp.where(lo, pair, jnp.zeros_like(pair)), jnp.where(lo, jnp.zeros_like(pair), pair)]
    return jnp.concatenate(parts, axis=0)


def _unstack_heads(full, ref, r0, g, lo):
    for p in range(GQA_GROUP // 2):
        even = full[(2 * p) * Q_BLOCK:(2 * p + 1) * Q_BLOCK, :]
        odd = full[(2 * p + 1) * Q_BLOCK:(2 * p + 2) * Q_BLOCK, :]
        ref[pl.ds(r0, Q_BLOCK), g * GROUP_W + p * PAIR_W:g * GROUP_W + (p + 1) * PAIR_W] = (
            jnp.where(lo, even, odd).astype(ref.dtype))


def _sink_row(sink_ref, g):
    return jnp.concatenate([jnp.full((1, Q_BLOCK), sink_ref[0, g * GQA_GROUP + h], F32)
                            for h in range(GQA_GROUP)], axis=1)


def _group_probs(qs, k2, bias, sink):
    s = _dot_nt(k2, qs) * (HEAD_DIM ** -0.5) + bias
    m = jnp.maximum(jnp.max(s, axis=0, keepdims=True), sink)
    p = jnp.exp(s - m)
    psink = jnp.exp(sink - m)
    inv = 1.0 / (jnp.sum(p, axis=0, keepdims=True) + psink)
    return p * inv, psink * inv


def _attn_fwd(projp, sinks, *, seq, q_blk, k_blk, v_blk, name, comm=None):
    T = projp.shape[0]
    QW = N_Q_HEADS * HEAD_DIM
    nblk = seq // Q_BLOCK

    def body(q_ref, k_ref, v_ref, sink_ref, o_ref, k2s, v2s):
        _dup_heads(k_ref, k2s, seq)
        _dup_heads(v_ref, v2s, seq)
        lo = _lane_lo()
        bias0, key_index = _band_bias()
        sink_rows = [_sink_row(sink_ref, g) for g in range(N_KV_HEADS)]

        def blk(n, carry):
            r0 = pl.multiple_of(n * Q_BLOCK, Q_BLOCK)
            band = pl.ds(r0, BAND)
            qb = q_ref[pl.ds(r0, Q_BLOCK), :]
            bias = _block_bias(bias0, key_index, r0)
            for g in range(N_KV_HEADS):
                probs_t, _ = _group_probs(_stack_heads(qb, g, lo), k2s[g, band, :], bias, sink_rows[g])
                _unstack_heads(_dot_tn(probs_t.astype(BF16), v2s[g, band, :]), o_ref, r0, g, lo)
            return carry

        lax.fori_loop(0, nblk, blk, 0, unroll=2)

    blocks = [((seq, QW), BF16)] * 2 + [((seq, KV_WIDTH), BF16)] * 2
    return _call(
        body, (projp, projp, projp, sinks), name=name, grid=(T // seq,),
        in_specs=[pl.BlockSpec((seq, QW), lambda b: (b, q_blk)),
                  pl.BlockSpec((seq, KV_WIDTH), lambda b: (b, k_blk)),
                  pl.BlockSpec((seq, KV_WIDTH), lambda b: (b, v_blk)),
                  pl.BlockSpec(memory_space=pltpu.SMEM)],
        out_specs=[pl.BlockSpec((seq, QW), lambda b: (b, 0))],
        out_shape=[SDS((T, QW), BF16)],
        scratch_shapes=[pltpu.VMEM((N_KV_HEADS, seq + ATT_BLOCK, PAIR_W), BF16)] * 2,
        params=_params(1, blocks, temp_bytes=2 * _nbytes((N_KV_HEADS, seq + ATT_BLOCK, PAIR_W), BF16)
                       + 8 * _nbytes((BAND, GROUP_ROWS), F32)), comm=comm, hbm_out=(0,))[0]


def _attn_bwd(projp, dao, sinks, *, seq, q_blk, k_blk, v_blk, name, comm=None):
    T = projp.shape[0]
    QW = N_Q_HEADS * HEAD_DIM
    assert seq % (2 * Q_BLOCK) == 0
    nblk = seq // Q_BLOCK

    def body(q_ref, k_ref, v_ref, do_ref, sink_ref, dq_ref, dk_ref, dv_ref, dsink_ref, k2s, v2s, dkacc, dvacc):
        _dup_heads(k_ref, k2s, seq)
        _dup_heads(v_ref, v2s, seq)
        dkacc[...] = jnp.zeros(dkacc.shape, F32)
        dvacc[...] = jnp.zeros(dvacc.shape, F32)
        lane = lax.broadcasted_iota(jnp.int32, (1, PAIR_W), 1)
        lo = lane < HEAD_DIM
        bias0, key_index = _band_bias()
        sink_rows = [_sink_row(sink_ref, g) for g in range(N_KV_HEADS)]

        def blk(n, tsinks):
            tsinks = list(tsinks)
            r0 = pl.multiple_of(n * Q_BLOCK, Q_BLOCK)
            band = pl.ds(r0, BAND)
            qb = q_ref[pl.ds(r0, Q_BLOCK), :]
            dob = do_ref[pl.ds(r0, Q_BLOCK), :]
            bias = _block_bias(bias0, key_index, r0)
            for g in range(N_KV_HEADS):
                qs = _stack_heads(qb, g, lo)
                dos = _stack_heads(dob, g, lo)
                k2 = k2s[g, band, :]
                v2 = v2s[g, band, :]
                probs_t, psink = _group_probs(qs, k2, bias, sink_rows[g])
                dp_t = _dot_nt(v2, dos)
                delta = jnp.sum(probs_t * dp_t, axis=0, keepdims=True)
                ds_t = (probs_t * (dp_t - delta) * (HEAD_DIM ** -0.5)).astype(BF16)
                tsinks[g] = tsinks[g] + psink * delta
                _unstack_heads(_dot_tn(ds_t, k2), dq_ref, r0, g, lo)
                dkacc[g, band, :] = dkacc[g, band, :] + _dot(ds_t, qs)
                dvacc[g, band, :] = dvacc[g, band, :] + _dot(probs_t.astype(BF16), dos)
            return tuple(tsinks)

        def two_blocks(m, tsinks):
            return blk(2 * m + 1, blk(2 * m, tsinks))

        tsinks = lax.fori_loop(0, nblk // 2, two_blocks, (jnp.zeros((1, GROUP_ROWS), F32),) * N_KV_HEADS)
        dsink = jnp.zeros((1, PAIR_W), F32)
        for g in range(N_KV_HEADS):
            for h in range(GQA_GROUP):
                dsink = dsink + jnp.where(lane == g * GQA_GROUP + h,
                                          -jnp.sum(tsinks[g][:, h * Q_BLOCK:(h + 1) * Q_BLOCK]), 0.0)
        _acc(dsink_ref, dsink, pl.program_id(0) == 0)

        def fold(acc, g):
            a = acc[g, pl.ds(ATT_BLOCK, seq), :]
            return a + pltpu.roll(a, HEAD_DIM, 1)

        dk_ref[...] = jnp.where(lo, fold(dkacc, 0), fold(dkacc, 1)).astype(BF16)
        dv_ref[...] = jnp.where(lo, fold(dvacc, 0), fold(dvacc, 1)).astype(BF16)

    blocks = [((seq, QW), BF16)] * 3 + [((seq, KV_WIDTH), BF16)] * 4
    kv_spec_out = pl.BlockSpec((seq, KV_WIDTH), lambda b: (b, 0))
    return _call(
        body, (projp, projp, projp, dao, sinks), name=name, grid=(T // seq,),
        in_specs=[pl.BlockSpec((seq, QW), lambda b: (b, q_blk)),
                  pl.BlockSpec((seq, KV_WIDTH), lambda b: (b, k_blk)),
                  pl.BlockSpec((seq, KV_WIDTH), lambda b: (b, v_blk)),
                  pl.BlockSpec((seq, QW), lambda b: (b, 0)),
                  pl.BlockSpec(memory_space=pltpu.SMEM)],
        out_specs=[pl.BlockSpec((seq, QW), lambda b: (b, 0)), kv_spec_out, kv_spec_out,
                   pl.BlockSpec((1, 128), lambda b: (0, 0))],
        out_shape=[SDS((T, QW), BF16), SDS((T, KV_WIDTH), BF16), SDS((T, KV_WIDTH), BF16), SDS((1, 128), F32)],
        scratch_shapes=[pltpu.VMEM((N_KV_HEADS, seq + ATT_BLOCK, PAIR_W), BF16)] * 2
        + [pltpu.VMEM((N_KV_HEADS, seq + ATT_BLOCK, PAIR_W), F32)] * 2,
        params=_params(1, blocks, temp_bytes=6 * _nbytes((N_KV_HEADS, seq + ATT_BLOCK, PAIR_W), BF16)
                       + 16 * _nbytes((BAND, GROUP_ROWS), F32)), comm=comm)


SUBLANES = 8


def _sublane_shifts(win):
    n = CONV_ROWS + CONV_HALO
    return [win] + [pltpu.roll(win, n - b, 0) for b in range(1, SUBLANES)]


def _window(shifted, off):
    a = off // SUBLANES * SUBLANES
    return shifted[off % SUBLANES][a:a + CONV_ROWS, :]


def _conv_fwd(projp, w, bias, *, seq, cw, a_col, b_col, name, comm=None):
    T = projp.shape[0]
    C = w.shape[1]
    nchunk = seq // CONV_ROWS

    def body(a_ref, b_ref, w_ref, bias_ref, y_ref, upad):
        upad[pl.ds(0, CONV_HALO), :] = jnp.zeros((CONV_HALO, cw), F32)
        upad[pl.ds(CONV_HALO, seq), :] = a_ref[...].astype(F32) * _sigmoid(b_ref[...].astype(F32))
        wv = w_ref[...]
        bv = bias_ref[...]

        def chunk(r, carry):
            r0 = pl.multiple_of(r * CONV_ROWS, CONV_ROWS)
            shifted = _sublane_shifts(upad[pl.ds(r0, CONV_ROWS + CONV_HALO), :])
            acc = jnp.broadcast_to(bv, (CONV_ROWS, cw))
            for k in range(CONV_WIDTH):
                acc = acc + wv[k:k + 1, :] * _window(shifted, CONV_HALO - (CONV_WIDTH - 1) + k)
            y_ref[pl.ds(r0, CONV_ROWS), :] = acc
            return carry

        lax.fori_loop(0, nchunk, chunk, 0)

    blocks = [((seq, cw), BF16)] * 2 + [((seq, cw), F32)]
    return _call(
        body, (projp, projp, w, bias), name=name, grid=(T // seq, C // cw),
        in_specs=[pl.BlockSpec((seq, cw), lambda b, c: (b, a_col // cw + c)),
                  pl.BlockSpec((seq, cw), lambda b, c: (b, b_col // cw + c)),
                  pl.BlockSpec((CONV_WIDTH, cw), lambda b, c: (0, c)),
                  pl.BlockSpec((1, cw), lambda b, c: (0, c))],
        out_specs=[pl.BlockSpec((seq, cw), lambda b, c: (b, c))],
        out_shape=[SDS((T, C), F32)],
        scratch_shapes=[pltpu.VMEM((seq + CONV_HALO, cw), F32)],
        params=_params(2, blocks, temp_bytes=6 * _nbytes((seq, cw), F32)), comm=comm, hbm_out=(0,))[0]


def _conv_bwd(dy, projp, w, *, seq, cw, a_col, b_col, name, comm=None):
    T = projp.shape[0]
    C = w.shape[1]
    nchunk = seq // CONV_ROWS
    SUB = 8

    def body(dy_ref, a_ref, b_ref, w_ref, da_ref, db_ref, dw_ref, dbias_ref, dypad, dwp):
        first = pl.program_id(1) == 0
        dyv = dy_ref[...]
        dypad[pl.ds(0, seq), :] = dyv
        dypad[pl.ds(seq, CONV_HALO), :] = jnp.zeros((CONV_HALO, cw), F32)
        dwp[...] = jnp.zeros(dwp.shape, F32)
        wv = w_ref[...]

        def chunk(r, carry):
            r0 = pl.multiple_of(r * CONV_ROWS, CONV_ROWS)
            dy_shifts = _sublane_shifts(dypad[pl.ds(r0, CONV_ROWS + CONV_HALO), :])
            ac = a_ref[pl.ds(r0, CONV_ROWS), :].astype(F32)
            sbc = _sigmoid(b_ref[pl.ds(r0, CONV_ROWS), :].astype(F32))
            uc = ac * sbc
            du = jnp.zeros((CONV_ROWS, cw), F32)
            for k in range(CONV_WIDTH):
                dyk = _window(dy_shifts, CONV_WIDTH - 1 - k)
                du = du + wv[k:k + 1, :] * dyk
                prod = uc * dyk
                part = prod[0:SUB, :]
                for s in range(1, CONV_ROWS // SUB):
                    part = part + prod[s * SUB:(s + 1) * SUB, :]
                dwp[pl.ds(k * SUB, SUB), :] = dwp[pl.ds(k * SUB, SUB), :] + part
            da_ref[pl.ds(r0, CONV_ROWS), :] = (du * sbc).astype(BF16)
            db_ref[pl.ds(r0, CONV_ROWS), :] = (du * ac * (sbc * (1.0 - sbc))).astype(BF16)
            return carry

        lax.fori_loop(0, nchunk, chunk, 0)

        @pl.when(first)
        def _():
            dw_ref[...] = jnp.zeros(dw_ref.shape, F32)
            dbias_ref[...] = jnp.zeros(dbias_ref.shape, F32)

        for k in range(CONV_WIDTH):
            dw_ref[k:k + 1, :] = dw_ref[k:k + 1, :] + _rowsum(dwp[pl.ds(k * SUB, SUB), :])
        dbias_ref[...] = dbias_ref[...] + _rowsum(dyv)

    blocks = [((seq, cw), F32)] + [((seq, cw), BF16)] * 4
    return _call(
        body, (dy, projp, projp, w), name=name, grid=(C // cw, T // seq),
        in_specs=[pl.BlockSpec((seq, cw), lambda c, b: (b, c)),
                  pl.BlockSpec((seq, cw), lambda c, b: (b, a_col // cw + c)),
                  pl.BlockSpec((seq, cw), lambda c, b: (b, b_col // cw + c)),
                  pl.BlockSpec((CONV_WIDTH, cw), lambda c, b: (0, c))],
        out_specs=[pl.BlockSpec((seq, cw), lambda c, b: (b, c)), pl.BlockSpec((seq, cw), lambda c, b: (b, c)),
                   pl.BlockSpec((CONV_WIDTH, cw), lambda c, b: (0, c)), pl.BlockSpec((1, cw), lambda c, b: (0, c))],
        out_shape=[SDS((T, C), BF16), SDS((T, C), BF16), SDS((CONV_WIDTH, C), F32), SDS((1, C), F32)],
        scratch_shapes=[pltpu.VMEM((seq + CONV_HALO, cw), F32), pltpu.VMEM((CONV_WIDTH * SUB, cw), F32)],
        params=_params(2, blocks, temp_bytes=8 * _nbytes((seq, cw), F32)), comm=comm, hbm_out=(0, 1))


def _matmul_tn(a, b, *, name, gate=None, comm=None):
    T, M = a.shape
    N = b.shape[1]
    bm = _pick(M, (768, 512, 256))
    lhs = [a] if gate is None else [a, gate]

    def body(*refs):
        b_ref, o_ref = refs[len(lhs)], refs[len(lhs) + 1]
        av = refs[0][...]
        if gate is not None:
            af = av.astype(F32)
            av = (af * _sigmoid(af) * refs[1][...].astype(F32)).astype(BF16)
        o_ref[...] = _dot_tn(av, b_ref[...]).astype(BF16)

    blocks = [((T, bm), BF16)] * len(lhs) + [((T, N), BF16), ((bm, N), BF16)]
    return _call(
        body, (*lhs, b), name=name, grid=(M // bm,),
        in_specs=[pl.BlockSpec((T, bm), lambda i: (0, i))] * len(lhs) + [pl.BlockSpec((T, N), lambda i: (0, 0))],
        out_specs=[pl.BlockSpec((bm, N), lambda i: (i, 0))],
        out_shape=[SDS((M, N), BF16)],
        params=_params(1, blocks, temp_bytes=(2 + 4 * len(lhs)) * _nbytes((T, bm), BF16) + 2 * _nbytes((bm, N), F32)),
        comm=comm)[0]


TN_BLOCK = 256


def _matmul_tn_pieces(groups, b, *, name, comm=None):
    T, N = b.shape
    flat = [a for g in groups for a in g]
    starts, n_steps = [], 0
    for g in groups:
        width = sum(a.shape[1] for a in g)
        assert width % TN_BLOCK == 0 and (len(g) == 1 or width == TN_BLOCK), [a.shape for a in g]
        starts.append(n_steps)
        n_steps += width // TN_BLOCK

    def body(*refs):
        a_refs, b_ref, o_ref = refs[:len(flat)], refs[len(flat)], refs[len(flat) + 1]
        i = pl.program_id(0)
        at = 0
        for g, start in zip(groups, starts):
            mine = a_refs[at:at + len(g)]
            at += len(g)
            steps = sum(a.shape[1] for a in g) // TN_BLOCK

            @pl.when(jnp.logical_and(i >= start, i < start + steps))
            def _(mine=mine):
                a = mine[0][...] if len(mine) == 1 else jnp.concatenate([r[...] for r in mine], axis=1)
                o_ref[...] = _dot_tn(a, b_ref[...]).astype(BF16)

    a_specs = []
    for g, start in zip(groups, starts):
        for a in g:
            if len(g) == 1:
                last = a.shape[1] // TN_BLOCK - 1
                a_specs.append(pl.BlockSpec(
                    (T, TN_BLOCK), lambda i, start=start, last=last: (0, jnp.clip(i - start, 0, last))))
            else:
                a_specs.append(pl.BlockSpec((T, a.shape[1]), lambda i: (0, 0)))
    blocks = [((T, TN_BLOCK), BF16)] * len(flat) + [((T, N), BF16), ((TN_BLOCK, N), BF16)]
    return _call(
        body, (*flat, b), name=name, grid=(n_steps,),
        in_specs=a_specs + [pl.BlockSpec((T, N), lambda i: (0, 0))],
        out_specs=[pl.BlockSpec((TN_BLOCK, N), lambda i: (i, 0))],
        out_shape=[SDS((n_steps * TN_BLOCK, N), BF16)],
        params=_params(1, blocks, temp_bytes=2 * _nbytes((T, TN_BLOCK), BF16) + 2 * _nbytes((TN_BLOCK, N), F32)),
        comm=comm)[0]


def _sum_parts(p_ref):
    g = p_ref[0].astype(F32)
    for s in range(1, p_ref.shape[0]):
        g = g + p_ref[s].astype(F32)
    return g


def _pair_add(g, staged, *, name):
    _, R, W = g.shape
    nq = staged.shape[0]
    tr = _row_tile(R)

    def body(g_ref, s_ref, o_ref):
        mine = jnp.where(lax.axis_index("c") == 0, g_ref[0, 0].astype(F32), g_ref[0, 1].astype(F32))
        o_ref[0] = (mine + s_ref[0].astype(F32)).astype(o_ref.dtype)

    return _call(
        body, (g.reshape(nq, 2, R, W), staged), name=name, grid=(nq, R // tr),
        in_specs=[pl.BlockSpec((1, 2, tr, W), lambda q, i: (q, 0, i, 0)),
                  pl.BlockSpec((1, tr, W), lambda q, i: (q, i, 0))],
        out_specs=[pl.BlockSpec((1, tr, W), lambda q, i: (q, i, 0))],
        out_shape=[SDS((nq, R, W), g.dtype)],
        params=_params(2, [((4, tr, W), g.dtype)], temp_bytes=3 * _nbytes((tr, W), F32)), hbm_out=(0,))[0]


def _adamw_update(w, g, m, v):
    m = ADAM_B1 * m + (1.0 - ADAM_B1) * g
    v = ADAM_B2 * v + (1.0 - ADAM_B2) * (g * g)
    m_hat = m / (1.0 - ADAM_B1 ** ADAM_STEP)
    v_hat = v / (1.0 - ADAM_B2 ** ADAM_STEP)
    delta = -ADAM_LR * (m_hat / (jnp.sqrt(v_hat) + ADAM_EPS) + ADAM_WD * w)
    return delta, m, v


def _row_tile(R):
    return _pick(R, (256, 128, 112, 88, 64, 32, 16, 8))


def _sum8(parts, *, name):
    n, R, W = parts.shape
    tr = _row_tile(R)

    def body(p_ref, o_ref):
        o_ref[...] = _sum_parts(p_ref)

    return _call(
        body, (parts,), name=name, grid=(R // tr,),
        in_specs=[pl.BlockSpec((n, tr, W), lambda i: (0, i, 0))],
        out_specs=[pl.BlockSpec((tr, W), lambda i: (i, 0))],
        out_shape=[SDS((R, W), F32)],
        params=_params(1, [((n, tr, W), parts.dtype), ((tr, W), F32)]))[0]


def _adamw(g, w, m, v, *, name):
    R, W = w.shape
    tr = _row_tile(R)

    def body(g_ref, w_ref, m_ref, v_ref, d_ref, mo_ref, vo_ref):
        d_ref[...], mo_ref[...], vo_ref[...] = _adamw_update(w_ref[...], g_ref[...], m_ref[...], v_ref[...])

    spec = pl.BlockSpec((tr, W), lambda i: (i, 0))
    return _call(
        body, (g, w, m, v), name=name, grid=(R // tr,),
        in_specs=[spec] * 4, out_specs=[spec] * 3, out_shape=[SDS((R, W), F32)] * 3,
        params=_params(1, [((tr, W), F32)] * 7))


def _sum8_adamw(parts, w, m, v, *, name):
    R, W = w.shape
    n = parts.shape[0]
    tr = _row_tile(R)

    def body(p_ref, w_ref, m_ref, v_ref, g_ref, d_ref, mo_ref, vo_ref):
        g = _sum_parts(p_ref)
        g_ref[...] = g
        d_ref[...], mo_ref[...], vo_ref[...] = _adamw_update(w_ref[...], g, m_ref[...], v_ref[...])

    spec = pl.BlockSpec((tr, W), lambda i: (i, 0))
    return _call(
        body, (parts, w, m, v), name=name, grid=(R // tr,),
        in_specs=[pl.BlockSpec((n, tr, W), lambda i: (0, i, 0))] + [spec] * 3,
        out_specs=[spec] * 4, out_shape=[SDS((R, W), F32)] * 4,
        params=_params(1, [((n, tr, W), parts.dtype)] + [((tr, W), F32)] * 7))


def _ada_fwd(c_all, w, bias, *, name):
    NB, D = c_all.shape
    N = w.shape[1]

    def body(c_ref, w_ref, b_ref, o_ref):
        cv = c_ref[...]
        ca = (cv * _sigmoid(cv)).astype(BF16)
        o_ref[...] = _dot(ca, w_ref[...].astype(BF16)) + b_ref[...]

    full = lambda s: pl.BlockSpec(s, lambda i: (0,) * len(s))
    return _call(
        body, (c_all, w, bias), name=name, grid=(1,),
        in_specs=[full((NB, D)), full((D, N)), full((1, N))], out_specs=[full((NB, N))],
        out_shape=[SDS((NB, N), F32)],
        params=_params(1, [((D, N), F32)], temp_bytes=_nbytes((D, N), BF16)))[0]


def _ada_bwd(c_all, gmod_all, *, n_col, name):
    NB, D = c_all.shape
    N = gmod_all.shape[1]

    def body(c_ref, g_ref, gw_ref, gb_ref):
        cv = c_ref[...]
        ca = (cv * _sigmoid(cv)).astype(BF16)
        first = pl.multiple_of(_lin(_my_pos()) * n_col, 128)
        gw_ref[...] = _dot_tn(ca, g_ref[:, pl.ds(first, n_col)].astype(BF16))
        gb_ref[...] = _rowsum(g_ref[...])

    full = lambda s: pl.BlockSpec(s, lambda i: (0,) * len(s))
    return _call(
        body, (c_all, gmod_all), name=name, grid=(1,),
        in_specs=[full((NB, D)), full((NB, N))], out_specs=[full((D, n_col)), full((1, N))],
        out_shape=[SDS((D, n_col), F32), SDS((1, N), F32)],
        params=_params(1, [((D, n_col), F32), ((NB, N), F32)]))


def kernel(x, c, w_ada, b_ada, norm_ffn1_g, ffn1_w_gate, ffn1_w_up, ffn1_w_down, norm_mix_g, w_in, attn_sinks, w_attn_o, conv_w_dw, conv_b_dw, conv_ln_g, conv_ln_b, w_conv_o, w_out, norm_ffn2_g, ffn2_w_gate, ffn2_w_up, ffn2_w_down, final_norm_g, loss_target, m_w_ada, m_b_ada, m_norm_ffn1_g, m_ffn1_w_gate, m_ffn1_w_up, m_ffn1_w_down, m_norm_mix_g, m_w_in, m_attn_sinks, m_w_attn_o, m_conv_w_dw, m_conv_b_dw, m_conv_ln_g, m_conv_ln_b, m_w_conv_o, m_w_out, m_norm_ffn2_g, m_ffn2_w_gate, m_ffn2_w_up, m_ffn2_w_down, m_final_norm_g, v_w_ada, v_b_ada, v_norm_ffn1_g, v_ffn1_w_gate, v_ffn1_w_up, v_ffn1_w_down, v_norm_mix_g, v_w_in, v_attn_sinks, v_w_attn_o, v_conv_w_dw, v_conv_b_dw, v_conv_ln_g, v_conv_ln_b, v_w_conv_o, v_w_out, v_norm_ffn2_g, v_ffn2_w_gate, v_ffn2_w_up, v_ffn2_w_down, v_final_norm_g):
    B, S, D = x.shape
    T = B * S
    QW = N_Q_HEADS * HEAD_DIM
    CC = conv_w_dw.shape[2] * N_DEV
    me = _lin(_my_pos())
    xf = x.reshape(T, D)
    tgt = loss_target.reshape(T, D)
    tm = min(512, S)
    kw = dict(seq=S, tm=tm)

    p_k, p_v, p_ca = QW, QW + KV_WIDTH, QW + 2 * KV_WIDTH
    p_cb, p_ga, p_gc = p_ca + CC, p_ca + 2 * CC, p_ca + 2 * CC + D

    def col_t(w):
        return w[0].T.astype(BF16)

    def row_b(w):
        return w[0].astype(BF16)

    def rows(g):
        return g.reshape(-1, g.shape[-1])

    def blocks8(g):
        return g.reshape(N_DEV, g.shape[0] // N_DEV, g.shape[1])

    def gather(*arrs, hbm_out=False):
        return _Comm([(a, "gather") for a in arrs], hbm_out=hbm_out)

    g_wg1, g_convw, g_c = _exchange(
        [(col_t(ffn1_w_gate), "gather"), (conv_w_dw[0], "gather"), (c, "gather")], name="gather_first")
    wg1 = rows(g_wg1)
    conv_w = g_convw.transpose(1, 0, 2).reshape(CONV_WIDTH, CC)
    c_all = g_c.reshape(N_DEV * B, D)

    n_col = N_MOD * D // N_DEV
    b_cols = lax.dynamic_slice(b_ada, (0, me * n_col), (1, n_col))
    mod_cols = _ada_fwd(c_all, w_ada[0], b_cols, name="ada_fwd")
    mod_mine = _exchange([(mod_cols.reshape(N_DEV, B, n_col), "scatter")], name="scatter_mod")[0]
    mod = mod_mine.transpose(1, 0, 2).reshape(B * N_MOD, 1, D)
    sh1, sc1, g1, sh2, sc2, g2, sh3, sc3, g3 = [_ModVec(mod, i) for i in range(N_MOD)]

    F = wg1.shape[0]
    tn_f = _pick(F, (1408, 1024, 512, 256))
    tn_in = _pick(w_in.shape[2] * N_DEV, (1792, 768, 512, 256))
    gate_blk = dict(ga_col=p_ga, gc_col=p_gc)
    att_blk = dict(q_blk=0, k_blk=p_k // KV_WIDTH, v_blk=p_v // KV_WIDTH)
    conv_kw = dict(seq=S, cw=256, a_col=p_ca, b_col=p_cb)

    cm = gather(col_t(ffn1_w_up))
    h1, (a1,) = _norm_mod_matmul(xf, norm_ffn1_g, sh1, sc1, [wg1], tn=tn_f, name="ffn1_gate", comm=cm, **kw)
    wu1 = rows(cm.out[0])
    cm = gather(row_b(ffn1_w_down), hbm_out=True)
    b1 = _matmul_nt(h1, wu1, tm=tm, tn=tn_f, name="ffn1_up", comm=cm)
    wd1 = rows(cm.out[0])
    cm = gather(col_t(w_in))
    x1, y1 = _ffn_down(a1, b1, wd1, xf, g1, name="ffn1_down", comm=cm, **kw)
    winp = rows(cm.out[0])
    cm = gather(row_b(w_attn_o), row_b(w_conv_o), row_b(w_out), col_t(ffn2_w_gate))
    h2, (projp,) = _norm_mod_matmul(x1, norm_mix_g, sh2, sc2, [winp], tn=tn_in, name="mix_in", comm=cm, **kw)
    wao, wco, wout, wg2 = [rows(o) for o in cm.out]
    cm = gather(col_t(ffn2_w_up), hbm_out=True)
    ao = _in_hbm(_attn_fwd(projp, attn_sinks, seq=S, name="attn_fwd", comm=cm, **att_blk))
    wu2 = rows(cm.out[0])
    cm = gather(row_b(ffn2_w_down), hbm_out=True)
    yc = _in_hbm(_conv_fwd(projp, conv_w, conv_b_dw, name="conv_fwd", comm=cm, **conv_kw))
    wd2 = rows(cm.out[0])
    x2, z, ya, ycv, cact, merged = _mix_out(ao, yc, projp, wao, wco, wout, x1, g2, conv_ln_g, conv_ln_b,
                                            name="mix_out", **gate_blk, **kw)
    h3, (a3, b3) = _norm_mod_matmul(x2, norm_ffn2_g, sh3, sc3, [wg2, wu2], tn=tn_f, name="ffn2_up", **kw)
    x3, y3 = _ffn_down(a3, b3, wd2, x2, g3, name="ffn2_down", **kw)
    dx3, loss_row, dgf = _final_loss(_in_hbm(x3), final_norm_g[None], _in_hbm(tgt), tm=tm, name="final_loss")
    dx3 = _in_hbm(dx3)

    parts = {}

    def pair(*gs):
        return [(blocks8(g), "pair") for g in gs]

    def cross(*rs):
        return [(r, "cross") for r in rs]

    def reduce_pairs(gs, staged, names):
        return [_pair_add(blocks8(g), s, name="pair_add_" + n) for g, s, n in zip(gs, staged, names)]

    dyb3, da3, db3, dg3 = _ffn_bwd_down(dx3, g3, y3, wd2, a3, b3, tn=tn_f, name="ffn2_bwd_down", **kw)
    gwd2 = _matmul_tn(a3, dyb3, gate=b3, name="gw_ffn2_down")
    cm = _Comm(pair(gwd2))
    dx2, dsh3, dsc3, dgn3 = _matmul_norm_mod_bwd([[da3], [db3]], [wg2, wu2], x2, norm_ffn2_g, sc3, dx3,
                                                 name="ffn2_bwd_up", out_dtype=GRAD_STREAM, comm=cm, **kw)
    r_wd2, = reduce_pairs([gwd2], cm.out, ["ffn2_w_down"])
    cm = _Comm(cross(r_wd2))
    gwg2 = _matmul_tn(da3, h3, name="gw_ffn2_gate", comm=cm)
    parts["ffn2_w_down"], = cm.out
    cm = _Comm(pair(gwg2))
    gwu2 = _matmul_tn(db3, h3, name="gw_ffn2_up", comm=cm)
    r_wg2, = reduce_pairs([gwg2], cm.out, ["ffn2_w_gate"])

    cm = _Comm(cross(r_wg2) + pair(gwu2))
    dzb, dyab, dycb, dga, dgc, dao, dyc, dg2, dlng, dlnb = _mix_out_bwd(
        dx2, g2, z, wout, projp, ya, ycv, wao, wco, yc, conv_ln_g, conv_ln_b, name="mix_out_bwd", comm=cm,
        **gate_blk, **kw)
    parts["ffn2_w_gate"] = cm.out[0]
    r_wu2, = reduce_pairs([gwu2], cm.out[1:], ["ffn2_w_up"])
    gwout = _matmul_tn(merged, dzb, name="gw_out")
    gwao = _matmul_tn(ao, dyab, name="gw_attn_o")
    gwco = _matmul_tn(cact, dycb, name="gw_conv_o")
    cm = _Comm(cross(r_wu2) + pair(gwout, gwao, gwco))
    dq, dk, dv, dsinks = _attn_bwd(projp, dao, attn_sinks, seq=S, name="attn_bwd", comm=cm, **att_blk)
    parts["ffn2_w_up"] = cm.out[0]
    r_mix = reduce_pairs([gwout, gwao, gwco], cm.out[1:], ["w_out", "w_attn_o", "w_conv_o"])
    cm = _Comm(cross(*r_mix))
    dca, dcb, dconvw, dconvb = _conv_bwd(_in_hbm(dyc), projp, conv_w, name="conv_bwd", comm=cm, **conv_kw)
    dca, dcb = _in_hbm(dca), _in_hbm(dcb)
    parts["w_out"], parts["w_attn_o"], parts["w_conv_o"] = cm.out
    gwin = _matmul_tn_pieces([[dq], [dk, dv], [dca], [dcb], [dga], [dgc]], h2, name="gw_in")
    cm = _Comm(pair(gwin))
    dx1, dsh2, dsc2, dgn2 = _matmul_norm_mod_bwd([[dq, dk, dv, dca, dcb, dga, dgc]], [winp], x1, norm_mix_g, sc2, dx2,
                                                 name="mix_in_bwd", out_dtype=GRAD_STREAM, comm=cm, **kw)
    r_win, = reduce_pairs([gwin], cm.out, ["w_in"])

    cm = _Comm(cross(r_win))
    dyb1, da1, db1, dg1 = _ffn_bwd_down(dx1, g1, y1, wd1, a1, b1, tn=tn_f, name="ffn1_bwd_down", comm=cm,
                                              **kw)
    parts["w_in"], = cm.out
    gwd1 = _matmul_tn(a1, dyb1, gate=b1, name="gw_ffn1_down")
    cm = _Comm(pair(gwd1))
    gwg1 = _matmul_tn(da1, h1, name="gw_ffn1_gate", comm=cm)
    r_wd1, = reduce_pairs([gwd1], cm.out, ["ffn1_w_down"])
    cm = _Comm(cross(r_wd1) + pair(gwg1))
    gwu1 = _matmul_tn(db1, h1, name="gw_ffn1_up", comm=cm)
    parts["ffn1_w_down"] = cm.out[0]
    r_wg1, = reduce_pairs([gwg1], cm.out[1:], ["ffn1_w_gate"])
    r_wu1, = reduce_pairs([gwu1], _exchange(pair(gwu1), name="pair_last"), ["ffn1_w_up"])
    cm = _Comm(cross(r_wg1, r_wu1))
    dx0, dsh1, dsc1, dgn1 = _matmul_norm_mod_bwd([[da1], [db1]], [wg1, wu1], xf, norm_ffn1_g, sc1, dx1,
                                                 name="ffn1_bwd_up", out_dtype=F32, comm=cm, **kw)
    parts["ffn1_w_gate"], parts["ffn1_w_up"] = cm.out

    n_small = 8
    gmod = jnp.concatenate([dsh1, dsc1, dg1, dsh2, dsc2, dg2, dsh3, dsc3, dg3], axis=1).reshape(B, N_MOD * D)
    sink_row = jnp.pad(dsinks[:, :N_Q_HEADS], ((0, 0), (0, D - N_Q_HEADS)))
    loss_pad = jnp.pad(loss_row, ((0, 0), (0, D - loss_row.shape[1])))
    small = jnp.concatenate([dgn1, dgn2, dgn3, dgf, dconvb, dlng, dlnb, sink_row, dconvw, loss_pad], axis=0)
    small_all, gmod_all = _exchange([(small, "gather"), (gmod, "gather")], name="exchange_last")
    gsmall = _sum8(small_all, name="sum_small")
    loss = gsmall[n_small + CONV_WIDTH, 0]
    g_w_ada, g_b_ada = _ada_bwd(c_all, gmod_all.reshape(N_DEV * B, N_MOD * D), n_col=n_col, name="ada_bwd")
    g_conv_w = lax.dynamic_slice(gsmall[n_small:n_small + CONV_WIDTH], (0, me * (CC // N_DEV)),
                                 (CONV_WIDTH, CC // N_DEV))

    def col_update(name, w, m, v):
        outs = _sum8_adamw(parts[name], w[0].T, m[0].T, v[0].T, name="adamw_" + name)
        return tuple(o.T for o in outs)

    def row_update(name, w, m, v):
        return tuple(_sum8_adamw(parts[name], w[0], m[0], v[0], name="adamw_" + name))

    upd = {
        "ffn1_w_gate": col_update("ffn1_w_gate", ffn1_w_gate, m_ffn1_w_gate, v_ffn1_w_gate),
        "ffn1_w_up": col_update("ffn1_w_up", ffn1_w_up, m_ffn1_w_up, v_ffn1_w_up),
        "ffn1_w_down": row_update("ffn1_w_down", ffn1_w_down, m_ffn1_w_down, v_ffn1_w_down),
        "w_in": col_update("w_in", w_in, m_w_in, v_w_in),
        "w_attn_o": row_update("w_attn_o", w_attn_o, m_w_attn_o, v_w_attn_o),
        "w_conv_o": row_update("w_conv_o", w_conv_o, m_w_conv_o, v_w_conv_o),
        "w_out": row_update("w_out", w_out, m_w_out, v_w_out),
        "ffn2_w_gate": col_update("ffn2_w_gate", ffn2_w_gate, m_ffn2_w_gate, v_ffn2_w_gate),
        "ffn2_w_up": col_update("ffn2_w_up", ffn2_w_up, m_ffn2_w_up, v_ffn2_w_up),
        "ffn2_w_down": row_update("ffn2_w_down", ffn2_w_down, m_ffn2_w_down, v_ffn2_w_down),
        "w_ada": (g_w_ada,) + tuple(_adamw(g_w_ada, w_ada[0], m_w_ada[0], v_w_ada[0], name="adamw_w_ada")),
        "conv_w_dw": (g_conv_w,) + tuple(_adamw(g_conv_w, conv_w_dw[0], m_conv_w_dw[0], v_conv_w_dw[0],
                                                name="adamw_conv_w_dw")),
    }
    for k in upd:
        upd[k] = tuple(t[None] for t in upd[k])

    def pad_sinks(t):
        return jnp.pad(t, ((0, 0), (0, D - N_Q_HEADS)))

    def pack(f1, mix, f2, fin, cb, lg, lb, sinks, bada):
        return jnp.concatenate([f1, mix, f2, fin[None], cb, lg, lb, pad_sinks(sinks), bada.reshape(N_MOD, D)], axis=0)

    w_s = pack(norm_ffn1_g, norm_mix_g, norm_ffn2_g, final_norm_g, conv_b_dw, conv_ln_g, conv_ln_b, attn_sinks, b_ada)
    m_s = pack(m_norm_ffn1_g, m_norm_mix_g, m_norm_ffn2_g, m_final_norm_g, m_conv_b_dw, m_conv_ln_g, m_conv_ln_b,
               m_attn_sinks, m_b_ada)
    v_s = pack(v_norm_ffn1_g, v_norm_mix_g, v_norm_ffn2_g, v_final_norm_g, v_conv_b_dw, v_conv_ln_g, v_conv_ln_b,
               v_attn_sinks, v_b_ada)
    g_s = jnp.concatenate([gsmall[:n_small], g_b_ada.reshape(N_MOD, D)], axis=0)
    small_out = (g_s,) + tuple(_adamw(g_s, w_s, m_s, v_s, name="adamw_vectors"))

    def unpack(t):
        return {
            "norm_ffn1_g": t[0:1], "norm_mix_g": t[1:2], "norm_ffn2_g": t[2:3], "final_norm_g": t[3],
            "conv_b_dw": t[4:5], "conv_ln_g": t[5:6], "conv_ln_b": t[6:7], "attn_sinks": t[7:8, :N_Q_HEADS],
            "b_ada": t[n_small:n_small + N_MOD].reshape(1, N_MOD * D),
        }

    small_un = [unpack(t) for t in small_out]
    for k in small_un[0]:
        upd[k] = tuple(s[k] for s in small_un)

    order = ["w_ada", "b_ada", "norm_ffn1_g", "ffn1_w_gate", "ffn1_w_up", "ffn1_w_down", "norm_mix_g", "w_in",
             "attn_sinks", "w_attn_o", "conv_w_dw", "conv_b_dw", "conv_ln_g", "conv_ln_b", "w_conv_o", "w_out",
             "norm_ffn2_g", "ffn2_w_gate", "ffn2_w_up", "ffn2_w_down", "final_norm_g"]
    grad_x = dx0.reshape(B, S, D)
    return (loss, grad_x, *[upd[k][0] for k in order], *[upd[k][1] for k in order],
            *[upd[k][2] for k in order], *[upd[k][3] for k in order])
```

```python
import dataclasses

import jax
import jax.numpy as jnp
from jax import lax
from jax.experimental import pallas as pl
from jax.experimental.pallas import tpu as pltpu

F32 = jnp.float32
BF16 = jnp.bfloat16
SDS = jax.ShapeDtypeStruct
MESH = pl.DeviceIdType.MESH

N_DEV = 8
EPS = 1e-6
HEAD_DIM = 64
N_Q_HEADS = 16
N_KV_HEADS = 2
GQA_GROUP = N_Q_HEADS // N_KV_HEADS
KV_WIDTH = N_KV_HEADS * HEAD_DIM
ATT_BLOCK = 128
CONV_WIDTH = 31
CONV_HALO = 32
CONV_ROWS = 128
N_MOD = 9
FFN_RESIDUAL = 0.5
ADAM_LR = 0.001
ADAM_B1 = 0.9
ADAM_B2 = 0.999
ADAM_EPS = 1e-08
ADAM_WD = 0.01
ADAM_STEP = 10
NEG_BIG = -1e30
GRAD_STREAM = BF16

V7X_VMEM_BYTES = 64 * 2**20
VMEM_CAP = V7X_VMEM_BYTES - 8 * 2**20


def _nbytes(shape, dtype):
    n = 1
    for s in shape:
        n *= s
    return n * jnp.dtype(dtype).itemsize


def _params(n_axes, blocks, temp_bytes=0):
    need = 2 * sum(_nbytes(s, d) for s, d in blocks) + temp_bytes + 4 * 2**20
    return pltpu.CompilerParams(dimension_semantics=("arbitrary",) * n_axes,
                                vmem_limit_bytes=int(min(max(need, 16 * 2**20), VMEM_CAP)))


def _dot_nt(a, b):
    return lax.dot_general(a, b, (((1,), (1,)), ((), ())), preferred_element_type=F32)


def _dot_tn(a, b):
    return lax.dot_general(a, b, (((0,), (0,)), ((), ())), preferred_element_type=F32)


def _dot(a, b):
    return jnp.dot(a, b, preferred_element_type=F32)


def _sigmoid(x):
    return jax.nn.sigmoid(x)


def _rowsum(v):
    return jnp.sum(v, axis=0, keepdims=True)


def _acc(ref, val, first):
    @pl.when(first)
    def _():
        ref[...] = val

    @pl.when(jnp.logical_not(first))
    def _():
        ref[...] = ref[...] + val


def _norm_mod(xf, gn, sh, sc):
    rstd = lax.rsqrt(jnp.mean(xf * xf, axis=-1, keepdims=True) + EPS)
    xhat = xf * rstd
    yn = xhat * gn
    return yn * (1.0 + sc) + sh, xhat, rstd, yn


def _pick(n, cands):
    for c in cands:
        if n % c == 0:
            return c
    return n


def _my_pos():
    return lax.axis_index("x"), lax.axis_index("y"), lax.axis_index("c")


def _peer(pos, k):
    x, y, c = pos
    return ((1 - x) if k & 4 else x, (1 - y) if k & 2 else y, (1 - c) if k & 1 else c)


def _lin(pos):
    return 4 * pos[0] + 2 * pos[1] + pos[2]


def _in_hbm(a):
    return pltpu.with_memory_space_constraint(a, pltpu.HBM)


class _Comm:
    N_COPY = N_DEV - 1
    N_CHIP = N_DEV // 2

    def __init__(self, items, hbm_out=False):
        self.hbm_out = hbm_out
        self.arrs = [a for a, _ in items]
        self.modes = [m for _, m in items]
        self.n = len(items)
        self.out = None

    def out_shape(self):
        def shape(a, m):
            return {"gather": (N_DEV,) + a.shape, "scatter": a.shape, "pair": (self.N_CHIP,) + a.shape[1:],
                    "cross": a.shape}[m]
        kind = pltpu.HBM if self.hbm_out else SDS
        return [kind(shape(a, m), a.dtype) for a, m in zip(self.arrs, self.modes)]

    def scratch(self):
        return [pltpu.SemaphoreType.DMA((self.n * self.N_COPY,)), pltpu.SemaphoreType.DMA((self.n * self.N_COPY,)),
                pltpu.SemaphoreType.DMA((self.n,))]

    def collective_id(self):
        modes = set(self.modes)
        if "scatter" in modes:
            return 3
        d2d, ici = bool(modes & {"gather", "pair"}), bool(modes & {"gather", "cross"})
        return {(True, False): 0, (False, True): 1, (True, True): 2}[(d2d, ici)]

    def barrier(self):
        x, y, c = _my_pos()
        peers = {0: [(x, y, 1 - c)],
                 1: [(1 - x, y, c), (x, 1 - y, c), (1 - x, 1 - y, c)],
                 2: [(x, y, 1 - c), (1 - x, y, c), (x, 1 - y, c), (1 - x, 1 - y, c)],
                 3: [_peer((x, y, c), k) for k in range(1, N_DEV)]}[self.collective_id()]
        sem = pltpu.get_barrier_semaphore()
        for p in peers:
            pl.semaphore_signal(sem, inc=1, device_id=p, device_id_type=MESH)
        pl.semaphore_wait(sem, len(peers))

    def _plan(self, mode, me):
        x, y, c = me
        sib = (x, y, 1 - c)
        chips = [(1 - x, y), (x, 1 - y), (1 - x, 1 - y)]

        def chip_lin(ch):
            return 2 * ch[0] + ch[1]

        if mode == "scatter":
            peers = [_peer(me, k + 1) for k in range(self.N_COPY)]
            return [(p, ("in", _lin(p)), _lin(me), _lin(p), None) for p in peers], (_lin(me), _lin(me))
        if mode == "gather":
            same = [(*ch, c) for ch in chips]
            other = [(*ch, 1 - c) for ch in chips]
            copies = [(sib, ("in", None), _lin(me), _lin(sib), None)]
            copies += [(p, ("in", None), _lin(me), _lin(p), None) for p in same]
            copies += [(sib, ("out", _lin(p)), _lin(p), _lin(o), 1 + j) for j, (p, o) in enumerate(zip(same, other))]
            return copies, (None, _lin(me))
        if mode == "pair":
            return [(sib, ("in", 2 * q + 1 - c), q, q, None) for q in range(self.N_CHIP)], None
        if mode == "cross":
            mine = chip_lin((x, y))
            return ([((*ch, c), ("in", chip_lin(ch)), mine, chip_lin(ch), None) for ch in chips], (mine, mine))
        raise ValueError(mode)

    def _copy(self, refs, me, i, k, recv):
        srcs, outs, (send_sems, recv_sems, _) = refs
        peer, (where, slot), send_slot, recv_slot, _ = self._plan(self.modes[i], me)[0][k]
        src = srcs[i] if where == "in" else outs[i]
        src = src if slot is None else src.at[slot]
        sem = i * self.N_COPY + k
        return pltpu.make_async_remote_copy(
            src_ref=src, dst_ref=outs[i].at[recv_slot if recv else send_slot], send_sem=send_sems.at[sem],
            recv_sem=recv_sems.at[sem], device_id=peer, device_id_type=MESH)

    def _local(self, refs, me, i):
        srcs, outs, (_, _, loc_sems) = refs
        local = self._plan(self.modes[i], me)[1]
        if local is None:
            return None
        own = srcs[i] if local[0] is None else srcs[i].at[local[0]]
        return pltpu.make_async_copy(own, outs[i].at[local[1]], loc_sems.at[i])

    def start(self, refs):
        me = _my_pos()
        for i in range(self.n):
            local = self._local(refs, me, i)
            if local is not None:
                local.start()
            for k, cp in enumerate(self._plan(self.modes[i], me)[0]):
                if cp[4] is None:
                    self._copy(refs, me, i, k, False).start()

    def forward(self, refs):
        me = _my_pos()
        for i in range(self.n):
            for k, cp in enumerate(self._plan(self.modes[i], me)[0]):
                if cp[4] is not None:
                    self._copy(refs, me, i, cp[4], True).wait_recv()
                    self._copy(refs, me, i, k, False).start()

    def finish(self, refs):
        me = _my_pos()
        plans = [self._plan(m, me)[0] for m in self.modes]
        for i in range(self.n):
            passed_on = [cp[4] for cp in plans[i] if cp[4] is not None]
            for k in range(len(plans[i])):
                if k not in passed_on:
                    self._copy(refs, me, i, k, True).wait_recv()
                self._copy(refs, me, i, k, False).wait_send()
            local = self._local(refs, me, i)
            if local is not None:
                local.wait()


_ANY = pl.BlockSpec(memory_space=pl.ANY)


def _call(body, args, *, name, grid, in_specs, out_specs, out_shape, params, scratch_shapes=(), comm=None,
          hbm_out=()):
    in_specs, out_specs, out_shape = list(in_specs), list(out_specs), list(out_shape)
    scratch_shapes = list(scratch_shapes)
    for k in hbm_out:
        out_shape[k] = pltpu.HBM(out_shape[k].shape, out_shape[k].dtype)
    if comm is None:
        return list(pl.pallas_call(body, name=name, grid=grid, in_specs=in_specs, out_specs=out_specs,
                                   out_shape=out_shape, scratch_shapes=scratch_shapes, compiler_params=params)(*args))
    n_in, n_out, n_scr, nc = len(in_specs), len(out_specs), len(scratch_shapes), comm.n
    n_steps = 1
    for g in grid:
        n_steps *= g

    def hosted(*refs):
        ins, c_in = refs[:n_in], refs[n_in:n_in + nc]
        outs = refs[n_in + nc:n_in + nc + n_out]
        c_out = refs[n_in + nc + n_out:n_in + 2 * nc + n_out]
        scr = refs[n_in + 2 * nc + n_out:n_in + 2 * nc + n_out + n_scr]
        sems = refs[n_in + 2 * nc + n_out + n_scr:]
        step = pl.program_id(0)
        for d in range(1, len(grid)):
            step = step * grid[d] + pl.program_id(d)
        c_refs = (c_in, c_out, sems)

        @pl.when(step == 0)
        def _():
            comm.barrier()
            comm.start(c_refs)

        if n_steps >= 3:
            @pl.when(step == n_steps - 2)
            def _():
                comm.forward(c_refs)

        body(*ins, *outs, *scr)

        @pl.when(step == n_steps - 1)
        def _():
            if n_steps < 3:
                comm.forward(c_refs)
            comm.finish(c_refs)

    res = pl.pallas_call(
        hosted, name=name, grid=grid, in_specs=in_specs + [_ANY] * nc, out_specs=out_specs + [_ANY] * nc,
        out_shape=out_shape + comm.out_shape(), scratch_shapes=scratch_shapes + comm.scratch(),
        compiler_params=dataclasses.replace(params, collective_id=comm.collective_id()))(*args, *comm.arrs)
    comm.out = list(res[n_out:])
    return list(res[:n_out])


def _exchange(items, *, name):
    comm = _Comm(items)

    def body(*refs):
        r = (refs[:comm.n], refs[comm.n:2 * comm.n], refs[2 * comm.n:])
        comm.barrier()
        comm.start(r)
        comm.forward(r)
        comm.finish(r)

    return list(pl.pallas_call(body, name=name, out_shape=comm.out_shape(), in_specs=[_ANY] * comm.n,
                               out_specs=[_ANY] * comm.n, scratch_shapes=comm.scratch(),
                               compiler_params=pltpu.CompilerParams(collective_id=comm.collective_id()))(*comm.arrs))


class _ModVec:
    def __init__(self, arr, idx):
        self.arr, self.idx = arr, idx

    def spec(self, tps, n_axes):
        idx, blk = self.idx, (1, 1, self.arr.shape[2])
        if n_axes == 1:
            return pl.BlockSpec(blk, lambda i: (i // tps * N_MOD + idx, 0, 0))
        return pl.BlockSpec(blk, lambda i, j: (i // tps * N_MOD + idx, 0, 0))


def _norm_mod_matmul(x, gn, sh, sc, wts, *, seq, tm, tn, name, comm=None):
    T, D = x.shape
    N = wts[0].shape[0]
    nw = len(wts)
    tps = seq // tm

    def body(x_ref, gn_ref, sh_ref, sc_ref, *rest):
        w_refs, h_ref, o_refs = rest[:nw], rest[nw], rest[nw + 1:]

        @pl.when(pl.program_id(1) == 0)
        def _():
            h_ref[...] = _norm_mod(x_ref[...], gn_ref[...], sh_ref[0], sc_ref[0])[0].astype(BF16)

        h = h_ref[...]
        for w_ref, o_ref in zip(w_refs, o_refs):
            o_ref[...] = _dot_nt(h, w_ref[...]).astype(o_ref.dtype)

    row = pl.BlockSpec((tm, D), lambda i, j: (i, 0))
    vec = pl.BlockSpec((1, D), lambda i, j: (0, 0))
    wspec = pl.BlockSpec((tn, D), lambda i, j: (j, 0))
    ospec = pl.BlockSpec((tm, tn), lambda i, j: (i, j))
    blocks = [((tm, D), F32), ((tm, D), BF16)] + [((tn, D), BF16), ((tm, tn), BF16)] * nw
    outs = _call(
        body, (x, gn, sh.arr, sc.arr, *wts), name=name, grid=(T // tm, N // tn),
        in_specs=[row, vec, sh.spec(tps, 2), sc.spec(tps, 2)] + [wspec] * nw,
        out_specs=[row] + [ospec] * nw,
        out_shape=[SDS((T, D), BF16)] + [SDS((T, N), BF16)] * nw,
        params=_params(2, blocks, temp_bytes=2 * _nbytes((tm, tn), F32) + 3 * _nbytes((tm, D), F32)), comm=comm)
    return outs[0], outs[1:]


def _matmul_nt(h, w, *, tm, tn, name, comm=None):
    T, D = h.shape
    N = w.shape[0]

    def body(h_ref, w_ref, o_ref):
        o_ref[...] = _dot_nt(h_ref[...], w_ref[...]).astype(o_ref.dtype)

    blocks = [((tm, D), BF16), ((tn, D), BF16), ((tm, tn), BF16)]
    return _call(
        body, (h, w), name=name, grid=(T // tm, N // tn),
        in_specs=[pl.BlockSpec((tm, D), lambda i, j: (i, 0)), pl.BlockSpec((tn, D), lambda i, j: (j, 0))],
        out_specs=[pl.BlockSpec((tm, tn), lambda i, j: (i, j))],
        out_shape=[SDS((T, N), BF16)],
        params=_params(2, blocks, temp_bytes=2 * _nbytes((tm, tn), F32)), comm=comm)[0]


def _ffn_down(a, b, wd, x, g, *, seq, tm, name, comm=None):
    T, F = a.shape
    D = wd.shape[1]
    tps = seq // tm

    def body(a_ref, b_ref, wd_ref, x_ref, g_ref, xo_ref, y_ref):
        af = a_ref[...].astype(F32)
        act = (af * _sigmoid(af) * b_ref[...].astype(F32)).astype(BF16)
        y = _dot(act, wd_ref[...])
        xo_ref[...] = x_ref[...] + (FFN_RESIDUAL * g_ref[0]) * y
        y_ref[...] = y.astype(BF16)

    wide = pl.BlockSpec((tm, F), lambda i: (i, 0))
    row = pl.BlockSpec((tm, D), lambda i: (i, 0))
    wspec = pl.BlockSpec((F, D), lambda i: (0, 0))
    blocks = [((tm, F), BF16)] * 2 + [((F, D), BF16), ((tm, D), F32), ((tm, D), F32), ((tm, D), BF16)]
    return _call(
        body, (a, b, wd, x, g.arr), name=name, grid=(T // tm,),
        in_specs=[wide, wide, wspec, row, g.spec(tps, 1)], out_specs=[row, row],
        out_shape=[SDS((T, D), F32), SDS((T, D), BF16)],
        params=_params(1, blocks, temp_bytes=3 * _nbytes((tm, F), F32)), comm=comm)


def _final_loss(x, gf, tgt, *, tm, name):
    T, D = x.shape
    nt = T // tm

    def body(x_ref, gf_ref, t_ref, dx_ref, loss_ref, dgf_ref, lacc):
        i = pl.program_id(0)
        xf = x_ref[...]
        gfv = gf_ref[...]
        rstd = lax.rsqrt(jnp.mean(xf * xf, axis=-1, keepdims=True) + EPS)
        xhat = xf * rstd
        err = xhat * gfv - t_ref[...]
        dy = err * (1.0 / D)
        dxhat = dy * gfv
        dx_ref[...] = (rstd * (dxhat - xhat * jnp.mean(dxhat * xhat, axis=-1, keepdims=True))).astype(dx_ref.dtype)
        _acc(dgf_ref, _rowsum(dy * xhat), i == 0)
        _acc(lacc, _rowsum(err * err), i == 0)

        @pl.when(i == nt - 1)
        def _():
            loss_ref[...] = jnp.broadcast_to((0.5 / D) * jnp.sum(lacc[...]), loss_ref.shape)

    row = pl.BlockSpec((tm, D), lambda i: (i, 0))
    vec = pl.BlockSpec((1, D), lambda i: (0, 0))
    lspec = pl.BlockSpec((1, 128), lambda i: (0, 0))
    blocks = [((tm, D), F32)] * 3
    return _call(
        body, (x, gf, tgt), name=name, grid=(nt,),
        in_specs=[row, vec, row], out_specs=[row, lspec, vec],
        out_shape=[SDS((T, D), GRAD_STREAM), SDS((1, 128), F32), SDS((1, D), F32)],
        scratch_shapes=[pltpu.VMEM((1, D), F32)],
        params=_params(1, blocks, temp_bytes=4 * _nbytes((tm, D), F32)), hbm_out=(0,))


def _ffn_bwd_down(dxo, g, y, wd, a, b, *, seq, tm, tn, name, comm=None):
    T, F = a.shape
    D = wd.shape[1]
    tps = seq // tm
    nb = T // seq

    def body(dxo_ref, g_ref, y_ref, wd_ref, a_ref, b_ref, dyb_ref, da_ref, db_ref, dg_ref):
        i = pl.program_id(0)

        @pl.when(pl.program_id(1) == 0)
        def _():
            dx = dxo_ref[...].astype(F32)
            dyb_ref[...] = ((FFN_RESIDUAL * g_ref[0]) * dx).astype(BF16)
            part = _rowsum(FFN_RESIDUAL * dx * y_ref[...].astype(F32))
            _acc(dg_ref, part[None], i % tps == 0)

        dact = _dot_nt(dyb_ref[...], wd_ref[...])
        af = a_ref[...].astype(F32)
        bf = b_ref[...].astype(F32)
        sg = _sigmoid(af)
        silu = af * sg
        da_ref[...] = (dact * bf * (sg + silu * (1.0 - sg))).astype(BF16)
        db_ref[...] = (dact * silu).astype(BF16)

    row = pl.BlockSpec((tm, D), lambda i, j: (i, 0))
    per_b = pl.BlockSpec((1, 1, D), lambda i, j: (i // tps, 0, 0))
    wspec = pl.BlockSpec((tn, D), lambda i, j: (j, 0))
    chunk = pl.BlockSpec((tm, tn), lambda i, j: (i, j))
    blocks = [((tm, D), F32), ((tm, D), BF16), ((tn, D), BF16), ((tm, D), BF16)] + [((tm, tn), BF16)] * 4
    return _call(
        body, (dxo, g.arr, y, wd, a, b), name=name, grid=(T // tm, F // tn),
        in_specs=[row, g.spec(tps, 2), row, wspec, chunk, chunk],
        out_specs=[row, chunk, chunk, per_b],
        out_shape=[SDS((T, D), BF16)] + [SDS((T, F), BF16)] * 2 + [SDS((nb, 1, D), F32)],
        params=_params(2, blocks, temp_bytes=6 * _nbytes((tm, tn), F32)), comm=comm)


def _matmul_norm_mod_bwd(ds, ws, x, gn, sc, dxo, *, seq, tm, name, out_dtype, comm=None):
    T, D = x.shape
    nk = len(ws)
    sizes = [len(g) for g in ds]
    ds = [d for g in ds for d in g]
    tps = seq // tm
    nb = T // seq

    def body(*refs):
        w_refs = refs[len(ds):len(ds) + nk]
        x_ref, gn_ref, sc_ref, dxo_ref, dxi_ref, dsh_ref, dsc_ref, dgn_ref = refs[len(ds) + nk:]
        i = pl.program_id(0)
        dh, at = None, 0
        for n, w_ref in zip(sizes, w_refs):
            pieces = [r[...] for r in refs[at:at + n]]
            at += n
            part = _dot(pieces[0] if n == 1 else jnp.concatenate(pieces, axis=1), w_ref[...])
            dh = part if dh is None else dh + part
        gnv = gn_ref[...]
        scv = sc_ref[0]
        _, xhat, rstd, yn = _norm_mod(x_ref[...], gnv, 0.0, scv)
        dyn = dh * (1.0 + scv)
        dxhat = dyn * gnv
        dxi_ref[...] = (dxo_ref[...].astype(F32)
                        + rstd * (dxhat - xhat * jnp.mean(dxhat * xhat, axis=-1, keepdims=True))).astype(out_dtype)
        first_of_seq = i % tps == 0
        _acc(dsh_ref, _rowsum(dh)[None], first_of_seq)
        _acc(dsc_ref, _rowsum(dh * yn)[None], first_of_seq)
        _acc(dgn_ref, _rowsum(dyn * xhat), i == 0)

    row = pl.BlockSpec((tm, D), lambda i: (i, 0))
    vec = pl.BlockSpec((1, D), lambda i: (0, 0))
    per_b = pl.BlockSpec((1, 1, D), lambda i: (i // tps, 0, 0))
    d_specs = [pl.BlockSpec((tm, d.shape[1]), lambda i: (i, 0)) for d in ds]
    w_specs = [pl.BlockSpec(w.shape, lambda i: (0, 0)) for w in ws]
    blocks = ([((tm, d.shape[1]), BF16) for d in ds] + [(w.shape, BF16) for w in ws] + [((tm, D), F32)] * 3)
    return _call(
        body, (*ds, *ws, x, gn, sc.arr, dxo), name=name, grid=(T // tm,),
        in_specs=d_specs + w_specs + [row, vec, sc.spec(tps, 1), row],
        out_specs=[row, per_b, per_b, vec],
        out_shape=[SDS((T, D), out_dtype), SDS((nb, 1, D), F32), SDS((nb, 1, D), F32), SDS((1, D), F32)],
        params=_params(1, blocks, temp_bytes=6 * _nbytes((tm, D), F32)), comm=comm)


def _layernorm_silu(yc, lg, lb):
    mu = jnp.mean(yc, axis=-1, keepdims=True)
    cen = yc - mu
    rstd = lax.rsqrt(jnp.mean(cen * cen, axis=-1, keepdims=True) + EPS)
    xh = cen * rstd
    l = xh * lg + lb
    s = _sigmoid(l)
    return l * s, xh, rstd, l, s


GATE_W = 256


def _gate_specs(tm, D, col):
    return [pl.BlockSpec((tm, GATE_W), lambda i, blk=col // GATE_W + t: (i, blk)) for t in range(D // GATE_W)]


def _gate(refs):
    return jnp.concatenate([r[...] for r in refs], axis=1).astype(F32)


def _mix_out(ao, yc, proj, wao, wco, wout, x1, g2, lg, lb, *, seq, tm, ga_col, gc_col, name, comm=None):
    T, D = x1.shape
    tps = seq // tm
    ng = D // GATE_W

    def body(ao_ref, yc_ref, *rest):
        ga_refs, gc_refs = rest[:ng], rest[ng:2 * ng]
        (wao_ref, wco_ref, wout_ref, x1_ref, g2_ref, lg_ref, lb_ref,
         x2_ref, z_ref, ya_ref, ycv_ref, cact_ref, mrg_ref) = rest[2 * ng:]
        ya = _dot(ao_ref[...], wao_ref[...])
        cact = _layernorm_silu(yc_ref[...], lg_ref[...], lb_ref[...])[0].astype(BF16)
        ycv = _dot(cact, wco_ref[...])
        merged = (_sigmoid(_gate(ga_refs)) * ya + _sigmoid(_gate(gc_refs)) * ycv).astype(BF16)
        z = _dot(merged, wout_ref[...])
        x2_ref[...] = x1_ref[...] + g2_ref[0] * z
        z_ref[...] = z.astype(BF16)
        ya_ref[...] = ya.astype(BF16)
        ycv_ref[...] = ycv.astype(BF16)
        cact_ref[...] = cact
        mrg_ref[...] = merged

    row = pl.BlockSpec((tm, D), lambda i: (i, 0))
    vec = pl.BlockSpec((1, D), lambda i: (0, 0))
    wspec = pl.BlockSpec((D, D), lambda i: (0, 0))
    gates = _gate_specs(tm, D, ga_col) + _gate_specs(tm, D, gc_col)
    blocks = ([((tm, D), BF16), ((tm, D), F32), ((tm, D), BF16), ((tm, D), BF16)] + [((D, D), BF16)] * 3
              + [((tm, D), F32)] * 2 + [((tm, D), BF16)] * 5)
    return _call(
        body, (ao, yc, *[proj] * (2 * ng), wao, wco, wout, x1, g2.arr, lg, lb), name=name, grid=(T // tm,),
        in_specs=[row, row, *gates, wspec, wspec, wspec, row, g2.spec(tps, 1), vec, vec],
        out_specs=[row] * 6,
        out_shape=[SDS((T, D), F32)] + [SDS((T, D), BF16)] * 5,
        params=_params(1, blocks, temp_bytes=8 * _nbytes((tm, D), F32)), comm=comm)


def _mix_out_bwd(dx2, g2, z, wout, proj, ya, ycv, wao, wco, yc, lg, lb, *, seq, tm, ga_col, gc_col, name,
                 comm=None):
    T, D = dx2.shape
    tps = seq // tm
    nb = T // seq
    ng = D // GATE_W

    def body(dx2_ref, g2_ref, z_ref, wout_ref, *rest):
        ga_refs, gc_refs = rest[:ng], rest[ng:2 * ng]
        (ya_ref, ycv_ref, wao_ref, wco_ref, yc_ref, lg_ref, lb_ref, dz_ref, dya_ref, dycv_ref, dga_ref, dgc_ref,
         dao_ref, dyc_ref, dg2_ref, dlg_ref, dlb_ref) = rest[2 * ng:]
        i = pl.program_id(0)
        dx = dx2_ref[...].astype(F32)
        _acc(dg2_ref, _rowsum(dx * z_ref[...].astype(F32))[None], i % tps == 0)
        dzb = (g2_ref[0] * dx).astype(BF16)
        dz_ref[...] = dzb
        dmerged = _dot_nt(dzb, wout_ref[...])
        sa = _sigmoid(_gate(ga_refs))
        sc_ = _sigmoid(_gate(gc_refs))
        dya = (dmerged * sa).astype(BF16)
        dycv = (dmerged * sc_).astype(BF16)
        dya_ref[...] = dya
        dycv_ref[...] = dycv
        dga_ref[...] = (dmerged * ya_ref[...].astype(F32) * (sa * (1.0 - sa))).astype(BF16)
        dgc_ref[...] = (dmerged * ycv_ref[...].astype(F32) * (sc_ * (1.0 - sc_))).astype(BF16)
        dao_ref[...] = _dot_nt(dya, wao_ref[...]).astype(BF16)
        dcact = _dot_nt(dycv, wco_ref[...])
        lgv = lg_ref[...]
        _, xh, rstd, l, s = _layernorm_silu(yc_ref[...], lgv, lb_ref[...])
        dl = dcact * (s * (1.0 + l * (1.0 - s)))
        _acc(dlb_ref, _rowsum(dl), i == 0)
        _acc(dlg_ref, _rowsum(dl * xh), i == 0)
        dxh = dl * lgv
        dyc_ref[...] = rstd * (dxh - jnp.mean(dxh, axis=-1, keepdims=True)
                               - xh * jnp.mean(dxh * xh, axis=-1, keepdims=True))

    row = pl.BlockSpec((tm, D), lambda i: (i, 0))
    vec = pl.BlockSpec((1, D), lambda i: (0, 0))
    per_b = pl.BlockSpec((1, 1, D), lambda i: (i // tps, 0, 0))
    wspec = pl.BlockSpec((D, D), lambda i: (0, 0))
    gates = _gate_specs(tm, D, ga_col) + _gate_specs(tm, D, gc_col)
    blocks = ([((tm, D), F32)] * 3 + [((tm, D), BF16)] * 11 + [((D, D), BF16)] * 3)
    return _call(
        body, (dx2, g2.arr, z, wout, *[proj] * (2 * ng), ya, ycv, wao, wco, yc, lg, lb), name=name,
        grid=(T // tm,),
        in_specs=[row, g2.spec(tps, 1), row, wspec, *gates, row, row, wspec, wspec, row, vec, vec],
        out_specs=[row] * 7 + [per_b, vec, vec],
        out_shape=[SDS((T, D), BF16)] * 6 + [SDS((T, D), F32), SDS((nb, 1, D), F32), SDS((1, D), F32),
                                             SDS((1, D), F32)],
        params=_params(1, blocks, temp_bytes=10 * _nbytes((tm, D), F32)), comm=comm)


Q_BLOCK = 64
BAND = Q_BLOCK + ATT_BLOCK
GROUP_ROWS = GQA_GROUP * Q_BLOCK
PAIR_W = 2 * HEAD_DIM
GROUP_W = GQA_GROUP * HEAD_DIM


def _lane_lo():
    return lax.broadcasted_iota(jnp.int32, (1, PAIR_W), 1) < HEAD_DIM


def _band_bias():
    sj = lax.broadcasted_iota(jnp.int32, (BAND, GROUP_ROWS), 0)
    qi = lax.broadcasted_iota(jnp.int32, (BAND, GROUP_ROWS), 1) & (Q_BLOCK - 1)
    rel = qi + ATT_BLOCK - sj
    bias = jnp.where(jnp.logical_and(rel >= 0, rel < ATT_BLOCK), 0.0, NEG_BIG)
    return bias, lax.broadcasted_iota(jnp.int32, (BAND, 1), 0)


def _block_bias(bias0, key_index, r0):
    return bias0 + jnp.where(key_index + r0 < ATT_BLOCK, NEG_BIG, 0.0)


def _dup_heads(src_ref, dst, seq):
    x = src_ref[...]
    i = lax.broadcasted_iota(jnp.int32, (KV_WIDTH, PAIR_W), 0)
    j = lax.broadcasted_iota(jnp.int32, (KV_WIDTH, PAIR_W), 1) & (HEAD_DIM - 1)
    for g in range(N_KV_HEADS):
        sel = jnp.where(i == j + g * HEAD_DIM, 1.0, 0.0).astype(BF16)
        dst[g, pl.ds(0, ATT_BLOCK), :] = jnp.zeros((ATT_BLOCK, PAIR_W), BF16)
        dst[g, pl.ds(ATT_BLOCK, seq), :] = _dot(x, sel).astype(BF16)


def _stack_heads(blk, g, lo):
    parts = []
    for p in range(GQA_GROUP // 2):
        pair = blk[:, g * GROUP_W + p * PAIR_W:g * GROUP_W + (p + 1) * PAIR_W]
        parts += [jnp.where(lo, pair, jnp.zeros_like(pair)), jnp.where(lo, jnp.zeros_like(pair), pair)]
    return jnp.concatenate(parts, axis=0)


def _unstack_heads(full, ref, r0, g, lo):
    for p in range(GQA_GROUP // 2):
        even = full[(2 * p) * Q_BLOCK:(2 * p + 1) * Q_BLOCK, :]
        odd = full[(2 * p + 1) * Q_BLOCK:(2 * p + 2) * Q_BLOCK, :]
        ref[pl.ds(r0, Q_BLOCK), g * GROUP_W + p * PAIR_W:g * GROUP_W + (p + 1) * PAIR_W] = (
            jnp.where(lo, even, odd).astype(ref.dtype))


def _sink_row(sink_ref, g):
    return jnp.concatenate([jnp.full((1, Q_BLOCK), sink_ref[0, g * GQA_GROUP + h], F32)
                            for h in range(GQA_GROUP)], axis=1)


def _group_probs(qs, k2, bias, sink):
    s = _dot_nt(k2, qs) * (HEAD_DIM ** -0.5) + bias
    m = jnp.maximum(jnp.max(s, axis=0, keepdims=True), sink)
    p = jnp.exp(s - m)
    psink = jnp.exp(sink - m)
    inv = 1.0 / (jnp.sum(p, axis=0, keepdims=True) + psink)
    return p * inv, psink * inv


def _attn_fwd(projp, sinks, *, seq, q_blk, k_blk, v_blk, name, comm=None):
    T = projp.shape[0]
    QW = N_Q_HEADS * HEAD_DIM
    nblk = seq // Q_BLOCK

    def body(q_ref, k_ref, v_ref, sink_ref, o_ref, k2s, v2s):
        _dup_heads(k_ref, k2s, seq)
        _dup_heads(v_ref, v2s, seq)
        lo = _lane_lo()
        bias0, key_index = _band_bias()
        sink_rows = [_sink_row(sink_ref, g) for g in range(N_KV_HEADS)]

        def blk(n, carry):
            r0 = pl.multiple_of(n * Q_BLOCK, Q_BLOCK)
            band = pl.ds(r0, BAND)
            qb = q_ref[pl.ds(r0, Q_BLOCK), :]
            bias = _block_bias(bias0, key_index, r0)
            for g in range(N_KV_HEADS):
                probs_t, _ = _group_probs(_stack_heads(qb, g, lo), k2s[g, band, :], bias, sink_rows[g])
                _unstack_heads(_dot_tn(probs_t.astype(BF16), v2s[g, band, :]), o_ref, r0, g, lo)
            return carry

        lax.fori_loop(0, nblk, blk, 0, unroll=4)

    blocks = [((seq, QW), BF16)] * 2 + [((seq, KV_WIDTH), BF16)] * 2
    return _call(
        body, (projp, projp, projp, sinks), name=name, grid=(T // seq,),
        in_specs=[pl.BlockSpec((seq, QW), lambda b: (b, q_blk)),
                  pl.BlockSpec((seq, KV_WIDTH), lambda b: (b, k_blk)),
                  pl.BlockSpec((seq, KV_WIDTH), lambda b: (b, v_blk)),
                  pl.BlockSpec(memory_space=pltpu.SMEM)],
        out_specs=[pl.BlockSpec((seq, QW), lambda b: (b, 0))],
        out_shape=[SDS((T, QW), BF16)],
        scratch_shapes=[pltpu.VMEM((N_KV_HEADS, seq + ATT_BLOCK, PAIR_W), BF16)] * 2,
        params=_params(1, blocks, temp_bytes=2 * _nbytes((N_KV_HEADS, seq + ATT_BLOCK, PAIR_W), BF16)
                       + 8 * _nbytes((BAND, GROUP_ROWS), F32)), comm=comm, hbm_out=(0,))[0]


def _attn_bwd(projp, dao, sinks, *, seq, q_blk, k_blk, v_blk, name, comm=None):
    T = projp.shape[0]
    QW = N_Q_HEADS * HEAD_DIM
    assert seq % (2 * Q_BLOCK) == 0
    nblk = seq // Q_BLOCK

    def body(q_ref, k_ref, v_ref, do_ref, sink_ref, dq_ref, dk_ref, dv_ref, dsink_ref, k2s, v2s, dkacc, dvacc):
        _dup_heads(k_ref, k2s, seq)
        _dup_heads(v_ref, v2s, seq)
        dkacc[...] = jnp.zeros(dkacc.shape, F32)
        dvacc[...] = jnp.zeros(dvacc.shape, F32)
        lane = lax.broadcasted_iota(jnp.int32, (1, PAIR_W), 1)
        lo = lane < HEAD_DIM
        bias0, key_index = _band_bias()
        sink_rows = [_sink_row(sink_ref, g) for g in range(N_KV_HEADS)]

        def blk(n, tsinks):
            tsinks = list(tsinks)
            r0 = pl.multiple_of(n * Q_BLOCK, Q_BLOCK)
            band = pl.ds(r0, BAND)
            qb = q_ref[pl.ds(r0, Q_BLOCK), :]
            dob = do_ref[pl.ds(r0, Q_BLOCK), :]
            bias = _block_bias(bias0, key_index, r0)
            for g in range(N_KV_HEADS):
                qs = _stack_heads(qb, g, lo)
                dos = _stack_heads(dob, g, lo)
                k2 = k2s[g, band, :]
                v2 = v2s[g, band, :]
                probs_t, psink = _group_probs(qs, k2, bias, sink_rows[g])
                dp_t = _dot_nt(v2, dos)
                delta = jnp.sum(probs_t * dp_t, axis=0, keepdims=True)
                ds_t = (probs_t * (dp_t - delta) * (HEAD_DIM ** -0.5)).astype(BF16)
                tsinks[g] = tsinks[g] + psink * delta
                _unstack_heads(_dot_tn(ds_t, k2), dq_ref, r0, g, lo)
                dkacc[g, band, :] = dkacc[g, band, :] + _dot(ds_t, qs)
                dvacc[g, band, :] = dvacc[g, band, :] + _dot(probs_t.astype(BF16), dos)
            return tuple(tsinks)

        def two_blocks(m, tsinks):
            return blk(2 * m + 1, blk(2 * m, tsinks))

        tsinks = lax.fori_loop(0, nblk // 2, two_blocks, (jnp.zeros((1, GROUP_ROWS), F32),) * N_KV_HEADS)
        dsink = jnp.zeros((1, PAIR_W), F32)
        for g in range(N_KV_HEADS):
            for h in range(GQA_GROUP):
                dsink = dsink + jnp.where(lane == g * GQA_GROUP + h,
                                          -jnp.sum(tsinks[g][:, h * Q_BLOCK:(h + 1) * Q_BLOCK]), 0.0)
        _acc(dsink_ref, dsink, pl.program_id(0) == 0)

        def fold(acc, g):
            a = acc[g, pl.ds(ATT_BLOCK, seq), :]
            return a + pltpu.roll(a, HEAD_DIM, 1)

        dk_ref[...] = jnp.where(lo, fold(dkacc, 0), fold(dkacc, 1)).astype(BF16)
        dv_ref[...] = jnp.where(lo, fold(dvacc, 0), fold(dvacc, 1)).astype(BF16)

    blocks = [((seq, QW), BF16)] * 3 + [((seq, KV_WIDTH), BF16)] * 4
    kv_spec_out = pl.BlockSpec((seq, KV_WIDTH), lambda b: (b, 0))
    return _call(
        body, (projp, projp, projp, dao, sinks), name=name, grid=(T // seq,),
        in_specs=[pl.BlockSpec((seq, QW), lambda b: (b, q_blk)),
                  pl.BlockSpec((seq, KV_WIDTH), lambda b: (b, k_blk)),
                  pl.BlockSpec((seq, KV_WIDTH), lambda b: (b, v_blk)),
                  pl.BlockSpec((seq, QW), lambda b: (b, 0)),
                  pl.BlockSpec(memory_space=pltpu.SMEM)],
        out_specs=[pl.BlockSpec((seq, QW), lambda b: (b, 0)), kv_spec_out, kv_spec_out,
                   pl.BlockSpec((1, 128), lambda b: (0, 0))],
        out_shape=[SDS((T, QW), BF16), SDS((T, KV_WIDTH), BF16), SDS((T, KV_WIDTH), BF16), SDS((1, 128), F32)],
        scratch_shapes=[pltpu.VMEM((N_KV_HEADS, seq + ATT_BLOCK, PAIR_W), BF16)] * 2
        + [pltpu.VMEM((N_KV_HEADS, seq + ATT_BLOCK, PAIR_W), F32)] * 2,
        params=_params(1, blocks, temp_bytes=6 * _nbytes((N_KV_HEADS, seq + ATT_BLOCK, PAIR_W), BF16)
                       + 16 * _nbytes((BAND, GROUP_ROWS), F32)), comm=comm)


SUBLANES = 8


def _sublane_shifts(win):
    n = CONV_ROWS + CONV_HALO
    return [win] + [pltpu.roll(win, n - b, 0) for b in range(1, SUBLANES)]


def _window(shifted, off):
    a = off // SUBLANES * SUBLANES
    return shifted[off % SUBLANES][a:a + CONV_ROWS, :]


def _conv_fwd(projp, w, bias, *, seq, cw, a_col, b_col, name, comm=None):
    T = projp.shape[0]
    C = w.shape[1]
    nchunk = seq // CONV_ROWS

    def body(a_ref, b_ref, w_ref, bias_ref, y_ref, upad):
        upad[pl.ds(0, CONV_HALO), :] = jnp.zeros((CONV_HALO, cw), F32)
        upad[pl.ds(CONV_HALO, seq), :] = a_ref[...].astype(F32) * _sigmoid(b_ref[...].astype(F32))
        wv = w_ref[...]
        bv = bias_ref[...]

        def chunk(r, carry):
            r0 = pl.multiple_of(r * CONV_ROWS, CONV_ROWS)
            shifted = _sublane_shifts(upad[pl.ds(r0, CONV_ROWS + CONV_HALO), :])
            acc = jnp.broadcast_to(bv, (CONV_ROWS, cw))
            for k in range(CONV_WIDTH):
                acc = acc + wv[k:k + 1, :] * _window(shifted, CONV_HALO - (CONV_WIDTH - 1) + k)
            y_ref[pl.ds(r0, CONV_ROWS), :] = acc
            return carry

        lax.fori_loop(0, nchunk, chunk, 0)

    blocks = [((seq, cw), BF16)] * 2 + [((seq, cw), F32)]
    return _call(
        body, (projp, projp, w, bias), name=name, grid=(T // seq, C // cw),
        in_specs=[pl.BlockSpec((seq, cw), lambda b, c: (b, a_col // cw + c)),
                  pl.BlockSpec((seq, cw), lambda b, c: (b, b_col // cw + c)),
                  pl.BlockSpec((CONV_WIDTH, cw), lambda b, c: (0, c)),
                  pl.BlockSpec((1, cw), lambda b, c: (0, c))],
        out_specs=[pl.BlockSpec((seq, cw), lambda b, c: (b, c))],
        out_shape=[SDS((T, C), F32)],
        scratch_shapes=[pltpu.VMEM((seq + CONV_HALO, cw), F32)],
        params=_params(2, blocks, temp_bytes=6 * _nbytes((seq, cw), F32)), comm=comm, hbm_out=(0,))[0]


def _conv_bwd(dy, projp, w, *, seq, cw, a_col, b_col, name, comm=None):
    T = projp.shape[0]
    C = w.shape[1]
    nchunk = seq // CONV_ROWS
    SUB = 8

    def body(dy_ref, a_ref, b_ref, w_ref, da_ref, db_ref, dw_ref, dbias_ref, dypad, dwp):
        first = pl.program_id(1) == 0
        dyv = dy_ref[...]
        dypad[pl.ds(0, seq), :] = dyv
        dypad[pl.ds(seq, CONV_HALO), :] = jnp.zeros((CONV_HALO, cw), F32)
        dwp[...] = jnp.zeros(dwp.shape, F32)
        wv = w_ref[...]

        def chunk(r, carry):
            r0 = pl.multiple_of(r * CONV_ROWS, CONV_ROWS)
            dy_shifts = _sublane_shifts(dypad[pl.ds(r0, CONV_ROWS + CONV_HALO), :])
            ac = a_ref[pl.ds(r0, CONV_ROWS), :].astype(F32)
            sbc = _sigmoid(b_ref[pl.ds(r0, CONV_ROWS), :].astype(F32))
            uc = ac * sbc
            du = jnp.zeros((CONV_ROWS, cw), F32)
            for k in range(CONV_WIDTH):
                dyk = _window(dy_shifts, CONV_WIDTH - 1 - k)
                du = du + wv[k:k + 1, :] * dyk
                prod = uc * dyk
                part = prod[0:SUB, :]
                for s in range(1, CONV_ROWS // SUB):
                    part = part + prod[s * SUB:(s + 1) * SUB, :]
                dwp[pl.ds(k * SUB, SUB), :] = dwp[pl.ds(k * SUB, SUB), :] + part
            da_ref[pl.ds(r0, CONV_ROWS), :] = (du * sbc).astype(BF16)
            db_ref[pl.ds(r0, CONV_ROWS), :] = (du * ac * (sbc * (1.0 - sbc))).astype(BF16)
            return carry

        lax.fori_loop(0, nchunk, chunk, 0)

        @pl.when(first)
        def _():
            dw_ref[...] = jnp.zeros(dw_ref.shape, F32)
            dbias_ref[...] = jnp.zeros(dbias_ref.shape, F32)

        for k in range(CONV_WIDTH):
            dw_ref[k:k + 1, :] = dw_ref[k:k + 1, :] + _rowsum(dwp[pl.ds(k * SUB, SUB), :])
        dbias_ref[...] = dbias_ref[...] + _rowsum(dyv)

    blocks = [((seq, cw), F32)] + [((seq, cw), BF16)] * 4
    return _call(
        body, (dy, projp, projp, w), name=name, grid=(C // cw, T // seq),
        in_specs=[pl.BlockSpec((seq, cw), lambda c, b: (b, c)),
                  pl.BlockSpec((seq, cw), lambda c, b: (b, a_col // cw + c)),
                  pl.BlockSpec((seq, cw), lambda c, b: (b, b_col // cw + c)),
                  pl.BlockSpec((CONV_WIDTH, cw), lambda c, b: (0, c))],
        out_specs=[pl.BlockSpec((seq, cw), lambda c, b: (b, c)), pl.BlockSpec((seq, cw), lambda c, b: (b, c)),
                   pl.BlockSpec((CONV_WIDTH, cw), lambda c, b: (0, c)), pl.BlockSpec((1, cw), lambda c, b: (0, c))],
        out_shape=[SDS((T, C), BF16), SDS((T, C), BF16), SDS((CONV_WIDTH, C), F32), SDS((1, C), F32)],
        scratch_shapes=[pltpu.VMEM((seq + CONV_HALO, cw), F32), pltpu.VMEM((CONV_WIDTH * SUB, cw), F32)],
        params=_params(2, blocks, temp_bytes=8 * _nbytes((seq, cw), F32)), comm=comm, hbm_out=(0, 1))


def _matmul_tn(a, b, *, name, gate=None, comm=None):
    T, M = a.shape
    N = b.shape[1]
    bm = _pick(M, (768, 512, 256))
    lhs = [a] if gate is None else [a, gate]

    def body(*refs):
        b_ref, o_ref = refs[len(lhs)], refs[len(lhs) + 1]
        av = refs[0][...]
        if gate is not None:
            af = av.astype(F32)
            av = (af * _sigmoid(af) * refs[1][...].astype(F32)).astype(BF16)
        o_ref[...] = _dot_tn(av, b_ref[...]).astype(BF16)

    blocks = [((T, bm), BF16)] * len(lhs) + [((T, N), BF16), ((bm, N), BF16)]
    return _call(
        body, (*lhs, b), name=name, grid=(M // bm,),
        in_specs=[pl.BlockSpec((T, bm), lambda i: (0, i))] * len(lhs) + [pl.BlockSpec((T, N), lambda i: (0, 0))],
        out_specs=[pl.BlockSpec((bm, N), lambda i: (i, 0))],
        out_shape=[SDS((M, N), BF16)],
        params=_params(1, blocks, temp_bytes=(2 + 4 * len(lhs)) * _nbytes((T, bm), BF16) + 2 * _nbytes((bm, N), F32)),
        comm=comm)[0]


TN_BLOCK = 256


def _matmul_tn_pieces(groups, b, *, name, comm=None):
    T, N = b.shape
    flat = [a for g in groups for a in g]
    starts, n_steps = [], 0
    for g in groups:
        width = sum(a.shape[1] for a in g)
        assert width % TN_BLOCK == 0 and (len(g) == 1 or width == TN_BLOCK), [a.shape for a in g]
        starts.append(n_steps)
        n_steps += width // TN_BLOCK

    def body(*refs):
        a_refs, b_ref, o_ref = refs[:len(flat)], refs[len(flat)], refs[len(flat) + 1]
        i = pl.program_id(0)
        at = 0
        for g, start in zip(groups, starts):
            mine = a_refs[at:at + len(g)]
            at += len(g)
            steps = sum(a.shape[1] for a in g) // TN_BLOCK

            @pl.when(jnp.logical_and(i >= start, i < start + steps))
            def _(mine=mine):
                a = mine[0][...] if len(mine) == 1 else jnp.concatenate([r[...] for r in mine], axis=1)
                o_ref[...] = _dot_tn(a, b_ref[...]).astype(BF16)

    a_specs = []
    for g, start in zip(groups, starts):
        for a in g:
            if len(g) == 1:
                last = a.shape[1] // TN_BLOCK - 1
                a_specs.append(pl.BlockSpec(
                    (T, TN_BLOCK), lambda i, start=start, last=last: (0, jnp.clip(i - start, 0, last))))
            else:
                a_specs.append(pl.BlockSpec((T, a.shape[1]), lambda i: (0, 0)))
    blocks = [((T, TN_BLOCK), BF16)] * len(flat) + [((T, N), BF16), ((TN_BLOCK, N), BF16)]
    return _call(
        body, (*flat, b), name=name, grid=(n_steps,),
        in_specs=a_specs + [pl.BlockSpec((T, N), lambda i: (0, 0))],
        out_specs=[pl.BlockSpec((TN_BLOCK, N), lambda i: (i, 0))],
        out_shape=[SDS((n_steps * TN_BLOCK, N), BF16)],
        params=_params(1, blocks, temp_bytes=2 * _nbytes((T, TN_BLOCK), BF16) + 2 * _nbytes((TN_BLOCK, N), F32)),
        comm=comm)[0]


def _sum_parts(p_ref):
    g = p_ref[0].astype(F32)
    for s in range(1, p_ref.shape[0]):
        g = g + p_ref[s].astype(F32)
    return g


def _pair_add(g, staged, *, name):
    _, R, W = g.shape
    nq = staged.shape[0]
    tr = _row_tile(R)

    def body(g_ref, s_ref, o_ref):
        mine = jnp.where(lax.axis_index("c") == 0, g_ref[0, 0].astype(F32), g_ref[0, 1].astype(F32))
        o_ref[0] = (mine + s_ref[0].astype(F32)).astype(o_ref.dtype)

    return _call(
        body, (g.reshape(nq, 2, R, W), staged), name=name, grid=(nq, R // tr),
        in_specs=[pl.BlockSpec((1, 2, tr, W), lambda q, i: (q, 0, i, 0)),
                  pl.BlockSpec((1, tr, W), lambda q, i: (q, i, 0))],
        out_specs=[pl.BlockSpec((1, tr, W), lambda q, i: (q, i, 0))],
        out_shape=[SDS((nq, R, W), g.dtype)],
        params=_params(2, [((4, tr, W), g.dtype)], temp_bytes=3 * _nbytes((tr, W), F32)))[0]


def _adamw_update(w, g, m, v):
    m = ADAM_B1 * m + (1.0 - ADAM_B1) * g
    v = ADAM_B2 * v + (1.0 - ADAM_B2) * (g * g)
    m_hat = m / (1.0 - ADAM_B1 ** ADAM_STEP)
    v_hat = v / (1.0 - ADAM_B2 ** ADAM_STEP)
    delta = -ADAM_LR * (m_hat / (jnp.sqrt(v_hat) + ADAM_EPS) + ADAM_WD * w)
    return delta, m, v


def _row_tile(R):
    return _pick(R, (256, 128, 112, 88, 64, 32, 16, 8))


def _sum8(parts, *, name):
    n, R, W = parts.shape
    tr = _row_tile(R)

    def body(p_ref, o_ref):
        o_ref[...] = _sum_parts(p_ref)

    return _call(
        body, (parts,), name=name, grid=(R // tr,),
        in_specs=[pl.BlockSpec((n, tr, W), lambda i: (0, i, 0))],
        out_specs=[pl.BlockSpec((tr, W), lambda i: (i, 0))],
        out_shape=[SDS((R, W), F32)],
        params=_params(1, [((n, tr, W), parts.dtype), ((tr, W), F32)]))[0]


def _adamw(g, w, m, v, *, name):
    R, W = w.shape
    tr = _row_tile(R)

    def body(g_ref, w_ref, m_ref, v_ref, d_ref, mo_ref, vo_ref):
        d_ref[...], mo_ref[...], vo_ref[...] = _adamw_update(w_ref[...], g_ref[...], m_ref[...], v_ref[...])

    spec = pl.BlockSpec((tr, W), lambda i: (i, 0))
    return _call(
        body, (g, w, m, v), name=name, grid=(R // tr,),
        in_specs=[spec] * 4, out_specs=[spec] * 3, out_shape=[SDS((R, W), F32)] * 3,
        params=_params(1, [((tr, W), F32)] * 7))


def _sum8_adamw(parts, w, m, v, *, name):
    R, W = w.shape
    n = parts.shape[0]
    tr = _row_tile(R)

    def body(p_ref, w_ref, m_ref, v_ref, g_ref, d_ref, mo_ref, vo_ref):
        g = _sum_parts(p_ref)
        g_ref[...] = g
        d_ref[...], mo_ref[...], vo_ref[...] = _adamw_update(w_ref[...], g, m_ref[...], v_ref[...])

    spec = pl.BlockSpec((tr, W), lambda i: (i, 0))
    return _call(
        body, (parts, w, m, v), name=name, grid=(R // tr,),
        in_specs=[pl.BlockSpec((n, tr, W), lambda i: (0, i, 0))] + [spec] * 3,
        out_specs=[spec] * 4, out_shape=[SDS((R, W), F32)] * 4,
        params=_params(1, [((n, tr, W), parts.dtype)] + [((tr, W), F32)] * 7))


def _ada_fwd(c_all, w, bias, *, name):
    NB, D = c_all.shape
    N = w.shape[1]

    def body(c_ref, w_ref, b_ref, o_ref):
        cv = c_ref[...]
        ca = (cv * _sigmoid(cv)).astype(BF16)
        o_ref[...] = _dot(ca, w_ref[...].astype(BF16)) + b_ref[...]

    full = lambda s: pl.BlockSpec(s, lambda i: (0,) * len(s))
    return _call(
        body, (c_all, w, bias), name=name, grid=(1,),
        in_specs=[full((NB, D)), full((D, N)), full((1, N))], out_specs=[full((NB, N))],
        out_shape=[SDS((NB, N), F32)],
        params=_params(1, [((D, N), F32)], temp_bytes=_nbytes((D, N), BF16)))[0]


def _ada_bwd(c_all, gmod_all, *, n_col, name):
    NB, D = c_all.shape
    N = gmod_all.shape[1]

    def body(c_ref, g_ref, gw_ref, gb_ref):
        cv = c_ref[...]
        ca = (cv * _sigmoid(cv)).astype(BF16)
        first = pl.multiple_of(_lin(_my_pos()) * n_col, 128)
        gw_ref[...] = _dot_tn(ca, g_ref[:, pl.ds(first, n_col)].astype(BF16))
        gb_ref[...] = _rowsum(g_ref[...])

    full = lambda s: pl.BlockSpec(s, lambda i: (0,) * len(s))
    return _call(
        body, (c_all, gmod_all), name=name, grid=(1,),
        in_specs=[full((NB, D)), full((NB, N))], out_specs=[full((D, n_col)), full((1, N))],
        out_shape=[SDS((D, n_col), F32), SDS((1, N), F32)],
        params=_params(1, [((D, n_col), F32), ((NB, N), F32)]))


def kernel(x, c, w_ada, b_ada, norm_ffn1_g, ffn1_w_gate, ffn1_w_up, ffn1_w_down, norm_mix_g, w_in, attn_sinks, w_attn_o, conv_w_dw, conv_b_dw, conv_ln_g, conv_ln_b, w_conv_o, w_out, norm_ffn2_g, ffn2_w_gate, ffn2_w_up, ffn2_w_down, final_norm_g, loss_target, m_w_ada, m_b_ada, m_norm_ffn1_g, m_ffn1_w_gate, m_ffn1_w_up, m_ffn1_w_down, m_norm_mix_g, m_w_in, m_attn_sinks, m_w_attn_o, m_conv_w_dw, m_conv_b_dw, m_conv_ln_g, m_conv_ln_b, m_w_conv_o, m_w_out, m_norm_ffn2_g, m_ffn2_w_gate, m_ffn2_w_up, m_ffn2_w_down, m_final_norm_g, v_w_ada, v_b_ada, v_norm_ffn1_g, v_ffn1_w_gate, v_ffn1_w_up, v_ffn1_w_down, v_norm_mix_g, v_w_in, v_attn_sinks, v_w_attn_o, v_conv_w_dw, v_conv_b_dw, v_conv_ln_g, v_conv_ln_b, v_w_conv_o, v_w_out, v_norm_ffn2_g, v_ffn2_w_gate, v_ffn2_w_up, v_ffn2_w_down, v_final_norm_g):
    B, S, D = x.shape
    T = B * S
    QW = N_Q_HEADS * HEAD_DIM
    CC = conv_w_dw.shape[2] * N_DEV
    me = _lin(_my_pos())
    xf = x.reshape(T, D)
    tgt = loss_target.reshape(T, D)
    tm = min(512, S)
    kw = dict(seq=S, tm=tm)

    p_k, p_v, p_ca = QW, QW + KV_WIDTH, QW + 2 * KV_WIDTH
    p_cb, p_ga, p_gc = p_ca + CC, p_ca + 2 * CC, p_ca + 2 * CC + D

    def col_t(w):
        return w[0].T.astype(BF16)

    def row_b(w):
        return w[0].astype(BF16)

    def rows(g):
        return g.reshape(-1, g.shape[-1])

    def blocks8(g):
        return g.reshape(N_DEV, g.shape[0] // N_DEV, g.shape[1])

    def gather(*arrs, hbm_out=False):
        return _Comm([(a, "gather") for a in arrs], hbm_out=hbm_out)

    g_wg1, g_convw, g_c = _exchange(
        [(col_t(ffn1_w_gate), "gather"), (conv_w_dw[0], "gather"), (c, "gather")], name="gather_first")
    wg1 = rows(g_wg1)
    conv_w = g_convw.transpose(1, 0, 2).reshape(CONV_WIDTH, CC)
    c_all = g_c.reshape(N_DEV * B, D)

    n_col = N_MOD * D // N_DEV
    b_cols = lax.dynamic_slice(b_ada, (0, me * n_col), (1, n_col))
    mod_cols = _ada_fwd(c_all, w_ada[0], b_cols, name="ada_fwd")
    mod_mine = _exchange([(mod_cols.reshape(N_DEV, B, n_col), "scatter")], name="scatter_mod")[0]
    mod = mod_mine.transpose(1, 0, 2).reshape(B * N_MOD, 1, D)
    sh1, sc1, g1, sh2, sc2, g2, sh3, sc3, g3 = [_ModVec(mod, i) for i in range(N_MOD)]

    F = wg1.shape[0]
    tn_f = _pick(F, (1408, 1024, 512, 256))
    tn_in = _pick(w_in.shape[2] * N_DEV, (1792, 768, 512, 256))
    gate_blk = dict(ga_col=p_ga, gc_col=p_gc)
    att_blk = dict(q_blk=0, k_blk=p_k // KV_WIDTH, v_blk=p_v // KV_WIDTH)
    conv_kw = dict(seq=S, cw=256, a_col=p_ca, b_col=p_cb)

    cm = gather(col_t(ffn1_w_up))
    h1, (a1,) = _norm_mod_matmul(xf, norm_ffn1_g, sh1, sc1, [_in_hbm(wg1)], tn=tn_f, name="ffn1_gate", comm=cm, **kw)
    wu1 = rows(cm.out[0])
    cm = gather(row_b(ffn1_w_down), hbm_out=True)
    b1 = _matmul_nt(h1, wu1, tm=tm, tn=tn_f, name="ffn1_up", comm=cm)
    wd1 = rows(cm.out[0])
    cm = gather(col_t(w_in))
    x1, y1 = _ffn_down(a1, b1, wd1, xf, g1, name="ffn1_down", comm=cm, **kw)
    winp = rows(cm.out[0])
    cm = gather(row_b(w_attn_o), row_b(w_conv_o), row_b(w_out), col_t(ffn2_w_gate))
    h2, (projp,) = _norm_mod_matmul(x1, norm_mix_g, sh2, sc2, [winp], tn=tn_in, name="mix_in", comm=cm, **kw)
    wao, wco, wout, wg2 = [rows(o) for o in cm.out]
    cm = gather(col_t(ffn2_w_up), hbm_out=True)
    ao = _in_hbm(_attn_fwd(projp, attn_sinks, seq=S, name="attn_fwd", comm=cm, **att_blk))
    wu2 = rows(cm.out[0])
    cm = gather(row_b(ffn2_w_down), hbm_out=True)
    yc = _in_hbm(_conv_fwd(projp, conv_w, conv_b_dw, name="conv_fwd", comm=cm, **conv_kw))
    wd2 = rows(cm.out[0])
    x2, z, ya, ycv, cact, merged = _mix_out(ao, yc, projp, wao, wco, wout, x1, g2, conv_ln_g, conv_ln_b,
                                            name="mix_out", **gate_blk, **kw)
    h3, (a3, b3) = _norm_mod_matmul(x2, norm_ffn2_g, sh3, sc3, [wg2, wu2], tn=tn_f, name="ffn2_up", **kw)
    x3, y3 = _ffn_down(a3, b3, wd2, x2, g3, name="ffn2_down", **kw)
    dx3, loss_row, dgf = _final_loss(_in_hbm(x3), final_norm_g[None], _in_hbm(tgt), tm=tm, name="final_loss")
    dx3 = _in_hbm(dx3)

    parts = {}

    def pair(*gs):
        return [(blocks8(g), "pair") for g in gs]

    def cross(*rs):
        return [(r, "cross") for r in rs]

    def reduce_pairs(gs, staged, names):
        return [_pair_add(blocks8(g), s, name="pair_add_" + n) for g, s, n in zip(gs, staged, names)]

    dyb3, da3, db3, dg3 = _ffn_bwd_down(dx3, g3, y3, wd2, a3, b3, tn=tn_f, name="ffn2_bwd_down", **kw)
    gwd2 = _matmul_tn(a3, dyb3, gate=b3, name="gw_ffn2_down")
    cm = _Comm(pair(gwd2))
    dx2, dsh3, dsc3, dgn3 = _matmul_norm_mod_bwd([[da3], [db3]], [wg2, wu2], x2, norm_ffn2_g, sc3, dx3,
                                                 name="ffn2_bwd_up", out_dtype=GRAD_STREAM, comm=cm, **kw)
    r_wd2, = reduce_pairs([gwd2], cm.out, ["ffn2_w_down"])
    cm = _Comm(cross(r_wd2))
    gwg2 = _matmul_tn(da3, h3, name="gw_ffn2_gate", comm=cm)
    parts["ffn2_w_down"], = cm.out
    cm = _Comm(pair(gwg2))
    gwu2 = _matmul_tn(db3, h3, name="gw_ffn2_up", comm=cm)
    r_wg2, = reduce_pairs([gwg2], cm.out, ["ffn2_w_gate"])

    cm = _Comm(cross(r_wg2) + pair(gwu2))
    dzb, dyab, dycb, dga, dgc, dao, dyc, dg2, dlng, dlnb = _mix_out_bwd(
        dx2, g2, z, wout, projp, ya, ycv, wao, wco, yc, conv_ln_g, conv_ln_b, name="mix_out_bwd", comm=cm,
        **gate_blk, **kw)
    parts["ffn2_w_gate"] = cm.out[0]
    r_wu2, = reduce_pairs([gwu2], cm.out[1:], ["ffn2_w_up"])
    gwout = _matmul_tn(merged, dzb, name="gw_out")
    gwao = _matmul_tn(ao, dyab, name="gw_attn_o")
    gwco = _matmul_tn(cact, dycb, name="gw_conv_o")
    cm = _Comm(cross(r_wu2) + pair(gwout, gwao, gwco))
    dq, dk, dv, dsinks = _attn_bwd(projp, dao, attn_sinks, seq=S, name="attn_bwd", comm=cm, **att_blk)
    parts["ffn2_w_up"] = cm.out[0]
    r_mix = reduce_pairs([gwout, gwao, gwco], cm.out[1:], ["w_out", "w_attn_o", "w_conv_o"])
    cm = _Comm(cross(*r_mix))
    dca, dcb, dconvw, dconvb = _conv_bwd(dyc, projp, conv_w, name="conv_bwd", comm=cm, **conv_kw)
    dca, dcb = _in_hbm(dca), _in_hbm(dcb)
    parts["w_out"], parts["w_attn_o"], parts["w_conv_o"] = cm.out
    gwin = _matmul_tn_pieces([[dq], [dk, dv], [dca], [dcb], [dga], [dgc]], h2, name="gw_in")
    cm = _Comm(pair(gwin))
    dx1, dsh2, dsc2, dgn2 = _matmul_norm_mod_bwd([[dq, dk, dv, dca, dcb, dga, dgc]], [winp], x1, norm_mix_g, sc2, dx2,
                                                 name="mix_in_bwd", out_dtype=GRAD_STREAM, comm=cm, **kw)
    r_win, = reduce_pairs([gwin], cm.out, ["w_in"])

    cm = _Comm(cross(r_win))
    dyb1, da1, db1, dg1 = _ffn_bwd_down(dx1, g1, y1, wd1, a1, b1, tn=tn_f, name="ffn1_bwd_down", comm=cm,
                                              **kw)
    parts["w_in"], = cm.out
    gwd1 = _matmul_tn(a1, dyb1, gate=b1, name="gw_ffn1_down")
    cm = _Comm(pair(gwd1))
    gwg1 = _matmul_tn(da1, h1, name="gw_ffn1_gate", comm=cm)
    r_wd1, = reduce_pairs([gwd1], cm.out, ["ffn1_w_down"])
    cm = _Comm(cross(r_wd1) + pair(gwg1))
    gwu1 = _matmul_tn(db1, h1, name="gw_ffn1_up", comm=cm)
    parts["ffn1_w_down"] = cm.out[0]
    r_wg1, = reduce_pairs([gwg1], cm.out[1:], ["ffn1_w_gate"])
    r_wu1, = reduce_pairs([gwu1], _exchange(pair(gwu1), name="pair_last"), ["ffn1_w_up"])
    cm = _Comm(cross(r_wg1, r_wu1))
    dx0, dsh1, dsc1, dgn1 = _matmul_norm_mod_bwd([[da1], [db1]], [wg1, wu1], xf, norm_ffn1_g, sc1, dx1,
                                                 name="ffn1_bwd_up", out_dtype=F32, comm=cm, **kw)
    parts["ffn1_w_gate"], parts["ffn1_w_up"] = cm.out

    n_small = 8
    gmod = jnp.concatenate([dsh1, dsc1, dg1, dsh2, dsc2, dg2, dsh3, dsc3, dg3], axis=1).reshape(B, N_MOD * D)
    sink_row = jnp.pad(dsinks[:, :N_Q_HEADS], ((0, 0), (0, D - N_Q_HEADS)))
    loss_pad = jnp.pad(loss_row, ((0, 0), (0, D - loss_row.shape[1])))
    small = jnp.concatenate([dgn1, dgn2, dgn3, dgf, dconvb, dlng, dlnb, sink_row, dconvw, loss_pad], axis=0)
    small_all, gmod_all = _exchange([(small, "gather"), (gmod, "gather")], name="exchange_last")
    gsmall = _sum8(small_all, name="sum_small")
    loss = gsmall[n_small + CONV_WIDTH, 0]
    g_w_ada, g_b_ada = _ada_bwd(c_all, gmod_all.reshape(N_DEV * B, N_MOD * D), n_col=n_col, name="ada_bwd")
    g_conv_w = lax.dynamic_slice(gsmall[n_small:n_small + CONV_WIDTH], (0, me * (CC // N_DEV)),
                                 (CONV_WIDTH, CC // N_DEV))

    def col_update(name, w, m, v):
        outs = _sum8_adamw(parts[name], w[0].T, m[0].T, v[0].T, name="adamw_" + name)
        return tuple(o.T for o in outs)

    def row_update(name, w, m, v):
        return tuple(_sum8_adamw(parts[name], w[0], m[0], v[0], name="adamw_" + name))

    upd = {
        "ffn1_w_gate": col_update("ffn1_w_gate", ffn1_w_gate, m_ffn1_w_gate, v_ffn1_w_gate),
        "ffn1_w_up": col_update("ffn1_w_up", ffn1_w_up, m_ffn1_w_up, v_ffn1_w_up),
        "ffn1_w_down": row_update("ffn1_w_down", ffn1_w_down, m_ffn1_w_down, v_ffn1_w_down),
        "w_in": col_update("w_in", w_in, m_w_in, v_w_in),
        "w_attn_o": row_update("w_attn_o", w_attn_o, m_w_attn_o, v_w_attn_o),
        "w_conv_o": row_update("w_conv_o", w_conv_o, m_w_conv_o, v_w_conv_o),
        "w_out": row_update("w_out", w_out, m_w_out, v_w_out),
        "ffn2_w_gate": col_update("ffn2_w_gate", ffn2_w_gate, m_ffn2_w_gate, v_ffn2_w_gate),
        "ffn2_w_up": col_update("ffn2_w_up", ffn2_w_up, m_ffn2_w_up, v_ffn2_w_up),
        "ffn2_w_down": row_update("ffn2_w_down", ffn2_w_down, m_ffn2_w_down, v_ffn2_w_down),
        "w_ada": (g_w_ada,) + tuple(_adamw(g_w_ada, w_ada[0], m_w_ada[0], v_w_ada[0], name="adamw_w_ada")),
        "conv_w_dw": (g_conv_w,) + tuple(_adamw(g_conv_w, conv_w_dw[0], m_conv_w_dw[0], v_conv_w_dw[0],
                                                name="adamw_conv_w_dw")),
    }
    for k in upd:
        upd[k] = tuple(t[None] for t in upd[k])

    def pad_sinks(t):
        return jnp.pad(t, ((0, 0), (0, D - N_Q_HEADS)))

    def pack(f1, mix, f2, fin, cb, lg, lb, sinks, bada):
        return jnp.concatenate([f1, mix, f2, fin[None], cb, lg, lb, pad_sinks(sinks), bada.reshape(N_MOD, D)], axis=0)

    w_s = pack(norm_ffn1_g, norm_mix_g, norm_ffn2_g, final_norm_g, conv_b_dw, conv_ln_g, conv_ln_b, attn_sinks, b_ada)
    m_s = pack(m_norm_ffn1_g, m_norm_mix_g, m_norm_ffn2_g, m_final_norm_g, m_conv_b_dw, m_conv_ln_g, m_conv_ln_b,
               m_attn_sinks, m_b_ada)
    v_s = pack(v_norm_ffn1_g, v_norm_mix_g, v_norm_ffn2_g, v_final_norm_g, v_conv_b_dw, v_conv_ln_g, v_conv_ln_b,
               v_attn_sinks, v_b_ada)
    g_s = jnp.concatenate([gsmall[:n_small], g_b_ada.reshape(N_MOD, D)], axis=0)
    small_out = (g_s,) + tuple(_adamw(g_s, w_s, m_s, v_s, name="adamw_vectors"))

    def unpack(t):
        return {
            "norm_ffn1_g": t[0:1], "norm_mix_g": t[1:2], "norm_ffn2_g": t[2:3], "final_norm_g": t[3],
            "conv_b_dw": t[4:5], "conv_ln_g": t[5:6], "conv_ln_b": t[6:7], "attn_sinks": t[7:8, :N_Q_HEADS],
            "b_ada": t[n_small:n_small + N_MOD].reshape(1, N_MOD * D),
        }

    small_un = [unpack(t) for t in small_out]
    for k in small_un[0]:
        upd[k] = tuple(s[k] for s in small_un)

    order = ["w_ada", "b_ada", "norm_ffn1_g", "ffn1_w_gate", "ffn1_w_up", "ffn1_w_down", "norm_mix_g", "w_in",
             "attn_sinks", "w_attn_o", "conv_w_dw", "conv_b_dw", "conv_ln_g", "conv_ln_b", "w_conv_o", "w_out",
             "norm_ffn2_g", "ffn2_w_gate", "ffn2_w_up", "ffn2_w_down", "final_norm_g"]
    grad_x = dx0.reshape(B, S, D)
    return (loss, grad_x, *[upd[k][0] for k in order], *[upd[k][1] for k in order],
            *[upd[k][2] for k in order], *[upd[k][3] for k in order])
```

```python
import dataclasses

import jax
import jax.numpy as jnp
from jax import lax
from jax.experimental import pallas as pl
from jax.experimental.pallas import tpu as pltpu

F32 = jnp.float32
BF16 = jnp.bfloat16
SDS = jax.ShapeDtypeStruct
MESH = pl.DeviceIdType.MESH

N_DEV = 8
EPS = 1e-6
HEAD_DIM = 64
N_Q_HEADS = 16
N_KV_HEADS = 2
GQA_GROUP = N_Q_HEADS // N_KV_HEADS
KV_WIDTH = N_KV_HEADS * HEAD_DIM
ATT_BLOCK = 128
CONV_WIDTH = 31
CONV_HALO = 32
CONV_ROWS = 128
N_MOD = 9
FFN_RESIDUAL = 0.5
ADAM_LR = 0.001
ADAM_B1 = 0.9
ADAM_B2 = 0.999
ADAM_EPS = 1e-08
ADAM_WD = 0.01
ADAM_STEP = 10
NEG_BIG = -1e30
GRAD_STREAM = BF16

V7X_VMEM_BYTES = 64 * 2**20
VMEM_CAP = V7X_VMEM_BYTES - 8 * 2**20


def _nbytes(shape, dtype):
    n = 1
    for s in shape:
        n *= s
    return n * jnp.dtype(dtype).itemsize


def _params(n_axes, blocks, temp_bytes=0):
    need = 2 * sum(_nbytes(s, d) for s, d in blocks) + temp_bytes + 4 * 2**20
    return pltpu.CompilerParams(dimension_semantics=("arbitrary",) * n_axes,
                                vmem_limit_bytes=int(min(max(need, 16 * 2**20), VMEM_CAP)))


def _dot_nt(a, b):
    return lax.dot_general(a, b, (((1,), (1,)), ((), ())), preferred_element_type=F32)


def _dot_tn(a, b):
    return lax.dot_general(a, b, (((0,), (0,)), ((), ())), preferred_element_type=F32)


def _dot(a, b):
    return jnp.dot(a, b, preferred_element_type=F32)


def _sigmoid(x):
    return jax.nn.sigmoid(x)


def _rowsum(v):
    return jnp.sum(v, axis=0, keepdims=True)


def _acc(ref, val, first):
    @pl.when(first)
    def _():
        ref[...] = val

    @pl.when(jnp.logical_not(first))
    def _():
        ref[...] = ref[...] + val


def _norm_mod(xf, gn, sh, sc):
    rstd = lax.rsqrt(jnp.mean(xf * xf, axis=-1, keepdims=True) + EPS)
    xhat = xf * rstd
    yn = xhat * gn
    return yn * (1.0 + sc) + sh, xhat, rstd, yn


def _pick(n, cands):
    for c in cands:
        if n % c == 0:
            return c
    return n


def _my_pos():
    return lax.axis_index("x"), lax.axis_index("y"), lax.axis_index("c")


def _peer(pos, k):
    x, y, c = pos
    return ((1 - x) if k & 4 else x, (1 - y) if k & 2 else y, (1 - c) if k & 1 else c)


def _lin(pos):
    return 4 * pos[0] + 2 * pos[1] + pos[2]


def _in_hbm(a):
    return pltpu.with_memory_space_constraint(a, pltpu.HBM)


class _Comm:
    N_COPY = N_DEV - 1
    N_CHIP = N_DEV // 2

    def __init__(self, items, hbm_out=False):
        self.hbm_out = hbm_out
        self.arrs = [a for a, _ in items]
        self.modes = [m for _, m in items]
        self.n = len(items)
        self.out = None

    def out_shape(self):
        def shape(a, m):
            return {"gather": (N_DEV,) + a.shape, "scatter": a.shape, "pair": (self.N_CHIP,) + a.shape[1:],
                    "cross": a.shape}[m]
        kind = pltpu.HBM if self.hbm_out else SDS
        return [kind(shape(a, m), a.dtype) for a, m in zip(self.arrs, self.modes)]

    def scratch(self):
        return [pltpu.SemaphoreType.DMA((self.n * self.N_COPY,)), pltpu.SemaphoreType.DMA((self.n * self.N_COPY,)),
                pltpu.SemaphoreType.DMA((self.n,))]

    def collective_id(self):
        modes = set(self.modes)
        if "scatter" in modes:
            return 3
        d2d, ici = bool(modes & {"gather", "pair"}), bool(modes & {"gather", "cross"})
        return {(True, False): 0, (False, True): 1, (True, True): 2}[(d2d, ici)]

    def barrier(self):
        x, y, c = _my_pos()
        peers = {0: [(x, y, 1 - c)],
                 1: [(1 - x, y, c), (x, 1 - y, c), (1 - x, 1 - y, c)],
                 2: [(x, y, 1 - c), (1 - x, y, c), (x, 1 - y, c), (1 - x, 1 - y, c)],
                 3: [_peer((x, y, c), k) for k in range(1, N_DEV)]}[self.collective_id()]
        sem = pltpu.get_barrier_semaphore()
        for p in peers:
            pl.semaphore_signal(sem, inc=1, device_id=p, device_id_type=MESH)
        pl.semaphore_wait(sem, len(peers))

    def _plan(self, mode, me):
        x, y, c = me
        sib = (x, y, 1 - c)
        chips = [(1 - x, y), (x, 1 - y), (1 - x, 1 - y)]

        def chip_lin(ch):
            return 2 * ch[0] + ch[1]

        if mode == "scatter":
            peers = [_peer(me, k + 1) for k in range(self.N_COPY)]
            return [(p, ("in", _lin(p)), _lin(me), _lin(p), None) for p in peers], (_lin(me), _lin(me))
        if mode == "gather":
            same = [(*ch, c) for ch in chips]
            other = [(*ch, 1 - c) for ch in chips]
            copies = [(sib, ("in", None), _lin(me), _lin(sib), None)]
            copies += [(p, ("in", None), _lin(me), _lin(p), None) for p in same]
            copies += [(sib, ("out", _lin(p)), _lin(p), _lin(o), 1 + j) for j, (p, o) in enumerate(zip(same, other))]
            return copies, (None, _lin(me))
        if mode == "pair":
            return [(sib, ("in", 2 * q + 1 - c), q, q, None) for q in range(self.N_CHIP)], None
        if mode == "cross":
            mine = chip_lin((x, y))
            return ([((*ch, c), ("in", chip_lin(ch)), mine, chip_lin(ch), None) for ch in chips], (mine, mine))
        raise ValueError(mode)

    def _copy(self, refs, me, i, k, recv):
        srcs, outs, (send_sems, recv_sems, _) = refs
        peer, (where, slot), send_slot, recv_slot, _ = self._plan(self.modes[i], me)[0][k]
        src = srcs[i] if where == "in" else outs[i]
        src = src if slot is None else src.at[slot]
        sem = i * self.N_COPY + k
        return pltpu.make_async_remote_copy(
            src_ref=src, dst_ref=outs[i].at[recv_slot if recv else send_slot], send_sem=send_sems.at[sem],
            recv_sem=recv_sems.at[sem], device_id=peer, device_id_type=MESH)

    def _local(self, refs, me, i):
        srcs, outs, (_, _, loc_sems) = refs
        local = self._plan(self.modes[i], me)[1]
        if local is None:
            return None
        own = srcs[i] if local[0] is None else srcs[i].at[local[0]]
        return pltpu.make_async_copy(own, outs[i].at[local[1]], loc_sems.at[i])

    def start(self, refs):
        me = _my_pos()
        for i in range(self.n):
            local = self._local(refs, me, i)
            if local is not None:
                local.start()
            for k, cp in enumerate(self._plan(self.modes[i], me)[0]):
                if cp[4] is None:
                    self._copy(refs, me, i, k, False).start()

    def forward(self, refs):
        me = _my_pos()
        for i in range(self.n):
            for k, cp in enumerate(self._plan(self.modes[i], me)[0]):
                if cp[4] is not None:
                    self._copy(refs, me, i, cp[4], True).wait_recv()
                    self._copy(refs, me, i, k, False).start()

    def finish(self, refs):
        me = _my_pos()
        plans = [self._plan(m, me)[0] for m in self.modes]
        for i in range(self.n):
            passed_on = [cp[4] for cp in plans[i] if cp[4] is not None]
            for k in range(len(plans[i])):
                if k not in passed_on:
                    self._copy(refs, me, i, k, True).wait_recv()
                self._copy(refs, me, i, k, False).wait_send()
            local = self._local(refs, me, i)
            if local is not None:
                local.wait()


_ANY = pl.BlockSpec(memory_space=pl.ANY)


def _call(body, args, *, name, grid, in_specs, out_specs, out_shape, params, scratch_shapes=(), comm=None,
          hbm_out=()):
    in_specs, out_specs, out_shape = list(in_specs), list(out_specs), list(out_shape)
    scratch_shapes = list(scratch_shapes)
    for k in hbm_out:
        out_shape[k] = pltpu.HBM(out_shape[k].shape, out_shape[k].dtype)
    if comm is None:
        return list(pl.pallas_call(body, name=name, grid=grid, in_specs=in_specs, out_specs=out_specs,
                                   out_shape=out_shape, scratch_shapes=scratch_shapes, compiler_params=params)(*args))
    n_in, n_out, n_scr, nc = len(in_specs), len(out_specs), len(scratch_shapes), comm.n
    n_steps = 1
    for g in grid:
        n_steps *= g

    def hosted(*refs):
        ins, c_in = refs[:n_in], refs[n_in:n_in + nc]
        outs = refs[n_in + nc:n_in + nc + n_out]
        c_out = refs[n_in + nc + n_out:n_in + 2 * nc + n_out]
        scr = refs[n_in + 2 * nc + n_out:n_in + 2 * nc + n_out + n_scr]
        sems = refs[n_in + 2 * nc + n_out + n_scr:]
        step = pl.program_id(0)
        for d in range(1, len(grid)):
            step = step * grid[d] + pl.program_id(d)
        c_refs = (c_in, c_out, sems)

        @pl.when(step == 0)
        def _():
            comm.barrier()
            comm.start(c_refs)

        if n_steps >= 3:
            @pl.when(step == n_steps - 2)
            def _():
                comm.forward(c_refs)

        body(*ins, *outs, *scr)

        @pl.when(step == n_steps - 1)
        def _():
            if n_steps < 3:
                comm.forward(c_refs)
            comm.finish(c_refs)

    res = pl.pallas_call(
        hosted, name=name, grid=grid, in_specs=in_specs + [_ANY] * nc, out_specs=out_specs + [_ANY] * nc,
        out_shape=out_shape + comm.out_shape(), scratch_shapes=scratch_shapes + comm.scratch(),
        compiler_params=dataclasses.replace(params, collective_id=comm.collective_id()))(*args, *comm.arrs)
    comm.out = list(res[n_out:])
    return list(res[:n_out])


def _exchange(items, *, name):
    comm = _Comm(items)

    def body(*refs):
        r = (refs[:comm.n], refs[comm.n:2 * comm.n], refs[2 * comm.n:])
        comm.barrier()
        comm.start(r)
        comm.forward(r)
        comm.finish(r)

    return list(pl.pallas_call(body, name=name, out_shape=comm.out_shape(), in_specs=[_ANY] * comm.n,
                               out_specs=[_ANY] * comm.n, scratch_shapes=comm.scratch(),
                               compiler_params=pltpu.CompilerParams(collective_id=comm.collective_id()))(*comm.arrs))


class _ModVec:
    def __init__(self, arr, idx):
        self.arr, self.idx = arr, idx

    def spec(self, tps, n_axes):
        idx, blk = self.idx, (1, 1, self.arr.shape[2])
        if n_axes == 1:
            return pl.BlockSpec(blk, lambda i: (i // tps * N_MOD + idx, 0, 0))
        return pl.BlockSpec(blk, lambda i, j: (i // tps * N_MOD + idx, 0, 0))


def _norm_mod_matmul(x, gn, sh, sc, wts, *, seq, tm, tn, name, comm=None):
    T, D = x.shape
    N = wts[0].shape[0]
    nw = len(wts)
    tps = seq // tm

    def body(x_ref, gn_ref, sh_ref, sc_ref, *rest):
        w_refs, h_ref, o_refs = rest[:nw], rest[nw], rest[nw + 1:]

        @pl.when(pl.program_id(1) == 0)
        def _():
            h_ref[...] = _norm_mod(x_ref[...], gn_ref[...], sh_ref[0], sc_ref[0])[0].astype(BF16)

        h = h_ref[...]
        for w_ref, o_ref in zip(w_refs, o_refs):
            o_ref[...] = _dot_nt(h, w_ref[...]).astype(o_ref.dtype)

    row = pl.BlockSpec((tm, D), lambda i, j: (i, 0))
    vec = pl.BlockSpec((1, D), lambda i, j: (0, 0))
    wspec = pl.BlockSpec((tn, D), lambda i, j: (j, 0))
    ospec = pl.BlockSpec((tm, tn), lambda i, j: (i, j))
    blocks = [((tm, D), F32), ((tm, D), BF16)] + [((tn, D), BF16), ((tm, tn), BF16)] * nw
    outs = _call(
        body, (x, gn, sh.arr, sc.arr, *wts), name=name, grid=(T // tm, N // tn),
        in_specs=[row, vec, sh.spec(tps, 2), sc.spec(tps, 2)] + [wspec] * nw,
        out_specs=[row] + [ospec] * nw,
        out_shape=[SDS((T, D), BF16)] + [SDS((T, N), BF16)] * nw,
        params=_params(2, blocks, temp_bytes=2 * _nbytes((tm, tn), F32) + 3 * _nbytes((tm, D), F32)), comm=comm)
    return outs[0], outs[1:]


def _matmul_nt(h, w, *, tm, tn, name, comm=None):
    T, D = h.shape
    N = w.shape[0]

    def body(h_ref, w_ref, o_ref):
        o_ref[...] = _dot_nt(h_ref[...], w_ref[...]).astype(o_ref.dtype)

    blocks = [((tm, D), BF16), ((tn, D), BF16), ((tm, tn), BF16)]
    return _call(
        body, (h, w), name=name, grid=(T // tm, N // tn),
        in_specs=[pl.BlockSpec((tm, D), lambda i, j: (i, 0)), pl.BlockSpec((tn, D), lambda i, j: (j, 0))],
        out_specs=[pl.BlockSpec((tm, tn), lambda i, j: (i, j))],
        out_shape=[SDS((T, N), BF16)],
        params=_params(2, blocks, temp_bytes=2 * _nbytes((tm, tn), F32)), comm=comm)[0]


def _ffn_down(a, b, wd, x, g, *, seq, tm, name, comm=None):
    T, F = a.shape
    D = wd.shape[1]
    tps = seq // tm

    def body(a_ref, b_ref, wd_ref, x_ref, g_ref, xo_ref, y_ref):
        af = a_ref[...].astype(F32)
        act = (af * _sigmoid(af) * b_ref[...].astype(F32)).astype(BF16)
        y = _dot(act, wd_ref[...])
        xo_ref[...] = x_ref[...] + (FFN_RESIDUAL * g_ref[0]) * y
        y_ref[...] = y.astype(BF16)

    wide = pl.BlockSpec((tm, F), lambda i: (i, 0))
    row = pl.BlockSpec((tm, D), lambda i: (i, 0))
    wspec = pl.BlockSpec((F, D), lambda i: (0, 0))
    blocks = [((tm, F), BF16)] * 2 + [((F, D), BF16), ((tm, D), F32), ((tm, D), F32), ((tm, D), BF16)]
    return _call(
        body, (a, b, wd, x, g.arr), name=name, grid=(T // tm,),
        in_specs=[wide, wide, wspec, row, g.spec(tps, 1)], out_specs=[row, row],
        out_shape=[SDS((T, D), F32), SDS((T, D), BF16)],
        params=_params(1, blocks, temp_bytes=3 * _nbytes((tm, F), F32)), comm=comm)


def _final_loss(x, gf, tgt, *, tm, name):
    T, D = x.shape
    nt = T // tm

    def body(x_ref, gf_ref, t_ref, dx_ref, loss_ref, dgf_ref, lacc):
        i = pl.program_id(0)
        xf = x_ref[...]
        gfv = gf_ref[...]
        rstd = lax.rsqrt(jnp.mean(xf * xf, axis=-1, keepdims=True) + EPS)
        xhat = xf * rstd
        err = xhat * gfv - t_ref[...]
        dy = err * (1.0 / D)
        dxhat = dy * gfv
        dx_ref[...] = (rstd * (dxhat - xhat * jnp.mean(dxhat * xhat, axis=-1, keepdims=True))).astype(dx_ref.dtype)
        _acc(dgf_ref, _rowsum(dy * xhat), i == 0)
        _acc(lacc, _rowsum(err * err), i == 0)

        @pl.when(i == nt - 1)
        def _():
            loss_ref[...] = jnp.broadcast_to((0.5 / D) * jnp.sum(lacc[...]), loss_ref.shape)

    row = pl.BlockSpec((tm, D), lambda i: (i, 0))
    vec = pl.BlockSpec((1, D), lambda i: (0, 0))
    lspec = pl.BlockSpec((1, 128), lambda i: (0, 0))
    blocks = [((tm, D), F32)] * 3
    return _call(
        body, (x, gf, tgt), name=name, grid=(nt,),
        in_specs=[row, vec, row], out_specs=[row, lspec, vec],
        out_shape=[SDS((T, D), GRAD_STREAM), SDS((1, 128), F32), SDS((1, D), F32)],
        scratch_shapes=[pltpu.VMEM((1, D), F32)],
        params=_params(1, blocks, temp_bytes=4 * _nbytes((tm, D), F32)), hbm_out=(0,))


def _ffn_bwd_down(dxo, g, y, wd, a, b, *, seq, tm, tn, name, comm=None):
    T, F = a.shape
    D = wd.shape[1]
    tps = seq // tm
    nb = T // seq

    def body(dxo_ref, g_ref, y_ref, wd_ref, a_ref, b_ref, dyb_ref, da_ref, db_ref, dg_ref):
        i = pl.program_id(0)

        @pl.when(pl.program_id(1) == 0)
        def _():
            dx = dxo_ref[...].astype(F32)
            dyb_ref[...] = ((FFN_RESIDUAL * g_ref[0]) * dx).astype(BF16)
            part = _rowsum(FFN_RESIDUAL * dx * y_ref[...].astype(F32))
            _acc(dg_ref, part[None], i % tps == 0)

        dact = _dot_nt(dyb_ref[...], wd_ref[...])
        af = a_ref[...].astype(F32)
        bf = b_ref[...].astype(F32)
        sg = _sigmoid(af)
        silu = af * sg
        da_ref[...] = (dact * bf * (sg + silu * (1.0 - sg))).astype(BF16)
        db_ref[...] = (dact * silu).astype(BF16)

    row = pl.BlockSpec((tm, D), lambda i, j: (i, 0))
    per_b = pl.BlockSpec((1, 1, D), lambda i, j: (i // tps, 0, 0))
    wspec = pl.BlockSpec((tn, D), lambda i, j: (j, 0))
    chunk = pl.BlockSpec((tm, tn), lambda i, j: (i, j))
    blocks = [((tm, D), F32), ((tm, D), BF16), ((tn, D), BF16), ((tm, D), BF16)] + [((tm, tn), BF16)] * 4
    return _call(
        body, (dxo, g.arr, y, wd, a, b), name=name, grid=(T // tm, F // tn),
        in_specs=[row, g.spec(tps, 2), row, wspec, chunk, chunk],
        out_specs=[row, chunk, chunk, per_b],
        out_shape=[SDS((T, D), BF16)] + [SDS((T, F), BF16)] * 2 + [SDS((nb, 1, D), F32)],
        params=_params(2, blocks, temp_bytes=6 * _nbytes((tm, tn), F32)), comm=comm)


def _matmul_norm_mod_bwd(ds, ws, x, gn, sc, dxo, *, seq, tm, name, out_dtype, comm=None):
    T, D = x.shape
    nk = len(ws)
    sizes = [len(g) for g in ds]
    ds = [d for g in ds for d in g]
    tps = seq // tm
    nb = T // seq

    def body(*refs):
        w_refs = refs[len(ds):len(ds) + nk]
        x_ref, gn_ref, sc_ref, dxo_ref, dxi_ref, dsh_ref, dsc_ref, dgn_ref = refs[len(ds) + nk:]
        i = pl.program_id(0)
        dh, at = None, 0
        for n, w_ref in zip(sizes, w_refs):
            pieces = [r[...] for r in refs[at:at + n]]
            at += n
            part = _dot(pieces[0] if n == 1 else jnp.concatenate(pieces, axis=1), w_ref[...])
            dh = part if dh is None else dh + part
        gnv = gn_ref[...]
        scv = sc_ref[0]
        _, xhat, rstd, yn = _norm_mod(x_ref[...], gnv, 0.0, scv)
        dyn = dh * (1.0 + scv)
        dxhat = dyn * gnv
        dxi_ref[...] = (dxo_ref[...].astype(F32)
                        + rstd * (dxhat - xhat * jnp.mean(dxhat * xhat, axis=-1, keepdims=True))).astype(out_dtype)
        first_of_seq = i % tps == 0
        _acc(dsh_ref, _rowsum(dh)[None], first_of_seq)
        _acc(dsc_ref, _rowsum(dh * yn)[None], first_of_seq)
        _acc(dgn_ref, _rowsum(dyn * xhat), i == 0)

    row = pl.BlockSpec((tm, D), lambda i: (i, 0))
    vec = pl.BlockSpec((1, D), lambda i: (0, 0))
    per_b = pl.BlockSpec((1, 1, D), lambda i: (i // tps, 0, 0))
    d_specs = [pl.BlockSpec((tm, d.shape[1]), lambda i: (i, 0)) for d in ds]
    w_specs = [pl.BlockSpec(w.shape, lambda i: (0, 0)) for w in ws]
    blocks = ([((tm, d.shape[1]), BF16) for d in ds] + [(w.shape, BF16) for w in ws] + [((tm, D), F32)] * 3)
    return _call(
        body, (*ds, *ws, x, gn, sc.arr, dxo), name=name, grid=(T // tm,),
        in_specs=d_specs + w_specs + [row, vec, sc.spec(tps, 1), row],
        out_specs=[row, per_b, per_b, vec],
        out_shape=[SDS((T, D), out_dtype), SDS((nb, 1, D), F32), SDS((nb, 1, D), F32), SDS((1, D), F32)],
        params=_params(1, blocks, temp_bytes=6 * _nbytes((tm, D), F32)), comm=comm)


def _layernorm_silu(yc, lg, lb):
    mu = jnp.mean(yc, axis=-1, keepdims=True)
    cen = yc - mu
    rstd = lax.rsqrt(jnp.mean(cen * cen, axis=-1, keepdims=True) + EPS)
    xh = cen * rstd
    l = xh * lg + lb
    s = _sigmoid(l)
    return l * s, xh, rstd, l, s


GATE_W = 256


def _gate_specs(tm, D, col):
    return [pl.BlockSpec((tm, GATE_W), lambda i, blk=col // GATE_W + t: (i, blk)) for t in range(D // GATE_W)]


def _gate(refs):
    return jnp.concatenate([r[...] for r in refs], axis=1).astype(F32)


def _mix_out(ao, yc, proj, wao, wco, wout, x1, g2, lg, lb, *, seq, tm, ga_col, gc_col, name, comm=None):
    T, D = x1.shape
    tps = seq // tm
    ng = D // GATE_W

    def body(ao_ref, yc_ref, *rest):
        ga_refs, gc_refs = rest[:ng], rest[ng:2 * ng]
        (wao_ref, wco_ref, wout_ref, x1_ref, g2_ref, lg_ref, lb_ref,
         x2_ref, z_ref, ya_ref, ycv_ref, cact_ref, mrg_ref) = rest[2 * ng:]
        ya = _dot(ao_ref[...], wao_ref[...])
        cact = _layernorm_silu(yc_ref[...], lg_ref[...], lb_ref[...])[0].astype(BF16)
        ycv = _dot(cact, wco_ref[...])
        merged = (_sigmoid(_gate(ga_refs)) * ya + _sigmoid(_gate(gc_refs)) * ycv).astype(BF16)
        z = _dot(merged, wout_ref[...])
        x2_ref[...] = x1_ref[...] + g2_ref[0] * z
        z_ref[...] = z.astype(BF16)
        ya_ref[...] = ya.astype(BF16)
        ycv_ref[...] = ycv.astype(BF16)
        cact_ref[...] = cact
        mrg_ref[...] = merged

    row = pl.BlockSpec((tm, D), lambda i: (i, 0))
    vec = pl.BlockSpec((1, D), lambda i: (0, 0))
    wspec = pl.BlockSpec((D, D), lambda i: (0, 0))
    gates = _gate_specs(tm, D, ga_col) + _gate_specs(tm, D, gc_col)
    blocks = ([((tm, D), BF16), ((tm, D), F32), ((tm, D), BF16), ((tm, D), BF16)] + [((D, D), BF16)] * 3
              + [((tm, D), F32)] * 2 + [((tm, D), BF16)] * 5)
    return _call(
        body, (ao, yc, *[proj] * (2 * ng), wao, wco, wout, x1, g2.arr, lg, lb), name=name, grid=(T // tm,),
        in_specs=[row, row, *gates, wspec, wspec, wspec, row, g2.spec(tps, 1), vec, vec],
        out_specs=[row] * 6,
        out_shape=[SDS((T, D), F32)] + [SDS((T, D), BF16)] * 5,
        params=_params(1, blocks, temp_bytes=8 * _nbytes((tm, D), F32)), comm=comm)


def _mix_out_bwd(dx2, g2, z, wout, proj, ya, ycv, wao, wco, yc, lg, lb, *, seq, tm, ga_col, gc_col, name,
                 comm=None):
    T, D = dx2.shape
    tps = seq // tm
    nb = T // seq
    ng = D // GATE_W

    def body(dx2_ref, g2_ref, z_ref, wout_ref, *rest):
        ga_refs, gc_refs = rest[:ng], rest[ng:2 * ng]
        (ya_ref, ycv_ref, wao_ref, wco_ref, yc_ref, lg_ref, lb_ref, dz_ref, dya_ref, dycv_ref, dga_ref, dgc_ref,
         dao_ref, dyc_ref, dg2_ref, dlg_ref, dlb_ref) = rest[2 * ng:]
        i = pl.program_id(0)
        dx = dx2_ref[...].astype(F32)
        _acc(dg2_ref, _rowsum(dx * z_ref[...].astype(F32))[None], i % tps == 0)
        dzb = (g2_ref[0] * dx).astype(BF16)
        dz_ref[...] = dzb
        dmerged = _dot_nt(dzb, wout_ref[...])
        sa = _sigmoid(_gate(ga_refs))
        sc_ = _sigmoid(_gate(gc_refs))
        dya = (dmerged * sa).astype(BF16)
        dycv = (dmerged * sc_).astype(BF16)
        dya_ref[...] = dya
        dycv_ref[...] = dycv
        dga_ref[...] = (dmerged * ya_ref[...].astype(F32) * (sa * (1.0 - sa))).astype(BF16)
        dgc_ref[...] = (dmerged * ycv_ref[...].astype(F32) * (sc_ * (1.0 - sc_))).astype(BF16)
        dao_ref[...] = _dot_nt(dya, wao_ref[...]).astype(BF16)
        dcact = _dot_nt(dycv, wco_ref[...])
        lgv = lg_ref[...]
        _, xh, rstd, l, s = _layernorm_silu(yc_ref[...], lgv, lb_ref[...])
        dl = dcact * (s * (1.0 + l * (1.0 - s)))
        _acc(dlb_ref, _rowsum(dl), i == 0)
        _acc(dlg_ref, _rowsum(dl * xh), i == 0)
        dxh = dl * lgv
        dyc_ref[...] = rstd * (dxh - jnp.mean(dxh, axis=-1, keepdims=True)
                               - xh * jnp.mean(dxh * xh, axis=-1, keepdims=True))

    row = pl.BlockSpec((tm, D), lambda i: (i, 0))
    vec = pl.BlockSpec((1, D), lambda i: (0, 0))
    per_b = pl.BlockSpec((1, 1, D), lambda i: (i // tps, 0, 0))
    wspec = pl.BlockSpec((D, D), lambda i: (0, 0))
    gates = _gate_specs(tm, D, ga_col) + _gate_specs(tm, D, gc_col)
    blocks = ([((tm, D), F32)] * 3 + [((tm, D), BF16)] * 11 + [((D, D), BF16)] * 3)
    return _call(
        body, (dx2, g2.arr, z, wout, *[proj] * (2 * ng), ya, ycv, wao, wco, yc, lg, lb), name=name,
        grid=(T // tm,),
        in_specs=[row, g2.spec(tps, 1), row, wspec, *gates, row, row, wspec, wspec, row, vec, vec],
        out_specs=[row] * 7 + [per_b, vec, vec],
        out_shape=[SDS((T, D), BF16)] * 6 + [SDS((T, D), F32), SDS((nb, 1, D), F32), SDS((1, D), F32),
                                             SDS((1, D), F32)],
        params=_params(1, blocks, temp_bytes=10 * _nbytes((tm, D), F32)), comm=comm)


Q_BLOCK = 64
BAND = Q_BLOCK + ATT_BLOCK
GROUP_ROWS = GQA_GROUP * Q_BLOCK
PAIR_W = 2 * HEAD_DIM
GROUP_W = GQA_GROUP * HEAD_DIM


def _lane_lo():
    return lax.broadcasted_iota(jnp.int32, (1, PAIR_W), 1) < HEAD_DIM


def _band_bias():
    sj = lax.broadcasted_iota(jnp.int32, (BAND, GROUP_ROWS), 0)
    qi = lax.broadcasted_iota(jnp.int32, (BAND, GROUP_ROWS), 1) & (Q_BLOCK - 1)
    rel = qi + ATT_BLOCK - sj
    bias = jnp.where(jnp.logical_and(rel >= 0, rel < ATT_BLOCK), 0.0, NEG_BIG)
    return bias, lax.broadcasted_iota(jnp.int32, (BAND, 1), 0)


def _block_bias(bias0, key_index, r0):
    return bias0 + jnp.where(key_index + r0 < ATT_BLOCK, NEG_BIG, 0.0)


def _dup_heads(src_ref, dst, seq):
    x = src_ref[...]
    i = lax.broadcasted_iota(jnp.int32, (KV_WIDTH, PAIR_W), 0)
    j = lax.broadcasted_iota(jnp.int32, (KV_WIDTH, PAIR_W), 1) & (HEAD_DIM - 1)
    for g in range(N_KV_HEADS):
        sel = jnp.where(i == j + g * HEAD_DIM, 1.0, 0.0).astype(BF16)
        dst[g, pl.ds(0, ATT_BLOCK), :] = jnp.zeros((ATT_BLOCK, PAIR_W), BF16)
        dst[g, pl.ds(ATT_BLOCK, seq), :] = _dot(x, sel).astype(BF16)


def _stack_heads(blk, g, lo):
    parts = []
    for p in range(GQA_GROUP // 2):
        pair = blk[:, g * GROUP_W + p * PAIR_W:g * GROUP_W + (p + 1) * PAIR_W]
        parts += [jnp.where(lo, pair, jnp.zeros_like(pair)), jnp.where(lo, jnp.zeros_like(pair), pair)]
    return jnp.concatenate(parts, axis=0)


def _unstack_heads(full, ref, r0, g, lo):
    for p in range(GQA_GROUP // 2):
        even = full[(2 * p) * Q_BLOCK:(2 * p + 1) * Q_BLOCK, :]
        odd = full[(2 * p + 1) * Q_BLOCK:(2 * p + 2) * Q_BLOCK, :]
        ref[pl.ds(r0, Q_BLOCK), g * GROUP_W + p * PAIR_W:g * GROUP_W + (p + 1) * PAIR_W] = (
            jnp.where(lo, even, odd).astype(ref.dtype))


def _sink_row(sink_ref, g):
    return jnp.concatenate([jnp.full((1, Q_BLOCK), sink_ref[0, g * GQA_GROUP + h], F32)
                            for h in range(GQA_GROUP)], axis=1)


def _group_probs(qs, k2, bias, sink):
    s = _dot_nt(k2, qs) * (HEAD_DIM ** -0.5) + bias
    m = jnp.maximum(jnp.max(s, axis=0, keepdims=True), sink)
    p = jnp.exp(s - m)
    psink = jnp.exp(sink - m)
    inv = 1.0 / (jnp.sum(p, axis=0, keepdims=True) + psink)
    return p * inv, psink * inv


def _attn_fwd(projp, sinks, *, seq, q_blk, k_blk, v_blk, name, comm=None):
    T = projp.shape[0]
    QW = N_Q_HEADS * HEAD_DIM
    nblk = seq // Q_BLOCK

    def body(q_ref, k_ref, v_ref, sink_ref, o_ref, k2s, v2s):
        _dup_heads(k_ref, k2s, seq)
        _dup_heads(v_ref, v2s, seq)
        lo = _lane_lo()
        bias0, key_index = _band_bias()
        sink_rows = [_sink_row(sink_ref, g) for g in range(N_KV_HEADS)]

        def blk(n, carry):
            r0 = pl.multiple_of(n * Q_BLOCK, Q_BLOCK)
            band = pl.ds(r0, BAND)
            qb = q_ref[pl.ds(r0, Q_BLOCK), :]
            bias = _block_bias(bias0, key_index, r0)
            for g in range(N_KV_HEADS):
                probs_t, _ = _group_probs(_stack_heads(qb, g, lo), k2s[g, band, :], bias, sink_rows[g])
                _unstack_heads(_dot_tn(probs_t.astype(BF16), v2s[g, band, :]), o_ref, r0, g, lo)
            return carry

        lax.fori_loop(0, nblk, blk, 0, unroll=4)

    blocks = [((seq, QW), BF16)] * 2 + [((seq, KV_WIDTH), BF16)] * 2
    return _call(
        body, (projp, projp, projp, sinks), name=name, grid=(T // seq,),
        in_specs=[pl.BlockSpec((seq, QW), lambda b: (b, q_blk)),
                  pl.BlockSpec((seq, KV_WIDTH), lambda b: (b, k_blk)),
                  pl.BlockSpec((seq, KV_WIDTH), lambda b: (b, v_blk)),
                  pl.BlockSpec(memory_space=pltpu.SMEM)],
        out_specs=[pl.BlockSpec((seq, QW), lambda b: (b, 0))],
        out_shape=[SDS((T, QW), BF16)],
        scratch_shapes=[pltpu.VMEM((N_KV_HEADS, seq + ATT_BLOCK, PAIR_W), BF16)] * 2,
        params=_params(1, blocks, temp_bytes=2 * _nbytes((N_KV_HEADS, seq + ATT_BLOCK, PAIR_W), BF16)
                       + 8 * _nbytes((BAND, GROUP_ROWS), F32)), comm=comm, hbm_out=(0,))[0]


def _attn_bwd(projp, dao, sinks, *, seq, q_blk, k_blk, v_blk, name, comm=None):
    T = projp.shape[0]
    QW = N_Q_HEADS * HEAD_DIM
    assert seq % (2 * Q_BLOCK) == 0
    nblk = seq // Q_BLOCK

    def body(q_ref, k_ref, v_ref, do_ref, sink_ref, dq_ref, dk_ref, dv_ref, dsink_ref, k2s, v2s, dkacc, dvacc):
        _dup_heads(k_ref, k2s, seq)
        _dup_heads(v_ref, v2s, seq)
        dkacc[...] = jnp.zeros(dkacc.shape, F32)
        dvacc[...] = jnp.zeros(dvacc.shape, F32)
        lane = lax.broadcasted_iota(jnp.int32, (1, PAIR_W), 1)
        lo = lane < HEAD_DIM
        bias0, key_index = _band_bias()
        sink_rows = [_sink_row(sink_ref, g) for g in range(N_KV_HEADS)]

        def blk(n, tsinks):
            tsinks = list(tsinks)
            r0 = pl.multiple_of(n * Q_BLOCK, Q_BLOCK)
            band = pl.ds(r0, BAND)
            qb = q_ref[pl.ds(r0, Q_BLOCK), :]
            dob = do_ref[pl.ds(r0, Q_BLOCK), :]
            bias = _block_bias(bias0, key_index, r0)
            for g in range(N_KV_HEADS):
                qs = _stack_heads(qb, g, lo)
                dos = _stack_heads(dob, g, lo)
                k2 = k2s[g, band, :]
                v2 = v2s[g, band, :]
                probs_t, psink = _group_probs(qs, k2, bias, sink_rows[g])
                dp_t = _dot_nt(v2, dos)
                delta = jnp.sum(probs_t * dp_t, axis=0, keepdims=True)
                ds_t = (probs_t * (dp_t - delta) * (HEAD_DIM ** -0.5)).astype(BF16)
                tsinks[g] = tsinks[g] + psink * delta
                _unstack_heads(_dot_tn(ds_t, k2), dq_ref, r0, g, lo)
                dkacc[g, band, :] = dkacc[g, band, :] + _dot(ds_t, qs)
                dvacc[g, band, :] = dvacc[g, band, :] + _dot(probs_t.astype(BF16), dos)
            return tuple(tsinks)

        def two_blocks(m, tsinks):
            return blk(2 * m + 1, blk(2 * m, tsinks))

        tsinks = lax.fori_loop(0, nblk // 2, two_blocks, (jnp.zeros((1, GROUP_ROWS), F32),) * N_KV_HEADS)
        dsink = jnp.zeros((1, PAIR_W), F32)
        for g in range(N_KV_HEADS):
            for h in range(GQA_GROUP):
                dsink = dsink + jnp.where(lane == g * GQA_GROUP + h,
                                          -jnp.sum(tsinks[g][:, h * Q_BLOCK:(h + 1) * Q_BLOCK]), 0.0)
        _acc(dsink_ref, dsink, pl.program_id(0) == 0)

        def fold(acc, g):
            a = acc[g, pl.ds(ATT_BLOCK, seq), :]
            return a + pltpu.roll(a, HEAD_DIM, 1)

        dk_ref[...] = jnp.where(lo, fold(dkacc, 0), fold(dkacc, 1)).astype(BF16)
        dv_ref[...] = jnp.where(lo, fold(dvacc, 0), fold(dvacc, 1)).astype(BF16)

    blocks = [((seq, QW), BF16)] * 3 + [((seq, KV_WIDTH), BF16)] * 4
    kv_spec_out = pl.BlockSpec((seq, KV_WIDTH), lambda b: (b, 0))
    return _call(
        body, (projp, projp, projp, dao, sinks), name=name, grid=(T // seq,),
        in_specs=[pl.BlockSpec((seq, QW), lambda b: (b, q_blk)),
                  pl.BlockSpec((seq, KV_WIDTH), lambda b: (b, k_blk)),
                  pl.BlockSpec((seq, KV_WIDTH), lambda b: (b, v_blk)),
                  pl.BlockSpec((seq, QW), lambda b: (b, 0)),
                  pl.BlockSpec(memory_space=pltpu.SMEM)],
        out_specs=[pl.BlockSpec((seq, QW), lambda b: (b, 0)), kv_spec_out, kv_spec_out,
                   pl.BlockSpec((1, 128), lambda b: (0, 0))],
        out_shape=[SDS((T, QW), BF16), SDS((T, KV_WIDTH), BF16), SDS((T, KV_WIDTH), BF16), SDS((1, 128), F32)],
        scratch_shapes=[pltpu.VMEM((N_KV_HEADS, seq + ATT_BLOCK, PAIR_W), BF16)] * 2
        + [pltpu.VMEM((N_KV_HEADS, seq + ATT_BLOCK, PAIR_W), F32)] * 2,
        params=_params(1, blocks, temp_bytes=6 * _nbytes((N_KV_HEADS, seq + ATT_BLOCK, PAIR_W), BF16)
                       + 16 * _nbytes((BAND, GROUP_ROWS), F32)), comm=comm)


SUBLANES = 8


def _sublane_shifts(win):
    n = CONV_ROWS + CONV_HALO
    return [win] + [pltpu.roll(win, n - b, 0) for b in range(1, SUBLANES)]


def _window(shifted, off):
    a = off // SUBLANES * SUBLANES
    return shifted[off % SUBLANES][a:a + CONV_ROWS, :]


def _conv_fwd(projp, w, bias, *, seq, cw, a_col, b_col, name, comm=None):
    T = projp.shape[0]
    C = w.shape[1]
    nchunk = seq // CONV_ROWS

    def body(a_ref, b_ref, w_ref, bias_ref, y_ref, upad):
        upad[pl.ds(0, CONV_HALO), :] = jnp.zeros((CONV_HALO, cw), F32)
        upad[pl.ds(CONV_HALO, seq), :] = a_ref[...].astype(F32) * _sigmoid(b_ref[...].astype(F32))
        wv = w_ref[...]
        bv = bias_ref[...]

        def chunk(r, carry):
            r0 = pl.multiple_of(r * CONV_ROWS, CONV_ROWS)
            shifted = _sublane_shifts(upad[pl.ds(r0, CONV_ROWS + CONV_HALO), :])
            acc = jnp.broadcast_to(bv, (CONV_ROWS, cw))
            for k in range(CONV_WIDTH):
                acc = acc + wv[k:k + 1, :] * _window(shifted, CONV_HALO - (CONV_WIDTH - 1) + k)
            y_ref[pl.ds(r0, CONV_ROWS), :] = acc
            return carry

        lax.fori_loop(0, nchunk, chunk, 0)

    blocks = [((seq, cw), BF16)] * 2 + [((seq, cw), F32)]
    return _call(
        body, (projp, projp, w, bias), name=name, grid=(T // seq, C // cw),
        in_specs=[pl.BlockSpec((seq, cw), lambda b, c: (b, a_col // cw + c)),
                  pl.BlockSpec((seq, cw), lambda b, c: (b, b_col // cw + c)),
                  pl.BlockSpec((CONV_WIDTH, cw), lambda b, c: (0, c)),
                  pl.BlockSpec((1, cw), lambda b, c: (0, c))],
        out_specs=[pl.BlockSpec((seq, cw), lambda b, c: (b, c))],
        out_shape=[SDS((T, C), F32)],
        scratch_shapes=[pltpu.VMEM((seq + CONV_HALO, cw), F32)],
        params=_params(2, blocks, temp_bytes=6 * _nbytes((seq, cw), F32)), comm=comm, hbm_out=(0,))[0]


def _conv_bwd(dy, projp, w, *, seq, cw, a_col, b_col, name, comm=None):
    T = projp.shape[0]
    C = w.shape[1]
    nchunk = seq // CONV_ROWS
    SUB = 8

    def body(dy_ref, a_ref, b_ref, w_ref, da_ref, db_ref, dw_ref, dbias_ref, dypad, dwp):
        first = pl.program_id(1) == 0
        dyv = dy_ref[...]
        dypad[pl.ds(0, seq), :] = dyv
        dypad[pl.ds(seq, CONV_HALO), :] = jnp.zeros((CONV_HALO, cw), F32)
        dwp[...] = jnp.zeros(dwp.shape, F32)
        wv = w_ref[...]

        def chunk(r, carry):
            r0 = pl.multiple_of(r * CONV_ROWS, CONV_ROWS)
            dy_shifts = _sublane_shifts(dypad[pl.ds(r0, CONV_ROWS + CONV_HALO), :])
            ac = a_ref[pl.ds(r0, CONV_ROWS), :].astype(F32)
            sbc = _sigmoid(b_ref[pl.ds(r0, CONV_ROWS), :].astype(F32))
            uc = ac * sbc
            du = jnp.zeros((CONV_ROWS, cw), F32)
            for k in range(CONV_WIDTH):
                dyk = _window(dy_shifts, CONV_WIDTH - 1 - k)
                du = du + wv[k:k + 1, :] * dyk
                prod = uc * dyk
                part = prod[0:SUB, :]
                for s in range(1, CONV_ROWS // SUB):
                    part = part + prod[s * SUB:(s + 1) * SUB, :]
                dwp[pl.ds(k * SUB, SUB), :] = dwp[pl.ds(k * SUB, SUB), :] + part
            da_ref[pl.ds(r0, CONV_ROWS), :] = (du * sbc).astype(BF16)
            db_ref[pl.ds(r0, CONV_ROWS), :] = (du * ac * (sbc * (1.0 - sbc))).astype(BF16)
            return carry

        lax.fori_loop(0, nchunk, chunk, 0)

        @pl.when(first)
        def _():
            dw_ref[...] = jnp.zeros(dw_ref.shape, F32)
            dbias_ref[...] = jnp.zeros(dbias_ref.shape, F32)

        for k in range(CONV_WIDTH):
            dw_ref[k:k + 1, :] = dw_ref[k:k + 1, :] + _rowsum(dwp[pl.ds(k * SUB, SUB), :])
        dbias_ref[...] = dbias_ref[...] + _rowsum(dyv)

    blocks = [((seq, cw), F32)] + [((seq, cw), BF16)] * 4
    return _call(
        body, (dy, projp, projp, w), name=name, grid=(C // cw, T // seq),
        in_specs=[pl.BlockSpec((seq, cw), lambda c, b: (b, c)),
                  pl.BlockSpec((seq, cw), lambda c, b: (b, a_col // cw + c)),
                  pl.BlockSpec((seq, cw), lambda c, b: (b, b_col // cw + c)),
                  pl.BlockSpec((CONV_WIDTH, cw), lambda c, b: (0, c))],
        out_specs=[pl.BlockSpec((seq, cw), lambda c, b: (b, c)), pl.BlockSpec((seq, cw), lambda c, b: (b, c)),
                   pl.BlockSpec((CONV_WIDTH, cw), lambda c, b: (0, c)), pl.BlockSpec((1, cw), lambda c, b: (0, c))],
        out_shape=[SDS((T, C), BF16), SDS((T, C), BF16), SDS((CONV_WIDTH, C), F32), SDS((1, C), F32)],
        scratch_shapes=[pltpu.VMEM((seq + CONV_HALO, cw), F32), pltpu.VMEM((CONV_WIDTH * SUB, cw), F32)],
        params=_params(2, blocks, temp_bytes=8 * _nbytes((seq, cw), F32)), comm=comm, hbm_out=(0, 1))


def _matmul_tn(a, b, *, name, gate=None, comm=None):
    T, M = a.shape
    N = b.shape[1]
    bm = _pick(M, (768, 512, 256))
    lhs = [a] if gate is None else [a, gate]

    def body(*refs):
        b_ref, o_ref = refs[len(lhs)], refs[len(lhs) + 1]
        av = refs[0][...]
        if gate is not None:
            af = av.astype(F32)
            av = (af * _sigmoid(af) * refs[1][...].astype(F32)).astype(BF16)
        o_ref[...] = _dot_tn(av, b_ref[...]).astype(BF16)

    blocks = [((T, bm), BF16)] * len(lhs) + [((T, N), BF16), ((bm, N), BF16)]
    return _call(
        body, (*lhs, b), name=name, grid=(M // bm,),
        in_specs=[pl.BlockSpec((T, bm), lambda i: (0, i))] * len(lhs) + [pl.BlockSpec((T, N), lambda i: (0, 0))],
        out_specs=[pl.BlockSpec((bm, N), lambda i: (i, 0))],
        out_shape=[SDS((M, N), BF16)],
        params=_params(1, blocks, temp_bytes=(2 + 4 * len(lhs)) * _nbytes((T, bm), BF16) + 2 * _nbytes((bm, N), F32)),
        comm=comm)[0]


TN_BLOCK = 256


def _matmul_tn_pieces(groups, b, *, name, comm=None):
    T, N = b.shape
    flat = [a for g in groups for a in g]
    starts, n_steps = [], 0
    for g in groups:
        width = sum(a.shape[1] for a in g)
        assert width % TN_BLOCK == 0 and (len(g) == 1 or width == TN_BLOCK), [a.shape for a in g]
        starts.append(n_steps)
        n_steps += width // TN_BLOCK

    def body(*refs):
        a_refs, b_ref, o_ref = refs[:len(flat)], refs[len(flat)], refs[len(flat) + 1]
        i = pl.program_id(0)
        at = 0
        for g, start in zip(groups, starts):
            mine = a_refs[at:at + len(g)]
            at += len(g)
            steps = sum(a.shape[1] for a in g) // TN_BLOCK

            @pl.when(jnp.logical_and(i >= start, i < start + steps))
            def _(mine=mine):
                a = mine[0][...] if len(mine) == 1 else jnp.concatenate([r[...] for r in mine], axis=1)
                o_ref[...] = _dot_tn(a, b_ref[...]).astype(BF16)

    a_specs = []
    for g, start in zip(groups, starts):
        for a in g:
            if len(g) == 1:
                last = a.shape[1] // TN_BLOCK - 1
                a_specs.append(pl.BlockSpec(
                    (T, TN_BLOCK), lambda i, start=start, last=last: (0, jnp.clip(i - start, 0, last))))
            else:
                a_specs.append(pl.BlockSpec((T, a.shape[1]), lambda i: (0, 0)))
    blocks = [((T, TN_BLOCK), BF16)] * len(flat) + [((T, N), BF16), ((TN_BLOCK, N), BF16)]
    return _call(
        body, (*flat, b), name=name, grid=(n_steps,),
        in_specs=a_specs + [pl.BlockSpec((T, N), lambda i: (0, 0))],
        out_specs=[pl.BlockSpec((TN_BLOCK, N), lambda i: (i, 0))],
        out_shape=[SDS((n_steps * TN_BLOCK, N), BF16)],
        params=_params(1, blocks, temp_bytes=2 * _nbytes((T, TN_BLOCK), BF16) + 2 * _nbytes((TN_BLOCK, N), F32)),
        comm=comm)[0]


def _sum_parts(p_ref):
    g = p_ref[0].astype(F32)
    for s in range(1, p_ref.shape[0]):
        g = g + p_ref[s].astype(F32)
    return g


def _pair_add(g, staged, *, name):
    _, R, W = g.shape
    nq = staged.shape[0]
    tr = _row_tile(R)

    def body(g_ref, s_ref, o_ref):
        mine = jnp.where(lax.axis_index("c") == 0, g_ref[0, 0].astype(F32), g_ref[0, 1].astype(F32))
        o_ref[0] = (mine + s_ref[0].astype(F32)).astype(o_ref.dtype)

    return _call(
        body, (g.reshape(nq, 2, R, W), staged), name=name, grid=(nq, R // tr),
        in_specs=[pl.BlockSpec((1, 2, tr, W), lambda q, i: (q, 0, i, 0)),
                  pl.BlockSpec((1, tr, W), lambda q, i: (q, i, 0))],
        out_specs=[pl.BlockSpec((1, tr, W), lambda q, i: (q, i, 0))],
        out_shape=[SDS((nq, R, W), g.dtype)],
        params=_params(2, [((4, tr, W), g.dtype)], temp_bytes=3 * _nbytes((tr, W), F32)))[0]


def _adamw_update(w, g, m, v):
    m = ADAM_B1 * m + (1.0 - ADAM_B1) * g
    v = ADAM_B2 * v + (1.0 - ADAM_B2) * (g * g)
    m_hat = m / (1.0 - ADAM_B1 ** ADAM_STEP)
    v_hat = v / (1.0 - ADAM_B2 ** ADAM_STEP)
    delta = -ADAM_LR * (m_hat / (jnp.sqrt(v_hat) + ADAM_EPS) + ADAM_WD * w)
    return delta, m, v


def _row_tile(R):
    return _pick(R, (256, 128, 112, 88, 64, 32, 16, 8))


def _sum8(parts, *, name):
    n, R, W = parts.shape
    tr = _row_tile(R)

    def body(p_ref, o_ref):
        o_ref[...] = _sum_parts(p_ref)

    return _call(
        body, (parts,), name=name, grid=(R // tr,),
        in_specs=[pl.BlockSpec((n, tr, W), lambda i: (0, i, 0))],
        out_specs=[pl.BlockSpec((tr, W), lambda i: (i, 0))],
        out_shape=[SDS((R, W), F32)],
        params=_params(1, [((n, tr, W), parts.dtype), ((tr, W), F32)]))[0]


def _adamw(g, w, m, v, *, name):
    R, W = w.shape
    tr = _row_tile(R)

    def body(g_ref, w_ref, m_ref, v_ref, d_ref, mo_ref, vo_ref):
        d_ref[...], mo_ref[...], vo_ref[...] = _adamw_update(w_ref[...], g_ref[...], m_ref[...], v_ref[...])

    spec = pl.BlockSpec((tr, W), lambda i: (i, 0))
    return _call(
        body, (g, w, m, v), name=name, grid=(R // tr,),
        in_specs=[spec] * 4, out_specs=[spec] * 3, out_shape=[SDS((R, W), F32)] * 3,
        params=_params(1, [((tr, W), F32)] * 7))


def _sum8_adamw(parts, w, m, v, *, name):
    R, W = w.shape
    n = parts.shape[0]
    tr = _row_tile(R)

    def body(p_ref, w_ref, m_ref, v_ref, g_ref, d_ref, mo_ref, vo_ref):
        g = _sum_parts(p_ref)
        g_ref[...] = g
        d_ref[...], mo_ref[...], vo_ref[...] = _adamw_update(w_ref[...], g, m_ref[...], v_ref[...])

    spec = pl.BlockSpec((tr, W), lambda i: (i, 0))
    return _call(
        body, (parts, w, m, v), name=name, grid=(R // tr,),
        in_specs=[pl.BlockSpec((n, tr, W), lambda i: (0, i, 0))] + [spec] * 3,
        out_specs=[spec] * 4, out_shape=[SDS((R, W), F32)] * 4,
        params=_params(1, [((n, tr, W), parts.dtype)] + [((tr, W), F32)] * 7))


def _ada_fwd(c_all, w, bias, *, name):
    NB, D = c_all.shape
    N = w.shape[1]

    def body(c_ref, w_ref, b_ref, o_ref):
        cv = c_ref[...]
        ca = (cv * _sigmoid(cv)).astype(BF16)
        o_ref[...] = _dot(ca, w_ref[...].astype(BF16)) + b_ref[...]

    full = lambda s: pl.BlockSpec(s, lambda i: (0,) * len(s))
    return _call(
        body, (c_all, w, bias), name=name, grid=(1,),
        in_specs=[full((NB, D)), full((D, N)), full((1, N))], out_specs=[full((NB, N))],
        out_shape=[SDS((NB, N), F32)],
        params=_params(1, [((D, N), F32)], temp_bytes=_nbytes((D, N), BF16)))[0]


def _ada_bwd(c_all, gmod_all, *, n_col, name):
    NB, D = c_all.shape
    N = gmod_all.shape[1]

    def body(c_ref, g_ref, gw_ref, gb_ref):
        cv = c_ref[...]
        ca = (cv * _sigmoid(cv)).astype(BF16)
        first = pl.multiple_of(_lin(_my_pos()) * n_col, 128)
        gw_ref[...] = _dot_tn(ca, g_ref[:, pl.ds(first, n_col)].astype(BF16))
        gb_ref[...] = _rowsum(g_ref[...])

    full = lambda s: pl.BlockSpec(s, lambda i: (0,) * len(s))
    return _call(
        body, (c_all, gmod_all), name=name, grid=(1,),
        in_specs=[full((NB, D)), full((NB, N))], out_specs=[full((D, n_col)), full((1, N))],
        out_shape=[SDS((D, n_col), F32), SDS((1, N), F32)],
        params=_params(1, [((D, n_col), F32), ((NB, N), F32)]))


def kernel(x, c, w_ada, b_ada, norm_ffn1_g, ffn1_w_gate, ffn1_w_up, ffn1_w_down, norm_mix_g, w_in, attn_sinks, w_attn_o, conv_w_dw, conv_b_dw, conv_ln_g, conv_ln_b, w_conv_o, w_out, norm_ffn2_g, ffn2_w_gate, ffn2_w_up, ffn2_w_down, final_norm_g, loss_target, m_w_ada, m_b_ada, m_norm_ffn1_g, m_ffn1_w_gate, m_ffn1_w_up, m_ffn1_w_down, m_norm_mix_g, m_w_in, m_attn_sinks, m_w_attn_o, m_conv_w_dw, m_conv_b_dw, m_conv_ln_g, m_conv_ln_b, m_w_conv_o, m_w_out, m_norm_ffn2_g, m_ffn2_w_gate, m_ffn2_w_up, m_ffn2_w_down, m_final_norm_g, v_w_ada, v_b_ada, v_norm_ffn1_g, v_ffn1_w_gate, v_ffn1_w_up, v_ffn1_w_down, v_norm_mix_g, v_w_in, v_attn_sinks, v_w_attn_o, v_conv_w_dw, v_conv_b_dw, v_conv_ln_g, v_conv_ln_b, v_w_conv_o, v_w_out, v_norm_ffn2_g, v_ffn2_w_gate, v_ffn2_w_up, v_ffn2_w_down, v_final_norm_g):
    B, S, D = x.shape
    T = B * S
    QW = N_Q_HEADS * HEAD_DIM
    CC = conv_w_dw.shape[2] * N_DEV
    me = _lin(_my_pos())
    xf = x.reshape(T, D)
    tgt = loss_target.reshape(T, D)
    tm = min(512, S)
    kw = dict(seq=S, tm=tm)

    p_k, p_v, p_ca = QW, QW + KV_WIDTH, QW + 2 * KV_WIDTH
    p_cb, p_ga, p_gc = p_ca + CC, p_ca + 2 * CC, p_ca + 2 * CC + D

    def col_t(w):
        return w[0].T.astype(BF16)

    def row_b(w):
        return w[0].astype(BF16)

    def rows(g):
        return g.reshape(-1, g.shape[-1])

    def blocks8(g):
        return g.reshape(N_DEV, g.shape[0] // N_DEV, g.shape[1])

    def gather(*arrs, hbm_out=False):
        return _Comm([(a, "gather") for a in arrs], hbm_out=hbm_out)

    g_wg1, g_convw, g_c = _exchange(
        [(col_t(ffn1_w_gate), "gather"), (conv_w_dw[0], "gather"), (c, "gather")], name="gather_first")
    wg1 = rows(g_wg1)
    conv_w = g_convw.transpose(1, 0, 2).reshape(CONV_WIDTH, CC)
    c_all = g_c.reshape(N_DEV * B, D)

    n_col = N_MOD * D // N_DEV
    b_cols = lax.dynamic_slice(b_ada, (0, me * n_col), (1, n_col))
    mod_cols = _ada_fwd(c_all, w_ada[0], b_cols, name="ada_fwd")
    mod_mine = _exchange([(mod_cols.reshape(N_DEV, B, n_col), "scatter")], name="scatter_mod")[0]
    mod = mod_mine.transpose(1, 0, 2).reshape(B * N_MOD, 1, D)
    sh1, sc1, g1, sh2, sc2, g2, sh3, sc3, g3 = [_ModVec(mod, i) for i in range(N_MOD)]

    F = wg1.shape[0]
    tn_f = _pick(F, (1408, 1024, 512, 256))
    tn_in = _pick(w_in.shape[2] * N_DEV, (1792, 768, 512, 256))
    gate_blk = dict(ga_col=p_ga, gc_col=p_gc)
    att_blk = dict(q_blk=0, k_blk=p_k // KV_WIDTH, v_blk=p_v // KV_WIDTH)
    conv_kw = dict(seq=S, cw=256, a_col=p_ca, b_col=p_cb)

    cm = gather(col_t(ffn1_w_up))
    h1, (a1,) = _norm_mod_matmul(xf, norm_ffn1_g, sh1, sc1, [wg1], tn=tn_f, name="ffn1_gate", comm=cm, **kw)
    wu1 = rows(cm.out[0])
    cm = gather(row_b(ffn1_w_down), hbm_out=True)
    b1 = _matmul_nt(h1, wu1, tm=tm, tn=tn_f, name="ffn1_up", comm=cm)
    wd1 = rows(cm.out[0])
    cm = gather(col_t(w_in))
    x1, y1 = _ffn_down(a1, b1, wd1, xf, g1, name="ffn1_down", comm=cm, **kw)
    winp = rows(cm.out[0])
    cm = gather(row_b(w_attn_o), row_b(w_conv_o), row_b(w_out), col_t(ffn2_w_gate))
    h2, (projp,) = _norm_mod_matmul(x1, norm_mix_g, sh2, sc2, [winp], tn=tn_in, name="mix_in", comm=cm, **kw)
    wao, wco, wout, wg2 = [rows(o) for o in cm.out]
    cm = gather(col_t(ffn2_w_up), hbm_out=True)
    ao = _in_hbm(_attn_fwd(projp, attn_sinks, seq=S, name="attn_fwd", comm=cm, **att_blk))
    wu2 = rows(cm.out[0])
    cm = gather(row_b(ffn2_w_down), hbm_out=True)
    yc = _in_hbm(_conv_fwd(projp, conv_w, conv_b_dw, name="conv_fwd", comm=cm, **conv_kw))
    wd2 = rows(cm.out[0])
    x2, z, ya, ycv, cact, merged = _mix_out(ao, yc, projp, wao, wco, wout, x1, g2, conv_ln_g, conv_ln_b,
                                            name="mix_out", **gate_blk, **kw)
    h3, (a3, b3) = _norm_mod_matmul(x2, norm_ffn2_g, sh3, sc3, [wg2, wu2], tn=tn_f, name="ffn2_up", **kw)
    x3, y3 = _ffn_down(a3, b3, wd2, x2, g3, name="ffn2_down", **kw)
    dx3, loss_row, dgf = _final_loss(_in_hbm(x3), final_norm_g[None], _in_hbm(tgt), tm=tm, name="final_loss")
    dx3 = _in_hbm(dx3)

    parts = {}

    def pair(*gs):
        return [(blocks8(g), "pair") for g in gs]

    def cross(*rs):
        return [(r, "cross") for r in rs]

    def reduce_pairs(gs, staged, names):
        return [_pair_add(blocks8(g), s, name="pair_add_" + n) for g, s, n in zip(gs, staged, names)]

    dyb3, da3, db3, dg3 = _ffn_bwd_down(dx3, g3, y3, wd2, a3, b3, tn=tn_f, name="ffn2_bwd_down", **kw)
    gwd2 = _matmul_tn(a3, dyb3, gate=b3, name="gw_ffn2_down")
    cm = _Comm(pair(gwd2))
    dx2, dsh3, dsc3, dgn3 = _matmul_norm_mod_bwd([[da3], [db3]], [wg2, wu2], x2, norm_ffn2_g, sc3, dx3,
                                                 name="ffn2_bwd_up", out_dtype=GRAD_STREAM, comm=cm, **kw)
    r_wd2, = reduce_pairs([gwd2], cm.out, ["ffn2_w_down"])
    cm = _Comm(cross(r_wd2))
    gwg2 = _matmul_tn(da3, h3, name="gw_ffn2_gate", comm=cm)
    parts["ffn2_w_down"], = cm.out
    cm = _Comm(pair(gwg2))
    gwu2 = _matmul_tn(db3, h3, name="gw_ffn2_up", comm=cm)
    r_wg2, = reduce_pairs([gwg2], cm.out, ["ffn2_w_gate"])

    cm = _Comm(cross(r_wg2) + pair(gwu2))
    dzb, dyab, dycb, dga, dgc, dao, dyc, dg2, dlng, dlnb = _mix_out_bwd(
        dx2, g2, z, wout, projp, ya, ycv, wao, wco, yc, conv_ln_g, conv_ln_b, name="mix_out_bwd", comm=cm,
        **gate_blk, **kw)
    parts["ffn2_w_gate"] = cm.out[0]
    r_wu2, = reduce_pairs([gwu2], cm.out[1:], ["ffn2_w_up"])
    gwout = _matmul_tn(merged, dzb, name="gw_out")
    gwao = _matmul_tn(ao, dyab, name="gw_attn_o")
    gwco = _matmul_tn(cact, dycb, name="gw_conv_o")
    cm = _Comm(cross(r_wu2) + pair(gwout, gwao, gwco))
    dq, dk, dv, dsinks = _attn_bwd(projp, dao, attn_sinks, seq=S, name="attn_bwd", comm=cm, **att_blk)
    parts["ffn2_w_up"] = cm.out[0]
    r_mix = reduce_pairs([gwout, gwao, gwco], cm.out[1:], ["w_out", "w_attn_o", "w_conv_o"])
    cm = _Comm(cross(*r_mix))
    dca, dcb, dconvw, dconvb = _conv_bwd(dyc, projp, conv_w, name="conv_bwd", comm=cm, **conv_kw)
    dca, dcb = _in_hbm(dca), _in_hbm(dcb)
    parts["w_out"], parts["w_attn_o"], parts["w_conv_o"] = cm.out
    gwin = _matmul_tn_pieces([[dq], [dk, dv], [dca], [dcb], [dga], [dgc]], h2, name="gw_in")
    cm = _Comm(pair(gwin))
    dx1, dsh2, dsc2, dgn2 = _matmul_norm_mod_bwd([[dq, dk, dv, dca, dcb, dga, dgc]], [winp], x1, norm_mix_g, sc2, dx2,
                                                 name="mix_in_bwd", out_dtype=GRAD_STREAM, comm=cm, **kw)
    r_win, = reduce_pairs([gwin], cm.out, ["w_in"])

    cm = _Comm(cross(r_win))
    dyb1, da1, db1, dg1 = _ffn_bwd_down(dx1, g1, y1, wd1, a1, b1, tn=tn_f, name="ffn1_bwd_down", comm=cm,
                                              **kw)
    parts["w_in"], = cm.out
    gwd1 = _matmul_tn(a1, dyb1, gate=b1, name="gw_ffn1_down")
    cm = _Comm(pair(gwd1))
    gwg1 = _matmul_tn(da1, h1, name="gw_ffn1_gate", comm=cm)
    r_wd1, = reduce_pairs([gwd1], cm.out, ["ffn1_w_down"])
    cm = _Comm(cross(r_wd1) + pair(gwg1))
    gwu1 = _matmul_tn(db1, h1, name="gw_ffn1_up", comm=cm)
    parts["ffn1_w_down"] = cm.out[0]
    r_wg1, = reduce_pairs([gwg1], cm.out[1:], ["ffn1_w_gate"])
    r_wu1, = reduce_pairs([gwu1], _exchange(pair(gwu1), name="pair_last"), ["ffn1_w_up"])
    cm = _Comm(cross(r_wg1, r_wu1))
    dx0, dsh1, dsc1, dgn1 = _matmul_norm_mod_bwd([[da1], [db1]], [wg1, wu1], xf, norm_ffn1_g, sc1, dx1,
                                                 name="ffn1_bwd_up", out_dtype=F32, comm=cm, **kw)
    parts["ffn1_w_gate"], parts["ffn1_w_up"] = cm.out

    n_small = 8
    gmod = jnp.concatenate([dsh1, dsc1, dg1, dsh2, dsc2, dg2, dsh3, dsc3, dg3], axis=1).reshape(B, N_MOD * D)
    sink_row = jnp.pad(dsinks[:, :N_Q_HEADS], ((0, 0), (0, D - N_Q_HEADS)))
    loss_pad = jnp.pad(loss_row, ((0, 0), (0, D - loss_row.shape[1])))
    small = jnp.concatenate([dgn1, dgn2, dgn3, dgf, dconvb, dlng, dlnb, sink_row, dconvw, loss_pad], axis=0)
    small_all, gmod_all = _exchange([(small, "gather"), (gmod, "gather")], name="exchange_last")
    gsmall = _sum8(small_all, name="sum_small")
    loss = gsmall[n_small + CONV_WIDTH, 0]
    g_w_ada, g_b_ada = _ada_bwd(c_all, gmod_all.reshape(N_DEV * B, N_MOD * D), n_col=n_col, name="ada_bwd")
    g_conv_w = lax.dynamic_slice(gsmall[n_small:n_small + CONV_WIDTH], (0, me * (CC // N_DEV)),
                                 (CONV_WIDTH, CC // N_DEV))

    def col_update(name, w, m, v):
        outs = _sum8_adamw(parts[name], w[0].T, m[0].T, v[0].T, name="adamw_" + name)
        return tuple(o.T for o in outs)

    def row_update(name, w, m, v):
        return tuple(_sum8_adamw(parts[name], w[0], m[0], v[0], name="adamw_" + name))

    upd = {
        "ffn1_w_gate": col_update("ffn1_w_gate", ffn1_w_gate, m_ffn1_w_gate, v_ffn1_w_gate),
        "ffn1_w_up": col_update("ffn1_w_up", ffn1_w_up, m_ffn1_w_up, v_ffn1_w_up),
        "ffn1_w_down": row_update("ffn1_w_down", ffn1_w_down, m_ffn1_w_down, v_ffn1_w_down),
        "w_in": col_update("w_in", w_in, m_w_in, v_w_in),
        "w_attn_o": row_update("w_attn_o", w_attn_o, m_w_attn_o, v_w_attn_o),
        "w_conv_o": row_update("w_conv_o", w_conv_o, m_w_conv_o, v_w_conv_o),
        "w_out": row_update("w_out", w_out, m_w_out, v_w_out),
        "ffn2_w_gate": col_update("ffn2_w_gate", ffn2_w_gate, m_ffn2_w_gate, v_ffn2_w_gate),
        "ffn2_w_up": col_update("ffn2_w_up", ffn2_w_up, m_ffn2_w_up, v_ffn2_w_up),
        "ffn2_w_down": row_update("ffn2_w_down", ffn2_w_down, m_ffn2_w_down, v_ffn2_w_down),
        "w_ada": (g_w_ada,) + tuple(_adamw(g_w_ada, w_ada[0], m_w_ada[0], v_w_ada[0], name="adamw_w_ada")),
        "conv_w_dw": (g_conv_w,) + tuple(_adamw(g_conv_w, conv_w_dw[0], m_conv_w_dw[0], v_conv_w_dw[0],
                                                name="adamw_conv_w_dw")),
    }
    for k in upd:
        upd[k] = tuple(t[None] for t in upd[k])

    def pad_sinks(t):
        return jnp.pad(t, ((0, 0), (0, D - N_Q_HEADS)))

    def pack(f1, mix, f2, fin, cb, lg, lb, sinks, bada):
        return jnp.concatenate([f1, mix, f2, fin[None], cb, lg, lb, pad_sinks(sinks), bada.reshape(N_MOD, D)], axis=0)

    w_s = pack(norm_ffn1_g, norm_mix_g, norm_ffn2_g, final_norm_g, conv_b_dw, conv_ln_g, conv_ln_b, attn_sinks, b_ada)
    m_s = pack(m_norm_ffn1_g, m_norm_mix_g, m_norm_ffn2_g, m_final_norm_g, m_conv_b_dw, m_conv_ln_g, m_conv_ln_b,
               m_attn_sinks, m_b_ada)
    v_s = pack(v_norm_ffn1_g, v_norm_mix_g, v_norm_ffn2_g, v_final_norm_g, v_conv_b_dw, v_conv_ln_g, v_conv_ln_b,
               v_attn_sinks, v_b_ada)
    g_s = jnp.concatenate([gsmall[:n_small], g_b_ada.reshape(N_MOD, D)], axis=0)
    small_out = (g_s,) + tuple(_adamw(g_s, w_s, m_s, v_s, name="adamw_vectors"))

    def unpack(t):
        return {
            "norm_ffn1_g": t[0:1], "norm_mix_g": t[1:2], "norm_ffn2_g": t[2:3], "final_norm_g": t[3],
            "conv_b_dw": t[4:5], "conv_ln_g": t[5:6], "conv_ln_b": t[6:7], "attn_sinks": t[7:8, :N_Q_HEADS],
            "b_ada": t[n_small:n_small + N_MOD].reshape(1, N_MOD * D),
        }

    small_un = [unpack(t) for t in small_out]
    for k in small_un[0]:
        upd[k] = tuple(s[k] for s in small_un)

    order = ["w_ada", "b_ada", "norm_ffn1_g", "ffn1_w_gate", "ffn1_w_up", "ffn1_w_down", "norm_mix_g", "w_in",
             "attn_sinks", "w_attn_o", "conv_w_dw", "conv_b_dw", "conv_ln_g", "conv_ln_b", "w_conv_o", "w_out",
             "norm_ffn2_g", "ffn2_w_gate", "ffn2_w_up", "ffn2_w_down", "final_norm_g"]
    grad_x = dx0.reshape(B, S, D)
    return (loss, grad_x, *[upd[k][0] for k in order], *[upd[k][1] for k in order],
            *[upd[k][2] for k in order], *[upd[k][3] for k in order])
```

```python
import dataclasses

import jax
import jax.numpy as jnp
from jax import lax
from jax.experimental import pallas as pl
from jax.experimental.pallas import tpu as pltpu

F32 = jnp.float32
BF16 = jnp.bfloat16
SDS = jax.ShapeDtypeStruct
MESH = pl.DeviceIdType.MESH

N_DEV = 8
EPS = 1e-6
HEAD_DIM = 64
N_Q_HEADS = 16
N_KV_HEADS = 2
GQA_GROUP = N_Q_HEADS // N_KV_HEADS
KV_WIDTH = N_KV_HEADS * HEAD_DIM
ATT_BLOCK = 128
CONV_WIDTH = 31
CONV_HALO = 32
CONV_ROWS = 128
N_MOD = 9
FFN_RESIDUAL = 0.5
ADAM_LR = 0.001
ADAM_B1 = 0.9
ADAM_B2 = 0.999
ADAM_EPS = 1e-08
ADAM_WD = 0.01
ADAM_STEP = 10
NEG_BIG = -1e30
GRAD_STREAM = BF16

V7X_VMEM_BYTES = 64 * 2**20
VMEM_CAP = V7X_VMEM_BYTES - 8 * 2**20


def _nbytes(shape, dtype):
    n = 1
    for s in shape:
        n *= s
    return n * jnp.dtype(dtype).itemsize


def _params(n_axes, blocks, temp_bytes=0):
    need = 2 * sum(_nbytes(s, d) for s, d in blocks) + temp_bytes + 4 * 2**20
    return pltpu.CompilerParams(dimension_semantics=("arbitrary",) * n_axes,
                                vmem_limit_bytes=int(min(max(need, 16 * 2**20), VMEM_CAP)))


def _dot_nt(a, b):
    return lax.dot_general(a, b, (((1,), (1,)), ((), ())), preferred_element_type=F32)


def _dot_tn(a, b):
    return lax.dot_general(a, b, (((0,), (0,)), ((), ())), preferred_element_type=F32)


def _dot(a, b):
    return jnp.dot(a, b, preferred_element_type=F32)


def _sigmoid(x):
    return jax.nn.sigmoid(x)


def _rowsum(v):
    return jnp.sum(v, axis=0, keepdims=True)


def _acc(ref, val, first):
    @pl.when(first)
    def _():
        ref[...] = val

    @pl.when(jnp.logical_not(first))
    def _():
        ref[...] = ref[...] + val


def _norm_mod(xf, gn, sh, sc):
    rstd = lax.rsqrt(jnp.mean(xf * xf, axis=-1, keepdims=True) + EPS)
    xhat = xf * rstd
    yn = xhat * gn
    return yn * (1.0 + sc) + sh, xhat, rstd, yn


def _pick(n, cands):
    for c in cands:
        if n % c == 0:
            return c
    return n


def _my_pos():
    return lax.axis_index("x"), lax.axis_index("y"), lax.axis_index("c")


def _peer(pos, k):
    x, y, c = pos
    return ((1 - x) if k & 4 else x, (1 - y) if k & 2 else y, (1 - c) if k & 1 else c)


def _lin(pos):
    return 4 * pos[0] + 2 * pos[1] + pos[2]


def _in_hbm(a):
    return pltpu.with_memory_space_constraint(a, pltpu.HBM)


class _Comm:
    N_COPY = N_DEV - 1
    N_CHIP = N_DEV // 2

    def __init__(self, items, hbm_out=False):
        self.hbm_out = hbm_out
        self.arrs = [a for a, _ in items]
        self.modes = [m for _, m in items]
        self.n = len(items)
        self.out = None

    def out_shape(self):
        def shape(a, m):
            return {"gather": (N_DEV,) + a.shape, "scatter": a.shape, "pair": (self.N_CHIP,) + a.shape[1:],
                    "cross": a.shape}[m]
        kind = pltpu.HBM if self.hbm_out else SDS
        return [kind(shape(a, m), a.dtype) for a, m in zip(self.arrs, self.modes)]

    def scratch(self):
        return [pltpu.SemaphoreType.DMA((self.n * self.N_COPY,)), pltpu.SemaphoreType.DMA((self.n * self.N_COPY,)),
                pltpu.SemaphoreType.DMA((self.n,))]

    def collective_id(self):
        modes = set(self.modes)
        if "scatter" in modes:
            return 3
        d2d, ici = bool(modes & {"gather", "pair"}), bool(modes & {"gather", "cross"})
        return {(True, False): 0, (False, True): 1, (True, True): 2}[(d2d, ici)]

    def barrier(self):
        x, y, c = _my_pos()
        peers = {0: [(x, y, 1 - c)],
                 1: [(1 - x, y, c), (x, 1 - y, c), (1 - x, 1 - y, c)],
                 2: [(x, y, 1 - c), (1 - x, y, c), (x, 1 - y, c), (1 - x, 1 - y, c)],
                 3: [_peer((x, y, c), k) for k in range(1, N_DEV)]}[self.collective_id()]
        sem = pltpu.get_barrier_semaphore()
        for p in peers:
            pl.semaphore_signal(sem, inc=1, device_id=p, device_id_type=MESH)
        pl.semaphore_wait(sem, len(peers))

    def _plan(self, mode, me):
        x, y, c = me
        sib = (x, y, 1 - c)
        chips = [(1 - x, y), (x, 1 - y), (1 - x, 1 - y)]

        def chip_lin(ch):
            return 2 * ch[0] + ch[1]

        if mode == "scatter":
            peers = [_peer(me, k + 1) for k in range(self.N_COPY)]
            return [(p, ("in", _lin(p)), _lin(me), _lin(p), None) for p in peers], (_lin(me), _lin(me))
        if mode == "gather":
            same = [(*ch, c) for ch in chips]
            other = [(*ch, 1 - c) for ch in chips]
            copies = [(sib, ("in", None), _lin(me), _lin(sib), None)]
            copies += [(p, ("in", None), _lin(me), _lin(p), None) for p in same]
            copies += [(sib, ("out", _lin(p)), _lin(p), _lin(o), 1 + j) for j, (p, o) in enumerate(zip(same, other))]
            return copies, (None, _lin(me))
        if mode == "pair":
            return [(sib, ("in", 2 * q + 1 - c), q, q, None) for q in range(self.N_CHIP)], None
        if mode == "cross":
            mine = chip_lin((x, y))
            return ([((*ch, c), ("in", chip_lin(ch)), mine, chip_lin(ch), None) for ch in chips], (mine, mine))
        raise ValueError(mode)

    def _copy(self, refs, me, i, k, recv):
        srcs, outs, (send_sems, recv_sems, _) = refs
        peer, (where, slot), send_slot, recv_slot, _ = self._plan(self.modes[i], me)[0][k]
        src = srcs[i] if where == "in" else outs[i]
        src = src if slot is None else src.at[slot]
        sem = i * self.N_COPY + k
        return pltpu.make_async_remote_copy(
            src_ref=src, dst_ref=outs[i].at[recv_slot if recv else send_slot], send_sem=send_sems.at[sem],
            recv_sem=recv_sems.at[sem], device_id=peer, device_id_type=MESH)

    def _local(self, refs, me, i):
        srcs, outs, (_, _, loc_sems) = refs
        local = self._plan(self.modes[i], me)[1]
        if local is None:
            return None
        own = srcs[i] if local[0] is None else srcs[i].at[local[0]]
        return pltpu.make_async_copy(own, outs[i].at[local[1]], loc_sems.at[i])

    def start(self, refs):
        me = _my_pos()
        for i in range(self.n):
            local = self._local(refs, me, i)
            if local is not None:
                local.start()
            for k, cp in enumerate(self._plan(self.modes[i], me)[0]):
                if cp[4] is None:
                    self._copy(refs, me, i, k, False).start()

    def forward(self, refs):
        me = _my_pos()
        for i in range(self.n):
            for k, cp in enumerate(self._plan(self.modes[i], me)[0]):
                if cp[4] is not None:
                    self._copy(refs, me, i, cp[4], True).wait_recv()
                    self._copy(refs, me, i, k, False).start()

    def finish(self, refs):
        me = _my_pos()
        plans = [self._plan(m, me)[0] for m in self.modes]
        for i in range(self.n):
            passed_on = [cp[4] for cp in plans[i] if cp[4] is not None]
            for k in range(len(plans[i])):
                if k not in passed_on:
                    self._copy(refs, me, i, k, True).wait_recv()
                self._copy(refs, me, i, k, False).wait_send()
            local = self._local(refs, me, i)
            if local is not None:
                local.wait()


_ANY = pl.BlockSpec(memory_space=pl.ANY)


def _call(body, args, *, name, grid, in_specs, out_specs, out_shape, params, scratch_shapes=(), comm=None,
          hbm_out=()):
    in_specs, out_specs, out_shape = list(in_specs), list(out_specs), list(out_shape)
    scratch_shapes = list(scratch_shapes)
    for k in hbm_out:
        out_shape[k] = pltpu.HBM(out_shape[k].shape, out_shape[k].dtype)
    if comm is None:
        return list(pl.pallas_call(body, name=name, grid=grid, in_specs=in_specs, out_specs=out_specs,
                                   out_shape=out_shape, scratch_shapes=scratch_shapes, compiler_params=params)(*args))
    n_in, n_out, n_scr, nc = len(in_specs), len(out_specs), len(scratch_shapes), comm.n
    n_steps = 1
    for g in grid:
        n_steps *= g

    def hosted(*refs):
        ins, c_in = refs[:n_in], refs[n_in:n_in + nc]
        outs = refs[n_in + nc:n_in + nc + n_out]
        c_out = refs[n_in + nc + n_out:n_in + 2 * nc + n_out]
        scr = refs[n_in + 2 * nc + n_out:n_in + 2 * nc + n_out + n_scr]
        sems = refs[n_in + 2 * nc + n_out + n_scr:]
        step = pl.program_id(0)
        for d in range(1, len(grid)):
            step = step * grid[d] + pl.program_id(d)
        c_refs = (c_in, c_out, sems)

        @pl.when(step == 0)
        def _():
            comm.barrier()
            comm.start(c_refs)

        if n_steps >= 3:
            @pl.when(step == n_steps - 2)
            def _():
                comm.forward(c_refs)

        body(*ins, *outs, *scr)

        @pl.when(step == n_steps - 1)
        def _():
            if n_steps < 3:
                comm.forward(c_refs)
            comm.finish(c_refs)

    res = pl.pallas_call(
        hosted, name=name, grid=grid, in_specs=in_specs + [_ANY] * nc, out_specs=out_specs + [_ANY] * nc,
        out_shape=out_shape + comm.out_shape(), scratch_shapes=scratch_shapes + comm.scratch(),
        compiler_params=dataclasses.replace(params, collective_id=comm.collective_id()))(*args, *comm.arrs)
    comm.out = list(res[n_out:])
    return list(res[:n_out])


def _exchange(items, *, name):
    comm = _Comm(items)

    def body(*refs):
        r = (refs[:comm.n], refs[comm.n:2 * comm.n], refs[2 * comm.n:])
        comm.barrier()
        comm.start(r)
        comm.forward(r)
        comm.finish(r)

    return list(pl.pallas_call(body, name=name, out_shape=comm.out_shape(), in_specs=[_ANY] * comm.n,
                               out_specs=[_ANY] * comm.n, scratch_shapes=comm.scratch(),
                               compiler_params=pltpu.CompilerParams(collective_id=comm.collective_id()))(*comm.arrs))


class _ModVec:
    def __init__(self, arr, idx):
        self.arr, self.idx = arr, idx

    def spec(self, tps, n_axes):
        idx, blk = self.idx, (1, 1, self.arr.shape[2])
        if n_axes == 1:
            return pl.BlockSpec(blk, lambda i: (i // tps * N_MOD + idx, 0, 0))
        return pl.BlockSpec(blk, lambda i, j: (i // tps * N_MOD + idx, 0, 0))


def _norm_mod_matmul(x, gn, sh, sc, wts, *, seq, tm, tn, name, comm=None):
    T, D = x.shape
    N = wts[0].shape[0]
    nw = len(wts)
    tps = seq // tm

    def body(x_ref, gn_ref, sh_ref, sc_ref, *rest):
        w_refs, h_ref, o_refs = rest[:nw], rest[nw], rest[nw + 1:]

        @pl.when(pl.program_id(1) == 0)
        def _():
            h_ref[...] = _norm_mod(x_ref[...], gn_ref[...], sh_ref[0], sc_ref[0])[0].astype(BF16)

        h = h_ref[...]
        for w_ref, o_ref in zip(w_refs, o_refs):
            o_ref[...] = _dot_nt(h, w_ref[...]).astype(o_ref.dtype)

    row = pl.BlockSpec((tm, D), lambda i, j: (i, 0))
    vec = pl.BlockSpec((1, D), lambda i, j: (0, 0))
    wspec = pl.BlockSpec((tn, D), lambda i, j: (j, 0))
    ospec = pl.BlockSpec((tm, tn), lambda i, j: (i, j))
    blocks = [((tm, D), F32), ((tm, D), BF16)] + [((tn, D), BF16), ((tm, tn), BF16)] * nw
    outs = _call(
        body, (x, gn, sh.arr, sc.arr, *wts), name=name, grid=(T // tm, N // tn),
        in_specs=[row, vec, sh.spec(tps, 2), sc.spec(tps, 2)] + [wspec] * nw,
        out_specs=[row] + [ospec] * nw,
        out_shape=[SDS((T, D), BF16)] + [SDS((T, N), BF16)] * nw,
        params=_params(2, blocks, temp_bytes=2 * _nbytes((tm, tn), F32) + 3 * _nbytes((tm, D), F32)), comm=comm)
    return outs[0], outs[1:]


def _matmul_nt(h, w, *, tm, tn, name, comm=None):
    T, D = h.shape
    N = w.shape[0]

    def body(h_ref, w_ref, o_ref):
        o_ref[...] = _dot_nt(h_ref[...], w_ref[...]).astype(o_ref.dtype)

    blocks = [((tm, D), BF16), ((tn, D), BF16), ((tm, tn), BF16)]
    return _call(
        body, (h, w), name=name, grid=(T // tm, N // tn),
        in_specs=[pl.BlockSpec((tm, D), lambda i, j: (i, 0)), pl.BlockSpec((tn, D), lambda i, j: (j, 0))],
        out_specs=[pl.BlockSpec((tm, tn), lambda i, j: (i, j))],
        out_shape=[SDS((T, N), BF16)],
        params=_params(2, blocks, temp_bytes=2 * _nbytes((tm, tn), F32)), comm=comm)[0]


def _ffn_down(a, b, wd, x, g, *, seq, tm, name, comm=None):
    T, F = a.shape
    D = wd.shape[1]
    tps = seq // tm

    def body(a_ref, b_ref, wd_ref, x_ref, g_ref, xo_ref, y_ref):
        af = a_ref[...].astype(F32)
        act = (af * _sigmoid(af) * b_ref[...].astype(F32)).astype(BF16)
        y = _dot(act, wd_ref[...])
        xo_ref[...] = x_ref[...] + (FFN_RESIDUAL * g_ref[0]) * y
        y_ref[...] = y.astype(BF16)

    wide = pl.BlockSpec((tm, F), lambda i: (i, 0))
    row = pl.BlockSpec((tm, D), lambda i: (i, 0))
    wspec = pl.BlockSpec((F, D), lambda i: (0, 0))
    blocks = [((tm, F), BF16)] * 2 + [((F, D), BF16), ((tm, D), F32), ((tm, D), F32), ((tm, D), BF16)]
    return _call(
        body, (a, b, wd, x, g.arr), name=name, grid=(T // tm,),
        in_specs=[wide, wide, wspec, row, g.spec(tps, 1)], out_specs=[row, row],
        out_shape=[SDS((T, D), F32), SDS((T, D), BF16)],
        params=_params(1, blocks, temp_bytes=3 * _nbytes((tm, F), F32)), comm=comm)


def _final_loss(x, gf, tgt, *, tm, name):
    T, D = x.shape
    nt = T // tm

    def body(x_ref, gf_ref, t_ref, dx_ref, loss_ref, dgf_ref, lacc):
        i = pl.program_id(0)
        xf = x_ref[...]
        gfv = gf_ref[...]
        rstd = lax.rsqrt(jnp.mean(xf * xf, axis=-1, keepdims=True) + EPS)
        xhat = xf * rstd
        err = xhat * gfv - t_ref[...]
        dy = err * (1.0 / D)
        dxhat = dy * gfv
        dx_ref[...] = (rstd * (dxhat - xhat * jnp.mean(dxhat * xhat, axis=-1, keepdims=True))).astype(dx_ref.dtype)
        _acc(dgf_ref, _rowsum(dy * xhat), i == 0)
        _acc(lacc, _rowsum(err * err), i == 0)

        @pl.when(i == nt - 1)
        def _():
            loss_ref[...] = jnp.broadcast_to((0.5 / D) * jnp.sum(lacc[...]), loss_ref.shape)

    row = pl.BlockSpec((tm, D), lambda i: (i, 0))
    vec = pl.BlockSpec((1, D), lambda i: (0, 0))
    lspec = pl.BlockSpec((1, 128), lambda i: (0, 0))
    blocks = [((tm, D), F32)] * 3
    return _call(
        body, (x, gf, tgt), name=name, grid=(nt,),
        in_specs=[row, vec, row], out_specs=[row, lspec, vec],
        out_shape=[SDS((T, D), GRAD_STREAM), SDS((1, 128), F32), SDS((1, D), F32)],
        scratch_shapes=[pltpu.VMEM((1, D), F32)],
        params=_params(1, blocks, temp_bytes=4 * _nbytes((tm, D), F32)), hbm_out=(0,))


def _ffn_bwd_down(dxo, g, y, wd, a, b, *, seq, tm, tn, name, comm=None):
    T, F = a.shape
    D = wd.shape[1]
    tps = seq // tm
    nb = T // seq

    def body(dxo_ref, g_ref, y_ref, wd_ref, a_ref, b_ref, dyb_ref, da_ref, db_ref, dg_ref):
        i = pl.program_id(0)

        @pl.when(pl.program_id(1) == 0)
        def _():
            dx = dxo_ref[...].astype(F32)
            dyb_ref[...] = ((FFN_RESIDUAL * g_ref[0]) * dx).astype(BF16)
            part = _rowsum(FFN_RESIDUAL * dx * y_ref[...].astype(F32))
            _acc(dg_ref, part[None], i % tps == 0)

        dact = _dot_nt(dyb_ref[...], wd_ref[...])
        af = a_ref[...].astype(F32)
        bf = b_ref[...].astype(F32)
        sg = _sigmoid(af)
        silu = af * sg
        da_ref[...] = (dact * bf * (sg + silu * (1.0 - sg))).astype(BF16)
        db_ref[...] = (dact * silu).astype(BF16)

    row = pl.BlockSpec((tm, D), lambda i, j: (i, 0))
    per_b = pl.BlockSpec((1, 1, D), lambda i, j: (i // tps, 0, 0))
    wspec = pl.BlockSpec((tn, D), lambda i, j: (j, 0))
    chunk = pl.BlockSpec((tm, tn), lambda i, j: (i, j))
    blocks = [((tm, D), F32), ((tm, D), BF16), ((tn, D), BF16), ((tm, D), BF16)] + [((tm, tn), BF16)] * 4
    return _call(
        body, (dxo, g.arr, y, wd, a, b), name=name, grid=(T // tm, F // tn),
        in_specs=[row, g.spec(tps, 2), row, wspec, chunk, chunk],
        out_specs=[row, chunk, chunk, per_b],
        out_shape=[SDS((T, D), BF16)] + [SDS((T, F), BF16)] * 2 + [SDS((nb, 1, D), F32)],
        params=_params(2, blocks, temp_bytes=6 * _nbytes((tm, tn), F32)), comm=comm)


def _matmul_norm_mod_bwd(ds, ws, x, gn, sc, dxo, *, seq, tm, name, out_dtype, comm=None):
    T, D = x.shape
    nk = len(ws)
    sizes = [len(g) for g in ds]
    ds = [d for g in ds for d in g]
    tps = seq // tm
    nb = T // seq

    def body(*refs):
        w_refs = refs[len(ds):len(ds) + nk]
        x_ref, gn_ref, sc_ref, dxo_ref, dxi_ref, dsh_ref, dsc_ref, dgn_ref = refs[len(ds) + nk:]
        i = pl.program_id(0)
        dh, at = None, 0
        for n, w_ref in zip(sizes, w_refs):
            pieces = [r[...] for r in refs[at:at + n]]
            at += n
            part = _dot(pieces[0] if n == 1 else jnp.concatenate(pieces, axis=1), w_ref[...])
            dh = part if dh is None else dh + part
        gnv = gn_ref[...]
        scv = sc_ref[0]
        _, xhat, rstd, yn = _norm_mod(x_ref[...], gnv, 0.0, scv)
        dyn = dh * (1.0 + scv)
        dxhat = dyn * gnv
        dxi_ref[...] = (dxo_ref[...].astype(F32)
                        + rstd * (dxhat - xhat * jnp.mean(dxhat * xhat, axis=-1, keepdims=True))).astype(out_dtype)
        first_of_seq = i % tps == 0
        _acc(dsh_ref, _rowsum(dh)[None], first_of_seq)
        _acc(dsc_ref, _rowsum(dh * yn)[None], first_of_seq)
        _acc(dgn_ref, _rowsum(dyn * xhat), i == 0)

    row = pl.BlockSpec((tm, D), lambda i: (i, 0))
    vec = pl.BlockSpec((1, D), lambda i: (0, 0))
    per_b = pl.BlockSpec((1, 1, D), lambda i: (i // tps, 0, 0))
    d_specs = [pl.BlockSpec((tm, d.shape[1]), lambda i: (i, 0)) for d in ds]
    w_specs = [pl.BlockSpec(w.shape, lambda i: (0, 0)) for w in ws]
    blocks = ([((tm, d.shape[1]), BF16) for d in ds] + [(w.shape, BF16) for w in ws] + [((tm, D), F32)] * 3)
    return _call(
        body, (*ds, *ws, x, gn, sc.arr, dxo), name=name, grid=(T // tm,),
        in_specs=d_specs + w_specs + [row, vec, sc.spec(tps, 1), row],
        out_specs=[row, per_b, per_b, vec],
        out_shape=[SDS((T, D), out_dtype), SDS((nb, 1, D), F32), SDS((nb, 1, D), F32), SDS((1, D), F32)],
        params=_params(1, blocks, temp_bytes=6 * _nbytes((tm, D), F32)), comm=comm)


def _layernorm_silu(yc, lg, lb):
    mu = jnp.mean(yc, axis=-1, keepdims=True)
    cen = yc - mu
    rstd = lax.rsqrt(jnp.mean(cen * cen, axis=-1, keepdims=True) + EPS)
    xh = cen * rstd
    l = xh * lg + lb
    s = _sigmoid(l)
    return l * s, xh, rstd, l, s


GATE_W = 256


def _gate_specs(tm, D, col):
    return [pl.BlockSpec((tm, GATE_W), lambda i, blk=col // GATE_W + t: (i, blk)) for t in range(D // GATE_W)]


def _gate(refs):
    return jnp.concatenate([r[...] for r in refs], axis=1).astype(F32)


def _mix_out(ao, yc, proj, wao, wco, wout, x1, g2, lg, lb, *, seq, tm, ga_col, gc_col, name, comm=None):
    T, D = x1.shape
    tps = seq // tm
    ng = D // GATE_W

    def body(ao_ref, yc_ref, *rest):
        ga_refs, gc_refs = rest[:ng], rest[ng:2 * ng]
        (wao_ref, wco_ref, wout_ref, x1_ref, g2_ref, lg_ref, lb_ref,
         x2_ref, z_ref, ya_ref, ycv_ref, cact_ref, mrg_ref) = rest[2 * ng:]
        ya = _dot(ao_ref[...], wao_ref[...])
        cact = _layernorm_silu(yc_ref[...], lg_ref[...], lb_ref[...])[0].astype(BF16)
        ycv = _dot(cact, wco_ref[...])
        merged = (_sigmoid(_gate(ga_refs)) * ya + _sigmoid(_gate(gc_refs)) * ycv).astype(BF16)
        z = _dot(merged, wout_ref[...])
        x2_ref[...] = x1_ref[...] + g2_ref[0] * z
        z_ref[...] = z.astype(BF16)
        ya_ref[...] = ya.astype(BF16)
        ycv_ref[...] = ycv.astype(BF16)
        cact_ref[...] = cact
        mrg_ref[...] = merged

    row = pl.BlockSpec((tm, D), lambda i: (i, 0))
    vec = pl.BlockSpec((1, D), lambda i: (0, 0))
    wspec = pl.BlockSpec((D, D), lambda i: (0, 0))
    gates = _gate_specs(tm, D, ga_col) + _gate_specs(tm, D, gc_col)
    blocks = ([((tm, D), BF16), ((tm, D), F32), ((tm, D), BF16), ((tm, D), BF16)] + [((D, D), BF16)] * 3
              + [((tm, D), F32)] * 2 + [((tm, D), BF16)] * 5)
    return _call(
        body, (ao, yc, *[proj] * (2 * ng), wao, wco, wout, x1, g2.arr, lg, lb), name=name, grid=(T // tm,),
        in_specs=[row, row, *gates, wspec, wspec, wspec, row, g2.spec(tps, 1), vec, vec],
        out_specs=[row] * 6,
        out_shape=[SDS((T, D), F32)] + [SDS((T, D), BF16)] * 5,
        params=_params(1, blocks, temp_bytes=8 * _nbytes((tm, D), F32)), comm=comm)


def _mix_out_bwd(dx2, g2, z, wout, proj, ya, ycv, wao, wco, yc, lg, lb, *, seq, tm, ga_col, gc_col, name,
                 comm=None):
    T, D = dx2.shape
    tps = seq // tm
    nb = T // seq
    ng = D // GATE_W

    def body(dx2_ref, g2_ref, z_ref, wout_ref, *rest):
        ga_refs, gc_refs = rest[:ng], rest[ng:2 * ng]
        (ya_ref, ycv_ref, wao_ref, wco_ref, yc_ref, lg_ref, lb_ref, dz_ref, dya_ref, dycv_ref, dga_ref, dgc_ref,
         dao_ref, dyc_ref, dg2_ref, dlg_ref, dlb_ref) = rest[2 * ng:]
        i = pl.program_id(0)
        dx = dx2_ref[...].astype(F32)
        _acc(dg2_ref, _rowsum(dx * z_ref[...].astype(F32))[None], i % tps == 0)
        dzb = (g2_ref[0] * dx).astype(BF16)
        dz_ref[...] = dzb
        dmerged = _dot_nt(dzb, wout_ref[...])
        sa = _sigmoid(_gate(ga_refs))
        sc_ = _sigmoid(_gate(gc_refs))
        dya = (dmerged * sa).astype(BF16)
        dycv = (dmerged * sc_).astype(BF16)
        dya_ref[...] = dya
        dycv_ref[...] = dycv
        dga_ref[...] = (dmerged * ya_ref[...].astype(F32) * (sa * (1.0 - sa))).astype(BF16)
        dgc_ref[...] = (dmerged * ycv_ref[...].astype(F32) * (sc_ * (1.0 - sc_))).astype(BF16)
        dao_ref[...] = _dot_nt(dya, wao_ref[...]).astype(BF16)
        dcact = _dot_nt(dycv, wco_ref[...])
        lgv = lg_ref[...]
        _, xh, rstd, l, s = _layernorm_silu(yc_ref[...], lgv, lb_ref[...])
        dl = dcact * (s * (1.0 + l * (1.0 - s)))
        _acc(dlb_ref, _rowsum(dl), i == 0)
        _acc(dlg_ref, _rowsum(dl * xh), i == 0)
        dxh = dl * lgv
        dyc_ref[...] = rstd * (dxh - jnp.mean(dxh, axis=-1, keepdims=True)
                               - xh * jnp.mean(dxh * xh, axis=-1, keepdims=True))

    row = pl.BlockSpec((tm, D), lambda i: (i, 0))
    vec = pl.BlockSpec((1, D), lambda i: (0, 0))
    per_b = pl.BlockSpec((1, 1, D), lambda i: (i // tps, 0, 0))
    wspec = pl.BlockSpec((D, D), lambda i: (0, 0))
    gates = _gate_specs(tm, D, ga_col) + _gate_specs(tm, D, gc_col)
    blocks = ([((tm, D), F32)] * 3 + [((tm, D), BF16)] * 11 + [((D, D), BF16)] * 3)
    return _call(
        body, (dx2, g2.arr, z, wout, *[proj] * (2 * ng), ya, ycv, wao, wco, yc, lg, lb), name=name,
        grid=(T // tm,),
        in_specs=[row, g2.spec(tps, 1), row, wspec, *gates, row, row, wspec, wspec, row, vec, vec],
        out_specs=[row] * 7 + [per_b, vec, vec],
        out_shape=[SDS((T, D), BF16)] * 6 + [SDS((T, D), F32), SDS((nb, 1, D), F32), SDS((1, D), F32),
                                             SDS((1, D), F32)],
        params=_params(1, blocks, temp_bytes=10 * _nbytes((tm, D), F32)), comm=comm)


Q_BLOCK = 64
BAND = Q_BLOCK + ATT_BLOCK
GROUP_ROWS = GQA_GROUP * Q_BLOCK
PAIR_W = 2 * HEAD_DIM
GROUP_W = GQA_GROUP * HEAD_DIM


def _lane_lo():
    return lax.broadcasted_iota(jnp.int32, (1, PAIR_W), 1) < HEAD_DIM


def _band_bias():
    sj = lax.broadcasted_iota(jnp.int32, (BAND, GROUP_ROWS), 0)
    qi = lax.broadcasted_iota(jnp.int32, (BAND, GROUP_ROWS), 1) & (Q_BLOCK - 1)
    rel = qi + ATT_BLOCK - sj
    bias = jnp.where(jnp.logical_and(rel >= 0, rel < ATT_BLOCK), 0.0, NEG_BIG)
    return bias, lax.broadcasted_iota(jnp.int32, (BAND, 1), 0)


def _block_bias(bias0, key_index, r0):
    return bias0 + jnp.where(key_index + r0 < ATT_BLOCK, NEG_BIG, 0.0)


def _dup_heads(src_ref, dst, seq):
    x = src_ref[...]
    i = lax.broadcasted_iota(jnp.int32, (KV_WIDTH, PAIR_W), 0)
    j = lax.broadcasted_iota(jnp.int32, (KV_WIDTH, PAIR_W), 1) & (HEAD_DIM - 1)
    for g in range(N_KV_HEADS):
        sel = jnp.where(i == j + g * HEAD_DIM, 1.0, 0.0).astype(BF16)
        dst[g, pl.ds(0, ATT_BLOCK), :] = jnp.zeros((ATT_BLOCK, PAIR_W), BF16)
        dst[g, pl.ds(ATT_BLOCK, seq), :] = _dot(x, sel).astype(BF16)


def _stack_heads(blk, g, lo):
    parts = []
    for p in range(GQA_GROUP // 2):
        pair = blk[:, g * GROUP_W + p * PAIR_W:g * GROUP_W + (p + 1) * PAIR_W]
        parts += [jnp.where(lo, pair, jnp.zeros_like(pair)), jnp.where(lo, jnp.zeros_like(pair), pair)]
    return jnp.concatenate(parts, axis=0)


def _unstack_heads(full, ref, r0, g, lo):
    for p in range(GQA_GROUP // 2):
        even = full[(2 * p) * Q_BLOCK:(2 * p + 1) * Q_BLOCK, :]
        odd = full[(2 * p + 1) * Q_BLOCK:(2 * p + 2) * Q_BLOCK, :]
        ref[pl.ds(r0, Q_BLOCK), g * GROUP_W + p * PAIR_W:g * GROUP_W + (p + 1) * PAIR_W] = (
            jnp.where(lo, even, odd).astype(ref.dtype))


def _sink_row(sink_ref, g):
    return jnp.concatenate([jnp.full((1, Q_BLOCK), sink_ref[0, g * GQA_GROUP + h], F32)
                            for h in range(GQA_GROUP)], axis=1)


def _group_probs(qs, k2, bias, sink):
    s = _dot_nt(k2, qs) * (HEAD_DIM ** -0.5) + bias
    m = jnp.maximum(jnp.max(s, axis=0, keepdims=True), sink)
    p = jnp.exp(s - m)
    psink = jnp.exp(sink - m)
    inv = 1.0 / (jnp.sum(p, axis=0, keepdims=True) + psink)
    return p * inv, psink * inv


def _attn_fwd(projp, sinks, *, seq, q_blk, k_blk, v_blk, name, comm=None):
    T = projp.shape[0]
    QW = N_Q_HEADS * HEAD_DIM
    nblk = seq // Q_BLOCK

    def body(q_ref, k_ref, v_ref, sink_ref, o_ref, k2s, v2s):
        _dup_heads(k_ref, k2s, seq)
        _dup_heads(v_ref, v2s, seq)
        lo = _lane_lo()
        bias0, key_index = _band_bias()
        sink_rows = [_sink_row(sink_ref, g) for g in range(N_KV_HEADS)]

        def blk(n, carry):
            r0 = pl.multiple_of(n * Q_BLOCK, Q_BLOCK)
            band = pl.ds(r0, BAND)
            qb = q_ref[pl.ds(r0, Q_BLOCK), :]
            bias = _block_bias(bias0, key_index, r0)
            for g in range(N_KV_HEADS):
                probs_t, _ = _group_probs(_stack_heads(qb, g, lo), k2s[g, band, :], bias, sink_rows[g])
                _unstack_heads(_dot_tn(probs_t.astype(BF16), v2s[g, band, :]), o_ref, r0, g, lo)
            return carry

        lax.fori_loop(0, nblk, blk, 0, unroll=4)

    blocks = [((seq, QW), BF16)] * 2 + [((seq, KV_WIDTH), BF16)] * 2
    return _call(
        body, (projp, projp, projp, sinks), name=name, grid=(T // seq,),
        in_specs=[pl.BlockSpec((seq, QW), lambda b: (b, q_blk)),
                  pl.BlockSpec((seq, KV_WIDTH), lambda b: (b, k_blk)),
                  pl.BlockSpec((seq, KV_WIDTH), lambda b: (b, v_blk)),
                  pl.BlockSpec(memory_space=pltpu.SMEM)],
        out_specs=[pl.BlockSpec((seq, QW), lambda b: (b, 0))],
        out_shape=[SDS((T, QW), BF16)],
        scratch_shapes=[pltpu.VMEM((N_KV_HEADS, seq + ATT_BLOCK, PAIR_W), BF16)] * 2,
        params=_params(1, blocks, temp_bytes=2 * _nbytes((N_KV_HEADS, seq + ATT_BLOCK, PAIR_W), BF16)
                       + 8 * _nbytes((BAND, GROUP_ROWS), F32)), comm=comm, hbm_out=(0,))[0]


def _attn_bwd(projp, dao, sinks, *, seq, q_blk, k_blk, v_blk, name, comm=None):
    T = projp.shape[0]
    QW = N_Q_HEADS * HEAD_DIM
    assert seq % (2 * Q_BLOCK) == 0
    nblk = seq // Q_BLOCK

    def body(q_ref, k_ref, v_ref, do_ref, sink_ref, dq_ref, dk_ref, dv_ref, dsink_ref, k2s, v2s, dkacc, dvacc):
        _dup_heads(k_ref, k2s, seq)
        _dup_heads(v_ref, v2s, seq)
        dkacc[...] = jnp.zeros(dkacc.shape, F32)
        dvacc[...] = jnp.zeros(dvacc.shape, F32)
        lane = lax.broadcasted_iota(jnp.int32, (1, PAIR_W), 1)
        lo = lane < HEAD_DIM
        bias0, key_index = _band_bias()
        sink_rows = [_sink_row(sink_ref, g) for g in range(N_KV_HEADS)]

        def blk(n, tsinks):
            tsinks = list(tsinks)
            r0 = pl.multiple_of(n * Q_BLOCK, Q_BLOCK)
            band = pl.ds(r0, BAND)
            qb = q_ref[pl.ds(r0, Q_BLOCK), :]
            dob = do_ref[pl.ds(r0, Q_BLOCK), :]
            bias = _block_bias(bias0, key_index, r0)
            for g in range(N_KV_HEADS):
                qs = _stack_heads(qb, g, lo)
                dos = _stack_heads(dob, g, lo)
                k2 = k2s[g, band, :]
                v2 = v2s[g, band, :]
                probs_t, psink = _group_probs(qs, k2, bias, sink_rows[g])
                dp_t = _dot_nt(v2, dos)
                delta = jnp.sum(probs_t * dp_t, axis=0, keepdims=True)
                ds_t = (probs_t * (dp_t - delta) * (HEAD_DIM ** -0.5)).astype(BF16)
                tsinks[g] = tsinks[g] + psink * delta
                _unstack_heads(_dot_tn(ds_t, k2), dq_ref, r0, g, lo)
                dkacc[g, band, :] = dkacc[g, band, :] + _dot(ds_t, qs)
                dvacc[g, band, :] = dvacc[g, band, :] + _dot(probs_t.astype(BF16), dos)
            return tuple(tsinks)

        def two_blocks(m, tsinks):
            return blk(2 * m + 1, blk(2 * m, tsinks))

        tsinks = lax.fori_loop(0, nblk // 2, two_blocks, (jnp.zeros((1, GROUP_ROWS), F32),) * N_KV_HEADS)
        dsink = jnp.zeros((1, PAIR_W), F32)
        for g in range(N_KV_HEADS):
            for h in range(GQA_GROUP):
                dsink = dsink + jnp.where(lane == g * GQA_GROUP + h,
                                          -jnp.sum(tsinks[g][:, h * Q_BLOCK:(h + 1) * Q_BLOCK]), 0.0)
        _acc(dsink_ref, dsink, pl.program_id(0) == 0)

        def fold(acc, g):
            a = acc[g, pl.ds(ATT_BLOCK, seq), :]
            return a + pltpu.roll(a, HEAD_DIM, 1)

        dk_ref[...] = jnp.where(lo, fold(dkacc, 0), fold(dkacc, 1)).astype(BF16)
        dv_ref[...] = jnp.where(lo, fold(dvacc, 0), fold(dvacc, 1)).astype(BF16)

    blocks = [((seq, QW), BF16)] * 3 + [((seq, KV_WIDTH), BF16)] * 4
    kv_spec_out = pl.BlockSpec((seq, KV_WIDTH), lambda b: (b, 0))
    return _call(
        body, (projp, projp, projp, dao, sinks), name=name, grid=(T // seq,),
        in_specs=[pl.BlockSpec((seq, QW), lambda b: (b, q_blk)),
                  pl.BlockSpec((seq, KV_WIDTH), lambda b: (b, k_blk)),
                  pl.BlockSpec((seq, KV_WIDTH), lambda b: (b, v_blk)),
                  pl.BlockSpec((seq, QW), lambda b: (b, 0)),
                  pl.BlockSpec(memory_space=pltpu.SMEM)],
        out_specs=[pl.BlockSpec((seq, QW), lambda b: (b, 0)), kv_spec_out, kv_spec_out,
                   pl.BlockSpec((1, 128), lambda b: (0, 0))],
        out_shape=[SDS((T, QW), BF16), SDS((T, KV_WIDTH), BF16), SDS((T, KV_WIDTH), BF16), SDS((1, 128), F32)],
        scratch_shapes=[pltpu.VMEM((N_KV_HEADS, seq + ATT_BLOCK, PAIR_W), BF16)] * 2
        + [pltpu.VMEM((N_KV_HEADS, seq + ATT_BLOCK, PAIR_W), F32)] * 2,
        params=_params(1, blocks, temp_bytes=6 * _nbytes((N_KV_HEADS, seq + ATT_BLOCK, PAIR_W), BF16)
                       + 16 * _nbytes((BAND, GROUP_ROWS), F32)), comm=comm)


SUBLANES = 8


def _sublane_shifts(win):
    n = CONV_ROWS + CONV_HALO
    return [win] + [pltpu.roll(win, n - b, 0) for b in range(1, SUBLANES)]


def _window(shifted, off):
    a = off // SUBLANES * SUBLANES
    return shifted[off % SUBLANES][a:a + CONV_ROWS, :]


def _conv_fwd(projp, w, bias, *, seq, cw, a_col, b_col, name, comm=None):
    T = projp.shape[0]
    C = w.shape[1]
    nchunk = seq // CONV_ROWS

    def body(a_ref, b_ref, w_ref, bias_ref, y_ref, upad):
        upad[pl.ds(0, CONV_HALO), :] = jnp.zeros((CONV_HALO, cw), F32)
        upad[pl.ds(CONV_HALO, seq), :] = a_ref[...].astype(F32) * _sigmoid(b_ref[...].astype(F32))
        wv = w_ref[...]
        bv = bias_ref[...]

        def chunk(r, carry):
            r0 = pl.multiple_of(r * CONV_ROWS, CONV_ROWS)
            shifted = _sublane_shifts(upad[pl.ds(r0, CONV_ROWS + CONV_HALO), :])
            acc = jnp.broadcast_to(bv, (CONV_ROWS, cw))
            for k in range(CONV_WIDTH):
                acc = acc + wv[k:k + 1, :] * _window(shifted, CONV_HALO - (CONV_WIDTH - 1) + k)
            y_ref[pl.ds(r0, CONV_ROWS), :] = acc
            return carry

        lax.fori_loop(0, nchunk, chunk, 0)

    blocks = [((seq, cw), BF16)] * 2 + [((seq, cw), F32)]
    return _call(
        body, (projp, projp, w, bias), name=name, grid=(T // seq, C // cw),
        in_specs=[pl.BlockSpec((seq, cw), lambda b, c: (b, a_col // cw + c)),
                  pl.BlockSpec((seq, cw), lambda b, c: (b, b_col // cw + c)),
                  pl.BlockSpec((CONV_WIDTH, cw), lambda b, c: (0, c)),
                  pl.BlockSpec((1, cw), lambda b, c: (0, c))],
        out_specs=[pl.BlockSpec((seq, cw), lambda b, c: (b, c))],
        out_shape=[SDS((T, C), F32)],
        scratch_shapes=[pltpu.VMEM((seq + CONV_HALO, cw), F32)],
        params=_params(2, blocks, temp_bytes=6 * _nbytes((seq, cw), F32)), comm=comm, hbm_out=(0,))[0]


def _conv_bwd(dy, projp, w, *, seq, cw, a_col, b_col, name, comm=None):
    T = projp.shape[0]
    C = w.shape[1]
    nchunk = seq // CONV_ROWS
    SUB = 8

    def body(dy_ref, a_ref, b_ref, w_ref, da_ref, db_ref, dw_ref, dbias_ref, dypad, dwp):
        first = pl.program_id(1) == 0
        dyv = dy_ref[...]
        dypad[pl.ds(0, seq), :] = dyv
        dypad[pl.ds(seq, CONV_HALO), :] = jnp.zeros((CONV_HALO, cw), F32)
        dwp[...] = jnp.zeros(dwp.shape, F32)
        wv = w_ref[...]

        def chunk(r, carry):
            r0 = pl.multiple_of(r * CONV_ROWS, CONV_ROWS)
            dy_shifts = _sublane_shifts(dypad[pl.ds(r0, CONV_ROWS + CONV_HALO), :])
            ac = a_ref[pl.ds(r0, CONV_ROWS), :].astype(F32)
            sbc = _sigmoid(b_ref[pl.ds(r0, CONV_ROWS), :].astype(F32))
            uc = ac * sbc
            du = jnp.zeros((CONV_ROWS, cw), F32)
            for k in range(CONV_WIDTH):
                dyk = _window(dy_shifts, CONV_WIDTH - 1 - k)
                du = du + wv[k:k + 1, :] * dyk
                prod = uc * dyk
                part = prod[0:SUB, :]
                for s in range(1, CONV_ROWS // SUB):
                    part = part + prod[s * SUB:(s + 1) * SUB, :]
                dwp[pl.ds(k * SUB, SUB), :] = dwp[pl.ds(k * SUB, SUB), :] + part
            da_ref[pl.ds(r0, CONV_ROWS), :] = (du * sbc).astype(BF16)
            db_ref[pl.ds(r0, CONV_ROWS), :] = (du * ac * (sbc * (1.0 - sbc))).astype(BF16)
            return carry

        lax.fori_loop(0, nchunk, chunk, 0)

        @pl.when(first)
        def _():
            dw_ref[...] = jnp.zeros(dw_ref.shape, F32)
            dbias_ref[...] = jnp.zeros(dbias_ref.shape, F32)

        for k in range(CONV_WIDTH):
            dw_ref[k:k + 1, :] = dw_ref[k:k + 1, :] + _rowsum(dwp[pl.ds(k * SUB, SUB), :])
        dbias_ref[...] = dbias_ref[...] + _rowsum(dyv)

    blocks = [((seq, cw), F32)] + [((seq, cw), BF16)] * 4
    return _call(
        body, (dy, projp, projp, w), name=name, grid=(C // cw, T // seq),
        in_specs=[pl.BlockSpec((seq, cw), lambda c, b: (b, c)),
                  pl.BlockSpec((seq, cw), lambda c, b: (b, a_col // cw + c)),
                  pl.BlockSpec((seq, cw), lambda c, b: (b, b_col // cw + c)),
                  pl.BlockSpec((CONV_WIDTH, cw), lambda c, b: (0, c))],
        out_specs=[pl.BlockSpec((seq, cw), lambda c, b: (b, c)), pl.BlockSpec((seq, cw), lambda c, b: (b, c)),
                   pl.BlockSpec((CONV_WIDTH, cw), lambda c, b: (0, c)), pl.BlockSpec((1, cw), lambda c, b: (0, c))],
        out_shape=[SDS((T, C), BF16), SDS((T, C), BF16), SDS((CONV_WIDTH, C), F32), SDS((1, C), F32)],
        scratch_shapes=[pltpu.VMEM((seq + CONV_HALO, cw), F32), pltpu.VMEM((CONV_WIDTH * SUB, cw), F32)],
        params=_params(2, blocks, temp_bytes=8 * _nbytes((seq, cw), F32)), comm=comm, hbm_out=(0, 1))


def _matmul_tn(a, b, *, name, gate=None, comm=None):
    T, M = a.shape
    N = b.shape[1]
    bm = _pick(M, (768, 512, 256))
    lhs = [a] if gate is None else [a, gate]

    def body(*refs):
        b_ref, o_ref = refs[len(lhs)], refs[len(lhs) + 1]
        av = refs[0][...]
        if gate is not None:
            af = av.astype(F32)
            av = (af * _sigmoid(af) * refs[1][...].astype(F32)).astype(BF16)
        o_ref[...] = _dot_tn(av, b_ref[...]).astype(BF16)

    blocks = [((T, bm), BF16)] * len(lhs) + [((T, N), BF16), ((bm, N), BF16)]
    return _call(
        body, (*lhs, b), name=name, grid=(M // bm,),
        in_specs=[pl.BlockSpec((T, bm), lambda i: (0, i))] * len(lhs) + [pl.BlockSpec((T, N), lambda i: (0, 0))],
        out_specs=[pl.BlockSpec((bm, N), lambda i: (i, 0))],
        out_shape=[SDS((M, N), BF16)],
        params=_params(1, blocks, temp_bytes=(2 + 4 * len(lhs)) * _nbytes((T, bm), BF16) + 2 * _nbytes((bm, N), F32)),
        comm=comm)[0]


TN_BLOCK = 256


def _matmul_tn_pieces(groups, b, *, name, comm=None):
    T, N = b.shape
    flat = [a for g in groups for a in g]
    starts, n_steps = [], 0
    for g in groups:
        width = sum(a.shape[1] for a in g)
        assert width % TN_BLOCK == 0 and (len(g) == 1 or width == TN_BLOCK), [a.shape for a in g]
        starts.append(n_steps)
        n_steps += width // TN_BLOCK

    def body(*refs):
        a_refs, b_ref, o_ref = refs[:len(flat)], refs[len(flat)], refs[len(flat) + 1]
        i = pl.program_id(0)
        at = 0
        for g, start in zip(groups, starts):
            mine = a_refs[at:at + len(g)]
            at += len(g)
            steps = sum(a.shape[1] for a in g) // TN_BLOCK

            @pl.when(jnp.logical_and(i >= start, i < start + steps))
            def _(mine=mine):
                a = mine[0][...] if len(mine) == 1 else jnp.concatenate([r[...] for r in mine], axis=1)
                o_ref[...] = _dot_tn(a, b_ref[...]).astype(BF16)

    a_specs = []
    for g, start in zip(groups, starts):
        for a in g:
            if len(g) == 1:
                last = a.shape[1] // TN_BLOCK - 1
                a_specs.append(pl.BlockSpec(
                    (T, TN_BLOCK), lambda i, start=start, last=last: (0, jnp.clip(i - start, 0, last))))
            else:
                a_specs.append(pl.BlockSpec((T, a.shape[1]), lambda i: (0, 0)))
    blocks = [((T, TN_BLOCK), BF16)] * len(flat) + [((T, N), BF16), ((TN_BLOCK, N), BF16)]
    return _call(
        body, (*flat, b), name=name, grid=(n_steps,),
        in_specs=a_specs + [pl.BlockSpec((T, N), lambda i: (0, 0))],
        out_specs=[pl.BlockSpec((TN_BLOCK, N), lambda i: (i, 0))],
        out_shape=[SDS((n_steps * TN_BLOCK, N), BF16)],
        params=_params(1, blocks, temp_bytes=2 * _nbytes((T, TN_BLOCK), BF16) + 2 * _nbytes((TN_BLOCK, N), F32)),
        comm=comm)[0]


def _sum_parts(p_ref):
    g = p_ref[0].astype(F32)
    for s in range(1, p_ref.shape[0]):
        g = g + p_ref[s].astype(F32)
    return g


def _pair_add(g, staged, *, name):
    _, R, W = g.shape
    nq = staged.shape[0]
    tr = _row_tile(R)

    def body(g_ref, s_ref, o_ref):
        mine = jnp.where(lax.axis_index("c") == 0, g_ref[0, 0].astype(F32), g_ref[0, 1].astype(F32))
        o_ref[0] = (mine + s_ref[0].astype(F32)).astype(o_ref.dtype)

    return _call(
        body, (g.reshape(nq, 2, R, W), staged), name=name, grid=(nq, R // tr),
        in_specs=[pl.BlockSpec((1, 2, tr, W), lambda q, i: (q, 0, i, 0)),
                  pl.BlockSpec((1, tr, W), lambda q, i: (q, i, 0))],
        out_specs=[pl.BlockSpec((1, tr, W), lambda q, i: (q, i, 0))],
        out_shape=[SDS((nq, R, W), g.dtype)],
        params=_params(2, [((4, tr, W), g.dtype)], temp_bytes=3 * _nbytes((tr, W), F32)))[0]


def _adamw_update(w, g, m, v):
    m = ADAM_B1 * m + (1.0 - ADAM_B1) * g
    v = ADAM_B2 * v + (1.0 - ADAM_B2) * (g * g)
    m_hat = m / (1.0 - ADAM_B1 ** ADAM_STEP)
    v_hat = v / (1.0 - ADAM_B2 ** ADAM_STEP)
    delta = -ADAM_LR * (m_hat / (jnp.sqrt(v_hat) + ADAM_EPS) + ADAM_WD * w)
    return delta, m, v


def _row_tile(R):
    return _pick(R, (256, 128, 112, 88, 64, 32, 16, 8))


def _sum8(parts, *, name):
    n, R, W = parts.shape
    tr = _row_tile(R)

    def body(p_ref, o_ref):
        o_ref[...] = _sum_parts(p_ref)

    return _call(
        body, (parts,), name=name, grid=(R // tr,),
        in_specs=[pl.BlockSpec((n, tr, W), lambda i: (0, i, 0))],
        out_specs=[pl.BlockSpec((tr, W), lambda i: (i, 0))],
        out_shape=[SDS((R, W), F32)],
        params=_params(1, [((n, tr, W), parts.dtype), ((tr, W), F32)]))[0]


def _adamw(g, w, m, v, *, name):
    R, W = w.shape
    tr = _row_tile(R)

    def body(g_ref, w_ref, m_ref, v_ref, d_ref, mo_ref, vo_ref):
        d_ref[...], mo_ref[...], vo_ref[...] = _adamw_update(w_ref[...], g_ref[...], m_ref[...], v_ref[...])

    spec = pl.BlockSpec((tr, W), lambda i: (i, 0))
    return _call(
        body, (g, w, m, v), name=name, grid=(R // tr,),
        in_specs=[spec] * 4, out_specs=[spec] * 3, out_shape=[SDS((R, W), F32)] * 3,
        params=_params(1, [((tr, W), F32)] * 7))


def _sum8_adamw(parts, w, m, v, *, name):
    R, W = w.shape
    n = parts.shape[0]
    tr = _row_tile(R)

    def body(p_ref, w_ref, m_ref, v_ref, g_ref, d_ref, mo_ref, vo_ref):
        g = _sum_parts(p_ref)
        g_ref[...] = g
        d_ref[...], mo_ref[...], vo_ref[...] = _adamw_update(w_ref[...], g, m_ref[...], v_ref[...])

    spec = pl.BlockSpec((tr, W), lambda i: (i, 0))
    return _call(
        body, (parts, w, m, v), name=name, grid=(R // tr,),
        in_specs=[pl.BlockSpec((n, tr, W), lambda i: (0, i, 0))] + [spec] * 3,
        out_specs=[spec] * 4, out_shape=[SDS((R, W), F32)] * 4,
        params=_params(1, [((n, tr, W), parts.dtype)] + [((tr, W), F32)] * 7))


def _ada_fwd(c_all, w, bias, *, name):
    NB, D = c_all.shape
    N = w.shape[1]

    def body(c_ref, w_ref, b_ref, o_ref):
        cv = c_ref[...]
        ca = (cv * _sigmoid(cv)).astype(BF16)
        o_ref[...] = _dot(ca, w_ref[...].astype(BF16)) + b_ref[...]

    full = lambda s: pl.BlockSpec(s, lambda i: (0,) * len(s))
    return _call(
        body, (c_all, w, bias), name=name, grid=(1,),
        in_specs=[full((NB, D)), full((D, N)), full((1, N))], out_specs=[full((NB, N))],
        out_shape=[SDS((NB, N), F32)],
        params=_params(1, [((D, N), F32)], temp_bytes=_nbytes((D, N), BF16)))[0]


def _ada_bwd(c_all, gmod_all, *, n_col, name):
    NB, D = c_all.shape
    N = gmod_all.shape[1]

    def body(c_ref, g_ref, gw_ref, gb_ref):
        cv = c_ref[...]
        ca = (cv * _sigmoid(cv)).astype(BF16)
        first = pl.multiple_of(_lin(_my_pos()) * n_col, 128)
        gw_ref[...] = _dot_tn(ca, g_ref[:, pl.ds(first, n_col)].astype(BF16))
        gb_ref[...] = _rowsum(g_ref[...])

    full = lambda s: pl.BlockSpec(s, lambda i: (0,) * len(s))
    return _call(
        body, (c_all, gmod_all), name=name, grid=(1,),
        in_specs=[full((NB, D)), full((NB, N))], out_specs=[full((D, n_col)), full((1, N))],
        out_shape=[SDS((D, n_col), F32), SDS((1, N), F32)],
        params=_params(1, [((D, n_col), F32), ((NB, N), F32)]))


def kernel(x, c, w_ada, b_ada, norm_ffn1_g, ffn1_w_gate, ffn1_w_up, ffn1_w_down, norm_mix_g, w_in, attn_sinks, w_attn_o, conv_w_dw, conv_b_dw, conv_ln_g, conv_ln_b, w_conv_o, w_out, norm_ffn2_g, ffn2_w_gate, ffn2_w_up, ffn2_w_down, final_norm_g, loss_target, m_w_ada, m_b_ada, m_norm_ffn1_g, m_ffn1_w_gate, m_ffn1_w_up, m_ffn1_w_down, m_norm_mix_g, m_w_in, m_attn_sinks, m_w_attn_o, m_conv_w_dw, m_conv_b_dw, m_conv_ln_g, m_conv_ln_b, m_w_conv_o, m_w_out, m_norm_ffn2_g, m_ffn2_w_gate, m_ffn2_w_up, m_ffn2_w_down, m_final_norm_g, v_w_ada, v_b_ada, v_norm_ffn1_g, v_ffn1_w_gate, v_ffn1_w_up, v_ffn1_w_down, v_norm_mix_g, v_w_in, v_attn_sinks, v_w_attn_o, v_conv_w_dw, v_conv_b_dw, v_conv_ln_g, v_conv_ln_b, v_w_conv_o, v_w_out, v_norm_ffn2_g, v_ffn2_w_gate, v_ffn2_w_up, v_ffn2_w_down, v_final_norm_g):
    B, S, D = x.shape
    T = B * S
    QW = N_Q_HEADS * HEAD_DIM
    CC = conv_w_dw.shape[2] * N_DEV
    me = _lin(_my_pos())
    xf = x.reshape(T, D)
    tgt = loss_target.reshape(T, D)
    tm = min(512, S)
    kw = dict(seq=S, tm=tm)

    p_k, p_v, p_ca = QW, QW + KV_WIDTH, QW + 2 * KV_WIDTH
    p_cb, p_ga, p_gc = p_ca + CC, p_ca + 2 * CC, p_ca + 2 * CC + D

    def col_t(w):
        return w[0].T.astype(BF16)

    def row_b(w):
        return w[0].astype(BF16)

    def rows(g):
        return g.reshape(-1, g.shape[-1])

    def blocks8(g):
        return g.reshape(N_DEV, g.shape[0] // N_DEV, g.shape[1])

    def gather(*arrs, hbm_out=False):
        return _Comm([(a, "gather") for a in arrs], hbm_out=hbm_out)

    g_wg1, g_convw, g_c = _exchange(
        [(col_t(ffn1_w_gate), "gather"), (conv_w_dw[0], "gather"), (c, "gather")], name="gather_first")
    wg1 = rows(g_wg1)
    conv_w = g_convw.transpose(1, 0, 2).reshape(CONV_WIDTH, CC)
    c_all = g_c.reshape(N_DEV * B, D)

    n_col = N_MOD * D // N_DEV
    b_cols = lax.dynamic_slice(b_ada, (0, me * n_col), (1, n_col))
    mod_cols = _ada_fwd(c_all, w_ada[0], b_cols, name="ada_fwd")
    mod_mine = _exchange([(mod_cols.reshape(N_DEV, B, n_col), "scatter")], name="scatter_mod")[0]
    mod = mod_mine.transpose(1, 0, 2).reshape(B * N_MOD, 1, D)
    sh1, sc1, g1, sh2, sc2, g2, sh3, sc3, g3 = [_ModVec(mod, i) for i in range(N_MOD)]

    F = wg1.shape[0]
    tn_f = _pick(F, (1408, 1024, 512, 256))
    tn_in = _pick(w_in.shape[2] * N_DEV, (1792, 768, 512, 256))
    gate_blk = dict(ga_col=p_ga, gc_col=p_gc)
    att_blk = dict(q_blk=0, k_blk=p_k // KV_WIDTH, v_blk=p_v // KV_WIDTH)
    conv_kw = dict(seq=S, cw=256, a_col=p_ca, b_col=p_cb)

    cm = gather(col_t(ffn1_w_up))
    h1, (a1,) = _norm_mod_matmul(xf, norm_ffn1_g, sh1, sc1, [wg1], tn=tn_f, name="ffn1_gate", comm=cm, **kw)
    wu1 = rows(cm.out[0])
    cm = gather(row_b(ffn1_w_down), hbm_out=True)
    b1 = _matmul_nt(h1, wu1, tm=tm, tn=tn_f, name="ffn1_up", comm=cm)
    wd1 = rows(cm.out[0])
    cm = gather(col_t(w_in))
    x1, y1 = _ffn_down(a1, b1, wd1, xf, g1, name="ffn1_down", comm=cm, **kw)
    winp = rows(cm.out[0])
    h2, (projp,) = _norm_mod_matmul(x1, norm_mix_g, sh2, sc2, [winp], tn=tn_in, name="mix_in", **kw)
    cm = gather(col_t(ffn2_w_gate))
    ao = _in_hbm(_attn_fwd(projp, attn_sinks, seq=S, name="attn_fwd", comm=cm, **att_blk))
    wg2 = rows(cm.out[0])
    cm = gather(row_b(w_attn_o), row_b(w_conv_o), row_b(w_out), hbm_out=True)
    yc = _in_hbm(_conv_fwd(projp, conv_w, conv_b_dw, name="conv_fwd", comm=cm, **conv_kw))
    wao, wco, wout = [rows(o) for o in cm.out]
    cm = gather(col_t(ffn2_w_up))
    x2, z, ya, ycv, cact, merged = _mix_out(ao, yc, projp, wao, wco, wout, x1, g2, conv_ln_g, conv_ln_b,
                                            name="mix_out", comm=cm, **gate_blk, **kw)
    wu2 = rows(cm.out[0])
    cm = gather(row_b(ffn2_w_down), hbm_out=True)
    h3, (a3, b3) = _norm_mod_matmul(x2, norm_ffn2_g, sh3, sc3, [wg2, wu2], tn=tn_f, name="ffn2_up", comm=cm, **kw)
    wd2 = rows(cm.out[0])
    x3, y3 = _ffn_down(a3, b3, wd2, x2, g3, name="ffn2_down", **kw)
    dx3, loss_row, dgf = _final_loss(_in_hbm(x3), final_norm_g[None], _in_hbm(tgt), tm=tm, name="final_loss")
    dx3 = _in_hbm(dx3)

    parts = {}

    def pair(*gs):
        return [(blocks8(g), "pair") for g in gs]

    def cross(*rs):
        return [(r, "cross") for r in rs]

    def reduce_pairs(gs, staged, names):
        return [_pair_add(blocks8(g), s, name="pair_add_" + n) for g, s, n in zip(gs, staged, names)]

    dyb3, da3, db3, dg3 = _ffn_bwd_down(dx3, g3, y3, wd2, a3, b3, tn=tn_f, name="ffn2_bwd_down", **kw)
    gwd2 = _matmul_tn(a3, dyb3, gate=b3, name="gw_ffn2_down")
    cm = _Comm(pair(gwd2))
    dx2, dsh3, dsc3, dgn3 = _matmul_norm_mod_bwd([[da3], [db3]], [wg2, wu2], x2, norm_ffn2_g, sc3, dx3,
                                                 name="ffn2_bwd_up", out_dtype=GRAD_STREAM, comm=cm, **kw)
    r_wd2, = reduce_pairs([gwd2], cm.out, ["ffn2_w_down"])
    cm = _Comm(cross(r_wd2))
    gwg2 = _matmul_tn(da3, h3, name="gw_ffn2_gate", comm=cm)
    parts["ffn2_w_down"], = cm.out
    cm = _Comm(pair(gwg2))
    gwu2 = _matmul_tn(db3, h3, name="gw_ffn2_up", comm=cm)
    r_wg2, = reduce_pairs([gwg2], cm.out, ["ffn2_w_gate"])

    cm = _Comm(cross(r_wg2) + pair(gwu2))
    dzb, dyab, dycb, dga, dgc, dao, dyc, dg2, dlng, dlnb = _mix_out_bwd(
        dx2, g2, z, wout, projp, ya, ycv, wao, wco, yc, conv_ln_g, conv_ln_b, name="mix_out_bwd", comm=cm,
        **gate_blk, **kw)
    parts["ffn2_w_gate"] = cm.out[0]
    r_wu2, = reduce_pairs([gwu2], cm.out[1:], ["ffn2_w_up"])
    gwout = _matmul_tn(merged, dzb, name="gw_out")
    gwao = _matmul_tn(ao, dyab, name="gw_attn_o")
    gwco = _matmul_tn(cact, dycb, name="gw_conv_o")
    cm = _Comm(cross(r_wu2) + pair(gwout, gwao, gwco))
    dq, dk, dv, dsinks = _attn_bwd(projp, dao, attn_sinks, seq=S, name="attn_bwd", comm=cm, **att_blk)
    parts["ffn2_w_up"] = cm.out[0]
    r_mix = reduce_pairs([gwout, gwao, gwco], cm.out[1:], ["w_out", "w_attn_o", "w_conv_o"])
    cm = _Comm(cross(*r_mix))
    dca, dcb, dconvw, dconvb = _conv_bwd(dyc, projp, conv_w, name="conv_bwd", comm=cm, **conv_kw)
    dca, dcb = _in_hbm(dca), _in_hbm(dcb)
    parts["w_out"], parts["w_attn_o"], parts["w_conv_o"] = cm.out
    gwin = _matmul_tn_pieces([[dq], [dk, dv], [dca], [dcb], [dga], [dgc]], h2, name="gw_in")
    cm = _Comm(pair(gwin))
    dx1, dsh2, dsc2, dgn2 = _matmul_norm_mod_bwd([[dq, dk, dv, dca, dcb, dga, dgc]], [winp], x1, norm_mix_g, sc2, dx2,
                                                 name="mix_in_bwd", out_dtype=GRAD_STREAM, comm=cm, **kw)
    r_win, = reduce_pairs([gwin], cm.out, ["w_in"])

    cm = _Comm(cross(r_win))
    dyb1, da1, db1, dg1 = _ffn_bwd_down(dx1, g1, y1, wd1, a1, b1, tn=tn_f, name="ffn1_bwd_down", comm=cm,
                                              **kw)
    parts["w_in"], = cm.out
    gwd1 = _matmul_tn(a1, dyb1, gate=b1, name="gw_ffn1_down")
    cm = _Comm(pair(gwd1))
    gwg1 = _matmul_tn(da1, h1, name="gw_ffn1_gate", comm=cm)
    r_wd1, = reduce_pairs([gwd1], cm.out, ["ffn1_w_down"])
    cm = _Comm(cross(r_wd1) + pair(gwg1))
    gwu1 = _matmul_tn(db1, h1, name="gw_ffn1_up", comm=cm)
    parts["ffn1_w_down"] = cm.out[0]
    r_wg1, = reduce_pairs([gwg1], cm.out[1:], ["ffn1_w_gate"])
    r_wu1, = reduce_pairs([gwu1], _exchange(pair(gwu1), name="pair_last"), ["ffn1_w_up"])
    cm = _Comm(cross(r_wg1, r_wu1))
    dx0, dsh1, dsc1, dgn1 = _matmul_norm_mod_bwd([[da1], [db1]], [wg1, wu1], xf, norm_ffn1_g, sc1, dx1,
                                                 name="ffn1_bwd_up", out_dtype=F32, comm=cm, **kw)
    parts["ffn1_w_gate"], parts["ffn1_w_up"] = cm.out

    n_small = 8
    gmod = jnp.concatenate([dsh1, dsc1, dg1, dsh2, dsc2, dg2, dsh3, dsc3, dg3], axis=1).reshape(B, N_MOD * D)
    sink_row = jnp.pad(dsinks[:, :N_Q_HEADS], ((0, 0), (0, D - N_Q_HEADS)))
    loss_pad = jnp.pad(loss_row, ((0, 0), (0, D - loss_row.shape[1])))
    small = jnp.concatenate([dgn1, dgn2, dgn3, dgf, dconvb, dlng, dlnb, sink_row, dconvw, loss_pad], axis=0)
    small_all, gmod_all = _exchange([(small, "gather"), (gmod, "gather")], name="exchange_last")
    gsmall = _sum8(small_all, name="sum_small")
    loss = gsmall[n_small + CONV_WIDTH, 0]
    g_w_ada, g_b_ada = _ada_bwd(c_all, gmod_all.reshape(N_DEV * B, N_MOD * D), n_col=n_col, name="ada_bwd")
    g_conv_w = lax.dynamic_slice(gsmall[n_small:n_small + CONV_WIDTH], (0, me * (CC // N_DEV)),
                                 (CONV_WIDTH, CC // N_DEV))

    def col_update(name, w, m, v):
        outs = _sum8_adamw(parts[name], w[0].T, m[0].T, v[0].T, name="adamw_" + name)
        return tuple(o.T for o in outs)

    def row_update(name, w, m, v):
        return tuple(_sum8_adamw(parts[name], w[0], m[0], v[0], name="adamw_" + name))

    upd = {
        "ffn1_w_gate": col_update("ffn1_w_gate", ffn1_w_gate, m_ffn1_w_gate, v_ffn1_w_gate),
        "ffn1_w_up": col_update("ffn1_w_up", ffn1_w_up, m_ffn1_w_up, v_ffn1_w_up),
        "ffn1_w_down": row_update("ffn1_w_down", ffn1_w_down, m_ffn1_w_down, v_ffn1_w_down),
        "w_in": col_update("w_in", w_in, m_w_in, v_w_in),
        "w_attn_o": row_update("w_attn_o", w_attn_o, m_w_attn_o, v_w_attn_o),
        "w_conv_o": row_update("w_conv_o", w_conv_o, m_w_conv_o, v_w_conv_o),
        "w_out": row_update("w_out", w_out, m_w_out, v_w_out),
        "ffn2_w_gate": col_update("ffn2_w_gate", ffn2_w_gate, m_ffn2_w_gate, v_ffn2_w_gate),
        "ffn2_w_up": col_update("ffn2_w_up", ffn2_w_up, m_ffn2_w_up, v_ffn2_w_up),
        "ffn2_w_down": row_update("ffn2_w_down", ffn2_w_down, m_ffn2_w_down, v_ffn2_w_down),
        "w_ada": (g_w_ada,) + tuple(_adamw(g_w_ada, w_ada[0], m_w_ada[0], v_w_ada[0], name="adamw_w_ada")),
        "conv_w_dw": (g_conv_w,) + tuple(_adamw(g_conv_w, conv_w_dw[0], m_conv_w_dw[0], v_conv_w_dw[0],
                                                name="adamw_conv_w_dw")),
    }
    for k in upd:
        upd[k] = tuple(t[None] for t in upd[k])

    def pad_sinks(t):
        return jnp.pad(t, ((0, 0), (0, D - N_Q_HEADS)))

    def pack(f1, mix, f2, fin, cb, lg, lb, sinks, bada):
        return jnp.concatenate([f1, mix, f2, fin[None], cb, lg, lb, pad_sinks(sinks), bada.reshape(N_MOD, D)], axis=0)

    w_s = pack(norm_ffn1_g, norm_mix_g, norm_ffn2_g, final_norm_g, conv_b_dw, conv_ln_g, conv_ln_b, attn_sinks, b_ada)
    m_s = pack(m_norm_ffn1_g, m_norm_mix_g, m_norm_ffn2_g, m_final_norm_g, m_conv_b_dw, m_conv_ln_g, m_conv_ln_b,
               m_attn_sinks, m_b_ada)
    v_s = pack(v_norm_ffn1_g, v_norm_mix_g, v_norm_ffn2_g, v_final_norm_g, v_conv_b_dw, v_conv_ln_g, v_conv_ln_b,
               v_attn_sinks, v_b_ada)
    g_s = jnp.concatenate([gsmall[:n_small], g_b_ada.reshape(N_MOD, D)], axis=0)
    small_out = (g_s,) + tuple(_adamw(g_s, w_s, m_s, v_s, name="adamw_vectors"))

    def unpack(t):
        return {
            "norm_ffn1_g": t[0:1], "norm_mix_g": t[1:2], "norm_ffn2_g": t[2:3], "final_norm_g": t[3],
            "conv_b_dw": t[4:5], "conv_ln_g": t[5:6], "conv_ln_b": t[6:7], "attn_sinks": t[7:8, :N_Q_HEADS],
            "b_ada": t[n_small:n_small + N_MOD].reshape(1, N_MOD * D),
        }

    small_un = [unpack(t) for t in small_out]
    for k in small_un[0]:
        upd[k] = tuple(s[k] for s in small_un)

    order = ["w_ada", "b_ada", "norm_ffn1_g", "ffn1_w_gate", "ffn1_w_up", "ffn1_w_down", "norm_mix_g", "w_in",
             "attn_sinks", "w_attn_o", "conv_w_dw", "conv_b_dw", "conv_ln_g", "conv_ln_b", "w_conv_o", "w_out",
             "norm_ffn2_g", "ffn2_w_gate", "ffn2_w_up", "ffn2_w_down", "final_norm_g"]
    grad_x = dx0.reshape(B, S, D)
    return (loss, grad_x, *[upd[k][0] for k in order], *[upd[k][1] for k in order],
            *[upd[k][2] for k in order], *[upd[k][3] for k in order])
```

```python
import dataclasses

import jax
import jax.numpy as jnp
from jax import lax
from jax.experimental import pallas as pl
from jax.experimental.pallas import tpu as pltpu

F32 = jnp.float32
BF16 = jnp.bfloat16
SDS = jax.ShapeDtypeStruct
MESH = pl.DeviceIdType.MESH

N_DEV = 8
EPS = 1e-6
HEAD_DIM = 64
N_Q_HEADS = 16
N_KV_HEADS = 2
GQA_GROUP = N_Q_HEADS // N_KV_HEADS
KV_WIDTH = N_KV_HEADS * HEAD_DIM
ATT_BLOCK = 128
CONV_WIDTH = 31
CONV_HALO = 32
CONV_ROWS = 128
N_MOD = 9
FFN_RESIDUAL = 0.5
ADAM_LR = 0.001
ADAM_B1 = 0.9
ADAM_B2 = 0.999
ADAM_EPS = 1e-08
ADAM_WD = 0.01
ADAM_STEP = 10
NEG_BIG = -1e30
GRAD_STREAM = BF16

V7X_VMEM_BYTES = 64 * 2**20
VMEM_CAP = V7X_VMEM_BYTES - 8 * 2**20


def _nbytes(shape, dtype):
    n = 1
    for s in shape:
        n *= s
    return n * jnp.dtype(dtype).itemsize


def _params(n_axes, blocks, temp_bytes=0):
    need = 2 * sum(_nbytes(s, d) for s, d in blocks) + temp_bytes + 4 * 2**20
    return pltpu.CompilerParams(dimension_semantics=("arbitrary",) * n_axes,
                                vmem_limit_bytes=int(min(max(need, 16 * 2**20), VMEM_CAP)))


def _dot_nt(a, b):
    return lax.dot_general(a, b, (((1,), (1,)), ((), ())), preferred_element_type=F32)


def _dot_tn(a, b):
    return lax.dot_general(a, b, (((0,), (0,)), ((), ())), preferred_element_type=F32)


def _dot(a, b):
    return jnp.dot(a, b, preferred_element_type=F32)


def _sigmoid(x):
    return jax.nn.sigmoid(x)


def _rowsum(v):
    return jnp.sum(v, axis=0, keepdims=True)


def _acc(ref, val, first):
    @pl.when(first)
    def _():
        ref[...] = val

    @pl.when(jnp.logical_not(first))
    def _():
        ref[...] = ref[...] + val


def _norm_mod(xf, gn, sh, sc):
    rstd = lax.rsqrt(jnp.mean(xf * xf, axis=-1, keepdims=True) + EPS)
    xhat = xf * rstd
    yn = xhat * gn
    return yn * (1.0 + sc) + sh, xhat, rstd, yn


def _pick(n, cands):
    for c in cands:
        if n % c == 0:
            return c
    return n


def _my_pos():
    return lax.axis_index("x"), lax.axis_index("y"), lax.axis_index("c")


def _peer(pos, k):
    x, y, c = pos
    return ((1 - x) if k & 4 else x, (1 - y) if k & 2 else y, (1 - c) if k & 1 else c)


def _lin(pos):
    return 4 * pos[0] + 2 * pos[1] + pos[2]


def _in_hbm(a):
    return pltpu.with_memory_space_constraint(a, pltpu.HBM)


class _Comm:
    N_COPY = N_DEV - 1
    N_CHIP = N_DEV // 2

    def __init__(self, items, hbm_out=False):
        self.hbm_out = hbm_out
        self.arrs = [a for a, _ in items]
        self.modes = [m for _, m in items]
        self.n = len(items)
        self.out = None

    def out_shape(self):
        def shape(a, m):
            return {"gather": (N_DEV,) + a.shape, "scatter": a.shape, "pair": (self.N_CHIP,) + a.shape[1:],
                    "cross": a.shape}[m]
        kind = pltpu.HBM if self.hbm_out else SDS
        return [kind(shape(a, m), a.dtype) for a, m in zip(self.arrs, self.modes)]

    def scratch(self):
        return [pltpu.SemaphoreType.DMA((self.n * self.N_COPY,)), pltpu.SemaphoreType.DMA((self.n * self.N_COPY,)),
                pltpu.SemaphoreType.DMA((self.n,))]

    def collective_id(self):
        modes = set(self.modes)
        if "scatter" in modes:
            return 3
        d2d, ici = bool(modes & {"gather", "pair"}), bool(modes & {"gather", "cross"})
        return {(True, False): 0, (False, True): 1, (True, True): 2}[(d2d, ici)]

    def barrier(self):
        x, y, c = _my_pos()
        peers = {0: [(x, y, 1 - c)],
                 1: [(1 - x, y, c), (x, 1 - y, c), (1 - x, 1 - y, c)],
                 2: [(x, y, 1 - c), (1 - x, y, c), (x, 1 - y, c), (1 - x, 1 - y, c)],
                 3: [_peer((x, y, c), k) for k in range(1, N_DEV)]}[self.collective_id()]
        sem = pltpu.get_barrier_semaphore()
        for p in peers:
            pl.semaphore_signal(sem, inc=1, device_id=p, device_id_type=MESH)
        pl.semaphore_wait(sem, len(peers))

    def _plan(self, mode, me):
        x, y, c = me
        sib = (x, y, 1 - c)
        chips = [(1 - x, y), (x, 1 - y), (1 - x, 1 - y)]

        def chip_lin(ch):
            return 2 * ch[0] + ch[1]

        if mode == "scatter":
            peers = [_peer(me, k + 1) for k in range(self.N_COPY)]
            return [(p, ("in", _lin(p)), _lin(me), _lin(p), None) for p in peers], (_lin(me), _lin(me))
        if mode == "gather":
            same = [(*ch, c) for ch in chips]
            other = [(*ch, 1 - c) for ch in chips]
            copies = [(sib, ("in", None), _lin(me), _lin(sib), None)]
            copies += [(p, ("in", None), _lin(me), _lin(p), None) for p in same]
            copies += [(sib, ("out", _lin(p)), _lin(p), _lin(o), 1 + j) for j, (p, o) in enumerate(zip(same, other))]
            return copies, (None, _lin(me))
        if mode == "pair":
            return [(sib, ("in", 2 * q + 1 - c), q, q, None) for q in range(self.N_CHIP)], None
        if mode == "cross":
            mine = chip_lin((x, y))
            return ([((*ch, c), ("in", chip_lin(ch)), mine, chip_lin(ch), None) for ch in chips], (mine, mine))
        raise ValueError(mode)

    def _copy(self, refs, me, i, k, recv):
        srcs, outs, (send_sems, recv_sems, _) = refs
        peer, (where, slot), send_slot, recv_slot, _ = self._plan(self.modes[i], me)[0][k]
        src = srcs[i] if where == "in" else outs[i]
        src = src if slot is None else src.at[slot]
        sem = i * self.N_COPY + k
        return pltpu.make_async_remote_copy(
            src_ref=src, dst_ref=outs[i].at[recv_slot if recv else send_slot], send_sem=send_sems.at[sem],
            recv_sem=recv_sems.at[sem], device_id=peer, device_id_type=MESH)

    def _local(self, refs, me, i):
        srcs, outs, (_, _, loc_sems) = refs
        local = self._plan(self.modes[i], me)[1]
        if local is None:
            return None
        own = srcs[i] if local[0] is None else srcs[i].at[local[0]]
        return pltpu.make_async_copy(own, outs[i].at[local[1]], loc_sems.at[i])

    def start(self, refs):
        me = _my_pos()
        for i in range(self.n):
            local = self._local(refs, me, i)
            if local is not None:
                local.start()
            for k, cp in enumerate(self._plan(self.modes[i], me)[0]):
                if cp[4] is None:
                    self._copy(refs, me, i, k, False).start()

    def forward(self, refs):
        me = _my_pos()
        for i in range(self.n):
            for k, cp in enumerate(self._plan(self.modes[i], me)[0]):
                if cp[4] is not None:
                    self._copy(refs, me, i, cp[4], True).wait_recv()
                    self._copy(refs, me, i, k, False).start()

    def finish(self, refs):
        me = _my_pos()
        plans = [self._plan(m, me)[0] for m in self.modes]
        for i in range(self.n):
            passed_on = [cp[4] for cp in plans[i] if cp[4] is not None]
            for k in range(len(plans[i])):
                if k not in passed_on:
                    self._copy(refs, me, i, k, True).wait_recv()
                self._copy(refs, me, i, k, False).wait_send()
            local = self._local(refs, me, i)
            if local is not None:
                local.wait()


_ANY = pl.BlockSpec(memory_space=pl.ANY)


def _call(body, args, *, name, grid, in_specs, out_specs, out_shape, params, scratch_shapes=(), comm=None,
          hbm_out=()):
    in_specs, out_specs, out_shape = list(in_specs), list(out_specs), list(out_shape)
    scratch_shapes = list(scratch_shapes)
    for k in hbm_out:
        out_shape[k] = pltpu.HBM(out_shape[k].shape, out_shape[k].dtype)
    if comm is None:
        return list(pl.pallas_call(body, name=name, grid=grid, in_specs=in_specs, out_specs=out_specs,
                                   out_shape=out_shape, scratch_shapes=scratch_shapes, compiler_params=params)(*args))
    n_in, n_out, n_scr, nc = len(in_specs), len(out_specs), len(scratch_shapes), comm.n
    n_steps = 1
    for g in grid:
        n_steps *= g

    def hosted(*refs):
        ins, c_in = refs[:n_in], refs[n_in:n_in + nc]
        outs = refs[n_in + nc:n_in + nc + n_out]
        c_out = refs[n_in + nc + n_out:n_in + 2 * nc + n_out]
        scr = refs[n_in + 2 * nc + n_out:n_in + 2 * nc + n_out + n_scr]
        sems = refs[n_in + 2 * nc + n_out + n_scr:]
        step = pl.program_id(0)
        for d in range(1, len(grid)):
            step = step * grid[d] + pl.program_id(d)
        c_refs = (c_in, c_out, sems)

        @pl.when(step == 0)
        def _():
            comm.barrier()
            comm.start(c_refs)

        if n_steps >= 3:
            @pl.when(step == n_steps - 2)
            def _():
                comm.forward(c_refs)

        body(*ins, *outs, *scr)

        @pl.when(step == n_steps - 1)
        def _():
            if n_steps < 3:
                comm.forward(c_refs)
            comm.finish(c_refs)

    res = pl.pallas_call(
        hosted, name=name, grid=grid, in_specs=in_specs + [_ANY] * nc, out_specs=out_specs + [_ANY] * nc,
        out_shape=out_shape + comm.out_shape(), scratch_shapes=scratch_shapes + comm.scratch(),
        compiler_params=dataclasses.replace(params, collective_id=comm.collective_id()))(*args, *comm.arrs)
    comm.out = list(res[n_out:])
    return list(res[:n_out])


def _exchange(items, *, name):
    comm = _Comm(items)

    def body(*refs):
        r = (refs[:comm.n], refs[comm.n:2 * comm.n], refs[2 * comm.n:])
        comm.barrier()
        comm.start(r)
        comm.forward(r)
        comm.finish(r)

    return list(pl.pallas_call(body, name=name, out_shape=comm.out_shape(), in_specs=[_ANY] * comm.n,
                               out_specs=[_ANY] * comm.n, scratch_shapes=comm.scratch(),
                               compiler_params=pltpu.CompilerParams(collective_id=comm.collective_id()))(*comm.arrs))


class _ModVec:
    def __init__(self, arr, idx):
        self.arr, self.idx = arr, idx

    def spec(self, tps, n_axes):
        idx, blk = self.idx, (1, 1, self.arr.shape[2])
        if n_axes == 1:
            return pl.BlockSpec(blk, lambda i: (i // tps * N_MOD + idx, 0, 0))
        return pl.BlockSpec(blk, lambda i, j: (i // tps * N_MOD + idx, 0, 0))


def _norm_mod_matmul(x, gn, sh, sc, wts, *, seq, tm, tn, name, comm=None):
    T, D = x.shape
    N = wts[0].shape[0]
    nw = len(wts)
    tps = seq // tm

    def body(x_ref, gn_ref, sh_ref, sc_ref, *rest):
        w_refs, h_ref, o_refs = rest[:nw], rest[nw], rest[nw + 1:]

        @pl.when(pl.program_id(1) == 0)
        def _():
            h_ref[...] = _norm_mod(x_ref[...], gn_ref[...], sh_ref[0], sc_ref[0])[0].astype(BF16)

        h = h_ref[...]
        for w_ref, o_ref in zip(w_refs, o_refs):
            o_ref[...] = _dot_nt(h, w_ref[...]).astype(o_ref.dtype)

    row = pl.BlockSpec((tm, D), lambda i, j: (i, 0))
    vec = pl.BlockSpec((1, D), lambda i, j: (0, 0))
    wspec = pl.BlockSpec((tn, D), lambda i, j: (j, 0))
    ospec = pl.BlockSpec((tm, tn), lambda i, j: (i, j))
    blocks = [((tm, D), F32), ((tm, D), BF16)] + [((tn, D), BF16), ((tm, tn), BF16)] * nw
    outs = _call(
        body, (x, gn, sh.arr, sc.arr, *wts), name=name, grid=(T // tm, N // tn),
        in_specs=[row, vec, sh.spec(tps, 2), sc.spec(tps, 2)] + [wspec] * nw,
        out_specs=[row] + [ospec] * nw,
        out_shape=[SDS((T, D), BF16)] + [SDS((T, N), BF16)] * nw,
        params=_params(2, blocks, temp_bytes=2 * _nbytes((tm, tn), F32) + 3 * _nbytes((tm, D), F32)), comm=comm)
    return outs[0], outs[1:]


def _matmul_nt(h, w, *, tm, tn, name, comm=None):
    T, D = h.shape
    N = w.shape[0]

    def body(h_ref, w_ref, o_ref):
        o_ref[...] = _dot_nt(h_ref[...], w_ref[...]).astype(o_ref.dtype)

    blocks = [((tm, D), BF16), ((tn, D), BF16), ((tm, tn), BF16)]
    return _call(
        body, (h, w), name=name, grid=(T // tm, N // tn),
        in_specs=[pl.BlockSpec((tm, D), lambda i, j: (i, 0)), pl.BlockSpec((tn, D), lambda i, j: (j, 0))],
        out_specs=[pl.BlockSpec((tm, tn), lambda i, j: (i, j))],
        out_shape=[SDS((T, N), BF16)],
        params=_params(2, blocks, temp_bytes=2 * _nbytes((tm, tn), F32)), comm=comm)[0]


def _ffn_down(a, b, wd, x, g, *, seq, tm, name, comm=None):
    T, F = a.shape
    D = wd.shape[1]
    tps = seq // tm

    def body(a_ref, b_ref, wd_ref, x_ref, g_ref, xo_ref, y_ref):
        af = a_ref[...].astype(F32)
        act = (af * _sigmoid(af) * b_ref[...].astype(F32)).astype(BF16)
        y = _dot(act, wd_ref[...])
        xo_ref[...] = x_ref[...] + (FFN_RESIDUAL * g_ref[0]) * y
        y_ref[...] = y.astype(BF16)

    wide = pl.BlockSpec((tm, F), lambda i: (i, 0))
    row = pl.BlockSpec((tm, D), lambda i: (i, 0))
    wspec = pl.BlockSpec((F, D), lambda i: (0, 0))
    blocks = [((tm, F), BF16)] * 2 + [((F, D), BF16), ((tm, D), F32), ((tm, D), F32), ((tm, D), BF16)]
    return _call(
        body, (a, b, wd, x, g.arr), name=name, grid=(T // tm,),
        in_specs=[wide, wide, wspec, row, g.spec(tps, 1)], out_specs=[row, row],
        out_shape=[SDS((T, D), F32), SDS((T, D), BF16)],
        params=_params(1, blocks, temp_bytes=3 * _nbytes((tm, F), F32)), comm=comm)


def _final_loss(x, gf, tgt, *, tm, name):
    T, D = x.shape
    nt = T // tm

    def body(x_ref, gf_ref, t_ref, dx_ref, loss_ref, dgf_ref, lacc):
        i = pl.program_id(0)
        xf = x_ref[...]
        gfv = gf_ref[...]
        rstd = lax.rsqrt(jnp.mean(xf * xf, axis=-1, keepdims=True) + EPS)
        xhat = xf * rstd
        err = xhat * gfv - t_ref[...]
        dy = err * (1.0 / D)
        dxhat = dy * gfv
        dx_ref[...] = (rstd * (dxhat - xhat * jnp.mean(dxhat * xhat, axis=-1, keepdims=True))).astype(dx_ref.dtype)
        _acc(dgf_ref, _rowsum(dy * xhat), i == 0)
        _acc(lacc, _rowsum(err * err), i == 0)

        @pl.when(i == nt - 1)
        def _():
            loss_ref[...] = jnp.broadcast_to((0.5 / D) * jnp.sum(lacc[...]), loss_ref.shape)

    row = pl.BlockSpec((tm, D), lambda i: (i, 0))
    vec = pl.BlockSpec((1, D), lambda i: (0, 0))
    lspec = pl.BlockSpec((1, 128), lambda i: (0, 0))
    blocks = [((tm, D), F32)] * 3
    return _call(
        body, (x, gf, tgt), name=name, grid=(nt,),
        in_specs=[row, vec, row], out_specs=[row, lspec, vec],
        out_shape=[SDS((T, D), GRAD_STREAM), SDS((1, 128), F32), SDS((1, D), F32)],
        scratch_shapes=[pltpu.VMEM((1, D), F32)],
        params=_params(1, blocks, temp_bytes=4 * _nbytes((tm, D), F32)), hbm_out=(0,))


def _ffn_bwd_down(dxo, g, y, wd, a, b, *, seq, tm, tn, name, comm=None):
    T, F = a.shape
    D = wd.shape[1]
    tps = seq // tm
    nb = T // seq

    def body(dxo_ref, g_ref, y_ref, wd_ref, a_ref, b_ref, dyb_ref, da_ref, db_ref, dg_ref):
        i = pl.program_id(0)

        @pl.when(pl.program_id(1) == 0)
        def _():
            dx = dxo_ref[...].astype(F32)
            dyb_ref[...] = ((FFN_RESIDUAL * g_ref[0]) * dx).astype(BF16)
            part = _rowsum(FFN_RESIDUAL * dx * y_ref[...].astype(F32))
            _acc(dg_ref, part[None], i % tps == 0)

        dact = _dot_nt(dyb_ref[...], wd_ref[...])
        af = a_ref[...].astype(F32)
        bf = b_ref[...].astype(F32)
        sg = _sigmoid(af)
        silu = af * sg
        da_ref[...] = (dact * bf * (sg + silu * (1.0 - sg))).astype(BF16)
        db_ref[...] = (dact * silu).astype(BF16)

    row = pl.BlockSpec((tm, D), lambda i, j: (i, 0))
    per_b = pl.BlockSpec((1, 1, D), lambda i, j: (i // tps, 0, 0))
    wspec = pl.BlockSpec((tn, D), lambda i, j: (j, 0))
    chunk = pl.BlockSpec((tm, tn), lambda i, j: (i, j))
    blocks = [((tm, D), F32), ((tm, D), BF16), ((tn, D), BF16), ((tm, D), BF16)] + [((tm, tn), BF16)] * 4
    return _call(
        body, (dxo, g.arr, y, wd, a, b), name=name, grid=(T // tm, F // tn),
        in_specs=[row, g.spec(tps, 2), row, wspec, chunk, chunk],
        out_specs=[row, chunk, chunk, per_b],
        out_shape=[SDS((T, D), BF16)] + [SDS((T, F), BF16)] * 2 + [SDS((nb, 1, D), F32)],
        params=_params(2, blocks, temp_bytes=6 * _nbytes((tm, tn), F32)), comm=comm)


def _matmul_norm_mod_bwd(ds, ws, x, gn, sc, dxo, *, seq, tm, name, out_dtype, comm=None):
    T, D = x.shape
    nk = len(ws)
    sizes = [len(g) for g in ds]
    ds = [d for g in ds for d in g]
    tps = seq // tm
    nb = T // seq

    def body(*refs):
        w_refs = refs[len(ds):len(ds) + nk]
        x_ref, gn_ref, sc_ref, dxo_ref, dxi_ref, dsh_ref, dsc_ref, dgn_ref = refs[len(ds) + nk:]
        i = pl.program_id(0)
        dh, at = None, 0
        for n, w_ref in zip(sizes, w_refs):
            pieces = [r[...] for r in refs[at:at + n]]
            at += n
            part = _dot(pieces[0] if n == 1 else jnp.concatenate(pieces, axis=1), w_ref[...])
            dh = part if dh is None else dh + part
        gnv = gn_ref[...]
        scv = sc_ref[0]
        _, xhat, rstd, yn = _norm_mod(x_ref[...], gnv, 0.0, scv)
        dyn = dh * (1.0 + scv)
        dxhat = dyn * gnv
        dxi_ref[...] = (dxo_ref[...].astype(F32)
                        + rstd * (dxhat - xhat * jnp.mean(dxhat * xhat, axis=-1, keepdims=True))).astype(out_dtype)
        first_of_seq = i % tps == 0
        _acc(dsh_ref, _rowsum(dh)[None], first_of_seq)
        _acc(dsc_ref, _rowsum(dh * yn)[None], first_of_seq)
        _acc(dgn_ref, _rowsum(dyn * xhat), i == 0)

    row = pl.BlockSpec((tm, D), lambda i: (i, 0))
    vec = pl.BlockSpec((1, D), lambda i: (0, 0))
    per_b = pl.BlockSpec((1, 1, D), lambda i: (i // tps, 0, 0))
    d_specs = [pl.BlockSpec((tm, d.shape[1]), lambda i: (i, 0)) for d in ds]
    w_specs = [pl.BlockSpec(w.shape, lambda i: (0, 0)) for w in ws]
    blocks = ([((tm, d.shape[1]), BF16) for d in ds] + [(w.shape, BF16) for w in ws] + [((tm, D), F32)] * 3)
    return _call(
        body, (*ds, *ws, x, gn, sc.arr, dxo), name=name, grid=(T // tm,),
        in_specs=d_specs + w_specs + [row, vec, sc.spec(tps, 1), row],
        out_specs=[row, per_b, per_b, vec],
        out_shape=[SDS((T, D), out_dtype), SDS((nb, 1, D), F32), SDS((nb, 1, D), F32), SDS((1, D), F32)],
        params=_params(1, blocks, temp_bytes=6 * _nbytes((tm, D), F32)), comm=comm)


def _layernorm_silu(yc, lg, lb):
    mu = jnp.mean(yc, axis=-1, keepdims=True)
    cen = yc - mu
    rstd = lax.rsqrt(jnp.mean(cen * cen, axis=-1, keepdims=True) + EPS)
    xh = cen * rstd
    l = xh * lg + lb
    s = _sigmoid(l)
    return l * s, xh, rstd, l, s


GATE_W = 256


def _gate_specs(tm, D, col):
    return [pl.BlockSpec((tm, GATE_W), lambda i, blk=col // GATE_W + t: (i, blk)) for t in range(D // GATE_W)]


def _gate(refs):
    return jnp.concatenate([r[...] for r in refs], axis=1).astype(F32)


def _mix_out(ao, yc, proj, wao, wco, wout, x1, g2, lg, lb, *, seq, tm, ga_col, gc_col, name, comm=None):
    T, D = x1.shape
    tps = seq // tm
    ng = D // GATE_W

    def body(ao_ref, yc_ref, *rest):
        ga_refs, gc_refs = rest[:ng], rest[ng:2 * ng]
        (wao_ref, wco_ref, wout_ref, x1_ref, g2_ref, lg_ref, lb_ref,
         x2_ref, z_ref, ya_ref, ycv_ref, cact_ref, mrg_ref) = rest[2 * ng:]
        ya = _dot(ao_ref[...], wao_ref[...])
        cact = _layernorm_silu(yc_ref[...], lg_ref[...], lb_ref[...])[0].astype(BF16)
        ycv = _dot(cact, wco_ref[...])
        merged = (_sigmoid(_gate(ga_refs)) * ya + _sigmoid(_gate(gc_refs)) * ycv).astype(BF16)
        z = _dot(merged, wout_ref[...])
        x2_ref[...] = x1_ref[...] + g2_ref[0] * z
        z_ref[...] = z.astype(BF16)
        ya_ref[...] = ya.astype(BF16)
        ycv_ref[...] = ycv.astype(BF16)
        cact_ref[...] = cact
        mrg_ref[...] = merged

    row = pl.BlockSpec((tm, D), lambda i: (i, 0))
    vec = pl.BlockSpec((1, D), lambda i: (0, 0))
    wspec = pl.BlockSpec((D, D), lambda i: (0, 0))
    gates = _gate_specs(tm, D, ga_col) + _gate_specs(tm, D, gc_col)
    blocks = ([((tm, D), BF16), ((tm, D), F32), ((tm, D), BF16), ((tm, D), BF16)] + [((D, D), BF16)] * 3
              + [((tm, D), F32)] * 2 + [((tm, D), BF16)] * 5)
    return _call(
        body, (ao, yc, *[proj] * (2 * ng), wao, wco, wout, x1, g2.arr, lg, lb), name=name, grid=(T // tm,),
        in_specs=[row, row, *gates, wspec, wspec, wspec, row, g2.spec(tps, 1), vec, vec],
        out_specs=[row] * 6,
        out_shape=[SDS((T, D), F32)] + [SDS((T, D), BF16)] * 5,
        params=_params(1, blocks, temp_bytes=8 * _nbytes((tm, D), F32)), comm=comm)


def _mix_out_bwd(dx2, g2, z, wout, proj, ya, ycv, wao, wco, yc, lg, lb, *, seq, tm, ga_col, gc_col, name,
                 comm=None):
    T, D = dx2.shape
    tps = seq // tm
    nb = T // seq
    ng = D // GATE_W

    def body(dx2_ref, g2_ref, z_ref, wout_ref, *rest):
        ga_refs, gc_refs = rest[:ng], rest[ng:2 * ng]
        (ya_ref, ycv_ref, wao_ref, wco_ref, yc_ref, lg_ref, lb_ref, dz_ref, dya_ref, dycv_ref, dga_ref, dgc_ref,
         dao_ref, dyc_ref, dg2_ref, dlg_ref, dlb_ref) = rest[2 * ng:]
        i = pl.program_id(0)
        dx = dx2_ref[...].astype(F32)
        _acc(dg2_ref, _rowsum(dx * z_ref[...].astype(F32))[None], i % tps == 0)
        dzb = (g2_ref[0] * dx).astype(BF16)
        dz_ref[...] = dzb
        dmerged = _dot_nt(dzb, wout_ref[...])
        sa = _sigmoid(_gate(ga_refs))
        sc_ = _sigmoid(_gate(gc_refs))
        dya = (dmerged * sa).astype(BF16)
        dycv = (dmerged * sc_).astype(BF16)
        dya_ref[...] = dya
        dycv_ref[...] = dycv
        dga_ref[...] = (dmerged * ya_ref[...].astype(F32) * (sa * (1.0 - sa))).astype(BF16)
        dgc_ref[...] = (dmerged * ycv_ref[...].astype(F32) * (sc_ * (1.0 - sc_))).astype(BF16)
        dao_ref[...] = _dot_nt(dya, wao_ref[...]).astype(BF16)
        dcact = _dot_nt(dycv, wco_ref[...])
        lgv = lg_ref[...]
        _, xh, rstd, l, s = _layernorm_silu(yc_ref[...], lgv, lb_ref[...])
        dl = dcact * (s * (1.0 + l * (1.0 - s)))
        _acc(dlb_ref, _rowsum(dl), i == 0)
        _acc(dlg_ref, _rowsum(dl * xh), i == 0)
        dxh = dl * lgv
        dyc_ref[...] = rstd * (dxh - jnp.mean(dxh, axis=-1, keepdims=True)
                               - xh * jnp.mean(dxh * xh, axis=-1, keepdims=True))

    row = pl.BlockSpec((tm, D), lambda i: (i, 0))
    vec = pl.BlockSpec((1, D), lambda i: (0, 0))
    per_b = pl.BlockSpec((1, 1, D), lambda i: (i // tps, 0, 0))
    wspec = pl.BlockSpec((D, D), lambda i: (0, 0))
    gates = _gate_specs(tm, D, ga_col) + _gate_specs(tm, D, gc_col)
    blocks = ([((tm, D), F32)] * 3 + [((tm, D), BF16)] * 11 + [((D, D), BF16)] * 3)
    return _call(
        body, (dx2, g2.arr, z, wout, *[proj] * (2 * ng), ya, ycv, wao, wco, yc, lg, lb), name=name,
        grid=(T // tm,),
        in_specs=[row, g2.spec(tps, 1), row, wspec, *gates, row, row, wspec, wspec, row, vec, vec],
        out_specs=[row] * 7 + [per_b, vec, vec],
        out_shape=[SDS((T, D), BF16)] * 6 + [SDS((T, D), F32), SDS((nb, 1, D), F32), SDS((1, D), F32),
                                             SDS((1, D), F32)],
        params=_params(1, blocks, temp_bytes=10 * _nbytes((tm, D), F32)), comm=comm)


Q_BLOCK = 64
BAND = Q_BLOCK + ATT_BLOCK
GROUP_ROWS = GQA_GROUP * Q_BLOCK
PAIR_W = 2 * HEAD_DIM
GROUP_W = GQA_GROUP * HEAD_DIM


def _lane_lo():
    return lax.broadcasted_iota(jnp.int32, (1, PAIR_W), 1) < HEAD_DIM


def _band_bias():
    sj = lax.broadcasted_iota(jnp.int32, (BAND, GROUP_ROWS), 0)
    qi = lax.broadcasted_iota(jnp.int32, (BAND, GROUP_ROWS), 1) & (Q_BLOCK - 1)
    rel = qi + ATT_BLOCK - sj
    bias = jnp.where(jnp.logical_and(rel >= 0, rel < ATT_BLOCK), 0.0, NEG_BIG)
    return bias, lax.broadcasted_iota(jnp.int32, (BAND, 1), 0)


def _block_bias(bias0, key_index, r0):
    return bias0 + jnp.where(key_index + r0 < ATT_BLOCK, NEG_BIG, 0.0)


def _dup_heads(src_ref, dst, seq):
    x = src_ref[...]
    i = lax.broadcasted_iota(jnp.int32, (KV_WIDTH, PAIR_W), 0)
    j = lax.broadcasted_iota(jnp.int32, (KV_WIDTH, PAIR_W), 1) & (HEAD_DIM - 1)
    for g in range(N_KV_HEADS):
        sel = jnp.where(i == j + g * HEAD_DIM, 1.0, 0.0).astype(BF16)
        dst[g, pl.ds(0, ATT_BLOCK), :] = jnp.zeros((ATT_BLOCK, PAIR_W), BF16)
        dst[g, pl.ds(ATT_BLOCK, seq), :] = _dot(x, sel).astype(BF16)


def _stack_heads(blk, g, lo):
    parts = []
    for p in range(GQA_GROUP // 2):
        pair = blk[:, g * GROUP_W + p * PAIR_W:g * GROUP_W + (p + 1) * PAIR_W]
        parts += [jnp.where(lo, pair, jnp.zeros_like(pair)), jnp.where(lo, jnp.zeros_like(pair), pair)]
    return jnp.concatenate(parts, axis=0)


def _unstack_heads(full, ref, r0, g, lo):
    for p in range(GQA_GROUP // 2):
        even = full[(2 * p) * Q_BLOCK:(2 * p + 1) * Q_BLOCK, :]
        odd = full[(2 * p + 1) * Q_BLOCK:(2 * p + 2) * Q_BLOCK, :]
        ref[pl.ds(r0, Q_BLOCK), g * GROUP_W + p * PAIR_W:g * GROUP_W + (p + 1) * PAIR_W] = (
            jnp.where(lo, even, odd).astype(ref.dtype))


def _sink_row(sink_ref, g):
    return jnp.concatenate([jnp.full((1, Q_BLOCK), sink_ref[0, g * GQA_GROUP + h], F32)
                            for h in range(GQA_GROUP)], axis=1)


def _group_probs(qs, k2, bias, sink):
    s = _dot_nt(k2, qs) * (HEAD_DIM ** -0.5) + bias
    m = jnp.maximum(jnp.max(s, axis=0, keepdims=True), sink)
    p = jnp.exp(s - m)
    psink = jnp.exp(sink - m)
    inv = 1.0 / (jnp.sum(p, axis=0, keepdims=True) + psink)
    return p * inv, psink * inv


def _attn_fwd(projp, sinks, *, seq, q_blk, k_blk, v_blk, name, comm=None):
    T = projp.shape[0]
    QW = N_Q_HEADS * HEAD_DIM
    nblk = seq // Q_BLOCK

    def body(q_ref, k_ref, v_ref, sink_ref, o_ref, k2s, v2s):
        _dup_heads(k_ref, k2s, seq)
        _dup_heads(v_ref, v2s, seq)
        lo = _lane_lo()
        bias0, key_index = _band_bias()
        sink_rows = [_sink_row(sink_ref, g) for g in range(N_KV_HEADS)]

        def blk(n, carry):
            r0 = pl.multiple_of(n * Q_BLOCK, Q_BLOCK)
            band = pl.ds(r0, BAND)
            qb = q_ref[pl.ds(r0, Q_BLOCK), :]
            bias = _block_bias(bias0, key_index, r0)
            for g in range(N_KV_HEADS):
                probs_t, _ = _group_probs(_stack_heads(qb, g, lo), k2s[g, band, :], bias, sink_rows[g])
                _unstack_heads(_dot_tn(probs_t.astype(BF16), v2s[g, band, :]), o_ref, r0, g, lo)
            return carry

        lax.fori_loop(0, nblk, blk, 0, unroll=4)

    blocks = [((seq, QW), BF16)] * 2 + [((seq, KV_WIDTH), BF16)] * 2
    return _call(
        body, (projp, projp, projp, sinks), name=name, grid=(T // seq,),
        in_specs=[pl.BlockSpec((seq, QW), lambda b: (b, q_blk)),
                  pl.BlockSpec((seq, KV_WIDTH), lambda b: (b, k_blk)),
                  pl.BlockSpec((seq, KV_WIDTH), lambda b: (b, v_blk)),
                  pl.BlockSpec(memory_space=pltpu.SMEM)],
        out_specs=[pl.BlockSpec((seq, QW), lambda b: (b, 0))],
        out_shape=[SDS((T, QW), BF16)],
        scratch_shapes=[pltpu.VMEM((N_KV_HEADS, seq + ATT_BLOCK, PAIR_W), BF16)] * 2,
        params=_params(1, blocks, temp_bytes=2 * _nbytes((N_KV_HEADS, seq + ATT_BLOCK, PAIR_W), BF16)
                       + 8 * _nbytes((BAND, GROUP_ROWS), F32)), comm=comm, hbm_out=(0,))[0]


def _attn_bwd(projp, dao, sinks, *, seq, q_blk, k_blk, v_blk, name, comm=None):
    T = projp.shape[0]
    QW = N_Q_HEADS * HEAD_DIM
    assert seq % (2 * Q_BLOCK) == 0
    nblk = seq // Q_BLOCK

    def body(q_ref, k_ref, v_ref, do_ref, sink_ref, dq_ref, dk_ref, dv_ref, dsink_ref, k2s, v2s, dkacc, dvacc):
        _dup_heads(k_ref, k2s, seq)
        _dup_heads(v_ref, v2s, seq)
        dkacc[...] = jnp.zeros(dkacc.shape, F32)
        dvacc[...] = jnp.zeros(dvacc.shape, F32)
        lane = lax.broadcasted_iota(jnp.int32, (1, PAIR_W), 1)
        lo = lane < HEAD_DIM
        bias0, key_index = _band_bias()
        sink_rows = [_sink_row(sink_ref, g) for g in range(N_KV_HEADS)]

        def blk(n, tsinks):
            tsinks = list(tsinks)
            r0 = pl.multiple_of(n * Q_BLOCK, Q_BLOCK)
            band = pl.ds(r0, BAND)
            qb = q_ref[pl.ds(r0, Q_BLOCK), :]
            dob = do_ref[pl.ds(r0, Q_BLOCK), :]
            bias = _block_bias(bias0, key_index, r0)
            for g in range(N_KV_HEADS):
                qs = _stack_heads(qb, g, lo)
                dos = _stack_heads(dob, g, lo)
                k2 = k2s[g, band, :]
                v2 = v2s[g, band, :]
                probs_t, psink = _group_probs(qs, k2, bias, sink_rows[g])
                dp_t = _dot_nt(v2, dos)
                delta = jnp.sum(probs_t * dp_t, axis=0, keepdims=True)
                ds_t = (probs_t * (dp_t - delta) * (HEAD_DIM ** -0.5)).astype(BF16)
                tsinks[g] = tsinks[g] + psink * delta
                _unstack_heads(_dot_tn(ds_t, k2), dq_ref, r0, g, lo)
                dkacc[g, band, :] = dkacc[g, band, :] + _dot(ds_t, qs)
                dvacc[g, band, :] = dvacc[g, band, :] + _dot(probs_t.astype(BF16), dos)
            return tuple(tsinks)

        def two_blocks(m, tsinks):
            return blk(2 * m + 1, blk(2 * m, tsinks))

        tsinks = lax.fori_loop(0, nblk // 2, two_blocks, (jnp.zeros((1, GROUP_ROWS), F32),) * N_KV_HEADS)
        dsink = jnp.zeros((1, PAIR_W), F32)
        for g in range(N_KV_HEADS):
            for h in range(GQA_GROUP):
                dsink = dsink + jnp.where(lane == g * GQA_GROUP + h,
                                          -jnp.sum(tsinks[g][:, h * Q_BLOCK:(h + 1) * Q_BLOCK]), 0.0)
        _acc(dsink_ref, dsink, pl.program_id(0) == 0)

        def fold(acc, g):
            a = acc[g, pl.ds(ATT_BLOCK, seq), :]
            return a + pltpu.roll(a, HEAD_DIM, 1)

        dk_ref[...] = jnp.where(lo, fold(dkacc, 0), fold(dkacc, 1)).astype(BF16)
        dv_ref[...] = jnp.where(lo, fold(dvacc, 0), fold(dvacc, 1)).astype(BF16)

    blocks = [((seq, QW), BF16)] * 3 + [((seq, KV_WIDTH), BF16)] * 4
    kv_spec_out = pl.BlockSpec((seq, KV_WIDTH), lambda b: (b, 0))
    return _call(
        body, (projp, projp, projp, dao, sinks), name=name, grid=(T // seq,),
        in_specs=[pl.BlockSpec((seq, QW), lambda b: (b, q_blk)),
                  pl.BlockSpec((seq, KV_WIDTH), lambda b: (b, k_blk)),
                  pl.BlockSpec((seq, KV_WIDTH), lambda b: (b, v_blk)),
                  pl.BlockSpec((seq, QW), lambda b: (b, 0)),
                  pl.BlockSpec(memory_space=pltpu.SMEM)],
        out_specs=[pl.BlockSpec((seq, QW), lambda b: (b, 0)), kv_spec_out, kv_spec_out,
                   pl.BlockSpec((1, 128), lambda b: (0, 0))],
        out_shape=[SDS((T, QW), BF16), SDS((T, KV_WIDTH), BF16), SDS((T, KV_WIDTH), BF16), SDS((1, 128), F32)],
        scratch_shapes=[pltpu.VMEM((N_KV_HEADS, seq + ATT_BLOCK, PAIR_W), BF16)] * 2
        + [pltpu.VMEM((N_KV_HEADS, seq + ATT_BLOCK, PAIR_W), F32)] * 2,
        params=_params(1, blocks, temp_bytes=6 * _nbytes((N_KV_HEADS, seq + ATT_BLOCK, PAIR_W), BF16)
                       + 16 * _nbytes((BAND, GROUP_ROWS), F32)), comm=comm)


SUBLANES = 8


def _sublane_shifts(win):
    n = CONV_ROWS + CONV_HALO
    return [win] + [pltpu.roll(win, n - b, 0) for b in range(1, SUBLANES)]


def _window(shifted, off):
    a = off // SUBLANES * SUBLANES
    return shifted[off % SUBLANES][a:a + CONV_ROWS, :]


def _conv_fwd(projp, w, bias, *, seq, cw, a_col, b_col, name, comm=None):
    T = projp.shape[0]
    C = w.shape[1]
    nchunk = seq // CONV_ROWS

    def body(a_ref, b_ref, w_ref, bias_ref, y_ref, upad):
        upad[pl.ds(0, CONV_HALO), :] = jnp.zeros((CONV_HALO, cw), F32)
        upad[pl.ds(CONV_HALO, seq), :] = a_ref[...].astype(F32) * _sigmoid(b_ref[...].astype(F32))
        wv = w_ref[...]
        bv = bias_ref[...]

        def chunk(r, carry):
            r0 = pl.multiple_of(r * CONV_ROWS, CONV_ROWS)
            shifted = _sublane_shifts(upad[pl.ds(r0, CONV_ROWS + CONV_HALO), :])
            acc = jnp.broadcast_to(bv, (CONV_ROWS, cw))
            for k in range(CONV_WIDTH):
                acc = acc + wv[k:k + 1, :] * _window(shifted, CONV_HALO - (CONV_WIDTH - 1) + k)
            y_ref[pl.ds(r0, CONV_ROWS), :] = acc
            return carry

        lax.fori_loop(0, nchunk, chunk, 0)

    blocks = [((seq, cw), BF16)] * 2 + [((seq, cw), F32)]
    return _call(
        body, (projp, projp, w, bias), name=name, grid=(T // seq, C // cw),
        in_specs=[pl.BlockSpec((seq, cw), lambda b, c: (b, a_col // cw + c)),
                  pl.BlockSpec((seq, cw), lambda b, c: (b, b_col // cw + c)),
                  pl.BlockSpec((CONV_WIDTH, cw), lambda b, c: (0, c)),
                  pl.BlockSpec((1, cw), lambda b, c: (0, c))],
        out_specs=[pl.BlockSpec((seq, cw), lambda b, c: (b, c))],
        out_shape=[SDS((T, C), F32)],
        scratch_shapes=[pltpu.VMEM((seq + CONV_HALO, cw), F32)],
        params=_params(2, blocks, temp_bytes=6 * _nbytes((seq, cw), F32)), comm=comm, hbm_out=(0,))[0]


def _conv_bwd(dy, projp, w, *, seq, cw, a_col, b_col, name, comm=None):
    T = projp.shape[0]
    C = w.shape[1]
    nchunk = seq // CONV_ROWS
    SUB = 8

    def body(dy_ref, a_ref, b_ref, w_ref, da_ref, db_ref, dw_ref, dbias_ref, dypad, dwp):
        first = pl.program_id(1) == 0
        dyv = dy_ref[...]
        dypad[pl.ds(0, seq), :] = dyv
        dypad[pl.ds(seq, CONV_HALO), :] = jnp.zeros((CONV_HALO, cw), F32)
        dwp[...] = jnp.zeros(dwp.shape, F32)
        wv = w_ref[...]

        def chunk(r, carry):
            r0 = pl.multiple_of(r * CONV_ROWS, CONV_ROWS)
            dy_shifts = _sublane_shifts(dypad[pl.ds(r0, CONV_ROWS + CONV_HALO), :])
            ac = a_ref[pl.ds(r0, CONV_ROWS), :].astype(F32)
            sbc = _sigmoid(b_ref[pl.ds(r0, CONV_ROWS), :].astype(F32))
            uc = ac * sbc
            du = jnp.zeros((CONV_ROWS, cw), F32)
            for k in range(CONV_WIDTH):
                dyk = _window(dy_shifts, CONV_WIDTH - 1 - k)
                du = du + wv[k:k + 1, :] * dyk
                prod = uc * dyk
                part = prod[0:SUB, :]
                for s in range(1, CONV_ROWS // SUB):
                    part = part + prod[s * SUB:(s + 1) * SUB, :]
                dwp[pl.ds(k * SUB, SUB), :] = dwp[pl.ds(k * SUB, SUB), :] + part
            da_ref[pl.ds(r0, CONV_ROWS), :] = (du * sbc).astype(BF16)
            db_ref[pl.ds(r0, CONV_ROWS), :] = (du * ac * (sbc * (1.0 - sbc))).astype(BF16)
            return carry

        lax.fori_loop(0, nchunk, chunk, 0)

        @pl.when(first)
        def _():
            dw_ref[...] = jnp.zeros(dw_ref.shape, F32)
            dbias_ref[...] = jnp.zeros(dbias_ref.shape, F32)

        for k in range(CONV_WIDTH):
            dw_ref[k:k + 1, :] = dw_ref[k:k + 1, :] + _rowsum(dwp[pl.ds(k * SUB, SUB), :])
        dbias_ref[...] = dbias_ref[...] + _rowsum(dyv)

    blocks = [((seq, cw), F32)] + [((seq, cw), BF16)] * 4
    return _call(
        body, (dy, projp, projp, w), name=name, grid=(C // cw, T // seq),
        in_specs=[pl.BlockSpec((seq, cw), lambda c, b: (b, c)),
                  pl.BlockSpec((seq, cw), lambda c, b: (b, a_col // cw + c)),
                  pl.BlockSpec((seq, cw), lambda c, b: (b, b_col // cw + c)),
                  pl.BlockSpec((CONV_WIDTH, cw), lambda c, b: (0, c))],
        out_specs=[pl.BlockSpec((seq, cw), lambda c, b: (b, c)), pl.BlockSpec((seq, cw), lambda c, b: (b, c)),
                   pl.BlockSpec((CONV_WIDTH, cw), lambda c, b: (0, c)), pl.BlockSpec((1, cw), lambda c, b: (0, c))],
        out_shape=[SDS((T, C), BF16), SDS((T, C), BF16), SDS((CONV_WIDTH, C), F32), SDS((1, C), F32)],
        scratch_shapes=[pltpu.VMEM((seq + CONV_HALO, cw), F32), pltpu.VMEM((CONV_WIDTH * SUB, cw), F32)],
        params=_params(2, blocks, temp_bytes=8 * _nbytes((seq, cw), F32)), comm=comm, hbm_out=(0, 1))


def _matmul_tn(a, b, *, name, gate=None, comm=None):
    T, M = a.shape
    N = b.shape[1]
    bm = _pick(M, (768, 512, 256))
    lhs = [a] if gate is None else [a, gate]

    def body(*refs):
        b_ref, o_ref = refs[len(lhs)], refs[len(lhs) + 1]
        av = refs[0][...]
        if gate is not None:
            af = av.astype(F32)
            av = (af * _sigmoid(af) * refs[1][...].astype(F32)).astype(BF16)
        o_ref[...] = _dot_tn(av, b_ref[...]).astype(BF16)

    blocks = [((T, bm), BF16)] * len(lhs) + [((T, N), BF16), ((bm, N), BF16)]
    return _call(
        body, (*lhs, b), name=name, grid=(M // bm,),
        in_specs=[pl.BlockSpec((T, bm), lambda i: (0, i))] * len(lhs) + [pl.BlockSpec((T, N), lambda i: (0, 0))],
        out_specs=[pl.BlockSpec((bm, N), lambda i: (i, 0))],
        out_shape=[SDS((M, N), BF16)],
        params=_params(1, blocks, temp_bytes=(2 + 4 * len(lhs)) * _nbytes((T, bm), BF16) + 2 * _nbytes((bm, N), F32)),
        comm=comm)[0]


TN_BLOCK = 256


def _matmul_tn_pieces(groups, b, *, name, comm=None):
    T, N = b.shape
    flat = [a for g in groups for a in g]
    starts, n_steps = [], 0
    for g in groups:
        width = sum(a.shape[1] for a in g)
        assert width % TN_BLOCK == 0 and (len(g) == 1 or width == TN_BLOCK), [a.shape for a in g]
        starts.append(n_steps)
        n_steps += width // TN_BLOCK

    def body(*refs):
        a_refs, b_ref, o_ref = refs[:len(flat)], refs[len(flat)], refs[len(flat) + 1]
        i = pl.program_id(0)
        at = 0
        for g, start in zip(groups, starts):
            mine = a_refs[at:at + len(g)]
            at += len(g)
            steps = sum(a.shape[1] for a in g) // TN_BLOCK

            @pl.when(jnp.logical_and(i >= start, i < start + steps))
            def _(mine=mine):
                a = mine[0][...] if len(mine) == 1 else jnp.concatenate([r[...] for r in mine], axis=1)
                o_ref[...] = _dot_tn(a, b_ref[...]).astype(BF16)

    a_specs = []
    for g, start in zip(groups, starts):
        for a in g:
            if len(g) == 1:
                last = a.shape[1] // TN_BLOCK - 1
                a_specs.append(pl.BlockSpec(
                    (T, TN_BLOCK), lambda i, start=start, last=last: (0, jnp.clip(i - start, 0, last))))
            else:
                a_specs.append(pl.BlockSpec((T, a.shape[1]), lambda i: (0, 0)))
    blocks = [((T, TN_BLOCK), BF16)] * len(flat) + [((T, N), BF16), ((TN_BLOCK, N), BF16)]
    return _call(
        body, (*flat, b), name=name, grid=(n_steps,),
        in_specs=a_specs + [pl.BlockSpec((T, N), lambda i: (0, 0))],
        out_specs=[pl.BlockSpec((TN_BLOCK, N), lambda i: (i, 0))],
        out_shape=[SDS((n_steps * TN_BLOCK, N), BF16)],
        params=_params(1, blocks, temp_bytes=2 * _nbytes((T, TN_BLOCK), BF16) + 2 * _nbytes((TN_BLOCK, N), F32)),
        comm=comm)[0]


def _sum_parts(p_ref):
    g = p_ref[0].astype(F32)
    for s in range(1, p_ref.shape[0]):
        g = g + p_ref[s].astype(F32)
    return g


def _pair_add(g, staged, *, name):
    _, R, W = g.shape
    nq = staged.shape[0]
    tr = _row_tile(R)

    def body(g_ref, s_ref, o_ref):
        mine = jnp.where(lax.axis_index("c") == 0, g_ref[0, 0].astype(F32), g_ref[0, 1].astype(F32))
        o_ref[0] = (mine + s_ref[0].astype(F32)).astype(o_ref.dtype)

    return _call(
        body, (g.reshape(nq, 2, R, W), staged), name=name, grid=(nq, R // tr),
        in_specs=[pl.BlockSpec((1, 2, tr, W), lambda q, i: (q, 0, i, 0)),
                  pl.BlockSpec((1, tr, W), lambda q, i: (q, i, 0))],
        out_specs=[pl.BlockSpec((1, tr, W), lambda q, i: (q, i, 0))],
        out_shape=[SDS((nq, R, W), g.dtype)],
        params=_params(2, [((4, tr, W), g.dtype)], temp_bytes=3 * _nbytes((tr, W), F32)))[0]


def _adamw_update(w, g, m, v):
    m = ADAM_B1 * m + (1.0 - ADAM_B1) * g
    v = ADAM_B2 * v + (1.0 - ADAM_B2) * (g * g)
    m_hat = m / (1.0 - ADAM_B1 ** ADAM_STEP)
    v_hat = v / (1.0 - ADAM_B2 ** ADAM_STEP)
    delta = -ADAM_LR * (m_hat / (jnp.sqrt(v_hat) + ADAM_EPS) + ADAM_WD * w)
    return delta, m, v


def _row_tile(R):
    return _pick(R, (256, 128, 112, 88, 64, 32, 16, 8))


def _sum8(parts, *, name):
    n, R, W = parts.shape
    tr = _row_tile(R)

    def body(p_ref, o_ref):
        o_ref[...] = _sum_parts(p_ref)

    return _call(
        body, (parts,), name=name, grid=(R // tr,),
        in_specs=[pl.BlockSpec((n, tr, W), lambda i: (0, i, 0))],
        out_specs=[pl.BlockSpec((tr, W), lambda i: (i, 0))],
        out_shape=[SDS((R, W), F32)],
        params=_params(1, [((n, tr, W), parts.dtype), ((tr, W), F32)]))[0]


def _adamw(g, w, m, v, *, name):
    R, W = w.shape
    tr = _row_tile(R)

    def body(g_ref, w_ref, m_ref, v_ref, d_ref, mo_ref, vo_ref):
        d_ref[...], mo_ref[...], vo_ref[...] = _adamw_update(w_ref[...], g_ref[...], m_ref[...], v_ref[...])

    spec = pl.BlockSpec((tr, W), lambda i: (i, 0))
    return _call(
        body, (g, w, m, v), name=name, grid=(R // tr,),
        in_specs=[spec] * 4, out_specs=[spec] * 3, out_shape=[SDS((R, W), F32)] * 3,
        params=_params(1, [((tr, W), F32)] * 7))


def _sum8_adamw(parts, w, m, v, *, name):
    R, W = w.shape
    n = parts.shape[0]
    tr = _row_tile(R)

    def body(p_ref, w_ref, m_ref, v_ref, g_ref, d_ref, mo_ref, vo_ref):
        g = _sum_parts(p_ref)
        g_ref[...] = g
        d_ref[...], mo_ref[...], vo_ref[...] = _adamw_update(w_ref[...], g, m_ref[...], v_ref[...])

    spec = pl.BlockSpec((tr, W), lambda i: (i, 0))
    return _call(
        body, (parts, w, m, v), name=name, grid=(R // tr,),
        in_specs=[pl.BlockSpec((n, tr, W), lambda i: (0, i, 0))] + [spec] * 3,
        out_specs=[spec] * 4, out_shape=[SDS((R, W), F32)] * 4,
        params=_params(1, [((n, tr, W), parts.dtype)] + [((tr, W), F32)] * 7))


def _ada_fwd(c_all, w, bias, *, name):
    NB, D = c_all.shape
    N = w.shape[1]

    def body(c_ref, w_ref, b_ref, o_ref):
        cv = c_ref[...]
        ca = (cv * _sigmoid(cv)).astype(BF16)
        o_ref[...] = _dot(ca, w_ref[...].astype(BF16)) + b_ref[...]

    full = lambda s: pl.BlockSpec(s, lambda i: (0,) * len(s))
    return _call(
        body, (c_all, w, bias), name=name, grid=(1,),
        in_specs=[full((NB, D)), full((D, N)), full((1, N))], out_specs=[full((NB, N))],
        out_shape=[SDS((NB, N), F32)],
        params=_params(1, [((D, N), F32)], temp_bytes=_nbytes((D, N), BF16)))[0]


def _ada_bwd(c_all, gmod_all, *, n_col, name):
    NB, D = c_all.shape
    N = gmod_all.shape[1]

    def body(c_ref, g_ref, gw_ref, gb_ref):
        cv = c_ref[...]
        ca = (cv * _sigmoid(cv)).astype(BF16)
        first = pl.multiple_of(_lin(_my_pos()) * n_col, 128)
        gw_ref[...] = _dot_tn(ca, g_ref[:, pl.ds(first, n_col)].astype(BF16))
        gb_ref[...] = _rowsum(g_ref[...])

    full = lambda s: pl.BlockSpec(s, lambda i: (0,) * len(s))
    return _call(
        body, (c_all, gmod_all), name=name, grid=(1,),
        in_specs=[full((NB, D)), full((NB, N))], out_specs=[full((D, n_col)), full((1, N))],
        out_shape=[SDS((D, n_col), F32), SDS((1, N), F32)],
        params=_params(1, [((D, n_col), F32), ((NB, N), F32)]))


def kernel(x, c, w_ada, b_ada, norm_ffn1_g, ffn1_w_gate, ffn1_w_up, ffn1_w_down, norm_mix_g, w_in, attn_sinks, w_attn_o, conv_w_dw, conv_b_dw, conv_ln_g, conv_ln_b, w_conv_o, w_out, norm_ffn2_g, ffn2_w_gate, ffn2_w_up, ffn2_w_down, final_norm_g, loss_target, m_w_ada, m_b_ada, m_norm_ffn1_g, m_ffn1_w_gate, m_ffn1_w_up, m_ffn1_w_down, m_norm_mix_g, m_w_in, m_attn_sinks, m_w_attn_o, m_conv_w_dw, m_conv_b_dw, m_conv_ln_g, m_conv_ln_b, m_w_conv_o, m_w_out, m_norm_ffn2_g, m_ffn2_w_gate, m_ffn2_w_up, m_ffn2_w_down, m_final_norm_g, v_w_ada, v_b_ada, v_norm_ffn1_g, v_ffn1_w_gate, v_ffn1_w_up, v_ffn1_w_down, v_norm_mix_g, v_w_in, v_attn_sinks, v_w_attn_o, v_conv_w_dw, v_conv_b_dw, v_conv_ln_g, v_conv_ln_b, v_w_conv_o, v_w_out, v_norm_ffn2_g, v_ffn2_w_gate, v_ffn2_w_up, v_ffn2_w_down, v_final_norm_g):
    B, S, D = x.shape
    T = B * S
    QW = N_Q_HEADS * HEAD_DIM
    CC = conv_w_dw.shape[2] * N_DEV
    me = _lin(_my_pos())
    xf = x.reshape(T, D)
    tgt = loss_target.reshape(T, D)
    tm = min(512, S)
    kw = dict(seq=S, tm=tm)

    p_k, p_v, p_ca = QW, QW + KV_WIDTH, QW + 2 * KV_WIDTH
    p_cb, p_ga, p_gc = p_ca + CC, p_ca + 2 * CC, p_ca + 2 * CC + D

    def col_t(w):
        return w[0].T.astype(BF16)

    def row_b(w):
        return w[0].astype(BF16)

    def rows(g):
        return g.reshape(-1, g.shape[-1])

    def blocks8(g):
        return g.reshape(N_DEV, g.shape[0] // N_DEV, g.shape[1])

    def gather(*arrs, hbm_out=False):
        return _Comm([(a, "gather") for a in arrs], hbm_out=hbm_out)

    g_wg1, g_convw, g_c = _exchange(
        [(col_t(ffn1_w_gate), "gather"), (conv_w_dw[0], "gather"), (c, "gather")], name="gather_first")
    wg1 = rows(g_wg1)
    conv_w = g_convw.transpose(1, 0, 2).reshape(CONV_WIDTH, CC)
    c_all = g_c.reshape(N_DEV * B, D)

    n_col = N_MOD * D // N_DEV
    b_cols = lax.dynamic_slice(b_ada, (0, me * n_col), (1, n_col))
    mod_cols = _ada_fwd(c_all, w_ada[0], b_cols, name="ada_fwd")
    mod_mine = _exchange([(mod_cols.reshape(N_DEV, B, n_col), "scatter")], name="scatter_mod")[0]
    mod = mod_mine.transpose(1, 0, 2).reshape(B * N_MOD, 1, D)
    sh1, sc1, g1, sh2, sc2, g2, sh3, sc3, g3 = [_ModVec(mod, i) for i in range(N_MOD)]

    F = wg1.shape[0]
    tn_f = _pick(F, (1408, 1024, 512, 256))
    tn_in = _pick(w_in.shape[2] * N_DEV, (1792, 768, 512, 256))
    gate_blk = dict(ga_col=p_ga, gc_col=p_gc)
    att_blk = dict(q_blk=0, k_blk=p_k // KV_WIDTH, v_blk=p_v // KV_WIDTH)
    conv_kw = dict(seq=S, cw=256, a_col=p_ca, b_col=p_cb)

    cm = gather(col_t(ffn1_w_up))
    h1, (a1,) = _norm_mod_matmul(xf, norm_ffn1_g, sh1, sc1, [wg1], tn=tn_f, name="ffn1_gate", comm=cm, **kw)
    wu1 = rows(cm.out[0])
    cm = gather(row_b(ffn1_w_down), hbm_out=True)
    b1 = _matmul_nt(h1, wu1, tm=tm, tn=tn_f, name="ffn1_up", comm=cm)
    wd1 = rows(cm.out[0])
    cm = gather(col_t(w_in))
    x1, y1 = _ffn_down(a1, b1, wd1, xf, g1, name="ffn1_down", comm=cm, **kw)
    winp = rows(cm.out[0])
    cm = gather(col_t(ffn2_w_up), hbm_out=True)
    h2, (projp,) = _norm_mod_matmul(x1, norm_mix_g, sh2, sc2, [winp], tn=tn_in, name="mix_in", comm=cm, **kw)
    wu2 = rows(cm.out[0])
    cm = gather(col_t(ffn2_w_gate))
    ao = _in_hbm(_attn_fwd(projp, attn_sinks, seq=S, name="attn_fwd", comm=cm, **att_blk))
    wg2 = rows(cm.out[0])
    cm = gather(row_b(w_attn_o), row_b(w_conv_o), row_b(w_out), hbm_out=True)
    yc = _in_hbm(_conv_fwd(projp, conv_w, conv_b_dw, name="conv_fwd", comm=cm, **conv_kw))
    wao, wco, wout = [rows(o) for o in cm.out]
    x2, z, ya, ycv, cact, merged = _mix_out(ao, yc, projp, wao, wco, wout, x1, g2, conv_ln_g, conv_ln_b,
                                            name="mix_out", **gate_blk, **kw)
    cm = gather(row_b(ffn2_w_down), hbm_out=True)
    h3, (a3, b3) = _norm_mod_matmul(x2, norm_ffn2_g, sh3, sc3, [wg2, wu2], tn=tn_f, name="ffn2_up", comm=cm, **kw)
    wd2 = rows(cm.out[0])
    x3, y3 = _ffn_down(a3, b3, wd2, x2, g3, name="ffn2_down", **kw)
    dx3, loss_row, dgf = _final_loss(_in_hbm(x3), final_norm_g[None], _in_hbm(tgt), tm=tm, name="final_loss")
    dx3 = _in_hbm(dx3)

    parts = {}

    def pair(*gs):
        return [(blocks8(g), "pair") for g in gs]

    def cross(*rs):
        return [(r, "cross") for r in rs]

    def reduce_pairs(gs, staged, names):
        return [_pair_add(blocks8(g), s, name="pair_add_" + n) for g, s, n in zip(gs, staged, names)]

    dyb3, da3, db3, dg3 = _ffn_bwd_down(dx3, g3, y3, wd2, a3, b3, tn=tn_f, name="ffn2_bwd_down", **kw)
    gwd2 = _matmul_tn(a3, dyb3, gate=b3, name="gw_ffn2_down")
    cm = _Comm(pair(gwd2))
    dx2, dsh3, dsc3, dgn3 = _matmul_norm_mod_bwd([[da3], [db3]], [wg2, wu2], x2, norm_ffn2_g, sc3, dx3,
                                                 name="ffn2_bwd_up", out_dtype=GRAD_STREAM, comm=cm, **kw)
    r_wd2, = reduce_pairs([gwd2], cm.out, ["ffn2_w_down"])
    cm = _Comm(cross(r_wd2))
    gwg2 = _matmul_tn(da3, h3, name="gw_ffn2_gate", comm=cm)
    parts["ffn2_w_down"], = cm.out
    cm = _Comm(pair(gwg2))
    gwu2 = _matmul_tn(db3, h3, name="gw_ffn2_up", comm=cm)
    r_wg2, = reduce_pairs([gwg2], cm.out, ["ffn2_w_gate"])

    cm = _Comm(cross(r_wg2) + pair(gwu2))
    dzb, dyab, dycb, dga, dgc, dao, dyc, dg2, dlng, dlnb = _mix_out_bwd(
        dx2, g2, z, wout, projp, ya, ycv, wao, wco, yc, conv_ln_g, conv_ln_b, name="mix_out_bwd", comm=cm,
        **gate_blk, **kw)
    parts["ffn2_w_gate"] = cm.out[0]
    r_wu2, = reduce_pairs([gwu2], cm.out[1:], ["ffn2_w_up"])
    gwout = _matmul_tn(merged, dzb, name="gw_out")
    gwao = _matmul_tn(ao, dyab, name="gw_attn_o")
    gwco = _matmul_tn(cact, dycb, name="gw_conv_o")
    cm = _Comm(cross(r_wu2) + pair(gwout, gwao, gwco))
    dq, dk, dv, dsinks = _attn_bwd(projp, dao, attn_sinks, seq=S, name="attn_bwd", comm=cm, **att_blk)
    parts["ffn2_w_up"] = cm.out[0]
    r_mix = reduce_pairs([gwout, gwao, gwco], cm.out[1:], ["w_out", "w_attn_o", "w_conv_o"])
    cm = _Comm(cross(*r_mix))
    dca, dcb, dconvw, dconvb = _conv_bwd(dyc, projp, conv_w, name="conv_bwd", comm=cm, **conv_kw)
    dca, dcb = _in_hbm(dca), _in_hbm(dcb)
    parts["w_out"], parts["w_attn_o"], parts["w_conv_o"] = cm.out
    gwin = _matmul_tn_pieces([[dq], [dk, dv], [dca], [dcb], [dga], [dgc]], h2, name="gw_in")
    cm = _Comm(pair(gwin))
    dx1, dsh2, dsc2, dgn2 = _matmul_norm_mod_bwd([[dq, dk, dv, dca, dcb, dga, dgc]], [winp], x1, norm_mix_g, sc2, dx2,
                                                 name="mix_in_bwd", out_dtype=GRAD_STREAM, comm=cm, **kw)
    r_win, = reduce_pairs([gwin], cm.out, ["w_in"])

    cm = _Comm(cross(r_win))
    dyb1, da1, db1, dg1 = _ffn_bwd_down(dx1, g1, y1, wd1, a1, b1, tn=tn_f, name="ffn1_bwd_down", comm=cm,
                                              **kw)
    parts["w_in"], = cm.out
    gwd1 = _matmul_tn(a1, dyb1, gate=b1, name="gw_ffn1_down")
    cm = _Comm(pair(gwd1))
    gwg1 = _matmul_tn(da1, h1, name="gw_ffn1_gate", comm=cm)
    r_wd1, = reduce_pairs([gwd1], cm.out, ["ffn1_w_down"])
    cm = _Comm(cross(r_wd1) + pair(gwg1))
    gwu1 = _matmul_tn(db1, h1, name="gw_ffn1_up", comm=cm)
    parts["ffn1_w_down"] = cm.out[0]
    r_wg1, = reduce_pairs([gwg1], cm.out[1:], ["ffn1_w_gate"])
    r_wu1, = reduce_pairs([gwu1], _exchange(pair(gwu1), name="pair_last"), ["ffn1_w_up"])
    cm = _Comm(cross(r_wg1, r_wu1))
    dx0, dsh1, dsc1, dgn1 = _matmul_norm_mod_bwd([[da1], [db1]], [wg1, wu1], xf, norm_ffn1_g, sc1, dx1,
                                                 name="ffn1_bwd_up", out_dtype=F32, comm=cm, **kw)
    parts["ffn1_w_gate"], parts["ffn1_w_up"] = cm.out

    n_small = 8
    gmod = jnp.concatenate([dsh1, dsc1, dg1, dsh2, dsc2, dg2, dsh3, dsc3, dg3], axis=1).reshape(B, N_MOD * D)
    sink_row = jnp.pad(dsinks[:, :N_Q_HEADS], ((0, 0), (0, D - N_Q_HEADS)))
    loss_pad = jnp.pad(loss_row, ((0, 0), (0, D - loss_row.shape[1])))
    small = jnp.concatenate([dgn1, dgn2, dgn3, dgf, dconvb, dlng, dlnb, sink_row, dconvw, loss_pad], axis=0)
    small_all, gmod_all = _exchange([(small, "gather"), (gmod, "gather")], name="exchange_last")
    gsmall = _sum8(small_all, name="sum_small")
    loss = gsmall[n_small + CONV_WIDTH, 0]
    g_w_ada, g_b_ada = _ada_bwd(c_all, gmod_all.reshape(N_DEV * B, N_MOD * D), n_col=n_col, name="ada_bwd")
    g_conv_w = lax.dynamic_slice(gsmall[n_small:n_small + CONV_WIDTH], (0, me * (CC // N_DEV)),
                                 (CONV_WIDTH, CC // N_DEV))

    def col_update(name, w, m, v):
        outs = _sum8_adamw(parts[name], w[0].T, m[0].T, v[0].T, name="adamw_" + name)
        return tuple(o.T for o in outs)

    def row_update(name, w, m, v):
        return tuple(_sum8_adamw(parts[name], w[0], m[0], v[0], name="adamw_" + name))

    upd = {
        "ffn1_w_gate": col_update("ffn1_w_gate", ffn1_w_gate, m_ffn1_w_gate, v_ffn1_w_gate),
        "ffn1_w_up": col_update("ffn1_w_up", ffn1_w_up, m_ffn1_w_up, v_ffn1_w_up),
        "ffn1_w_down": row_update("ffn1_w_down", ffn1_w_down, m_ffn1_w_down, v_ffn1_w_down),
        "w_in": col_update("w_in", w_in, m_w_in, v_w_in),
        "w_attn_o": row_update("w_attn_o", w_attn_o, m_w_attn_o, v_w_attn_o),
        "w_conv_o": row_update("w_conv_o", w_conv_o, m_w_conv_o, v_w_conv_o),
        "w_out": row_update("w_out", w_out, m_w_out, v_w_out),
        "ffn2_w_gate": col_update("ffn2_w_gate", ffn2_w_gate, m_ffn2_w_gate, v_ffn2_w_gate),
        "ffn2_w_up": col_update("ffn2_w_up", ffn2_w_up, m_ffn2_w_up, v_ffn2_w_up),
        "ffn2_w_down": row_update("ffn2_w_down", ffn2_w_down, m_ffn2_w_down, v_ffn2_w_down),
        "w_ada": (g_w_ada,) + tuple(_adamw(g_w_ada, w_ada[0], m_w_ada[0], v_w_ada[0], name="adamw_w_ada")),
        "conv_w_dw": (g_conv_w,) + tuple(_adamw(g_conv_w, conv_w_dw[0], m_conv_w_dw[0], v_conv_w_dw[0],
                                                name="adamw_conv_w_dw")),
    }
    for k in upd:
        upd[k] = tuple(t[None] for t in upd[k])

    def pad_sinks(t):
        return jnp.pad(t, ((0, 0), (0, D - N_Q_HEADS)))

    def pack(f1, mix, f2, fin, cb, lg, lb, sinks, bada):
        return jnp.concatenate([f1, mix, f2, fin[None], cb, lg, lb, pad_sinks(sinks), bada.reshape(N_MOD, D)], axis=0)

    w_s = pack(norm_ffn1_g, norm_mix_g, norm_ffn2_g, final_norm_g, conv_b_dw, conv_ln_g, conv_ln_b, attn_sinks, b_ada)
    m_s = pack(m_norm_ffn1_g, m_norm_mix_g, m_norm_ffn2_g, m_final_norm_g, m_conv_b_dw, m_conv_ln_g, m_conv_ln_b,
               m_attn_sinks, m_b_ada)
    v_s = pack(v_norm_ffn1_g, v_norm_mix_g, v_norm_ffn2_g, v_final_norm_g, v_conv_b_dw, v_conv_ln_g, v_conv_ln_b,
               v_attn_sinks, v_b_ada)
    g_s = jnp.concatenate([gsmall[:n_small], g_b_ada.reshape(N_MOD, D)], axis=0)
    small_out = (g_s,) + tuple(_adamw(g_s, w_s, m_s, v_s, name="adamw_vectors"))

    def unpack(t):
        return {
            "norm_ffn1_g": t[0:1], "norm_mix_g": t[1:2], "norm_ffn2_g": t[2:3], "final_norm_g": t[3],
            "conv_b_dw": t[4:5], "conv_ln_g": t[5:6], "conv_ln_b": t[6:7], "attn_sinks": t[7:8, :N_Q_HEADS],
            "b_ada": t[n_small:n_small + N_MOD].reshape(1, N_MOD * D),
        }

    small_un = [unpack(t) for t in small_out]
    for k in small_un[0]:
        upd[k] = tuple(s[k] for s in small_un)

    order = ["w_ada", "b_ada", "norm_ffn1_g", "ffn1_w_gate", "ffn1_w_up", "ffn1_w_down", "norm_mix_g", "w_in",
             "attn_sinks", "w_attn_o", "conv_w_dw", "conv_b_dw", "conv_ln_g", "conv_ln_b", "w_conv_o", "w_out",
             "norm_ffn2_g", "ffn2_w_gate", "ffn2_w_up", "ffn2_w_down", "final_norm_g"]
    grad_x = dx0.reshape(B, S, D)
    return (loss, grad_x, *[upd[k][0] for k in order], *[upd[k][1] for k in order],
            *[upd[k][2] for k in order], *[upd[k][3] for k in order])
```

```python
import dataclasses

import jax
import jax.numpy as jnp
from jax import lax
from jax.experimental import pallas as pl
from jax.experimental.pallas import tpu as pltpu

F32 = jnp.float32
BF16 = jnp.bfloat16
SDS = jax.ShapeDtypeStruct
MESH = pl.DeviceIdType.MESH

N_DEV = 8
EPS = 1e-6
HEAD_DIM = 64
N_Q_HEADS = 16
N_KV_HEADS = 2
GQA_GROUP = N_Q_HEADS // N_KV_HEADS
KV_WIDTH = N_KV_HEADS * HEAD_DIM
ATT_BLOCK = 128
CONV_WIDTH = 31
CONV_HALO = 32
CONV_ROWS = 128
N_MOD = 9
FFN_RESIDUAL = 0.5
ADAM_LR = 0.001
ADAM_B1 = 0.9
ADAM_B2 = 0.999
ADAM_EPS = 1e-08
ADAM_WD = 0.01
ADAM_STEP = 10
NEG_BIG = -1e30
GRAD_STREAM = BF16

V7X_VMEM_BYTES = 64 * 2**20
VMEM_CAP = V7X_VMEM_BYTES - 8 * 2**20


def _nbytes(shape, dtype):
    n = 1
    for s in shape:
        n *= s
    return n * jnp.dtype(dtype).itemsize


def _params(n_axes, blocks, temp_bytes=0):
    need = 2 * sum(_nbytes(s, d) for s, d in blocks) + temp_bytes + 4 * 2**20
    return pltpu.CompilerParams(dimension_semantics=("arbitrary",) * n_axes,
                                vmem_limit_bytes=int(min(max(need, 16 * 2**20), VMEM_CAP)))


def _dot_nt(a, b):
    return lax.dot_general(a, b, (((1,), (1,)), ((), ())), preferred_element_type=F32)


def _dot_tn(a, b):
    return lax.dot_general(a, b, (((0,), (0,)), ((), ())), preferred_element_type=F32)


def _dot(a, b):
    return jnp.dot(a, b, preferred_element_type=F32)


def _sigmoid(x):
    return jax.nn.sigmoid(x)


def _rowsum(v):
    return jnp.sum(v, axis=0, keepdims=True)


def _acc(ref, val, first):
    @pl.when(first)
    def _():
        ref[...] = val

    @pl.when(jnp.logical_not(first))
    def _():
        ref[...] = ref[...] + val


def _norm_mod(xf, gn, sh, sc):
    rstd = lax.rsqrt(jnp.mean(xf * xf, axis=-1, keepdims=True) + EPS)
    xhat = xf * rstd
    yn = xhat * gn
    return yn * (1.0 + sc) + sh, xhat, rstd, yn


def _pick(n, cands):
    for c in cands:
        if n % c == 0:
            return c
    return n


def _my_pos():
    return lax.axis_index("x"), lax.axis_index("y"), lax.axis_index("c")


def _peer(pos, k):
    x, y, c = pos
    return ((1 - x) if k & 4 else x, (1 - y) if k & 2 else y, (1 - c) if k & 1 else c)


def _lin(pos):
    return 4 * pos[0] + 2 * pos[1] + pos[2]


def _in_hbm(a):
    return pltpu.with_memory_space_constraint(a, pltpu.HBM)


class _Comm:
    N_COPY = N_DEV - 1
    N_CHIP = N_DEV // 2

    def __init__(self, items, hbm_out=False):
        self.hbm_out = hbm_out
        self.arrs = [a for a, _ in items]
        self.modes = [m for _, m in items]
        self.n = len(items)
        self.out = None

    def out_shape(self):
        def shape(a, m):
            return {"gather": (N_DEV,) + a.shape, "scatter": a.shape, "pair": (self.N_CHIP,) + a.shape[1:],
                    "cross": a.shape}[m]
        kind = pltpu.HBM if self.hbm_out else SDS
        return [kind(shape(a, m), a.dtype) for a, m in zip(self.arrs, self.modes)]

    def scratch(self):
        return [pltpu.SemaphoreType.DMA((self.n * self.N_COPY,)), pltpu.SemaphoreType.DMA((self.n * self.N_COPY,)),
                pltpu.SemaphoreType.DMA((self.n,))]

    def collective_id(self):
        modes = set(self.modes)
        if "scatter" in modes:
            return 3
        d2d, ici = bool(modes & {"gather", "pair"}), bool(modes & {"gather", "cross"})
        return {(True, False): 0, (False, True): 1, (True, True): 2}[(d2d, ici)]

    def barrier(self):
        x, y, c = _my_pos()
        peers = {0: [(x, y, 1 - c)],
                 1: [(1 - x, y, c), (x, 1 - y, c), (1 - x, 1 - y, c)],
                 2: [(x, y, 1 - c), (1 - x, y, c), (x, 1 - y, c), (1 - x, 1 - y, c)],
                 3: [_peer((x, y, c), k) for k in range(1, N_DEV)]}[self.collective_id()]
        sem = pltpu.get_barrier_semaphore()
        for p in peers:
            pl.semaphore_signal(sem, inc=1, device_id=p, device_id_type=MESH)
        pl.semaphore_wait(sem, len(peers))

    def _plan(self, mode, me):
        x, y, c = me
        sib = (x, y, 1 - c)
        chips = [(1 - x, y), (x, 1 - y), (1 - x, 1 - y)]

        def chip_lin(ch):
            return 2 * ch[0] + ch[1]

        if mode == "scatter":
            peers = [_peer(me, k + 1) for k in range(self.N_COPY)]
            return [(p, ("in", _lin(p)), _lin(me), _lin(p), None) for p in peers], (_lin(me), _lin(me))
        if mode == "gather":
            same = [(*ch, c) for ch in chips]
            other = [(*ch, 1 - c) for ch in chips]
            copies = [(sib, ("in", None), _lin(me), _lin(sib), None)]
            copies += [(p, ("in", None), _lin(me), _lin(p), None) for p in same]
            copies += [(sib, ("out", _lin(p)), _lin(p), _lin(o), 1 + j) for j, (p, o) in enumerate(zip(same, other))]
            return copies, (None, _lin(me))
        if mode == "pair":
            return [(sib, ("in", 2 * q + 1 - c), q, q, None) for q in range(self.N_CHIP)], None
        if mode == "cross":
            mine = chip_lin((x, y))
            return ([((*ch, c), ("in", chip_lin(ch)), mine, chip_lin(ch), None) for ch in chips], (mine, mine))
        raise ValueError(mode)

    def _copy(self, refs, me, i, k, recv):
        srcs, outs, (send_sems, recv_sems, _) = refs
        peer, (where, slot), send_slot, recv_slot, _ = self._plan(self.modes[i], me)[0][k]
        src = srcs[i] if where == "in" else outs[i]
        src = src if slot is None else src.at[slot]
        sem = i * self.N_COPY + k
        return pltpu.make_async_remote_copy(
            src_ref=src, dst_ref=outs[i].at[recv_slot if recv else send_slot], send_sem=send_sems.at[sem],
            recv_sem=recv_sems.at[sem], device_id=peer, device_id_type=MESH)

    def _local(self, refs, me, i):
        srcs, outs, (_, _, loc_sems) = refs
        local = self._plan(self.modes[i], me)[1]
        if local is None:
            return None
        own = srcs[i] if local[0] is None else srcs[i].at[local[0]]
        return pltpu.make_async_copy(own, outs[i].at[local[1]], loc_sems.at[i])

    def start(self, refs):
        me = _my_pos()
        for i in range(self.n):
            local = self._local(refs, me, i)
            if local is not None:
                local.start()
            for k, cp in enumerate(self._plan(self.modes[i], me)[0]):
                if cp[4] is None:
                    self._copy(refs, me, i, k, False).start()

    def forward(self, refs):
        me = _my_pos()
        for i in range(self.n):
            for k, cp in enumerate(self._plan(self.modes[i], me)[0]):
                if cp[4] is not None:
                    self._copy(refs, me, i, cp[4], True).wait_recv()
                    self._copy(refs, me, i, k, False).start()

    def finish(self, refs):
        me = _my_pos()
        plans = [self._plan(m, me)[0] for m in self.modes]
        for i in range(self.n):
            passed_on = [cp[4] for cp in plans[i] if cp[4] is not None]
            for k in range(len(plans[i])):
                if k not in passed_on:
                    self._copy(refs, me, i, k, True).wait_recv()
                self._copy(refs, me, i, k, False).wait_send()
            local = self._local(refs, me, i)
            if local is not None:
                local.wait()


_ANY = pl.BlockSpec(memory_space=pl.ANY)


def _call(body, args, *, name, grid, in_specs, out_specs, out_shape, params, scratch_shapes=(), comm=None,
          hbm_out=()):
    in_specs, out_specs, out_shape = list(in_specs), list(out_specs), list(out_shape)
    scratch_shapes = list(scratch_shapes)
    for k in hbm_out:
        out_shape[k] = pltpu.HBM(out_shape[k].shape, out_shape[k].dtype)
    if comm is None:
        return list(pl.pallas_call(body, name=name, grid=grid, in_specs=in_specs, out_specs=out_specs,
                                   out_shape=out_shape, scratch_shapes=scratch_shapes, compiler_params=params)(*args))
    n_in, n_out, n_scr, nc = len(in_specs), len(out_specs), len(scratch_shapes), comm.n
    n_steps = 1
    for g in grid:
        n_steps *= g

    def hosted(*refs):
        ins, c_in = refs[:n_in], refs[n_in:n_in + nc]
        outs = refs[n_in + nc:n_in + nc + n_out]
        c_out = refs[n_in + nc + n_out:n_in + 2 * nc + n_out]
        scr = refs[n_in + 2 * nc + n_out:n_in + 2 * nc + n_out + n_scr]
        sems = refs[n_in + 2 * nc + n_out + n_scr:]
        step = pl.program_id(0)
        for d in range(1, len(grid)):
            step = step * grid[d] + pl.program_id(d)
        c_refs = (c_in, c_out, sems)

        @pl.when(step == 0)
        def _():
            comm.barrier()
            comm.start(c_refs)

        if n_steps >= 3:
            @pl.when(step == n_steps - 2)
            def _():
                comm.forward(c_refs)

        body(*ins, *outs, *scr)

        @pl.when(step == n_steps - 1)
        def _():
            if n_steps < 3:
                comm.forward(c_refs)
            comm.finish(c_refs)

    res = pl.pallas_call(
        hosted, name=name, grid=grid, in_specs=in_specs + [_ANY] * nc, out_specs=out_specs + [_ANY] * nc,
        out_shape=out_shape + comm.out_shape(), scratch_shapes=scratch_shapes + comm.scratch(),
        compiler_params=dataclasses.replace(params, collective_id=comm.collective_id()))(*args, *comm.arrs)
    comm.out = list(res[n_out:])
    return list(res[:n_out])


def _exchange(items, *, name):
    comm = _Comm(items)

    def body(*refs):
        r = (refs[:comm.n], refs[comm.n:2 * comm.n], refs[2 * comm.n:])
        comm.barrier()
        comm.start(r)
        comm.forward(r)
        comm.finish(r)

    return list(pl.pallas_call(body, name=name, out_shape=comm.out_shape(), in_specs=[_ANY] * comm.n,
                               out_specs=[_ANY] * comm.n, scratch_shapes=comm.scratch(),
                               compiler_params=pltpu.CompilerParams(collective_id=comm.collective_id()))(*comm.arrs))


class _ModVec:
    def __init__(self, arr, idx):
        self.arr, self.idx = arr, idx

    def spec(self, tps, n_axes):
        idx, blk = self.idx, (1, 1, self.arr.shape[2])
        if n_axes == 1:
            return pl.BlockSpec(blk, lambda i: (i // tps * N_MOD + idx, 0, 0))
        return pl.BlockSpec(blk, lambda i, j: (i // tps * N_MOD + idx, 0, 0))


def _norm_mod_matmul(x, gn, sh, sc, wts, *, seq, tm, tn, name, comm=None):
    T, D = x.shape
    N = wts[0].shape[0]
    nw = len(wts)
    tps = seq // tm

    def body(x_ref, gn_ref, sh_ref, sc_ref, *rest):
        w_refs, h_ref, o_refs = rest[:nw], rest[nw], rest[nw + 1:]

        @pl.when(pl.program_id(1) == 0)
        def _():
            h_ref[...] = _norm_mod(x_ref[...], gn_ref[...], sh_ref[0], sc_ref[0])[0].astype(BF16)

        h = h_ref[...]
        for w_ref, o_ref in zip(w_refs, o_refs):
            o_ref[...] = _dot_nt(h, w_ref[...]).astype(o_ref.dtype)

    row = pl.BlockSpec((tm, D), lambda i, j: (i, 0))
    vec = pl.BlockSpec((1, D), lambda i, j: (0, 0))
    wspec = pl.BlockSpec((tn, D), lambda i, j: (j, 0))
    ospec = pl.BlockSpec((tm, tn), lambda i, j: (i, j))
    blocks = [((tm, D), F32), ((tm, D), BF16)] + [((tn, D), BF16), ((tm, tn), BF16)] * nw
    outs = _call(
        body, (x, gn, sh.arr, sc.arr, *wts), name=name, grid=(T // tm, N // tn),
        in_specs=[row, vec, sh.spec(tps, 2), sc.spec(tps, 2)] + [wspec] * nw,
        out_specs=[row] + [ospec] * nw,
        out_shape=[SDS((T, D), BF16)] + [SDS((T, N), BF16)] * nw,
        params=_params(2, blocks, temp_bytes=2 * _nbytes((tm, tn), F32) + 3 * _nbytes((tm, D), F32)), comm=comm)
    return outs[0], outs[1:]


def _matmul_nt(h, w, *, tm, tn, name, comm=None):
    T, D = h.shape
    N = w.shape[0]

    def body(h_ref, w_ref, o_ref):
        o_ref[...] = _dot_nt(h_ref[...], w_ref[...]).astype(o_ref.dtype)

    blocks = [((tm, D), BF16), ((tn, D), BF16), ((tm, tn), BF16)]
    return _call(
        body, (h, w), name=name, grid=(T // tm, N // tn),
        in_specs=[pl.BlockSpec((tm, D), lambda i, j: (i, 0)), pl.BlockSpec((tn, D), lambda i, j: (j, 0))],
        out_specs=[pl.BlockSpec((tm, tn), lambda i, j: (i, j))],
        out_shape=[SDS((T, N), BF16)],
        params=_params(2, blocks, temp_bytes=2 * _nbytes((tm, tn), F32)), comm=comm)[0]


def _ffn_down(a, b, wd, x, g, *, seq, tm, name, comm=None):
    T, F = a.shape
    D = wd.shape[1]
    tps = seq // tm

    def body(a_ref, b_ref, wd_ref, x_ref, g_ref, xo_ref, y_ref):
        af = a_ref[...].astype(F32)
        act = (af * _sigmoid(af) * b_ref[...].astype(F32)).astype(BF16)
        y = _dot(act, wd_ref[...])
        xo_ref[...] = x_ref[...] + (FFN_RESIDUAL * g_ref[0]) * y
        y_ref[...] = y.astype(BF16)

    wide = pl.BlockSpec((tm, F), lambda i: (i, 0))
    row = pl.BlockSpec((tm, D), lambda i: (i, 0))
    wspec = pl.BlockSpec((F, D), lambda i: (0, 0))
    blocks = [((tm, F), BF16)] * 2 + [((F, D), BF16), ((tm, D), F32), ((tm, D), F32), ((tm, D), BF16)]
    return _call(
        body, (a, b, wd, x, g.arr), name=name, grid=(T // tm,),
        in_specs=[wide, wide, wspec, row, g.spec(tps, 1)], out_specs=[row, row],
        out_shape=[SDS((T, D), F32), SDS((T, D), BF16)],
        params=_params(1, blocks, temp_bytes=3 * _nbytes((tm, F), F32)), comm=comm)


def _final_loss(x, gf, tgt, *, tm, name):
    T, D = x.shape
    nt = T // tm

    def body(x_ref, gf_ref, t_ref, dx_ref, loss_ref, dgf_ref, lacc):
        i = pl.program_id(0)
        xf = x_ref[...]
        gfv = gf_ref[...]
        rstd = lax.rsqrt(jnp.mean(xf * xf, axis=-1, keepdims=True) + EPS)
        xhat = xf * rstd
        err = xhat * gfv - t_ref[...]
        dy = err * (1.0 / D)
        dxhat = dy * gfv
        dx_ref[...] = (rstd * (dxhat - xhat * jnp.mean(dxhat * xhat, axis=-1, keepdims=True))).astype(dx_ref.dtype)
        _acc(dgf_ref, _rowsum(dy * xhat), i == 0)
        _acc(lacc, _rowsum(err * err), i == 0)

        @pl.when(i == nt - 1)
        def _():
            loss_ref[...] = jnp.broadcast_to((0.5 / D) * jnp.sum(lacc[...]), loss_ref.shape)

    row = pl.BlockSpec((tm, D), lambda i: (i, 0))
    vec = pl.BlockSpec((1, D), lambda i: (0, 0))
    lspec = pl.BlockSpec((1, 128), lambda i: (0, 0))
    blocks = [((tm, D), F32)] * 3
    return _call(
        body, (x, gf, tgt), name=name, grid=(nt,),
        in_specs=[row, vec, row], out_specs=[row, lspec, vec],
        out_shape=[SDS((T, D), GRAD_STREAM), SDS((1, 128), F32), SDS((1, D), F32)],
        scratch_shapes=[pltpu.VMEM((1, D), F32)],
        params=_params(1, blocks, temp_bytes=4 * _nbytes((tm, D), F32)), hbm_out=(0,))


def _ffn_bwd_down(dxo, g, y, wd, a, b, *, seq, tm, tn, name, comm=None):
    T, F = a.shape
    D = wd.shape[1]
    tps = seq // tm
    nb = T // seq

    def body(dxo_ref, g_ref, y_ref, wd_ref, a_ref, b_ref, dyb_ref, da_ref, db_ref, dg_ref):
        i = pl.program_id(0)

        @pl.when(pl.program_id(1) == 0)
        def _():
            dx = dxo_ref[...].astype(F32)
            dyb_ref[...] = ((FFN_RESIDUAL * g_ref[0]) * dx).astype(BF16)
            part = _rowsum(FFN_RESIDUAL * dx * y_ref[...].astype(F32))
            _acc(dg_ref, part[None], i % tps == 0)

        dact = _dot_nt(dyb_ref[...], wd_ref[...])
        af = a_ref[...].astype(F32)
        bf = b_ref[...].astype(F32)
        sg = _sigmoid(af)
        silu = af * sg
        da_ref[...] = (dact * bf * (sg + silu * (1.0 - sg))).astype(BF16)
        db_ref[...] = (dact * silu).astype(BF16)

    row = pl.BlockSpec((tm, D), lambda i, j: (i, 0))
    per_b = pl.BlockSpec((1, 1, D), lambda i, j: (i // tps, 0, 0))
    wspec = pl.BlockSpec((tn, D), lambda i, j: (j, 0))
    chunk = pl.BlockSpec((tm, tn), lambda i, j: (i, j))
    blocks = [((tm, D), F32), ((tm, D), BF16), ((tn, D), BF16), ((tm, D), BF16)] + [((tm, tn), BF16)] * 4
    return _call(
        body, (dxo, g.arr, y, wd, a, b), name=name, grid=(T // tm, F // tn),
        in_specs=[row, g.spec(tps, 2), row, wspec, chunk, chunk],
        out_specs=[row, chunk, chunk, per_b],
        out_shape=[SDS((T, D), BF16)] + [SDS((T, F), BF16)] * 2 + [SDS((nb, 1, D), F32)],
        params=_params(2, blocks, temp_bytes=6 * _nbytes((tm, tn), F32)), comm=comm)


def _matmul_norm_mod_bwd(ds, ws, x, gn, sc, dxo, *, seq, tm, name, out_dtype, comm=None):
    T, D = x.shape
    nk = len(ws)
    sizes = [len(g) for g in ds]
    ds = [d for g in ds for d in g]
    tps = seq // tm
    nb = T // seq

    def body(*refs):
        w_refs = refs[len(ds):len(ds) + nk]
        x_ref, gn_ref, sc_ref, dxo_ref, dxi_ref, dsh_ref, dsc_ref, dgn_ref = refs[len(ds) + nk:]
        i = pl.program_id(0)
        dh, at = None, 0
        for n, w_ref in zip(sizes, w_refs):
            pieces = [r[...] for r in refs[at:at + n]]
            at += n
            part = _dot(pieces[0] if n == 1 else jnp.concatenate(pieces, axis=1), w_ref[...])
            dh = part if dh is None else dh + part
        gnv = gn_ref[...]
        scv = sc_ref[0]
        _, xhat, rstd, yn = _norm_mod(x_ref[...], gnv, 0.0, scv)
        dyn = dh * (1.0 + scv)
        dxhat = dyn * gnv
        dxi_ref[...] = (dxo_ref[...].astype(F32)
                        + rstd * (dxhat - xhat * jnp.mean(dxhat * xhat, axis=-1, keepdims=True))).astype(out_dtype)
        first_of_seq = i % tps == 0
        _acc(dsh_ref, _rowsum(dh)[None], first_of_seq)
        _acc(dsc_ref, _rowsum(dh * yn)[None], first_of_seq)
        _acc(dgn_ref, _rowsum(dyn * xhat), i == 0)

    row = pl.BlockSpec((tm, D), lambda i: (i, 0))
    vec = pl.BlockSpec((1, D), lambda i: (0, 0))
    per_b = pl.BlockSpec((1, 1, D), lambda i: (i // tps, 0, 0))
    d_specs = [pl.BlockSpec((tm, d.shape[1]), lambda i: (i, 0)) for d in ds]
    w_specs = [pl.BlockSpec(w.shape, lambda i: (0, 0)) for w in ws]
    blocks = ([((tm, d.shape[1]), BF16) for d in ds] + [(w.shape, BF16) for w in ws] + [((tm, D), F32)] * 3)
    return _call(
        body, (*ds, *ws, x, gn, sc.arr, dxo), name=name, grid=(T // tm,),
        in_specs=d_specs + w_specs + [row, vec, sc.spec(tps, 1), row],
        out_specs=[row, per_b, per_b, vec],
        out_shape=[SDS((T, D), out_dtype), SDS((nb, 1, D), F32), SDS((nb, 1, D), F32), SDS((1, D), F32)],
        params=_params(1, blocks, temp_bytes=6 * _nbytes((tm, D), F32)), comm=comm)


def _layernorm_silu(yc, lg, lb):
    mu = jnp.mean(yc, axis=-1, keepdims=True)
    cen = yc - mu
    rstd = lax.rsqrt(jnp.mean(cen * cen, axis=-1, keepdims=True) + EPS)
    xh = cen * rstd
    l = xh * lg + lb
    s = _sigmoid(l)
    return l * s, xh, rstd, l, s


GATE_W = 256


def _gate_specs(tm, D, col):
    return [pl.BlockSpec((tm, GATE_W), lambda i, blk=col // GATE_W + t: (i, blk)) for t in range(D // GATE_W)]


def _gate(refs):
    return jnp.concatenate([r[...] for r in refs], axis=1).astype(F32)


def _mix_out(ao, yc, proj, wao, wco, wout, x1, g2, lg, lb, *, seq, tm, ga_col, gc_col, name, comm=None):
    T, D = x1.shape
    tps = seq // tm
    ng = D // GATE_W

    def body(ao_ref, yc_ref, *rest):
        ga_refs, gc_refs = rest[:ng], rest[ng:2 * ng]
        (wao_ref, wco_ref, wout_ref, x1_ref, g2_ref, lg_ref, lb_ref,
         x2_ref, z_ref, ya_ref, ycv_ref, cact_ref, mrg_ref) = rest[2 * ng:]
        ya = _dot(ao_ref[...], wao_ref[...])
        cact = _layernorm_silu(yc_ref[...], lg_ref[...], lb_ref[...])[0].astype(BF16)
        ycv = _dot(cact, wco_ref[...])
        merged = (_sigmoid(_gate(ga_refs)) * ya + _sigmoid(_gate(gc_refs)) * ycv).astype(BF16)
        z = _dot(merged, wout_ref[...])
        x2_ref[...] = x1_ref[...] + g2_ref[0] * z
        z_ref[...] = z.astype(BF16)
        ya_ref[...] = ya.astype(BF16)
        ycv_ref[...] = ycv.astype(BF16)
        cact_ref[...] = cact
        mrg_ref[...] = merged

    row = pl.BlockSpec((tm, D), lambda i: (i, 0))
    vec = pl.BlockSpec((1, D), lambda i: (0, 0))
    wspec = pl.BlockSpec((D, D), lambda i: (0, 0))
    gates = _gate_specs(tm, D, ga_col) + _gate_specs(tm, D, gc_col)
    blocks = ([((tm, D), BF16), ((tm, D), F32), ((tm, D), BF16), ((tm, D), BF16)] + [((D, D), BF16)] * 3
              + [((tm, D), F32)] * 2 + [((tm, D), BF16)] * 5)
    return _call(
        body, (ao, yc, *[proj] * (2 * ng), wao, wco, wout, x1, g2.arr, lg, lb), name=name, grid=(T // tm,),
        in_specs=[row, row, *gates, wspec, wspec, wspec, row, g2.spec(tps, 1), vec, vec],
        out_specs=[row] * 6,
        out_shape=[SDS((T, D), F32)] + [SDS((T, D), BF16)] * 5,
        params=_params(1, blocks, temp_bytes=8 * _nbytes((tm, D), F32)), comm=comm)


def _mix_out_bwd(dx2, g2, z, wout, proj, ya, ycv, wao, wco, yc, lg, lb, *, seq, tm, ga_col, gc_col, name,
                 comm=None):
    T, D = dx2.shape
    tps = seq // tm
    nb = T // seq
    ng = D // GATE_W

    def body(dx2_ref, g2_ref, z_ref, wout_ref, *rest):
        ga_refs, gc_refs = rest[:ng], rest[ng:2 * ng]
        (ya_ref, ycv_ref, wao_ref, wco_ref, yc_ref, lg_ref, lb_ref, dz_ref, dya_ref, dycv_ref, dga_ref, dgc_ref,
         dao_ref, dyc_ref, dg2_ref, dlg_ref, dlb_ref) = rest[2 * ng:]
        i = pl.program_id(0)
        dx = dx2_ref[...].astype(F32)
        _acc(dg2_ref, _rowsum(dx * z_ref[...].astype(F32))[None], i % tps == 0)
        dzb = (g2_ref[0] * dx).astype(BF16)
        dz_ref[...] = dzb
        dmerged = _dot_nt(dzb, wout_ref[...])
        sa = _sigmoid(_gate(ga_refs))
        sc_ = _sigmoid(_gate(gc_refs))
        dya = (dmerged * sa).astype(BF16)
        dycv = (dmerged * sc_).astype(BF16)
        dya_ref[...] = dya
        dycv_ref[...] = dycv
        dga_ref[...] = (dmerged * ya_ref[...].astype(F32) * (sa * (1.0 - sa))).astype(BF16)
        dgc_ref[...] = (dmerged * ycv_ref[...].astype(F32) * (sc_ * (1.0 - sc_))).astype(BF16)
        dao_ref[...] = _dot_nt(dya, wao_ref[...]).astype(BF16)
        dcact = _dot_nt(dycv, wco_ref[...])
        lgv = lg_ref[...]
        _, xh, rstd, l, s = _layernorm_silu(yc_ref[...], lgv, lb_ref[...])
        dl = dcact * (s * (1.0 + l * (1.0 - s)))
        _acc(dlb_ref, _rowsum(dl), i == 0)
        _acc(dlg_ref, _rowsum(dl * xh), i == 0)
        dxh = dl * lgv
        dyc_ref[...] = rstd * (dxh - jnp.mean(dxh, axis=-1, keepdims=True)
                               - xh * jnp.mean(dxh * xh, axis=-1, keepdims=True))

    row = pl.BlockSpec((tm, D), lambda i: (i, 0))
    vec = pl.BlockSpec((1, D), lambda i: (0, 0))
    per_b = pl.BlockSpec((1, 1, D), lambda i: (i // tps, 0, 0))
    wspec = pl.BlockSpec((D, D), lambda i: (0, 0))
    gates = _gate_specs(tm, D, ga_col) + _gate_specs(tm, D, gc_col)
    blocks = ([((tm, D), F32)] * 3 + [((tm, D), BF16)] * 11 + [((D, D), BF16)] * 3)
    return _call(
        body, (dx2, g2.arr, z, wout, *[proj] * (2 * ng), ya, ycv, wao, wco, yc, lg, lb), name=name,
        grid=(T // tm,),
        in_specs=[row, g2.spec(tps, 1), row, wspec, *gates, row, row, wspec, wspec, row, vec, vec],
        out_specs=[row] * 7 + [per_b, vec, vec],
        out_shape=[SDS((T, D), BF16)] * 6 + [SDS((T, D), F32), SDS((nb, 1, D), F32), SDS((1, D), F32),
                                             SDS((1, D), F32)],
        params=_params(1, blocks, temp_bytes=10 * _nbytes((tm, D), F32)), comm=comm)


Q_BLOCK = 64
BAND = Q_BLOCK + ATT_BLOCK
GROUP_ROWS = GQA_GROUP * Q_BLOCK
PAIR_W = 2 * HEAD_DIM
GROUP_W = GQA_GROUP * HEAD_DIM


def _lane_lo():
    return lax.broadcasted_iota(jnp.int32, (1, PAIR_W), 1) < HEAD_DIM


def _band_bias():
    sj = lax.broadcasted_iota(jnp.int32, (BAND, GROUP_ROWS), 0)
    qi = lax.broadcasted_iota(jnp.int32, (BAND, GROUP_ROWS), 1) & (Q_BLOCK - 1)
    rel = qi + ATT_BLOCK - sj
    bias = jnp.where(jnp.logical_and(rel >= 0, rel < ATT_BLOCK), 0.0, NEG_BIG)
    return bias, lax.broadcasted_iota(jnp.int32, (BAND, 1), 0)


def _block_bias(bias0, key_index, r0):
    return bias0 + jnp.where(key_index + r0 < ATT_BLOCK, NEG_BIG, 0.0)


def _dup_heads(src_ref, dst, seq):
    x = src_ref[...]
    i = lax.broadcasted_iota(jnp.int32, (KV_WIDTH, PAIR_W), 0)
    j = lax.broadcasted_iota(jnp.int32, (KV_WIDTH, PAIR_W), 1) & (HEAD_DIM - 1)
    for g in range(N_KV_HEADS):
        sel = jnp.where(i == j + g * HEAD_DIM, 1.0, 0.0).astype(BF16)
        dst[g, pl.ds(0, ATT_BLOCK), :] = jnp.zeros((ATT_BLOCK, PAIR_W), BF16)
        dst[g, pl.ds(ATT_BLOCK, seq), :] = _dot(x, sel).astype(BF16)


def _stack_heads(blk, g, lo):
    parts = []
    for p in range(GQA_GROUP // 2):
        pair = blk[:, g * GROUP_W + p * PAIR_W:g * GROUP_W + (p + 1) * PAIR_W]
        parts += [jnp.where(lo, pair, jnp.zeros_like(pair)), jnp.where(lo, jnp.zeros_like(pair), pair)]
    return jnp.concatenate(parts, axis=0)


def _unstack_heads(full, ref, r0, g, lo):
    for p in range(GQA_GROUP // 2):
        even = full[(2 * p) * Q_BLOCK:(2 * p + 1) * Q_BLOCK, :]
        odd = full[(2 * p + 1) * Q_BLOCK:(2 * p + 2) * Q_BLOCK, :]
        ref[pl.ds(r0, Q_BLOCK), g * GROUP_W + p * PAIR_W:g * GROUP_W + (p + 1) * PAIR_W] = (
            jnp.where(lo, even, odd).astype(ref.dtype))


def _sink_row(sink_ref, g):
    return jnp.concatenate([jnp.full((1, Q_BLOCK), sink_ref[0, g * GQA_GROUP + h], F32)
                            for h in range(GQA_GROUP)], axis=1)


def _group_probs(qs, k2, bias, sink):
    s = _dot_nt(k2, qs) * (HEAD_DIM ** -0.5) + bias
    m = jnp.maximum(jnp.max(s, axis=0, keepdims=True), sink)
    p = jnp.exp(s - m)
    psink = jnp.exp(sink - m)
    inv = 1.0 / (jnp.sum(p, axis=0, keepdims=True) + psink)
    return p * inv, psink * inv


def _attn_fwd(projp, sinks, *, seq, q_blk, k_blk, v_blk, name, comm=None):
    T = projp.shape[0]
    QW = N_Q_HEADS * HEAD_DIM
    nblk = seq // Q_BLOCK

    def body(q_ref, k_ref, v_ref, sink_ref, o_ref, k2s, v2s):
        _dup_heads(k_ref, k2s, seq)
        _dup_heads(v_ref, v2s, seq)
        lo = _lane_lo()
        bias0, key_index = _band_bias()
        sink_rows = [_sink_row(sink_ref, g) for g in range(N_KV_HEADS)]

        def blk(n, carry):
            r0 = pl.multiple_of(n * Q_BLOCK, Q_BLOCK)
            band = pl.ds(r0, BAND)
            qb = q_ref[pl.ds(r0, Q_BLOCK), :]
            bias = _block_bias(bias0, key_index, r0)
            for g in range(N_KV_HEADS):
                probs_t, _ = _group_probs(_stack_heads(qb, g, lo), k2s[g, band, :], bias, sink_rows[g])
                _unstack_heads(_dot_tn(probs_t.astype(BF16), v2s[g, band, :]), o_ref, r0, g, lo)
            return carry

        lax.fori_loop(0, nblk, blk, 0, unroll=4)

    blocks = [((seq, QW), BF16)] * 2 + [((seq, KV_WIDTH), BF16)] * 2
    return _call(
        body, (projp, projp, projp, sinks), name=name, grid=(T // seq,),
        in_specs=[pl.BlockSpec((seq, QW), lambda b: (b, q_blk)),
                  pl.BlockSpec((seq, KV_WIDTH), lambda b: (b, k_blk)),
                  pl.BlockSpec((seq, KV_WIDTH), lambda b: (b, v_blk)),
                  pl.BlockSpec(memory_space=pltpu.SMEM)],
        out_specs=[pl.BlockSpec((seq, QW), lambda b: (b, 0))],
        out_shape=[SDS((T, QW), BF16)],
        scratch_shapes=[pltpu.VMEM((N_KV_HEADS, seq + ATT_BLOCK, PAIR_W), BF16)] * 2,
        params=_params(1, blocks, temp_bytes=2 * _nbytes((N_KV_HEADS, seq + ATT_BLOCK, PAIR_W), BF16)
                       + 8 * _nbytes((BAND, GROUP_ROWS), F32)), comm=comm, hbm_out=(0,))[0]


def _attn_bwd(projp, dao, sinks, *, seq, q_blk, k_blk, v_blk, name, comm=None):
    T = projp.shape[0]
    QW = N_Q_HEADS * HEAD_DIM
    assert seq % (2 * Q_BLOCK) == 0
    nblk = seq // Q_BLOCK

    def body(q_ref, k_ref, v_ref, do_ref, sink_ref, dq_ref, dk_ref, dv_ref, dsink_ref, k2s, v2s, dkacc, dvacc):
        _dup_heads(k_ref, k2s, seq)
        _dup_heads(v_ref, v2s, seq)
        dkacc[...] = jnp.zeros(dkacc.shape, F32)
        dvacc[...] = jnp.zeros(dvacc.shape, F32)
        lane = lax.broadcasted_iota(jnp.int32, (1, PAIR_W), 1)
        lo = lane < HEAD_DIM
        bias0, key_index = _band_bias()
        sink_rows = [_sink_row(sink_ref, g) for g in range(N_KV_HEADS)]

        def blk(n, tsinks):
            tsinks = list(tsinks)
            r0 = pl.multiple_of(n * Q_BLOCK, Q_BLOCK)
            band = pl.ds(r0, BAND)
            qb = q_ref[pl.ds(r0, Q_BLOCK), :]
            dob = do_ref[pl.ds(r0, Q_BLOCK), :]
            bias = _block_bias(bias0, key_index, r0)
            for g in range(N_KV_HEADS):
                qs = _stack_heads(qb, g, lo)
                dos = _stack_heads(dob, g, lo)
                k2 = k2s[g, band, :]
                v2 = v2s[g, band, :]
                probs_t, psink = _group_probs(qs, k2, bias, sink_rows[g])
                dp_t = _dot_nt(v2, dos)
                delta = jnp.sum(probs_t * dp_t, axis=0, keepdims=True)
                ds_t = (probs_t * (dp_t - delta) * (HEAD_DIM ** -0.5)).astype(BF16)
                tsinks[g] = tsinks[g] + psink * delta
                _unstack_heads(_dot_tn(ds_t, k2), dq_ref, r0, g, lo)
                dkacc[g, band, :] = dkacc[g, band, :] + _dot(ds_t, qs)
                dvacc[g, band, :] = dvacc[g, band, :] + _dot(probs_t.astype(BF16), dos)
            return tuple(tsinks)

        def two_blocks(m, tsinks):
            return blk(2 * m + 1, blk(2 * m, tsinks))

        tsinks = lax.fori_loop(0, nblk // 2, two_blocks, (jnp.zeros((1, GROUP_ROWS), F32),) * N_KV_HEADS)
        dsink = jnp.zeros((1, PAIR_W), F32)
        for g in range(N_KV_HEADS):
            for h in range(GQA_GROUP):
                dsink = dsink + jnp.where(lane == g * GQA_GROUP + h,
                                          -jnp.sum(tsinks[g][:, h * Q_BLOCK:(h + 1) * Q_BLOCK]), 0.0)
        _acc(dsink_ref, dsink, pl.program_id(0) == 0)

        def fold(acc, g):
            a = acc[g, pl.ds(ATT_BLOCK, seq), :]
            return a + pltpu.roll(a, HEAD_DIM, 1)

        dk_ref[...] = jnp.where(lo, fold(dkacc, 0), fold(dkacc, 1)).astype(BF16)
        dv_ref[...] = jnp.where(lo, fold(dvacc, 0), fold(dvacc, 1)).astype(BF16)

    blocks = [((seq, QW), BF16)] * 3 + [((seq, KV_WIDTH), BF16)] * 4
    kv_spec_out = pl.BlockSpec((seq, KV_WIDTH), lambda b: (b, 0))
    return _call(
        body, (projp, projp, projp, dao, sinks), name=name, grid=(T // seq,),
        in_specs=[pl.BlockSpec((seq, QW), lambda b: (b, q_blk)),
                  pl.BlockSpec((seq, KV_WIDTH), lambda b: (b, k_blk)),
                  pl.BlockSpec((seq, KV_WIDTH), lambda b: (b, v_blk)),
                  pl.BlockSpec((seq, QW), lambda b: (b, 0)),
                  pl.BlockSpec(memory_space=pltpu.SMEM)],
        out_specs=[pl.BlockSpec((seq, QW), lambda b: (b, 0)), kv_spec_out, kv_spec_out,
                   pl.BlockSpec((1, 128), lambda b: (0, 0))],
        out_shape=[SDS((T, QW), BF16), SDS((T, KV_WIDTH), BF16), SDS((T, KV_WIDTH), BF16), SDS((1, 128), F32)],
        scratch_shapes=[pltpu.VMEM((N_KV_HEADS, seq + ATT_BLOCK, PAIR_W), BF16)] * 2
        + [pltpu.VMEM((N_KV_HEADS, seq + ATT_BLOCK, PAIR_W), F32)] * 2,
        params=_params(1, blocks, temp_bytes=6 * _nbytes((N_KV_HEADS, seq + ATT_BLOCK, PAIR_W), BF16)
                       + 16 * _nbytes((BAND, GROUP_ROWS), F32)), comm=comm)


SUBLANES = 8


def _sublane_shifts(win):
    n = CONV_ROWS + CONV_HALO
    return [win] + [pltpu.roll(win, n - b, 0) for b in range(1, SUBLANES)]


def _window(shifted, off):
    a = off // SUBLANES * SUBLANES
    return shifted[off % SUBLANES][a:a + CONV_ROWS, :]


def _conv_fwd(projp, w, bias, *, seq, cw, a_col, b_col, name, comm=None):
    T = projp.shape[0]
    C = w.shape[1]
    nchunk = seq // CONV_ROWS

    def body(a_ref, b_ref, w_ref, bias_ref, y_ref, upad):
        upad[pl.ds(0, CONV_HALO), :] = jnp.zeros((CONV_HALO, cw), F32)
        upad[pl.ds(CONV_HALO, seq), :] = a_ref[...].astype(F32) * _sigmoid(b_ref[...].astype(F32))
        wv = w_ref[...]
        bv = bias_ref[...]

        def chunk(r, carry):
            r0 = pl.multiple_of(r * CONV_ROWS, CONV_ROWS)
            shifted = _sublane_shifts(upad[pl.ds(r0, CONV_ROWS + CONV_HALO), :])
            acc = jnp.broadcast_to(bv, (CONV_ROWS, cw))
            for k in range(CONV_WIDTH):
                acc = acc + wv[k:k + 1, :] * _window(shifted, CONV_HALO - (CONV_WIDTH - 1) + k)
            y_ref[pl.ds(r0, CONV_ROWS), :] = acc
            return carry

        lax.fori_loop(0, nchunk, chunk, 0)

    blocks = [((seq, cw), BF16)] * 2 + [((seq, cw), F32)]
    return _call(
        body, (projp, projp, w, bias), name=name, grid=(T // seq, C // cw),
        in_specs=[pl.BlockSpec((seq, cw), lambda b, c: (b, a_col // cw + c)),
                  pl.BlockSpec((seq, cw), lambda b, c: (b, b_col // cw + c)),
                  pl.BlockSpec((CONV_WIDTH, cw), lambda b, c: (0, c)),
                  pl.BlockSpec((1, cw), lambda b, c: (0, c))],
        out_specs=[pl.BlockSpec((seq, cw), lambda b, c: (b, c))],
        out_shape=[SDS((T, C), F32)],
        scratch_shapes=[pltpu.VMEM((seq + CONV_HALO, cw), F32)],
        params=_params(2, blocks, temp_bytes=6 * _nbytes((seq, cw), F32)), comm=comm, hbm_out=(0,))[0]


def _conv_bwd(dy, projp, w, *, seq, cw, a_col, b_col, name, comm=None):
    T = projp.shape[0]
    C = w.shape[1]
    nchunk = seq // CONV_ROWS
    SUB = 8

    def body(dy_ref, a_ref, b_ref, w_ref, da_ref, db_ref, dw_ref, dbias_ref, dypad, dwp):
        first = pl.program_id(1) == 0
        dyv = dy_ref[...]
        dypad[pl.ds(0, seq), :] = dyv
        dypad[pl.ds(seq, CONV_HALO), :] = jnp.zeros((CONV_HALO, cw), F32)
        dwp[...] = jnp.zeros(dwp.shape, F32)
        wv = w_ref[...]

        def chunk(r, carry):
            r0 = pl.multiple_of(r * CONV_ROWS, CONV_ROWS)
            dy_shifts = _sublane_shifts(dypad[pl.ds(r0, CONV_ROWS + CONV_HALO), :])
            ac = a_ref[pl.ds(r0, CONV_ROWS), :].astype(F32)
            sbc = _sigmoid(b_ref[pl.ds(r0, CONV_ROWS), :].astype(F32))
            uc = ac * sbc
            du = jnp.zeros((CONV_ROWS, cw), F32)
            for k in range(CONV_WIDTH):
                dyk = _window(dy_shifts, CONV_WIDTH - 1 - k)
                du = du + wv[k:k + 1, :] * dyk
                prod = uc * dyk
                part = prod[0:SUB, :]
                for s in range(1, CONV_ROWS // SUB):
                    part = part + prod[s * SUB:(s + 1) * SUB, :]
                dwp[pl.ds(k * SUB, SUB), :] = dwp[pl.ds(k * SUB, SUB), :] + part
            da_ref[pl.ds(r0, CONV_ROWS), :] = (du * sbc).astype(BF16)
            db_ref[pl.ds(r0, CONV_ROWS), :] = (du * ac * (sbc * (1.0 - sbc))).astype(BF16)
            return carry

        lax.fori_loop(0, nchunk, chunk, 0)

        @pl.when(first)
        def _():
            dw_ref[...] = jnp.zeros(dw_ref.shape, F32)
            dbias_ref[...] = jnp.zeros(dbias_ref.shape, F32)

        for k in range(CONV_WIDTH):
            dw_ref[k:k + 1, :] = dw_ref[k:k + 1, :] + _rowsum(dwp[pl.ds(k * SUB, SUB), :])
        dbias_ref[...] = dbias_ref[...] + _rowsum(dyv)

    blocks = [((seq, cw), F32)] + [((seq, cw), BF16)] * 4
    return _call(
        body, (dy, projp, projp, w), name=name, grid=(C // cw, T // seq),
        in_specs=[pl.BlockSpec((seq, cw), lambda c, b: (b, c)),
                  pl.BlockSpec((seq, cw), lambda c, b: (b, a_col // cw + c)),
                  pl.BlockSpec((seq, cw), lambda c, b: (b, b_col // cw + c)),
                  pl.BlockSpec((CONV_WIDTH, cw), lambda c, b: (0, c))],
        out_specs=[pl.BlockSpec((seq, cw), lambda c, b: (b, c)), pl.BlockSpec((seq, cw), lambda c, b: (b, c)),
                   pl.BlockSpec((CONV_WIDTH, cw), lambda c, b: (0, c)), pl.BlockSpec((1, cw), lambda c, b: (0, c))],
        out_shape=[SDS((T, C), BF16), SDS((T, C), BF16), SDS((CONV_WIDTH, C), F32), SDS((1, C), F32)],
        scratch_shapes=[pltpu.VMEM((seq + CONV_HALO, cw), F32), pltpu.VMEM((CONV_WIDTH * SUB, cw), F32)],
        params=_params(2, blocks, temp_bytes=8 * _nbytes((seq, cw), F32)), comm=comm, hbm_out=(0, 1))


def _matmul_tn(a, b, *, name, gate=None, comm=None):
    T, M = a.shape
    N = b.shape[1]
    bm = _pick(M, (768, 512, 256))
    lhs = [a] if gate is None else [a, gate]

    def body(*refs):
        b_ref, o_ref = refs[len(lhs)], refs[len(lhs) + 1]
        av = refs[0][...]
        if gate is not None:
            af = av.astype(F32)
            av = (af * _sigmoid(af) * refs[1][...].astype(F32)).astype(BF16)
        o_ref[...] = _dot_tn(av, b_ref[...]).astype(BF16)

    blocks = [((T, bm), BF16)] * len(lhs) + [((T, N), BF16), ((bm, N), BF16)]
    return _call(
        body, (*lhs, b), name=name, grid=(M // bm,),
        in_specs=[pl.BlockSpec((T, bm), lambda i: (0, i))] * len(lhs) + [pl.BlockSpec((T, N), lambda i: (0, 0))],
        out_specs=[pl.BlockSpec((bm, N), lambda i: (i, 0))],
        out_shape=[SDS((M, N), BF16)],
        params=_params(1, blocks, temp_bytes=(2 + 4 * len(lhs)) * _nbytes((T, bm), BF16) + 2 * _nbytes((bm, N), F32)),
        comm=comm)[0]


TN_BLOCK = 256


def _matmul_tn_pieces(groups, b, *, name, comm=None):
    T, N = b.shape
    flat = [a for g in groups for a in g]
    starts, n_steps = [], 0
    for g in groups:
        width = sum(a.shape[1] for a in g)
        assert width % TN_BLOCK == 0 and (len(g) == 1 or width == TN_BLOCK), [a.shape for a in g]
        starts.append(n_steps)
        n_steps += width // TN_BLOCK

    def body(*refs):
        a_refs, b_ref, o_ref = refs[:len(flat)], refs[len(flat)], refs[len(flat) + 1]
        i = pl.program_id(0)
        at = 0
        for g, start in zip(groups, starts):
            mine = a_refs[at:at + len(g)]
            at += len(g)
            steps = sum(a.shape[1] for a in g) // TN_BLOCK

            @pl.when(jnp.logical_and(i >= start, i < start + steps))
            def _(mine=mine):
                a = mine[0][...] if len(mine) == 1 else jnp.concatenate([r[...] for r in mine], axis=1)
                o_ref[...] = _dot_tn(a, b_ref[...]).astype(BF16)

    a_specs = []
    for g, start in zip(groups, starts):
        for a in g:
            if len(g) == 1:
                last = a.shape[1] // TN_BLOCK - 1
                a_specs.append(pl.BlockSpec(
                    (T, TN_BLOCK), lambda i, start=start, last=last: (0, jnp.clip(i - start, 0, last))))
            else:
                a_specs.append(pl.BlockSpec((T, a.shape[1]), lambda i: (0, 0)))
    blocks = [((T, TN_BLOCK), BF16)] * len(flat) + [((T, N), BF16), ((TN_BLOCK, N), BF16)]
    return _call(
        body, (*flat, b), name=name, grid=(n_steps,),
        in_specs=a_specs + [pl.BlockSpec((T, N), lambda i: (0, 0))],
        out_specs=[pl.BlockSpec((TN_BLOCK, N), lambda i: (i, 0))],
        out_shape=[SDS((n_steps * TN_BLOCK, N), BF16)],
        params=_params(1, blocks, temp_bytes=2 * _nbytes((T, TN_BLOCK), BF16) + 2 * _nbytes((TN_BLOCK, N), F32)),
        comm=comm)[0]


def _sum_parts(p_ref):
    g = p_ref[0].astype(F32)
    for s in range(1, p_ref.shape[0]):
        g = g + p_ref[s].astype(F32)
    return g


def _pair_add(g, staged, *, name):
    _, R, W = g.shape
    nq = staged.shape[0]
    tr = _row_tile(R)

    def body(g_ref, s_ref, o_ref):
        mine = jnp.where(lax.axis_index("c") == 0, g_ref[0, 0].astype(F32), g_ref[0, 1].astype(F32))
        o_ref[0] = (mine + s_ref[0].astype(F32)).astype(o_ref.dtype)

    return _call(
        body, (g.reshape(nq, 2, R, W), staged), name=name, grid=(nq, R // tr),
        in_specs=[pl.BlockSpec((1, 2, tr, W), lambda q, i: (q, 0, i, 0)),
                  pl.BlockSpec((1, tr, W), lambda q, i: (q, i, 0))],
        out_specs=[pl.BlockSpec((1, tr, W), lambda q, i: (q, i, 0))],
        out_shape=[SDS((nq, R, W), g.dtype)],
        params=_params(2, [((4, tr, W), g.dtype)], temp_bytes=3 * _nbytes((tr, W), F32)))[0]


def _adamw_update(w, g, m, v):
    m = ADAM_B1 * m + (1.0 - ADAM_B1) * g
    v = ADAM_B2 * v + (1.0 - ADAM_B2) * (g * g)
    m_hat = m / (1.0 - ADAM_B1 ** ADAM_STEP)
    v_hat = v / (1.0 - ADAM_B2 ** ADAM_STEP)
    delta = -ADAM_LR * (m_hat / (jnp.sqrt(v_hat) + ADAM_EPS) + ADAM_WD * w)
    return delta, m, v


def _row_tile(R):
    return _pick(R, (256, 128, 112, 88, 64, 32, 16, 8))


def _sum8(parts, *, name):
    n, R, W = parts.shape
    tr = _row_tile(R)

    def body(p_ref, o_ref):
        o_ref[...] = _sum_parts(p_ref)

    return _call(
        body, (parts,), name=name, grid=(R // tr,),
        in_specs=[pl.BlockSpec((n, tr, W), lambda i: (0, i, 0))],
        out_specs=[pl.BlockSpec((tr, W), lambda i: (i, 0))],
        out_shape=[SDS((R, W), F32)],
        params=_params(1, [((n, tr, W), parts.dtype), ((tr, W), F32)]))[0]


def _adamw(g, w, m, v, *, name):
    R, W = w.shape
    tr = _row_tile(R)

    def body(g_ref, w_ref, m_ref, v_ref, d_ref, mo_ref, vo_ref):
        d_ref[...], mo_ref[...], vo_ref[...] = _adamw_update(w_ref[...], g_ref[...], m_ref[...], v_ref[...])

    spec = pl.BlockSpec((tr, W), lambda i: (i, 0))
    return _call(
        body, (g, w, m, v), name=name, grid=(R // tr,),
        in_specs=[spec] * 4, out_specs=[spec] * 3, out_shape=[SDS((R, W), F32)] * 3,
        params=_params(1, [((tr, W), F32)] * 7))


def _sum8_adamw(parts, w, m, v, *, name):
    R, W = w.shape
    n = parts.shape[0]
    tr = _row_tile(R)

    def body(p_ref, w_ref, m_ref, v_ref, g_ref, d_ref, mo_ref, vo_ref):
        g = _sum_parts(p_ref)
        g_ref[...] = g
        d_ref[...], mo_ref[...], vo_ref[...] = _adamw_update(w_ref[...], g, m_ref[...], v_ref[...])

    spec = pl.BlockSpec((tr, W), lambda i: (i, 0))
    return _call(
        body, (parts, w, m, v), name=name, grid=(R // tr,),
        in_specs=[pl.BlockSpec((n, tr, W), lambda i: (0, i, 0))] + [spec] * 3,
        out_specs=[spec] * 4, out_shape=[SDS((R, W), F32)] * 4,
        params=_params(1, [((n, tr, W), parts.dtype)] + [((tr, W), F32)] * 7))


def _ada_fwd(c_all, w, bias, *, name):
    NB, D = c_all.shape
    N = w.shape[1]

    def body(c_ref, w_ref, b_ref, o_ref):
        cv = c_ref[...]
        ca = (cv * _sigmoid(cv)).astype(BF16)
        o_ref[...] = _dot(ca, w_ref[...].astype(BF16)) + b_ref[...]

    full = lambda s: pl.BlockSpec(s, lambda i: (0,) * len(s))
    return _call(
        body, (c_all, w, bias), name=name, grid=(1,),
        in_specs=[full((NB, D)), full((D, N)), full((1, N))], out_specs=[full((NB, N))],
        out_shape=[SDS((NB, N), F32)],
        params=_params(1, [((D, N), F32)], temp_bytes=_nbytes((D, N), BF16)))[0]


def _ada_bwd(c_all, gmod_all, *, n_col, name):
    NB, D = c_all.shape
    N = gmod_all.shape[1]

    def body(c_ref, g_ref, gw_ref, gb_ref):
        cv = c_ref[...]
        ca = (cv * _sigmoid(cv)).astype(BF16)
        first = pl.multiple_of(_lin(_my_pos()) * n_col, 128)
        gw_ref[...] = _dot_tn(ca, g_ref[:, pl.ds(first, n_col)].astype(BF16))
        gb_ref[...] = _rowsum(g_ref[...])

    full = lambda s: pl.BlockSpec(s, lambda i: (0,) * len(s))
    return _call(
        body, (c_all, gmod_all), name=name, grid=(1,),
        in_specs=[full((NB, D)), full((NB, N))], out_specs=[full((D, n_col)), full((1, N))],
        out_shape=[SDS((D, n_col), F32), SDS((1, N), F32)],
        params=_params(1, [((D, n_col), F32), ((NB, N), F32)]))


def kernel(x, c, w_ada, b_ada, norm_ffn1_g, ffn1_w_gate, ffn1_w_up, ffn1_w_down, norm_mix_g, w_in, attn_sinks, w_attn_o, conv_w_dw, conv_b_dw, conv_ln_g, conv_ln_b, w_conv_o, w_out, norm_ffn2_g, ffn2_w_gate, ffn2_w_up, ffn2_w_down, final_norm_g, loss_target, m_w_ada, m_b_ada, m_norm_ffn1_g, m_ffn1_w_gate, m_ffn1_w_up, m_ffn1_w_down, m_norm_mix_g, m_w_in, m_attn_sinks, m_w_attn_o, m_conv_w_dw, m_conv_b_dw, m_conv_ln_g, m_conv_ln_b, m_w_conv_o, m_w_out, m_norm_ffn2_g, m_ffn2_w_gate, m_ffn2_w_up, m_ffn2_w_down, m_final_norm_g, v_w_ada, v_b_ada, v_norm_ffn1_g, v_ffn1_w_gate, v_ffn1_w_up, v_ffn1_w_down, v_norm_mix_g, v_w_in, v_attn_sinks, v_w_attn_o, v_conv_w_dw, v_conv_b_dw, v_conv_ln_g, v_conv_ln_b, v_w_conv_o, v_w_out, v_norm_ffn2_g, v_ffn2_w_gate, v_ffn2_w_up, v_ffn2_w_down, v_final_norm_g):
    B, S, D = x.shape
    T = B * S
    QW = N_Q_HEADS * HEAD_DIM
    CC = conv_w_dw.shape[2] * N_DEV
    me = _lin(_my_pos())
    xf = x.reshape(T, D)
    tgt = loss_target.reshape(T, D)
    tm = min(512, S)
    kw = dict(seq=S, tm=tm)

    p_k, p_v, p_ca = QW, QW + KV_WIDTH, QW + 2 * KV_WIDTH
    p_cb, p_ga, p_gc = p_ca + CC, p_ca + 2 * CC, p_ca + 2 * CC + D

    def col_t(w):
        return w[0].T.astype(BF16)

    def row_b(w):
        return w[0].astype(BF16)

    def rows(g):
        return g.reshape(-1, g.shape[-1])

    def blocks8(g):
        return g.reshape(N_DEV, g.shape[0] // N_DEV, g.shape[1])

    def gather(*arrs, hbm_out=False):
        return _Comm([(a, "gather") for a in arrs], hbm_out=hbm_out)

    g_wg1, g_convw, g_c = _exchange(
        [(col_t(ffn1_w_gate), "gather"), (conv_w_dw[0], "gather"), (c, "gather")], name="gather_first")
    wg1 = rows(g_wg1)
    conv_w = g_convw.transpose(1, 0, 2).reshape(CONV_WIDTH, CC)
    c_all = g_c.reshape(N_DEV * B, D)

    n_col = N_MOD * D // N_DEV
    b_cols = lax.dynamic_slice(b_ada, (0, me * n_col), (1, n_col))
    mod_cols = _ada_fwd(c_all, w_ada[0], b_cols, name="ada_fwd")
    mod_mine = _exchange([(mod_cols.reshape(N_DEV, B, n_col), "scatter")], name="scatter_mod")[0]
    mod = mod_mine.transpose(1, 0, 2).reshape(B * N_MOD, 1, D)
    sh1, sc1, g1, sh2, sc2, g2, sh3, sc3, g3 = [_ModVec(mod, i) for i in range(N_MOD)]

    F = wg1.shape[0]
    tn_f = _pick(F, (1408, 1024, 512, 256))
    tn_in = _pick(w_in.shape[2] * N_DEV, (1792, 768, 512, 256))
    gate_blk = dict(ga_col=p_ga, gc_col=p_gc)
    att_blk = dict(q_blk=0, k_blk=p_k // KV_WIDTH, v_blk=p_v // KV_WIDTH)
    conv_kw = dict(seq=S, cw=256, a_col=p_ca, b_col=p_cb)

    cm = gather(col_t(ffn1_w_up))
    h1, (a1,) = _norm_mod_matmul(xf, norm_ffn1_g, sh1, sc1, [wg1], tn=tn_f, name="ffn1_gate", comm=cm, **kw)
    wu1 = rows(cm.out[0])
    cm = gather(row_b(ffn1_w_down), hbm_out=True)
    b1 = _matmul_nt(h1, wu1, tm=tm, tn=tn_f, name="ffn1_up", comm=cm)
    wd1 = rows(cm.out[0])
    cm = gather(col_t(w_in))
    x1, y1 = _ffn_down(a1, b1, wd1, xf, g1, name="ffn1_down", comm=cm, **kw)
    winp = rows(cm.out[0])
    cm = gather(col_t(ffn2_w_up), hbm_out=True)
    h2, (projp,) = _norm_mod_matmul(x1, norm_mix_g, sh2, sc2, [winp], tn=tn_in, name="mix_in", comm=cm, **kw)
    wu2 = rows(cm.out[0])
    cm = gather(col_t(ffn2_w_gate), hbm_out=True)
    ao = _in_hbm(_attn_fwd(projp, attn_sinks, seq=S, name="attn_fwd", comm=cm, **att_blk))
    wg2 = rows(cm.out[0])
    cm = gather(row_b(w_attn_o), row_b(w_conv_o), row_b(w_out), hbm_out=True)
    yc = _in_hbm(_conv_fwd(projp, conv_w, conv_b_dw, name="conv_fwd", comm=cm, **conv_kw))
    wao, wco, wout = [rows(o) for o in cm.out]
    x2, z, ya, ycv, cact, merged = _mix_out(ao, yc, projp, wao, wco, wout, x1, g2, conv_ln_g, conv_ln_b,
                                            name="mix_out", **gate_blk, **kw)
    cm = gather(row_b(ffn2_w_down), hbm_out=True)
    h3, (a3, b3) = _norm_mod_matmul(x2, norm_ffn2_g, sh3, sc3, [wg2, wu2], tn=tn_f, name="ffn2_up", comm=cm, **kw)
    wd2 = rows(cm.out[0])
    x3, y3 = _ffn_down(a3, b3, wd2, x2, g3, name="ffn2_down", **kw)
    dx3, loss_row, dgf = _final_loss(_in_hbm(x3), final_norm_g[None], _in_hbm(tgt), tm=tm, name="final_loss")
    dx3 = _in_hbm(dx3)

    parts = {}

    def pair(*gs):
        return [(blocks8(g), "pair") for g in gs]

    def cross(*rs):
        return [(r, "cross") for r in rs]

    def reduce_pairs(gs, staged, names):
        return [_pair_add(blocks8(g), s, name="pair_add_" + n) for g, s, n in zip(gs, staged, names)]

    dyb3, da3, db3, dg3 = _ffn_bwd_down(dx3, g3, y3, wd2, a3, b3, tn=tn_f, name="ffn2_bwd_down", **kw)
    gwd2 = _matmul_tn(a3, dyb3, gate=b3, name="gw_ffn2_down")
    cm = _Comm(pair(gwd2))
    dx2, dsh3, dsc3, dgn3 = _matmul_norm_mod_bwd([[da3], [db3]], [wg2, wu2], x2, norm_ffn2_g, sc3, dx3,
                                                 name="ffn2_bwd_up", out_dtype=GRAD_STREAM, comm=cm, **kw)
    r_wd2, = reduce_pairs([gwd2], cm.out, ["ffn2_w_down"])
    cm = _Comm(cross(r_wd2))
    gwg2 = _matmul_tn(da3, h3, name="gw_ffn2_gate", comm=cm)
    parts["ffn2_w_down"], = cm.out
    cm = _Comm(pair(gwg2))
    gwu2 = _matmul_tn(db3, h3, name="gw_ffn2_up", comm=cm)
    r_wg2, = reduce_pairs([gwg2], cm.out, ["ffn2_w_gate"])

    cm = _Comm(cross(r_wg2) + pair(gwu2))
    dzb, dyab, dycb, dga, dgc, dao, dyc, dg2, dlng, dlnb = _mix_out_bwd(
        dx2, g2, z, wout, projp, ya, ycv, wao, wco, yc, conv_ln_g, conv_ln_b, name="mix_out_bwd", comm=cm,
        **gate_blk, **kw)
    parts["ffn2_w_gate"] = cm.out[0]
    r_wu2, = reduce_pairs([gwu2], cm.out[1:], ["ffn2_w_up"])
    gwout = _matmul_tn(merged, dzb, name="gw_out")
    gwao = _matmul_tn(ao, dyab, name="gw_attn_o")
    gwco = _matmul_tn(cact, dycb, name="gw_conv_o")
    cm = _Comm(cross(r_wu2) + pair(gwout, gwao, gwco))
    dq, dk, dv, dsinks = _attn_bwd(projp, dao, attn_sinks, seq=S, name="attn_bwd", comm=cm, **att_blk)
    parts["ffn2_w_up"] = cm.out[0]
    r_mix = reduce_pairs([gwout, gwao, gwco], cm.out[1:], ["w_out", "w_attn_o", "w_conv_o"])
    cm = _Comm(cross(*r_mix))
    dca, dcb, dconvw, dconvb = _conv_bwd(dyc, projp, conv_w, name="conv_bwd", comm=cm, **conv_kw)
    dca, dcb = _in_hbm(dca), _in_hbm(dcb)
    parts["w_out"], parts["w_attn_o"], parts["w_conv_o"] = cm.out
    gwin = _matmul_tn_pieces([[dq], [dk, dv], [dca], [dcb], [dga], [dgc]], h2, name="gw_in")
    cm = _Comm(pair(gwin))
    dx1, dsh2, dsc2, dgn2 = _matmul_norm_mod_bwd([[dq, dk, dv, dca, dcb, dga, dgc]], [winp], x1, norm_mix_g, sc2, dx2,
                                                 name="mix_in_bwd", out_dtype=GRAD_STREAM, comm=cm, **kw)
    r_win, = reduce_pairs([gwin], cm.out, ["w_in"])

    cm = _Comm(cross(r_win))
    dyb1, da1, db1, dg1 = _ffn_bwd_down(dx1, g1, y1, wd1, a1, b1, tn=tn_f, name="ffn1_bwd_down", comm=cm,
                                              **kw)
    parts["w_in"], = cm.out
    gwd1 = _matmul_tn(a1, dyb1, gate=b1, name="gw_ffn1_down")
    cm = _Comm(pair(gwd1))
    gwg1 = _matmul_tn(da1, h1, name="gw_ffn1_gate", comm=cm)
    r_wd1, = reduce_pairs([gwd1], cm.out, ["ffn1_w_down"])
    cm = _Comm(cross(r_wd1) + pair(gwg1))
    gwu1 = _matmul_tn(db1, h1, name="gw_ffn1_up", comm=cm)
    parts["ffn1_w_down"] = cm.out[0]
    r_wg1, = reduce_pairs([gwg1], cm.out[1:], ["ffn1_w_gate"])
    r_wu1, = reduce_pairs([gwu1], _exchange(pair(gwu1), name="pair_last"), ["ffn1_w_up"])
    cm = _Comm(cross(r_wg1, r_wu1))
    dx0, dsh1, dsc1, dgn1 = _matmul_norm_mod_bwd([[da1], [db1]], [wg1, wu1], xf, norm_ffn1_g, sc1, dx1,
                                                 name="ffn1_bwd_up", out_dtype=F32, comm=cm, **kw)
    parts["ffn1_w_gate"], parts["ffn1_w_up"] = cm.out

    n_small = 8
    gmod = jnp.concatenate([dsh1, dsc1, dg1, dsh2, dsc2, dg2, dsh3, dsc3, dg3], axis=1).reshape(B, N_MOD * D)
    sink_row = jnp.pad(dsinks[:, :N_Q_HEADS], ((0, 0), (0, D - N_Q_HEADS)))
    loss_pad = jnp.pad(loss_row, ((0, 0), (0, D - loss_row.shape[1])))
    small = jnp.concatenate([dgn1, dgn2, dgn3, dgf, dconvb, dlng, dlnb, sink_row, dconvw, loss_pad], axis=0)
    small_all, gmod_all = _exchange([(small, "gather"), (gmod, "gather")], name="exchange_last")
    gsmall = _sum8(small_all, name="sum_small")
    loss = gsmall[n_small + CONV_WIDTH, 0]
    g_w_ada, g_b_ada = _ada_bwd(c_all, gmod_all.reshape(N_DEV * B, N_MOD * D), n_col=n_col, name="ada_bwd")
    g_conv_w = lax.dynamic_slice(gsmall[n_small:n_small + CONV_WIDTH], (0, me * (CC // N_DEV)),
                                 (CONV_WIDTH, CC // N_DEV))

    def col_update(name, w, m, v):
        outs = _sum8_adamw(parts[name], w[0].T, m[0].T, v[0].T, name="adamw_" + name)
        return tuple(o.T for o in outs)

    def row_update(name, w, m, v):
        return tuple(_sum8_adamw(parts[name], w[0], m[0], v[0], name="adamw_" + name))

    upd = {
        "ffn1_w_gate": col_update("ffn1_w_gate", ffn1_w_gate, m_ffn1_w_gate, v_ffn1_w_gate),
        "ffn1_w_up": col_update("ffn1_w_up", ffn1_w_up, m_ffn1_w_up, v_ffn1_w_up),
        "ffn1_w_down": row_update("ffn1_w_down", ffn1_w_down, m_ffn1_w_down, v_ffn1_w_down),
        "w_in": col_update("w_in", w_in, m_w_in, v_w_in),
        "w_attn_o": row_update("w_attn_o", w_attn_o, m_w_attn_o, v_w_attn_o),
        "w_conv_o": row_update("w_conv_o", w_conv_o, m_w_conv_o, v_w_conv_o),
        "w_out": row_update("w_out", w_out, m_w_out, v_w_out),
        "ffn2_w_gate": col_update("ffn2_w_gate", ffn2_w_gate, m_ffn2_w_gate, v_ffn2_w_gate),
        "ffn2_w_up": col_update("ffn2_w_up", ffn2_w_up, m_ffn2_w_up, v_ffn2_w_up),
        "ffn2_w_down": row_update("ffn2_w_down", ffn2_w_down, m_ffn2_w_down, v_ffn2_w_down),
        "w_ada": (g_w_ada,) + tuple(_adamw(g_w_ada, w_ada[0], m_w_ada[0], v_w_ada[0], name="adamw_w_ada")),
        "conv_w_dw": (g_conv_w,) + tuple(_adamw(g_conv_w, conv_w_dw[0], m_conv_w_dw[0], v_conv_w_dw[0],
                                                name="adamw_conv_w_dw")),
    }
    for k in upd:
        upd[k] = tuple(t[None] for t in upd[k])

    def pad_sinks(t):
        return jnp.pad(t, ((0, 0), (0, D - N_Q_HEADS)))

    def pack(f1, mix, f2, fin, cb, lg, lb, sinks, bada):
        return jnp.concatenate([f1, mix, f2, fin[None], cb, lg, lb, pad_sinks(sinks), bada.reshape(N_MOD, D)], axis=0)

    w_s = pack(norm_ffn1_g, norm_mix_g, norm_ffn2_g, final_norm_g, conv_b_dw, conv_ln_g, conv_ln_b, attn_sinks, b_ada)
    m_s = pack(m_norm_ffn1_g, m_norm_mix_g, m_norm_ffn2_g, m_final_norm_g, m_conv_b_dw, m_conv_ln_g, m_conv_ln_b,
               m_attn_sinks, m_b_ada)
    v_s = pack(v_norm_ffn1_g, v_norm_mix_g, v_norm_ffn2_g, v_final_norm_g, v_conv_b_dw, v_conv_ln_g, v_conv_ln_b,
               v_attn_sinks, v_b_ada)
    g_s = jnp.concatenate([gsmall[:n_small], g_b_ada.reshape(N_MOD, D)], axis=0)
    small_out = (g_s,) + tuple(_adamw(g_s, w_s, m_s, v_s, name="adamw_vectors"))

    def unpack(t):
        return {
            "norm_ffn1_g": t[0:1], "norm_mix_g": t[1:2], "norm_ffn2_g": t[2:3], "final_norm_g": t[3],
            "conv_b_dw": t[4:5], "conv_ln_g": t[5:6], "conv_ln_b": t[6:7], "attn_sinks": t[7:8, :N_Q_HEADS],
            "b_ada": t[n_small:n_small + N_MOD].reshape(1, N_MOD * D),
        }

    small_un = [unpack(t) for t in small_out]
    for k in small_un[0]:
        upd[k] = tuple(s[k] for s in small_un)

    order = ["w_ada", "b_ada", "norm_ffn1_g", "ffn1_w_gate", "ffn1_w_up", "ffn1_w_down", "norm_mix_g", "w_in",
             "attn_sinks", "w_attn_o", "conv_w_dw", "conv_b_dw", "conv_ln_g", "conv_ln_b", "w_conv_o", "w_out",
             "norm_ffn2_g", "ffn2_w_gate", "ffn2_w_up", "ffn2_w_down", "final_norm_g"]
    grad_x = dx0.reshape(B, S, D)
    return (loss, grad_x, *[upd[k][0] for k in order], *[upd[k][1] for k in order],
            *[upd[k][2] for k in order], *[upd[k][3] for k in order])
```
